```python
import math
import jax
import jax.numpy as jnp
from jax import lax
import numpy as np

D_MODEL = 2048
BATCH = 8
SEQ = 2048
DEPTH = 1

ATTN_HEAD_DIM = 64
D_ATTN = D_MODEL // 2
N_ATTN_HEADS = D_ATTN // ATTN_HEAD_DIM
ROT_DIM = ATTN_HEAD_DIM // 4
ROPE_THETA = 500000.0
DILATED_PATTERNS = ((128, 1), (512, 4), (2048, 16))
ATTN_BLOCK = 128
SSM_HEAD_DIM = 64
D_SSM = D_MODEL // 2
N_SSM_HEADS = D_SSM // SSM_HEAD_DIM
SSM_GROUPS = 4
SSM_STATE = 128
CONV_WIDTH = 4
SSD_CHUNK = 128
D_CONV = D_SSM + 2 * SSM_GROUPS * SSM_STATE
DT_MIN = 1e-3
DT_MAX = 1e-1
D_MIX = D_ATTN + D_SSM
D_IN = 3 * D_ATTN + D_SSM + D_CONV + N_SSM_HEADS
N_MEM = 256
N_CROSS_HEADS = 4
CROSS_HEAD_DIM = 128
D_CROSS = N_CROSS_HEADS * CROSS_HEAD_DIM
D_FF = 4 * D_MODEL
EPS = 1e-6

kernel_name = "hymba_ssd_dilated_attn_memxattn_sqrelu"


def rms_norm(x, g):
    xf = x.astype(jnp.float32)
    y = xf * lax.rsqrt(jnp.mean(xf * xf, axis=-1, keepdims=True) + EPS)
    return (y * g.astype(jnp.float32)).astype(x.dtype)


def partial_rope(x, positions):
    half = ROT_DIM // 2
    inv_freq = ROPE_THETA ** (-2.0 * jnp.arange(half, dtype=jnp.float32) / ROT_DIM)
    ang = positions.astype(jnp.float32)[..., None] * inv_freq
    cos = jnp.cos(ang)[:, :, None, :]
    sin = jnp.sin(ang)[:, :, None, :]
    xf = x.astype(jnp.float32)
    x1 = xf[..., :half]
    x2 = xf[..., half:ROT_DIM]
    out = jnp.concatenate([x1 * cos - x2 * sin, x2 * cos + x1 * sin, xf[..., ROT_DIM:]], axis=-1)
    return out.astype(x.dtype)


def dilated_window_branch(q, k, v, window, dilation):
    bsz, s_len, n_h, hd = q.shape
    steps = window // dilation
    span = dilation * ATTN_BLOCK
    s_pad = -(-s_len // span) * span
    nb = s_pad // span

    def to_blocks(t):
        t = jnp.pad(t, ((0, 0), (0, s_pad - s_len), (0, 0), (0, 0)))
        return t.reshape(bsz, nb, ATTN_BLOCK, dilation, n_h, hd)

    qb, kb, vb = to_blocks(q), to_blocks(k), to_blocks(v)

    def band(t):
        prev = jnp.concatenate([jnp.zeros_like(t[:, :1]), t[:, :-1]], axis=1)
        return jnp.concatenate([prev, t], axis=2)

    kband, vband = band(kb), band(vb)
    s = jnp.einsum('bnqrhd,bnkrhd->bnrhqk', qb, kband)
    qi = jnp.arange(ATTN_BLOCK)[:, None]
    kj = jnp.arange(2 * ATTN_BLOCK)[None, :]
    dist = qi + ATTN_BLOCK - kj
    in_window = (dist >= 0) & (dist <= steps)
    has_prev = (jnp.arange(nb)[:, None, None] > 0) | (kj[None] >= ATTN_BLOCK)
    mask = in_window[None] & has_prev
    s = jnp.where(mask[None, :, None, None], s, -jnp.inf)
    m = jnp.max(s, axis=-1, keepdims=True)
    p = jnp.exp(s - m)
    denom = jnp.sum(p, axis=-1)
    lse = m[..., 0] + jnp.log(denom)
    o = jnp.einsum('bnrhqk,bnkrhd->bnqrhd', p, vband)
    o = o / jnp.transpose(denom, (0, 1, 4, 2, 3))[..., None]
    o = o.reshape(bsz, s_pad, n_h, hd)[:, :s_len]
    lse = jnp.transpose(lse, (0, 1, 4, 2, 3)).reshape(bsz, s_pad, n_h)[:, :s_len]
    return o, lse


def dilated_attention(q, k, v, positions, g_q, g_k):
    q = partial_rope(rms_norm(q, g_q), positions).astype(jnp.float32) * (ATTN_HEAD_DIM ** -0.5)
    k = partial_rope(rms_norm(k, g_k), positions).astype(jnp.float32)
    v = v.astype(jnp.float32)
    outs, lses = [], []
    for window, dilation in DILATED_PATTERNS:
        o, l = dilated_window_branch(q, k, v, window, dilation)
        outs.append(o)
        lses.append(l)
    wts = jax.nn.softmax(jnp.stack(lses, axis=0), axis=0)
    return jnp.einsum('gbsh,gbshd->bshd', wts, jnp.stack(outs, axis=0))


def ssd_chunked(x, dt, a, b_mat, c_mat):
    bsz, l_len, n_h, p_dim = x.shape
    g, n = b_mat.shape[2], b_mat.shape[3]
    hg = n_h // g
    nc = l_len // SSD_CHUNK
    q = SSD_CHUNK
    xc = x.reshape(bsz, nc, q, g, hg, p_dim)
    dtc = dt.reshape(bsz, nc, q, g, hg)
    bc = b_mat.reshape(bsz, nc, q, g, n)
    cc = c_mat.reshape(bsz, nc, q, g, n)
    a_cs = jnp.cumsum(dtc * a.reshape(g, hg), axis=2)
    seg = a_cs[:, :, :, None] - a_cs[:, :, None, :]
    causal = jnp.tril(jnp.ones((q, q), dtype=bool))[:, :, None, None]
    l_mat = jnp.exp(jnp.where(causal, seg, -jnp.inf))
    cb = jnp.einsum('bclgn,bcsgn->bclsg', cc, bc)
    w = cb[..., None] * l_mat * dtc[:, :, None]
    y_diag = jnp.einsum('bclsgh,bcsghp->bclghp', w, xc)
    decay_states = jnp.exp(a_cs[:, :, -1:] - a_cs)
    states = jnp.einsum('bcsgn,bcsgh,bcsghp->bcghpn', bc, decay_states * dtc, xc)
    chunk_decay = jnp.exp(a_cs[:, :, -1])

    def step(h, inp):
        s_c, a_c = inp
        return h * a_c[..., None, None] + s_c, h

    h0 = jnp.zeros((bsz, g, hg, p_dim, n), jnp.float32)
    _, prev = lax.scan(step, h0, (jnp.moveaxis(states, 1, 0), jnp.moveaxis(chunk_decay, 1, 0)))
    prev = jnp.moveaxis(prev, 0, 1)
    y_off = jnp.einsum('bclgn,bcghpn->bclghp', cc, prev) * jnp.exp(a_cs)[..., None]
    return (y_diag + y_off).reshape(bsz, l_len, n_h, p_dim)


def ssd_mixer(z, xbc, dt_raw, conv_w, conv_b, dt_bias, a_log, d_skip, g_out):
    bsz, l_len, _ = xbc.shape
    xbc = lax.conv_general_dilated(
        xbc, conv_w.astype(xbc.dtype)[:, None, :], window_strides=(1,),
        padding=[(CONV_WIDTH - 1, 0)], dimension_numbers=('NWC', 'WIO', 'NWC'),
        feature_group_count=D_CONV) + conv_b.astype(xbc.dtype)
    xbc = jax.nn.silu(xbc)
    xs = xbc[..., :D_SSM].astype(jnp.float32).reshape(bsz, l_len, N_SSM_HEADS, SSM_HEAD_DIM)
    b_mat = xbc[..., D_SSM:D_SSM + SSM_GROUPS * SSM_STATE].astype(jnp.float32).reshape(bsz, l_len, SSM_GROUPS, SSM_STATE)
    c_mat = xbc[..., D_SSM + SSM_GROUPS * SSM_STATE:].astype(jnp.float32).reshape(bsz, l_len, SSM_GROUPS, SSM_STATE)
    dt = jax.nn.softplus(dt_raw.astype(jnp.float32) + dt_bias.astype(jnp.float32))
    a = -jnp.exp(a_log.astype(jnp.float32))
    y = ssd_chunked(xs, dt, a, b_mat, c_mat) + d_skip.astype(jnp.float32)[:, None] * xs
    y = y.reshape(bsz, l_len, D_SSM) * jax.nn.silu(z.astype(jnp.float32))
    y = rms_norm(y.reshape(bsz, l_len, SSM_GROUPS, D_SSM // SSM_GROUPS),
                 g_out.reshape(SSM_GROUPS, D_SSM // SSM_GROUPS))
    return y.reshape(bsz, l_len, D_SSM).astype(z.dtype)


def cross_attention(h, mem_h, w_q, w_kv, w_o, g_q, g_k):
    bsz, s_len, _ = h.shape
    q = (h @ w_q).reshape(bsz, s_len, N_CROSS_HEADS, CROSS_HEAD_DIM)
    kv = (mem_h @ w_kv).reshape(bsz, -1, 2, N_CROSS_HEADS, CROSS_HEAD_DIM)
    k, v = kv[:, :, 0], kv[:, :, 1]
    q = rms_norm(q, g_q).astype(jnp.float32) * (CROSS_HEAD_DIM ** -0.5)
    k = rms_norm(k, g_k).astype(jnp.float32)
    p = jax.nn.softmax(jnp.einsum('bshd,bmhd->bhsm', q, k), axis=-1)
    o = jnp.einsum('bhsm,bmhd->bshd', p, v.astype(jnp.float32)).astype(h.dtype)
    return o.reshape(bsz, s_len, D_CROSS) @ w_o


def _fwd_setup_inputs(seed: int = 0) -> dict:
    key = jax.random.key(seed)
    ks = jax.random.split(key, 26)
    f32 = jnp.float32

    def nrm(k, shape, scale):
        return jax.random.normal(k, shape, f32) * scale

    def gain(k, shape):
        return 1.0 + 0.02 * jax.random.normal(k, shape, f32)

    dt0 = jnp.exp(jax.random.uniform(ks[9], (DEPTH, N_SSM_HEADS), f32, math.log(DT_MIN), math.log(DT_MAX)))
    return {
        "x": nrm(ks[0], (BATCH, SEQ, D_MODEL), 1.0),
        "mem": nrm(ks[1], (BATCH, N_MEM, D_MODEL), 1.0),
        "positions": jnp.broadcast_to(jnp.arange(SEQ, dtype=jnp.int32), (BATCH, SEQ)),
        "g_mix": gain(ks[2], (DEPTH, D_MODEL)),
        "w_in": nrm(ks[3], (DEPTH, D_MODEL, D_IN), D_MODEL ** -0.5),
        "g_q": gain(ks[4], (DEPTH, ATTN_HEAD_DIM)),
        "g_k": gain(ks[5], (DEPTH, ATTN_HEAD_DIM)),
        "g_attn_out": gain(ks[6], (DEPTH, D_ATTN)),
        "conv_w": nrm(ks[7], (DEPTH, CONV_WIDTH, D_CONV), CONV_WIDTH ** -0.5),
        "conv_b": nrm(ks[8], (DEPTH, D_CONV), 0.02),
        "dt_bias": dt0 + jnp.log(-jnp.expm1(-dt0)),
        "a_log": jnp.log(jax.random.uniform(ks[10], (DEPTH, N_SSM_HEADS), f32, 1.0, 16.0)),
        "d_skip": 1.0 + 0.1 * jax.random.normal(ks[11], (DEPTH, N_SSM_HEADS), f32),
        "g_ssm_out": gain(ks[12], (DEPTH, D_SSM)),
        "w_out": nrm(ks[13], (DEPTH, D_MIX, D_MODEL), D_MIX ** -0.5),
        "g_cross": gain(ks[14], (DEPTH, D_MODEL)),
        "g_mem": gain(ks[15], (DEPTH, D_MODEL)),
        "w_cq": nrm(ks[16], (DEPTH, D_MODEL, D_CROSS), D_MODEL ** -0.5),
        "w_ckv": nrm(ks[17], (DEPTH, D_MODEL, 2 * D_CROSS), D_MODEL ** -0.5),
        "g_cq": gain(ks[18], (DEPTH, CROSS_HEAD_DIM)),
        "g_ck": gain(ks[19], (DEPTH, CROSS_HEAD_DIM)),
        "w_co": nrm(ks[20], (DEPTH, D_CROSS, D_MODEL), D_CROSS ** -0.5),
        "g_mlp": gain(ks[21], (DEPTH, D_MODEL)),
        "w_up": nrm(ks[22], (DEPTH, D_MODEL, D_FF), D_MODEL ** -0.5),
        "w_down": nrm(ks[23], (DEPTH, D_FF, D_MODEL), D_FF ** -0.5),
    }


def _fwd_reference(x, mem, positions, g_mix, w_in, g_q, g_k, g_attn_out, conv_w, conv_b, dt_bias,
              a_log, d_skip, g_ssm_out, w_out, g_cross, g_mem, w_cq, w_ckv, g_cq, g_ck, w_co,
              g_mlp, w_up, w_down):
    bsz, s_len, _ = x.shape
    splits = np.cumsum([D_ATTN, D_ATTN, D_ATTN, D_SSM, D_CONV]).tolist()
    for i in range(DEPTH):
        h = rms_norm(x, g_mix[i])
        q, k, v, z, xbc, dt_raw = jnp.split(h @ w_in[i], splits, axis=-1)
        shp = (bsz, s_len, N_ATTN_HEADS, ATTN_HEAD_DIM)
        attn = dilated_attention(q.reshape(shp), k.reshape(shp), v.reshape(shp), positions, g_q[i], g_k[i])
        attn = rms_norm(attn.reshape(bsz, s_len, D_ATTN), g_attn_out[i]).astype(x.dtype)
        ssm = ssd_mixer(z, xbc, dt_raw, conv_w[i], conv_b[i], dt_bias[i], a_log[i], d_skip[i], g_ssm_out[i])
        x = x + jnp.concatenate([attn, ssm], axis=-1) @ w_out[i]
        x = x + cross_attention(rms_norm(x, g_cross[i]), rms_norm(mem, g_mem[i]),
                                w_cq[i], w_ckv[i], w_co[i], g_cq[i], g_ck[i])
        hm = rms_norm(x, g_mlp[i])
        x = x + jnp.square(jax.nn.relu(hm @ w_up[i])) @ w_down[i]
    return x


import jax as _jax
import jax.numpy as _jnp

TWIN_FORMAT = 'train_step'
FWD_PARAMS = ['x', 'mem', 'positions', 'g_mix', 'w_in', 'g_q', 'g_k', 'g_attn_out', 'conv_w', 'conv_b', 'dt_bias', 'a_log', 'd_skip', 'g_ssm_out', 'w_out', 'g_cross', 'g_mem', 'w_cq', 'w_ckv', 'g_cq', 'g_ck', 'w_co', 'g_mlp', 'w_up', 'w_down']
TWIN_WEIGHTS = ['g_mix', 'w_in', 'g_q', 'g_k', 'g_attn_out', 'conv_w', 'conv_b', 'dt_bias', 'a_log', 'd_skip', 'g_ssm_out', 'w_out', 'g_cross', 'g_mem', 'w_cq', 'w_ckv', 'g_cq', 'g_ck', 'w_co', 'g_mlp', 'w_up', 'w_down']
TWIN_DIFF_INPUT = 'x'
TWIN_INPUTS = ['x', 'mem', 'positions', 'g_mix', 'w_in', 'g_q', 'g_k', 'g_attn_out', 'conv_w', 'conv_b', 'dt_bias', 'a_log', 'd_skip', 'g_ssm_out', 'w_out', 'g_cross', 'g_mem', 'w_cq', 'w_ckv', 'g_cq', 'g_ck', 'w_co', 'g_mlp', 'w_up', 'w_down', 'loss_target', 'm_g_mix', 'm_w_in', 'm_g_q', 'm_g_k', 'm_g_attn_out', 'm_conv_w', 'm_conv_b', 'm_dt_bias', 'm_a_log', 'm_d_skip', 'm_g_ssm_out', 'm_w_out', 'm_g_cross', 'm_g_mem', 'm_w_cq', 'm_w_ckv', 'm_g_cq', 'm_g_ck', 'm_w_co', 'm_g_mlp', 'm_w_up', 'm_w_down', 'v_g_mix', 'v_w_in', 'v_g_q', 'v_g_k', 'v_g_attn_out', 'v_conv_w', 'v_conv_b', 'v_dt_bias', 'v_a_log', 'v_d_skip', 'v_g_ssm_out', 'v_w_out', 'v_g_cross', 'v_g_mem', 'v_w_cq', 'v_w_ckv', 'v_g_cq', 'v_g_ck', 'v_w_co', 'v_g_mlp', 'v_w_up', 'v_w_down']
TWIN_OUTPUTS = ['loss', 'grad_x', 'grad_g_mix', 'grad_w_in', 'grad_g_q', 'grad_g_k', 'grad_g_attn_out', 'grad_conv_w', 'grad_conv_b', 'grad_dt_bias', 'grad_a_log', 'grad_d_skip', 'grad_g_ssm_out', 'grad_w_out', 'grad_g_cross', 'grad_g_mem', 'grad_w_cq', 'grad_w_ckv', 'grad_g_cq', 'grad_g_ck', 'grad_w_co', 'grad_g_mlp', 'grad_w_up', 'grad_w_down', 'delta_g_mix', 'delta_w_in', 'delta_g_q', 'delta_g_k', 'delta_g_attn_out', 'delta_conv_w', 'delta_conv_b', 'delta_dt_bias', 'delta_a_log', 'delta_d_skip', 'delta_g_ssm_out', 'delta_w_out', 'delta_g_cross', 'delta_g_mem', 'delta_w_cq', 'delta_w_ckv', 'delta_g_cq', 'delta_g_ck', 'delta_w_co', 'delta_g_mlp', 'delta_w_up', 'delta_w_down', 'new_m_g_mix', 'new_m_w_in', 'new_m_g_q', 'new_m_g_k', 'new_m_g_attn_out', 'new_m_conv_w', 'new_m_conv_b', 'new_m_dt_bias', 'new_m_a_log', 'new_m_d_skip', 'new_m_g_ssm_out', 'new_m_w_out', 'new_m_g_cross', 'new_m_g_mem', 'new_m_w_cq', 'new_m_w_ckv', 'new_m_g_cq', 'new_m_g_ck', 'new_m_w_co', 'new_m_g_mlp', 'new_m_w_up', 'new_m_w_down', 'new_v_g_mix', 'new_v_w_in', 'new_v_g_q', 'new_v_g_k', 'new_v_g_attn_out', 'new_v_conv_w', 'new_v_conv_b', 'new_v_dt_bias', 'new_v_a_log', 'new_v_d_skip', 'new_v_g_ssm_out', 'new_v_w_out', 'new_v_g_cross', 'new_v_g_mem', 'new_v_w_cq', 'new_v_w_ckv', 'new_v_g_cq', 'new_v_g_ck', 'new_v_w_co', 'new_v_g_mlp', 'new_v_w_up', 'new_v_w_down']
TWIN_LEAF_KINDS = {'loss': 'loss', 'grad_x': 'grad_x', 'grad_g_mix': 'grad_w', 'grad_w_in': 'grad_w', 'grad_g_q': 'grad_w', 'grad_g_k': 'grad_w', 'grad_g_attn_out': 'grad_w', 'grad_conv_w': 'grad_w', 'grad_conv_b': 'grad_w', 'grad_dt_bias': 'grad_w', 'grad_a_log': 'grad_w', 'grad_d_skip': 'grad_w', 'grad_g_ssm_out': 'grad_w', 'grad_w_out': 'grad_w', 'grad_g_cross': 'grad_w', 'grad_g_mem': 'grad_w', 'grad_w_cq': 'grad_w', 'grad_w_ckv': 'grad_w', 'grad_g_cq': 'grad_w', 'grad_g_ck': 'grad_w', 'grad_w_co': 'grad_w', 'grad_g_mlp': 'grad_w', 'grad_w_up': 'grad_w', 'grad_w_down': 'grad_w', 'delta_g_mix': 'delta_w', 'delta_w_in': 'delta_w', 'delta_g_q': 'delta_w', 'delta_g_k': 'delta_w', 'delta_g_attn_out': 'delta_w', 'delta_conv_w': 'delta_w', 'delta_conv_b': 'delta_w', 'delta_dt_bias': 'delta_w', 'delta_a_log': 'delta_w', 'delta_d_skip': 'delta_w', 'delta_g_ssm_out': 'delta_w', 'delta_w_out': 'delta_w', 'delta_g_cross': 'delta_w', 'delta_g_mem': 'delta_w', 'delta_w_cq': 'delta_w', 'delta_w_ckv': 'delta_w', 'delta_g_cq': 'delta_w', 'delta_g_ck': 'delta_w', 'delta_w_co': 'delta_w', 'delta_g_mlp': 'delta_w', 'delta_w_up': 'delta_w', 'delta_w_down': 'delta_w', 'new_m_g_mix': 'new_m', 'new_m_w_in': 'new_m', 'new_m_g_q': 'new_m', 'new_m_g_k': 'new_m', 'new_m_g_attn_out': 'new_m', 'new_m_conv_w': 'new_m', 'new_m_conv_b': 'new_m', 'new_m_dt_bias': 'new_m', 'new_m_a_log': 'new_m', 'new_m_d_skip': 'new_m', 'new_m_g_ssm_out': 'new_m', 'new_m_w_out': 'new_m', 'new_m_g_cross': 'new_m', 'new_m_g_mem': 'new_m', 'new_m_w_cq': 'new_m', 'new_m_w_ckv': 'new_m', 'new_m_g_cq': 'new_m', 'new_m_g_ck': 'new_m', 'new_m_w_co': 'new_m', 'new_m_g_mlp': 'new_m', 'new_m_w_up': 'new_m', 'new_m_w_down': 'new_m', 'new_v_g_mix': 'new_v', 'new_v_w_in': 'new_v', 'new_v_g_q': 'new_v', 'new_v_g_k': 'new_v', 'new_v_g_attn_out': 'new_v', 'new_v_conv_w': 'new_v', 'new_v_conv_b': 'new_v', 'new_v_dt_bias': 'new_v', 'new_v_a_log': 'new_v', 'new_v_d_skip': 'new_v', 'new_v_g_ssm_out': 'new_v', 'new_v_w_out': 'new_v', 'new_v_g_cross': 'new_v', 'new_v_g_mem': 'new_v', 'new_v_w_cq': 'new_v', 'new_v_w_ckv': 'new_v', 'new_v_g_cq': 'new_v', 'new_v_g_ck': 'new_v', 'new_v_w_co': 'new_v', 'new_v_g_mlp': 'new_v', 'new_v_w_up': 'new_v', 'new_v_w_down': 'new_v'}


def _forward(args):
    return _fwd_reference(*[args[k] for k in FWD_PARAMS])


def _output_shape():
    out = _jax.eval_shape(lambda: _forward(_fwd_setup_inputs(0)))
    return out.shape, out.dtype

N_MICROBATCH = 1
ADAM_LR = 0.001
ADAM_B1 = 0.9
ADAM_B2 = 0.999
ADAM_EPS = 1e-08
ADAM_WD = 0.01
ADAM_STEP = 10
PER_EXAMPLE_BATCH_AXIS = {'x': 0, 'mem': 0, 'positions': 0, 'loss_target': 0}
SHARED_INPUTS = []
_WEIGHT_DTYPES = {'g_mix': _jnp.float32, 'w_in': _jnp.float32, 'g_q': _jnp.float32, 'g_k': _jnp.float32, 'g_attn_out': _jnp.float32, 'conv_w': _jnp.float32, 'conv_b': _jnp.float32, 'dt_bias': _jnp.float32, 'a_log': _jnp.float32, 'd_skip': _jnp.float32, 'g_ssm_out': _jnp.float32, 'w_out': _jnp.float32, 'g_cross': _jnp.float32, 'g_mem': _jnp.float32, 'w_cq': _jnp.float32, 'w_ckv': _jnp.float32, 'g_cq': _jnp.float32, 'g_ck': _jnp.float32, 'w_co': _jnp.float32, 'g_mlp': _jnp.float32, 'w_up': _jnp.float32, 'w_down': _jnp.float32}
MOMENT_SCALE = {'g_mix': 5.504852e-01, 'w_in': 3.135064e-01, 'g_q': 9.489815e-01, 'g_k': 9.361181e-01, 'g_attn_out': 7.888813e+00, 'conv_w': 5.192273e-01, 'conv_b': 1.791508e+00, 'dt_bias': 6.513492e-01, 'a_log': 2.937040e+00, 'd_skip': 4.150479e+00, 'g_ssm_out': 1.014221e+01, 'w_out': 8.463929e-01, 'g_cross': 3.598480e-02, 'g_mem': 1.998143e-01, 'w_cq': 7.011705e-02, 'w_ckv': 2.627343e-01, 'g_cq': 1.249407e+00, 'g_ck': 1.250065e+00, 'w_co': 1.862732e-01, 'g_mlp': 2.371647e+01, 'w_up': 3.646133e-01, 'w_down': 1.984672e+00}


def _to_microbatches(a, axis):
    t = _jnp.moveaxis(a, axis, 0)
    t = t.reshape((N_MICROBATCH, t.shape[0] // N_MICROBATCH) + t.shape[1:])
    return _jnp.moveaxis(t, 1, axis + 1)


def setup_inputs(seed: int = 0) -> dict:
    inp = _fwd_setup_inputs(seed)
    key = _jax.random.fold_in(_jax.random.key(seed), 7919)
    shape, _ = _output_shape()
    out = dict(inp)
    out["loss_target"] = _jax.random.normal(_jax.random.fold_in(key, 0), shape, _jnp.float32)
    for i, name in enumerate(TWIN_WEIGHTS):
        w = inp[name].astype(_jnp.float32)
        if MOMENT_SCALE is None:
            s = _jnp.sqrt(_jnp.mean(_jnp.square(w)) + 1e-30)
        else:
            s = MOMENT_SCALE[name]
        km, kv = _jax.random.split(_jax.random.fold_in(key, i + 1))
        out[name] = w
        out["m_" + name] = s * _jax.random.normal(km, w.shape, _jnp.float32)
        out["v_" + name] = (s * s) * _jax.random.uniform(kv, w.shape, _jnp.float32, 0.5, 1.5)
    if N_MICROBATCH > 1:
        for name, axis in PER_EXAMPLE_BATCH_AXIS.items():
            out[name] = _to_microbatches(out[name], axis)
    return {'x': out['x'], 'mem': out['mem'], 'positions': out['positions'], 'g_mix': out['g_mix'], 'w_in': out['w_in'], 'g_q': out['g_q'], 'g_k': out['g_k'], 'g_attn_out': out['g_attn_out'], 'conv_w': out['conv_w'], 'conv_b': out['conv_b'], 'dt_bias': out['dt_bias'], 'a_log': out['a_log'], 'd_skip': out['d_skip'], 'g_ssm_out': out['g_ssm_out'], 'w_out': out['w_out'], 'g_cross': out['g_cross'], 'g_mem': out['g_mem'], 'w_cq': out['w_cq'], 'w_ckv': out['w_ckv'], 'g_cq': out['g_cq'], 'g_ck': out['g_ck'], 'w_co': out['w_co'], 'g_mlp': out['g_mlp'], 'w_up': out['w_up'], 'w_down': out['w_down'], 'loss_target': out['loss_target'], 'm_g_mix': out['m_g_mix'], 'm_w_in': out['m_w_in'], 'm_g_q': out['m_g_q'], 'm_g_k': out['m_g_k'], 'm_g_attn_out': out['m_g_attn_out'], 'm_conv_w': out['m_conv_w'], 'm_conv_b': out['m_conv_b'], 'm_dt_bias': out['m_dt_bias'], 'm_a_log': out['m_a_log'], 'm_d_skip': out['m_d_skip'], 'm_g_ssm_out': out['m_g_ssm_out'], 'm_w_out': out['m_w_out'], 'm_g_cross': out['m_g_cross'], 'm_g_mem': out['m_g_mem'], 'm_w_cq': out['m_w_cq'], 'm_w_ckv': out['m_w_ckv'], 'm_g_cq': out['m_g_cq'], 'm_g_ck': out['m_g_ck'], 'm_w_co': out['m_w_co'], 'm_g_mlp': out['m_g_mlp'], 'm_w_up': out['m_w_up'], 'm_w_down': out['m_w_down'], 'v_g_mix': out['v_g_mix'], 'v_w_in': out['v_w_in'], 'v_g_q': out['v_g_q'], 'v_g_k': out['v_g_k'], 'v_g_attn_out': out['v_g_attn_out'], 'v_conv_w': out['v_conv_w'], 'v_conv_b': out['v_conv_b'], 'v_dt_bias': out['v_dt_bias'], 'v_a_log': out['v_a_log'], 'v_d_skip': out['v_d_skip'], 'v_g_ssm_out': out['v_g_ssm_out'], 'v_w_out': out['v_w_out'], 'v_g_cross': out['v_g_cross'], 'v_g_mem': out['v_g_mem'], 'v_w_cq': out['v_w_cq'], 'v_w_ckv': out['v_w_ckv'], 'v_g_cq': out['v_g_cq'], 'v_g_ck': out['v_g_ck'], 'v_w_co': out['v_w_co'], 'v_g_mlp': out['v_g_mlp'], 'v_w_up': out['v_w_up'], 'v_w_down': out['v_w_down']}


def _loss(weights, diff, rest, loss_target):
    with _jax.named_scope("forward"):
        args = {**rest, TWIN_DIFF_INPUT: diff, **{k: w.astype(_WEIGHT_DTYPES[k]) for k, w in weights.items()}}
        y = _forward(args)
    with _jax.named_scope("loss_head"):
        err = _jnp.square(y.astype(_jnp.float32) - loss_target)
        return 0.5 * _jnp.sum(_jnp.mean(err, axis=-1)) if err.ndim else 0.5 * err


def _adamw(w, g, m, v):
    m = ADAM_B1 * m + (1.0 - ADAM_B1) * g
    v = ADAM_B2 * v + (1.0 - ADAM_B2) * _jnp.square(g)
    m_hat = m / (1.0 - ADAM_B1 ** ADAM_STEP)
    v_hat = v / (1.0 - ADAM_B2 ** ADAM_STEP)
    delta = -ADAM_LR * (m_hat / (_jnp.sqrt(v_hat) + ADAM_EPS) + ADAM_WD * w)
    return delta, m, v


def reference(x, mem, positions, g_mix, w_in, g_q, g_k, g_attn_out, conv_w, conv_b, dt_bias, a_log, d_skip, g_ssm_out, w_out, g_cross, g_mem, w_cq, w_ckv, g_cq, g_ck, w_co, g_mlp, w_up, w_down, loss_target, m_g_mix, m_w_in, m_g_q, m_g_k, m_g_attn_out, m_conv_w, m_conv_b, m_dt_bias, m_a_log, m_d_skip, m_g_ssm_out, m_w_out, m_g_cross, m_g_mem, m_w_cq, m_w_ckv, m_g_cq, m_g_ck, m_w_co, m_g_mlp, m_w_up, m_w_down, v_g_mix, v_w_in, v_g_q, v_g_k, v_g_attn_out, v_conv_w, v_conv_b, v_dt_bias, v_a_log, v_d_skip, v_g_ssm_out, v_w_out, v_g_cross, v_g_mem, v_w_cq, v_w_ckv, v_g_cq, v_g_ck, v_w_co, v_g_mlp, v_w_up, v_w_down):
    given = dict(x=x, mem=mem, positions=positions, g_mix=g_mix, w_in=w_in, g_q=g_q, g_k=g_k, g_attn_out=g_attn_out, conv_w=conv_w, conv_b=conv_b, dt_bias=dt_bias, a_log=a_log, d_skip=d_skip, g_ssm_out=g_ssm_out, w_out=w_out, g_cross=g_cross, g_mem=g_mem, w_cq=w_cq, w_ckv=w_ckv, g_cq=g_cq, g_ck=g_ck, w_co=w_co, g_mlp=g_mlp, w_up=w_up, w_down=w_down, loss_target=loss_target, m_g_mix=m_g_mix, m_w_in=m_w_in, m_g_q=m_g_q, m_g_k=m_g_k, m_g_attn_out=m_g_attn_out, m_conv_w=m_conv_w, m_conv_b=m_conv_b, m_dt_bias=m_dt_bias, m_a_log=m_a_log, m_d_skip=m_d_skip, m_g_ssm_out=m_g_ssm_out, m_w_out=m_w_out, m_g_cross=m_g_cross, m_g_mem=m_g_mem, m_w_cq=m_w_cq, m_w_ckv=m_w_ckv, m_g_cq=m_g_cq, m_g_ck=m_g_ck, m_w_co=m_w_co, m_g_mlp=m_g_mlp, m_w_up=m_w_up, m_w_down=m_w_down, v_g_mix=v_g_mix, v_w_in=v_w_in, v_g_q=v_g_q, v_g_k=v_g_k, v_g_attn_out=v_g_attn_out, v_conv_w=v_conv_w, v_conv_b=v_conv_b, v_dt_bias=v_dt_bias, v_a_log=v_a_log, v_d_skip=v_d_skip, v_g_ssm_out=v_g_ssm_out, v_w_out=v_w_out, v_g_cross=v_g_cross, v_g_mem=v_g_mem, v_w_cq=v_w_cq, v_w_ckv=v_w_ckv, v_g_cq=v_g_cq, v_g_ck=v_g_ck, v_w_co=v_w_co, v_g_mlp=v_g_mlp, v_w_up=v_w_up, v_w_down=v_w_down)
    weights = {n: given[n] for n in TWIN_WEIGHTS}
    shared = {n: given[n] for n in SHARED_INPUTS}
    per_example = {n: given[n] for n in ['x', 'mem', 'positions']}
    grad_fn = _jax.value_and_grad(_loss, argnums=(0, 1))

    def one_microbatch(ex, loss_target):
        ex = dict(ex)
        diff = ex.pop(TWIN_DIFF_INPUT)
        return grad_fn(weights, diff, {**shared, **ex}, loss_target)

    if N_MICROBATCH == 1:
        loss, (grad_w, grad_x) = one_microbatch(per_example, given["loss_target"])
    else:
        def body(carry, xs):
            loss_sum, grad_sum = carry
            l_k, (gw_k, gx_k) = one_microbatch(xs[0], xs[1])
            with _jax.named_scope("update"):
                return (loss_sum + l_k, _jax.tree.map(_jnp.add, grad_sum, gw_k)), gx_k

        init = (_jnp.zeros((), _jnp.float32), _jax.tree.map(_jnp.zeros_like, weights))
        (loss, grad_w), grad_x = _jax.lax.scan(body, init, (per_example, given["loss_target"]))
    with _jax.named_scope("update"):
        delta_w, new_m, new_v = {}, {}, {}
        for n in TWIN_WEIGHTS:
            delta_w[n], new_m[n], new_v[n] = _adamw(weights[n], grad_w[n], given["m_" + n], given["v_" + n])
    return (loss, grad_x, *[grad_w[n] for n in TWIN_WEIGHTS], *[delta_w[n] for n in TWIN_WEIGHTS],
            *[new_m[n] for n in TWIN_WEIGHTS], *[new_v[n] for n in TWIN_WEIGHTS])
```

```python
import functools
import math

import jax
import jax.numpy as jnp
from jax import lax
from jax.experimental import pallas as pl
from jax.experimental.pallas import tpu as pltpu

F32 = jnp.float32
BF16 = jnp.bfloat16

SEQ = 2048
D_MODEL = 2048
HEAD = 64
D_ATTN = 1024
D_SSM = 1024
N_GROUPS = 4
N_STATE = 128
CHUNK = 128
ATT_BLK = 128
N_MEM = 256
D_CROSS = 512
D_FF = 8192
D_MAIN = 6144
N_DT = 16
DT_PAD = 512
ROT = 16
ROPE_THETA = 500000.0
EPS = 1e-6
NEG = -1e30
BRANCH_BLOCKS = (16, 4, 1)
DILATIONS = (1, 4, 16)

ADAM_LR, ADAM_B1, ADAM_B2, ADAM_EPS, ADAM_WD, ADAM_STEP = 0.001, 0.9, 0.999, 1e-08, 0.01, 10

VMEM_LIMIT = 56 * 1024 * 1024
MESH = pl.DeviceIdType.MESH


def _params(sem, **kw):
    return pltpu.CompilerParams(dimension_semantics=sem, vmem_limit_bytes=VMEM_LIMIT, **kw)


def _bdot(a, b, dims):
    return lax.dot_general(a.astype(BF16), b.astype(BF16), (dims, ((), ())), preferred_element_type=F32)


def _fdot(a, b, dims):
    return lax.dot_general(a, b, (dims, ((), ())), preferred_element_type=F32, precision=lax.Precision.HIGHEST)


NN = ((1,), (0,))
NT = ((1,), (1,))
TN = ((0,), (0,))


def _tile(n, want):
    t = min(n, want)
    while n % t:
        t //= 2
    return t


def _matmul(a, b, *, mode, name, outs, extra=(), epilogue=None, col_shards=1, tm=512, tn=1024, tk=1024):
    if mode == "nn":
        (m, k), n = a.shape, b.shape[1]
    elif mode == "nt":
        (m, k), n = a.shape, b.shape[0]
    else:
        (k, m), n = a.shape, b.shape[1]
    tm, tn, tk = _tile(m, tm), _tile(n // col_shards, tn), _tile(k, tk)
    nk = k // tk
    per_shard = n // col_shards // tn
    dims = {"nn": NN, "nt": NT, "tn": TN}[mode]
    a_spec = pl.BlockSpec((tk, tm), lambda i, j, kk: (kk, i)) if mode == "tn" else pl.BlockSpec((tm, tk), lambda i, j, kk: (i, kk))
    b_spec = pl.BlockSpec((tn, tk), lambda i, j, kk: (j, kk)) if mode == "nt" else pl.BlockSpec((tk, tn), lambda i, j, kk: (kk, j))
    o_spec = pl.BlockSpec((tm, tn), lambda i, j, kk: (i, j))
    n_extra, n_out = len(extra), len(outs)

    def body(a_ref, b_ref, *rest):
        extra_refs, out_refs, acc_ref = rest[:n_extra], rest[n_extra:n_extra + n_out], rest[-1]
        kk = pl.program_id(2)

        @pl.when(kk == 0)
        def _():
            acc_ref[...] = jnp.zeros_like(acc_ref)

        acc_ref[...] += _bdot(a_ref[...], b_ref[...], dims)

        @pl.when(kk == nk - 1)
        def _():
            acc = acc_ref[...]
            res = (acc,) if epilogue is None else epilogue(acc, *[e[...] for e in extra_refs])
            for o_ref, r in zip(out_refs, res):
                o_ref[...] = r.astype(o_ref.dtype)

    if col_shards == 1:
        out_specs, out_dims = [o_spec] * n_out, (m, n)
    else:
        sharded = pl.BlockSpec((None, tm, tn), lambda i, j, kk: (j // per_shard, i, j % per_shard))
        out_specs, out_dims = [sharded] * n_out, (col_shards, m, n // col_shards)
    res = pl.pallas_call(
        body, name=name, grid=(m // tm, n // tn, nk),
        in_specs=[a_spec, b_spec] + [o_spec] * n_extra,
        out_specs=out_specs,
        out_shape=[jax.ShapeDtypeStruct(out_dims, dt) for dt in outs],
        scratch_shapes=[pltpu.VMEM((tm, tn), F32)],
        compiler_params=_params(("parallel", "parallel", "arbitrary")),
    )(a, b, *extra)
    return res[0] if n_out == 1 else res


def _row_spec(tr, bw, cb, per_group):
    return pl.BlockSpec((tr, bw), (lambda g, i: (i, cb + g)) if per_group else (lambda g, i: (i, cb)))


def _vec_spec(bw, cb, per_group):
    return pl.BlockSpec((1, bw), (lambda g, i: (0, cb + g)) if per_group else (lambda g, i: (0, cb)))


def _rowwise(fn, rows, vecs, outs, *, name, n_rows=SEQ, tr=256, groups=1):
    n_r, n_v = len(rows), len(vecs)

    def body(*refs):
        vals = [r[...].astype(F32) for r in refs[:n_r + n_v]]
        res = fn(*vals)
        for o_ref, r in zip(refs[n_r + n_v:], res):
            o_ref[...] = r.astype(o_ref.dtype)

    res = pl.pallas_call(
        body, name=name, grid=(groups, n_rows // tr),
        in_specs=[_row_spec(tr, bw, cb, pg) for _, bw, cb, pg in rows] + [_vec_spec(bw, cb, pg) for _, bw, cb, pg in vecs],
        out_specs=[_row_spec(tr, bw, cb, pg) for _, _, bw, cb, pg in outs],
        out_shape=[jax.ShapeDtypeStruct((n_rows, w), dt) for w, dt, _, _, _ in outs],
        compiler_params=_params(("parallel", "parallel")),
    )(*[r[0] for r in rows], *[v[0] for v in vecs])
    return res


def _rowwise_vjp(fn, rows, vecs, cts, row_grads, vec_grads, *, name, n_rows=SEQ, tr=256, groups=1):
    n_r, n_v = len(rows), len(vecs)
    ct_ops = [op for group in cts for op in group]
    ct_sizes = [len(group) for group in cts]
    res_ops = [g[6] for g in row_grads if g[6] is not None]
    n_ct, n_res, n_rg = len(ct_ops), len(res_ops), len(row_grads)

    def body(*refs):
        vals = [r[...].astype(F32) for r in refs[:n_r + n_v]]
        pos = n_r + n_v
        ct_vals = []
        for size in ct_sizes:
            acc = refs[pos][...].astype(F32)
            for t in range(1, size):
                acc = acc + refs[pos + t][...].astype(F32)
            ct_vals.append(acc)
            pos += size
        res_refs = refs[pos:pos + n_res]
        out_refs = refs[pos + n_res:]
        _, pullback = jax.vjp(fn, *vals)
        grads = pullback(tuple(ct_vals))
        r_i = 0
        for o_ref, g in zip(out_refs[:n_rg], row_grads):
            val = grads[g[0]]
            if g[6] is not None:
                val = val + res_refs[r_i][...].astype(F32)
                r_i += 1
            o_ref[...] = val.astype(o_ref.dtype)
        first = (pl.program_id(1) == 0)
        for o_ref, g in zip(out_refs[n_rg:], vec_grads):
            val = jnp.sum(grads[n_r + g[0]], axis=0, keepdims=True)
            init = first if g[4] else jnp.logical_and(first, pl.program_id(0) == 0)

            @pl.when(init)
            def _(o_ref=o_ref, val=val):
                o_ref[...] = val

            @pl.when(jnp.logical_not(init))
            def _(o_ref=o_ref, val=val):
                o_ref[...] += val

    in_specs = [_row_spec(tr, bw, cb, pg) for _, bw, cb, pg in rows] + [_vec_spec(bw, cb, pg) for _, bw, cb, pg in vecs]
    in_specs += [_row_spec(tr, bw, cb, pg) for _, bw, cb, pg in ct_ops + res_ops]
    out_specs = [_row_spec(tr, g[3], g[4], g[5]) for g in row_grads] + [_vec_spec(g[2], g[3], g[4]) for g in vec_grads]
    out_shape = [jax.ShapeDtypeStruct((n_rows, g[1]), g[2]) for g in row_grads]
    out_shape += [jax.ShapeDtypeStruct((1, g[1]), F32) for g in vec_grads]
    return pl.pallas_call(
        body, name=name, grid=(groups, n_rows // tr),
        in_specs=in_specs, out_specs=out_specs, out_shape=out_shape,
        compiler_params=_params(("arbitrary", "arbitrary")),
    )(*[r[0] for r in rows], *[v[0] for v in vecs], *[c[0] for c in ct_ops], *[r[0] for r in res_ops])


def _full(arr, width=None):
    return (arr, arr.shape[1] if width is None else width, 0, False)


def _make_xor(sh):
    def raw(x):
        n = x.shape[-1]
        lane = lax.broadcasted_iota(jnp.int32, x.shape, x.ndim - 1)
        up = pltpu.roll(x, n - sh, x.ndim - 1)
        down = pltpu.roll(x, sh, x.ndim - 1)
        return jnp.where((lane & sh) == 0, up, down)

    f = jax.custom_vjp(raw)
    f.defvjp(lambda x: (raw(x), None), lambda _, ct: (raw(ct),))
    return f


_XOR = {sh: _make_xor(sh) for sh in (1, 2, 4, 8, 16, 32)}


def _head_sum(x):
    for sh in (1, 2, 4, 8, 16, 32):
        x = x + _XOR[sh](x)
    return x


def _rms(x, g):
    return x * lax.rsqrt(jnp.mean(x * x, axis=-1, keepdims=True) + EPS) * g


def _head_rms_rope(x, g, cos, sin, scale):
    y = x * lax.rsqrt(_head_sum(x * x) * (1.0 / HEAD) + EPS) * g
    return (y * cos + _XOR[8](y) * sin) * scale


def _qk_fn(q, k, v, cos, sin, gq, gk):
    return (_head_rms_rope(q, gq, cos, sin, HEAD ** -0.5), _head_rms_rope(k, gk, cos, sin, 1.0), v)


def _norm_fn(x, g):
    return (_rms(x, g),)


def _merge_fn(o0, o1, o2, l0, l1, l2, g):
    m = jnp.maximum(jnp.maximum(l0, l1), l2)
    e0, e1, e2 = jnp.exp(l0 - m), jnp.exp(l1 - m), jnp.exp(l2 - m)
    mix = (e0 * o0 + e1 * o1 + e2 * o2) / (e0 + e1 + e2)
    return (_rms(mix, g),)


def _gate_fn(y, z, g):
    return (_rms(y * (z * jax.nn.sigmoid(z)), g),)


def _attn_head(q, kc, kp, vc, vp, has_prev):
    qi = lax.broadcasted_iota(jnp.int32, (ATT_BLK, ATT_BLK), 0)
    kj = lax.broadcasted_iota(jnp.int32, (ATT_BLK, ATT_BLK), 1)
    s_c = jnp.where(qi >= kj, _bdot(q, kc, NT), NEG)
    s_p = jnp.where(jnp.logical_and(kj >= qi, has_prev), _bdot(q, kp, NT), NEG)
    m = jnp.maximum(jnp.max(s_c, axis=-1, keepdims=True), jnp.max(s_p, axis=-1, keepdims=True))
    p_c, p_p = jnp.exp(s_c - m), jnp.exp(s_p - m)
    den = jnp.sum(p_c, axis=-1, keepdims=True) + jnp.sum(p_p, axis=-1, keepdims=True)
    o = (_bdot(p_c, vc, NN) + _bdot(p_p, vp, NN)) / den
    lse = m + jnp.log(den)
    return o, jnp.broadcast_to(lse, (ATT_BLK, HEAD))


def _branch_has_prev(b, n):
    blocks = lax.shift_right_logical(jnp.int32(16), 2 * b)
    return (n & (blocks - 1)) != 0


def _attn_specs():
    cur = pl.BlockSpec((1, ATT_BLK, 128), lambda b, h, n: (b, n, h))
    prev = pl.BlockSpec((1, ATT_BLK, 128), lambda b, h, n: (b, jnp.maximum(n - 1, 0), h))
    return cur, prev


def _attention_fwd(qp, kp, vp):
    cur, prev = _attn_specs()

    def body(q_ref, kc_ref, kp_ref, vc_ref, vp_ref, o_ref, l_ref):
        has_prev = _branch_has_prev(pl.program_id(0), pl.program_id(2))
        for h in range(2):
            sl = pl.ds(h * HEAD, HEAD)
            o, lse = _attn_head(*[r[0, :, sl].astype(F32) for r in (q_ref, kc_ref, kp_ref, vc_ref, vp_ref)], has_prev)
            o_ref[0, :, sl] = o
            l_ref[0, :, sl] = lse

    return pl.pallas_call(
        body, name="attn_fwd", grid=(3, D_ATTN // 128, SEQ // ATT_BLK),
        in_specs=[cur, cur, prev, cur, prev], out_specs=[cur, cur],
        out_shape=[jax.ShapeDtypeStruct((3, SEQ, D_ATTN), F32)] * 2,
        compiler_params=_params(("parallel", "parallel", "parallel")),
    )(qp, kp, kp, vp, vp)


def _attention_bwd(qp, kp, vp, do, dl):
    cur, prev = _attn_specs()
    whole = pl.BlockSpec((1, SEQ, 128), lambda b, h, n: (b, 0, h))

    def body(q_ref, kc_ref, kp_ref, vc_ref, vp_ref, do_ref, dl_ref, dq_ref, dk_ref, dv_ref):
        n = pl.program_id(2)
        has_prev = _branch_has_prev(pl.program_id(0), n)

        @pl.when(n == 0)
        def _():
            dk_ref[...] = jnp.zeros_like(dk_ref)
            dv_ref[...] = jnp.zeros_like(dv_ref)

        rows_c = pl.ds(pl.multiple_of(n * ATT_BLK, ATT_BLK), ATT_BLK)
        rows_p = pl.ds(pl.multiple_of(jnp.maximum(n - 1, 0) * ATT_BLK, ATT_BLK), ATT_BLK)
        for h in range(2):
            sl = pl.ds(h * HEAD, HEAD)
            vals = [r[0, :, sl].astype(F32) for r in (q_ref, kc_ref, kp_ref, vc_ref, vp_ref)]
            _, pullback = jax.vjp(lambda *a: _attn_head(*a, has_prev), *vals)
            dq, dkc, dkp, dvc, dvp = pullback((do_ref[0, :, sl], dl_ref[0, :, sl]))
            dq_ref[0, :, sl] = dq
            dk_ref[0, rows_c, sl] += dkc
            dv_ref[0, rows_c, sl] += dvc
            dk_ref[0, rows_p, sl] += dkp
            dv_ref[0, rows_p, sl] += dvp

    return pl.pallas_call(
        body, name="attn_bwd", grid=(3, D_ATTN // 128, SEQ // ATT_BLK),
        in_specs=[cur, cur, prev, cur, prev, cur, cur], out_specs=[cur, whole, whole],
        out_shape=[jax.ShapeDtypeStruct((3, SEQ, D_ATTN), F32)] * 3,
        compiler_params=_params(("parallel", "parallel", "arbitrary")),
    )(qp, kp, kp, vp, vp, do, dl)


def _permute_one(t, b):
    r, nb = DILATIONS[b], BRANCH_BLOCKS[b]
    return t.reshape(nb, ATT_BLK, r, t.shape[-1]).transpose(2, 0, 1, 3).reshape(SEQ, t.shape[-1])


def _unpermute_one(t, b):
    r, nb = DILATIONS[b], BRANCH_BLOCKS[b]
    return t.reshape(r, nb, ATT_BLK, t.shape[-1]).transpose(1, 2, 0, 3).reshape(SEQ, t.shape[-1])


def _permute(t, inverse=False):
    if inverse:
        return [_unpermute_one(t[b], b) for b in range(3)]
    return jnp.stack([_permute_one(t, b) for b in range(3)])


CONV_COLS = 256
XBC_BLOCK0 = 4096 // CONV_COLS


def _shift_rows(x, s):
    n = x.shape[0]
    t = lax.broadcasted_iota(jnp.int32, x.shape, 0)
    if s >= 0:
        return jnp.where(t >= s, pltpu.roll(x, s, 0), 0.0)
    return jnp.where(t < n + s, pltpu.roll(x, n + s, 0), 0.0)


def _conv_pre(x, w_ref, b_ref):
    pre = b_ref[...] + w_ref[3:4, :] * x
    for k in range(3):
        pre = pre + w_ref[k:k + 1, :] * _shift_rows(x, 3 - k)
    return pre


def _conv_fwd(proj, conv_w, conv_b):
    cols = conv_w.shape[1]

    def body(x_ref, w_ref, b_ref, o_ref):
        pre = _conv_pre(x_ref[...], w_ref, b_ref)
        o_ref[...] = pre * jax.nn.sigmoid(pre)

    blk = pl.BlockSpec((SEQ, CONV_COLS), lambda j: (0, j))
    return pl.pallas_call(
        body, name="conv_fwd", grid=(cols // CONV_COLS,),
        in_specs=[pl.BlockSpec((SEQ, CONV_COLS), lambda j: (0, XBC_BLOCK0 + j)),
                  pl.BlockSpec((4, CONV_COLS), lambda j: (0, j)), pl.BlockSpec((1, CONV_COLS), lambda j: (0, j))],
        out_specs=blk, out_shape=jax.ShapeDtypeStruct((SEQ, cols), F32),
        compiler_params=_params(("parallel",)),
    )(proj, conv_w, conv_b)


def _conv_bwd(proj, conv_w, conv_b, dy):
    cols = conv_w.shape[1]

    def body(x_ref, w_ref, b_ref, dy_ref, dx_ref, dw_ref, db_ref):
        x = x_ref[...]
        pre = _conv_pre(x, w_ref, b_ref)
        sg = jax.nn.sigmoid(pre)
        dpre = dy_ref[...] * (sg * (1.0 + pre * (1.0 - sg)))
        db_ref[...] = jnp.sum(dpre, axis=0, keepdims=True)
        dx = w_ref[3:4, :] * dpre
        dw_ref[3:4, :] = jnp.sum(dpre * x, axis=0, keepdims=True)
        for k in range(3):
            dx = dx + w_ref[k:k + 1, :] * _shift_rows(dpre, k - 3)
            dw_ref[k:k + 1, :] = jnp.sum(dpre * _shift_rows(x, 3 - k), axis=0, keepdims=True)
        dw_ref[4:8, :] = jnp.zeros((4, CONV_COLS), F32)
        dx_ref[...] = dx.astype(dx_ref.dtype)

    blk = pl.BlockSpec((SEQ, CONV_COLS), lambda j: (0, j))
    return pl.pallas_call(
        body, name="conv_bwd", grid=(cols // CONV_COLS,),
        in_specs=[pl.BlockSpec((SEQ, CONV_COLS), lambda j: (0, XBC_BLOCK0 + j)),
                  pl.BlockSpec((4, CONV_COLS), lambda j: (0, j)), pl.BlockSpec((1, CONV_COLS), lambda j: (0, j)), blk],
        out_specs=[blk, pl.BlockSpec((8, CONV_COLS), lambda j: (0, j)), pl.BlockSpec((1, CONV_COLS), lambda j: (0, j))],
        out_shape=[jax.ShapeDtypeStruct((SEQ, cols), BF16), jax.ShapeDtypeStruct((8, cols), F32),
                   jax.ShapeDtypeStruct((1, cols), F32)],
        compiler_params=_params(("parallel",)),
    )(proj, conv_w, conv_b, dy)


HEADS_PER_GROUP = 4


def _ssd_chunk(x0, x1, x2, x3, bm, cm, dtr, bias, alog, dsk, h0, h1, h2, h3):
    xs, hs = (x0, x1, x2, x3), (h0, h1, h2, h3)
    row = lax.broadcasted_iota(jnp.int32, (CHUNK, CHUNK), 0)
    col = lax.broadcasted_iota(jnp.int32, (CHUNK, CHUNK), 1)
    causal = row >= col
    tril = causal.astype(F32)
    z = dtr + bias
    dt = jnp.maximum(z, 0.0) + jnp.log(1.0 + jnp.exp(-jnp.abs(z)))
    a = -jnp.exp(alog)
    acs = _fdot(tril, dt * a, NN)
    acs_t, dt_t = acs.T, dt.T
    cb = _bdot(cm, bm, NT)
    lane = lax.broadcasted_iota(jnp.int32, (1, CHUNK), 1)
    sub = lax.broadcasted_iota(jnp.int32, (CHUNK, 1), 0)
    ys, hn = [], []
    for j in range(HEADS_PER_GROUP):
        on_lane, on_sub = (lane == j).astype(F32), (sub == j).astype(F32)
        acs_c = jnp.sum(acs * on_lane, axis=1, keepdims=True)
        dt_c = jnp.sum(dt * on_lane, axis=1, keepdims=True)
        acs_r = jnp.sum(acs_t * on_sub, axis=0, keepdims=True)
        dt_r = jnp.sum(dt_t * on_sub, axis=0, keepdims=True)
        acs_last = jnp.sum(acs_c * (sub == CHUNK - 1).astype(F32), axis=0, keepdims=True)
        d_j = jnp.sum(dsk * on_lane, axis=1, keepdims=True)
        decay = jnp.exp(jnp.where(causal, acs_c - acs_r, NEG))
        w = cb * decay * dt_r
        y_diag = _bdot(w, xs[j], NN)
        y_off = _bdot(cm, hs[j], NT) * jnp.exp(acs_c)
        ys.append(y_diag + y_off + d_j * xs[j])
        state = _bdot(xs[j] * (jnp.exp(acs_last - acs_c) * dt_c), bm, TN)
        hn.append(hs[j] * jnp.exp(acs_last) + state)
    return (*ys, *hn)


def _ssd_specs(reverse):
    n_chunks = SEQ // CHUNK
    c_of = (lambda c: n_chunks - 1 - c) if reverse else (lambda c: c)
    x_spec = pl.BlockSpec((CHUNK, 256), lambda g, c: (c_of(c), g))
    b_spec = pl.BlockSpec((CHUNK, N_STATE), lambda g, c: (c_of(c), 8 + g))
    c_spec = pl.BlockSpec((CHUNK, N_STATE), lambda g, c: (c_of(c), 12 + g))
    dt_spec = pl.BlockSpec((CHUNK, 128), lambda g, c: (c_of(c), g))
    vec_spec = pl.BlockSpec((1, 128), lambda g, c: (0, g))
    h_spec = pl.BlockSpec((1, 1, HEADS_PER_GROUP, HEAD, N_STATE), lambda g, c: (c_of(c), g, 0, 0, 0))
    return x_spec, b_spec, c_spec, dt_spec, vec_spec, h_spec


def _ssd_fwd(xbc, dt_raw, bias, alog, dsk):
    x_spec, b_spec, c_spec, dt_spec, vec_spec, h_spec = _ssd_specs(False)

    def body(x_ref, b_ref, c_ref, dt_ref, bias_ref, alog_ref, dsk_ref, y_ref, hin_ref, h_scr):
        @pl.when(pl.program_id(1) == 0)
        def _():
            h_scr[...] = jnp.zeros_like(h_scr)

        hs = [h_scr[j] for j in range(HEADS_PER_GROUP)]
        for j in range(HEADS_PER_GROUP):
            hin_ref[0, 0, j] = hs[j]
        xs = [x_ref[:, pl.ds(j * HEAD, HEAD)] for j in range(HEADS_PER_GROUP)]
        res = _ssd_chunk(*xs, b_ref[...], c_ref[...], dt_ref[...], bias_ref[...], alog_ref[...], dsk_ref[...], *hs)
        for j in range(HEADS_PER_GROUP):
            y_ref[:, pl.ds(j * HEAD, HEAD)] = res[j]
            h_scr[j] = res[HEADS_PER_GROUP + j]

    return pl.pallas_call(
        body, name="ssd_fwd", grid=(N_GROUPS, SEQ // CHUNK),
        in_specs=[x_spec, b_spec, c_spec, dt_spec, vec_spec, vec_spec, vec_spec],
        out_specs=[x_spec, h_spec],
        out_shape=[jax.ShapeDtypeStruct((SEQ, D_SSM), F32),
                   jax.ShapeDtypeStruct((SEQ // CHUNK, N_GROUPS, HEADS_PER_GROUP, HEAD, N_STATE), F32)],
        scratch_shapes=[pltpu.VMEM((HEADS_PER_GROUP, HEAD, N_STATE), F32)],
        compiler_params=_params(("parallel", "arbitrary")),
    )(xbc, xbc, xbc, dt_raw, bias, alog, dsk)


def _ssd_bwd(xbc, dt_raw, bias, alog, dsk, h_in, dy):
    x_spec, b_spec, c_spec, dt_spec, vec_spec, h_spec = _ssd_specs(True)
    dxbc_x = pl.BlockSpec((CHUNK, 256), x_spec.index_map)

    def body(x_ref, b_ref, c_ref, dt_ref, bias_ref, alog_ref, dsk_ref, hin_ref, dy_ref,
             dx_ref, db_ref, dc_ref, ddt_ref, dbias_ref, dalog_ref, ddsk_ref, dh_scr):
        first = pl.program_id(1) == 0

        @pl.when(first)
        def _():
            dh_scr[...] = jnp.zeros_like(dh_scr)

        xs = [x_ref[:, pl.ds(j * HEAD, HEAD)] for j in range(HEADS_PER_GROUP)]
        hs = [hin_ref[0, 0, j] for j in range(HEADS_PER_GROUP)]
        cts = [dy_ref[:, pl.ds(j * HEAD, HEAD)] for j in range(HEADS_PER_GROUP)] + [dh_scr[j] for j in range(HEADS_PER_GROUP)]
        _, pullback = jax.vjp(_ssd_chunk, *xs, b_ref[...], c_ref[...], dt_ref[...], bias_ref[...], alog_ref[...],
                              dsk_ref[...], *hs)
        g = pullback(tuple(cts))
        for j in range(HEADS_PER_GROUP):
            dx_ref[:, pl.ds(j * HEAD, HEAD)] = g[j]
            dh_scr[j] = g[10 + j]
        db_ref[...] = g[4]
        dc_ref[...] = g[5]
        ddt_ref[...] = g[6].astype(ddt_ref.dtype)
        for o_ref, val in ((dbias_ref, g[7]), (dalog_ref, g[8]), (ddsk_ref, g[9])):
            @pl.when(first)
            def _(o_ref=o_ref, val=val):
                o_ref[...] = val

            @pl.when(jnp.logical_not(first))
            def _(o_ref=o_ref, val=val):
                o_ref[...] += val

    n_chunks = SEQ // CHUNK
    out_b = pl.BlockSpec((CHUNK, N_STATE), lambda g, c: (n_chunks - 1 - c, g))
    res = pl.pallas_call(
        body, name="ssd_bwd", grid=(N_GROUPS, n_chunks),
        in_specs=[x_spec, b_spec, c_spec, dt_spec, vec_spec, vec_spec, vec_spec, h_spec, x_spec],
        out_specs=[dxbc_x, out_b, out_b, dt_spec, vec_spec, vec_spec, vec_spec],
        out_shape=[jax.ShapeDtypeStruct((SEQ, D_SSM), F32), jax.ShapeDtypeStruct((SEQ, N_GROUPS * N_STATE), F32),
                   jax.ShapeDtypeStruct((SEQ, N_GROUPS * N_STATE), F32), jax.ShapeDtypeStruct((SEQ, DT_PAD), BF16),
                   jax.ShapeDtypeStruct((1, DT_PAD), F32), jax.ShapeDtypeStruct((1, DT_PAD), F32),
                   jax.ShapeDtypeStruct((1, DT_PAD), F32)],
        scratch_shapes=[pltpu.VMEM((HEADS_PER_GROUP, HEAD, N_STATE), F32)],
        compiler_params=_params(("parallel", "arbitrary")),
    )(xbc, xbc, xbc, dt_raw, bias, alog, dsk, h_in, dy)
    return res


CROSS_HEAD = 128
CROSS_ROWS = 512


def _cross_head(q, k, v, gq, gk):
    qn = _rms(q, gq) * (CROSS_HEAD ** -0.5)
    kn = _rms(k, gk)
    s = _bdot(qn, kn, NT)
    p = jnp.exp(s - jnp.max(s, axis=-1, keepdims=True))
    return _bdot(p, v, NN) / jnp.sum(p, axis=-1, keepdims=True)


def _cross_specs():
    q_spec = pl.BlockSpec((CROSS_ROWS, CROSS_HEAD), lambda h, i: (i, h))
    k_spec = pl.BlockSpec((N_MEM, CROSS_HEAD), lambda h, i: (0, h))
    v_spec = pl.BlockSpec((N_MEM, CROSS_HEAD), lambda h, i: (0, 4 + h))
    g_spec = pl.BlockSpec((1, CROSS_HEAD), lambda h, i: (0, 0))
    return q_spec, k_spec, v_spec, g_spec


def _cross_fwd(qc, kv, gq, gk):
    q_spec, k_spec, v_spec, g_spec = _cross_specs()

    def body(q_ref, k_ref, v_ref, gq_ref, gk_ref, o_ref):
        o_ref[...] = _cross_head(q_ref[...], k_ref[...], v_ref[...], gq_ref[...], gk_ref[...]).astype(o_ref.dtype)

    return pl.pallas_call(
        body, name="cross_fwd", grid=(4, SEQ // CROSS_ROWS),
        in_specs=[q_spec, k_spec, v_spec, g_spec, g_spec], out_specs=q_spec,
        out_shape=jax.ShapeDtypeStruct((SEQ, D_CROSS), BF16),
        compiler_params=_params(("parallel", "parallel")),
    )(qc, kv, kv, gq, gk)


def _cross_bwd(qc, kv, gq, gk, do):
    q_spec, k_spec, v_spec, g_spec = _cross_specs()

    def body(q_ref, k_ref, v_ref, gq_ref, gk_ref, do_ref, dq_ref, dk_ref, dv_ref, dgq_ref, dgk_ref):
        _, pullback = jax.vjp(_cross_head, q_ref[...], k_ref[...], v_ref[...], gq_ref[...], gk_ref[...])
        dq, dk, dv, dgq, dgk = pullback(do_ref[...].astype(F32))
        dq_ref[...] = dq.astype(dq_ref.dtype)
        row0 = pl.program_id(1) == 0
        all0 = jnp.logical_and(row0, pl.program_id(0) == 0)
        for o_ref, val, init in ((dk_ref, dk, row0), (dv_ref, dv, row0), (dgq_ref, dgq, all0), (dgk_ref, dgk, all0)):
            @pl.when(init)
            def _(o_ref=o_ref, val=val):
                o_ref[...] = val

            @pl.when(jnp.logical_not(init))
            def _(o_ref=o_ref, val=val):
                o_ref[...] += val

    return pl.pallas_call(
        body, name="cross_bwd", grid=(4, SEQ // CROSS_ROWS),
        in_specs=[q_spec, k_spec, v_spec, g_spec, g_spec, q_spec],
        out_specs=[q_spec, k_spec, k_spec, g_spec, g_spec],
        out_shape=[jax.ShapeDtypeStruct((SEQ, D_CROSS), BF16), jax.ShapeDtypeStruct((N_MEM, D_CROSS), F32),
                   jax.ShapeDtypeStruct((N_MEM, D_CROSS), F32), jax.ShapeDtypeStruct((1, CROSS_HEAD), F32),
                   jax.ShapeDtypeStruct((1, CROSS_HEAD), F32)],
        compiler_params=_params(("arbitrary", "arbitrary")),
    )(qc, kv, kv, gq, gk, do)


def _loss_head(y, target):
    tr = 256

    def body(y_ref, t_ref, dy_ref, dyb_ref, loss_ref):
        err = y_ref[...] - t_ref[...]
        dy = err * (1.0 / D_MODEL)
        dy_ref[...] = dy
        dyb_ref[...] = dy.astype(BF16)
        part = jnp.sum(jnp.sum(err * err, axis=1, keepdims=True), axis=0, keepdims=True) * (0.5 / D_MODEL)
        part = jnp.broadcast_to(part, (1, 128))

        @pl.when(pl.program_id(0) == 0)
        def _():
            loss_ref[...] = part

        @pl.when(pl.program_id(0) != 0)
        def _():
            loss_ref[...] += part

    blk = pl.BlockSpec((tr, D_MODEL), lambda i: (i, 0))
    return pl.pallas_call(
        body, name="loss_head", grid=(SEQ // tr,),
        in_specs=[blk, blk], out_specs=[blk, blk, pl.BlockSpec((1, 128), lambda i: (0, 0))],
        out_shape=[jax.ShapeDtypeStruct((SEQ, D_MODEL), F32), jax.ShapeDtypeStruct((SEQ, D_MODEL), BF16),
                   jax.ShapeDtypeStruct((1, 128), F32)],
        compiler_params=_params(("arbitrary",)),
    )(y, target)


def _pad_heads(v):
    return jnp.pad(v.reshape(N_GROUPS, HEADS_PER_GROUP), ((0, 0), (0, 128 - HEADS_PER_GROUP))).reshape(1, DT_PAD)


def _unpad_heads(v):
    return v.reshape(v.shape[0], N_GROUPS, 128)[:, :, :HEADS_PER_GROUP].reshape(v.shape[0], N_DT)


def _rope_tables(positions):
    half = ROT // 2
    inv_freq = ROPE_THETA ** (-2.0 * jnp.arange(half, dtype=F32) / ROT)
    ang = positions.reshape(SEQ, 1).astype(F32) * inv_freq
    cos, sin = jnp.cos(ang), jnp.sin(ang)
    ones, zeros = jnp.ones((SEQ, HEAD - ROT), F32), jnp.zeros((SEQ, HEAD - ROT), F32)
    cos_h = jnp.concatenate([cos, cos, ones], axis=1)
    sin_h = jnp.concatenate([-sin, sin, zeros], axis=1)
    return jnp.tile(cos_h, (1, 2)), jnp.tile(sin_h, (1, 2))


def _add_res(acc, res):
    return (acc + res,)


def _local_step(x, mem, positions, target, p, w):
    grads = {}
    cos, sin = _rope_tables(positions)
    gq2, gk2 = jnp.tile(p["g_q"], (1, 2)), jnp.tile(p["g_k"], (1, 2))
    bias, alog, dsk = _pad_heads(p["dt_bias"]), _pad_heads(p["a_log"]), _pad_heads(p["d_skip"])
    norm_out = [(D_MODEL, BF16, D_MODEL, 0, False)]

    h = _rowwise(_norm_fn, [_full(x)], [_full(p["g_mix"])], norm_out, name="norm_in")[0]
    proj = _matmul(h, w["w_main"], mode="nn", name="in_proj", outs=[F32])
    dt_raw = _matmul(h, w["w_dt"], mode="nn", name="dt_proj", outs=[F32])
    qk_rows = [(proj, 128, 0, True), (proj, 128, 8, True), (proj, 128, 16, True), _full(cos), _full(sin)]
    qk_vecs = [_full(gq2), _full(gk2)]
    qn, kn, vb = _rowwise(_qk_fn, qk_rows, qk_vecs, [(D_ATTN, BF16, 128, 0, True)] * 3, name="qk_prep", groups=8)
    qp, kp, vp = _permute(qn), _permute(kn), _permute(vb)
    o_p, l_p = _attention_fwd(qp, kp, vp)
    o_b, l_b = _permute(o_p, inverse=True), _permute(l_p, inverse=True)
    merge_rows = [_full(t) for t in o_b + l_b]
    attn = _rowwise(_merge_fn, merge_rows, [_full(p["g_attn_out"])], [(D_ATTN, BF16, D_ATTN, 0, False)], name="attn_merge")[0]
    xbc = _conv_fwd(proj, p["conv_w"], p["conv_b"])
    y_ssd, h_in = _ssd_fwd(xbc, dt_raw, bias, alog, dsk)
    gate_rows = [(y_ssd, 256, 0, True), (proj, 256, 12, True)]
    gate_vecs = [(p["g_ssm_out"], 256, 0, True)]
    ssm = _rowwise(_gate_fn, gate_rows, gate_vecs, [(D_SSM, BF16, 256, 0, True)], name="ssm_gate", groups=4)[0]
    mix = jnp.concatenate([attn, ssm], axis=1)
    x1 = _matmul(mix, w["w_out"], mode="nn", name="out_proj", outs=[F32], extra=(x,), epilogue=_add_res)
    hc = _rowwise(_norm_fn, [_full(x1)], [_full(p["g_cross"])], norm_out, name="norm_cross")[0]
    memh = _rowwise(_norm_fn, [_full(mem)], [_full(p["g_mem"])], norm_out, name="norm_mem", n_rows=N_MEM)[0]
    qc = _matmul(hc, w["w_cq"], mode="nn", name="cq_proj", outs=[F32])
    kv = _matmul(memh, w["w_ckv"], mode="nn", name="ckv_proj", outs=[F32])
    oc = _cross_fwd(qc, kv, p["g_cq"], p["g_ck"])
    x2 = _matmul(oc, w["w_co"], mode="nn", name="co_proj", outs=[F32], extra=(x1,), epilogue=_add_res)
    hm = _rowwise(_norm_fn, [_full(x2)], [_full(p["g_mlp"])], norm_out, name="norm_mlp")[0]
    u, act = _matmul(hm, w["w_up"], mode="nn", name="up_proj", outs=[F32, BF16],
                     epilogue=lambda acc: (acc, jnp.square(jnp.maximum(acc, 0.0))))
    x3 = _matmul(act, w["w_down"], mode="nn", name="down_proj", outs=[F32], extra=(x2,), epilogue=_add_res)
    dy, dyb, loss = _loss_head(x3, target)

    grads["w_down"] = _matmul(act, dyb, mode="tn", name="dw_down", outs=[BF16])
    du = _matmul(dyb, w["w_down"], mode="nt", name="d_act", outs=[BF16], extra=(u,),
                 epilogue=lambda acc, uu: (acc * (2.0 * jnp.maximum(uu, 0.0)),))
    grads["w_up"] = _matmul(hm, du, mode="tn", name="dw_up", outs=[BF16], col_shards=4)
    dhm = _matmul(du, w["w_up"], mode="nt", name="d_hm", outs=[F32])
    dx2, grads["g_mlp"] = _rowwise_vjp(
        _norm_fn, [_full(x2)], [_full(p["g_mlp"])], [[_full(dhm)]],
        [(0, D_MODEL, F32, D_MODEL, 0, False, _full(dy))], [(0, D_MODEL, D_MODEL, 0, False)], name="norm_mlp_bwd")
    grads["w_co"] = _matmul(oc, dx2, mode="tn", name="dw_co", outs=[BF16], col_shards=4)
    doc = _matmul(dx2, w["w_co"], mode="nt", name="d_oc", outs=[BF16])
    dqc, dkc, dvc, grads["g_cq"], grads["g_ck"] = _cross_bwd(qc, kv, p["g_cq"], p["g_ck"], doc)
    grads["w_cq"] = _matmul(hc, dqc, mode="tn", name="dw_cq", outs=[BF16])
    dhc = _matmul(dqc, w["w_cq"], mode="nt", name="d_hc", outs=[F32])
    dkv = jnp.concatenate([dkc, dvc], axis=1)
    grads["w_ckv"] = _matmul(memh, dkv, mode="tn", name="dw_ckv", outs=[BF16])
    dmemh = _matmul(dkv, w["w_ckv"], mode="nt", name="d_memh", outs=[F32])
    grads["g_mem"] = _rowwise_vjp(_norm_fn, [_full(mem)], [_full(p["g_mem"])], [[_full(dmemh)]], [],
                                  [(0, D_MODEL, D_MODEL, 0, False)], name="norm_mem_bwd", n_rows=N_MEM)[0]
    dx1, grads["g_cross"] = _rowwise_vjp(
        _norm_fn, [_full(x1)], [_full(p["g_cross"])], [[_full(dhc)]],
        [(0, D_MODEL, F32, D_MODEL, 0, False, _full(dx2))], [(0, D_MODEL, D_MODEL, 0, False)], name="norm_cross_bwd")
    grads["w_out"] = _matmul(mix, dx1, mode="tn", name="dw_out", outs=[BF16])
    dmix = _matmul(dx1, w["w_out"], mode="nt", name="d_mix", outs=[F32])
    merge_grads = [(i, D_ATTN, F32, D_ATTN, 0, False, None) for i in range(6)]
    *dol, grads["g_attn_out"] = _rowwise_vjp(
        _merge_fn, merge_rows, [_full(p["g_attn_out"])], [[(dmix, D_ATTN, 0, False)]],
        merge_grads, [(0, D_ATTN, D_ATTN, 0, False)], name="attn_merge_bwd")
    do_p = jnp.stack([_permute_one(t, b) for b, t in enumerate(dol[:3])])
    dl_p = jnp.stack([_permute_one(t, b) for b, t in enumerate(dol[3:])])
    dq_p, dk_p, dv_p = _attention_bwd(qp, kp, vp, do_p, dl_p)
    qk_cts = [[(t, 128, 0, True) for t in _permute(d, inverse=True)] for d in (dq_p, dk_p, dv_p)]
    dq, dk, dv, dgq2, dgk2 = _rowwise_vjp(
        _qk_fn, qk_rows, qk_vecs, qk_cts, [(i, D_ATTN, BF16, 128, 0, True, None) for i in range(3)],
        [(0, 128, 128, 0, False), (1, 128, 128, 0, False)], name="qk_prep_bwd", groups=8)
    grads["g_q"] = dgq2[:, :HEAD] + dgq2[:, HEAD:]
    grads["g_k"] = dgk2[:, :HEAD] + dgk2[:, HEAD:]
    dy_ssd, dz, grads["g_ssm_out"] = _rowwise_vjp(
        _gate_fn, gate_rows, gate_vecs, [[(dmix, 256, 4, True)]],
        [(0, D_SSM, F32, 256, 0, True, None), (1, D_SSM, BF16, 256, 0, True, None)],
        [(0, D_SSM, 256, 0, True)], name="ssm_gate_bwd", groups=4)
    dxs, db, dc, ddt, dbias, dalog, ddsk = _ssd_bwd(xbc, dt_raw, bias, alog, dsk, h_in, dy_ssd)
    grads["dt_bias"], grads["a_log"], grads["d_skip"] = _unpad_heads(dbias), _unpad_heads(dalog), _unpad_heads(ddsk)
    dxbc_raw, dconv_w, grads["conv_b"] = _conv_bwd(proj, p["conv_w"], p["conv_b"], jnp.concatenate([dxs, db, dc], axis=1))
    grads["conv_w"] = dconv_w[:4]
    dproj = jnp.concatenate([dq, dk, dv, dz, dxbc_raw], axis=1)
    grads["w_main"] = _matmul(h, dproj, mode="tn", name="dw_main", outs=[BF16])
    grads["w_dt"] = _matmul(h, ddt, mode="tn", name="dw_dt", outs=[BF16])
    dh = _matmul(dproj, w["w_main"], mode="nt", name="d_h_main", outs=[F32])
    dh = _matmul(ddt, w["w_dt"], mode="nt", name="d_h_dt", outs=[F32], extra=(dh,), epilogue=_add_res)
    grad_x, grads["g_mix"] = _rowwise_vjp(
        _norm_fn, [_full(x)], [_full(p["g_mix"])], [[_full(dh)]],
        [(0, D_MODEL, F32, D_MODEL, 0, False, _full(dx1))], [(0, D_MODEL, D_MODEL, 0, False)], name="norm_in_bwd")
    return loss, grad_x, grads


MATRICES = ("w_in", "w_out", "w_cq", "w_ckv", "w_co", "w_up", "w_down")
ROW_SHARDED = ("w_out", "w_cq", "w_ckv", "w_down")
N_CHIPS = 4
ANY = pl.BlockSpec(memory_space=pl.ANY)


def _place():
    return lax.axis_index("x"), lax.axis_index("y"), lax.axis_index("c")


def _other_chips(x, y):
    return [(1 - x, y), (x, 1 - y), (1 - x, 1 - y)]


def _remote(src, dst, send_sem, recv_sem, device):
    return pltpu.make_async_remote_copy(src_ref=src, dst_ref=dst, send_sem=send_sem, recv_sem=recv_sem,
                                        device_id=device, device_id_type=MESH)


def _gathered_shape(name, shard):
    rows, cols = shard.shape
    if name == "w_in":
        return (N_CHIPS, rows, cols)
    return (N_CHIPS * rows, cols) if name in ROW_SHARDED else (rows, N_CHIPS * cols)


def _shard_window(name, ref, rows, cols, chip, half):
    r0, nr = (0, rows) if half is None else (half * (rows // 2), rows // 2)
    if name == "w_in":
        return ref.at[chip, pl.ds(r0, nr), :]
    if name in ROW_SHARDED:
        return ref.at[pl.ds(chip * rows + r0, nr), :]
    return ref.at[pl.ds(r0, nr), pl.ds(pl.multiple_of(chip * cols, 128), cols)]


def _gather_weights(shards):
    names = list(shards)
    n = len(names)

    def body(*refs):
        ins, outs = refs[:n], refs[n:2 * n]
        send1, recv1, send2, recv2, local_sem = refs[2 * n:]
        x, y, c = _place()
        mine, sibling, chips = 2 * x + y, (x, y, 1 - c), _other_chips(x, y)
        local, first, passed = [], [], []
        for w, name in enumerate(names):
            rows, cols = ins[w].shape
            cp = pltpu.make_async_copy(ins[w], _shard_window(name, outs[w], rows, cols, mine, None), local_sem.at[w])
            cp.start()
            local.append(cp)
            for k, (px, py) in enumerate(chips):
                cp = _remote(ins[w].at[pl.ds(c * (rows // 2), rows // 2), :], _shard_window(name, outs[w], rows, cols, mine, c),
                             send1.at[w, k], recv1.at[w, k], (px, py, c))
                cp.start()
                first.append(cp)
        for w, name in enumerate(names):
            rows, cols = ins[w].shape
            for k, (px, py) in enumerate(chips):
                win = _shard_window(name, outs[w], rows, cols, 2 * px + py, c)
                _remote(win, win, send1.at[w, k], recv1.at[w, k], (px, py, c)).wait_recv()
                cp = _remote(win, win, send2.at[w, k], recv2.at[w, k], sibling)
                cp.start()
                passed.append(cp)
        for w, name in enumerate(names):
            rows, cols = ins[w].shape
            for k, (px, py) in enumerate(chips):
                win = _shard_window(name, outs[w], rows, cols, 2 * px + py, 1 - c)
                _remote(win, win, send2.at[w, k], recv2.at[w, k], sibling).wait_recv()
        for cp in first + passed:
            cp.wait_send()
        for cp in local:
            cp.wait()

    res = pl.pallas_call(
        body, name="gather_weights",
        in_specs=[ANY] * n, out_specs=[ANY] * n,
        out_shape=[jax.ShapeDtypeStruct(_gathered_shape(name, shards[name]), BF16) for name in names],
        scratch_shapes=[pltpu.SemaphoreType.DMA((n, 3))] * 4 + [pltpu.SemaphoreType.DMA((n,))],
    )(*[shards[name] for name in names])
    return dict(zip(names, res))


def _sibling_swap(arrs):
    n = len(arrs)

    def body(*refs):
        ins, outs, send, recv = refs[:n], refs[n:2 * n], refs[2 * n], refs[2 * n + 1]
        x, y, c = _place()
        cps = [_remote(ins[w].at[:, 1 - c], outs[w], send.at[w], recv.at[w], (x, y, 1 - c)) for w in range(n)]
        for cp in cps:
            cp.start()
        for cp in cps:
            cp.wait()

    return pl.pallas_call(
        body, name="grad_sibling_swap", in_specs=[ANY] * n, out_specs=[ANY] * n,
        out_shape=[jax.ShapeDtypeStruct((a.shape[0],) + a.shape[2:], a.dtype) for a in arrs],
        scratch_shapes=[pltpu.SemaphoreType.DMA((n,))] * 2,
    )(*arrs)


def _chip_scatter(arrs):
    n = len(arrs)

    def body(*refs):
        ins, outs = refs[:n], refs[n:2 * n]
        send, recv, local_sem = refs[2 * n:]
        x, y, c = _place()
        mine, chips = 2 * x + y, _other_chips(x, y)
        cps, local = [], []
        for w in range(n):
            cp = pltpu.make_async_copy(ins[w].at[mine], outs[w].at[mine], local_sem.at[w])
            cp.start()
            local.append(cp)
            for k, (px, py) in enumerate(chips):
                cp = _remote(ins[w].at[2 * px + py], outs[w].at[mine], send.at[w, k], recv.at[w, k], (px, py, c))
                cp.start()
                cps.append(cp)
        for w in range(n):
            for k, (px, py) in enumerate(chips):
                win = outs[w].at[2 * px + py]
                _remote(win, win, send.at[w, k], recv.at[w, k], (px, py, c)).wait_recv()
        for cp in cps:
            cp.wait_send()
        for cp in local:
            cp.wait()

    return pl.pallas_call(
        body, name="grad_chip_scatter", in_specs=[ANY] * n, out_specs=[ANY] * n,
        out_shape=[jax.ShapeDtypeStruct(a.shape, a.dtype) for a in arrs],
        scratch_shapes=[pltpu.SemaphoreType.DMA((n, 3))] * 2 + [pltpu.SemaphoreType.DMA((n,))],
    )(*arrs)


def _sibling_share(arrs):
    n = len(arrs)

    def body(*refs):
        ins, outs = refs[:n], refs[n:2 * n]
        send, recv, local_sem = refs[2 * n:]
        x, y, c = _place()
        cps, local = [], []
        for w in range(n):
            cp = pltpu.make_async_copy(ins[w], outs[w].at[c], local_sem.at[w])
            cp.start()
            local.append(cp)
            cp = _remote(ins[w], outs[w].at[c], send.at[w], recv.at[w], (x, y, 1 - c))
            cp.start()
            cps.append(cp)
        for w in range(n):
            win = outs[w].at[1 - c]
            _remote(win, win, send.at[w], recv.at[w], (x, y, 1 - c)).wait_recv()
        for cp in cps:
            cp.wait_send()
        for cp in local:
            cp.wait()

    return pl.pallas_call(
        body, name="grad_sibling_share", in_specs=[ANY] * n, out_specs=[ANY] * n,
        out_shape=[jax.ShapeDtypeStruct((2,) + a.shape, a.dtype) for a in arrs],
        scratch_shapes=[pltpu.SemaphoreType.DMA((n,))] * 3,
    )(*arrs)


def _small_allreduce(buf, name):
    rows = buf.shape[0]

    def body(x_ref, out_ref, all_ref, send_sems, recv_sems, local_sem):
        x, y, c = _place()
        me, sibling, chips = (x, y, c), (x, y, 1 - c), _other_chips(x, y)

        def block(px, py, pc):
            return all_ref.at[pl.ds((4 * px + 2 * py + pc) * rows, rows), :]

        def copy(k, blk, to, src=None):
            return _remote(block(*blk) if src is None else src, block(*blk), send_sems.at[k], recv_sems.at[k], to)

        own = pltpu.make_async_copy(x_ref, block(*me), local_sem)
        own.start()
        first = [copy(0, me, sibling, src=x_ref)] + [copy(1 + j, me, (*chip, c), src=x_ref) for j, chip in enumerate(chips)]
        for cp in first:
            cp.start()
        passed = [copy(4 + j, (*chip, c), sibling) for j, chip in enumerate(chips)]
        for j, chip in enumerate(chips):
            copy(1 + j, (*chip, c), me).wait_recv()
            passed[j].start()
        copy(0, sibling, me).wait_recv()
        for j, chip in enumerate(chips):
            copy(4 + j, (*chip, 1 - c), me).wait_recv()
        for cp in first + passed:
            cp.wait_send()
        own.wait()
        acc = all_ref[pl.ds(0, rows), :]
        for d in range(1, 8):
            acc = acc + all_ref[pl.ds(d * rows, rows), :]
        out_ref[...] = acc

    vmem = pl.BlockSpec(memory_space=pltpu.VMEM)
    return pl.pallas_call(
        body, name=name, in_specs=[vmem], out_specs=vmem,
        out_shape=jax.ShapeDtypeStruct(buf.shape, F32),
        scratch_shapes=[pltpu.VMEM((8 * rows, 128), F32), pltpu.SemaphoreType.DMA((7,)), pltpu.SemaphoreType.DMA((7,)),
                        pltpu.SemaphoreType.DMA],
    )(buf)


ROW_TILE = 256


def _cast_bf16(w, name):
    rows, cols = w.shape
    tr = _tile(rows, ROW_TILE)

    def body(w_ref, o_ref):
        o_ref[...] = w_ref[...].astype(BF16)

    blk = pl.BlockSpec((tr, cols), lambda i: (i, 0))
    return pl.pallas_call(body, name=name, grid=(rows // tr,), in_specs=[blk], out_specs=blk,
                          out_shape=jax.ShapeDtypeStruct(w.shape, BF16), compiler_params=_params(("parallel",)))(w)


def _add_halves(arr, recv, c, name):
    _, _, hr, cols = arr.shape
    tr = _tile(hr, ROW_TILE)

    def body(c_ref, a_ref, r_ref, o_ref):
        o_ref[...] = (a_ref[...].astype(F32) + r_ref[...].astype(F32)).astype(o_ref.dtype)

    piece = pl.BlockSpec((None, tr, cols), lambda j, i, c_ref: (j, i, 0))
    grid_spec = pltpu.PrefetchScalarGridSpec(
        num_scalar_prefetch=1, grid=(N_CHIPS, hr // tr),
        in_specs=[pl.BlockSpec((None, None, tr, cols), lambda j, i, c_ref: (j, c_ref[0], i, 0)), piece], out_specs=piece)
    return pl.pallas_call(body, name=name, grid_spec=grid_spec, out_shape=jax.ShapeDtypeStruct(recv.shape, BF16),
                          compiler_params=_params(("parallel", "parallel")))(c.reshape(1).astype(jnp.int32), arr, recv)


def _sum_chips(q, name):
    _, hr, cols = q.shape
    tr = _tile(hr, ROW_TILE)

    def body(q_ref, o_ref):
        acc = q_ref[0].astype(F32)
        for j in range(1, N_CHIPS):
            acc = acc + q_ref[j].astype(F32)
        o_ref[...] = acc

    return pl.pallas_call(
        body, name=name, grid=(hr // tr,),
        in_specs=[pl.BlockSpec((N_CHIPS, tr, cols), lambda i: (0, i, 0))], out_specs=pl.BlockSpec((tr, cols), lambda i: (i, 0)),
        out_shape=jax.ShapeDtypeStruct((hr, cols), F32), compiler_params=_params(("parallel",)))(q)


def _adamw(w, g, m, v, name):
    rows, cols = w.shape
    tr = _tile(rows, ROW_TILE)

    def body(w_ref, g_ref, m_ref, v_ref, d_ref, nm_ref, nv_ref):
        g_ = g_ref[...]
        m_new = ADAM_B1 * m_ref[...] + (1.0 - ADAM_B1) * g_
        v_new = ADAM_B2 * v_ref[...] + (1.0 - ADAM_B2) * (g_ * g_)
        m_hat = m_new / (1.0 - ADAM_B1 ** ADAM_STEP)
        v_hat = v_new / (1.0 - ADAM_B2 ** ADAM_STEP)
        d_ref[...] = -ADAM_LR * (m_hat / (jnp.sqrt(v_hat) + ADAM_EPS) + ADAM_WD * w_ref[...])
        nm_ref[...] = m_new
        nv_ref[...] = v_new

    blk = pl.BlockSpec((tr, cols), lambda i: (i, 0))
    return pl.pallas_call(body, name=name, grid=(rows // tr,), in_specs=[blk] * 4, out_specs=[blk] * 3,
                          out_shape=[jax.ShapeDtypeStruct(w.shape, F32)] * 3, compiler_params=_params(("parallel",)))(w, g, m, v)


VECTORS = ("g_mix", "g_q", "g_k", "g_attn_out", "conv_b", "dt_bias", "a_log", "d_skip", "g_ssm_out", "g_cross", "g_mem",
           "g_cq", "g_ck", "g_mlp")
WEIGHTS = ("g_mix", "w_in", "g_q", "g_k", "g_attn_out", "conv_w", "conv_b", "dt_bias", "a_log", "d_skip", "g_ssm_out", "w_out",
           "g_cross", "g_mem", "w_cq", "w_ckv", "g_cq", "g_ck", "w_co", "g_mlp", "w_up", "w_down")


def _pack(parts):
    flat = jnp.concatenate([t.reshape(-1) for t in parts])
    total = -(-flat.shape[0] // 1024) * 1024
    return jnp.pad(flat, (0, total - flat.shape[0])).reshape(total // 128, 128)


def _unpack(buf, shapes):
    flat, out, pos = buf.reshape(-1), [], 0
    for shape in shapes:
        size = math.prod(shape)
        out.append(flat[pos:pos + size].reshape(shape))
        pos += size
    return out


def kernel(x, mem, positions, g_mix, w_in, g_q, g_k, g_attn_out, conv_w, conv_b, dt_bias, a_log, d_skip, g_ssm_out, w_out, g_cross, g_mem, w_cq, w_ckv, g_cq, g_ck, w_co, g_mlp, w_up, w_down, loss_target, m_g_mix, m_w_in, m_g_q, m_g_k, m_g_attn_out, m_conv_w, m_conv_b, m_dt_bias, m_a_log, m_d_skip, m_g_ssm_out, m_w_out, m_g_cross, m_g_mem, m_w_cq, m_w_ckv, m_g_cq, m_g_ck, m_w_co, m_g_mlp, m_w_up, m_w_down, v_g_mix, v_w_in, v_g_q, v_g_k, v_g_attn_out, v_conv_w, v_conv_b, v_dt_bias, v_a_log, v_d_skip, v_g_ssm_out, v_w_out, v_g_cross, v_g_mem, v_w_cq, v_w_ckv, v_g_cq, v_g_ck, v_w_co, v_g_mlp, v_w_up, v_w_down):
    args = dict(locals())
    weights = {n: args[n][0] for n in WEIGHTS}
    mom_m = {n: args["m_" + n][0] for n in WEIGHTS}
    mom_v = {n: args["v_" + n][0] for n in WEIGHTS}
    x_idx, y_idx, c_idx = _place()
    chip = 2 * x_idx + y_idx

    full = _gather_weights({n: _cast_bf16(weights[n], "cast_" + n) for n in MATRICES})
    w_in_full = jnp.transpose(full.pop("w_in"), (1, 0, 2)).reshape(D_MODEL, D_MAIN + N_DT)
    full["w_main"] = w_in_full[:, :D_MAIN]
    full["w_dt"] = jnp.pad(w_in_full[:, D_MAIN:].reshape(D_MODEL, N_GROUPS, HEADS_PER_GROUP),
                           ((0, 0), (0, 0), (0, 128 - HEADS_PER_GROUP))).reshape(D_MODEL, DT_PAD)
    conv_parts = _small_allreduce(_pack([jnp.zeros((N_CHIPS, 4, 512), F32).at[chip].set(0.5 * weights["conv_w"])]),
                                  "gather_conv_taps")
    conv_full = _unpack(conv_parts, [(N_CHIPS, 4, 512)])[0].transpose(1, 0, 2).reshape(4, 4 * 512)
    params = {n: weights[n].reshape(1, -1) for n in VECTORS}
    params["conv_w"] = conv_full

    loss, grad_x, grads = _local_step(x[0], mem[0], positions[0], loss_target[0], params, full)

    gw_in = jnp.concatenate([grads.pop("w_main"), _unpad_heads(grads.pop("w_dt"))], axis=1)
    grads["w_in"] = gw_in.reshape(D_MODEL, N_CHIPS, gw_in.shape[1] // N_CHIPS).transpose(1, 0, 2)
    pieces = []
    for n in MATRICES:
        rows, cols = weights[n].shape
        pieces.append(grads[n].reshape(N_CHIPS, 2, rows // 2, cols))
    from_sibling = _sibling_swap(pieces)
    chip_sums = [_add_halves(a, r, c_idx, "add_halves_" + n) for n, a, r in zip(MATRICES, pieces, from_sibling)]
    from_chips = _chip_scatter(chip_sums)
    halves = [_sum_chips(q, "sum_chips_" + n) for n, q in zip(MATRICES, from_chips)]
    shared = _sibling_share(halves)
    out_g, out_d, out_m, out_v = {}, {}, {}, {}
    for n, g in zip(MATRICES, shared):
        out_g[n] = g.reshape(weights[n].shape)
        out_d[n], out_m[n], out_v[n] = _adamw(weights[n], out_g[n], mom_m[n], mom_v[n], "adamw_" + n)

    small = [grads[n] for n in VECTORS] + [grads["conv_w"]]
    summed = _unpack(_small_allreduce(_pack(small), "allreduce_vectors"), [t.shape for t in small])
    g_small = dict(zip(VECTORS, summed[:-1]))
    g_small["conv_w"] = lax.dynamic_slice_in_dim(summed[-1], chip * 512, 512, axis=1)
    names = VECTORS + ("conv_w",)
    shapes = [weights[n].shape for n in names]
    packed = [_pack([src[n] for n in names]) for src in (weights, g_small, mom_m, mom_v)]
    small_out = [_unpack(t, shapes) for t in _adamw(*packed, "adamw_small")]
    for i, n in enumerate(names):
        out_g[n] = g_small[n].reshape(shapes[i])
        out_d[n], out_m[n], out_v[n] = small_out[0][i], small_out[1][i], small_out[2][i]

    total_loss = lax.psum(loss[0, 0], ("x", "y", "c"))
    outs = [total_loss, grad_x[None]]
    for group in (out_g, out_d, out_m, out_v):
        outs += [group[n][None] for n in WEIGHTS]
    return tuple(outs)
```

```python
import functools
import math

import jax
import jax.numpy as jnp
from jax import lax
from jax.experimental import pallas as pl
from jax.experimental.pallas import tpu as pltpu

F32 = jnp.float32
BF16 = jnp.bfloat16

SEQ = 2048
D_MODEL = 2048
HEAD = 64
D_ATTN = 1024
D_SSM = 1024
N_GROUPS = 4
N_STATE = 128
CHUNK = 128
ATT_BLK = 128
N_MEM = 256
D_CROSS = 512
D_FF = 8192
D_MAIN = 6144
N_DT = 16
DT_PAD = 512
ROT = 16
ROPE_THETA = 500000.0
EPS = 1e-6
NEG = -1e30
BRANCH_BLOCKS = (16, 4, 1)
DILATIONS = (1, 4, 16)

ADAM_LR, ADAM_B1, ADAM_B2, ADAM_EPS, ADAM_WD, ADAM_STEP = 0.001, 0.9, 0.999, 1e-08, 0.01, 10

VMEM_LIMIT = 56 * 1024 * 1024
MESH = pl.DeviceIdType.MESH


def _params(sem, **kw):
    return pltpu.CompilerParams(dimension_semantics=sem, vmem_limit_bytes=VMEM_LIMIT, **kw)


def _bdot(a, b, dims):
    return lax.dot_general(a.astype(BF16), b.astype(BF16), (dims, ((), ())), preferred_element_type=F32)


def _fdot(a, b, dims):
    return lax.dot_general(a, b, (dims, ((), ())), preferred_element_type=F32, precision=lax.Precision.HIGHEST)


NN = ((1,), (0,))
NT = ((1,), (1,))
TN = ((0,), (0,))


def _tile(n, want):
    t = min(n, want)
    while n % t:
        t //= 2
    return t


def _matmul(a, b, *, mode, name, outs, extra=(), epilogue=None, col_shards=1, tm=512, tn=1024, tk=1024):
    if mode == "nn":
        (m, k), n = a.shape, b.shape[1]
    elif mode == "nt":
        (m, k), n = a.shape, b.shape[0]
    else:
        (k, m), n = a.shape, b.shape[1]
    tm, tn, tk = _tile(m, tm), _tile(n // col_shards, tn), _tile(k, tk)
    nk = k // tk
    per_shard = n // col_shards // tn
    dims = {"nn": NN, "nt": NT, "tn": TN}[mode]
    a_spec = pl.BlockSpec((tk, tm), lambda i, j, kk: (kk, i)) if mode == "tn" else pl.BlockSpec((tm, tk), lambda i, j, kk: (i, kk))
    b_spec = pl.BlockSpec((tn, tk), lambda i, j, kk: (j, kk)) if mode == "nt" else pl.BlockSpec((tk, tn), lambda i, j, kk: (kk, j))
    o_spec = pl.BlockSpec((tm, tn), lambda i, j, kk: (i, j))
    n_extra, n_out = len(extra), len(outs)

    def body(a_ref, b_ref, *rest):
        extra_refs, out_refs, acc_ref = rest[:n_extra], rest[n_extra:n_extra + n_out], rest[-1]
        kk = pl.program_id(2)

        @pl.when(kk == 0)
        def _():
            acc_ref[...] = jnp.zeros_like(acc_ref)

        acc_ref[...] += _bdot(a_ref[...], b_ref[...], dims)

        @pl.when(kk == nk - 1)
        def _():
            acc = acc_ref[...]
            res = (acc,) if epilogue is None else epilogue(acc, *[e[...] for e in extra_refs])
            for o_ref, r in zip(out_refs, res):
                o_ref[...] = r.astype(o_ref.dtype)

    if col_shards == 1:
        out_specs, out_dims = [o_spec] * n_out, (m, n)
    else:
        sharded = pl.BlockSpec((None, tm, tn), lambda i, j, kk: (j // per_shard, i, j % per_shard))
        out_specs, out_dims = [sharded] * n_out, (col_shards, m, n // col_shards)
    res = pl.pallas_call(
        body, name=name, grid=(m // tm, n // tn, nk),
        in_specs=[a_spec, b_spec] + [o_spec] * n_extra,
        out_specs=out_specs,
        out_shape=[jax.ShapeDtypeStruct(out_dims, dt) for dt in outs],
        scratch_shapes=[pltpu.VMEM((tm, tn), F32)],
        compiler_params=_params(("parallel", "parallel", "arbitrary")),
    )(a, b, *extra)
    return res[0] if n_out == 1 else res


def _row_spec(tr, bw, cb, per_group):
    return pl.BlockSpec((tr, bw), (lambda g, i: (i, cb + g)) if per_group else (lambda g, i: (i, cb)))


def _vec_spec(bw, cb, per_group):
    return pl.BlockSpec((1, bw), (lambda g, i: (0, cb + g)) if per_group else (lambda g, i: (0, cb)))


def _rowwise(fn, rows, vecs, outs, *, name, n_rows=SEQ, tr=256, groups=1):
    n_r, n_v = len(rows), len(vecs)

    def body(*refs):
        vals = [r[...].astype(F32) for r in refs[:n_r + n_v]]
        res = fn(*vals)
        for o_ref, r in zip(refs[n_r + n_v:], res):
            o_ref[...] = r.astype(o_ref.dtype)

    res = pl.pallas_call(
        body, name=name, grid=(groups, n_rows // tr),
        in_specs=[_row_spec(tr, bw, cb, pg) for _, bw, cb, pg in rows] + [_vec_spec(bw, cb, pg) for _, bw, cb, pg in vecs],
        out_specs=[_row_spec(tr, bw, cb, pg) for _, _, bw, cb, pg in outs],
        out_shape=[jax.ShapeDtypeStruct((n_rows, w), dt) for w, dt, _, _, _ in outs],
        compiler_params=_params(("parallel", "parallel")),
    )(*[r[0] for r in rows], *[v[0] for v in vecs])
    return res


def _rowwise_vjp(fn, rows, vecs, cts, row_grads, vec_grads, *, name, n_rows=SEQ, tr=256, groups=1):
    n_r, n_v = len(rows), len(vecs)
    ct_ops = [op for group in cts for op in group]
    ct_sizes = [len(group) for group in cts]
    res_ops = [g[6] for g in row_grads if g[6] is not None]
    n_ct, n_res, n_rg = len(ct_ops), len(res_ops), len(row_grads)

    def body(*refs):
        vals = [r[...].astype(F32) for r in refs[:n_r + n_v]]
        pos = n_r + n_v
        ct_vals = []
        for size in ct_sizes:
            acc = refs[pos][...].astype(F32)
            for t in range(1, size):
                acc = acc + refs[pos + t][...].astype(F32)
            ct_vals.append(acc)
            pos += size
        res_refs = refs[pos:pos + n_res]
        out_refs = refs[pos + n_res:]
        _, pullback = jax.vjp(fn, *vals)
        grads = pullback(tuple(ct_vals))
        r_i = 0
        for o_ref, g in zip(out_refs[:n_rg], row_grads):
            val = grads[g[0]]
            if g[6] is not None:
                val = val + res_refs[r_i][...].astype(F32)
                r_i += 1
            o_ref[...] = val.astype(o_ref.dtype)
        first = (pl.program_id(1) == 0)
        for o_ref, g in zip(out_refs[n_rg:], vec_grads):
            val = jnp.sum(grads[n_r + g[0]], axis=0, keepdims=True)
            init = first if g[4] else jnp.logical_and(first, pl.program_id(0) == 0)

            @pl.when(init)
            def _(o_ref=o_ref, val=val):
                o_ref[...] = val

            @pl.when(jnp.logical_not(init))
            def _(o_ref=o_ref, val=val):
                o_ref[...] += val

    in_specs = [_row_spec(tr, bw, cb, pg) for _, bw, cb, pg in rows] + [_vec_spec(bw, cb, pg) for _, bw, cb, pg in vecs]
    in_specs += [_row_spec(tr, bw, cb, pg) for _, bw, cb, pg in ct_ops + res_ops]
    out_specs = [_row_spec(tr, g[3], g[4], g[5]) for g in row_grads] + [_vec_spec(g[2], g[3], g[4]) for g in vec_grads]
    out_shape = [jax.ShapeDtypeStruct((n_rows, g[1]), g[2]) for g in row_grads]
    out_shape += [jax.ShapeDtypeStruct((1, g[1]), F32) for g in vec_grads]
    return pl.pallas_call(
        body, name=name, grid=(groups, n_rows // tr),
        in_specs=in_specs, out_specs=out_specs, out_shape=out_shape,
        compiler_params=_params(("arbitrary", "arbitrary")),
    )(*[r[0] for r in rows], *[v[0] for v in vecs], *[c[0] for c in ct_ops], *[r[0] for r in res_ops])


def _full(arr, width=None):
    return (arr, arr.shape[1] if width is None else width, 0, False)


def _make_xor(sh):
    def raw(x):
        n = x.shape[-1]
        lane = lax.broadcasted_iota(jnp.int32, x.shape, x.ndim - 1)
        up = pltpu.roll(x, n - sh, x.ndim - 1)
        down = pltpu.roll(x, sh, x.ndim - 1)
        return jnp.where((lane & sh) == 0, up, down)

    f = jax.custom_vjp(raw)
    f.defvjp(lambda x: (raw(x), None), lambda _, ct: (raw(ct),))
    return f


_XOR = {sh: _make_xor(sh) for sh in (1, 2, 4, 8, 16, 32)}


def _head_sum(x):
    for sh in (1, 2, 4, 8, 16, 32):
        x = x + _XOR[sh](x)
    return x


def _rms(x, g):
    return x * lax.rsqrt(jnp.mean(x * x, axis=-1, keepdims=True) + EPS) * g


def _head_rms_rope(x, g, cos, sin, scale):
    y = x * lax.rsqrt(_head_sum(x * x) * (1.0 / HEAD) + EPS) * g
    return (y * cos + _XOR[8](y) * sin) * scale


def _qk_fn(q, k, v, cos, sin, gq, gk):
    return (_head_rms_rope(q, gq, cos, sin, HEAD ** -0.5), _head_rms_rope(k, gk, cos, sin, 1.0), v)


def _norm_fn(x, g):
    return (_rms(x, g),)


def _merge_fn(o0, o1, o2, l0, l1, l2, g):
    m = jnp.maximum(jnp.maximum(l0, l1), l2)
    e0, e1, e2 = jnp.exp(l0 - m), jnp.exp(l1 - m), jnp.exp(l2 - m)
    mix = (e0 * o0 + e1 * o1 + e2 * o2) / (e0 + e1 + e2)
    return (_rms(mix, g),)


def _gate_fn(y, z, g):
    return (_rms(y * (z * jax.nn.sigmoid(z)), g),)


def _attn_pair(q, kc, vc, kp=None, vp=None, has_prev=None):
    qi = lax.broadcasted_iota(jnp.int32, (ATT_BLK, ATT_BLK), 0)
    kj = lax.broadcasted_iota(jnp.int32, (ATT_BLK, ATT_BLK), 1)
    lane = lax.broadcasted_iota(jnp.int32, (1, 2 * HEAD), 1)
    o, lse = 0.0, 0.0
    for h in range(2):
        pick = ((lane >= h * HEAD) & (lane < (h + 1) * HEAD)).astype(F32)
        qh = q * pick
        s_c = jnp.where(qi >= kj, _bdot(qh, kc, NT), NEG)
        m = jnp.max(s_c, axis=-1, keepdims=True)
        if kp is not None:
            s_p = jnp.where(jnp.logical_and(kj >= qi, has_prev), _bdot(qh, kp, NT), NEG)
            m = jnp.maximum(m, jnp.max(s_p, axis=-1, keepdims=True))
        p_c = jnp.exp(s_c - m)
        den = jnp.sum(p_c, axis=-1, keepdims=True)
        acc = _bdot(p_c, vc, NN)
        if kp is not None:
            p_p = jnp.exp(s_p - m)
            den = den + jnp.sum(p_p, axis=-1, keepdims=True)
            acc = acc + _bdot(p_p, vp, NN)
        o = o + pick * (acc / den)
        lse = lse + pick * (m + jnp.log(den))
    return o, lse


def _attn_config(b):
    r = DILATIONS[b]
    return r, ATT_BLK * r, (512 if r == 1 else 128), BRANCH_BLOCKS[b] > 1


def _for_residues(r, fn):
    if r <= 4:
        for rho in range(r):
            fn(rho)
    else:
        def step(t, carry):
            for u in range(4):
                fn(4 * t + u)
            return carry

        lax.fori_loop(0, r // 4, step, 0)


def _strided_rows(start, r):
    if r > 1:
        return pl.ds(start, ATT_BLK, stride=r)
    return pl.ds(start if isinstance(start, int) else pl.multiple_of(start, ATT_BLK), ATT_BLK)


def _attention_fwd(qn, kn, vn, b):
    r, rows, lanes, with_prev = _attn_config(b)
    cur = pl.BlockSpec((rows, lanes), lambda g, n: (n, g))
    prev = pl.BlockSpec((rows, lanes), lambda g, n: (jnp.maximum(n - 1, 0), g))

    def body(*refs):
        ins, (o_ref, l_ref) = refs[:-2], refs[-2:]
        has_prev = pl.program_id(1) > 0

        def one(rho):
            sub = _strided_rows(rho, r)
            for pair in range(lanes // 128):
                sl = pl.ds(pair * 128, 128)
                args = [ref[sub, sl] for ref in ins] + ([has_prev] if with_prev else [])
                o_ref[sub, sl], l_ref[sub, sl] = _attn_pair(*args)

        _for_residues(r, one)

    operands = (qn, kn, vn, kn, vn) if with_prev else (qn, kn, vn)
    return pl.pallas_call(
        body, name="attn_fwd_%d" % r, grid=(D_ATTN // lanes, SEQ // rows),
        in_specs=[cur, cur, cur] + ([prev, prev] if with_prev else []), out_specs=[cur, cur],
        out_shape=[jax.ShapeDtypeStruct((SEQ, D_ATTN), F32)] * 2,
        compiler_params=_params(("parallel", "parallel")),
    )(*operands)


def _attention_bwd(qn, kn, vn, do, dl, b):
    r, rows, lanes, with_prev = _attn_config(b)
    cur = pl.BlockSpec((rows, lanes), lambda g, n: (n, g))
    prev = pl.BlockSpec((rows, lanes), lambda g, n: (jnp.maximum(n - 1, 0), g))
    whole = pl.BlockSpec((SEQ, lanes), lambda g, n: (0, g))
    n_in = 5 if with_prev else 3

    def body(*refs):
        ins, (do_ref, dl_ref, dq_ref, dk_ref, dv_ref) = refs[:n_in], refs[n_in:]
        n = pl.program_id(1)
        has_prev = n > 0

        @pl.when(n == 0)
        def _():
            dk_ref[...] = jnp.zeros_like(dk_ref)
            dv_ref[...] = jnp.zeros_like(dv_ref)

        def one(rho):
            sub = _strided_rows(rho, r)
            sub_c = _strided_rows(n * rows + rho, r)
            sub_p = _strided_rows(jnp.maximum(n - 1, 0) * rows + rho, r)
            for pair in range(lanes // 128):
                sl = pl.ds(pair * 128, 128)
                vals = [ref[sub, sl] for ref in ins]
                if with_prev:
                    _, pullback = jax.vjp(lambda *a: _attn_pair(*a, has_prev), *vals)
                else:
                    _, pullback = jax.vjp(_attn_pair, *vals)
                grads = pullback((do_ref[sub, sl], dl_ref[sub, sl]))
                dq_ref[sub, sl] = grads[0]
                dk_ref[sub_c, sl] += grads[1]
                dv_ref[sub_c, sl] += grads[2]
                if with_prev:
                    dk_ref[sub_p, sl] += grads[3]
                    dv_ref[sub_p, sl] += grads[4]

        _for_residues(r, one)

    operands = (qn, kn, vn, kn, vn) if with_prev else (qn, kn, vn)
    return pl.pallas_call(
        body, name="attn_bwd_%d" % r, grid=(D_ATTN // lanes, SEQ // rows),
        in_specs=[cur, cur, cur] + ([prev, prev] if with_prev else []) + [cur, cur], out_specs=[cur, whole, whole],
        out_shape=[jax.ShapeDtypeStruct((SEQ, D_ATTN), F32)] * 3,
        compiler_params=_params(("parallel", "arbitrary")),
    )(*operands, do, dl)


CONV_COLS = 256
XBC_BLOCK0 = 4096 // CONV_COLS


def _shift_rows(x, s):
    n = x.shape[0]
    t = lax.broadcasted_iota(jnp.int32, x.shape, 0)
    if s >= 0:
        return jnp.where(t >= s, pltpu.roll(x, s, 0), 0.0)
    return jnp.where(t < n + s, pltpu.roll(x, n + s, 0), 0.0)


def _conv_pre(x, w_ref, b_ref):
    pre = b_ref[...] + w_ref[3:4, :] * x
    for k in range(3):
        pre = pre + w_ref[k:k + 1, :] * _shift_rows(x, 3 - k)
    return pre


def _conv_fwd(proj, conv_w, conv_b):
    cols = conv_w.shape[1]

    def body(x_ref, w_ref, b_ref, o_ref):
        pre = _conv_pre(x_ref[...], w_ref, b_ref)
        o_ref[...] = pre * jax.nn.sigmoid(pre)

    blk = pl.BlockSpec((SEQ, CONV_COLS), lambda j: (0, j))
    return pl.pallas_call(
        body, name="conv_fwd", grid=(cols // CONV_COLS,),
        in_specs=[pl.BlockSpec((SEQ, CONV_COLS), lambda j: (0, XBC_BLOCK0 + j)),
                  pl.BlockSpec((4, CONV_COLS), lambda j: (0, j)), pl.BlockSpec((1, CONV_COLS), lambda j: (0, j))],
        out_specs=blk, out_shape=jax.ShapeDtypeStruct((SEQ, cols), F32),
        compiler_params=_params(("parallel",)),
    )(proj, conv_w, conv_b)


def _conv_bwd(proj, conv_w, conv_b, dy):
    cols = conv_w.shape[1]

    def body(x_ref, w_ref, b_ref, dy_ref, dx_ref, dw_ref, db_ref):
        x = x_ref[...]
        pre = _conv_pre(x, w_ref, b_ref)
        sg = jax.nn.sigmoid(pre)
        dpre = dy_ref[...] * (sg * (1.0 + pre * (1.0 - sg)))
        db_ref[...] = jnp.sum(dpre, axis=0, keepdims=True)
        dx = w_ref[3:4, :] * dpre
        dw_ref[3:4, :] = jnp.sum(dpre * x, axis=0, keepdims=True)
        for k in range(3):
            dx = dx + w_ref[k:k + 1, :] * _shift_rows(dpre, k - 3)
            dw_ref[k:k + 1, :] = jnp.sum(dpre * _shift_rows(x, 3 - k), axis=0, keepdims=True)
        dw_ref[4:8, :] = jnp.zeros((4, CONV_COLS), F32)
        dx_ref[...] = dx.astype(dx_ref.dtype)

    blk = pl.BlockSpec((SEQ, CONV_COLS), lambda j: (0, j))
    return pl.pallas_call(
        body, name="conv_bwd", grid=(cols // CONV_COLS,),
        in_specs=[pl.BlockSpec((SEQ, CONV_COLS), lambda j: (0, XBC_BLOCK0 + j)),
                  pl.BlockSpec((4, CONV_COLS), lambda j: (0, j)), pl.BlockSpec((1, CONV_COLS), lambda j: (0, j)), blk],
        out_specs=[blk, pl.BlockSpec((8, CONV_COLS), lambda j: (0, j)), pl.BlockSpec((1, CONV_COLS), lambda j: (0, j))],
        out_shape=[jax.ShapeDtypeStruct((SEQ, cols), BF16), jax.ShapeDtypeStruct((8, cols), F32),
                   jax.ShapeDtypeStruct((1, cols), F32)],
        compiler_params=_params(("parallel",)),
    )(proj, conv_w, conv_b, dy)


HEADS_PER_GROUP = 4


def _ssd_chunk(x0, x1, x2, x3, bm, cm, dtr, bias, alog, dsk, h0, h1, h2, h3):
    xs, hs = (x0, x1, x2, x3), (h0, h1, h2, h3)
    row = lax.broadcasted_iota(jnp.int32, (CHUNK, CHUNK), 0)
    col = lax.broadcasted_iota(jnp.int32, (CHUNK, CHUNK), 1)
    causal = row >= col
    tril = causal.astype(F32)
    z = dtr + bias
    dt = jnp.maximum(z, 0.0) + jnp.log(1.0 + jnp.exp(-jnp.abs(z)))
    a = -jnp.exp(alog)
    acs = _fdot(tril, dt * a, NN)
    acs_t, dt_t = acs.T, dt.T
    cb = _bdot(cm, bm, NT)
    lane = lax.broadcasted_iota(jnp.int32, (1, CHUNK), 1)
    sub = lax.broadcasted_iota(jnp.int32, (CHUNK, 1), 0)
    ys, hn = [], []
    for j in range(HEADS_PER_GROUP):
        on_lane, on_sub = (lane == j).astype(F32), (sub == j).astype(F32)
        acs_c = jnp.sum(acs * on_lane, axis=1, keepdims=True)
        dt_c = jnp.sum(dt * on_lane, axis=1, keepdims=True)
        acs_r = jnp.sum(acs_t * on_sub, axis=0, keepdims=True)
        dt_r = jnp.sum(dt_t * on_sub, axis=0, keepdims=True)
        acs_last = jnp.sum(acs_c * (sub == CHUNK - 1).astype(F32), axis=0, keepdims=True)
        d_j = jnp.sum(dsk * on_lane, axis=1, keepdims=True)
        decay = jnp.exp(jnp.where(causal, acs_c - acs_r, NEG))
        w = cb * decay * dt_r
        y_diag = _bdot(w, xs[j], NN)
        y_off = _bdot(cm, hs[j], NT) * jnp.exp(acs_c)
        ys.append(y_diag + y_off + d_j * xs[j])
        state = _bdot(xs[j] * (jnp.exp(acs_last - acs_c) * dt_c), bm, TN)
        hn.append(hs[j] * jnp.exp(acs_last) + state)
    return (*ys, *hn)


def _ssd_specs(reverse):
    n_chunks = SEQ // CHUNK
    c_of = (lambda c: n_chunks - 1 - c) if reverse else (lambda c: c)
    x_spec = pl.BlockSpec((CHUNK, 256), lambda g, c: (c_of(c), g))
    b_spec = pl.BlockSpec((CHUNK, N_STATE), lambda g, c: (c_of(c), 8 + g))
    c_spec = pl.BlockSpec((CHUNK, N_STATE), lambda g, c: (c_of(c), 12 + g))
    dt_spec = pl.BlockSpec((CHUNK, 128), lambda g, c: (c_of(c), g))
    vec_spec = pl.BlockSpec((1, 128), lambda g, c: (0, g))
    h_spec = pl.BlockSpec((1, 1, HEADS_PER_GROUP, HEAD, N_STATE), lambda g, c: (c_of(c), g, 0, 0, 0))
    return x_spec, b_spec, c_spec, dt_spec, vec_spec, h_spec


def _ssd_fwd(xbc, dt_raw, bias, alog, dsk):
    x_spec, b_spec, c_spec, dt_spec, vec_spec, h_spec = _ssd_specs(False)

    def body(x_ref, b_ref, c_ref, dt_ref, bias_ref, alog_ref, dsk_ref, y_ref, hin_ref, h_scr):
        @pl.when(pl.program_id(1) == 0)
        def _():
            h_scr[...] = jnp.zeros_like(h_scr)

        hs = [h_scr[j] for j in range(HEADS_PER_GROUP)]
        for j in range(HEADS_PER_GROUP):
            hin_ref[0, 0, j] = hs[j]
        xs = [x_ref[:, pl.ds(j * HEAD, HEAD)] for j in range(HEADS_PER_GROUP)]
        res = _ssd_chunk(*xs, b_ref[...], c_ref[...], dt_ref[...], bias_ref[...], alog_ref[...], dsk_ref[...], *hs)
        for j in range(HEADS_PER_GROUP):
            y_ref[:, pl.ds(j * HEAD, HEAD)] = res[j]
            h_scr[j] = res[HEADS_PER_GROUP + j]

    return pl.pallas_call(
        body, name="ssd_fwd", grid=(N_GROUPS, SEQ // CHUNK),
        in_specs=[x_spec, b_spec, c_spec, dt_spec, vec_spec, vec_spec, vec_spec],
        out_specs=[x_spec, h_spec],
        out_shape=[jax.ShapeDtypeStruct((SEQ, D_SSM), F32),
                   jax.ShapeDtypeStruct((SEQ // CHUNK, N_GROUPS, HEADS_PER_GROUP, HEAD, N_STATE), F32)],
        scratch_shapes=[pltpu.VMEM((HEADS_PER_GROUP, HEAD, N_STATE), F32)],
        compiler_params=_params(("parallel", "arbitrary")),
    )(xbc, xbc, xbc, dt_raw, bias, alog, dsk)


def _ssd_bwd(xbc, dt_raw, bias, alog, dsk, h_in, dy):
    x_spec, b_spec, c_spec, dt_spec, vec_spec, h_spec = _ssd_specs(True)
    dxbc_x = pl.BlockSpec((CHUNK, 256), x_spec.index_map)

    def body(x_ref, b_ref, c_ref, dt_ref, bias_ref, alog_ref, dsk_ref, hin_ref, dy_ref,
             dx_ref, db_ref, dc_ref, ddt_ref, dbias_ref, dalog_ref, ddsk_ref, dh_scr):
        first = pl.program_id(1) == 0

        @pl.when(first)
        def _():
            dh_scr[...] = jnp.zeros_like(dh_scr)

        xs = [x_ref[:, pl.ds(j * HEAD, HEAD)] for j in range(HEADS_PER_GROUP)]
        hs = [hin_ref[0, 0, j] for j in range(HEADS_PER_GROUP)]
        cts = [dy_ref[:, pl.ds(j * HEAD, HEAD)] for j in range(HEADS_PER_GROUP)] + [dh_scr[j] for j in range(HEADS_PER_GROUP)]
        _, pullback = jax.vjp(_ssd_chunk, *xs, b_ref[...], c_ref[...], dt_ref[...], bias_ref[...], alog_ref[...],
                              dsk_ref[...], *hs)
        g = pullback(tuple(cts))
        for j in range(HEADS_PER_GROUP):
            dx_ref[:, pl.ds(j * HEAD, HEAD)] = g[j]
            dh_scr[j] = g[10 + j]
        db_ref[...] = g[4]
        dc_ref[...] = g[5]
        ddt_ref[...] = g[6].astype(ddt_ref.dtype)
        for o_ref, val in ((dbias_ref, g[7]), (dalog_ref, g[8]), (ddsk_ref, g[9])):
            @pl.when(first)
            def _(o_ref=o_ref, val=val):
                o_ref[...] = val

            @pl.when(jnp.logical_not(first))
            def _(o_ref=o_ref, val=val):
                o_ref[...] += val

    n_chunks = SEQ // CHUNK
    out_b = pl.BlockSpec((CHUNK, N_STATE), lambda g, c: (n_chunks - 1 - c, g))
    res = pl.pallas_call(
        body, name="ssd_bwd", grid=(N_GROUPS, n_chunks),
        in_specs=[x_spec, b_spec, c_spec, dt_spec, vec_spec, vec_spec, vec_spec, h_spec, x_spec],
        out_specs=[dxbc_x, out_b, out_b, dt_spec, vec_spec, vec_spec, vec_spec],
        out_shape=[jax.ShapeDtypeStruct((SEQ, D_SSM), F32), jax.ShapeDtypeStruct((SEQ, N_GROUPS * N_STATE), F32),
                   jax.ShapeDtypeStruct((SEQ, N_GROUPS * N_STATE), F32), jax.ShapeDtypeStruct((SEQ, DT_PAD), BF16),
                   jax.ShapeDtypeStruct((1, DT_PAD), F32), jax.ShapeDtypeStruct((1, DT_PAD), F32),
                   jax.ShapeDtypeStruct((1, DT_PAD), F32)],
        scratch_shapes=[pltpu.VMEM((HEADS_PER_GROUP, HEAD, N_STATE), F32)],
        compiler_params=_params(("parallel", "arbitrary")),
    )(xbc, xbc, xbc, dt_raw, bias, alog, dsk, h_in, dy)
    return res


CROSS_HEAD = 128
CROSS_ROWS = 512


def _cross_head(q, k, v, gq, gk):
    qn = _rms(q, gq) * (CROSS_HEAD ** -0.5)
    kn = _rms(k, gk)
    s = _bdot(qn, kn, NT)
    p = jnp.exp(s - jnp.max(s, axis=-1, keepdims=True))
    return _bdot(p, v, NN) / jnp.sum(p, axis=-1, keepdims=True)


def _cross_specs():
    q_spec = pl.BlockSpec((CROSS_ROWS, CROSS_HEAD), lambda h, i: (i, h))
    k_spec = pl.BlockSpec((N_MEM, CROSS_HEAD), lambda h, i: (0, h))
    v_spec = pl.BlockSpec((N_MEM, CROSS_HEAD), lambda h, i: (0, 4 + h))
    g_spec = pl.BlockSpec((1, CROSS_HEAD), lambda h, i: (0, 0))
    return q_spec, k_spec, v_spec, g_spec


def _cross_fwd(qc, kv, gq, gk):
    q_spec, k_spec, v_spec, g_spec = _cross_specs()

    def body(q_ref, k_ref, v_ref, gq_ref, gk_ref, o_ref):
        o_ref[...] = _cross_head(q_ref[...], k_ref[...], v_ref[...], gq_ref[...], gk_ref[...]).astype(o_ref.dtype)

    return pl.pallas_call(
        body, name="cross_fwd", grid=(4, SEQ // CROSS_ROWS),
        in_specs=[q_spec, k_spec, v_spec, g_spec, g_spec], out_specs=q_spec,
        out_shape=jax.ShapeDtypeStruct((SEQ, D_CROSS), BF16),
        compiler_params=_params(("parallel", "parallel")),
    )(qc, kv, kv, gq, gk)


def _cross_bwd(qc, kv, gq, gk, do):
    q_spec, k_spec, v_spec, g_spec = _cross_specs()

    def body(q_ref, k_ref, v_ref, gq_ref, gk_ref, do_ref, dq_ref, dk_ref, dv_ref, dgq_ref, dgk_ref):
        _, pullback = jax.vjp(_cross_head, q_ref[...], k_ref[...], v_ref[...], gq_ref[...], gk_ref[...])
        dq, dk, dv, dgq, dgk = pullback(do_ref[...].astype(F32))
        dq_ref[...] = dq.astype(dq_ref.dtype)
        row0 = pl.program_id(1) == 0
        all0 = jnp.logical_and(row0, pl.program_id(0) == 0)
        for o_ref, val, init in ((dk_ref, dk, row0), (dv_ref, dv, row0), (dgq_ref, dgq, all0), (dgk_ref, dgk, all0)):
            @pl.when(init)
            def _(o_ref=o_ref, val=val):
                o_ref[...] = val

            @pl.when(jnp.logical_not(init))
            def _(o_ref=o_ref, val=val):
                o_ref[...] += val

    return pl.pallas_call(
        body, name="cross_bwd", grid=(4, SEQ // CROSS_ROWS),
        in_specs=[q_spec, k_spec, v_spec, g_spec, g_spec, q_spec],
        out_specs=[q_spec, k_spec, k_spec, g_spec, g_spec],
        out_shape=[jax.ShapeDtypeStruct((SEQ, D_CROSS), BF16), jax.ShapeDtypeStruct((N_MEM, D_CROSS), F32),
                   jax.ShapeDtypeStruct((N_MEM, D_CROSS), F32), jax.ShapeDtypeStruct((1, CROSS_HEAD), F32),
                   jax.ShapeDtypeStruct((1, CROSS_HEAD), F32)],
        compiler_params=_params(("arbitrary", "arbitrary")),
    )(qc, kv, kv, gq, gk, do)


def _loss_head(y, target):
    tr = 256

    def body(y_ref, t_ref, dy_ref, dyb_ref, loss_ref):
        err = y_ref[...] - t_ref[...]
        dy = err * (1.0 / D_MODEL)
        dy_ref[...] = dy
        dyb_ref[...] = dy.astype(BF16)
        part = jnp.sum(jnp.sum(err * err, axis=1, keepdims=True), axis=0, keepdims=True) * (0.5 / D_MODEL)
        part = jnp.broadcast_to(part, (1, 128))

        @pl.when(pl.program_id(0) == 0)
        def _():
            loss_ref[...] = part

        @pl.when(pl.program_id(0) != 0)
        def _():
            loss_ref[...] += part

    blk = pl.BlockSpec((tr, D_MODEL), lambda i: (i, 0))
    return pl.pallas_call(
        body, name="loss_head", grid=(SEQ // tr,),
        in_specs=[blk, blk], out_specs=[blk, blk, pl.BlockSpec((1, 128), lambda i: (0, 0))],
        out_shape=[jax.ShapeDtypeStruct((SEQ, D_MODEL), F32), jax.ShapeDtypeStruct((SEQ, D_MODEL), BF16),
                   jax.ShapeDtypeStruct((1, 128), F32)],
        compiler_params=_params(("arbitrary",)),
    )(y, target)


def _pad_heads(v):
    return jnp.pad(v.reshape(N_GROUPS, HEADS_PER_GROUP), ((0, 0), (0, 128 - HEADS_PER_GROUP))).reshape(1, DT_PAD)


def _unpad_heads(v):
    return v.reshape(v.shape[0], N_GROUPS, 128)[:, :, :HEADS_PER_GROUP].reshape(v.shape[0], N_DT)


def _rope_tables(positions):
    half = ROT // 2
    inv_freq = ROPE_THETA ** (-2.0 * jnp.arange(half, dtype=F32) / ROT)
    ang = positions.reshape(SEQ, 1).astype(F32) * inv_freq
    cos, sin = jnp.cos(ang), jnp.sin(ang)
    ones, zeros = jnp.ones((SEQ, HEAD - ROT), F32), jnp.zeros((SEQ, HEAD - ROT), F32)
    cos_h = jnp.concatenate([cos, cos, ones], axis=1)
    sin_h = jnp.concatenate([-sin, sin, zeros], axis=1)
    return jnp.tile(cos_h, (1, 2)), jnp.tile(sin_h, (1, 2))


def _add_res(acc, res):
    return (acc + res,)


def _local_step(x, mem, positions, target, p, w):
    grads = {}
    cos, sin = _rope_tables(positions)
    gq2, gk2 = jnp.tile(p["g_q"], (1, 2)), jnp.tile(p["g_k"], (1, 2))
    bias, alog, dsk = _pad_heads(p["dt_bias"]), _pad_heads(p["a_log"]), _pad_heads(p["d_skip"])
    norm_out = [(D_MODEL, BF16, D_MODEL, 0, False)]

    h = _rowwise(_norm_fn, [_full(x)], [_full(p["g_mix"])], norm_out, name="norm_in")[0]
    proj = _matmul(h, w["w_main"], mode="nn", name="in_proj", outs=[F32])
    dt_raw = _matmul(h, w["w_dt"], mode="nn", name="dt_proj", outs=[F32])
    qk_rows = [(proj, 128, 0, True), (proj, 128, 8, True), (proj, 128, 16, True), _full(cos), _full(sin)]
    qk_vecs = [_full(gq2), _full(gk2)]
    qn, kn, vn = _rowwise(_qk_fn, qk_rows, qk_vecs, [(D_ATTN, F32, 128, 0, True)] * 3, name="qk_prep", groups=8)
    branches = [_attention_fwd(qn, kn, vn, b) for b in range(3)]
    merge_rows = [_full(o) for o, _ in branches] + [_full(lse) for _, lse in branches]
    attn = _rowwise(_merge_fn, merge_rows, [_full(p["g_attn_out"])], [(D_ATTN, BF16, D_ATTN, 0, False)], name="attn_merge")[0]
    xbc = _conv_fwd(proj, p["conv_w"], p["conv_b"])
    y_ssd, h_in = _ssd_fwd(xbc, dt_raw, bias, alog, dsk)
    gate_rows = [(y_ssd, 256, 0, True), (proj, 256, 12, True)]
    gate_vecs = [(p["g_ssm_out"], 256, 0, True)]
    ssm = _rowwise(_gate_fn, gate_rows, gate_vecs, [(D_SSM, BF16, 256, 0, True)], name="ssm_gate", groups=4)[0]
    mix = jnp.concatenate([attn, ssm], axis=1)
    x1 = _matmul(mix, w["w_out"], mode="nn", name="out_proj", outs=[F32], extra=(x,), epilogue=_add_res)
    hc = _rowwise(_norm_fn, [_full(x1)], [_full(p["g_cross"])], norm_out, name="norm_cross")[0]
    memh = _rowwise(_norm_fn, [_full(mem)], [_full(p["g_mem"])], norm_out, name="norm_mem", n_rows=N_MEM)[0]
    qc = _matmul(hc, w["w_cq"], mode="nn", name="cq_proj", outs=[F32])
    kv = _matmul(memh, w["w_ckv"], mode="nn", name="ckv_proj", outs=[F32])
    oc = _cross_fwd(qc, kv, p["g_cq"], p["g_ck"])
    x2 = _matmul(oc, w["w_co"], mode="nn", name="co_proj", outs=[F32], extra=(x1,), epilogue=_add_res)
    hm = _rowwise(_norm_fn, [_full(x2)], [_full(p["g_mlp"])], norm_out, name="norm_mlp")[0]
    u, act = _matmul(hm, w["w_up"], mode="nn", name="up_proj", outs=[F32, BF16],
                     epilogue=lambda acc: (acc, jnp.square(jnp.maximum(acc, 0.0))))
    x3 = _matmul(act, w["w_down"], mode="nn", name="down_proj", outs=[F32], extra=(x2,), epilogue=_add_res)
    dy, dyb, loss = _loss_head(x3, target)

    grads["w_down"] = _matmul(act, dyb, mode="tn", name="dw_down", outs=[BF16])
    du = _matmul(dyb, w["w_down"], mode="nt", name="d_act", outs=[BF16], extra=(u,),
                 epilogue=lambda acc, uu: (acc * (2.0 * jnp.maximum(uu, 0.0)),))
    grads["w_up"] = _matmul(hm, du, mode="tn", name="dw_up", outs=[BF16], col_shards=4)
    dhm = _matmul(du, w["w_up"], mode="nt", name="d_hm", outs=[F32])
    dx2, grads["g_mlp"] = _rowwise_vjp(
        _norm_fn, [_full(x2)], [_full(p["g_mlp"])], [[_full(dhm)]],
        [(0, D_MODEL, F32, D_MODEL, 0, False, _full(dy))], [(0, D_MODEL, D_MODEL, 0, False)], name="norm_mlp_bwd")
    grads["w_co"] = _matmul(oc, dx2, mode="tn", name="dw_co", outs=[BF16], col_shards=4)
    doc = _matmul(dx2, w["w_co"], mode="nt", name="d_oc", outs=[BF16])
    dqc, dkc, dvc, grads["g_cq"], grads["g_ck"] = _cross_bwd(qc, kv, p["g_cq"], p["g_ck"], doc)
    grads["w_cq"] = _matmul(hc, dqc, mode="tn", name="dw_cq", outs=[BF16])
    dhc = _matmul(dqc, w["w_cq"], mode="nt", name="d_hc", outs=[F32])
    dkv = jnp.concatenate([dkc, dvc], axis=1)
    grads["w_ckv"] = _matmul(memh, dkv, mode="tn", name="dw_ckv", outs=[BF16])
    dmemh = _matmul(dkv, w["w_ckv"], mode="nt", name="d_memh", outs=[F32])
    grads["g_mem"] = _rowwise_vjp(_norm_fn, [_full(mem)], [_full(p["g_mem"])], [[_full(dmemh)]], [],
                                  [(0, D_MODEL, D_MODEL, 0, False)], name="norm_mem_bwd", n_rows=N_MEM)[0]
    dx1, grads["g_cross"] = _rowwise_vjp(
        _norm_fn, [_full(x1)], [_full(p["g_cross"])], [[_full(dhc)]],
        [(0, D_MODEL, F32, D_MODEL, 0, False, _full(dx2))], [(0, D_MODEL, D_MODEL, 0, False)], name="norm_cross_bwd")
    grads["w_out"] = _matmul(mix, dx1, mode="tn", name="dw_out", outs=[BF16])
    dmix = _matmul(dx1, w["w_out"], mode="nt", name="d_mix", outs=[F32])
    merge_grads = [(i, D_ATTN, F32, D_ATTN, 0, False, None) for i in range(6)]
    *dol, grads["g_attn_out"] = _rowwise_vjp(
        _merge_fn, merge_rows, [_full(p["g_attn_out"])], [[(dmix, D_ATTN, 0, False)]],
        merge_grads, [(0, D_ATTN, D_ATTN, 0, False)], name="attn_merge_bwd")
    dqkv = [_attention_bwd(qn, kn, vn, dol[b], dol[3 + b], b) for b in range(3)]
    qk_cts = [[(dqkv[b][i], 128, 0, True) for b in range(3)] for i in range(3)]
    dq, dk, dv, dgq2, dgk2 = _rowwise_vjp(
        _qk_fn, qk_rows, qk_vecs, qk_cts, [(i, D_ATTN, BF16, 128, 0, True, None) for i in range(3)],
        [(0, 128, 128, 0, False), (1, 128, 128, 0, False)], name="qk_prep_bwd", groups=8)
    grads["g_q"] = dgq2[:, :HEAD] + dgq2[:, HEAD:]
    grads["g_k"] = dgk2[:, :HEAD] + dgk2[:, HEAD:]
    dy_ssd, dz, grads["g_ssm_out"] = _rowwise_vjp(
        _gate_fn, gate_rows, gate_vecs, [[(dmix, 256, 4, True)]],
        [(0, D_SSM, F32, 256, 0, True, None), (1, D_SSM, BF16, 256, 0, True, None)],
        [(0, D_SSM, 256, 0, True)], name="ssm_gate_bwd", groups=4)
    dxs, db, dc, ddt, dbias, dalog, ddsk = _ssd_bwd(xbc, dt_raw, bias, alog, dsk, h_in, dy_ssd)
    grads["dt_bias"], grads["a_log"], grads["d_skip"] = _unpad_heads(dbias), _unpad_heads(dalog), _unpad_heads(ddsk)
    dxbc_raw, dconv_w, grads["conv_b"] = _conv_bwd(proj, p["conv_w"], p["conv_b"], jnp.concatenate([dxs, db, dc], axis=1))
    grads["conv_w"] = dconv_w[:4]
    dproj = jnp.concatenate([dq, dk, dv, dz, dxbc_raw], axis=1)
    grads["w_main"] = _matmul(h, dproj, mode="tn", name="dw_main", outs=[BF16])
    grads["w_dt"] = _matmul(h, ddt, mode="tn", name="dw_dt", outs=[BF16])
    dh = _matmul(dproj, w["w_main"], mode="nt", name="d_h_main", outs=[F32])
    dh = _matmul(ddt, w["w_dt"], mode="nt", name="d_h_dt", outs=[F32], extra=(dh,), epilogue=_add_res)
    grad_x, grads["g_mix"] = _rowwise_vjp(
        _norm_fn, [_full(x)], [_full(p["g_mix"])], [[_full(dh)]],
        [(0, D_MODEL, F32, D_MODEL, 0, False, _full(dx1))], [(0, D_MODEL, D_MODEL, 0, False)], name="norm_in_bwd")
    return loss, grad_x, grads


MATRICES = ("w_in", "w_out", "w_cq", "w_ckv", "w_co", "w_up", "w_down")
ROW_SHARDED = ("w_out", "w_cq", "w_ckv", "w_down")
N_CHIPS = 4
ANY = pl.BlockSpec(memory_space=pl.ANY)


def _place():
    return lax.axis_index("x"), lax.axis_index("y"), lax.axis_index("c")


def _other_chips(x, y):
    return [(1 - x, y), (x, 1 - y), (1 - x, 1 - y)]


def _remote(src, dst, send_sem, recv_sem, device):
    return pltpu.make_async_remote_copy(src_ref=src, dst_ref=dst, send_sem=send_sem, recv_sem=recv_sem,
                                        device_id=device, device_id_type=MESH)


def _gathered_shape(name, shard):
    rows, cols = shard.shape
    if name == "w_in":
        return (N_CHIPS, rows, cols)
    return (N_CHIPS * rows, cols) if name in ROW_SHARDED else (rows, N_CHIPS * cols)


def _shard_window(name, ref, rows, cols, chip, half):
    r0, nr = (0, rows) if half is None else (half * (rows // 2), rows // 2)
    if name == "w_in":
        return ref.at[chip, pl.ds(r0, nr), :]
    if name in ROW_SHARDED:
        return ref.at[pl.ds(chip * rows + r0, nr), :]
    return ref.at[pl.ds(r0, nr), pl.ds(pl.multiple_of(chip * cols, 128), cols)]


def _cast_into_gathered(w, name, chip):
    rows, cols = w.shape
    tr = _tile(rows, ROW_TILE)

    def body(chip_ref, w_ref, o_ref):
        o_ref[...] = w_ref[...].astype(BF16)

    if name == "w_in":
        out_spec = pl.BlockSpec((None, tr, cols), lambda i, chip_ref: (chip_ref[0], i, 0))
    elif name in ROW_SHARDED:
        out_spec = pl.BlockSpec((tr, cols), lambda i, chip_ref: (chip_ref[0] * (rows // tr) + i, 0))
    else:
        out_spec = pl.BlockSpec((tr, cols), lambda i, chip_ref: (i, chip_ref[0]))
    grid_spec = pltpu.PrefetchScalarGridSpec(
        num_scalar_prefetch=1, grid=(rows // tr,),
        in_specs=[pl.BlockSpec((tr, cols), lambda i, chip_ref: (i, 0))], out_specs=out_spec)
    return pl.pallas_call(body, name="cast_" + name, grid_spec=grid_spec,
                          out_shape=jax.ShapeDtypeStruct(_gathered_shape(name, w), BF16),
                          compiler_params=_params(("parallel",)))(chip.reshape(1).astype(jnp.int32), w)


def _gather_weights(arrs, shard_shapes):
    names = list(arrs)
    n = len(names)

    def body(*refs):
        outs = refs[n:2 * n]
        send1, recv1, send2, recv2 = refs[2 * n:]
        x, y, c = _place()
        mine, sibling, chips = 2 * x + y, (x, y, 1 - c), _other_chips(x, y)
        first, passed = [], []
        for w, name in enumerate(names):
            win = _shard_window(name, outs[w], *shard_shapes[name], mine, c)
            for k, (px, py) in enumerate(chips):
                cp = _remote(win, win, send1.at[w, k], recv1.at[w, k], (px, py, c))
                cp.start()
                first.append(cp)
        for w, name in enumerate(names):
            for k, (px, py) in enumerate(chips):
                win = _shard_window(name, outs[w], *shard_shapes[name], 2 * px + py, c)
                _remote(win, win, send1.at[w, k], recv1.at[w, k], (px, py, c)).wait_recv()
                cp = _remote(win, win, send2.at[w, k], recv2.at[w, k], sibling)
                cp.start()
                passed.append(cp)
        for w, name in enumerate(names):
            for k, (px, py) in enumerate(chips):
                win = _shard_window(name, outs[w], *shard_shapes[name], 2 * px + py, 1 - c)
                _remote(win, win, send2.at[w, k], recv2.at[w, k], sibling).wait_recv()
        for cp in first + passed:
            cp.wait_send()

    res = pl.pallas_call(
        body, name="gather_weights",
        in_specs=[ANY] * n, out_specs=[ANY] * n,
        out_shape=[jax.ShapeDtypeStruct(arrs[name].shape, BF16) for name in names],
        input_output_aliases={w: w for w in range(n)},
        scratch_shapes=[pltpu.SemaphoreType.DMA((n, 3))] * 4,
    )(*[arrs[name] for name in names])
    return dict(zip(names, res))


def _sibling_swap(arrs):
    n = len(arrs)

    def body(*refs):
        ins, outs, send, recv = refs[:n], refs[n:2 * n], refs[2 * n], refs[2 * n + 1]
        x, y, c = _place()
        cps = [_remote(ins[w].at[:, 1 - c], outs[w], send.at[w], recv.at[w], (x, y, 1 - c)) for w in range(n)]
        for cp in cps:
            cp.start()
        for cp in cps:
            cp.wait()

    return pl.pallas_call(
        body, name="grad_sibling_swap", in_specs=[ANY] * n, out_specs=[ANY] * n,
        out_shape=[jax.ShapeDtypeStruct((a.shape[0],) + a.shape[2:], a.dtype) for a in arrs],
        scratch_shapes=[pltpu.SemaphoreType.DMA((n,))] * 2,
    )(*arrs)


def _chip_scatter(arrs):
    n = len(arrs)

    def body(*refs):
        ins, outs, send, recv = refs[:n], refs[n:2 * n], refs[2 * n], refs[2 * n + 1]
        x, y, c = _place()
        cps = []
        for w in range(n):
            for k, (px, py) in enumerate(_other_chips(x, y)):
                cp = _remote(ins[w].at[2 * px + py], outs[w].at[k], send.at[w, k], recv.at[w, k], (px, py, c))
                cp.start()
                cps.append(cp)
        for cp in cps:
            cp.wait()

    return pl.pallas_call(
        body, name="grad_chip_scatter", in_specs=[ANY] * n, out_specs=[ANY] * n,
        out_shape=[jax.ShapeDtypeStruct((3,) + a.shape[1:], a.dtype) for a in arrs],
        scratch_shapes=[pltpu.SemaphoreType.DMA((n, 3))] * 2,
    )(*arrs)


def _sibling_share(arrs):
    n = len(arrs)

    def body(*refs):
        ins, outs, send, recv = refs[:n], refs[n:2 * n], refs[2 * n], refs[2 * n + 1]
        x, y, c = _place()
        cps = [_remote(ins[w], outs[w], send.at[w], recv.at[w], (x, y, 1 - c)) for w in range(n)]
        for cp in cps:
            cp.start()
        for cp in cps:
            cp.wait()

    return pl.pallas_call(
        body, name="grad_sibling_share", in_specs=[ANY] * n, out_specs=[ANY] * n,
        out_shape=[jax.ShapeDtypeStruct(a.shape, a.dtype) for a in arrs],
        scratch_shapes=[pltpu.SemaphoreType.DMA((n,))] * 2,
    )(*arrs)


def _small_allreduce(buf, name):
    rows = buf.shape[0]

    def body(x_ref, out_ref, all_ref, send_sems, recv_sems, local_sem):
        x, y, c = _place()
        me, sibling, chips = (x, y, c), (x, y, 1 - c), _other_chips(x, y)

        def block(px, py, pc):
            return all_ref.at[pl.ds((4 * px + 2 * py + pc) * rows, rows), :]

        def copy(k, blk, to, src=None):
            return _remote(block(*blk) if src is None else src, block(*blk), send_sems.at[k], recv_sems.at[k], to)

        own = pltpu.make_async_copy(x_ref, block(*me), local_sem)
        own.start()
        first = [copy(0, me, sibling, src=x_ref)] + [copy(1 + j, me, (*chip, c), src=x_ref) for j, chip in enumerate(chips)]
        for cp in first:
            cp.start()
        passed = [copy(4 + j, (*chip, c), sibling) for j, chip in enumerate(chips)]
        for j, chip in enumerate(chips):
            copy(1 + j, (*chip, c), me).wait_recv()
            passed[j].start()
        copy(0, sibling, me).wait_recv()
        for j, chip in enumerate(chips):
            copy(4 + j, (*chip, 1 - c), me).wait_recv()
        for cp in first + passed:
            cp.wait_send()
        own.wait()
        acc = all_ref[pl.ds(0, rows), :]
        for d in range(1, 8):
            acc = acc + all_ref[pl.ds(d * rows, rows), :]
        out_ref[...] = acc

    vmem = pl.BlockSpec(memory_space=pltpu.VMEM)
    return pl.pallas_call(
        body, name=name, in_specs=[vmem], out_specs=vmem,
        out_shape=jax.ShapeDtypeStruct(buf.shape, F32),
        scratch_shapes=[pltpu.VMEM((8 * rows, 128), F32), pltpu.SemaphoreType.DMA((7,)), pltpu.SemaphoreType.DMA((7,)),
                        pltpu.SemaphoreType.DMA],
    )(buf)


ROW_TILE = 256


def _add_halves(arr, recv, c, name):
    _, _, hr, cols = arr.shape
    tr = _tile(hr, ROW_TILE)

    def body(c_ref, a_ref, r_ref, o_ref):
        o_ref[...] = (a_ref[...].astype(F32) + r_ref[...].astype(F32)).astype(o_ref.dtype)

    piece = pl.BlockSpec((None, tr, cols), lambda j, i, c_ref: (j, i, 0))
    grid_spec = pltpu.PrefetchScalarGridSpec(
        num_scalar_prefetch=1, grid=(N_CHIPS, hr // tr),
        in_specs=[pl.BlockSpec((None, None, tr, cols), lambda j, i, c_ref: (j, c_ref[0], i, 0)), piece], out_specs=piece)
    return pl.pallas_call(body, name=name, grid_spec=grid_spec, out_shape=jax.ShapeDtypeStruct(recv.shape, BF16),
                          compiler_params=_params(("parallel", "parallel")))(c.reshape(1).astype(jnp.int32), arr, recv)


def _flip_slot(d):
    return jnp.where(d == 1, 1, jnp.where(d == 3, 2, 0))


def _sum_chips(p, q, chip, name):
    _, hr, cols = p.shape
    tr = _tile(hr, ROW_TILE)

    def body(chip_ref, p_ref, q_ref, o_ref):
        j = pl.program_id(1)
        term = jnp.where(j == chip_ref[0], p_ref[...].astype(F32), q_ref[...].astype(F32))

        @pl.when(j == 0)
        def _():
            o_ref[...] = term

        @pl.when(j != 0)
        def _():
            o_ref[...] += term

    grid_spec = pltpu.PrefetchScalarGridSpec(
        num_scalar_prefetch=1, grid=(hr // tr, N_CHIPS),
        in_specs=[pl.BlockSpec((None, tr, cols), lambda i, j, chip_ref: (chip_ref[0], i, 0)),
                  pl.BlockSpec((None, tr, cols), lambda i, j, chip_ref: (_flip_slot(j ^ chip_ref[0]), i, 0))],
        out_specs=pl.BlockSpec((tr, cols), lambda i, j, chip_ref: (i, 0)))
    return pl.pallas_call(body, name=name, grid_spec=grid_spec, out_shape=jax.ShapeDtypeStruct((hr, cols), F32),
                          compiler_params=_params(("parallel", "arbitrary")))(chip.reshape(1).astype(jnp.int32), p, q)


def _adamw_halves(w, g_own, g_other, m, v, c, name):
    rows, cols = w.shape
    tr = _tile(rows // 2, ROW_TILE)
    per_half = rows // 2 // tr

    def body(c_ref, w_ref, own_ref, other_ref, m_ref, v_ref, g_ref, d_ref, nm_ref, nv_ref):
        mine = (pl.program_id(0) // per_half) == c_ref[0]
        g_ = jnp.where(mine, own_ref[...], other_ref[...])
        g_ref[...] = g_
        d_ref[...], nm_ref[...], nv_ref[...] = _adamw_math(w_ref[...], g_, m_ref[...], v_ref[...])

    blk = pl.BlockSpec((tr, cols), lambda i, c_ref: (i, 0))
    half = pl.BlockSpec((tr, cols), lambda i, c_ref: (i % per_half, 0))
    grid_spec = pltpu.PrefetchScalarGridSpec(num_scalar_prefetch=1, grid=(rows // tr,),
                                             in_specs=[blk, half, half, blk, blk], out_specs=[blk] * 4)
    return pl.pallas_call(body, name=name, grid_spec=grid_spec, out_shape=[jax.ShapeDtypeStruct(w.shape, F32)] * 4,
                          compiler_params=_params(("parallel",)))(c.reshape(1).astype(jnp.int32), w, g_own, g_other, m, v)


def _adamw_math(w, g, m, v):
    m_new = ADAM_B1 * m + (1.0 - ADAM_B1) * g
    v_new = ADAM_B2 * v + (1.0 - ADAM_B2) * (g * g)
    m_hat = m_new / (1.0 - ADAM_B1 ** ADAM_STEP)
    v_hat = v_new / (1.0 - ADAM_B2 ** ADAM_STEP)
    return -ADAM_LR * (m_hat / (jnp.sqrt(v_hat) + ADAM_EPS) + ADAM_WD * w), m_new, v_new


def _adamw(w, g, m, v, name):
    rows, cols = w.shape
    tr = _tile(rows, ROW_TILE)

    def body(w_ref, g_ref, m_ref, v_ref, d_ref, nm_ref, nv_ref):
        d_ref[...], nm_ref[...], nv_ref[...] = _adamw_math(w_ref[...], g_ref[...], m_ref[...], v_ref[...])

    blk = pl.BlockSpec((tr, cols), lambda i: (i, 0))
    return pl.pallas_call(body, name=name, grid=(rows // tr,), in_specs=[blk] * 4, out_specs=[blk] * 3,
                          out_shape=[jax.ShapeDtypeStruct(w.shape, F32)] * 3, compiler_params=_params(("parallel",)))(w, g, m, v)


VECTORS = ("g_mix", "g_q", "g_k", "g_attn_out", "conv_b", "dt_bias", "a_log", "d_skip", "g_ssm_out", "g_cross", "g_mem",
           "g_cq", "g_ck", "g_mlp")
WEIGHTS = ("g_mix", "w_in", "g_q", "g_k", "g_attn_out", "conv_w", "conv_b", "dt_bias", "a_log", "d_skip", "g_ssm_out", "w_out",
           "g_cross", "g_mem", "w_cq", "w_ckv", "g_cq", "g_ck", "w_co", "g_mlp", "w_up", "w_down")


def _pack(parts):
    flat = jnp.concatenate([t.reshape(-1) for t in parts])
    total = -(-flat.shape[0] // 1024) * 1024
    return jnp.pad(flat, (0, total - flat.shape[0])).reshape(total // 128, 128)


def _unpack(buf, shapes):
    flat, out, pos = buf.reshape(-1), [], 0
    for shape in shapes:
        size = math.prod(shape)
        out.append(flat[pos:pos + size].reshape(shape))
        pos += size
    return out


def kernel(x, mem, positions, g_mix, w_in, g_q, g_k, g_attn_out, conv_w, conv_b, dt_bias, a_log, d_skip, g_ssm_out, w_out, g_cross, g_mem, w_cq, w_ckv, g_cq, g_ck, w_co, g_mlp, w_up, w_down, loss_target, m_g_mix, m_w_in, m_g_q, m_g_k, m_g_attn_out, m_conv_w, m_conv_b, m_dt_bias, m_a_log, m_d_skip, m_g_ssm_out, m_w_out, m_g_cross, m_g_mem, m_w_cq, m_w_ckv, m_g_cq, m_g_ck, m_w_co, m_g_mlp, m_w_up, m_w_down, v_g_mix, v_w_in, v_g_q, v_g_k, v_g_attn_out, v_conv_w, v_conv_b, v_dt_bias, v_a_log, v_d_skip, v_g_ssm_out, v_w_out, v_g_cross, v_g_mem, v_w_cq, v_w_ckv, v_g_cq, v_g_ck, v_w_co, v_g_mlp, v_w_up, v_w_down):
    args = dict(locals())
    weights = {n: args[n][0] for n in WEIGHTS}
    mom_m = {n: args["m_" + n][0] for n in WEIGHTS}
    mom_v = {n: args["v_" + n][0] for n in WEIGHTS}
    x_idx, y_idx, c_idx = _place()
    chip = 2 * x_idx + y_idx

    full = _gather_weights({n: _cast_into_gathered(weights[n], n, chip) for n in MATRICES},
                           {n: weights[n].shape for n in MATRICES})
    w_in_full = jnp.transpose(full.pop("w_in"), (1, 0, 2)).reshape(D_MODEL, D_MAIN + N_DT)
    full["w_main"] = w_in_full[:, :D_MAIN]
    full["w_dt"] = jnp.pad(w_in_full[:, D_MAIN:].reshape(D_MODEL, N_GROUPS, HEADS_PER_GROUP),
                           ((0, 0), (0, 0), (0, 128 - HEADS_PER_GROUP))).reshape(D_MODEL, DT_PAD)
    conv_parts = _small_allreduce(_pack([jnp.zeros((N_CHIPS, 4, 512), F32).at[chip].set(0.5 * weights["conv_w"])]),
                                  "gather_conv_taps")
    conv_full = _unpack(conv_parts, [(N_CHIPS, 4, 512)])[0].transpose(1, 0, 2).reshape(4, 4 * 512)
    params = {n: weights[n].reshape(1, -1) for n in VECTORS}
    params["conv_w"] = conv_full

    loss, grad_x, grads = _local_step(x[0], mem[0], positions[0], loss_target[0], params, full)

    gw_in = jnp.concatenate([grads.pop("w_main"), _unpad_heads(grads.pop("w_dt"))], axis=1)
    grads["w_in"] = gw_in.reshape(D_MODEL, N_CHIPS, gw_in.shape[1] // N_CHIPS).transpose(1, 0, 2)
    pieces = []
    for n in MATRICES:
        rows, cols = weights[n].shape
        pieces.append(grads[n].reshape(N_CHIPS, 2, rows // 2, cols))
    from_sibling = _sibling_swap(pieces)
    chip_sums = [_add_halves(a, r, c_idx, "add_halves_" + n) for n, a, r in zip(MATRICES, pieces, from_sibling)]
    from_chips = _chip_scatter(chip_sums)
    halves = [_sum_chips(p, q, chip, "sum_chips_" + n) for n, p, q in zip(MATRICES, chip_sums, from_chips)]
    other_halves = _sibling_share(halves)
    out_g, out_d, out_m, out_v = {}, {}, {}, {}
    for n, own, other in zip(MATRICES, halves, other_halves):
        out_g[n], out_d[n], out_m[n], out_v[n] = _adamw_halves(weights[n], own, other, mom_m[n], mom_v[n], c_idx, "adamw_" + n)

    small = [grads[n] for n in VECTORS] + [grads["conv_w"]]
    summed = _unpack(_small_allreduce(_pack(small), "allreduce_vectors"), [t.shape for t in small])
    g_small = dict(zip(VECTORS, summed[:-1]))
    g_small["conv_w"] = lax.dynamic_slice_in_dim(summed[-1], chip * 512, 512, axis=1)
    names = VECTORS + ("conv_w",)
    shapes = [weights[n].shape for n in names]
    packed = [_pack([src[n] for n in names]) for src in (weights, g_small, mom_m, mom_v)]
    small_out = [_unpack(t, shapes) for t in _adamw(*packed, "adamw_small")]
    for i, n in enumerate(names):
        out_g[n] = g_small[n].reshape(shapes[i])
        out_d[n], out_m[n], out_v[n] = small_out[0][i], small_out[1][i], small_out[2][i]

    total_loss = lax.psum(loss[0, 0], ("x", "y", "c"))
    outs = [total_loss, grad_x[None]]
    for group in (out_g, out_d, out_m, out_v):
        outs += [group[n][None] for n in WEIGHTS]
    return tuple(outs)
```

```python
import functools
import math

import jax
import jax.numpy as jnp
from jax import lax
from jax.experimental import pallas as pl
from jax.experimental.pallas import tpu as pltpu

F32 = jnp.float32
BF16 = jnp.bfloat16

SEQ = 2048
D_MODEL = 2048
HEAD = 64
D_ATTN = 1024
D_SSM = 1024
N_GROUPS = 4
N_STATE = 128
CHUNK = 128
ATT_BLK = 128
N_MEM = 256
D_CROSS = 512
D_FF = 8192
D_MAIN = 6144
N_DT = 16
DT_PAD = 512
ROT = 16
ROPE_THETA = 500000.0
EPS = 1e-6
NEG = -1e30
BRANCH_BLOCKS = (16, 4, 1)
DILATIONS = (1, 4, 16)

ADAM_LR, ADAM_B1, ADAM_B2, ADAM_EPS, ADAM_WD, ADAM_STEP = 0.001, 0.9, 0.999, 1e-08, 0.01, 10

VMEM_LIMIT = 56 * 1024 * 1024
MESH = pl.DeviceIdType.MESH


def _params(sem, **kw):
    return pltpu.CompilerParams(dimension_semantics=sem, vmem_limit_bytes=VMEM_LIMIT, **kw)


def _bdot(a, b, dims):
    return lax.dot_general(a.astype(BF16), b.astype(BF16), (dims, ((), ())), preferred_element_type=F32)


def _fdot(a, b, dims):
    return lax.dot_general(a, b, (dims, ((), ())), preferred_element_type=F32, precision=lax.Precision.HIGHEST)


NN = ((1,), (0,))
NT = ((1,), (1,))
TN = ((0,), (0,))


def _tile(n, want):
    t = min(n, want)
    while n % t:
        t //= 2
    return t


def _matmul(a, b, *, mode, name, outs, extra=(), epilogue=None, col_shards=1, tm=512, tn=1024, tk=1024):
    if mode == "nn":
        (m, k), n = a.shape, b.shape[1]
    elif mode == "nt":
        (m, k), n = a.shape, b.shape[0]
    else:
        (k, m), n = a.shape, b.shape[1]
    tm, tn, tk = _tile(m, tm), _tile(n // col_shards, tn), _tile(k, tk)
    nk = k // tk
    per_shard = n // col_shards // tn
    dims = {"nn": NN, "nt": NT, "tn": TN}[mode]
    a_spec = pl.BlockSpec((tk, tm), lambda i, j, kk: (kk, i)) if mode == "tn" else pl.BlockSpec((tm, tk), lambda i, j, kk: (i, kk))
    b_spec = pl.BlockSpec((tn, tk), lambda i, j, kk: (j, kk)) if mode == "nt" else pl.BlockSpec((tk, tn), lambda i, j, kk: (kk, j))
    o_spec = pl.BlockSpec((tm, tn), lambda i, j, kk: (i, j))
    n_extra, n_out = len(extra), len(outs)

    def body(a_ref, b_ref, *rest):
        extra_refs, out_refs, acc_ref = rest[:n_extra], rest[n_extra:n_extra + n_out], rest[-1]
        kk = pl.program_id(2)

        @pl.when(kk == 0)
        def _():
            acc_ref[...] = jnp.zeros_like(acc_ref)

        acc_ref[...] += _bdot(a_ref[...], b_ref[...], dims)

        @pl.when(kk == nk - 1)
        def _():
            acc = acc_ref[...]
            res = (acc,) if epilogue is None else epilogue(acc, *[e[...] for e in extra_refs])
            for o_ref, r in zip(out_refs, res):
                o_ref[...] = r.astype(o_ref.dtype)

    if col_shards == 1:
        out_specs, out_dims = [o_spec] * n_out, (m, n)
    else:
        sharded = pl.BlockSpec((None, tm, tn), lambda i, j, kk: (j // per_shard, i, j % per_shard))
        out_specs, out_dims = [sharded] * n_out, (col_shards, m, n // col_shards)
    res = pl.pallas_call(
        body, name=name, grid=(m // tm, n // tn, nk),
        in_specs=[a_spec, b_spec] + [o_spec] * n_extra,
        out_specs=out_specs,
        out_shape=[jax.ShapeDtypeStruct(out_dims, dt) for dt in outs],
        scratch_shapes=[pltpu.VMEM((tm, tn), F32)],
        compiler_params=_params(("parallel", "parallel", "arbitrary")),
    )(a, b, *extra)
    return res[0] if n_out == 1 else res


def _row_spec(tr, bw, cb, per_group):
    return pl.BlockSpec((tr, bw), (lambda g, i: (i, cb + g)) if per_group else (lambda g, i: (i, cb)))


def _vec_spec(bw, cb, per_group):
    return pl.BlockSpec((1, bw), (lambda g, i: (0, cb + g)) if per_group else (lambda g, i: (0, cb)))


def _rowwise(fn, rows, vecs, outs, *, name, n_rows=SEQ, tr=256, groups=1):
    n_r, n_v = len(rows), len(vecs)

    def body(*refs):
        vals = [r[...].astype(F32) for r in refs[:n_r + n_v]]
        res = fn(*vals)
        for o_ref, r in zip(refs[n_r + n_v:], res):
            o_ref[...] = r.astype(o_ref.dtype)

    res = pl.pallas_call(
        body, name=name, grid=(groups, n_rows // tr),
        in_specs=[_row_spec(tr, bw, cb, pg) for _, bw, cb, pg in rows] + [_vec_spec(bw, cb, pg) for _, bw, cb, pg in vecs],
        out_specs=[_row_spec(tr, bw, cb, pg) for _, _, bw, cb, pg in outs],
        out_shape=[jax.ShapeDtypeStruct((n_rows, w), dt) for w, dt, _, _, _ in outs],
        compiler_params=_params(("parallel", "parallel")),
    )(*[r[0] for r in rows], *[v[0] for v in vecs])
    return res


def _rowwise_vjp(fn, rows, vecs, cts, row_grads, vec_grads, *, name, n_rows=SEQ, tr=256, groups=1):
    n_r, n_v = len(rows), len(vecs)
    ct_ops = [op for group in cts for op in group]
    ct_sizes = [len(group) for group in cts]
    res_ops = [g[6] for g in row_grads if g[6] is not None]
    n_ct, n_res, n_rg = len(ct_ops), len(res_ops), len(row_grads)

    def body(*refs):
        vals = [r[...].astype(F32) for r in refs[:n_r + n_v]]
        pos = n_r + n_v
        ct_vals = []
        for size in ct_sizes:
            acc = refs[pos][...].astype(F32)
            for t in range(1, size):
                acc = acc + refs[pos + t][...].astype(F32)
            ct_vals.append(acc)
            pos += size
        res_refs = refs[pos:pos + n_res]
        out_refs = refs[pos + n_res:]
        _, pullback = jax.vjp(fn, *vals)
        grads = pullback(tuple(ct_vals))
        r_i = 0
        for o_ref, g in zip(out_refs[:n_rg], row_grads):
            val = grads[g[0]]
            if g[6] is not None:
                val = val + res_refs[r_i][...].astype(F32)
                r_i += 1
            o_ref[...] = val.astype(o_ref.dtype)
        first = (pl.program_id(1) == 0)
        for o_ref, g in zip(out_refs[n_rg:], vec_grads):
            val = jnp.sum(grads[n_r + g[0]], axis=0, keepdims=True)
            init = first if g[4] else jnp.logical_and(first, pl.program_id(0) == 0)

            @pl.when(init)
            def _(o_ref=o_ref, val=val):
                o_ref[...] = val

            @pl.when(jnp.logical_not(init))
            def _(o_ref=o_ref, val=val):
                o_ref[...] += val

    in_specs = [_row_spec(tr, bw, cb, pg) for _, bw, cb, pg in rows] + [_vec_spec(bw, cb, pg) for _, bw, cb, pg in vecs]
    in_specs += [_row_spec(tr, bw, cb, pg) for _, bw, cb, pg in ct_ops + res_ops]
    out_specs = [_row_spec(tr, g[3], g[4], g[5]) for g in row_grads] + [_vec_spec(g[2], g[3], g[4]) for g in vec_grads]
    out_shape = [jax.ShapeDtypeStruct((n_rows, g[1]), g[2]) for g in row_grads]
    out_shape += [jax.ShapeDtypeStruct((1, g[1]), F32) for g in vec_grads]
    return pl.pallas_call(
        body, name=name, grid=(groups, n_rows // tr),
        in_specs=in_specs, out_specs=out_specs, out_shape=out_shape,
        compiler_params=_params(("arbitrary", "arbitrary")),
    )(*[r[0] for r in rows], *[v[0] for v in vecs], *[c[0] for c in ct_ops], *[r[0] for r in res_ops])


def _full(arr, width=None):
    return (arr, arr.shape[1] if width is None else width, 0, False)


def _make_xor(sh):
    def raw(x):
        n = x.shape[-1]
        lane = lax.broadcasted_iota(jnp.int32, x.shape, x.ndim - 1)
        up = pltpu.roll(x, n - sh, x.ndim - 1)
        down = pltpu.roll(x, sh, x.ndim - 1)
        return jnp.where((lane & sh) == 0, up, down)

    f = jax.custom_vjp(raw)
    f.defvjp(lambda x: (raw(x), None), lambda _, ct: (raw(ct),))
    return f


_XOR = {sh: _make_xor(sh) for sh in (1, 2, 4, 8, 16, 32)}


def _head_sum(x):
    for sh in (1, 2, 4, 8, 16, 32):
        x = x + _XOR[sh](x)
    return x


def _rms(x, g):
    return x * lax.rsqrt(jnp.mean(x * x, axis=-1, keepdims=True) + EPS) * g


def _head_rms_rope(x, g, cos, sin, scale):
    y = x * lax.rsqrt(_head_sum(x * x) * (1.0 / HEAD) + EPS) * g
    return (y * cos + _XOR[8](y) * sin) * scale


def _qk_fn(q, k, v, cos, sin, gq, gk):
    return (_head_rms_rope(q, gq, cos, sin, HEAD ** -0.5), _head_rms_rope(k, gk, cos, sin, 1.0), v)


def _norm_fn(x, g):
    return (_rms(x, g),)


def _merge_fn(o0, o1, o2, l0, l1, l2, g):
    m = jnp.maximum(jnp.maximum(l0, l1), l2)
    e0, e1, e2 = jnp.exp(l0 - m), jnp.exp(l1 - m), jnp.exp(l2 - m)
    mix = (e0 * o0 + e1 * o1 + e2 * o2) / (e0 + e1 + e2)
    return (_rms(mix, g),)


def _gate_fn(y, z, g):
    return (_rms(y * (z * jax.nn.sigmoid(z)), g),)


def _attn_pair(q, kc, vc, kp=None, vp=None, has_prev=None):
    qi = lax.broadcasted_iota(jnp.int32, (ATT_BLK, ATT_BLK), 0)
    kj = lax.broadcasted_iota(jnp.int32, (ATT_BLK, ATT_BLK), 1)
    lane = lax.broadcasted_iota(jnp.int32, (1, 2 * HEAD), 1)
    o, lse = 0.0, 0.0
    for h in range(2):
        pick = ((lane >= h * HEAD) & (lane < (h + 1) * HEAD)).astype(F32)
        qh = q * pick
        s_c = jnp.where(qi >= kj, _bdot(qh, kc, NT), NEG)
        m = jnp.max(s_c, axis=-1, keepdims=True)
        if kp is not None:
            s_p = jnp.where(jnp.logical_and(kj >= qi, has_prev), _bdot(qh, kp, NT), NEG)
            m = jnp.maximum(m, jnp.max(s_p, axis=-1, keepdims=True))
        p_c = jnp.exp(s_c - m)
        den = jnp.sum(p_c, axis=-1, keepdims=True)
        acc = _bdot(p_c, vc, NN)
        if kp is not None:
            p_p = jnp.exp(s_p - m)
            den = den + jnp.sum(p_p, axis=-1, keepdims=True)
            acc = acc + _bdot(p_p, vp, NN)
        o = o + pick * (acc / den)
        lse = lse + pick * (m + jnp.log(den))
    return o, lse


def _attn_config(b):
    r = DILATIONS[b]
    return r, ATT_BLK * r, (512 if r == 1 else 128), BRANCH_BLOCKS[b] > 1


def _for_residues(r, fn):
    if r <= 4:
        for rho in range(r):
            fn(rho)
    else:
        def step(t, carry):
            for u in range(4):
                fn(4 * t + u)
            return carry

        lax.fori_loop(0, r // 4, step, 0)


def _strided_rows(start, r):
    if r > 1:
        return pl.ds(start, ATT_BLK, stride=r)
    return pl.ds(start if isinstance(start, int) else pl.multiple_of(start, ATT_BLK), ATT_BLK)


def _attention_fwd(qn, kn, vn, b):
    r, rows, lanes, with_prev = _attn_config(b)
    cur = pl.BlockSpec((rows, lanes), lambda g, n: (n, g))
    prev = pl.BlockSpec((rows, lanes), lambda g, n: (jnp.maximum(n - 1, 0), g))

    def body(*refs):
        ins, (o_ref, l_ref) = refs[:-2], refs[-2:]
        has_prev = pl.program_id(1) > 0

        def one(rho):
            sub = _strided_rows(rho, r)
            for pair in range(lanes // 128):
                sl = pl.ds(pair * 128, 128)
                args = [ref[sub, sl] for ref in ins] + ([has_prev] if with_prev else [])
                o_ref[sub, sl], l_ref[sub, sl] = _attn_pair(*args)

        _for_residues(r, one)

    operands = (qn, kn, vn, kn, vn) if with_prev else (qn, kn, vn)
    return pl.pallas_call(
        body, name="attn_fwd_%d" % r, grid=(D_ATTN // lanes, SEQ // rows),
        in_specs=[cur, cur, cur] + ([prev, prev] if with_prev else []), out_specs=[cur, cur],
        out_shape=[jax.ShapeDtypeStruct((SEQ, D_ATTN), F32)] * 2,
        compiler_params=_params(("parallel", "parallel")),
    )(*operands)


def _attention_bwd(qn, kn, vn, do, dl, b):
    r, rows, lanes, with_prev = _attn_config(b)
    cur = pl.BlockSpec((rows, lanes), lambda g, n: (n, g))
    prev = pl.BlockSpec((rows, lanes), lambda g, n: (jnp.maximum(n - 1, 0), g))
    whole = pl.BlockSpec((SEQ, lanes), lambda g, n: (0, g))
    n_in = 5 if with_prev else 3

    def body(*refs):
        ins, (do_ref, dl_ref, dq_ref, dk_ref, dv_ref) = refs[:n_in], refs[n_in:]
        n = pl.program_id(1)
        has_prev = n > 0

        @pl.when(n == 0)
        def _():
            dk_ref[...] = jnp.zeros_like(dk_ref)
            dv_ref[...] = jnp.zeros_like(dv_ref)

        def one(rho):
            sub = _strided_rows(rho, r)
            sub_c = _strided_rows(n * rows + rho, r)
            sub_p = _strided_rows(jnp.maximum(n - 1, 0) * rows + rho, r)
            for pair in range(lanes // 128):
                sl = pl.ds(pair * 128, 128)
                vals = [ref[sub, sl] for ref in ins]
                if with_prev:
                    _, pullback = jax.vjp(lambda *a: _attn_pair(*a, has_prev), *vals)
                else:
                    _, pullback = jax.vjp(_attn_pair, *vals)
                grads = pullback((do_ref[sub, sl], dl_ref[sub, sl]))
                dq_ref[sub, sl] = grads[0]
                dk_ref[sub_c, sl] += grads[1]
                dv_ref[sub_c, sl] += grads[2]
                if with_prev:
                    dk_ref[sub_p, sl] += grads[3]
                    dv_ref[sub_p, sl] += grads[4]

        _for_residues(r, one)

    operands = (qn, kn, vn, kn, vn) if with_prev else (qn, kn, vn)
    return pl.pallas_call(
        body, name="attn_bwd_%d" % r, grid=(D_ATTN // lanes, SEQ // rows),
        in_specs=[cur, cur, cur] + ([prev, prev] if with_prev else []) + [cur, cur], out_specs=[cur, whole, whole],
        out_shape=[jax.ShapeDtypeStruct((SEQ, D_ATTN), F32)] * 3,
        compiler_params=_params(("parallel", "arbitrary")),
    )(*operands, do, dl)


CONV_COLS = 256
XBC_BLOCK0 = 4096 // CONV_COLS


def _shift_rows(x, s):
    n = x.shape[0]
    t = lax.broadcasted_iota(jnp.int32, x.shape, 0)
    if s >= 0:
        return jnp.where(t >= s, pltpu.roll(x, s, 0), 0.0)
    return jnp.where(t < n + s, pltpu.roll(x, n + s, 0), 0.0)


def _conv_pre(x, w_ref, b_ref):
    pre = b_ref[...] + w_ref[3:4, :] * x
    for k in range(3):
        pre = pre + w_ref[k:k + 1, :] * _shift_rows(x, 3 - k)
    return pre


def _conv_fwd(proj, conv_w, conv_b):
    cols = conv_w.shape[1]

    def body(x_ref, w_ref, b_ref, o_ref):
        pre = _conv_pre(x_ref[...], w_ref, b_ref)
        o_ref[...] = pre * jax.nn.sigmoid(pre)

    blk = pl.BlockSpec((SEQ, CONV_COLS), lambda j: (0, j))
    return pl.pallas_call(
        body, name="conv_fwd", grid=(cols // CONV_COLS,),
        in_specs=[pl.BlockSpec((SEQ, CONV_COLS), lambda j: (0, XBC_BLOCK0 + j)),
                  pl.BlockSpec((4, CONV_COLS), lambda j: (0, j)), pl.BlockSpec((1, CONV_COLS), lambda j: (0, j))],
        out_specs=blk, out_shape=jax.ShapeDtypeStruct((SEQ, cols), F32),
        compiler_params=_params(("parallel",)),
    )(proj, conv_w, conv_b)


def _conv_bwd(proj, conv_w, conv_b, dy):
    cols = conv_w.shape[1]

    def body(x_ref, w_ref, b_ref, dy_ref, dx_ref, dw_ref, db_ref):
        x = x_ref[...]
        pre = _conv_pre(x, w_ref, b_ref)
        sg = jax.nn.sigmoid(pre)
        dpre = dy_ref[...] * (sg * (1.0 + pre * (1.0 - sg)))
        db_ref[...] = jnp.sum(dpre, axis=0, keepdims=True)
        dx = w_ref[3:4, :] * dpre
        dw_ref[3:4, :] = jnp.sum(dpre * x, axis=0, keepdims=True)
        for k in range(3):
            dx = dx + w_ref[k:k + 1, :] * _shift_rows(dpre, k - 3)
            dw_ref[k:k + 1, :] = jnp.sum(dpre * _shift_rows(x, 3 - k), axis=0, keepdims=True)
        dw_ref[4:8, :] = jnp.zeros((4, CONV_COLS), F32)
        dx_ref[...] = dx.astype(dx_ref.dtype)

    blk = pl.BlockSpec((SEQ, CONV_COLS), lambda j: (0, j))
    return pl.pallas_call(
        body, name="conv_bwd", grid=(cols // CONV_COLS,),
        in_specs=[pl.BlockSpec((SEQ, CONV_COLS), lambda j: (0, XBC_BLOCK0 + j)),
                  pl.BlockSpec((4, CONV_COLS), lambda j: (0, j)), pl.BlockSpec((1, CONV_COLS), lambda j: (0, j)), blk],
        out_specs=[blk, pl.BlockSpec((8, CONV_COLS), lambda j: (0, j)), pl.BlockSpec((1, CONV_COLS), lambda j: (0, j))],
        out_shape=[jax.ShapeDtypeStruct((SEQ, cols), BF16), jax.ShapeDtypeStruct((8, cols), F32),
                   jax.ShapeDtypeStruct((1, cols), F32)],
        compiler_params=_params(("parallel",)),
    )(proj, conv_w, conv_b, dy)


HEADS_PER_GROUP = 4


def _ssd_chunk(x0, x1, x2, x3, bm, cm, dtr, bias, alog, dsk, h0, h1, h2, h3):
    xs, hs = (x0, x1, x2, x3), (h0, h1, h2, h3)
    row = lax.broadcasted_iota(jnp.int32, (CHUNK, CHUNK), 0)
    col = lax.broadcasted_iota(jnp.int32, (CHUNK, CHUNK), 1)
    causal = row >= col
    tril = causal.astype(F32)
    z = dtr + bias
    dt = jnp.maximum(z, 0.0) + jnp.log(1.0 + jnp.exp(-jnp.abs(z)))
    a = -jnp.exp(alog)
    acs = _fdot(tril, dt * a, NN)
    acs_t, dt_t = acs.T, dt.T
    cb = _bdot(cm, bm, NT)
    lane = lax.broadcasted_iota(jnp.int32, (1, CHUNK), 1)
    sub = lax.broadcasted_iota(jnp.int32, (CHUNK, 1), 0)
    ys, hn = [], []
    for j in range(HEADS_PER_GROUP):
        on_lane, on_sub = (lane == j).astype(F32), (sub == j).astype(F32)
        acs_c = jnp.sum(acs * on_lane, axis=1, keepdims=True)
        dt_c = jnp.sum(dt * on_lane, axis=1, keepdims=True)
        acs_r = jnp.sum(acs_t * on_sub, axis=0, keepdims=True)
        dt_r = jnp.sum(dt_t * on_sub, axis=0, keepdims=True)
        acs_last = jnp.sum(acs_c * (sub == CHUNK - 1).astype(F32), axis=0, keepdims=True)
        d_j = jnp.sum(dsk * on_lane, axis=1, keepdims=True)
        decay = jnp.exp(jnp.where(causal, acs_c - acs_r, NEG))
        w = cb * decay * dt_r
        y_diag = _bdot(w, xs[j], NN)
        y_off = _bdot(cm, hs[j], NT) * jnp.exp(acs_c)
        ys.append(y_diag + y_off + d_j * xs[j])
        state = _bdot(xs[j] * (jnp.exp(acs_last - acs_c) * dt_c), bm, TN)
        hn.append(hs[j] * jnp.exp(acs_last) + state)
    return (*ys, *hn)


def _ssd_specs(reverse):
    n_chunks = SEQ // CHUNK
    c_of = (lambda c: n_chunks - 1 - c) if reverse else (lambda c: c)
    x_spec = pl.BlockSpec((CHUNK, 256), lambda g, c: (c_of(c), g))
    b_spec = pl.BlockSpec((CHUNK, N_STATE), lambda g, c: (c_of(c), 8 + g))
    c_spec = pl.BlockSpec((CHUNK, N_STATE), lambda g, c: (c_of(c), 12 + g))
    dt_spec = pl.BlockSpec((CHUNK, 128), lambda g, c: (c_of(c), g))
    vec_spec = pl.BlockSpec((1, 128), lambda g, c: (0, g))
    h_spec = pl.BlockSpec((1, 1, HEADS_PER_GROUP, HEAD, N_STATE), lambda g, c: (c_of(c), g, 0, 0, 0))
    return x_spec, b_spec, c_spec, dt_spec, vec_spec, h_spec


def _ssd_fwd(xbc, dt_raw, bias, alog, dsk):
    x_spec, b_spec, c_spec, dt_spec, vec_spec, h_spec = _ssd_specs(False)

    def body(x_ref, b_ref, c_ref, dt_ref, bias_ref, alog_ref, dsk_ref, y_ref, hin_ref, h_scr):
        @pl.when(pl.program_id(1) == 0)
        def _():
            h_scr[...] = jnp.zeros_like(h_scr)

        hs = [h_scr[j] for j in range(HEADS_PER_GROUP)]
        for j in range(HEADS_PER_GROUP):
            hin_ref[0, 0, j] = hs[j]
        xs = [x_ref[:, pl.ds(j * HEAD, HEAD)] for j in range(HEADS_PER_GROUP)]
        res = _ssd_chunk(*xs, b_ref[...], c_ref[...], dt_ref[...], bias_ref[...], alog_ref[...], dsk_ref[...], *hs)
        for j in range(HEADS_PER_GROUP):
            y_ref[:, pl.ds(j * HEAD, HEAD)] = res[j]
            h_scr[j] = res[HEADS_PER_GROUP + j]

    return pl.pallas_call(
        body, name="ssd_fwd", grid=(N_GROUPS, SEQ // CHUNK),
        in_specs=[x_spec, b_spec, c_spec, dt_spec, vec_spec, vec_spec, vec_spec],
        out_specs=[x_spec, h_spec],
        out_shape=[jax.ShapeDtypeStruct((SEQ, D_SSM), F32),
                   jax.ShapeDtypeStruct((SEQ // CHUNK, N_GROUPS, HEADS_PER_GROUP, HEAD, N_STATE), F32)],
        scratch_shapes=[pltpu.VMEM((HEADS_PER_GROUP, HEAD, N_STATE), F32)],
        compiler_params=_params(("parallel", "arbitrary")),
    )(xbc, xbc, xbc, dt_raw, bias, alog, dsk)


def _ssd_bwd(xbc, dt_raw, bias, alog, dsk, h_in, dy):
    x_spec, b_spec, c_spec, dt_spec, vec_spec, h_spec = _ssd_specs(True)
    dxbc_x = pl.BlockSpec((CHUNK, 256), x_spec.index_map)

    def body(x_ref, b_ref, c_ref, dt_ref, bias_ref, alog_ref, dsk_ref, hin_ref, dy_ref,
             dx_ref, db_ref, dc_ref, ddt_ref, dbias_ref, dalog_ref, ddsk_ref, dh_scr):
        first = pl.program_id(1) == 0

        @pl.when(first)
        def _():
            dh_scr[...] = jnp.zeros_like(dh_scr)

        xs = [x_ref[:, pl.ds(j * HEAD, HEAD)] for j in range(HEADS_PER_GROUP)]
        hs = [hin_ref[0, 0, j] for j in range(HEADS_PER_GROUP)]
        cts = [dy_ref[:, pl.ds(j * HEAD, HEAD)] for j in range(HEADS_PER_GROUP)] + [dh_scr[j] for j in range(HEADS_PER_GROUP)]
        _, pullback = jax.vjp(_ssd_chunk, *xs, b_ref[...], c_ref[...], dt_ref[...], bias_ref[...], alog_ref[...],
                              dsk_ref[...], *hs)
        g = pullback(tuple(cts))
        for j in range(HEADS_PER_GROUP):
            dx_ref[:, pl.ds(j * HEAD, HEAD)] = g[j]
            dh_scr[j] = g[10 + j]
        db_ref[...] = g[4]
        dc_ref[...] = g[5]
        ddt_ref[...] = g[6].astype(ddt_ref.dtype)
        for o_ref, val in ((dbias_ref, g[7]), (dalog_ref, g[8]), (ddsk_ref, g[9])):
            @pl.when(first)
            def _(o_ref=o_ref, val=val):
                o_ref[...] = val

            @pl.when(jnp.logical_not(first))
            def _(o_ref=o_ref, val=val):
                o_ref[...] += val

    n_chunks = SEQ // CHUNK
    out_b = pl.BlockSpec((CHUNK, N_STATE), lambda g, c: (n_chunks - 1 - c, g))
    res = pl.pallas_call(
        body, name="ssd_bwd", grid=(N_GROUPS, n_chunks),
        in_specs=[x_spec, b_spec, c_spec, dt_spec, vec_spec, vec_spec, vec_spec, h_spec, x_spec],
        out_specs=[dxbc_x, out_b, out_b, dt_spec, vec_spec, vec_spec, vec_spec],
        out_shape=[jax.ShapeDtypeStruct((SEQ, D_SSM), F32), jax.ShapeDtypeStruct((SEQ, N_GROUPS * N_STATE), F32),
                   jax.ShapeDtypeStruct((SEQ, N_GROUPS * N_STATE), F32), jax.ShapeDtypeStruct((SEQ, DT_PAD), BF16),
                   jax.ShapeDtypeStruct((1, DT_PAD), F32), jax.ShapeDtypeStruct((1, DT_PAD), F32),
                   jax.ShapeDtypeStruct((1, DT_PAD), F32)],
        scratch_shapes=[pltpu.VMEM((HEADS_PER_GROUP, HEAD, N_STATE), F32)],
        compiler_params=_params(("parallel", "arbitrary")),
    )(xbc, xbc, xbc, dt_raw, bias, alog, dsk, h_in, dy)
    return res


CROSS_HEAD = 128
CROSS_ROWS = 512


def _cross_head(q, k, v, gq, gk):
    qn = _rms(q, gq) * (CROSS_HEAD ** -0.5)
    kn = _rms(k, gk)
    s = _bdot(qn, kn, NT)
    p = jnp.exp(s - jnp.max(s, axis=-1, keepdims=True))
    return _bdot(p, v, NN) / jnp.sum(p, axis=-1, keepdims=True)


def _cross_specs():
    q_spec = pl.BlockSpec((CROSS_ROWS, CROSS_HEAD), lambda h, i: (i, h))
    k_spec = pl.BlockSpec((N_MEM, CROSS_HEAD), lambda h, i: (0, h))
    v_spec = pl.BlockSpec((N_MEM, CROSS_HEAD), lambda h, i: (0, 4 + h))
    g_spec = pl.BlockSpec((1, CROSS_HEAD), lambda h, i: (0, 0))
    return q_spec, k_spec, v_spec, g_spec


def _cross_fwd(qc, kv, gq, gk):
    q_spec, k_spec, v_spec, g_spec = _cross_specs()

    def body(q_ref, k_ref, v_ref, gq_ref, gk_ref, o_ref):
        o_ref[...] = _cross_head(q_ref[...], k_ref[...], v_ref[...], gq_ref[...], gk_ref[...]).astype(o_ref.dtype)

    return pl.pallas_call(
        body, name="cross_fwd", grid=(4, SEQ // CROSS_ROWS),
        in_specs=[q_spec, k_spec, v_spec, g_spec, g_spec], out_specs=q_spec,
        out_shape=jax.ShapeDtypeStruct((SEQ, D_CROSS), BF16),
        compiler_params=_params(("parallel", "parallel")),
    )(qc, kv, kv, gq, gk)


def _cross_bwd(qc, kv, gq, gk, do):
    q_spec, k_spec, v_spec, g_spec = _cross_specs()

    def body(q_ref, k_ref, v_ref, gq_ref, gk_ref, do_ref, dq_ref, dk_ref, dv_ref, dgq_ref, dgk_ref):
        _, pullback = jax.vjp(_cross_head, q_ref[...], k_ref[...], v_ref[...], gq_ref[...], gk_ref[...])
        dq, dk, dv, dgq, dgk = pullback(do_ref[...].astype(F32))
        dq_ref[...] = dq.astype(dq_ref.dtype)
        row0 = pl.program_id(1) == 0
        all0 = jnp.logical_and(row0, pl.program_id(0) == 0)
        for o_ref, val, init in ((dk_ref, dk, row0), (dv_ref, dv, row0), (dgq_ref, dgq, all0), (dgk_ref, dgk, all0)):
            @pl.when(init)
            def _(o_ref=o_ref, val=val):
                o_ref[...] = val

            @pl.when(jnp.logical_not(init))
            def _(o_ref=o_ref, val=val):
                o_ref[...] += val

    return pl.pallas_call(
        body, name="cross_bwd", grid=(4, SEQ // CROSS_ROWS),
        in_specs=[q_spec, k_spec, v_spec, g_spec, g_spec, q_spec],
        out_specs=[q_spec, k_spec, k_spec, g_spec, g_spec],
        out_shape=[jax.ShapeDtypeStruct((SEQ, D_CROSS), BF16), jax.ShapeDtypeStruct((N_MEM, D_CROSS), F32),
                   jax.ShapeDtypeStruct((N_MEM, D_CROSS), F32), jax.ShapeDtypeStruct((1, CROSS_HEAD), F32),
                   jax.ShapeDtypeStruct((1, CROSS_HEAD), F32)],
        compiler_params=_params(("arbitrary", "arbitrary")),
    )(qc, kv, kv, gq, gk, do)


def _loss_head(y, target):
    tr = 256

    def body(y_ref, t_ref, dy_ref, dyb_ref, loss_ref):
        err = y_ref[...] - t_ref[...]
        dy = err * (1.0 / D_MODEL)
        dy_ref[...] = dy
        dyb_ref[...] = dy.astype(BF16)
        part = jnp.sum(jnp.sum(err * err, axis=1, keepdims=True), axis=0, keepdims=True) * (0.5 / D_MODEL)
        part = jnp.broadcast_to(part, (1, 128))

        @pl.when(pl.program_id(0) == 0)
        def _():
            loss_ref[...] = part

        @pl.when(pl.program_id(0) != 0)
        def _():
            loss_ref[...] += part

    blk = pl.BlockSpec((tr, D_MODEL), lambda i: (i, 0))
    return pl.pallas_call(
        body, name="loss_head", grid=(SEQ // tr,),
        in_specs=[blk, blk], out_specs=[blk, blk, pl.BlockSpec((1, 128), lambda i: (0, 0))],
        out_shape=[jax.ShapeDtypeStruct((SEQ, D_MODEL), F32), jax.ShapeDtypeStruct((SEQ, D_MODEL), BF16),
                   jax.ShapeDtypeStruct((1, 128), F32)],
        compiler_params=_params(("arbitrary",)),
    )(y, target)


def _pad_heads(v):
    return jnp.pad(v.reshape(N_GROUPS, HEADS_PER_GROUP), ((0, 0), (0, 128 - HEADS_PER_GROUP))).reshape(1, DT_PAD)


def _unpad_heads(v):
    return v.reshape(v.shape[0], N_GROUPS, 128)[:, :, :HEADS_PER_GROUP].reshape(v.shape[0], N_DT)


def _rope_tables(positions):
    half = ROT // 2
    inv_freq = ROPE_THETA ** (-2.0 * jnp.arange(half, dtype=F32) / ROT)
    ang = positions.reshape(SEQ, 1).astype(F32) * inv_freq
    cos, sin = jnp.cos(ang), jnp.sin(ang)
    ones, zeros = jnp.ones((SEQ, HEAD - ROT), F32), jnp.zeros((SEQ, HEAD - ROT), F32)
    cos_h = jnp.concatenate([cos, cos, ones], axis=1)
    sin_h = jnp.concatenate([-sin, sin, zeros], axis=1)
    return jnp.tile(cos_h, (1, 2)), jnp.tile(sin_h, (1, 2))


def _add_res(acc, res):
    return (acc + res,)


def _local_step(x, mem, positions, target, p, w, more_weights=None, grads=None):
    grads = {} if grads is None else grads
    w = dict(w)
    cos, sin = _rope_tables(positions)
    gq2, gk2 = jnp.tile(p["g_q"], (1, 2)), jnp.tile(p["g_k"], (1, 2))
    bias, alog, dsk = _pad_heads(p["dt_bias"]), _pad_heads(p["a_log"]), _pad_heads(p["d_skip"])
    norm_out = [(D_MODEL, BF16, D_MODEL, 0, False)]

    h = _rowwise(_norm_fn, [_full(x)], [_full(p["g_mix"])], norm_out, name="norm_in")[0]
    proj = _matmul(h, w["w_main"], mode="nn", name="in_proj", outs=[F32])
    dt_raw = _matmul(h, w["w_dt"], mode="nn", name="dt_proj", outs=[F32])
    qk_rows = [(proj, 128, 0, True), (proj, 128, 8, True), (proj, 128, 16, True), _full(cos), _full(sin)]
    qk_vecs = [_full(gq2), _full(gk2)]
    qn, kn, vn = _rowwise(_qk_fn, qk_rows, qk_vecs, [(D_ATTN, F32, 128, 0, True)] * 3, name="qk_prep", groups=8)
    branches = [_attention_fwd(qn, kn, vn, b) for b in range(3)]
    merge_rows = [_full(o) for o, _ in branches] + [_full(lse) for _, lse in branches]
    attn = _rowwise(_merge_fn, merge_rows, [_full(p["g_attn_out"])], [(D_ATTN, BF16, D_ATTN, 0, False)], name="attn_merge")[0]
    xbc = _conv_fwd(proj, p["conv_w"], p["conv_b"])
    y_ssd, h_in = _ssd_fwd(xbc, dt_raw, bias, alog, dsk)
    gate_rows = [(y_ssd, 256, 0, True), (proj, 256, 12, True)]
    gate_vecs = [(p["g_ssm_out"], 256, 0, True)]
    ssm = _rowwise(_gate_fn, gate_rows, gate_vecs, [(D_SSM, BF16, 256, 0, True)], name="ssm_gate", groups=4)[0]
    mix = jnp.concatenate([attn, ssm], axis=1)
    if more_weights is not None:
        w.update(more_weights("mixer_done", mix))
    x1 = _matmul(mix, w["w_out"], mode="nn", name="out_proj", outs=[F32], extra=(x,), epilogue=_add_res)
    hc = _rowwise(_norm_fn, [_full(x1)], [_full(p["g_cross"])], norm_out, name="norm_cross")[0]
    memh = _rowwise(_norm_fn, [_full(mem)], [_full(p["g_mem"])], norm_out, name="norm_mem", n_rows=N_MEM)[0]
    qc = _matmul(hc, w["w_cq"], mode="nn", name="cq_proj", outs=[F32])
    kv = _matmul(memh, w["w_ckv"], mode="nn", name="ckv_proj", outs=[F32])
    oc = _cross_fwd(qc, kv, p["g_cq"], p["g_ck"])
    x2 = _matmul(oc, w["w_co"], mode="nn", name="co_proj", outs=[F32], extra=(x1,), epilogue=_add_res)
    hm = _rowwise(_norm_fn, [_full(x2)], [_full(p["g_mlp"])], norm_out, name="norm_mlp")[0]
    if more_weights is not None:
        w.update(more_weights("cross_done", hm))
    u, act = _matmul(hm, w["w_up"], mode="nn", name="up_proj", outs=[F32, BF16],
                     epilogue=lambda acc: (acc, jnp.square(jnp.maximum(acc, 0.0))))
    x3 = _matmul(act, w["w_down"], mode="nn", name="down_proj", outs=[F32], extra=(x2,), epilogue=_add_res)
    dy, dyb, loss = _loss_head(x3, target)

    grads["w_down"] = _matmul(act, dyb, mode="tn", name="dw_down", outs=[BF16])
    du = _matmul(dyb, w["w_down"], mode="nt", name="d_act", outs=[BF16], extra=(u,),
                 epilogue=lambda acc, uu: (acc * (2.0 * jnp.maximum(uu, 0.0)),))
    grads["w_up"] = _matmul(hm, du, mode="tn", name="dw_up", outs=[BF16], col_shards=4)
    dhm = _matmul(du, w["w_up"], mode="nt", name="d_hm", outs=[F32])
    dx2, grads["g_mlp"] = _rowwise_vjp(
        _norm_fn, [_full(x2)], [_full(p["g_mlp"])], [[_full(dhm)]],
        [(0, D_MODEL, F32, D_MODEL, 0, False, _full(dy))], [(0, D_MODEL, D_MODEL, 0, False)], name="norm_mlp_bwd")
    grads["w_co"] = _matmul(oc, dx2, mode="tn", name="dw_co", outs=[BF16], col_shards=4)
    doc = _matmul(dx2, w["w_co"], mode="nt", name="d_oc", outs=[BF16])
    dqc, dkc, dvc, grads["g_cq"], grads["g_ck"] = _cross_bwd(qc, kv, p["g_cq"], p["g_ck"], doc)
    grads["w_cq"] = _matmul(hc, dqc, mode="tn", name="dw_cq", outs=[BF16])
    dhc = _matmul(dqc, w["w_cq"], mode="nt", name="d_hc", outs=[F32])
    dkv = jnp.concatenate([dkc, dvc], axis=1)
    grads["w_ckv"] = _matmul(memh, dkv, mode="tn", name="dw_ckv", outs=[BF16])
    dmemh = _matmul(dkv, w["w_ckv"], mode="nt", name="d_memh", outs=[F32])
    grads["g_mem"] = _rowwise_vjp(_norm_fn, [_full(mem)], [_full(p["g_mem"])], [[_full(dmemh)]], [],
                                  [(0, D_MODEL, D_MODEL, 0, False)], name="norm_mem_bwd", n_rows=N_MEM)[0]
    dx1, grads["g_cross"] = _rowwise_vjp(
        _norm_fn, [_full(x1)], [_full(p["g_cross"])], [[_full(dhc)]],
        [(0, D_MODEL, F32, D_MODEL, 0, False, _full(dx2))], [(0, D_MODEL, D_MODEL, 0, False)], name="norm_cross_bwd")
    grads["w_out"] = _matmul(mix, dx1, mode="tn", name="dw_out", outs=[BF16])
    dmix = _matmul(dx1, w["w_out"], mode="nt", name="d_mix", outs=[F32])
    merge_grads = [(i, D_ATTN, F32, D_ATTN, 0, False, None) for i in range(6)]
    *dol, grads["g_attn_out"] = _rowwise_vjp(
        _merge_fn, merge_rows, [_full(p["g_attn_out"])], [[(dmix, D_ATTN, 0, False)]],
        merge_grads, [(0, D_ATTN, D_ATTN, 0, False)], name="attn_merge_bwd")
    dqkv = [_attention_bwd(qn, kn, vn, dol[b], dol[3 + b], b) for b in range(3)]
    qk_cts = [[(dqkv[b][i], 128, 0, True) for b in range(3)] for i in range(3)]
    dq, dk, dv, dgq2, dgk2 = _rowwise_vjp(
        _qk_fn, qk_rows, qk_vecs, qk_cts, [(i, D_ATTN, BF16, 128, 0, True, None) for i in range(3)],
        [(0, 128, 128, 0, False), (1, 128, 128, 0, False)], name="qk_prep_bwd", groups=8)
    grads["g_q"] = dgq2[:, :HEAD] + dgq2[:, HEAD:]
    grads["g_k"] = dgk2[:, :HEAD] + dgk2[:, HEAD:]
    dy_ssd, dz, grads["g_ssm_out"] = _rowwise_vjp(
        _gate_fn, gate_rows, gate_vecs, [[(dmix, 256, 4, True)]],
        [(0, D_SSM, F32, 256, 0, True, None), (1, D_SSM, BF16, 256, 0, True, None)],
        [(0, D_SSM, 256, 0, True)], name="ssm_gate_bwd", groups=4)
    dxs, db, dc, ddt, dbias, dalog, ddsk = _ssd_bwd(xbc, dt_raw, bias, alog, dsk, h_in, dy_ssd)
    grads["dt_bias"], grads["a_log"], grads["d_skip"] = _unpad_heads(dbias), _unpad_heads(dalog), _unpad_heads(ddsk)
    dxbc_raw, dconv_w, grads["conv_b"] = _conv_bwd(proj, p["conv_w"], p["conv_b"], jnp.concatenate([dxs, db, dc], axis=1))
    grads["conv_w"] = dconv_w[:4]
    dproj = jnp.concatenate([dq, dk, dv, dz, dxbc_raw], axis=1)
    grads["w_main"] = _matmul(h, dproj, mode="tn", name="dw_main", outs=[BF16])
    grads["w_dt"] = _matmul(h, ddt, mode="tn", name="dw_dt", outs=[BF16])
    dh = _matmul(dproj, w["w_main"], mode="nt", name="d_h_main", outs=[F32])
    dh = _matmul(ddt, w["w_dt"], mode="nt", name="d_h_dt", outs=[F32], extra=(dh,), epilogue=_add_res)
    grad_x, grads["g_mix"] = _rowwise_vjp(
        _norm_fn, [_full(x)], [_full(p["g_mix"])], [[_full(dh)]],
        [(0, D_MODEL, F32, D_MODEL, 0, False, _full(dx1))], [(0, D_MODEL, D_MODEL, 0, False)], name="norm_in_bwd")
    return loss, grad_x, grads


MATRICES = ("w_in", "w_out", "w_cq", "w_ckv", "w_co", "w_up", "w_down")
ROW_SHARDED = ("w_out", "w_cq", "w_ckv", "w_down")
N_CHIPS = 4
ANY = pl.BlockSpec(memory_space=pl.ANY)


def _place():
    return lax.axis_index("x"), lax.axis_index("y"), lax.axis_index("c")


def _other_chips(x, y):
    return [(1 - x, y), (x, 1 - y), (1 - x, 1 - y)]


def _remote(src, dst, send_sem, recv_sem, device):
    return pltpu.make_async_remote_copy(src_ref=src, dst_ref=dst, send_sem=send_sem, recv_sem=recv_sem,
                                        device_id=device, device_id_type=MESH)


def _gathered_shape(name, shard):
    rows, cols = shard.shape
    if name == "w_in":
        return (N_CHIPS, rows, cols)
    return (N_CHIPS * rows, cols) if name in ROW_SHARDED else (rows, N_CHIPS * cols)


def _shard_window(name, ref, rows, cols, chip, half):
    r0, nr = (0, rows) if half is None else (half * (rows // 2), rows // 2)
    if name == "w_in":
        return ref.at[chip, pl.ds(r0, nr), :]
    if name in ROW_SHARDED:
        return ref.at[pl.ds(chip * rows + r0, nr), :]
    return ref.at[pl.ds(r0, nr), pl.ds(pl.multiple_of(chip * cols, 128), cols)]


def _cast_into_gathered(w, name, chip):
    rows, cols = w.shape
    tr = _tile(rows, ROW_TILE)

    def body(chip_ref, w_ref, o_ref):
        o_ref[...] = w_ref[...].astype(BF16)

    if name == "w_in":
        out_spec = pl.BlockSpec((None, tr, cols), lambda i, chip_ref: (chip_ref[0], i, 0))
    elif name in ROW_SHARDED:
        out_spec = pl.BlockSpec((tr, cols), lambda i, chip_ref: (chip_ref[0] * (rows // tr) + i, 0))
    else:
        out_spec = pl.BlockSpec((tr, cols), lambda i, chip_ref: (i, chip_ref[0]))
    grid_spec = pltpu.PrefetchScalarGridSpec(
        num_scalar_prefetch=1, grid=(rows // tr,),
        in_specs=[pl.BlockSpec((tr, cols), lambda i, chip_ref: (i, 0))], out_specs=out_spec)
    return pl.pallas_call(body, name="cast_" + name, grid_spec=grid_spec,
                          out_shape=jax.ShapeDtypeStruct(_gathered_shape(name, w), BF16),
                          compiler_params=_params(("parallel",)))(chip.reshape(1).astype(jnp.int32), w)


HBM = pl.BlockSpec(memory_space=pltpu.HBM)
SEM = pl.BlockSpec(memory_space=pltpu.SEMAPHORE)
EFFECT = pltpu.SideEffectType.DATAFLOW_SIDE_EFFECTING


def _split_start(name, bufs, plan, counts):
    n, n_g = len(bufs), len(counts)

    def body(*refs):
        ins, sems, token = refs[:n], refs[n:n + 2 * n_g], refs[-1]
        for g, copies in enumerate(plan(ins)):
            for i, (src, dst, device, _) in enumerate(copies):
                _remote(src, dst, sems[2 * g].at[i], sems[2 * g + 1].at[i], device).start()
        token[...] = jnp.zeros_like(token)

    sem_shapes = [pltpu.SemaphoreType.DMA((cnt,)) for cnt in counts for _ in range(2)]
    res = pl.pallas_call(
        body, name=name,
        out_shape=(*sem_shapes, *[pltpu.HBM(b.shape, b.dtype) for b in bufs], jax.ShapeDtypeStruct((8, 128), F32)),
        in_specs=(HBM,) * n, out_specs=(*(SEM,) * (2 * n_g), *(HBM,) * n, pl.BlockSpec(memory_space=pltpu.VMEM)),
        input_output_aliases={i: 2 * n_g + i for i in range(n)},
        compiler_params=pltpu.CompilerParams(has_side_effects=EFFECT),
    )(*[pltpu.with_memory_space_constraint(b, pltpu.HBM) for b in bufs])
    sems = [(res[2 * g], res[2 * g + 1]) for g in range(n_g)]
    return sems, list(res[2 * n_g:2 * n_g + n]), res[-1]


def _split_wait(name, bufs, sems, plan, after):
    n = len(bufs)

    def body(*refs):
        ins, send, recv = refs[:n], refs[n], refs[n + 1]
        (copies,) = plan(ins)
        for i, (src, _, device, landing) in enumerate(copies):
            cp = _remote(src, landing, send.at[i], recv.at[i], device)
            cp.wait_send()
            cp.wait_recv()

    res = pl.pallas_call(
        body, name=name, out_shape=tuple(pltpu.HBM(b.shape, b.dtype) for b in bufs),
        in_specs=(*(HBM,) * n, SEM, SEM, ANY), out_specs=(HBM,) * n,
        input_output_aliases={i: i for i in range(n)},
        compiler_params=pltpu.CompilerParams(has_side_effects=EFFECT),
    )(*bufs, sems[0], sems[1], after)
    return list(res)


def _ici_plan(names, shard_shapes):
    def plan(refs):
        x, y, c = _place()
        copies = []
        for ref, name in zip(refs, names):
            win = _shard_window(name, ref, *shard_shapes[name], 2 * x + y, c)
            for px, py in _other_chips(x, y):
                copies.append((win, win, (px, py, c), _shard_window(name, ref, *shard_shapes[name], 2 * px + py, c)))
        return [copies]
    return plan


def _pass_on_plan(names, shard_shapes):
    def plan(refs):
        x, y, c = _place()
        copies = []
        for ref, name in zip(refs, names):
            for px, py in _other_chips(x, y):
                win = _shard_window(name, ref, *shard_shapes[name], 2 * px + py, c)
                copies.append((win, win, (x, y, 1 - c), _shard_window(name, ref, *shard_shapes[name], 2 * px + py, 1 - c)))
        return [copies]
    return plan


def _scatter_plan(n_pairs):
    def plan(refs):
        x, y, c = _place()
        copies = []
        for src, dst in zip(refs[:n_pairs], refs[n_pairs:]):
            for k, (px, py) in enumerate(_other_chips(x, y)):
                copies.append((src.at[2 * px + py], dst.at[k], (px, py, c), dst.at[k]))
        return [copies]
    return plan


def _sibling_swap(arrs, name):
    n = len(arrs)

    def body(*refs):
        ins, outs, send, recv = refs[:n], refs[n:2 * n], refs[2 * n], refs[2 * n + 1]
        x, y, c = _place()
        cps = [_remote(ins[w].at[:, 1 - c], outs[w], send.at[w], recv.at[w], (x, y, 1 - c)) for w in range(n)]
        for cp in cps:
            cp.start()
        for cp in cps:
            cp.wait()

    return pl.pallas_call(
        body, name=name, in_specs=[ANY] * n, out_specs=[ANY] * n,
        out_shape=[jax.ShapeDtypeStruct((a.shape[0],) + a.shape[2:], a.dtype) for a in arrs],
        scratch_shapes=[pltpu.SemaphoreType.DMA((n,))] * 2,
    )(*arrs)


def _sibling_share(arrs):
    n = len(arrs)

    def body(*refs):
        ins, outs, send, recv = refs[:n], refs[n:2 * n], refs[2 * n], refs[2 * n + 1]
        x, y, c = _place()
        cps = [_remote(ins[w], outs[w], send.at[w], recv.at[w], (x, y, 1 - c)) for w in range(n)]
        for cp in cps:
            cp.start()
        for cp in cps:
            cp.wait()

    return pl.pallas_call(
        body, name="grad_sibling_share", in_specs=[ANY] * n, out_specs=[ANY] * n,
        out_shape=[jax.ShapeDtypeStruct(a.shape, a.dtype) for a in arrs],
        scratch_shapes=[pltpu.SemaphoreType.DMA((n,))] * 2,
    )(*arrs)


def _small_allreduce(buf, name):
    rows = buf.shape[0]

    def body(x_ref, out_ref, all_ref, send_sems, recv_sems, local_sem):
        x, y, c = _place()
        me, sibling, chips = (x, y, c), (x, y, 1 - c), _other_chips(x, y)

        def block(px, py, pc):
            return all_ref.at[pl.ds((4 * px + 2 * py + pc) * rows, rows), :]

        def copy(k, blk, to, src=None):
            return _remote(block(*blk) if src is None else src, block(*blk), send_sems.at[k], recv_sems.at[k], to)

        own = pltpu.make_async_copy(x_ref, block(*me), local_sem)
        own.start()
        first = [copy(0, me, sibling, src=x_ref)] + [copy(1 + j, me, (*chip, c), src=x_ref) for j, chip in enumerate(chips)]
        for cp in first:
            cp.start()
        passed = [copy(4 + j, (*chip, c), sibling) for j, chip in enumerate(chips)]
        for j, chip in enumerate(chips):
            copy(1 + j, (*chip, c), me).wait_recv()
            passed[j].start()
        copy(0, sibling, me).wait_recv()
        for j, chip in enumerate(chips):
            copy(4 + j, (*chip, 1 - c), me).wait_recv()
        for cp in first + passed:
            cp.wait_send()
        own.wait()
        acc = all_ref[pl.ds(0, rows), :]
        for d in range(1, 8):
            acc = acc + all_ref[pl.ds(d * rows, rows), :]
        out_ref[...] = acc

    vmem = pl.BlockSpec(memory_space=pltpu.VMEM)
    return pl.pallas_call(
        body, name=name, in_specs=[vmem], out_specs=vmem,
        out_shape=jax.ShapeDtypeStruct(buf.shape, F32),
        scratch_shapes=[pltpu.VMEM((8 * rows, 128), F32), pltpu.SemaphoreType.DMA((7,)), pltpu.SemaphoreType.DMA((7,)),
                        pltpu.SemaphoreType.DMA],
    )(buf)


ROW_TILE = 256


def _add_halves(arr, recv, c, name):
    _, _, hr, cols = arr.shape
    tr = _tile(hr, ROW_TILE)

    def body(c_ref, a_ref, r_ref, o_ref):
        o_ref[...] = (a_ref[...].astype(F32) + r_ref[...].astype(F32)).astype(o_ref.dtype)

    piece = pl.BlockSpec((None, tr, cols), lambda j, i, c_ref: (j, i, 0))
    grid_spec = pltpu.PrefetchScalarGridSpec(
        num_scalar_prefetch=1, grid=(N_CHIPS, hr // tr),
        in_specs=[pl.BlockSpec((None, None, tr, cols), lambda j, i, c_ref: (j, c_ref[0], i, 0)), piece], out_specs=piece)
    return pl.pallas_call(body, name=name, grid_spec=grid_spec, out_shape=jax.ShapeDtypeStruct(recv.shape, BF16),
                          compiler_params=_params(("parallel", "parallel")))(c.reshape(1).astype(jnp.int32), arr, recv)


def _flip_slot(d):
    return jnp.where(d == 1, 1, jnp.where(d == 3, 2, 0))


def _sum_chips(p, q, chip, name):
    _, hr, cols = p.shape
    tr = _tile(hr, ROW_TILE)

    def body(chip_ref, p_ref, q_ref, o_ref):
        j = pl.program_id(1)
        term = jnp.where(j == chip_ref[0], p_ref[...].astype(F32), q_ref[...].astype(F32))

        @pl.when(j == 0)
        def _():
            o_ref[...] = term

        @pl.when(j != 0)
        def _():
            o_ref[...] += term

    grid_spec = pltpu.PrefetchScalarGridSpec(
        num_scalar_prefetch=1, grid=(hr // tr, N_CHIPS),
        in_specs=[pl.BlockSpec((None, tr, cols), lambda i, j, chip_ref: (chip_ref[0], i, 0)),
                  pl.BlockSpec((None, tr, cols), lambda i, j, chip_ref: (_flip_slot(j ^ chip_ref[0]), i, 0))],
        out_specs=pl.BlockSpec((tr, cols), lambda i, j, chip_ref: (i, 0)))
    return pl.pallas_call(body, name=name, grid_spec=grid_spec, out_shape=jax.ShapeDtypeStruct((hr, cols), F32),
                          compiler_params=_params(("parallel", "arbitrary")))(chip.reshape(1).astype(jnp.int32), p, q)


def _adamw_halves(w, g_own, g_other, m, v, c, name):
    rows, cols = w.shape
    tr = _tile(rows // 2, ROW_TILE)
    per_half = rows // 2 // tr

    def body(c_ref, w_ref, own_ref, other_ref, m_ref, v_ref, g_ref, d_ref, nm_ref, nv_ref):
        mine = (pl.program_id(0) // per_half) == c_ref[0]
        g_ = jnp.where(mine, own_ref[...], other_ref[...])
        g_ref[...] = g_
        d_ref[...], nm_ref[...], nv_ref[...] = _adamw_math(w_ref[...], g_, m_ref[...], v_ref[...])

    blk = pl.BlockSpec((tr, cols), lambda i, c_ref: (i, 0))
    half = pl.BlockSpec((tr, cols), lambda i, c_ref: (i % per_half, 0))
    grid_spec = pltpu.PrefetchScalarGridSpec(num_scalar_prefetch=1, grid=(rows // tr,),
                                             in_specs=[blk, half, half, blk, blk], out_specs=[blk] * 4)
    return pl.pallas_call(body, name=name, grid_spec=grid_spec, out_shape=[jax.ShapeDtypeStruct(w.shape, F32)] * 4,
                          compiler_params=_params(("parallel",)))(c.reshape(1).astype(jnp.int32), w, g_own, g_other, m, v)


def _adamw_math(w, g, m, v):
    m_new = ADAM_B1 * m + (1.0 - ADAM_B1) * g
    v_new = ADAM_B2 * v + (1.0 - ADAM_B2) * (g * g)
    m_hat = m_new / (1.0 - ADAM_B1 ** ADAM_STEP)
    v_hat = v_new / (1.0 - ADAM_B2 ** ADAM_STEP)
    return -ADAM_LR * (m_hat / (jnp.sqrt(v_hat) + ADAM_EPS) + ADAM_WD * w), m_new, v_new


def _adamw(w, g, m, v, name):
    rows, cols = w.shape
    tr = _tile(rows, ROW_TILE)

    def body(w_ref, g_ref, m_ref, v_ref, d_ref, nm_ref, nv_ref):
        d_ref[...], nm_ref[...], nv_ref[...] = _adamw_math(w_ref[...], g_ref[...], m_ref[...], v_ref[...])

    blk = pl.BlockSpec((tr, cols), lambda i: (i, 0))
    return pl.pallas_call(body, name=name, grid=(rows // tr,), in_specs=[blk] * 4, out_specs=[blk] * 3,
                          out_shape=[jax.ShapeDtypeStruct(w.shape, F32)] * 3, compiler_params=_params(("parallel",)))(w, g, m, v)


VECTORS = ("g_mix", "g_q", "g_k", "g_attn_out", "conv_b", "dt_bias", "a_log", "d_skip", "g_ssm_out", "g_cross", "g_mem",
           "g_cq", "g_ck", "g_mlp")
WEIGHTS = ("g_mix", "w_in", "g_q", "g_k", "g_attn_out", "conv_w", "conv_b", "dt_bias", "a_log", "d_skip", "g_ssm_out", "w_out",
           "g_cross", "g_mem", "w_cq", "w_ckv", "g_cq", "g_ck", "w_co", "g_mlp", "w_up", "w_down")


def _pack(parts):
    flat = jnp.concatenate([t.reshape(-1) for t in parts])
    total = -(-flat.shape[0] // 1024) * 1024
    return jnp.pad(flat, (0, total - flat.shape[0])).reshape(total // 128, 128)


def _unpack(buf, shapes):
    flat, out, pos = buf.reshape(-1), [], 0
    for shape in shapes:
        size = math.prod(shape)
        out.append(flat[pos:pos + size].reshape(shape))
        pos += size
    return out


def kernel(x, mem, positions, g_mix, w_in, g_q, g_k, g_attn_out, conv_w, conv_b, dt_bias, a_log, d_skip, g_ssm_out, w_out, g_cross, g_mem, w_cq, w_ckv, g_cq, g_ck, w_co, g_mlp, w_up, w_down, loss_target, m_g_mix, m_w_in, m_g_q, m_g_k, m_g_attn_out, m_conv_w, m_conv_b, m_dt_bias, m_a_log, m_d_skip, m_g_ssm_out, m_w_out, m_g_cross, m_g_mem, m_w_cq, m_w_ckv, m_g_cq, m_g_ck, m_w_co, m_g_mlp, m_w_up, m_w_down, v_g_mix, v_w_in, v_g_q, v_g_k, v_g_attn_out, v_conv_w, v_conv_b, v_dt_bias, v_a_log, v_d_skip, v_g_ssm_out, v_w_out, v_g_cross, v_g_mem, v_w_cq, v_w_ckv, v_g_cq, v_g_ck, v_w_co, v_g_mlp, v_w_up, v_w_down):
    args = dict(locals())
    weights = {n: args[n][0] for n in WEIGHTS}
    mom_m = {n: args["m_" + n][0] for n in WEIGHTS}
    mom_v = {n: args["v_" + n][0] for n in WEIGHTS}
    x_idx, y_idx, c_idx = _place()
    chip = 2 * x_idx + y_idx

    shapes = {n: weights[n].shape for n in MATRICES}
    first, mid, late = ("w_in",), ("w_out", "w_cq", "w_ckv", "w_co"), ("w_up", "w_down")
    bufs = [_cast_into_gathered(weights[n], n, chip) for n in first + mid + late]
    plan = lambda refs: _ici_plan(first, shapes)(refs[:1]) + _ici_plan(mid + late, shapes)(refs[1:])
    ici_sems, bufs, token = _split_start("gather_ici_start", bufs, plan, [3, 18])
    w_in_buf = _split_wait("gather_ici_wait_w_in", bufs[:1], ici_sems[0], _ici_plan(first, shapes), token)
    pass_sems, w_in_buf, token = _split_start("gather_pass_start_w_in", w_in_buf, _pass_on_plan(first, shapes), [3])
    w_in_buf = _split_wait("gather_pass_wait_w_in", w_in_buf, pass_sems[0], _pass_on_plan(first, shapes), token)
    w_in_full = jnp.transpose(w_in_buf[0], (1, 0, 2)).reshape(D_MODEL, D_MAIN + N_DT)
    full = {"w_main": w_in_full[:, :D_MAIN],
            "w_dt": jnp.pad(w_in_full[:, D_MAIN:].reshape(D_MODEL, N_GROUPS, HEADS_PER_GROUP),
                            ((0, 0), (0, 0), (0, 128 - HEADS_PER_GROUP))).reshape(D_MODEL, DT_PAD)}
    in_flight = {}

    def more_weights(stage, after):
        if stage == "mixer_done":
            rest = _split_wait("gather_ici_wait_rest", bufs[1:], ici_sems[1], _ici_plan(mid + late, shapes), after)
            plan = lambda refs: _pass_on_plan(mid, shapes)(refs[:4]) + _pass_on_plan(late, shapes)(refs[4:])
            sems, rest, token = _split_start("gather_pass_start_rest", rest, plan, [12, 6])
            in_flight["late"] = (rest[4:], sems[1])
            return dict(zip(mid, _split_wait("gather_pass_wait_mid", rest[:4], sems[0], _pass_on_plan(mid, shapes), token)))
        late_bufs, sems = in_flight.pop("late")
        return dict(zip(late, _split_wait("gather_pass_wait_late", late_bufs, sems, _pass_on_plan(late, shapes), after)))

    conv_parts = _small_allreduce(_pack([jnp.zeros((N_CHIPS, 4, 512), F32).at[chip].set(0.5 * weights["conv_w"])]),
                                  "gather_conv_taps")
    conv_full = _unpack(conv_parts, [(N_CHIPS, 4, 512)])[0].transpose(1, 0, 2).reshape(4, 4 * 512)
    params = {n: weights[n].reshape(1, -1) for n in VECTORS}
    params["conv_w"] = conv_full

    groups = (("w_down",), ("w_up",), ("w_co", "w_cq", "w_ckv", "w_out"), ("w_in",))
    scattered = []

    class GradStore(dict):
        def __setitem__(self, name, value):
            super().__setitem__(name, value)
            if "w_main" in self and "w_dt" in self and "w_in" not in self:
                gw_in = jnp.concatenate([self["w_main"], _unpad_heads(self["w_dt"])], axis=1)
                self["w_in"] = gw_in.reshape(D_MODEL, N_CHIPS, gw_in.shape[1] // N_CHIPS).transpose(1, 0, 2)
            for group in groups:
                if name in group and all(n in self for n in group):
                    pieces = [self[n].reshape(N_CHIPS, 2, shapes[n][0] // 2, shapes[n][1]) for n in group]
                    from_sibling = _sibling_swap(pieces, "grad_swap_" + group[0])
                    sums = [_add_halves(a, r, c_idx, "add_halves_" + n) for n, a, r in zip(group, pieces, from_sibling)]
                    landing = [lax.empty((3,) + s.shape[1:], BF16) for s in sums]
                    sems, thru, _ = _split_start("grad_scatter_start_" + group[0], sums + landing, _scatter_plan(len(sums)),
                                                 [3 * len(sums)])
                    scattered.append((group, sems[0], thru))

    loss, grad_x, grads = _local_step(x[0], mem[0], positions[0], loss_target[0], params, full, more_weights, GradStore())

    halves = {}
    for group, sems, thru in scattered:
        thru = _split_wait("grad_scatter_wait_" + group[0], thru, sems, _scatter_plan(len(group)), grad_x)
        for i, n in enumerate(group):
            halves[n] = _sum_chips(thru[i], thru[len(group) + i], chip, "sum_chips_" + n)
    other_halves = dict(zip(MATRICES, _sibling_share([halves[n] for n in MATRICES])))
    out_g, out_d, out_m, out_v = {}, {}, {}, {}
    for n in MATRICES:
        out_g[n], out_d[n], out_m[n], out_v[n] = _adamw_halves(weights[n], halves[n], other_halves[n], mom_m[n], mom_v[n],
                                                               c_idx, "adamw_" + n)

    small = [grads[n] for n in VECTORS] + [grads["conv_w"]]
    summed = _unpack(_small_allreduce(_pack(small), "allreduce_vectors"), [t.shape for t in small])
    g_small = dict(zip(VECTORS, summed[:-1]))
    g_small["conv_w"] = lax.dynamic_slice_in_dim(summed[-1], chip * 512, 512, axis=1)
    names = VECTORS + ("conv_w",)
    shapes = [weights[n].shape for n in names]
    packed = [_pack([src[n] for n in names]) for src in (weights, g_small, mom_m, mom_v)]
    small_out = [_unpack(t, shapes) for t in _adamw(*packed, "adamw_small")]
    for i, n in enumerate(names):
        out_g[n] = g_small[n].reshape(shapes[i])
        out_d[n], out_m[n], out_v[n] = small_out[0][i], small_out[1][i], small_out[2][i]

    total_loss = lax.psum(loss[0, 0], ("x", "y", "c"))
    outs = [total_loss, grad_x[None]]
    for group in (out_g, out_d, out_m, out_v):
        outs += [group[n][None] for n in WEIGHTS]
    return tuple(outs)
```

```python
import functools
import math

import jax
import jax.numpy as jnp
from jax import lax
from jax.experimental import pallas as pl
from jax.experimental.pallas import tpu as pltpu

F32 = jnp.float32
BF16 = jnp.bfloat16

SEQ = 2048
D_MODEL = 2048
HEAD = 64
D_ATTN = 1024
D_SSM = 1024
N_GROUPS = 4
N_STATE = 128
CHUNK = 128
ATT_BLK = 128
N_MEM = 256
D_CROSS = 512
D_FF = 8192
D_MAIN = 6144
N_DT = 16
DT_PAD = 512
ROT = 16
ROPE_THETA = 500000.0
EPS = 1e-6
NEG = -1e30
BRANCH_BLOCKS = (16, 4, 1)
DILATIONS = (1, 4, 16)

ADAM_LR, ADAM_B1, ADAM_B2, ADAM_EPS, ADAM_WD, ADAM_STEP = 0.001, 0.9, 0.999, 1e-08, 0.01, 10

VMEM_LIMIT = 56 * 1024 * 1024
MESH = pl.DeviceIdType.MESH


def _params(sem, **kw):
    return pltpu.CompilerParams(dimension_semantics=sem, vmem_limit_bytes=VMEM_LIMIT, **kw)


def _bdot(a, b, dims):
    return lax.dot_general(a.astype(BF16), b.astype(BF16), (dims, ((), ())), preferred_element_type=F32)


def _fdot(a, b, dims):
    return lax.dot_general(a, b, (dims, ((), ())), preferred_element_type=F32, precision=lax.Precision.HIGHEST)


NN = ((1,), (0,))
NT = ((1,), (1,))
TN = ((0,), (0,))


def _tile(n, want):
    t = min(n, want)
    while n % t:
        t //= 2
    return t


def _matmul(a, b, *, mode, name, outs, extra=(), epilogue=None, col_shards=1, after=(), tm=512, tn=1024, tk=1024):
    if mode == "nn":
        (m, k), n = a.shape, b.shape[1]
    elif mode == "nt":
        (m, k), n = a.shape, b.shape[0]
    else:
        (k, m), n = a.shape, b.shape[1]
    tm, tn, tk = _tile(m, tm), _tile(n // col_shards, tn), _tile(k, tk)
    nk = k // tk
    per_shard = n // col_shards // tn
    dims = {"nn": NN, "nt": NT, "tn": TN}[mode]
    a_spec = pl.BlockSpec((tk, tm), lambda i, j, kk: (kk, i)) if mode == "tn" else pl.BlockSpec((tm, tk), lambda i, j, kk: (i, kk))
    b_spec = pl.BlockSpec((tn, tk), lambda i, j, kk: (j, kk)) if mode == "nt" else pl.BlockSpec((tk, tn), lambda i, j, kk: (kk, j))
    o_spec = pl.BlockSpec((tm, tn), lambda i, j, kk: (i, j))
    n_extra, n_out, n_after = len(extra), len(outs), len(after)

    def body(a_ref, b_ref, *rest):
        extra_refs, out_refs, acc_ref = rest[:n_extra], rest[n_extra + n_after:n_extra + n_after + n_out], rest[-1]
        kk = pl.program_id(2)

        @pl.when(kk == 0)
        def _():
            acc_ref[...] = jnp.zeros_like(acc_ref)

        acc_ref[...] += _bdot(a_ref[...], b_ref[...], dims)

        @pl.when(kk == nk - 1)
        def _():
            acc = acc_ref[...]
            res = (acc,) if epilogue is None else epilogue(acc, *[e[...] for e in extra_refs])
            for o_ref, r in zip(out_refs, res):
                o_ref[...] = r.astype(o_ref.dtype)

    if col_shards == 1:
        out_specs, out_dims = [o_spec] * n_out, (m, n)
    else:
        sharded = pl.BlockSpec((None, tm, tn), lambda i, j, kk: (j // per_shard, i, j % per_shard))
        out_specs, out_dims = [sharded] * n_out, (col_shards, m, n // col_shards)
    res = pl.pallas_call(
        body, name=name, grid=(m // tm, n // tn, nk),
        in_specs=[a_spec, b_spec] + [o_spec] * n_extra + [pl.BlockSpec(memory_space=pl.ANY)] * n_after,
        out_specs=out_specs,
        out_shape=[jax.ShapeDtypeStruct(out_dims, dt) for dt in outs],
        scratch_shapes=[pltpu.VMEM((tm, tn), F32)],
        compiler_params=_params(("parallel", "parallel", "arbitrary")),
    )(a, b, *extra, *after)
    return res[0] if n_out == 1 else res


def _row_spec(tr, bw, cb, per_group):
    return pl.BlockSpec((tr, bw), (lambda g, i: (i, cb + g)) if per_group else (lambda g, i: (i, cb)))


def _vec_spec(bw, cb, per_group):
    return pl.BlockSpec((1, bw), (lambda g, i: (0, cb + g)) if per_group else (lambda g, i: (0, cb)))


def _rowwise(fn, rows, vecs, outs, *, name, n_rows=SEQ, tr=256, groups=1):
    n_r, n_v = len(rows), len(vecs)

    def body(*refs):
        vals = [r[...].astype(F32) for r in refs[:n_r + n_v]]
        res = fn(*vals)
        for o_ref, r in zip(refs[n_r + n_v:], res):
            o_ref[...] = r.astype(o_ref.dtype)

    res = pl.pallas_call(
        body, name=name, grid=(groups, n_rows // tr),
        in_specs=[_row_spec(tr, bw, cb, pg) for _, bw, cb, pg in rows] + [_vec_spec(bw, cb, pg) for _, bw, cb, pg in vecs],
        out_specs=[_row_spec(tr, bw, cb, pg) for _, _, bw, cb, pg in outs],
        out_shape=[jax.ShapeDtypeStruct((n_rows, w), dt) for w, dt, _, _, _ in outs],
        compiler_params=_params(("parallel", "parallel")),
    )(*[r[0] for r in rows], *[v[0] for v in vecs])
    return res


def _rowwise_vjp(fn, rows, vecs, cts, row_grads, vec_grads, *, name, n_rows=SEQ, tr=256, groups=1):
    n_r, n_v = len(rows), len(vecs)
    ct_ops = [op for group in cts for op in group]
    ct_sizes = [len(group) for group in cts]
    res_ops = [g[6] for g in row_grads if g[6] is not None]
    n_ct, n_res, n_rg = len(ct_ops), len(res_ops), len(row_grads)

    def body(*refs):
        vals = [r[...].astype(F32) for r in refs[:n_r + n_v]]
        pos = n_r + n_v
        ct_vals = []
        for size in ct_sizes:
            acc = refs[pos][...].astype(F32)
            for t in range(1, size):
                acc = acc + refs[pos + t][...].astype(F32)
            ct_vals.append(acc)
            pos += size
        res_refs = refs[pos:pos + n_res]
        out_refs = refs[pos + n_res:]
        _, pullback = jax.vjp(fn, *vals)
        grads = pullback(tuple(ct_vals))
        r_i = 0
        for o_ref, g in zip(out_refs[:n_rg], row_grads):
            val = grads[g[0]]
            if g[6] is not None:
                val = val + res_refs[r_i][...].astype(F32)
                r_i += 1
            o_ref[...] = val.astype(o_ref.dtype)
        first = (pl.program_id(1) == 0)
        for o_ref, g in zip(out_refs[n_rg:], vec_grads):
            val = jnp.sum(grads[n_r + g[0]], axis=0, keepdims=True)
            init = first if g[4] else jnp.logical_and(first, pl.program_id(0) == 0)

            @pl.when(init)
            def _(o_ref=o_ref, val=val):
                o_ref[...] = val

            @pl.when(jnp.logical_not(init))
            def _(o_ref=o_ref, val=val):
                o_ref[...] += val

    in_specs = [_row_spec(tr, bw, cb, pg) for _, bw, cb, pg in rows] + [_vec_spec(bw, cb, pg) for _, bw, cb, pg in vecs]
    in_specs += [_row_spec(tr, bw, cb, pg) for _, bw, cb, pg in ct_ops + res_ops]
    out_specs = [_row_spec(tr, g[3], g[4], g[5]) for g in row_grads] + [_vec_spec(g[2], g[3], g[4]) for g in vec_grads]
    out_shape = [jax.ShapeDtypeStruct((n_rows, g[1]), g[2]) for g in row_grads]
    out_shape += [jax.ShapeDtypeStruct((1, g[1]), F32) for g in vec_grads]
    return pl.pallas_call(
        body, name=name, grid=(groups, n_rows // tr),
        in_specs=in_specs, out_specs=out_specs, out_shape=out_shape,
        compiler_params=_params(("arbitrary", "arbitrary")),
    )(*[r[0] for r in rows], *[v[0] for v in vecs], *[c[0] for c in ct_ops], *[r[0] for r in res_ops])


def _full(arr, width=None):
    return (arr, arr.shape[1] if width is None else width, 0, False)


def _make_xor(sh):
    def raw(x):
        n = x.shape[-1]
        lane = lax.broadcasted_iota(jnp.int32, x.shape, x.ndim - 1)
        up = pltpu.roll(x, n - sh, x.ndim - 1)
        down = pltpu.roll(x, sh, x.ndim - 1)
        return jnp.where((lane & sh) == 0, up, down)

    f = jax.custom_vjp(raw)
    f.defvjp(lambda x: (raw(x), None), lambda _, ct: (raw(ct),))
    return f


_XOR = {sh: _make_xor(sh) for sh in (1, 2, 4, 8, 16, 32)}


def _head_sum(x):
    for sh in (1, 2, 4, 8, 16, 32):
        x = x + _XOR[sh](x)
    return x


def _rms(x, g):
    return x * lax.rsqrt(jnp.mean(x * x, axis=-1, keepdims=True) + EPS) * g


def _head_rms_rope(x, g, cos, sin, scale):
    y = x * lax.rsqrt(_head_sum(x * x) * (1.0 / HEAD) + EPS) * g
    return (y * cos + _XOR[8](y) * sin) * scale


def _qk_fn(q, k, v, cos, sin, gq, gk):
    return (_head_rms_rope(q, gq, cos, sin, HEAD ** -0.5), _head_rms_rope(k, gk, cos, sin, 1.0), v)


def _norm_fn(x, g):
    return (_rms(x, g),)


def _merge_fn(o0, o1, o2, l0, l1, l2, g):
    m = jnp.maximum(jnp.maximum(l0, l1), l2)
    e0, e1, e2 = jnp.exp(l0 - m), jnp.exp(l1 - m), jnp.exp(l2 - m)
    mix = (e0 * o0 + e1 * o1 + e2 * o2) / (e0 + e1 + e2)
    return (_rms(mix, g),)


def _gate_fn(y, z, g):
    return (_rms(y * (z * jax.nn.sigmoid(z)), g),)


def _attn_pair(q, kc, vc, kp=None, vp=None, has_prev=None):
    qi = lax.broadcasted_iota(jnp.int32, (ATT_BLK, ATT_BLK), 0)
    kj = lax.broadcasted_iota(jnp.int32, (ATT_BLK, ATT_BLK), 1)
    lane = lax.broadcasted_iota(jnp.int32, (1, 2 * HEAD), 1)
    o, lse = 0.0, 0.0
    for h in range(2):
        pick = ((lane >= h * HEAD) & (lane < (h + 1) * HEAD)).astype(F32)
        qh = q * pick
        s_c = jnp.where(qi >= kj, _bdot(qh, kc, NT), NEG)
        m = jnp.max(s_c, axis=-1, keepdims=True)
        if kp is not None:
            s_p = jnp.where(jnp.logical_and(kj >= qi, has_prev), _bdot(qh, kp, NT), NEG)
            m = jnp.maximum(m, jnp.max(s_p, axis=-1, keepdims=True))
        p_c = jnp.exp(s_c - m)
        den = jnp.sum(p_c, axis=-1, keepdims=True)
        acc = _bdot(p_c, vc, NN)
        if kp is not None:
            p_p = jnp.exp(s_p - m)
            den = den + jnp.sum(p_p, axis=-1, keepdims=True)
            acc = acc + _bdot(p_p, vp, NN)
        o = o + pick * (acc / den)
        lse = lse + pick * (m + jnp.log(den))
    return o, lse


def _attn_config(b):
    r = DILATIONS[b]
    return r, ATT_BLK * r, (512 if r == 1 else 128), BRANCH_BLOCKS[b] > 1


def _for_residues(r, fn):
    if r <= 4:
        for rho in range(r):
            fn(rho)
    else:
        def step(t, carry):
            for u in range(4):
                fn(4 * t + u)
            return carry

        lax.fori_loop(0, r // 4, step, 0)


def _strided_rows(start, r):
    if r > 1:
        return pl.ds(start, ATT_BLK, stride=r)
    return pl.ds(start if isinstance(start, int) else pl.multiple_of(start, ATT_BLK), ATT_BLK)


def _attention_fwd(qn, kn, vn, b):
    r, rows, lanes, with_prev = _attn_config(b)
    cur = pl.BlockSpec((rows, lanes), lambda g, n: (n, g))
    prev = pl.BlockSpec((rows, lanes), lambda g, n: (jnp.maximum(n - 1, 0), g))

    def body(*refs):
        ins, (o_ref, l_ref) = refs[:-2], refs[-2:]
        has_prev = pl.program_id(1) > 0

        def one(rho):
            sub = _strided_rows(rho, r)
            for pair in range(lanes // 128):
                sl = pl.ds(pair * 128, 128)
                args = [ref[sub, sl] for ref in ins] + ([has_prev] if with_prev else [])
                o_ref[sub, sl], l_ref[sub, sl] = _attn_pair(*args)

        _for_residues(r, one)

    operands = (qn, kn, vn, kn, vn) if with_prev else (qn, kn, vn)
    return pl.pallas_call(
        body, name="attn_fwd_%d" % r, grid=(D_ATTN // lanes, SEQ // rows),
        in_specs=[cur, cur, cur] + ([prev, prev] if with_prev else []), out_specs=[cur, cur],
        out_shape=[jax.ShapeDtypeStruct((SEQ, D_ATTN), F32)] * 2,
        compiler_params=_params(("parallel", "parallel")),
    )(*operands)


def _attention_bwd(qn, kn, vn, do, dl, b):
    r, rows, lanes, with_prev = _attn_config(b)
    cur = pl.BlockSpec((rows, lanes), lambda g, n: (n, g))
    prev = pl.BlockSpec((rows, lanes), lambda g, n: (jnp.maximum(n - 1, 0), g))
    whole = pl.BlockSpec((SEQ, lanes), lambda g, n: (0, g))
    n_in = 5 if with_prev else 3

    def body(*refs):
        ins, (do_ref, dl_ref, dq_ref, dk_ref, dv_ref) = refs[:n_in], refs[n_in:]
        n = pl.program_id(1)
        has_prev = n > 0

        @pl.when(n == 0)
        def _():
            dk_ref[...] = jnp.zeros_like(dk_ref)
            dv_ref[...] = jnp.zeros_like(dv_ref)

        def one(rho):
            sub = _strided_rows(rho, r)
            sub_c = _strided_rows(n * rows + rho, r)
            sub_p = _strided_rows(jnp.maximum(n - 1, 0) * rows + rho, r)
            for pair in range(lanes // 128):
                sl = pl.ds(pair * 128, 128)
                vals = [ref[sub, sl] for ref in ins]
                if with_prev:
                    _, pullback = jax.vjp(lambda *a: _attn_pair(*a, has_prev), *vals)
                else:
                    _, pullback = jax.vjp(_attn_pair, *vals)
                grads = pullback((do_ref[sub, sl], dl_ref[sub, sl]))
                dq_ref[sub, sl] = grads[0]
                dk_ref[sub_c, sl] += grads[1]
                dv_ref[sub_c, sl] += grads[2]
                if with_prev:
                    dk_ref[sub_p, sl] += grads[3]
                    dv_ref[sub_p, sl] += grads[4]

        _for_residues(r, one)

    operands = (qn, kn, vn, kn, vn) if with_prev else (qn, kn, vn)
    return pl.pallas_call(
        body, name="attn_bwd_%d" % r, grid=(D_ATTN // lanes, SEQ // rows),
        in_specs=[cur, cur, cur] + ([prev, prev] if with_prev else []) + [cur, cur], out_specs=[cur, whole, whole],
        out_shape=[jax.ShapeDtypeStruct((SEQ, D_ATTN), F32)] * 3,
        compiler_params=_params(("parallel", "arbitrary")),
    )(*operands, do, dl)


CONV_COLS = 256
XBC_BLOCK0 = 4096 // CONV_COLS


def _shift_rows(x, s):
    n = x.shape[0]
    t = lax.broadcasted_iota(jnp.int32, x.shape, 0)
    if s >= 0:
        return jnp.where(t >= s, pltpu.roll(x, s, 0), 0.0)
    return jnp.where(t < n + s, pltpu.roll(x, n + s, 0), 0.0)


def _conv_pre(x, w_ref, b_ref):
    pre = b_ref[...] + w_ref[3:4, :] * x
    for k in range(3):
        pre = pre + w_ref[k:k + 1, :] * _shift_rows(x, 3 - k)
    return pre


def _conv_fwd(proj, conv_w, conv_b):
    cols = conv_w.shape[1]

    def body(x_ref, w_ref, b_ref, o_ref):
        pre = _conv_pre(x_ref[...], w_ref, b_ref)
        o_ref[...] = pre * jax.nn.sigmoid(pre)

    blk = pl.BlockSpec((SEQ, CONV_COLS), lambda j: (0, j))
    return pl.pallas_call(
        body, name="conv_fwd", grid=(cols // CONV_COLS,),
        in_specs=[pl.BlockSpec((SEQ, CONV_COLS), lambda j: (0, XBC_BLOCK0 + j)),
                  pl.BlockSpec((4, CONV_COLS), lambda j: (0, j)), pl.BlockSpec((1, CONV_COLS), lambda j: (0, j))],
        out_specs=blk, out_shape=jax.ShapeDtypeStruct((SEQ, cols), F32),
        compiler_params=_params(("parallel",)),
    )(proj, conv_w, conv_b)


def _conv_bwd(proj, conv_w, conv_b, dy):
    cols = conv_w.shape[1]

    def body(x_ref, w_ref, b_ref, dy_ref, dx_ref, dw_ref, db_ref):
        x = x_ref[...]
        pre = _conv_pre(x, w_ref, b_ref)
        sg = jax.nn.sigmoid(pre)
        dpre = dy_ref[...] * (sg * (1.0 + pre * (1.0 - sg)))
        db_ref[...] = jnp.sum(dpre, axis=0, keepdims=True)
        dx = w_ref[3:4, :] * dpre
        dw_ref[3:4, :] = jnp.sum(dpre * x, axis=0, keepdims=True)
        for k in range(3):
            dx = dx + w_ref[k:k + 1, :] * _shift_rows(dpre, k - 3)
            dw_ref[k:k + 1, :] = jnp.sum(dpre * _shift_rows(x, 3 - k), axis=0, keepdims=True)
        dw_ref[4:8, :] = jnp.zeros((4, CONV_COLS), F32)
        dx_ref[...] = dx.astype(dx_ref.dtype)

    blk = pl.BlockSpec((SEQ, CONV_COLS), lambda j: (0, j))
    return pl.pallas_call(
        body, name="conv_bwd", grid=(cols // CONV_COLS,),
        in_specs=[pl.BlockSpec((SEQ, CONV_COLS), lambda j: (0, XBC_BLOCK0 + j)),
                  pl.BlockSpec((4, CONV_COLS), lambda j: (0, j)), pl.BlockSpec((1, CONV_COLS), lambda j: (0, j)), blk],
        out_specs=[blk, pl.BlockSpec((8, CONV_COLS), lambda j: (0, j)), pl.BlockSpec((1, CONV_COLS), lambda j: (0, j))],
        out_shape=[jax.ShapeDtypeStruct((SEQ, cols), BF16), jax.ShapeDtypeStruct((8, cols), F32),
                   jax.ShapeDtypeStruct((1, cols), F32)],
        compiler_params=_params(("parallel",)),
    )(proj, conv_w, conv_b, dy)


HEADS_PER_GROUP = 4


def _ssd_chunk(x0, x1, x2, x3, bm, cm, dtr, bias, alog, dsk, h0, h1, h2, h3):
    xs, hs = (x0, x1, x2, x3), (h0, h1, h2, h3)
    row = lax.broadcasted_iota(jnp.int32, (CHUNK, CHUNK), 0)
    col = lax.broadcasted_iota(jnp.int32, (CHUNK, CHUNK), 1)
    causal = row >= col
    tril = causal.astype(F32)
    z = dtr + bias
    dt = jnp.maximum(z, 0.0) + jnp.log(1.0 + jnp.exp(-jnp.abs(z)))
    a = -jnp.exp(alog)
    acs = _fdot(tril, dt * a, NN)
    acs_t, dt_t = acs.T, dt.T
    cb = _bdot(cm, bm, NT)
    lane = lax.broadcasted_iota(jnp.int32, (1, CHUNK), 1)
    sub = lax.broadcasted_iota(jnp.int32, (CHUNK, 1), 0)
    ys, hn = [], []
    for j in range(HEADS_PER_GROUP):
        on_lane, on_sub = (lane == j).astype(F32), (sub == j).astype(F32)
        acs_c = jnp.sum(acs * on_lane, axis=1, keepdims=True)
        dt_c = jnp.sum(dt * on_lane, axis=1, keepdims=True)
        acs_r = jnp.sum(acs_t * on_sub, axis=0, keepdims=True)
        dt_r = jnp.sum(dt_t * on_sub, axis=0, keepdims=True)
        acs_last = jnp.sum(acs_c * (sub == CHUNK - 1).astype(F32), axis=0, keepdims=True)
        d_j = jnp.sum(dsk * on_lane, axis=1, keepdims=True)
        decay = jnp.exp(jnp.where(causal, acs_c - acs_r, NEG))
        w = cb * decay * dt_r
        y_diag = _bdot(w, xs[j], NN)
        y_off = _bdot(cm, hs[j], NT) * jnp.exp(acs_c)
        ys.append(y_diag + y_off + d_j * xs[j])
        state = _bdot(xs[j] * (jnp.exp(acs_last - acs_c) * dt_c), bm, TN)
        hn.append(hs[j] * jnp.exp(acs_last) + state)
    return (*ys, *hn)


def _ssd_specs(reverse):
    n_chunks = SEQ // CHUNK
    c_of = (lambda c: n_chunks - 1 - c) if reverse else (lambda c: c)
    x_spec = pl.BlockSpec((CHUNK, 256), lambda g, c: (c_of(c), g))
    b_spec = pl.BlockSpec((CHUNK, N_STATE), lambda g, c: (c_of(c), 8 + g))
    c_spec = pl.BlockSpec((CHUNK, N_STATE), lambda g, c: (c_of(c), 12 + g))
    dt_spec = pl.BlockSpec((CHUNK, 128), lambda g, c: (c_of(c), g))
    vec_spec = pl.BlockSpec((1, 128), lambda g, c: (0, g))
    h_spec = pl.BlockSpec((1, 1, HEADS_PER_GROUP, HEAD, N_STATE), lambda g, c: (c_of(c), g, 0, 0, 0))
    return x_spec, b_spec, c_spec, dt_spec, vec_spec, h_spec


def _ssd_fwd(xbc, dt_raw, bias, alog, dsk):
    x_spec, b_spec, c_spec, dt_spec, vec_spec, h_spec = _ssd_specs(False)

    def body(x_ref, b_ref, c_ref, dt_ref, bias_ref, alog_ref, dsk_ref, y_ref, hin_ref, h_scr):
        @pl.when(pl.program_id(1) == 0)
        def _():
            h_scr[...] = jnp.zeros_like(h_scr)

        hs = [h_scr[j] for j in range(HEADS_PER_GROUP)]
        for j in range(HEADS_PER_GROUP):
            hin_ref[0, 0, j] = hs[j]
        xs = [x_ref[:, pl.ds(j * HEAD, HEAD)] for j in range(HEADS_PER_GROUP)]
        res = _ssd_chunk(*xs, b_ref[...], c_ref[...], dt_ref[...], bias_ref[...], alog_ref[...], dsk_ref[...], *hs)
        for j in range(HEADS_PER_GROUP):
            y_ref[:, pl.ds(j * HEAD, HEAD)] = res[j]
            h_scr[j] = res[HEADS_PER_GROUP + j]

    return pl.pallas_call(
        body, name="ssd_fwd", grid=(N_GROUPS, SEQ // CHUNK),
        in_specs=[x_spec, b_spec, c_spec, dt_spec, vec_spec, vec_spec, vec_spec],
        out_specs=[x_spec, h_spec],
        out_shape=[jax.ShapeDtypeStruct((SEQ, D_SSM), F32),
                   jax.ShapeDtypeStruct((SEQ // CHUNK, N_GROUPS, HEADS_PER_GROUP, HEAD, N_STATE), F32)],
        scratch_shapes=[pltpu.VMEM((HEADS_PER_GROUP, HEAD, N_STATE), F32)],
        compiler_params=_params(("parallel", "arbitrary")),
    )(xbc, xbc, xbc, dt_raw, bias, alog, dsk)


def _ssd_bwd(xbc, dt_raw, bias, alog, dsk, h_in, dy):
    x_spec, b_spec, c_spec, dt_spec, vec_spec, h_spec = _ssd_specs(True)
    dxbc_x = pl.BlockSpec((CHUNK, 256), x_spec.index_map)

    def body(x_ref, b_ref, c_ref, dt_ref, bias_ref, alog_ref, dsk_ref, hin_ref, dy_ref,
             dx_ref, db_ref, dc_ref, ddt_ref, dbias_ref, dalog_ref, ddsk_ref, dh_scr):
        first = pl.program_id(1) == 0

        @pl.when(first)
        def _():
            dh_scr[...] = jnp.zeros_like(dh_scr)

        xs = [x_ref[:, pl.ds(j * HEAD, HEAD)] for j in range(HEADS_PER_GROUP)]
        hs = [hin_ref[0, 0, j] for j in range(HEADS_PER_GROUP)]
        cts = [dy_ref[:, pl.ds(j * HEAD, HEAD)] for j in range(HEADS_PER_GROUP)] + [dh_scr[j] for j in range(HEADS_PER_GROUP)]
        _, pullback = jax.vjp(_ssd_chunk, *xs, b_ref[...], c_ref[...], dt_ref[...], bias_ref[...], alog_ref[...],
                              dsk_ref[...], *hs)
        g = pullback(tuple(cts))
        for j in range(HEADS_PER_GROUP):
            dx_ref[:, pl.ds(j * HEAD, HEAD)] = g[j]
            dh_scr[j] = g[10 + j]
        db_ref[...] = g[4]
        dc_ref[...] = g[5]
        ddt_ref[...] = g[6].astype(ddt_ref.dtype)
        for o_ref, val in ((dbias_ref, g[7]), (dalog_ref, g[8]), (ddsk_ref, g[9])):
            @pl.when(first)
            def _(o_ref=o_ref, val=val):
                o_ref[...] = val

            @pl.when(jnp.logical_not(first))
            def _(o_ref=o_ref, val=val):
                o_ref[...] += val

    n_chunks = SEQ // CHUNK
    out_b = pl.BlockSpec((CHUNK, N_STATE), lambda g, c: (n_chunks - 1 - c, g))
    res = pl.pallas_call(
        body, name="ssd_bwd", grid=(N_GROUPS, n_chunks),
        in_specs=[x_spec, b_spec, c_spec, dt_spec, vec_spec, vec_spec, vec_spec, h_spec, x_spec],
        out_specs=[dxbc_x, out_b, out_b, dt_spec, vec_spec, vec_spec, vec_spec],
        out_shape=[jax.ShapeDtypeStruct((SEQ, D_SSM), F32), jax.ShapeDtypeStruct((SEQ, N_GROUPS * N_STATE), F32),
                   jax.ShapeDtypeStruct((SEQ, N_GROUPS * N_STATE), F32), jax.ShapeDtypeStruct((SEQ, DT_PAD), BF16),
                   jax.ShapeDtypeStruct((1, DT_PAD), F32), jax.ShapeDtypeStruct((1, DT_PAD), F32),
                   jax.ShapeDtypeStruct((1, DT_PAD), F32)],
        scratch_shapes=[pltpu.VMEM((HEADS_PER_GROUP, HEAD, N_STATE), F32)],
        compiler_params=_params(("parallel", "arbitrary")),
    )(xbc, xbc, xbc, dt_raw, bias, alog, dsk, h_in, dy)
    return res


CROSS_HEAD = 128
CROSS_ROWS = 512


def _cross_head(q, k, v, gq, gk):
    qn = _rms(q, gq) * (CROSS_HEAD ** -0.5)
    kn = _rms(k, gk)
    s = _bdot(qn, kn, NT)
    p = jnp.exp(s - jnp.max(s, axis=-1, keepdims=True))
    return _bdot(p, v, NN) / jnp.sum(p, axis=-1, keepdims=True)


def _cross_specs():
    q_spec = pl.BlockSpec((CROSS_ROWS, CROSS_HEAD), lambda h, i: (i, h))
    k_spec = pl.BlockSpec((N_MEM, CROSS_HEAD), lambda h, i: (0, h))
    v_spec = pl.BlockSpec((N_MEM, CROSS_HEAD), lambda h, i: (0, 4 + h))
    g_spec = pl.BlockSpec((1, CROSS_HEAD), lambda h, i: (0, 0))
    return q_spec, k_spec, v_spec, g_spec


def _cross_fwd(qc, kv, gq, gk):
    q_spec, k_spec, v_spec, g_spec = _cross_specs()

    def body(q_ref, k_ref, v_ref, gq_ref, gk_ref, o_ref):
        o_ref[...] = _cross_head(q_ref[...], k_ref[...], v_ref[...], gq_ref[...], gk_ref[...]).astype(o_ref.dtype)

    return pl.pallas_call(
        body, name="cross_fwd", grid=(4, SEQ // CROSS_ROWS),
        in_specs=[q_spec, k_spec, v_spec, g_spec, g_spec], out_specs=q_spec,
        out_shape=jax.ShapeDtypeStruct((SEQ, D_CROSS), BF16),
        compiler_params=_params(("parallel", "parallel")),
    )(qc, kv, kv, gq, gk)


def _cross_bwd(qc, kv, gq, gk, do):
    q_spec, k_spec, v_spec, g_spec = _cross_specs()

    def body(q_ref, k_ref, v_ref, gq_ref, gk_ref, do_ref, dq_ref, dk_ref, dv_ref, dgq_ref, dgk_ref):
        _, pullback = jax.vjp(_cross_head, q_ref[...], k_ref[...], v_ref[...], gq_ref[...], gk_ref[...])
        dq, dk, dv, dgq, dgk = pullback(do_ref[...].astype(F32))
        dq_ref[...] = dq.astype(dq_ref.dtype)
        row0 = pl.program_id(1) == 0
        all0 = jnp.logical_and(row0, pl.program_id(0) == 0)
        for o_ref, val, init in ((dk_ref, dk, row0), (dv_ref, dv, row0), (dgq_ref, dgq, all0), (dgk_ref, dgk, all0)):
            @pl.when(init)
            def _(o_ref=o_ref, val=val):
                o_ref[...] = val

            @pl.when(jnp.logical_not(init))
            def _(o_ref=o_ref, val=val):
                o_ref[...] += val

    return pl.pallas_call(
        body, name="cross_bwd", grid=(4, SEQ // CROSS_ROWS),
        in_specs=[q_spec, k_spec, v_spec, g_spec, g_spec, q_spec],
        out_specs=[q_spec, k_spec, k_spec, g_spec, g_spec],
        out_shape=[jax.ShapeDtypeStruct((SEQ, D_CROSS), BF16), jax.ShapeDtypeStruct((N_MEM, D_CROSS), F32),
                   jax.ShapeDtypeStruct((N_MEM, D_CROSS), F32), jax.ShapeDtypeStruct((1, CROSS_HEAD), F32),
                   jax.ShapeDtypeStruct((1, CROSS_HEAD), F32)],
        compiler_params=_params(("arbitrary", "arbitrary")),
    )(qc, kv, kv, gq, gk, do)


def _loss_head(y, target):
    tr = 256

    def body(y_ref, t_ref, dy_ref, dyb_ref, loss_ref):
        err = y_ref[...] - t_ref[...]
        dy = err * (1.0 / D_MODEL)
        dy_ref[...] = dy
        dyb_ref[...] = dy.astype(BF16)
        part = jnp.sum(jnp.sum(err * err, axis=1, keepdims=True), axis=0, keepdims=True) * (0.5 / D_MODEL)
        part = jnp.broadcast_to(part, (1, 128))

        @pl.when(pl.program_id(0) == 0)
        def _():
            loss_ref[...] = part

        @pl.when(pl.program_id(0) != 0)
        def _():
            loss_ref[...] += part

    blk = pl.BlockSpec((tr, D_MODEL), lambda i: (i, 0))
    return pl.pallas_call(
        body, name="loss_head", grid=(SEQ // tr,),
        in_specs=[blk, blk], out_specs=[blk, blk, pl.BlockSpec((1, 128), lambda i: (0, 0))],
        out_shape=[jax.ShapeDtypeStruct((SEQ, D_MODEL), F32), jax.ShapeDtypeStruct((SEQ, D_MODEL), BF16),
                   jax.ShapeDtypeStruct((1, 128), F32)],
        compiler_params=_params(("arbitrary",)),
    )(y, target)


def _pad_heads(v):
    return jnp.pad(v.reshape(N_GROUPS, HEADS_PER_GROUP), ((0, 0), (0, 128 - HEADS_PER_GROUP))).reshape(1, DT_PAD)


def _unpad_heads(v):
    return v.reshape(v.shape[0], N_GROUPS, 128)[:, :, :HEADS_PER_GROUP].reshape(v.shape[0], N_DT)


def _rope_tables(positions):
    half = ROT // 2
    inv_freq = ROPE_THETA ** (-2.0 * jnp.arange(half, dtype=F32) / ROT)
    ang = positions.reshape(SEQ, 1).astype(F32) * inv_freq
    cos, sin = jnp.cos(ang), jnp.sin(ang)
    ones, zeros = jnp.ones((SEQ, HEAD - ROT), F32), jnp.zeros((SEQ, HEAD - ROT), F32)
    cos_h = jnp.concatenate([cos, cos, ones], axis=1)
    sin_h = jnp.concatenate([-sin, sin, zeros], axis=1)
    return jnp.tile(cos_h, (1, 2)), jnp.tile(sin_h, (1, 2))


def _add_res(acc, res):
    return (acc + res,)


def _take_token(grads):
    token = getattr(grads, "token", None)
    if token is None:
        return ()
    grads.token = None
    return (token,)


def _local_step(x, mem, positions, target, p, w, more_weights=None, grads=None):
    grads = {} if grads is None else grads
    w = dict(w)
    cos, sin = _rope_tables(positions)
    gq2, gk2 = jnp.tile(p["g_q"], (1, 2)), jnp.tile(p["g_k"], (1, 2))
    bias, alog, dsk = _pad_heads(p["dt_bias"]), _pad_heads(p["a_log"]), _pad_heads(p["d_skip"])
    norm_out = [(D_MODEL, BF16, D_MODEL, 0, False)]

    h = _rowwise(_norm_fn, [_full(x)], [_full(p["g_mix"])], norm_out, name="norm_in")[0]
    proj = _matmul(h, w["w_main"], mode="nn", name="in_proj", outs=[F32])
    dt_raw = _matmul(h, w["w_dt"], mode="nn", name="dt_proj", outs=[F32])
    qk_rows = [(proj, 128, 0, True), (proj, 128, 8, True), (proj, 128, 16, True), _full(cos), _full(sin)]
    qk_vecs = [_full(gq2), _full(gk2)]
    qn, kn, vn = _rowwise(_qk_fn, qk_rows, qk_vecs, [(D_ATTN, F32, 128, 0, True)] * 3, name="qk_prep", groups=8)
    branches = [_attention_fwd(qn, kn, vn, b) for b in range(3)]
    merge_rows = [_full(o) for o, _ in branches] + [_full(lse) for _, lse in branches]
    attn = _rowwise(_merge_fn, merge_rows, [_full(p["g_attn_out"])], [(D_ATTN, BF16, D_ATTN, 0, False)], name="attn_merge")[0]
    xbc = _conv_fwd(proj, p["conv_w"], p["conv_b"])
    y_ssd, h_in = _ssd_fwd(xbc, dt_raw, bias, alog, dsk)
    gate_rows = [(y_ssd, 256, 0, True), (proj, 256, 12, True)]
    gate_vecs = [(p["g_ssm_out"], 256, 0, True)]
    ssm = _rowwise(_gate_fn, gate_rows, gate_vecs, [(D_SSM, BF16, 256, 0, True)], name="ssm_gate", groups=4)[0]
    mix = jnp.concatenate([attn, ssm], axis=1)
    if more_weights is not None:
        w.update(more_weights("mixer_done", mix))
    x1 = _matmul(mix, w["w_out"], mode="nn", name="out_proj", outs=[F32], extra=(x,), epilogue=_add_res)
    hc = _rowwise(_norm_fn, [_full(x1)], [_full(p["g_cross"])], norm_out, name="norm_cross")[0]
    memh = _rowwise(_norm_fn, [_full(mem)], [_full(p["g_mem"])], norm_out, name="norm_mem", n_rows=N_MEM)[0]
    qc = _matmul(hc, w["w_cq"], mode="nn", name="cq_proj", outs=[F32])
    kv = _matmul(memh, w["w_ckv"], mode="nn", name="ckv_proj", outs=[F32])
    oc = _cross_fwd(qc, kv, p["g_cq"], p["g_ck"])
    x2 = _matmul(oc, w["w_co"], mode="nn", name="co_proj", outs=[F32], extra=(x1,), epilogue=_add_res)
    hm = _rowwise(_norm_fn, [_full(x2)], [_full(p["g_mlp"])], norm_out, name="norm_mlp")[0]
    if more_weights is not None:
        w.update(more_weights("cross_done", hm))
    u, act = _matmul(hm, w["w_up"], mode="nn", name="up_proj", outs=[F32, BF16],
                     epilogue=lambda acc: (acc, jnp.square(jnp.maximum(acc, 0.0))))
    x3 = _matmul(act, w["w_down"], mode="nn", name="down_proj", outs=[F32], extra=(x2,), epilogue=_add_res)
    dy, dyb, loss = _loss_head(x3, target)

    grads["w_down"] = _matmul(act, dyb, mode="tn", name="dw_down", outs=[BF16])
    du = _matmul(dyb, w["w_down"], mode="nt", name="d_act", outs=[BF16], extra=(u,), after=_take_token(grads),
                 epilogue=lambda acc, uu: (acc * (2.0 * jnp.maximum(uu, 0.0)),))
    grads["w_up"] = _matmul(hm, du, mode="tn", name="dw_up", outs=[BF16], col_shards=4)
    dhm = _matmul(du, w["w_up"], mode="nt", name="d_hm", outs=[F32], after=_take_token(grads))
    dx2, grads["g_mlp"] = _rowwise_vjp(
        _norm_fn, [_full(x2)], [_full(p["g_mlp"])], [[_full(dhm)]],
        [(0, D_MODEL, F32, D_MODEL, 0, False, _full(dy))], [(0, D_MODEL, D_MODEL, 0, False)], name="norm_mlp_bwd")
    grads["w_co"] = _matmul(oc, dx2, mode="tn", name="dw_co", outs=[BF16], col_shards=4)
    doc = _matmul(dx2, w["w_co"], mode="nt", name="d_oc", outs=[BF16])
    dqc, dkc, dvc, grads["g_cq"], grads["g_ck"] = _cross_bwd(qc, kv, p["g_cq"], p["g_ck"], doc)
    grads["w_cq"] = _matmul(hc, dqc, mode="tn", name="dw_cq", outs=[BF16])
    dhc = _matmul(dqc, w["w_cq"], mode="nt", name="d_hc", outs=[F32])
    dkv = jnp.concatenate([dkc, dvc], axis=1)
    grads["w_ckv"] = _matmul(memh, dkv, mode="tn", name="dw_ckv", outs=[BF16])
    dmemh = _matmul(dkv, w["w_ckv"], mode="nt", name="d_memh", outs=[F32])
    grads["g_mem"] = _rowwise_vjp(_norm_fn, [_full(mem)], [_full(p["g_mem"])], [[_full(dmemh)]], [],
                                  [(0, D_MODEL, D_MODEL, 0, False)], name="norm_mem_bwd", n_rows=N_MEM)[0]
    dx1, grads["g_cross"] = _rowwise_vjp(
        _norm_fn, [_full(x1)], [_full(p["g_cross"])], [[_full(dhc)]],
        [(0, D_MODEL, F32, D_MODEL, 0, False, _full(dx2))], [(0, D_MODEL, D_MODEL, 0, False)], name="norm_cross_bwd")
    grads["w_out"] = _matmul(mix, dx1, mode="tn", name="dw_out", outs=[BF16])
    dmix = _matmul(dx1, w["w_out"], mode="nt", name="d_mix", outs=[F32], after=_take_token(grads))
    merge_grads = [(i, D_ATTN, F32, D_ATTN, 0, False, None) for i in range(6)]
    *dol, grads["g_attn_out"] = _rowwise_vjp(
        _merge_fn, merge_rows, [_full(p["g_attn_out"])], [[(dmix, D_ATTN, 0, False)]],
        merge_grads, [(0, D_ATTN, D_ATTN, 0, False)], name="attn_merge_bwd")
    dqkv = [_attention_bwd(qn, kn, vn, dol[b], dol[3 + b], b) for b in range(3)]
    qk_cts = [[(dqkv[b][i], 128, 0, True) for b in range(3)] for i in range(3)]
    dq, dk, dv, dgq2, dgk2 = _rowwise_vjp(
        _qk_fn, qk_rows, qk_vecs, qk_cts, [(i, D_ATTN, BF16, 128, 0, True, None) for i in range(3)],
        [(0, 128, 128, 0, False), (1, 128, 128, 0, False)], name="qk_prep_bwd", groups=8)
    grads["g_q"] = dgq2[:, :HEAD] + dgq2[:, HEAD:]
    grads["g_k"] = dgk2[:, :HEAD] + dgk2[:, HEAD:]
    dy_ssd, dz, grads["g_ssm_out"] = _rowwise_vjp(
        _gate_fn, gate_rows, gate_vecs, [[(dmix, 256, 4, True)]],
        [(0, D_SSM, F32, 256, 0, True, None), (1, D_SSM, BF16, 256, 0, True, None)],
        [(0, D_SSM, 256, 0, True)], name="ssm_gate_bwd", groups=4)
    dxs, db, dc, ddt, dbias, dalog, ddsk = _ssd_bwd(xbc, dt_raw, bias, alog, dsk, h_in, dy_ssd)
    grads["dt_bias"], grads["a_log"], grads["d_skip"] = _unpad_heads(dbias), _unpad_heads(dalog), _unpad_heads(ddsk)
    dxbc_raw, dconv_w, grads["conv_b"] = _conv_bwd(proj, p["conv_w"], p["conv_b"], jnp.concatenate([dxs, db, dc], axis=1))
    grads["conv_w"] = dconv_w[:4]
    dproj = jnp.concatenate([dq, dk, dv, dz, dxbc_raw], axis=1)
    grads["w_main"] = _matmul(h, dproj, mode="tn", name="dw_main", outs=[BF16])
    grads["w_dt"] = _matmul(h, ddt, mode="tn", name="dw_dt", outs=[BF16])
    dh = _matmul(dproj, w["w_main"], mode="nt", name="d_h_main", outs=[F32], after=_take_token(grads))
    dh = _matmul(ddt, w["w_dt"], mode="nt", name="d_h_dt", outs=[F32], extra=(dh,), epilogue=_add_res)
    grad_x, grads["g_mix"] = _rowwise_vjp(
        _norm_fn, [_full(x)], [_full(p["g_mix"])], [[_full(dh)]],
        [(0, D_MODEL, F32, D_MODEL, 0, False, _full(dx1))], [(0, D_MODEL, D_MODEL, 0, False)], name="norm_in_bwd")
    return loss, grad_x, grads


MATRICES = ("w_in", "w_out", "w_cq", "w_ckv", "w_co", "w_up", "w_down")
ROW_SHARDED = ("w_out", "w_cq", "w_ckv", "w_down")
N_CHIPS = 4
ANY = pl.BlockSpec(memory_space=pl.ANY)


def _place():
    return lax.axis_index("x"), lax.axis_index("y"), lax.axis_index("c")


def _other_chips(x, y):
    return [(1 - x, y), (x, 1 - y), (1 - x, 1 - y)]


def _remote(src, dst, send_sem, recv_sem, device):
    return pltpu.make_async_remote_copy(src_ref=src, dst_ref=dst, send_sem=send_sem, recv_sem=recv_sem,
                                        device_id=device, device_id_type=MESH)


def _gathered_shape(name, shard):
    rows, cols = shard.shape
    if name == "w_in":
        return (N_CHIPS, rows, cols)
    return (N_CHIPS * rows, cols) if name in ROW_SHARDED else (rows, N_CHIPS * cols)


def _shard_window(name, ref, rows, cols, chip, half):
    r0, nr = (0, rows) if half is None else (half * (rows // 2), rows // 2)
    if name == "w_in":
        return ref.at[chip, pl.ds(r0, nr), :]
    if name in ROW_SHARDED:
        return ref.at[pl.ds(chip * rows + r0, nr), :]
    return ref.at[pl.ds(r0, nr), pl.ds(pl.multiple_of(chip * cols, 128), cols)]


def _cast_into_gathered(w, name, chip):
    rows, cols = w.shape
    tr = _tile(rows, ROW_TILE)

    def body(chip_ref, w_ref, o_ref):
        o_ref[...] = w_ref[...].astype(BF16)

    if name == "w_in":
        out_spec = pl.BlockSpec((None, tr, cols), lambda i, chip_ref: (chip_ref[0], i, 0))
    elif name in ROW_SHARDED:
        out_spec = pl.BlockSpec((tr, cols), lambda i, chip_ref: (chip_ref[0] * (rows // tr) + i, 0))
    else:
        out_spec = pl.BlockSpec((tr, cols), lambda i, chip_ref: (i, chip_ref[0]))
    grid_spec = pltpu.PrefetchScalarGridSpec(
        num_scalar_prefetch=1, grid=(rows // tr,),
        in_specs=[pl.BlockSpec((tr, cols), lambda i, chip_ref: (i, 0))], out_specs=out_spec)
    return pl.pallas_call(body, name="cast_" + name, grid_spec=grid_spec,
                          out_shape=jax.ShapeDtypeStruct(_gathered_shape(name, w), BF16),
                          compiler_params=_params(("parallel",)))(chip.reshape(1).astype(jnp.int32), w)


HBM = pl.BlockSpec(memory_space=pltpu.HBM)
SEM = pl.BlockSpec(memory_space=pltpu.SEMAPHORE)
EFFECT = pltpu.SideEffectType.DATAFLOW_SIDE_EFFECTING


def _split_start(name, bufs, plan, counts, after=()):
    n, n_g, n_after = len(bufs), len(counts), len(after)

    def body(*refs):
        ins, sems, token = refs[:n], refs[n + n_after:n + n_after + 2 * n_g], refs[-1]
        for g, copies in enumerate(plan(ins)):
            for i, (src, dst, device, _) in enumerate(copies):
                _remote(src, dst, sems[2 * g].at[i], sems[2 * g + 1].at[i], device).start()
        token[...] = jnp.zeros_like(token)

    sem_shapes = [pltpu.SemaphoreType.DMA((cnt,)) for cnt in counts for _ in range(2)]
    res = pl.pallas_call(
        body, name=name,
        out_shape=(*sem_shapes, *[pltpu.HBM(b.shape, b.dtype) for b in bufs], jax.ShapeDtypeStruct((8, 128), F32)),
        in_specs=(*(HBM,) * n, *(ANY,) * n_after),
        out_specs=(*(SEM,) * (2 * n_g), *(HBM,) * n, pl.BlockSpec(memory_space=pltpu.VMEM)),
        input_output_aliases={i: 2 * n_g + i for i in range(n)},
        compiler_params=pltpu.CompilerParams(has_side_effects=EFFECT),
    )(*[pltpu.with_memory_space_constraint(b, pltpu.HBM) for b in bufs], *after)
    sems = [(res[2 * g], res[2 * g + 1]) for g in range(n_g)]
    return sems, list(res[2 * n_g:2 * n_g + n]), res[-1]


def _split_wait(name, bufs, sems, plan, after):
    n = len(bufs)

    def body(*refs):
        ins, send, recv = refs[:n], refs[n], refs[n + 1]
        (copies,) = plan(ins)
        for i, (src, _, device, landing) in enumerate(copies):
            cp = _remote(src, landing, send.at[i], recv.at[i], device)
            cp.wait_send()
            cp.wait_recv()

    res = pl.pallas_call(
        body, name=name, out_shape=tuple(pltpu.HBM(b.shape, b.dtype) for b in bufs),
        in_specs=(*(HBM,) * n, SEM, SEM, ANY), out_specs=(HBM,) * n,
        input_output_aliases={i: i for i in range(n)},
        compiler_params=pltpu.CompilerParams(has_side_effects=EFFECT),
    )(*bufs, sems[0], sems[1], after)
    return list(res)


def _ici_plan(names, shard_shapes):
    def plan(refs):
        x, y, c = _place()
        copies = []
        for ref, name in zip(refs, names):
            win = _shard_window(name, ref, *shard_shapes[name], 2 * x + y, c)
            for px, py in _other_chips(x, y):
                copies.append((win, win, (px, py, c), _shard_window(name, ref, *shard_shapes[name], 2 * px + py, c)))
        return [copies]
    return plan


def _pass_on_plan(names, shard_shapes):
    def plan(refs):
        x, y, c = _place()
        copies = []
        for ref, name in zip(refs, names):
            for px, py in _other_chips(x, y):
                win = _shard_window(name, ref, *shard_shapes[name], 2 * px + py, c)
                copies.append((win, win, (x, y, 1 - c), _shard_window(name, ref, *shard_shapes[name], 2 * px + py, 1 - c)))
        return [copies]
    return plan


def _scatter_plan(n_pairs):
    def plan(refs):
        x, y, c = _place()
        copies = []
        for src, dst in zip(refs[:n_pairs], refs[n_pairs:]):
            for k, (px, py) in enumerate(_other_chips(x, y)):
                copies.append((src.at[2 * px + py], dst.at[k], (px, py, c), dst.at[k]))
        return [copies]
    return plan


def _sibling_swap(arrs, name):
    n = len(arrs)

    def body(*refs):
        ins, outs, send, recv = refs[:n], refs[n:2 * n], refs[2 * n], refs[2 * n + 1]
        x, y, c = _place()
        cps = [_remote(ins[w].at[:, 1 - c], outs[w], send.at[w], recv.at[w], (x, y, 1 - c)) for w in range(n)]
        for cp in cps:
            cp.start()
        for cp in cps:
            cp.wait()

    return pl.pallas_call(
        body, name=name, in_specs=[ANY] * n, out_specs=[ANY] * n,
        out_shape=[jax.ShapeDtypeStruct((a.shape[0],) + a.shape[2:], a.dtype) for a in arrs],
        scratch_shapes=[pltpu.SemaphoreType.DMA((n,))] * 2,
    )(*arrs)


def _sibling_share(arrs):
    n = len(arrs)

    def body(*refs):
        ins, outs, send, recv = refs[:n], refs[n:2 * n], refs[2 * n], refs[2 * n + 1]
        x, y, c = _place()
        cps = [_remote(ins[w], outs[w], send.at[w], recv.at[w], (x, y, 1 - c)) for w in range(n)]
        for cp in cps:
            cp.start()
        for cp in cps:
            cp.wait()

    return pl.pallas_call(
        body, name="grad_sibling_share", in_specs=[ANY] * n, out_specs=[ANY] * n,
        out_shape=[jax.ShapeDtypeStruct(a.shape, a.dtype) for a in arrs],
        scratch_shapes=[pltpu.SemaphoreType.DMA((n,))] * 2,
    )(*arrs)


def _small_allreduce(buf, name):
    rows = buf.shape[0]

    def body(x_ref, out_ref, all_ref, send_sems, recv_sems, local_sem):
        x, y, c = _place()
        me, sibling, chips = (x, y, c), (x, y, 1 - c), _other_chips(x, y)

        def block(px, py, pc):
            return all_ref.at[pl.ds((4 * px + 2 * py + pc) * rows, rows), :]

        def copy(k, blk, to, src=None):
            return _remote(block(*blk) if src is None else src, block(*blk), send_sems.at[k], recv_sems.at[k], to)

        own = pltpu.make_async_copy(x_ref, block(*me), local_sem)
        own.start()
        first = [copy(0, me, sibling, src=x_ref)] + [copy(1 + j, me, (*chip, c), src=x_ref) for j, chip in enumerate(chips)]
        for cp in first:
            cp.start()
        passed = [copy(4 + j, (*chip, c), sibling) for j, chip in enumerate(chips)]
        for j, chip in enumerate(chips):
            copy(1 + j, (*chip, c), me).wait_recv()
            passed[j].start()
        copy(0, sibling, me).wait_recv()
        for j, chip in enumerate(chips):
            copy(4 + j, (*chip, 1 - c), me).wait_recv()
        for cp in first + passed:
            cp.wait_send()
        own.wait()
        acc = all_ref[pl.ds(0, rows), :]
        for d in range(1, 8):
            acc = acc + all_ref[pl.ds(d * rows, rows), :]
        out_ref[...] = acc

    vmem = pl.BlockSpec(memory_space=pltpu.VMEM)
    return pl.pallas_call(
        body, name=name, in_specs=[vmem], out_specs=vmem,
        out_shape=jax.ShapeDtypeStruct(buf.shape, F32),
        scratch_shapes=[pltpu.VMEM((8 * rows, 128), F32), pltpu.SemaphoreType.DMA((7,)), pltpu.SemaphoreType.DMA((7,)),
                        pltpu.SemaphoreType.DMA],
    )(buf)


ROW_TILE = 256


def _add_halves(arr, recv, c, name):
    _, _, hr, cols = arr.shape
    tr = _tile(hr, ROW_TILE)

    def body(c_ref, a_ref, r_ref, o_ref):
        o_ref[...] = (a_ref[...].astype(F32) + r_ref[...].astype(F32)).astype(o_ref.dtype)

    piece = pl.BlockSpec((None, tr, cols), lambda j, i, c_ref: (j, i, 0))
    grid_spec = pltpu.PrefetchScalarGridSpec(
        num_scalar_prefetch=1, grid=(N_CHIPS, hr // tr),
        in_specs=[pl.BlockSpec((None, None, tr, cols), lambda j, i, c_ref: (j, c_ref[0], i, 0)), piece], out_specs=piece)
    return pl.pallas_call(body, name=name, grid_spec=grid_spec, out_shape=jax.ShapeDtypeStruct(recv.shape, BF16),
                          compiler_params=_params(("parallel", "parallel")))(c.reshape(1).astype(jnp.int32), arr, recv)


def _flip_slot(d):
    return jnp.where(d == 1, 1, jnp.where(d == 3, 2, 0))


def _sum_chips(p, q, chip, name):
    _, hr, cols = p.shape
    tr = _tile(hr, ROW_TILE)

    def body(chip_ref, p_ref, q_ref, o_ref):
        j = pl.program_id(1)
        term = jnp.where(j == chip_ref[0], p_ref[...].astype(F32), q_ref[...].astype(F32))

        @pl.when(j == 0)
        def _():
            o_ref[...] = term

        @pl.when(j != 0)
        def _():
            o_ref[...] += term

    grid_spec = pltpu.PrefetchScalarGridSpec(
        num_scalar_prefetch=1, grid=(hr // tr, N_CHIPS),
        in_specs=[pl.BlockSpec((None, tr, cols), lambda i, j, chip_ref: (chip_ref[0], i, 0)),
                  pl.BlockSpec((None, tr, cols), lambda i, j, chip_ref: (_flip_slot(j ^ chip_ref[0]), i, 0))],
        out_specs=pl.BlockSpec((tr, cols), lambda i, j, chip_ref: (i, 0)))
    return pl.pallas_call(body, name=name, grid_spec=grid_spec, out_shape=jax.ShapeDtypeStruct((hr, cols), F32),
                          compiler_params=_params(("parallel", "arbitrary")))(chip.reshape(1).astype(jnp.int32), p, q)


def _adamw_halves(w, g_own, g_other, m, v, c, name):
    rows, cols = w.shape
    tr = _tile(rows // 2, ROW_TILE)
    per_half = rows // 2 // tr

    def body(c_ref, w_ref, own_ref, other_ref, m_ref, v_ref, g_ref, d_ref, nm_ref, nv_ref):
        mine = (pl.program_id(0) // per_half) == c_ref[0]
        g_ = jnp.where(mine, own_ref[...], other_ref[...])
        g_ref[...] = g_
        d_ref[...], nm_ref[...], nv_ref[...] = _adamw_math(w_ref[...], g_, m_ref[...], v_ref[...])

    blk = pl.BlockSpec((tr, cols), lambda i, c_ref: (i, 0))
    half = pl.BlockSpec((tr, cols), lambda i, c_ref: (i % per_half, 0))
    grid_spec = pltpu.PrefetchScalarGridSpec(num_scalar_prefetch=1, grid=(rows // tr,),
                                             in_specs=[blk, half, half, blk, blk], out_specs=[blk] * 4)
    return pl.pallas_call(body, name=name, grid_spec=grid_spec, out_shape=[jax.ShapeDtypeStruct(w.shape, F32)] * 4,
                          compiler_params=_params(("parallel",)))(c.reshape(1).astype(jnp.int32), w, g_own, g_other, m, v)


def _adamw_math(w, g, m, v):
    m_new = ADAM_B1 * m + (1.0 - ADAM_B1) * g
    v_new = ADAM_B2 * v + (1.0 - ADAM_B2) * (g * g)
    m_hat = m_new / (1.0 - ADAM_B1 ** ADAM_STEP)
    v_hat = v_new / (1.0 - ADAM_B2 ** ADAM_STEP)
    return -ADAM_LR * (m_hat / (jnp.sqrt(v_hat) + ADAM_EPS) + ADAM_WD * w), m_new, v_new


def _adamw(w, g, m, v, name):
    rows, cols = w.shape
    tr = _tile(rows, ROW_TILE)

    def body(w_ref, g_ref, m_ref, v_ref, d_ref, nm_ref, nv_ref):
        d_ref[...], nm_ref[...], nv_ref[...] = _adamw_math(w_ref[...], g_ref[...], m_ref[...], v_ref[...])

    blk = pl.BlockSpec((tr, cols), lambda i: (i, 0))
    return pl.pallas_call(body, name=name, grid=(rows // tr,), in_specs=[blk] * 4, out_specs=[blk] * 3,
                          out_shape=[jax.ShapeDtypeStruct(w.shape, F32)] * 3, compiler_params=_params(("parallel",)))(w, g, m, v)


VECTORS = ("g_mix", "g_q", "g_k", "g_attn_out", "conv_b", "dt_bias", "a_log", "d_skip", "g_ssm_out", "g_cross", "g_mem",
           "g_cq", "g_ck", "g_mlp")
WEIGHTS = ("g_mix", "w_in", "g_q", "g_k", "g_attn_out", "conv_w", "conv_b", "dt_bias", "a_log", "d_skip", "g_ssm_out", "w_out",
           "g_cross", "g_mem", "w_cq", "w_ckv", "g_cq", "g_ck", "w_co", "g_mlp", "w_up", "w_down")


def _pack(parts):
    flat = jnp.concatenate([t.reshape(-1) for t in parts])
    total = -(-flat.shape[0] // 1024) * 1024
    return jnp.pad(flat, (0, total - flat.shape[0])).reshape(total // 128, 128)


def _unpack(buf, shapes):
    flat, out, pos = buf.reshape(-1), [], 0
    for shape in shapes:
        size = math.prod(shape)
        out.append(flat[pos:pos + size].reshape(shape))
        pos += size
    return out


def kernel(x, mem, positions, g_mix, w_in, g_q, g_k, g_attn_out, conv_w, conv_b, dt_bias, a_log, d_skip, g_ssm_out, w_out, g_cross, g_mem, w_cq, w_ckv, g_cq, g_ck, w_co, g_mlp, w_up, w_down, loss_target, m_g_mix, m_w_in, m_g_q, m_g_k, m_g_attn_out, m_conv_w, m_conv_b, m_dt_bias, m_a_log, m_d_skip, m_g_ssm_out, m_w_out, m_g_cross, m_g_mem, m_w_cq, m_w_ckv, m_g_cq, m_g_ck, m_w_co, m_g_mlp, m_w_up, m_w_down, v_g_mix, v_w_in, v_g_q, v_g_k, v_g_attn_out, v_conv_w, v_conv_b, v_dt_bias, v_a_log, v_d_skip, v_g_ssm_out, v_w_out, v_g_cross, v_g_mem, v_w_cq, v_w_ckv, v_g_cq, v_g_ck, v_w_co, v_g_mlp, v_w_up, v_w_down):
    args = dict(locals())
    weights = {n: args[n][0] for n in WEIGHTS}
    mom_m = {n: args["m_" + n][0] for n in WEIGHTS}
    mom_v = {n: args["v_" + n][0] for n in WEIGHTS}
    x_idx, y_idx, c_idx = _place()
    chip = 2 * x_idx + y_idx

    conv_parts = _small_allreduce(_pack([jnp.zeros((N_CHIPS, 4, 512), F32).at[chip].set(0.5 * weights["conv_w"])]),
                                  "gather_conv_taps")
    shapes = {n: weights[n].shape for n in MATRICES}
    first, mid, late = ("w_in",), ("w_out", "w_cq", "w_ckv", "w_co"), ("w_up", "w_down")
    bufs = [_cast_into_gathered(weights[n], n, chip) for n in first + mid + late]
    plan = lambda refs: _ici_plan(first, shapes)(refs[:1]) + _ici_plan(mid + late, shapes)(refs[1:])
    ici_sems, bufs, token = _split_start("gather_ici_start", bufs, plan, [3, 18], after=(conv_parts,))
    w_in_buf = _split_wait("gather_ici_wait_w_in", bufs[:1], ici_sems[0], _ici_plan(first, shapes), token)
    pass_sems, w_in_buf, token = _split_start("gather_pass_start_w_in", w_in_buf, _pass_on_plan(first, shapes), [3])
    w_in_buf = _split_wait("gather_pass_wait_w_in", w_in_buf, pass_sems[0], _pass_on_plan(first, shapes), token)
    w_in_full = jnp.transpose(w_in_buf[0], (1, 0, 2)).reshape(D_MODEL, D_MAIN + N_DT)
    full = {"w_main": w_in_full[:, :D_MAIN],
            "w_dt": jnp.pad(w_in_full[:, D_MAIN:].reshape(D_MODEL, N_GROUPS, HEADS_PER_GROUP),
                            ((0, 0), (0, 0), (0, 128 - HEADS_PER_GROUP))).reshape(D_MODEL, DT_PAD)}
    in_flight = {}

    def more_weights(stage, after):
        if stage == "mixer_done":
            rest = _split_wait("gather_ici_wait_rest", bufs[1:], ici_sems[1], _ici_plan(mid + late, shapes), after)
            plan = lambda refs: _pass_on_plan(mid, shapes)(refs[:4]) + _pass_on_plan(late, shapes)(refs[4:])
            sems, rest, token = _split_start("gather_pass_start_rest", rest, plan, [12, 6])
            in_flight["late"] = (rest[4:], sems[1])
            return dict(zip(mid, _split_wait("gather_pass_wait_mid", rest[:4], sems[0], _pass_on_plan(mid, shapes), token)))
        late_bufs, sems = in_flight.pop("late")
        return dict(zip(late, _split_wait("gather_pass_wait_late", late_bufs, sems, _pass_on_plan(late, shapes), after)))

    conv_full =_unpack(conv_parts, [(N_CHIPS, 4, 512)])[0].transpose(1, 0, 2).reshape(4, 4 * 512)
    params = {n: weights[n].reshape(1, -1) for n in VECTORS}
    params["conv_w"] = conv_full

    groups = (("w_down",), ("w_up",), ("w_co", "w_cq", "w_ckv", "w_out"), ("w_in",))
    scattered = []

    class GradStore(dict):
        def __setitem__(self, name, value):
            super().__setitem__(name, value)
            if "w_main" in self and "w_dt" in self and "w_in" not in self:
                gw_in = jnp.concatenate([self["w_main"], _unpad_heads(self["w_dt"])], axis=1)
                self["w_in"] = gw_in.reshape(D_MODEL, N_CHIPS, gw_in.shape[1] // N_CHIPS).transpose(1, 0, 2)
            for group in groups:
                if name in group and all(n in self for n in group):
                    pieces = [self[n].reshape(N_CHIPS, 2, shapes[n][0] // 2, shapes[n][1]) for n in group]
                    from_sibling = _sibling_swap(pieces, "grad_swap_" + group[0])
                    sums = [_add_halves(a, r, c_idx, "add_halves_" + n) for n, a, r in zip(group, pieces, from_sibling)]
                    landing = [lax.empty((3,) + s.shape[1:], BF16) for s in sums]
                    sems, thru, self.token = _split_start("grad_scatter_start_" + group[0], sums + landing,
                                                          _scatter_plan(len(sums)), [3 * len(sums)])
                    scattered.append((group, sems[0], thru))

    loss, grad_x, grads = _local_step(x[0], mem[0], positions[0], loss_target[0], params, full, more_weights, GradStore())

    halves = {}
    for group, sems, thru in scattered:
        thru = _split_wait("grad_scatter_wait_" + group[0], thru, sems, _scatter_plan(len(group)), grad_x)
        for i, n in enumerate(group):
            halves[n] = _sum_chips(thru[i], thru[len(group) + i], chip, "sum_chips_" + n)
    other_halves = dict(zip(MATRICES, _sibling_share([halves[n] for n in MATRICES])))
    out_g, out_d, out_m, out_v = {}, {}, {}, {}
    for n in MATRICES:
        out_g[n], out_d[n], out_m[n], out_v[n] = _adamw_halves(weights[n], halves[n], other_halves[n], mom_m[n], mom_v[n],
                                                               c_idx, "adamw_" + n)

    small = [grads[n] for n in VECTORS] + [grads["conv_w"]]
    summed = _unpack(_small_allreduce(_pack(small), "allreduce_vectors"), [t.shape for t in small])
    g_small = dict(zip(VECTORS, summed[:-1]))
    g_small["conv_w"] = lax.dynamic_slice_in_dim(summed[-1], chip * 512, 512, axis=1)
    names = VECTORS + ("conv_w",)
    shapes = [weights[n].shape for n in names]
    packed = [_pack([src[n] for n in names]) for src in (weights, g_small, mom_m, mom_v)]
    small_out = [_unpack(t, shapes) for t in _adamw(*packed, "adamw_small")]
    for i, n in enumerate(names):
        out_g[n] = g_small[n].reshape(shapes[i])
        out_d[n], out_m[n], out_v[n] = small_out[0][i], small_out[1][i], small_out[2][i]

    total_loss = lax.psum(loss[0, 0], ("x", "y", "c"))
    outs = [total_loss, grad_x[None]]
    for group in (out_g, out_d, out_m, out_v):
        outs += [group[n][None] for n in WEIGHTS]
    return tuple(outs)
```

```python
import functools
import math

import jax
import jax.numpy as jnp
from jax import lax
from jax.experimental import pallas as pl
from jax.experimental.pallas import tpu as pltpu

F32 = jnp.float32
BF16 = jnp.bfloat16

SEQ = 2048
D_MODEL = 2048
HEAD = 64
D_ATTN = 1024
D_SSM = 1024
N_GROUPS = 4
N_STATE = 128
CHUNK = 128
ATT_BLK = 128
N_MEM = 256
D_CROSS = 512
D_FF = 8192
D_MAIN = 6144
N_DT = 16
DT_PAD = 512
ROT = 16
ROPE_THETA = 500000.0
EPS = 1e-6
NEG = -1e30
BRANCH_BLOCKS = (16, 4, 1)
DILATIONS = (1, 4, 16)

ADAM_LR, ADAM_B1, ADAM_B2, ADAM_EPS, ADAM_WD, ADAM_STEP = 0.001, 0.9, 0.999, 1e-08, 0.01, 10

VMEM_LIMIT = 56 * 1024 * 1024
MESH = pl.DeviceIdType.MESH


def _params(sem, **kw):
    return pltpu.CompilerParams(dimension_semantics=sem, vmem_limit_bytes=VMEM_LIMIT, **kw)


def _bdot(a, b, dims):
    return lax.dot_general(a.astype(BF16), b.astype(BF16), (dims, ((), ())), preferred_element_type=F32)


def _fdot(a, b, dims):
    return lax.dot_general(a, b, (dims, ((), ())), preferred_element_type=F32, precision=lax.Precision.HIGHEST)


NN = ((1,), (0,))
NT = ((1,), (1,))
TN = ((0,), (0,))


def _tile(n, want):
    t = min(n, want)
    while n % t:
        t //= 2
    return t


def _matmul(a, b, *, mode, name, outs, extra=(), epilogue=None, col_shards=1, after=(), tm=1024, tn=1024, tk=2048):
    if mode == "nn":
        (m, k), n = a.shape, b.shape[1]
    elif mode == "nt":
        (m, k), n = a.shape, b.shape[0]
    else:
        (k, m), n = a.shape, b.shape[1]
    tm, tn, tk = _tile(m, tm), _tile(n // col_shards, tn), _tile(k, tk)
    nk = k // tk
    per_shard = n // col_shards // tn
    dims = {"nn": NN, "nt": NT, "tn": TN}[mode]
    a_spec = pl.BlockSpec((tk, tm), lambda i, j, kk: (kk, i)) if mode == "tn" else pl.BlockSpec((tm, tk), lambda i, j, kk: (i, kk))
    b_spec = pl.BlockSpec((tn, tk), lambda i, j, kk: (j, kk)) if mode == "nt" else pl.BlockSpec((tk, tn), lambda i, j, kk: (kk, j))
    o_spec = pl.BlockSpec((tm, tn), lambda i, j, kk: (i, j))
    n_extra, n_out, n_after = len(extra), len(outs), len(after)

    def body(a_ref, b_ref, *rest):
        extra_refs, out_refs, acc_ref = rest[:n_extra], rest[n_extra + n_after:n_extra + n_after + n_out], rest[-1]
        kk = pl.program_id(2)

        @pl.when(kk == 0)
        def _():
            acc_ref[...] = jnp.zeros_like(acc_ref)

        acc_ref[...] += _bdot(a_ref[...], b_ref[...], dims)

        @pl.when(kk == nk - 1)
        def _():
            acc = acc_ref[...]
            res = (acc,) if epilogue is None else epilogue(acc, *[e[...] for e in extra_refs])
            for o_ref, r in zip(out_refs, res):
                o_ref[...] = r.astype(o_ref.dtype)

    if col_shards == 1:
        out_specs, out_dims = [o_spec] * n_out, (m, n)
    else:
        sharded = pl.BlockSpec((None, tm, tn), lambda i, j, kk: (j // per_shard, i, j % per_shard))
        out_specs, out_dims = [sharded] * n_out, (col_shards, m, n // col_shards)
    res = pl.pallas_call(
        body, name=name, grid=(m // tm, n // tn, nk),
        in_specs=[a_spec, b_spec] + [o_spec] * n_extra + [pl.BlockSpec(memory_space=pl.ANY)] * n_after,
        out_specs=out_specs,
        out_shape=[jax.ShapeDtypeStruct(out_dims, dt) for dt in outs],
        scratch_shapes=[pltpu.VMEM((tm, tn), F32)],
        compiler_params=_params(("parallel", "parallel", "arbitrary")),
    )(a, b, *extra, *after)
    return res[0] if n_out == 1 else res


def _row_spec(tr, bw, cb, per_group):
    return pl.BlockSpec((tr, bw), (lambda g, i: (i, cb + g)) if per_group else (lambda g, i: (i, cb)))


def _vec_spec(bw, cb, per_group):
    return pl.BlockSpec((1, bw), (lambda g, i: (0, cb + g)) if per_group else (lambda g, i: (0, cb)))


def _rowwise(fn, rows, vecs, outs, *, name, n_rows=SEQ, tr=256, groups=1, after=()):
    n_r, n_v, n_after = len(rows), len(vecs), len(after)

    def body(*refs):
        vals = [r[...].astype(F32) for r in refs[:n_r + n_v]]
        res = fn(*vals)
        for o_ref, r in zip(refs[n_r + n_v + n_after:], res):
            o_ref[...] = r.astype(o_ref.dtype)

    res = pl.pallas_call(
        body, name=name, grid=(groups, n_rows // tr),
        in_specs=[_row_spec(tr, bw, cb, pg) for _, bw, cb, pg in rows] + [_vec_spec(bw, cb, pg) for _, bw, cb, pg in vecs]
        + [pl.BlockSpec(memory_space=pl.ANY)] * n_after,
        out_specs=[_row_spec(tr, bw, cb, pg) for _, _, bw, cb, pg in outs],
        out_shape=[jax.ShapeDtypeStruct((n_rows, w), dt) for w, dt, _, _, _ in outs],
        compiler_params=_params(("parallel", "parallel")),
    )(*[r[0] for r in rows], *[v[0] for v in vecs], *after)
    return res


def _rowwise_vjp(fn, rows, vecs, cts, row_grads, vec_grads, *, name, n_rows=SEQ, tr=256, groups=1):
    n_r, n_v = len(rows), len(vecs)
    ct_ops = [op for group in cts for op in group]
    ct_sizes = [len(group) for group in cts]
    res_ops = [g[6] for g in row_grads if g[6] is not None]
    n_ct, n_res, n_rg = len(ct_ops), len(res_ops), len(row_grads)

    def body(*refs):
        vals = [r[...].astype(F32) for r in refs[:n_r + n_v]]
        pos = n_r + n_v
        ct_vals = []
        for size in ct_sizes:
            acc = refs[pos][...].astype(F32)
            for t in range(1, size):
                acc = acc + refs[pos + t][...].astype(F32)
            ct_vals.append(acc)
            pos += size
        res_refs = refs[pos:pos + n_res]
        out_refs = refs[pos + n_res:]
        _, pullback = jax.vjp(fn, *vals)
        grads = pullback(tuple(ct_vals))
        r_i = 0
        for o_ref, g in zip(out_refs[:n_rg], row_grads):
            val = grads[g[0]]
            if g[6] is not None:
                val = val + res_refs[r_i][...].astype(F32)
                r_i += 1
            o_ref[...] = val.astype(o_ref.dtype)
        first = (pl.program_id(1) == 0)
        for o_ref, g in zip(out_refs[n_rg:], vec_grads):
            val = jnp.sum(grads[n_r + g[0]], axis=0, keepdims=True)
            init = first if g[4] else jnp.logical_and(first, pl.program_id(0) == 0)

            @pl.when(init)
            def _(o_ref=o_ref, val=val):
                o_ref[...] = val

            @pl.when(jnp.logical_not(init))
            def _(o_ref=o_ref, val=val):
                o_ref[...] += val

    in_specs = [_row_spec(tr, bw, cb, pg) for _, bw, cb, pg in rows] + [_vec_spec(bw, cb, pg) for _, bw, cb, pg in vecs]
    in_specs += [_row_spec(tr, bw, cb, pg) for _, bw, cb, pg in ct_ops + res_ops]
    out_specs = [_row_spec(tr, g[3], g[4], g[5]) for g in row_grads] + [_vec_spec(g[2], g[3], g[4]) for g in vec_grads]
    out_shape = [jax.ShapeDtypeStruct((n_rows, g[1]), g[2]) for g in row_grads]
    out_shape += [jax.ShapeDtypeStruct((1, g[1]), F32) for g in vec_grads]
    return pl.pallas_call(
        body, name=name, grid=(groups, n_rows // tr),
        in_specs=in_specs, out_specs=out_specs, out_shape=out_shape,
        compiler_params=_params(("arbitrary", "arbitrary")),
    )(*[r[0] for r in rows], *[v[0] for v in vecs], *[c[0] for c in ct_ops], *[r[0] for r in res_ops])


def _full(arr, width=None):
    return (arr, arr.shape[1] if width is None else width, 0, False)


def _make_xor(sh):
    def raw(x):
        n = x.shape[-1]
        lane = lax.broadcasted_iota(jnp.int32, x.shape, x.ndim - 1)
        up = pltpu.roll(x, n - sh, x.ndim - 1)
        down = pltpu.roll(x, sh, x.ndim - 1)
        return jnp.where((lane & sh) == 0, up, down)

    f = jax.custom_vjp(raw)
    f.defvjp(lambda x: (raw(x), None), lambda _, ct: (raw(ct),))
    return f


_SWAP_ROPE_HALVES = _make_xor(ROT // 2)


def _head_sum(x):
    n = x.shape[-1]
    same_head = (lax.broadcasted_iota(jnp.int32, (n, n), 0) // HEAD) == (lax.broadcasted_iota(jnp.int32, (n, n), 1) // HEAD)
    return _fdot(x, same_head.astype(F32), NN)


def _rms(x, g):
    return x * lax.rsqrt(jnp.mean(x * x, axis=-1, keepdims=True) + EPS) * g


def _head_rms_rope(x, g, cos, sin, scale):
    y = x * lax.rsqrt(_head_sum(x * x) * (1.0 / HEAD) + EPS) * g
    return (y * cos + _SWAP_ROPE_HALVES(y) * sin) * scale


def _qk_fn(q, k, v, cos, sin, gq, gk):
    return (_head_rms_rope(q, gq, cos, sin, HEAD ** -0.5), _head_rms_rope(k, gk, cos, sin, 1.0), v)


def _norm_fn(x, g):
    return (_rms(x, g),)


def _merge_fn(o0, o1, o2, l0, l1, l2, g):
    m = lax.stop_gradient(jnp.maximum(jnp.maximum(l0, l1), l2))
    e0, e1, e2 = jnp.exp(l0 - m), jnp.exp(l1 - m), jnp.exp(l2 - m)
    mix = (e0 * o0 + e1 * o1 + e2 * o2) / (e0 + e1 + e2)
    return (_rms(mix, g),)


def _gate_fn(y, z, g):
    return (_rms(y * (z * jax.nn.sigmoid(z)), g),)


def _attn_pair(q, kc, vc, kp=None, vp=None, has_prev=None):
    qi = lax.broadcasted_iota(jnp.int32, (ATT_BLK, ATT_BLK), 0)
    kj = lax.broadcasted_iota(jnp.int32, (ATT_BLK, ATT_BLK), 1)
    lane = lax.broadcasted_iota(jnp.int32, (1, 2 * HEAD), 1)
    o, lse = 0.0, 0.0
    for h in range(2):
        pick = ((lane >= h * HEAD) & (lane < (h + 1) * HEAD)).astype(F32)
        qh = q * pick
        s_c = jnp.where(qi >= kj, _bdot(qh, kc, NT), NEG)
        m = jnp.max(s_c, axis=-1, keepdims=True)
        if kp is not None:
            s_p = jnp.where(jnp.logical_and(kj >= qi, has_prev), _bdot(qh, kp, NT), NEG)
            m = jnp.maximum(m, jnp.max(s_p, axis=-1, keepdims=True))
        m = lax.stop_gradient(m)
        p_c = jnp.exp(s_c - m)
        den = jnp.sum(p_c, axis=-1, keepdims=True)
        acc = _bdot(p_c, vc, NN)
        if kp is not None:
            p_p = jnp.exp(s_p - m)
            den = den + jnp.sum(p_p, axis=-1, keepdims=True)
            acc = acc + _bdot(p_p, vp, NN)
        o = o + (pick * (1.0 / den)) * acc
        lse = lse + pick * (m + jnp.log(den))
    return o, lse


def _attn_config(b):
    r = DILATIONS[b]
    return r, ATT_BLK * r, (512 if r == 1 else 128), BRANCH_BLOCKS[b] > 1


def _for_residues(r, fn):
    if r <= 4:
        for rho in range(r):
            fn(rho)
    else:
        def step(t, carry):
            for u in range(4):
                fn(4 * t + u)
            return carry

        lax.fori_loop(0, r // 4, step, 0)


def _strided_rows(start, r):
    if r > 1:
        return pl.ds(start, ATT_BLK, stride=r)
    return pl.ds(start if isinstance(start, int) else pl.multiple_of(start, ATT_BLK), ATT_BLK)


def _attention_fwd(qn, kn, vn, b):
    r, rows, lanes, with_prev = _attn_config(b)
    cur = pl.BlockSpec((rows, lanes), lambda g, n: (n, g))
    prev = pl.BlockSpec((rows, lanes), lambda g, n: (jnp.maximum(n - 1, 0), g))

    def body(*refs):
        ins, (o_ref, l_ref) = refs[:-2], refs[-2:]
        has_prev = pl.program_id(1) > 0

        def one(rho):
            sub = _strided_rows(rho, r)
            for pair in range(lanes // 128):
                sl = pl.ds(pair * 128, 128)
                args = [ref[sub, sl] for ref in ins] + ([has_prev] if with_prev else [])
                o_ref[sub, sl], l_ref[sub, sl] = _attn_pair(*args)

        _for_residues(r, one)

    operands = (qn, kn, vn, kn, vn) if with_prev else (qn, kn, vn)
    return pl.pallas_call(
        body, name="attn_fwd_%d" % r, grid=(D_ATTN // lanes, SEQ // rows),
        in_specs=[cur, cur, cur] + ([prev, prev] if with_prev else []), out_specs=[cur, cur],
        out_shape=[jax.ShapeDtypeStruct((SEQ, D_ATTN), F32)] * 2,
        compiler_params=_params(("parallel", "parallel")),
    )(*operands)


def _attention_bwd(qn, kn, vn, do, dl, b):
    r, rows, lanes, with_prev = _attn_config(b)
    cur = pl.BlockSpec((rows, lanes), lambda g, n: (n, g))
    prev = pl.BlockSpec((rows, lanes), lambda g, n: (jnp.maximum(n - 1, 0), g))
    whole = pl.BlockSpec((SEQ, lanes), lambda g, n: (0, g))
    n_in = 5 if with_prev else 3

    def body(*refs):
        ins, (do_ref, dl_ref, dq_ref, dk_ref, dv_ref) = refs[:n_in], refs[n_in:]
        n = pl.program_id(1)
        has_prev = n > 0

        @pl.when(n == 0)
        def _():
            dk_ref[...] = jnp.zeros_like(dk_ref)
            dv_ref[...] = jnp.zeros_like(dv_ref)

        def one(rho):
            sub = _strided_rows(rho, r)
            sub_c = _strided_rows(n * rows + rho, r)
            sub_p = _strided_rows(jnp.maximum(n - 1, 0) * rows + rho, r)
            for pair in range(lanes // 128):
                sl = pl.ds(pair * 128, 128)
                vals = [ref[sub, sl] for ref in ins]
                if with_prev:
                    _, pullback = jax.vjp(lambda *a: _attn_pair(*a, has_prev), *vals)
                else:
                    _, pullback = jax.vjp(_attn_pair, *vals)
                grads = pullback((do_ref[sub, sl], dl_ref[sub, sl]))
                dq_ref[sub, sl] = grads[0]
                dk_ref[sub_c, sl] += grads[1]
                dv_ref[sub_c, sl] += grads[2]
                if with_prev:
                    dk_ref[sub_p, sl] += grads[3]
                    dv_ref[sub_p, sl] += grads[4]

        _for_residues(r, one)

    operands = (qn, kn, vn, kn, vn) if with_prev else (qn, kn, vn)
    return pl.pallas_call(
        body, name="attn_bwd_%d" % r, grid=(D_ATTN // lanes, SEQ // rows),
        in_specs=[cur, cur, cur] + ([prev, prev] if with_prev else []) + [cur, cur], out_specs=[cur, whole, whole],
        out_shape=[jax.ShapeDtypeStruct((SEQ, D_ATTN), F32)] * 3,
        compiler_params=_params(("parallel", "arbitrary")),
    )(*operands, do, dl)


CONV_COLS = 256
XBC_BLOCK0 = 4096 // CONV_COLS


def _shift_rows(x, s):
    n = x.shape[0]
    t = lax.broadcasted_iota(jnp.int32, x.shape, 0)
    if s >= 0:
        return jnp.where(t >= s, pltpu.roll(x, s, 0), 0.0)
    return jnp.where(t < n + s, pltpu.roll(x, n + s, 0), 0.0)


def _conv_pre(x, w_ref, b_ref):
    pre = b_ref[...] + w_ref[3:4, :] * x
    for k in range(3):
        pre = pre + w_ref[k:k + 1, :] * _shift_rows(x, 3 - k)
    return pre


def _conv_fwd(proj, conv_w, conv_b):
    cols = conv_w.shape[1]

    def body(x_ref, w_ref, b_ref, o_ref):
        pre = _conv_pre(x_ref[...], w_ref, b_ref)
        o_ref[...] = pre * jax.nn.sigmoid(pre)

    blk = pl.BlockSpec((SEQ, CONV_COLS), lambda j: (0, j))
    return pl.pallas_call(
        body, name="conv_fwd", grid=(cols // CONV_COLS,),
        in_specs=[pl.BlockSpec((SEQ, CONV_COLS), lambda j: (0, XBC_BLOCK0 + j)),
                  pl.BlockSpec((4, CONV_COLS), lambda j: (0, j)), pl.BlockSpec((1, CONV_COLS), lambda j: (0, j))],
        out_specs=blk, out_shape=jax.ShapeDtypeStruct((SEQ, cols), F32),
        compiler_params=_params(("parallel",)),
    )(proj, conv_w, conv_b)


def _conv_bwd(proj, conv_w, conv_b, dy):
    cols = conv_w.shape[1]

    def body(x_ref, w_ref, b_ref, dy_ref, dx_ref, dw_ref, db_ref):
        x = x_ref[...]
        pre = _conv_pre(x, w_ref, b_ref)
        sg = jax.nn.sigmoid(pre)
        dpre = dy_ref[...] * (sg * (1.0 + pre * (1.0 - sg)))
        db_ref[...] = jnp.sum(dpre, axis=0, keepdims=True)
        dx = w_ref[3:4, :] * dpre
        dw_ref[3:4, :] = jnp.sum(dpre * x, axis=0, keepdims=True)
        for k in range(3):
            dx = dx + w_ref[k:k + 1, :] * _shift_rows(dpre, k - 3)
            dw_ref[k:k + 1, :] = jnp.sum(dpre * _shift_rows(x, 3 - k), axis=0, keepdims=True)
        dw_ref[4:8, :] = jnp.zeros((4, CONV_COLS), F32)
        dx_ref[...] = dx.astype(dx_ref.dtype)

    blk = pl.BlockSpec((SEQ, CONV_COLS), lambda j: (0, j))
    return pl.pallas_call(
        body, name="conv_bwd", grid=(cols // CONV_COLS,),
        in_specs=[pl.BlockSpec((SEQ, CONV_COLS), lambda j: (0, XBC_BLOCK0 + j)),
                  pl.BlockSpec((4, CONV_COLS), lambda j: (0, j)), pl.BlockSpec((1, CONV_COLS), lambda j: (0, j)), blk],
        out_specs=[blk, pl.BlockSpec((8, CONV_COLS), lambda j: (0, j)), pl.BlockSpec((1, CONV_COLS), lambda j: (0, j))],
        out_shape=[jax.ShapeDtypeStruct((SEQ, cols), BF16), jax.ShapeDtypeStruct((8, cols), F32),
                   jax.ShapeDtypeStruct((1, cols), F32)],
        compiler_params=_params(("parallel",)),
    )(proj, conv_w, conv_b, dy)


HEADS_PER_GROUP = 4


def _ssd_chunk(x0, x1, x2, x3, bm, cm, dtr, bias, alog, dsk, h0, h1, h2, h3):
    xs, hs = (x0, x1, x2, x3), (h0, h1, h2, h3)
    row = lax.broadcasted_iota(jnp.int32, (CHUNK, CHUNK), 0)
    col = lax.broadcasted_iota(jnp.int32, (CHUNK, CHUNK), 1)
    causal = row >= col
    tril = causal.astype(F32)
    z = dtr + bias
    dt = jnp.maximum(z, 0.0) + jnp.log(1.0 + jnp.exp(-jnp.abs(z)))
    a = -jnp.exp(alog)
    acs = _fdot(tril, dt * a, NN)
    acs_t, dt_t = acs.T, dt.T
    cb = _bdot(cm, bm, NT)
    lane = lax.broadcasted_iota(jnp.int32, (1, CHUNK), 1)
    sub = lax.broadcasted_iota(jnp.int32, (CHUNK, 1), 0)
    ys, hn = [], []
    for j in range(HEADS_PER_GROUP):
        on_lane, on_sub = (lane == j).astype(F32), (sub == j).astype(F32)
        acs_c = jnp.sum(acs * on_lane, axis=1, keepdims=True)
        dt_c = jnp.sum(dt * on_lane, axis=1, keepdims=True)
        acs_r = jnp.sum(acs_t * on_sub, axis=0, keepdims=True)
        dt_r = jnp.sum(dt_t * on_sub, axis=0, keepdims=True)
        acs_last = jnp.sum(acs_c * (sub == CHUNK - 1).astype(F32), axis=0, keepdims=True)
        d_j = jnp.sum(dsk * on_lane, axis=1, keepdims=True)
        decay = jnp.exp(jnp.where(causal, acs_c - acs_r, NEG))
        w = cb * decay * dt_r
        y_diag = _bdot(w, xs[j], NN)
        y_off = _bdot(cm, hs[j], NT) * jnp.exp(acs_c)
        ys.append(y_diag + y_off + d_j * xs[j])
        state = _bdot(xs[j] * (jnp.exp(acs_last - acs_c) * dt_c), bm, TN)
        hn.append(hs[j] * jnp.exp(acs_last) + state)
    return (*ys, *hn)


def _ssd_specs(reverse):
    n_chunks = SEQ // CHUNK
    c_of = (lambda c: n_chunks - 1 - c) if reverse else (lambda c: c)
    x_spec = pl.BlockSpec((CHUNK, 256), lambda g, c: (c_of(c), g))
    b_spec = pl.BlockSpec((CHUNK, N_STATE), lambda g, c: (c_of(c), 8 + g))
    c_spec = pl.BlockSpec((CHUNK, N_STATE), lambda g, c: (c_of(c), 12 + g))
    dt_spec = pl.BlockSpec((CHUNK, 128), lambda g, c: (c_of(c), g))
    vec_spec = pl.BlockSpec((1, 128), lambda g, c: (0, g))
    h_spec = pl.BlockSpec((1, 1, HEADS_PER_GROUP, HEAD, N_STATE), lambda g, c: (c_of(c), g, 0, 0, 0))
    return x_spec, b_spec, c_spec, dt_spec, vec_spec, h_spec


def _ssd_fwd(xbc, dt_raw, bias, alog, dsk):
    x_spec, b_spec, c_spec, dt_spec, vec_spec, h_spec = _ssd_specs(False)

    def body(x_ref, b_ref, c_ref, dt_ref, bias_ref, alog_ref, dsk_ref, y_ref, hin_ref, h_scr):
        @pl.when(pl.program_id(1) == 0)
        def _():
            h_scr[...] = jnp.zeros_like(h_scr)

        hs = [h_scr[j] for j in range(HEADS_PER_GROUP)]
        for j in range(HEADS_PER_GROUP):
            hin_ref[0, 0, j] = hs[j]
        xs = [x_ref[:, pl.ds(j * HEAD, HEAD)] for j in range(HEADS_PER_GROUP)]
        res = _ssd_chunk(*xs, b_ref[...], c_ref[...], dt_ref[...], bias_ref[...], alog_ref[...], dsk_ref[...], *hs)
        for j in range(HEADS_PER_GROUP):
            y_ref[:, pl.ds(j * HEAD, HEAD)] = res[j]
            h_scr[j] = res[HEADS_PER_GROUP + j]

    return pl.pallas_call(
        body, name="ssd_fwd", grid=(N_GROUPS, SEQ // CHUNK),
        in_specs=[x_spec, b_spec, c_spec, dt_spec, vec_spec, vec_spec, vec_spec],
        out_specs=[x_spec, h_spec],
        out_shape=[jax.ShapeDtypeStruct((SEQ, D_SSM), F32),
                   jax.ShapeDtypeStruct((SEQ // CHUNK, N_GROUPS, HEADS_PER_GROUP, HEAD, N_STATE), F32)],
        scratch_shapes=[pltpu.VMEM((HEADS_PER_GROUP, HEAD, N_STATE), F32)],
        compiler_params=_params(("parallel", "arbitrary")),
    )(xbc, xbc, xbc, dt_raw, bias, alog, dsk)


def _ssd_bwd(xbc, dt_raw, bias, alog, dsk, h_in, dy):
    x_spec, b_spec, c_spec, dt_spec, vec_spec, h_spec = _ssd_specs(True)
    dxbc_x = pl.BlockSpec((CHUNK, 256), x_spec.index_map)

    def body(x_ref, b_ref, c_ref, dt_ref, bias_ref, alog_ref, dsk_ref, hin_ref, dy_ref,
             dx_ref, db_ref, dc_ref, ddt_ref, dbias_ref, dalog_ref, ddsk_ref, dh_scr):
        first = pl.program_id(1) == 0

        @pl.when(first)
        def _():
            dh_scr[...] = jnp.zeros_like(dh_scr)

        xs = [x_ref[:, pl.ds(j * HEAD, HEAD)] for j in range(HEADS_PER_GROUP)]
        hs = [hin_ref[0, 0, j] for j in range(HEADS_PER_GROUP)]
        cts = [dy_ref[:, pl.ds(j * HEAD, HEAD)] for j in range(HEADS_PER_GROUP)] + [dh_scr[j] for j in range(HEADS_PER_GROUP)]
        _, pullback = jax.vjp(_ssd_chunk, *xs, b_ref[...], c_ref[...], dt_ref[...], bias_ref[...], alog_ref[...],
                              dsk_ref[...], *hs)
        g = pullback(tuple(cts))
        for j in range(HEADS_PER_GROUP):
            dx_ref[:, pl.ds(j * HEAD, HEAD)] = g[j]
            dh_scr[j] = g[10 + j]
        db_ref[...] = g[4]
        dc_ref[...] = g[5]
        ddt_ref[...] = g[6].astype(ddt_ref.dtype)
        for o_ref, val in ((dbias_ref, g[7]), (dalog_ref, g[8]), (ddsk_ref, g[9])):
            @pl.when(first)
            def _(o_ref=o_ref, val=val):
                o_ref[...] = val

            @pl.when(jnp.logical_not(first))
            def _(o_ref=o_ref, val=val):
                o_ref[...] += val

    n_chunks = SEQ // CHUNK
    out_b = pl.BlockSpec((CHUNK, N_STATE), lambda g, c: (n_chunks - 1 - c, g))
    res = pl.pallas_call(
        body, name="ssd_bwd", grid=(N_GROUPS, n_chunks),
        in_specs=[x_spec, b_spec, c_spec, dt_spec, vec_spec, vec_spec, vec_spec, h_spec, x_spec],
        out_specs=[dxbc_x, out_b, out_b, dt_spec, vec_spec, vec_spec, vec_spec],
        out_shape=[jax.ShapeDtypeStruct((SEQ, D_SSM), F32), jax.ShapeDtypeStruct((SEQ, N_GROUPS * N_STATE), F32),
                   jax.ShapeDtypeStruct((SEQ, N_GROUPS * N_STATE), F32), jax.ShapeDtypeStruct((SEQ, DT_PAD), BF16),
                   jax.ShapeDtypeStruct((1, DT_PAD), F32), jax.ShapeDtypeStruct((1, DT_PAD), F32),
                   jax.ShapeDtypeStruct((1, DT_PAD), F32)],
        scratch_shapes=[pltpu.VMEM((HEADS_PER_GROUP, HEAD, N_STATE), F32)],
        compiler_params=_params(("parallel", "arbitrary")),
    )(xbc, xbc, xbc, dt_raw, bias, alog, dsk, h_in, dy)
    return res


CROSS_HEAD = 128
CROSS_ROWS = 512


def _cross_head(q, k, v, gq, gk):
    qn = _rms(q, gq) * (CROSS_HEAD ** -0.5)
    kn = _rms(k, gk)
    s = _bdot(qn, kn, NT)
    p = jnp.exp(s - lax.stop_gradient(jnp.max(s, axis=-1, keepdims=True)))
    return _bdot(p, v, NN) * (1.0 / jnp.sum(p, axis=-1, keepdims=True))


def _cross_specs():
    q_spec = pl.BlockSpec((CROSS_ROWS, CROSS_HEAD), lambda h, i: (i, h))
    k_spec = pl.BlockSpec((N_MEM, CROSS_HEAD), lambda h, i: (0, h))
    v_spec = pl.BlockSpec((N_MEM, CROSS_HEAD), lambda h, i: (0, 4 + h))
    g_spec = pl.BlockSpec((1, CROSS_HEAD), lambda h, i: (0, 0))
    return q_spec, k_spec, v_spec, g_spec


def _cross_fwd(qc, kv, gq, gk):
    q_spec, k_spec, v_spec, g_spec = _cross_specs()

    def body(q_ref, k_ref, v_ref, gq_ref, gk_ref, o_ref):
        o_ref[...] = _cross_head(q_ref[...], k_ref[...], v_ref[...], gq_ref[...], gk_ref[...]).astype(o_ref.dtype)

    return pl.pallas_call(
        body, name="cross_fwd", grid=(4, SEQ // CROSS_ROWS),
        in_specs=[q_spec, k_spec, v_spec, g_spec, g_spec], out_specs=q_spec,
        out_shape=jax.ShapeDtypeStruct((SEQ, D_CROSS), BF16),
        compiler_params=_params(("parallel", "parallel")),
    )(qc, kv, kv, gq, gk)


def _cross_bwd(qc, kv, gq, gk, do):
    q_spec, k_spec, v_spec, g_spec = _cross_specs()

    def body(q_ref, k_ref, v_ref, gq_ref, gk_ref, do_ref, dq_ref, dk_ref, dv_ref, dgq_ref, dgk_ref):
        _, pullback = jax.vjp(_cross_head, q_ref[...], k_ref[...], v_ref[...], gq_ref[...], gk_ref[...])
        dq, dk, dv, dgq, dgk = pullback(do_ref[...].astype(F32))
        dq_ref[...] = dq.astype(dq_ref.dtype)
        row0 = pl.program_id(1) == 0
        all0 = jnp.logical_and(row0, pl.program_id(0) == 0)
        for o_ref, val, init in ((dk_ref, dk, row0), (dv_ref, dv, row0), (dgq_ref, dgq, all0), (dgk_ref, dgk, all0)):
            @pl.when(init)
            def _(o_ref=o_ref, val=val):
                o_ref[...] = val

            @pl.when(jnp.logical_not(init))
            def _(o_ref=o_ref, val=val):
                o_ref[...] += val

    return pl.pallas_call(
        body, name="cross_bwd", grid=(4, SEQ // CROSS_ROWS),
        in_specs=[q_spec, k_spec, v_spec, g_spec, g_spec, q_spec],
        out_specs=[q_spec, k_spec, k_spec, g_spec, g_spec],
        out_shape=[jax.ShapeDtypeStruct((SEQ, D_CROSS), BF16), jax.ShapeDtypeStruct((N_MEM, D_CROSS), F32),
                   jax.ShapeDtypeStruct((N_MEM, D_CROSS), F32), jax.ShapeDtypeStruct((1, CROSS_HEAD), F32),
                   jax.ShapeDtypeStruct((1, CROSS_HEAD), F32)],
        compiler_params=_params(("arbitrary", "arbitrary")),
    )(qc, kv, kv, gq, gk, do)


def _loss_head(y, target):
    tr = 256

    def body(y_ref, t_ref, dy_ref, dyb_ref, loss_ref):
        err = y_ref[...] - t_ref[...]
        dy = err * (1.0 / D_MODEL)
        dy_ref[...] = dy
        dyb_ref[...] = dy.astype(BF16)
        part = jnp.sum(jnp.sum(err * err, axis=1, keepdims=True), axis=0, keepdims=True) * (0.5 / D_MODEL)
        part = jnp.broadcast_to(part, (1, 128))

        @pl.when(pl.program_id(0) == 0)
        def _():
            loss_ref[...] = part

        @pl.when(pl.program_id(0) != 0)
        def _():
            loss_ref[...] += part

    blk = pl.BlockSpec((tr, D_MODEL), lambda i: (i, 0))
    return pl.pallas_call(
        body, name="loss_head", grid=(SEQ // tr,),
        in_specs=[blk, blk], out_specs=[blk, blk, pl.BlockSpec((1, 128), lambda i: (0, 0))],
        out_shape=[jax.ShapeDtypeStruct((SEQ, D_MODEL), F32), jax.ShapeDtypeStruct((SEQ, D_MODEL), BF16),
                   jax.ShapeDtypeStruct((1, 128), F32)],
        compiler_params=_params(("arbitrary",)),
    )(y, target)


def _pad_heads(v):
    return jnp.pad(v.reshape(N_GROUPS, HEADS_PER_GROUP), ((0, 0), (0, 128 - HEADS_PER_GROUP))).reshape(1, DT_PAD)


def _unpad_heads(v):
    return v.reshape(v.shape[0], N_GROUPS, 128)[:, :, :HEADS_PER_GROUP].reshape(v.shape[0], N_DT)


def _rope_tables(positions):
    half = ROT // 2
    inv_freq = ROPE_THETA ** (-2.0 * jnp.arange(half, dtype=F32) / ROT)
    ang = positions.reshape(SEQ, 1).astype(F32) * inv_freq
    cos, sin = jnp.cos(ang), jnp.sin(ang)
    ones, zeros = jnp.ones((SEQ, HEAD - ROT), F32), jnp.zeros((SEQ, HEAD - ROT), F32)
    cos_h = jnp.concatenate([cos, cos, ones], axis=1)
    sin_h = jnp.concatenate([-sin, sin, zeros], axis=1)
    return jnp.tile(cos_h, (1, 2)), jnp.tile(sin_h, (1, 2))


def _add_res(acc, res):
    return (acc + res,)


def _take_token(grads):
    token = getattr(grads, "token", None)
    if token is None:
        return ()
    grads.token = None
    return (token,)


def _local_step(x, mem, positions, target, p, w, more_weights=None, grads=None, h=None):
    grads = {} if grads is None else grads
    w = dict(w)
    cos, sin = _rope_tables(positions)
    gq2, gk2 = jnp.tile(p["g_q"], (1, 2)), jnp.tile(p["g_k"], (1, 2))
    bias, alog, dsk = _pad_heads(p["dt_bias"]), _pad_heads(p["a_log"]), _pad_heads(p["d_skip"])
    norm_out = [(D_MODEL, BF16, D_MODEL, 0, False)]

    if h is None:
        h = _rowwise(_norm_fn, [_full(x)], [_full(p["g_mix"])], norm_out, name="norm_in")[0]
    proj = _matmul(h, w["w_main"], mode="nn", name="in_proj", outs=[F32])
    dt_raw = _matmul(h, w["w_dt"], mode="nn", name="dt_proj", outs=[F32])
    qk_rows = [(proj, 128, 0, True), (proj, 128, 8, True), (proj, 128, 16, True), _full(cos), _full(sin)]
    qk_vecs = [_full(gq2), _full(gk2)]
    qn, kn, vn = _rowwise(_qk_fn, qk_rows, qk_vecs, [(D_ATTN, F32, 128, 0, True)] * 3, name="qk_prep", groups=8, tr=1024)
    branches = [_attention_fwd(qn, kn, vn, b) for b in range(3)]
    merge_rows = [_full(o) for o, _ in branches] + [_full(lse) for _, lse in branches]
    attn = _rowwise(_merge_fn, merge_rows, [_full(p["g_attn_out"])], [(D_ATTN, BF16, D_ATTN, 0, False)], name="attn_merge")[0]
    xbc = _conv_fwd(proj, p["conv_w"], p["conv_b"])
    y_ssd, h_in = _ssd_fwd(xbc, dt_raw, bias, alog, dsk)
    gate_rows = [(y_ssd, 256, 0, True), (proj, 256, 12, True)]
    gate_vecs = [(p["g_ssm_out"], 256, 0, True)]
    ssm = _rowwise(_gate_fn, gate_rows, gate_vecs, [(D_SSM, BF16, 256, 0, True)], name="ssm_gate", groups=4)[0]
    mix = jnp.concatenate([attn, ssm], axis=1)
    if more_weights is not None:
        w.update(more_weights("mixer_done", mix))
    x1 = _matmul(mix, w["w_out"], mode="nn", name="out_proj", outs=[F32], extra=(x,), epilogue=_add_res)
    hc = _rowwise(_norm_fn, [_full(x1)], [_full(p["g_cross"])], norm_out, name="norm_cross")[0]
    memh = _rowwise(_norm_fn, [_full(mem)], [_full(p["g_mem"])], norm_out, name="norm_mem", n_rows=N_MEM)[0]
    qc = _matmul(hc, w["w_cq"], mode="nn", name="cq_proj", outs=[F32])
    kv = _matmul(memh, w["w_ckv"], mode="nn", name="ckv_proj", outs=[F32])
    oc = _cross_fwd(qc, kv, p["g_cq"], p["g_ck"])
    x2 = _matmul(oc, w["w_co"], mode="nn", name="co_proj", outs=[F32], extra=(x1,), epilogue=_add_res)
    hm = _rowwise(_norm_fn, [_full(x2)], [_full(p["g_mlp"])], norm_out, name="norm_mlp")[0]
    if more_weights is not None:
        w.update(more_weights("cross_done", hm))
    u, act = _matmul(hm, w["w_up"], mode="nn", name="up_proj", outs=[F32, BF16],
                     epilogue=lambda acc: (acc, jnp.square(jnp.maximum(acc, 0.0))))
    x3 = _matmul(act, w["w_down"], mode="nn", name="down_proj", outs=[F32], extra=(x2,), epilogue=_add_res)
    dy, dyb, loss = _loss_head(x3, target)

    grads["w_down"] = _matmul(act, dyb, mode="tn", name="dw_down", outs=[BF16])
    du = _matmul(dyb, w["w_down"], mode="nt", name="d_act", outs=[BF16], extra=(u,), after=_take_token(grads),
                 epilogue=lambda acc, uu: (acc * (2.0 * jnp.maximum(uu, 0.0)),))
    grads["w_up"] = _matmul(hm, du, mode="tn", name="dw_up", outs=[BF16], col_shards=4)
    dhm = _matmul(du, w["w_up"], mode="nt", name="d_hm", outs=[F32], after=_take_token(grads))
    dx2, grads["g_mlp"] = _rowwise_vjp(
        _norm_fn, [_full(x2)], [_full(p["g_mlp"])], [[_full(dhm)]],
        [(0, D_MODEL, F32, D_MODEL, 0, False, _full(dy))], [(0, D_MODEL, D_MODEL, 0, False)], name="norm_mlp_bwd")
    grads["w_co"] = _matmul(oc, dx2, mode="tn", name="dw_co", outs=[BF16], col_shards=4)
    doc = _matmul(dx2, w["w_co"], mode="nt", name="d_oc", outs=[BF16])
    dqc, dkc, dvc, grads["g_cq"], grads["g_ck"] = _cross_bwd(qc, kv, p["g_cq"], p["g_ck"], doc)
    grads["w_cq"] = _matmul(hc, dqc, mode="tn", name="dw_cq", outs=[BF16])
    dhc = _matmul(dqc, w["w_cq"], mode="nt", name="d_hc", outs=[F32])
    dkv = jnp.concatenate([dkc, dvc], axis=1)
    grads["w_ckv"] = _matmul(memh, dkv, mode="tn", name="dw_ckv", outs=[BF16])
    dmemh = _matmul(dkv, w["w_ckv"], mode="nt", name="d_memh", outs=[F32])
    grads["g_mem"] = _rowwise_vjp(_norm_fn, [_full(mem)], [_full(p["g_mem"])], [[_full(dmemh)]], [],
                                  [(0, D_MODEL, D_MODEL, 0, False)], name="norm_mem_bwd", n_rows=N_MEM)[0]
    dx1, grads["g_cross"] = _rowwise_vjp(
        _norm_fn, [_full(x1)], [_full(p["g_cross"])], [[_full(dhc)]],
        [(0, D_MODEL, F32, D_MODEL, 0, False, _full(dx2))], [(0, D_MODEL, D_MODEL, 0, False)], name="norm_cross_bwd")
    grads["w_out"] = _matmul(mix, dx1, mode="tn", name="dw_out", outs=[BF16])
    dmix = _matmul(dx1, w["w_out"], mode="nt", name="d_mix", outs=[F32], after=_take_token(grads))
    merge_grads = [(i, D_ATTN, F32, D_ATTN, 0, False, None) for i in range(6)]
    *dol, grads["g_attn_out"] = _rowwise_vjp(
        _merge_fn, merge_rows, [_full(p["g_attn_out"])], [[(dmix, D_ATTN, 0, False)]],
        merge_grads, [(0, D_ATTN, D_ATTN, 0, False)], name="attn_merge_bwd")
    dqkv = [_attention_bwd(qn, kn, vn, dol[b], dol[3 + b], b) for b in range(3)]
    qk_cts = [[(dqkv[b][i], 128, 0, True) for b in range(3)] for i in range(3)]
    dq, dk, dv, dgq2, dgk2 = _rowwise_vjp(
        _qk_fn, qk_rows, qk_vecs, qk_cts, [(i, D_ATTN, BF16, 128, 0, True, None) for i in range(3)],
        [(0, 128, 128, 0, False), (1, 128, 128, 0, False)], name="qk_prep_bwd", groups=8, tr=512)
    grads["g_q"] = dgq2[:, :HEAD] + dgq2[:, HEAD:]
    grads["g_k"] = dgk2[:, :HEAD] + dgk2[:, HEAD:]
    dy_ssd, dz, grads["g_ssm_out"] = _rowwise_vjp(
        _gate_fn, gate_rows, gate_vecs, [[(dmix, 256, 4, True)]],
        [(0, D_SSM, F32, 256, 0, True, None), (1, D_SSM, BF16, 256, 0, True, None)],
        [(0, D_SSM, 256, 0, True)], name="ssm_gate_bwd", groups=4)
    dxs, db, dc, ddt, dbias, dalog, ddsk = _ssd_bwd(xbc, dt_raw, bias, alog, dsk, h_in, dy_ssd)
    grads["dt_bias"], grads["a_log"], grads["d_skip"] = _unpad_heads(dbias), _unpad_heads(dalog), _unpad_heads(ddsk)
    dxbc_raw, dconv_w, grads["conv_b"] = _conv_bwd(proj, p["conv_w"], p["conv_b"], jnp.concatenate([dxs, db, dc], axis=1))
    grads["conv_w"] = dconv_w[:4]
    dproj = jnp.concatenate([dq, dk, dv, dz, dxbc_raw], axis=1)
    grads["w_main"] = _matmul(h, dproj, mode="tn", name="dw_main", outs=[BF16])
    grads["w_dt"] = _matmul(h, ddt, mode="tn", name="dw_dt", outs=[BF16])
    dh = _matmul(dproj, w["w_main"], mode="nt", name="d_h_main", outs=[F32], after=_take_token(grads))
    dh = _matmul(ddt, w["w_dt"], mode="nt", name="d_h_dt", outs=[F32], extra=(dh,), epilogue=_add_res)
    grad_x, grads["g_mix"] = _rowwise_vjp(
        _norm_fn, [_full(x)], [_full(p["g_mix"])], [[_full(dh)]],
        [(0, D_MODEL, F32, D_MODEL, 0, False, _full(dx1))], [(0, D_MODEL, D_MODEL, 0, False)], name="norm_in_bwd")
    return loss, grad_x, grads


MATRICES = ("w_in", "w_out", "w_cq", "w_ckv", "w_co", "w_up", "w_down")
ROW_SHARDED = ("w_out", "w_cq", "w_ckv", "w_down")
N_CHIPS = 4
ANY = pl.BlockSpec(memory_space=pl.ANY)


def _place():
    return lax.axis_index("x"), lax.axis_index("y"), lax.axis_index("c")


def _other_chips(x, y):
    return [(1 - x, y), (x, 1 - y), (1 - x, 1 - y)]


def _remote(src, dst, send_sem, recv_sem, device):
    return pltpu.make_async_remote_copy(src_ref=src, dst_ref=dst, send_sem=send_sem, recv_sem=recv_sem,
                                        device_id=device, device_id_type=MESH)


def _gathered_shape(name, shard):
    rows, cols = shard.shape
    if name == "w_in":
        return (N_CHIPS, rows, cols)
    return (N_CHIPS * rows, cols) if name in ROW_SHARDED else (rows, N_CHIPS * cols)


def _shard_window(name, ref, rows, cols, chip, half):
    r0, nr = (0, rows) if half is None else (half * (rows // 2), rows // 2)
    if name == "w_in":
        return ref.at[chip, pl.ds(r0, nr), :]
    if name in ROW_SHARDED:
        return ref.at[pl.ds(chip * rows + r0, nr), :]
    return ref.at[pl.ds(r0, nr), pl.ds(pl.multiple_of(chip * cols, 128), cols)]


def _cast_into_gathered(w, name, chip, after=()):
    rows, cols = w.shape
    tr = _tile(rows, ROW_TILE)

    def body(chip_ref, w_ref, *rest):
        rest[-1][...] = w_ref[...].astype(BF16)

    if name == "w_in":
        out_spec = pl.BlockSpec((None, tr, cols), lambda i, chip_ref: (chip_ref[0], i, 0))
    elif name in ROW_SHARDED:
        out_spec = pl.BlockSpec((tr, cols), lambda i, chip_ref: (chip_ref[0] * (rows // tr) + i, 0))
    else:
        out_spec = pl.BlockSpec((tr, cols), lambda i, chip_ref: (i, chip_ref[0]))
    grid_spec = pltpu.PrefetchScalarGridSpec(
        num_scalar_prefetch=1, grid=(rows // tr,),
        in_specs=[pl.BlockSpec((tr, cols), lambda i, chip_ref: (i, 0))] + [pl.BlockSpec(memory_space=pl.ANY)] * len(after),
        out_specs=out_spec)
    return pl.pallas_call(body, name="cast_" + name, grid_spec=grid_spec,
                          out_shape=jax.ShapeDtypeStruct(_gathered_shape(name, w), BF16),
                          compiler_params=_params(("parallel",)))(chip.reshape(1).astype(jnp.int32), w, *after)


HBM = pl.BlockSpec(memory_space=pltpu.HBM)
SEM = pl.BlockSpec(memory_space=pltpu.SEMAPHORE)
EFFECT = pltpu.SideEffectType.DATAFLOW_SIDE_EFFECTING


def _split_start(name, bufs, plan, counts, after=()):
    n, n_g, n_after = len(bufs), len(counts), len(after)

    def body(*refs):
        ins, sems, token = refs[:n], refs[n + n_after:n + n_after + 2 * n_g], refs[-1]
        for g, copies in enumerate(plan(ins)):
            for i, (src, dst, device, _) in enumerate(copies):
                _remote(src, dst, sems[2 * g].at[i], sems[2 * g + 1].at[i], device).start()
        token[...] = jnp.zeros_like(token)

    sem_shapes = [pltpu.SemaphoreType.DMA((cnt,)) for cnt in counts for _ in range(2)]
    res = pl.pallas_call(
        body, name=name,
        out_shape=(*sem_shapes, *[pltpu.HBM(b.shape, b.dtype) for b in bufs], jax.ShapeDtypeStruct((8, 128), F32)),
        in_specs=(*(HBM,) * n, *(ANY,) * n_after),
        out_specs=(*(SEM,) * (2 * n_g), *(HBM,) * n, pl.BlockSpec(memory_space=pltpu.VMEM)),
        input_output_aliases={i: 2 * n_g + i for i in range(n)},
        compiler_params=pltpu.CompilerParams(has_side_effects=EFFECT),
    )(*[pltpu.with_memory_space_constraint(b, pltpu.HBM) for b in bufs], *after)
    sems = [(res[2 * g], res[2 * g + 1]) for g in range(n_g)]
    return sems, list(res[2 * n_g:2 * n_g + n]), res[-1]


def _split_wait(name, bufs, sems, plan, *after):
    n = len(bufs)

    def body(*refs):
        ins, send, recv = refs[:n], refs[n], refs[n + 1]
        (copies,) = plan(ins)
        for i, (src, _, device, landing) in enumerate(copies):
            cp = _remote(src, landing, send.at[i], recv.at[i], device)
            cp.wait_send()
            cp.wait_recv()

    res = pl.pallas_call(
        body, name=name, out_shape=tuple(pltpu.HBM(b.shape, b.dtype) for b in bufs),
        in_specs=(*(HBM,) * n, SEM, SEM, *(ANY,) * len(after)), out_specs=(HBM,) * n,
        input_output_aliases={i: i for i in range(n)},
        compiler_params=pltpu.CompilerParams(has_side_effects=EFFECT),
    )(*bufs, sems[0], sems[1], *after)
    return list(res)


def _ici_plan(names, shard_shapes):
    def plan(refs):
        x, y, c = _place()
        copies = []
        for ref, name in zip(refs, names):
            win = _shard_window(name, ref, *shard_shapes[name], 2 * x + y, c)
            for px, py in _other_chips(x, y):
                copies.append((win, win, (px, py, c), _shard_window(name, ref, *shard_shapes[name], 2 * px + py, c)))
        return [copies]
    return plan


def _pass_on_plan(names, shard_shapes):
    def plan(refs):
        x, y, c = _place()
        copies = []
        for ref, name in zip(refs, names):
            for px, py in _other_chips(x, y):
                win = _shard_window(name, ref, *shard_shapes[name], 2 * px + py, c)
                copies.append((win, win, (x, y, 1 - c), _shard_window(name, ref, *shard_shapes[name], 2 * px + py, 1 - c)))
        return [copies]
    return plan


def _scatter_plan(n_pairs):
    def plan(refs):
        x, y, c = _place()
        copies = []
        for src, dst in zip(refs[:n_pairs], refs[n_pairs:]):
            for k, (px, py) in enumerate(_other_chips(x, y)):
                copies.append((src.at[2 * px + py], dst.at[k], (px, py, c), dst.at[k]))
        return [copies]
    return plan


def _sibling_swap(arrs, name):
    n = len(arrs)

    def body(*refs):
        ins, outs, send, recv = refs[:n], refs[n:2 * n], refs[2 * n], refs[2 * n + 1]
        x, y, c = _place()
        cps = [_remote(ins[w].at[:, 1 - c], outs[w], send.at[w], recv.at[w], (x, y, 1 - c)) for w in range(n)]
        for cp in cps:
            cp.start()
        for cp in cps:
            cp.wait()

    return pl.pallas_call(
        body, name=name, in_specs=[ANY] * n, out_specs=[ANY] * n,
        out_shape=[jax.ShapeDtypeStruct((a.shape[0],) + a.shape[2:], a.dtype) for a in arrs],
        scratch_shapes=[pltpu.SemaphoreType.DMA((n,))] * 2,
    )(*arrs)


def _sibling_share(arrs):
    n = len(arrs)

    def body(*refs):
        ins, outs, send, recv = refs[:n], refs[n:2 * n], refs[2 * n], refs[2 * n + 1]
        x, y, c = _place()
        cps = [_remote(ins[w], outs[w], send.at[w], recv.at[w], (x, y, 1 - c)) for w in range(n)]
        for cp in cps:
            cp.start()
        for cp in cps:
            cp.wait()

    return pl.pallas_call(
        body, name="grad_sibling_share", in_specs=[ANY] * n, out_specs=[ANY] * n,
        out_shape=[jax.ShapeDtypeStruct(a.shape, a.dtype) for a in arrs],
        scratch_shapes=[pltpu.SemaphoreType.DMA((n,))] * 2,
    )(*arrs)


def _small_allreduce(buf, name):
    rows = buf.shape[0]

    def body(x_ref, out_ref, all_ref, send_sems, recv_sems, local_sem):
        x, y, c = _place()
        me, sibling, chips = (x, y, c), (x, y, 1 - c), _other_chips(x, y)

        def block(px, py, pc):
            return all_ref.at[pl.ds((4 * px + 2 * py + pc) * rows, rows), :]

        def copy(k, blk, to, src=None):
            return _remote(block(*blk) if src is None else src, block(*blk), send_sems.at[k], recv_sems.at[k], to)

        own = pltpu.make_async_copy(x_ref, block(*me), local_sem)
        own.start()
        first = [copy(0, me, sibling, src=x_ref)] + [copy(1 + j, me, (*chip, c), src=x_ref) for j, chip in enumerate(chips)]
        for cp in first:
            cp.start()
        passed = [copy(4 + j, (*chip, c), sibling) for j, chip in enumerate(chips)]
        for j, chip in enumerate(chips):
            copy(1 + j, (*chip, c), me).wait_recv()
            passed[j].start()
        copy(0, sibling, me).wait_recv()
        for j, chip in enumerate(chips):
            copy(4 + j, (*chip, 1 - c), me).wait_recv()
        for cp in first + passed:
            cp.wait_send()
        own.wait()
        acc = all_ref[pl.ds(0, rows), :]
        for d in range(1, 8):
            acc = acc + all_ref[pl.ds(d * rows, rows), :]
        out_ref[...] = acc

    vmem = pl.BlockSpec(memory_space=pltpu.VMEM)
    return pl.pallas_call(
        body, name=name, in_specs=[vmem], out_specs=vmem,
        out_shape=jax.ShapeDtypeStruct(buf.shape, F32),
        scratch_shapes=[pltpu.VMEM((8 * rows, 128), F32), pltpu.SemaphoreType.DMA((7,)), pltpu.SemaphoreType.DMA((7,)),
                        pltpu.SemaphoreType.DMA],
    )(buf)


ROW_TILE = 256


def _add_halves(arr, recv, c, name):
    _, _, hr, cols = arr.shape
    tr = _tile(hr, ROW_TILE)

    def body(c_ref, a_ref, r_ref, o_ref):
        o_ref[...] = (a_ref[...].astype(F32) + r_ref[...].astype(F32)).astype(o_ref.dtype)

    piece = pl.BlockSpec((None, tr, cols), lambda j, i, c_ref: (j, i, 0))
    grid_spec = pltpu.PrefetchScalarGridSpec(
        num_scalar_prefetch=1, grid=(N_CHIPS, hr // tr),
        in_specs=[pl.BlockSpec((None, None, tr, cols), lambda j, i, c_ref: (j, c_ref[0], i, 0)), piece], out_specs=piece)
    return pl.pallas_call(body, name=name, grid_spec=grid_spec, out_shape=jax.ShapeDtypeStruct(recv.shape, BF16),
                          compiler_params=_params(("parallel", "parallel")))(c.reshape(1).astype(jnp.int32), arr, recv)


def _flip_slot(d):
    return jnp.where(d == 1, 1, jnp.where(d == 3, 2, 0))


def _sum_chips(p, q, chip, name):
    _, hr, cols = p.shape
    tr = _tile(hr, ROW_TILE)

    def body(chip_ref, p_ref, q_ref, o_ref):
        j = pl.program_id(1)
        term = jnp.where(j == chip_ref[0], p_ref[...].astype(F32), q_ref[...].astype(F32))

        @pl.when(j == 0)
        def _():
            o_ref[...] = term

        @pl.when(j != 0)
        def _():
            o_ref[...] += term

    grid_spec = pltpu.PrefetchScalarGridSpec(
        num_scalar_prefetch=1, grid=(hr // tr, N_CHIPS),
        in_specs=[pl.BlockSpec((None, tr, cols), lambda i, j, chip_ref: (chip_ref[0], i, 0)),
                  pl.BlockSpec((None, tr, cols), lambda i, j, chip_ref: (_flip_slot(j ^ chip_ref[0]), i, 0))],
        out_specs=pl.BlockSpec((tr, cols), lambda i, j, chip_ref: (i, 0)))
    return pl.pallas_call(body, name=name, grid_spec=grid_spec, out_shape=jax.ShapeDtypeStruct((hr, cols), F32),
                          compiler_params=_params(("parallel", "arbitrary")))(chip.reshape(1).astype(jnp.int32), p, q)


def _adamw_halves(w, g_own, g_other, m, v, c, name):
    rows, cols = w.shape
    tr = _tile(rows // 2, ROW_TILE)
    per_half = rows // 2 // tr

    def body(c_ref, w_ref, own_ref, other_ref, m_ref, v_ref, g_ref, d_ref, nm_ref, nv_ref):
        mine = (pl.program_id(0) // per_half) == c_ref[0]
        g_ = jnp.where(mine, own_ref[...], other_ref[...])
        g_ref[...] = g_
        d_ref[...], nm_ref[...], nv_ref[...] = _adamw_math(w_ref[...], g_, m_ref[...], v_ref[...])

    blk = pl.BlockSpec((tr, cols), lambda i, c_ref: (i, 0))
    half = pl.BlockSpec((tr, cols), lambda i, c_ref: (i % per_half, 0))
    grid_spec = pltpu.PrefetchScalarGridSpec(num_scalar_prefetch=1, grid=(rows // tr,),
                                             in_specs=[blk, half, half, blk, blk], out_specs=[blk] * 4)
    return pl.pallas_call(body, name=name, grid_spec=grid_spec, out_shape=[jax.ShapeDtypeStruct(w.shape, F32)] * 4,
                          compiler_params=_params(("parallel",)))(c.reshape(1).astype(jnp.int32), w, g_own, g_other, m, v)


def _adamw_math(w, g, m, v):
    m_new = ADAM_B1 * m + (1.0 - ADAM_B1) * g
    v_new = ADAM_B2 * v + (1.0 - ADAM_B2) * (g * g)
    m_hat = m_new / (1.0 - ADAM_B1 ** ADAM_STEP)
    v_hat = v_new / (1.0 - ADAM_B2 ** ADAM_STEP)
    return -ADAM_LR * (m_hat / (jnp.sqrt(v_hat) + ADAM_EPS) + ADAM_WD * w), m_new, v_new


def _adamw(w, g, m, v, name):
    rows, cols = w.shape
    tr = _tile(rows, ROW_TILE)

    def body(w_ref, g_ref, m_ref, v_ref, d_ref, nm_ref, nv_ref):
        d_ref[...], nm_ref[...], nv_ref[...] = _adamw_math(w_ref[...], g_ref[...], m_ref[...], v_ref[...])

    blk = pl.BlockSpec((tr, cols), lambda i: (i, 0))
    return pl.pallas_call(body, name=name, grid=(rows // tr,), in_specs=[blk] * 4, out_specs=[blk] * 3,
                          out_shape=[jax.ShapeDtypeStruct(w.shape, F32)] * 3, compiler_params=_params(("parallel",)))(w, g, m, v)


VECTORS = ("g_mix", "g_q", "g_k", "g_attn_out", "conv_b", "dt_bias", "a_log", "d_skip", "g_ssm_out", "g_cross", "g_mem",
           "g_cq", "g_ck", "g_mlp")
WEIGHTS = ("g_mix", "w_in", "g_q", "g_k", "g_attn_out", "conv_w", "conv_b", "dt_bias", "a_log", "d_skip", "g_ssm_out", "w_out",
           "g_cross", "g_mem", "w_cq", "w_ckv", "g_cq", "g_ck", "w_co", "g_mlp", "w_up", "w_down")


def _pack(parts):
    flat = jnp.concatenate([t.reshape(-1) for t in parts])
    total = -(-flat.shape[0] // 1024) * 1024
    return jnp.pad(flat, (0, total - flat.shape[0])).reshape(total // 128, 128)


def _unpack(buf, shapes):
    flat, out, pos = buf.reshape(-1), [], 0
    for shape in shapes:
        size = math.prod(shape)
        out.append(flat[pos:pos + size].reshape(shape))
        pos += size
    return out


def kernel(x, mem, positions, g_mix, w_in, g_q, g_k, g_attn_out, conv_w, conv_b, dt_bias, a_log, d_skip, g_ssm_out, w_out, g_cross, g_mem, w_cq, w_ckv, g_cq, g_ck, w_co, g_mlp, w_up, w_down, loss_target, m_g_mix, m_w_in, m_g_q, m_g_k, m_g_attn_out, m_conv_w, m_conv_b, m_dt_bias, m_a_log, m_d_skip, m_g_ssm_out, m_w_out, m_g_cross, m_g_mem, m_w_cq, m_w_ckv, m_g_cq, m_g_ck, m_w_co, m_g_mlp, m_w_up, m_w_down, v_g_mix, v_w_in, v_g_q, v_g_k, v_g_attn_out, v_conv_w, v_conv_b, v_dt_bias, v_a_log, v_d_skip, v_g_ssm_out, v_w_out, v_g_cross, v_g_mem, v_w_cq, v_w_ckv, v_g_cq, v_g_ck, v_w_co, v_g_mlp, v_w_up, v_w_down):
    args = dict(locals())
    weights = {n: args[n][0] for n in WEIGHTS}
    mom_m = {n: args["m_" + n][0] for n in WEIGHTS}
    mom_v = {n: args["v_" + n][0] for n in WEIGHTS}
    x_idx, y_idx, c_idx = _place()
    chip = 2 * x_idx + y_idx

    conv_parts = _small_allreduce(_pack([jnp.zeros((N_CHIPS, 4, 512), F32).at[chip].set(0.5 * weights["conv_w"])]),
                                  "gather_conv_taps")
    shapes = {n: weights[n].shape for n in MATRICES}
    first, mid, late = ("w_in",), ("w_out", "w_cq", "w_ckv", "w_co"), ("w_up", "w_down")
    w_in_buf = [_cast_into_gathered(weights["w_in"], "w_in", chip)]
    sems_in, w_in_buf, token = _split_start("gather_ici_start_w_in", w_in_buf, _ici_plan(first, shapes), [3], after=(conv_parts,))
    bufs = [_cast_into_gathered(weights[n], n, chip, after=(token,)) for n in mid + late]
    sems_rest, bufs, token = _split_start("gather_ici_start_rest", bufs, _ici_plan(mid + late, shapes), [18], after=(token,))
    ici_sems = (sems_in[0], sems_rest[0])
    params = {n: weights[n].reshape(1, -1) for n in VECTORS}
    h_in = _rowwise(_norm_fn, [_full(x[0])], [_full(params["g_mix"])], [(D_MODEL, BF16, D_MODEL, 0, False)], name="norm_in",
                    after=(token,))[0]
    w_in_buf = _split_wait("gather_ici_wait_w_in", w_in_buf, ici_sems[0], _ici_plan(first, shapes), token, h_in)
    pass_sems, w_in_buf, token = _split_start("gather_pass_start_w_in", w_in_buf, _pass_on_plan(first, shapes), [3])
    w_in_buf = _split_wait("gather_pass_wait_w_in", w_in_buf, pass_sems[0], _pass_on_plan(first, shapes), token)
    w_in_full = jnp.transpose(w_in_buf[0], (1, 0, 2)).reshape(D_MODEL, D_MAIN + N_DT)
    full = {"w_main": w_in_full[:, :D_MAIN],
            "w_dt": jnp.pad(w_in_full[:, D_MAIN:].reshape(D_MODEL, N_GROUPS, HEADS_PER_GROUP),
                            ((0, 0), (0, 0), (0, 128 - HEADS_PER_GROUP))).reshape(D_MODEL, DT_PAD)}
    in_flight = {}

    def more_weights(stage, after):
        if stage == "mixer_done":
            rest = _split_wait("gather_ici_wait_rest", bufs, ici_sems[1], _ici_plan(mid + late, shapes), after)
            plan = lambda refs: _pass_on_plan(mid, shapes)(refs[:4]) + _pass_on_plan(late, shapes)(refs[4:])
            sems, rest, token = _split_start("gather_pass_start_rest", rest, plan, [12, 6])
            in_flight["late"] = (rest[4:], sems[1])
            return dict(zip(mid, _split_wait("gather_pass_wait_mid", rest[:4], sems[0], _pass_on_plan(mid, shapes), token)))
        late_bufs, sems = in_flight.pop("late")
        return dict(zip(late, _split_wait("gather_pass_wait_late", late_bufs, sems, _pass_on_plan(late, shapes), after)))

    params["conv_w"] = _unpack(conv_parts, [(N_CHIPS, 4, 512)])[0].transpose(1, 0, 2).reshape(4, 4 * 512)

    groups = (("w_down",), ("w_up",), ("w_co", "w_cq", "w_ckv", "w_out"), ("w_in",))
    scattered = []

    class GradStore(dict):
        def __setitem__(self, name, value):
            super().__setitem__(name, value)
            if "w_main" in self and "w_dt" in self and "w_in" not in self:
                gw_in = jnp.concatenate([self["w_main"], _unpad_heads(self["w_dt"])], axis=1)
                self["w_in"] = gw_in.reshape(D_MODEL, N_CHIPS, gw_in.shape[1] // N_CHIPS).transpose(1, 0, 2)
            for group in groups:
                if name in group and all(n in self for n in group):
                    pieces = [self[n].reshape(N_CHIPS, 2, shapes[n][0] // 2, shapes[n][1]) for n in group]
                    from_sibling = _sibling_swap(pieces, "grad_swap_" + group[0])
                    sums = [_add_halves(a, r, c_idx, "add_halves_" + n) for n, a, r in zip(group, pieces, from_sibling)]
                    landing = [lax.empty((3,) + s.shape[1:], BF16) for s in sums]
                    sems, thru, self.token = _split_start("grad_scatter_start_" + group[0], sums + landing,
                                                          _scatter_plan(len(sums)), [3 * len(sums)])
                    scattered.append((group, sems[0], thru))

    loss, grad_x, grads = _local_step(x[0], mem[0], positions[0], loss_target[0], params, full, more_weights, GradStore(),
                                      h_in)

    halves = {}
    for group, sems, thru in scattered:
        thru = _split_wait("grad_scatter_wait_" + group[0], thru, sems, _scatter_plan(len(group)), grad_x)
        for i, n in enumerate(group):
            halves[n] = _sum_chips(thru[i], thru[len(group) + i], chip, "sum_chips_" + n)
    other_halves = dict(zip(MATRICES, _sibling_share([halves[n] for n in MATRICES])))
    out_g, out_d, out_m, out_v = {}, {}, {}, {}
    for n in MATRICES:
        out_g[n], out_d[n], out_m[n], out_v[n] = _adamw_halves(weights[n], halves[n], other_halves[n], mom_m[n], mom_v[n],
                                                               c_idx, "adamw_" + n)

    small = [grads[n] for n in VECTORS] + [grads["conv_w"]]
    summed = _unpack(_small_allreduce(_pack(small), "allreduce_vectors"), [t.shape for t in small])
    g_small = dict(zip(VECTORS, summed[:-1]))
    g_small["conv_w"] = lax.dynamic_slice_in_dim(summed[-1], chip * 512, 512, axis=1)
    names = VECTORS + ("conv_w",)
    shapes = [weights[n].shape for n in names]
    packed = [_pack([src[n] for n in names]) for src in (weights, g_small, mom_m, mom_v)]
    small_out = [_unpack(t, shapes) for t in _adamw(*packed, "adamw_small")]
    for i, n in enumerate(names):
        out_g[n] = g_small[n].reshape(shapes[i])
        out_d[n], out_m[n], out_v[n] = small_out[0][i], small_out[1][i], small_out[2][i]

    total_loss = lax.psum(loss[0, 0], ("x", "y", "c"))
    outs = [total_loss, grad_x[None]]
    for group in (out_g, out_d, out_m, out_v):
        outs += [group[n][None] for n in WEIGHTS]
    return tuple(outs)
```

```python
import functools
import math

import jax
import jax.numpy as jnp
from jax import lax
from jax.experimental import pallas as pl
from jax.experimental.pallas import tpu as pltpu

F32 = jnp.float32
BF16 = jnp.bfloat16

SEQ = 2048
D_MODEL = 2048
HEAD = 64
D_ATTN = 1024
D_SSM = 1024
N_GROUPS = 4
N_STATE = 128
CHUNK = 128
ATT_BLK = 128
N_MEM = 256
D_CROSS = 512
D_FF = 8192
D_MAIN = 6144
N_DT = 16
DT_PAD = 512
ROT = 16
ROPE_THETA = 500000.0
EPS = 1e-6
NEG = -1e30
BRANCH_BLOCKS = (16, 4, 1)
DILATIONS = (1, 4, 16)

ADAM_LR, ADAM_B1, ADAM_B2, ADAM_EPS, ADAM_WD, ADAM_STEP = 0.001, 0.9, 0.999, 1e-08, 0.01, 10

VMEM_LIMIT = 56 * 1024 * 1024
MESH = pl.DeviceIdType.MESH


def _params(sem, **kw):
    return pltpu.CompilerParams(dimension_semantics=sem, vmem_limit_bytes=VMEM_LIMIT, **kw)


def _bdot(a, b, dims):
    return lax.dot_general(a.astype(BF16), b.astype(BF16), (dims, ((), ())), preferred_element_type=F32)


def _fdot(a, b, dims):
    return lax.dot_general(a, b, (dims, ((), ())), preferred_element_type=F32, precision=lax.Precision.HIGHEST)


NN = ((1,), (0,))
NT = ((1,), (1,))
TN = ((0,), (0,))


def _tile(n, want):
    t = min(n, want)
    while n % t:
        t //= 2
    return t


def _matmul(a, b, *, mode, name, outs, extra=(), epilogue=None, col_shards=1, after=(), n_cols=None, out_cols=None,
            tm=1024, tn=1024, tk=2048):
    if mode == "nn":
        (m, k), n = a.shape, b.shape[1]
    elif mode == "nt":
        (m, k), n = a.shape, b.shape[0]
    else:
        (k, m), n = a.shape, b.shape[1]
    n = n if n_cols is None else n_cols
    tm, tn, tk = _tile(m, tm), _tile(n // col_shards, tn), _tile(k, tk)
    nk = k // tk
    per_shard = n // col_shards // tn
    dims = {"nn": NN, "nt": NT, "tn": TN}[mode]
    a_spec = pl.BlockSpec((tk, tm), lambda i, j, kk: (kk, i)) if mode == "tn" else pl.BlockSpec((tm, tk), lambda i, j, kk: (i, kk))
    b_spec = pl.BlockSpec((tn, tk), lambda i, j, kk: (j, kk)) if mode == "nt" else pl.BlockSpec((tk, tn), lambda i, j, kk: (kk, j))
    o_spec = pl.BlockSpec((tm, tn), lambda i, j, kk: (i, j))
    n_extra, n_out, n_after = len(extra), len(outs), len(after)

    def body(a_ref, b_ref, *rest):
        extra_refs, out_refs, acc_ref = rest[:n_extra], rest[n_extra + n_after:n_extra + n_after + n_out], rest[-1]
        def finish(acc):
            res = (acc,) if epilogue is None else epilogue(acc, *[e[...] for e in extra_refs])
            for o_ref, r in zip(out_refs, res):
                o_ref[...] = r.astype(o_ref.dtype)

        if nk == 1:
            finish(_bdot(a_ref[...], b_ref[...], dims))
            return
        kk = pl.program_id(2)

        @pl.when(kk == 0)
        def _():
            acc_ref[...] = jnp.zeros_like(acc_ref)

        acc_ref[...] += _bdot(a_ref[...], b_ref[...], dims)

        @pl.when(kk == nk - 1)
        def _():
            finish(acc_ref[...])

    if col_shards == 1:
        out_specs, out_dims = [o_spec] * n_out, (m, n if out_cols is None else out_cols)
    else:
        sharded = pl.BlockSpec((None, tm, tn), lambda i, j, kk: (j // per_shard, i, j % per_shard))
        out_specs, out_dims = [sharded] * n_out, (col_shards, m, n // col_shards)
    res = pl.pallas_call(
        body, name=name, grid=(m // tm, n // tn, nk),
        in_specs=[a_spec, b_spec] + [o_spec] * n_extra + [pl.BlockSpec(memory_space=pl.ANY)] * n_after,
        out_specs=out_specs,
        out_shape=[jax.ShapeDtypeStruct(out_dims, dt) for dt in outs],
        scratch_shapes=[pltpu.VMEM((tm, tn) if nk > 1 else (8, 128), F32)],
        compiler_params=_params(("parallel", "parallel", "arbitrary")),
    )(a, b, *extra, *after)
    return res[0] if n_out == 1 else res


def _row_spec(tr, bw, cb, per_group):
    return pl.BlockSpec((tr, bw), (lambda g, i: (i, cb + g)) if per_group else (lambda g, i: (i, cb)))


def _vec_spec(bw, cb, per_group):
    return pl.BlockSpec((1, bw), (lambda g, i: (0, cb + g)) if per_group else (lambda g, i: (0, cb)))


def _rowwise(fn, rows, vecs, outs, *, name, n_rows=SEQ, tr=256, groups=1, after=()):
    n_r, n_v, n_after = len(rows), len(vecs), len(after)

    def body(*refs):
        vals = [r[...].astype(F32) for r in refs[:n_r + n_v]]
        res = fn(*vals)
        for o_ref, r in zip(refs[n_r + n_v + n_after:], res):
            o_ref[...] = r.astype(o_ref.dtype)

    res = pl.pallas_call(
        body, name=name, grid=(groups, n_rows // tr),
        in_specs=[_row_spec(tr, bw, cb, pg) for _, bw, cb, pg in rows] + [_vec_spec(bw, cb, pg) for _, bw, cb, pg in vecs]
        + [pl.BlockSpec(memory_space=pl.ANY)] * n_after,
        out_specs=[_row_spec(tr, bw, cb, pg) for _, _, bw, cb, pg in outs],
        out_shape=[jax.ShapeDtypeStruct((n_rows, w), dt) for w, dt, _, _, _ in outs],
        compiler_params=_params(("parallel", "parallel")),
    )(*[r[0] for r in rows], *[v[0] for v in vecs], *after)
    return res


def _rowwise_vjp(fn, rows, vecs, cts, row_grads, vec_grads, *, name, n_rows=SEQ, tr=256, groups=1, after=()):
    n_r, n_v, n_after = len(rows), len(vecs), len(after)
    ct_ops = [op for group in cts for op in group]
    ct_sizes = [len(group) for group in cts]
    res_ops = [g[6] for g in row_grads if g[6] is not None]
    n_ct, n_res, n_rg = len(ct_ops), len(res_ops), len(row_grads)

    def body(*refs):
        vals = [r[...].astype(F32) for r in refs[:n_r + n_v]]
        pos = n_r + n_v
        ct_vals = []
        for size in ct_sizes:
            acc = refs[pos][...].astype(F32)
            for t in range(1, size):
                acc = acc + refs[pos + t][...].astype(F32)
            ct_vals.append(acc)
            pos += size
        res_refs = refs[pos:pos + n_res]
        out_refs = refs[pos + n_res + n_after:]
        _, pullback = jax.vjp(fn, *vals)
        grads = pullback(tuple(ct_vals))
        r_i = 0
        for o_ref, g in zip(out_refs[:n_rg], row_grads):
            val = grads[g[0]]
            if g[6] is not None:
                val = val + res_refs[r_i][...].astype(F32)
                r_i += 1
            o_ref[...] = val.astype(o_ref.dtype)
        first = (pl.program_id(1) == 0)
        for o_ref, g in zip(out_refs[n_rg:], vec_grads):
            val = jnp.sum(grads[n_r + g[0]], axis=0, keepdims=True)
            init = first if g[4] else jnp.logical_and(first, pl.program_id(0) == 0)

            @pl.when(init)
            def _(o_ref=o_ref, val=val):
                o_ref[...] = val

            @pl.when(jnp.logical_not(init))
            def _(o_ref=o_ref, val=val):
                o_ref[...] += val

    in_specs = [_row_spec(tr, bw, cb, pg) for _, bw, cb, pg in rows] + [_vec_spec(bw, cb, pg) for _, bw, cb, pg in vecs]
    in_specs += [_row_spec(tr, bw, cb, pg) for _, bw, cb, pg in ct_ops + res_ops] + [pl.BlockSpec(memory_space=pl.ANY)] * n_after
    out_specs =[_row_spec(tr, g[3], g[4], g[5]) for g in row_grads] + [_vec_spec(g[2], g[3], g[4]) for g in vec_grads]
    out_shape = [jax.ShapeDtypeStruct((n_rows, g[1]), g[2]) for g in row_grads]
    out_shape += [jax.ShapeDtypeStruct((1, g[1]), F32) for g in vec_grads]
    return pl.pallas_call(
        body, name=name, grid=(groups, n_rows // tr),
        in_specs=in_specs, out_specs=out_specs, out_shape=out_shape,
        compiler_params=_params(("arbitrary", "arbitrary")),
    )(*[r[0] for r in rows], *[v[0] for v in vecs], *[c[0] for c in ct_ops], *[r[0] for r in res_ops], *after)


def _full(arr, width=None):
    return (arr, arr.shape[1] if width is None else width, 0, False)


def _make_xor(sh):
    def raw(x):
        n = x.shape[-1]
        lane = lax.broadcasted_iota(jnp.int32, x.shape, x.ndim - 1)
        up = pltpu.roll(x, n - sh, x.ndim - 1)
        down = pltpu.roll(x, sh, x.ndim - 1)
        return jnp.where((lane & sh) == 0, up, down)

    f = jax.custom_vjp(raw)
    f.defvjp(lambda x: (raw(x), None), lambda _, ct: (raw(ct),))
    return f


_SWAP_ROPE_HALVES = _make_xor(ROT // 2)


def _head_sum(x):
    n = x.shape[-1]
    same_head = (lax.broadcasted_iota(jnp.int32, (n, n), 0) // HEAD) == (lax.broadcasted_iota(jnp.int32, (n, n), 1) // HEAD)
    return _fdot(x, same_head.astype(F32), NN)


def _rms(x, g):
    return x * lax.rsqrt(jnp.mean(x * x, axis=-1, keepdims=True) + EPS) * g


def _head_rms_rope(x, g, cos, sin, scale):
    y = x * lax.rsqrt(_head_sum(x * x) * (1.0 / HEAD) + EPS) * g
    return (y * cos + _SWAP_ROPE_HALVES(y) * sin) * scale


def _qk_fn(q, k, v, cos, sin, gq, gk):
    return (_head_rms_rope(q, gq, cos, sin, HEAD ** -0.5), _head_rms_rope(k, gk, cos, sin, 1.0), v)


def _norm_fn(x, g):
    return (_rms(x, g),)


def _merge_fn(o0, o1, o2, l0, l1, l2, g):
    m = lax.stop_gradient(jnp.maximum(jnp.maximum(l0, l1), l2))
    e0, e1, e2 = jnp.exp(l0 - m), jnp.exp(l1 - m), jnp.exp(l2 - m)
    mix = (e0 * o0 + e1 * o1 + e2 * o2) / (e0 + e1 + e2)
    return (_rms(mix, g),)


def _gate_fn(y, z, g):
    return (_rms(y * (z * jax.nn.sigmoid(z)), g),)


def _attn_pair(q, kc, vc, kp=None, vp=None, has_prev=None):
    qi = lax.broadcasted_iota(jnp.int32, (ATT_BLK, ATT_BLK), 0)
    kj = lax.broadcasted_iota(jnp.int32, (ATT_BLK, ATT_BLK), 1)
    lane = lax.broadcasted_iota(jnp.int32, (1, 2 * HEAD), 1)
    o, lse = 0.0, 0.0
    for h in range(2):
        pick = ((lane >= h * HEAD) & (lane < (h + 1) * HEAD)).astype(F32)
        qh = q * pick
        s_c = jnp.where(qi >= kj, _bdot(qh, kc, NT), NEG)
        m = jnp.max(s_c, axis=-1, keepdims=True)
        if kp is not None:
            s_p = jnp.where(jnp.logical_and(kj >= qi, has_prev), _bdot(qh, kp, NT), NEG)
            m = jnp.maximum(m, jnp.max(s_p, axis=-1, keepdims=True))
        m = lax.stop_gradient(m)
        p_c = jnp.exp(s_c - m)
        den = jnp.sum(p_c, axis=-1, keepdims=True)
        acc = _bdot(p_c, vc, NN)
        if kp is not None:
            p_p = jnp.exp(s_p - m)
            den = den + jnp.sum(p_p, axis=-1, keepdims=True)
            acc = acc + _bdot(p_p, vp, NN)
        o = o + (pick * (1.0 / den)) * acc
        lse = lse + pick * (m + jnp.log(den))
    return o, lse


def _attn_config(b):
    r = DILATIONS[b]
    return r, ATT_BLK * r, (512 if r == 1 else 128), BRANCH_BLOCKS[b] > 1


def _for_residues(r, fn):
    if r <= 4:
        for rho in range(r):
            fn(rho)
    else:
        def step(t, carry):
            for u in range(4):
                fn(4 * t + u)
            return carry

        lax.fori_loop(0, r // 4, step, 0)


def _strided_rows(start, r):
    if r > 1:
        return pl.ds(start, ATT_BLK, stride=r)
    return pl.ds(start if isinstance(start, int) else pl.multiple_of(start, ATT_BLK), ATT_BLK)


def _attention_fwd(qn, kn, vn, b):
    r, rows, lanes, with_prev = _attn_config(b)
    cur = pl.BlockSpec((rows, lanes), lambda g, n: (n, g))
    prev = pl.BlockSpec((rows, lanes), lambda g, n: (jnp.maximum(n - 1, 0), g))

    def body(*refs):
        ins, (o_ref, l_ref) = refs[:-2], refs[-2:]
        has_prev = pl.program_id(1) > 0

        def one(rho):
            sub = _strided_rows(rho, r)
            for pair in range(lanes // 128):
                sl = pl.ds(pair * 128, 128)
                args = [ref[sub, sl] for ref in ins] + ([has_prev] if with_prev else [])
                o_ref[sub, sl], l_ref[sub, sl] = _attn_pair(*args)

        _for_residues(r, one)

    operands = (qn, kn, vn, kn, vn) if with_prev else (qn, kn, vn)
    return pl.pallas_call(
        body, name="attn_fwd_%d" % r, grid=(D_ATTN // lanes, SEQ // rows),
        in_specs=[cur, cur, cur] + ([prev, prev] if with_prev else []), out_specs=[cur, cur],
        out_shape=[jax.ShapeDtypeStruct((SEQ, D_ATTN), F32)] * 2,
        compiler_params=_params(("parallel", "parallel")),
    )(*operands)


def _attention_bwd(qn, kn, vn, do, dl, b):
    r, rows, lanes, with_prev = _attn_config(b)
    cur = pl.BlockSpec((rows, lanes), lambda g, n: (n, g))
    prev = pl.BlockSpec((rows, lanes), lambda g, n: (jnp.maximum(n - 1, 0), g))
    whole = pl.BlockSpec((SEQ, lanes), lambda g, n: (0, g))
    n_in = 5 if with_prev else 3

    def body(*refs):
        ins, (do_ref, dl_ref, dq_ref, dk_ref, dv_ref) = refs[:n_in], refs[n_in:]
        n = pl.program_id(1)
        has_prev = n > 0

        @pl.when(n == 0)
        def _():
            dk_ref[...] = jnp.zeros_like(dk_ref)
            dv_ref[...] = jnp.zeros_like(dv_ref)

        def one(rho):
            sub = _strided_rows(rho, r)
            sub_c = _strided_rows(n * rows + rho, r)
            sub_p = _strided_rows(jnp.maximum(n - 1, 0) * rows + rho, r)
            for pair in range(lanes // 128):
                sl = pl.ds(pair * 128, 128)
                vals = [ref[sub, sl] for ref in ins]
                if with_prev:
                    _, pullback = jax.vjp(lambda *a: _attn_pair(*a, has_prev), *vals)
                else:
                    _, pullback = jax.vjp(_attn_pair, *vals)
                grads = pullback((do_ref[sub, sl], dl_ref[sub, sl]))
                dq_ref[sub, sl] = grads[0]
                dk_ref[sub_c, sl] += grads[1]
                dv_ref[sub_c, sl] += grads[2]
                if with_prev:
                    dk_ref[sub_p, sl] += grads[3]
                    dv_ref[sub_p, sl] += grads[4]

        _for_residues(r, one)

    operands = (qn, kn, vn, kn, vn) if with_prev else (qn, kn, vn)
    return pl.pallas_call(
        body, name="attn_bwd_%d" % r, grid=(D_ATTN // lanes, SEQ // rows),
        in_specs=[cur, cur, cur] + ([prev, prev] if with_prev else []) + [cur, cur], out_specs=[cur, whole, whole],
        out_shape=[jax.ShapeDtypeStruct((SEQ, D_ATTN), F32)] * 3,
        compiler_params=_params(("parallel", "arbitrary")),
    )(*operands, do, dl)


CONV_COLS = 256
XBC_BLOCK0 = 4096 // CONV_COLS


def _shift_rows(x, s):
    n = x.shape[0]
    t = lax.broadcasted_iota(jnp.int32, x.shape, 0)
    if s >= 0:
        return jnp.where(t >= s, pltpu.roll(x, s, 0), 0.0)
    return jnp.where(t < n + s, pltpu.roll(x, n + s, 0), 0.0)


def _conv_pre(x, w_ref, b_ref):
    pre = b_ref[...] + w_ref[3:4, :] * x
    for k in range(3):
        pre = pre + w_ref[k:k + 1, :] * _shift_rows(x, 3 - k)
    return pre


def _conv_fwd(proj, conv_w, conv_b):
    cols = conv_w.shape[1]

    def body(x_ref, w_ref, b_ref, o_ref):
        pre = _conv_pre(x_ref[...], w_ref, b_ref)
        o_ref[...] = pre * jax.nn.sigmoid(pre)

    blk = pl.BlockSpec((SEQ, CONV_COLS), lambda j: (0, j))
    return pl.pallas_call(
        body, name="conv_fwd", grid=(cols // CONV_COLS,),
        in_specs=[pl.BlockSpec((SEQ, CONV_COLS), lambda j: (0, XBC_BLOCK0 + j)),
                  pl.BlockSpec((4, CONV_COLS), lambda j: (0, j)), pl.BlockSpec((1, CONV_COLS), lambda j: (0, j))],
        out_specs=blk, out_shape=jax.ShapeDtypeStruct((SEQ, cols), F32),
        compiler_params=_params(("parallel",)),
    )(proj, conv_w, conv_b)


def _conv_bwd(proj, conv_w, conv_b, dy):
    cols = conv_w.shape[1]

    def body(x_ref, w_ref, b_ref, dy_ref, dx_ref, dw_ref, db_ref):
        x = x_ref[...]
        pre = _conv_pre(x, w_ref, b_ref)
        sg = jax.nn.sigmoid(pre)
        dpre = dy_ref[...] * (sg * (1.0 + pre * (1.0 - sg)))
        db_ref[...] = jnp.sum(dpre, axis=0, keepdims=True)
        dx = w_ref[3:4, :] * dpre
        dw_ref[3:4, :] = jnp.sum(dpre * x, axis=0, keepdims=True)
        for k in range(3):
            dx = dx + w_ref[k:k + 1, :] * _shift_rows(dpre, k - 3)
            dw_ref[k:k + 1, :] = jnp.sum(dpre * _shift_rows(x, 3 - k), axis=0, keepdims=True)
        dw_ref[4:8, :] = jnp.zeros((4, CONV_COLS), F32)
        dx_ref[...] = dx.astype(dx_ref.dtype)

    blk = pl.BlockSpec((SEQ, CONV_COLS), lambda j: (0, j))
    return pl.pallas_call(
        body, name="conv_bwd", grid=(cols // CONV_COLS,),
        in_specs=[pl.BlockSpec((SEQ, CONV_COLS), lambda j: (0, XBC_BLOCK0 + j)),
                  pl.BlockSpec((4, CONV_COLS), lambda j: (0, j)), pl.BlockSpec((1, CONV_COLS), lambda j: (0, j)), blk],
        out_specs=[blk, pl.BlockSpec((8, CONV_COLS), lambda j: (0, j)), pl.BlockSpec((1, CONV_COLS), lambda j: (0, j))],
        out_shape=[jax.ShapeDtypeStruct((SEQ, cols), BF16), jax.ShapeDtypeStruct((8, cols), F32),
                   jax.ShapeDtypeStruct((1, cols), F32)],
        compiler_params=_params(("parallel",)),
    )(proj, conv_w, conv_b, dy)


HEADS_PER_GROUP = 4


def _ssd_chunk(x0, x1, x2, x3, bm, cm, dtr, bias, alog, dsk, h0, h1, h2, h3):
    xs, hs = (x0, x1, x2, x3), (h0, h1, h2, h3)
    row = lax.broadcasted_iota(jnp.int32, (CHUNK, CHUNK), 0)
    col = lax.broadcasted_iota(jnp.int32, (CHUNK, CHUNK), 1)
    causal = row >= col
    tril = causal.astype(F32)
    z = dtr + bias
    dt = jnp.maximum(z, 0.0) + jnp.log(1.0 + jnp.exp(-jnp.abs(z)))
    a = -jnp.exp(alog)
    acs = _fdot(tril, dt * a, NN)
    acs_t, dt_t = acs.T, dt.T
    cb = _bdot(cm, bm, NT)
    lane = lax.broadcasted_iota(jnp.int32, (1, CHUNK), 1)
    sub = lax.broadcasted_iota(jnp.int32, (CHUNK, 1), 0)
    ys, hn = [], []
    for j in range(HEADS_PER_GROUP):
        on_lane, on_sub = (lane == j).astype(F32), (sub == j).astype(F32)
        acs_c = jnp.sum(acs * on_lane, axis=1, keepdims=True)
        dt_c = jnp.sum(dt * on_lane, axis=1, keepdims=True)
        acs_r = jnp.sum(acs_t * on_sub, axis=0, keepdims=True)
        dt_r = jnp.sum(dt_t * on_sub, axis=0, keepdims=True)
        acs_last = jnp.sum(acs_c * (sub == CHUNK - 1).astype(F32), axis=0, keepdims=True)
        d_j = jnp.sum(dsk * on_lane, axis=1, keepdims=True)
        decay = jnp.exp(jnp.where(causal, acs_c - acs_r, NEG))
        w = cb * decay * dt_r
        y_diag = _bdot(w, xs[j], NN)
        y_off = _bdot(cm, hs[j], NT) * jnp.exp(acs_c)
        ys.append(y_diag + y_off + d_j * xs[j])
        state = _bdot(xs[j] * (jnp.exp(acs_last - acs_c) * dt_c), bm, TN)
        hn.append(hs[j] * jnp.exp(acs_last) + state)
    return (*ys, *hn)


def _ssd_specs(reverse):
    n_chunks = SEQ // CHUNK
    c_of = (lambda c: n_chunks - 1 - c) if reverse else (lambda c: c)
    x_spec = pl.BlockSpec((CHUNK, 256), lambda g, c: (c_of(c), g))
    b_spec = pl.BlockSpec((CHUNK, N_STATE), lambda g, c: (c_of(c), 8 + g))
    c_spec = pl.BlockSpec((CHUNK, N_STATE), lambda g, c: (c_of(c), 12 + g))
    dt_spec = pl.BlockSpec((CHUNK, 128), lambda g, c: (c_of(c), g))
    vec_spec = pl.BlockSpec((1, 128), lambda g, c: (0, g))
    h_spec = pl.BlockSpec((1, 1, HEADS_PER_GROUP, HEAD, N_STATE), lambda g, c: (c_of(c), g, 0, 0, 0))
    return x_spec, b_spec, c_spec, dt_spec, vec_spec, h_spec


def _ssd_fwd(xbc, dt_raw, bias, alog, dsk):
    x_spec, b_spec, c_spec, dt_spec, vec_spec, h_spec = _ssd_specs(False)

    def body(x_ref, b_ref, c_ref, dt_ref, bias_ref, alog_ref, dsk_ref, y_ref, hin_ref, h_scr):
        @pl.when(pl.program_id(1) == 0)
        def _():
            h_scr[...] = jnp.zeros_like(h_scr)

        hs = [h_scr[j] for j in range(HEADS_PER_GROUP)]
        for j in range(HEADS_PER_GROUP):
            hin_ref[0, 0, j] = hs[j]
        xs = [x_ref[:, pl.ds(j * HEAD, HEAD)] for j in range(HEADS_PER_GROUP)]
        res = _ssd_chunk(*xs, b_ref[...], c_ref[...], dt_ref[...], bias_ref[...], alog_ref[...], dsk_ref[...], *hs)
        for j in range(HEADS_PER_GROUP):
            y_ref[:, pl.ds(j * HEAD, HEAD)] = res[j]
            h_scr[j] = res[HEADS_PER_GROUP + j]

    return pl.pallas_call(
        body, name="ssd_fwd", grid=(N_GROUPS, SEQ // CHUNK),
        in_specs=[x_spec, b_spec, c_spec, dt_spec, vec_spec, vec_spec, vec_spec],
        out_specs=[x_spec, h_spec],
        out_shape=[jax.ShapeDtypeStruct((SEQ, D_SSM), F32),
                   jax.ShapeDtypeStruct((SEQ // CHUNK, N_GROUPS, HEADS_PER_GROUP, HEAD, N_STATE), F32)],
        scratch_shapes=[pltpu.VMEM((HEADS_PER_GROUP, HEAD, N_STATE), F32)],
        compiler_params=_params(("parallel", "arbitrary")),
    )(xbc, xbc, xbc, dt_raw, bias, alog, dsk)


def _ssd_bwd(xbc, dt_raw, bias, alog, dsk, h_in, dy):
    x_spec, b_spec, c_spec, dt_spec, vec_spec, h_spec = _ssd_specs(True)
    dxbc_x = pl.BlockSpec((CHUNK, 256), x_spec.index_map)

    def body(x_ref, b_ref, c_ref, dt_ref, bias_ref, alog_ref, dsk_ref, hin_ref, dy_ref,
             dx_ref, db_ref, dc_ref, ddt_ref, dbias_ref, dalog_ref, ddsk_ref, dh_scr):
        first = pl.program_id(1) == 0

        @pl.when(first)
        def _():
            dh_scr[...] = jnp.zeros_like(dh_scr)

        xs = [x_ref[:, pl.ds(j * HEAD, HEAD)] for j in range(HEADS_PER_GROUP)]
        hs = [hin_ref[0, 0, j] for j in range(HEADS_PER_GROUP)]
        cts = [dy_ref[:, pl.ds(j * HEAD, HEAD)] for j in range(HEADS_PER_GROUP)] + [dh_scr[j] for j in range(HEADS_PER_GROUP)]
        _, pullback = jax.vjp(_ssd_chunk, *xs, b_ref[...], c_ref[...], dt_ref[...], bias_ref[...], alog_ref[...],
                              dsk_ref[...], *hs)
        g = pullback(tuple(cts))
        for j in range(HEADS_PER_GROUP):
            dx_ref[:, pl.ds(j * HEAD, HEAD)] = g[j]
            dh_scr[j] = g[10 + j]
        db_ref[...] = g[4]
        dc_ref[...] = g[5]
        ddt_ref[...] = g[6].astype(ddt_ref.dtype)
        for o_ref, val in ((dbias_ref, g[7]), (dalog_ref, g[8]), (ddsk_ref, g[9])):
            @pl.when(first)
            def _(o_ref=o_ref, val=val):
                o_ref[...] = val

            @pl.when(jnp.logical_not(first))
            def _(o_ref=o_ref, val=val):
                o_ref[...] += val

    n_chunks = SEQ // CHUNK
    out_b = pl.BlockSpec((CHUNK, N_STATE), lambda g, c: (n_chunks - 1 - c, g))
    res = pl.pallas_call(
        body, name="ssd_bwd", grid=(N_GROUPS, n_chunks),
        in_specs=[x_spec, b_spec, c_spec, dt_spec, vec_spec, vec_spec, vec_spec, h_spec, x_spec],
        out_specs=[dxbc_x, out_b, out_b, dt_spec, vec_spec, vec_spec, vec_spec],
        out_shape=[jax.ShapeDtypeStruct((SEQ, D_SSM), F32), jax.ShapeDtypeStruct((SEQ, N_GROUPS * N_STATE), F32),
                   jax.ShapeDtypeStruct((SEQ, N_GROUPS * N_STATE), F32), jax.ShapeDtypeStruct((SEQ, DT_PAD), BF16),
                   jax.ShapeDtypeStruct((1, DT_PAD), F32), jax.ShapeDtypeStruct((1, DT_PAD), F32),
                   jax.ShapeDtypeStruct((1, DT_PAD), F32)],
        scratch_shapes=[pltpu.VMEM((HEADS_PER_GROUP, HEAD, N_STATE), F32)],
        compiler_params=_params(("parallel", "arbitrary")),
    )(xbc, xbc, xbc, dt_raw, bias, alog, dsk, h_in, dy)
    return res


CROSS_HEAD = 128
CROSS_ROWS = 512


def _cross_head(q, k, v, gq, gk):
    qn = _rms(q, gq) * (CROSS_HEAD ** -0.5)
    kn = _rms(k, gk)
    s = _bdot(qn, kn, NT)
    p = jnp.exp(s - lax.stop_gradient(jnp.max(s, axis=-1, keepdims=True)))
    return _bdot(p, v, NN) * (1.0 / jnp.sum(p, axis=-1, keepdims=True))


def _cross_specs():
    q_spec = pl.BlockSpec((CROSS_ROWS, CROSS_HEAD), lambda h, i: (i, h))
    k_spec = pl.BlockSpec((N_MEM, CROSS_HEAD), lambda h, i: (0, h))
    v_spec = pl.BlockSpec((N_MEM, CROSS_HEAD), lambda h, i: (0, 4 + h))
    g_spec = pl.BlockSpec((1, CROSS_HEAD), lambda h, i: (0, 0))
    return q_spec, k_spec, v_spec, g_spec


def _cross_fwd(qc, kv, gq, gk):
    q_spec, k_spec, v_spec, g_spec = _cross_specs()

    def body(q_ref, k_ref, v_ref, gq_ref, gk_ref, o_ref):
        o_ref[...] = _cross_head(q_ref[...], k_ref[...], v_ref[...], gq_ref[...], gk_ref[...]).astype(o_ref.dtype)

    return pl.pallas_call(
        body, name="cross_fwd", grid=(4, SEQ // CROSS_ROWS),
        in_specs=[q_spec, k_spec, v_spec, g_spec, g_spec], out_specs=q_spec,
        out_shape=jax.ShapeDtypeStruct((SEQ, D_CROSS), BF16),
        compiler_params=_params(("parallel", "parallel")),
    )(qc, kv, kv, gq, gk)


def _cross_bwd(qc, kv, gq, gk, do):
    q_spec, k_spec, v_spec, g_spec = _cross_specs()

    def body(q_ref, k_ref, v_ref, gq_ref, gk_ref, do_ref, dq_ref, dk_ref, dv_ref, dgq_ref, dgk_ref):
        _, pullback = jax.vjp(_cross_head, q_ref[...], k_ref[...], v_ref[...], gq_ref[...], gk_ref[...])
        dq, dk, dv, dgq, dgk = pullback(do_ref[...].astype(F32))
        dq_ref[...] = dq.astype(dq_ref.dtype)
        row0 = pl.program_id(1) == 0
        all0 = jnp.logical_and(row0, pl.program_id(0) == 0)
        for o_ref, val, init in ((dk_ref, dk, row0), (dv_ref, dv, row0), (dgq_ref, dgq, all0), (dgk_ref, dgk, all0)):
            @pl.when(init)
            def _(o_ref=o_ref, val=val):
                o_ref[...] = val

            @pl.when(jnp.logical_not(init))
            def _(o_ref=o_ref, val=val):
                o_ref[...] += val

    return pl.pallas_call(
        body, name="cross_bwd", grid=(4, SEQ // CROSS_ROWS),
        in_specs=[q_spec, k_spec, v_spec, g_spec, g_spec, q_spec],
        out_specs=[q_spec, k_spec, k_spec, g_spec, g_spec],
        out_shape=[jax.ShapeDtypeStruct((SEQ, D_CROSS), BF16), jax.ShapeDtypeStruct((N_MEM, D_CROSS), F32),
                   jax.ShapeDtypeStruct((N_MEM, D_CROSS), F32), jax.ShapeDtypeStruct((1, CROSS_HEAD), F32),
                   jax.ShapeDtypeStruct((1, CROSS_HEAD), F32)],
        compiler_params=_params(("arbitrary", "arbitrary")),
    )(qc, kv, kv, gq, gk, do)


def _loss_head(y, target):
    tr = 256

    def body(y_ref, t_ref, dy_ref, dyb_ref, loss_ref):
        err = y_ref[...] - t_ref[...]
        dy = err * (1.0 / D_MODEL)
        dy_ref[...] = dy
        dyb_ref[...] = dy.astype(BF16)
        part = jnp.sum(jnp.sum(err * err, axis=1, keepdims=True), axis=0, keepdims=True) * (0.5 / D_MODEL)
        part = jnp.broadcast_to(part, (1, 128))

        @pl.when(pl.program_id(0) == 0)
        def _():
            loss_ref[...] = part

        @pl.when(pl.program_id(0) != 0)
        def _():
            loss_ref[...] += part

    blk = pl.BlockSpec((tr, D_MODEL), lambda i: (i, 0))
    return pl.pallas_call(
        body, name="loss_head", grid=(SEQ // tr,),
        in_specs=[blk, blk], out_specs=[blk, blk, pl.BlockSpec((1, 128), lambda i: (0, 0))],
        out_shape=[jax.ShapeDtypeStruct((SEQ, D_MODEL), F32), jax.ShapeDtypeStruct((SEQ, D_MODEL), BF16),
                   jax.ShapeDtypeStruct((1, 128), F32)],
        compiler_params=_params(("arbitrary",)),
    )(y, target)


def _pad_heads(v):
    return jnp.pad(v.reshape(N_GROUPS, HEADS_PER_GROUP), ((0, 0), (0, 128 - HEADS_PER_GROUP))).reshape(1, DT_PAD)


def _unpad_heads(v):
    return v.reshape(v.shape[0], N_GROUPS, 128)[:, :, :HEADS_PER_GROUP].reshape(v.shape[0], N_DT)


def _rope_tables(positions):
    half = ROT // 2
    inv_freq = ROPE_THETA ** (-2.0 * jnp.arange(half, dtype=F32) / ROT)
    ang = positions.reshape(SEQ, 1).astype(F32) * inv_freq
    cos, sin = jnp.cos(ang), jnp.sin(ang)
    ones, zeros = jnp.ones((SEQ, HEAD - ROT), F32), jnp.zeros((SEQ, HEAD - ROT), F32)
    cos_h = jnp.concatenate([cos, cos, ones], axis=1)
    sin_h = jnp.concatenate([-sin, sin, zeros], axis=1)
    return jnp.tile(cos_h, (1, 2)), jnp.tile(sin_h, (1, 2))


def _add_res(acc, res):
    return (acc + res,)


def _settle(grads, *after):
    if hasattr(grads, "settle"):
        grads.settle(*after)


def _take_token(grads):
    token = getattr(grads, "token", None)
    if token is None:
        return ()
    grads.token = None
    return (token,)


def _local_step(x, mem, positions, target, p, w, more_weights=None, grads=None, h=None):
    grads = {} if grads is None else grads
    w = dict(w)
    cos, sin = _rope_tables(positions)
    gq2, gk2 = jnp.tile(p["g_q"], (1, 2)), jnp.tile(p["g_k"], (1, 2))
    bias, alog, dsk = _pad_heads(p["dt_bias"]), _pad_heads(p["a_log"]), _pad_heads(p["d_skip"])
    norm_out = [(D_MODEL, BF16, D_MODEL, 0, False)]

    if h is None:
        h = _rowwise(_norm_fn, [_full(x)], [_full(p["g_mix"])], norm_out, name="norm_in")[0]
    proj = _matmul(h, w["w_in"], mode="nn", name="in_proj", outs=[F32], n_cols=D_MAIN)
    dt_raw = _matmul(h, w["w_dt"], mode="nn", name="dt_proj", outs=[F32])
    qk_rows = [(proj, 128, 0, True), (proj, 128, 8, True), (proj, 128, 16, True), _full(cos), _full(sin)]
    qk_vecs = [_full(gq2), _full(gk2)]
    qn, kn, vn = _rowwise(_qk_fn, qk_rows, qk_vecs, [(D_ATTN, F32, 128, 0, True)] * 3, name="qk_prep", groups=8, tr=1024)
    branches = [_attention_fwd(qn, kn, vn, b) for b in range(3)]
    merge_rows = [_full(o) for o, _ in branches] + [_full(lse) for _, lse in branches]
    attn = _rowwise(_merge_fn, merge_rows, [_full(p["g_attn_out"])], [(D_ATTN, BF16, D_ATTN, 0, False)], name="attn_merge")[0]
    xbc = _conv_fwd(proj, p["conv_w"], p["conv_b"])
    y_ssd, h_in = _ssd_fwd(xbc, dt_raw, bias, alog, dsk)
    gate_rows = [(y_ssd, 256, 0, True), (proj, 256, 12, True)]
    gate_vecs = [(p["g_ssm_out"], 256, 0, True)]
    ssm = _rowwise(_gate_fn, gate_rows, gate_vecs, [(D_SSM, BF16, 256, 0, True)], name="ssm_gate", groups=4)[0]
    mix = jnp.concatenate([attn, ssm], axis=1)
    if more_weights is not None:
        w.update(more_weights("mixer_done", mix))
    x1 = _matmul(mix, w["w_out"], mode="nn", name="out_proj", outs=[F32], extra=(x,), epilogue=_add_res)
    hc = _rowwise(_norm_fn, [_full(x1)], [_full(p["g_cross"])], norm_out, name="norm_cross")[0]
    memh = _rowwise(_norm_fn, [_full(mem)], [_full(p["g_mem"])], norm_out, name="norm_mem", n_rows=N_MEM)[0]
    qc = _matmul(hc, w["w_cq"], mode="nn", name="cq_proj", outs=[F32])
    kv = _matmul(memh, w["w_ckv"], mode="nn", name="ckv_proj", outs=[F32])
    oc = _cross_fwd(qc, kv, p["g_cq"], p["g_ck"])
    x2 = _matmul(oc, w["w_co"], mode="nn", name="co_proj", outs=[F32], extra=(x1,), epilogue=_add_res)
    hm = _rowwise(_norm_fn, [_full(x2)], [_full(p["g_mlp"])], norm_out, name="norm_mlp")[0]
    if more_weights is not None:
        w.update(more_weights("cross_done", hm))
    u, act = _matmul(hm, w["w_up"], mode="nn", name="up_proj", outs=[F32, BF16],
                     epilogue=lambda acc: (acc, jnp.square(jnp.maximum(acc, 0.0))))
    x3 = _matmul(act, w["w_down"], mode="nn", name="down_proj", outs=[F32], extra=(x2,), epilogue=_add_res)
    dy, dyb, loss = _loss_head(x3, target)

    grads["w_down"] = _matmul(act, dyb, mode="tn", name="dw_down", outs=[BF16], after=_take_token(grads))
    du = _matmul(dyb, w["w_down"], mode="nt", name="d_act", outs=[BF16], extra=(u,), after=_take_token(grads),
                 epilogue=lambda acc, uu: (acc * (2.0 * jnp.maximum(uu, 0.0)),))
    _settle(grads, du)
    grads["w_up"] = _matmul(hm, du, mode="tn", name="dw_up", outs=[BF16], col_shards=4, after=_take_token(grads))
    dhm = _matmul(du, w["w_up"], mode="nt", name="d_hm", outs=[F32], after=_take_token(grads))
    _settle(grads, dhm)
    dx2, grads["g_mlp"] = _rowwise_vjp(
        _norm_fn, [_full(x2)], [_full(p["g_mlp"])], [[_full(dhm)]],
        [(0, D_MODEL, F32, D_MODEL, 0, False, _full(dy))], [(0, D_MODEL, D_MODEL, 0, False)], name="norm_mlp_bwd")
    grads["w_co"] = _matmul(oc, dx2, mode="tn", name="dw_co", outs=[BF16], col_shards=4, after=_take_token(grads))
    doc = _matmul(dx2, w["w_co"], mode="nt", name="d_oc", outs=[BF16])
    dqc, dkc, dvc, grads["g_cq"], grads["g_ck"] = _cross_bwd(qc, kv, p["g_cq"], p["g_ck"], doc)
    grads["w_cq"] = _matmul(hc, dqc, mode="tn", name="dw_cq", outs=[BF16])
    dhc = _matmul(dqc, w["w_cq"], mode="nt", name="d_hc", outs=[F32])
    dkv = jnp.concatenate([dkc, dvc], axis=1)
    grads["w_ckv"] = _matmul(memh, dkv, mode="tn", name="dw_ckv", outs=[BF16])
    dmemh = _matmul(dkv, w["w_ckv"], mode="nt", name="d_memh", outs=[F32])
    grads["g_mem"] = _rowwise_vjp(_norm_fn, [_full(mem)], [_full(p["g_mem"])], [[_full(dmemh)]], [],
                                  [(0, D_MODEL, D_MODEL, 0, False)], name="norm_mem_bwd", n_rows=N_MEM)[0]
    dx1, grads["g_cross"] = _rowwise_vjp(
        _norm_fn, [_full(x1)], [_full(p["g_cross"])], [[_full(dhc)]],
        [(0, D_MODEL, F32, D_MODEL, 0, False, _full(dx2))], [(0, D_MODEL, D_MODEL, 0, False)], name="norm_cross_bwd")
    grads["w_out"] = _matmul(mix, dx1, mode="tn", name="dw_out", outs=[BF16])
    dmix = _matmul(dx1, w["w_out"], mode="nt", name="d_mix", outs=[F32], after=_take_token(grads))
    _settle(grads, dmix)
    merge_grads = [(i, D_ATTN, F32, D_ATTN, 0, False, None) for i in range(6)]
    *dol, grads["g_attn_out"] = _rowwise_vjp(
        _merge_fn, merge_rows, [_full(p["g_attn_out"])], [[(dmix, D_ATTN, 0, False)]],
        merge_grads, [(0, D_ATTN, D_ATTN, 0, False)], name="attn_merge_bwd", after=_take_token(grads))
    dqkv = [_attention_bwd(qn, kn, vn, dol[b], dol[3 + b], b) for b in range(3)]
    qk_cts = [[(dqkv[b][i], 128, 0, True) for b in range(3)] for i in range(3)]
    dq, dk, dv, dgq2, dgk2 = _rowwise_vjp(
        _qk_fn, qk_rows, qk_vecs, qk_cts, [(i, D_ATTN, BF16, 128, 0, True, None) for i in range(3)],
        [(0, 128, 128, 0, False), (1, 128, 128, 0, False)], name="qk_prep_bwd", groups=8, tr=512)
    grads["g_q"] = dgq2[:, :HEAD] + dgq2[:, HEAD:]
    grads["g_k"] = dgk2[:, :HEAD] + dgk2[:, HEAD:]
    dy_ssd, dz, grads["g_ssm_out"] = _rowwise_vjp(
        _gate_fn, gate_rows, gate_vecs, [[(dmix, 256, 4, True)]],
        [(0, D_SSM, F32, 256, 0, True, None), (1, D_SSM, BF16, 256, 0, True, None)],
        [(0, D_SSM, 256, 0, True)], name="ssm_gate_bwd", groups=4)
    dxs, db, dc, ddt, dbias, dalog, ddsk = _ssd_bwd(xbc, dt_raw, bias, alog, dsk, h_in, dy_ssd)
    grads["dt_bias"], grads["a_log"], grads["d_skip"] = _unpad_heads(dbias), _unpad_heads(dalog), _unpad_heads(ddsk)
    dxbc_raw, dconv_w, grads["conv_b"] = _conv_bwd(proj, p["conv_w"], p["conv_b"], jnp.concatenate([dxs, db, dc], axis=1))
    grads["conv_w"] = dconv_w[:4]
    dproj = jnp.concatenate([dq, dk, dv, dz, dxbc_raw], axis=1)
    grads["w_main"] = _matmul(h, dproj, mode="tn", name="dw_main", outs=[BF16], out_cols=D_MAIN + N_DT)
    grads["w_dt"] = _matmul(h, ddt, mode="tn", name="dw_dt", outs=[BF16])
    dh = _matmul(dproj, w["w_in"], mode="nt", name="d_h_main", outs=[F32], after=_take_token(grads))
    dh = _matmul(ddt, w["w_dt"], mode="nt", name="d_h_dt", outs=[F32], extra=(dh,), epilogue=_add_res)
    grad_x, grads["g_mix"] = _rowwise_vjp(
        _norm_fn, [_full(x)], [_full(p["g_mix"])], [[_full(dh)]],
        [(0, D_MODEL, F32, D_MODEL, 0, False, _full(dx1))], [(0, D_MODEL, D_MODEL, 0, False)], name="norm_in_bwd")
    return loss, grad_x, grads


MATRICES = ("w_in", "w_out", "w_cq", "w_ckv", "w_co", "w_up", "w_down")
ROW_SHARDED = ("w_out", "w_cq", "w_ckv", "w_down")
N_CHIPS = 4
ANY = pl.BlockSpec(memory_space=pl.ANY)


def _place():
    return lax.axis_index("x"), lax.axis_index("y"), lax.axis_index("c")


def _other_chips(x, y):
    return [(1 - x, y), (x, 1 - y), (1 - x, 1 - y)]


def _remote(src, dst, send_sem, recv_sem, device):
    return pltpu.make_async_remote_copy(src_ref=src, dst_ref=dst, send_sem=send_sem, recv_sem=recv_sem,
                                        device_id=device, device_id_type=MESH)


def _gathered_shape(name, shard):
    rows, cols = shard.shape
    if name == "w_in":
        return (N_CHIPS, rows, cols)
    return (N_CHIPS * rows, cols) if name in ROW_SHARDED else (rows, N_CHIPS * cols)


def _shard_window(name, ref, rows, cols, chip, half):
    r0, nr = (0, rows) if half is None else (half * (rows // 2), rows // 2)
    if name == "w_in":
        return ref.at[chip, pl.ds(r0, nr), :]
    if name in ROW_SHARDED:
        return ref.at[pl.ds(chip * rows + r0, nr), :]
    return ref.at[pl.ds(r0, nr), pl.ds(pl.multiple_of(chip * cols, 128), cols)]


def _cast_into_gathered(w, name, chip, after=()):
    rows, cols = w.shape
    tr = _tile(rows, ROW_TILE)

    def body(chip_ref, w_ref, *rest):
        rest[-1][...] = w_ref[...].astype(BF16)

    if name == "w_in":
        out_spec = pl.BlockSpec((None, tr, cols), lambda i, chip_ref: (chip_ref[0], i, 0))
    elif name in ROW_SHARDED:
        out_spec = pl.BlockSpec((tr, cols), lambda i, chip_ref: (chip_ref[0] * (rows // tr) + i, 0))
    else:
        out_spec = pl.BlockSpec((tr, cols), lambda i, chip_ref: (i, chip_ref[0]))
    grid_spec = pltpu.PrefetchScalarGridSpec(
        num_scalar_prefetch=1, grid=(rows // tr,),
        in_specs=[pl.BlockSpec((tr, cols), lambda i, chip_ref: (i, 0))] + [pl.BlockSpec(memory_space=pl.ANY)] * len(after),
        out_specs=out_spec)
    return pl.pallas_call(body, name="cast_" + name, grid_spec=grid_spec,
                          out_shape=jax.ShapeDtypeStruct(_gathered_shape(name, w), BF16),
                          compiler_params=_params(("parallel",)))(chip.reshape(1).astype(jnp.int32), w, *after)


HBM = pl.BlockSpec(memory_space=pltpu.HBM)
SEM = pl.BlockSpec(memory_space=pltpu.SEMAPHORE)
EFFECT = pltpu.SideEffectType.DATAFLOW_SIDE_EFFECTING


def _split_start(name, bufs, plan, counts, after=()):
    n, n_g, n_after = len(bufs), len(counts), len(after)

    def body(*refs):
        ins, sems, token = refs[:n], refs[n + n_after:n + n_after + 2 * n_g], refs[-1]
        for g, copies in enumerate(plan(ins)):
            for i, (src, dst, device, _) in enumerate(copies):
                _remote(src, dst, sems[2 * g].at[i], sems[2 * g + 1].at[i], device).start()
        token[...] = jnp.zeros_like(token)

    sem_shapes = [pltpu.SemaphoreType.DMA((cnt,)) for cnt in counts for _ in range(2)]
    res = pl.pallas_call(
        body, name=name,
        out_shape=(*sem_shapes, *[pltpu.HBM(b.shape, b.dtype) for b in bufs], jax.ShapeDtypeStruct((8, 128), F32)),
        in_specs=(*(HBM,) * n, *(ANY,) * n_after),
        out_specs=(*(SEM,) * (2 * n_g), *(HBM,) * n, pl.BlockSpec(memory_space=pltpu.VMEM)),
        input_output_aliases={i: 2 * n_g + i for i in range(n)},
        compiler_params=pltpu.CompilerParams(has_side_effects=EFFECT),
    )(*[pltpu.with_memory_space_constraint(b, pltpu.HBM) for b in bufs], *after)
    sems = [(res[2 * g], res[2 * g + 1]) for g in range(n_g)]
    return sems, list(res[2 * n_g:2 * n_g + n]), res[-1]


def _split_wait(name, bufs, sems, plan, *after):
    n = len(bufs)

    def body(*refs):
        ins, send, recv = refs[:n], refs[n], refs[n + 1]
        (copies,) = plan(ins)
        for i, (src, _, device, landing) in enumerate(copies):
            cp = _remote(src, landing, send.at[i], recv.at[i], device)
            cp.wait_send()
            cp.wait_recv()

    res = pl.pallas_call(
        body, name=name, out_shape=tuple(pltpu.HBM(b.shape, b.dtype) for b in bufs),
        in_specs=(*(HBM,) * n, SEM, SEM, *(ANY,) * len(after)), out_specs=(HBM,) * n,
        input_output_aliases={i: i for i in range(n)},
        compiler_params=pltpu.CompilerParams(has_side_effects=EFFECT),
    )(*bufs, sems[0], sems[1], *after)
    return list(res)


def _ici_plan(names, shard_shapes):
    def plan(refs):
        x, y, c = _place()
        copies = []
        for ref, name in zip(refs, names):
            win = _shard_window(name, ref, *shard_shapes[name], 2 * x + y, c)
            for px, py in _other_chips(x, y):
                copies.append((win, win, (px, py, c), _shard_window(name, ref, *shard_shapes[name], 2 * px + py, c)))
        return [copies]
    return plan


def _pass_on_plan(names, shard_shapes):
    def plan(refs):
        x, y, c = _place()
        copies = []
        for ref, name in zip(refs, names):
            for px, py in _other_chips(x, y):
                win = _shard_window(name, ref, *shard_shapes[name], 2 * px + py, c)
                copies.append((win, win, (x, y, 1 - c), _shard_window(name, ref, *shard_shapes[name], 2 * px + py, 1 - c)))
        return [copies]
    return plan


def _swap_plan(n_pairs):
    def plan(refs):
        x, y, c = _place()
        return [[(src.at[:, 1 - c], dst, (x, y, 1 - c), dst) for src, dst in zip(refs[:n_pairs], refs[n_pairs:])]]
    return plan


def _scatter_plan(n_pairs):
    def plan(refs):
        x, y, c = _place()
        copies = []
        for src, dst in zip(refs[:n_pairs], refs[n_pairs:]):
            for k, (px, py) in enumerate(_other_chips(x, y)):
                copies.append((src.at[2 * px + py], dst.at[k], (px, py, c), dst.at[k]))
        return [copies]
    return plan


def _sibling_swap(arrs, name):
    n = len(arrs)

    def body(*refs):
        ins, outs, send, recv = refs[:n], refs[n:2 * n], refs[2 * n], refs[2 * n + 1]
        x, y, c = _place()
        cps = [_remote(ins[w].at[:, 1 - c], outs[w], send.at[w], recv.at[w], (x, y, 1 - c)) for w in range(n)]
        for cp in cps:
            cp.start()
        for cp in cps:
            cp.wait()

    return pl.pallas_call(
        body, name=name, in_specs=[ANY] * n, out_specs=[ANY] * n,
        out_shape=[jax.ShapeDtypeStruct((a.shape[0],) + a.shape[2:], a.dtype) for a in arrs],
        scratch_shapes=[pltpu.SemaphoreType.DMA((n,))] * 2,
    )(*arrs)


def _sibling_share(arrs):
    n = len(arrs)

    def body(*refs):
        ins, outs, send, recv = refs[:n], refs[n:2 * n], refs[2 * n], refs[2 * n + 1]
        x, y, c = _place()
        cps = [_remote(ins[w], outs[w], send.at[w], recv.at[w], (x, y, 1 - c)) for w in range(n)]
        for cp in cps:
            cp.start()
        for cp in cps:
            cp.wait()

    return pl.pallas_call(
        body, name="grad_sibling_share", in_specs=[ANY] * n, out_specs=[ANY] * n,
        out_shape=[jax.ShapeDtypeStruct(a.shape, a.dtype) for a in arrs],
        scratch_shapes=[pltpu.SemaphoreType.DMA((n,))] * 2,
    )(*arrs)


def _small_allreduce(buf, name):
    rows = buf.shape[0]

    def body(x_ref, out_ref, all_ref, send_sems, recv_sems, local_sem):
        x, y, c = _place()
        me, sibling, chips = (x, y, c), (x, y, 1 - c), _other_chips(x, y)

        def block(px, py, pc):
            return all_ref.at[pl.ds((4 * px + 2 * py + pc) * rows, rows), :]

        def copy(k, blk, to, src=None):
            return _remote(block(*blk) if src is None else src, block(*blk), send_sems.at[k], recv_sems.at[k], to)

        own = pltpu.make_async_copy(x_ref, block(*me), local_sem)
        own.start()
        first = [copy(0, me, sibling, src=x_ref)] + [copy(1 + j, me, (*chip, c), src=x_ref) for j, chip in enumerate(chips)]
        for cp in first:
            cp.start()
        passed = [copy(4 + j, (*chip, c), sibling) for j, chip in enumerate(chips)]
        for j, chip in enumerate(chips):
            copy(1 + j, (*chip, c), me).wait_recv()
            passed[j].start()
        copy(0, sibling, me).wait_recv()
        for j, chip in enumerate(chips):
            copy(4 + j, (*chip, 1 - c), me).wait_recv()
        for cp in first + passed:
            cp.wait_send()
        own.wait()
        acc = all_ref[pl.ds(0, rows), :]
        for d in range(1, 8):
            acc = acc + all_ref[pl.ds(d * rows, rows), :]
        out_ref[...] = acc

    vmem = pl.BlockSpec(memory_space=pltpu.VMEM)
    return pl.pallas_call(
        body, name=name, in_specs=[vmem], out_specs=vmem,
        out_shape=jax.ShapeDtypeStruct(buf.shape, F32),
        scratch_shapes=[pltpu.VMEM((8 * rows, 128), F32), pltpu.SemaphoreType.DMA((7,)), pltpu.SemaphoreType.DMA((7,)),
                        pltpu.SemaphoreType.DMA],
    )(buf)


ROW_TILE = 256


def _add_halves(arr, recv, c, name):
    _, _, hr, cols = arr.shape
    tr = _tile(hr, ROW_TILE)

    def body(c_ref, a_ref, r_ref, o_ref):
        o_ref[...] = (a_ref[...].astype(F32) + r_ref[...].astype(F32)).astype(o_ref.dtype)

    piece = pl.BlockSpec((None, tr, cols), lambda j, i, c_ref: (j, i, 0))
    grid_spec = pltpu.PrefetchScalarGridSpec(
        num_scalar_prefetch=1, grid=(N_CHIPS, hr // tr),
        in_specs=[pl.BlockSpec((None, None, tr, cols), lambda j, i, c_ref: (j, c_ref[0], i, 0)), piece], out_specs=piece)
    return pl.pallas_call(body, name=name, grid_spec=grid_spec, out_shape=jax.ShapeDtypeStruct(recv.shape, BF16),
                          compiler_params=_params(("parallel", "parallel")))(c.reshape(1).astype(jnp.int32), arr, recv)


def _flip_slot(d):
    return jnp.where(d == 1, 1, jnp.where(d == 3, 2, 0))


def _sum_chips(p, q, chip, name):
    _, hr, cols = p.shape
    tr = _tile(hr, ROW_TILE)

    def body(chip_ref, p_ref, q_ref, o_ref):
        j = pl.program_id(1)
        term = jnp.where(j == chip_ref[0], p_ref[...].astype(F32), q_ref[...].astype(F32))

        @pl.when(j == 0)
        def _():
            o_ref[...] = term

        @pl.when(j != 0)
        def _():
            o_ref[...] += term

    grid_spec = pltpu.PrefetchScalarGridSpec(
        num_scalar_prefetch=1, grid=(hr // tr, N_CHIPS),
        in_specs=[pl.BlockSpec((None, tr, cols), lambda i, j, chip_ref: (chip_ref[0], i, 0)),
                  pl.BlockSpec((None, tr, cols), lambda i, j, chip_ref: (_flip_slot(j ^ chip_ref[0]), i, 0))],
        out_specs=pl.BlockSpec((tr, cols), lambda i, j, chip_ref: (i, 0)))
    return pl.pallas_call(body, name=name, grid_spec=grid_spec, out_shape=jax.ShapeDtypeStruct((hr, cols), F32),
                          compiler_params=_params(("parallel", "arbitrary")))(chip.reshape(1).astype(jnp.int32), p, q)


def _adamw_halves(w, g_own, g_other, m, v, c, name):
    rows, cols = w.shape
    tr = _tile(rows // 2, ROW_TILE)
    per_half = rows // 2 // tr

    def body(c_ref, w_ref, own_ref, other_ref, m_ref, v_ref, g_ref, d_ref, nm_ref, nv_ref):
        mine = (pl.program_id(0) // per_half) == c_ref[0]
        g_ = jnp.where(mine, own_ref[...], other_ref[...])
        g_ref[...] = g_
        d_ref[...], nm_ref[...], nv_ref[...] = _adamw_math(w_ref[...], g_, m_ref[...], v_ref[...])

    blk = pl.BlockSpec((tr, cols), lambda i, c_ref: (i, 0))
    own = pl.BlockSpec((tr, cols), lambda i, c_ref: (jnp.where(i // per_half == c_ref[0], i % per_half, 0), 0))
    other = pl.BlockSpec((tr, cols), lambda i, c_ref: (jnp.where(i // per_half == c_ref[0], 0, i % per_half), 0))
    grid_spec = pltpu.PrefetchScalarGridSpec(num_scalar_prefetch=1, grid=(rows // tr,),
                                             in_specs=[blk, own, other, blk, blk], out_specs=[blk] * 4)
    return pl.pallas_call(body, name=name, grid_spec=grid_spec, out_shape=[jax.ShapeDtypeStruct(w.shape, F32)] * 4,
                          compiler_params=_params(("parallel",)))(c.reshape(1).astype(jnp.int32), w, g_own, g_other, m, v)


def _adamw_math(w, g, m, v):
    m_new = ADAM_B1 * m + (1.0 - ADAM_B1) * g
    v_new = ADAM_B2 * v + (1.0 - ADAM_B2) * (g * g)
    m_hat = m_new / (1.0 - ADAM_B1 ** ADAM_STEP)
    v_hat = v_new / (1.0 - ADAM_B2 ** ADAM_STEP)
    return -ADAM_LR * (m_hat / (jnp.sqrt(v_hat) + ADAM_EPS) + ADAM_WD * w), m_new, v_new


def _adamw(w, g, m, v, name):
    rows, cols = w.shape
    tr = _tile(rows, ROW_TILE)

    def body(w_ref, g_ref, m_ref, v_ref, d_ref, nm_ref, nv_ref):
        d_ref[...], nm_ref[...], nv_ref[...] = _adamw_math(w_ref[...], g_ref[...], m_ref[...], v_ref[...])

    blk = pl.BlockSpec((tr, cols), lambda i: (i, 0))
    return pl.pallas_call(body, name=name, grid=(rows // tr,), in_specs=[blk] * 4, out_specs=[blk] * 3,
                          out_shape=[jax.ShapeDtypeStruct(w.shape, F32)] * 3, compiler_params=_params(("parallel",)))(w, g, m, v)


VECTORS = ("g_mix", "g_q", "g_k", "g_attn_out", "conv_b", "dt_bias", "a_log", "d_skip", "g_ssm_out", "g_cross", "g_mem",
           "g_cq", "g_ck", "g_mlp")
WEIGHTS = ("g_mix", "w_in", "g_q", "g_k", "g_attn_out", "conv_w", "conv_b", "dt_bias", "a_log", "d_skip", "g_ssm_out", "w_out",
           "g_cross", "g_mem", "w_cq", "w_ckv", "g_cq", "g_ck", "w_co", "g_mlp", "w_up", "w_down")


def _pack(parts):
    flat = jnp.concatenate([t.reshape(-1) for t in parts])
    total = -(-flat.shape[0] // 1024) * 1024
    return jnp.pad(flat, (0, total - flat.shape[0])).reshape(total // 128, 128)


def _unpack(buf, shapes):
    flat, out, pos = buf.reshape(-1), [], 0
    for shape in shapes:
        size = math.prod(shape)
        out.append(flat[pos:pos + size].reshape(shape))
        pos += size
    return out


def kernel(x, mem, positions, g_mix, w_in, g_q, g_k, g_attn_out, conv_w, conv_b, dt_bias, a_log, d_skip, g_ssm_out, w_out, g_cross, g_mem, w_cq, w_ckv, g_cq, g_ck, w_co, g_mlp, w_up, w_down, loss_target, m_g_mix, m_w_in, m_g_q, m_g_k, m_g_attn_out, m_conv_w, m_conv_b, m_dt_bias, m_a_log, m_d_skip, m_g_ssm_out, m_w_out, m_g_cross, m_g_mem, m_w_cq, m_w_ckv, m_g_cq, m_g_ck, m_w_co, m_g_mlp, m_w_up, m_w_down, v_g_mix, v_w_in, v_g_q, v_g_k, v_g_attn_out, v_conv_w, v_conv_b, v_dt_bias, v_a_log, v_d_skip, v_g_ssm_out, v_w_out, v_g_cross, v_g_mem, v_w_cq, v_w_ckv, v_g_cq, v_g_ck, v_w_co, v_g_mlp, v_w_up, v_w_down):
    args = dict(locals())
    weights = {n: args[n][0] for n in WEIGHTS}
    mom_m = {n: args["m_" + n][0] for n in WEIGHTS}
    mom_v = {n: args["v_" + n][0] for n in WEIGHTS}
    x_idx, y_idx, c_idx = _place()
    chip = 2 * x_idx + y_idx

    conv_parts = _small_allreduce(_pack([jnp.zeros((N_CHIPS, 4, 512), F32).at[chip].set(0.5 * weights["conv_w"])]),
                                  "gather_conv_taps")
    shapes = {n: weights[n].shape for n in MATRICES}
    first, mid, late = ("w_in",), ("w_out", "w_cq", "w_ckv", "w_co"), ("w_up", "w_down")
    w_in_buf = [_cast_into_gathered(weights["w_in"], "w_in", chip)]
    sems_in, w_in_buf, token = _split_start("gather_ici_start_w_in", w_in_buf, _ici_plan(first, shapes), [3], after=(conv_parts,))
    bufs = [_cast_into_gathered(weights[n], n, chip, after=(token,)) for n in mid + late]
    sems_rest, bufs, token = _split_start("gather_ici_start_rest", bufs, _ici_plan(mid + late, shapes), [18], after=(token,))
    ici_sems = (sems_in[0], sems_rest[0])
    params = {n: weights[n].reshape(1, -1) for n in VECTORS}
    h_in = _rowwise(_norm_fn, [_full(x[0])], [_full(params["g_mix"])], [(D_MODEL, BF16, D_MODEL, 0, False)], name="norm_in",
                    after=(token,))[0]
    w_in_buf = _split_wait("gather_ici_wait_w_in", w_in_buf, ici_sems[0], _ici_plan(first, shapes), token, h_in)
    pass_sems, w_in_buf, token = _split_start("gather_pass_start_w_in", w_in_buf, _pass_on_plan(first, shapes), [3])
    w_in_buf = _split_wait("gather_pass_wait_w_in", w_in_buf, pass_sems[0], _pass_on_plan(first, shapes), token)
    w_in_full = jnp.transpose(w_in_buf[0], (1, 0, 2)).reshape(D_MODEL, D_MAIN + N_DT)
    full = {"w_in": w_in_full,
            "w_dt": jnp.pad(w_in_full[:, D_MAIN:].reshape(D_MODEL, N_GROUPS, HEADS_PER_GROUP),
                            ((0, 0), (0, 0), (0, 128 - HEADS_PER_GROUP))).reshape(D_MODEL, DT_PAD)}
    in_flight = {}

    def more_weights(stage, after):
        if stage == "mixer_done":
            rest = _split_wait("gather_ici_wait_rest", bufs, ici_sems[1], _ici_plan(mid + late, shapes), after)
            plan = lambda refs: _pass_on_plan(mid, shapes)(refs[:4]) + _pass_on_plan(late, shapes)(refs[4:])
            sems, rest, token = _split_start("gather_pass_start_rest", rest, plan, [12, 6])
            in_flight["late"] = (rest[4:], sems[1])
            return dict(zip(mid, _split_wait("gather_pass_wait_mid", rest[:4], sems[0], _pass_on_plan(mid, shapes), token)))
        late_bufs, sems = in_flight.pop("late")
        return dict(zip(late, _split_wait("gather_pass_wait_late", late_bufs, sems, _pass_on_plan(late, shapes), after)))

    params["conv_w"] = _unpack(conv_parts, [(N_CHIPS, 4, 512)])[0].transpose(1, 0, 2).reshape(4, 4 * 512)

    groups = (("w_down",), ("w_up",), ("w_co", "w_cq", "w_ckv", "w_out"), ("w_in",))
    scattered = []

    class GradStore(dict):
        pending = None

        def __setitem__(self, name, value):
            super().__setitem__(name, value)
            if "w_main" in self and "w_dt" in self and "w_in" not in self:
                gw_in = lax.dynamic_update_slice(self["w_main"], _unpad_heads(self["w_dt"]), (0, D_MAIN))
                self["w_in"] = gw_in.reshape(D_MODEL, N_CHIPS, gw_in.shape[1] // N_CHIPS).transpose(1, 0, 2)
            for group in groups:
                if name in group and all(n in self for n in group):
                    self.settle()
                    pieces = [self[n].reshape(N_CHIPS, 2, shapes[n][0] // 2, shapes[n][1]) for n in group]
                    if group == groups[-1]:
                        self.scatter(group, pieces, _sibling_swap(pieces, "grad_swap_" + group[0]))
                    else:
                        landing = [lax.empty((N_CHIPS,) + a.shape[2:], BF16) for a in pieces]
                        sems, thru, self.token = _split_start("grad_swap_start_" + group[0], pieces + landing,
                                                              _swap_plan(len(pieces)), [len(pieces)])
                        self.pending = (group, sems[0], thru)

        def settle(self, *after):
            if self.pending is not None:
                group, sems, thru = self.pending
                self.pending = None
                thru = _split_wait("grad_swap_wait_" + group[0], thru, sems, _swap_plan(len(group)), *after)
                self.scatter(group, thru[:len(group)], thru[len(group):])

        def scatter(self, group, pieces, from_sibling):
            sums = [_add_halves(a, r, c_idx, "add_halves_" + n) for n, a, r in zip(group, pieces, from_sibling)]
            landing = [lax.empty((3,) + s.shape[1:], BF16) for s in sums]
            sems, thru, self.token = _split_start("grad_scatter_start_" + group[0], sums + landing,
                                                  _scatter_plan(len(sums)), [3 * len(sums)])
            scattered.append((group, sems[0], thru))

    loss, grad_x, grads = _local_step(x[0], mem[0], positions[0], loss_target[0], params, full, more_weights, GradStore(),
                                      h_in)

    halves = {}
    for group, sems, thru in scattered:
        thru = _split_wait("grad_scatter_wait_" + group[0], thru, sems, _scatter_plan(len(group)), grad_x)
        for i, n in enumerate(group):
            halves[n] = _sum_chips(thru[i], thru[len(group) + i], chip, "sum_chips_" + n)
    other_halves = dict(zip(MATRICES, _sibling_share([halves[n] for n in MATRICES])))
    out_g, out_d, out_m, out_v = {}, {}, {}, {}
    for n in MATRICES:
        out_g[n], out_d[n], out_m[n], out_v[n] = _adamw_halves(weights[n], halves[n], other_halves[n], mom_m[n], mom_v[n],
                                                               c_idx, "adamw_" + n)

    small = [grads[n] for n in VECTORS] + [grads["conv_w"]]
    summed = _unpack(_small_allreduce(_pack(small), "allreduce_vectors"), [t.shape for t in small])
    g_small = dict(zip(VECTORS, summed[:-1]))
    g_small["conv_w"] = lax.dynamic_slice_in_dim(summed[-1], chip * 512, 512, axis=1)
    names = VECTORS + ("conv_w",)
    shapes = [weights[n].shape for n in names]
    packed = [_pack([src[n] for n in names]) for src in (weights, g_small, mom_m, mom_v)]
    small_out = [_unpack(t, shapes) for t in _adamw(*packed, "adamw_small")]
    for i, n in enumerate(names):
        out_g[n] = g_small[n].reshape(shapes[i])
        out_d[n], out_m[n], out_v[n] = small_out[0][i], small_out[1][i], small_out[2][i]

    total_loss = lax.psum(loss[0, 0], ("x", "y", "c"))
    outs = [total_loss, grad_x[None]]
    for group in (out_g, out_d, out_m, out_v):
        outs += [group[n][None] for n in WEIGHTS]
    return tuple(outs)
```

```python
import functools
import math

import jax
import jax.numpy as jnp
from jax import lax
from jax.experimental import pallas as pl
from jax.experimental.pallas import tpu as pltpu

F32 = jnp.float32
BF16 = jnp.bfloat16

SEQ = 2048
D_MODEL = 2048
HEAD = 64
D_ATTN = 1024
D_SSM = 1024
N_GROUPS = 4
N_STATE = 128
CHUNK = 128
ATT_BLK = 128
N_MEM = 256
D_CROSS = 512
D_FF = 8192
D_MAIN = 6144
N_DT = 16
DT_PAD = 512
ROT = 16
ROPE_THETA = 500000.0
EPS = 1e-6
NEG = -1e30
BRANCH_BLOCKS = (16, 4, 1)
DILATIONS = (1, 4, 16)

ADAM_LR, ADAM_B1, ADAM_B2, ADAM_EPS, ADAM_WD, ADAM_STEP = 0.001, 0.9, 0.999, 1e-08, 0.01, 10

VMEM_LIMIT = 56 * 1024 * 1024
MESH = pl.DeviceIdType.MESH


def _params(sem, **kw):
    return pltpu.CompilerParams(dimension_semantics=sem, vmem_limit_bytes=VMEM_LIMIT, **kw)


def _bdot(a, b, dims):
    return lax.dot_general(a.astype(BF16), b.astype(BF16), (dims, ((), ())), preferred_element_type=F32)


def _fdot(a, b, dims):
    return lax.dot_general(a, b, (dims, ((), ())), preferred_element_type=F32, precision=lax.Precision.HIGHEST)


NN = ((1,), (0,))
NT = ((1,), (1,))
TN = ((0,), (0,))


def _tile(n, want):
    t = min(n, want)
    while n % t:
        t //= 2
    return t


def _matmul(a, b, *, mode, name, outs, extra=(), epilogue=None, col_shards=1, after=(), n_cols=None, out_cols=None,
            tm=1024, tn=1024, tk=2048):
    if mode == "nn":
        (m, k), n = a.shape, b.shape[1]
    elif mode == "nt":
        (m, k), n = a.shape, b.shape[0]
    else:
        (k, m), n = a.shape, b.shape[1]
    n = n if n_cols is None else n_cols
    tm, tn, tk = _tile(m, tm), _tile(n // col_shards, tn), _tile(k, tk)
    nk = k // tk
    per_shard = n // col_shards // tn
    dims = {"nn": NN, "nt": NT, "tn": TN}[mode]
    a_spec = pl.BlockSpec((tk, tm), lambda i, j, kk: (kk, i)) if mode == "tn" else pl.BlockSpec((tm, tk), lambda i, j, kk: (i, kk))
    b_spec = pl.BlockSpec((tn, tk), lambda i, j, kk: (j, kk)) if mode == "nt" else pl.BlockSpec((tk, tn), lambda i, j, kk: (kk, j))
    o_spec = pl.BlockSpec((tm, tn), lambda i, j, kk: (i, j))
    n_extra, n_out, n_after = len(extra), len(outs), len(after)

    def body(a_ref, b_ref, *rest):
        extra_refs, out_refs, acc_ref = rest[:n_extra], rest[n_extra + n_after:n_extra + n_after + n_out], rest[-1]
        def finish(acc):
            res = (acc,) if epilogue is None else epilogue(acc, *[e[...] for e in extra_refs])
            for o_ref, r in zip(out_refs, res):
                o_ref[...] = r.astype(o_ref.dtype)

        if nk == 1:
            finish(_bdot(a_ref[...], b_ref[...], dims))
            return
        kk = pl.program_id(2)

        @pl.when(kk == 0)
        def _():
            acc_ref[...] = jnp.zeros_like(acc_ref)

        acc_ref[...] += _bdot(a_ref[...], b_ref[...], dims)

        @pl.when(kk == nk - 1)
        def _():
            finish(acc_ref[...])

    if col_shards == 1:
        out_specs, out_dims = [o_spec] * n_out, (m, n if out_cols is None else out_cols)
    else:
        sharded = pl.BlockSpec((None, tm, tn), lambda i, j, kk: (j // per_shard, i, j % per_shard))
        out_specs, out_dims = [sharded] * n_out, (col_shards, m, n // col_shards)
    res = pl.pallas_call(
        body, name=name, grid=(m // tm, n // tn, nk),
        in_specs=[a_spec, b_spec] + [o_spec] * n_extra + [pl.BlockSpec(memory_space=pl.ANY)] * n_after,
        out_specs=out_specs,
        out_shape=[jax.ShapeDtypeStruct(out_dims, dt) for dt in outs],
        scratch_shapes=[pltpu.VMEM((tm, tn) if nk > 1 else (8, 128), F32)],
        compiler_params=_params(("parallel", "parallel", "arbitrary")),
    )(a, b, *extra, *after)
    return res[0] if n_out == 1 else res


def _row_spec(tr, bw, cb, per_group):
    return pl.BlockSpec((tr, bw), (lambda g, i: (i, cb + g)) if per_group else (lambda g, i: (i, cb)))


def _vec_spec(bw, cb, per_group):
    return pl.BlockSpec((1, bw), (lambda g, i: (0, cb + g)) if per_group else (lambda g, i: (0, cb)))


def _rowwise(fn, rows, vecs, outs, *, name, n_rows=SEQ, tr=256, groups=1, after=()):
    n_r, n_v, n_after = len(rows), len(vecs), len(after)

    def body(*refs):
        vals = [r[...].astype(F32) for r in refs[:n_r + n_v]]
        res = fn(*vals)
        for o_ref, r in zip(refs[n_r + n_v + n_after:], res):
            o_ref[...] = r.astype(o_ref.dtype)

    res = pl.pallas_call(
        body, name=name, grid=(groups, n_rows // tr),
        in_specs=[_row_spec(tr, bw, cb, pg) for _, bw, cb, pg in rows] + [_vec_spec(bw, cb, pg) for _, bw, cb, pg in vecs]
        + [pl.BlockSpec(memory_space=pl.ANY)] * n_after,
        out_specs=[_row_spec(tr, bw, cb, pg) for _, _, bw, cb, pg in outs],
        out_shape=[jax.ShapeDtypeStruct((n_rows, w), dt) for w, dt, _, _, _ in outs],
        compiler_params=_params(("parallel", "parallel")),
    )(*[r[0] for r in rows], *[v[0] for v in vecs], *after)
    return res


def _rowwise_vjp(fn, rows, vecs, cts, row_grads, vec_grads, *, name, n_rows=SEQ, tr=256, groups=1, after=()):
    n_r, n_v, n_after = len(rows), len(vecs), len(after)
    ct_ops = [op for group in cts for op in group]
    ct_sizes = [len(group) for group in cts]
    res_ops = [g[6] for g in row_grads if g[6] is not None]
    n_ct, n_res, n_rg = len(ct_ops), len(res_ops), len(row_grads)

    def body(*refs):
        vals = [r[...].astype(F32) for r in refs[:n_r + n_v]]
        pos = n_r + n_v
        ct_vals = []
        for size in ct_sizes:
            acc = refs[pos][...].astype(F32)
            for t in range(1, size):
                acc = acc + refs[pos + t][...].astype(F32)
            ct_vals.append(acc)
            pos += size
        res_refs = refs[pos:pos + n_res]
        out_refs = refs[pos + n_res + n_after:]
        _, pullback = jax.vjp(fn, *vals)
        grads = pullback(tuple(ct_vals))
        r_i = 0
        for o_ref, g in zip(out_refs[:n_rg], row_grads):
            val = grads[g[0]]
            if g[6] is not None:
                val = val + res_refs[r_i][...].astype(F32)
                r_i += 1
            o_ref[...] = val.astype(o_ref.dtype)
        first = (pl.program_id(1) == 0)
        for o_ref, g in zip(out_refs[n_rg:], vec_grads):
            val = jnp.sum(grads[n_r + g[0]], axis=0, keepdims=True)
            init = first if g[4] else jnp.logical_and(first, pl.program_id(0) == 0)

            @pl.when(init)
            def _(o_ref=o_ref, val=val):
                o_ref[...] = val

            @pl.when(jnp.logical_not(init))
            def _(o_ref=o_ref, val=val):
                o_ref[...] += val

    in_specs = [_row_spec(tr, bw, cb, pg) for _, bw, cb, pg in rows] + [_vec_spec(bw, cb, pg) for _, bw, cb, pg in vecs]
    in_specs += [_row_spec(tr, bw, cb, pg) for _, bw, cb, pg in ct_ops + res_ops] + [pl.BlockSpec(memory_space=pl.ANY)] * n_after
    out_specs =[_row_spec(tr, g[3], g[4], g[5]) for g in row_grads] + [_vec_spec(g[2], g[3], g[4]) for g in vec_grads]
    out_shape = [jax.ShapeDtypeStruct((n_rows, g[1]), g[2]) for g in row_grads]
    out_shape += [jax.ShapeDtypeStruct((1, g[1]), F32) for g in vec_grads]
    return pl.pallas_call(
        body, name=name, grid=(groups, n_rows // tr),
        in_specs=in_specs, out_specs=out_specs, out_shape=out_shape,
        compiler_params=_params(("arbitrary", "arbitrary")),
    )(*[r[0] for r in rows], *[v[0] for v in vecs], *[c[0] for c in ct_ops], *[r[0] for r in res_ops], *after)


def _full(arr, width=None):
    return (arr, arr.shape[1] if width is None else width, 0, False)


def _make_xor(sh):
    def raw(x):
        n = x.shape[-1]
        lane = lax.broadcasted_iota(jnp.int32, x.shape, x.ndim - 1)
        up = pltpu.roll(x, n - sh, x.ndim - 1)
        down = pltpu.roll(x, sh, x.ndim - 1)
        return jnp.where((lane & sh) == 0, up, down)

    f = jax.custom_vjp(raw)
    f.defvjp(lambda x: (raw(x), None), lambda _, ct: (raw(ct),))
    return f


_SWAP_ROPE_HALVES = _make_xor(ROT // 2)


def _head_sum(x):
    n = x.shape[-1]
    same_head = (lax.broadcasted_iota(jnp.int32, (n, n), 0) // HEAD) == (lax.broadcasted_iota(jnp.int32, (n, n), 1) // HEAD)
    return _fdot(x, same_head.astype(F32), NN)


def _rms(x, g):
    return x * lax.rsqrt(jnp.mean(x * x, axis=-1, keepdims=True) + EPS) * g


def _head_rms_rope(x, g, cos, sin, scale):
    y = x * lax.rsqrt(_head_sum(x * x) * (1.0 / HEAD) + EPS) * g
    return (y * cos + _SWAP_ROPE_HALVES(y) * sin) * scale


def _qk_fn(q, k, v, cos, sin, gq, gk):
    return (_head_rms_rope(q, gq, cos, sin, HEAD ** -0.5), _head_rms_rope(k, gk, cos, sin, 1.0), v)


def _norm_fn(x, g):
    return (_rms(x, g),)


def _merge_fn(o0, o1, o2, l0, l1, l2, g):
    m = lax.stop_gradient(jnp.maximum(jnp.maximum(l0, l1), l2))
    e0, e1, e2 = jnp.exp(l0 - m), jnp.exp(l1 - m), jnp.exp(l2 - m)
    mix = (e0 * o0 + e1 * o1 + e2 * o2) / (e0 + e1 + e2)
    return (_rms(mix, g),)


def _gate_fn(y, z, g):
    return (_rms(y * (z * jax.nn.sigmoid(z)), g),)


def _attn_pair(q, kc, vc, kp=None, vp=None, has_prev=None):
    qi = lax.broadcasted_iota(jnp.int32, (ATT_BLK, ATT_BLK), 0)
    kj = lax.broadcasted_iota(jnp.int32, (ATT_BLK, ATT_BLK), 1)
    lane = lax.broadcasted_iota(jnp.int32, (1, 2 * HEAD), 1)
    o, lse = 0.0, 0.0
    for h in range(2):
        pick = ((lane >= h * HEAD) & (lane < (h + 1) * HEAD)).astype(F32)
        qh = q * pick
        s_c = jnp.where(qi >= kj, _bdot(qh, kc, NT), NEG)
        m = jnp.max(s_c, axis=-1, keepdims=True)
        if kp is not None:
            s_p = jnp.where(jnp.logical_and(kj >= qi, has_prev), _bdot(qh, kp, NT), NEG)
            m = jnp.maximum(m, jnp.max(s_p, axis=-1, keepdims=True))
        m = lax.stop_gradient(m)
        p_c = jnp.exp(s_c - m)
        den = jnp.sum(p_c, axis=-1, keepdims=True)
        acc = _bdot(p_c, vc, NN)
        if kp is not None:
            p_p = jnp.exp(s_p - m)
            den = den + jnp.sum(p_p, axis=-1, keepdims=True)
            acc = acc + _bdot(p_p, vp, NN)
        o = o + (pick * (1.0 / den)) * acc
        lse = lse + pick * (m + jnp.log(den))
    return o, lse


def _attn_config(b):
    r = DILATIONS[b]
    return r, ATT_BLK * r, (512 if r == 1 else 128), BRANCH_BLOCKS[b] > 1


def _for_residues(r, fn):
    if r <= 4:
        for rho in range(r):
            fn(rho)
    else:
        def step(t, carry):
            for u in range(4):
                fn(4 * t + u)
            return carry

        lax.fori_loop(0, r // 4, step, 0)


def _strided_rows(start, r):
    if r > 1:
        return pl.ds(start, ATT_BLK, stride=r)
    return pl.ds(start if isinstance(start, int) else pl.multiple_of(start, ATT_BLK), ATT_BLK)


def _attention_fwd(qn, kn, vn, b):
    r, rows, lanes, with_prev = _attn_config(b)
    cur = pl.BlockSpec((rows, lanes), lambda g, n: (n, g))
    prev = pl.BlockSpec((rows, lanes), lambda g, n: (jnp.maximum(n - 1, 0), g))

    def body(*refs):
        ins, (o_ref, l_ref) = refs[:-2], refs[-2:]
        has_prev = pl.program_id(1) > 0

        def one(rho):
            sub = _strided_rows(rho, r)
            for pair in range(lanes // 128):
                sl = pl.ds(pair * 128, 128)
                args = [ref[sub, sl] for ref in ins] + ([has_prev] if with_prev else [])
                o_ref[sub, sl], l_ref[sub, sl] = _attn_pair(*args)

        _for_residues(r, one)

    operands = (qn, kn, vn, kn, vn) if with_prev else (qn, kn, vn)
    return pl.pallas_call(
        body, name="attn_fwd_%d" % r, grid=(D_ATTN // lanes, SEQ // rows),
        in_specs=[cur, cur, cur] + ([prev, prev] if with_prev else []), out_specs=[cur, cur],
        out_shape=[jax.ShapeDtypeStruct((SEQ, D_ATTN), F32)] * 2,
        compiler_params=_params(("parallel", "parallel")),
    )(*operands)


def _attn_pair_bwd(q, kc, vc, kp, vp, o, lse, do, dl, has_prev):
    qi = lax.broadcasted_iota(jnp.int32, (ATT_BLK, ATT_BLK), 0)
    kj = lax.broadcasted_iota(jnp.int32, (ATT_BLK, ATT_BLK), 1)
    lane = lax.broadcasted_iota(jnp.int32, (1, 2 * HEAD), 1)
    tiles = [(kc, vc, qi >= kj)]
    if kp is not None:
        tiles.append((kp, vp, jnp.logical_and(kj >= qi, has_prev)))
    dq = 0.0
    dks, dvs = [0.0] * len(tiles), [0.0] * len(tiles)
    for h in range(2):
        pick = ((lane >= h * HEAD) & (lane < (h + 1) * HEAD)).astype(F32)
        qh, doh = q * pick, do * pick
        lse_h = jnp.sum(lse * (lane == h * HEAD).astype(F32), axis=-1, keepdims=True)
        base = jnp.sum(dl * pick - doh * o, axis=-1, keepdims=True)
        dq_h = 0.0
        for t, (k_, v_, mask) in enumerate(tiles):
            p = jnp.exp(jnp.where(mask, _bdot(qh, k_, NT), NEG) - lse_h)
            ds = p * (_bdot(doh, v_, NT) + base)
            dq_h = dq_h + _bdot(ds, k_, NN)
            dks[t] = dks[t] + _bdot(ds, qh, TN)
            dvs[t] = dvs[t] + _bdot(p, doh, TN)
        dq = dq + pick * dq_h
    return (dq, dks[0], dvs[0]) + ((dks[1], dvs[1]) if kp is not None else ())


def _attention_bwd(qn, kn, vn, o, lse, do, dl, b):
    r, rows, lanes, with_prev = _attn_config(b)
    cur = pl.BlockSpec((rows, lanes), lambda g, n: (n, g))
    prev = pl.BlockSpec((rows, lanes), lambda g, n: (jnp.maximum(n - 1, 0), g))
    whole = pl.BlockSpec((SEQ, lanes), lambda g, n: (0, g))
    n_in = 5 if with_prev else 3

    def body(*refs):
        ins, (o_ref, l_ref, do_ref, dl_ref, dq_ref, dk_ref, dv_ref) = refs[:n_in], refs[n_in:]
        n = pl.program_id(1)

        @pl.when(n == 0)
        def _():
            dk_ref[...] = jnp.zeros_like(dk_ref)
            dv_ref[...] = jnp.zeros_like(dv_ref)

        def one(rho):
            sub = _strided_rows(rho, r)
            sub_c = _strided_rows(n * rows + rho, r)
            sub_p = _strided_rows(jnp.maximum(n - 1, 0) * rows + rho, r)
            for pair in range(lanes // 128):
                sl = pl.ds(pair * 128, 128)
                vals = [ref[sub, sl] for ref in ins] + ([] if with_prev else [None, None])
                grads = _attn_pair_bwd(*vals, o_ref[sub, sl], l_ref[sub, sl], do_ref[sub, sl], dl_ref[sub, sl], n > 0)
                dq_ref[sub, sl] = grads[0]
                dk_ref[sub_c, sl] += grads[1]
                dv_ref[sub_c, sl] += grads[2]
                if with_prev:
                    dk_ref[sub_p, sl] += grads[3]
                    dv_ref[sub_p, sl] += grads[4]

        _for_residues(r, one)

    operands = (qn, kn, vn, kn, vn) if with_prev else (qn, kn, vn)
    return pl.pallas_call(
        body, name="attn_bwd_%d" % r, grid=(D_ATTN // lanes, SEQ // rows),
        in_specs=[cur, cur, cur] + ([prev, prev] if with_prev else []) + [cur] * 4, out_specs=[cur, whole, whole],
        out_shape=[jax.ShapeDtypeStruct((SEQ, D_ATTN), F32)] * 3,
        compiler_params=_params(("parallel", "arbitrary")),
    )(*operands, o, lse, do, dl)


CONV_COLS = 256
XBC_BLOCK0 = 4096 // CONV_COLS


def _shift_rows(x, s):
    n = x.shape[0]
    t = lax.broadcasted_iota(jnp.int32, x.shape, 0)
    if s >= 0:
        return jnp.where(t >= s, pltpu.roll(x, s, 0), 0.0)
    return jnp.where(t < n + s, pltpu.roll(x, n + s, 0), 0.0)


def _conv_pre(x, w_ref, b_ref):
    pre = b_ref[...] + w_ref[3:4, :] * x
    for k in range(3):
        pre = pre + w_ref[k:k + 1, :] * _shift_rows(x, 3 - k)
    return pre


def _conv_fwd(proj, conv_w, conv_b):
    cols = conv_w.shape[1]

    def body(x_ref, w_ref, b_ref, o_ref):
        pre = _conv_pre(x_ref[...], w_ref, b_ref)
        o_ref[...] = pre * jax.nn.sigmoid(pre)

    blk = pl.BlockSpec((SEQ, CONV_COLS), lambda j: (0, j))
    return pl.pallas_call(
        body, name="conv_fwd", grid=(cols // CONV_COLS,),
        in_specs=[pl.BlockSpec((SEQ, CONV_COLS), lambda j: (0, XBC_BLOCK0 + j)),
                  pl.BlockSpec((4, CONV_COLS), lambda j: (0, j)), pl.BlockSpec((1, CONV_COLS), lambda j: (0, j))],
        out_specs=blk, out_shape=jax.ShapeDtypeStruct((SEQ, cols), F32),
        compiler_params=_params(("parallel",)),
    )(proj, conv_w, conv_b)


def _conv_bwd(proj, conv_w, conv_b, dy):
    cols = conv_w.shape[1]

    def body(x_ref, w_ref, b_ref, dy_ref, dx_ref, dw_ref, db_ref):
        x = x_ref[...]
        pre = _conv_pre(x, w_ref, b_ref)
        sg = jax.nn.sigmoid(pre)
        dpre = dy_ref[...] * (sg * (1.0 + pre * (1.0 - sg)))
        db_ref[...] = jnp.sum(dpre, axis=0, keepdims=True)
        dx = w_ref[3:4, :] * dpre
        dw_ref[3:4, :] = jnp.sum(dpre * x, axis=0, keepdims=True)
        for k in range(3):
            dx = dx + w_ref[k:k + 1, :] * _shift_rows(dpre, k - 3)
            dw_ref[k:k + 1, :] = jnp.sum(dpre * _shift_rows(x, 3 - k), axis=0, keepdims=True)
        dw_ref[4:8, :] = jnp.zeros((4, CONV_COLS), F32)
        dx_ref[...] = dx.astype(dx_ref.dtype)

    blk = pl.BlockSpec((SEQ, CONV_COLS), lambda j: (0, j))
    return pl.pallas_call(
        body, name="conv_bwd", grid=(cols // CONV_COLS,),
        in_specs=[pl.BlockSpec((SEQ, CONV_COLS), lambda j: (0, XBC_BLOCK0 + j)),
                  pl.BlockSpec((4, CONV_COLS), lambda j: (0, j)), pl.BlockSpec((1, CONV_COLS), lambda j: (0, j)), blk],
        out_specs=[blk, pl.BlockSpec((8, CONV_COLS), lambda j: (0, j)), pl.BlockSpec((1, CONV_COLS), lambda j: (0, j))],
        out_shape=[jax.ShapeDtypeStruct((SEQ, cols), BF16), jax.ShapeDtypeStruct((8, cols), F32),
                   jax.ShapeDtypeStruct((1, cols), F32)],
        compiler_params=_params(("parallel",)),
    )(proj, conv_w, conv_b, dy)


HEADS_PER_GROUP = 4


def _ssd_chunk(x0, x1, x2, x3, bm, cm, dtr, bias, alog, dsk, h0, h1, h2, h3):
    xs, hs = (x0, x1, x2, x3), (h0, h1, h2, h3)
    row = lax.broadcasted_iota(jnp.int32, (CHUNK, CHUNK), 0)
    col = lax.broadcasted_iota(jnp.int32, (CHUNK, CHUNK), 1)
    causal = row >= col
    tril = causal.astype(F32)
    z = dtr + bias
    dt = jnp.maximum(z, 0.0) + jnp.log(1.0 + jnp.exp(-jnp.abs(z)))
    a = -jnp.exp(alog)
    acs = _fdot(tril, dt * a, NN)
    acs_t, dt_t = acs.T, dt.T
    cb = _bdot(cm, bm, NT)
    lane = lax.broadcasted_iota(jnp.int32, (1, CHUNK), 1)
    sub = lax.broadcasted_iota(jnp.int32, (CHUNK, 1), 0)
    ys, hn = [], []
    for j in range(HEADS_PER_GROUP):
        on_lane, on_sub = (lane == j).astype(F32), (sub == j).astype(F32)
        acs_c = jnp.sum(acs * on_lane, axis=1, keepdims=True)
        dt_c = jnp.sum(dt * on_lane, axis=1, keepdims=True)
        acs_r = jnp.sum(acs_t * on_sub, axis=0, keepdims=True)
        dt_r = jnp.sum(dt_t * on_sub, axis=0, keepdims=True)
        acs_last = jnp.sum(acs_c * (sub == CHUNK - 1).astype(F32), axis=0, keepdims=True)
        d_j = jnp.sum(dsk * on_lane, axis=1, keepdims=True)
        decay = jnp.exp(jnp.where(causal, acs_c - acs_r, NEG))
        w = cb * decay * dt_r
        y_diag = _bdot(w, xs[j], NN)
        y_off = _bdot(cm, hs[j], NT) * jnp.exp(acs_c)
        ys.append(y_diag + y_off + d_j * xs[j])
        state = _bdot(xs[j] * (jnp.exp(acs_last - acs_c) * dt_c), bm, TN)
        hn.append(hs[j] * jnp.exp(acs_last) + state)
    return (*ys, *hn)


def _ssd_specs(reverse):
    n_chunks = SEQ // CHUNK
    c_of = (lambda c: n_chunks - 1 - c) if reverse else (lambda c: c)
    x_spec = pl.BlockSpec((CHUNK, 256), lambda g, c: (c_of(c), g))
    b_spec = pl.BlockSpec((CHUNK, N_STATE), lambda g, c: (c_of(c), 8 + g))
    c_spec = pl.BlockSpec((CHUNK, N_STATE), lambda g, c: (c_of(c), 12 + g))
    dt_spec = pl.BlockSpec((CHUNK, 128), lambda g, c: (c_of(c), g))
    vec_spec = pl.BlockSpec((1, 128), lambda g, c: (0, g))
    h_spec = pl.BlockSpec((1, 1, HEADS_PER_GROUP, HEAD, N_STATE), lambda g, c: (c_of(c), g, 0, 0, 0))
    return x_spec, b_spec, c_spec, dt_spec, vec_spec, h_spec


def _ssd_fwd(xbc, dt_raw, bias, alog, dsk):
    x_spec, b_spec, c_spec, dt_spec, vec_spec, h_spec = _ssd_specs(False)

    def body(x_ref, b_ref, c_ref, dt_ref, bias_ref, alog_ref, dsk_ref, y_ref, hin_ref, h_scr):
        @pl.when(pl.program_id(1) == 0)
        def _():
            h_scr[...] = jnp.zeros_like(h_scr)

        hs = [h_scr[j] for j in range(HEADS_PER_GROUP)]
        for j in range(HEADS_PER_GROUP):
            hin_ref[0, 0, j] = hs[j]
        xs = [x_ref[:, pl.ds(j * HEAD, HEAD)] for j in range(HEADS_PER_GROUP)]
        res = _ssd_chunk(*xs, b_ref[...], c_ref[...], dt_ref[...], bias_ref[...], alog_ref[...], dsk_ref[...], *hs)
        for j in range(HEADS_PER_GROUP):
            y_ref[:, pl.ds(j * HEAD, HEAD)] = res[j]
            h_scr[j] = res[HEADS_PER_GROUP + j]

    return pl.pallas_call(
        body, name="ssd_fwd", grid=(N_GROUPS, SEQ // CHUNK),
        in_specs=[x_spec, b_spec, c_spec, dt_spec, vec_spec, vec_spec, vec_spec],
        out_specs=[x_spec, h_spec],
        out_shape=[jax.ShapeDtypeStruct((SEQ, D_SSM), F32),
                   jax.ShapeDtypeStruct((SEQ // CHUNK, N_GROUPS, HEADS_PER_GROUP, HEAD, N_STATE), F32)],
        scratch_shapes=[pltpu.VMEM((HEADS_PER_GROUP, HEAD, N_STATE), F32)],
        compiler_params=_params(("parallel", "arbitrary")),
    )(xbc, xbc, xbc, dt_raw, bias, alog, dsk)


def _ssd_bwd(xbc, dt_raw, bias, alog, dsk, h_in, dy):
    x_spec, b_spec, c_spec, dt_spec, vec_spec, h_spec = _ssd_specs(True)
    dxbc_x = pl.BlockSpec((CHUNK, 256), x_spec.index_map)

    def body(x_ref, b_ref, c_ref, dt_ref, bias_ref, alog_ref, dsk_ref, hin_ref, dy_ref,
             dx_ref, db_ref, dc_ref, ddt_ref, dbias_ref, dalog_ref, ddsk_ref, dh_scr):
        first = pl.program_id(1) == 0

        @pl.when(first)
        def _():
            dh_scr[...] = jnp.zeros_like(dh_scr)

        xs = [x_ref[:, pl.ds(j * HEAD, HEAD)] for j in range(HEADS_PER_GROUP)]
        hs = [hin_ref[0, 0, j] for j in range(HEADS_PER_GROUP)]
        cts = [dy_ref[:, pl.ds(j * HEAD, HEAD)] for j in range(HEADS_PER_GROUP)] + [dh_scr[j] for j in range(HEADS_PER_GROUP)]
        _, pullback = jax.vjp(_ssd_chunk, *xs, b_ref[...], c_ref[...], dt_ref[...], bias_ref[...], alog_ref[...],
                              dsk_ref[...], *hs)
        g = pullback(tuple(cts))
        for j in range(HEADS_PER_GROUP):
            dx_ref[:, pl.ds(j * HEAD, HEAD)] = g[j]
            dh_scr[j] = g[10 + j]
        db_ref[...] = g[4]
        dc_ref[...] = g[5]
        ddt_ref[...] = g[6].astype(ddt_ref.dtype)
        for o_ref, val in ((dbias_ref, g[7]), (dalog_ref, g[8]), (ddsk_ref, g[9])):
            @pl.when(first)
            def _(o_ref=o_ref, val=val):
                o_ref[...] = val

            @pl.when(jnp.logical_not(first))
            def _(o_ref=o_ref, val=val):
                o_ref[...] += val

    n_chunks = SEQ // CHUNK
    out_b = pl.BlockSpec((CHUNK, N_STATE), lambda g, c: (n_chunks - 1 - c, g))
    res = pl.pallas_call(
        body, name="ssd_bwd", grid=(N_GROUPS, n_chunks),
        in_specs=[x_spec, b_spec, c_spec, dt_spec, vec_spec, vec_spec, vec_spec, h_spec, x_spec],
        out_specs=[dxbc_x, out_b, out_b, dt_spec, vec_spec, vec_spec, vec_spec],
        out_shape=[jax.ShapeDtypeStruct((SEQ, D_SSM), F32), jax.ShapeDtypeStruct((SEQ, N_GROUPS * N_STATE), F32),
                   jax.ShapeDtypeStruct((SEQ, N_GROUPS * N_STATE), F32), jax.ShapeDtypeStruct((SEQ, DT_PAD), BF16),
                   jax.ShapeDtypeStruct((1, DT_PAD), F32), jax.ShapeDtypeStruct((1, DT_PAD), F32),
                   jax.ShapeDtypeStruct((1, DT_PAD), F32)],
        scratch_shapes=[pltpu.VMEM((HEADS_PER_GROUP, HEAD, N_STATE), F32)],
        compiler_params=_params(("parallel", "arbitrary")),
    )(xbc, xbc, xbc, dt_raw, bias, alog, dsk, h_in, dy)
    return res


CROSS_HEAD = 128
CROSS_ROWS = 512


def _cross_head(q, k, v, gq, gk):
    qn = _rms(q, gq) * (CROSS_HEAD ** -0.5)
    kn = _rms(k, gk)
    s = _bdot(qn, kn, NT)
    p = jnp.exp(s - lax.stop_gradient(jnp.max(s, axis=-1, keepdims=True)))
    return _bdot(p, v, NN) * (1.0 / jnp.sum(p, axis=-1, keepdims=True))


def _cross_specs():
    q_spec = pl.BlockSpec((CROSS_ROWS, CROSS_HEAD), lambda h, i: (i, h))
    k_spec = pl.BlockSpec((N_MEM, CROSS_HEAD), lambda h, i: (0, h))
    v_spec = pl.BlockSpec((N_MEM, CROSS_HEAD), lambda h, i: (0, 4 + h))
    g_spec = pl.BlockSpec((1, CROSS_HEAD), lambda h, i: (0, 0))
    return q_spec, k_spec, v_spec, g_spec


def _cross_fwd(qc, kv, gq, gk):
    q_spec, k_spec, v_spec, g_spec = _cross_specs()

    def body(q_ref, k_ref, v_ref, gq_ref, gk_ref, o_ref):
        o_ref[...] = _cross_head(q_ref[...], k_ref[...], v_ref[...], gq_ref[...], gk_ref[...]).astype(o_ref.dtype)

    return pl.pallas_call(
        body, name="cross_fwd", grid=(4, SEQ // CROSS_ROWS),
        in_specs=[q_spec, k_spec, v_spec, g_spec, g_spec], out_specs=q_spec,
        out_shape=jax.ShapeDtypeStruct((SEQ, D_CROSS), BF16),
        compiler_params=_params(("parallel", "parallel")),
    )(qc, kv, kv, gq, gk)


def _cross_bwd(qc, kv, gq, gk, do):
    q_spec, k_spec, v_spec, g_spec = _cross_specs()

    def body(q_ref, k_ref, v_ref, gq_ref, gk_ref, do_ref, dq_ref, dk_ref, dv_ref, dgq_ref, dgk_ref):
        _, pullback = jax.vjp(_cross_head, q_ref[...], k_ref[...], v_ref[...], gq_ref[...], gk_ref[...])
        dq, dk, dv, dgq, dgk = pullback(do_ref[...].astype(F32))
        dq_ref[...] = dq.astype(dq_ref.dtype)
        row0 = pl.program_id(1) == 0
        all0 = jnp.logical_and(row0, pl.program_id(0) == 0)
        for o_ref, val, init in ((dk_ref, dk, row0), (dv_ref, dv, row0), (dgq_ref, dgq, all0), (dgk_ref, dgk, all0)):
            @pl.when(init)
            def _(o_ref=o_ref, val=val):
                o_ref[...] = val

            @pl.when(jnp.logical_not(init))
            def _(o_ref=o_ref, val=val):
                o_ref[...] += val

    return pl.pallas_call(
        body, name="cross_bwd", grid=(4, SEQ // CROSS_ROWS),
        in_specs=[q_spec, k_spec, v_spec, g_spec, g_spec, q_spec],
        out_specs=[q_spec, k_spec, k_spec, g_spec, g_spec],
        out_shape=[jax.ShapeDtypeStruct((SEQ, D_CROSS), BF16), jax.ShapeDtypeStruct((N_MEM, D_CROSS), F32),
                   jax.ShapeDtypeStruct((N_MEM, D_CROSS), F32), jax.ShapeDtypeStruct((1, CROSS_HEAD), F32),
                   jax.ShapeDtypeStruct((1, CROSS_HEAD), F32)],
        compiler_params=_params(("arbitrary", "arbitrary")),
    )(qc, kv, kv, gq, gk, do)


def _loss_head(y, target):
    tr = 256

    def body(y_ref, t_ref, dy_ref, dyb_ref, loss_ref):
        err = y_ref[...] - t_ref[...]
        dy = err * (1.0 / D_MODEL)
        dy_ref[...] = dy
        dyb_ref[...] = dy.astype(BF16)
        part = jnp.sum(jnp.sum(err * err, axis=1, keepdims=True), axis=0, keepdims=True) * (0.5 / D_MODEL)
        part = jnp.broadcast_to(part, (1, 128))

        @pl.when(pl.program_id(0) == 0)
        def _():
            loss_ref[...] = part

        @pl.when(pl.program_id(0) != 0)
        def _():
            loss_ref[...] += part

    blk = pl.BlockSpec((tr, D_MODEL), lambda i: (i, 0))
    return pl.pallas_call(
        body, name="loss_head", grid=(SEQ // tr,),
        in_specs=[blk, blk], out_specs=[blk, blk, pl.BlockSpec((1, 128), lambda i: (0, 0))],
        out_shape=[jax.ShapeDtypeStruct((SEQ, D_MODEL), F32), jax.ShapeDtypeStruct((SEQ, D_MODEL), BF16),
                   jax.ShapeDtypeStruct((1, 128), F32)],
        compiler_params=_params(("arbitrary",)),
    )(y, target)


def _pad_heads(v):
    return jnp.pad(v.reshape(N_GROUPS, HEADS_PER_GROUP), ((0, 0), (0, 128 - HEADS_PER_GROUP))).reshape(1, DT_PAD)


def _unpad_heads(v):
    return v.reshape(v.shape[0], N_GROUPS, 128)[:, :, :HEADS_PER_GROUP].reshape(v.shape[0], N_DT)


def _rope_tables(positions):
    half = ROT // 2
    inv_freq = ROPE_THETA ** (-2.0 * jnp.arange(half, dtype=F32) / ROT)
    ang = positions.reshape(SEQ, 1).astype(F32) * inv_freq
    cos, sin = jnp.cos(ang), jnp.sin(ang)
    ones, zeros = jnp.ones((SEQ, HEAD - ROT), F32), jnp.zeros((SEQ, HEAD - ROT), F32)
    cos_h = jnp.concatenate([cos, cos, ones], axis=1)
    sin_h = jnp.concatenate([-sin, sin, zeros], axis=1)
    return jnp.tile(cos_h, (1, 2)), jnp.tile(sin_h, (1, 2))


def _add_res(acc, res):
    return (acc + res,)


def _settle(grads, *after):
    if hasattr(grads, "settle"):
        grads.settle(*after)


def _take_token(grads):
    token = getattr(grads, "token", None)
    if token is None:
        return ()
    grads.token = None
    return (token,)


def _local_step(x, mem, positions, target, p, w, more_weights=None, grads=None, h=None):
    grads = {} if grads is None else grads
    w = dict(w)
    cos, sin = _rope_tables(positions)
    gq2, gk2 = jnp.tile(p["g_q"], (1, 2)), jnp.tile(p["g_k"], (1, 2))
    bias, alog, dsk = _pad_heads(p["dt_bias"]), _pad_heads(p["a_log"]), _pad_heads(p["d_skip"])
    norm_out = [(D_MODEL, BF16, D_MODEL, 0, False)]

    if h is None:
        h = _rowwise(_norm_fn, [_full(x)], [_full(p["g_mix"])], norm_out, name="norm_in")[0]
    proj = _matmul(h, w["w_in"], mode="nn", name="in_proj", outs=[F32], n_cols=D_MAIN)
    dt_raw = _matmul(h, w["w_dt"], mode="nn", name="dt_proj", outs=[F32])
    qk_rows = [(proj, 128, 0, True), (proj, 128, 8, True), (proj, 128, 16, True), _full(cos), _full(sin)]
    qk_vecs = [_full(gq2), _full(gk2)]
    qn, kn, vn = _rowwise(_qk_fn, qk_rows, qk_vecs, [(D_ATTN, F32, 128, 0, True)] * 3, name="qk_prep", groups=8, tr=1024)
    branches = [_attention_fwd(qn, kn, vn, b) for b in range(3)]
    merge_rows = [_full(o) for o, _ in branches] + [_full(lse) for _, lse in branches]
    attn = _rowwise(_merge_fn, merge_rows, [_full(p["g_attn_out"])], [(D_ATTN, BF16, D_ATTN, 0, False)], name="attn_merge")[0]
    xbc = _conv_fwd(proj, p["conv_w"], p["conv_b"])
    y_ssd, h_in = _ssd_fwd(xbc, dt_raw, bias, alog, dsk)
    gate_rows = [(y_ssd, 256, 0, True), (proj, 256, 12, True)]
    gate_vecs = [(p["g_ssm_out"], 256, 0, True)]
    ssm = _rowwise(_gate_fn, gate_rows, gate_vecs, [(D_SSM, BF16, 256, 0, True)], name="ssm_gate", groups=4)[0]
    mix = jnp.concatenate([attn, ssm], axis=1)
    if more_weights is not None:
        w.update(more_weights("mixer_done", mix))
    x1 = _matmul(mix, w["w_out"], mode="nn", name="out_proj", outs=[F32], extra=(x,), epilogue=_add_res)
    hc = _rowwise(_norm_fn, [_full(x1)], [_full(p["g_cross"])], norm_out, name="norm_cross")[0]
    memh = _rowwise(_norm_fn, [_full(mem)], [_full(p["g_mem"])], norm_out, name="norm_mem", n_rows=N_MEM)[0]
    qc = _matmul(hc, w["w_cq"], mode="nn", name="cq_proj", outs=[F32])
    kv = _matmul(memh, w["w_ckv"], mode="nn", name="ckv_proj", outs=[F32])
    oc = _cross_fwd(qc, kv, p["g_cq"], p["g_ck"])
    x2 = _matmul(oc, w["w_co"], mode="nn", name="co_proj", outs=[F32], extra=(x1,), epilogue=_add_res)
    hm = _rowwise(_norm_fn, [_full(x2)], [_full(p["g_mlp"])], norm_out, name="norm_mlp")[0]
    if more_weights is not None:
        w.update(more_weights("cross_done", hm))
    u, act = _matmul(hm, w["w_up"], mode="nn", name="up_proj", outs=[F32, BF16],
                     epilogue=lambda acc: (acc, jnp.square(jnp.maximum(acc, 0.0))))
    x3 = _matmul(act, w["w_down"], mode="nn", name="down_proj", outs=[F32], extra=(x2,), epilogue=_add_res)
    dy, dyb, loss = _loss_head(x3, target)

    grads["w_down"] = _matmul(act, dyb, mode="tn", name="dw_down", outs=[BF16], after=_take_token(grads))
    du = _matmul(dyb, w["w_down"], mode="nt", name="d_act", outs=[BF16], extra=(u,), after=_take_token(grads),
                 epilogue=lambda acc, uu: (acc * (2.0 * jnp.maximum(uu, 0.0)),))
    _settle(grads, du)
    grads["w_up"] = _matmul(hm, du, mode="tn", name="dw_up", outs=[BF16], col_shards=4, after=_take_token(grads))
    dhm = _matmul(du, w["w_up"], mode="nt", name="d_hm", outs=[F32], after=_take_token(grads))
    _settle(grads, dhm)
    dx2, grads["g_mlp"] = _rowwise_vjp(
        _norm_fn, [_full(x2)], [_full(p["g_mlp"])], [[_full(dhm)]],
        [(0, D_MODEL, F32, D_MODEL, 0, False, _full(dy))], [(0, D_MODEL, D_MODEL, 0, False)], name="norm_mlp_bwd")
    grads["w_co"] = _matmul(oc, dx2, mode="tn", name="dw_co", outs=[BF16], col_shards=4, after=_take_token(grads))
    doc = _matmul(dx2, w["w_co"], mode="nt", name="d_oc", outs=[BF16])
    dqc, dkc, dvc, grads["g_cq"], grads["g_ck"] = _cross_bwd(qc, kv, p["g_cq"], p["g_ck"], doc)
    grads["w_cq"] = _matmul(hc, dqc, mode="tn", name="dw_cq", outs=[BF16])
    dhc = _matmul(dqc, w["w_cq"], mode="nt", name="d_hc", outs=[F32])
    dkv = jnp.concatenate([dkc, dvc], axis=1)
    grads["w_ckv"] = _matmul(memh, dkv, mode="tn", name="dw_ckv", outs=[BF16])
    dmemh = _matmul(dkv, w["w_ckv"], mode="nt", name="d_memh", outs=[F32])
    grads["g_mem"] = _rowwise_vjp(_norm_fn, [_full(mem)], [_full(p["g_mem"])], [[_full(dmemh)]], [],
                                  [(0, D_MODEL, D_MODEL, 0, False)], name="norm_mem_bwd", n_rows=N_MEM)[0]
    dx1, grads["g_cross"] = _rowwise_vjp(
        _norm_fn, [_full(x1)], [_full(p["g_cross"])], [[_full(dhc)]],
        [(0, D_MODEL, F32, D_MODEL, 0, False, _full(dx2))], [(0, D_MODEL, D_MODEL, 0, False)], name="norm_cross_bwd")
    grads["w_out"] = _matmul(mix, dx1, mode="tn", name="dw_out", outs=[BF16])
    dmix = _matmul(dx1, w["w_out"], mode="nt", name="d_mix", outs=[F32], after=_take_token(grads))
    _settle(grads, dmix)
    merge_grads = [(i, D_ATTN, F32, D_ATTN, 0, False, None) for i in range(6)]
    *dol, grads["g_attn_out"] = _rowwise_vjp(
        _merge_fn, merge_rows, [_full(p["g_attn_out"])], [[(dmix, D_ATTN, 0, False)]],
        merge_grads, [(0, D_ATTN, D_ATTN, 0, False)], name="attn_merge_bwd", after=_take_token(grads))
    dqkv = [_attention_bwd(qn, kn, vn, *branches[b], dol[b], dol[3 + b], b) for b in range(3)]
    qk_cts = [[(dqkv[b][i], 128, 0, True) for b in range(3)] for i in range(3)]
    dq, dk, dv, dgq2, dgk2 = _rowwise_vjp(
        _qk_fn, qk_rows, qk_vecs, qk_cts, [(i, D_ATTN, BF16, 128, 0, True, None) for i in range(3)],
        [(0, 128, 128, 0, False), (1, 128, 128, 0, False)], name="qk_prep_bwd", groups=8, tr=512)
    grads["g_q"] = dgq2[:, :HEAD] + dgq2[:, HEAD:]
    grads["g_k"] = dgk2[:, :HEAD] + dgk2[:, HEAD:]
    dy_ssd, dz, grads["g_ssm_out"] = _rowwise_vjp(
        _gate_fn, gate_rows, gate_vecs, [[(dmix, 256, 4, True)]],
        [(0, D_SSM, F32, 256, 0, True, None), (1, D_SSM, BF16, 256, 0, True, None)],
        [(0, D_SSM, 256, 0, True)], name="ssm_gate_bwd", groups=4)
    dxs, db, dc, ddt, dbias, dalog, ddsk = _ssd_bwd(xbc, dt_raw, bias, alog, dsk, h_in, dy_ssd)
    grads["dt_bias"], grads["a_log"], grads["d_skip"] = _unpad_heads(dbias), _unpad_heads(dalog), _unpad_heads(ddsk)
    dxbc_raw, dconv_w, grads["conv_b"] = _conv_bwd(proj, p["conv_w"], p["conv_b"], jnp.concatenate([dxs, db, dc], axis=1))
    grads["conv_w"] = dconv_w[:4]
    dproj = jnp.concatenate([dq, dk, dv, dz, dxbc_raw], axis=1)
    grads["w_main"] = _matmul(h, dproj, mode="tn", name="dw_main", outs=[BF16], out_cols=D_MAIN + N_DT)
    grads["w_dt"] = _matmul(h, ddt, mode="tn", name="dw_dt", outs=[BF16])
    dh = _matmul(dproj, w["w_in"], mode="nt", name="d_h_main", outs=[F32], after=_take_token(grads))
    dh = _matmul(ddt, w["w_dt"], mode="nt", name="d_h_dt", outs=[F32], extra=(dh,), epilogue=_add_res)
    grad_x, grads["g_mix"] = _rowwise_vjp(
        _norm_fn, [_full(x)], [_full(p["g_mix"])], [[_full(dh)]],
        [(0, D_MODEL, F32, D_MODEL, 0, False, _full(dx1))], [(0, D_MODEL, D_MODEL, 0, False)], name="norm_in_bwd")
    return loss, grad_x, grads


MATRICES = ("w_in", "w_out", "w_cq", "w_ckv", "w_co", "w_up", "w_down")
ROW_SHARDED = ("w_out", "w_cq", "w_ckv", "w_down")
N_CHIPS = 4
ANY = pl.BlockSpec(memory_space=pl.ANY)


def _place():
    return lax.axis_index("x"), lax.axis_index("y"), lax.axis_index("c")


def _other_chips(x, y):
    return [(1 - x, y), (x, 1 - y), (1 - x, 1 - y)]


def _remote(src, dst, send_sem, recv_sem, device):
    return pltpu.make_async_remote_copy(src_ref=src, dst_ref=dst, send_sem=send_sem, recv_sem=recv_sem,
                                        device_id=device, device_id_type=MESH)


def _gathered_shape(name, shard):
    rows, cols = shard.shape
    if name == "w_in":
        return (N_CHIPS, rows, cols)
    return (N_CHIPS * rows, cols) if name in ROW_SHARDED else (rows, N_CHIPS * cols)


def _shard_window(name, ref, rows, cols, chip, half):
    r0, nr = (0, rows) if half is None else (half * (rows // 2), rows // 2)
    if name == "w_in":
        return ref.at[chip, pl.ds(r0, nr), :]
    if name in ROW_SHARDED:
        return ref.at[pl.ds(chip * rows + r0, nr), :]
    return ref.at[pl.ds(r0, nr), pl.ds(pl.multiple_of(chip * cols, 128), cols)]


def _cast_into_gathered(w, name, chip, after=()):
    rows, cols = w.shape
    tr = _tile(rows, ROW_TILE)

    def body(chip_ref, w_ref, *rest):
        rest[-1][...] = w_ref[...].astype(BF16)

    if name == "w_in":
        out_spec = pl.BlockSpec((None, tr, cols), lambda i, chip_ref: (chip_ref[0], i, 0))
    elif name in ROW_SHARDED:
        out_spec = pl.BlockSpec((tr, cols), lambda i, chip_ref: (chip_ref[0] * (rows // tr) + i, 0))
    else:
        out_spec = pl.BlockSpec((tr, cols), lambda i, chip_ref: (i, chip_ref[0]))
    grid_spec = pltpu.PrefetchScalarGridSpec(
        num_scalar_prefetch=1, grid=(rows // tr,),
        in_specs=[pl.BlockSpec((tr, cols), lambda i, chip_ref: (i, 0))] + [pl.BlockSpec(memory_space=pl.ANY)] * len(after),
        out_specs=out_spec)
    return pl.pallas_call(body, name="cast_" + name, grid_spec=grid_spec,
                          out_shape=jax.ShapeDtypeStruct(_gathered_shape(name, w), BF16),
                          compiler_params=_params(("parallel",)))(chip.reshape(1).astype(jnp.int32), w, *after)


def _w_in_columns(arr, to_shards):
    rows, piece = D_MODEL, (D_MAIN + N_DT) // N_CHIPS
    tr = ROW_TILE

    def body(a_ref, o_ref):
        for j in range(N_CHIPS):
            if to_shards:
                o_ref[j] = a_ref[:, pl.ds(piece * j, piece)]
            else:
                o_ref[:, pl.ds(piece * j, piece)] = a_ref[j]

    pieces = pl.BlockSpec((N_CHIPS, tr, piece), lambda i: (0, i, 0))
    matrix = pl.BlockSpec((tr, N_CHIPS * piece), lambda i: (i, 0))
    out_dims = (N_CHIPS, rows, piece) if to_shards else (rows, N_CHIPS * piece)
    return pl.pallas_call(
        body, name="w_in_to_shards" if to_shards else "w_in_from_shards", grid=(rows // tr,),
        in_specs=[matrix if to_shards else pieces], out_specs=pieces if to_shards else matrix,
        out_shape=jax.ShapeDtypeStruct(out_dims, arr.dtype), compiler_params=_params(("parallel",)))(arr)


HBM = pl.BlockSpec(memory_space=pltpu.HBM)
SEM = pl.BlockSpec(memory_space=pltpu.SEMAPHORE)
EFFECT = pltpu.SideEffectType.DATAFLOW_SIDE_EFFECTING


def _split_start(name, bufs, plan, counts, after=()):
    n, n_g, n_after = len(bufs), len(counts), len(after)

    def body(*refs):
        ins, sems, token = refs[:n], refs[n + n_after:n + n_after + 2 * n_g], refs[-1]
        for g, copies in enumerate(plan(ins)):
            for i, (src, dst, device, _) in enumerate(copies):
                _remote(src, dst, sems[2 * g].at[i], sems[2 * g + 1].at[i], device).start()
        token[...] = jnp.zeros_like(token)

    sem_shapes = [pltpu.SemaphoreType.DMA((cnt,)) for cnt in counts for _ in range(2)]
    res = pl.pallas_call(
        body, name=name,
        out_shape=(*sem_shapes, *[pltpu.HBM(b.shape, b.dtype) for b in bufs], jax.ShapeDtypeStruct((8, 128), F32)),
        in_specs=(*(HBM,) * n, *(ANY,) * n_after),
        out_specs=(*(SEM,) * (2 * n_g), *(HBM,) * n, pl.BlockSpec(memory_space=pltpu.VMEM)),
        input_output_aliases={i: 2 * n_g + i for i in range(n)},
        compiler_params=pltpu.CompilerParams(has_side_effects=EFFECT),
    )(*[pltpu.with_memory_space_constraint(b, pltpu.HBM) for b in bufs], *after)
    sems = [(res[2 * g], res[2 * g + 1]) for g in range(n_g)]
    return sems, list(res[2 * n_g:2 * n_g + n]), res[-1]


def _split_wait(name, bufs, sems, plan, *after):
    n = len(bufs)

    def body(*refs):
        ins, send, recv = refs[:n], refs[n], refs[n + 1]
        (copies,) = plan(ins)
        for i, (src, _, device, landing) in enumerate(copies):
            cp = _remote(src, landing, send.at[i], recv.at[i], device)
            cp.wait_send()
            cp.wait_recv()

    res = pl.pallas_call(
        body, name=name, out_shape=tuple(pltpu.HBM(b.shape, b.dtype) for b in bufs),
        in_specs=(*(HBM,) * n, SEM, SEM, *(ANY,) * len(after)), out_specs=(HBM,) * n,
        input_output_aliases={i: i for i in range(n)},
        compiler_params=pltpu.CompilerParams(has_side_effects=EFFECT),
    )(*bufs, sems[0], sems[1], *after)
    return list(res)


def _ici_plan(names, shard_shapes):
    def plan(refs):
        x, y, c = _place()
        copies = []
        for ref, name in zip(refs, names):
            win = _shard_window(name, ref, *shard_shapes[name], 2 * x + y, c)
            for px, py in _other_chips(x, y):
                copies.append((win, win, (px, py, c), _shard_window(name, ref, *shard_shapes[name], 2 * px + py, c)))
        return [copies]
    return plan


def _pass_on_plan(names, shard_shapes):
    def plan(refs):
        x, y, c = _place()
        copies = []
        for ref, name in zip(refs, names):
            for px, py in _other_chips(x, y):
                win = _shard_window(name, ref, *shard_shapes[name], 2 * px + py, c)
                copies.append((win, win, (x, y, 1 - c), _shard_window(name, ref, *shard_shapes[name], 2 * px + py, 1 - c)))
        return [copies]
    return plan


def _swap_plan(n_pairs):
    def plan(refs):
        x, y, c = _place()
        return [[(src.at[:, 1 - c], dst, (x, y, 1 - c), dst) for src, dst in zip(refs[:n_pairs], refs[n_pairs:])]]
    return plan


def _scatter_plan(n_pairs):
    def plan(refs):
        x, y, c = _place()
        copies = []
        for src, dst in zip(refs[:n_pairs], refs[n_pairs:]):
            for k, (px, py) in enumerate(_other_chips(x, y)):
                copies.append((src.at[2 * px + py], dst.at[k], (px, py, c), dst.at[k]))
        return [copies]
    return plan


def _sibling_swap(arrs, name):
    n = len(arrs)

    def body(*refs):
        ins, outs, send, recv = refs[:n], refs[n:2 * n], refs[2 * n], refs[2 * n + 1]
        x, y, c = _place()
        cps = [_remote(ins[w].at[:, 1 - c], outs[w], send.at[w], recv.at[w], (x, y, 1 - c)) for w in range(n)]
        for cp in cps:
            cp.start()
        for cp in cps:
            cp.wait()

    return pl.pallas_call(
        body, name=name, in_specs=[ANY] * n, out_specs=[ANY] * n,
        out_shape=[jax.ShapeDtypeStruct((a.shape[0],) + a.shape[2:], a.dtype) for a in arrs],
        scratch_shapes=[pltpu.SemaphoreType.DMA((n,))] * 2,
    )(*arrs)


def _sibling_share(arrs):
    n = len(arrs)

    def body(*refs):
        ins, outs, send, recv = refs[:n], refs[n:2 * n], refs[2 * n], refs[2 * n + 1]
        x, y, c = _place()
        cps = [_remote(ins[w], outs[w], send.at[w], recv.at[w], (x, y, 1 - c)) for w in range(n)]
        for cp in cps:
            cp.start()
        for cp in cps:
            cp.wait()

    return pl.pallas_call(
        body, name="grad_sibling_share", in_specs=[ANY] * n, out_specs=[ANY] * n,
        out_shape=[jax.ShapeDtypeStruct(a.shape, a.dtype) for a in arrs],
        scratch_shapes=[pltpu.SemaphoreType.DMA((n,))] * 2,
    )(*arrs)


def _small_allreduce(buf, name):
    rows = buf.shape[0]

    def body(x_ref, out_ref, all_ref, send_sems, recv_sems, local_sem):
        x, y, c = _place()
        me, sibling, chips = (x, y, c), (x, y, 1 - c), _other_chips(x, y)

        def block(px, py, pc):
            return all_ref.at[pl.ds((4 * px + 2 * py + pc) * rows, rows), :]

        def copy(k, blk, to, src=None):
            return _remote(block(*blk) if src is None else src, block(*blk), send_sems.at[k], recv_sems.at[k], to)

        own = pltpu.make_async_copy(x_ref, block(*me), local_sem)
        own.start()
        first = [copy(0, me, sibling, src=x_ref)] + [copy(1 + j, me, (*chip, c), src=x_ref) for j, chip in enumerate(chips)]
        for cp in first:
            cp.start()
        passed = [copy(4 + j, (*chip, c), sibling) for j, chip in enumerate(chips)]
        for j, chip in enumerate(chips):
            copy(1 + j, (*chip, c), me).wait_recv()
            passed[j].start()
        copy(0, sibling, me).wait_recv()
        for j, chip in enumerate(chips):
            copy(4 + j, (*chip, 1 - c), me).wait_recv()
        for cp in first + passed:
            cp.wait_send()
        own.wait()
        acc = all_ref[pl.ds(0, rows), :]
        for d in range(1, 8):
            acc = acc + all_ref[pl.ds(d * rows, rows), :]
        out_ref[...] = acc

    vmem = pl.BlockSpec(memory_space=pltpu.VMEM)
    return pl.pallas_call(
        body, name=name, in_specs=[vmem], out_specs=vmem,
        out_shape=jax.ShapeDtypeStruct(buf.shape, F32),
        scratch_shapes=[pltpu.VMEM((8 * rows, 128), F32), pltpu.SemaphoreType.DMA((7,)), pltpu.SemaphoreType.DMA((7,)),
                        pltpu.SemaphoreType.DMA],
    )(buf)


ROW_TILE = 256


def _add_halves(arr, recv, c, name):
    _, _, hr, cols = arr.shape
    tr = _tile(hr, ROW_TILE)

    def body(c_ref, a_ref, r_ref, o_ref):
        o_ref[...] = (a_ref[...].astype(F32) + r_ref[...].astype(F32)).astype(o_ref.dtype)

    piece = pl.BlockSpec((None, tr, cols), lambda j, i, c_ref: (j, i, 0))
    grid_spec = pltpu.PrefetchScalarGridSpec(
        num_scalar_prefetch=1, grid=(N_CHIPS, hr // tr),
        in_specs=[pl.BlockSpec((None, None, tr, cols), lambda j, i, c_ref: (j, c_ref[0], i, 0)), piece], out_specs=piece)
    return pl.pallas_call(body, name=name, grid_spec=grid_spec, out_shape=jax.ShapeDtypeStruct(recv.shape, BF16),
                          compiler_params=_params(("parallel", "parallel")))(c.reshape(1).astype(jnp.int32), arr, recv)


def _flip_slot(d):
    return jnp.where(d == 1, 1, jnp.where(d == 3, 2, 0))


def _sum_chips(p, q, chip, name):
    _, hr, cols = p.shape
    tr = _tile(hr, ROW_TILE)

    def body(chip_ref, p_ref, q_ref, o_ref):
        j = pl.program_id(1)
        term = jnp.where(j == chip_ref[0], p_ref[...].astype(F32), q_ref[...].astype(F32))

        @pl.when(j == 0)
        def _():
            o_ref[...] = term

        @pl.when(j != 0)
        def _():
            o_ref[...] += term

    grid_spec = pltpu.PrefetchScalarGridSpec(
        num_scalar_prefetch=1, grid=(hr // tr, N_CHIPS),
        in_specs=[pl.BlockSpec((None, tr, cols), lambda i, j, chip_ref: (chip_ref[0], i, 0)),
                  pl.BlockSpec((None, tr, cols), lambda i, j, chip_ref: (_flip_slot(j ^ chip_ref[0]), i, 0))],
        out_specs=pl.BlockSpec((tr, cols), lambda i, j, chip_ref: (i, 0)))
    return pl.pallas_call(body, name=name, grid_spec=grid_spec, out_shape=jax.ShapeDtypeStruct((hr, cols), F32),
                          compiler_params=_params(("parallel", "arbitrary")))(chip.reshape(1).astype(jnp.int32), p, q)


def _adamw_halves(w, g_own, g_other, m, v, c, name):
    rows, cols = w.shape
    tr = _tile(rows // 2, ROW_TILE)
    per_half = rows // 2 // tr

    def body(c_ref, w_ref, own_ref, other_ref, m_ref, v_ref, g_ref, d_ref, nm_ref, nv_ref):
        mine = (pl.program_id(0) // per_half) == c_ref[0]
        g_ = jnp.where(mine, own_ref[...], other_ref[...])
        g_ref[...] = g_
        d_ref[...], nm_ref[...], nv_ref[...] = _adamw_math(w_ref[...], g_, m_ref[...], v_ref[...])

    blk = pl.BlockSpec((tr, cols), lambda i, c_ref: (i, 0))
    own = pl.BlockSpec((tr, cols), lambda i, c_ref: (jnp.where(i // per_half == c_ref[0], i % per_half, 0), 0))
    other = pl.BlockSpec((tr, cols), lambda i, c_ref: (jnp.where(i // per_half == c_ref[0], 0, i % per_half), 0))
    grid_spec = pltpu.PrefetchScalarGridSpec(num_scalar_prefetch=1, grid=(rows // tr,),
                                             in_specs=[blk, own, other, blk, blk], out_specs=[blk] * 4)
    return pl.pallas_call(body, name=name, grid_spec=grid_spec, out_shape=[jax.ShapeDtypeStruct(w.shape, F32)] * 4,
                          compiler_params=_params(("parallel",)))(c.reshape(1).astype(jnp.int32), w, g_own, g_other, m, v)


def _adamw_math(w, g, m, v):
    m_new = ADAM_B1 * m + (1.0 - ADAM_B1) * g
    v_new = ADAM_B2 * v + (1.0 - ADAM_B2) * (g * g)
    m_hat = m_new / (1.0 - ADAM_B1 ** ADAM_STEP)
    v_hat = v_new / (1.0 - ADAM_B2 ** ADAM_STEP)
    return -ADAM_LR * (m_hat / (jnp.sqrt(v_hat) + ADAM_EPS) + ADAM_WD * w), m_new, v_new


def _adamw(w, g, m, v, name):
    rows, cols = w.shape
    tr = _tile(rows, ROW_TILE)

    def body(w_ref, g_ref, m_ref, v_ref, d_ref, nm_ref, nv_ref):
        d_ref[...], nm_ref[...], nv_ref[...] = _adamw_math(w_ref[...], g_ref[...], m_ref[...], v_ref[...])

    blk = pl.BlockSpec((tr, cols), lambda i: (i, 0))
    return pl.pallas_call(body, name=name, grid=(rows // tr,), in_specs=[blk] * 4, out_specs=[blk] * 3,
                          out_shape=[jax.ShapeDtypeStruct(w.shape, F32)] * 3, compiler_params=_params(("parallel",)))(w, g, m, v)


VECTORS = ("g_mix", "g_q", "g_k", "g_attn_out", "conv_b", "dt_bias", "a_log", "d_skip", "g_ssm_out", "g_cross", "g_mem",
           "g_cq", "g_ck", "g_mlp")
WEIGHTS = ("g_mix", "w_in", "g_q", "g_k", "g_attn_out", "conv_w", "conv_b", "dt_bias", "a_log", "d_skip", "g_ssm_out", "w_out",
           "g_cross", "g_mem", "w_cq", "w_ckv", "g_cq", "g_ck", "w_co", "g_mlp", "w_up", "w_down")


def _pack(parts):
    flat = jnp.concatenate([t.reshape(-1) for t in parts])
    total = -(-flat.shape[0] // 1024) * 1024
    return jnp.pad(flat, (0, total - flat.shape[0])).reshape(total // 128, 128)


def _unpack(buf, shapes):
    flat, out, pos = buf.reshape(-1), [], 0
    for shape in shapes:
        size = math.prod(shape)
        out.append(flat[pos:pos + size].reshape(shape))
        pos += size
    return out


def kernel(x, mem, positions, g_mix, w_in, g_q, g_k, g_attn_out, conv_w, conv_b, dt_bias, a_log, d_skip, g_ssm_out, w_out, g_cross, g_mem, w_cq, w_ckv, g_cq, g_ck, w_co, g_mlp, w_up, w_down, loss_target, m_g_mix, m_w_in, m_g_q, m_g_k, m_g_attn_out, m_conv_w, m_conv_b, m_dt_bias, m_a_log, m_d_skip, m_g_ssm_out, m_w_out, m_g_cross, m_g_mem, m_w_cq, m_w_ckv, m_g_cq, m_g_ck, m_w_co, m_g_mlp, m_w_up, m_w_down, v_g_mix, v_w_in, v_g_q, v_g_k, v_g_attn_out, v_conv_w, v_conv_b, v_dt_bias, v_a_log, v_d_skip, v_g_ssm_out, v_w_out, v_g_cross, v_g_mem, v_w_cq, v_w_ckv, v_g_cq, v_g_ck, v_w_co, v_g_mlp, v_w_up, v_w_down):
    args = dict(locals())
    weights = {n: args[n][0] for n in WEIGHTS}
    mom_m = {n: args["m_" + n][0] for n in WEIGHTS}
    mom_v = {n: args["v_" + n][0] for n in WEIGHTS}
    x_idx, y_idx, c_idx = _place()
    chip = 2 * x_idx + y_idx

    conv_parts = _small_allreduce(_pack([jnp.zeros((N_CHIPS, 4, 512), F32).at[chip].set(0.5 * weights["conv_w"])]),
                                  "gather_conv_taps")
    shapes = {n: weights[n].shape for n in MATRICES}
    first, mid, late = ("w_in",), ("w_out", "w_cq", "w_ckv", "w_co"), ("w_up", "w_down")
    w_in_buf = [_cast_into_gathered(weights["w_in"], "w_in", chip)]
    sems_in, w_in_buf, token = _split_start("gather_ici_start_w_in", w_in_buf, _ici_plan(first, shapes), [3], after=(conv_parts,))
    bufs = [_cast_into_gathered(weights[n], n, chip, after=(token,)) for n in mid + late]
    sems_rest, bufs, token = _split_start("gather_ici_start_rest", bufs, _ici_plan(mid + late, shapes), [18], after=(token,))
    ici_sems = (sems_in[0], sems_rest[0])
    params = {n: weights[n].reshape(1, -1) for n in VECTORS}
    h_in = _rowwise(_norm_fn, [_full(x[0])], [_full(params["g_mix"])], [(D_MODEL, BF16, D_MODEL, 0, False)], name="norm_in",
                    after=(token,))[0]
    w_in_buf = _split_wait("gather_ici_wait_w_in", w_in_buf, ici_sems[0], _ici_plan(first, shapes), token, h_in)
    pass_sems, w_in_buf, token = _split_start("gather_pass_start_w_in", w_in_buf, _pass_on_plan(first, shapes), [3])
    w_in_buf = _split_wait("gather_pass_wait_w_in", w_in_buf, pass_sems[0], _pass_on_plan(first, shapes), token)
    w_in_full = _w_in_columns(w_in_buf[0], to_shards=False)
    full = {"w_in": w_in_full,
            "w_dt": jnp.pad(w_in_full[:, D_MAIN:].reshape(D_MODEL, N_GROUPS, HEADS_PER_GROUP),
                            ((0, 0), (0, 0), (0, 128 - HEADS_PER_GROUP))).reshape(D_MODEL, DT_PAD)}
    in_flight = {}

    def more_weights(stage, after):
        if stage == "mixer_done":
            rest = _split_wait("gather_ici_wait_rest", bufs, ici_sems[1], _ici_plan(mid + late, shapes), after)
            plan = lambda refs: _pass_on_plan(mid, shapes)(refs[:4]) + _pass_on_plan(late, shapes)(refs[4:])
            sems, rest, token = _split_start("gather_pass_start_rest", rest, plan, [12, 6])
            in_flight["late"] = (rest[4:], sems[1])
            return dict(zip(mid, _split_wait("gather_pass_wait_mid", rest[:4], sems[0], _pass_on_plan(mid, shapes), token)))
        late_bufs, sems = in_flight.pop("late")
        return dict(zip(late, _split_wait("gather_pass_wait_late", late_bufs, sems, _pass_on_plan(late, shapes), after)))

    params["conv_w"] = _unpack(conv_parts, [(N_CHIPS, 4, 512)])[0].transpose(1, 0, 2).reshape(4, 4 * 512)

    groups = (("w_down",), ("w_up",), ("w_co", "w_cq", "w_ckv", "w_out"), ("w_in",))
    scattered = []

    class GradStore(dict):
        pending = None

        def __setitem__(self, name, value):
            super().__setitem__(name, value)
            if "w_main" in self and "w_dt" in self and "w_in" not in self:
                gw_in = lax.dynamic_update_slice(self["w_main"], _unpad_heads(self["w_dt"]), (0, D_MAIN))
                self["w_in"] = _w_in_columns(gw_in, to_shards=True)
            for group in groups:
                if name in group and all(n in self for n in group):
                    self.settle()
                    pieces = [self[n].reshape(N_CHIPS, 2, shapes[n][0] // 2, shapes[n][1]) for n in group]
                    if group == groups[-1]:
                        self.scatter(group, pieces, _sibling_swap(pieces, "grad_swap_" + group[0]))
                    else:
                        landing = [lax.empty((N_CHIPS,) + a.shape[2:], BF16) for a in pieces]
                        sems, thru, self.token = _split_start("grad_swap_start_" + group[0], pieces + landing,
                                                              _swap_plan(len(pieces)), [len(pieces)])
                        self.pending = (group, sems[0], thru)

        def settle(self, *after):
            if self.pending is not None:
                group, sems, thru = self.pending
                self.pending = None
                thru = _split_wait("grad_swap_wait_" + group[0], thru, sems, _swap_plan(len(group)), *after)
                self.scatter(group, thru[:len(group)], thru[len(group):])

        def scatter(self, group, pieces, from_sibling):
            sums = [_add_halves(a, r, c_idx, "add_halves_" + n) for n, a, r in zip(group, pieces, from_sibling)]
            landing = [lax.empty((3,) + s.shape[1:], BF16) for s in sums]
            sems, thru, self.token = _split_start("grad_scatter_start_" + group[0], sums + landing,
                                                  _scatter_plan(len(sums)), [3 * len(sums)])
            scattered.append((group, sems[0], thru))

    loss, grad_x, grads = _local_step(x[0], mem[0], positions[0], loss_target[0], params, full, more_weights, GradStore(),
                                      h_in)

    halves = {}
    for group, sems, thru in scattered:
        thru = _split_wait("grad_scatter_wait_" + group[0], thru, sems, _scatter_plan(len(group)), grad_x)
        for i, n in enumerate(group):
            halves[n] = _sum_chips(thru[i], thru[len(group) + i], chip, "sum_chips_" + n)
    other_halves = dict(zip(MATRICES, _sibling_share([halves[n] for n in MATRICES])))
    out_g, out_d, out_m, out_v = {}, {}, {}, {}
    for n in MATRICES:
        out_g[n], out_d[n], out_m[n], out_v[n] = _adamw_halves(weights[n], halves[n], other_halves[n], mom_m[n], mom_v[n],
                                                               c_idx, "adamw_" + n)

    small = [grads[n] for n in VECTORS] + [grads["conv_w"]]
    summed = _unpack(_small_allreduce(_pack(small), "allreduce_vectors"), [t.shape for t in small])
    g_small = dict(zip(VECTORS, summed[:-1]))
    g_small["conv_w"] = lax.dynamic_slice_in_dim(summed[-1], chip * 512, 512, axis=1)
    names = VECTORS + ("conv_w",)
    shapes = [weights[n].shape for n in names]
    packed = [_pack([src[n] for n in names]) for src in (weights, g_small, mom_m, mom_v)]
    small_out = [_unpack(t, shapes) for t in _adamw(*packed, "adamw_small")]
    for i, n in enumerate(names):
        out_g[n] = g_small[n].reshape(shapes[i])
        out_d[n], out_m[n], out_v[n] = small_out[0][i], small_out[1][i], small_out[2][i]

    total_loss = lax.psum(loss[0, 0], ("x", "y", "c"))
    outs = [total_loss, grad_x[None]]
    for group in (out_g, out_d, out_m, out_v):
        outs += [group[n][None] for n in WEIGHTS]
    return tuple(outs)
```

```python
import functools
import math

import jax
import jax.numpy as jnp
from jax import lax
from jax.experimental import pallas as pl
from jax.experimental.pallas import tpu as pltpu

F32 = jnp.float32
BF16 = jnp.bfloat16

SEQ = 2048
D_MODEL = 2048
HEAD = 64
D_ATTN = 1024
D_SSM = 1024
N_GROUPS = 4
N_STATE = 128
CHUNK = 128
ATT_BLK = 128
N_MEM = 256
D_CROSS = 512
D_FF = 8192
D_MAIN = 6144
N_DT = 16
DT_PAD = 512
ROT = 16
ROPE_THETA = 500000.0
EPS = 1e-6
NEG = -1e30
BRANCH_BLOCKS = (16, 4, 1)
DILATIONS = (1, 4, 16)

ADAM_LR, ADAM_B1, ADAM_B2, ADAM_EPS, ADAM_WD, ADAM_STEP = 0.001, 0.9, 0.999, 1e-08, 0.01, 10

VMEM_LIMIT = 56 * 1024 * 1024
MESH = pl.DeviceIdType.MESH


def _params(sem, **kw):
    return pltpu.CompilerParams(dimension_semantics=sem, vmem_limit_bytes=VMEM_LIMIT, **kw)


def _bdot(a, b, dims):
    return lax.dot_general(a.astype(BF16), b.astype(BF16), (dims, ((), ())), preferred_element_type=F32)


def _fdot(a, b, dims):
    return lax.dot_general(a, b, (dims, ((), ())), preferred_element_type=F32, precision=lax.Precision.HIGHEST)


NN = ((1,), (0,))
NT = ((1,), (1,))
TN = ((0,), (0,))


def _tile(n, want):
    t = min(n, want)
    while n % t:
        t //= 2
    return t


def _matmul(a, b, *, mode, name, outs, extra=(), epilogue=None, col_shards=1, after=(), n_cols=None, out_cols=None,
            tm=1024, tn=1024, tk=2048):
    if mode == "nn":
        (m, k), n = a.shape, b.shape[1]
    elif mode == "nt":
        (m, k), n = a.shape, b.shape[0]
    else:
        (k, m), n = a.shape, b.shape[1]
    n = n if n_cols is None else n_cols
    tm, tn, tk = _tile(m, tm), _tile(n // col_shards, tn), _tile(k, tk)
    nk = k // tk
    per_shard = n // col_shards // tn
    dims = {"nn": NN, "nt": NT, "tn": TN}[mode]
    a_spec = pl.BlockSpec((tk, tm), lambda i, j, kk: (kk, i)) if mode == "tn" else pl.BlockSpec((tm, tk), lambda i, j, kk: (i, kk))
    b_spec = pl.BlockSpec((tn, tk), lambda i, j, kk: (j, kk)) if mode == "nt" else pl.BlockSpec((tk, tn), lambda i, j, kk: (kk, j))
    o_spec = pl.BlockSpec((tm, tn), lambda i, j, kk: (i, j))
    n_extra, n_out, n_after = len(extra), len(outs), len(after)

    def body(a_ref, b_ref, *rest):
        extra_refs, out_refs, acc_ref = rest[:n_extra], rest[n_extra + n_after:n_extra + n_after + n_out], rest[-1]
        def finish(acc):
            res = (acc,) if epilogue is None else epilogue(acc, *[e[...] for e in extra_refs])
            for o_ref, r in zip(out_refs, res):
                o_ref[...] = r.astype(o_ref.dtype)

        if nk == 1:
            finish(_bdot(a_ref[...], b_ref[...], dims))
            return
        kk = pl.program_id(2)

        @pl.when(kk == 0)
        def _():
            acc_ref[...] = jnp.zeros_like(acc_ref)

        acc_ref[...] += _bdot(a_ref[...], b_ref[...], dims)

        @pl.when(kk == nk - 1)
        def _():
            finish(acc_ref[...])

    if col_shards == 1:
        out_specs, out_dims = [o_spec] * n_out, (m, n if out_cols is None else out_cols)
    else:
        sharded = pl.BlockSpec((None, tm, tn), lambda i, j, kk: (j // per_shard, i, j % per_shard))
        out_specs, out_dims = [sharded] * n_out, (col_shards, m, n // col_shards)
    res = pl.pallas_call(
        body, name=name, grid=(m // tm, n // tn, nk),
        in_specs=[a_spec, b_spec] + [o_spec] * n_extra + [pl.BlockSpec(memory_space=pl.ANY)] * n_after,
        out_specs=out_specs,
        out_shape=[jax.ShapeDtypeStruct(out_dims, dt) for dt in outs],
        scratch_shapes=[pltpu.VMEM((tm, tn) if nk > 1 else (8, 128), F32)],
        compiler_params=_params(("parallel", "parallel", "arbitrary")),
    )(a, b, *extra, *after)
    return res[0] if n_out == 1 else res


def _row_spec(tr, bw, cb, per_group):
    return pl.BlockSpec((tr, bw), (lambda g, i: (i, cb + g)) if per_group else (lambda g, i: (i, cb)))


def _vec_spec(bw, cb, per_group):
    return pl.BlockSpec((1, bw), (lambda g, i: (0, cb + g)) if per_group else (lambda g, i: (0, cb)))


def _rowwise(fn, rows, vecs, outs, *, name, n_rows=SEQ, tr=256, groups=1, after=()):
    n_r, n_v, n_after = len(rows), len(vecs), len(after)

    def body(*refs):
        vals = [r[...].astype(F32) for r in refs[:n_r + n_v]]
        res = fn(*vals)
        for o_ref, r in zip(refs[n_r + n_v + n_after:], res):
            o_ref[...] = r.astype(o_ref.dtype)

    res = pl.pallas_call(
        body, name=name, grid=(groups, n_rows // tr),
        in_specs=[_row_spec(tr, bw, cb, pg) for _, bw, cb, pg in rows] + [_vec_spec(bw, cb, pg) for _, bw, cb, pg in vecs]
        + [pl.BlockSpec(memory_space=pl.ANY)] * n_after,
        out_specs=[_row_spec(tr, bw, cb, pg) for _, _, bw, cb, pg in outs],
        out_shape=[jax.ShapeDtypeStruct((n_rows, w), dt) for w, dt, _, _, _ in outs],
        compiler_params=_params(("parallel", "parallel")),
    )(*[r[0] for r in rows], *[v[0] for v in vecs], *after)
    return res


def _rowwise_vjp(fn, rows, vecs, cts, row_grads, vec_grads, *, name, n_rows=SEQ, tr=256, groups=1, after=()):
    n_r, n_v, n_after = len(rows), len(vecs), len(after)
    ct_ops = [op for group in cts for op in group]
    ct_sizes = [len(group) for group in cts]
    res_ops = [g[6] for g in row_grads if g[6] is not None]
    n_ct, n_res, n_rg = len(ct_ops), len(res_ops), len(row_grads)

    def body(*refs):
        vals = [r[...].astype(F32) for r in refs[:n_r + n_v]]
        pos = n_r + n_v
        ct_vals = []
        for size in ct_sizes:
            acc = refs[pos][...].astype(F32)
            for t in range(1, size):
                acc = acc + refs[pos + t][...].astype(F32)
            ct_vals.append(acc)
            pos += size
        res_refs = refs[pos:pos + n_res]
        out_refs = refs[pos + n_res + n_after:]
        _, pullback = jax.vjp(fn, *vals)
        grads = pullback(tuple(ct_vals))
        r_i = 0
        for o_ref, g in zip(out_refs[:n_rg], row_grads):
            val = grads[g[0]]
            if g[6] is not None:
                val = val + res_refs[r_i][...].astype(F32)
                r_i += 1
            o_ref[...] = val.astype(o_ref.dtype)
        first = (pl.program_id(1) == 0)
        for o_ref, g in zip(out_refs[n_rg:], vec_grads):
            val = jnp.sum(grads[n_r + g[0]], axis=0, keepdims=True)
            init = first if g[4] else jnp.logical_and(first, pl.program_id(0) == 0)

            @pl.when(init)
            def _(o_ref=o_ref, val=val):
                o_ref[...] = val

            @pl.when(jnp.logical_not(init))
            def _(o_ref=o_ref, val=val):
                o_ref[...] += val

    in_specs = [_row_spec(tr, bw, cb, pg) for _, bw, cb, pg in rows] + [_vec_spec(bw, cb, pg) for _, bw, cb, pg in vecs]
    in_specs += [_row_spec(tr, bw, cb, pg) for _, bw, cb, pg in ct_ops + res_ops] + [pl.BlockSpec(memory_space=pl.ANY)] * n_after
    out_specs =[_row_spec(tr, g[3], g[4], g[5]) for g in row_grads] + [_vec_spec(g[2], g[3], g[4]) for g in vec_grads]
    out_shape = [jax.ShapeDtypeStruct((n_rows, g[1]), g[2]) for g in row_grads]
    out_shape += [jax.ShapeDtypeStruct((1, g[1]), F32) for g in vec_grads]
    return pl.pallas_call(
        body, name=name, grid=(groups, n_rows // tr),
        in_specs=in_specs, out_specs=out_specs, out_shape=out_shape,
        compiler_params=_params(("arbitrary", "arbitrary")),
    )(*[r[0] for r in rows], *[v[0] for v in vecs], *[c[0] for c in ct_ops], *[r[0] for r in res_ops], *after)


def _full(arr, width=None):
    return (arr, arr.shape[1] if width is None else width, 0, False)


def _make_xor(sh):
    def raw(x):
        n = x.shape[-1]
        lane = lax.broadcasted_iota(jnp.int32, x.shape, x.ndim - 1)
        up = pltpu.roll(x, n - sh, x.ndim - 1)
        down = pltpu.roll(x, sh, x.ndim - 1)
        return jnp.where((lane & sh) == 0, up, down)

    f = jax.custom_vjp(raw)
    f.defvjp(lambda x: (raw(x), None), lambda _, ct: (raw(ct),))
    return f


_SWAP_ROPE_HALVES = _make_xor(ROT // 2)


def _head_sum(x):
    n = x.shape[-1]
    same_head = (lax.broadcasted_iota(jnp.int32, (n, n), 0) // HEAD) == (lax.broadcasted_iota(jnp.int32, (n, n), 1) // HEAD)
    return _fdot(x, same_head.astype(F32), NN)


def _rms(x, g):
    return x * lax.rsqrt(jnp.mean(x * x, axis=-1, keepdims=True) + EPS) * g


def _head_rms_rope(x, g, cos, sin, scale):
    y = x * lax.rsqrt(_head_sum(x * x) * (1.0 / HEAD) + EPS) * g
    return (y * cos + _SWAP_ROPE_HALVES(y) * sin) * scale


def _qk_fn(q, k, v, cos, sin, gq, gk):
    return (_head_rms_rope(q, gq, cos, sin, HEAD ** -0.5), _head_rms_rope(k, gk, cos, sin, 1.0), v)


def _norm_fn(x, g):
    return (_rms(x, g),)


def _merge_fn(o0, o1, o2, l0, l1, l2, g):
    m = lax.stop_gradient(jnp.maximum(jnp.maximum(l0, l1), l2))
    e0, e1, e2 = jnp.exp(l0 - m), jnp.exp(l1 - m), jnp.exp(l2 - m)
    mix = (e0 * o0 + e1 * o1 + e2 * o2) / (e0 + e1 + e2)
    return (_rms(mix, g),)


def _gate_fn(y, z, g):
    return (_rms(y * (z * jax.nn.sigmoid(z)), g),)


def _attn_pair(q, kc, vc, kp=None, vp=None, has_prev=None):
    qi = lax.broadcasted_iota(jnp.int32, (ATT_BLK, ATT_BLK), 0)
    kj = lax.broadcasted_iota(jnp.int32, (ATT_BLK, ATT_BLK), 1)
    lane = lax.broadcasted_iota(jnp.int32, (1, 2 * HEAD), 1)
    o, lse = 0.0, 0.0
    for h in range(2):
        pick = ((lane >= h * HEAD) & (lane < (h + 1) * HEAD)).astype(F32)
        qh = q * pick
        s_c = jnp.where(qi >= kj, _bdot(qh, kc, NT), NEG)
        m = jnp.max(s_c, axis=-1, keepdims=True)
        if kp is not None:
            s_p = jnp.where(jnp.logical_and(kj >= qi, has_prev), _bdot(qh, kp, NT), NEG)
            m = jnp.maximum(m, jnp.max(s_p, axis=-1, keepdims=True))
        m = lax.stop_gradient(m)
        p_c = jnp.exp(s_c - m)
        den = jnp.sum(p_c, axis=-1, keepdims=True)
        acc = _bdot(p_c, vc, NN)
        if kp is not None:
            p_p = jnp.exp(s_p - m)
            den = den + jnp.sum(p_p, axis=-1, keepdims=True)
            acc = acc + _bdot(p_p, vp, NN)
        o = o + (pick * (1.0 / den)) * acc
        lse = lse + pick * (m + jnp.log(den))
    return o, lse


def _attn_config(b):
    r = DILATIONS[b]
    return r, ATT_BLK * r, (512 if r == 1 else 128), BRANCH_BLOCKS[b] > 1


def _for_residues(r, fn):
    if r <= 4:
        for rho in range(r):
            fn(rho)
    else:
        def step(t, carry):
            for u in range(4):
                fn(4 * t + u)
            return carry

        lax.fori_loop(0, r // 4, step, 0)


def _strided_rows(start, r):
    if r > 1:
        return pl.ds(start, ATT_BLK, stride=r)
    return pl.ds(start if isinstance(start, int) else pl.multiple_of(start, ATT_BLK), ATT_BLK)


def _attention_fwd(qn, kn, vn, b):
    r, rows, lanes, with_prev = _attn_config(b)
    cur = pl.BlockSpec((rows, lanes), lambda g, n: (n, g))
    prev = pl.BlockSpec((rows, lanes), lambda g, n: (jnp.maximum(n - 1, 0), g))

    def body(*refs):
        ins, (o_ref, l_ref) = refs[:-2], refs[-2:]
        has_prev = pl.program_id(1) > 0

        def one(rho):
            sub = _strided_rows(rho, r)
            for pair in range(lanes // 128):
                sl = pl.ds(pair * 128, 128)
                args = [ref[sub, sl] for ref in ins] + ([has_prev] if with_prev else [])
                o_ref[sub, sl], l_ref[sub, sl] = _attn_pair(*args)

        _for_residues(r, one)

    operands = (qn, kn, vn, kn, vn) if with_prev else (qn, kn, vn)
    return pl.pallas_call(
        body, name="attn_fwd_%d" % r, grid=(D_ATTN // lanes, SEQ // rows),
        in_specs=[cur, cur, cur] + ([prev, prev] if with_prev else []), out_specs=[cur, cur],
        out_shape=[jax.ShapeDtypeStruct((SEQ, D_ATTN), F32)] * 2,
        compiler_params=_params(("parallel", "parallel")),
    )(*operands)


def _attn_pair_bwd(q, kc, vc, kp, vp, o, lse, do, dl, has_prev):
    qi = lax.broadcasted_iota(jnp.int32, (ATT_BLK, ATT_BLK), 0)
    kj = lax.broadcasted_iota(jnp.int32, (ATT_BLK, ATT_BLK), 1)
    lane = lax.broadcasted_iota(jnp.int32, (1, 2 * HEAD), 1)
    tiles = [(kc, vc, qi >= kj)]
    if kp is not None:
        tiles.append((kp, vp, jnp.logical_and(kj >= qi, has_prev)))
    dq = 0.0
    dks, dvs = [0.0] * len(tiles), [0.0] * len(tiles)
    for h in range(2):
        pick = ((lane >= h * HEAD) & (lane < (h + 1) * HEAD)).astype(F32)
        qh, doh = q * pick, do * pick
        lse_h = jnp.sum(lse * (lane == h * HEAD).astype(F32), axis=-1, keepdims=True)
        base = jnp.sum(dl * pick - doh * o, axis=-1, keepdims=True)
        dq_h = 0.0
        for t, (k_, v_, mask) in enumerate(tiles):
            p = jnp.exp(jnp.where(mask, _bdot(qh, k_, NT), NEG) - lse_h)
            ds = p * (_bdot(doh, v_, NT) + base)
            dq_h = dq_h + _bdot(ds, k_, NN)
            dks[t] = dks[t] + _bdot(ds, qh, TN)
            dvs[t] = dvs[t] + _bdot(p, doh, TN)
        dq = dq + pick * dq_h
    return (dq, dks[0], dvs[0]) + ((dks[1], dvs[1]) if kp is not None else ())


def _attention_bwd(qn, kn, vn, o, lse, do, dl, b):
    r, rows, lanes, with_prev = _attn_config(b)
    cur = pl.BlockSpec((rows, lanes), lambda g, n: (n, g))
    prev = pl.BlockSpec((rows, lanes), lambda g, n: (jnp.maximum(n - 1, 0), g))
    whole = pl.BlockSpec((SEQ, lanes), lambda g, n: (0, g))
    n_in = 5 if with_prev else 3

    def body(*refs):
        ins, (o_ref, l_ref, do_ref, dl_ref, dq_ref, dk_ref, dv_ref) = refs[:n_in], refs[n_in:]
        n = pl.program_id(1)

        @pl.when(n == 0)
        def _():
            dk_ref[...] = jnp.zeros_like(dk_ref)
            dv_ref[...] = jnp.zeros_like(dv_ref)

        def one(rho):
            sub = _strided_rows(rho, r)
            sub_c = _strided_rows(n * rows + rho, r)
            sub_p = _strided_rows(jnp.maximum(n - 1, 0) * rows + rho, r)
            for pair in range(lanes // 128):
                sl = pl.ds(pair * 128, 128)
                vals = [ref[sub, sl] for ref in ins] + ([] if with_prev else [None, None])
                grads = _attn_pair_bwd(*vals, o_ref[sub, sl], l_ref[sub, sl], do_ref[sub, sl], dl_ref[sub, sl], n > 0)
                dq_ref[sub, sl] = grads[0]
                dk_ref[sub_c, sl] += grads[1]
                dv_ref[sub_c, sl] += grads[2]
                if with_prev:
                    dk_ref[sub_p, sl] += grads[3]
                    dv_ref[sub_p, sl] += grads[4]

        _for_residues(r, one)

    operands = (qn, kn, vn, kn, vn) if with_prev else (qn, kn, vn)
    return pl.pallas_call(
        body, name="attn_bwd_%d" % r, grid=(D_ATTN // lanes, SEQ // rows),
        in_specs=[cur, cur, cur] + ([prev, prev] if with_prev else []) + [cur] * 4, out_specs=[cur, whole, whole],
        out_shape=[jax.ShapeDtypeStruct((SEQ, D_ATTN), F32)] * 3,
        compiler_params=_params(("parallel", "arbitrary")),
    )(*operands, o, lse, do, dl)


CONV_COLS = 256
XBC_BLOCK0 = 4096 // CONV_COLS


def _shift_rows(x, s):
    n = x.shape[0]
    t = lax.broadcasted_iota(jnp.int32, x.shape, 0)
    if s >= 0:
        return jnp.where(t >= s, pltpu.roll(x, s, 0), 0.0)
    return jnp.where(t < n + s, pltpu.roll(x, n + s, 0), 0.0)


def _conv_pre(x, w_ref, b_ref):
    pre = b_ref[...] + w_ref[3:4, :] * x
    for k in range(3):
        pre = pre + w_ref[k:k + 1, :] * _shift_rows(x, 3 - k)
    return pre


def _conv_fwd(proj, conv_w, conv_b):
    cols = conv_w.shape[1]

    def body(x_ref, w_ref, b_ref, o_ref):
        pre = _conv_pre(x_ref[...], w_ref, b_ref)
        o_ref[...] = pre * jax.nn.sigmoid(pre)

    blk = pl.BlockSpec((SEQ, CONV_COLS), lambda j: (0, j))
    return pl.pallas_call(
        body, name="conv_fwd", grid=(cols // CONV_COLS,),
        in_specs=[pl.BlockSpec((SEQ, CONV_COLS), lambda j: (0, XBC_BLOCK0 + j)),
                  pl.BlockSpec((4, CONV_COLS), lambda j: (0, j)), pl.BlockSpec((1, CONV_COLS), lambda j: (0, j))],
        out_specs=blk, out_shape=jax.ShapeDtypeStruct((SEQ, cols), F32),
        compiler_params=_params(("parallel",)),
    )(proj, conv_w, conv_b)


def _conv_bwd(proj, conv_w, conv_b, dy):
    cols = conv_w.shape[1]

    def body(x_ref, w_ref, b_ref, dy_ref, dx_ref, dw_ref, db_ref):
        x = x_ref[...]
        pre = _conv_pre(x, w_ref, b_ref)
        sg = jax.nn.sigmoid(pre)
        dpre = dy_ref[...] * (sg * (1.0 + pre * (1.0 - sg)))
        db_ref[...] = jnp.sum(dpre, axis=0, keepdims=True)
        dx = w_ref[3:4, :] * dpre
        dw_ref[3:4, :] = jnp.sum(dpre * x, axis=0, keepdims=True)
        for k in range(3):
            dx = dx + w_ref[k:k + 1, :] * _shift_rows(dpre, k - 3)
            dw_ref[k:k + 1, :] = jnp.sum(dpre * _shift_rows(x, 3 - k), axis=0, keepdims=True)
        dw_ref[4:8, :] = jnp.zeros((4, CONV_COLS), F32)
        dx_ref[...] = dx.astype(dx_ref.dtype)

    blk = pl.BlockSpec((SEQ, CONV_COLS), lambda j: (0, j))
    return pl.pallas_call(
        body, name="conv_bwd", grid=(cols // CONV_COLS,),
        in_specs=[pl.BlockSpec((SEQ, CONV_COLS), lambda j: (0, XBC_BLOCK0 + j)),
                  pl.BlockSpec((4, CONV_COLS), lambda j: (0, j)), pl.BlockSpec((1, CONV_COLS), lambda j: (0, j)), blk],
        out_specs=[blk, pl.BlockSpec((8, CONV_COLS), lambda j: (0, j)), pl.BlockSpec((1, CONV_COLS), lambda j: (0, j))],
        out_shape=[jax.ShapeDtypeStruct((SEQ, cols), BF16), jax.ShapeDtypeStruct((8, cols), F32),
                   jax.ShapeDtypeStruct((1, cols), F32)],
        compiler_params=_params(("parallel",)),
    )(proj, conv_w, conv_b, dy)


HEADS_PER_GROUP = 4


def _ssd_chunk(x0, x1, x2, x3, bm, cm, dtr, bias, alog, dsk, h0, h1, h2, h3):
    xs, hs = (x0, x1, x2, x3), (h0, h1, h2, h3)
    row = lax.broadcasted_iota(jnp.int32, (CHUNK, CHUNK), 0)
    col = lax.broadcasted_iota(jnp.int32, (CHUNK, CHUNK), 1)
    causal = row >= col
    tril = causal.astype(F32)
    z = dtr + bias
    dt = jnp.maximum(z, 0.0) + jnp.log(1.0 + jnp.exp(-jnp.abs(z)))
    a = -jnp.exp(alog)
    acs = _fdot(tril, dt * a, NN)
    acs_t, dt_t = acs.T, dt.T
    cb = _bdot(cm, bm, NT)
    lane = lax.broadcasted_iota(jnp.int32, (1, CHUNK), 1)
    sub = lax.broadcasted_iota(jnp.int32, (CHUNK, 1), 0)
    ys, hn = [], []
    for j in range(HEADS_PER_GROUP):
        on_lane, on_sub = (lane == j).astype(F32), (sub == j).astype(F32)
        acs_c = jnp.sum(acs * on_lane, axis=1, keepdims=True)
        dt_c = jnp.sum(dt * on_lane, axis=1, keepdims=True)
        acs_r = jnp.sum(acs_t * on_sub, axis=0, keepdims=True)
        dt_r = jnp.sum(dt_t * on_sub, axis=0, keepdims=True)
        acs_last = jnp.sum(acs_c * (sub == CHUNK - 1).astype(F32), axis=0, keepdims=True)
        d_j = jnp.sum(dsk * on_lane, axis=1, keepdims=True)
        decay = jnp.exp(jnp.where(causal, acs_c - acs_r, NEG))
        w = cb * decay * dt_r
        y_diag = _bdot(w, xs[j], NN)
        y_off = _bdot(cm, hs[j], NT) * jnp.exp(acs_c)
        ys.append(y_diag + y_off + d_j * xs[j])
        state = _bdot(xs[j] * (jnp.exp(acs_last - acs_c) * dt_c), bm, TN)
        hn.append(hs[j] * jnp.exp(acs_last) + state)
    return (*ys, *hn)


def _ssd_specs(reverse):
    n_chunks = SEQ // CHUNK
    c_of = (lambda c: n_chunks - 1 - c) if reverse else (lambda c: c)
    x_spec = pl.BlockSpec((CHUNK, 256), lambda g, c: (c_of(c), g))
    b_spec = pl.BlockSpec((CHUNK, N_STATE), lambda g, c: (c_of(c), 8 + g))
    c_spec = pl.BlockSpec((CHUNK, N_STATE), lambda g, c: (c_of(c), 12 + g))
    dt_spec = pl.BlockSpec((CHUNK, 128), lambda g, c: (c_of(c), g))
    vec_spec = pl.BlockSpec((1, 128), lambda g, c: (0, g))
    h_spec = pl.BlockSpec((1, 1, HEADS_PER_GROUP, HEAD, N_STATE), lambda g, c: (c_of(c), g, 0, 0, 0))
    return x_spec, b_spec, c_spec, dt_spec, vec_spec, h_spec


def _ssd_fwd(xbc, dt_raw, bias, alog, dsk):
    x_spec, b_spec, c_spec, dt_spec, vec_spec, h_spec = _ssd_specs(False)

    def body(x_ref, b_ref, c_ref, dt_ref, bias_ref, alog_ref, dsk_ref, y_ref, hin_ref, h_scr):
        @pl.when(pl.program_id(1) == 0)
        def _():
            h_scr[...] = jnp.zeros_like(h_scr)

        hs = [h_scr[j] for j in range(HEADS_PER_GROUP)]
        for j in range(HEADS_PER_GROUP):
            hin_ref[0, 0, j] = hs[j]
        xs = [x_ref[:, pl.ds(j * HEAD, HEAD)] for j in range(HEADS_PER_GROUP)]
        res = _ssd_chunk(*xs, b_ref[...], c_ref[...], dt_ref[...], bias_ref[...], alog_ref[...], dsk_ref[...], *hs)
        for j in range(HEADS_PER_GROUP):
            y_ref[:, pl.ds(j * HEAD, HEAD)] = res[j]
            h_scr[j] = res[HEADS_PER_GROUP + j]

    return pl.pallas_call(
        body, name="ssd_fwd", grid=(N_GROUPS, SEQ // CHUNK),
        in_specs=[x_spec, b_spec, c_spec, dt_spec, vec_spec, vec_spec, vec_spec],
        out_specs=[x_spec, h_spec],
        out_shape=[jax.ShapeDtypeStruct((SEQ, D_SSM), F32),
                   jax.ShapeDtypeStruct((SEQ // CHUNK, N_GROUPS, HEADS_PER_GROUP, HEAD, N_STATE), F32)],
        scratch_shapes=[pltpu.VMEM((HEADS_PER_GROUP, HEAD, N_STATE), F32)],
        compiler_params=_params(("parallel", "arbitrary")),
    )(xbc, xbc, xbc, dt_raw, bias, alog, dsk)


def _ssd_bwd(xbc, dt_raw, bias, alog, dsk, h_in, dy):
    x_spec, b_spec, c_spec, dt_spec, vec_spec, h_spec = _ssd_specs(True)
    dxbc_x = pl.BlockSpec((CHUNK, 256), x_spec.index_map)

    def body(x_ref, b_ref, c_ref, dt_ref, bias_ref, alog_ref, dsk_ref, hin_ref, dy_ref,
             dx_ref, db_ref, dc_ref, ddt_ref, dbias_ref, dalog_ref, ddsk_ref, dh_scr):
        first = pl.program_id(1) == 0

        @pl.when(first)
        def _():
            dh_scr[...] = jnp.zeros_like(dh_scr)

        xs = [x_ref[:, pl.ds(j * HEAD, HEAD)] for j in range(HEADS_PER_GROUP)]
        hs = [hin_ref[0, 0, j] for j in range(HEADS_PER_GROUP)]
        cts = [dy_ref[:, pl.ds(j * HEAD, HEAD)] for j in range(HEADS_PER_GROUP)] + [dh_scr[j] for j in range(HEADS_PER_GROUP)]
        _, pullback = jax.vjp(_ssd_chunk, *xs, b_ref[...], c_ref[...], dt_ref[...], bias_ref[...], alog_ref[...],
                              dsk_ref[...], *hs)
        g = pullback(tuple(cts))
        for j in range(HEADS_PER_GROUP):
            dx_ref[:, pl.ds(j * HEAD, HEAD)] = g[j]
            dh_scr[j] = g[10 + j]
        db_ref[...] = g[4]
        dc_ref[...] = g[5]
        ddt_ref[...] = g[6].astype(ddt_ref.dtype)
        for o_ref, val in ((dbias_ref, g[7]), (dalog_ref, g[8]), (ddsk_ref, g[9])):
            @pl.when(first)
            def _(o_ref=o_ref, val=val):
                o_ref[...] = val

            @pl.when(jnp.logical_not(first))
            def _(o_ref=o_ref, val=val):
                o_ref[...] += val

    n_chunks = SEQ // CHUNK
    out_b = pl.BlockSpec((CHUNK, N_STATE), lambda g, c: (n_chunks - 1 - c, g))
    res = pl.pallas_call(
        body, name="ssd_bwd", grid=(N_GROUPS, n_chunks),
        in_specs=[x_spec, b_spec, c_spec, dt_spec, vec_spec, vec_spec, vec_spec, h_spec, x_spec],
        out_specs=[dxbc_x, out_b, out_b, dt_spec, vec_spec, vec_spec, vec_spec],
        out_shape=[jax.ShapeDtypeStruct((SEQ, D_SSM), F32), jax.ShapeDtypeStruct((SEQ, N_GROUPS * N_STATE), F32),
                   jax.ShapeDtypeStruct((SEQ, N_GROUPS * N_STATE), F32), jax.ShapeDtypeStruct((SEQ, DT_PAD), BF16),
                   jax.ShapeDtypeStruct((1, DT_PAD), F32), jax.ShapeDtypeStruct((1, DT_PAD), F32),
                   jax.ShapeDtypeStruct((1, DT_PAD), F32)],
        scratch_shapes=[pltpu.VMEM((HEADS_PER_GROUP, HEAD, N_STATE), F32)],
        compiler_params=_params(("parallel", "arbitrary")),
    )(xbc, xbc, xbc, dt_raw, bias, alog, dsk, h_in, dy)
    return res


CROSS_HEAD = 128
CROSS_ROWS = 512


def _cross_head(q, k, v, gq, gk):
    qn = _rms(q, gq) * (CROSS_HEAD ** -0.5)
    kn = _rms(k, gk)
    s = _bdot(qn, kn, NT)
    p = jnp.exp(s - lax.stop_gradient(jnp.max(s, axis=-1, keepdims=True)))
    return _bdot(p, v, NN) * (1.0 / jnp.sum(p, axis=-1, keepdims=True))


def _cross_specs():
    q_spec = pl.BlockSpec((CROSS_ROWS, CROSS_HEAD), lambda h, i: (i, h))
    k_spec = pl.BlockSpec((N_MEM, CROSS_HEAD), lambda h, i: (0, h))
    v_spec = pl.BlockSpec((N_MEM, CROSS_HEAD), lambda h, i: (0, 4 + h))
    g_spec = pl.BlockSpec((1, CROSS_HEAD), lambda h, i: (0, 0))
    return q_spec, k_spec, v_spec, g_spec


def _cross_fwd(qc, kv, gq, gk):
    q_spec, k_spec, v_spec, g_spec = _cross_specs()

    def body(q_ref, k_ref, v_ref, gq_ref, gk_ref, o_ref):
        o_ref[...] = _cross_head(q_ref[...], k_ref[...], v_ref[...], gq_ref[...], gk_ref[...]).astype(o_ref.dtype)

    return pl.pallas_call(
        body, name="cross_fwd", grid=(4, SEQ // CROSS_ROWS),
        in_specs=[q_spec, k_spec, v_spec, g_spec, g_spec], out_specs=q_spec,
        out_shape=jax.ShapeDtypeStruct((SEQ, D_CROSS), BF16),
        compiler_params=_params(("parallel", "parallel")),
    )(qc, kv, kv, gq, gk)


def _cross_bwd(qc, kv, gq, gk, do):
    q_spec, k_spec, v_spec, g_spec = _cross_specs()

    def body(q_ref, k_ref, v_ref, gq_ref, gk_ref, do_ref, dq_ref, dk_ref, dv_ref, dgq_ref, dgk_ref):
        _, pullback = jax.vjp(_cross_head, q_ref[...], k_ref[...], v_ref[...], gq_ref[...], gk_ref[...])
        dq, dk, dv, dgq, dgk = pullback(do_ref[...].astype(F32))
        dq_ref[...] = dq.astype(dq_ref.dtype)
        row0 = pl.program_id(1) == 0
        all0 = jnp.logical_and(row0, pl.program_id(0) == 0)
        for o_ref, val, init in ((dk_ref, dk, row0), (dv_ref, dv, row0), (dgq_ref, dgq, all0), (dgk_ref, dgk, all0)):
            @pl.when(init)
            def _(o_ref=o_ref, val=val):
                o_ref[...] = val

            @pl.when(jnp.logical_not(init))
            def _(o_ref=o_ref, val=val):
                o_ref[...] += val

    return pl.pallas_call(
        body, name="cross_bwd", grid=(4, SEQ // CROSS_ROWS),
        in_specs=[q_spec, k_spec, v_spec, g_spec, g_spec, q_spec],
        out_specs=[q_spec, k_spec, k_spec, g_spec, g_spec],
        out_shape=[jax.ShapeDtypeStruct((SEQ, D_CROSS), BF16), jax.ShapeDtypeStruct((N_MEM, D_CROSS), F32),
                   jax.ShapeDtypeStruct((N_MEM, D_CROSS), F32), jax.ShapeDtypeStruct((1, CROSS_HEAD), F32),
                   jax.ShapeDtypeStruct((1, CROSS_HEAD), F32)],
        compiler_params=_params(("arbitrary", "arbitrary")),
    )(qc, kv, kv, gq, gk, do)


def _loss_head(y, target):
    tr = 256

    def body(y_ref, t_ref, dy_ref, dyb_ref, loss_ref):
        err = y_ref[...] - t_ref[...]
        dy = err * (1.0 / D_MODEL)
        dy_ref[...] = dy
        dyb_ref[...] = dy.astype(BF16)
        part = jnp.sum(jnp.sum(err * err, axis=1, keepdims=True), axis=0, keepdims=True) * (0.5 / D_MODEL)
        part = jnp.broadcast_to(part, (1, 128))

        @pl.when(pl.program_id(0) == 0)
        def _():
            loss_ref[...] = part

        @pl.when(pl.program_id(0) != 0)
        def _():
            loss_ref[...] += part

    blk = pl.BlockSpec((tr, D_MODEL), lambda i: (i, 0))
    return pl.pallas_call(
        body, name="loss_head", grid=(SEQ // tr,),
        in_specs=[blk, blk], out_specs=[blk, blk, pl.BlockSpec((1, 128), lambda i: (0, 0))],
        out_shape=[jax.ShapeDtypeStruct((SEQ, D_MODEL), F32), jax.ShapeDtypeStruct((SEQ, D_MODEL), BF16),
                   jax.ShapeDtypeStruct((1, 128), F32)],
        compiler_params=_params(("arbitrary",)),
    )(y, target)


def _pad_heads(v):
    return jnp.pad(v.reshape(N_GROUPS, HEADS_PER_GROUP), ((0, 0), (0, 128 - HEADS_PER_GROUP))).reshape(1, DT_PAD)


def _unpad_heads(v):
    return v.reshape(v.shape[0], N_GROUPS, 128)[:, :, :HEADS_PER_GROUP].reshape(v.shape[0], N_DT)


def _rope_tables(positions):
    half = ROT // 2
    inv_freq = ROPE_THETA ** (-2.0 * jnp.arange(half, dtype=F32) / ROT)
    ang = positions.reshape(SEQ, 1).astype(F32) * inv_freq
    cos, sin = jnp.cos(ang), jnp.sin(ang)
    ones, zeros = jnp.ones((SEQ, HEAD - ROT), F32), jnp.zeros((SEQ, HEAD - ROT), F32)
    cos_h = jnp.concatenate([cos, cos, ones], axis=1)
    sin_h = jnp.concatenate([-sin, sin, zeros], axis=1)
    return jnp.tile(cos_h, (1, 2)), jnp.tile(sin_h, (1, 2))


def _add_res(acc, res):
    return (acc + res,)


def _settle(grads, *after):
    if hasattr(grads, "settle"):
        grads.settle(*after)


def _take_token(grads):
    token = getattr(grads, "token", None)
    if token is None:
        return ()
    grads.token = None
    return (token,)


def _local_step(x, mem, positions, target, p, w, more_weights=None, grads=None, h=None):
    grads = {} if grads is None else grads
    w = dict(w)
    cos, sin = _rope_tables(positions)
    gq2, gk2 = jnp.tile(p["g_q"], (1, 2)), jnp.tile(p["g_k"], (1, 2))
    bias, alog, dsk = _pad_heads(p["dt_bias"]), _pad_heads(p["a_log"]), _pad_heads(p["d_skip"])
    norm_out = [(D_MODEL, BF16, D_MODEL, 0, False)]

    if h is None:
        h = _rowwise(_norm_fn, [_full(x)], [_full(p["g_mix"])], norm_out, name="norm_in")[0]
    proj = _matmul(h, w["w_in"], mode="nn", name="in_proj", outs=[F32], n_cols=D_MAIN)
    dt_raw = _matmul(h, w["w_dt"], mode="nn", name="dt_proj", outs=[F32])
    qk_rows = [(proj, 128, 0, True), (proj, 128, 8, True), (proj, 128, 16, True), _full(cos), _full(sin)]
    qk_vecs = [_full(gq2), _full(gk2)]
    qn, kn, vn = _rowwise(_qk_fn, qk_rows, qk_vecs, [(D_ATTN, F32, 128, 0, True)] * 3, name="qk_prep", groups=8, tr=1024)
    branches = [_attention_fwd(qn, kn, vn, b) for b in range(3)]
    merge_rows = [_full(o) for o, _ in branches] + [_full(lse) for _, lse in branches]
    attn = _rowwise(_merge_fn, merge_rows, [_full(p["g_attn_out"])], [(D_ATTN, BF16, D_ATTN, 0, False)], name="attn_merge")[0]
    xbc = _conv_fwd(proj, p["conv_w"], p["conv_b"])
    y_ssd, h_in = _ssd_fwd(xbc, dt_raw, bias, alog, dsk)
    gate_rows = [(y_ssd, 256, 0, True), (proj, 256, 12, True)]
    gate_vecs = [(p["g_ssm_out"], 256, 0, True)]
    ssm = _rowwise(_gate_fn, gate_rows, gate_vecs, [(D_SSM, BF16, 256, 0, True)], name="ssm_gate", groups=4)[0]
    mix = jnp.concatenate([attn, ssm], axis=1)
    if more_weights is not None:
        w.update(more_weights("mixer_done", mix))
    x1 = _matmul(mix, w["w_out"], mode="nn", name="out_proj", outs=[F32], extra=(x,), epilogue=_add_res)
    hc = _rowwise(_norm_fn, [_full(x1)], [_full(p["g_cross"])], norm_out, name="norm_cross")[0]
    memh = _rowwise(_norm_fn, [_full(mem)], [_full(p["g_mem"])], norm_out, name="norm_mem", n_rows=N_MEM)[0]
    qc = _matmul(hc, w["w_cq"], mode="nn", name="cq_proj", outs=[F32])
    kv = _matmul(memh, w["w_ckv"], mode="nn", name="ckv_proj", outs=[F32])
    oc = _cross_fwd(qc, kv, p["g_cq"], p["g_ck"])
    x2 = _matmul(oc, w["w_co"], mode="nn", name="co_proj", outs=[F32], extra=(x1,), epilogue=_add_res)
    hm = _rowwise(_norm_fn, [_full(x2)], [_full(p["g_mlp"])], norm_out, name="norm_mlp")[0]
    if more_weights is not None:
        w.update(more_weights("cross_done", hm))
    u, act = _matmul(hm, w["w_up"], mode="nn", name="up_proj", outs=[F32, BF16],
                     epilogue=lambda acc: (acc, jnp.square(jnp.maximum(acc, 0.0))))
    x3 = _matmul(act, w["w_down"], mode="nn", name="down_proj", outs=[F32], extra=(x2,), epilogue=_add_res)
    dy, dyb, loss = _loss_head(x3, target)

    grads["w_down"] = _matmul(act, dyb, mode="tn", name="dw_down", outs=[BF16], after=_take_token(grads))
    du = _matmul(dyb, w["w_down"], mode="nt", name="d_act", outs=[BF16], extra=(u,), after=_take_token(grads),
                 epilogue=lambda acc, uu: (acc * (2.0 * jnp.maximum(uu, 0.0)),))
    _settle(grads, du)
    grads["w_up"] = _matmul(hm, du, mode="tn", name="dw_up", outs=[BF16], col_shards=4, after=_take_token(grads))
    dhm = _matmul(du, w["w_up"], mode="nt", name="d_hm", outs=[F32], after=_take_token(grads))
    _settle(grads, dhm)
    dx2, grads["g_mlp"] = _rowwise_vjp(
        _norm_fn, [_full(x2)], [_full(p["g_mlp"])], [[_full(dhm)]],
        [(0, D_MODEL, F32, D_MODEL, 0, False, _full(dy))], [(0, D_MODEL, D_MODEL, 0, False)], name="norm_mlp_bwd")
    grads["w_co"] = _matmul(oc, dx2, mode="tn", name="dw_co", outs=[BF16], col_shards=4, after=_take_token(grads))
    doc = _matmul(dx2, w["w_co"], mode="nt", name="d_oc", outs=[BF16])
    dqc, dkc, dvc, grads["g_cq"], grads["g_ck"] = _cross_bwd(qc, kv, p["g_cq"], p["g_ck"], doc)
    grads["w_cq"] = _matmul(hc, dqc, mode="tn", name="dw_cq", outs=[BF16])
    dhc = _matmul(dqc, w["w_cq"], mode="nt", name="d_hc", outs=[F32])
    dkv = jnp.concatenate([dkc, dvc], axis=1)
    grads["w_ckv"] = _matmul(memh, dkv, mode="tn", name="dw_ckv", outs=[BF16])
    dmemh = _matmul(dkv, w["w_ckv"], mode="nt", name="d_memh", outs=[F32])
    grads["g_mem"] = _rowwise_vjp(_norm_fn, [_full(mem)], [_full(p["g_mem"])], [[_full(dmemh)]], [],
                                  [(0, D_MODEL, D_MODEL, 0, False)], name="norm_mem_bwd", n_rows=N_MEM)[0]
    dx1, grads["g_cross"] = _rowwise_vjp(
        _norm_fn, [_full(x1)], [_full(p["g_cross"])], [[_full(dhc)]],
        [(0, D_MODEL, F32, D_MODEL, 0, False, _full(dx2))], [(0, D_MODEL, D_MODEL, 0, False)], name="norm_cross_bwd")
    grads["w_out"] = _matmul(mix, dx1, mode="tn", name="dw_out", outs=[BF16])
    dmix = _matmul(dx1, w["w_out"], mode="nt", name="d_mix", outs=[F32], after=_take_token(grads))
    _settle(grads, dmix)
    merge_grads = [(i, D_ATTN, F32, D_ATTN, 0, False, None) for i in range(6)]
    *dol, grads["g_attn_out"] = _rowwise_vjp(
        _merge_fn, merge_rows, [_full(p["g_attn_out"])], [[(dmix, D_ATTN, 0, False)]],
        merge_grads, [(0, D_ATTN, D_ATTN, 0, False)], name="attn_merge_bwd", after=_take_token(grads))
    dqkv = [_attention_bwd(qn, kn, vn, *branches[b], dol[b], dol[3 + b], b) for b in range(3)]
    qk_cts = [[(dqkv[b][i], 128, 0, True) for b in range(3)] for i in range(3)]
    dq, dk, dv, dgq2, dgk2 = _rowwise_vjp(
        _qk_fn, qk_rows, qk_vecs, qk_cts, [(i, D_ATTN, BF16, 128, 0, True, None) for i in range(3)],
        [(0, 128, 128, 0, False), (1, 128, 128, 0, False)], name="qk_prep_bwd", groups=8, tr=512)
    grads["g_q"] = dgq2[:, :HEAD] + dgq2[:, HEAD:]
    grads["g_k"] = dgk2[:, :HEAD] + dgk2[:, HEAD:]
    dy_ssd, dz, grads["g_ssm_out"] = _rowwise_vjp(
        _gate_fn, gate_rows, gate_vecs, [[(dmix, 256, 4, True)]],
        [(0, D_SSM, F32, 256, 0, True, None), (1, D_SSM, BF16, 256, 0, True, None)],
        [(0, D_SSM, 256, 0, True)], name="ssm_gate_bwd", groups=4)
    dxs, db, dc, ddt, dbias, dalog, ddsk = _ssd_bwd(xbc, dt_raw, bias, alog, dsk, h_in, dy_ssd)
    grads["dt_bias"], grads["a_log"], grads["d_skip"] = _unpad_heads(dbias), _unpad_heads(dalog), _unpad_heads(ddsk)
    dxbc_raw, dconv_w, grads["conv_b"] = _conv_bwd(proj, p["conv_w"], p["conv_b"], jnp.concatenate([dxs, db, dc], axis=1))
    grads["conv_w"] = dconv_w[:4]
    dproj = jnp.concatenate([dq, dk, dv, dz, dxbc_raw], axis=1)
    grads["w_main"] = _matmul(h, dproj, mode="tn", name="dw_main", outs=[BF16], out_cols=D_MAIN + N_DT)
    grads["w_dt"] = _matmul(h, ddt, mode="tn", name="dw_dt", outs=[BF16])
    dh = _matmul(dproj, w["w_in"], mode="nt", name="d_h_main", outs=[F32], after=_take_token(grads))
    dh = _matmul(ddt, w["w_dt"], mode="nt", name="d_h_dt", outs=[F32], extra=(dh,), epilogue=_add_res)
    grad_x, grads["g_mix"] = _rowwise_vjp(
        _norm_fn, [_full(x)], [_full(p["g_mix"])], [[_full(dh)]],
        [(0, D_MODEL, F32, D_MODEL, 0, False, _full(dx1))], [(0, D_MODEL, D_MODEL, 0, False)], name="norm_in_bwd")
    return loss, grad_x, grads


MATRICES = ("w_in", "w_out", "w_cq", "w_ckv", "w_co", "w_up", "w_down")
ROW_SHARDED = ("w_out", "w_cq", "w_ckv", "w_down")
N_CHIPS = 4
ANY = pl.BlockSpec(memory_space=pl.ANY)


def _place():
    return lax.axis_index("x"), lax.axis_index("y"), lax.axis_index("c")


def _other_chips(x, y):
    return [(1 - x, y), (x, 1 - y), (1 - x, 1 - y)]


def _remote(src, dst, send_sem, recv_sem, device):
    return pltpu.make_async_remote_copy(src_ref=src, dst_ref=dst, send_sem=send_sem, recv_sem=recv_sem,
                                        device_id=device, device_id_type=MESH)


def _gathered_shape(name, shard):
    rows, cols = shard.shape
    if name == "w_in":
        return (N_CHIPS, rows, cols)
    return (N_CHIPS * rows, cols) if name in ROW_SHARDED else (rows, N_CHIPS * cols)


def _shard_window(name, ref, rows, cols, chip, half):
    r0, nr = (0, rows) if half is None else (half * (rows // 2), rows // 2)
    if name == "w_in":
        return ref.at[chip, pl.ds(r0, nr), :]
    if name in ROW_SHARDED:
        return ref.at[pl.ds(chip * rows + r0, nr), :]
    return ref.at[pl.ds(r0, nr), pl.ds(pl.multiple_of(chip * cols, 128), cols)]


def _cast_into_gathered(w, name, chip, after=()):
    rows, cols = w.shape
    tr = _tile(rows, ROW_TILE)

    def body(chip_ref, w_ref, *rest):
        rest[-1][...] = w_ref[...].astype(BF16)

    if name == "w_in":
        out_spec = pl.BlockSpec((None, tr, cols), lambda i, chip_ref: (chip_ref[0], i, 0))
    elif name in ROW_SHARDED:
        out_spec = pl.BlockSpec((tr, cols), lambda i, chip_ref: (chip_ref[0] * (rows // tr) + i, 0))
    else:
        out_spec = pl.BlockSpec((tr, cols), lambda i, chip_ref: (i, chip_ref[0]))
    grid_spec = pltpu.PrefetchScalarGridSpec(
        num_scalar_prefetch=1, grid=(rows // tr,),
        in_specs=[pl.BlockSpec((tr, cols), lambda i, chip_ref: (i, 0))] + [pl.BlockSpec(memory_space=pl.ANY)] * len(after),
        out_specs=out_spec)
    return pl.pallas_call(body, name="cast_" + name, grid_spec=grid_spec,
                          out_shape=jax.ShapeDtypeStruct(_gathered_shape(name, w), BF16),
                          compiler_params=_params(("parallel",)))(chip.reshape(1).astype(jnp.int32), w, *after)


def _w_in_columns(arr, to_shards):
    rows, piece = D_MODEL, (D_MAIN + N_DT) // N_CHIPS
    tr = ROW_TILE

    def body(a_ref, o_ref):
        for j in range(N_CHIPS):
            if to_shards:
                o_ref[j] = a_ref[:, pl.ds(piece * j, piece)]
            else:
                o_ref[:, pl.ds(piece * j, piece)] = a_ref[j]

    pieces = pl.BlockSpec((N_CHIPS, tr, piece), lambda i: (0, i, 0))
    matrix = pl.BlockSpec((tr, N_CHIPS * piece), lambda i: (i, 0))
    out_dims = (N_CHIPS, rows, piece) if to_shards else (rows, N_CHIPS * piece)
    return pl.pallas_call(
        body, name="w_in_to_shards" if to_shards else "w_in_from_shards", grid=(rows // tr,),
        in_specs=[matrix if to_shards else pieces], out_specs=pieces if to_shards else matrix,
        out_shape=jax.ShapeDtypeStruct(out_dims, arr.dtype), compiler_params=_params(("parallel",)))(arr)


HBM = pl.BlockSpec(memory_space=pltpu.HBM)
SEM = pl.BlockSpec(memory_space=pltpu.SEMAPHORE)
EFFECT = pltpu.SideEffectType.DATAFLOW_SIDE_EFFECTING


def _split_start(name, bufs, plan, counts, after=()):
    n, n_g, n_after = len(bufs), len(counts), len(after)

    def body(*refs):
        ins, sems, token = refs[:n], refs[n + n_after:n + n_after + 2 * n_g], refs[-1]
        for g, copies in enumerate(plan(ins)):
            for i, (src, dst, device, _) in enumerate(copies):
                _remote(src, dst, sems[2 * g].at[i], sems[2 * g + 1].at[i], device).start()
        token[...] = jnp.zeros_like(token)

    sem_shapes = [pltpu.SemaphoreType.DMA((cnt,)) for cnt in counts for _ in range(2)]
    res = pl.pallas_call(
        body, name=name,
        out_shape=(*sem_shapes, *[pltpu.HBM(b.shape, b.dtype) for b in bufs], jax.ShapeDtypeStruct((8, 128), F32)),
        in_specs=(*(HBM,) * n, *(ANY,) * n_after),
        out_specs=(*(SEM,) * (2 * n_g), *(HBM,) * n, pl.BlockSpec(memory_space=pltpu.VMEM)),
        input_output_aliases={i: 2 * n_g + i for i in range(n)},
        compiler_params=pltpu.CompilerParams(has_side_effects=EFFECT),
    )(*[pltpu.with_memory_space_constraint(b, pltpu.HBM) for b in bufs], *after)
    sems = [(res[2 * g], res[2 * g + 1]) for g in range(n_g)]
    return sems, list(res[2 * n_g:2 * n_g + n]), res[-1]


def _split_wait(name, bufs, sems, plan, *after):
    n = len(bufs)

    def body(*refs):
        ins, send, recv = refs[:n], refs[n], refs[n + 1]
        (copies,) = plan(ins)
        for i, (src, _, device, landing) in enumerate(copies):
            cp = _remote(src, landing, send.at[i], recv.at[i], device)
            cp.wait_send()
            cp.wait_recv()

    res = pl.pallas_call(
        body, name=name, out_shape=tuple(pltpu.HBM(b.shape, b.dtype) for b in bufs),
        in_specs=(*(HBM,) * n, SEM, SEM, *(ANY,) * len(after)), out_specs=(HBM,) * n,
        input_output_aliases={i: i for i in range(n)},
        compiler_params=pltpu.CompilerParams(has_side_effects=EFFECT),
    )(*bufs, sems[0], sems[1], *after)
    return list(res)


def _ici_plan(names, shard_shapes):
    def plan(refs):
        x, y, c = _place()
        copies = []
        for ref, name in zip(refs, names):
            win = _shard_window(name, ref, *shard_shapes[name], 2 * x + y, c)
            for px, py in _other_chips(x, y):
                copies.append((win, win, (px, py, c), _shard_window(name, ref, *shard_shapes[name], 2 * px + py, c)))
        return [copies]
    return plan


def _pass_on_plan(names, shard_shapes):
    def plan(refs):
        x, y, c = _place()
        copies = []
        for ref, name in zip(refs, names):
            for px, py in _other_chips(x, y):
                win = _shard_window(name, ref, *shard_shapes[name], 2 * px + py, c)
                copies.append((win, win, (x, y, 1 - c), _shard_window(name, ref, *shard_shapes[name], 2 * px + py, 1 - c)))
        return [copies]
    return plan


def _swap_plan(n_pairs):
    def plan(refs):
        x, y, c = _place()
        return [[(src.at[:, 1 - c], dst, (x, y, 1 - c), dst) for src, dst in zip(refs[:n_pairs], refs[n_pairs:])]]
    return plan


def _share_plan(n_pairs):
    def plan(refs):
        x, y, c = _place()
        return [[(src, dst, (x, y, 1 - c), dst)] for src, dst in zip(refs[:n_pairs], refs[n_pairs:])]
    return plan


def _scatter_plan(n_pairs):
    def plan(refs):
        x, y, c = _place()
        copies = []
        for src, dst in zip(refs[:n_pairs], refs[n_pairs:]):
            for k, (px, py) in enumerate(_other_chips(x, y)):
                copies.append((src.at[2 * px + py], dst.at[k], (px, py, c), dst.at[k]))
        return [copies]
    return plan


def _sibling_swap(arrs, name):
    n = len(arrs)

    def body(*refs):
        ins, outs, send, recv = refs[:n], refs[n:2 * n], refs[2 * n], refs[2 * n + 1]
        x, y, c = _place()
        cps = [_remote(ins[w].at[:, 1 - c], outs[w], send.at[w], recv.at[w], (x, y, 1 - c)) for w in range(n)]
        for cp in cps:
            cp.start()
        for cp in cps:
            cp.wait()

    return pl.pallas_call(
        body, name=name, in_specs=[ANY] * n, out_specs=[ANY] * n,
        out_shape=[jax.ShapeDtypeStruct((a.shape[0],) + a.shape[2:], a.dtype) for a in arrs],
        scratch_shapes=[pltpu.SemaphoreType.DMA((n,))] * 2,
    )(*arrs)


def _small_allreduce(buf, name):
    rows = buf.shape[0]

    def body(x_ref, out_ref, all_ref, send_sems, recv_sems, local_sem):
        x, y, c = _place()
        me, sibling, chips = (x, y, c), (x, y, 1 - c), _other_chips(x, y)

        def block(px, py, pc):
            return all_ref.at[pl.ds((4 * px + 2 * py + pc) * rows, rows), :]

        def copy(k, blk, to, src=None):
            return _remote(block(*blk) if src is None else src, block(*blk), send_sems.at[k], recv_sems.at[k], to)

        own = pltpu.make_async_copy(x_ref, block(*me), local_sem)
        own.start()
        first = [copy(0, me, sibling, src=x_ref)] + [copy(1 + j, me, (*chip, c), src=x_ref) for j, chip in enumerate(chips)]
        for cp in first:
            cp.start()
        passed = [copy(4 + j, (*chip, c), sibling) for j, chip in enumerate(chips)]
        for j, chip in enumerate(chips):
            copy(1 + j, (*chip, c), me).wait_recv()
            passed[j].start()
        copy(0, sibling, me).wait_recv()
        for j, chip in enumerate(chips):
            copy(4 + j, (*chip, 1 - c), me).wait_recv()
        for cp in first + passed:
            cp.wait_send()
        own.wait()
        acc = all_ref[pl.ds(0, rows), :]
        for d in range(1, 8):
            acc = acc + all_ref[pl.ds(d * rows, rows), :]
        out_ref[...] = acc

    vmem = pl.BlockSpec(memory_space=pltpu.VMEM)
    return pl.pallas_call(
        body, name=name, in_specs=[vmem], out_specs=vmem,
        out_shape=jax.ShapeDtypeStruct(buf.shape, F32),
        scratch_shapes=[pltpu.VMEM((8 * rows, 128), F32), pltpu.SemaphoreType.DMA((7,)), pltpu.SemaphoreType.DMA((7,)),
                        pltpu.SemaphoreType.DMA],
    )(buf)


ROW_TILE = 256


def _add_halves(arr, recv, c, name):
    _, _, hr, cols = arr.shape
    tr = _tile(hr, ROW_TILE)

    def body(c_ref, a_ref, r_ref, o_ref):
        o_ref[...] = (a_ref[...].astype(F32) + r_ref[...].astype(F32)).astype(o_ref.dtype)

    piece = pl.BlockSpec((None, tr, cols), lambda j, i, c_ref: (j, i, 0))
    grid_spec = pltpu.PrefetchScalarGridSpec(
        num_scalar_prefetch=1, grid=(N_CHIPS, hr // tr),
        in_specs=[pl.BlockSpec((None, None, tr, cols), lambda j, i, c_ref: (j, c_ref[0], i, 0)), piece], out_specs=piece)
    return pl.pallas_call(body, name=name, grid_spec=grid_spec, out_shape=jax.ShapeDtypeStruct(recv.shape, BF16),
                          compiler_params=_params(("parallel", "parallel")))(c.reshape(1).astype(jnp.int32), arr, recv)


def _flip_slot(d):
    return jnp.where(d == 1, 1, jnp.where(d == 3, 2, 0))


def _sum_chips(p, q, chip, name):
    _, hr, cols = p.shape
    tr = _tile(hr, ROW_TILE)

    def body(chip_ref, p_ref, q_ref, o_ref):
        j = pl.program_id(1)
        term = jnp.where(j == chip_ref[0], p_ref[...].astype(F32), q_ref[...].astype(F32))

        @pl.when(j == 0)
        def _():
            o_ref[...] = term

        @pl.when(j != 0)
        def _():
            o_ref[...] += term

    grid_spec = pltpu.PrefetchScalarGridSpec(
        num_scalar_prefetch=1, grid=(hr // tr, N_CHIPS),
        in_specs=[pl.BlockSpec((None, tr, cols), lambda i, j, chip_ref: (chip_ref[0], i, 0)),
                  pl.BlockSpec((None, tr, cols), lambda i, j, chip_ref: (_flip_slot(j ^ chip_ref[0]), i, 0))],
        out_specs=pl.BlockSpec((tr, cols), lambda i, j, chip_ref: (i, 0)))
    return pl.pallas_call(body, name=name, grid_spec=grid_spec, out_shape=jax.ShapeDtypeStruct((hr, cols), F32),
                          compiler_params=_params(("parallel", "arbitrary")))(chip.reshape(1).astype(jnp.int32), p, q)


def _adamw_halves(w, g_own, g_other, m, v, c, name):
    rows, cols = w.shape
    tr = _tile(rows // 2, ROW_TILE)
    per_half = rows // 2 // tr

    def body(c_ref, w_ref, own_ref, other_ref, m_ref, v_ref, g_ref, d_ref, nm_ref, nv_ref):
        mine = (pl.program_id(0) // per_half) == c_ref[0]
        g_ = jnp.where(mine, own_ref[...], other_ref[...])
        g_ref[...] = g_
        d_ref[...], nm_ref[...], nv_ref[...] = _adamw_math(w_ref[...], g_, m_ref[...], v_ref[...])

    blk = pl.BlockSpec((tr, cols), lambda i, c_ref: (i, 0))
    own = pl.BlockSpec((tr, cols), lambda i, c_ref: (jnp.where(i // per_half == c_ref[0], i % per_half, 0), 0))
    other = pl.BlockSpec((tr, cols), lambda i, c_ref: (jnp.where(i // per_half == c_ref[0], 0, i % per_half), 0))
    grid_spec = pltpu.PrefetchScalarGridSpec(num_scalar_prefetch=1, grid=(rows // tr,),
                                             in_specs=[blk, own, other, blk, blk], out_specs=[blk] * 4)
    return pl.pallas_call(body, name=name, grid_spec=grid_spec, out_shape=[jax.ShapeDtypeStruct(w.shape, F32)] * 4,
                          compiler_params=_params(("parallel",)))(c.reshape(1).astype(jnp.int32), w, g_own, g_other, m, v)


W_IN_COLS = (D_MAIN + N_DT) // N_CHIPS
W_IN_MAIN = W_IN_COLS // 128 * 128
W_IN_TAIL = W_IN_COLS - 128
W_IN_PARTS = ((0, W_IN_MAIN), (W_IN_TAIL, 128))


def _cast_w_in_transposed(w_t, chip, after=()):
    def body(chip_ref, w_ref, *rest):
        for start, size in W_IN_PARTS:
            rest[-1][:, pl.ds(start, size)] = w_ref[pl.ds(start, size), :].T.astype(BF16)

    grid_spec = pltpu.PrefetchScalarGridSpec(
        num_scalar_prefetch=1, grid=(D_MODEL // ROW_TILE,),
        in_specs=[pl.BlockSpec((W_IN_COLS, ROW_TILE), lambda i, chip_ref: (0, i))] + [pl.BlockSpec(memory_space=pl.ANY)] * len(after),
        out_specs=pl.BlockSpec((None, ROW_TILE, W_IN_COLS), lambda i, chip_ref: (chip_ref[0], i, 0)))
    return pl.pallas_call(body, name="cast_w_in", grid_spec=grid_spec,
                          out_shape=jax.ShapeDtypeStruct((N_CHIPS, D_MODEL, W_IN_COLS), BF16),
                          compiler_params=_params(("parallel",)))(chip.reshape(1).astype(jnp.int32), w_t, *after)


def _adamw_w_in_transposed(w_t, g_own, g_other, m_t, v_t, c):
    per_half = D_MODEL // 2 // ROW_TILE

    def body(c_ref, w_ref, own_ref, other_ref, m_ref, v_ref, g_ref, d_ref, nm_ref, nv_ref):
        mine = (pl.program_id(0) // per_half) == c_ref[0]
        for start, size in W_IN_PARTS:
            cols, rows = pl.ds(start, size), pl.ds(start, size)
            g_ = jnp.where(mine, own_ref[:, cols], other_ref[:, cols]).T
            g_ref[rows, :] = g_
            d_ref[rows, :], nm_ref[rows, :], nv_ref[rows, :] = _adamw_math(w_ref[rows, :], g_, m_ref[rows, :], v_ref[rows, :])

    blk = pl.BlockSpec((W_IN_COLS, ROW_TILE), lambda i, c_ref: (0, i))
    own = pl.BlockSpec((ROW_TILE, W_IN_COLS), lambda i, c_ref: (jnp.where(i // per_half == c_ref[0], i % per_half, 0), 0))
    other = pl.BlockSpec((ROW_TILE, W_IN_COLS), lambda i, c_ref: (jnp.where(i // per_half == c_ref[0], 0, i % per_half), 0))
    grid_spec = pltpu.PrefetchScalarGridSpec(num_scalar_prefetch=1, grid=(D_MODEL // ROW_TILE,),
                                             in_specs=[blk, own, other, blk, blk], out_specs=[blk] * 4)
    return pl.pallas_call(body, name="adamw_w_in", grid_spec=grid_spec, out_shape=[jax.ShapeDtypeStruct(w_t.shape, F32)] * 4,
                          compiler_params=_params(("parallel",)))(c.reshape(1).astype(jnp.int32), w_t, g_own, g_other, m_t, v_t)


def _adamw_math(w, g, m, v):
    m_new = ADAM_B1 * m + (1.0 - ADAM_B1) * g
    v_new = ADAM_B2 * v + (1.0 - ADAM_B2) * (g * g)
    m_hat = m_new / (1.0 - ADAM_B1 ** ADAM_STEP)
    v_hat = v_new / (1.0 - ADAM_B2 ** ADAM_STEP)
    return -ADAM_LR * (m_hat / (jnp.sqrt(v_hat) + ADAM_EPS) + ADAM_WD * w), m_new, v_new


def _adamw(w, g, m, v, name):
    rows, cols = w.shape
    tr = _tile(rows, ROW_TILE)

    def body(w_ref, g_ref, m_ref, v_ref, d_ref, nm_ref, nv_ref):
        d_ref[...], nm_ref[...], nv_ref[...] = _adamw_math(w_ref[...], g_ref[...], m_ref[...], v_ref[...])

    blk = pl.BlockSpec((tr, cols), lambda i: (i, 0))
    return pl.pallas_call(body, name=name, grid=(rows // tr,), in_specs=[blk] * 4, out_specs=[blk] * 3,
                          out_shape=[jax.ShapeDtypeStruct(w.shape, F32)] * 3, compiler_params=_params(("parallel",)))(w, g, m, v)


VECTORS = ("g_mix", "g_q", "g_k", "g_attn_out", "conv_b", "dt_bias", "a_log", "d_skip", "g_ssm_out", "g_cross", "g_mem",
           "g_cq", "g_ck", "g_mlp")
WEIGHTS = ("g_mix", "w_in", "g_q", "g_k", "g_attn_out", "conv_w", "conv_b", "dt_bias", "a_log", "d_skip", "g_ssm_out", "w_out",
           "g_cross", "g_mem", "w_cq", "w_ckv", "g_cq", "g_ck", "w_co", "g_mlp", "w_up", "w_down")


def _pack(parts):
    flat = jnp.concatenate([t.reshape(-1) for t in parts])
    total = -(-flat.shape[0] // 1024) * 1024
    return jnp.pad(flat, (0, total - flat.shape[0])).reshape(total // 128, 128)


def _unpack(buf, shapes):
    flat, out, pos = buf.reshape(-1), [], 0
    for shape in shapes:
        size = math.prod(shape)
        out.append(flat[pos:pos + size].reshape(shape))
        pos += size
    return out


def kernel(x, mem, positions, g_mix, w_in, g_q, g_k, g_attn_out, conv_w, conv_b, dt_bias, a_log, d_skip, g_ssm_out, w_out, g_cross, g_mem, w_cq, w_ckv, g_cq, g_ck, w_co, g_mlp, w_up, w_down, loss_target, m_g_mix, m_w_in, m_g_q, m_g_k, m_g_attn_out, m_conv_w, m_conv_b, m_dt_bias, m_a_log, m_d_skip, m_g_ssm_out, m_w_out, m_g_cross, m_g_mem, m_w_cq, m_w_ckv, m_g_cq, m_g_ck, m_w_co, m_g_mlp, m_w_up, m_w_down, v_g_mix, v_w_in, v_g_q, v_g_k, v_g_attn_out, v_conv_w, v_conv_b, v_dt_bias, v_a_log, v_d_skip, v_g_ssm_out, v_w_out, v_g_cross, v_g_mem, v_w_cq, v_w_ckv, v_g_cq, v_g_ck, v_w_co, v_g_mlp, v_w_up, v_w_down):
    args = dict(locals())
    weights = {n: args[n][0] for n in WEIGHTS}
    mom_m = {n: args["m_" + n][0] for n in WEIGHTS}
    mom_v = {n: args["v_" + n][0] for n in WEIGHTS}
    x_idx, y_idx, c_idx = _place()
    chip = 2 * x_idx + y_idx

    conv_parts = _small_allreduce(_pack([jnp.zeros((N_CHIPS, 4, 512), F32).at[chip].set(0.5 * weights["conv_w"])]),
                                  "gather_conv_taps")
    shapes = {n: weights[n].shape for n in MATRICES}
    first, mid, late = ("w_in",), ("w_out", "w_cq", "w_ckv", "w_co"), ("w_up", "w_down")
    w_in_t, m_in_t, v_in_t = (jnp.swapaxes(t, 1, 2)[0] for t in (w_in, m_w_in, v_w_in))
    w_in_buf = [_cast_w_in_transposed(w_in_t, chip)]
    sems_in, w_in_buf, token = _split_start("gather_ici_start_w_in", w_in_buf, _ici_plan(first, shapes), [3], after=(conv_parts,))
    bufs = [_cast_into_gathered(weights[n], n, chip, after=(token,)) for n in mid + late]
    sems_rest, bufs, token = _split_start("gather_ici_start_rest", bufs, _ici_plan(mid + late, shapes), [18], after=(token,))
    ici_sems = (sems_in[0], sems_rest[0])
    params = {n: weights[n].reshape(1, -1) for n in VECTORS}
    h_in = _rowwise(_norm_fn, [_full(x[0])], [_full(params["g_mix"])], [(D_MODEL, BF16, D_MODEL, 0, False)], name="norm_in",
                    after=(token,))[0]
    w_in_buf = _split_wait("gather_ici_wait_w_in", w_in_buf, ici_sems[0], _ici_plan(first, shapes), token, h_in)
    pass_sems, w_in_buf, token = _split_start("gather_pass_start_w_in", w_in_buf, _pass_on_plan(first, shapes), [3])
    w_in_buf = _split_wait("gather_pass_wait_w_in", w_in_buf, pass_sems[0], _pass_on_plan(first, shapes), token)
    w_in_full = _w_in_columns(w_in_buf[0], to_shards=False)
    full = {"w_in": w_in_full,
            "w_dt": jnp.pad(w_in_full[:, D_MAIN:].reshape(D_MODEL, N_GROUPS, HEADS_PER_GROUP),
                            ((0, 0), (0, 0), (0, 128 - HEADS_PER_GROUP))).reshape(D_MODEL, DT_PAD)}
    in_flight = {}

    def more_weights(stage, after):
        if stage == "mixer_done":
            rest = _split_wait("gather_ici_wait_rest", bufs, ici_sems[1], _ici_plan(mid + late, shapes), after)
            plan = lambda refs: _pass_on_plan(mid, shapes)(refs[:4]) + _pass_on_plan(late, shapes)(refs[4:])
            sems, rest, token = _split_start("gather_pass_start_rest", rest, plan, [12, 6])
            in_flight["late"] = (rest[4:], sems[1])
            return dict(zip(mid, _split_wait("gather_pass_wait_mid", rest[:4], sems[0], _pass_on_plan(mid, shapes), token)))
        late_bufs, sems = in_flight.pop("late")
        return dict(zip(late, _split_wait("gather_pass_wait_late", late_bufs, sems, _pass_on_plan(late, shapes), after)))

    params["conv_w"] = _unpack(conv_parts, [(N_CHIPS, 4, 512)])[0].transpose(1, 0, 2).reshape(4, 4 * 512)

    groups = (("w_down",), ("w_up",), ("w_co", "w_cq", "w_ckv", "w_out"), ("w_in",))
    scattered = []

    class GradStore(dict):
        pending = None

        def __setitem__(self, name, value):
            super().__setitem__(name, value)
            if "w_main" in self and "w_dt" in self and "w_in" not in self:
                gw_in = lax.dynamic_update_slice(self["w_main"], _unpad_heads(self["w_dt"]), (0, D_MAIN))
                self["w_in"] = _w_in_columns(gw_in, to_shards=True)
            for group in groups:
                if name in group and all(n in self for n in group):
                    self.settle()
                    pieces = [self[n].reshape(N_CHIPS, 2, shapes[n][0] // 2, shapes[n][1]) for n in group]
                    if group == groups[-1]:
                        self.scatter(group, pieces, _sibling_swap(pieces, "grad_swap_" + group[0]))
                    else:
                        landing = [lax.empty((N_CHIPS,) + a.shape[2:], BF16) for a in pieces]
                        sems, thru, self.token = _split_start("grad_swap_start_" + group[0], pieces + landing,
                                                              _swap_plan(len(pieces)), [len(pieces)])
                        self.pending = (group, sems[0], thru)

        def settle(self, *after):
            if self.pending is not None:
                group, sems, thru = self.pending
                self.pending = None
                thru = _split_wait("grad_swap_wait_" + group[0], thru, sems, _swap_plan(len(group)), *after)
                self.scatter(group, thru[:len(group)], thru[len(group):])

        def scatter(self, group, pieces, from_sibling):
            sums = [_add_halves(a, r, c_idx, "add_halves_" + n) for n, a, r in zip(group, pieces, from_sibling)]
            landing = [lax.empty((3,) + s.shape[1:], BF16) for s in sums]
            sems, thru, self.token = _split_start("grad_scatter_start_" + group[0], sums + landing,
                                                  _scatter_plan(len(sums)), [3 * len(sums)])
            scattered.append((group, sems[0], thru))

    loss, grad_x, grads = _local_step(x[0], mem[0], positions[0], loss_target[0], params, full, more_weights, GradStore(),
                                      h_in)

    halves = {}
    for group, sems, thru in scattered:
        thru = _split_wait("grad_scatter_wait_" + group[0], thru, sems, _scatter_plan(len(group)), grad_x)
        for i, n in enumerate(group):
            halves[n] = _sum_chips(thru[i], thru[len(group) + i], chip, "sum_chips_" + n)
    order = ("w_cq", "w_co", "w_ckv", "w_out", "w_in", "w_up", "w_down")
    sources = [halves[n] for n in order]
    landing = [lax.empty(s.shape, F32) for s in sources]
    share_sems, thru, token = _split_start("grad_share_start", sources + landing, _share_plan(len(order)), [1] * len(order))
    out_g, out_d, out_m, out_v = {}, {}, {}, {}
    for i, n in enumerate(order):
        own, other = _split_wait("grad_share_wait_" + n, [thru[i], thru[len(order) + i]], share_sems[i], _share_plan(1), token)
        if n == "w_in":
            res_t = _adamw_w_in_transposed(w_in_t, own, other, m_in_t, v_in_t, c_idx)
            out_g[n], out_d[n], out_m[n], out_v[n] = (t.T for t in res_t)
        else:
            out_g[n], out_d[n], out_m[n], out_v[n] = _adamw_halves(weights[n], own, other, mom_m[n], mom_v[n], c_idx, "adamw_" + n)
        token = out_v[n]

    small = [grads[n] for n in VECTORS] + [grads["conv_w"]]
    summed = _unpack(_small_allreduce(_pack(small), "allreduce_vectors"), [t.shape for t in small])
    g_small = dict(zip(VECTORS, summed[:-1]))
    g_small["conv_w"] = lax.dynamic_slice_in_dim(summed[-1], chip * 512, 512, axis=1)
    names = VECTORS + ("conv_w",)
    shapes = [weights[n].shape for n in names]
    packed = [_pack([src[n] for n in names]) for src in (weights, g_small, mom_m, mom_v)]
    small_out = [_unpack(t, shapes) for t in _adamw(*packed, "adamw_small")]
    for i, n in enumerate(names):
        out_g[n] = g_small[n].reshape(shapes[i])
        out_d[n], out_m[n], out_v[n] = small_out[0][i], small_out[1][i], small_out[2][i]

    total_loss = lax.psum(loss[0, 0], ("x", "y", "c"))
    outs = [total_loss, grad_x[None]]
    for group in (out_g, out_d, out_m, out_v):
        outs += [group[n][None] for n in WEIGHTS]
    return tuple(outs)
```

```python
import functools
import math

import jax
import jax.numpy as jnp
from jax import lax
from jax.experimental import pallas as pl
from jax.experimental.pallas import tpu as pltpu

F32 = jnp.float32
BF16 = jnp.bfloat16

SEQ = 2048
D_MODEL = 2048
HEAD = 64
D_ATTN = 1024
D_SSM = 1024
N_GROUPS = 4
N_STATE = 128
CHUNK = 128
ATT_BLK = 128
N_MEM = 256
D_CROSS = 512
D_FF = 8192
D_MAIN = 6144
N_DT = 16
DT_PAD = 512
ROT = 16
ROPE_THETA = 500000.0
EPS = 1e-6
NEG = -1e30
BRANCH_BLOCKS = (16, 4, 1)
DILATIONS = (1, 4, 16)

ADAM_LR, ADAM_B1, ADAM_B2, ADAM_EPS, ADAM_WD, ADAM_STEP = 0.001, 0.9, 0.999, 1e-08, 0.01, 10

VMEM_LIMIT = 56 * 1024 * 1024
MESH = pl.DeviceIdType.MESH


def _params(sem, **kw):
    return pltpu.CompilerParams(dimension_semantics=sem, vmem_limit_bytes=VMEM_LIMIT, **kw)


def _bdot(a, b, dims):
    return lax.dot_general(a.astype(BF16), b.astype(BF16), (dims, ((), ())), preferred_element_type=F32)


def _fdot(a, b, dims):
    return lax.dot_general(a, b, (dims, ((), ())), preferred_element_type=F32, precision=lax.Precision.HIGHEST)


NN = ((1,), (0,))
NT = ((1,), (1,))
TN = ((0,), (0,))


def _tile(n, want):
    t = min(n, want)
    while n % t:
        t //= 2
    return t


def _matmul(a, b, *, mode, name, outs, extra=(), epilogue=None, col_shards=1, after=(), n_cols=None, out_cols=None,
            tm=1024, tn=1024, tk=2048):
    if mode == "nn":
        (m, k), n = a.shape, b.shape[1]
    elif mode == "nt":
        (m, k), n = a.shape, b.shape[0]
    else:
        (k, m), n = a.shape, b.shape[1]
    n = n if n_cols is None else n_cols
    tm, tn, tk = _tile(m, tm), _tile(n // col_shards, tn), _tile(k, tk)
    nk = k // tk
    per_shard = n // col_shards // tn
    dims = {"nn": NN, "nt": NT, "tn": TN}[mode]
    a_spec = pl.BlockSpec((tk, tm), lambda i, j, kk: (kk, i)) if mode == "tn" else pl.BlockSpec((tm, tk), lambda i, j, kk: (i, kk))
    b_spec = pl.BlockSpec((tn, tk), lambda i, j, kk: (j, kk)) if mode == "nt" else pl.BlockSpec((tk, tn), lambda i, j, kk: (kk, j))
    o_spec = pl.BlockSpec((tm, tn), lambda i, j, kk: (i, j))
    n_extra, n_out, n_after = len(extra), len(outs), len(after)

    def body(a_ref, b_ref, *rest):
        extra_refs, out_refs, acc_ref = rest[:n_extra], rest[n_extra + n_after:n_extra + n_after + n_out], rest[-1]
        def finish(acc):
            res = (acc,) if epilogue is None else epilogue(acc, *[e[...] for e in extra_refs])
            for o_ref, r in zip(out_refs, res):
                o_ref[...] = r.astype(o_ref.dtype)

        if nk == 1:
            finish(_bdot(a_ref[...], b_ref[...], dims))
            return
        kk = pl.program_id(2)

        @pl.when(kk == 0)
        def _():
            acc_ref[...] = jnp.zeros_like(acc_ref)

        acc_ref[...] += _bdot(a_ref[...], b_ref[...], dims)

        @pl.when(kk == nk - 1)
        def _():
            finish(acc_ref[...])

    if col_shards == 1:
        out_specs, out_dims = [o_spec] * n_out, (m, n if out_cols is None else out_cols)
    else:
        sharded = pl.BlockSpec((None, tm, tn), lambda i, j, kk: (j // per_shard, i, j % per_shard))
        out_specs, out_dims = [sharded] * n_out, (col_shards, m, n // col_shards)
    res = pl.pallas_call(
        body, name=name, grid=(m // tm, n // tn, nk),
        in_specs=[a_spec, b_spec] + [o_spec] * n_extra + [pl.BlockSpec(memory_space=pl.ANY)] * n_after,
        out_specs=out_specs,
        out_shape=[jax.ShapeDtypeStruct(out_dims, dt) for dt in outs],
        scratch_shapes=[pltpu.VMEM((tm, tn) if nk > 1 else (8, 128), F32)],
        compiler_params=_params(("parallel", "parallel", "arbitrary")),
    )(a, b, *extra, *after)
    return res[0] if n_out == 1 else res


def _row_spec(tr, bw, cb, per_group):
    return pl.BlockSpec((tr, bw), (lambda g, i: (i, cb + g)) if per_group else (lambda g, i: (i, cb)))


def _vec_spec(bw, cb, per_group):
    return pl.BlockSpec((1, bw), (lambda g, i: (0, cb + g)) if per_group else (lambda g, i: (0, cb)))


def _rowwise(fn, rows, vecs, outs, *, name, n_rows=SEQ, tr=256, groups=1, after=()):
    n_r, n_v, n_after = len(rows), len(vecs), len(after)

    def body(*refs):
        vals = [r[...].astype(F32) for r in refs[:n_r + n_v]]
        res = fn(*vals)
        for o_ref, r in zip(refs[n_r + n_v + n_after:], res):
            o_ref[...] = r.astype(o_ref.dtype)

    res = pl.pallas_call(
        body, name=name, grid=(groups, n_rows // tr),
        in_specs=[_row_spec(tr, bw, cb, pg) for _, bw, cb, pg in rows] + [_vec_spec(bw, cb, pg) for _, bw, cb, pg in vecs]
        + [pl.BlockSpec(memory_space=pl.ANY)] * n_after,
        out_specs=[_row_spec(tr, bw, cb, pg) for _, _, bw, cb, pg in outs],
        out_shape=[jax.ShapeDtypeStruct((n_rows, w), dt) for w, dt, _, _, _ in outs],
        compiler_params=_params(("parallel", "parallel")),
    )(*[r[0] for r in rows], *[v[0] for v in vecs], *after)
    return res


def _rowwise_vjp(fn, rows, vecs, cts, row_grads, vec_grads, *, name, n_rows=SEQ, tr=256, groups=1, after=()):
    n_r, n_v, n_after = len(rows), len(vecs), len(after)
    ct_ops = [op for group in cts for op in group]
    ct_sizes = [len(group) for group in cts]
    res_ops = [g[6] for g in row_grads if g[6] is not None]
    n_ct, n_res, n_rg = len(ct_ops), len(res_ops), len(row_grads)

    def body(*refs):
        vals = [r[...].astype(F32) for r in refs[:n_r + n_v]]
        pos = n_r + n_v
        ct_vals = []
        for size in ct_sizes:
            acc = refs[pos][...].astype(F32)
            for t in range(1, size):
                acc = acc + refs[pos + t][...].astype(F32)
            ct_vals.append(acc)
            pos += size
        res_refs = refs[pos:pos + n_res]
        out_refs = refs[pos + n_res + n_after:]
        _, pullback = jax.vjp(fn, *vals)
        grads = pullback(tuple(ct_vals))
        r_i = 0
        for o_ref, g in zip(out_refs[:n_rg], row_grads):
            val = grads[g[0]]
            if g[6] is not None:
                val = val + res_refs[r_i][...].astype(F32)
                r_i += 1
            o_ref[...] = val.astype(o_ref.dtype)
        first = (pl.program_id(1) == 0)
        for o_ref, g in zip(out_refs[n_rg:], vec_grads):
            val = jnp.sum(grads[n_r + g[0]], axis=0, keepdims=True)
            init = first if g[4] else jnp.logical_and(first, pl.program_id(0) == 0)

            @pl.when(init)
            def _(o_ref=o_ref, val=val):
                o_ref[...] = val

            @pl.when(jnp.logical_not(init))
            def _(o_ref=o_ref, val=val):
                o_ref[...] += val

    in_specs = [_row_spec(tr, bw, cb, pg) for _, bw, cb, pg in rows] + [_vec_spec(bw, cb, pg) for _, bw, cb, pg in vecs]
    in_specs += [_row_spec(tr, bw, cb, pg) for _, bw, cb, pg in ct_ops + res_ops] + [pl.BlockSpec(memory_space=pl.ANY)] * n_after
    out_specs =[_row_spec(tr, g[3], g[4], g[5]) for g in row_grads] + [_vec_spec(g[2], g[3], g[4]) for g in vec_grads]
    out_shape = [jax.ShapeDtypeStruct((n_rows, g[1]), g[2]) for g in row_grads]
    out_shape += [jax.ShapeDtypeStruct((1, g[1]), F32) for g in vec_grads]
    return pl.pallas_call(
        body, name=name, grid=(groups, n_rows // tr),
        in_specs=in_specs, out_specs=out_specs, out_shape=out_shape,
        compiler_params=_params(("arbitrary", "arbitrary")),
    )(*[r[0] for r in rows], *[v[0] for v in vecs], *[c[0] for c in ct_ops], *[r[0] for r in res_ops], *after)


def _full(arr, width=None):
    return (arr, arr.shape[1] if width is None else width, 0, False)


def _make_xor(sh):
    def raw(x):
        n = x.shape[-1]
        lane = lax.broadcasted_iota(jnp.int32, x.shape, x.ndim - 1)
        up = pltpu.roll(x, n - sh, x.ndim - 1)
        down = pltpu.roll(x, sh, x.ndim - 1)
        return jnp.where((lane & sh) == 0, up, down)

    f = jax.custom_vjp(raw)
    f.defvjp(lambda x: (raw(x), None), lambda _, ct: (raw(ct),))
    return f


_SWAP_ROPE_HALVES = _make_xor(ROT // 2)


def _head_sum(x):
    n = x.shape[-1]
    same_head = (lax.broadcasted_iota(jnp.int32, (n, n), 0) // HEAD) == (lax.broadcasted_iota(jnp.int32, (n, n), 1) // HEAD)
    return _fdot(x, same_head.astype(F32), NN)


def _rms(x, g):
    return x * lax.rsqrt(jnp.mean(x * x, axis=-1, keepdims=True) + EPS) * g


def _head_rms_rope(x, g, cos, sin, scale):
    y = x * lax.rsqrt(_head_sum(x * x) * (1.0 / HEAD) + EPS) * g
    return (y * cos + _SWAP_ROPE_HALVES(y) * sin) * scale


def _qk_fn(q, k, v, cos, sin, gq, gk):
    return (_head_rms_rope(q, gq, cos, sin, HEAD ** -0.5), _head_rms_rope(k, gk, cos, sin, 1.0), v)


def _norm_fn(x, g):
    return (_rms(x, g),)


def _merge_fn(o0, o1, o2, l0, l1, l2, g):
    m = lax.stop_gradient(jnp.maximum(jnp.maximum(l0, l1), l2))
    e0, e1, e2 = jnp.exp(l0 - m), jnp.exp(l1 - m), jnp.exp(l2 - m)
    mix = (e0 * o0 + e1 * o1 + e2 * o2) / (e0 + e1 + e2)
    return (_rms(mix, g),)


def _gate_fn(y, z, g):
    return (_rms(y * (z * jax.nn.sigmoid(z)), g),)


def _attn_pair(q, kc, vc, kp=None, vp=None, has_prev=None):
    pick0, pick1 = _head_picks()
    k_band, v_band, mask = _attn_band(kc, vc, kp, vp, has_prev)
    s = jnp.where(mask, _bdot(jnp.concatenate([q * pick0, q * pick1], axis=0), k_band, NT), NEG)
    m = jnp.max(s, axis=-1, keepdims=True)
    p = jnp.exp(s - m)
    den = jnp.sum(p, axis=-1, keepdims=True)
    acc = _bdot(p, v_band, NN) * (1.0 / den)
    lse_rows = m + jnp.log(den)
    o = pick0 * acc[:ATT_BLK] + pick1 * acc[ATT_BLK:]
    lse = pick0 * lse_rows[:ATT_BLK] + pick1 * lse_rows[ATT_BLK:]
    return o, lse


def _head_picks():
    lane = lax.broadcasted_iota(jnp.int32, (1, 2 * HEAD), 1)
    return (lane < HEAD).astype(F32), (lane >= HEAD).astype(F32)


def _attn_band(kc, vc, kp, vp, has_prev):
    n_keys = ATT_BLK if kp is None else 2 * ATT_BLK
    qi = lax.broadcasted_iota(jnp.int32, (2 * ATT_BLK, n_keys), 0) & (ATT_BLK - 1)
    kj = lax.broadcasted_iota(jnp.int32, (2 * ATT_BLK, n_keys), 1)
    if kp is None:
        return kc, vc, qi >= kj
    in_prev = jnp.logical_and(jnp.logical_and(kj < ATT_BLK, kj >= qi), has_prev)
    mask = jnp.logical_or(in_prev, jnp.logical_and(kj >= ATT_BLK, qi >= kj - ATT_BLK))
    return jnp.concatenate([kp, kc], axis=0), jnp.concatenate([vp, vc], axis=0), mask


def _attn_config(b):
    r = DILATIONS[b]
    return r, ATT_BLK * r, (512 if r == 1 else 128), BRANCH_BLOCKS[b] > 1


def _for_residues(r, fn):
    if r <= 4:
        for rho in range(r):
            fn(rho)
    else:
        def step(t, carry):
            for u in range(4):
                fn(4 * t + u)
            return carry

        lax.fori_loop(0, r // 4, step, 0)


def _strided_rows(start, r):
    if r > 1:
        return pl.ds(start, ATT_BLK, stride=r)
    return pl.ds(start if isinstance(start, int) else pl.multiple_of(start, ATT_BLK), ATT_BLK)


def _attention_fwd(qn, kn, vn, b):
    r, rows, lanes, with_prev = _attn_config(b)
    cur = pl.BlockSpec((rows, lanes), lambda g, n: (n, g))
    prev = pl.BlockSpec((rows, lanes), lambda g, n: (jnp.maximum(n - 1, 0), g))

    def body(*refs):
        ins, (o_ref, l_ref) = refs[:-2], refs[-2:]
        has_prev = pl.program_id(1) > 0

        def one(rho):
            sub = _strided_rows(rho, r)
            for pair in range(lanes // 128):
                sl = pl.ds(pair * 128, 128)
                args = [ref[sub, sl] for ref in ins] + ([has_prev] if with_prev else [])
                o_ref[sub, sl], l_ref[sub, sl] = _attn_pair(*args)

        _for_residues(r, one)

    operands = (qn, kn, vn, kn, vn) if with_prev else (qn, kn, vn)
    return pl.pallas_call(
        body, name="attn_fwd_%d" % r, grid=(D_ATTN // lanes, SEQ // rows),
        in_specs=[cur, cur, cur] + ([prev, prev] if with_prev else []), out_specs=[cur, cur],
        out_shape=[jax.ShapeDtypeStruct((SEQ, D_ATTN), F32)] * 2,
        compiler_params=_params(("parallel", "parallel")),
    )(*operands)


def _attn_pair_bwd(q, kc, vc, kp, vp, o, lse, do, dl, has_prev):
    pick0, pick1 = _head_picks()
    lane = lax.broadcasted_iota(jnp.int32, (1, 2 * HEAD), 1)
    k_band, v_band, mask = _attn_band(kc, vc, kp, vp, has_prev)
    q2 = jnp.concatenate([q * pick0, q * pick1], axis=0)
    do2 = jnp.concatenate([do * pick0, do * pick1], axis=0)
    lse2 = jnp.concatenate([jnp.sum(lse * (lane == 0).astype(F32), axis=-1, keepdims=True),
                            jnp.sum(lse * (lane == HEAD).astype(F32), axis=-1, keepdims=True)], axis=0)
    base = jnp.sum(jnp.concatenate([dl * pick0, dl * pick1], axis=0) - do2 * jnp.concatenate([o, o], axis=0),
                   axis=-1, keepdims=True)
    p = jnp.exp(jnp.where(mask, _bdot(q2, k_band, NT), NEG) - lse2)
    ds = p * (_bdot(do2, v_band, NT) + base)
    dq2 = _bdot(ds, k_band, NN)
    dq = pick0 * dq2[:ATT_BLK] + pick1 * dq2[ATT_BLK:]
    dk, dv = _bdot(ds, q2, TN), _bdot(p, do2, TN)
    if kp is None:
        return dq, dk, dv
    return dq, dk[ATT_BLK:], dv[ATT_BLK:], dk[:ATT_BLK], dv[:ATT_BLK]


def _attention_bwd(qn, kn, vn, o, lse, do, dl, b):
    r, rows, lanes, with_prev = _attn_config(b)
    cur = pl.BlockSpec((rows, lanes), lambda g, n: (n, g))
    prev = pl.BlockSpec((rows, lanes), lambda g, n: (jnp.maximum(n - 1, 0), g))
    whole = pl.BlockSpec((SEQ, lanes), lambda g, n: (0, g))
    n_in = 5 if with_prev else 3

    def body(*refs):
        ins, (o_ref, l_ref, do_ref, dl_ref, dq_ref, dk_ref, dv_ref) = refs[:n_in], refs[n_in:]
        n = pl.program_id(1)

        @pl.when(n == 0)
        def _():
            dk_ref[...] = jnp.zeros_like(dk_ref)
            dv_ref[...] = jnp.zeros_like(dv_ref)

        def one(rho):
            sub = _strided_rows(rho, r)
            sub_c = _strided_rows(n * rows + rho, r)
            sub_p = _strided_rows(jnp.maximum(n - 1, 0) * rows + rho, r)
            for pair in range(lanes // 128):
                sl = pl.ds(pair * 128, 128)
                vals = [ref[sub, sl] for ref in ins] + ([] if with_prev else [None, None])
                grads = _attn_pair_bwd(*vals, o_ref[sub, sl], l_ref[sub, sl], do_ref[sub, sl], dl_ref[sub, sl], n > 0)
                dq_ref[sub, sl] = grads[0]
                dk_ref[sub_c, sl] += grads[1]
                dv_ref[sub_c, sl] += grads[2]
                if with_prev:
                    dk_ref[sub_p, sl] += grads[3]
                    dv_ref[sub_p, sl] += grads[4]

        _for_residues(r, one)

    operands = (qn, kn, vn, kn, vn) if with_prev else (qn, kn, vn)
    return pl.pallas_call(
        body, name="attn_bwd_%d" % r, grid=(D_ATTN // lanes, SEQ // rows),
        in_specs=[cur, cur, cur] + ([prev, prev] if with_prev else []) + [cur] * 4, out_specs=[cur, whole, whole],
        out_shape=[jax.ShapeDtypeStruct((SEQ, D_ATTN), F32)] * 3,
        compiler_params=_params(("parallel", "arbitrary")),
    )(*operands, o, lse, do, dl)


CONV_COLS = 256
XBC_BLOCK0 = 4096 // CONV_COLS


def _shift_rows(x, s):
    n = x.shape[0]
    t = lax.broadcasted_iota(jnp.int32, x.shape, 0)
    if s >= 0:
        return jnp.where(t >= s, pltpu.roll(x, s, 0), 0.0)
    return jnp.where(t < n + s, pltpu.roll(x, n + s, 0), 0.0)


def _conv_pre(x, w_ref, b_ref):
    pre = b_ref[...] + w_ref[3:4, :] * x
    for k in range(3):
        pre = pre + w_ref[k:k + 1, :] * _shift_rows(x, 3 - k)
    return pre


def _conv_fwd(proj, conv_w, conv_b):
    cols = conv_w.shape[1]

    def body(x_ref, w_ref, b_ref, o_ref):
        pre = _conv_pre(x_ref[...], w_ref, b_ref)
        o_ref[...] = pre * jax.nn.sigmoid(pre)

    blk = pl.BlockSpec((SEQ, CONV_COLS), lambda j: (0, j))
    return pl.pallas_call(
        body, name="conv_fwd", grid=(cols // CONV_COLS,),
        in_specs=[pl.BlockSpec((SEQ, CONV_COLS), lambda j: (0, XBC_BLOCK0 + j)),
                  pl.BlockSpec((4, CONV_COLS), lambda j: (0, j)), pl.BlockSpec((1, CONV_COLS), lambda j: (0, j))],
        out_specs=blk, out_shape=jax.ShapeDtypeStruct((SEQ, cols), F32),
        compiler_params=_params(("parallel",)),
    )(proj, conv_w, conv_b)


def _conv_bwd(proj, conv_w, conv_b, dy):
    cols = conv_w.shape[1]

    def body(x_ref, w_ref, b_ref, dy_ref, dx_ref, dw_ref, db_ref):
        x = x_ref[...]
        pre = _conv_pre(x, w_ref, b_ref)
        sg = jax.nn.sigmoid(pre)
        dpre = dy_ref[...] * (sg * (1.0 + pre * (1.0 - sg)))
        db_ref[...] = jnp.sum(dpre, axis=0, keepdims=True)
        dx = w_ref[3:4, :] * dpre
        dw_ref[3:4, :] = jnp.sum(dpre * x, axis=0, keepdims=True)
        for k in range(3):
            dx = dx + w_ref[k:k + 1, :] * _shift_rows(dpre, k - 3)
            dw_ref[k:k + 1, :] = jnp.sum(dpre * _shift_rows(x, 3 - k), axis=0, keepdims=True)
        dw_ref[4:8, :] = jnp.zeros((4, CONV_COLS), F32)
        dx_ref[...] = dx.astype(dx_ref.dtype)

    blk = pl.BlockSpec((SEQ, CONV_COLS), lambda j: (0, j))
    return pl.pallas_call(
        body, name="conv_bwd", grid=(cols // CONV_COLS,),
        in_specs=[pl.BlockSpec((SEQ, CONV_COLS), lambda j: (0, XBC_BLOCK0 + j)),
                  pl.BlockSpec((4, CONV_COLS), lambda j: (0, j)), pl.BlockSpec((1, CONV_COLS), lambda j: (0, j)), blk],
        out_specs=[blk, pl.BlockSpec((8, CONV_COLS), lambda j: (0, j)), pl.BlockSpec((1, CONV_COLS), lambda j: (0, j))],
        out_shape=[jax.ShapeDtypeStruct((SEQ, cols), BF16), jax.ShapeDtypeStruct((8, cols), F32),
                   jax.ShapeDtypeStruct((1, cols), F32)],
        compiler_params=_params(("parallel",)),
    )(proj, conv_w, conv_b, dy)


HEADS_PER_GROUP = 4


def _ssd_chunk(x0, x1, x2, x3, bm, cm, dtr, bias, alog, dsk, h0, h1, h2, h3):
    xs, hs = (x0, x1, x2, x3), (h0, h1, h2, h3)
    row = lax.broadcasted_iota(jnp.int32, (CHUNK, CHUNK), 0)
    col = lax.broadcasted_iota(jnp.int32, (CHUNK, CHUNK), 1)
    causal = row >= col
    tril = causal.astype(F32)
    z = dtr + bias
    dt = jnp.maximum(z, 0.0) + jnp.log(1.0 + jnp.exp(-jnp.abs(z)))
    a = -jnp.exp(alog)
    acs = _fdot(tril, dt * a, NN)
    acs_t, dt_t = acs.T, dt.T
    cb = _bdot(cm, bm, NT)
    lane = lax.broadcasted_iota(jnp.int32, (1, CHUNK), 1)
    sub = lax.broadcasted_iota(jnp.int32, (CHUNK, 1), 0)
    ys, hn = [], []
    for j in range(HEADS_PER_GROUP):
        on_lane, on_sub = (lane == j).astype(F32), (sub == j).astype(F32)
        acs_c = jnp.sum(acs * on_lane, axis=1, keepdims=True)
        dt_c = jnp.sum(dt * on_lane, axis=1, keepdims=True)
        acs_r = jnp.sum(acs_t * on_sub, axis=0, keepdims=True)
        dt_r = jnp.sum(dt_t * on_sub, axis=0, keepdims=True)
        acs_last = jnp.sum(acs_c * (sub == CHUNK - 1).astype(F32), axis=0, keepdims=True)
        d_j = jnp.sum(dsk * on_lane, axis=1, keepdims=True)
        decay = jnp.exp(jnp.where(causal, acs_c - acs_r, NEG))
        w = cb * decay * dt_r
        y_diag = _bdot(w, xs[j], NN)
        y_off = _bdot(cm, hs[j], NT) * jnp.exp(acs_c)
        ys.append(y_diag + y_off + d_j * xs[j])
        state = _bdot(xs[j] * (jnp.exp(acs_last - acs_c) * dt_c), bm, TN)
        hn.append(hs[j] * jnp.exp(acs_last) + state)
    return (*ys, *hn)


def _ssd_specs(reverse):
    n_chunks = SEQ // CHUNK
    c_of = (lambda c: n_chunks - 1 - c) if reverse else (lambda c: c)
    x_spec = pl.BlockSpec((CHUNK, 256), lambda g, c: (c_of(c), g))
    b_spec = pl.BlockSpec((CHUNK, N_STATE), lambda g, c: (c_of(c), 8 + g))
    c_spec = pl.BlockSpec((CHUNK, N_STATE), lambda g, c: (c_of(c), 12 + g))
    dt_spec = pl.BlockSpec((CHUNK, 128), lambda g, c: (c_of(c), g))
    vec_spec = pl.BlockSpec((1, 128), lambda g, c: (0, g))
    h_spec = pl.BlockSpec((1, 1, HEADS_PER_GROUP, HEAD, N_STATE), lambda g, c: (c_of(c), g, 0, 0, 0))
    return x_spec, b_spec, c_spec, dt_spec, vec_spec, h_spec


def _ssd_fwd(xbc, dt_raw, bias, alog, dsk):
    x_spec, b_spec, c_spec, dt_spec, vec_spec, h_spec = _ssd_specs(False)

    def body(x_ref, b_ref, c_ref, dt_ref, bias_ref, alog_ref, dsk_ref, y_ref, hin_ref, h_scr):
        @pl.when(pl.program_id(1) == 0)
        def _():
            h_scr[...] = jnp.zeros_like(h_scr)

        hs = [h_scr[j] for j in range(HEADS_PER_GROUP)]
        for j in range(HEADS_PER_GROUP):
            hin_ref[0, 0, j] = hs[j]
        xs = [x_ref[:, pl.ds(j * HEAD, HEAD)] for j in range(HEADS_PER_GROUP)]
        res = _ssd_chunk(*xs, b_ref[...], c_ref[...], dt_ref[...], bias_ref[...], alog_ref[...], dsk_ref[...], *hs)
        for j in range(HEADS_PER_GROUP):
            y_ref[:, pl.ds(j * HEAD, HEAD)] = res[j]
            h_scr[j] = res[HEADS_PER_GROUP + j]

    return pl.pallas_call(
        body, name="ssd_fwd", grid=(N_GROUPS, SEQ // CHUNK),
        in_specs=[x_spec, b_spec, c_spec, dt_spec, vec_spec, vec_spec, vec_spec],
        out_specs=[x_spec, h_spec],
        out_shape=[jax.ShapeDtypeStruct((SEQ, D_SSM), F32),
                   jax.ShapeDtypeStruct((SEQ // CHUNK, N_GROUPS, HEADS_PER_GROUP, HEAD, N_STATE), F32)],
        scratch_shapes=[pltpu.VMEM((HEADS_PER_GROUP, HEAD, N_STATE), F32)],
        compiler_params=_params(("parallel", "arbitrary")),
    )(xbc, xbc, xbc, dt_raw, bias, alog, dsk)


def _ssd_bwd(xbc, dt_raw, bias, alog, dsk, h_in, dy):
    x_spec, b_spec, c_spec, dt_spec, vec_spec, h_spec = _ssd_specs(True)
    dxbc_x = pl.BlockSpec((CHUNK, 256), x_spec.index_map)

    def body(x_ref, b_ref, c_ref, dt_ref, bias_ref, alog_ref, dsk_ref, hin_ref, dy_ref,
             dx_ref, db_ref, dc_ref, ddt_ref, dbias_ref, dalog_ref, ddsk_ref, dh_scr):
        first = pl.program_id(1) == 0

        @pl.when(first)
        def _():
            dh_scr[...] = jnp.zeros_like(dh_scr)

        xs = [x_ref[:, pl.ds(j * HEAD, HEAD)] for j in range(HEADS_PER_GROUP)]
        hs = [hin_ref[0, 0, j] for j in range(HEADS_PER_GROUP)]
        cts = [dy_ref[:, pl.ds(j * HEAD, HEAD)] for j in range(HEADS_PER_GROUP)] + [dh_scr[j] for j in range(HEADS_PER_GROUP)]
        _, pullback = jax.vjp(_ssd_chunk, *xs, b_ref[...], c_ref[...], dt_ref[...], bias_ref[...], alog_ref[...],
                              dsk_ref[...], *hs)
        g = pullback(tuple(cts))
        for j in range(HEADS_PER_GROUP):
            dx_ref[:, pl.ds(j * HEAD, HEAD)] = g[j]
            dh_scr[j] = g[10 + j]
        db_ref[...] = g[4]
        dc_ref[...] = g[5]
        ddt_ref[...] = g[6].astype(ddt_ref.dtype)
        for o_ref, val in ((dbias_ref, g[7]), (dalog_ref, g[8]), (ddsk_ref, g[9])):
            @pl.when(first)
            def _(o_ref=o_ref, val=val):
                o_ref[...] = val

            @pl.when(jnp.logical_not(first))
            def _(o_ref=o_ref, val=val):
                o_ref[...] += val

    n_chunks = SEQ // CHUNK
    out_b = pl.BlockSpec((CHUNK, N_STATE), lambda g, c: (n_chunks - 1 - c, g))
    res = pl.pallas_call(
        body, name="ssd_bwd", grid=(N_GROUPS, n_chunks),
        in_specs=[x_spec, b_spec, c_spec, dt_spec, vec_spec, vec_spec, vec_spec, h_spec, x_spec],
        out_specs=[dxbc_x, out_b, out_b, dt_spec, vec_spec, vec_spec, vec_spec],
        out_shape=[jax.ShapeDtypeStruct((SEQ, D_SSM), F32), jax.ShapeDtypeStruct((SEQ, N_GROUPS * N_STATE), F32),
                   jax.ShapeDtypeStruct((SEQ, N_GROUPS * N_STATE), F32), jax.ShapeDtypeStruct((SEQ, DT_PAD), BF16),
                   jax.ShapeDtypeStruct((1, DT_PAD), F32), jax.ShapeDtypeStruct((1, DT_PAD), F32),
                   jax.ShapeDtypeStruct((1, DT_PAD), F32)],
        scratch_shapes=[pltpu.VMEM((HEADS_PER_GROUP, HEAD, N_STATE), F32)],
        compiler_params=_params(("parallel", "arbitrary")),
    )(xbc, xbc, xbc, dt_raw, bias, alog, dsk, h_in, dy)
    return res


CROSS_HEAD = 128
CROSS_ROWS = 512


def _cross_head(q, k, v, gq, gk):
    qn = _rms(q, gq) * (CROSS_HEAD ** -0.5)
    kn = _rms(k, gk)
    s = _bdot(qn, kn, NT)
    p = jnp.exp(s - lax.stop_gradient(jnp.max(s, axis=-1, keepdims=True)))
    return _bdot(p, v, NN) * (1.0 / jnp.sum(p, axis=-1, keepdims=True))


def _cross_specs():
    q_spec = pl.BlockSpec((CROSS_ROWS, CROSS_HEAD), lambda h, i: (i, h))
    k_spec = pl.BlockSpec((N_MEM, CROSS_HEAD), lambda h, i: (0, h))
    v_spec = pl.BlockSpec((N_MEM, CROSS_HEAD), lambda h, i: (0, 4 + h))
    g_spec = pl.BlockSpec((1, CROSS_HEAD), lambda h, i: (0, 0))
    return q_spec, k_spec, v_spec, g_spec


def _cross_fwd(qc, kv, gq, gk):
    q_spec, k_spec, v_spec, g_spec = _cross_specs()

    def body(q_ref, k_ref, v_ref, gq_ref, gk_ref, o_ref):
        o_ref[...] = _cross_head(q_ref[...], k_ref[...], v_ref[...], gq_ref[...], gk_ref[...]).astype(o_ref.dtype)

    return pl.pallas_call(
        body, name="cross_fwd", grid=(4, SEQ // CROSS_ROWS),
        in_specs=[q_spec, k_spec, v_spec, g_spec, g_spec], out_specs=q_spec,
        out_shape=jax.ShapeDtypeStruct((SEQ, D_CROSS), BF16),
        compiler_params=_params(("parallel", "parallel")),
    )(qc, kv, kv, gq, gk)


def _cross_bwd(qc, kv, gq, gk, do):
    q_spec, k_spec, v_spec, g_spec = _cross_specs()

    def body(q_ref, k_ref, v_ref, gq_ref, gk_ref, do_ref, dq_ref, dk_ref, dv_ref, dgq_ref, dgk_ref):
        _, pullback = jax.vjp(_cross_head, q_ref[...], k_ref[...], v_ref[...], gq_ref[...], gk_ref[...])
        dq, dk, dv, dgq, dgk = pullback(do_ref[...].astype(F32))
        dq_ref[...] = dq.astype(dq_ref.dtype)
        row0 = pl.program_id(1) == 0
        all0 = jnp.logical_and(row0, pl.program_id(0) == 0)
        for o_ref, val, init in ((dk_ref, dk, row0), (dv_ref, dv, row0), (dgq_ref, dgq, all0), (dgk_ref, dgk, all0)):
            @pl.when(init)
            def _(o_ref=o_ref, val=val):
                o_ref[...] = val

            @pl.when(jnp.logical_not(init))
            def _(o_ref=o_ref, val=val):
                o_ref[...] += val

    return pl.pallas_call(
        body, name="cross_bwd", grid=(4, SEQ // CROSS_ROWS),
        in_specs=[q_spec, k_spec, v_spec, g_spec, g_spec, q_spec],
        out_specs=[q_spec, k_spec, k_spec, g_spec, g_spec],
        out_shape=[jax.ShapeDtypeStruct((SEQ, D_CROSS), BF16), jax.ShapeDtypeStruct((N_MEM, D_CROSS), F32),
                   jax.ShapeDtypeStruct((N_MEM, D_CROSS), F32), jax.ShapeDtypeStruct((1, CROSS_HEAD), F32),
                   jax.ShapeDtypeStruct((1, CROSS_HEAD), F32)],
        compiler_params=_params(("arbitrary", "arbitrary")),
    )(qc, kv, kv, gq, gk, do)


def _loss_head(y, target):
    tr = 256

    def body(y_ref, t_ref, dy_ref, dyb_ref, loss_ref):
        err = y_ref[...] - t_ref[...]
        dy = err * (1.0 / D_MODEL)
        dy_ref[...] = dy
        dyb_ref[...] = dy.astype(BF16)
        part = jnp.sum(jnp.sum(err * err, axis=1, keepdims=True), axis=0, keepdims=True) * (0.5 / D_MODEL)
        part = jnp.broadcast_to(part, (1, 128))

        @pl.when(pl.program_id(0) == 0)
        def _():
            loss_ref[...] = part

        @pl.when(pl.program_id(0) != 0)
        def _():
            loss_ref[...] += part

    blk = pl.BlockSpec((tr, D_MODEL), lambda i: (i, 0))
    return pl.pallas_call(
        body, name="loss_head", grid=(SEQ // tr,),
        in_specs=[blk, blk], out_specs=[blk, blk, pl.BlockSpec((1, 128), lambda i: (0, 0))],
        out_shape=[jax.ShapeDtypeStruct((SEQ, D_MODEL), F32), jax.ShapeDtypeStruct((SEQ, D_MODEL), BF16),
                   jax.ShapeDtypeStruct((1, 128), F32)],
        compiler_params=_params(("arbitrary",)),
    )(y, target)


def _pad_heads(v):
    return jnp.pad(v.reshape(N_GROUPS, HEADS_PER_GROUP), ((0, 0), (0, 128 - HEADS_PER_GROUP))).reshape(1, DT_PAD)


def _unpad_heads(v):
    return v.reshape(v.shape[0], N_GROUPS, 128)[:, :, :HEADS_PER_GROUP].reshape(v.shape[0], N_DT)


def _rope_tables(positions):
    half = ROT // 2
    inv_freq = ROPE_THETA ** (-2.0 * jnp.arange(half, dtype=F32) / ROT)
    ang = positions.reshape(SEQ, 1).astype(F32) * inv_freq
    cos, sin = jnp.cos(ang), jnp.sin(ang)
    ones, zeros = jnp.ones((SEQ, HEAD - ROT), F32), jnp.zeros((SEQ, HEAD - ROT), F32)
    cos_h = jnp.concatenate([cos, cos, ones], axis=1)
    sin_h = jnp.concatenate([-sin, sin, zeros], axis=1)
    return jnp.tile(cos_h, (1, 2)), jnp.tile(sin_h, (1, 2))


def _add_res(acc, res):
    return (acc + res,)


def _settle(grads, *after):
    if hasattr(grads, "settle"):
        grads.settle(*after)


def _take_token(grads):
    token = getattr(grads, "token", None)
    if token is None:
        return ()
    grads.token = None
    return (token,)


def _local_step(x, mem, positions, target, p, w, more_weights=None, grads=None, h=None):
    grads = {} if grads is None else grads
    w = dict(w)
    cos, sin = _rope_tables(positions)
    gq2, gk2 = jnp.tile(p["g_q"], (1, 2)), jnp.tile(p["g_k"], (1, 2))
    bias, alog, dsk = _pad_heads(p["dt_bias"]), _pad_heads(p["a_log"]), _pad_heads(p["d_skip"])
    norm_out = [(D_MODEL, BF16, D_MODEL, 0, False)]

    if h is None:
        h = _rowwise(_norm_fn, [_full(x)], [_full(p["g_mix"])], norm_out, name="norm_in")[0]
    proj = _matmul(h, w["w_in"], mode="nn", name="in_proj", outs=[F32], n_cols=D_MAIN)
    dt_raw = _matmul(h, w["w_dt"], mode="nn", name="dt_proj", outs=[F32])
    qk_rows = [(proj, 128, 0, True), (proj, 128, 8, True), (proj, 128, 16, True), _full(cos), _full(sin)]
    qk_vecs = [_full(gq2), _full(gk2)]
    qn, kn, vn = _rowwise(_qk_fn, qk_rows, qk_vecs, [(D_ATTN, F32, 128, 0, True)] * 3, name="qk_prep", groups=8, tr=1024)
    branches = [_attention_fwd(qn, kn, vn, b) for b in range(3)]
    merge_rows = [_full(o) for o, _ in branches] + [_full(lse) for _, lse in branches]
    attn = _rowwise(_merge_fn, merge_rows, [_full(p["g_attn_out"])], [(D_ATTN, BF16, D_ATTN, 0, False)], name="attn_merge")[0]
    xbc = _conv_fwd(proj, p["conv_w"], p["conv_b"])
    y_ssd, h_in = _ssd_fwd(xbc, dt_raw, bias, alog, dsk)
    gate_rows = [(y_ssd, 256, 0, True), (proj, 256, 12, True)]
    gate_vecs = [(p["g_ssm_out"], 256, 0, True)]
    ssm = _rowwise(_gate_fn, gate_rows, gate_vecs, [(D_SSM, BF16, 256, 0, True)], name="ssm_gate", groups=4)[0]
    mix = jnp.concatenate([attn, ssm], axis=1)
    if more_weights is not None:
        w.update(more_weights("mixer_done", mix))
    x1 = _matmul(mix, w["w_out"], mode="nn", name="out_proj", outs=[F32], extra=(x,), epilogue=_add_res)
    hc = _rowwise(_norm_fn, [_full(x1)], [_full(p["g_cross"])], norm_out, name="norm_cross")[0]
    memh = _rowwise(_norm_fn, [_full(mem)], [_full(p["g_mem"])], norm_out, name="norm_mem", n_rows=N_MEM)[0]
    qc = _matmul(hc, w["w_cq"], mode="nn", name="cq_proj", outs=[F32])
    kv = _matmul(memh, w["w_ckv"], mode="nn", name="ckv_proj", outs=[F32])
    oc = _cross_fwd(qc, kv, p["g_cq"], p["g_ck"])
    x2 = _matmul(oc, w["w_co"], mode="nn", name="co_proj", outs=[F32], extra=(x1,), epilogue=_add_res)
    hm = _rowwise(_norm_fn, [_full(x2)], [_full(p["g_mlp"])], norm_out, name="norm_mlp")[0]
    if more_weights is not None:
        w.update(more_weights("cross_done", hm))
    u, act = _matmul(hm, w["w_up"], mode="nn", name="up_proj", outs=[F32, BF16],
                     epilogue=lambda acc: (acc, jnp.square(jnp.maximum(acc, 0.0))))
    x3 = _matmul(act, w["w_down"], mode="nn", name="down_proj", outs=[F32], extra=(x2,), epilogue=_add_res)
    dy, dyb, loss = _loss_head(x3, target)

    grads["w_down"] = _matmul(act, dyb, mode="tn", name="dw_down", outs=[BF16], after=_take_token(grads))
    du = _matmul(dyb, w["w_down"], mode="nt", name="d_act", outs=[BF16], extra=(u,), after=_take_token(grads),
                 epilogue=lambda acc, uu: (acc * (2.0 * jnp.maximum(uu, 0.0)),))
    _settle(grads, du)
    grads["w_up"] = _matmul(hm, du, mode="tn", name="dw_up", outs=[BF16], col_shards=4, after=_take_token(grads))
    dhm = _matmul(du, w["w_up"], mode="nt", name="d_hm", outs=[F32], after=_take_token(grads))
    _settle(grads, dhm)
    dx2, grads["g_mlp"] = _rowwise_vjp(
        _norm_fn, [_full(x2)], [_full(p["g_mlp"])], [[_full(dhm)]],
        [(0, D_MODEL, F32, D_MODEL, 0, False, _full(dy))], [(0, D_MODEL, D_MODEL, 0, False)], name="norm_mlp_bwd")
    grads["w_co"] = _matmul(oc, dx2, mode="tn", name="dw_co", outs=[BF16], col_shards=4, after=_take_token(grads))
    doc = _matmul(dx2, w["w_co"], mode="nt", name="d_oc", outs=[BF16])
    dqc, dkc, dvc, grads["g_cq"], grads["g_ck"] = _cross_bwd(qc, kv, p["g_cq"], p["g_ck"], doc)
    grads["w_cq"] = _matmul(hc, dqc, mode="tn", name="dw_cq", outs=[BF16])
    dhc = _matmul(dqc, w["w_cq"], mode="nt", name="d_hc", outs=[F32])
    dkv = jnp.concatenate([dkc, dvc], axis=1)
    grads["w_ckv"] = _matmul(memh, dkv, mode="tn", name="dw_ckv", outs=[BF16])
    dmemh = _matmul(dkv, w["w_ckv"], mode="nt", name="d_memh", outs=[F32])
    grads["g_mem"] = _rowwise_vjp(_norm_fn, [_full(mem)], [_full(p["g_mem"])], [[_full(dmemh)]], [],
                                  [(0, D_MODEL, D_MODEL, 0, False)], name="norm_mem_bwd", n_rows=N_MEM)[0]
    dx1, grads["g_cross"] = _rowwise_vjp(
        _norm_fn, [_full(x1)], [_full(p["g_cross"])], [[_full(dhc)]],
        [(0, D_MODEL, F32, D_MODEL, 0, False, _full(dx2))], [(0, D_MODEL, D_MODEL, 0, False)], name="norm_cross_bwd")
    grads["w_out"] = _matmul(mix, dx1, mode="tn", name="dw_out", outs=[BF16])
    dmix = _matmul(dx1, w["w_out"], mode="nt", name="d_mix", outs=[F32], after=_take_token(grads))
    _settle(grads, dmix)
    merge_grads = [(i, D_ATTN, F32, D_ATTN, 0, False, None) for i in range(6)]
    *dol, grads["g_attn_out"] = _rowwise_vjp(
        _merge_fn, merge_rows, [_full(p["g_attn_out"])], [[(dmix, D_ATTN, 0, False)]],
        merge_grads, [(0, D_ATTN, D_ATTN, 0, False)], name="attn_merge_bwd", after=_take_token(grads))
    dqkv = [_attention_bwd(qn, kn, vn, *branches[b], dol[b], dol[3 + b], b) for b in range(3)]
    qk_cts = [[(dqkv[b][i], 128, 0, True) for b in range(3)] for i in range(3)]
    dq, dk, dv, dgq2, dgk2 = _rowwise_vjp(
        _qk_fn, qk_rows, qk_vecs, qk_cts, [(i, D_ATTN, BF16, 128, 0, True, None) for i in range(3)],
        [(0, 128, 128, 0, False), (1, 128, 128, 0, False)], name="qk_prep_bwd", groups=8, tr=512)
    grads["g_q"] = dgq2[:, :HEAD] + dgq2[:, HEAD:]
    grads["g_k"] = dgk2[:, :HEAD] + dgk2[:, HEAD:]
    dy_ssd, dz, grads["g_ssm_out"] = _rowwise_vjp(
        _gate_fn, gate_rows, gate_vecs, [[(dmix, 256, 4, True)]],
        [(0, D_SSM, F32, 256, 0, True, None), (1, D_SSM, BF16, 256, 0, True, None)],
        [(0, D_SSM, 256, 0, True)], name="ssm_gate_bwd", groups=4)
    dxs, db, dc, ddt, dbias, dalog, ddsk = _ssd_bwd(xbc, dt_raw, bias, alog, dsk, h_in, dy_ssd)
    grads["dt_bias"], grads["a_log"], grads["d_skip"] = _unpad_heads(dbias), _unpad_heads(dalog), _unpad_heads(ddsk)
    dxbc_raw, dconv_w, grads["conv_b"] = _conv_bwd(proj, p["conv_w"], p["conv_b"], jnp.concatenate([dxs, db, dc], axis=1))
    grads["conv_w"] = dconv_w[:4]
    dproj = jnp.concatenate([dq, dk, dv, dz, dxbc_raw], axis=1)
    grads["w_main"] = _matmul(h, dproj, mode="tn", name="dw_main", outs=[BF16], out_cols=D_MAIN + N_DT)
    grads["w_dt"] = _matmul(h, ddt, mode="tn", name="dw_dt", outs=[BF16])
    dh = _matmul(dproj, w["w_in"], mode="nt", name="d_h_main", outs=[F32], after=_take_token(grads))
    dh = _matmul(ddt, w["w_dt"], mode="nt", name="d_h_dt", outs=[F32], extra=(dh,), epilogue=_add_res)
    grad_x, grads["g_mix"] = _rowwise_vjp(
        _norm_fn, [_full(x)], [_full(p["g_mix"])], [[_full(dh)]],
        [(0, D_MODEL, F32, D_MODEL, 0, False, _full(dx1))], [(0, D_MODEL, D_MODEL, 0, False)], name="norm_in_bwd")
    return loss, grad_x, grads


MATRICES = ("w_in", "w_out", "w_cq", "w_ckv", "w_co", "w_up", "w_down")
ROW_SHARDED = ("w_out", "w_cq", "w_ckv", "w_down")
N_CHIPS = 4
ANY = pl.BlockSpec(memory_space=pl.ANY)


def _place():
    return lax.axis_index("x"), lax.axis_index("y"), lax.axis_index("c")


def _other_chips(x, y):
    return [(1 - x, y), (x, 1 - y), (1 - x, 1 - y)]


def _remote(src, dst, send_sem, recv_sem, device):
    return pltpu.make_async_remote_copy(src_ref=src, dst_ref=dst, send_sem=send_sem, recv_sem=recv_sem,
                                        device_id=device, device_id_type=MESH)


def _gathered_shape(name, shard):
    rows, cols = shard.shape
    if name == "w_in":
        return (N_CHIPS, rows, cols)
    return (N_CHIPS * rows, cols) if name in ROW_SHARDED else (rows, N_CHIPS * cols)


def _shard_window(name, ref, rows, cols, chip, half):
    r0, nr = (0, rows) if half is None else (half * (rows // 2), rows // 2)
    if name == "w_in":
        return ref.at[chip, pl.ds(r0, nr), :]
    if name in ROW_SHARDED:
        return ref.at[pl.ds(chip * rows + r0, nr), :]
    return ref.at[pl.ds(r0, nr), pl.ds(pl.multiple_of(chip * cols, 128), cols)]


def _cast_into_gathered(w, name, chip, after=()):
    rows, cols = w.shape
    tr = _tile(rows, ROW_TILE)

    def body(chip_ref, w_ref, *rest):
        rest[-1][...] = w_ref[...].astype(BF16)

    if name == "w_in":
        out_spec = pl.BlockSpec((None, tr, cols), lambda i, chip_ref: (chip_ref[0], i, 0))
    elif name in ROW_SHARDED:
        out_spec = pl.BlockSpec((tr, cols), lambda i, chip_ref: (chip_ref[0] * (rows // tr) + i, 0))
    else:
        out_spec = pl.BlockSpec((tr, cols), lambda i, chip_ref: (i, chip_ref[0]))
    grid_spec = pltpu.PrefetchScalarGridSpec(
        num_scalar_prefetch=1, grid=(rows // tr,),
        in_specs=[pl.BlockSpec((tr, cols), lambda i, chip_ref: (i, 0))] + [pl.BlockSpec(memory_space=pl.ANY)] * len(after),
        out_specs=out_spec)
    return pl.pallas_call(body, name="cast_" + name, grid_spec=grid_spec,
                          out_shape=jax.ShapeDtypeStruct(_gathered_shape(name, w), BF16),
                          compiler_params=_params(("parallel",)))(chip.reshape(1).astype(jnp.int32), w, *after)


def _w_in_columns(arr, to_shards):
    rows, piece = D_MODEL, (D_MAIN + N_DT) // N_CHIPS
    tr = ROW_TILE

    def body(a_ref, o_ref):
        for j in range(N_CHIPS):
            if to_shards:
                o_ref[j] = a_ref[:, pl.ds(piece * j, piece)]
            else:
                o_ref[:, pl.ds(piece * j, piece)] = a_ref[j]

    pieces = pl.BlockSpec((N_CHIPS, tr, piece), lambda i: (0, i, 0))
    matrix = pl.BlockSpec((tr, N_CHIPS * piece), lambda i: (i, 0))
    out_dims = (N_CHIPS, rows, piece) if to_shards else (rows, N_CHIPS * piece)
    return pl.pallas_call(
        body, name="w_in_to_shards" if to_shards else "w_in_from_shards", grid=(rows // tr,),
        in_specs=[matrix if to_shards else pieces], out_specs=pieces if to_shards else matrix,
        out_shape=jax.ShapeDtypeStruct(out_dims, arr.dtype), compiler_params=_params(("parallel",)))(arr)


HBM = pl.BlockSpec(memory_space=pltpu.HBM)
SEM = pl.BlockSpec(memory_space=pltpu.SEMAPHORE)
EFFECT = pltpu.SideEffectType.DATAFLOW_SIDE_EFFECTING


def _split_start(name, bufs, plan, counts, after=()):
    n, n_g, n_after = len(bufs), len(counts), len(after)

    def body(*refs):
        ins, sems, token = refs[:n], refs[n + n_after:n + n_after + 2 * n_g], refs[-1]
        for g, copies in enumerate(plan(ins)):
            for i, (src, dst, device, _) in enumerate(copies):
                _remote(src, dst, sems[2 * g].at[i], sems[2 * g + 1].at[i], device).start()
        token[...] = jnp.zeros_like(token)

    sem_shapes = [pltpu.SemaphoreType.DMA((cnt,)) for cnt in counts for _ in range(2)]
    res = pl.pallas_call(
        body, name=name,
        out_shape=(*sem_shapes, *[pltpu.HBM(b.shape, b.dtype) for b in bufs], jax.ShapeDtypeStruct((8, 128), F32)),
        in_specs=(*(HBM,) * n, *(ANY,) * n_after),
        out_specs=(*(SEM,) * (2 * n_g), *(HBM,) * n, pl.BlockSpec(memory_space=pltpu.VMEM)),
        input_output_aliases={i: 2 * n_g + i for i in range(n)},
        compiler_params=pltpu.CompilerParams(has_side_effects=EFFECT),
    )(*[pltpu.with_memory_space_constraint(b, pltpu.HBM) for b in bufs], *after)
    sems = [(res[2 * g], res[2 * g + 1]) for g in range(n_g)]
    return sems, list(res[2 * n_g:2 * n_g + n]), res[-1]


def _split_wait(name, bufs, sems, plan, *after):
    n = len(bufs)

    def body(*refs):
        ins, send, recv = refs[:n], refs[n], refs[n + 1]
        (copies,) = plan(ins)
        for i, (src, _, device, landing) in enumerate(copies):
            cp = _remote(src, landing, send.at[i], recv.at[i], device)
            cp.wait_send()
            cp.wait_recv()

    res = pl.pallas_call(
        body, name=name, out_shape=tuple(pltpu.HBM(b.shape, b.dtype) for b in bufs),
        in_specs=(*(HBM,) * n, SEM, SEM, *(ANY,) * len(after)), out_specs=(HBM,) * n,
        input_output_aliases={i: i for i in range(n)},
        compiler_params=pltpu.CompilerParams(has_side_effects=EFFECT),
    )(*bufs, sems[0], sems[1], *after)
    return list(res)


def _ici_plan(names, shard_shapes):
    def plan(refs):
        x, y, c = _place()
        copies = []
        for ref, name in zip(refs, names):
            win = _shard_window(name, ref, *shard_shapes[name], 2 * x + y, c)
            for px, py in _other_chips(x, y):
                copies.append((win, win, (px, py, c), _shard_window(name, ref, *shard_shapes[name], 2 * px + py, c)))
        return [copies]
    return plan


def _pass_on_plan(names, shard_shapes):
    def plan(refs):
        x, y, c = _place()
        copies = []
        for ref, name in zip(refs, names):
            for px, py in _other_chips(x, y):
                win = _shard_window(name, ref, *shard_shapes[name], 2 * px + py, c)
                copies.append((win, win, (x, y, 1 - c), _shard_window(name, ref, *shard_shapes[name], 2 * px + py, 1 - c)))
        return [copies]
    return plan


def _swap_plan(n_pairs):
    def plan(refs):
        x, y, c = _place()
        return [[(src.at[:, 1 - c], dst, (x, y, 1 - c), dst) for src, dst in zip(refs[:n_pairs], refs[n_pairs:])]]
    return plan


def _share_plan(n_pairs):
    def plan(refs):
        x, y, c = _place()
        return [[(src, dst, (x, y, 1 - c), dst)] for src, dst in zip(refs[:n_pairs], refs[n_pairs:])]
    return plan


def _scatter_plan(n_pairs):
    def plan(refs):
        x, y, c = _place()
        copies = []
        for src, dst in zip(refs[:n_pairs], refs[n_pairs:]):
            for k, (px, py) in enumerate(_other_chips(x, y)):
                copies.append((src.at[2 * px + py], dst.at[k], (px, py, c), dst.at[k]))
        return [copies]
    return plan


def _sibling_swap(arrs, name):
    n = len(arrs)

    def body(*refs):
        ins, outs, send, recv = refs[:n], refs[n:2 * n], refs[2 * n], refs[2 * n + 1]
        x, y, c = _place()
        cps = [_remote(ins[w].at[:, 1 - c], outs[w], send.at[w], recv.at[w], (x, y, 1 - c)) for w in range(n)]
        for cp in cps:
            cp.start()
        for cp in cps:
            cp.wait()

    return pl.pallas_call(
        body, name=name, in_specs=[ANY] * n, out_specs=[ANY] * n,
        out_shape=[jax.ShapeDtypeStruct((a.shape[0],) + a.shape[2:], a.dtype) for a in arrs],
        scratch_shapes=[pltpu.SemaphoreType.DMA((n,))] * 2,
    )(*arrs)


def _small_allreduce(buf, name):
    rows = buf.shape[0]

    def body(x_ref, out_ref, all_ref, send_sems, recv_sems, local_sem):
        x, y, c = _place()
        me, sibling, chips = (x, y, c), (x, y, 1 - c), _other_chips(x, y)

        def block(px, py, pc):
            return all_ref.at[pl.ds((4 * px + 2 * py + pc) * rows, rows), :]

        def copy(k, blk, to, src=None):
            return _remote(block(*blk) if src is None else src, block(*blk), send_sems.at[k], recv_sems.at[k], to)

        own = pltpu.make_async_copy(x_ref, block(*me), local_sem)
        own.start()
        first = [copy(0, me, sibling, src=x_ref)] + [copy(1 + j, me, (*chip, c), src=x_ref) for j, chip in enumerate(chips)]
        for cp in first:
            cp.start()
        passed = [copy(4 + j, (*chip, c), sibling) for j, chip in enumerate(chips)]
        for j, chip in enumerate(chips):
            copy(1 + j, (*chip, c), me).wait_recv()
            passed[j].start()
        copy(0, sibling, me).wait_recv()
        for j, chip in enumerate(chips):
            copy(4 + j, (*chip, 1 - c), me).wait_recv()
        for cp in first + passed:
            cp.wait_send()
        own.wait()
        acc = all_ref[pl.ds(0, rows), :]
        for d in range(1, 8):
            acc = acc + all_ref[pl.ds(d * rows, rows), :]
        out_ref[...] = acc

    vmem = pl.BlockSpec(memory_space=pltpu.VMEM)
    return pl.pallas_call(
        body, name=name, in_specs=[vmem], out_specs=vmem,
        out_shape=jax.ShapeDtypeStruct(buf.shape, F32),
        scratch_shapes=[pltpu.VMEM((8 * rows, 128), F32), pltpu.SemaphoreType.DMA((7,)), pltpu.SemaphoreType.DMA((7,)),
                        pltpu.SemaphoreType.DMA],
    )(buf)


ROW_TILE = 256


def _add_halves(arr, recv, c, name):
    _, _, hr, cols = arr.shape
    tr = _tile(hr, ROW_TILE)

    def body(c_ref, a_ref, r_ref, o_ref):
        o_ref[...] = (a_ref[...].astype(F32) + r_ref[...].astype(F32)).astype(o_ref.dtype)

    piece = pl.BlockSpec((None, tr, cols), lambda j, i, c_ref: (j, i, 0))
    grid_spec = pltpu.PrefetchScalarGridSpec(
        num_scalar_prefetch=1, grid=(N_CHIPS, hr // tr),
        in_specs=[pl.BlockSpec((None, None, tr, cols), lambda j, i, c_ref: (j, c_ref[0], i, 0)), piece], out_specs=piece)
    return pl.pallas_call(body, name=name, grid_spec=grid_spec, out_shape=jax.ShapeDtypeStruct(recv.shape, BF16),
                          compiler_params=_params(("parallel", "parallel")))(c.reshape(1).astype(jnp.int32), arr, recv)


def _flip_slot(d):
    return jnp.where(d == 1, 1, jnp.where(d == 3, 2, 0))


def _sum_chips(p, q, chip, name):
    _, hr, cols = p.shape
    tr = _tile(hr, ROW_TILE)

    def body(chip_ref, p_ref, q_ref, o_ref):
        j = pl.program_id(1)
        term = jnp.where(j == chip_ref[0], p_ref[...].astype(F32), q_ref[...].astype(F32))

        @pl.when(j == 0)
        def _():
            o_ref[...] = term

        @pl.when(j != 0)
        def _():
            o_ref[...] += term

    grid_spec = pltpu.PrefetchScalarGridSpec(
        num_scalar_prefetch=1, grid=(hr // tr, N_CHIPS),
        in_specs=[pl.BlockSpec((None, tr, cols), lambda i, j, chip_ref: (chip_ref[0], i, 0)),
                  pl.BlockSpec((None, tr, cols), lambda i, j, chip_ref: (_flip_slot(j ^ chip_ref[0]), i, 0))],
        out_specs=pl.BlockSpec((tr, cols), lambda i, j, chip_ref: (i, 0)))
    return pl.pallas_call(body, name=name, grid_spec=grid_spec, out_shape=jax.ShapeDtypeStruct((hr, cols), F32),
                          compiler_params=_params(("parallel", "arbitrary")))(chip.reshape(1).astype(jnp.int32), p, q)


def _adamw_halves(w, g_own, g_other, m, v, c, name):
    rows, cols = w.shape
    tr = _tile(rows // 2, ROW_TILE)
    per_half = rows // 2 // tr

    def body(c_ref, w_ref, own_ref, other_ref, m_ref, v_ref, g_ref, d_ref, nm_ref, nv_ref):
        mine = (pl.program_id(0) // per_half) == c_ref[0]
        g_ = jnp.where(mine, own_ref[...], other_ref[...])
        g_ref[...] = g_
        d_ref[...], nm_ref[...], nv_ref[...] = _adamw_math(w_ref[...], g_, m_ref[...], v_ref[...])

    blk = pl.BlockSpec((tr, cols), lambda i, c_ref: (i, 0))
    own = pl.BlockSpec((tr, cols), lambda i, c_ref: (jnp.where(i // per_half == c_ref[0], i % per_half, 0), 0))
    other = pl.BlockSpec((tr, cols), lambda i, c_ref: (jnp.where(i // per_half == c_ref[0], 0, i % per_half), 0))
    grid_spec = pltpu.PrefetchScalarGridSpec(num_scalar_prefetch=1, grid=(rows // tr,),
                                             in_specs=[blk, own, other, blk, blk], out_specs=[blk] * 4)
    return pl.pallas_call(body, name=name, grid_spec=grid_spec, out_shape=[jax.ShapeDtypeStruct(w.shape, F32)] * 4,
                          compiler_params=_params(("parallel",)))(c.reshape(1).astype(jnp.int32), w, g_own, g_other, m, v)


W_IN_COLS = (D_MAIN + N_DT) // N_CHIPS
W_IN_MAIN = W_IN_COLS // 128 * 128
W_IN_TAIL = W_IN_COLS - 128
W_IN_PARTS = ((0, W_IN_MAIN), (W_IN_TAIL, 128))


def _cast_w_in_transposed(w_t, chip, after=()):
    def body(chip_ref, w_ref, *rest):
        for start, size in W_IN_PARTS:
            rest[-1][:, pl.ds(start, size)] = w_ref[pl.ds(start, size), :].T.astype(BF16)

    grid_spec = pltpu.PrefetchScalarGridSpec(
        num_scalar_prefetch=1, grid=(D_MODEL // ROW_TILE,),
        in_specs=[pl.BlockSpec((W_IN_COLS, ROW_TILE), lambda i, chip_ref: (0, i))] + [pl.BlockSpec(memory_space=pl.ANY)] * len(after),
        out_specs=pl.BlockSpec((None, ROW_TILE, W_IN_COLS), lambda i, chip_ref: (chip_ref[0], i, 0)))
    return pl.pallas_call(body, name="cast_w_in", grid_spec=grid_spec,
                          out_shape=jax.ShapeDtypeStruct((N_CHIPS, D_MODEL, W_IN_COLS), BF16),
                          compiler_params=_params(("parallel",)))(chip.reshape(1).astype(jnp.int32), w_t, *after)


def _adamw_w_in_transposed(w_t, g_own, g_other, m_t, v_t, c):
    per_half = D_MODEL // 2 // ROW_TILE

    def body(c_ref, w_ref, own_ref, other_ref, m_ref, v_ref, g_ref, d_ref, nm_ref, nv_ref):
        mine = (pl.program_id(0) // per_half) == c_ref[0]
        for start, size in W_IN_PARTS:
            cols, rows = pl.ds(start, size), pl.ds(start, size)
            g_ = jnp.where(mine, own_ref[:, cols], other_ref[:, cols]).T
            g_ref[rows, :] = g_
            d_ref[rows, :], nm_ref[rows, :], nv_ref[rows, :] = _adamw_math(w_ref[rows, :], g_, m_ref[rows, :], v_ref[rows, :])

    blk = pl.BlockSpec((W_IN_COLS, ROW_TILE), lambda i, c_ref: (0, i))
    own = pl.BlockSpec((ROW_TILE, W_IN_COLS), lambda i, c_ref: (jnp.where(i // per_half == c_ref[0], i % per_half, 0), 0))
    other = pl.BlockSpec((ROW_TILE, W_IN_COLS), lambda i, c_ref: (jnp.where(i // per_half == c_ref[0], 0, i % per_half), 0))
    grid_spec = pltpu.PrefetchScalarGridSpec(num_scalar_prefetch=1, grid=(D_MODEL // ROW_TILE,),
                                             in_specs=[blk, own, other, blk, blk], out_specs=[blk] * 4)
    return pl.pallas_call(body, name="adamw_w_in", grid_spec=grid_spec, out_shape=[jax.ShapeDtypeStruct(w_t.shape, F32)] * 4,
                          compiler_params=_params(("parallel",)))(c.reshape(1).astype(jnp.int32), w_t, g_own, g_other, m_t, v_t)


def _adamw_math(w, g, m, v):
    m_new = ADAM_B1 * m + (1.0 - ADAM_B1) * g
    v_new = ADAM_B2 * v + (1.0 - ADAM_B2) * (g * g)
    m_hat = m_new / (1.0 - ADAM_B1 ** ADAM_STEP)
    v_hat = v_new / (1.0 - ADAM_B2 ** ADAM_STEP)
    return -ADAM_LR * (m_hat / (jnp.sqrt(v_hat) + ADAM_EPS) + ADAM_WD * w), m_new, v_new


def _adamw(w, g, m, v, name):
    rows, cols = w.shape
    tr = _tile(rows, ROW_TILE)

    def body(w_ref, g_ref, m_ref, v_ref, d_ref, nm_ref, nv_ref):
        d_ref[...], nm_ref[...], nv_ref[...] = _adamw_math(w_ref[...], g_ref[...], m_ref[...], v_ref[...])

    blk = pl.BlockSpec((tr, cols), lambda i: (i, 0))
    return pl.pallas_call(body, name=name, grid=(rows // tr,), in_specs=[blk] * 4, out_specs=[blk] * 3,
                          out_shape=[jax.ShapeDtypeStruct(w.shape, F32)] * 3, compiler_params=_params(("parallel",)))(w, g, m, v)


VECTORS = ("g_mix", "g_q", "g_k", "g_attn_out", "conv_b", "dt_bias", "a_log", "d_skip", "g_ssm_out", "g_cross", "g_mem",
           "g_cq", "g_ck", "g_mlp")
WEIGHTS = ("g_mix", "w_in", "g_q", "g_k", "g_attn_out", "conv_w", "conv_b", "dt_bias", "a_log", "d_skip", "g_ssm_out", "w_out",
           "g_cross", "g_mem", "w_cq", "w_ckv", "g_cq", "g_ck", "w_co", "g_mlp", "w_up", "w_down")


def _pack(parts):
    flat = jnp.concatenate([t.reshape(-1) for t in parts])
    total = -(-flat.shape[0] // 1024) * 1024
    return jnp.pad(flat, (0, total - flat.shape[0])).reshape(total // 128, 128)


def _unpack(buf, shapes):
    flat, out, pos = buf.reshape(-1), [], 0
    for shape in shapes:
        size = math.prod(shape)
        out.append(flat[pos:pos + size].reshape(shape))
        pos += size
    return out


def kernel(x, mem, positions, g_mix, w_in, g_q, g_k, g_attn_out, conv_w, conv_b, dt_bias, a_log, d_skip, g_ssm_out, w_out, g_cross, g_mem, w_cq, w_ckv, g_cq, g_ck, w_co, g_mlp, w_up, w_down, loss_target, m_g_mix, m_w_in, m_g_q, m_g_k, m_g_attn_out, m_conv_w, m_conv_b, m_dt_bias, m_a_log, m_d_skip, m_g_ssm_out, m_w_out, m_g_cross, m_g_mem, m_w_cq, m_w_ckv, m_g_cq, m_g_ck, m_w_co, m_g_mlp, m_w_up, m_w_down, v_g_mix, v_w_in, v_g_q, v_g_k, v_g_attn_out, v_conv_w, v_conv_b, v_dt_bias, v_a_log, v_d_skip, v_g_ssm_out, v_w_out, v_g_cross, v_g_mem, v_w_cq, v_w_ckv, v_g_cq, v_g_ck, v_w_co, v_g_mlp, v_w_up, v_w_down):
    args = dict(locals())
    weights = {n: args[n][0] for n in WEIGHTS}
    mom_m = {n: args["m_" + n][0] for n in WEIGHTS}
    mom_v = {n: args["v_" + n][0] for n in WEIGHTS}
    x_idx, y_idx, c_idx = _place()
    chip = 2 * x_idx + y_idx

    conv_parts = _small_allreduce(_pack([jnp.zeros((N_CHIPS, 4, 512), F32).at[chip].set(0.5 * weights["conv_w"])]),
                                  "gather_conv_taps")
    shapes = {n: weights[n].shape for n in MATRICES}
    first, mid, late = ("w_in",), ("w_out", "w_cq", "w_ckv", "w_co"), ("w_up", "w_down")
    w_in_t, m_in_t, v_in_t = (jnp.swapaxes(t, 1, 2)[0] for t in (w_in, m_w_in, v_w_in))
    w_in_buf = [_cast_w_in_transposed(w_in_t, chip)]
    sems_in, w_in_buf, token = _split_start("gather_ici_start_w_in", w_in_buf, _ici_plan(first, shapes), [3], after=(conv_parts,))
    bufs = [_cast_into_gathered(weights[n], n, chip, after=(token,)) for n in mid + late]
    sems_rest, bufs, token = _split_start("gather_ici_start_rest", bufs, _ici_plan(mid + late, shapes), [18], after=(token,))
    ici_sems = (sems_in[0], sems_rest[0])
    params = {n: weights[n].reshape(1, -1) for n in VECTORS}
    h_in = _rowwise(_norm_fn, [_full(x[0])], [_full(params["g_mix"])], [(D_MODEL, BF16, D_MODEL, 0, False)], name="norm_in",
                    after=(token,))[0]
    w_in_buf = _split_wait("gather_ici_wait_w_in", w_in_buf, ici_sems[0], _ici_plan(first, shapes), token, h_in)
    pass_sems, w_in_buf, token = _split_start("gather_pass_start_w_in", w_in_buf, _pass_on_plan(first, shapes), [3])
    w_in_buf = _split_wait("gather_pass_wait_w_in", w_in_buf, pass_sems[0], _pass_on_plan(first, shapes), token)
    w_in_full = _w_in_columns(w_in_buf[0], to_shards=False)
    full = {"w_in": w_in_full,
            "w_dt": jnp.pad(w_in_full[:, D_MAIN:].reshape(D_MODEL, N_GROUPS, HEADS_PER_GROUP),
                            ((0, 0), (0, 0), (0, 128 - HEADS_PER_GROUP))).reshape(D_MODEL, DT_PAD)}
    in_flight = {}

    def more_weights(stage, after):
        if stage == "mixer_done":
            rest = _split_wait("gather_ici_wait_rest", bufs, ici_sems[1], _ici_plan(mid + late, shapes), after)
            plan = lambda refs: _pass_on_plan(mid, shapes)(refs[:4]) + _pass_on_plan(late, shapes)(refs[4:])
            sems, rest, token = _split_start("gather_pass_start_rest", rest, plan, [12, 6])
            in_flight["late"] = (rest[4:], sems[1])
            return dict(zip(mid, _split_wait("gather_pass_wait_mid", rest[:4], sems[0], _pass_on_plan(mid, shapes), token)))
        late_bufs, sems = in_flight.pop("late")
        return dict(zip(late, _split_wait("gather_pass_wait_late", late_bufs, sems, _pass_on_plan(late, shapes), after)))

    params["conv_w"] = _unpack(conv_parts, [(N_CHIPS, 4, 512)])[0].transpose(1, 0, 2).reshape(4, 4 * 512)

    groups = (("w_down",), ("w_up",), ("w_co", "w_cq", "w_ckv", "w_out"), ("w_in",))
    scattered = []

    class GradStore(dict):
        pending = None

        def __setitem__(self, name, value):
            super().__setitem__(name, value)
            if "w_main" in self and "w_dt" in self and "w_in" not in self:
                gw_in = lax.dynamic_update_slice(self["w_main"], _unpad_heads(self["w_dt"]), (0, D_MAIN))
                self["w_in"] = _w_in_columns(gw_in, to_shards=True)
            for group in groups:
                if name in group and all(n in self for n in group):
                    self.settle()
                    pieces = [self[n].reshape(N_CHIPS, 2, shapes[n][0] // 2, shapes[n][1]) for n in group]
                    if group == groups[-1]:
                        self.scatter(group, pieces, _sibling_swap(pieces, "grad_swap_" + group[0]))
                    else:
                        landing = [lax.empty((N_CHIPS,) + a.shape[2:], BF16) for a in pieces]
                        sems, thru, self.token = _split_start("grad_swap_start_" + group[0], pieces + landing,
                                                              _swap_plan(len(pieces)), [len(pieces)])
                        self.pending = (group, sems[0], thru)

        def settle(self, *after):
            if self.pending is not None:
                group, sems, thru = self.pending
                self.pending = None
                thru = _split_wait("grad_swap_wait_" + group[0], thru, sems, _swap_plan(len(group)), *after)
                self.scatter(group, thru[:len(group)], thru[len(group):])

        def scatter(self, group, pieces, from_sibling):
            sums = [_add_halves(a, r, c_idx, "add_halves_" + n) for n, a, r in zip(group, pieces, from_sibling)]
            landing = [lax.empty((3,) + s.shape[1:], BF16) for s in sums]
            sems, thru, self.token = _split_start("grad_scatter_start_" + group[0], sums + landing,
                                                  _scatter_plan(len(sums)), [3 * len(sums)])
            scattered.append((group, sems[0], thru))

    loss, grad_x, grads = _local_step(x[0], mem[0], positions[0], loss_target[0], params, full, more_weights, GradStore(),
                                      h_in)

    halves = {}
    for group, sems, thru in scattered:
        thru = _split_wait("grad_scatter_wait_" + group[0], thru, sems, _scatter_plan(len(group)), grad_x)
        for i, n in enumerate(group):
            halves[n] = _sum_chips(thru[i], thru[len(group) + i], chip, "sum_chips_" + n)
    order = ("w_cq", "w_co", "w_ckv", "w_out", "w_in", "w_up", "w_down")
    sources = [halves[n] for n in order]
    landing = [lax.empty(s.shape, F32) for s in sources]
    share_sems, thru, token = _split_start("grad_share_start", sources + landing, _share_plan(len(order)), [1] * len(order))
    out_g, out_d, out_m, out_v = {}, {}, {}, {}
    for i, n in enumerate(order):
        own, other = _split_wait("grad_share_wait_" + n, [thru[i], thru[len(order) + i]], share_sems[i], _share_plan(1), token)
        if n == "w_in":
            res_t = _adamw_w_in_transposed(w_in_t, own, other, m_in_t, v_in_t, c_idx)
            out_g[n], out_d[n], out_m[n], out_v[n] = (t.T for t in res_t)
        else:
            out_g[n], out_d[n], out_m[n], out_v[n] = _adamw_halves(weights[n], own, other, mom_m[n], mom_v[n], c_idx, "adamw_" + n)
        token = out_v[n]

    small = [grads[n] for n in VECTORS] + [grads["conv_w"]]
    summed = _unpack(_small_allreduce(_pack(small), "allreduce_vectors"), [t.shape for t in small])
    g_small = dict(zip(VECTORS, summed[:-1]))
    g_small["conv_w"] = lax.dynamic_slice_in_dim(summed[-1], chip * 512, 512, axis=1)
    names = VECTORS + ("conv_w",)
    shapes = [weights[n].shape for n in names]
    packed = [_pack([src[n] for n in names]) for src in (weights, g_small, mom_m, mom_v)]
    small_out = [_unpack(t, shapes) for t in _adamw(*packed, "adamw_small")]
    for i, n in enumerate(names):
        out_g[n] = g_small[n].reshape(shapes[i])
        out_d[n], out_m[n], out_v[n] = small_out[0][i], small_out[1][i], small_out[2][i]

    total_loss = lax.psum(loss[0, 0], ("x", "y", "c"))
    outs = [total_loss, grad_x[None]]
    for group in (out_g, out_d, out_m, out_v):
        outs += [group[n][None] for n in WEIGHTS]
    return tuple(outs)
```

```python
import functools
import math

import jax
import jax.numpy as jnp
from jax import lax
from jax.experimental import pallas as pl
from jax.experimental.pallas import tpu as pltpu

F32 = jnp.float32
BF16 = jnp.bfloat16

SEQ = 2048
D_MODEL = 2048
HEAD = 64
D_ATTN = 1024
D_SSM = 1024
N_GROUPS = 4
N_STATE = 128
CHUNK = 128
ATT_BLK = 128
N_MEM = 256
D_CROSS = 512
D_FF = 8192
D_MAIN = 6144
N_DT = 16
DT_PAD = 512
ROT = 16
ROPE_THETA = 500000.0
EPS = 1e-6
NEG = -1e30
BRANCH_BLOCKS = (16, 4, 1)
DILATIONS = (1, 4, 16)

ADAM_LR, ADAM_B1, ADAM_B2, ADAM_EPS, ADAM_WD, ADAM_STEP = 0.001, 0.9, 0.999, 1e-08, 0.01, 10

VMEM_LIMIT = 56 * 1024 * 1024
MESH = pl.DeviceIdType.MESH


def _params(sem, **kw):
    return pltpu.CompilerParams(dimension_semantics=sem, vmem_limit_bytes=VMEM_LIMIT, **kw)


def _bdot(a, b, dims):
    return lax.dot_general(a.astype(BF16), b.astype(BF16), (dims, ((), ())), preferred_element_type=F32)


def _fdot(a, b, dims):
    return lax.dot_general(a, b, (dims, ((), ())), preferred_element_type=F32, precision=lax.Precision.HIGHEST)


NN = ((1,), (0,))
NT = ((1,), (1,))
TN = ((0,), (0,))


def _tile(n, want):
    t = min(n, want)
    while n % t:
        t //= 2
    return t


def _matmul(a, b, *, mode, name, outs, extra=(), epilogue=None, col_shards=1, after=(), n_cols=None, out_cols=None,
            tm=1024, tn=1024, tk=2048):
    if mode == "nn":
        (m, k), n = a.shape, b.shape[1]
    elif mode == "nt":
        (m, k), n = a.shape, b.shape[0]
    else:
        (k, m), n = a.shape, b.shape[1]
    n = n if n_cols is None else n_cols
    tm, tn, tk = _tile(m, tm), _tile(n // col_shards, tn), _tile(k, tk)
    nk = k // tk
    per_shard = n // col_shards // tn
    dims = {"nn": NN, "nt": NT, "tn": TN}[mode]
    a_spec = pl.BlockSpec((tk, tm), lambda i, j, kk: (kk, i)) if mode == "tn" else pl.BlockSpec((tm, tk), lambda i, j, kk: (i, kk))
    b_spec = pl.BlockSpec((tn, tk), lambda i, j, kk: (j, kk)) if mode == "nt" else pl.BlockSpec((tk, tn), lambda i, j, kk: (kk, j))
    o_spec = pl.BlockSpec((tm, tn), lambda i, j, kk: (i, j))
    n_extra, n_out, n_after = len(extra), len(outs), len(after)

    def body(a_ref, b_ref, *rest):
        extra_refs, out_refs, acc_ref = rest[:n_extra], rest[n_extra + n_after:n_extra + n_after + n_out], rest[-1]
        def finish(acc):
            res = (acc,) if epilogue is None else epilogue(acc, *[e[...] for e in extra_refs])
            for o_ref, r in zip(out_refs, res):
                o_ref[...] = r.astype(o_ref.dtype)

        if nk == 1:
            finish(_bdot(a_ref[...], b_ref[...], dims))
            return
        kk = pl.program_id(2)

        @pl.when(kk == 0)
        def _():
            acc_ref[...] = jnp.zeros_like(acc_ref)

        acc_ref[...] += _bdot(a_ref[...], b_ref[...], dims)

        @pl.when(kk == nk - 1)
        def _():
            finish(acc_ref[...])

    if col_shards == 1:
        out_specs, out_dims = [o_spec] * n_out, (m, n if out_cols is None else out_cols)
    else:
        sharded = pl.BlockSpec((None, tm, tn), lambda i, j, kk: (j // per_shard, i, j % per_shard))
        out_specs, out_dims = [sharded] * n_out, (col_shards, m, n // col_shards)
    res = pl.pallas_call(
        body, name=name, grid=(m // tm, n // tn, nk),
        in_specs=[a_spec, b_spec] + [o_spec] * n_extra + [pl.BlockSpec(memory_space=pl.ANY)] * n_after,
        out_specs=out_specs,
        out_shape=[jax.ShapeDtypeStruct(out_dims, dt) for dt in outs],
        scratch_shapes=[pltpu.VMEM((tm, tn) if nk > 1 else (8, 128), F32)],
        compiler_params=_params(("parallel", "parallel", "arbitrary")),
    )(a, b, *extra, *after)
    return res[0] if n_out == 1 else res


def _row_spec(tr, bw, cb, per_group):
    return pl.BlockSpec((tr, bw), (lambda g, i: (i, cb + g)) if per_group else (lambda g, i: (i, cb)))


def _vec_spec(bw, cb, per_group):
    return pl.BlockSpec((1, bw), (lambda g, i: (0, cb + g)) if per_group else (lambda g, i: (0, cb)))


def _rowwise(fn, rows, vecs, outs, *, name, n_rows=SEQ, tr=256, groups=1, after=()):
    n_r, n_v, n_after = len(rows), len(vecs), len(after)

    def body(*refs):
        vals = [r[...].astype(F32) for r in refs[:n_r + n_v]]
        res = fn(*vals)
        for o_ref, r in zip(refs[n_r + n_v + n_after:], res):
            o_ref[...] = r.astype(o_ref.dtype)

    res = pl.pallas_call(
        body, name=name, grid=(groups, n_rows // tr),
        in_specs=[_row_spec(tr, bw, cb, pg) for _, bw, cb, pg in rows] + [_vec_spec(bw, cb, pg) for _, bw, cb, pg in vecs]
        + [pl.BlockSpec(memory_space=pl.ANY)] * n_after,
        out_specs=[_row_spec(tr, bw, cb, pg) for _, _, bw, cb, pg in outs],
        out_shape=[jax.ShapeDtypeStruct((n_rows, w), dt) for w, dt, _, _, _ in outs],
        compiler_params=_params(("parallel", "parallel")),
    )(*[r[0] for r in rows], *[v[0] for v in vecs], *after)
    return res


def _rowwise_vjp(fn, rows, vecs, cts, row_grads, vec_grads, *, name, n_rows=SEQ, tr=256, groups=1, after=()):
    n_r, n_v, n_after = len(rows), len(vecs), len(after)
    ct_ops = [op for group in cts for op in group]
    ct_sizes = [len(group) for group in cts]
    res_ops = [g[6] for g in row_grads if g[6] is not None]
    n_ct, n_res, n_rg = len(ct_ops), len(res_ops), len(row_grads)

    def body(*refs):
        vals = [r[...].astype(F32) for r in refs[:n_r + n_v]]
        pos = n_r + n_v
        ct_vals = []
        for size in ct_sizes:
            acc = refs[pos][...].astype(F32)
            for t in range(1, size):
                acc = acc + refs[pos + t][...].astype(F32)
            ct_vals.append(acc)
            pos += size
        res_refs = refs[pos:pos + n_res]
        out_refs = refs[pos + n_res + n_after:]
        _, pullback = jax.vjp(fn, *vals)
        grads = pullback(tuple(ct_vals))
        r_i = 0
        for o_ref, g in zip(out_refs[:n_rg], row_grads):
            val = grads[g[0]]
            if g[6] is not None:
                val = val + res_refs[r_i][...].astype(F32)
                r_i += 1
            o_ref[...] = val.astype(o_ref.dtype)
        first = (pl.program_id(1) == 0)
        for o_ref, g in zip(out_refs[n_rg:], vec_grads):
            val = jnp.sum(grads[n_r + g[0]], axis=0, keepdims=True)
            init = first if g[4] else jnp.logical_and(first, pl.program_id(0) == 0)

            @pl.when(init)
            def _(o_ref=o_ref, val=val):
                o_ref[...] = val

            @pl.when(jnp.logical_not(init))
            def _(o_ref=o_ref, val=val):
                o_ref[...] += val

    in_specs = [_row_spec(tr, bw, cb, pg) for _, bw, cb, pg in rows] + [_vec_spec(bw, cb, pg) for _, bw, cb, pg in vecs]
    in_specs += [_row_spec(tr, bw, cb, pg) for _, bw, cb, pg in ct_ops + res_ops] + [pl.BlockSpec(memory_space=pl.ANY)] * n_after
    out_specs =[_row_spec(tr, g[3], g[4], g[5]) for g in row_grads] + [_vec_spec(g[2], g[3], g[4]) for g in vec_grads]
    out_shape = [jax.ShapeDtypeStruct((n_rows, g[1]), g[2]) for g in row_grads]
    out_shape += [jax.ShapeDtypeStruct((1, g[1]), F32) for g in vec_grads]
    return pl.pallas_call(
        body, name=name, grid=(groups, n_rows // tr),
        in_specs=in_specs, out_specs=out_specs, out_shape=out_shape,
        compiler_params=_params(("arbitrary", "arbitrary")),
    )(*[r[0] for r in rows], *[v[0] for v in vecs], *[c[0] for c in ct_ops], *[r[0] for r in res_ops], *after)


def _full(arr, width=None):
    return (arr, arr.shape[1] if width is None else width, 0, False)


def _make_xor(sh):
    def raw(x):
        n = x.shape[-1]
        lane = lax.broadcasted_iota(jnp.int32, x.shape, x.ndim - 1)
        up = pltpu.roll(x, n - sh, x.ndim - 1)
        down = pltpu.roll(x, sh, x.ndim - 1)
        return jnp.where((lane & sh) == 0, up, down)

    f = jax.custom_vjp(raw)
    f.defvjp(lambda x: (raw(x), None), lambda _, ct: (raw(ct),))
    return f


_SWAP_ROPE_HALVES = _make_xor(ROT // 2)


def _head_sum(x):
    n = x.shape[-1]
    same_head = (lax.broadcasted_iota(jnp.int32, (n, n), 0) // HEAD) == (lax.broadcasted_iota(jnp.int32, (n, n), 1) // HEAD)
    return _fdot(x, same_head.astype(F32), NN)


def _rms(x, g):
    return x * lax.rsqrt(jnp.mean(x * x, axis=-1, keepdims=True) + EPS) * g


def _head_rms_rope(x, g, cos, sin, scale):
    y = x * lax.rsqrt(_head_sum(x * x) * (1.0 / HEAD) + EPS) * g
    return (y * cos + _SWAP_ROPE_HALVES(y) * sin) * scale


def _qk_fn(q, k, v, cos, sin, gq, gk):
    return (_head_rms_rope(q, gq, cos, sin, HEAD ** -0.5), _head_rms_rope(k, gk, cos, sin, 1.0), v)


def _norm_fn(x, g):
    return (_rms(x, g),)


def _merge_fn(o0, o1, o2, l0, l1, l2, g):
    m = lax.stop_gradient(jnp.maximum(jnp.maximum(l0, l1), l2))
    e0, e1, e2 = jnp.exp(l0 - m), jnp.exp(l1 - m), jnp.exp(l2 - m)
    mix = (e0 * o0 + e1 * o1 + e2 * o2) / (e0 + e1 + e2)
    return (_rms(mix, g),)


def _gate_fn(y, z, g):
    return (_rms(y * (z * jax.nn.sigmoid(z)), g),)


def _attn_pair(q, kc, vc, kp=None, vp=None, has_prev=None):
    pick0, pick1 = _head_picks()
    k_band, v_band, mask = _attn_band(kc, vc, kp, vp, has_prev)
    s = jnp.where(mask, _bdot(jnp.concatenate([q * pick0, q * pick1], axis=0), k_band, NT), NEG)
    m = jnp.max(s, axis=-1, keepdims=True)
    p = jnp.exp(s - m)
    den = jnp.sum(p, axis=-1, keepdims=True)
    acc = _bdot(p, v_band, NN) * (1.0 / den)
    lse_rows = m + jnp.log(den)
    o = pick0 * acc[:ATT_BLK] + pick1 * acc[ATT_BLK:]
    lse = pick0 * lse_rows[:ATT_BLK] + pick1 * lse_rows[ATT_BLK:]
    return o, lse


def _head_picks():
    lane = lax.broadcasted_iota(jnp.int32, (1, 2 * HEAD), 1)
    return (lane < HEAD).astype(F32), (lane >= HEAD).astype(F32)


def _attn_band(kc, vc, kp, vp, has_prev):
    n_keys = ATT_BLK if kp is None else 2 * ATT_BLK
    qi = lax.broadcasted_iota(jnp.int32, (2 * ATT_BLK, n_keys), 0) & (ATT_BLK - 1)
    kj = lax.broadcasted_iota(jnp.int32, (2 * ATT_BLK, n_keys), 1)
    if kp is None:
        return kc, vc, qi >= kj
    in_prev = jnp.logical_and(jnp.logical_and(kj < ATT_BLK, kj >= qi), has_prev)
    mask = jnp.logical_or(in_prev, jnp.logical_and(kj >= ATT_BLK, qi >= kj - ATT_BLK))
    return jnp.concatenate([kp, kc], axis=0), jnp.concatenate([vp, vc], axis=0), mask


def _attn_config(b):
    r = DILATIONS[b]
    return r, ATT_BLK * r, (512 if r == 1 else 128), BRANCH_BLOCKS[b] > 1


def _for_residues(r, fn):
    if r <= 4:
        for rho in range(r):
            fn(rho)
    else:
        def step(t, carry):
            for u in range(4):
                fn(4 * t + u)
            return carry

        lax.fori_loop(0, r // 4, step, 0)


def _strided_rows(start, r):
    if r > 1:
        return pl.ds(start, ATT_BLK, stride=r)
    return pl.ds(start if isinstance(start, int) else pl.multiple_of(start, ATT_BLK), ATT_BLK)


def _attention_fwd(qn, kn, vn, b):
    r, rows, lanes, with_prev = _attn_config(b)
    cur = pl.BlockSpec((rows, lanes), lambda g, n: (n, g))
    prev = pl.BlockSpec((rows, lanes), lambda g, n: (jnp.maximum(n - 1, 0), g))

    def body(*refs):
        ins, (o_ref, l_ref) = refs[:-2], refs[-2:]
        has_prev = pl.program_id(1) > 0

        def one(rho):
            sub = _strided_rows(rho, r)
            for pair in range(lanes // 128):
                sl = pl.ds(pair * 128, 128)
                args = [ref[sub, sl] for ref in ins] + ([has_prev] if with_prev else [])
                o_ref[sub, sl], l_ref[sub, sl] = _attn_pair(*args)

        _for_residues(r, one)

    operands = (qn, kn, vn, kn, vn) if with_prev else (qn, kn, vn)
    return pl.pallas_call(
        body, name="attn_fwd_%d" % r, grid=(D_ATTN // lanes, SEQ // rows),
        in_specs=[cur, cur, cur] + ([prev, prev] if with_prev else []), out_specs=[cur, cur],
        out_shape=[jax.ShapeDtypeStruct((SEQ, D_ATTN), F32)] * 2,
        compiler_params=_params(("parallel", "parallel")),
    )(*operands)


def _attn_pair_bwd(q, kc, vc, kp, vp, o, lse, do, dl, has_prev):
    pick0, pick1 = _head_picks()
    lane = lax.broadcasted_iota(jnp.int32, (1, 2 * HEAD), 1)
    k_band, v_band, mask = _attn_band(kc, vc, kp, vp, has_prev)
    q2 = jnp.concatenate([q * pick0, q * pick1], axis=0)
    do2 = jnp.concatenate([do * pick0, do * pick1], axis=0)
    lse2 = jnp.concatenate([jnp.sum(lse * (lane == 0).astype(F32), axis=-1, keepdims=True),
                            jnp.sum(lse * (lane == HEAD).astype(F32), axis=-1, keepdims=True)], axis=0)
    base = jnp.sum(jnp.concatenate([dl * pick0, dl * pick1], axis=0) - do2 * jnp.concatenate([o, o], axis=0),
                   axis=-1, keepdims=True)
    p = jnp.exp(jnp.where(mask, _bdot(q2, k_band, NT), NEG) - lse2)
    ds = p * (_bdot(do2, v_band, NT) + base)
    dq2 = _bdot(ds, k_band, NN)
    dq = pick0 * dq2[:ATT_BLK] + pick1 * dq2[ATT_BLK:]
    dk, dv = _bdot(ds, q2, TN), _bdot(p, do2, TN)
    if kp is None:
        return dq, dk, dv
    return dq, dk[ATT_BLK:], dv[ATT_BLK:], dk[:ATT_BLK], dv[:ATT_BLK]


def _attention_bwd(qn, kn, vn, o, lse, do, dl, b):
    r, rows, lanes, with_prev = _attn_config(b)
    cur = pl.BlockSpec((rows, lanes), lambda g, n: (n, g))
    prev = pl.BlockSpec((rows, lanes), lambda g, n: (jnp.maximum(n - 1, 0), g))
    whole = pl.BlockSpec((SEQ, lanes), lambda g, n: (0, g))
    n_in = 5 if with_prev else 3

    def body(*refs):
        ins, (o_ref, l_ref, do_ref, dl_ref, dq_ref, dk_ref, dv_ref) = refs[:n_in], refs[n_in:]
        n = pl.program_id(1)

        @pl.when(n == 0)
        def _():
            dk_ref[...] = jnp.zeros_like(dk_ref)
            dv_ref[...] = jnp.zeros_like(dv_ref)

        def one(rho):
            sub = _strided_rows(rho, r)
            sub_c = _strided_rows(n * rows + rho, r)
            sub_p = _strided_rows(jnp.maximum(n - 1, 0) * rows + rho, r)
            for pair in range(lanes // 128):
                sl = pl.ds(pair * 128, 128)
                vals = [ref[sub, sl] for ref in ins] + ([] if with_prev else [None, None])
                grads = _attn_pair_bwd(*vals, o_ref[sub, sl], l_ref[sub, sl], do_ref[sub, sl], dl_ref[sub, sl], n > 0)
                dq_ref[sub, sl] = grads[0]
                dk_ref[sub_c, sl] += grads[1]
                dv_ref[sub_c, sl] += grads[2]
                if with_prev:
                    dk_ref[sub_p, sl] += grads[3]
                    dv_ref[sub_p, sl] += grads[4]

        _for_residues(r, one)

    operands = (qn, kn, vn, kn, vn) if with_prev else (qn, kn, vn)
    return pl.pallas_call(
        body, name="attn_bwd_%d" % r, grid=(D_ATTN // lanes, SEQ // rows),
        in_specs=[cur, cur, cur] + ([prev, prev] if with_prev else []) + [cur] * 4, out_specs=[cur, whole, whole],
        out_shape=[jax.ShapeDtypeStruct((SEQ, D_ATTN), F32)] * 3,
        compiler_params=_params(("parallel", "arbitrary")),
    )(*operands, o, lse, do, dl)


CONV_COLS = 256
XBC_BLOCK0 = 4096 // CONV_COLS


def _shift_rows(x, s):
    n = x.shape[0]
    t = lax.broadcasted_iota(jnp.int32, x.shape, 0)
    if s >= 0:
        return jnp.where(t >= s, pltpu.roll(x, s, 0), 0.0)
    return jnp.where(t < n + s, pltpu.roll(x, n + s, 0), 0.0)


def _conv_pre(x, w_ref, b_ref):
    pre = b_ref[...] + w_ref[3:4, :] * x
    for k in range(3):
        pre = pre + w_ref[k:k + 1, :] * _shift_rows(x, 3 - k)
    return pre


def _conv_fwd(proj, conv_w, conv_b):
    cols = conv_w.shape[1]

    def body(x_ref, w_ref, b_ref, o_ref):
        pre = _conv_pre(x_ref[...], w_ref, b_ref)
        o_ref[...] = pre * jax.nn.sigmoid(pre)

    blk = pl.BlockSpec((SEQ, CONV_COLS), lambda j: (0, j))
    return pl.pallas_call(
        body, name="conv_fwd", grid=(cols // CONV_COLS,),
        in_specs=[pl.BlockSpec((SEQ, CONV_COLS), lambda j: (0, XBC_BLOCK0 + j)),
                  pl.BlockSpec((4, CONV_COLS), lambda j: (0, j)), pl.BlockSpec((1, CONV_COLS), lambda j: (0, j))],
        out_specs=blk, out_shape=jax.ShapeDtypeStruct((SEQ, cols), F32),
        compiler_params=_params(("parallel",)),
    )(proj, conv_w, conv_b)


def _conv_bwd(proj, conv_w, conv_b, dy):
    cols = conv_w.shape[1]

    def body(x_ref, w_ref, b_ref, dy_ref, dx_ref, dw_ref, db_ref):
        x = x_ref[...]
        pre = _conv_pre(x, w_ref, b_ref)
        sg = jax.nn.sigmoid(pre)
        dpre = dy_ref[...] * (sg * (1.0 + pre * (1.0 - sg)))
        db_ref[...] = jnp.sum(dpre, axis=0, keepdims=True)
        dx = w_ref[3:4, :] * dpre
        dw_ref[3:4, :] = jnp.sum(dpre * x, axis=0, keepdims=True)
        for k in range(3):
            dx = dx + w_ref[k:k + 1, :] * _shift_rows(dpre, k - 3)
            dw_ref[k:k + 1, :] = jnp.sum(dpre * _shift_rows(x, 3 - k), axis=0, keepdims=True)
        dw_ref[4:8, :] = jnp.zeros((4, CONV_COLS), F32)
        dx_ref[...] = dx.astype(dx_ref.dtype)

    blk = pl.BlockSpec((SEQ, CONV_COLS), lambda j: (0, j))
    return pl.pallas_call(
        body, name="conv_bwd", grid=(cols // CONV_COLS,),
        in_specs=[pl.BlockSpec((SEQ, CONV_COLS), lambda j: (0, XBC_BLOCK0 + j)),
                  pl.BlockSpec((4, CONV_COLS), lambda j: (0, j)), pl.BlockSpec((1, CONV_COLS), lambda j: (0, j)), blk],
        out_specs=[blk, pl.BlockSpec((8, CONV_COLS), lambda j: (0, j)), pl.BlockSpec((1, CONV_COLS), lambda j: (0, j))],
        out_shape=[jax.ShapeDtypeStruct((SEQ, cols), BF16), jax.ShapeDtypeStruct((8, cols), F32),
                   jax.ShapeDtypeStruct((1, cols), F32)],
        compiler_params=_params(("parallel",)),
    )(proj, conv_w, conv_b, dy)


HEADS_PER_GROUP = 4


GROUP_WIDTH = HEADS_PER_GROUP * HEAD


def _ssd_chunk(x, bm, cm, dtr, bias, alog, dsk, h):
    row = lax.broadcasted_iota(jnp.int32, (CHUNK, CHUNK), 0)
    col = lax.broadcasted_iota(jnp.int32, (CHUNK, CHUNK), 1)
    causal = row >= col
    z = dtr + bias
    dt = jnp.maximum(z, 0.0) + jnp.log(1.0 + jnp.exp(-jnp.abs(z)))
    acs = _fdot(causal.astype(F32), dt * -jnp.exp(alog), NN)
    acs_t, dt_t = acs.T, dt.T
    cb = _bdot(cm, bm, NT)
    lane = lax.broadcasted_iota(jnp.int32, (1, CHUNK), 1)
    sub = lax.broadcasted_iota(jnp.int32, (CHUNK, 1), 0)
    wide = lax.broadcasted_iota(jnp.int32, (1, GROUP_WIDTH), 1) // HEAD
    tall = lax.broadcasted_iota(jnp.int32, (GROUP_WIDTH, 1), 0) // HEAD
    acs_last = jnp.sum(acs * (sub == CHUNK - 1).astype(F32), axis=0, keepdims=True)
    to_lanes = (lax.broadcasted_iota(jnp.int32, (CHUNK, GROUP_WIDTH), 0)
                == lax.broadcasted_iota(jnp.int32, (CHUNK, GROUP_WIDTH), 1) // HEAD).astype(F32)
    grow = _fdot(jnp.exp(acs), to_lanes, NN)
    keep = _fdot(jnp.exp(acs_last - acs) * dt, to_lanes, NN)
    w_parts, x_parts, skip, carry = [], [], 0.0, 0.0
    for j in range(HEADS_PER_GROUP):
        on_lane, on_sub = (lane == j).astype(F32), (sub == j).astype(F32)
        acs_c = jnp.sum(acs * on_lane, axis=1, keepdims=True)
        acs_r = jnp.sum(acs_t * on_sub, axis=0, keepdims=True)
        dt_r = jnp.sum(dt_t * on_sub, axis=0, keepdims=True)
        w_parts.append(cb * jnp.exp(jnp.where(causal, acs_c - acs_r, NEG)) * dt_r)
        x_parts.append(x * (wide == j).astype(F32))
        skip = skip + jnp.sum(dsk * on_lane, axis=1, keepdims=True) * (wide == j).astype(F32)
        carry = carry + jnp.sum(jnp.exp(acs_last) * on_lane, axis=1, keepdims=True) * (tall == j).astype(F32)
    y_diag = _bdot(jnp.concatenate(w_parts, axis=1), jnp.concatenate(x_parts, axis=0), NN)
    y = y_diag + _bdot(cm, h, NT) * grow + skip * x
    return y, h * carry + _bdot(x * keep, bm, TN)


def _ssd_specs(reverse):
    n_chunks = SEQ // CHUNK
    c_of = (lambda c: n_chunks - 1 - c) if reverse else (lambda c: c)
    x_spec = pl.BlockSpec((CHUNK, 256), lambda g, c: (c_of(c), g))
    b_spec = pl.BlockSpec((CHUNK, N_STATE), lambda g, c: (c_of(c), 8 + g))
    c_spec = pl.BlockSpec((CHUNK, N_STATE), lambda g, c: (c_of(c), 12 + g))
    dt_spec = pl.BlockSpec((CHUNK, 128), lambda g, c: (c_of(c), g))
    vec_spec = pl.BlockSpec((1, 128), lambda g, c: (0, g))
    h_spec = pl.BlockSpec((None, None, GROUP_WIDTH, N_STATE), lambda g, c: (c_of(c), g, 0, 0))
    return x_spec, b_spec, c_spec, dt_spec, vec_spec, h_spec


def _ssd_fwd(xbc, dt_raw, bias, alog, dsk):
    x_spec, b_spec, c_spec, dt_spec, vec_spec, h_spec = _ssd_specs(False)

    def body(x_ref, b_ref, c_ref, dt_ref, bias_ref, alog_ref, dsk_ref, y_ref, hin_ref, h_scr):
        @pl.when(pl.program_id(1) == 0)
        def _():
            h_scr[...] = jnp.zeros_like(h_scr)

        h = h_scr[...]
        hin_ref[...] = h
        y_ref[...], h_scr[...] = _ssd_chunk(x_ref[...], b_ref[...], c_ref[...], dt_ref[...], bias_ref[...], alog_ref[...],
                                            dsk_ref[...], h)

    return pl.pallas_call(
        body, name="ssd_fwd", grid=(N_GROUPS, SEQ // CHUNK),
        in_specs=[x_spec, b_spec, c_spec, dt_spec, vec_spec, vec_spec, vec_spec],
        out_specs=[x_spec, h_spec],
        out_shape=[jax.ShapeDtypeStruct((SEQ, D_SSM), F32),
                   jax.ShapeDtypeStruct((SEQ // CHUNK, N_GROUPS, GROUP_WIDTH, N_STATE), F32)],
        scratch_shapes=[pltpu.VMEM((GROUP_WIDTH, N_STATE), F32)],
        compiler_params=_params(("parallel", "arbitrary")),
    )(xbc, xbc, xbc, dt_raw, bias, alog, dsk)


def _ssd_bwd(xbc, dt_raw, bias, alog, dsk, h_in, dy):
    x_spec, b_spec, c_spec, dt_spec, vec_spec, h_spec = _ssd_specs(True)
    dxbc_x = pl.BlockSpec((CHUNK, 256), x_spec.index_map)

    def body(x_ref, b_ref, c_ref, dt_ref, bias_ref, alog_ref, dsk_ref, hin_ref, dy_ref,
             dx_ref, db_ref, dc_ref, ddt_ref, dbias_ref, dalog_ref, ddsk_ref, dh_scr):
        first = pl.program_id(1) == 0

        @pl.when(first)
        def _():
            dh_scr[...] = jnp.zeros_like(dh_scr)

        _, pullback = jax.vjp(_ssd_chunk, x_ref[...], b_ref[...], c_ref[...], dt_ref[...], bias_ref[...], alog_ref[...],
                              dsk_ref[...], hin_ref[...])
        g = pullback((dy_ref[...], dh_scr[...]))
        dx_ref[...], db_ref[...], dc_ref[...] = g[0], g[1], g[2]
        ddt_ref[...] = g[3].astype(ddt_ref.dtype)
        dh_scr[...] = g[7]
        for o_ref, val in ((dbias_ref, g[4]), (dalog_ref, g[5]), (ddsk_ref, g[6])):
            @pl.when(first)
            def _(o_ref=o_ref, val=val):
                o_ref[...] = val

            @pl.when(jnp.logical_not(first))
            def _(o_ref=o_ref, val=val):
                o_ref[...] += val

    n_chunks = SEQ // CHUNK
    out_b = pl.BlockSpec((CHUNK, N_STATE), lambda g, c: (n_chunks - 1 - c, g))
    res = pl.pallas_call(
        body, name="ssd_bwd", grid=(N_GROUPS, n_chunks),
        in_specs=[x_spec, b_spec, c_spec, dt_spec, vec_spec, vec_spec, vec_spec, h_spec, x_spec],
        out_specs=[dxbc_x, out_b, out_b, dt_spec, vec_spec, vec_spec, vec_spec],
        out_shape=[jax.ShapeDtypeStruct((SEQ, D_SSM), F32), jax.ShapeDtypeStruct((SEQ, N_GROUPS * N_STATE), F32),
                   jax.ShapeDtypeStruct((SEQ, N_GROUPS * N_STATE), F32), jax.ShapeDtypeStruct((SEQ, DT_PAD), BF16),
                   jax.ShapeDtypeStruct((1, DT_PAD), F32), jax.ShapeDtypeStruct((1, DT_PAD), F32),
                   jax.ShapeDtypeStruct((1, DT_PAD), F32)],
        scratch_shapes=[pltpu.VMEM((GROUP_WIDTH, N_STATE), F32)],
        compiler_params=_params(("parallel", "arbitrary")),
    )(xbc, xbc, xbc, dt_raw, bias, alog, dsk, h_in, dy)
    return res


CROSS_HEAD = 128
CROSS_ROWS = 512


def _cross_head(q, k, v, gq, gk):
    qn = _rms(q, gq) * (CROSS_HEAD ** -0.5)
    kn = _rms(k, gk)
    s = _bdot(qn, kn, NT)
    p = jnp.exp(s - lax.stop_gradient(jnp.max(s, axis=-1, keepdims=True)))
    return _bdot(p, v, NN) * (1.0 / jnp.sum(p, axis=-1, keepdims=True))


def _cross_specs():
    q_spec = pl.BlockSpec((CROSS_ROWS, CROSS_HEAD), lambda h, i: (i, h))
    k_spec = pl.BlockSpec((N_MEM, CROSS_HEAD), lambda h, i: (0, h))
    v_spec = pl.BlockSpec((N_MEM, CROSS_HEAD), lambda h, i: (0, 4 + h))
    g_spec = pl.BlockSpec((1, CROSS_HEAD), lambda h, i: (0, 0))
    return q_spec, k_spec, v_spec, g_spec


def _cross_fwd(qc, kv, gq, gk):
    q_spec, k_spec, v_spec, g_spec = _cross_specs()

    def body(q_ref, k_ref, v_ref, gq_ref, gk_ref, o_ref):
        o_ref[...] = _cross_head(q_ref[...], k_ref[...], v_ref[...], gq_ref[...], gk_ref[...]).astype(o_ref.dtype)

    return pl.pallas_call(
        body, name="cross_fwd", grid=(4, SEQ // CROSS_ROWS),
        in_specs=[q_spec, k_spec, v_spec, g_spec, g_spec], out_specs=q_spec,
        out_shape=jax.ShapeDtypeStruct((SEQ, D_CROSS), BF16),
        compiler_params=_params(("parallel", "parallel")),
    )(qc, kv, kv, gq, gk)


def _cross_bwd(qc, kv, gq, gk, do):
    q_spec, k_spec, v_spec, g_spec = _cross_specs()

    def body(q_ref, k_ref, v_ref, gq_ref, gk_ref, do_ref, dq_ref, dk_ref, dv_ref, dgq_ref, dgk_ref):
        _, pullback = jax.vjp(_cross_head, q_ref[...], k_ref[...], v_ref[...], gq_ref[...], gk_ref[...])
        dq, dk, dv, dgq, dgk = pullback(do_ref[...].astype(F32))
        dq_ref[...] = dq.astype(dq_ref.dtype)
        row0 = pl.program_id(1) == 0
        all0 = jnp.logical_and(row0, pl.program_id(0) == 0)
        for o_ref, val, init in ((dk_ref, dk, row0), (dv_ref, dv, row0), (dgq_ref, dgq, all0), (dgk_ref, dgk, all0)):
            @pl.when(init)
            def _(o_ref=o_ref, val=val):
                o_ref[...] = val

            @pl.when(jnp.logical_not(init))
            def _(o_ref=o_ref, val=val):
                o_ref[...] += val

    return pl.pallas_call(
        body, name="cross_bwd", grid=(4, SEQ // CROSS_ROWS),
        in_specs=[q_spec, k_spec, v_spec, g_spec, g_spec, q_spec],
        out_specs=[q_spec, k_spec, k_spec, g_spec, g_spec],
        out_shape=[jax.ShapeDtypeStruct((SEQ, D_CROSS), BF16), jax.ShapeDtypeStruct((N_MEM, D_CROSS), F32),
                   jax.ShapeDtypeStruct((N_MEM, D_CROSS), F32), jax.ShapeDtypeStruct((1, CROSS_HEAD), F32),
                   jax.ShapeDtypeStruct((1, CROSS_HEAD), F32)],
        compiler_params=_params(("arbitrary", "arbitrary")),
    )(qc, kv, kv, gq, gk, do)


def _loss_head(y, target):
    tr = 256

    def body(y_ref, t_ref, dy_ref, dyb_ref, loss_ref):
        err = y_ref[...] - t_ref[...]
        dy = err * (1.0 / D_MODEL)
        dy_ref[...] = dy
        dyb_ref[...] = dy.astype(BF16)
        part = jnp.sum(jnp.sum(err * err, axis=1, keepdims=True), axis=0, keepdims=True) * (0.5 / D_MODEL)
        part = jnp.broadcast_to(part, (1, 128))

        @pl.when(pl.program_id(0) == 0)
        def _():
            loss_ref[...] = part

        @pl.when(pl.program_id(0) != 0)
        def _():
            loss_ref[...] += part

    blk = pl.BlockSpec((tr, D_MODEL), lambda i: (i, 0))
    return pl.pallas_call(
        body, name="loss_head", grid=(SEQ // tr,),
        in_specs=[blk, blk], out_specs=[blk, blk, pl.BlockSpec((1, 128), lambda i: (0, 0))],
        out_shape=[jax.ShapeDtypeStruct((SEQ, D_MODEL), F32), jax.ShapeDtypeStruct((SEQ, D_MODEL), BF16),
                   jax.ShapeDtypeStruct((1, 128), F32)],
        compiler_params=_params(("arbitrary",)),
    )(y, target)


def _pad_heads(v):
    return jnp.pad(v.reshape(N_GROUPS, HEADS_PER_GROUP), ((0, 0), (0, 128 - HEADS_PER_GROUP))).reshape(1, DT_PAD)


def _unpad_heads(v):
    return v.reshape(v.shape[0], N_GROUPS, 128)[:, :, :HEADS_PER_GROUP].reshape(v.shape[0], N_DT)


def _rope_tables(positions):
    half = ROT // 2
    inv_freq = ROPE_THETA ** (-2.0 * jnp.arange(half, dtype=F32) / ROT)
    ang = positions.reshape(SEQ, 1).astype(F32) * inv_freq
    cos, sin = jnp.cos(ang), jnp.sin(ang)
    ones, zeros = jnp.ones((SEQ, HEAD - ROT), F32), jnp.zeros((SEQ, HEAD - ROT), F32)
    cos_h = jnp.concatenate([cos, cos, ones], axis=1)
    sin_h = jnp.concatenate([-sin, sin, zeros], axis=1)
    return jnp.tile(cos_h, (1, 2)), jnp.tile(sin_h, (1, 2))


def _add_res(acc, res):
    return (acc + res,)


def _settle(grads, *after):
    if hasattr(grads, "settle"):
        grads.settle(*after)


def _take_token(grads):
    token = getattr(grads, "token", None)
    if token is None:
        return ()
    grads.token = None
    return (token,)


def _local_step(x, mem, positions, target, p, w, more_weights=None, grads=None, h=None):
    grads = {} if grads is None else grads
    w = dict(w)
    cos, sin = _rope_tables(positions)
    gq2, gk2 = jnp.tile(p["g_q"], (1, 2)), jnp.tile(p["g_k"], (1, 2))
    bias, alog, dsk = _pad_heads(p["dt_bias"]), _pad_heads(p["a_log"]), _pad_heads(p["d_skip"])
    norm_out = [(D_MODEL, BF16, D_MODEL, 0, False)]

    if h is None:
        h = _rowwise(_norm_fn, [_full(x)], [_full(p["g_mix"])], norm_out, name="norm_in")[0]
    proj = _matmul(h, w["w_in"], mode="nn", name="in_proj", outs=[F32], n_cols=D_MAIN)
    dt_raw = _matmul(h, w["w_dt"], mode="nn", name="dt_proj", outs=[F32])
    qk_rows = [(proj, 128, 0, True), (proj, 128, 8, True), (proj, 128, 16, True), _full(cos), _full(sin)]
    qk_vecs = [_full(gq2), _full(gk2)]
    qn, kn, vn = _rowwise(_qk_fn, qk_rows, qk_vecs, [(D_ATTN, F32, 128, 0, True)] * 3, name="qk_prep", groups=8, tr=1024)
    branches = [_attention_fwd(qn, kn, vn, b) for b in range(3)]
    merge_rows = [_full(o) for o, _ in branches] + [_full(lse) for _, lse in branches]
    attn = _rowwise(_merge_fn, merge_rows, [_full(p["g_attn_out"])], [(D_ATTN, BF16, D_ATTN, 0, False)], name="attn_merge")[0]
    xbc = _conv_fwd(proj, p["conv_w"], p["conv_b"])
    y_ssd, h_in = _ssd_fwd(xbc, dt_raw, bias, alog, dsk)
    gate_rows = [(y_ssd, 256, 0, True), (proj, 256, 12, True)]
    gate_vecs = [(p["g_ssm_out"], 256, 0, True)]
    ssm = _rowwise(_gate_fn, gate_rows, gate_vecs, [(D_SSM, BF16, 256, 0, True)], name="ssm_gate", groups=4)[0]
    mix = jnp.concatenate([attn, ssm], axis=1)
    if more_weights is not None:
        w.update(more_weights("mixer_done", mix))
    x1 = _matmul(mix, w["w_out"], mode="nn", name="out_proj", outs=[F32], extra=(x,), epilogue=_add_res)
    hc = _rowwise(_norm_fn, [_full(x1)], [_full(p["g_cross"])], norm_out, name="norm_cross")[0]
    memh = _rowwise(_norm_fn, [_full(mem)], [_full(p["g_mem"])], norm_out, name="norm_mem", n_rows=N_MEM)[0]
    qc = _matmul(hc, w["w_cq"], mode="nn", name="cq_proj", outs=[F32])
    kv = _matmul(memh, w["w_ckv"], mode="nn", name="ckv_proj", outs=[F32])
    oc = _cross_fwd(qc, kv, p["g_cq"], p["g_ck"])
    x2 = _matmul(oc, w["w_co"], mode="nn", name="co_proj", outs=[F32], extra=(x1,), epilogue=_add_res)
    hm = _rowwise(_norm_fn, [_full(x2)], [_full(p["g_mlp"])], norm_out, name="norm_mlp")[0]
    if more_weights is not None:
        w.update(more_weights("cross_done", hm))
    u, act = _matmul(hm, w["w_up"], mode="nn", name="up_proj", outs=[F32, BF16],
                     epilogue=lambda acc: (acc, jnp.square(jnp.maximum(acc, 0.0))))
    x3 = _matmul(act, w["w_down"], mode="nn", name="down_proj", outs=[F32], extra=(x2,), epilogue=_add_res)
    dy, dyb, loss = _loss_head(x3, target)

    grads["w_down"] = _matmul(act, dyb, mode="tn", name="dw_down", outs=[BF16], after=_take_token(grads))
    du = _matmul(dyb, w["w_down"], mode="nt", name="d_act", outs=[BF16], extra=(u,), after=_take_token(grads),
                 epilogue=lambda acc, uu: (acc * (2.0 * jnp.maximum(uu, 0.0)),))
    _settle(grads, du)
    grads["w_up"] = _matmul(hm, du, mode="tn", name="dw_up", outs=[BF16], col_shards=4, after=_take_token(grads))
    dhm = _matmul(du, w["w_up"], mode="nt", name="d_hm", outs=[F32], after=_take_token(grads))
    _settle(grads, dhm)
    dx2, grads["g_mlp"] = _rowwise_vjp(
        _norm_fn, [_full(x2)], [_full(p["g_mlp"])], [[_full(dhm)]],
        [(0, D_MODEL, F32, D_MODEL, 0, False, _full(dy))], [(0, D_MODEL, D_MODEL, 0, False)], name="norm_mlp_bwd")
    grads["w_co"] = _matmul(oc, dx2, mode="tn", name="dw_co", outs=[BF16], col_shards=4, after=_take_token(grads))
    doc = _matmul(dx2, w["w_co"], mode="nt", name="d_oc", outs=[BF16])
    dqc, dkc, dvc, grads["g_cq"], grads["g_ck"] = _cross_bwd(qc, kv, p["g_cq"], p["g_ck"], doc)
    grads["w_cq"] = _matmul(hc, dqc, mode="tn", name="dw_cq", outs=[BF16])
    dhc = _matmul(dqc, w["w_cq"], mode="nt", name="d_hc", outs=[F32])
    dkv = jnp.concatenate([dkc, dvc], axis=1)
    grads["w_ckv"] = _matmul(memh, dkv, mode="tn", name="dw_ckv", outs=[BF16])
    dmemh = _matmul(dkv, w["w_ckv"], mode="nt", name="d_memh", outs=[F32])
    grads["g_mem"] = _rowwise_vjp(_norm_fn, [_full(mem)], [_full(p["g_mem"])], [[_full(dmemh)]], [],
                                  [(0, D_MODEL, D_MODEL, 0, False)], name="norm_mem_bwd", n_rows=N_MEM)[0]
    dx1, grads["g_cross"] = _rowwise_vjp(
        _norm_fn, [_full(x1)], [_full(p["g_cross"])], [[_full(dhc)]],
        [(0, D_MODEL, F32, D_MODEL, 0, False, _full(dx2))], [(0, D_MODEL, D_MODEL, 0, False)], name="norm_cross_bwd")
    grads["w_out"] = _matmul(mix, dx1, mode="tn", name="dw_out", outs=[BF16])
    dmix = _matmul(dx1, w["w_out"], mode="nt", name="d_mix", outs=[F32], after=_take_token(grads))
    _settle(grads, dmix)
    merge_grads = [(i, D_ATTN, F32, D_ATTN, 0, False, None) for i in range(6)]
    *dol, grads["g_attn_out"] = _rowwise_vjp(
        _merge_fn, merge_rows, [_full(p["g_attn_out"])], [[(dmix, D_ATTN, 0, False)]],
        merge_grads, [(0, D_ATTN, D_ATTN, 0, False)], name="attn_merge_bwd", after=_take_token(grads))
    dqkv = [_attention_bwd(qn, kn, vn, *branches[b], dol[b], dol[3 + b], b) for b in range(3)]
    qk_cts = [[(dqkv[b][i], 128, 0, True) for b in range(3)] for i in range(3)]
    dq, dk, dv, dgq2, dgk2 = _rowwise_vjp(
        _qk_fn, qk_rows, qk_vecs, qk_cts, [(i, D_ATTN, BF16, 128, 0, True, None) for i in range(3)],
        [(0, 128, 128, 0, False), (1, 128, 128, 0, False)], name="qk_prep_bwd", groups=8, tr=512)
    grads["g_q"] = dgq2[:, :HEAD] + dgq2[:, HEAD:]
    grads["g_k"] = dgk2[:, :HEAD] + dgk2[:, HEAD:]
    dy_ssd, dz, grads["g_ssm_out"] = _rowwise_vjp(
        _gate_fn, gate_rows, gate_vecs, [[(dmix, 256, 4, True)]],
        [(0, D_SSM, F32, 256, 0, True, None), (1, D_SSM, BF16, 256, 0, True, None)],
        [(0, D_SSM, 256, 0, True)], name="ssm_gate_bwd", groups=4)
    dxs, db, dc, ddt, dbias, dalog, ddsk = _ssd_bwd(xbc, dt_raw, bias, alog, dsk, h_in, dy_ssd)
    grads["dt_bias"], grads["a_log"], grads["d_skip"] = _unpad_heads(dbias), _unpad_heads(dalog), _unpad_heads(ddsk)
    dxbc_raw, dconv_w, grads["conv_b"] = _conv_bwd(proj, p["conv_w"], p["conv_b"], jnp.concatenate([dxs, db, dc], axis=1))
    grads["conv_w"] = dconv_w[:4]
    dproj = jnp.concatenate([dq, dk, dv, dz, dxbc_raw], axis=1)
    grads["w_main"] = _matmul(h, dproj, mode="tn", name="dw_main", outs=[BF16], out_cols=D_MAIN + N_DT)
    grads["w_dt"] = _matmul(h, ddt, mode="tn", name="dw_dt", outs=[BF16])
    dh = _matmul(dproj, w["w_in"], mode="nt", name="d_h_main", outs=[F32], after=_take_token(grads))
    dh = _matmul(ddt, w["w_dt"], mode="nt", name="d_h_dt", outs=[F32], extra=(dh,), epilogue=_add_res)
    grad_x, grads["g_mix"] = _rowwise_vjp(
        _norm_fn, [_full(x)], [_full(p["g_mix"])], [[_full(dh)]],
        [(0, D_MODEL, F32, D_MODEL, 0, False, _full(dx1))], [(0, D_MODEL, D_MODEL, 0, False)], name="norm_in_bwd")
    return loss, grad_x, grads


MATRICES = ("w_in", "w_out", "w_cq", "w_ckv", "w_co", "w_up", "w_down")
ROW_SHARDED = ("w_out", "w_cq", "w_ckv", "w_down")
N_CHIPS = 4
ANY = pl.BlockSpec(memory_space=pl.ANY)


def _place():
    return lax.axis_index("x"), lax.axis_index("y"), lax.axis_index("c")


def _other_chips(x, y):
    return [(1 - x, y), (x, 1 - y), (1 - x, 1 - y)]


def _remote(src, dst, send_sem, recv_sem, device):
    return pltpu.make_async_remote_copy(src_ref=src, dst_ref=dst, send_sem=send_sem, recv_sem=recv_sem,
                                        device_id=device, device_id_type=MESH)


def _gathered_shape(name, shard):
    rows, cols = shard.shape
    if name == "w_in":
        return (N_CHIPS, rows, cols)
    return (N_CHIPS * rows, cols) if name in ROW_SHARDED else (rows, N_CHIPS * cols)


def _shard_window(name, ref, rows, cols, chip, half):
    r0, nr = (0, rows) if half is None else (half * (rows // 2), rows // 2)
    if name == "w_in":
        return ref.at[chip, pl.ds(r0, nr), :]
    if name in ROW_SHARDED:
        return ref.at[pl.ds(chip * rows + r0, nr), :]
    return ref.at[pl.ds(r0, nr), pl.ds(pl.multiple_of(chip * cols, 128), cols)]


def _cast_into_gathered(w, name, chip, after=()):
    rows, cols = w.shape
    tr = _tile(rows, ROW_TILE)

    def body(chip_ref, w_ref, *rest):
        rest[-1][...] = w_ref[...].astype(BF16)

    if name == "w_in":
        out_spec = pl.BlockSpec((None, tr, cols), lambda i, chip_ref: (chip_ref[0], i, 0))
    elif name in ROW_SHARDED:
        out_spec = pl.BlockSpec((tr, cols), lambda i, chip_ref: (chip_ref[0] * (rows // tr) + i, 0))
    else:
        out_spec = pl.BlockSpec((tr, cols), lambda i, chip_ref: (i, chip_ref[0]))
    grid_spec = pltpu.PrefetchScalarGridSpec(
        num_scalar_prefetch=1, grid=(rows // tr,),
        in_specs=[pl.BlockSpec((tr, cols), lambda i, chip_ref: (i, 0))] + [pl.BlockSpec(memory_space=pl.ANY)] * len(after),
        out_specs=out_spec)
    return pl.pallas_call(body, name="cast_" + name, grid_spec=grid_spec,
                          out_shape=jax.ShapeDtypeStruct(_gathered_shape(name, w), BF16),
                          compiler_params=_params(("parallel",)))(chip.reshape(1).astype(jnp.int32), w, *after)


def _w_in_columns(arr, to_shards):
    rows, piece = D_MODEL, (D_MAIN + N_DT) // N_CHIPS
    tr = ROW_TILE

    def body(a_ref, o_ref):
        for j in range(N_CHIPS):
            if to_shards:
                o_ref[j] = a_ref[:, pl.ds(piece * j, piece)]
            else:
                o_ref[:, pl.ds(piece * j, piece)] = a_ref[j]

    pieces = pl.BlockSpec((N_CHIPS, tr, piece), lambda i: (0, i, 0))
    matrix = pl.BlockSpec((tr, N_CHIPS * piece), lambda i: (i, 0))
    out_dims = (N_CHIPS, rows, piece) if to_shards else (rows, N_CHIPS * piece)
    return pl.pallas_call(
        body, name="w_in_to_shards" if to_shards else "w_in_from_shards", grid=(rows // tr,),
        in_specs=[matrix if to_shards else pieces], out_specs=pieces if to_shards else matrix,
        out_shape=jax.ShapeDtypeStruct(out_dims, arr.dtype), compiler_params=_params(("parallel",)))(arr)


HBM = pl.BlockSpec(memory_space=pltpu.HBM)
SEM = pl.BlockSpec(memory_space=pltpu.SEMAPHORE)
EFFECT = pltpu.SideEffectType.DATAFLOW_SIDE_EFFECTING


def _split_start(name, bufs, plan, counts, after=()):
    n, n_g, n_after = len(bufs), len(counts), len(after)

    def body(*refs):
        ins, sems, token = refs[:n], refs[n + n_after:n + n_after + 2 * n_g], refs[-1]
        for g, copies in enumerate(plan(ins)):
            for i, (src, dst, device, _) in enumerate(copies):
                _remote(src, dst, sems[2 * g].at[i], sems[2 * g + 1].at[i], device).start()
        token[...] = jnp.zeros_like(token)

    sem_shapes = [pltpu.SemaphoreType.DMA((cnt,)) for cnt in counts for _ in range(2)]
    res = pl.pallas_call(
        body, name=name,
        out_shape=(*sem_shapes, *[pltpu.HBM(b.shape, b.dtype) for b in bufs], jax.ShapeDtypeStruct((8, 128), F32)),
        in_specs=(*(HBM,) * n, *(ANY,) * n_after),
        out_specs=(*(SEM,) * (2 * n_g), *(HBM,) * n, pl.BlockSpec(memory_space=pltpu.VMEM)),
        input_output_aliases={i: 2 * n_g + i for i in range(n)},
        compiler_params=pltpu.CompilerParams(has_side_effects=EFFECT),
    )(*[pltpu.with_memory_space_constraint(b, pltpu.HBM) for b in bufs], *after)
    sems = [(res[2 * g], res[2 * g + 1]) for g in range(n_g)]
    return sems, list(res[2 * n_g:2 * n_g + n]), res[-1]


def _split_wait(name, bufs, sems, plan, *after):
    n = len(bufs)

    def body(*refs):
        ins, send, recv = refs[:n], refs[n], refs[n + 1]
        (copies,) = plan(ins)
        for i, (src, _, device, landing) in enumerate(copies):
            cp = _remote(src, landing, send.at[i], recv.at[i], device)
            cp.wait_send()
            cp.wait_recv()

    res = pl.pallas_call(
        body, name=name, out_shape=tuple(pltpu.HBM(b.shape, b.dtype) for b in bufs),
        in_specs=(*(HBM,) * n, SEM, SEM, *(ANY,) * len(after)), out_specs=(HBM,) * n,
        input_output_aliases={i: i for i in range(n)},
        compiler_params=pltpu.CompilerParams(has_side_effects=EFFECT),
    )(*bufs, sems[0], sems[1], *after)
    return list(res)


def _ici_plan(names, shard_shapes):
    def plan(refs):
        x, y, c = _place()
        copies = []
        for ref, name in zip(refs, names):
            win = _shard_window(name, ref, *shard_shapes[name], 2 * x + y, c)
            for px, py in _other_chips(x, y):
                copies.append((win, win, (px, py, c), _shard_window(name, ref, *shard_shapes[name], 2 * px + py, c)))
        return [copies]
    return plan


def _pass_on_plan(names, shard_shapes):
    def plan(refs):
        x, y, c = _place()
        copies = []
        for ref, name in zip(refs, names):
            for px, py in _other_chips(x, y):
                win = _shard_window(name, ref, *shard_shapes[name], 2 * px + py, c)
                copies.append((win, win, (x, y, 1 - c), _shard_window(name, ref, *shard_shapes[name], 2 * px + py, 1 - c)))
        return [copies]
    return plan


def _swap_plan(n_pairs):
    def plan(refs):
        x, y, c = _place()
        return [[(src.at[:, 1 - c], dst, (x, y, 1 - c), dst) for src, dst in zip(refs[:n_pairs], refs[n_pairs:])]]
    return plan


def _share_plan(n_pairs):
    def plan(refs):
        x, y, c = _place()
        return [[(src, dst, (x, y, 1 - c), dst)] for src, dst in zip(refs[:n_pairs], refs[n_pairs:])]
    return plan


def _scatter_plan(n_pairs):
    def plan(refs):
        x, y, c = _place()
        copies = []
        for src, dst in zip(refs[:n_pairs], refs[n_pairs:]):
            for k, (px, py) in enumerate(_other_chips(x, y)):
                copies.append((src.at[2 * px + py], dst.at[k], (px, py, c), dst.at[k]))
        return [copies]
    return plan


def _sibling_swap(arrs, name):
    n = len(arrs)

    def body(*refs):
        ins, outs, send, recv = refs[:n], refs[n:2 * n], refs[2 * n], refs[2 * n + 1]
        x, y, c = _place()
        cps = [_remote(ins[w].at[:, 1 - c], outs[w], send.at[w], recv.at[w], (x, y, 1 - c)) for w in range(n)]
        for cp in cps:
            cp.start()
        for cp in cps:
            cp.wait()

    return pl.pallas_call(
        body, name=name, in_specs=[ANY] * n, out_specs=[ANY] * n,
        out_shape=[jax.ShapeDtypeStruct((a.shape[0],) + a.shape[2:], a.dtype) for a in arrs],
        scratch_shapes=[pltpu.SemaphoreType.DMA((n,))] * 2,
    )(*arrs)


def _small_allreduce(buf, name):
    rows = buf.shape[0]

    def body(x_ref, out_ref, all_ref, send_sems, recv_sems, local_sem):
        x, y, c = _place()
        me, sibling, chips = (x, y, c), (x, y, 1 - c), _other_chips(x, y)

        def block(px, py, pc):
            return all_ref.at[pl.ds((4 * px + 2 * py + pc) * rows, rows), :]

        def copy(k, blk, to, src=None):
            return _remote(block(*blk) if src is None else src, block(*blk), send_sems.at[k], recv_sems.at[k], to)

        own = pltpu.make_async_copy(x_ref, block(*me), local_sem)
        own.start()
        first = [copy(0, me, sibling, src=x_ref)] + [copy(1 + j, me, (*chip, c), src=x_ref) for j, chip in enumerate(chips)]
        for cp in first:
            cp.start()
        passed = [copy(4 + j, (*chip, c), sibling) for j, chip in enumerate(chips)]
        for j, chip in enumerate(chips):
            copy(1 + j, (*chip, c), me).wait_recv()
            passed[j].start()
        copy(0, sibling, me).wait_recv()
        for j, chip in enumerate(chips):
            copy(4 + j, (*chip, 1 - c), me).wait_recv()
        for cp in first + passed:
            cp.wait_send()
        own.wait()
        acc = all_ref[pl.ds(0, rows), :]
        for d in range(1, 8):
            acc = acc + all_ref[pl.ds(d * rows, rows), :]
        out_ref[...] = acc

    vmem = pl.BlockSpec(memory_space=pltpu.VMEM)
    return pl.pallas_call(
        body, name=name, in_specs=[vmem], out_specs=vmem,
        out_shape=jax.ShapeDtypeStruct(buf.shape, F32),
        scratch_shapes=[pltpu.VMEM((8 * rows, 128), F32), pltpu.SemaphoreType.DMA((7,)), pltpu.SemaphoreType.DMA((7,)),
                        pltpu.SemaphoreType.DMA],
    )(buf)


ROW_TILE = 256
BIG_ROW_TILE = 1024


def _add_halves(arr, recv, c, name):
    _, _, hr, cols = arr.shape
    tr = _tile(hr, BIG_ROW_TILE)

    def body(c_ref, a_ref, r_ref, o_ref):
        o_ref[...] = (a_ref[...].astype(F32) + r_ref[...].astype(F32)).astype(o_ref.dtype)

    piece = pl.BlockSpec((None, tr, cols), lambda j, i, c_ref: (j, i, 0))
    grid_spec = pltpu.PrefetchScalarGridSpec(
        num_scalar_prefetch=1, grid=(N_CHIPS, hr // tr),
        in_specs=[pl.BlockSpec((None, None, tr, cols), lambda j, i, c_ref: (j, c_ref[0], i, 0)), piece], out_specs=piece)
    return pl.pallas_call(body, name=name, grid_spec=grid_spec, out_shape=jax.ShapeDtypeStruct(recv.shape, BF16),
                          compiler_params=_params(("parallel", "parallel")))(c.reshape(1).astype(jnp.int32), arr, recv)


def _flip_slot(d):
    return jnp.where(d == 1, 1, jnp.where(d == 3, 2, 0))


def _sum_chips(p, q, chip, name):
    _, hr, cols = p.shape
    tr = _tile(hr, BIG_ROW_TILE)

    def body(chip_ref, p_ref, q_ref, o_ref):
        j = pl.program_id(1)
        term = jnp.where(j == chip_ref[0], p_ref[...].astype(F32), q_ref[...].astype(F32))

        @pl.when(j == 0)
        def _():
            o_ref[...] = term

        @pl.when(j != 0)
        def _():
            o_ref[...] += term

    grid_spec = pltpu.PrefetchScalarGridSpec(
        num_scalar_prefetch=1, grid=(hr // tr, N_CHIPS),
        in_specs=[pl.BlockSpec((None, tr, cols), lambda i, j, chip_ref: (chip_ref[0], i, 0)),
                  pl.BlockSpec((None, tr, cols), lambda i, j, chip_ref: (_flip_slot(j ^ chip_ref[0]), i, 0))],
        out_specs=pl.BlockSpec((tr, cols), lambda i, j, chip_ref: (i, 0)))
    return pl.pallas_call(body, name=name, grid_spec=grid_spec, out_shape=jax.ShapeDtypeStruct((hr, cols), F32),
                          compiler_params=_params(("parallel", "arbitrary")))(chip.reshape(1).astype(jnp.int32), p, q)


def _adamw_halves(w, g_own, g_other, m, v, c, name):
    rows, cols = w.shape
    tr = _tile(rows // 2, ROW_TILE)
    per_half = rows // 2 // tr

    def body(c_ref, w_ref, own_ref, other_ref, m_ref, v_ref, g_ref, d_ref, nm_ref, nv_ref):
        mine = (pl.program_id(0) // per_half) == c_ref[0]
        g_ = jnp.where(mine, own_ref[...], other_ref[...])
        g_ref[...] = g_
        d_ref[...], nm_ref[...], nv_ref[...] = _adamw_math(w_ref[...], g_, m_ref[...], v_ref[...])

    blk = pl.BlockSpec((tr, cols), lambda i, c_ref: (i, 0))
    own = pl.BlockSpec((tr, cols), lambda i, c_ref: (jnp.where(i // per_half == c_ref[0], i % per_half, 0), 0))
    other = pl.BlockSpec((tr, cols), lambda i, c_ref: (jnp.where(i // per_half == c_ref[0], 0, i % per_half), 0))
    grid_spec = pltpu.PrefetchScalarGridSpec(num_scalar_prefetch=1, grid=(rows // tr,),
                                             in_specs=[blk, own, other, blk, blk], out_specs=[blk] * 4)
    return pl.pallas_call(body, name=name, grid_spec=grid_spec, out_shape=[jax.ShapeDtypeStruct(w.shape, F32)] * 4,
                          compiler_params=_params(("parallel",)))(c.reshape(1).astype(jnp.int32), w, g_own, g_other, m, v)


W_IN_COLS = (D_MAIN + N_DT) // N_CHIPS
W_IN_MAIN = W_IN_COLS // 128 * 128
W_IN_TAIL = W_IN_COLS - 128
W_IN_PARTS = ((0, W_IN_MAIN), (W_IN_TAIL, 128))


def _cast_w_in_transposed(w_t, chip, after=()):
    def body(chip_ref, w_ref, *rest):
        for start, size in W_IN_PARTS:
            rest[-1][:, pl.ds(start, size)] = w_ref[pl.ds(start, size), :].T.astype(BF16)

    grid_spec = pltpu.PrefetchScalarGridSpec(
        num_scalar_prefetch=1, grid=(D_MODEL // ROW_TILE,),
        in_specs=[pl.BlockSpec((W_IN_COLS, ROW_TILE), lambda i, chip_ref: (0, i))] + [pl.BlockSpec(memory_space=pl.ANY)] * len(after),
        out_specs=pl.BlockSpec((None, ROW_TILE, W_IN_COLS), lambda i, chip_ref: (chip_ref[0], i, 0)))
    return pl.pallas_call(body, name="cast_w_in", grid_spec=grid_spec,
                          out_shape=jax.ShapeDtypeStruct((N_CHIPS, D_MODEL, W_IN_COLS), BF16),
                          compiler_params=_params(("parallel",)))(chip.reshape(1).astype(jnp.int32), w_t, *after)


def _adamw_w_in_transposed(w_t, g_own, g_other, m_t, v_t, c):
    per_half = D_MODEL // 2 // ROW_TILE

    def body(c_ref, w_ref, own_ref, other_ref, m_ref, v_ref, g_ref, d_ref, nm_ref, nv_ref):
        mine = (pl.program_id(0) // per_half) == c_ref[0]
        for start, size in W_IN_PARTS:
            cols, rows = pl.ds(start, size), pl.ds(start, size)
            g_ = jnp.where(mine, own_ref[:, cols], other_ref[:, cols]).T
            g_ref[rows, :] = g_
            d_ref[rows, :], nm_ref[rows, :], nv_ref[rows, :] = _adamw_math(w_ref[rows, :], g_, m_ref[rows, :], v_ref[rows, :])

    blk = pl.BlockSpec((W_IN_COLS, ROW_TILE), lambda i, c_ref: (0, i))
    own = pl.BlockSpec((ROW_TILE, W_IN_COLS), lambda i, c_ref: (jnp.where(i // per_half == c_ref[0], i % per_half, 0), 0))
    other = pl.BlockSpec((ROW_TILE, W_IN_COLS), lambda i, c_ref: (jnp.where(i // per_half == c_ref[0], 0, i % per_half), 0))
    grid_spec = pltpu.PrefetchScalarGridSpec(num_scalar_prefetch=1, grid=(D_MODEL // ROW_TILE,),
                                             in_specs=[blk, own, other, blk, blk], out_specs=[blk] * 4)
    return pl.pallas_call(body, name="adamw_w_in", grid_spec=grid_spec, out_shape=[jax.ShapeDtypeStruct(w_t.shape, F32)] * 4,
                          compiler_params=_params(("parallel",)))(c.reshape(1).astype(jnp.int32), w_t, g_own, g_other, m_t, v_t)


def _adamw_math(w, g, m, v):
    m_new = ADAM_B1 * m + (1.0 - ADAM_B1) * g
    v_new = ADAM_B2 * v + (1.0 - ADAM_B2) * (g * g)
    m_hat = m_new / (1.0 - ADAM_B1 ** ADAM_STEP)
    v_hat = v_new / (1.0 - ADAM_B2 ** ADAM_STEP)
    return -ADAM_LR * (m_hat / (jnp.sqrt(v_hat) + ADAM_EPS) + ADAM_WD * w), m_new, v_new


def _adamw(w, g, m, v, name):
    rows, cols = w.shape
    tr = _tile(rows, ROW_TILE)

    def body(w_ref, g_ref, m_ref, v_ref, d_ref, nm_ref, nv_ref):
        d_ref[...], nm_ref[...], nv_ref[...] = _adamw_math(w_ref[...], g_ref[...], m_ref[...], v_ref[...])

    blk = pl.BlockSpec((tr, cols), lambda i: (i, 0))
    return pl.pallas_call(body, name=name, grid=(rows // tr,), in_specs=[blk] * 4, out_specs=[blk] * 3,
                          out_shape=[jax.ShapeDtypeStruct(w.shape, F32)] * 3, compiler_params=_params(("parallel",)))(w, g, m, v)


VECTORS = ("g_mix", "g_q", "g_k", "g_attn_out", "conv_b", "dt_bias", "a_log", "d_skip", "g_ssm_out", "g_cross", "g_mem",
           "g_cq", "g_ck", "g_mlp")
WEIGHTS = ("g_mix", "w_in", "g_q", "g_k", "g_attn_out", "conv_w", "conv_b", "dt_bias", "a_log", "d_skip", "g_ssm_out", "w_out",
           "g_cross", "g_mem", "w_cq", "w_ckv", "g_cq", "g_ck", "w_co", "g_mlp", "w_up", "w_down")


def _pack(parts):
    flat = jnp.concatenate([t.reshape(-1) for t in parts])
    total = -(-flat.shape[0] // 1024) * 1024
    return jnp.pad(flat, (0, total - flat.shape[0])).reshape(total // 128, 128)


def _unpack(buf, shapes):
    flat, out, pos = buf.reshape(-1), [], 0
    for shape in shapes:
        size = math.prod(shape)
        out.append(flat[pos:pos + size].reshape(shape))
        pos += size
    return out


def kernel(x, mem, positions, g_mix, w_in, g_q, g_k, g_attn_out, conv_w, conv_b, dt_bias, a_log, d_skip, g_ssm_out, w_out, g_cross, g_mem, w_cq, w_ckv, g_cq, g_ck, w_co, g_mlp, w_up, w_down, loss_target, m_g_mix, m_w_in, m_g_q, m_g_k, m_g_attn_out, m_conv_w, m_conv_b, m_dt_bias, m_a_log, m_d_skip, m_g_ssm_out, m_w_out, m_g_cross, m_g_mem, m_w_cq, m_w_ckv, m_g_cq, m_g_ck, m_w_co, m_g_mlp, m_w_up, m_w_down, v_g_mix, v_w_in, v_g_q, v_g_k, v_g_attn_out, v_conv_w, v_conv_b, v_dt_bias, v_a_log, v_d_skip, v_g_ssm_out, v_w_out, v_g_cross, v_g_mem, v_w_cq, v_w_ckv, v_g_cq, v_g_ck, v_w_co, v_g_mlp, v_w_up, v_w_down):
    args = dict(locals())
    weights = {n: args[n][0] for n in WEIGHTS}
    mom_m = {n: args["m_" + n][0] for n in WEIGHTS}
    mom_v = {n: args["v_" + n][0] for n in WEIGHTS}
    x_idx, y_idx, c_idx = _place()
    chip = 2 * x_idx + y_idx

    conv_parts = _small_allreduce(_pack([jnp.zeros((N_CHIPS, 4, 512), F32).at[chip].set(0.5 * weights["conv_w"])]),
                                  "gather_conv_taps")
    shapes = {n: weights[n].shape for n in MATRICES}
    first, mid, late = ("w_in",), ("w_out", "w_cq", "w_ckv", "w_co"), ("w_up", "w_down")
    w_in_t, m_in_t, v_in_t = (jnp.swapaxes(t, 1, 2)[0] for t in (w_in, m_w_in, v_w_in))
    w_in_buf = [_cast_w_in_transposed(w_in_t, chip)]
    sems_in, w_in_buf, token = _split_start("gather_ici_start_w_in", w_in_buf, _ici_plan(first, shapes), [3], after=(conv_parts,))
    bufs = [_cast_into_gathered(weights[n], n, chip, after=(token,)) for n in mid + late]
    sems_rest, bufs, token = _split_start("gather_ici_start_rest", bufs, _ici_plan(mid + late, shapes), [18], after=(token,))
    ici_sems = (sems_in[0], sems_rest[0])
    params = {n: weights[n].reshape(1, -1) for n in VECTORS}
    h_in = _rowwise(_norm_fn, [_full(x[0])], [_full(params["g_mix"])], [(D_MODEL, BF16, D_MODEL, 0, False)], name="norm_in",
                    after=(token,))[0]
    w_in_buf = _split_wait("gather_ici_wait_w_in", w_in_buf, ici_sems[0], _ici_plan(first, shapes), token, h_in)
    pass_sems, w_in_buf, token = _split_start("gather_pass_start_w_in", w_in_buf, _pass_on_plan(first, shapes), [3])
    w_in_buf = _split_wait("gather_pass_wait_w_in", w_in_buf, pass_sems[0], _pass_on_plan(first, shapes), token)
    w_in_full = _w_in_columns(w_in_buf[0], to_shards=False)
    full = {"w_in": w_in_full,
            "w_dt": jnp.pad(w_in_full[:, D_MAIN:].reshape(D_MODEL, N_GROUPS, HEADS_PER_GROUP),
                            ((0, 0), (0, 0), (0, 128 - HEADS_PER_GROUP))).reshape(D_MODEL, DT_PAD)}
    in_flight = {}

    def more_weights(stage, after):
        if stage == "mixer_done":
            rest = _split_wait("gather_ici_wait_rest", bufs, ici_sems[1], _ici_plan(mid + late, shapes), after)
            plan = lambda refs: _pass_on_plan(mid, shapes)(refs[:4]) + _pass_on_plan(late, shapes)(refs[4:])
            sems, rest, token = _split_start("gather_pass_start_rest", rest, plan, [12, 6])
            in_flight["late"] = (rest[4:], sems[1])
            return dict(zip(mid, _split_wait("gather_pass_wait_mid", rest[:4], sems[0], _pass_on_plan(mid, shapes), token)))
        late_bufs, sems = in_flight.pop("late")
        return dict(zip(late, _split_wait("gather_pass_wait_late", late_bufs, sems, _pass_on_plan(late, shapes), after)))

    params["conv_w"] = _unpack(conv_parts, [(N_CHIPS, 4, 512)])[0].transpose(1, 0, 2).reshape(4, 4 * 512)

    groups = (("w_down",), ("w_up",), ("w_co", "w_cq", "w_ckv", "w_out"), ("w_in",))
    scattered = []

    class GradStore(dict):
        pending = None

        def __setitem__(self, name, value):
            super().__setitem__(name, value)
            if "w_main" in self and "w_dt" in self and "w_in" not in self:
                gw_in = lax.dynamic_update_slice(self["w_main"], _unpad_heads(self["w_dt"]), (0, D_MAIN))
                self["w_in"] = _w_in_columns(gw_in, to_shards=True)
            for group in groups:
                if name in group and all(n in self for n in group):
                    self.settle()
                    pieces = [self[n].reshape(N_CHIPS, 2, shapes[n][0] // 2, shapes[n][1]) for n in group]
                    if group == groups[-1]:
                        self.scatter(group, pieces, _sibling_swap(pieces, "grad_swap_" + group[0]))
                    else:
                        landing = [lax.empty((N_CHIPS,) + a.shape[2:], BF16) for a in pieces]
                        sems, thru, self.token = _split_start("grad_swap_start_" + group[0], pieces + landing,
                                                              _swap_plan(len(pieces)), [len(pieces)])
                        self.pending = (group, sems[0], thru)

        def settle(self, *after):
            if self.pending is not None:
                group, sems, thru = self.pending
                self.pending = None
                thru = _split_wait("grad_swap_wait_" + group[0], thru, sems, _swap_plan(len(group)), *after)
                self.scatter(group, thru[:len(group)], thru[len(group):])

        def scatter(self, group, pieces, from_sibling):
            sums = [_add_halves(a, r, c_idx, "add_halves_" + n) for n, a, r in zip(group, pieces, from_sibling)]
            landing = [lax.empty((3,) + s.shape[1:], BF16) for s in sums]
            sems, thru, self.token = _split_start("grad_scatter_start_" + group[0], sums + landing,
                                                  _scatter_plan(len(sums)), [3 * len(sums)])
            scattered.append((group, sems[0], thru))

    loss, grad_x, grads = _local_step(x[0], mem[0], positions[0], loss_target[0], params, full, more_weights, GradStore(),
                                      h_in)

    halves = {}
    for group, sems, thru in scattered:
        thru = _split_wait("grad_scatter_wait_" + group[0], thru, sems, _scatter_plan(len(group)), grad_x)
        for i, n in enumerate(group):
            halves[n] = _sum_chips(thru[i], thru[len(group) + i], chip, "sum_chips_" + n)
    order = ("w_cq", "w_co", "w_ckv", "w_out", "w_in", "w_up", "w_down")
    sources = [halves[n] for n in order]
    landing = [lax.empty(s.shape, F32) for s in sources]
    share_sems, thru, token = _split_start("grad_share_start", sources + landing, _share_plan(len(order)), [1] * len(order))
    out_g, out_d, out_m, out_v = {}, {}, {}, {}
    for i, n in enumerate(order):
        own, other = _split_wait("grad_share_wait_" + n, [thru[i], thru[len(order) + i]], share_sems[i], _share_plan(1), token)
        if n == "w_in":
            res_t = _adamw_w_in_transposed(w_in_t, own, other, m_in_t, v_in_t, c_idx)
            out_g[n], out_d[n], out_m[n], out_v[n] = (t.T for t in res_t)
        else:
            out_g[n], out_d[n], out_m[n], out_v[n] = _adamw_halves(weights[n], own, other, mom_m[n], mom_v[n], c_idx, "adamw_" + n)
        token = out_v[n]

    small = [grads[n] for n in VECTORS] + [grads["conv_w"]]
    summed = _unpack(_small_allreduce(_pack(small), "allreduce_vectors"), [t.shape for t in small])
    g_small = dict(zip(VECTORS, summed[:-1]))
    g_small["conv_w"] = lax.dynamic_slice_in_dim(summed[-1], chip * 512, 512, axis=1)
    names = VECTORS + ("conv_w",)
    shapes = [weights[n].shape for n in names]
    packed = [_pack([src[n] for n in names]) for src in (weights, g_small, mom_m, mom_v)]
    small_out = [_unpack(t, shapes) for t in _adamw(*packed, "adamw_small")]
    for i, n in enumerate(names):
        out_g[n] = g_small[n].reshape(shapes[i])
        out_d[n], out_m[n], out_v[n] = small_out[0][i], small_out[1][i], small_out[2][i]

    total_loss = lax.psum(loss[0, 0], ("x", "y", "c"))
    outs = [total_loss, grad_x[None]]
    for group in (out_g, out_d, out_m, out_v):
        outs += [group[n][None] for n in WEIGHTS]
    return tuple(outs)
```

```python
import functools
import math

import jax
import jax.numpy as jnp
from jax import lax
from jax.experimental import pallas as pl
from jax.experimental.pallas import tpu as pltpu

F32 = jnp.float32
BF16 = jnp.bfloat16

SEQ = 2048
D_MODEL = 2048
HEAD = 64
D_ATTN = 1024
D_SSM = 1024
N_GROUPS = 4
N_STATE = 128
CHUNK = 128
ATT_BLK = 128
N_MEM = 256
D_CROSS = 512
D_FF = 8192
D_MAIN = 6144
N_DT = 16
DT_PAD = 512
ROT = 16
ROPE_THETA = 500000.0
EPS = 1e-6
NEG = -1e30
BRANCH_BLOCKS = (16, 4, 1)
DILATIONS = (1, 4, 16)

ADAM_LR, ADAM_B1, ADAM_B2, ADAM_EPS, ADAM_WD, ADAM_STEP = 0.001, 0.9, 0.999, 1e-08, 0.01, 10

VMEM_LIMIT = 56 * 1024 * 1024
MESH = pl.DeviceIdType.MESH


def _params(sem, **kw):
    return pltpu.CompilerParams(dimension_semantics=sem, vmem_limit_bytes=VMEM_LIMIT, **kw)


def _bdot(a, b, dims):
    return lax.dot_general(a.astype(BF16), b.astype(BF16), (dims, ((), ())), preferred_element_type=F32)


def _fdot(a, b, dims):
    return lax.dot_general(a, b, (dims, ((), ())), preferred_element_type=F32, precision=lax.Precision.HIGHEST)


NN = ((1,), (0,))
NT = ((1,), (1,))
TN = ((0,), (0,))


def _tile(n, want):
    t = min(n, want)
    while n % t:
        t //= 2
    return t


def _matmul(a, b, *, mode, name, outs, extra=(), epilogue=None, col_shards=1, after=(), n_cols=None, out_cols=None,
            tile_sums=0, tm=1024, tn=1024, tk=2048):
    if mode == "nn":
        (m, k), n = a.shape, b.shape[1]
    elif mode == "nt":
        (m, k), n = a.shape, b.shape[0]
    else:
        (k, m), n = a.shape, b.shape[1]
    n = n if n_cols is None else n_cols
    tm, tn, tk = _tile(m, tm), _tile(n // col_shards, tn), _tile(k, tk)
    nk = k // tk
    per_shard = n // col_shards // tn
    dims = {"nn": NN, "nt": NT, "tn": TN}[mode]
    a_spec = pl.BlockSpec((tk, tm), lambda i, j, kk: (kk, i)) if mode == "tn" else pl.BlockSpec((tm, tk), lambda i, j, kk: (i, kk))
    b_spec = pl.BlockSpec((tn, tk), lambda i, j, kk: (j, kk)) if mode == "nt" else pl.BlockSpec((tk, tn), lambda i, j, kk: (kk, j))
    o_spec = pl.BlockSpec((tm, tn), lambda i, j, kk: (i, j))
    n_extra, n_out, n_after = len(extra), len(outs), len(after)

    def body(a_ref, b_ref, *rest):
        extra_refs, out_refs, acc_ref = rest[:n_extra], rest[n_extra + n_after:-1], rest[-1]

        def finish(acc):
            res = (acc,) if epilogue is None else epilogue(acc, *[e[...] for e in extra_refs])
            for o_ref, r in zip(out_refs[:n_out], res):
                o_ref[...] = r.astype(o_ref.dtype)
            for o_ref, r in zip(out_refs[n_out:], res[n_out:]):
                o_ref[...] = jnp.broadcast_to(r, o_ref.shape)

        if nk == 1:
            finish(_bdot(a_ref[...], b_ref[...], dims))
            return
        kk = pl.program_id(2)

        @pl.when(kk == 0)
        def _():
            acc_ref[...] = jnp.zeros_like(acc_ref)

        acc_ref[...] += _bdot(a_ref[...], b_ref[...], dims)

        @pl.when(kk == nk - 1)
        def _():
            finish(acc_ref[...])

    if col_shards == 1:
        out_specs, out_dims = [o_spec] * n_out, (m, n if out_cols is None else out_cols)
    else:
        sharded = pl.BlockSpec((None, tm, tn), lambda i, j, kk: (j // per_shard, i, j % per_shard))
        out_specs, out_dims = [sharded] * n_out, (col_shards, m, n // col_shards)
    res = pl.pallas_call(
        body, name=name, grid=(m // tm, n // tn, nk),
        in_specs=[a_spec, b_spec] + [o_spec] * n_extra + [pl.BlockSpec(memory_space=pl.ANY)] * n_after,
        out_specs=out_specs + [pl.BlockSpec((8, 128), lambda i, j, kk: (i, j))] * tile_sums,
        out_shape=[jax.ShapeDtypeStruct(out_dims, dt) for dt in outs]
        + [jax.ShapeDtypeStruct((m // tm * 8, n // tn * 128), F32)] * tile_sums,
        scratch_shapes=[pltpu.VMEM((tm, tn) if nk > 1 else (8, 128), F32)],
        compiler_params=_params(("parallel", "parallel", "arbitrary")),
    )(a, b, *extra, *after)
    res = list(res[:n_out]) + [t[::8, ::128] for t in res[n_out:]]
    return res[0] if len(res) == 1 else res


def _row_spec(tr, bw, cb, per_group):
    return pl.BlockSpec((tr, bw), (lambda g, i: (i, cb + g)) if per_group else (lambda g, i: (i, cb)))


def _vec_spec(bw, cb, per_group):
    return pl.BlockSpec((1, bw), (lambda g, i: (0, cb + g)) if per_group else (lambda g, i: (0, cb)))


def _rowwise(fn, rows, vecs, outs, *, name, n_rows=SEQ, tr=256, groups=1, after=()):
    n_r, n_v, n_after = len(rows), len(vecs), len(after)

    def body(*refs):
        vals = [r[...].astype(F32) for r in refs[:n_r + n_v]]
        res = fn(*vals)
        for o_ref, r in zip(refs[n_r + n_v + n_after:], res):
            o_ref[...] = r.astype(o_ref.dtype)

    res = pl.pallas_call(
        body, name=name, grid=(groups, n_rows // tr),
        in_specs=[_row_spec(tr, bw, cb, pg) for _, bw, cb, pg in rows] + [_vec_spec(bw, cb, pg) for _, bw, cb, pg in vecs]
        + [pl.BlockSpec(memory_space=pl.ANY)] * n_after,
        out_specs=[_row_spec(tr, bw, cb, pg) for _, _, bw, cb, pg in outs],
        out_shape=[jax.ShapeDtypeStruct((n_rows, w), dt) for w, dt, _, _, _ in outs],
        compiler_params=_params(("parallel", "parallel")),
    )(*[r[0] for r in rows], *[v[0] for v in vecs], *after)
    return res


def _rowwise_vjp(fn, rows, vecs, cts, row_grads, vec_grads, *, name, n_rows=SEQ, tr=256, groups=1, after=()):
    n_r, n_v, n_after = len(rows), len(vecs), len(after)
    ct_ops = [op for group in cts for op in group]
    ct_sizes = [len(group) for group in cts]
    res_ops = [g[6] for g in row_grads if g[6] is not None]
    n_ct, n_res, n_rg = len(ct_ops), len(res_ops), len(row_grads)

    def body(*refs):
        vals = [r[...].astype(F32) for r in refs[:n_r + n_v]]
        pos = n_r + n_v
        ct_vals = []
        for size in ct_sizes:
            acc = refs[pos][...].astype(F32)
            for t in range(1, size):
                acc = acc + refs[pos + t][...].astype(F32)
            ct_vals.append(acc)
            pos += size
        res_refs = refs[pos:pos + n_res]
        out_refs = refs[pos + n_res + n_after:]
        _, pullback = jax.vjp(fn, *vals)
        grads = pullback(tuple(ct_vals))
        r_i = 0
        for o_ref, g in zip(out_refs[:n_rg], row_grads):
            val = grads[g[0]]
            if g[6] is not None:
                val = val + res_refs[r_i][...].astype(F32)
                r_i += 1
            o_ref[...] = val.astype(o_ref.dtype)
        first = (pl.program_id(1) == 0)
        for o_ref, g in zip(out_refs[n_rg:], vec_grads):
            val = jnp.sum(grads[n_r + g[0]], axis=0, keepdims=True)
            init = first if g[4] else jnp.logical_and(first, pl.program_id(0) == 0)

            @pl.when(init)
            def _(o_ref=o_ref, val=val):
                o_ref[...] = val

            @pl.when(jnp.logical_not(init))
            def _(o_ref=o_ref, val=val):
                o_ref[...] += val

    in_specs = [_row_spec(tr, bw, cb, pg) for _, bw, cb, pg in rows] + [_vec_spec(bw, cb, pg) for _, bw, cb, pg in vecs]
    in_specs += [_row_spec(tr, bw, cb, pg) for _, bw, cb, pg in ct_ops + res_ops] + [pl.BlockSpec(memory_space=pl.ANY)] * n_after
    out_specs =[_row_spec(tr, g[3], g[4], g[5]) for g in row_grads] + [_vec_spec(g[2], g[3], g[4]) for g in vec_grads]
    out_shape = [jax.ShapeDtypeStruct((n_rows, g[1]), g[2]) for g in row_grads]
    out_shape += [jax.ShapeDtypeStruct((1, g[1]), F32) for g in vec_grads]
    return pl.pallas_call(
        body, name=name, grid=(groups, n_rows // tr),
        in_specs=in_specs, out_specs=out_specs, out_shape=out_shape,
        compiler_params=_params(("arbitrary", "arbitrary")),
    )(*[r[0] for r in rows], *[v[0] for v in vecs], *[c[0] for c in ct_ops], *[r[0] for r in res_ops], *after)


def _full(arr, width=None):
    return (arr, arr.shape[1] if width is None else width, 0, False)


def _make_xor(sh):
    def raw(x):
        n = x.shape[-1]
        lane = lax.broadcasted_iota(jnp.int32, x.shape, x.ndim - 1)
        up = pltpu.roll(x, n - sh, x.ndim - 1)
        down = pltpu.roll(x, sh, x.ndim - 1)
        return jnp.where((lane & sh) == 0, up, down)

    f = jax.custom_vjp(raw)
    f.defvjp(lambda x: (raw(x), None), lambda _, ct: (raw(ct),))
    return f


_SWAP_ROPE_HALVES = _make_xor(ROT // 2)


def _head_sum(x):
    n = x.shape[-1]
    same_head = (lax.broadcasted_iota(jnp.int32, (n, n), 0) // HEAD) == (lax.broadcasted_iota(jnp.int32, (n, n), 1) // HEAD)
    return _fdot(x, same_head.astype(F32), NN)


def _rms(x, g):
    return x * lax.rsqrt(jnp.mean(x * x, axis=-1, keepdims=True) + EPS) * g


def _head_rms_rope(x, g, cos, sin, scale):
    y = x * lax.rsqrt(_head_sum(x * x) * (1.0 / HEAD) + EPS) * g
    return (y * cos + _SWAP_ROPE_HALVES(y) * sin) * scale


def _qk_fn(q, k, v, cos, sin, gq, gk):
    return (_head_rms_rope(q, gq, cos, sin, HEAD ** -0.5), _head_rms_rope(k, gk, cos, sin, 1.0), v)


def _norm_fn(x, g):
    return (_rms(x, g),)


def _merge_fn(o0, o1, o2, l0, l1, l2, g):
    m = lax.stop_gradient(jnp.maximum(jnp.maximum(l0, l1), l2))
    e0, e1, e2 = jnp.exp(l0 - m), jnp.exp(l1 - m), jnp.exp(l2 - m)
    mix = (e0 * o0 + e1 * o1 + e2 * o2) / (e0 + e1 + e2)
    return (_rms(mix, g),)


def _gate_fn(y, z, g):
    return (_rms(y * (z * jax.nn.sigmoid(z)), g),)


def _attn_pair(q, kc, vc, kp=None, vp=None, has_prev=None):
    pick0, pick1 = _head_picks()
    k_band, v_band, mask = _attn_band(kc, vc, kp, vp, has_prev)
    s = jnp.where(mask, _bdot(jnp.concatenate([q * pick0, q * pick1], axis=0), k_band, NT), NEG)
    m = jnp.max(s, axis=-1, keepdims=True)
    p = jnp.exp(s - m)
    den = jnp.sum(p, axis=-1, keepdims=True)
    acc = _bdot(p, v_band, NN) * (1.0 / den)
    lse_rows = m + jnp.log(den)
    o = pick0 * acc[:ATT_BLK] + pick1 * acc[ATT_BLK:]
    lse = pick0 * lse_rows[:ATT_BLK] + pick1 * lse_rows[ATT_BLK:]
    return o, lse


def _head_picks():
    lane = lax.broadcasted_iota(jnp.int32, (1, 2 * HEAD), 1)
    return (lane < HEAD).astype(F32), (lane >= HEAD).astype(F32)


def _attn_band(kc, vc, kp, vp, has_prev):
    n_keys = ATT_BLK if kp is None else 2 * ATT_BLK
    qi = lax.broadcasted_iota(jnp.int32, (2 * ATT_BLK, n_keys), 0) & (ATT_BLK - 1)
    kj = lax.broadcasted_iota(jnp.int32, (2 * ATT_BLK, n_keys), 1)
    if kp is None:
        return kc, vc, qi >= kj
    in_prev = jnp.logical_and(jnp.logical_and(kj < ATT_BLK, kj >= qi), has_prev)
    mask = jnp.logical_or(in_prev, jnp.logical_and(kj >= ATT_BLK, qi >= kj - ATT_BLK))
    return jnp.concatenate([kp, kc], axis=0), jnp.concatenate([vp, vc], axis=0), mask


def _attn_config(b):
    r = DILATIONS[b]
    return r, ATT_BLK * r, (512 if r == 1 else 128), BRANCH_BLOCKS[b] > 1


def _for_residues(r, fn):
    if r <= 4:
        for rho in range(r):
            fn(rho)
    else:
        def step(t, carry):
            for u in range(4):
                fn(4 * t + u)
            return carry

        lax.fori_loop(0, r // 4, step, 0)


def _strided_rows(start, r):
    if r > 1:
        return pl.ds(start, ATT_BLK, stride=r)
    return pl.ds(start if isinstance(start, int) else pl.multiple_of(start, ATT_BLK), ATT_BLK)


def _attention_fwd(qn, kn, vn, b):
    r, rows, lanes, with_prev = _attn_config(b)
    cur = pl.BlockSpec((rows, lanes), lambda g, n: (n, g))
    prev = pl.BlockSpec((rows, lanes), lambda g, n: (jnp.maximum(n - 1, 0), g))

    def body(*refs):
        ins, (o_ref, l_ref) = refs[:-2], refs[-2:]
        has_prev = pl.program_id(1) > 0

        def one(rho):
            sub = _strided_rows(rho, r)
            for pair in range(lanes // 128):
                sl = pl.ds(pair * 128, 128)
                args = [ref[sub, sl] for ref in ins] + ([has_prev] if with_prev else [])
                o_ref[sub, sl], l_ref[sub, sl] = _attn_pair(*args)

        _for_residues(r, one)

    operands = (qn, kn, vn, kn, vn) if with_prev else (qn, kn, vn)
    return pl.pallas_call(
        body, name="attn_fwd_%d" % r, grid=(D_ATTN // lanes, SEQ // rows),
        in_specs=[cur, cur, cur] + ([prev, prev] if with_prev else []), out_specs=[cur, cur],
        out_shape=[jax.ShapeDtypeStruct((SEQ, D_ATTN), F32)] * 2,
        compiler_params=_params(("parallel", "parallel")),
    )(*operands)


def _attn_pair_bwd(q, kc, vc, kp, vp, o, lse, do, dl, has_prev):
    pick0, pick1 = _head_picks()
    lane = lax.broadcasted_iota(jnp.int32, (1, 2 * HEAD), 1)
    k_band, v_band, mask = _attn_band(kc, vc, kp, vp, has_prev)
    q2 = jnp.concatenate([q * pick0, q * pick1], axis=0)
    do2 = jnp.concatenate([do * pick0, do * pick1], axis=0)
    lse2 = jnp.concatenate([jnp.sum(lse * (lane == 0).astype(F32), axis=-1, keepdims=True),
                            jnp.sum(lse * (lane == HEAD).astype(F32), axis=-1, keepdims=True)], axis=0)
    base = jnp.sum(jnp.concatenate([dl * pick0, dl * pick1], axis=0) - do2 * jnp.concatenate([o, o], axis=0),
                   axis=-1, keepdims=True)
    p = jnp.exp(jnp.where(mask, _bdot(q2, k_band, NT), NEG) - lse2)
    ds = p * (_bdot(do2, v_band, NT) + base)
    dq2 = _bdot(ds, k_band, NN)
    dq = pick0 * dq2[:ATT_BLK] + pick1 * dq2[ATT_BLK:]
    dk, dv = _bdot(ds, q2, TN), _bdot(p, do2, TN)
    if kp is None:
        return dq, dk, dv
    return dq, dk[ATT_BLK:], dv[ATT_BLK:], dk[:ATT_BLK], dv[:ATT_BLK]


def _attention_bwd(qn, kn, vn, o, lse, do, dl, b):
    r, rows, lanes, with_prev = _attn_config(b)
    cur = pl.BlockSpec((rows, lanes), lambda g, n: (n, g))
    prev = pl.BlockSpec((rows, lanes), lambda g, n: (jnp.maximum(n - 1, 0), g))
    whole = pl.BlockSpec((SEQ, lanes), lambda g, n: (0, g))
    n_in = 5 if with_prev else 3

    def body(*refs):
        ins, (o_ref, l_ref, do_ref, dl_ref, dq_ref, dk_ref, dv_ref) = refs[:n_in], refs[n_in:]
        n = pl.program_id(1)

        @pl.when(n == 0)
        def _():
            dk_ref[...] = jnp.zeros_like(dk_ref)
            dv_ref[...] = jnp.zeros_like(dv_ref)

        def one(rho):
            sub = _strided_rows(rho, r)
            sub_c = _strided_rows(n * rows + rho, r)
            sub_p = _strided_rows(jnp.maximum(n - 1, 0) * rows + rho, r)
            for pair in range(lanes // 128):
                sl = pl.ds(pair * 128, 128)
                vals = [ref[sub, sl] for ref in ins] + ([] if with_prev else [None, None])
                grads = _attn_pair_bwd(*vals, o_ref[sub, sl], l_ref[sub, sl], do_ref[sub, sl], dl_ref[sub, sl], n > 0)
                dq_ref[sub, sl] = grads[0]
                dk_ref[sub_c, sl] += grads[1]
                dv_ref[sub_c, sl] += grads[2]
                if with_prev:
                    dk_ref[sub_p, sl] += grads[3]
                    dv_ref[sub_p, sl] += grads[4]

        _for_residues(r, one)

    operands = (qn, kn, vn, kn, vn) if with_prev else (qn, kn, vn)
    return pl.pallas_call(
        body, name="attn_bwd_%d" % r, grid=(D_ATTN // lanes, SEQ // rows),
        in_specs=[cur, cur, cur] + ([prev, prev] if with_prev else []) + [cur] * 4, out_specs=[cur, whole, whole],
        out_shape=[jax.ShapeDtypeStruct((SEQ, D_ATTN), F32)] * 3,
        compiler_params=_params(("parallel", "arbitrary")),
    )(*operands, o, lse, do, dl)


CONV_COLS = 256
XBC_BLOCK0 = 4096 // CONV_COLS


def _shift_rows(x, s):
    n = x.shape[0]
    t = lax.broadcasted_iota(jnp.int32, x.shape, 0)
    if s >= 0:
        return jnp.where(t >= s, pltpu.roll(x, s, 0), 0.0)
    return jnp.where(t < n + s, pltpu.roll(x, n + s, 0), 0.0)


def _conv_pre(x, w_ref, b_ref):
    pre = b_ref[...] + w_ref[3:4, :] * x
    for k in range(3):
        pre = pre + w_ref[k:k + 1, :] * _shift_rows(x, 3 - k)
    return pre


def _conv_fwd(proj, conv_w, conv_b):
    cols = conv_w.shape[1]

    def body(x_ref, w_ref, b_ref, o_ref):
        pre = _conv_pre(x_ref[...], w_ref, b_ref)
        o_ref[...] = pre * jax.nn.sigmoid(pre)

    blk = pl.BlockSpec((SEQ, CONV_COLS), lambda j: (0, j))
    return pl.pallas_call(
        body, name="conv_fwd", grid=(cols // CONV_COLS,),
        in_specs=[pl.BlockSpec((SEQ, CONV_COLS), lambda j: (0, XBC_BLOCK0 + j)),
                  pl.BlockSpec((4, CONV_COLS), lambda j: (0, j)), pl.BlockSpec((1, CONV_COLS), lambda j: (0, j))],
        out_specs=blk, out_shape=jax.ShapeDtypeStruct((SEQ, cols), F32),
        compiler_params=_params(("parallel",)),
    )(proj, conv_w, conv_b)


def _conv_bwd(proj, conv_w, conv_b, dxs, db, dc):
    cols = conv_w.shape[1]
    x_blocks, b_blocks = dxs.shape[1] // CONV_COLS, db.shape[1] // CONV_COLS

    def body(x_ref, w_ref, b_ref, dxs_ref, db_ref_in, dc_ref_in, dx_ref, dw_ref, db_ref):
        j = pl.program_id(0)
        dy = jnp.where(j < x_blocks, dxs_ref[...], jnp.where(j < x_blocks + b_blocks, db_ref_in[...], dc_ref_in[...]))
        x = x_ref[...]
        pre = _conv_pre(x, w_ref, b_ref)
        sg = jax.nn.sigmoid(pre)
        dpre = dy * (sg * (1.0 + pre * (1.0 - sg)))
        db_ref[...] = jnp.sum(dpre, axis=0, keepdims=True)
        dx = w_ref[3:4, :] * dpre
        dw_ref[3:4, :] = jnp.sum(dpre * x, axis=0, keepdims=True)
        for k in range(3):
            dx = dx + w_ref[k:k + 1, :] * _shift_rows(dpre, k - 3)
            dw_ref[k:k + 1, :] = jnp.sum(dpre * _shift_rows(x, 3 - k), axis=0, keepdims=True)
        dw_ref[4:8, :] = jnp.zeros((4, CONV_COLS), F32)
        dx_ref[...] = dx.astype(dx_ref.dtype)

    blk = pl.BlockSpec((SEQ, CONV_COLS), lambda j: (0, j))
    parts = [pl.BlockSpec((SEQ, CONV_COLS), lambda j: (0, jnp.minimum(j, x_blocks - 1))),
             pl.BlockSpec((SEQ, CONV_COLS), lambda j: (0, jnp.clip(j - x_blocks, 0, b_blocks - 1))),
             pl.BlockSpec((SEQ, CONV_COLS), lambda j: (0, jnp.clip(j - x_blocks - b_blocks, 0, b_blocks - 1)))]
    return pl.pallas_call(
        body, name="conv_bwd", grid=(cols // CONV_COLS,),
        in_specs=[pl.BlockSpec((SEQ, CONV_COLS), lambda j: (0, XBC_BLOCK0 + j)),
                  pl.BlockSpec((4, CONV_COLS), lambda j: (0, j)), pl.BlockSpec((1, CONV_COLS), lambda j: (0, j))] + parts,
        out_specs=[blk, pl.BlockSpec((8, CONV_COLS), lambda j: (0, j)), pl.BlockSpec((1, CONV_COLS), lambda j: (0, j))],
        out_shape=[jax.ShapeDtypeStruct((SEQ, cols), BF16), jax.ShapeDtypeStruct((8, cols), F32),
                   jax.ShapeDtypeStruct((1, cols), F32)],
        compiler_params=_params(("parallel",)),
    )(proj, conv_w, conv_b, dxs, db, dc)


HEADS_PER_GROUP = 4


GROUP_WIDTH = HEADS_PER_GROUP * HEAD


def _ssd_chunk(x, bm, cm, dtr, bias, alog, dsk, h):
    row = lax.broadcasted_iota(jnp.int32, (CHUNK, CHUNK), 0)
    col = lax.broadcasted_iota(jnp.int32, (CHUNK, CHUNK), 1)
    causal = row >= col
    z = dtr + bias
    dt = jnp.maximum(z, 0.0) + jnp.log(1.0 + jnp.exp(-jnp.abs(z)))
    acs = _fdot(causal.astype(F32), dt * -jnp.exp(alog), NN)
    acs_t, dt_t = acs.T, dt.T
    cb = _bdot(cm, bm, NT)
    lane = lax.broadcasted_iota(jnp.int32, (1, CHUNK), 1)
    sub = lax.broadcasted_iota(jnp.int32, (CHUNK, 1), 0)
    wide = lax.broadcasted_iota(jnp.int32, (1, GROUP_WIDTH), 1) // HEAD
    tall = lax.broadcasted_iota(jnp.int32, (GROUP_WIDTH, 1), 0) // HEAD
    acs_last = jnp.sum(acs * (sub == CHUNK - 1).astype(F32), axis=0, keepdims=True)
    to_lanes = (lax.broadcasted_iota(jnp.int32, (CHUNK, GROUP_WIDTH), 0)
                == lax.broadcasted_iota(jnp.int32, (CHUNK, GROUP_WIDTH), 1) // HEAD).astype(F32)
    grow = _fdot(jnp.exp(acs), to_lanes, NN)
    keep = _fdot(jnp.exp(acs_last - acs) * dt, to_lanes, NN)
    w_parts, x_parts, skip, carry = [], [], 0.0, 0.0
    for j in range(HEADS_PER_GROUP):
        on_lane, on_sub = (lane == j).astype(F32), (sub == j).astype(F32)
        acs_c = jnp.sum(acs * on_lane, axis=1, keepdims=True)
        acs_r = jnp.sum(acs_t * on_sub, axis=0, keepdims=True)
        dt_r = jnp.sum(dt_t * on_sub, axis=0, keepdims=True)
        w_parts.append(cb * jnp.exp(jnp.where(causal, acs_c - acs_r, NEG)) * dt_r)
        x_parts.append(x * (wide == j).astype(F32))
        skip = skip + jnp.sum(dsk * on_lane, axis=1, keepdims=True) * (wide == j).astype(F32)
        carry = carry + jnp.sum(jnp.exp(acs_last) * on_lane, axis=1, keepdims=True) * (tall == j).astype(F32)
    y_diag = _bdot(jnp.concatenate(w_parts, axis=1), jnp.concatenate(x_parts, axis=0), NN)
    y = y_diag + _bdot(cm, h, NT) * grow + skip * x
    return y, h * carry + _bdot(x * keep, bm, TN)


def _ssd_specs(reverse):
    n_chunks = SEQ // CHUNK
    c_of = (lambda c: n_chunks - 1 - c) if reverse else (lambda c: c)
    x_spec = pl.BlockSpec((CHUNK, 256), lambda g, c: (c_of(c), g))
    b_spec = pl.BlockSpec((CHUNK, N_STATE), lambda g, c: (c_of(c), 8 + g))
    c_spec = pl.BlockSpec((CHUNK, N_STATE), lambda g, c: (c_of(c), 12 + g))
    dt_spec = pl.BlockSpec((CHUNK, 128), lambda g, c: (c_of(c), g))
    vec_spec = pl.BlockSpec((1, 128), lambda g, c: (0, g))
    h_spec = pl.BlockSpec((None, None, GROUP_WIDTH, N_STATE), lambda g, c: (c_of(c), g, 0, 0))
    return x_spec, b_spec, c_spec, dt_spec, vec_spec, h_spec


def _ssd_fwd(xbc, dt_raw, bias, alog, dsk):
    x_spec, b_spec, c_spec, dt_spec, vec_spec, h_spec = _ssd_specs(False)

    def body(x_ref, b_ref, c_ref, dt_ref, bias_ref, alog_ref, dsk_ref, y_ref, hin_ref, h_scr):
        @pl.when(pl.program_id(1) == 0)
        def _():
            h_scr[...] = jnp.zeros_like(h_scr)

        h = h_scr[...]
        hin_ref[...] = h
        y_ref[...], h_scr[...] = _ssd_chunk(x_ref[...], b_ref[...], c_ref[...], dt_ref[...], bias_ref[...], alog_ref[...],
                                            dsk_ref[...], h)

    return pl.pallas_call(
        body, name="ssd_fwd", grid=(N_GROUPS, SEQ // CHUNK),
        in_specs=[x_spec, b_spec, c_spec, dt_spec, vec_spec, vec_spec, vec_spec],
        out_specs=[x_spec, h_spec],
        out_shape=[jax.ShapeDtypeStruct((SEQ, D_SSM), F32),
                   jax.ShapeDtypeStruct((SEQ // CHUNK, N_GROUPS, GROUP_WIDTH, N_STATE), F32)],
        scratch_shapes=[pltpu.VMEM((GROUP_WIDTH, N_STATE), F32)],
        compiler_params=_params(("parallel", "arbitrary")),
    )(xbc, xbc, xbc, dt_raw, bias, alog, dsk)


def _ssd_bwd(xbc, dt_raw, bias, alog, dsk, h_in, dy):
    x_spec, b_spec, c_spec, dt_spec, vec_spec, h_spec = _ssd_specs(True)
    dxbc_x = pl.BlockSpec((CHUNK, 256), x_spec.index_map)

    def body(x_ref, b_ref, c_ref, dt_ref, bias_ref, alog_ref, dsk_ref, hin_ref, dy_ref,
             dx_ref, db_ref, dc_ref, ddt_ref, dbias_ref, dalog_ref, ddsk_ref, dh_scr):
        first = pl.program_id(1) == 0

        @pl.when(first)
        def _():
            dh_scr[...] = jnp.zeros_like(dh_scr)

        _, pullback = jax.vjp(_ssd_chunk, x_ref[...], b_ref[...], c_ref[...], dt_ref[...], bias_ref[...], alog_ref[...],
                              dsk_ref[...], hin_ref[...])
        g = pullback((dy_ref[...], dh_scr[...]))
        dx_ref[...], db_ref[...], dc_ref[...] = g[0], g[1], g[2]
        ddt_ref[...] = g[3].astype(ddt_ref.dtype)
        dh_scr[...] = g[7]
        for o_ref, val in ((dbias_ref, g[4]), (dalog_ref, g[5]), (ddsk_ref, g[6])):
            @pl.when(first)
            def _(o_ref=o_ref, val=val):
                o_ref[...] = val

            @pl.when(jnp.logical_not(first))
            def _(o_ref=o_ref, val=val):
                o_ref[...] += val

    n_chunks = SEQ // CHUNK
    out_b = pl.BlockSpec((CHUNK, N_STATE), lambda g, c: (n_chunks - 1 - c, g))
    res = pl.pallas_call(
        body, name="ssd_bwd", grid=(N_GROUPS, n_chunks),
        in_specs=[x_spec, b_spec, c_spec, dt_spec, vec_spec, vec_spec, vec_spec, h_spec, x_spec],
        out_specs=[dxbc_x, out_b, out_b, dt_spec, vec_spec, vec_spec, vec_spec],
        out_shape=[jax.ShapeDtypeStruct((SEQ, D_SSM), F32), jax.ShapeDtypeStruct((SEQ, N_GROUPS * N_STATE), F32),
                   jax.ShapeDtypeStruct((SEQ, N_GROUPS * N_STATE), F32), jax.ShapeDtypeStruct((SEQ, DT_PAD), BF16),
                   jax.ShapeDtypeStruct((1, DT_PAD), F32), jax.ShapeDtypeStruct((1, DT_PAD), F32),
                   jax.ShapeDtypeStruct((1, DT_PAD), F32)],
        scratch_shapes=[pltpu.VMEM((GROUP_WIDTH, N_STATE), F32)],
        compiler_params=_params(("parallel", "arbitrary")),
    )(xbc, xbc, xbc, dt_raw, bias, alog, dsk, h_in, dy)
    return res


CROSS_HEAD = 128
CROSS_ROWS = 512


def _cross_head(q, k, v, gq, gk):
    qn = _rms(q, gq) * (CROSS_HEAD ** -0.5)
    kn = _rms(k, gk)
    s = _bdot(qn, kn, NT)
    p = jnp.exp(s - lax.stop_gradient(jnp.max(s, axis=-1, keepdims=True)))
    return _bdot(p, v, NN) * (1.0 / jnp.sum(p, axis=-1, keepdims=True))


def _cross_specs():
    q_spec = pl.BlockSpec((CROSS_ROWS, CROSS_HEAD), lambda h, i: (i, h))
    k_spec = pl.BlockSpec((N_MEM, CROSS_HEAD), lambda h, i: (0, h))
    v_spec = pl.BlockSpec((N_MEM, CROSS_HEAD), lambda h, i: (0, 4 + h))
    g_spec = pl.BlockSpec((1, CROSS_HEAD), lambda h, i: (0, 0))
    return q_spec, k_spec, v_spec, g_spec


def _cross_fwd(qc, kv, gq, gk):
    q_spec, k_spec, v_spec, g_spec = _cross_specs()

    def body(q_ref, k_ref, v_ref, gq_ref, gk_ref, o_ref):
        o_ref[...] = _cross_head(q_ref[...], k_ref[...], v_ref[...], gq_ref[...], gk_ref[...]).astype(o_ref.dtype)

    return pl.pallas_call(
        body, name="cross_fwd", grid=(4, SEQ // CROSS_ROWS),
        in_specs=[q_spec, k_spec, v_spec, g_spec, g_spec], out_specs=q_spec,
        out_shape=jax.ShapeDtypeStruct((SEQ, D_CROSS), BF16),
        compiler_params=_params(("parallel", "parallel")),
    )(qc, kv, kv, gq, gk)


def _cross_bwd(qc, kv, gq, gk, do):
    q_spec, k_spec, v_spec, g_spec = _cross_specs()

    def body(q_ref, k_ref, v_ref, gq_ref, gk_ref, do_ref, dq_ref, dk_ref, dv_ref, dgq_ref, dgk_ref):
        _, pullback = jax.vjp(_cross_head, q_ref[...], k_ref[...], v_ref[...], gq_ref[...], gk_ref[...])
        dq, dk, dv, dgq, dgk = pullback(do_ref[...].astype(F32))
        dq_ref[...] = dq.astype(dq_ref.dtype)
        row0 = pl.program_id(1) == 0
        all0 = jnp.logical_and(row0, pl.program_id(0) == 0)
        for o_ref, val, init in ((dk_ref, dk, row0), (dv_ref, dv, row0), (dgq_ref, dgq, all0), (dgk_ref, dgk, all0)):
            @pl.when(init)
            def _(o_ref=o_ref, val=val):
                o_ref[...] = val

            @pl.when(jnp.logical_not(init))
            def _(o_ref=o_ref, val=val):
                o_ref[...] += val

    return pl.pallas_call(
        body, name="cross_bwd", grid=(4, SEQ // CROSS_ROWS),
        in_specs=[q_spec, k_spec, v_spec, g_spec, g_spec, q_spec],
        out_specs=[q_spec, k_spec, k_spec, g_spec, g_spec],
        out_shape=[jax.ShapeDtypeStruct((SEQ, D_CROSS), BF16), jax.ShapeDtypeStruct((N_MEM, D_CROSS), F32),
                   jax.ShapeDtypeStruct((N_MEM, D_CROSS), F32), jax.ShapeDtypeStruct((1, CROSS_HEAD), F32),
                   jax.ShapeDtypeStruct((1, CROSS_HEAD), F32)],
        compiler_params=_params(("arbitrary", "arbitrary")),
    )(qc, kv, kv, gq, gk, do)


def _loss_epilogue(acc, residual, target):
    err = acc + residual - target
    dy = err * (1.0 / D_MODEL)
    part = jnp.sum(jnp.sum(err * err, axis=1, keepdims=True), axis=0, keepdims=True) * (0.5 / D_MODEL)
    return dy, dy, part


def _pad_heads(v):
    return jnp.pad(v.reshape(N_GROUPS, HEADS_PER_GROUP), ((0, 0), (0, 128 - HEADS_PER_GROUP))).reshape(1, DT_PAD)


def _unpad_heads(v):
    return v.reshape(v.shape[0], N_GROUPS, 128)[:, :, :HEADS_PER_GROUP].reshape(v.shape[0], N_DT)


def _rope_tables(positions):
    half = ROT // 2
    inv_freq = ROPE_THETA ** (-2.0 * jnp.arange(half, dtype=F32) / ROT)
    ang = positions.reshape(SEQ, 1).astype(F32) * inv_freq
    cos, sin = jnp.cos(ang), jnp.sin(ang)
    ones, zeros = jnp.ones((SEQ, HEAD - ROT), F32), jnp.zeros((SEQ, HEAD - ROT), F32)
    cos_h = jnp.concatenate([cos, cos, ones], axis=1)
    sin_h = jnp.concatenate([-sin, sin, zeros], axis=1)
    return jnp.tile(cos_h, (1, 2)), jnp.tile(sin_h, (1, 2))


def _add_res(acc, res):
    return (acc + res,)


def _settle(grads, *after):
    if hasattr(grads, "settle"):
        grads.settle(*after)


def _take_token(grads):
    token = getattr(grads, "token", None)
    if token is None:
        return ()
    grads.token = None
    return (token,)


def _local_step(x, mem, positions, target, p, w, more_weights=None, grads=None, h=None):
    grads = {} if grads is None else grads
    w = dict(w)
    cos, sin = _rope_tables(positions)
    gq2, gk2 = jnp.tile(p["g_q"], (1, 2)), jnp.tile(p["g_k"], (1, 2))
    bias, alog, dsk = _pad_heads(p["dt_bias"]), _pad_heads(p["a_log"]), _pad_heads(p["d_skip"])
    norm_out = [(D_MODEL, BF16, D_MODEL, 0, False)]

    if h is None:
        h = _rowwise(_norm_fn, [_full(x)], [_full(p["g_mix"])], norm_out, name="norm_in")[0]
    proj = _matmul(h, w["w_in"], mode="nn", name="in_proj", outs=[F32], n_cols=D_MAIN)
    dt_raw = _matmul(h, w["w_dt"], mode="nn", name="dt_proj", outs=[F32])
    qk_rows = [(proj, 128, 0, True), (proj, 128, 8, True), (proj, 128, 16, True), _full(cos), _full(sin)]
    qk_vecs = [_full(gq2), _full(gk2)]
    qn, kn, vn = _rowwise(_qk_fn, qk_rows, qk_vecs, [(D_ATTN, F32, 128, 0, True)] * 3, name="qk_prep", groups=8, tr=1024)
    branches = [_attention_fwd(qn, kn, vn, b) for b in range(3)]
    merge_rows = [_full(o) for o, _ in branches] + [_full(lse) for _, lse in branches]
    attn = _rowwise(_merge_fn, merge_rows, [_full(p["g_attn_out"])], [(D_ATTN, BF16, D_ATTN, 0, False)], name="attn_merge")[0]
    xbc = _conv_fwd(proj, p["conv_w"], p["conv_b"])
    y_ssd, h_in = _ssd_fwd(xbc, dt_raw, bias, alog, dsk)
    gate_rows = [(y_ssd, 256, 0, True), (proj, 256, 12, True)]
    gate_vecs = [(p["g_ssm_out"], 256, 0, True)]
    ssm = _rowwise(_gate_fn, gate_rows, gate_vecs, [(D_SSM, BF16, 256, 0, True)], name="ssm_gate", groups=4)[0]
    mix = jnp.concatenate([attn, ssm], axis=1)
    if more_weights is not None:
        w.update(more_weights("mixer_done", mix))
    x1 = _matmul(mix, w["w_out"], mode="nn", name="out_proj", outs=[F32], extra=(x,), epilogue=_add_res)
    hc = _rowwise(_norm_fn, [_full(x1)], [_full(p["g_cross"])], norm_out, name="norm_cross")[0]
    memh = _rowwise(_norm_fn, [_full(mem)], [_full(p["g_mem"])], norm_out, name="norm_mem", n_rows=N_MEM)[0]
    qc = _matmul(hc, w["w_cq"], mode="nn", name="cq_proj", outs=[F32])
    kv = _matmul(memh, w["w_ckv"], mode="nn", name="ckv_proj", outs=[F32])
    oc = _cross_fwd(qc, kv, p["g_cq"], p["g_ck"])
    x2 = _matmul(oc, w["w_co"], mode="nn", name="co_proj", outs=[F32], extra=(x1,), epilogue=_add_res)
    hm = _rowwise(_norm_fn, [_full(x2)], [_full(p["g_mlp"])], norm_out, name="norm_mlp")[0]
    if more_weights is not None:
        w.update(more_weights("cross_done", hm))
    u, act = _matmul(hm, w["w_up"], mode="nn", name="up_proj", outs=[F32, BF16],
                     epilogue=lambda acc: (acc, jnp.square(jnp.maximum(acc, 0.0))))
    dy, dyb, loss_tiles = _matmul(act, w["w_down"], mode="nn", name="down_proj", outs=[F32, BF16], extra=(x2, target),
                                  epilogue=_loss_epilogue, tile_sums=1)
    loss = jnp.sum(loss_tiles).reshape(1, 1)

    grads["w_down"] = _matmul(act, dyb, mode="tn", name="dw_down", outs=[BF16], after=_take_token(grads))
    du = _matmul(dyb, w["w_down"], mode="nt", name="d_act", outs=[BF16], extra=(u,), after=_take_token(grads),
                 epilogue=lambda acc, uu: (acc * (2.0 * jnp.maximum(uu, 0.0)),))
    _settle(grads, du)
    grads["w_up"] = _matmul(hm, du, mode="tn", name="dw_up", outs=[BF16], col_shards=4, after=_take_token(grads))
    dhm = _matmul(du, w["w_up"], mode="nt", name="d_hm", outs=[F32], after=_take_token(grads))
    _settle(grads, dhm)
    dx2, grads["g_mlp"] = _rowwise_vjp(
        _norm_fn, [_full(x2)], [_full(p["g_mlp"])], [[_full(dhm)]],
        [(0, D_MODEL, F32, D_MODEL, 0, False, _full(dy))], [(0, D_MODEL, D_MODEL, 0, False)], name="norm_mlp_bwd")
    grads["w_co"] = _matmul(oc, dx2, mode="tn", name="dw_co", outs=[BF16], col_shards=4, after=_take_token(grads))
    doc = _matmul(dx2, w["w_co"], mode="nt", name="d_oc", outs=[BF16])
    dqc, dkc, dvc, grads["g_cq"], grads["g_ck"] = _cross_bwd(qc, kv, p["g_cq"], p["g_ck"], doc)
    grads["w_cq"] = _matmul(hc, dqc, mode="tn", name="dw_cq", outs=[BF16])
    dhc = _matmul(dqc, w["w_cq"], mode="nt", name="d_hc", outs=[F32])
    dkv = jnp.concatenate([dkc, dvc], axis=1)
    grads["w_ckv"] = _matmul(memh, dkv, mode="tn", name="dw_ckv", outs=[BF16])
    dmemh = _matmul(dkv, w["w_ckv"], mode="nt", name="d_memh", outs=[F32])
    grads["g_mem"] = _rowwise_vjp(_norm_fn, [_full(mem)], [_full(p["g_mem"])], [[_full(dmemh)]], [],
                                  [(0, D_MODEL, D_MODEL, 0, False)], name="norm_mem_bwd", n_rows=N_MEM)[0]
    dx1, grads["g_cross"] = _rowwise_vjp(
        _norm_fn, [_full(x1)], [_full(p["g_cross"])], [[_full(dhc)]],
        [(0, D_MODEL, F32, D_MODEL, 0, False, _full(dx2))], [(0, D_MODEL, D_MODEL, 0, False)], name="norm_cross_bwd")
    grads["w_out"] = _matmul(mix, dx1, mode="tn", name="dw_out", outs=[BF16])
    dmix = _matmul(dx1, w["w_out"], mode="nt", name="d_mix", outs=[F32], after=_take_token(grads))
    _settle(grads, dmix)
    merge_grads = [(i, D_ATTN, F32, D_ATTN, 0, False, None) for i in range(6)]
    *dol, grads["g_attn_out"] = _rowwise_vjp(
        _merge_fn, merge_rows, [_full(p["g_attn_out"])], [[(dmix, D_ATTN, 0, False)]],
        merge_grads, [(0, D_ATTN, D_ATTN, 0, False)], name="attn_merge_bwd", after=_take_token(grads))
    dqkv = [_attention_bwd(qn, kn, vn, *branches[b], dol[b], dol[3 + b], b) for b in range(3)]
    qk_cts = [[(dqkv[b][i], 128, 0, True) for b in range(3)] for i in range(3)]
    dq, dk, dv, dgq2, dgk2 = _rowwise_vjp(
        _qk_fn, qk_rows, qk_vecs, qk_cts, [(i, D_ATTN, BF16, 128, 0, True, None) for i in range(3)],
        [(0, 128, 128, 0, False), (1, 128, 128, 0, False)], name="qk_prep_bwd", groups=8, tr=512)
    grads["g_q"] = dgq2[:, :HEAD] + dgq2[:, HEAD:]
    grads["g_k"] = dgk2[:, :HEAD] + dgk2[:, HEAD:]
    dy_ssd, dz, grads["g_ssm_out"] = _rowwise_vjp(
        _gate_fn, gate_rows, gate_vecs, [[(dmix, 256, 4, True)]],
        [(0, D_SSM, F32, 256, 0, True, None), (1, D_SSM, BF16, 256, 0, True, None)],
        [(0, D_SSM, 256, 0, True)], name="ssm_gate_bwd", groups=4)
    dxs, db, dc, ddt, dbias, dalog, ddsk = _ssd_bwd(xbc, dt_raw, bias, alog, dsk, h_in, dy_ssd)
    grads["dt_bias"], grads["a_log"], grads["d_skip"] = _unpad_heads(dbias), _unpad_heads(dalog), _unpad_heads(ddsk)
    dxbc_raw, dconv_w, grads["conv_b"] = _conv_bwd(proj, p["conv_w"], p["conv_b"], dxs, db, dc)
    grads["conv_w"] = dconv_w[:4]
    dproj = jnp.concatenate([dq, dk, dv, dz, dxbc_raw], axis=1)
    grads["w_main"] = _matmul(h, dproj, mode="tn", name="dw_main", outs=[BF16], out_cols=D_MAIN + N_DT)
    grads["w_dt"] = _matmul(h, ddt, mode="tn", name="dw_dt", outs=[BF16])
    dh = _matmul(dproj, w["w_in"], mode="nt", name="d_h_main", outs=[F32], after=_take_token(grads))
    dh = _matmul(ddt, w["w_dt"], mode="nt", name="d_h_dt", outs=[F32], extra=(dh,), epilogue=_add_res)
    grad_x, grads["g_mix"] = _rowwise_vjp(
        _norm_fn, [_full(x)], [_full(p["g_mix"])], [[_full(dh)]],
        [(0, D_MODEL, F32, D_MODEL, 0, False, _full(dx1))], [(0, D_MODEL, D_MODEL, 0, False)], name="norm_in_bwd")
    return loss, grad_x, grads


MATRICES = ("w_in", "w_out", "w_cq", "w_ckv", "w_co", "w_up", "w_down")
ROW_SHARDED = ("w_out", "w_cq", "w_ckv", "w_down")
N_CHIPS = 4
ANY = pl.BlockSpec(memory_space=pl.ANY)


def _place():
    return lax.axis_index("x"), lax.axis_index("y"), lax.axis_index("c")


def _other_chips(x, y):
    return [(1 - x, y), (x, 1 - y), (1 - x, 1 - y)]


def _remote(src, dst, send_sem, recv_sem, device):
    return pltpu.make_async_remote_copy(src_ref=src, dst_ref=dst, send_sem=send_sem, recv_sem=recv_sem,
                                        device_id=device, device_id_type=MESH)


def _gathered_shape(name, shard):
    rows, cols = shard.shape
    if name == "w_in":
        return (N_CHIPS, rows, cols)
    return (N_CHIPS * rows, cols) if name in ROW_SHARDED else (rows, N_CHIPS * cols)


def _shard_window(name, ref, rows, cols, chip, half):
    r0, nr = (0, rows) if half is None else (half * (rows // 2), rows // 2)
    if name == "w_in":
        return ref.at[chip, pl.ds(r0, nr), :]
    if name in ROW_SHARDED:
        return ref.at[pl.ds(chip * rows + r0, nr), :]
    return ref.at[pl.ds(r0, nr), pl.ds(pl.multiple_of(chip * cols, 128), cols)]


def _cast_into_gathered(w, name, chip, after=()):
    rows, cols = w.shape
    tr = _tile(rows, ROW_TILE)

    def body(chip_ref, w_ref, *rest):
        rest[-1][...] = w_ref[...].astype(BF16)

    if name == "w_in":
        out_spec = pl.BlockSpec((None, tr, cols), lambda i, chip_ref: (chip_ref[0], i, 0))
    elif name in ROW_SHARDED:
        out_spec = pl.BlockSpec((tr, cols), lambda i, chip_ref: (chip_ref[0] * (rows // tr) + i, 0))
    else:
        out_spec = pl.BlockSpec((tr, cols), lambda i, chip_ref: (i, chip_ref[0]))
    grid_spec = pltpu.PrefetchScalarGridSpec(
        num_scalar_prefetch=1, grid=(rows // tr,),
        in_specs=[pl.BlockSpec((tr, cols), lambda i, chip_ref: (i, 0))] + [pl.BlockSpec(memory_space=pl.ANY)] * len(after),
        out_specs=out_spec)
    return pl.pallas_call(body, name="cast_" + name, grid_spec=grid_spec,
                          out_shape=jax.ShapeDtypeStruct(_gathered_shape(name, w), BF16),
                          compiler_params=_params(("parallel",)))(chip.reshape(1).astype(jnp.int32), w, *after)


def _w_in_columns(arr, to_shards):
    rows, piece = D_MODEL, (D_MAIN + N_DT) // N_CHIPS
    tr = ROW_TILE

    def body(a_ref, o_ref):
        for j in range(N_CHIPS):
            if to_shards:
                o_ref[j] = a_ref[:, pl.ds(piece * j, piece)]
            else:
                o_ref[:, pl.ds(piece * j, piece)] = a_ref[j]

    pieces = pl.BlockSpec((N_CHIPS, tr, piece), lambda i: (0, i, 0))
    matrix = pl.BlockSpec((tr, N_CHIPS * piece), lambda i: (i, 0))
    out_dims = (N_CHIPS, rows, piece) if to_shards else (rows, N_CHIPS * piece)
    return pl.pallas_call(
        body, name="w_in_to_shards" if to_shards else "w_in_from_shards", grid=(rows // tr,),
        in_specs=[matrix if to_shards else pieces], out_specs=pieces if to_shards else matrix,
        out_shape=jax.ShapeDtypeStruct(out_dims, arr.dtype), compiler_params=_params(("parallel",)))(arr)


HBM = pl.BlockSpec(memory_space=pltpu.HBM)
SEM = pl.BlockSpec(memory_space=pltpu.SEMAPHORE)
EFFECT = pltpu.SideEffectType.DATAFLOW_SIDE_EFFECTING


def _split_start(name, bufs, plan, counts, after=()):
    n, n_g, n_after = len(bufs), len(counts), len(after)

    def body(*refs):
        ins, sems, token = refs[:n], refs[n + n_after:n + n_after + 2 * n_g], refs[-1]
        for g, copies in enumerate(plan(ins)):
            for i, (src, dst, device, _) in enumerate(copies):
                _remote(src, dst, sems[2 * g].at[i], sems[2 * g + 1].at[i], device).start()
        token[...] = jnp.zeros_like(token)

    sem_shapes = [pltpu.SemaphoreType.DMA((cnt,)) for cnt in counts for _ in range(2)]
    res = pl.pallas_call(
        body, name=name,
        out_shape=(*sem_shapes, *[pltpu.HBM(b.shape, b.dtype) for b in bufs], jax.ShapeDtypeStruct((8, 128), F32)),
        in_specs=(*(HBM,) * n, *(ANY,) * n_after),
        out_specs=(*(SEM,) * (2 * n_g), *(HBM,) * n, pl.BlockSpec(memory_space=pltpu.VMEM)),
        input_output_aliases={i: 2 * n_g + i for i in range(n)},
        compiler_params=pltpu.CompilerParams(has_side_effects=EFFECT),
    )(*[pltpu.with_memory_space_constraint(b, pltpu.HBM) for b in bufs], *after)
    sems = [(res[2 * g], res[2 * g + 1]) for g in range(n_g)]
    return sems, list(res[2 * n_g:2 * n_g + n]), res[-1]


def _split_wait(name, bufs, sems, plan, *after):
    n = len(bufs)

    def body(*refs):
        ins, send, recv = refs[:n], refs[n], refs[n + 1]
        (copies,) = plan(ins)
        for i, (src, _, device, landing) in enumerate(copies):
            cp = _remote(src, landing, send.at[i], recv.at[i], device)
            cp.wait_send()
            cp.wait_recv()

    res = pl.pallas_call(
        body, name=name, out_shape=tuple(pltpu.HBM(b.shape, b.dtype) for b in bufs),
        in_specs=(*(HBM,) * n, SEM, SEM, *(ANY,) * len(after)), out_specs=(HBM,) * n,
        input_output_aliases={i: i for i in range(n)},
        compiler_params=pltpu.CompilerParams(has_side_effects=EFFECT),
    )(*bufs, sems[0], sems[1], *after)
    return list(res)


def _ici_plan(names, shard_shapes):
    def plan(refs):
        x, y, c = _place()
        copies = []
        for ref, name in zip(refs, names):
            win = _shard_window(name, ref, *shard_shapes[name], 2 * x + y, c)
            for px, py in _other_chips(x, y):
                copies.append((win, win, (px, py, c), _shard_window(name, ref, *shard_shapes[name], 2 * px + py, c)))
        return [copies]
    return plan


def _pass_on_plan(names, shard_shapes):
    def plan(refs):
        x, y, c = _place()
        copies = []
        for ref, name in zip(refs, names):
            for px, py in _other_chips(x, y):
                win = _shard_window(name, ref, *shard_shapes[name], 2 * px + py, c)
                copies.append((win, win, (x, y, 1 - c), _shard_window(name, ref, *shard_shapes[name], 2 * px + py, 1 - c)))
        return [copies]
    return plan


def _swap_plan(n_pairs):
    def plan(refs):
        x, y, c = _place()
        return [[(src.at[:, 1 - c], dst, (x, y, 1 - c), dst) for src, dst in zip(refs[:n_pairs], refs[n_pairs:])]]
    return plan


def _share_plan(n_pairs):
    def plan(refs):
        x, y, c = _place()
        return [[(src, dst, (x, y, 1 - c), dst)] for src, dst in zip(refs[:n_pairs], refs[n_pairs:])]
    return plan


def _scatter_plan(n_pairs):
    def plan(refs):
        x, y, c = _place()
        copies = []
        for src, dst in zip(refs[:n_pairs], refs[n_pairs:]):
            for k, (px, py) in enumerate(_other_chips(x, y)):
                copies.append((src.at[2 * px + py], dst.at[k], (px, py, c), dst.at[k]))
        return [copies]
    return plan


def _sibling_swap(arrs, name):
    n = len(arrs)

    def body(*refs):
        ins, outs, send, recv = refs[:n], refs[n:2 * n], refs[2 * n], refs[2 * n + 1]
        x, y, c = _place()
        cps = [_remote(ins[w].at[:, 1 - c], outs[w], send.at[w], recv.at[w], (x, y, 1 - c)) for w in range(n)]
        for cp in cps:
            cp.start()
        for cp in cps:
            cp.wait()

    return pl.pallas_call(
        body, name=name, in_specs=[ANY] * n, out_specs=[ANY] * n,
        out_shape=[jax.ShapeDtypeStruct((a.shape[0],) + a.shape[2:], a.dtype) for a in arrs],
        scratch_shapes=[pltpu.SemaphoreType.DMA((n,))] * 2,
    )(*arrs)


def _small_allreduce(buf, name):
    rows = buf.shape[0]

    def body(x_ref, out_ref, all_ref, send_sems, recv_sems, local_sem):
        x, y, c = _place()
        me, sibling, chips = (x, y, c), (x, y, 1 - c), _other_chips(x, y)

        def block(px, py, pc):
            return all_ref.at[pl.ds((4 * px + 2 * py + pc) * rows, rows), :]

        def copy(k, blk, to, src=None):
            return _remote(block(*blk) if src is None else src, block(*blk), send_sems.at[k], recv_sems.at[k], to)

        own = pltpu.make_async_copy(x_ref, block(*me), local_sem)
        own.start()
        first = [copy(0, me, sibling, src=x_ref)] + [copy(1 + j, me, (*chip, c), src=x_ref) for j, chip in enumerate(chips)]
        for cp in first:
            cp.start()
        passed = [copy(4 + j, (*chip, c), sibling) for j, chip in enumerate(chips)]
        for j, chip in enumerate(chips):
            copy(1 + j, (*chip, c), me).wait_recv()
            passed[j].start()
        copy(0, sibling, me).wait_recv()
        for j, chip in enumerate(chips):
            copy(4 + j, (*chip, 1 - c), me).wait_recv()
        for cp in first + passed:
            cp.wait_send()
        own.wait()
        acc = all_ref[pl.ds(0, rows), :]
        for d in range(1, 8):
            acc = acc + all_ref[pl.ds(d * rows, rows), :]
        out_ref[...] = acc

    vmem = pl.BlockSpec(memory_space=pltpu.VMEM)
    return pl.pallas_call(
        body, name=name, in_specs=[vmem], out_specs=vmem,
        out_shape=jax.ShapeDtypeStruct(buf.shape, F32),
        scratch_shapes=[pltpu.VMEM((8 * rows, 128), F32), pltpu.SemaphoreType.DMA((7,)), pltpu.SemaphoreType.DMA((7,)),
                        pltpu.SemaphoreType.DMA],
    )(buf)


ROW_TILE = 256
BIG_ROW_TILE = 1024


def _add_halves(arr, recv, c, name):
    _, _, hr, cols = arr.shape
    tr = _tile(hr, BIG_ROW_TILE)

    def body(c_ref, a_ref, r_ref, o_ref):
        o_ref[...] = (a_ref[...].astype(F32) + r_ref[...].astype(F32)).astype(o_ref.dtype)

    piece = pl.BlockSpec((None, tr, cols), lambda j, i, c_ref: (j, i, 0))
    grid_spec = pltpu.PrefetchScalarGridSpec(
        num_scalar_prefetch=1, grid=(N_CHIPS, hr // tr),
        in_specs=[pl.BlockSpec((None, None, tr, cols), lambda j, i, c_ref: (j, c_ref[0], i, 0)), piece], out_specs=piece)
    return pl.pallas_call(body, name=name, grid_spec=grid_spec, out_shape=jax.ShapeDtypeStruct(recv.shape, BF16),
                          compiler_params=_params(("parallel", "parallel")))(c.reshape(1).astype(jnp.int32), arr, recv)


def _flip_slot(d):
    return jnp.where(d == 1, 1, jnp.where(d == 3, 2, 0))


def _sum_chips(p, q, chip, name):
    _, hr, cols = p.shape
    tr = _tile(hr, BIG_ROW_TILE)

    def body(chip_ref, p_ref, q_ref, o_ref):
        j = pl.program_id(1)
        term = jnp.where(j == chip_ref[0], p_ref[...].astype(F32), q_ref[...].astype(F32))

        @pl.when(j == 0)
        def _():
            o_ref[...] = term

        @pl.when(j != 0)
        def _():
            o_ref[...] += term

    grid_spec = pltpu.PrefetchScalarGridSpec(
        num_scalar_prefetch=1, grid=(hr // tr, N_CHIPS),
        in_specs=[pl.BlockSpec((None, tr, cols), lambda i, j, chip_ref: (chip_ref[0], i, 0)),
                  pl.BlockSpec((None, tr, cols), lambda i, j, chip_ref: (_flip_slot(j ^ chip_ref[0]), i, 0))],
        out_specs=pl.BlockSpec((tr, cols), lambda i, j, chip_ref: (i, 0)))
    return pl.pallas_call(body, name=name, grid_spec=grid_spec, out_shape=jax.ShapeDtypeStruct((hr, cols), F32),
                          compiler_params=_params(("parallel", "arbitrary")))(chip.reshape(1).astype(jnp.int32), p, q)


def _adamw_halves(w, g_own, g_other, m, v, c, name):
    rows, cols = w.shape
    tr = _tile(rows // 2, ROW_TILE)
    per_half = rows // 2 // tr

    def body(c_ref, w_ref, own_ref, other_ref, m_ref, v_ref, g_ref, d_ref, nm_ref, nv_ref):
        mine = (pl.program_id(0) // per_half) == c_ref[0]
        g_ = jnp.where(mine, own_ref[...], other_ref[...])
        g_ref[...] = g_
        d_ref[...], nm_ref[...], nv_ref[...] = _adamw_math(w_ref[...], g_, m_ref[...], v_ref[...])

    blk = pl.BlockSpec((tr, cols), lambda i, c_ref: (i, 0))
    own = pl.BlockSpec((tr, cols), lambda i, c_ref: (jnp.where(i // per_half == c_ref[0], i % per_half, 0), 0))
    other = pl.BlockSpec((tr, cols), lambda i, c_ref: (jnp.where(i // per_half == c_ref[0], 0, i % per_half), 0))
    grid_spec = pltpu.PrefetchScalarGridSpec(num_scalar_prefetch=1, grid=(rows // tr,),
                                             in_specs=[blk, own, other, blk, blk], out_specs=[blk] * 4)
    return pl.pallas_call(body, name=name, grid_spec=grid_spec, out_shape=[jax.ShapeDtypeStruct(w.shape, F32)] * 4,
                          compiler_params=_params(("parallel",)))(c.reshape(1).astype(jnp.int32), w, g_own, g_other, m, v)


W_IN_COLS = (D_MAIN + N_DT) // N_CHIPS
W_IN_MAIN = W_IN_COLS // 128 * 128
W_IN_TAIL = W_IN_COLS - 128
W_IN_PARTS = ((0, W_IN_MAIN), (W_IN_TAIL, 128))


def _cast_w_in_transposed(w_t, chip, after=()):
    def body(chip_ref, w_ref, *rest):
        for start, size in W_IN_PARTS:
            rest[-1][:, pl.ds(start, size)] = w_ref[pl.ds(start, size), :].T.astype(BF16)

    grid_spec = pltpu.PrefetchScalarGridSpec(
        num_scalar_prefetch=1, grid=(D_MODEL // ROW_TILE,),
        in_specs=[pl.BlockSpec((W_IN_COLS, ROW_TILE), lambda i, chip_ref: (0, i))] + [pl.BlockSpec(memory_space=pl.ANY)] * len(after),
        out_specs=pl.BlockSpec((None, ROW_TILE, W_IN_COLS), lambda i, chip_ref: (chip_ref[0], i, 0)))
    return pl.pallas_call(body, name="cast_w_in", grid_spec=grid_spec,
                          out_shape=jax.ShapeDtypeStruct((N_CHIPS, D_MODEL, W_IN_COLS), BF16),
                          compiler_params=_params(("parallel",)))(chip.reshape(1).astype(jnp.int32), w_t, *after)


def _adamw_w_in_transposed(w_t, g_own, g_other, m_t, v_t, c):
    per_half = D_MODEL // 2 // ROW_TILE

    def body(c_ref, w_ref, own_ref, other_ref, m_ref, v_ref, g_ref, d_ref, nm_ref, nv_ref):
        mine = (pl.program_id(0) // per_half) == c_ref[0]
        for start, size in W_IN_PARTS:
            cols, rows = pl.ds(start, size), pl.ds(start, size)
            g_ = jnp.where(mine, own_ref[:, cols], other_ref[:, cols]).T
            g_ref[rows, :] = g_
            d_ref[rows, :], nm_ref[rows, :], nv_ref[rows, :] = _adamw_math(w_ref[rows, :], g_, m_ref[rows, :], v_ref[rows, :])

    blk = pl.BlockSpec((W_IN_COLS, ROW_TILE), lambda i, c_ref: (0, i))
    own = pl.BlockSpec((ROW_TILE, W_IN_COLS), lambda i, c_ref: (jnp.where(i // per_half == c_ref[0], i % per_half, 0), 0))
    other = pl.BlockSpec((ROW_TILE, W_IN_COLS), lambda i, c_ref: (jnp.where(i // per_half == c_ref[0], 0, i % per_half), 0))
    grid_spec = pltpu.PrefetchScalarGridSpec(num_scalar_prefetch=1, grid=(D_MODEL // ROW_TILE,),
                                             in_specs=[blk, own, other, blk, blk], out_specs=[blk] * 4)
    return pl.pallas_call(body, name="adamw_w_in", grid_spec=grid_spec, out_shape=[jax.ShapeDtypeStruct(w_t.shape, F32)] * 4,
                          compiler_params=_params(("parallel",)))(c.reshape(1).astype(jnp.int32), w_t, g_own, g_other, m_t, v_t)


def _adamw_math(w, g, m, v):
    m_new = ADAM_B1 * m + (1.0 - ADAM_B1) * g
    v_new = ADAM_B2 * v + (1.0 - ADAM_B2) * (g * g)
    m_hat = m_new / (1.0 - ADAM_B1 ** ADAM_STEP)
    v_hat = v_new / (1.0 - ADAM_B2 ** ADAM_STEP)
    return -ADAM_LR * (m_hat / (jnp.sqrt(v_hat) + ADAM_EPS) + ADAM_WD * w), m_new, v_new


VECTORS = ("g_mix", "g_q", "g_k", "g_attn_out", "conv_b", "dt_bias", "a_log", "d_skip", "g_ssm_out", "g_cross", "g_mem",
           "g_cq", "g_ck", "g_mlp")
WEIGHTS = ("g_mix", "w_in", "g_q", "g_k", "g_attn_out", "conv_w", "conv_b", "dt_bias", "a_log", "d_skip", "g_ssm_out", "w_out",
           "g_cross", "g_mem", "w_cq", "w_ckv", "g_cq", "g_ck", "w_co", "g_mlp", "w_up", "w_down")


def _pack(parts):
    flat = jnp.concatenate([t.reshape(-1) for t in parts])
    total = -(-flat.shape[0] // 1024) * 1024
    return jnp.pad(flat, (0, total - flat.shape[0])).reshape(total // 128, 128)


def _rows_of(n):
    return -(-n // 128)


def _pack_rows(parts):
    rows = []
    for t in parts:
        flat = t.reshape(-1)
        rows.append(jnp.pad(flat, (0, 128 * _rows_of(flat.shape[0]) - flat.shape[0])).reshape(-1, 128))
    buf = jnp.concatenate(rows)
    return jnp.pad(buf, ((0, -buf.shape[0] % 8), (0, 0)))


def _adamw_vectors(summed, chip, vectors, conv):
    groups = list(vectors) + [conv]
    offsets, row = [], 0
    for w, _, _ in groups:
        offsets.append(row)
        row += _rows_of(w.shape[1]) if w.shape[0] == 1 else 4 * _rows_of(N_CHIPS * w.shape[1])
    conv_blocks = _rows_of(conv[0].shape[1])

    def body(chip_ref, sum_ref, *refs):
        ins, outs = refs[:3 * len(groups)], refs[3 * len(groups):]

        def update(i, g, idx):
            w_ref, m_ref, v_ref = ins[3 * i:3 * i + 3]
            delta, new_m, new_v = _adamw_math(w_ref[idx], g, m_ref[idx], v_ref[idx])
            for o_ref, val in zip(outs[4 * i:4 * i + 4], (g, delta, new_m, new_v)):
                o_ref[idx] = val

        for i, (w, _, _) in enumerate(vectors):
            for t in range(_rows_of(w.shape[1])):
                width = min(128, w.shape[1] - 128 * t)
                update(i, sum_ref[pl.ds(offsets[i] + t, 1), pl.ds(0, width)], (slice(None), pl.ds(128 * t, width)))
        for tap in range(4):
            for blk in range(conv_blocks):
                src = offsets[-1] + tap * N_CHIPS * conv_blocks + chip_ref[0] * conv_blocks + blk
                update(len(vectors), sum_ref[pl.ds(src, 1), :], (pl.ds(tap, 1), pl.ds(128 * blk, 128)))

    def whole(a):
        return pl.BlockSpec(a.shape, lambda i, chip_ref: (0,) * a.ndim)

    operands = [t for group in groups for t in group]
    grid_spec = pltpu.PrefetchScalarGridSpec(
        num_scalar_prefetch=1, grid=(1,), in_specs=[whole(summed)] + [whole(t) for t in operands],
        out_specs=[whole(w) for w, _, _ in groups for _ in range(4)])
    res = pl.pallas_call(body, name="adamw_vectors", grid_spec=grid_spec,
                         out_shape=[jax.ShapeDtypeStruct(w.shape, F32) for w, _, _ in groups for _ in range(4)],
                         compiler_params=_params(("arbitrary",)))(chip.reshape(1).astype(jnp.int32), summed, *operands)
    return [res[4 * i:4 * i + 4] for i in range(len(groups))]


def _unpack(buf, shapes):
    flat, out, pos = buf.reshape(-1), [], 0
    for shape in shapes:
        size = math.prod(shape)
        out.append(flat[pos:pos + size].reshape(shape))
        pos += size
    return out


def kernel(x, mem, positions, g_mix, w_in, g_q, g_k, g_attn_out, conv_w, conv_b, dt_bias, a_log, d_skip, g_ssm_out, w_out, g_cross, g_mem, w_cq, w_ckv, g_cq, g_ck, w_co, g_mlp, w_up, w_down, loss_target, m_g_mix, m_w_in, m_g_q, m_g_k, m_g_attn_out, m_conv_w, m_conv_b, m_dt_bias, m_a_log, m_d_skip, m_g_ssm_out, m_w_out, m_g_cross, m_g_mem, m_w_cq, m_w_ckv, m_g_cq, m_g_ck, m_w_co, m_g_mlp, m_w_up, m_w_down, v_g_mix, v_w_in, v_g_q, v_g_k, v_g_attn_out, v_conv_w, v_conv_b, v_dt_bias, v_a_log, v_d_skip, v_g_ssm_out, v_w_out, v_g_cross, v_g_mem, v_w_cq, v_w_ckv, v_g_cq, v_g_ck, v_w_co, v_g_mlp, v_w_up, v_w_down):
    args = dict(locals())
    weights = {n: args[n][0] for n in WEIGHTS}
    mom_m = {n: args["m_" + n][0] for n in WEIGHTS}
    mom_v = {n: args["v_" + n][0] for n in WEIGHTS}
    x_idx, y_idx, c_idx = _place()
    chip = 2 * x_idx + y_idx

    conv_parts = _small_allreduce(_pack([jnp.zeros((N_CHIPS, 4, 512), F32).at[chip].set(0.5 * weights["conv_w"])]),
                                  "gather_conv_taps")
    shapes = {n: weights[n].shape for n in MATRICES}
    first, mid, late = ("w_in",), ("w_out", "w_cq", "w_ckv", "w_co"), ("w_up", "w_down")
    w_in_t, m_in_t, v_in_t = (jnp.swapaxes(t, 1, 2)[0] for t in (w_in, m_w_in, v_w_in))
    w_in_buf = [_cast_w_in_transposed(w_in_t, chip)]
    sems_in, w_in_buf, token = _split_start("gather_ici_start_w_in", w_in_buf, _ici_plan(first, shapes), [3], after=(conv_parts,))
    bufs = [_cast_into_gathered(weights[n], n, chip, after=(token,)) for n in mid + late]
    sems_rest, bufs, token = _split_start("gather_ici_start_rest", bufs, _ici_plan(mid + late, shapes), [18], after=(token,))
    ici_sems = (sems_in[0], sems_rest[0])
    params = {n: weights[n].reshape(1, -1) for n in VECTORS}
    h_in = _rowwise(_norm_fn, [_full(x[0])], [_full(params["g_mix"])], [(D_MODEL, BF16, D_MODEL, 0, False)], name="norm_in",
                    after=(token,))[0]
    w_in_buf = _split_wait("gather_ici_wait_w_in", w_in_buf, ici_sems[0], _ici_plan(first, shapes), token, h_in)
    pass_sems, w_in_buf, token = _split_start("gather_pass_start_w_in", w_in_buf, _pass_on_plan(first, shapes), [3])
    w_in_buf = _split_wait("gather_pass_wait_w_in", w_in_buf, pass_sems[0], _pass_on_plan(first, shapes), token)
    w_in_full = _w_in_columns(w_in_buf[0], to_shards=False)
    full = {"w_in": w_in_full,
            "w_dt": jnp.pad(w_in_full[:, D_MAIN:].reshape(D_MODEL, N_GROUPS, HEADS_PER_GROUP),
                            ((0, 0), (0, 0), (0, 128 - HEADS_PER_GROUP))).reshape(D_MODEL, DT_PAD)}
    in_flight = {}

    def more_weights(stage, after):
        if stage == "mixer_done":
            rest = _split_wait("gather_ici_wait_rest", bufs, ici_sems[1], _ici_plan(mid + late, shapes), after)
            plan = lambda refs: _pass_on_plan(mid, shapes)(refs[:4]) + _pass_on_plan(late, shapes)(refs[4:])
            sems, rest, token = _split_start("gather_pass_start_rest", rest, plan, [12, 6])
            in_flight["late"] = (rest[4:], sems[1])
            return dict(zip(mid, _split_wait("gather_pass_wait_mid", rest[:4], sems[0], _pass_on_plan(mid, shapes), token)))
        late_bufs, sems = in_flight.pop("late")
        return dict(zip(late, _split_wait("gather_pass_wait_late", late_bufs, sems, _pass_on_plan(late, shapes), after)))

    params["conv_w"] = _unpack(conv_parts, [(N_CHIPS, 4, 512)])[0].transpose(1, 0, 2).reshape(4, 4 * 512)

    groups = (("w_down",), ("w_up",), ("w_co", "w_cq", "w_ckv", "w_out"), ("w_in",))
    scattered = []

    class GradStore(dict):
        pending = None

        def __setitem__(self, name, value):
            super().__setitem__(name, value)
            if "w_main" in self and "w_dt" in self and "w_in" not in self:
                gw_in = lax.dynamic_update_slice(self["w_main"], _unpad_heads(self["w_dt"]), (0, D_MAIN))
                self["w_in"] = _w_in_columns(gw_in, to_shards=True)
            for group in groups:
                if name in group and all(n in self for n in group):
                    self.settle()
                    pieces = [self[n].reshape(N_CHIPS, 2, shapes[n][0] // 2, shapes[n][1]) for n in group]
                    if group == groups[-1]:
                        self.scatter(group, pieces, _sibling_swap(pieces, "grad_swap_" + group[0]))
                    else:
                        landing = [lax.empty((N_CHIPS,) + a.shape[2:], BF16) for a in pieces]
                        sems, thru, self.token = _split_start("grad_swap_start_" + group[0], pieces + landing,
                                                              _swap_plan(len(pieces)), [len(pieces)])
                        self.pending = (group, sems[0], thru)

        def settle(self, *after):
            if self.pending is not None:
                group, sems, thru = self.pending
                self.pending = None
                thru = _split_wait("grad_swap_wait_" + group[0], thru, sems, _swap_plan(len(group)), *after)
                self.scatter(group, thru[:len(group)], thru[len(group):])

        def scatter(self, group, pieces, from_sibling):
            sums = [_add_halves(a, r, c_idx, "add_halves_" + n) for n, a, r in zip(group, pieces, from_sibling)]
            landing = [lax.empty((3,) + s.shape[1:], BF16) for s in sums]
            sems, thru, self.token = _split_start("grad_scatter_start_" + group[0], sums + landing,
                                                  _scatter_plan(len(sums)), [3 * len(sums)])
            scattered.append((group, sems[0], thru))

    loss, grad_x, grads = _local_step(x[0], mem[0], positions[0], loss_target[0], params, full, more_weights, GradStore(),
                                      h_in)

    out_g, out_d, out_m, out_v = {}, {}, {}, {}

    def finish(entries, order, token):
        halves = {}
        for group, sems, thru in entries:
            thru = _split_wait("grad_scatter_wait_" + group[0], thru, sems, _scatter_plan(len(group)), token)
            for i, n in enumerate(group):
                halves[n] = _sum_chips(thru[i], thru[len(group) + i], chip, "sum_chips_" + n)
        sources = [halves[n] for n in order]
        landing = [lax.empty(s.shape, F32) for s in sources]
        sems, thru, token = _split_start("grad_share_start_" + order[0], sources + landing, _share_plan(len(order)),
                                         [1] * len(order))
        for i, n in enumerate(order):
            own, other = _split_wait("grad_share_wait_" + n, [thru[i], thru[len(order) + i]], sems[i], _share_plan(1), token)
            if n == "w_in":
                res_t = _adamw_w_in_transposed(w_in_t, own, other, m_in_t, v_in_t, c_idx)
                out_g[n], out_d[n], out_m[n], out_v[n] = (t.T for t in res_t)
            else:
                out_g[n], out_d[n], out_m[n], out_v[n] = _adamw_halves(weights[n], own, other, mom_m[n], mom_v[n], c_idx,
                                                                       "adamw_" + n)
            token = out_v[n]
        return token

    token = finish(scattered[:-1], ("w_cq", "w_co", "w_ckv", "w_out", "w_up", "w_down"), grad_x)
    finish(scattered[-1:], ("w_in",), token)

    names = VECTORS + ("conv_w",)
    summed = _small_allreduce(_pack_rows([grads[n] for n in names]), "allreduce_vectors")
    small_out = _adamw_vectors(summed, chip, [(args[n], args["m_" + n], args["v_" + n]) for n in VECTORS],
                               (weights["conv_w"], mom_m["conv_w"], mom_v["conv_w"]))
    for n, res in zip(names, small_out):
        out_g[n], out_d[n], out_m[n], out_v[n] = (t.reshape(weights[n].shape) for t in res)

    total_loss = lax.psum(loss[0, 0], ("x", "y", "c"))
    outs = [total_loss, grad_x[None]]
    for group in (out_g, out_d, out_m, out_v):
        outs += [group[n][None] for n in WEIGHTS]
    return tuple(outs)
```

```python
import functools
import math

import jax
import jax.numpy as jnp
from jax import lax
from jax.experimental import pallas as pl
from jax.experimental.pallas import tpu as pltpu

F32 = jnp.float32
BF16 = jnp.bfloat16

SEQ = 2048
D_MODEL = 2048
HEAD = 64
D_ATTN = 1024
D_SSM = 1024
N_GROUPS = 4
N_STATE = 128
CHUNK = 128
ATT_BLK = 128
N_MEM = 256
D_CROSS = 512
D_FF = 8192
D_MAIN = 6144
N_DT = 16
DT_PAD = 512
ROT = 16
ROPE_THETA = 500000.0
EPS = 1e-6
NEG = -1e30
BRANCH_BLOCKS = (16, 4, 1)
DILATIONS = (1, 4, 16)

ADAM_LR, ADAM_B1, ADAM_B2, ADAM_EPS, ADAM_WD, ADAM_STEP = 0.001, 0.9, 0.999, 1e-08, 0.01, 10

VMEM_LIMIT = 56 * 1024 * 1024
MESH = pl.DeviceIdType.MESH


def _params(sem, **kw):
    return pltpu.CompilerParams(dimension_semantics=sem, vmem_limit_bytes=VMEM_LIMIT, **kw)


def _bdot(a, b, dims):
    return lax.dot_general(a.astype(BF16), b.astype(BF16), (dims, ((), ())), preferred_element_type=F32)


def _fdot(a, b, dims):
    return lax.dot_general(a, b, (dims, ((), ())), preferred_element_type=F32, precision=lax.Precision.HIGHEST)


NN = ((1,), (0,))
NT = ((1,), (1,))
TN = ((0,), (0,))


def _tile(n, want):
    t = min(n, want)
    while n % t:
        t //= 2
    return t


def _matmul(a, b, *, mode, name, outs, extra=(), epilogue=None, col_shards=1, after=(), n_cols=None, out_cols=None,
            tile_sums=0, tm=1024, tn=1024, tk=2048):
    if mode == "nn":
        (m, k), n = a.shape, b.shape[1]
    elif mode == "nt":
        (m, k), n = a.shape, b.shape[0]
    else:
        (k, m), n = a.shape, b.shape[1]
    n = n if n_cols is None else n_cols
    tm, tn, tk = _tile(m, tm), _tile(n // col_shards, tn), _tile(k, tk)
    nk = k // tk
    per_shard = n // col_shards // tn
    dims = {"nn": NN, "nt": NT, "tn": TN}[mode]
    a_spec = pl.BlockSpec((tk, tm), lambda i, j, kk: (kk, i)) if mode == "tn" else pl.BlockSpec((tm, tk), lambda i, j, kk: (i, kk))
    b_spec = pl.BlockSpec((tn, tk), lambda i, j, kk: (j, kk)) if mode == "nt" else pl.BlockSpec((tk, tn), lambda i, j, kk: (kk, j))
    o_spec = pl.BlockSpec((tm, tn), lambda i, j, kk: (i, j))
    n_extra, n_out, n_after = len(extra), len(outs), len(after)

    def body(a_ref, b_ref, *rest):
        extra_refs, out_refs, acc_ref = rest[:n_extra], rest[n_extra + n_after:-1], rest[-1]

        def finish(acc):
            res = (acc,) if epilogue is None else epilogue(acc, *[e[...] for e in extra_refs])
            for o_ref, r in zip(out_refs[:n_out], res):
                o_ref[...] = r.astype(o_ref.dtype)
            for o_ref, r in zip(out_refs[n_out:], res[n_out:]):
                o_ref[...] = jnp.broadcast_to(r, o_ref.shape)

        if nk == 1:
            finish(_bdot(a_ref[...], b_ref[...], dims))
            return
        kk = pl.program_id(2)

        @pl.when(kk == 0)
        def _():
            acc_ref[...] = jnp.zeros_like(acc_ref)

        acc_ref[...] += _bdot(a_ref[...], b_ref[...], dims)

        @pl.when(kk == nk - 1)
        def _():
            finish(acc_ref[...])

    if col_shards == 1:
        out_specs, out_dims = [o_spec] * n_out, (m, n if out_cols is None else out_cols)
    else:
        sharded = pl.BlockSpec((None, tm, tn), lambda i, j, kk: (j // per_shard, i, j % per_shard))
        out_specs, out_dims = [sharded] * n_out, (col_shards, m, n // col_shards)
    res = pl.pallas_call(
        body, name=name, grid=(m // tm, n // tn, nk),
        in_specs=[a_spec, b_spec] + [o_spec] * n_extra + [pl.BlockSpec(memory_space=pl.ANY)] * n_after,
        out_specs=out_specs + [pl.BlockSpec((8, 128), lambda i, j, kk: (i, j))] * tile_sums,
        out_shape=[jax.ShapeDtypeStruct(out_dims, dt) for dt in outs]
        + [jax.ShapeDtypeStruct((m // tm * 8, n // tn * 128), F32)] * tile_sums,
        scratch_shapes=[pltpu.VMEM((tm, tn) if nk > 1 else (8, 128), F32)],
        compiler_params=_params(("parallel", "parallel", "arbitrary")),
    )(a, b, *extra, *after)
    res = list(res[:n_out]) + [t[::8, ::128] for t in res[n_out:]]
    return res[0] if len(res) == 1 else res


def _row_spec(tr, bw, cb, per_group):
    return pl.BlockSpec((tr, bw), (lambda g, i: (i, cb + g)) if per_group else (lambda g, i: (i, cb)))


def _vec_spec(bw, cb, per_group):
    return pl.BlockSpec((1, bw), (lambda g, i: (0, cb + g)) if per_group else (lambda g, i: (0, cb)))


def _rowwise(fn, rows, vecs, outs, *, name, n_rows=SEQ, tr=256, groups=1, after=()):
    n_r, n_v, n_after = len(rows), len(vecs), len(after)

    def body(*refs):
        vals = [r[...].astype(F32) for r in refs[:n_r + n_v]]
        res = fn(*vals)
        for o_ref, r in zip(refs[n_r + n_v + n_after:], res):
            o_ref[...] = r.astype(o_ref.dtype)

    res = pl.pallas_call(
        body, name=name, grid=(groups, n_rows // tr),
        in_specs=[_row_spec(tr, bw, cb, pg) for _, bw, cb, pg in rows] + [_vec_spec(bw, cb, pg) for _, bw, cb, pg in vecs]
        + [pl.BlockSpec(memory_space=pl.ANY)] * n_after,
        out_specs=[_row_spec(tr, bw, cb, pg) for _, _, bw, cb, pg in outs],
        out_shape=[jax.ShapeDtypeStruct((n_rows, w), dt) for w, dt, _, _, _ in outs],
        compiler_params=_params(("parallel", "parallel")),
    )(*[r[0] for r in rows], *[v[0] for v in vecs], *after)
    return res


def _rowwise_vjp(fn, rows, vecs, cts, row_grads, vec_grads, *, name, n_rows=SEQ, tr=256, groups=1, after=()):
    n_r, n_v, n_after = len(rows), len(vecs), len(after)
    ct_ops = [op for group in cts for op in group]
    ct_sizes = [len(group) for group in cts]
    res_ops = [g[6] for g in row_grads if g[6] is not None]
    n_ct, n_res, n_rg = len(ct_ops), len(res_ops), len(row_grads)

    def body(*refs):
        vals = [r[...].astype(F32) for r in refs[:n_r + n_v]]
        pos = n_r + n_v
        ct_vals = []
        for size in ct_sizes:
            acc = refs[pos][...].astype(F32)
            for t in range(1, size):
                acc = acc + refs[pos + t][...].astype(F32)
            ct_vals.append(acc)
            pos += size
        res_refs = refs[pos:pos + n_res]
        out_refs = refs[pos + n_res + n_after:]
        _, pullback = jax.vjp(fn, *vals)
        grads = pullback(tuple(ct_vals))
        r_i = 0
        for o_ref, g in zip(out_refs[:n_rg], row_grads):
            val = grads[g[0]]
            if g[6] is not None:
                val = val + res_refs[r_i][...].astype(F32)
                r_i += 1
            o_ref[...] = val.astype(o_ref.dtype)
        first = (pl.program_id(1) == 0)
        for o_ref, g in zip(out_refs[n_rg:], vec_grads):
            val = jnp.sum(grads[n_r + g[0]], axis=0, keepdims=True)
            init = first if g[4] else jnp.logical_and(first, pl.program_id(0) == 0)

            @pl.when(init)
            def _(o_ref=o_ref, val=val):
                o_ref[...] = val

            @pl.when(jnp.logical_not(init))
            def _(o_ref=o_ref, val=val):
                o_ref[...] += val

    in_specs = [_row_spec(tr, bw, cb, pg) for _, bw, cb, pg in rows] + [_vec_spec(bw, cb, pg) for _, bw, cb, pg in vecs]
    in_specs += [_row_spec(tr, bw, cb, pg) for _, bw, cb, pg in ct_ops + res_ops] + [pl.BlockSpec(memory_space=pl.ANY)] * n_after
    out_specs =[_row_spec(tr, g[3], g[4], g[5]) for g in row_grads] + [_vec_spec(g[2], g[3], g[4]) for g in vec_grads]
    out_shape = [jax.ShapeDtypeStruct((n_rows, g[1]), g[2]) for g in row_grads]
    out_shape += [jax.ShapeDtypeStruct((1, g[1]), F32) for g in vec_grads]
    return pl.pallas_call(
        body, name=name, grid=(groups, n_rows // tr),
        in_specs=in_specs, out_specs=out_specs, out_shape=out_shape,
        compiler_params=_params(("arbitrary", "arbitrary")),
    )(*[r[0] for r in rows], *[v[0] for v in vecs], *[c[0] for c in ct_ops], *[r[0] for r in res_ops], *after)


def _full(arr, width=None):
    return (arr, arr.shape[1] if width is None else width, 0, False)


def _make_xor(sh):
    def raw(x):
        n = x.shape[-1]
        lane = lax.broadcasted_iota(jnp.int32, x.shape, x.ndim - 1)
        up = pltpu.roll(x, n - sh, x.ndim - 1)
        down = pltpu.roll(x, sh, x.ndim - 1)
        return jnp.where((lane & sh) == 0, up, down)

    f = jax.custom_vjp(raw)
    f.defvjp(lambda x: (raw(x), None), lambda _, ct: (raw(ct),))
    return f


_SWAP_ROPE_HALVES = _make_xor(ROT // 2)


def _head_sum(x):
    n = x.shape[-1]
    same_head = (lax.broadcasted_iota(jnp.int32, (n, n), 0) // HEAD) == (lax.broadcasted_iota(jnp.int32, (n, n), 1) // HEAD)
    return _fdot(x, same_head.astype(F32), NN)


def _rms(x, g):
    return x * lax.rsqrt(jnp.mean(x * x, axis=-1, keepdims=True) + EPS) * g


def _head_rms_rope(x, g, cos, sin, scale):
    y = x * lax.rsqrt(_head_sum(x * x) * (1.0 / HEAD) + EPS) * g
    return (y * cos + _SWAP_ROPE_HALVES(y) * sin) * scale


def _qk_fn(q, k, v, cos, sin, gq, gk):
    return (_head_rms_rope(q, gq, cos, sin, HEAD ** -0.5), _head_rms_rope(k, gk, cos, sin, 1.0), v)


def _norm_fn(x, g):
    return (_rms(x, g),)


def _merge_fn(o0, o1, o2, l0, l1, l2, g):
    m = lax.stop_gradient(jnp.maximum(jnp.maximum(l0, l1), l2))
    e0, e1, e2 = jnp.exp(l0 - m), jnp.exp(l1 - m), jnp.exp(l2 - m)
    mix = (e0 * o0 + e1 * o1 + e2 * o2) / (e0 + e1 + e2)
    return (_rms(mix, g),)


def _gate_fn(y, z, g):
    return (_rms(y * (z * jax.nn.sigmoid(z)), g),)


def _attn_pair(q, kc, vc, kp=None, vp=None, has_prev=None):
    pick0, pick1 = _head_picks()
    k_band, v_band, mask = _attn_band(kc, vc, kp, vp, has_prev)
    s = jnp.where(mask, _bdot(jnp.concatenate([q * pick0, q * pick1], axis=0), k_band, NT), NEG)
    m = jnp.max(s, axis=-1, keepdims=True)
    p = jnp.exp(s - m)
    den = jnp.sum(p, axis=-1, keepdims=True)
    acc = _bdot(p, v_band, NN) * (1.0 / den)
    lse_rows = m + jnp.log(den)
    o = pick0 * acc[:ATT_BLK] + pick1 * acc[ATT_BLK:]
    lse = pick0 * lse_rows[:ATT_BLK] + pick1 * lse_rows[ATT_BLK:]
    return o, lse


def _head_picks():
    lane = lax.broadcasted_iota(jnp.int32, (1, 2 * HEAD), 1)
    return (lane < HEAD).astype(F32), (lane >= HEAD).astype(F32)


def _attn_band(kc, vc, kp, vp, has_prev):
    n_keys = ATT_BLK if kp is None else 2 * ATT_BLK
    qi = lax.broadcasted_iota(jnp.int32, (2 * ATT_BLK, n_keys), 0) & (ATT_BLK - 1)
    kj = lax.broadcasted_iota(jnp.int32, (2 * ATT_BLK, n_keys), 1)
    if kp is None:
        return kc, vc, qi >= kj
    in_prev = jnp.logical_and(jnp.logical_and(kj < ATT_BLK, kj >= qi), has_prev)
    mask = jnp.logical_or(in_prev, jnp.logical_and(kj >= ATT_BLK, qi >= kj - ATT_BLK))
    return jnp.concatenate([kp, kc], axis=0), jnp.concatenate([vp, vc], axis=0), mask


def _attn_config(b):
    r = DILATIONS[b]
    return r, ATT_BLK * r, (512 if r == 1 else 128), BRANCH_BLOCKS[b] > 1


def _for_residues(r, fn):
    if r <= 4:
        for rho in range(r):
            fn(rho)
    else:
        def step(t, carry):
            for u in range(4):
                fn(4 * t + u)
            return carry

        lax.fori_loop(0, r // 4, step, 0)


def _strided_rows(start, r):
    if r > 1:
        return pl.ds(start, ATT_BLK, stride=r)
    return pl.ds(start if isinstance(start, int) else pl.multiple_of(start, ATT_BLK), ATT_BLK)


def _attention_fwd(qn, kn, vn, b):
    r, rows, lanes, with_prev = _attn_config(b)
    cur = pl.BlockSpec((rows, lanes), lambda g, n: (n, g))
    prev = pl.BlockSpec((rows, lanes), lambda g, n: (jnp.maximum(n - 1, 0), g))

    def body(*refs):
        ins, (o_ref, l_ref) = refs[:-2], refs[-2:]
        has_prev = pl.program_id(1) > 0

        def one(rho):
            sub = _strided_rows(rho, r)
            for pair in range(lanes // 128):
                sl = pl.ds(pair * 128, 128)
                args = [ref[sub, sl] for ref in ins] + ([has_prev] if with_prev else [])
                o_ref[sub, sl], l_ref[sub, sl] = _attn_pair(*args)

        _for_residues(r, one)

    operands = (qn, kn, vn, kn, vn) if with_prev else (qn, kn, vn)
    return pl.pallas_call(
        body, name="attn_fwd_%d" % r, grid=(D_ATTN // lanes, SEQ // rows),
        in_specs=[cur, cur, cur] + ([prev, prev] if with_prev else []), out_specs=[cur, cur],
        out_shape=[jax.ShapeDtypeStruct((SEQ, D_ATTN), F32)] * 2,
        compiler_params=_params(("parallel", "parallel")),
    )(*operands)


def _attn_pair_bwd(q, kc, vc, kp, vp, o, lse, do, dl, has_prev):
    pick0, pick1 = _head_picks()
    lane = lax.broadcasted_iota(jnp.int32, (1, 2 * HEAD), 1)
    k_band, v_band, mask = _attn_band(kc, vc, kp, vp, has_prev)
    q2 = jnp.concatenate([q * pick0, q * pick1], axis=0)
    do2 = jnp.concatenate([do * pick0, do * pick1], axis=0)
    lse2 = jnp.concatenate([jnp.sum(lse * (lane == 0).astype(F32), axis=-1, keepdims=True),
                            jnp.sum(lse * (lane == HEAD).astype(F32), axis=-1, keepdims=True)], axis=0)
    base = jnp.sum(jnp.concatenate([dl * pick0, dl * pick1], axis=0) - do2 * jnp.concatenate([o, o], axis=0),
                   axis=-1, keepdims=True)
    p = jnp.exp(jnp.where(mask, _bdot(q2, k_band, NT), NEG) - lse2)
    ds = p * (_bdot(do2, v_band, NT) + base)
    dq2 = _bdot(ds, k_band, NN)
    dq = pick0 * dq2[:ATT_BLK] + pick1 * dq2[ATT_BLK:]
    dk, dv = _bdot(ds, q2, TN), _bdot(p, do2, TN)
    if kp is None:
        return dq, dk, dv
    return dq, dk[ATT_BLK:], dv[ATT_BLK:], dk[:ATT_BLK], dv[:ATT_BLK]


def _attention_bwd(qn, kn, vn, o, lse, do, dl, b):
    r, rows, lanes, with_prev = _attn_config(b)
    cur = pl.BlockSpec((rows, lanes), lambda g, n: (n, g))
    prev = pl.BlockSpec((rows, lanes), lambda g, n: (jnp.maximum(n - 1, 0), g))
    whole = pl.BlockSpec((SEQ, lanes), lambda g, n: (0, g))
    n_in = 5 if with_prev else 3

    def body(*refs):
        ins, (o_ref, l_ref, do_ref, dl_ref, dq_ref, dk_ref, dv_ref) = refs[:n_in], refs[n_in:]
        n = pl.program_id(1)

        @pl.when(n == 0)
        def _():
            dk_ref[...] = jnp.zeros_like(dk_ref)
            dv_ref[...] = jnp.zeros_like(dv_ref)

        def one(rho):
            sub = _strided_rows(rho, r)
            sub_c = _strided_rows(n * rows + rho, r)
            sub_p = _strided_rows(jnp.maximum(n - 1, 0) * rows + rho, r)
            for pair in range(lanes // 128):
                sl = pl.ds(pair * 128, 128)
                vals = [ref[sub, sl] for ref in ins] + ([] if with_prev else [None, None])
                grads = _attn_pair_bwd(*vals, o_ref[sub, sl], l_ref[sub, sl], do_ref[sub, sl], dl_ref[sub, sl], n > 0)
                dq_ref[sub, sl] = grads[0]
                dk_ref[sub_c, sl] += grads[1]
                dv_ref[sub_c, sl] += grads[2]
                if with_prev:
                    dk_ref[sub_p, sl] += grads[3]
                    dv_ref[sub_p, sl] += grads[4]

        _for_residues(r, one)

    operands = (qn, kn, vn, kn, vn) if with_prev else (qn, kn, vn)
    return pl.pallas_call(
        body, name="attn_bwd_%d" % r, grid=(D_ATTN // lanes, SEQ // rows),
        in_specs=[cur, cur, cur] + ([prev, prev] if with_prev else []) + [cur] * 4, out_specs=[cur, whole, whole],
        out_shape=[jax.ShapeDtypeStruct((SEQ, D_ATTN), F32)] * 3,
        compiler_params=_params(("parallel", "arbitrary")),
    )(*operands, o, lse, do, dl)


CONV_COLS = 256
XBC_BLOCK0 = 4096 // CONV_COLS


def _shift_rows(x, s):
    n = x.shape[0]
    t = lax.broadcasted_iota(jnp.int32, x.shape, 0)
    if s >= 0:
        return jnp.where(t >= s, pltpu.roll(x, s, 0), 0.0)
    return jnp.where(t < n + s, pltpu.roll(x, n + s, 0), 0.0)


def _conv_pre(x, w_ref, b_ref):
    pre = b_ref[...] + w_ref[3:4, :] * x
    for k in range(3):
        pre = pre + w_ref[k:k + 1, :] * _shift_rows(x, 3 - k)
    return pre


def _conv_fwd(proj, conv_w, conv_b):
    cols = conv_w.shape[1]

    def body(x_ref, w_ref, b_ref, o_ref):
        pre = _conv_pre(x_ref[...], w_ref, b_ref)
        o_ref[...] = pre * jax.nn.sigmoid(pre)

    blk = pl.BlockSpec((SEQ, CONV_COLS), lambda j: (0, j))
    return pl.pallas_call(
        body, name="conv_fwd", grid=(cols // CONV_COLS,),
        in_specs=[pl.BlockSpec((SEQ, CONV_COLS), lambda j: (0, XBC_BLOCK0 + j)),
                  pl.BlockSpec((4, CONV_COLS), lambda j: (0, j)), pl.BlockSpec((1, CONV_COLS), lambda j: (0, j))],
        out_specs=blk, out_shape=jax.ShapeDtypeStruct((SEQ, cols), F32),
        compiler_params=_params(("parallel",)),
    )(proj, conv_w, conv_b)


def _conv_bwd(proj, conv_w, conv_b, dxs, db, dc):
    cols = conv_w.shape[1]
    x_blocks, b_blocks = dxs.shape[1] // CONV_COLS, db.shape[1] // CONV_COLS

    def body(x_ref, w_ref, b_ref, dxs_ref, db_ref_in, dc_ref_in, dx_ref, dw_ref, db_ref):
        j = pl.program_id(0)
        dy = jnp.where(j < x_blocks, dxs_ref[...], jnp.where(j < x_blocks + b_blocks, db_ref_in[...], dc_ref_in[...]))
        x = x_ref[...]
        pre = _conv_pre(x, w_ref, b_ref)
        sg = jax.nn.sigmoid(pre)
        dpre = dy * (sg * (1.0 + pre * (1.0 - sg)))
        db_ref[...] = jnp.sum(dpre, axis=0, keepdims=True)
        dx = w_ref[3:4, :] * dpre
        dw_ref[3:4, :] = jnp.sum(dpre * x, axis=0, keepdims=True)
        for k in range(3):
            dx = dx + w_ref[k:k + 1, :] * _shift_rows(dpre, k - 3)
            dw_ref[k:k + 1, :] = jnp.sum(dpre * _shift_rows(x, 3 - k), axis=0, keepdims=True)
        dw_ref[4:8, :] = jnp.zeros((4, CONV_COLS), F32)
        dx_ref[...] = dx.astype(dx_ref.dtype)

    blk = pl.BlockSpec((SEQ, CONV_COLS), lambda j: (0, j))
    parts = [pl.BlockSpec((SEQ, CONV_COLS), lambda j: (0, jnp.minimum(j, x_blocks - 1))),
             pl.BlockSpec((SEQ, CONV_COLS), lambda j: (0, jnp.clip(j - x_blocks, 0, b_blocks - 1))),
             pl.BlockSpec((SEQ, CONV_COLS), lambda j: (0, jnp.clip(j - x_blocks - b_blocks, 0, b_blocks - 1)))]
    return pl.pallas_call(
        body, name="conv_bwd", grid=(cols // CONV_COLS,),
        in_specs=[pl.BlockSpec((SEQ, CONV_COLS), lambda j: (0, XBC_BLOCK0 + j)),
                  pl.BlockSpec((4, CONV_COLS), lambda j: (0, j)), pl.BlockSpec((1, CONV_COLS), lambda j: (0, j))] + parts,
        out_specs=[blk, pl.BlockSpec((8, CONV_COLS), lambda j: (0, j)), pl.BlockSpec((1, CONV_COLS), lambda j: (0, j))],
        out_shape=[jax.ShapeDtypeStruct((SEQ, cols), BF16), jax.ShapeDtypeStruct((8, cols), F32),
                   jax.ShapeDtypeStruct((1, cols), F32)],
        compiler_params=_params(("parallel",)),
    )(proj, conv_w, conv_b, dxs, db, dc)


HEADS_PER_GROUP = 4


GROUP_WIDTH = HEADS_PER_GROUP * HEAD


def _ssd_chunk(x, bm, cm, dtr, bias, alog, dsk, h):
    row = lax.broadcasted_iota(jnp.int32, (CHUNK, CHUNK), 0)
    col = lax.broadcasted_iota(jnp.int32, (CHUNK, CHUNK), 1)
    causal = row >= col
    z = dtr + bias
    dt = jnp.maximum(z, 0.0) + jnp.log(1.0 + jnp.exp(-jnp.abs(z)))
    acs = _fdot(causal.astype(F32), dt * -jnp.exp(alog), NN)
    acs_t, dt_t = acs.T, dt.T
    cb = _bdot(cm, bm, NT)
    lane = lax.broadcasted_iota(jnp.int32, (1, CHUNK), 1)
    sub = lax.broadcasted_iota(jnp.int32, (CHUNK, 1), 0)
    wide = lax.broadcasted_iota(jnp.int32, (1, GROUP_WIDTH), 1) // HEAD
    tall = lax.broadcasted_iota(jnp.int32, (GROUP_WIDTH, 1), 0) // HEAD
    acs_last = jnp.sum(acs * (sub == CHUNK - 1).astype(F32), axis=0, keepdims=True)
    to_lanes = (lax.broadcasted_iota(jnp.int32, (CHUNK, GROUP_WIDTH), 0)
                == lax.broadcasted_iota(jnp.int32, (CHUNK, GROUP_WIDTH), 1) // HEAD).astype(F32)
    grow = _fdot(jnp.exp(acs), to_lanes, NN)
    keep = _fdot(jnp.exp(acs_last - acs) * dt, to_lanes, NN)
    w_parts, x_parts, skip, carry = [], [], 0.0, 0.0
    for j in range(HEADS_PER_GROUP):
        on_lane, on_sub = (lane == j).astype(F32), (sub == j).astype(F32)
        acs_c = jnp.sum(acs * on_lane, axis=1, keepdims=True)
        acs_r = jnp.sum(acs_t * on_sub, axis=0, keepdims=True)
        dt_r = jnp.sum(dt_t * on_sub, axis=0, keepdims=True)
        w_parts.append(cb * jnp.exp(jnp.where(causal, acs_c - acs_r, NEG)) * dt_r)
        x_parts.append(x * (wide == j).astype(F32))
        skip = skip + jnp.sum(dsk * on_lane, axis=1, keepdims=True) * (wide == j).astype(F32)
        carry = carry + jnp.sum(jnp.exp(acs_last) * on_lane, axis=1, keepdims=True) * (tall == j).astype(F32)
    y_diag = _bdot(jnp.concatenate(w_parts, axis=1), jnp.concatenate(x_parts, axis=0), NN)
    y = y_diag + _bdot(cm, h, NT) * grow + skip * x
    return y, h * carry + _bdot(x * keep, bm, TN)


def _ssd_specs(reverse):
    n_chunks = SEQ // CHUNK
    c_of = (lambda c: n_chunks - 1 - c) if reverse else (lambda c: c)
    x_spec = pl.BlockSpec((CHUNK, 256), lambda g, c: (c_of(c), g))
    b_spec = pl.BlockSpec((CHUNK, N_STATE), lambda g, c: (c_of(c), 8 + g))
    c_spec = pl.BlockSpec((CHUNK, N_STATE), lambda g, c: (c_of(c), 12 + g))
    dt_spec = pl.BlockSpec((CHUNK, 128), lambda g, c: (c_of(c), g))
    vec_spec = pl.BlockSpec((1, 128), lambda g, c: (0, g))
    h_spec = pl.BlockSpec((None, None, GROUP_WIDTH, N_STATE), lambda g, c: (c_of(c), g, 0, 0))
    return x_spec, b_spec, c_spec, dt_spec, vec_spec, h_spec


def _ssd_fwd(xbc, dt_raw, bias, alog, dsk):
    x_spec, b_spec, c_spec, dt_spec, vec_spec, h_spec = _ssd_specs(False)

    def body(x_ref, b_ref, c_ref, dt_ref, bias_ref, alog_ref, dsk_ref, y_ref, hin_ref, h_scr):
        @pl.when(pl.program_id(1) == 0)
        def _():
            h_scr[...] = jnp.zeros_like(h_scr)

        h = h_scr[...]
        hin_ref[...] = h
        y_ref[...], h_scr[...] = _ssd_chunk(x_ref[...], b_ref[...], c_ref[...], dt_ref[...], bias_ref[...], alog_ref[...],
                                            dsk_ref[...], h)

    return pl.pallas_call(
        body, name="ssd_fwd", grid=(N_GROUPS, SEQ // CHUNK),
        in_specs=[x_spec, b_spec, c_spec, dt_spec, vec_spec, vec_spec, vec_spec],
        out_specs=[x_spec, h_spec],
        out_shape=[jax.ShapeDtypeStruct((SEQ, D_SSM), F32),
                   jax.ShapeDtypeStruct((SEQ // CHUNK, N_GROUPS, GROUP_WIDTH, N_STATE), F32)],
        scratch_shapes=[pltpu.VMEM((GROUP_WIDTH, N_STATE), F32)],
        compiler_params=_params(("parallel", "arbitrary")),
    )(xbc, xbc, xbc, dt_raw, bias, alog, dsk)


def _ssd_bwd(xbc, dt_raw, bias, alog, dsk, h_in, dy):
    x_spec, b_spec, c_spec, dt_spec, vec_spec, h_spec = _ssd_specs(True)
    dxbc_x = pl.BlockSpec((CHUNK, 256), x_spec.index_map)

    def body(x_ref, b_ref, c_ref, dt_ref, bias_ref, alog_ref, dsk_ref, hin_ref, dy_ref,
             dx_ref, db_ref, dc_ref, ddt_ref, dbias_ref, dalog_ref, ddsk_ref, dh_scr):
        first = pl.program_id(1) == 0

        @pl.when(first)
        def _():
            dh_scr[...] = jnp.zeros_like(dh_scr)

        _, pullback = jax.vjp(_ssd_chunk, x_ref[...], b_ref[...], c_ref[...], dt_ref[...], bias_ref[...], alog_ref[...],
                              dsk_ref[...], hin_ref[...])
        g = pullback((dy_ref[...], dh_scr[...]))
        dx_ref[...], db_ref[...], dc_ref[...] = g[0], g[1], g[2]
        ddt_ref[...] = g[3].astype(ddt_ref.dtype)
        dh_scr[...] = g[7]
        for o_ref, val in ((dbias_ref, g[4]), (dalog_ref, g[5]), (ddsk_ref, g[6])):
            @pl.when(first)
            def _(o_ref=o_ref, val=val):
                o_ref[...] = val

            @pl.when(jnp.logical_not(first))
            def _(o_ref=o_ref, val=val):
                o_ref[...] += val

    n_chunks = SEQ // CHUNK
    out_b = pl.BlockSpec((CHUNK, N_STATE), lambda g, c: (n_chunks - 1 - c, g))
    res = pl.pallas_call(
        body, name="ssd_bwd", grid=(N_GROUPS, n_chunks),
        in_specs=[x_spec, b_spec, c_spec, dt_spec, vec_spec, vec_spec, vec_spec, h_spec, x_spec],
        out_specs=[dxbc_x, out_b, out_b, dt_spec, vec_spec, vec_spec, vec_spec],
        out_shape=[jax.ShapeDtypeStruct((SEQ, D_SSM), F32), jax.ShapeDtypeStruct((SEQ, N_GROUPS * N_STATE), F32),
                   jax.ShapeDtypeStruct((SEQ, N_GROUPS * N_STATE), F32), jax.ShapeDtypeStruct((SEQ, DT_PAD), BF16),
                   jax.ShapeDtypeStruct((1, DT_PAD), F32), jax.ShapeDtypeStruct((1, DT_PAD), F32),
                   jax.ShapeDtypeStruct((1, DT_PAD), F32)],
        scratch_shapes=[pltpu.VMEM((GROUP_WIDTH, N_STATE), F32)],
        compiler_params=_params(("parallel", "arbitrary")),
    )(xbc, xbc, xbc, dt_raw, bias, alog, dsk, h_in, dy)
    return res


CROSS_HEAD = 128
CROSS_ROWS = 512


def _cross_head(q, k, v, gq, gk):
    qn = _rms(q, gq) * (CROSS_HEAD ** -0.5)
    kn = _rms(k, gk)
    s = _bdot(qn, kn, NT)
    p = jnp.exp(s - lax.stop_gradient(jnp.max(s, axis=-1, keepdims=True)))
    return _bdot(p, v, NN) * (1.0 / jnp.sum(p, axis=-1, keepdims=True))


def _cross_specs():
    q_spec = pl.BlockSpec((CROSS_ROWS, CROSS_HEAD), lambda h, i: (i, h))
    k_spec = pl.BlockSpec((N_MEM, CROSS_HEAD), lambda h, i: (0, h))
    v_spec = pl.BlockSpec((N_MEM, CROSS_HEAD), lambda h, i: (0, 4 + h))
    g_spec = pl.BlockSpec((1, CROSS_HEAD), lambda h, i: (0, 0))
    return q_spec, k_spec, v_spec, g_spec


def _cross_fwd(qc, kv, gq, gk):
    q_spec, k_spec, v_spec, g_spec = _cross_specs()

    def body(q_ref, k_ref, v_ref, gq_ref, gk_ref, o_ref):
        o_ref[...] = _cross_head(q_ref[...], k_ref[...], v_ref[...], gq_ref[...], gk_ref[...]).astype(o_ref.dtype)

    return pl.pallas_call(
        body, name="cross_fwd", grid=(4, SEQ // CROSS_ROWS),
        in_specs=[q_spec, k_spec, v_spec, g_spec, g_spec], out_specs=q_spec,
        out_shape=jax.ShapeDtypeStruct((SEQ, D_CROSS), BF16),
        compiler_params=_params(("parallel", "parallel")),
    )(qc, kv, kv, gq, gk)


def _cross_bwd(qc, kv, gq, gk, do):
    q_spec, k_spec, v_spec, g_spec = _cross_specs()

    def body(q_ref, k_ref, v_ref, gq_ref, gk_ref, do_ref, dq_ref, dk_ref, dv_ref, dgq_ref, dgk_ref):
        _, pullback = jax.vjp(_cross_head, q_ref[...], k_ref[...], v_ref[...], gq_ref[...], gk_ref[...])
        dq, dk, dv, dgq, dgk = pullback(do_ref[...].astype(F32))
        dq_ref[...] = dq.astype(dq_ref.dtype)
        row0 = pl.program_id(1) == 0
        all0 = jnp.logical_and(row0, pl.program_id(0) == 0)
        for o_ref, val, init in ((dk_ref, dk, row0), (dv_ref, dv, row0), (dgq_ref, dgq, all0), (dgk_ref, dgk, all0)):
            @pl.when(init)
            def _(o_ref=o_ref, val=val):
                o_ref[...] = val

            @pl.when(jnp.logical_not(init))
            def _(o_ref=o_ref, val=val):
                o_ref[...] += val

    return pl.pallas_call(
        body, name="cross_bwd", grid=(4, SEQ // CROSS_ROWS),
        in_specs=[q_spec, k_spec, v_spec, g_spec, g_spec, q_spec],
        out_specs=[q_spec, k_spec, k_spec, g_spec, g_spec],
        out_shape=[jax.ShapeDtypeStruct((SEQ, D_CROSS), BF16), jax.ShapeDtypeStruct((N_MEM, D_CROSS), F32),
                   jax.ShapeDtypeStruct((N_MEM, D_CROSS), F32), jax.ShapeDtypeStruct((1, CROSS_HEAD), F32),
                   jax.ShapeDtypeStruct((1, CROSS_HEAD), F32)],
        compiler_params=_params(("arbitrary", "arbitrary")),
    )(qc, kv, kv, gq, gk, do)


def _loss_epilogue(acc, residual, target):
    err = acc + residual - target
    dy = err * (1.0 / D_MODEL)
    part = jnp.sum(jnp.sum(err * err, axis=1, keepdims=True), axis=0, keepdims=True) * (0.5 / D_MODEL)
    return dy, dy, part


def _pad_heads(v):
    return jnp.pad(v.reshape(N_GROUPS, HEADS_PER_GROUP), ((0, 0), (0, 128 - HEADS_PER_GROUP))).reshape(1, DT_PAD)


def _unpad_heads(v):
    return v.reshape(v.shape[0], N_GROUPS, 128)[:, :, :HEADS_PER_GROUP].reshape(v.shape[0], N_DT)


def _rope_tables(positions):
    half = ROT // 2
    inv_freq = ROPE_THETA ** (-2.0 * jnp.arange(half, dtype=F32) / ROT)
    ang = positions.reshape(SEQ, 1).astype(F32) * inv_freq
    cos, sin = jnp.cos(ang), jnp.sin(ang)
    ones, zeros = jnp.ones((SEQ, HEAD - ROT), F32), jnp.zeros((SEQ, HEAD - ROT), F32)
    cos_h = jnp.concatenate([cos, cos, ones], axis=1)
    sin_h = jnp.concatenate([-sin, sin, zeros], axis=1)
    return jnp.tile(cos_h, (1, 2)), jnp.tile(sin_h, (1, 2))


def _add_res(acc, res):
    return (acc + res,)


def _settle(grads, *after):
    if hasattr(grads, "settle"):
        grads.settle(*after)


def _take_token(grads):
    token = getattr(grads, "token", None)
    if token is None:
        return ()
    grads.token = None
    return (token,)


def _local_step(x, mem, positions, target, p, w, more_weights=None, grads=None, h=None):
    grads = {} if grads is None else grads
    w = dict(w)
    cos, sin = _rope_tables(positions)
    gq2, gk2 = jnp.tile(p["g_q"], (1, 2)), jnp.tile(p["g_k"], (1, 2))
    bias, alog, dsk = _pad_heads(p["dt_bias"]), _pad_heads(p["a_log"]), _pad_heads(p["d_skip"])
    norm_out = [(D_MODEL, BF16, D_MODEL, 0, False)]

    if h is None:
        h = _rowwise(_norm_fn, [_full(x)], [_full(p["g_mix"])], norm_out, name="norm_in")[0]
    proj = _matmul(h, w["w_in"], mode="nn", name="in_proj", outs=[F32], n_cols=D_MAIN)
    dt_raw = _matmul(h, w["w_dt"], mode="nn", name="dt_proj", outs=[F32])
    qk_rows = [(proj, 128, 0, True), (proj, 128, 8, True), (proj, 128, 16, True), _full(cos), _full(sin)]
    qk_vecs = [_full(gq2), _full(gk2)]
    qn, kn, vn = _rowwise(_qk_fn, qk_rows, qk_vecs, [(D_ATTN, F32, 128, 0, True)] * 3, name="qk_prep", groups=8, tr=1024)
    branches = [_attention_fwd(qn, kn, vn, b) for b in range(3)]
    merge_rows = [_full(o) for o, _ in branches] + [_full(lse) for _, lse in branches]
    attn = _rowwise(_merge_fn, merge_rows, [_full(p["g_attn_out"])], [(D_ATTN, BF16, D_ATTN, 0, False)], name="attn_merge")[0]
    xbc = _conv_fwd(proj, p["conv_w"], p["conv_b"])
    y_ssd, h_in = _ssd_fwd(xbc, dt_raw, bias, alog, dsk)
    gate_rows = [(y_ssd, 256, 0, True), (proj, 256, 12, True)]
    gate_vecs = [(p["g_ssm_out"], 256, 0, True)]
    ssm = _rowwise(_gate_fn, gate_rows, gate_vecs, [(D_SSM, BF16, 256, 0, True)], name="ssm_gate", groups=4)[0]
    mix = jnp.concatenate([attn, ssm], axis=1)
    if more_weights is not None:
        w.update(more_weights("mixer_done", mix))
    x1 = _matmul(mix, w["w_out"], mode="nn", name="out_proj", outs=[F32], extra=(x,), epilogue=_add_res)
    hc = _rowwise(_norm_fn, [_full(x1)], [_full(p["g_cross"])], norm_out, name="norm_cross")[0]
    memh = _rowwise(_norm_fn, [_full(mem)], [_full(p["g_mem"])], norm_out, name="norm_mem", n_rows=N_MEM)[0]
    qc = _matmul(hc, w["w_cq"], mode="nn", name="cq_proj", outs=[F32])
    if more_weights is not None:
        w.update(more_weights("cross_started", qc))
    kv = _matmul(memh, w["w_ckv"], mode="nn", name="ckv_proj", outs=[F32])
    oc = _cross_fwd(qc, kv, p["g_cq"], p["g_ck"])
    x2 = _matmul(oc, w["w_co"], mode="nn", name="co_proj", outs=[F32], extra=(x1,), epilogue=_add_res)
    hm = _rowwise(_norm_fn, [_full(x2)], [_full(p["g_mlp"])], norm_out, name="norm_mlp")[0]
    if more_weights is not None:
        w.update(more_weights("cross_done", hm))
    u, act = _matmul(hm, w["w_up"], mode="nn", name="up_proj", outs=[F32, BF16],
                     epilogue=lambda acc: (acc, jnp.square(jnp.maximum(acc, 0.0))))
    dy, dyb, loss_tiles = _matmul(act, w["w_down"], mode="nn", name="down_proj", outs=[F32, BF16], extra=(x2, target),
                                  epilogue=_loss_epilogue, tile_sums=1)
    loss = jnp.sum(loss_tiles).reshape(1, 1)

    grads["w_down"] = _matmul(act, dyb, mode="tn", name="dw_down", outs=[BF16], after=_take_token(grads))
    du = _matmul(dyb, w["w_down"], mode="nt", name="d_act", outs=[BF16], extra=(u,), after=_take_token(grads),
                 epilogue=lambda acc, uu: (acc * (2.0 * jnp.maximum(uu, 0.0)),))
    _settle(grads, du)
    grads["w_up"] = _matmul(hm, du, mode="tn", name="dw_up", outs=[BF16], col_shards=4, after=_take_token(grads))
    dhm = _matmul(du, w["w_up"], mode="nt", name="d_hm", outs=[F32], after=_take_token(grads))
    _settle(grads, dhm)
    dx2, grads["g_mlp"] = _rowwise_vjp(
        _norm_fn, [_full(x2)], [_full(p["g_mlp"])], [[_full(dhm)]],
        [(0, D_MODEL, F32, D_MODEL, 0, False, _full(dy))], [(0, D_MODEL, D_MODEL, 0, False)], name="norm_mlp_bwd")
    grads["w_co"] = _matmul(oc, dx2, mode="tn", name="dw_co", outs=[BF16], col_shards=4, after=_take_token(grads))
    doc = _matmul(dx2, w["w_co"], mode="nt", name="d_oc", outs=[BF16])
    dqc, dkc, dvc, grads["g_cq"], grads["g_ck"] = _cross_bwd(qc, kv, p["g_cq"], p["g_ck"], doc)
    grads["w_cq"] = _matmul(hc, dqc, mode="tn", name="dw_cq", outs=[BF16])
    dhc = _matmul(dqc, w["w_cq"], mode="nt", name="d_hc", outs=[F32])
    dkv = jnp.concatenate([dkc, dvc], axis=1)
    grads["w_ckv"] = _matmul(memh, dkv, mode="tn", name="dw_ckv", outs=[BF16])
    dmemh = _matmul(dkv, w["w_ckv"], mode="nt", name="d_memh", outs=[F32])
    grads["g_mem"] = _rowwise_vjp(_norm_fn, [_full(mem)], [_full(p["g_mem"])], [[_full(dmemh)]], [],
                                  [(0, D_MODEL, D_MODEL, 0, False)], name="norm_mem_bwd", n_rows=N_MEM)[0]
    dx1, grads["g_cross"] = _rowwise_vjp(
        _norm_fn, [_full(x1)], [_full(p["g_cross"])], [[_full(dhc)]],
        [(0, D_MODEL, F32, D_MODEL, 0, False, _full(dx2))], [(0, D_MODEL, D_MODEL, 0, False)], name="norm_cross_bwd")
    grads["w_out"] = _matmul(mix, dx1, mode="tn", name="dw_out", outs=[BF16])
    dmix = _matmul(dx1, w["w_out"], mode="nt", name="d_mix", outs=[F32], after=_take_token(grads))
    _settle(grads, dmix)
    merge_grads = [(i, D_ATTN, F32, D_ATTN, 0, False, None) for i in range(6)]
    *dol, grads["g_attn_out"] = _rowwise_vjp(
        _merge_fn, merge_rows, [_full(p["g_attn_out"])], [[(dmix, D_ATTN, 0, False)]],
        merge_grads, [(0, D_ATTN, D_ATTN, 0, False)], name="attn_merge_bwd", after=_take_token(grads))
    dqkv = [_attention_bwd(qn, kn, vn, *branches[b], dol[b], dol[3 + b], b) for b in range(3)]
    qk_cts = [[(dqkv[b][i], 128, 0, True) for b in range(3)] for i in range(3)]
    dq, dk, dv, dgq2, dgk2 = _rowwise_vjp(
        _qk_fn, qk_rows, qk_vecs, qk_cts, [(i, D_ATTN, BF16, 128, 0, True, None) for i in range(3)],
        [(0, 128, 128, 0, False), (1, 128, 128, 0, False)], name="qk_prep_bwd", groups=8, tr=512)
    grads["g_q"] = dgq2[:, :HEAD] + dgq2[:, HEAD:]
    grads["g_k"] = dgk2[:, :HEAD] + dgk2[:, HEAD:]
    dy_ssd, dz, grads["g_ssm_out"] = _rowwise_vjp(
        _gate_fn, gate_rows, gate_vecs, [[(dmix, 256, 4, True)]],
        [(0, D_SSM, F32, 256, 0, True, None), (1, D_SSM, BF16, 256, 0, True, None)],
        [(0, D_SSM, 256, 0, True)], name="ssm_gate_bwd", groups=4)
    dxs, db, dc, ddt, dbias, dalog, ddsk = _ssd_bwd(xbc, dt_raw, bias, alog, dsk, h_in, dy_ssd)
    grads["dt_bias"], grads["a_log"], grads["d_skip"] = _unpad_heads(dbias), _unpad_heads(dalog), _unpad_heads(ddsk)
    dxbc_raw, dconv_w, grads["conv_b"] = _conv_bwd(proj, p["conv_w"], p["conv_b"], dxs, db, dc)
    grads["conv_w"] = dconv_w[:4]
    dproj = jnp.concatenate([dq, dk, dv, dz, dxbc_raw], axis=1)
    grads["w_main"] = _matmul(h, dproj, mode="tn", name="dw_main", outs=[BF16], out_cols=D_MAIN + N_DT)
    grads["w_dt"] = _matmul(h, ddt, mode="tn", name="dw_dt", outs=[BF16])
    dh = _matmul(dproj, w["w_in"], mode="nt", name="d_h_main", outs=[F32], after=_take_token(grads))
    dh = _matmul(ddt, w["w_dt"], mode="nt", name="d_h_dt", outs=[F32], extra=(dh,), epilogue=_add_res)
    grad_x, grads["g_mix"] = _rowwise_vjp(
        _norm_fn, [_full(x)], [_full(p["g_mix"])], [[_full(dh)]],
        [(0, D_MODEL, F32, D_MODEL, 0, False, _full(dx1))], [(0, D_MODEL, D_MODEL, 0, False)], name="norm_in_bwd")
    return loss, grad_x, grads


MATRICES = ("w_in", "w_out", "w_cq", "w_ckv", "w_co", "w_up", "w_down")
ROW_SHARDED = ("w_out", "w_cq", "w_ckv", "w_down")
N_CHIPS = 4
ANY = pl.BlockSpec(memory_space=pl.ANY)


def _place():
    return lax.axis_index("x"), lax.axis_index("y"), lax.axis_index("c")


def _other_chips(x, y):
    return [(1 - x, y), (x, 1 - y), (1 - x, 1 - y)]


def _remote(src, dst, send_sem, recv_sem, device):
    return pltpu.make_async_remote_copy(src_ref=src, dst_ref=dst, send_sem=send_sem, recv_sem=recv_sem,
                                        device_id=device, device_id_type=MESH)


def _gathered_shape(name, shard):
    rows, cols = shard.shape
    if name == "w_in":
        return (N_CHIPS, rows, cols)
    return (N_CHIPS * rows, cols) if name in ROW_SHARDED else (rows, N_CHIPS * cols)


def _shard_window(name, ref, rows, cols, chip, half):
    r0, nr = (0, rows) if half is None else (half * (rows // 2), rows // 2)
    if name == "w_in":
        return ref.at[chip, pl.ds(r0, nr), :]
    if name in ROW_SHARDED:
        return ref.at[pl.ds(chip * rows + r0, nr), :]
    return ref.at[pl.ds(r0, nr), pl.ds(pl.multiple_of(chip * cols, 128), cols)]


def _cast_into_gathered(w, name, chip, after=()):
    rows, cols = w.shape
    tr = _tile(rows, ROW_TILE)

    def body(chip_ref, w_ref, *rest):
        rest[-1][...] = w_ref[...].astype(BF16)

    if name == "w_in":
        out_spec = pl.BlockSpec((None, tr, cols), lambda i, chip_ref: (chip_ref[0], i, 0))
    elif name in ROW_SHARDED:
        out_spec = pl.BlockSpec((tr, cols), lambda i, chip_ref: (chip_ref[0] * (rows // tr) + i, 0))
    else:
        out_spec = pl.BlockSpec((tr, cols), lambda i, chip_ref: (i, chip_ref[0]))
    grid_spec = pltpu.PrefetchScalarGridSpec(
        num_scalar_prefetch=1, grid=(rows // tr,),
        in_specs=[pl.BlockSpec((tr, cols), lambda i, chip_ref: (i, 0))] + [pl.BlockSpec(memory_space=pl.ANY)] * len(after),
        out_specs=out_spec)
    return pl.pallas_call(body, name="cast_" + name, grid_spec=grid_spec,
                          out_shape=jax.ShapeDtypeStruct(_gathered_shape(name, w), BF16),
                          compiler_params=_params(("parallel",)))(chip.reshape(1).astype(jnp.int32), w, *after)


def _w_in_columns(arr, to_shards):
    rows, piece = D_MODEL, (D_MAIN + N_DT) // N_CHIPS
    tr = ROW_TILE

    def body(a_ref, o_ref):
        for j in range(N_CHIPS):
            if to_shards:
                o_ref[j] = a_ref[:, pl.ds(piece * j, piece)]
            else:
                o_ref[:, pl.ds(piece * j, piece)] = a_ref[j]

    pieces = pl.BlockSpec((N_CHIPS, tr, piece), lambda i: (0, i, 0))
    matrix = pl.BlockSpec((tr, N_CHIPS * piece), lambda i: (i, 0))
    out_dims = (N_CHIPS, rows, piece) if to_shards else (rows, N_CHIPS * piece)
    return pl.pallas_call(
        body, name="w_in_to_shards" if to_shards else "w_in_from_shards", grid=(rows // tr,),
        in_specs=[matrix if to_shards else pieces], out_specs=pieces if to_shards else matrix,
        out_shape=jax.ShapeDtypeStruct(out_dims, arr.dtype), compiler_params=_params(("parallel",)))(arr)


HBM = pl.BlockSpec(memory_space=pltpu.HBM)
SEM = pl.BlockSpec(memory_space=pltpu.SEMAPHORE)
EFFECT = pltpu.SideEffectType.DATAFLOW_SIDE_EFFECTING


def _split_start(name, bufs, plan, counts, after=()):
    n, n_g, n_after = len(bufs), len(counts), len(after)

    def body(*refs):
        ins, sems, token = refs[:n], refs[n + n_after:n + n_after + 2 * n_g], refs[-1]
        for g, copies in enumerate(plan(ins)):
            for i, (src, dst, device, _) in enumerate(copies):
                _remote(src, dst, sems[2 * g].at[i], sems[2 * g + 1].at[i], device).start()
        token[...] = jnp.zeros_like(token)

    sem_shapes = [pltpu.SemaphoreType.DMA((cnt,)) for cnt in counts for _ in range(2)]
    res = pl.pallas_call(
        body, name=name,
        out_shape=(*sem_shapes, *[pltpu.HBM(b.shape, b.dtype) for b in bufs], jax.ShapeDtypeStruct((8, 128), F32)),
        in_specs=(*(HBM,) * n, *(ANY,) * n_after),
        out_specs=(*(SEM,) * (2 * n_g), *(HBM,) * n, pl.BlockSpec(memory_space=pltpu.VMEM)),
        input_output_aliases={i: 2 * n_g + i for i in range(n)},
        compiler_params=pltpu.CompilerParams(has_side_effects=EFFECT),
    )(*[pltpu.with_memory_space_constraint(b, pltpu.HBM) for b in bufs], *after)
    sems = [(res[2 * g], res[2 * g + 1]) for g in range(n_g)]
    return sems, list(res[2 * n_g:2 * n_g + n]), res[-1]


def _split_wait(name, bufs, sems, plan, *after):
    n = len(bufs)

    def body(*refs):
        ins, send, recv = refs[:n], refs[n], refs[n + 1]
        (copies,) = plan(ins)
        for i, (src, _, device, landing) in enumerate(copies):
            cp = _remote(src, landing, send.at[i], recv.at[i], device)
            cp.wait_send()
            cp.wait_recv()

    res = pl.pallas_call(
        body, name=name, out_shape=tuple(pltpu.HBM(b.shape, b.dtype) for b in bufs),
        in_specs=(*(HBM,) * n, SEM, SEM, *(ANY,) * len(after)), out_specs=(HBM,) * n,
        input_output_aliases={i: i for i in range(n)},
        compiler_params=pltpu.CompilerParams(has_side_effects=EFFECT),
    )(*bufs, sems[0], sems[1], *after)
    return list(res)


def _ici_plan(names, shard_shapes):
    def plan(refs):
        x, y, c = _place()
        copies = []
        for ref, name in zip(refs, names):
            win = _shard_window(name, ref, *shard_shapes[name], 2 * x + y, c)
            for px, py in _other_chips(x, y):
                copies.append((win, win, (px, py, c), _shard_window(name, ref, *shard_shapes[name], 2 * px + py, c)))
        return [copies]
    return plan


def _pass_on_plan(names, shard_shapes):
    def plan(refs):
        x, y, c = _place()
        copies = []
        for ref, name in zip(refs, names):
            for px, py in _other_chips(x, y):
                win = _shard_window(name, ref, *shard_shapes[name], 2 * px + py, c)
                copies.append((win, win, (x, y, 1 - c), _shard_window(name, ref, *shard_shapes[name], 2 * px + py, 1 - c)))
        return [copies]
    return plan


def _swap_plan(n_pairs):
    def plan(refs):
        x, y, c = _place()
        return [[(src.at[:, 1 - c], dst, (x, y, 1 - c), dst) for src, dst in zip(refs[:n_pairs], refs[n_pairs:])]]
    return plan


def _share_plan(n_pairs):
    def plan(refs):
        x, y, c = _place()
        return [[(src, dst, (x, y, 1 - c), dst)] for src, dst in zip(refs[:n_pairs], refs[n_pairs:])]
    return plan


def _scatter_plan(n_pairs):
    def plan(refs):
        x, y, c = _place()
        copies = []
        for src, dst in zip(refs[:n_pairs], refs[n_pairs:]):
            for k, (px, py) in enumerate(_other_chips(x, y)):
                copies.append((src.at[2 * px + py], dst.at[k], (px, py, c), dst.at[k]))
        return [copies]
    return plan


def _sibling_swap(arrs, name):
    n = len(arrs)

    def body(*refs):
        ins, outs, send, recv = refs[:n], refs[n:2 * n], refs[2 * n], refs[2 * n + 1]
        x, y, c = _place()
        cps = [_remote(ins[w].at[:, 1 - c], outs[w], send.at[w], recv.at[w], (x, y, 1 - c)) for w in range(n)]
        for cp in cps:
            cp.start()
        for cp in cps:
            cp.wait()

    return pl.pallas_call(
        body, name=name, in_specs=[ANY] * n, out_specs=[ANY] * n,
        out_shape=[jax.ShapeDtypeStruct((a.shape[0],) + a.shape[2:], a.dtype) for a in arrs],
        scratch_shapes=[pltpu.SemaphoreType.DMA((n,))] * 2,
    )(*arrs)


def _small_allreduce(buf, name):
    rows = buf.shape[0]

    def body(x_ref, out_ref, all_ref, send_sems, recv_sems, local_sem):
        x, y, c = _place()
        me, sibling, chips = (x, y, c), (x, y, 1 - c), _other_chips(x, y)

        def block(px, py, pc):
            return all_ref.at[pl.ds((4 * px + 2 * py + pc) * rows, rows), :]

        def copy(k, blk, to, src=None):
            return _remote(block(*blk) if src is None else src, block(*blk), send_sems.at[k], recv_sems.at[k], to)

        own = pltpu.make_async_copy(x_ref, block(*me), local_sem)
        own.start()
        first = [copy(0, me, sibling, src=x_ref)] + [copy(1 + j, me, (*chip, c), src=x_ref) for j, chip in enumerate(chips)]
        for cp in first:
            cp.start()
        passed = [copy(4 + j, (*chip, c), sibling) for j, chip in enumerate(chips)]
        for j, chip in enumerate(chips):
            copy(1 + j, (*chip, c), me).wait_recv()
            passed[j].start()
        copy(0, sibling, me).wait_recv()
        for j, chip in enumerate(chips):
            copy(4 + j, (*chip, 1 - c), me).wait_recv()
        for cp in first + passed:
            cp.wait_send()
        own.wait()
        acc = all_ref[pl.ds(0, rows), :]
        for d in range(1, 8):
            acc = acc + all_ref[pl.ds(d * rows, rows), :]
        out_ref[...] = acc

    vmem = pl.BlockSpec(memory_space=pltpu.VMEM)
    return pl.pallas_call(
        body, name=name, in_specs=[vmem], out_specs=vmem,
        out_shape=jax.ShapeDtypeStruct(buf.shape, F32),
        scratch_shapes=[pltpu.VMEM((8 * rows, 128), F32), pltpu.SemaphoreType.DMA((7,)), pltpu.SemaphoreType.DMA((7,)),
                        pltpu.SemaphoreType.DMA],
    )(buf)


ROW_TILE = 256
BIG_ROW_TILE = 1024


def _add_halves(arr, recv, c, name):
    _, _, hr, cols = arr.shape
    tr = _tile(hr, BIG_ROW_TILE)

    def body(c_ref, a_ref, r_ref, o_ref):
        o_ref[...] = (a_ref[...].astype(F32) + r_ref[...].astype(F32)).astype(o_ref.dtype)

    piece = pl.BlockSpec((None, tr, cols), lambda j, i, c_ref: (j, i, 0))
    grid_spec = pltpu.PrefetchScalarGridSpec(
        num_scalar_prefetch=1, grid=(N_CHIPS, hr // tr),
        in_specs=[pl.BlockSpec((None, None, tr, cols), lambda j, i, c_ref: (j, c_ref[0], i, 0)), piece], out_specs=piece)
    return pl.pallas_call(body, name=name, grid_spec=grid_spec, out_shape=jax.ShapeDtypeStruct(recv.shape, BF16),
                          compiler_params=_params(("parallel", "parallel")))(c.reshape(1).astype(jnp.int32), arr, recv)


def _flip_slot(d):
    return jnp.where(d == 1, 1, jnp.where(d == 3, 2, 0))


def _sum_chips(p, q, chip, name):
    _, hr, cols = p.shape
    tr = _tile(hr, BIG_ROW_TILE)

    def body(chip_ref, p_ref, q_ref, o_ref):
        j = pl.program_id(1)
        term = jnp.where(j == chip_ref[0], p_ref[...].astype(F32), q_ref[...].astype(F32))

        @pl.when(j == 0)
        def _():
            o_ref[...] = term

        @pl.when(j != 0)
        def _():
            o_ref[...] += term

    grid_spec = pltpu.PrefetchScalarGridSpec(
        num_scalar_prefetch=1, grid=(hr // tr, N_CHIPS),
        in_specs=[pl.BlockSpec((None, tr, cols), lambda i, j, chip_ref: (chip_ref[0], i, 0)),
                  pl.BlockSpec((None, tr, cols), lambda i, j, chip_ref: (_flip_slot(j ^ chip_ref[0]), i, 0))],
        out_specs=pl.BlockSpec((tr, cols), lambda i, j, chip_ref: (i, 0)))
    return pl.pallas_call(body, name=name, grid_spec=grid_spec, out_shape=jax.ShapeDtypeStruct((hr, cols), F32),
                          compiler_params=_params(("parallel", "arbitrary")))(chip.reshape(1).astype(jnp.int32), p, q)


def _adamw_halves(w, g_own, g_other, m, v, c, name):
    rows, cols = w.shape
    tr = _tile(rows // 2, ROW_TILE)
    per_half = rows // 2 // tr

    def body(c_ref, w_ref, own_ref, other_ref, m_ref, v_ref, g_ref, d_ref, nm_ref, nv_ref):
        mine = (pl.program_id(0) // per_half) == c_ref[0]
        g_ = jnp.where(mine, own_ref[...], other_ref[...])
        g_ref[...] = g_
        d_ref[...], nm_ref[...], nv_ref[...] = _adamw_math(w_ref[...], g_, m_ref[...], v_ref[...])

    blk = pl.BlockSpec((tr, cols), lambda i, c_ref: (i, 0))
    own = pl.BlockSpec((tr, cols), lambda i, c_ref: (jnp.where(i // per_half == c_ref[0], i % per_half, 0), 0))
    other = pl.BlockSpec((tr, cols), lambda i, c_ref: (jnp.where(i // per_half == c_ref[0], 0, i % per_half), 0))
    grid_spec = pltpu.PrefetchScalarGridSpec(num_scalar_prefetch=1, grid=(rows // tr,),
                                             in_specs=[blk, own, other, blk, blk], out_specs=[blk] * 4)
    return pl.pallas_call(body, name=name, grid_spec=grid_spec, out_shape=[jax.ShapeDtypeStruct(w.shape, F32)] * 4,
                          compiler_params=_params(("parallel",)))(c.reshape(1).astype(jnp.int32), w, g_own, g_other, m, v)


W_IN_COLS = (D_MAIN + N_DT) // N_CHIPS
W_IN_MAIN = W_IN_COLS // 128 * 128
W_IN_TAIL = W_IN_COLS - 128
W_IN_PARTS = ((0, W_IN_MAIN), (W_IN_TAIL, 128))


def _cast_w_in_transposed(w_t, chip, after=()):
    def body(chip_ref, w_ref, *rest):
        for start, size in W_IN_PARTS:
            rest[-1][:, pl.ds(start, size)] = w_ref[pl.ds(start, size), :].T.astype(BF16)

    grid_spec = pltpu.PrefetchScalarGridSpec(
        num_scalar_prefetch=1, grid=(D_MODEL // ROW_TILE,),
        in_specs=[pl.BlockSpec((W_IN_COLS, ROW_TILE), lambda i, chip_ref: (0, i))] + [pl.BlockSpec(memory_space=pl.ANY)] * len(after),
        out_specs=pl.BlockSpec((None, ROW_TILE, W_IN_COLS), lambda i, chip_ref: (chip_ref[0], i, 0)))
    return pl.pallas_call(body, name="cast_w_in", grid_spec=grid_spec,
                          out_shape=jax.ShapeDtypeStruct((N_CHIPS, D_MODEL, W_IN_COLS), BF16),
                          compiler_params=_params(("parallel",)))(chip.reshape(1).astype(jnp.int32), w_t, *after)


def _adamw_w_in_transposed(w_t, g_own, g_other, m_t, v_t, c):
    per_half = D_MODEL // 2 // ROW_TILE

    def body(c_ref, w_ref, own_ref, other_ref, m_ref, v_ref, g_ref, d_ref, nm_ref, nv_ref):
        mine = (pl.program_id(0) // per_half) == c_ref[0]
        for start, size in W_IN_PARTS:
            cols, rows = pl.ds(start, size), pl.ds(start, size)
            g_ = jnp.where(mine, own_ref[:, cols], other_ref[:, cols]).T
            g_ref[rows, :] = g_
            d_ref[rows, :], nm_ref[rows, :], nv_ref[rows, :] = _adamw_math(w_ref[rows, :], g_, m_ref[rows, :], v_ref[rows, :])

    blk = pl.BlockSpec((W_IN_COLS, ROW_TILE), lambda i, c_ref: (0, i))
    own = pl.BlockSpec((ROW_TILE, W_IN_COLS), lambda i, c_ref: (jnp.where(i // per_half == c_ref[0], i % per_half, 0), 0))
    other = pl.BlockSpec((ROW_TILE, W_IN_COLS), lambda i, c_ref: (jnp.where(i // per_half == c_ref[0], 0, i % per_half), 0))
    grid_spec = pltpu.PrefetchScalarGridSpec(num_scalar_prefetch=1, grid=(D_MODEL // ROW_TILE,),
                                             in_specs=[blk, own, other, blk, blk], out_specs=[blk] * 4)
    return pl.pallas_call(body, name="adamw_w_in", grid_spec=grid_spec, out_shape=[jax.ShapeDtypeStruct(w_t.shape, F32)] * 4,
                          compiler_params=_params(("parallel",)))(c.reshape(1).astype(jnp.int32), w_t, g_own, g_other, m_t, v_t)


def _adamw_math(w, g, m, v):
    m_new = ADAM_B1 * m + (1.0 - ADAM_B1) * g
    v_new = ADAM_B2 * v + (1.0 - ADAM_B2) * (g * g)
    m_hat = m_new / (1.0 - ADAM_B1 ** ADAM_STEP)
    v_hat = v_new / (1.0 - ADAM_B2 ** ADAM_STEP)
    return -ADAM_LR * (m_hat / (jnp.sqrt(v_hat) + ADAM_EPS) + ADAM_WD * w), m_new, v_new


VECTORS = ("g_mix", "g_q", "g_k", "g_attn_out", "conv_b", "dt_bias", "a_log", "d_skip", "g_ssm_out", "g_cross", "g_mem",
           "g_cq", "g_ck", "g_mlp")
WEIGHTS = ("g_mix", "w_in", "g_q", "g_k", "g_attn_out", "conv_w", "conv_b", "dt_bias", "a_log", "d_skip", "g_ssm_out", "w_out",
           "g_cross", "g_mem", "w_cq", "w_ckv", "g_cq", "g_ck", "w_co", "g_mlp", "w_up", "w_down")


def _pack(parts):
    flat = jnp.concatenate([t.reshape(-1) for t in parts])
    total = -(-flat.shape[0] // 1024) * 1024
    return jnp.pad(flat, (0, total - flat.shape[0])).reshape(total // 128, 128)


def _rows_of(n):
    return -(-n // 128)


def _slot_rows(n):
    return -(-n // 1024) * 8


def _pack_rows(parts):
    rows = []
    for t in parts:
        flat = t.reshape(-1)
        rows.append(jnp.pad(flat, (0, 128 * _slot_rows(flat.shape[0]) - flat.shape[0])).reshape(-1, 128))
    return jnp.concatenate(rows)


def _adamw_vectors(summed, chip, vectors, conv):
    groups = list(vectors) + [conv]
    offsets, row = [], 0
    for w, _, _ in groups:
        offsets.append(row)
        row += _slot_rows(w.shape[1]) if w.shape[0] == 1 else _slot_rows(4 * N_CHIPS * w.shape[1])
    conv_blocks = _rows_of(conv[0].shape[1])

    def body(chip_ref, sum_ref, *refs):
        ins, outs = refs[:3 * len(groups)], refs[3 * len(groups):]

        def update(i, g, idx):
            w_ref, m_ref, v_ref = ins[3 * i:3 * i + 3]
            delta, new_m, new_v = _adamw_math(w_ref[idx], g, m_ref[idx], v_ref[idx])
            for o_ref, val in zip(outs[4 * i:4 * i + 4], (g, delta, new_m, new_v)):
                o_ref[idx] = val

        for i, (w, _, _) in enumerate(vectors):
            for t in range(_rows_of(w.shape[1])):
                width = min(128, w.shape[1] - 128 * t)
                update(i, sum_ref[pl.ds(offsets[i] + t, 1), pl.ds(0, width)], (slice(None), pl.ds(128 * t, width)))
        for tap in range(4):
            for blk in range(conv_blocks):
                src = offsets[-1] + tap * N_CHIPS * conv_blocks + chip_ref[0] * conv_blocks + blk
                update(len(vectors), sum_ref[pl.ds(src, 1), :], (pl.ds(tap, 1), pl.ds(128 * blk, 128)))

    def whole(a):
        return pl.BlockSpec(a.shape, lambda i, chip_ref: (0,) * a.ndim)

    operands = [t for group in groups for t in group]
    grid_spec = pltpu.PrefetchScalarGridSpec(
        num_scalar_prefetch=1, grid=(1,), in_specs=[whole(summed)] + [whole(t) for t in operands],
        out_specs=[whole(w) for w, _, _ in groups for _ in range(4)])
    res = pl.pallas_call(body, name="adamw_vectors", grid_spec=grid_spec,
                         out_shape=[jax.ShapeDtypeStruct(w.shape, F32) for w, _, _ in groups for _ in range(4)],
                         compiler_params=_params(("arbitrary",)))(chip.reshape(1).astype(jnp.int32), summed, *operands)
    return [res[4 * i:4 * i + 4] for i in range(len(groups))]


def _unpack(buf, shapes):
    flat, out, pos = buf.reshape(-1), [], 0
    for shape in shapes:
        size = math.prod(shape)
        out.append(flat[pos:pos + size].reshape(shape))
        pos += size
    return out


def kernel(x, mem, positions, g_mix, w_in, g_q, g_k, g_attn_out, conv_w, conv_b, dt_bias, a_log, d_skip, g_ssm_out, w_out, g_cross, g_mem, w_cq, w_ckv, g_cq, g_ck, w_co, g_mlp, w_up, w_down, loss_target, m_g_mix, m_w_in, m_g_q, m_g_k, m_g_attn_out, m_conv_w, m_conv_b, m_dt_bias, m_a_log, m_d_skip, m_g_ssm_out, m_w_out, m_g_cross, m_g_mem, m_w_cq, m_w_ckv, m_g_cq, m_g_ck, m_w_co, m_g_mlp, m_w_up, m_w_down, v_g_mix, v_w_in, v_g_q, v_g_k, v_g_attn_out, v_conv_w, v_conv_b, v_dt_bias, v_a_log, v_d_skip, v_g_ssm_out, v_w_out, v_g_cross, v_g_mem, v_w_cq, v_w_ckv, v_g_cq, v_g_ck, v_w_co, v_g_mlp, v_w_up, v_w_down):
    args = dict(locals())
    weights = {n: args[n][0] for n in WEIGHTS}
    mom_m = {n: args["m_" + n][0] for n in WEIGHTS}
    mom_v = {n: args["v_" + n][0] for n in WEIGHTS}
    x_idx, y_idx, c_idx = _place()
    chip = 2 * x_idx + y_idx

    conv_parts = _small_allreduce(_pack([jnp.zeros((N_CHIPS, 4, 512), F32).at[chip].set(0.5 * weights["conv_w"])]),
                                  "gather_conv_taps")
    shapes = {n: weights[n].shape for n in MATRICES}
    first, mid, late = ("w_in",), ("w_out", "w_cq", "w_ckv", "w_co"), ("w_up", "w_down")
    w_in_t, m_in_t, v_in_t = (jnp.swapaxes(t, 1, 2)[0] for t in (w_in, m_w_in, v_w_in))
    w_in_buf = [_cast_w_in_transposed(w_in_t, chip)]
    sems_in, w_in_buf, token = _split_start("gather_ici_start_w_in", w_in_buf, _ici_plan(first, shapes), [3], after=(conv_parts,))
    bufs = [_cast_into_gathered(weights[n], n, chip, after=(token,)) for n in mid + late]
    plan = lambda refs: _ici_plan(mid, shapes)(refs[:4]) + _ici_plan(late, shapes)(refs[4:])
    sems_rest, bufs, token = _split_start("gather_ici_start_rest", bufs, plan, [12, 6], after=(token,))
    params = {n: weights[n].reshape(1, -1) for n in VECTORS}
    h_in = _rowwise(_norm_fn, [_full(x[0])], [_full(params["g_mix"])], [(D_MODEL, BF16, D_MODEL, 0, False)], name="norm_in",
                    after=(token,))[0]
    w_in_buf = _split_wait("gather_ici_wait_w_in", w_in_buf, sems_in[0], _ici_plan(first, shapes), token, h_in, m_in_t, v_in_t)
    pass_sems, w_in_buf, token = _split_start("gather_pass_start_w_in", w_in_buf, _pass_on_plan(first, shapes), [3])
    w_in_buf = _split_wait("gather_pass_wait_w_in", w_in_buf, pass_sems[0], _pass_on_plan(first, shapes), token)
    w_in_full = _w_in_columns(w_in_buf[0], to_shards=False)
    full = {"w_in": w_in_full,
            "w_dt": jnp.pad(w_in_full[:, D_MAIN:].reshape(D_MODEL, N_GROUPS, HEADS_PER_GROUP),
                            ((0, 0), (0, 0), (0, 128 - HEADS_PER_GROUP))).reshape(D_MODEL, DT_PAD)}
    in_flight = {}

    def more_weights(stage, after):
        if stage == "mixer_done":
            got = _split_wait("gather_ici_wait_mid", bufs[:4], sems_rest[0], _ici_plan(mid, shapes), after)
            sems, got, token = _split_start("gather_pass_start_mid", got, _pass_on_plan(mid, shapes), [12])
            return dict(zip(mid, _split_wait("gather_pass_wait_mid", got, sems[0], _pass_on_plan(mid, shapes), token)))
        if stage == "cross_started":
            got = _split_wait("gather_ici_wait_late", bufs[4:], sems_rest[1], _ici_plan(late, shapes), after)
            in_flight["late"] = _split_start("gather_pass_start_late", got, _pass_on_plan(late, shapes), [6])
            return {}
        sems, got, token = in_flight.pop("late")
        return dict(zip(late, _split_wait("gather_pass_wait_late", got, sems[0], _pass_on_plan(late, shapes), token, after)))

    params["conv_w"] = _unpack(conv_parts, [(N_CHIPS, 4, 512)])[0].transpose(1, 0, 2).reshape(4, 4 * 512)

    groups = (("w_down",), ("w_up",), ("w_co", "w_cq", "w_ckv", "w_out"), ("w_in",))
    scattered = []

    class GradStore(dict):
        pending = None

        def __setitem__(self, name, value):
            super().__setitem__(name, value)
            if "w_main" in self and "w_dt" in self and "w_in" not in self:
                gw_in = lax.dynamic_update_slice(self["w_main"], _unpad_heads(self["w_dt"]), (0, D_MAIN))
                self["w_in"] = _w_in_columns(gw_in, to_shards=True)
            for group in groups:
                if name in group and all(n in self for n in group):
                    self.settle()
                    pieces = [self[n].reshape(N_CHIPS, 2, shapes[n][0] // 2, shapes[n][1]) for n in group]
                    if group == groups[-1]:
                        self.scatter(group, pieces, _sibling_swap(pieces, "grad_swap_" + group[0]))
                    else:
                        landing = [lax.empty((N_CHIPS,) + a.shape[2:], BF16) for a in pieces]
                        sems, thru, self.token = _split_start("grad_swap_start_" + group[0], pieces + landing,
                                                              _swap_plan(len(pieces)), [len(pieces)])
                        self.pending = (group, sems[0], thru)

        def settle(self, *after):
            if self.pending is not None:
                group, sems, thru = self.pending
                self.pending = None
                thru = _split_wait("grad_swap_wait_" + group[0], thru, sems, _swap_plan(len(group)), *after)
                self.scatter(group, thru[:len(group)], thru[len(group):])

        def scatter(self, group, pieces, from_sibling):
            sums = [_add_halves(a, r, c_idx, "add_halves_" + n) for n, a, r in zip(group, pieces, from_sibling)]
            landing = [lax.empty((3,) + s.shape[1:], BF16) for s in sums]
            sems, thru, self.token = _split_start("grad_scatter_start_" + group[0], sums + landing,
                                                  _scatter_plan(len(sums)), [3 * len(sums)])
            scattered.append((group, sems[0], thru))

    loss, grad_x, grads = _local_step(x[0], mem[0], positions[0], loss_target[0], params, full, more_weights, GradStore(),
                                      h_in)

    out_g, out_d, out_m, out_v = {}, {}, {}, {}

    def finish(entries, order, token):
        halves = {}
        for group, sems, thru in entries:
            thru = _split_wait("grad_scatter_wait_" + group[0], thru, sems, _scatter_plan(len(group)), token)
            for i, n in enumerate(group):
                halves[n] = _sum_chips(thru[i], thru[len(group) + i], chip, "sum_chips_" + n)
        sources = [halves[n] for n in order]
        landing = [lax.empty(s.shape, F32) for s in sources]
        sems, thru, token = _split_start("grad_share_start_" + order[0], sources + landing, _share_plan(len(order)),
                                         [1] * len(order))
        for i, n in enumerate(order):
            own, other = _split_wait("grad_share_wait_" + n, [thru[i], thru[len(order) + i]], sems[i], _share_plan(1), token)
            if n == "w_in":
                res_t = _adamw_w_in_transposed(w_in_t, own, other, m_in_t, v_in_t, c_idx)
                out_g[n], out_d[n], out_m[n], out_v[n] = (t.T for t in res_t)
            else:
                out_g[n], out_d[n], out_m[n], out_v[n] = _adamw_halves(weights[n], own, other, mom_m[n], mom_v[n], c_idx,
                                                                       "adamw_" + n)
            token = out_v[n]
        return token

    token = finish(scattered[:-1], ("w_cq", "w_co", "w_ckv", "w_out", "w_up", "w_down"), grad_x)
    finish(scattered[-1:], ("w_in",), token)

    names = VECTORS + ("conv_w",)
    summed = _small_allreduce(_pack_rows([grads[n] for n in names] + [loss]), "allreduce_vectors")
    total_loss = summed[sum(_slot_rows(grads[n].size) for n in names), 0]
    small_out = _adamw_vectors(summed, chip, [(args[n], args["m_" + n], args["v_" + n]) for n in VECTORS],
                               (weights["conv_w"], mom_m["conv_w"], mom_v["conv_w"]))
    for n, res in zip(names, small_out):
        out_g[n], out_d[n], out_m[n], out_v[n] = (t.reshape(weights[n].shape) for t in res)

    outs =[total_loss, grad_x[None]]
    for group in (out_g, out_d, out_m, out_v):
        outs += [group[n][None] for n in WEIGHTS]
    return tuple(outs)
```

```python
import functools
import math

import jax
import jax.numpy as jnp
from jax import lax
from jax.experimental import pallas as pl
from jax.experimental.pallas import tpu as pltpu

F32 = jnp.float32
BF16 = jnp.bfloat16

SEQ = 2048
D_MODEL = 2048
HEAD = 64
D_ATTN = 1024
D_SSM = 1024
N_GROUPS = 4
N_STATE = 128
CHUNK = 128
ATT_BLK = 128
N_MEM = 256
D_CROSS = 512
D_FF = 8192
D_MAIN = 6144
N_DT = 16
DT_PAD = 512
ROT = 16
ROPE_THETA = 500000.0
EPS = 1e-6
NEG = -1e30
BRANCH_BLOCKS = (16, 4, 1)
DILATIONS = (1, 4, 16)

ADAM_LR, ADAM_B1, ADAM_B2, ADAM_EPS, ADAM_WD, ADAM_STEP = 0.001, 0.9, 0.999, 1e-08, 0.01, 10

VMEM_LIMIT = 56 * 1024 * 1024
MESH = pl.DeviceIdType.MESH


def _params(sem, **kw):
    return pltpu.CompilerParams(dimension_semantics=sem, vmem_limit_bytes=VMEM_LIMIT, **kw)


def _bdot(a, b, dims):
    return lax.dot_general(a.astype(BF16), b.astype(BF16), (dims, ((), ())), preferred_element_type=F32)


def _fdot(a, b, dims):
    return lax.dot_general(a, b, (dims, ((), ())), preferred_element_type=F32, precision=lax.Precision.HIGHEST)


NN = ((1,), (0,))
NT = ((1,), (1,))
TN = ((0,), (0,))


def _tile(n, want):
    t = min(n, want)
    while n % t:
        t //= 2
    return t


def _matmul(a, b, *, mode, name, outs, extra=(), epilogue=None, col_shards=1, after=(), n_cols=None, out_cols=None,
            tile_sums=0, tm=1024, tn=1024, tk=2048):
    if mode == "nn":
        (m, k), n = a.shape, b.shape[1]
    elif mode == "nt":
        (m, k), n = a.shape, b.shape[0]
    else:
        (k, m), n = a.shape, b.shape[1]
    n = n if n_cols is None else n_cols
    tm, tn, tk = _tile(m, tm), _tile(n // col_shards, tn), _tile(k, tk)
    nk = k // tk
    per_shard = n // col_shards // tn
    dims = {"nn": NN, "nt": NT, "tn": TN}[mode]
    a_spec = pl.BlockSpec((tk, tm), lambda i, j, kk: (kk, i)) if mode == "tn" else pl.BlockSpec((tm, tk), lambda i, j, kk: (i, kk))
    b_spec = pl.BlockSpec((tn, tk), lambda i, j, kk: (j, kk)) if mode == "nt" else pl.BlockSpec((tk, tn), lambda i, j, kk: (kk, j))
    o_spec = pl.BlockSpec((tm, tn), lambda i, j, kk: (i, j))
    n_extra, n_out, n_after = len(extra), len(outs), len(after)

    def body(a_ref, b_ref, *rest):
        extra_refs, out_refs, acc_ref = rest[:n_extra], rest[n_extra + n_after:-1], rest[-1]

        def finish(acc):
            res = (acc,) if epilogue is None else epilogue(acc, *[e[...] for e in extra_refs])
            for o_ref, r in zip(out_refs[:n_out], res):
                o_ref[...] = r.astype(o_ref.dtype)
            for o_ref, r in zip(out_refs[n_out:], res[n_out:]):
                o_ref[...] = jnp.broadcast_to(r, o_ref.shape)

        if nk == 1:
            finish(_bdot(a_ref[...], b_ref[...], dims))
            return
        kk = pl.program_id(2)

        @pl.when(kk == 0)
        def _():
            acc_ref[...] = jnp.zeros_like(acc_ref)

        acc_ref[...] += _bdot(a_ref[...], b_ref[...], dims)

        @pl.when(kk == nk - 1)
        def _():
            finish(acc_ref[...])

    if col_shards == 1:
        out_specs, out_dims = [o_spec] * n_out, (m, n if out_cols is None else out_cols)
    else:
        sharded = pl.BlockSpec((None, tm, tn), lambda i, j, kk: (j // per_shard, i, j % per_shard))
        out_specs, out_dims = [sharded] * n_out, (col_shards, m, n // col_shards)
    res = pl.pallas_call(
        body, name=name, grid=(m // tm, n // tn, nk),
        in_specs=[a_spec, b_spec] + [o_spec] * n_extra + [pl.BlockSpec(memory_space=pl.ANY)] * n_after,
        out_specs=out_specs + [pl.BlockSpec((8, 128), lambda i, j, kk: (i, j))] * tile_sums,
        out_shape=[jax.ShapeDtypeStruct(out_dims, dt) for dt in outs]
        + [jax.ShapeDtypeStruct((m // tm * 8, n // tn * 128), F32)] * tile_sums,
        scratch_shapes=[pltpu.VMEM((tm, tn) if nk > 1 else (8, 128), F32)],
        compiler_params=_params(("parallel", "parallel", "arbitrary")),
    )(a, b, *extra, *after)
    res = list(res[:n_out]) + [t[::8, ::128] for t in res[n_out:]]
    return res[0] if len(res) == 1 else res


def _row_spec(tr, bw, cb, per_group):
    return pl.BlockSpec((tr, bw), (lambda g, i: (i, cb + g)) if per_group else (lambda g, i: (i, cb)))


def _vec_spec(bw, cb, per_group):
    return pl.BlockSpec((1, bw), (lambda g, i: (0, cb + g)) if per_group else (lambda g, i: (0, cb)))


def _rowwise(fn, rows, vecs, outs, *, name, n_rows=SEQ, tr=256, groups=1, after=()):
    n_r, n_v, n_after = len(rows), len(vecs), len(after)

    def body(*refs):
        vals = [r[...].astype(F32) for r in refs[:n_r + n_v]]
        res = fn(*vals)
        for o_ref, r in zip(refs[n_r + n_v + n_after:], res):
            o_ref[...] = r.astype(o_ref.dtype)

    res = pl.pallas_call(
        body, name=name, grid=(groups, n_rows // tr),
        in_specs=[_row_spec(tr, bw, cb, pg) for _, bw, cb, pg in rows] + [_vec_spec(bw, cb, pg) for _, bw, cb, pg in vecs]
        + [pl.BlockSpec(memory_space=pl.ANY)] * n_after,
        out_specs=[_row_spec(tr, bw, cb, pg) for _, _, bw, cb, pg in outs],
        out_shape=[jax.ShapeDtypeStruct((n_rows, w), dt) for w, dt, _, _, _ in outs],
        compiler_params=_params(("parallel", "parallel")),
    )(*[r[0] for r in rows], *[v[0] for v in vecs], *after)
    return res


def _rowwise_vjp(fn, rows, vecs, cts, row_grads, vec_grads, *, name, n_rows=SEQ, tr=256, groups=1, after=()):
    n_r, n_v, n_after = len(rows), len(vecs), len(after)
    ct_ops = [op for group in cts for op in group]
    ct_sizes = [len(group) for group in cts]
    res_ops = [g[6] for g in row_grads if g[6] is not None]
    n_ct, n_res, n_rg = len(ct_ops), len(res_ops), len(row_grads)

    def body(*refs):
        vals = [r[...].astype(F32) for r in refs[:n_r + n_v]]
        pos = n_r + n_v
        ct_vals = []
        for size in ct_sizes:
            acc = refs[pos][...].astype(F32)
            for t in range(1, size):
                acc = acc + refs[pos + t][...].astype(F32)
            ct_vals.append(acc)
            pos += size
        res_refs = refs[pos:pos + n_res]
        out_refs = refs[pos + n_res + n_after:]
        _, pullback = jax.vjp(fn, *vals)
        grads = pullback(tuple(ct_vals))
        r_i = 0
        for o_ref, g in zip(out_refs[:n_rg], row_grads):
            val = grads[g[0]]
            if g[6] is not None:
                val = val + res_refs[r_i][...].astype(F32)
                r_i += 1
            o_ref[...] = val.astype(o_ref.dtype)
        first = (pl.program_id(1) == 0)
        for o_ref, g in zip(out_refs[n_rg:], vec_grads):
            val = jnp.sum(grads[n_r + g[0]], axis=0, keepdims=True)
            init = first if g[4] else jnp.logical_and(first, pl.program_id(0) == 0)

            @pl.when(init)
            def _(o_ref=o_ref, val=val):
                o_ref[...] = val

            @pl.when(jnp.logical_not(init))
            def _(o_ref=o_ref, val=val):
                o_ref[...] += val

    in_specs = [_row_spec(tr, bw, cb, pg) for _, bw, cb, pg in rows] + [_vec_spec(bw, cb, pg) for _, bw, cb, pg in vecs]
    in_specs += [_row_spec(tr, bw, cb, pg) for _, bw, cb, pg in ct_ops + res_ops] + [pl.BlockSpec(memory_space=pl.ANY)] * n_after
    out_specs =[_row_spec(tr, g[3], g[4], g[5]) for g in row_grads] + [_vec_spec(g[2], g[3], g[4]) for g in vec_grads]
    out_shape = [jax.ShapeDtypeStruct((n_rows, g[1]), g[2]) for g in row_grads]
    out_shape += [jax.ShapeDtypeStruct((1, g[1]), F32) for g in vec_grads]
    return pl.pallas_call(
        body, name=name, grid=(groups, n_rows // tr),
        in_specs=in_specs, out_specs=out_specs, out_shape=out_shape,
        compiler_params=_params(("arbitrary", "arbitrary")),
    )(*[r[0] for r in rows], *[v[0] for v in vecs], *[c[0] for c in ct_ops], *[r[0] for r in res_ops], *after)


def _full(arr, width=None):
    return (arr, arr.shape[1] if width is None else width, 0, False)


def _make_xor(sh):
    def raw(x):
        n = x.shape[-1]
        lane = lax.broadcasted_iota(jnp.int32, x.shape, x.ndim - 1)
        up = pltpu.roll(x, n - sh, x.ndim - 1)
        down = pltpu.roll(x, sh, x.ndim - 1)
        return jnp.where((lane & sh) == 0, up, down)

    f = jax.custom_vjp(raw)
    f.defvjp(lambda x: (raw(x), None), lambda _, ct: (raw(ct),))
    return f


_SWAP_ROPE_HALVES = _make_xor(ROT // 2)


def _head_sum(x):
    n = x.shape[-1]
    same_head = (lax.broadcasted_iota(jnp.int32, (n, n), 0) // HEAD) == (lax.broadcasted_iota(jnp.int32, (n, n), 1) // HEAD)
    return _fdot(x, same_head.astype(F32), NN)


def _rms(x, g):
    return x * lax.rsqrt(jnp.mean(x * x, axis=-1, keepdims=True) + EPS) * g


def _head_rms_rope(x, g, cos, sin, scale):
    y = x * lax.rsqrt(_head_sum(x * x) * (1.0 / HEAD) + EPS) * g
    return (y * cos + _SWAP_ROPE_HALVES(y) * sin) * scale


def _qk_fn(q, k, v, cos, sin, gq, gk):
    return (_head_rms_rope(q, gq, cos, sin, HEAD ** -0.5), _head_rms_rope(k, gk, cos, sin, 1.0), v)


def _norm_fn(x, g):
    return (_rms(x, g),)


def _merge_fn(o0, o1, o2, l0, l1, l2, g):
    m = lax.stop_gradient(jnp.maximum(jnp.maximum(l0, l1), l2))
    e0, e1, e2 = jnp.exp(l0 - m), jnp.exp(l1 - m), jnp.exp(l2 - m)
    mix = (e0 * o0 + e1 * o1 + e2 * o2) / (e0 + e1 + e2)
    return (_rms(mix, g),)


def _gate_fn(y, z, g):
    return (_rms(y * (z * jax.nn.sigmoid(z)), g),)


def _attn_pair(q, kc, vc, kp=None, vp=None, has_prev=None):
    pick0, pick1 = _head_picks()
    k_band, v_band, mask = _attn_band(kc, vc, kp, vp, has_prev)
    s = jnp.where(mask, _bdot(jnp.concatenate([q * pick0, q * pick1], axis=0), k_band, NT), NEG)
    m = jnp.max(s, axis=-1, keepdims=True)
    p = jnp.exp(s - m)
    den = jnp.sum(p, axis=-1, keepdims=True)
    acc = _bdot(p, v_band, NN) * (1.0 / den)
    lse_rows = m + jnp.log(den)
    o = pick0 * acc[:ATT_BLK] + pick1 * acc[ATT_BLK:]
    lse = pick0 * lse_rows[:ATT_BLK] + pick1 * lse_rows[ATT_BLK:]
    return o, lse


def _head_picks():
    lane = lax.broadcasted_iota(jnp.int32, (1, 2 * HEAD), 1)
    return (lane < HEAD).astype(F32), (lane >= HEAD).astype(F32)


def _attn_band(kc, vc, kp, vp, has_prev):
    n_keys = ATT_BLK if kp is None else 2 * ATT_BLK
    qi = lax.broadcasted_iota(jnp.int32, (2 * ATT_BLK, n_keys), 0) & (ATT_BLK - 1)
    kj = lax.broadcasted_iota(jnp.int32, (2 * ATT_BLK, n_keys), 1)
    if kp is None:
        return kc, vc, qi >= kj
    in_prev = jnp.logical_and(jnp.logical_and(kj < ATT_BLK, kj >= qi), has_prev)
    mask = jnp.logical_or(in_prev, jnp.logical_and(kj >= ATT_BLK, qi >= kj - ATT_BLK))
    return jnp.concatenate([kp, kc], axis=0), jnp.concatenate([vp, vc], axis=0), mask


def _attn_config(b):
    r = DILATIONS[b]
    return r, ATT_BLK * r, (512 if r == 1 else 128), BRANCH_BLOCKS[b] > 1


def _for_residues(r, fn):
    if r <= 4:
        for rho in range(r):
            fn(rho)
    else:
        def step(t, carry):
            for u in range(4):
                fn(4 * t + u)
            return carry

        lax.fori_loop(0, r // 4, step, 0)


def _strided_rows(start, r):
    if r > 1:
        return pl.ds(start, ATT_BLK, stride=r)
    return pl.ds(start if isinstance(start, int) else pl.multiple_of(start, ATT_BLK), ATT_BLK)


def _attention_fwd(qn, kn, vn, b):
    r, rows, lanes, with_prev = _attn_config(b)
    cur = pl.BlockSpec((rows, lanes), lambda g, n: (n, g))
    prev = pl.BlockSpec((rows, lanes), lambda g, n: (jnp.maximum(n - 1, 0), g))

    def body(*refs):
        ins, (o_ref, l_ref) = refs[:-2], refs[-2:]
        has_prev = pl.program_id(1) > 0

        def one(rho):
            sub = _strided_rows(rho, r)
            for pair in range(lanes // 128):
                sl = pl.ds(pair * 128, 128)
                args = [ref[sub, sl] for ref in ins] + ([has_prev] if with_prev else [])
                o_ref[sub, sl], l_ref[sub, sl] = _attn_pair(*args)

        _for_residues(r, one)

    operands = (qn, kn, vn, kn, vn) if with_prev else (qn, kn, vn)
    return pl.pallas_call(
        body, name="attn_fwd_%d" % r, grid=(D_ATTN // lanes, SEQ // rows),
        in_specs=[cur, cur, cur] + ([prev, prev] if with_prev else []), out_specs=[cur, cur],
        out_shape=[jax.ShapeDtypeStruct((SEQ, D_ATTN), F32)] * 2,
        compiler_params=_params(("parallel", "parallel")),
    )(*operands)


def _attn_pair_bwd(q, kc, vc, kp, vp, o, lse, do, dl, has_prev):
    pick0, pick1 = _head_picks()
    lane = lax.broadcasted_iota(jnp.int32, (1, 2 * HEAD), 1)
    k_band, v_band, mask = _attn_band(kc, vc, kp, vp, has_prev)
    q2 = jnp.concatenate([q * pick0, q * pick1], axis=0)
    do2 = jnp.concatenate([do * pick0, do * pick1], axis=0)
    lse2 = jnp.concatenate([jnp.sum(lse * (lane == 0).astype(F32), axis=-1, keepdims=True),
                            jnp.sum(lse * (lane == HEAD).astype(F32), axis=-1, keepdims=True)], axis=0)
    base = jnp.sum(jnp.concatenate([dl * pick0, dl * pick1], axis=0) - do2 * jnp.concatenate([o, o], axis=0),
                   axis=-1, keepdims=True)
    p = jnp.exp(jnp.where(mask, _bdot(q2, k_band, NT), NEG) - lse2)
    ds = p * (_bdot(do2, v_band, NT) + base)
    dq2 = _bdot(ds, k_band, NN)
    dq = pick0 * dq2[:ATT_BLK] + pick1 * dq2[ATT_BLK:]
    dk, dv = _bdot(ds, q2, TN), _bdot(p, do2, TN)
    if kp is None:
        return dq, dk, dv
    return dq, dk[ATT_BLK:], dv[ATT_BLK:], dk[:ATT_BLK], dv[:ATT_BLK]


def _attention_bwd(qn, kn, vn, o, lse, do, dl, b):
    r, rows, lanes, with_prev = _attn_config(b)
    cur = pl.BlockSpec((rows, lanes), lambda g, n: (n, g))
    prev = pl.BlockSpec((rows, lanes), lambda g, n: (jnp.maximum(n - 1, 0), g))
    whole = pl.BlockSpec((SEQ, lanes), lambda g, n: (0, g))
    n_in = 5 if with_prev else 3

    def body(*refs):
        ins, (o_ref, l_ref, do_ref, dl_ref, dq_ref, dk_ref, dv_ref) = refs[:n_in], refs[n_in:]
        n = pl.program_id(1)

        @pl.when(n == 0)
        def _():
            dk_ref[...] = jnp.zeros_like(dk_ref)
            dv_ref[...] = jnp.zeros_like(dv_ref)

        def one(rho):
            sub = _strided_rows(rho, r)
            sub_c = _strided_rows(n * rows + rho, r)
            sub_p = _strided_rows(jnp.maximum(n - 1, 0) * rows + rho, r)
            for pair in range(lanes // 128):
                sl = pl.ds(pair * 128, 128)
                vals = [ref[sub, sl] for ref in ins] + ([] if with_prev else [None, None])
                grads = _attn_pair_bwd(*vals, o_ref[sub, sl], l_ref[sub, sl], do_ref[sub, sl], dl_ref[sub, sl], n > 0)
                dq_ref[sub, sl] = grads[0]
                dk_ref[sub_c, sl] += grads[1]
                dv_ref[sub_c, sl] += grads[2]
                if with_prev:
                    dk_ref[sub_p, sl] += grads[3]
                    dv_ref[sub_p, sl] += grads[4]

        _for_residues(r, one)

    operands = (qn, kn, vn, kn, vn) if with_prev else (qn, kn, vn)
    return pl.pallas_call(
        body, name="attn_bwd_%d" % r, grid=(D_ATTN // lanes, SEQ // rows),
        in_specs=[cur, cur, cur] + ([prev, prev] if with_prev else []) + [cur] * 4, out_specs=[cur, whole, whole],
        out_shape=[jax.ShapeDtypeStruct((SEQ, D_ATTN), F32)] * 3,
        compiler_params=_params(("parallel", "arbitrary")),
    )(*operands, o, lse, do, dl)


CONV_COLS = 256
XBC_BLOCK0 = 4096 // CONV_COLS


def _shift_rows(x, s):
    n = x.shape[0]
    t = lax.broadcasted_iota(jnp.int32, x.shape, 0)
    if s >= 0:
        return jnp.where(t >= s, pltpu.roll(x, s, 0), 0.0)
    return jnp.where(t < n + s, pltpu.roll(x, n + s, 0), 0.0)


def _conv_pre(x, w_ref, b_ref):
    pre = b_ref[...] + w_ref[3:4, :] * x
    for k in range(3):
        pre = pre + w_ref[k:k + 1, :] * _shift_rows(x, 3 - k)
    return pre


def _conv_fwd(proj, conv_w, conv_b):
    cols = conv_w.shape[1]

    def body(x_ref, w_ref, b_ref, o_ref):
        pre = _conv_pre(x_ref[...], w_ref, b_ref)
        o_ref[...] = pre * jax.nn.sigmoid(pre)

    blk = pl.BlockSpec((SEQ, CONV_COLS), lambda j: (0, j))
    return pl.pallas_call(
        body, name="conv_fwd", grid=(cols // CONV_COLS,),
        in_specs=[pl.BlockSpec((SEQ, CONV_COLS), lambda j: (0, XBC_BLOCK0 + j)),
                  pl.BlockSpec((4, CONV_COLS), lambda j: (0, j)), pl.BlockSpec((1, CONV_COLS), lambda j: (0, j))],
        out_specs=blk, out_shape=jax.ShapeDtypeStruct((SEQ, cols), F32),
        compiler_params=_params(("parallel",)),
    )(proj, conv_w, conv_b)


def _conv_bwd(proj, conv_w, conv_b, dxs, db, dc):
    cols = conv_w.shape[1]
    x_blocks, b_blocks = dxs.shape[1] // CONV_COLS, db.shape[1] // CONV_COLS

    def body(x_ref, w_ref, b_ref, dxs_ref, db_ref_in, dc_ref_in, dx_ref, dw_ref, db_ref):
        j = pl.program_id(0)
        dy = jnp.where(j < x_blocks, dxs_ref[...], jnp.where(j < x_blocks + b_blocks, db_ref_in[...], dc_ref_in[...]))
        x = x_ref[...]
        pre = _conv_pre(x, w_ref, b_ref)
        sg = jax.nn.sigmoid(pre)
        dpre = dy * (sg * (1.0 + pre * (1.0 - sg)))
        db_ref[...] = jnp.sum(dpre, axis=0, keepdims=True)
        dx = w_ref[3:4, :] * dpre
        dw_ref[3:4, :] = jnp.sum(dpre * x, axis=0, keepdims=True)
        for k in range(3):
            dx = dx + w_ref[k:k + 1, :] * _shift_rows(dpre, k - 3)
            dw_ref[k:k + 1, :] = jnp.sum(dpre * _shift_rows(x, 3 - k), axis=0, keepdims=True)
        dw_ref[4:8, :] = jnp.zeros((4, CONV_COLS), F32)
        dx_ref[...] = dx.astype(dx_ref.dtype)

    blk = pl.BlockSpec((SEQ, CONV_COLS), lambda j: (0, j))
    parts = [pl.BlockSpec((SEQ, CONV_COLS), lambda j: (0, jnp.minimum(j, x_blocks - 1))),
             pl.BlockSpec((SEQ, CONV_COLS), lambda j: (0, jnp.clip(j - x_blocks, 0, b_blocks - 1))),
             pl.BlockSpec((SEQ, CONV_COLS), lambda j: (0, jnp.clip(j - x_blocks - b_blocks, 0, b_blocks - 1)))]
    return pl.pallas_call(
        body, name="conv_bwd", grid=(cols // CONV_COLS,),
        in_specs=[pl.BlockSpec((SEQ, CONV_COLS), lambda j: (0, XBC_BLOCK0 + j)),
                  pl.BlockSpec((4, CONV_COLS), lambda j: (0, j)), pl.BlockSpec((1, CONV_COLS), lambda j: (0, j))] + parts,
        out_specs=[blk, pl.BlockSpec((8, CONV_COLS), lambda j: (0, j)), pl.BlockSpec((1, CONV_COLS), lambda j: (0, j))],
        out_shape=[jax.ShapeDtypeStruct((SEQ, cols), BF16), jax.ShapeDtypeStruct((8, cols), F32),
                   jax.ShapeDtypeStruct((1, cols), F32)],
        compiler_params=_params(("parallel",)),
    )(proj, conv_w, conv_b, dxs, db, dc)


HEADS_PER_GROUP = 4


GROUP_WIDTH = HEADS_PER_GROUP * HEAD


def _ssd_chunk(x, bm, cm, dtr, bias, alog, dsk, h):
    row = lax.broadcasted_iota(jnp.int32, (CHUNK, CHUNK), 0)
    col = lax.broadcasted_iota(jnp.int32, (CHUNK, CHUNK), 1)
    causal = row >= col
    z = dtr + bias
    dt = jnp.maximum(z, 0.0) + jnp.log(1.0 + jnp.exp(-jnp.abs(z)))
    acs = _fdot(causal.astype(F32), dt * -jnp.exp(alog), NN)
    acs_t, dt_t = acs.T, dt.T
    cb = _bdot(cm, bm, NT)
    lane = lax.broadcasted_iota(jnp.int32, (1, CHUNK), 1)
    sub = lax.broadcasted_iota(jnp.int32, (CHUNK, 1), 0)
    wide = lax.broadcasted_iota(jnp.int32, (1, GROUP_WIDTH), 1) // HEAD
    tall = lax.broadcasted_iota(jnp.int32, (GROUP_WIDTH, 1), 0) // HEAD
    acs_last = jnp.sum(acs * (sub == CHUNK - 1).astype(F32), axis=0, keepdims=True)
    to_lanes = (lax.broadcasted_iota(jnp.int32, (CHUNK, GROUP_WIDTH), 0)
                == lax.broadcasted_iota(jnp.int32, (CHUNK, GROUP_WIDTH), 1) // HEAD).astype(F32)
    grow = _fdot(jnp.exp(acs), to_lanes, NN)
    keep = _fdot(jnp.exp(acs_last - acs) * dt, to_lanes, NN)
    w_parts, x_parts, skip, carry = [], [], 0.0, 0.0
    for j in range(HEADS_PER_GROUP):
        on_lane, on_sub = (lane == j).astype(F32), (sub == j).astype(F32)
        acs_c = jnp.sum(acs * on_lane, axis=1, keepdims=True)
        acs_r = jnp.sum(acs_t * on_sub, axis=0, keepdims=True)
        dt_r = jnp.sum(dt_t * on_sub, axis=0, keepdims=True)
        w_parts.append(cb * jnp.exp(jnp.where(causal, acs_c - acs_r, NEG)) * dt_r)
        x_parts.append(x * (wide == j).astype(F32))
        skip = skip + jnp.sum(dsk * on_lane, axis=1, keepdims=True) * (wide == j).astype(F32)
        carry = carry + jnp.sum(jnp.exp(acs_last) * on_lane, axis=1, keepdims=True) * (tall == j).astype(F32)
    y_diag = _bdot(jnp.concatenate(w_parts, axis=1), jnp.concatenate(x_parts, axis=0), NN)
    y = y_diag + _bdot(cm, h, NT) * grow + skip * x
    return y, h * carry + _bdot(x * keep, bm, TN)


def _ssd_specs(reverse):
    n_chunks = SEQ // CHUNK
    c_of = (lambda c: n_chunks - 1 - c) if reverse else (lambda c: c)
    x_spec = pl.BlockSpec((CHUNK, 256), lambda g, c: (c_of(c), g))
    b_spec = pl.BlockSpec((CHUNK, N_STATE), lambda g, c: (c_of(c), 8 + g))
    c_spec = pl.BlockSpec((CHUNK, N_STATE), lambda g, c: (c_of(c), 12 + g))
    dt_spec = pl.BlockSpec((CHUNK, 128), lambda g, c: (c_of(c), g))
    vec_spec = pl.BlockSpec((1, 128), lambda g, c: (0, g))
    h_spec = pl.BlockSpec((None, None, GROUP_WIDTH, N_STATE), lambda g, c: (c_of(c), g, 0, 0))
    return x_spec, b_spec, c_spec, dt_spec, vec_spec, h_spec


def _ssd_fwd(xbc, dt_raw, bias, alog, dsk):
    x_spec, b_spec, c_spec, dt_spec, vec_spec, h_spec = _ssd_specs(False)

    def body(x_ref, b_ref, c_ref, dt_ref, bias_ref, alog_ref, dsk_ref, y_ref, hin_ref, h_scr):
        @pl.when(pl.program_id(1) == 0)
        def _():
            h_scr[...] = jnp.zeros_like(h_scr)

        h = h_scr[...]
        hin_ref[...] = h
        y_ref[...], h_scr[...] = _ssd_chunk(x_ref[...], b_ref[...], c_ref[...], dt_ref[...], bias_ref[...], alog_ref[...],
                                            dsk_ref[...], h)

    return pl.pallas_call(
        body, name="ssd_fwd", grid=(N_GROUPS, SEQ // CHUNK),
        in_specs=[x_spec, b_spec, c_spec, dt_spec, vec_spec, vec_spec, vec_spec],
        out_specs=[x_spec, h_spec],
        out_shape=[jax.ShapeDtypeStruct((SEQ, D_SSM), F32),
                   jax.ShapeDtypeStruct((SEQ // CHUNK, N_GROUPS, GROUP_WIDTH, N_STATE), F32)],
        scratch_shapes=[pltpu.VMEM((GROUP_WIDTH, N_STATE), F32)],
        compiler_params=_params(("parallel", "arbitrary")),
    )(xbc, xbc, xbc, dt_raw, bias, alog, dsk)


def _ssd_bwd(xbc, dt_raw, bias, alog, dsk, h_in, dy):
    x_spec, b_spec, c_spec, dt_spec, vec_spec, h_spec = _ssd_specs(True)
    dxbc_x = pl.BlockSpec((CHUNK, 256), x_spec.index_map)

    def body(x_ref, b_ref, c_ref, dt_ref, bias_ref, alog_ref, dsk_ref, hin_ref, dy_ref,
             dx_ref, db_ref, dc_ref, ddt_ref, dbias_ref, dalog_ref, ddsk_ref, dh_scr):
        first = pl.program_id(1) == 0

        @pl.when(first)
        def _():
            dh_scr[...] = jnp.zeros_like(dh_scr)

        _, pullback = jax.vjp(_ssd_chunk, x_ref[...], b_ref[...], c_ref[...], dt_ref[...], bias_ref[...], alog_ref[...],
                              dsk_ref[...], hin_ref[...])
        g = pullback((dy_ref[...], dh_scr[...]))
        dx_ref[...], db_ref[...], dc_ref[...] = g[0], g[1], g[2]
        ddt_ref[...] = g[3].astype(ddt_ref.dtype)
        dh_scr[...] = g[7]
        for o_ref, val in ((dbias_ref, g[4]), (dalog_ref, g[5]), (ddsk_ref, g[6])):
            @pl.when(first)
            def _(o_ref=o_ref, val=val):
                o_ref[...] = val

            @pl.when(jnp.logical_not(first))
            def _(o_ref=o_ref, val=val):
                o_ref[...] += val

    n_chunks = SEQ // CHUNK
    out_b = pl.BlockSpec((CHUNK, N_STATE), lambda g, c: (n_chunks - 1 - c, g))
    res = pl.pallas_call(
        body, name="ssd_bwd", grid=(N_GROUPS, n_chunks),
        in_specs=[x_spec, b_spec, c_spec, dt_spec, vec_spec, vec_spec, vec_spec, h_spec, x_spec],
        out_specs=[dxbc_x, out_b, out_b, dt_spec, vec_spec, vec_spec, vec_spec],
        out_shape=[jax.ShapeDtypeStruct((SEQ, D_SSM), F32), jax.ShapeDtypeStruct((SEQ, N_GROUPS * N_STATE), F32),
                   jax.ShapeDtypeStruct((SEQ, N_GROUPS * N_STATE), F32), jax.ShapeDtypeStruct((SEQ, DT_PAD), BF16),
                   jax.ShapeDtypeStruct((1, DT_PAD), F32), jax.ShapeDtypeStruct((1, DT_PAD), F32),
                   jax.ShapeDtypeStruct((1, DT_PAD), F32)],
        scratch_shapes=[pltpu.VMEM((GROUP_WIDTH, N_STATE), F32)],
        compiler_params=_params(("parallel", "arbitrary")),
    )(xbc, xbc, xbc, dt_raw, bias, alog, dsk, h_in, dy)
    return res


CROSS_HEAD = 128
CROSS_ROWS = 512


def _cross_head(q, k, v, gq, gk):
    qn = _rms(q, gq) * (CROSS_HEAD ** -0.5)
    kn = _rms(k, gk)
    s = _bdot(qn, kn, NT)
    p = jnp.exp(s - lax.stop_gradient(jnp.max(s, axis=-1, keepdims=True)))
    return _bdot(p, v, NN) * (1.0 / jnp.sum(p, axis=-1, keepdims=True))


def _cross_specs():
    q_spec = pl.BlockSpec((CROSS_ROWS, CROSS_HEAD), lambda h, i: (i, h))
    k_spec = pl.BlockSpec((N_MEM, CROSS_HEAD), lambda h, i: (0, h))
    v_spec = pl.BlockSpec((N_MEM, CROSS_HEAD), lambda h, i: (0, 4 + h))
    g_spec = pl.BlockSpec((1, CROSS_HEAD), lambda h, i: (0, 0))
    return q_spec, k_spec, v_spec, g_spec


def _cross_fwd(qc, kv, gq, gk):
    q_spec, k_spec, v_spec, g_spec = _cross_specs()

    def body(q_ref, k_ref, v_ref, gq_ref, gk_ref, o_ref):
        o_ref[...] = _cross_head(q_ref[...], k_ref[...], v_ref[...], gq_ref[...], gk_ref[...]).astype(o_ref.dtype)

    return pl.pallas_call(
        body, name="cross_fwd", grid=(4, SEQ // CROSS_ROWS),
        in_specs=[q_spec, k_spec, v_spec, g_spec, g_spec], out_specs=q_spec,
        out_shape=jax.ShapeDtypeStruct((SEQ, D_CROSS), BF16),
        compiler_params=_params(("parallel", "parallel")),
    )(qc, kv, kv, gq, gk)


def _cross_bwd(qc, kv, gq, gk, do):
    q_spec, k_spec, v_spec, g_spec = _cross_specs()

    def body(q_ref, k_ref, v_ref, gq_ref, gk_ref, do_ref, dq_ref, dk_ref, dv_ref, dgq_ref, dgk_ref):
        _, pullback = jax.vjp(_cross_head, q_ref[...], k_ref[...], v_ref[...], gq_ref[...], gk_ref[...])
        dq, dk, dv, dgq, dgk = pullback(do_ref[...].astype(F32))
        dq_ref[...] = dq.astype(dq_ref.dtype)
        row0 = pl.program_id(1) == 0
        all0 = jnp.logical_and(row0, pl.program_id(0) == 0)
        for o_ref, val, init in ((dk_ref, dk, row0), (dv_ref, dv, row0), (dgq_ref, dgq, all0), (dgk_ref, dgk, all0)):
            @pl.when(init)
            def _(o_ref=o_ref, val=val):
                o_ref[...] = val

            @pl.when(jnp.logical_not(init))
            def _(o_ref=o_ref, val=val):
                o_ref[...] += val

    return pl.pallas_call(
        body, name="cross_bwd", grid=(4, SEQ // CROSS_ROWS),
        in_specs=[q_spec, k_spec, v_spec, g_spec, g_spec, q_spec],
        out_specs=[q_spec, k_spec, k_spec, g_spec, g_spec],
        out_shape=[jax.ShapeDtypeStruct((SEQ, D_CROSS), BF16), jax.ShapeDtypeStruct((N_MEM, D_CROSS), F32),
                   jax.ShapeDtypeStruct((N_MEM, D_CROSS), F32), jax.ShapeDtypeStruct((1, CROSS_HEAD), F32),
                   jax.ShapeDtypeStruct((1, CROSS_HEAD), F32)],
        compiler_params=_params(("arbitrary", "arbitrary")),
    )(qc, kv, kv, gq, gk, do)


def _loss_epilogue(acc, residual, target):
    err = acc + residual - target
    dy = err * (1.0 / D_MODEL)
    part = jnp.sum(jnp.sum(err * err, axis=1, keepdims=True), axis=0, keepdims=True) * (0.5 / D_MODEL)
    return dy, dy, part


def _pad_heads(v):
    return jnp.pad(v.reshape(N_GROUPS, HEADS_PER_GROUP), ((0, 0), (0, 128 - HEADS_PER_GROUP))).reshape(1, DT_PAD)


def _unpad_heads(v):
    return v.reshape(v.shape[0], N_GROUPS, 128)[:, :, :HEADS_PER_GROUP].reshape(v.shape[0], N_DT)


def _rope_tables(positions):
    half = ROT // 2
    inv_freq = ROPE_THETA ** (-2.0 * jnp.arange(half, dtype=F32) / ROT)
    ang = positions.reshape(SEQ, 1).astype(F32) * inv_freq
    cos, sin = jnp.cos(ang), jnp.sin(ang)
    ones, zeros = jnp.ones((SEQ, HEAD - ROT), F32), jnp.zeros((SEQ, HEAD - ROT), F32)
    cos_h = jnp.concatenate([cos, cos, ones], axis=1)
    sin_h = jnp.concatenate([-sin, sin, zeros], axis=1)
    return jnp.tile(cos_h, (1, 2)), jnp.tile(sin_h, (1, 2))


def _add_res(acc, res):
    return (acc + res,)


def _settle(grads, *after):
    if hasattr(grads, "settle"):
        grads.settle(*after)


def _take_token(grads):
    token = getattr(grads, "token", None)
    if token is None:
        return ()
    grads.token = None
    return (token,)


def _local_step(x, mem, positions, target, p, w, more_weights=None, grads=None, h=None):
    grads = {} if grads is None else grads
    w = dict(w)
    cos, sin = _rope_tables(positions)
    gq2, gk2 = jnp.tile(p["g_q"], (1, 2)), jnp.tile(p["g_k"], (1, 2))
    bias, alog, dsk = _pad_heads(p["dt_bias"]), _pad_heads(p["a_log"]), _pad_heads(p["d_skip"])
    norm_out = [(D_MODEL, BF16, D_MODEL, 0, False)]

    if h is None:
        h = _rowwise(_norm_fn, [_full(x)], [_full(p["g_mix"])], norm_out, name="norm_in")[0]
    proj = _matmul(h, w["w_in"], mode="nn", name="in_proj", outs=[F32], n_cols=D_MAIN)
    dt_raw = _matmul(h, w["w_dt"], mode="nn", name="dt_proj", outs=[F32])
    qk_rows = [(proj, 128, 0, True), (proj, 128, 8, True), (proj, 128, 16, True), _full(cos), _full(sin)]
    qk_vecs = [_full(gq2), _full(gk2)]
    qn, kn, vn = _rowwise(_qk_fn, qk_rows, qk_vecs, [(D_ATTN, F32, 128, 0, True)] * 3, name="qk_prep", groups=8, tr=1024)
    branches = [_attention_fwd(qn, kn, vn, b) for b in range(3)]
    merge_rows = [_full(o) for o, _ in branches] + [_full(lse) for _, lse in branches]
    attn = _rowwise(_merge_fn, merge_rows, [_full(p["g_attn_out"])], [(D_ATTN, BF16, D_ATTN, 0, False)], name="attn_merge")[0]
    xbc = _conv_fwd(proj, p["conv_w"], p["conv_b"])
    y_ssd, h_in = _ssd_fwd(xbc, dt_raw, bias, alog, dsk)
    gate_rows = [(y_ssd, 256, 0, True), (proj, 256, 12, True)]
    gate_vecs = [(p["g_ssm_out"], 256, 0, True)]
    ssm = _rowwise(_gate_fn, gate_rows, gate_vecs, [(D_SSM, BF16, 256, 0, True)], name="ssm_gate", groups=4)[0]
    mix = jnp.concatenate([attn, ssm], axis=1)
    if more_weights is not None:
        w.update(more_weights("mixer_done", mix))
    x1 = _matmul(mix, w["w_out"], mode="nn", name="out_proj", outs=[F32], extra=(x,), epilogue=_add_res)
    hc = _rowwise(_norm_fn, [_full(x1)], [_full(p["g_cross"])], norm_out, name="norm_cross")[0]
    memh = _rowwise(_norm_fn, [_full(mem)], [_full(p["g_mem"])], norm_out, name="norm_mem", n_rows=N_MEM)[0]
    qc = _matmul(hc, w["w_cq"], mode="nn", name="cq_proj", outs=[F32])
    if more_weights is not None:
        w.update(more_weights("cross_started", qc))
    kv = _matmul(memh, w["w_ckv"], mode="nn", name="ckv_proj", outs=[F32])
    oc = _cross_fwd(qc, kv, p["g_cq"], p["g_ck"])
    x2 = _matmul(oc, w["w_co"], mode="nn", name="co_proj", outs=[F32], extra=(x1,), epilogue=_add_res)
    hm = _rowwise(_norm_fn, [_full(x2)], [_full(p["g_mlp"])], norm_out, name="norm_mlp")[0]
    if more_weights is not None:
        w.update(more_weights("cross_done", hm))
    u, act = _matmul(hm, w["w_up"], mode="nn", name="up_proj", outs=[F32, BF16],
                     epilogue=lambda acc: (acc, jnp.square(jnp.maximum(acc, 0.0))))
    dy, dyb, loss_tiles = _matmul(act, w["w_down"], mode="nn", name="down_proj", outs=[F32, BF16], extra=(x2, target),
                                  epilogue=_loss_epilogue, tile_sums=1)
    loss = jnp.sum(loss_tiles).reshape(1, 1)

    grads["w_down"] = _matmul(act, dyb, mode="tn", name="dw_down", outs=[BF16], after=_take_token(grads))
    du = _matmul(dyb, w["w_down"], mode="nt", name="d_act", outs=[BF16], extra=(u,), after=_take_token(grads),
                 epilogue=lambda acc, uu: (acc * (2.0 * jnp.maximum(uu, 0.0)),))
    _settle(grads, du)
    grads["w_up"] = _matmul(hm, du, mode="tn", name="dw_up", outs=[BF16], col_shards=4, after=_take_token(grads))
    dhm = _matmul(du, w["w_up"], mode="nt", name="d_hm", outs=[F32], after=_take_token(grads), tk=4096)
    _settle(grads, dhm)
    dx2, grads["g_mlp"] = _rowwise_vjp(
        _norm_fn, [_full(x2)], [_full(p["g_mlp"])], [[_full(dhm)]],
        [(0, D_MODEL, F32, D_MODEL, 0, False, _full(dy))], [(0, D_MODEL, D_MODEL, 0, False)], name="norm_mlp_bwd")
    grads["w_co"] = _matmul(oc, dx2, mode="tn", name="dw_co", outs=[BF16], col_shards=4, after=_take_token(grads))
    doc = _matmul(dx2, w["w_co"], mode="nt", name="d_oc", outs=[BF16])
    dqc, dkc, dvc, grads["g_cq"], grads["g_ck"] = _cross_bwd(qc, kv, p["g_cq"], p["g_ck"], doc)
    grads["w_cq"] = _matmul(hc, dqc, mode="tn", name="dw_cq", outs=[BF16])
    dhc = _matmul(dqc, w["w_cq"], mode="nt", name="d_hc", outs=[F32])
    dkv = jnp.concatenate([dkc, dvc], axis=1)
    grads["w_ckv"] = _matmul(memh, dkv, mode="tn", name="dw_ckv", outs=[BF16])
    dmemh = _matmul(dkv, w["w_ckv"], mode="nt", name="d_memh", outs=[F32])
    grads["g_mem"] = _rowwise_vjp(_norm_fn, [_full(mem)], [_full(p["g_mem"])], [[_full(dmemh)]], [],
                                  [(0, D_MODEL, D_MODEL, 0, False)], name="norm_mem_bwd", n_rows=N_MEM)[0]
    dx1, grads["g_cross"] = _rowwise_vjp(
        _norm_fn, [_full(x1)], [_full(p["g_cross"])], [[_full(dhc)]],
        [(0, D_MODEL, F32, D_MODEL, 0, False, _full(dx2))], [(0, D_MODEL, D_MODEL, 0, False)], name="norm_cross_bwd")
    grads["w_out"] = _matmul(mix, dx1, mode="tn", name="dw_out", outs=[BF16])
    dmix = _matmul(dx1, w["w_out"], mode="nt", name="d_mix", outs=[F32], after=_take_token(grads))
    _settle(grads, dmix)
    merge_grads = [(i, D_ATTN, F32, D_ATTN, 0, False, None) for i in range(6)]
    *dol, grads["g_attn_out"] = _rowwise_vjp(
        _merge_fn, merge_rows, [_full(p["g_attn_out"])], [[(dmix, D_ATTN, 0, False)]],
        merge_grads, [(0, D_ATTN, D_ATTN, 0, False)], name="attn_merge_bwd", after=_take_token(grads))
    dqkv = [_attention_bwd(qn, kn, vn, *branches[b], dol[b], dol[3 + b], b) for b in range(3)]
    qk_cts = [[(dqkv[b][i], 128, 0, True) for b in range(3)] for i in range(3)]
    dq, dk, dv, dgq2, dgk2 = _rowwise_vjp(
        _qk_fn, qk_rows, qk_vecs, qk_cts, [(i, D_ATTN, BF16, 128, 0, True, None) for i in range(3)],
        [(0, 128, 128, 0, False), (1, 128, 128, 0, False)], name="qk_prep_bwd", groups=8, tr=512)
    grads["g_q"] = dgq2[:, :HEAD] + dgq2[:, HEAD:]
    grads["g_k"] = dgk2[:, :HEAD] + dgk2[:, HEAD:]
    dy_ssd, dz, grads["g_ssm_out"] = _rowwise_vjp(
        _gate_fn, gate_rows, gate_vecs, [[(dmix, 256, 4, True)]],
        [(0, D_SSM, F32, 256, 0, True, None), (1, D_SSM, BF16, 256, 0, True, None)],
        [(0, D_SSM, 256, 0, True)], name="ssm_gate_bwd", groups=4)
    dxs, db, dc, ddt, dbias, dalog, ddsk = _ssd_bwd(xbc, dt_raw, bias, alog, dsk, h_in, dy_ssd)
    grads["dt_bias"], grads["a_log"], grads["d_skip"] = _unpad_heads(dbias), _unpad_heads(dalog), _unpad_heads(ddsk)
    dxbc_raw, dconv_w, grads["conv_b"] = _conv_bwd(proj, p["conv_w"], p["conv_b"], dxs, db, dc)
    grads["conv_w"] = dconv_w[:4]
    dproj = jnp.concatenate([dq, dk, dv, dz, dxbc_raw], axis=1)
    grads["w_main"] = _matmul(h, dproj, mode="tn", name="dw_main", outs=[BF16], out_cols=D_MAIN + N_DT)
    grads["w_dt"] = _matmul(h, ddt, mode="tn", name="dw_dt", outs=[BF16])
    dh = _matmul(dproj, w["w_in"], mode="nt", name="d_h_main", outs=[F32], after=_take_token(grads))
    dh = _matmul(ddt, w["w_dt"], mode="nt", name="d_h_dt", outs=[F32], extra=(dh,), epilogue=_add_res)
    grad_x, grads["g_mix"] = _rowwise_vjp(
        _norm_fn, [_full(x)], [_full(p["g_mix"])], [[_full(dh)]],
        [(0, D_MODEL, F32, D_MODEL, 0, False, _full(dx1))], [(0, D_MODEL, D_MODEL, 0, False)], name="norm_in_bwd")
    return loss, grad_x, grads


MATRICES = ("w_in", "w_out", "w_cq", "w_ckv", "w_co", "w_up", "w_down")
ROW_SHARDED = ("w_out", "w_cq", "w_ckv", "w_down")
N_CHIPS = 4
ANY = pl.BlockSpec(memory_space=pl.ANY)


def _place():
    return lax.axis_index("x"), lax.axis_index("y"), lax.axis_index("c")


def _other_chips(x, y):
    return [(1 - x, y), (x, 1 - y), (1 - x, 1 - y)]


def _remote(src, dst, send_sem, recv_sem, device):
    return pltpu.make_async_remote_copy(src_ref=src, dst_ref=dst, send_sem=send_sem, recv_sem=recv_sem,
                                        device_id=device, device_id_type=MESH)


def _gathered_shape(name, shard):
    rows, cols = shard.shape
    if name == "w_in":
        return (N_CHIPS, rows, cols)
    return (N_CHIPS * rows, cols) if name in ROW_SHARDED else (rows, N_CHIPS * cols)


def _shard_window(name, ref, rows, cols, chip, half):
    r0, nr = (0, rows) if half is None else (half * (rows // 2), rows // 2)
    if name == "w_in":
        return ref.at[chip, pl.ds(r0, nr), :]
    if name in ROW_SHARDED:
        return ref.at[pl.ds(chip * rows + r0, nr), :]
    return ref.at[pl.ds(r0, nr), pl.ds(pl.multiple_of(chip * cols, 128), cols)]


def _cast_into_gathered(w, name, chip, after=()):
    rows, cols = w.shape
    tr = _tile(rows, ROW_TILE)

    def body(chip_ref, w_ref, *rest):
        rest[-1][...] = w_ref[...].astype(BF16)

    if name == "w_in":
        out_spec = pl.BlockSpec((None, tr, cols), lambda i, chip_ref: (chip_ref[0], i, 0))
    elif name in ROW_SHARDED:
        out_spec = pl.BlockSpec((tr, cols), lambda i, chip_ref: (chip_ref[0] * (rows // tr) + i, 0))
    else:
        out_spec = pl.BlockSpec((tr, cols), lambda i, chip_ref: (i, chip_ref[0]))
    grid_spec = pltpu.PrefetchScalarGridSpec(
        num_scalar_prefetch=1, grid=(rows // tr,),
        in_specs=[pl.BlockSpec((tr, cols), lambda i, chip_ref: (i, 0))] + [pl.BlockSpec(memory_space=pl.ANY)] * len(after),
        out_specs=out_spec)
    return pl.pallas_call(body, name="cast_" + name, grid_spec=grid_spec,
                          out_shape=jax.ShapeDtypeStruct(_gathered_shape(name, w), BF16),
                          compiler_params=_params(("parallel",)))(chip.reshape(1).astype(jnp.int32), w, *after)


def _w_in_columns(arr, to_shards):
    rows, piece = D_MODEL, (D_MAIN + N_DT) // N_CHIPS
    tr = ROW_TILE

    def body(a_ref, o_ref):
        for j in range(N_CHIPS):
            if to_shards:
                o_ref[j] = a_ref[:, pl.ds(piece * j, piece)]
            else:
                o_ref[:, pl.ds(piece * j, piece)] = a_ref[j]

    pieces = pl.BlockSpec((N_CHIPS, tr, piece), lambda i: (0, i, 0))
    matrix = pl.BlockSpec((tr, N_CHIPS * piece), lambda i: (i, 0))
    out_dims = (N_CHIPS, rows, piece) if to_shards else (rows, N_CHIPS * piece)
    return pl.pallas_call(
        body, name="w_in_to_shards" if to_shards else "w_in_from_shards", grid=(rows // tr,),
        in_specs=[matrix if to_shards else pieces], out_specs=pieces if to_shards else matrix,
        out_shape=jax.ShapeDtypeStruct(out_dims, arr.dtype), compiler_params=_params(("parallel",)))(arr)


HBM = pl.BlockSpec(memory_space=pltpu.HBM)
SEM = pl.BlockSpec(memory_space=pltpu.SEMAPHORE)
EFFECT = pltpu.SideEffectType.DATAFLOW_SIDE_EFFECTING


def _split_start(name, bufs, plan, counts, after=()):
    n, n_g, n_after = len(bufs), len(counts), len(after)

    def body(*refs):
        ins, sems, token = refs[:n], refs[n + n_after:n + n_after + 2 * n_g], refs[-1]
        for g, copies in enumerate(plan(ins)):
            for i, (src, dst, device, _) in enumerate(copies):
                _remote(src, dst, sems[2 * g].at[i], sems[2 * g + 1].at[i], device).start()
        token[...] = jnp.zeros_like(token)

    sem_shapes = [pltpu.SemaphoreType.DMA((cnt,)) for cnt in counts for _ in range(2)]
    res = pl.pallas_call(
        body, name=name,
        out_shape=(*sem_shapes, *[pltpu.HBM(b.shape, b.dtype) for b in bufs], jax.ShapeDtypeStruct((8, 128), F32)),
        in_specs=(*(HBM,) * n, *(ANY,) * n_after),
        out_specs=(*(SEM,) * (2 * n_g), *(HBM,) * n, pl.BlockSpec(memory_space=pltpu.VMEM)),
        input_output_aliases={i: 2 * n_g + i for i in range(n)},
        compiler_params=pltpu.CompilerParams(has_side_effects=EFFECT),
    )(*[pltpu.with_memory_space_constraint(b, pltpu.HBM) for b in bufs], *after)
    sems = [(res[2 * g], res[2 * g + 1]) for g in range(n_g)]
    return sems, list(res[2 * n_g:2 * n_g + n]), res[-1]


def _split_wait(name, bufs, sems, plan, *after):
    n = len(bufs)

    def body(*refs):
        ins, send, recv = refs[:n], refs[n], refs[n + 1]
        (copies,) = plan(ins)
        for i, (src, _, device, landing) in enumerate(copies):
            cp = _remote(src, landing, send.at[i], recv.at[i], device)
            cp.wait_send()
            cp.wait_recv()

    res = pl.pallas_call(
        body, name=name, out_shape=tuple(pltpu.HBM(b.shape, b.dtype) for b in bufs),
        in_specs=(*(HBM,) * n, SEM, SEM, *(ANY,) * len(after)), out_specs=(HBM,) * n,
        input_output_aliases={i: i for i in range(n)},
        compiler_params=pltpu.CompilerParams(has_side_effects=EFFECT),
    )(*bufs, sems[0], sems[1], *after)
    return list(res)


def _ici_plan(names, shard_shapes):
    def plan(refs):
        x, y, c = _place()
        copies = []
        for ref, name in zip(refs, names):
            win = _shard_window(name, ref, *shard_shapes[name], 2 * x + y, c)
            for px, py in _other_chips(x, y):
                copies.append((win, win, (px, py, c), _shard_window(name, ref, *shard_shapes[name], 2 * px + py, c)))
        return [copies]
    return plan


def _pass_on_plan(names, shard_shapes):
    def plan(refs):
        x, y, c = _place()
        copies = []
        for ref, name in zip(refs, names):
            for px, py in _other_chips(x, y):
                win = _shard_window(name, ref, *shard_shapes[name], 2 * px + py, c)
                copies.append((win, win, (x, y, 1 - c), _shard_window(name, ref, *shard_shapes[name], 2 * px + py, 1 - c)))
        return [copies]
    return plan


def _swap_plan(n_pairs):
    def plan(refs):
        x, y, c = _place()
        return [[(src.at[:, 1 - c], dst, (x, y, 1 - c), dst) for src, dst in zip(refs[:n_pairs], refs[n_pairs:])]]
    return plan


def _share_plan(n_pairs):
    def plan(refs):
        x, y, c = _place()
        return [[(src, dst, (x, y, 1 - c), dst)] for src, dst in zip(refs[:n_pairs], refs[n_pairs:])]
    return plan


def _scatter_plan(n_pairs):
    def plan(refs):
        x, y, c = _place()
        copies = []
        for src, dst in zip(refs[:n_pairs], refs[n_pairs:]):
            for k, (px, py) in enumerate(_other_chips(x, y)):
                copies.append((src.at[2 * px + py], dst.at[k], (px, py, c), dst.at[k]))
        return [copies]
    return plan


def _sibling_swap(arrs, name):
    n = len(arrs)

    def body(*refs):
        ins, outs, send, recv = refs[:n], refs[n:2 * n], refs[2 * n], refs[2 * n + 1]
        x, y, c = _place()
        cps = [_remote(ins[w].at[:, 1 - c], outs[w], send.at[w], recv.at[w], (x, y, 1 - c)) for w in range(n)]
        for cp in cps:
            cp.start()
        for cp in cps:
            cp.wait()

    return pl.pallas_call(
        body, name=name, in_specs=[ANY] * n, out_specs=[ANY] * n,
        out_shape=[jax.ShapeDtypeStruct((a.shape[0],) + a.shape[2:], a.dtype) for a in arrs],
        scratch_shapes=[pltpu.SemaphoreType.DMA((n,))] * 2,
    )(*arrs)


def _small_allreduce(buf, name):
    rows = buf.shape[0]

    def body(x_ref, out_ref, all_ref, send_sems, recv_sems, local_sem):
        x, y, c = _place()
        me, sibling, chips = (x, y, c), (x, y, 1 - c), _other_chips(x, y)

        def block(px, py, pc):
            return all_ref.at[pl.ds((4 * px + 2 * py + pc) * rows, rows), :]

        def copy(k, blk, to, src=None):
            return _remote(block(*blk) if src is None else src, block(*blk), send_sems.at[k], recv_sems.at[k], to)

        own = pltpu.make_async_copy(x_ref, block(*me), local_sem)
        own.start()
        first = [copy(0, me, sibling, src=x_ref)] + [copy(1 + j, me, (*chip, c), src=x_ref) for j, chip in enumerate(chips)]
        for cp in first:
            cp.start()
        passed = [copy(4 + j, (*chip, c), sibling) for j, chip in enumerate(chips)]
        for j, chip in enumerate(chips):
            copy(1 + j, (*chip, c), me).wait_recv()
            passed[j].start()
        copy(0, sibling, me).wait_recv()
        for j, chip in enumerate(chips):
            copy(4 + j, (*chip, 1 - c), me).wait_recv()
        for cp in first + passed:
            cp.wait_send()
        own.wait()
        acc = all_ref[pl.ds(0, rows), :]
        for d in range(1, 8):
            acc = acc + all_ref[pl.ds(d * rows, rows), :]
        out_ref[...] = acc

    vmem = pl.BlockSpec(memory_space=pltpu.VMEM)
    return pl.pallas_call(
        body, name=name, in_specs=[vmem], out_specs=vmem,
        out_shape=jax.ShapeDtypeStruct(buf.shape, F32),
        scratch_shapes=[pltpu.VMEM((8 * rows, 128), F32), pltpu.SemaphoreType.DMA((7,)), pltpu.SemaphoreType.DMA((7,)),
                        pltpu.SemaphoreType.DMA],
    )(buf)


ROW_TILE = 256
BIG_ROW_TILE = 1024


def _add_halves(arr, recv, c, name):
    _, _, hr, cols = arr.shape
    tr = _tile(hr, BIG_ROW_TILE)

    def body(c_ref, a_ref, r_ref, o_ref):
        o_ref[...] = (a_ref[...].astype(F32) + r_ref[...].astype(F32)).astype(o_ref.dtype)

    piece = pl.BlockSpec((None, tr, cols), lambda j, i, c_ref: (j, i, 0))
    grid_spec = pltpu.PrefetchScalarGridSpec(
        num_scalar_prefetch=1, grid=(N_CHIPS, hr // tr),
        in_specs=[pl.BlockSpec((None, None, tr, cols), lambda j, i, c_ref: (j, c_ref[0], i, 0)), piece], out_specs=piece)
    return pl.pallas_call(body, name=name, grid_spec=grid_spec, out_shape=jax.ShapeDtypeStruct(recv.shape, BF16),
                          compiler_params=_params(("parallel", "parallel")))(c.reshape(1).astype(jnp.int32), arr, recv)


def _flip_slot(d):
    return jnp.where(d == 1, 1, jnp.where(d == 3, 2, 0))


def _sum_chips(p, q, chip, name):
    _, hr, cols = p.shape
    tr = _tile(hr, BIG_ROW_TILE)

    def body(chip_ref, p_ref, q_ref, o_ref):
        j = pl.program_id(1)
        term = jnp.where(j == chip_ref[0], p_ref[...].astype(F32), q_ref[...].astype(F32))

        @pl.when(j == 0)
        def _():
            o_ref[...] = term

        @pl.when(j != 0)
        def _():
            o_ref[...] += term

    grid_spec = pltpu.PrefetchScalarGridSpec(
        num_scalar_prefetch=1, grid=(hr // tr, N_CHIPS),
        in_specs=[pl.BlockSpec((None, tr, cols), lambda i, j, chip_ref: (chip_ref[0], i, 0)),
                  pl.BlockSpec((None, tr, cols), lambda i, j, chip_ref: (_flip_slot(j ^ chip_ref[0]), i, 0))],
        out_specs=pl.BlockSpec((tr, cols), lambda i, j, chip_ref: (i, 0)))
    return pl.pallas_call(body, name=name, grid_spec=grid_spec, out_shape=jax.ShapeDtypeStruct((hr, cols), F32),
                          compiler_params=_params(("parallel", "arbitrary")))(chip.reshape(1).astype(jnp.int32), p, q)


def _adamw_halves(w, g_own, g_other, m, v, c, name):
    rows, cols = w.shape
    tr = _tile(rows // 2, ROW_TILE)
    per_half = rows // 2 // tr

    def body(c_ref, w_ref, own_ref, other_ref, m_ref, v_ref, g_ref, d_ref, nm_ref, nv_ref):
        mine = (pl.program_id(0) // per_half) == c_ref[0]
        g_ = jnp.where(mine, own_ref[...], other_ref[...])
        g_ref[...] = g_
        d_ref[...], nm_ref[...], nv_ref[...] = _adamw_math(w_ref[...], g_, m_ref[...], v_ref[...])

    blk = pl.BlockSpec((tr, cols), lambda i, c_ref: (i, 0))
    own = pl.BlockSpec((tr, cols), lambda i, c_ref: (jnp.where(i // per_half == c_ref[0], i % per_half, 0), 0))
    other = pl.BlockSpec((tr, cols), lambda i, c_ref: (jnp.where(i // per_half == c_ref[0], 0, i % per_half), 0))
    grid_spec = pltpu.PrefetchScalarGridSpec(num_scalar_prefetch=1, grid=(rows // tr,),
                                             in_specs=[blk, own, other, blk, blk], out_specs=[blk] * 4)
    return pl.pallas_call(body, name=name, grid_spec=grid_spec, out_shape=[jax.ShapeDtypeStruct(w.shape, F32)] * 4,
                          compiler_params=_params(("parallel",)))(c.reshape(1).astype(jnp.int32), w, g_own, g_other, m, v)


W_IN_COLS = (D_MAIN + N_DT) // N_CHIPS
W_IN_MAIN = W_IN_COLS // 128 * 128
W_IN_TAIL = W_IN_COLS - 128
W_IN_PARTS = ((0, W_IN_MAIN), (W_IN_TAIL, 128))


def _cast_w_in_transposed(w_t, chip, after=()):
    def body(chip_ref, w_ref, *rest):
        for start, size in W_IN_PARTS:
            rest[-1][:, pl.ds(start, size)] = w_ref[pl.ds(start, size), :].T.astype(BF16)

    grid_spec = pltpu.PrefetchScalarGridSpec(
        num_scalar_prefetch=1, grid=(D_MODEL // ROW_TILE,),
        in_specs=[pl.BlockSpec((W_IN_COLS, ROW_TILE), lambda i, chip_ref: (0, i))] + [pl.BlockSpec(memory_space=pl.ANY)] * len(after),
        out_specs=pl.BlockSpec((None, ROW_TILE, W_IN_COLS), lambda i, chip_ref: (chip_ref[0], i, 0)))
    return pl.pallas_call(body, name="cast_w_in", grid_spec=grid_spec,
                          out_shape=jax.ShapeDtypeStruct((N_CHIPS, D_MODEL, W_IN_COLS), BF16),
                          compiler_params=_params(("parallel",)))(chip.reshape(1).astype(jnp.int32), w_t, *after)


def _adamw_w_in_transposed(w_t, g_own, g_other, m_t, v_t, c):
    per_half = D_MODEL // 2 // ROW_TILE

    def body(c_ref, w_ref, own_ref, other_ref, m_ref, v_ref, g_ref, d_ref, nm_ref, nv_ref):
        mine = (pl.program_id(0) // per_half) == c_ref[0]
        for start, size in W_IN_PARTS:
            cols, rows = pl.ds(start, size), pl.ds(start, size)
            g_ = jnp.where(mine, own_ref[:, cols], other_ref[:, cols]).T
            g_ref[rows, :] = g_
            d_ref[rows, :], nm_ref[rows, :], nv_ref[rows, :] = _adamw_math(w_ref[rows, :], g_, m_ref[rows, :], v_ref[rows, :])

    blk = pl.BlockSpec((W_IN_COLS, ROW_TILE), lambda i, c_ref: (0, i))
    own = pl.BlockSpec((ROW_TILE, W_IN_COLS), lambda i, c_ref: (jnp.where(i // per_half == c_ref[0], i % per_half, 0), 0))
    other = pl.BlockSpec((ROW_TILE, W_IN_COLS), lambda i, c_ref: (jnp.where(i // per_half == c_ref[0], 0, i % per_half), 0))
    grid_spec = pltpu.PrefetchScalarGridSpec(num_scalar_prefetch=1, grid=(D_MODEL // ROW_TILE,),
                                             in_specs=[blk, own, other, blk, blk], out_specs=[blk] * 4)
    return pl.pallas_call(body, name="adamw_w_in", grid_spec=grid_spec, out_shape=[jax.ShapeDtypeStruct(w_t.shape, F32)] * 4,
                          compiler_params=_params(("parallel",)))(c.reshape(1).astype(jnp.int32), w_t, g_own, g_other, m_t, v_t)


def _adamw_math(w, g, m, v):
    m_new = ADAM_B1 * m + (1.0 - ADAM_B1) * g
    v_new = ADAM_B2 * v + (1.0 - ADAM_B2) * (g * g)
    m_hat = m_new / (1.0 - ADAM_B1 ** ADAM_STEP)
    v_hat = v_new / (1.0 - ADAM_B2 ** ADAM_STEP)
    return -ADAM_LR * (m_hat / (jnp.sqrt(v_hat) + ADAM_EPS) + ADAM_WD * w), m_new, v_new


VECTORS = ("g_mix", "g_q", "g_k", "g_attn_out", "conv_b", "dt_bias", "a_log", "d_skip", "g_ssm_out", "g_cross", "g_mem",
           "g_cq", "g_ck", "g_mlp")
WEIGHTS = ("g_mix", "w_in", "g_q", "g_k", "g_attn_out", "conv_w", "conv_b", "dt_bias", "a_log", "d_skip", "g_ssm_out", "w_out",
           "g_cross", "g_mem", "w_cq", "w_ckv", "g_cq", "g_ck", "w_co", "g_mlp", "w_up", "w_down")


def _pack(parts):
    flat = jnp.concatenate([t.reshape(-1) for t in parts])
    total = -(-flat.shape[0] // 1024) * 1024
    return jnp.pad(flat, (0, total - flat.shape[0])).reshape(total // 128, 128)


def _rows_of(n):
    return -(-n // 128)


def _slot_rows(n):
    return -(-n // 1024) * 8


def _pack_rows(parts):
    rows = []
    for t in parts:
        flat = t.reshape(-1)
        rows.append(jnp.pad(flat, (0, 128 * _slot_rows(flat.shape[0]) - flat.shape[0])).reshape(-1, 128))
    return jnp.concatenate(rows)


def _adamw_vectors(summed, chip, vectors, conv):
    groups = list(vectors) + [conv]
    offsets, row = [], 0
    for w, _, _ in groups:
        offsets.append(row)
        row += _slot_rows(w.shape[1]) if w.shape[0] == 1 else _slot_rows(4 * N_CHIPS * w.shape[1])
    conv_blocks = _rows_of(conv[0].shape[1])

    def body(chip_ref, sum_ref, *refs):
        ins, outs = refs[:3 * len(groups)], refs[3 * len(groups):]

        def update(i, g, idx):
            w_ref, m_ref, v_ref = ins[3 * i:3 * i + 3]
            delta, new_m, new_v = _adamw_math(w_ref[idx], g, m_ref[idx], v_ref[idx])
            for o_ref, val in zip(outs[4 * i:4 * i + 4], (g, delta, new_m, new_v)):
                o_ref[idx] = val

        for i, (w, _, _) in enumerate(vectors):
            for t in range(_rows_of(w.shape[1])):
                width = min(128, w.shape[1] - 128 * t)
                update(i, sum_ref[pl.ds(offsets[i] + t, 1), pl.ds(0, width)], (slice(None), pl.ds(128 * t, width)))
        for tap in range(4):
            for blk in range(conv_blocks):
                src = offsets[-1] + tap * N_CHIPS * conv_blocks + chip_ref[0] * conv_blocks + blk
                update(len(vectors), sum_ref[pl.ds(src, 1), :], (pl.ds(tap, 1), pl.ds(128 * blk, 128)))

    def whole(a):
        return pl.BlockSpec(a.shape, lambda i, chip_ref: (0,) * a.ndim)

    operands = [t for group in groups for t in group]
    grid_spec = pltpu.PrefetchScalarGridSpec(
        num_scalar_prefetch=1, grid=(1,), in_specs=[whole(summed)] + [whole(t) for t in operands],
        out_specs=[whole(w) for w, _, _ in groups for _ in range(4)])
    res = pl.pallas_call(body, name="adamw_vectors", grid_spec=grid_spec,
                         out_shape=[jax.ShapeDtypeStruct(w.shape, F32) for w, _, _ in groups for _ in range(4)],
                         compiler_params=_params(("arbitrary",)))(chip.reshape(1).astype(jnp.int32), summed, *operands)
    return [res[4 * i:4 * i + 4] for i in range(len(groups))]


def _unpack(buf, shapes):
    flat, out, pos = buf.reshape(-1), [], 0
    for shape in shapes:
        size = math.prod(shape)
        out.append(flat[pos:pos + size].reshape(shape))
        pos += size
    return out


def kernel(x, mem, positions, g_mix, w_in, g_q, g_k, g_attn_out, conv_w, conv_b, dt_bias, a_log, d_skip, g_ssm_out, w_out, g_cross, g_mem, w_cq, w_ckv, g_cq, g_ck, w_co, g_mlp, w_up, w_down, loss_target, m_g_mix, m_w_in, m_g_q, m_g_k, m_g_attn_out, m_conv_w, m_conv_b, m_dt_bias, m_a_log, m_d_skip, m_g_ssm_out, m_w_out, m_g_cross, m_g_mem, m_w_cq, m_w_ckv, m_g_cq, m_g_ck, m_w_co, m_g_mlp, m_w_up, m_w_down, v_g_mix, v_w_in, v_g_q, v_g_k, v_g_attn_out, v_conv_w, v_conv_b, v_dt_bias, v_a_log, v_d_skip, v_g_ssm_out, v_w_out, v_g_cross, v_g_mem, v_w_cq, v_w_ckv, v_g_cq, v_g_ck, v_w_co, v_g_mlp, v_w_up, v_w_down):
    args = dict(locals())
    weights = {n: args[n][0] for n in WEIGHTS}
    mom_m = {n: args["m_" + n][0] for n in WEIGHTS}
    mom_v = {n: args["v_" + n][0] for n in WEIGHTS}
    x_idx, y_idx, c_idx = _place()
    chip = 2 * x_idx + y_idx

    conv_parts = _small_allreduce(_pack([jnp.zeros((N_CHIPS, 4, 512), F32).at[chip].set(0.5 * weights["conv_w"])]),
                                  "gather_conv_taps")
    shapes = {n: weights[n].shape for n in MATRICES}
    first, mid, late = ("w_in",), ("w_out", "w_cq", "w_ckv", "w_co"), ("w_up", "w_down")
    w_in_t, m_in_t, v_in_t = (jnp.swapaxes(t, 1, 2)[0] for t in (w_in, m_w_in, v_w_in))
    w_in_buf = [_cast_w_in_transposed(w_in_t, chip)]
    sems_in, w_in_buf, token = _split_start("gather_ici_start_w_in", w_in_buf, _ici_plan(first, shapes), [3], after=(conv_parts,))
    bufs = [_cast_into_gathered(weights[n], n, chip, after=(token,)) for n in mid + late]
    params = {n: weights[n].reshape(1, -1) for n in VECTORS}
    h_in = _rowwise(_norm_fn, [_full(x[0])], [_full(params["g_mix"])], [(D_MODEL, BF16, D_MODEL, 0, False)], name="norm_in",
                    after=(token,))[0]
    w_in_buf = _split_wait("gather_ici_wait_w_in", w_in_buf, sems_in[0], _ici_plan(first, shapes), token, h_in, m_in_t, v_in_t,
                           *bufs)
    pass_sems, w_in_buf, token = _split_start("gather_pass_start_w_in", w_in_buf, _pass_on_plan(first, shapes), [3])
    plan = lambda refs: _ici_plan(mid, shapes)(refs[:4]) + _ici_plan(late, shapes)(refs[4:])
    sems_rest, bufs, token = _split_start("gather_ici_start_rest", bufs, plan, [12, 6], after=(token,))
    w_in_buf = _split_wait("gather_pass_wait_w_in", w_in_buf, pass_sems[0], _pass_on_plan(first, shapes), token)
    w_in_full = _w_in_columns(w_in_buf[0], to_shards=False)
    full = {"w_in": w_in_full,
            "w_dt": jnp.pad(w_in_full[:, D_MAIN:].reshape(D_MODEL, N_GROUPS, HEADS_PER_GROUP),
                            ((0, 0), (0, 0), (0, 128 - HEADS_PER_GROUP))).reshape(D_MODEL, DT_PAD)}
    in_flight = {}

    def more_weights(stage, after):
        if stage == "mixer_done":
            got = _split_wait("gather_ici_wait_mid", bufs[:4], sems_rest[0], _ici_plan(mid, shapes), after)
            sems, got, token = _split_start("gather_pass_start_mid", got, _pass_on_plan(mid, shapes), [12])
            return dict(zip(mid, _split_wait("gather_pass_wait_mid", got, sems[0], _pass_on_plan(mid, shapes), token)))
        if stage == "cross_started":
            got = _split_wait("gather_ici_wait_late", bufs[4:], sems_rest[1], _ici_plan(late, shapes), after)
            in_flight["late"] = _split_start("gather_pass_start_late", got, _pass_on_plan(late, shapes), [6])
            return {}
        sems, got, token = in_flight.pop("late")
        return dict(zip(late, _split_wait("gather_pass_wait_late", got, sems[0], _pass_on_plan(late, shapes), token, after)))

    params["conv_w"] = _unpack(conv_parts, [(N_CHIPS, 4, 512)])[0].transpose(1, 0, 2).reshape(4, 4 * 512)

    groups = (("w_down",), ("w_up",), ("w_co", "w_cq", "w_ckv", "w_out"), ("w_in",))
    scattered = []

    class GradStore(dict):
        pending = None

        def __setitem__(self, name, value):
            super().__setitem__(name, value)
            if "w_main" in self and "w_dt" in self and "w_in" not in self:
                gw_in = lax.dynamic_update_slice(self["w_main"], _unpad_heads(self["w_dt"]), (0, D_MAIN))
                self["w_in"] = _w_in_columns(gw_in, to_shards=True)
            for group in groups:
                if name in group and all(n in self for n in group):
                    self.settle()
                    pieces = [self[n].reshape(N_CHIPS, 2, shapes[n][0] // 2, shapes[n][1]) for n in group]
                    if group == groups[-1]:
                        self.scatter(group, pieces, _sibling_swap(pieces, "grad_swap_" + group[0]))
                    else:
                        landing = [lax.empty((N_CHIPS,) + a.shape[2:], BF16) for a in pieces]
                        sems, thru, self.token = _split_start("grad_swap_start_" + group[0], pieces + landing,
                                                              _swap_plan(len(pieces)), [len(pieces)])
                        self.pending = (group, sems[0], thru)

        def settle(self, *after):
            if self.pending is not None:
                group, sems, thru = self.pending
                self.pending = None
                thru = _split_wait("grad_swap_wait_" + group[0], thru, sems, _swap_plan(len(group)), *after)
                self.scatter(group, thru[:len(group)], thru[len(group):])

        def scatter(self, group, pieces, from_sibling):
            sums = [_add_halves(a, r, c_idx, "add_halves_" + n) for n, a, r in zip(group, pieces, from_sibling)]
            landing = [lax.empty((3,) + s.shape[1:], BF16) for s in sums]
            sems, thru, self.token = _split_start("grad_scatter_start_" + group[0], sums + landing,
                                                  _scatter_plan(len(sums)), [3 * len(sums)])
            scattered.append((group, sems[0], thru))

    loss, grad_x, grads = _local_step(x[0], mem[0], positions[0], loss_target[0], params, full, more_weights, GradStore(),
                                      h_in)

    out_g, out_d, out_m, out_v = {}, {}, {}, {}

    def finish(entries, order, token):
        halves = {}
        for group, sems, thru in entries:
            thru = _split_wait("grad_scatter_wait_" + group[0], thru, sems, _scatter_plan(len(group)), token)
            for i, n in enumerate(group):
                halves[n] = _sum_chips(thru[i], thru[len(group) + i], chip, "sum_chips_" + n)
        sources = [halves[n] for n in order]
        landing = [lax.empty(s.shape, F32) for s in sources]
        sems, thru, token = _split_start("grad_share_start_" + order[0], sources + landing, _share_plan(len(order)),
                                         [1] * len(order))
        for i, n in enumerate(order):
            own, other = _split_wait("grad_share_wait_" + n, [thru[i], thru[len(order) + i]], sems[i], _share_plan(1), token)
            if n == "w_in":
                res_t = _adamw_w_in_transposed(w_in_t, own, other, m_in_t, v_in_t, c_idx)
                out_g[n], out_d[n], out_m[n], out_v[n] = (t.T for t in res_t)
            else:
                out_g[n], out_d[n], out_m[n], out_v[n] = _adamw_halves(weights[n], own, other, mom_m[n], mom_v[n], c_idx,
                                                                       "adamw_" + n)
            token = out_v[n]
        return token

    token = finish(scattered[:-1], ("w_cq", "w_co", "w_ckv", "w_out", "w_up", "w_down"), grad_x)
    finish(scattered[-1:], ("w_in",), token)

    names = VECTORS + ("conv_w",)
    summed = _small_allreduce(_pack_rows([grads[n] for n in names] + [loss]), "allreduce_vectors")
    total_loss = summed[sum(_slot_rows(grads[n].size) for n in names), 0]
    small_out = _adamw_vectors(summed, chip, [(args[n], args["m_" + n], args["v_" + n]) for n in VECTORS],
                               (weights["conv_w"], mom_m["conv_w"], mom_v["conv_w"]))
    for n, res in zip(names, small_out):
        out_g[n], out_d[n], out_m[n], out_v[n] = (t.reshape(weights[n].shape) for t in res)

    outs =[total_loss, grad_x[None]]
    for group in (out_g, out_d, out_m, out_v):
        outs += [group[n][None] for n in WEIGHTS]
    return tuple(outs)
```

```python
import functools
import math

import jax
import jax.numpy as jnp
from jax import lax
from jax.experimental import pallas as pl
from jax.experimental.pallas import tpu as pltpu

F32 = jnp.float32
BF16 = jnp.bfloat16

SEQ = 2048
D_MODEL = 2048
HEAD = 64
D_ATTN = 1024
D_SSM = 1024
N_GROUPS = 4
N_STATE = 128
CHUNK = 128
ATT_BLK = 128
N_MEM = 256
D_CROSS = 512
D_FF = 8192
D_MAIN = 6144
N_DT = 16
DT_PAD = 512
ROT = 16
ROPE_THETA = 500000.0
EPS = 1e-6
NEG = -1e30
BRANCH_BLOCKS = (16, 4, 1)
DILATIONS = (1, 4, 16)

ADAM_LR, ADAM_B1, ADAM_B2, ADAM_EPS, ADAM_WD, ADAM_STEP = 0.001, 0.9, 0.999, 1e-08, 0.01, 10

VMEM_LIMIT = 56 * 1024 * 1024
MESH = pl.DeviceIdType.MESH


def _params(sem, **kw):
    return pltpu.CompilerParams(dimension_semantics=sem, vmem_limit_bytes=VMEM_LIMIT, **kw)


def _bdot(a, b, dims):
    return lax.dot_general(a.astype(BF16), b.astype(BF16), (dims, ((), ())), preferred_element_type=F32)


def _fdot(a, b, dims):
    return lax.dot_general(a, b, (dims, ((), ())), preferred_element_type=F32, precision=lax.Precision.HIGHEST)


NN = ((1,), (0,))
NT = ((1,), (1,))
TN = ((0,), (0,))


def _tile(n, want):
    t = min(n, want)
    while n % t:
        t //= 2
    return t


def _matmul(a, b, *, mode, name, outs, extra=(), epilogue=None, col_shards=1, after=(), n_cols=None, out_cols=None,
            tile_sums=0, tm=1024, tn=1024, tk=2048):
    if mode == "nn":
        (m, k), n = a.shape, b.shape[1]
    elif mode == "nt":
        (m, k), n = a.shape, b.shape[0]
    else:
        (k, m), n = a.shape, b.shape[1]
    n = n if n_cols is None else n_cols
    tm, tn, tk = _tile(m, tm), _tile(n // col_shards, tn), _tile(k, tk)
    nk = k // tk
    per_shard = n // col_shards // tn
    dims = {"nn": NN, "nt": NT, "tn": TN}[mode]
    a_spec = pl.BlockSpec((tk, tm), lambda i, j, kk: (kk, i)) if mode == "tn" else pl.BlockSpec((tm, tk), lambda i, j, kk: (i, kk))
    b_spec = pl.BlockSpec((tn, tk), lambda i, j, kk: (j, kk)) if mode == "nt" else pl.BlockSpec((tk, tn), lambda i, j, kk: (kk, j))
    o_spec = pl.BlockSpec((tm, tn), lambda i, j, kk: (i, j))
    n_extra, n_out, n_after = len(extra), len(outs), len(after)

    def body(a_ref, b_ref, *rest):
        extra_refs, out_refs, acc_ref = rest[:n_extra], rest[n_extra + n_after:-1], rest[-1]

        def finish(acc):
            res = (acc,) if epilogue is None else epilogue(acc, *[e[...] for e in extra_refs])
            for o_ref, r in zip(out_refs[:n_out], res):
                o_ref[...] = r.astype(o_ref.dtype)
            for o_ref, r in zip(out_refs[n_out:], res[n_out:]):
                o_ref[...] = jnp.broadcast_to(r, o_ref.shape)

        if nk == 1:
            finish(_bdot(a_ref[...], b_ref[...], dims))
            return
        kk = pl.program_id(2)

        @pl.when(kk == 0)
        def _():
            acc_ref[...] = jnp.zeros_like(acc_ref)

        acc_ref[...] += _bdot(a_ref[...], b_ref[...], dims)

        @pl.when(kk == nk - 1)
        def _():
            finish(acc_ref[...])

    if col_shards == 1:
        out_specs, out_dims = [o_spec] * n_out, (m, n if out_cols is None else out_cols)
    else:
        sharded = pl.BlockSpec((None, tm, tn), lambda i, j, kk: (j // per_shard, i, j % per_shard))
        out_specs, out_dims = [sharded] * n_out, (col_shards, m, n // col_shards)
    res = pl.pallas_call(
        body, name=name, grid=(m // tm, n // tn, nk),
        in_specs=[a_spec, b_spec] + [o_spec] * n_extra + [pl.BlockSpec(memory_space=pl.ANY)] * n_after,
        out_specs=out_specs + [pl.BlockSpec((8, 128), lambda i, j, kk: (i, j))] * tile_sums,
        out_shape=[jax.ShapeDtypeStruct(out_dims, dt) for dt in outs]
        + [jax.ShapeDtypeStruct((m // tm * 8, n // tn * 128), F32)] * tile_sums,
        scratch_shapes=[pltpu.VMEM((tm, tn) if nk > 1 else (8, 128), F32)],
        compiler_params=_params(("parallel", "parallel", "arbitrary")),
    )(a, b, *extra, *after)
    res = list(res[:n_out]) + [t[::8, ::128] for t in res[n_out:]]
    return res[0] if len(res) == 1 else res


def _row_spec(tr, bw, cb, per_group):
    return pl.BlockSpec((tr, bw), (lambda g, i: (i, cb + g)) if per_group else (lambda g, i: (i, cb)))


def _vec_spec(bw, cb, per_group):
    return pl.BlockSpec((1, bw), (lambda g, i: (0, cb + g)) if per_group else (lambda g, i: (0, cb)))


def _rowwise(fn, rows, vecs, outs, *, name, n_rows=SEQ, tr=256, groups=1, after=()):
    n_r, n_v, n_after = len(rows), len(vecs), len(after)

    def body(*refs):
        vals = [r[...].astype(F32) for r in refs[:n_r + n_v]]
        res = fn(*vals)
        for o_ref, r in zip(refs[n_r + n_v + n_after:], res):
            o_ref[...] = r.astype(o_ref.dtype)

    res = pl.pallas_call(
        body, name=name, grid=(groups, n_rows // tr),
        in_specs=[_row_spec(tr, bw, cb, pg) for _, bw, cb, pg in rows] + [_vec_spec(bw, cb, pg) for _, bw, cb, pg in vecs]
        + [pl.BlockSpec(memory_space=pl.ANY)] * n_after,
        out_specs=[_row_spec(tr, bw, cb, pg) for _, _, bw, cb, pg in outs],
        out_shape=[jax.ShapeDtypeStruct((n_rows, w), dt) for w, dt, _, _, _ in outs],
        compiler_params=_params(("parallel", "parallel")),
    )(*[r[0] for r in rows], *[v[0] for v in vecs], *after)
    return res


def _rowwise_vjp(fn, rows, vecs, cts, row_grads, vec_grads, *, name, n_rows=SEQ, tr=256, groups=1, after=()):
    n_r, n_v, n_after = len(rows), len(vecs), len(after)
    ct_ops = [op for group in cts for op in group]
    ct_sizes = [len(group) for group in cts]
    res_ops = [g[6] for g in row_grads if g[6] is not None]
    n_ct, n_res, n_rg = len(ct_ops), len(res_ops), len(row_grads)

    def body(*refs):
        vals = [r[...].astype(F32) for r in refs[:n_r + n_v]]
        pos = n_r + n_v
        ct_vals = []
        for size in ct_sizes:
            acc = refs[pos][...].astype(F32)
            for t in range(1, size):
                acc = acc + refs[pos + t][...].astype(F32)
            ct_vals.append(acc)
            pos += size
        res_refs = refs[pos:pos + n_res]
        out_refs = refs[pos + n_res + n_after:]
        _, pullback = jax.vjp(fn, *vals)
        grads = pullback(tuple(ct_vals))
        r_i = 0
        for o_ref, g in zip(out_refs[:n_rg], row_grads):
            val = grads[g[0]]
            if g[6] is not None:
                val = val + res_refs[r_i][...].astype(F32)
                r_i += 1
            o_ref[...] = val.astype(o_ref.dtype)
        first = (pl.program_id(1) == 0)
        for o_ref, g in zip(out_refs[n_rg:], vec_grads):
            val = jnp.sum(grads[n_r + g[0]], axis=0, keepdims=True)
            init = first if g[4] else jnp.logical_and(first, pl.program_id(0) == 0)

            @pl.when(init)
            def _(o_ref=o_ref, val=val):
                o_ref[...] = val

            @pl.when(jnp.logical_not(init))
            def _(o_ref=o_ref, val=val):
                o_ref[...] += val

    in_specs = [_row_spec(tr, bw, cb, pg) for _, bw, cb, pg in rows] + [_vec_spec(bw, cb, pg) for _, bw, cb, pg in vecs]
    in_specs += [_row_spec(tr, bw, cb, pg) for _, bw, cb, pg in ct_ops + res_ops] + [pl.BlockSpec(memory_space=pl.ANY)] * n_after
    out_specs =[_row_spec(tr, g[3], g[4], g[5]) for g in row_grads] + [_vec_spec(g[2], g[3], g[4]) for g in vec_grads]
    out_shape = [jax.ShapeDtypeStruct((n_rows, g[1]), g[2]) for g in row_grads]
    out_shape += [jax.ShapeDtypeStruct((1, g[1]), F32) for g in vec_grads]
    return pl.pallas_call(
        body, name=name, grid=(groups, n_rows // tr),
        in_specs=in_specs, out_specs=out_specs, out_shape=out_shape,
        compiler_params=_params(("arbitrary", "arbitrary")),
    )(*[r[0] for r in rows], *[v[0] for v in vecs], *[c[0] for c in ct_ops], *[r[0] for r in res_ops], *after)


def _full(arr, width=None):
    return (arr, arr.shape[1] if width is None else width, 0, False)


def _make_xor(sh):
    def raw(x):
        n = x.shape[-1]
        lane = lax.broadcasted_iota(jnp.int32, x.shape, x.ndim - 1)
        up = pltpu.roll(x, n - sh, x.ndim - 1)
        down = pltpu.roll(x, sh, x.ndim - 1)
        return jnp.where((lane & sh) == 0, up, down)

    f = jax.custom_vjp(raw)
    f.defvjp(lambda x: (raw(x), None), lambda _, ct: (raw(ct),))
    return f


_SWAP_ROPE_HALVES = _make_xor(ROT // 2)


def _head_sum(x):
    n = x.shape[-1]
    same_head = (lax.broadcasted_iota(jnp.int32, (n, n), 0) // HEAD) == (lax.broadcasted_iota(jnp.int32, (n, n), 1) // HEAD)
    return _fdot(x, same_head.astype(F32), NN)


def _rms(x, g):
    return x * lax.rsqrt(jnp.mean(x * x, axis=-1, keepdims=True) + EPS) * g


def _head_rms_rope(x, g, cos, sin, scale):
    y = x * lax.rsqrt(_head_sum(x * x) * (1.0 / HEAD) + EPS) * g
    return (y * cos + _SWAP_ROPE_HALVES(y) * sin) * scale


def _qk_fn(q, k, v, cos, sin, gq, gk):
    return (_head_rms_rope(q, gq, cos, sin, HEAD ** -0.5), _head_rms_rope(k, gk, cos, sin, 1.0), v)


def _norm_fn(x, g):
    return (_rms(x, g),)


def _merge_fn(o0, o1, o2, l0, l1, l2, g):
    m = lax.stop_gradient(jnp.maximum(jnp.maximum(l0, l1), l2))
    e0, e1, e2 = jnp.exp(l0 - m), jnp.exp(l1 - m), jnp.exp(l2 - m)
    mix = (e0 * o0 + e1 * o1 + e2 * o2) / (e0 + e1 + e2)
    return (_rms(mix, g),)


def _gate_fn(y, z, g):
    return (_rms(y * (z * jax.nn.sigmoid(z)), g),)


def _attn_pair(q, kc, vc, kp=None, vp=None, has_prev=None):
    pick0, pick1 = _head_picks()
    k_band, v_band, mask = _attn_band(kc, vc, kp, vp, has_prev)
    s = jnp.where(mask, _bdot(jnp.concatenate([q * pick0, q * pick1], axis=0), k_band, NT), NEG)
    m = jnp.max(s, axis=-1, keepdims=True)
    p = jnp.exp(s - m)
    den = jnp.sum(p, axis=-1, keepdims=True)
    acc = _bdot(p, v_band, NN) * (1.0 / den)
    lse_rows = m + jnp.log(den)
    o = pick0 * acc[:ATT_BLK] + pick1 * acc[ATT_BLK:]
    lse = pick0 * lse_rows[:ATT_BLK] + pick1 * lse_rows[ATT_BLK:]
    return o, lse


def _head_picks():
    lane = lax.broadcasted_iota(jnp.int32, (1, 2 * HEAD), 1)
    return (lane < HEAD).astype(F32), (lane >= HEAD).astype(F32)


def _attn_band(kc, vc, kp, vp, has_prev):
    n_keys = ATT_BLK if kp is None else 2 * ATT_BLK
    qi = lax.broadcasted_iota(jnp.int32, (2 * ATT_BLK, n_keys), 0) & (ATT_BLK - 1)
    kj = lax.broadcasted_iota(jnp.int32, (2 * ATT_BLK, n_keys), 1)
    if kp is None:
        return kc, vc, qi >= kj
    in_prev = jnp.logical_and(jnp.logical_and(kj < ATT_BLK, kj >= qi), has_prev)
    mask = jnp.logical_or(in_prev, jnp.logical_and(kj >= ATT_BLK, qi >= kj - ATT_BLK))
    return jnp.concatenate([kp, kc], axis=0), jnp.concatenate([vp, vc], axis=0), mask


def _attn_config(b):
    r = DILATIONS[b]
    return r, ATT_BLK * r, (512 if r == 1 else 128), BRANCH_BLOCKS[b] > 1


def _for_residues(r, fn):
    if r <= 4:
        for rho in range(r):
            fn(rho)
    else:
        def step(t, carry):
            for u in range(4):
                fn(4 * t + u)
            return carry

        lax.fori_loop(0, r // 4, step, 0)


def _strided_rows(start, r):
    if r > 1:
        return pl.ds(start, ATT_BLK, stride=r)
    return pl.ds(start if isinstance(start, int) else pl.multiple_of(start, ATT_BLK), ATT_BLK)


def _attention_fwd(qn, kn, vn, b):
    r, rows, lanes, with_prev = _attn_config(b)
    cur = pl.BlockSpec((rows, lanes), lambda g, n: (n, g))
    prev = pl.BlockSpec((rows, lanes), lambda g, n: (jnp.maximum(n - 1, 0), g))

    def body(*refs):
        ins, (o_ref, l_ref) = refs[:-2], refs[-2:]
        has_prev = pl.program_id(1) > 0

        def one(rho):
            sub = _strided_rows(rho, r)
            for pair in range(lanes // 128):
                sl = pl.ds(pair * 128, 128)
                args = [ref[sub, sl] for ref in ins] + ([has_prev] if with_prev else [])
                o_ref[sub, sl], l_ref[sub, sl] = _attn_pair(*args)

        _for_residues(r, one)

    operands = (qn, kn, vn, kn, vn) if with_prev else (qn, kn, vn)
    return pl.pallas_call(
        body, name="attn_fwd_%d" % r, grid=(D_ATTN // lanes, SEQ // rows),
        in_specs=[cur, cur, cur] + ([prev, prev] if with_prev else []), out_specs=[cur, cur],
        out_shape=[jax.ShapeDtypeStruct((SEQ, D_ATTN), F32)] * 2,
        compiler_params=_params(("parallel", "parallel")),
    )(*operands)


def _attn_pair_bwd(q, kc, vc, kp, vp, o, lse, do, dl, has_prev):
    pick0, pick1 = _head_picks()
    lane = lax.broadcasted_iota(jnp.int32, (1, 2 * HEAD), 1)
    k_band, v_band, mask = _attn_band(kc, vc, kp, vp, has_prev)
    q2 = jnp.concatenate([q * pick0, q * pick1], axis=0)
    do2 = jnp.concatenate([do * pick0, do * pick1], axis=0)
    lse2 = jnp.concatenate([jnp.sum(lse * (lane == 0).astype(F32), axis=-1, keepdims=True),
                            jnp.sum(lse * (lane == HEAD).astype(F32), axis=-1, keepdims=True)], axis=0)
    base = jnp.sum(jnp.concatenate([dl * pick0, dl * pick1], axis=0) - do2 * jnp.concatenate([o, o], axis=0),
                   axis=-1, keepdims=True)
    p = jnp.exp(jnp.where(mask, _bdot(q2, k_band, NT), NEG) - lse2)
    ds = p * (_bdot(do2, v_band, NT) + base)
    dq2 = _bdot(ds, k_band, NN)
    dq = pick0 * dq2[:ATT_BLK] + pick1 * dq2[ATT_BLK:]
    dk, dv = _bdot(ds, q2, TN), _bdot(p, do2, TN)
    if kp is None:
        return dq, dk, dv
    return dq, dk[ATT_BLK:], dv[ATT_BLK:], dk[:ATT_BLK], dv[:ATT_BLK]


def _attention_bwd(qn, kn, vn, o, lse, do, dl, b):
    r, rows, lanes, with_prev = _attn_config(b)
    cur = pl.BlockSpec((rows, lanes), lambda g, n: (n, g))
    prev = pl.BlockSpec((rows, lanes), lambda g, n: (jnp.maximum(n - 1, 0), g))
    whole = pl.BlockSpec((SEQ, lanes), lambda g, n: (0, g))
    n_in = 5 if with_prev else 3

    def body(*refs):
        ins, (o_ref, l_ref, do_ref, dl_ref, dq_ref, dk_ref, dv_ref) = refs[:n_in], refs[n_in:]
        n = pl.program_id(1)

        @pl.when(n == 0)
        def _():
            dk_ref[...] = jnp.zeros_like(dk_ref)
            dv_ref[...] = jnp.zeros_like(dv_ref)

        def one(rho):
            sub = _strided_rows(rho, r)
            sub_c = _strided_rows(n * rows + rho, r)
            sub_p = _strided_rows(jnp.maximum(n - 1, 0) * rows + rho, r)
            for pair in range(lanes // 128):
                sl = pl.ds(pair * 128, 128)
                vals = [ref[sub, sl] for ref in ins] + ([] if with_prev else [None, None])
                grads = _attn_pair_bwd(*vals, o_ref[sub, sl], l_ref[sub, sl], do_ref[sub, sl], dl_ref[sub, sl], n > 0)
                dq_ref[sub, sl] = grads[0]
                dk_ref[sub_c, sl] += grads[1]
                dv_ref[sub_c, sl] += grads[2]
                if with_prev:
                    dk_ref[sub_p, sl] += grads[3]
                    dv_ref[sub_p, sl] += grads[4]

        _for_residues(r, one)

    operands = (qn, kn, vn, kn, vn) if with_prev else (qn, kn, vn)
    return pl.pallas_call(
        body, name="attn_bwd_%d" % r, grid=(D_ATTN // lanes, SEQ // rows),
        in_specs=[cur, cur, cur] + ([prev, prev] if with_prev else []) + [cur] * 4, out_specs=[cur, whole, whole],
        out_shape=[jax.ShapeDtypeStruct((SEQ, D_ATTN), F32)] * 3,
        compiler_params=_params(("parallel", "arbitrary")),
    )(*operands, o, lse, do, dl)


CONV_COLS = 256
XBC_BLOCK0 = 4096 // CONV_COLS


def _shift_rows(x, s):
    n = x.shape[0]
    t = lax.broadcasted_iota(jnp.int32, x.shape, 0)
    if s >= 0:
        return jnp.where(t >= s, pltpu.roll(x, s, 0), 0.0)
    return jnp.where(t < n + s, pltpu.roll(x, n + s, 0), 0.0)


def _conv_pre(x, w_ref, b_ref):
    delayed = [_shift_rows(x, 3 - k) for k in range(3)]
    pre = b_ref[...] + w_ref[3:4, :] * x
    for k in range(3):
        pre = pre + w_ref[k:k + 1, :] * delayed[k]
    return pre, delayed


def _conv_fwd(proj, conv_w, conv_b):
    cols = conv_w.shape[1]

    def body(x_ref, w_ref, b_ref, o_ref):
        pre, _ = _conv_pre(x_ref[...], w_ref, b_ref)
        o_ref[...] = pre * jax.nn.sigmoid(pre)

    blk = pl.BlockSpec((SEQ, CONV_COLS), lambda j: (0, j))
    return pl.pallas_call(
        body, name="conv_fwd", grid=(cols // CONV_COLS,),
        in_specs=[pl.BlockSpec((SEQ, CONV_COLS), lambda j: (0, XBC_BLOCK0 + j)),
                  pl.BlockSpec((4, CONV_COLS), lambda j: (0, j)), pl.BlockSpec((1, CONV_COLS), lambda j: (0, j))],
        out_specs=blk, out_shape=jax.ShapeDtypeStruct((SEQ, cols), F32),
        compiler_params=_params(("parallel",)),
    )(proj, conv_w, conv_b)


def _conv_bwd(proj, conv_w, conv_b, dxs, db, dc):
    cols = conv_w.shape[1]
    x_blocks, b_blocks = dxs.shape[1] // CONV_COLS, db.shape[1] // CONV_COLS

    def body(x_ref, w_ref, b_ref, dxs_ref, db_ref_in, dc_ref_in, dx_ref, dw_ref, db_ref):
        j = pl.program_id(0)
        dy = jnp.where(j < x_blocks, dxs_ref[...], jnp.where(j < x_blocks + b_blocks, db_ref_in[...], dc_ref_in[...]))
        x = x_ref[...]
        pre, delayed = _conv_pre(x, w_ref, b_ref)
        sg = jax.nn.sigmoid(pre)
        dpre = dy * (sg * (1.0 + pre * (1.0 - sg)))
        db_ref[...] = jnp.sum(dpre, axis=0, keepdims=True)
        dx = w_ref[3:4, :] * dpre
        dw_ref[3:4, :] = jnp.sum(dpre * x, axis=0, keepdims=True)
        for k in range(3):
            dx = dx + w_ref[k:k + 1, :] * _shift_rows(dpre, k - 3)
            dw_ref[k:k + 1, :] = jnp.sum(dpre * delayed[k], axis=0, keepdims=True)
        dw_ref[4:8, :] = jnp.zeros((4, CONV_COLS), F32)
        dx_ref[...] = dx.astype(dx_ref.dtype)

    blk = pl.BlockSpec((SEQ, CONV_COLS), lambda j: (0, j))
    parts = [pl.BlockSpec((SEQ, CONV_COLS), lambda j: (0, jnp.minimum(j, x_blocks - 1))),
             pl.BlockSpec((SEQ, CONV_COLS), lambda j: (0, jnp.clip(j - x_blocks, 0, b_blocks - 1))),
             pl.BlockSpec((SEQ, CONV_COLS), lambda j: (0, jnp.clip(j - x_blocks - b_blocks, 0, b_blocks - 1)))]
    return pl.pallas_call(
        body, name="conv_bwd", grid=(cols // CONV_COLS,),
        in_specs=[pl.BlockSpec((SEQ, CONV_COLS), lambda j: (0, XBC_BLOCK0 + j)),
                  pl.BlockSpec((4, CONV_COLS), lambda j: (0, j)), pl.BlockSpec((1, CONV_COLS), lambda j: (0, j))] + parts,
        out_specs=[blk, pl.BlockSpec((8, CONV_COLS), lambda j: (0, j)), pl.BlockSpec((1, CONV_COLS), lambda j: (0, j))],
        out_shape=[jax.ShapeDtypeStruct((SEQ, cols), BF16), jax.ShapeDtypeStruct((8, cols), F32),
                   jax.ShapeDtypeStruct((1, cols), F32)],
        compiler_params=_params(("parallel",)),
    )(proj, conv_w, conv_b, dxs, db, dc)


HEADS_PER_GROUP = 4


GROUP_WIDTH = HEADS_PER_GROUP * HEAD


def _ssd_chunk(x, bm, cm, dtr, bias, alog, dsk, h):
    row = lax.broadcasted_iota(jnp.int32, (CHUNK, CHUNK), 0)
    col = lax.broadcasted_iota(jnp.int32, (CHUNK, CHUNK), 1)
    causal = row >= col
    z = dtr + bias
    dt = jnp.maximum(z, 0.0) + jnp.log(1.0 + jnp.exp(-jnp.abs(z)))
    acs = _fdot(causal.astype(F32), dt * -jnp.exp(alog), NN)
    acs_t, dt_t = acs.T, dt.T
    cb = _bdot(cm, bm, NT)
    lane = lax.broadcasted_iota(jnp.int32, (1, CHUNK), 1)
    sub = lax.broadcasted_iota(jnp.int32, (CHUNK, 1), 0)
    wide = lax.broadcasted_iota(jnp.int32, (1, GROUP_WIDTH), 1) // HEAD
    tall = lax.broadcasted_iota(jnp.int32, (GROUP_WIDTH, 1), 0) // HEAD
    acs_last = jnp.sum(acs * (sub == CHUNK - 1).astype(F32), axis=0, keepdims=True)
    to_lanes = (lax.broadcasted_iota(jnp.int32, (CHUNK, GROUP_WIDTH), 0)
                == lax.broadcasted_iota(jnp.int32, (CHUNK, GROUP_WIDTH), 1) // HEAD).astype(F32)
    grow = _fdot(jnp.exp(acs), to_lanes, NN)
    keep = _fdot(jnp.exp(acs_last - acs) * dt, to_lanes, NN)
    w_parts, x_parts, skip, carry = [], [], 0.0, 0.0
    for j in range(HEADS_PER_GROUP):
        on_lane, on_sub = (lane == j).astype(F32), (sub == j).astype(F32)
        acs_c = jnp.sum(acs * on_lane, axis=1, keepdims=True)
        acs_r = jnp.sum(acs_t * on_sub, axis=0, keepdims=True)
        dt_r = jnp.sum(dt_t * on_sub, axis=0, keepdims=True)
        w_parts.append(cb * jnp.exp(jnp.where(causal, acs_c - acs_r, NEG)) * dt_r)
        x_parts.append(x * (wide == j).astype(F32))
        skip = skip + jnp.sum(dsk * on_lane, axis=1, keepdims=True) * (wide == j).astype(F32)
        carry = carry + jnp.sum(jnp.exp(acs_last) * on_lane, axis=1, keepdims=True) * (tall == j).astype(F32)
    y_diag = _bdot(jnp.concatenate(w_parts, axis=1), jnp.concatenate(x_parts, axis=0), NN)
    y = y_diag + _bdot(cm, h, NT) * grow + skip * x
    return y, h * carry + _bdot(x * keep, bm, TN)


GROUPS_PER_STEP = 2
SSD_STEPS = N_GROUPS // GROUPS_PER_STEP


def _ssd_specs(reverse):
    n_chunks = SEQ // CHUNK
    c_of = (lambda c: n_chunks - 1 - c) if reverse else (lambda c: c)
    x_w, n_w, dt_w = GROUPS_PER_STEP * GROUP_WIDTH, GROUPS_PER_STEP * N_STATE, GROUPS_PER_STEP * 128
    x_spec = pl.BlockSpec((CHUNK, x_w), lambda g, c: (c_of(c), g))
    b_spec = pl.BlockSpec((CHUNK, n_w), lambda g, c: (c_of(c), D_SSM // n_w + g))
    c_spec = pl.BlockSpec((CHUNK, n_w), lambda g, c: (c_of(c), (D_SSM + N_GROUPS * N_STATE) // n_w + g))
    dt_spec = pl.BlockSpec((CHUNK, dt_w), lambda g, c: (c_of(c), g))
    vec_spec = pl.BlockSpec((1, dt_w), lambda g, c: (0, g))
    h_spec = pl.BlockSpec((None, GROUPS_PER_STEP, GROUP_WIDTH, N_STATE), lambda g, c: (c_of(c), g, 0, 0))
    return x_spec, b_spec, c_spec, dt_spec, vec_spec, h_spec


def _group_slices(u):
    return pl.ds(u * GROUP_WIDTH, GROUP_WIDTH), pl.ds(u * N_STATE, N_STATE), pl.ds(u * 128, 128)


def _ssd_fwd(xbc, dt_raw, bias, alog, dsk):
    x_spec, b_spec, c_spec, dt_spec, vec_spec, h_spec = _ssd_specs(False)

    def body(x_ref, b_ref, c_ref, dt_ref, bias_ref, alog_ref, dsk_ref, y_ref, hin_ref, h_scr):
        @pl.when(pl.program_id(1) == 0)
        def _():
            h_scr[...] = jnp.zeros_like(h_scr)

        for u in range(GROUPS_PER_STEP):
            xs, ns, ds = _group_slices(u)
            h = h_scr[u]
            hin_ref[u] = h
            y_ref[:, xs], h_scr[u] = _ssd_chunk(x_ref[:, xs], b_ref[:, ns], c_ref[:, ns], dt_ref[:, ds], bias_ref[:, ds],
                                                alog_ref[:, ds], dsk_ref[:, ds], h)

    return pl.pallas_call(
        body, name="ssd_fwd", grid=(SSD_STEPS, SEQ // CHUNK),
        in_specs=[x_spec, b_spec, c_spec, dt_spec, vec_spec, vec_spec, vec_spec],
        out_specs=[x_spec, h_spec],
        out_shape=[jax.ShapeDtypeStruct((SEQ, D_SSM), F32),
                   jax.ShapeDtypeStruct((SEQ // CHUNK, N_GROUPS, GROUP_WIDTH, N_STATE), F32)],
        scratch_shapes=[pltpu.VMEM((GROUPS_PER_STEP, GROUP_WIDTH, N_STATE), F32)],
        compiler_params=_params(("parallel", "arbitrary")),
    )(xbc, xbc, xbc, dt_raw, bias, alog, dsk)


def _ssd_bwd(xbc, dt_raw, bias, alog, dsk, h_in, dy):
    x_spec, b_spec, c_spec, dt_spec, vec_spec, h_spec = _ssd_specs(True)

    def body(x_ref, b_ref, c_ref, dt_ref, bias_ref, alog_ref, dsk_ref, hin_ref, dy_ref,
             dx_ref, db_ref, dc_ref, ddt_ref, dbias_ref, dalog_ref, ddsk_ref, dh_scr):
        first = pl.program_id(1) == 0

        @pl.when(first)
        def _():
            dh_scr[...] = jnp.zeros_like(dh_scr)

        for u in range(GROUPS_PER_STEP):
            xs, ns, ds = _group_slices(u)
            _, pullback = jax.vjp(_ssd_chunk, x_ref[:, xs], b_ref[:, ns], c_ref[:, ns], dt_ref[:, ds], bias_ref[:, ds],
                                  alog_ref[:, ds], dsk_ref[:, ds], hin_ref[u])
            g = pullback((dy_ref[:, xs], dh_scr[u]))
            dx_ref[:, xs], db_ref[:, ns], dc_ref[:, ns] = g[0], g[1], g[2]
            ddt_ref[:, ds] = g[3].astype(ddt_ref.dtype)
            dh_scr[u] = g[7]
            for o_ref, val in ((dbias_ref, g[4]), (dalog_ref, g[5]), (ddsk_ref, g[6])):
                @pl.when(first)
                def _(o_ref=o_ref, val=val, ds=ds):
                    o_ref[:, ds] = val

                @pl.when(jnp.logical_not(first))
                def _(o_ref=o_ref, val=val, ds=ds):
                    o_ref[:, ds] += val

    n_chunks = SEQ // CHUNK
    out_b = pl.BlockSpec((CHUNK, GROUPS_PER_STEP * N_STATE), lambda g, c: (n_chunks - 1 - c, g))
    res = pl.pallas_call(
        body, name="ssd_bwd", grid=(SSD_STEPS, n_chunks),
        in_specs=[x_spec, b_spec, c_spec, dt_spec, vec_spec, vec_spec, vec_spec, h_spec, x_spec],
        out_specs=[x_spec, out_b, out_b, dt_spec, vec_spec, vec_spec, vec_spec],
        out_shape=[jax.ShapeDtypeStruct((SEQ, D_SSM), F32), jax.ShapeDtypeStruct((SEQ, N_GROUPS * N_STATE), F32),
                   jax.ShapeDtypeStruct((SEQ, N_GROUPS * N_STATE), F32), jax.ShapeDtypeStruct((SEQ, DT_PAD), BF16),
                   jax.ShapeDtypeStruct((1, DT_PAD), F32), jax.ShapeDtypeStruct((1, DT_PAD), F32),
                   jax.ShapeDtypeStruct((1, DT_PAD), F32)],
        scratch_shapes=[pltpu.VMEM((GROUPS_PER_STEP, GROUP_WIDTH, N_STATE), F32)],
        compiler_params=_params(("parallel", "arbitrary")),
    )(xbc, xbc, xbc, dt_raw, bias, alog, dsk, h_in, dy)
    return res


CROSS_HEAD = 128
CROSS_ROWS = 512


def _cross_head(q, k, v, gq, gk):
    qn = _rms(q, gq) * (CROSS_HEAD ** -0.5)
    kn = _rms(k, gk)
    s = _bdot(qn, kn, NT)
    p = jnp.exp(s - lax.stop_gradient(jnp.max(s, axis=-1, keepdims=True)))
    return _bdot(p, v, NN) * (1.0 / jnp.sum(p, axis=-1, keepdims=True))


def _cross_specs():
    q_spec = pl.BlockSpec((CROSS_ROWS, CROSS_HEAD), lambda h, i: (i, h))
    k_spec = pl.BlockSpec((N_MEM, CROSS_HEAD), lambda h, i: (0, h))
    v_spec = pl.BlockSpec((N_MEM, CROSS_HEAD), lambda h, i: (0, 4 + h))
    g_spec = pl.BlockSpec((1, CROSS_HEAD), lambda h, i: (0, 0))
    return q_spec, k_spec, v_spec, g_spec


def _cross_fwd(qc, kv, gq, gk):
    q_spec, k_spec, v_spec, g_spec = _cross_specs()

    def body(q_ref, k_ref, v_ref, gq_ref, gk_ref, o_ref):
        o_ref[...] = _cross_head(q_ref[...], k_ref[...], v_ref[...], gq_ref[...], gk_ref[...]).astype(o_ref.dtype)

    return pl.pallas_call(
        body, name="cross_fwd", grid=(4, SEQ // CROSS_ROWS),
        in_specs=[q_spec, k_spec, v_spec, g_spec, g_spec], out_specs=q_spec,
        out_shape=jax.ShapeDtypeStruct((SEQ, D_CROSS), BF16),
        compiler_params=_params(("parallel", "parallel")),
    )(qc, kv, kv, gq, gk)


def _cross_bwd(qc, kv, gq, gk, do):
    q_spec, k_spec, v_spec, g_spec = _cross_specs()

    def body(q_ref, k_ref, v_ref, gq_ref, gk_ref, do_ref, dq_ref, dk_ref, dv_ref, dgq_ref, dgk_ref):
        _, pullback = jax.vjp(_cross_head, q_ref[...], k_ref[...], v_ref[...], gq_ref[...], gk_ref[...])
        dq, dk, dv, dgq, dgk = pullback(do_ref[...].astype(F32))
        dq_ref[...] = dq.astype(dq_ref.dtype)
        row0 = pl.program_id(1) == 0
        all0 = jnp.logical_and(row0, pl.program_id(0) == 0)
        for o_ref, val, init in ((dk_ref, dk, row0), (dv_ref, dv, row0), (dgq_ref, dgq, all0), (dgk_ref, dgk, all0)):
            @pl.when(init)
            def _(o_ref=o_ref, val=val):
                o_ref[...] = val

            @pl.when(jnp.logical_not(init))
            def _(o_ref=o_ref, val=val):
                o_ref[...] += val

    return pl.pallas_call(
        body, name="cross_bwd", grid=(4, SEQ // CROSS_ROWS),
        in_specs=[q_spec, k_spec, v_spec, g_spec, g_spec, q_spec],
        out_specs=[q_spec, k_spec, k_spec, g_spec, g_spec],
        out_shape=[jax.ShapeDtypeStruct((SEQ, D_CROSS), BF16), jax.ShapeDtypeStruct((N_MEM, D_CROSS), F32),
                   jax.ShapeDtypeStruct((N_MEM, D_CROSS), F32), jax.ShapeDtypeStruct((1, CROSS_HEAD), F32),
                   jax.ShapeDtypeStruct((1, CROSS_HEAD), F32)],
        compiler_params=_params(("arbitrary", "arbitrary")),
    )(qc, kv, kv, gq, gk, do)


def _loss_epilogue(acc, residual, target):
    err = acc + residual - target
    dy = err * (1.0 / D_MODEL)
    part = jnp.sum(jnp.sum(err * err, axis=1, keepdims=True), axis=0, keepdims=True) * (0.5 / D_MODEL)
    return dy, dy, part


def _pad_heads(v):
    return jnp.pad(v.reshape(N_GROUPS, HEADS_PER_GROUP), ((0, 0), (0, 128 - HEADS_PER_GROUP))).reshape(1, DT_PAD)


def _unpad_heads(v):
    return v.reshape(v.shape[0], N_GROUPS, 128)[:, :, :HEADS_PER_GROUP].reshape(v.shape[0], N_DT)


def _rope_tables(positions):
    half = ROT // 2
    inv_freq = ROPE_THETA ** (-2.0 * jnp.arange(half, dtype=F32) / ROT)
    ang = positions.reshape(SEQ, 1).astype(F32) * inv_freq
    cos, sin = jnp.cos(ang), jnp.sin(ang)
    ones, zeros = jnp.ones((SEQ, HEAD - ROT), F32), jnp.zeros((SEQ, HEAD - ROT), F32)
    cos_h = jnp.concatenate([cos, cos, ones], axis=1)
    sin_h = jnp.concatenate([-sin, sin, zeros], axis=1)
    return jnp.tile(cos_h, (1, 2)), jnp.tile(sin_h, (1, 2))


def _add_res(acc, res):
    return (acc + res,)


def _settle(grads, *after):
    if hasattr(grads, "settle"):
        grads.settle(*after)


def _take_token(grads):
    token = getattr(grads, "token", None)
    if token is None:
        return ()
    grads.token = None
    return (token,)


def _local_step(x, mem, positions, target, p, w, more_weights=None, grads=None, h=None):
    grads = {} if grads is None else grads
    w = dict(w)
    cos, sin = _rope_tables(positions)
    gq2, gk2 = jnp.tile(p["g_q"], (1, 2)), jnp.tile(p["g_k"], (1, 2))
    bias, alog, dsk = _pad_heads(p["dt_bias"]), _pad_heads(p["a_log"]), _pad_heads(p["d_skip"])
    norm_out = [(D_MODEL, BF16, D_MODEL, 0, False)]

    if h is None:
        h = _rowwise(_norm_fn, [_full(x)], [_full(p["g_mix"])], norm_out, name="norm_in")[0]
    proj = _matmul(h, w["w_in"], mode="nn", name="in_proj", outs=[F32], n_cols=D_MAIN)
    dt_raw = _matmul(h, w["w_dt"], mode="nn", name="dt_proj", outs=[F32])
    qk_rows = [(proj, 128, 0, True), (proj, 128, 8, True), (proj, 128, 16, True), _full(cos), _full(sin)]
    qk_vecs = [_full(gq2), _full(gk2)]
    qn, kn, vn = _rowwise(_qk_fn, qk_rows, qk_vecs, [(D_ATTN, F32, 128, 0, True)] * 3, name="qk_prep", groups=8, tr=1024)
    branches = [_attention_fwd(qn, kn, vn, b) for b in range(3)]
    merge_rows = [_full(o) for o, _ in branches] + [_full(lse) for _, lse in branches]
    attn = _rowwise(_merge_fn, merge_rows, [_full(p["g_attn_out"])], [(D_ATTN, BF16, D_ATTN, 0, False)], name="attn_merge")[0]
    xbc = _conv_fwd(proj, p["conv_w"], p["conv_b"])
    y_ssd, h_in = _ssd_fwd(xbc, dt_raw, bias, alog, dsk)
    gate_rows = [(y_ssd, 256, 0, True), (proj, 256, 12, True)]
    gate_vecs = [(p["g_ssm_out"], 256, 0, True)]
    ssm = _rowwise(_gate_fn, gate_rows, gate_vecs, [(D_SSM, BF16, 256, 0, True)], name="ssm_gate", groups=4)[0]
    mix = jnp.concatenate([attn, ssm], axis=1)
    if more_weights is not None:
        w.update(more_weights("mixer_done", mix))
    x1 = _matmul(mix, w["w_out"], mode="nn", name="out_proj", outs=[F32], extra=(x,), epilogue=_add_res)
    hc = _rowwise(_norm_fn, [_full(x1)], [_full(p["g_cross"])], norm_out, name="norm_cross")[0]
    memh = _rowwise(_norm_fn, [_full(mem)], [_full(p["g_mem"])], norm_out, name="norm_mem", n_rows=N_MEM)[0]
    qc = _matmul(hc, w["w_cq"], mode="nn", name="cq_proj", outs=[F32])
    if more_weights is not None:
        w.update(more_weights("cross_started", qc))
    kv = _matmul(memh, w["w_ckv"], mode="nn", name="ckv_proj", outs=[F32])
    oc = _cross_fwd(qc, kv, p["g_cq"], p["g_ck"])
    x2 = _matmul(oc, w["w_co"], mode="nn", name="co_proj", outs=[F32], extra=(x1,), epilogue=_add_res)
    hm = _rowwise(_norm_fn, [_full(x2)], [_full(p["g_mlp"])], norm_out, name="norm_mlp")[0]
    if more_weights is not None:
        w.update(more_weights("cross_done", hm))
    u, act = _matmul(hm, w["w_up"], mode="nn", name="up_proj", outs=[F32, BF16],
                     epilogue=lambda acc: (acc, jnp.square(jnp.maximum(acc, 0.0))))
    dy, dyb, loss_tiles = _matmul(act, w["w_down"], mode="nn", name="down_proj", outs=[F32, BF16], extra=(x2, target),
                                  epilogue=_loss_epilogue, tile_sums=1)
    loss = jnp.sum(loss_tiles).reshape(1, 1)

    grads["w_down"] = _matmul(act, dyb, mode="tn", name="dw_down", outs=[BF16], after=_take_token(grads))
    du = _matmul(dyb, w["w_down"], mode="nt", name="d_act", outs=[BF16], extra=(u,), after=_take_token(grads),
                 epilogue=lambda acc, uu: (acc * (2.0 * jnp.maximum(uu, 0.0)),))
    _settle(grads, du)
    grads["w_up"] = _matmul(hm, du, mode="tn", name="dw_up", outs=[BF16], col_shards=4, after=_take_token(grads))
    dhm = _matmul(du, w["w_up"], mode="nt", name="d_hm", outs=[F32], after=_take_token(grads), tk=4096)
    _settle(grads, dhm)
    dx2, grads["g_mlp"] = _rowwise_vjp(
        _norm_fn, [_full(x2)], [_full(p["g_mlp"])], [[_full(dhm)]],
        [(0, D_MODEL, F32, D_MODEL, 0, False, _full(dy))], [(0, D_MODEL, D_MODEL, 0, False)], name="norm_mlp_bwd")
    grads["w_co"] = _matmul(oc, dx2, mode="tn", name="dw_co", outs=[BF16], col_shards=4, after=_take_token(grads))
    doc = _matmul(dx2, w["w_co"], mode="nt", name="d_oc", outs=[BF16])
    dqc, dkc, dvc, grads["g_cq"], grads["g_ck"] = _cross_bwd(qc, kv, p["g_cq"], p["g_ck"], doc)
    grads["w_cq"] = _matmul(hc, dqc, mode="tn", name="dw_cq", outs=[BF16])
    dhc = _matmul(dqc, w["w_cq"], mode="nt", name="d_hc", outs=[F32])
    dkv = jnp.concatenate([dkc, dvc], axis=1)
    grads["w_ckv"] = _matmul(memh, dkv, mode="tn", name="dw_ckv", outs=[BF16])
    dmemh = _matmul(dkv, w["w_ckv"], mode="nt", name="d_memh", outs=[F32])
    grads["g_mem"] = _rowwise_vjp(_norm_fn, [_full(mem)], [_full(p["g_mem"])], [[_full(dmemh)]], [],
                                  [(0, D_MODEL, D_MODEL, 0, False)], name="norm_mem_bwd", n_rows=N_MEM)[0]
    dx1, grads["g_cross"] = _rowwise_vjp(
        _norm_fn, [_full(x1)], [_full(p["g_cross"])], [[_full(dhc)]],
        [(0, D_MODEL, F32, D_MODEL, 0, False, _full(dx2))], [(0, D_MODEL, D_MODEL, 0, False)], name="norm_cross_bwd")
    grads["w_out"] = _matmul(mix, dx1, mode="tn", name="dw_out", outs=[BF16])
    dmix = _matmul(dx1, w["w_out"], mode="nt", name="d_mix", outs=[F32], after=_take_token(grads))
    _settle(grads, dmix)
    merge_grads = [(i, D_ATTN, F32, D_ATTN, 0, False, None) for i in range(6)]
    *dol, grads["g_attn_out"] = _rowwise_vjp(
        _merge_fn, merge_rows, [_full(p["g_attn_out"])], [[(dmix, D_ATTN, 0, False)]],
        merge_grads, [(0, D_ATTN, D_ATTN, 0, False)], name="attn_merge_bwd", after=_take_token(grads))
    dqkv = [_attention_bwd(qn, kn, vn, *branches[b], dol[b], dol[3 + b], b) for b in range(3)]
    qk_cts = [[(dqkv[b][i], 128, 0, True) for b in range(3)] for i in range(3)]
    dq, dk, dv, dgq2, dgk2 = _rowwise_vjp(
        _qk_fn, qk_rows, qk_vecs, qk_cts, [(i, D_ATTN, BF16, 128, 0, True, None) for i in range(3)],
        [(0, 128, 128, 0, False), (1, 128, 128, 0, False)], name="qk_prep_bwd", groups=8, tr=512)
    grads["g_q"] = dgq2[:, :HEAD] + dgq2[:, HEAD:]
    grads["g_k"] = dgk2[:, :HEAD] + dgk2[:, HEAD:]
    dy_ssd, dz, grads["g_ssm_out"] = _rowwise_vjp(
        _gate_fn, gate_rows, gate_vecs, [[(dmix, 256, 4, True)]],
        [(0, D_SSM, F32, 256, 0, True, None), (1, D_SSM, BF16, 256, 0, True, None)],
        [(0, D_SSM, 256, 0, True)], name="ssm_gate_bwd", groups=4)
    dxs, db, dc, ddt, dbias, dalog, ddsk = _ssd_bwd(xbc, dt_raw, bias, alog, dsk, h_in, dy_ssd)
    grads["dt_bias"], grads["a_log"], grads["d_skip"] = _unpad_heads(dbias), _unpad_heads(dalog), _unpad_heads(ddsk)
    dxbc_raw, dconv_w, grads["conv_b"] = _conv_bwd(proj, p["conv_w"], p["conv_b"], dxs, db, dc)
    grads["conv_w"] = dconv_w[:4]
    dproj = jnp.concatenate([dq, dk, dv, dz, dxbc_raw], axis=1)
    grads["w_main"] = _matmul(h, dproj, mode="tn", name="dw_main", outs=[BF16], out_cols=D_MAIN + N_DT)
    grads["w_dt"] = _matmul(h, ddt, mode="tn", name="dw_dt", outs=[BF16])
    dh = _matmul(dproj, w["w_in"], mode="nt", name="d_h_main", outs=[F32], after=_take_token(grads))
    dh = _matmul(ddt, w["w_dt"], mode="nt", name="d_h_dt", outs=[F32], extra=(dh,), epilogue=_add_res)
    grad_x, grads["g_mix"] = _rowwise_vjp(
        _norm_fn, [_full(x)], [_full(p["g_mix"])], [[_full(dh)]],
        [(0, D_MODEL, F32, D_MODEL, 0, False, _full(dx1))], [(0, D_MODEL, D_MODEL, 0, False)], name="norm_in_bwd")
    return loss, grad_x, grads


MATRICES = ("w_in", "w_out", "w_cq", "w_ckv", "w_co", "w_up", "w_down")
ROW_SHARDED = ("w_out", "w_cq", "w_ckv", "w_down")
N_CHIPS = 4
ANY = pl.BlockSpec(memory_space=pl.ANY)


def _place():
    return lax.axis_index("x"), lax.axis_index("y"), lax.axis_index("c")


def _other_chips(x, y):
    return [(1 - x, y), (x, 1 - y), (1 - x, 1 - y)]


def _remote(src, dst, send_sem, recv_sem, device):
    return pltpu.make_async_remote_copy(src_ref=src, dst_ref=dst, send_sem=send_sem, recv_sem=recv_sem,
                                        device_id=device, device_id_type=MESH)


def _gathered_shape(name, shard):
    rows, cols = shard.shape
    if name == "w_in":
        return (N_CHIPS, rows, cols)
    return (N_CHIPS * rows, cols) if name in ROW_SHARDED else (rows, N_CHIPS * cols)


def _shard_window(name, ref, rows, cols, chip, half):
    r0, nr = (0, rows) if half is None else (half * (rows // 2), rows // 2)
    if name == "w_in":
        return ref.at[chip, pl.ds(r0, nr), :]
    if name in ROW_SHARDED:
        return ref.at[pl.ds(chip * rows + r0, nr), :]
    return ref.at[pl.ds(r0, nr), pl.ds(pl.multiple_of(chip * cols, 128), cols)]


def _cast_into_gathered(w, name, chip, after=()):
    rows, cols = w.shape
    tr = _tile(rows, ROW_TILE)

    def body(chip_ref, w_ref, *rest):
        rest[-1][...] = w_ref[...].astype(BF16)

    if name == "w_in":
        out_spec = pl.BlockSpec((None, tr, cols), lambda i, chip_ref: (chip_ref[0], i, 0))
    elif name in ROW_SHARDED:
        out_spec = pl.BlockSpec((tr, cols), lambda i, chip_ref: (chip_ref[0] * (rows // tr) + i, 0))
    else:
        out_spec = pl.BlockSpec((tr, cols), lambda i, chip_ref: (i, chip_ref[0]))
    grid_spec = pltpu.PrefetchScalarGridSpec(
        num_scalar_prefetch=1, grid=(rows // tr,),
        in_specs=[pl.BlockSpec((tr, cols), lambda i, chip_ref: (i, 0))] + [pl.BlockSpec(memory_space=pl.ANY)] * len(after),
        out_specs=out_spec)
    return pl.pallas_call(body, name="cast_" + name, grid_spec=grid_spec,
                          out_shape=jax.ShapeDtypeStruct(_gathered_shape(name, w), BF16),
                          compiler_params=_params(("parallel",)))(chip.reshape(1).astype(jnp.int32), w, *after)


def _w_in_columns(arr, to_shards):
    rows, piece = D_MODEL, (D_MAIN + N_DT) // N_CHIPS
    tr = ROW_TILE

    def body(a_ref, o_ref):
        for j in range(N_CHIPS):
            if to_shards:
                o_ref[j] = a_ref[:, pl.ds(piece * j, piece)]
            else:
                o_ref[:, pl.ds(piece * j, piece)] = a_ref[j]

    pieces = pl.BlockSpec((N_CHIPS, tr, piece), lambda i: (0, i, 0))
    matrix = pl.BlockSpec((tr, N_CHIPS * piece), lambda i: (i, 0))
    out_dims = (N_CHIPS, rows, piece) if to_shards else (rows, N_CHIPS * piece)
    return pl.pallas_call(
        body, name="w_in_to_shards" if to_shards else "w_in_from_shards", grid=(rows // tr,),
        in_specs=[matrix if to_shards else pieces], out_specs=pieces if to_shards else matrix,
        out_shape=jax.ShapeDtypeStruct(out_dims, arr.dtype), compiler_params=_params(("parallel",)))(arr)


HBM = pl.BlockSpec(memory_space=pltpu.HBM)
SEM = pl.BlockSpec(memory_space=pltpu.SEMAPHORE)
EFFECT = pltpu.SideEffectType.DATAFLOW_SIDE_EFFECTING


def _split_start(name, bufs, plan, counts, after=()):
    n, n_g, n_after = len(bufs), len(counts), len(after)

    def body(*refs):
        ins, sems, token = refs[:n], refs[n + n_after:n + n_after + 2 * n_g], refs[-1]
        for g, copies in enumerate(plan(ins)):
            for i, (src, dst, device, _) in enumerate(copies):
                _remote(src, dst, sems[2 * g].at[i], sems[2 * g + 1].at[i], device).start()
        token[...] = jnp.zeros_like(token)

    sem_shapes = [pltpu.SemaphoreType.DMA((cnt,)) for cnt in counts for _ in range(2)]
    res = pl.pallas_call(
        body, name=name,
        out_shape=(*sem_shapes, *[pltpu.HBM(b.shape, b.dtype) for b in bufs], jax.ShapeDtypeStruct((8, 128), F32)),
        in_specs=(*(HBM,) * n, *(ANY,) * n_after),
        out_specs=(*(SEM,) * (2 * n_g), *(HBM,) * n, pl.BlockSpec(memory_space=pltpu.VMEM)),
        input_output_aliases={i: 2 * n_g + i for i in range(n)},
        compiler_params=pltpu.CompilerParams(has_side_effects=EFFECT),
    )(*[pltpu.with_memory_space_constraint(b, pltpu.HBM) for b in bufs], *after)
    sems = [(res[2 * g], res[2 * g + 1]) for g in range(n_g)]
    return sems, list(res[2 * n_g:2 * n_g + n]), res[-1]


def _split_wait(name, bufs, sems, plan, *after):
    n = len(bufs)

    def body(*refs):
        ins, send, recv = refs[:n], refs[n], refs[n + 1]
        (copies,) = plan(ins)
        for i, (src, _, device, landing) in enumerate(copies):
            cp = _remote(src, landing, send.at[i], recv.at[i], device)
            cp.wait_send()
            cp.wait_recv()

    res = pl.pallas_call(
        body, name=name, out_shape=tuple(pltpu.HBM(b.shape, b.dtype) for b in bufs),
        in_specs=(*(HBM,) * n, SEM, SEM, *(ANY,) * len(after)), out_specs=(HBM,) * n,
        input_output_aliases={i: i for i in range(n)},
        compiler_params=pltpu.CompilerParams(has_side_effects=EFFECT),
    )(*bufs, sems[0], sems[1], *after)
    return list(res)


def _ici_plan(names, shard_shapes):
    def plan(refs):
        x, y, c = _place()
        copies = []
        for ref, name in zip(refs, names):
            win = _shard_window(name, ref, *shard_shapes[name], 2 * x + y, c)
            for px, py in _other_chips(x, y):
                copies.append((win, win, (px, py, c), _shard_window(name, ref, *shard_shapes[name], 2 * px + py, c)))
        return [copies]
    return plan


def _pass_on_plan(names, shard_shapes):
    def plan(refs):
        x, y, c = _place()
        copies = []
        for ref, name in zip(refs, names):
            for px, py in _other_chips(x, y):
                win = _shard_window(name, ref, *shard_shapes[name], 2 * px + py, c)
                copies.append((win, win, (x, y, 1 - c), _shard_window(name, ref, *shard_shapes[name], 2 * px + py, 1 - c)))
        return [copies]
    return plan


def _swap_plan(n_pairs):
    def plan(refs):
        x, y, c = _place()
        return [[(src.at[:, 1 - c], dst, (x, y, 1 - c), dst) for src, dst in zip(refs[:n_pairs], refs[n_pairs:])]]
    return plan


def _share_plan(n_pairs):
    def plan(refs):
        x, y, c = _place()
        return [[(src, dst, (x, y, 1 - c), dst)] for src, dst in zip(refs[:n_pairs], refs[n_pairs:])]
    return plan


def _scatter_plan(n_pairs):
    def plan(refs):
        x, y, c = _place()
        copies = []
        for src, dst in zip(refs[:n_pairs], refs[n_pairs:]):
            for k, (px, py) in enumerate(_other_chips(x, y)):
                copies.append((src.at[2 * px + py], dst.at[k], (px, py, c), dst.at[k]))
        return [copies]
    return plan


def _sibling_swap(arrs, name):
    n = len(arrs)

    def body(*refs):
        ins, outs, send, recv = refs[:n], refs[n:2 * n], refs[2 * n], refs[2 * n + 1]
        x, y, c = _place()
        cps = [_remote(ins[w].at[:, 1 - c], outs[w], send.at[w], recv.at[w], (x, y, 1 - c)) for w in range(n)]
        for cp in cps:
            cp.start()
        for cp in cps:
            cp.wait()

    return pl.pallas_call(
        body, name=name, in_specs=[ANY] * n, out_specs=[ANY] * n,
        out_shape=[jax.ShapeDtypeStruct((a.shape[0],) + a.shape[2:], a.dtype) for a in arrs],
        scratch_shapes=[pltpu.SemaphoreType.DMA((n,))] * 2,
    )(*arrs)


def _small_allreduce(buf, name, after=()):
    rows = buf.shape[0]

    def body(x_ref, *rest):
        out_ref, all_ref, send_sems, recv_sems, local_sem = rest[len(after):]
        x, y, c = _place()
        me, sibling, chips = (x, y, c), (x, y, 1 - c), _other_chips(x, y)

        def block(px, py, pc):
            return all_ref.at[pl.ds((4 * px + 2 * py + pc) * rows, rows), :]

        def copy(k, blk, to, src=None):
            return _remote(block(*blk) if src is None else src, block(*blk), send_sems.at[k], recv_sems.at[k], to)

        own = pltpu.make_async_copy(x_ref, block(*me), local_sem)
        own.start()
        first = [copy(0, me, sibling, src=x_ref)] + [copy(1 + j, me, (*chip, c), src=x_ref) for j, chip in enumerate(chips)]
        for cp in first:
            cp.start()
        passed = [copy(4 + j, (*chip, c), sibling) for j, chip in enumerate(chips)]
        for j, chip in enumerate(chips):
            copy(1 + j, (*chip, c), me).wait_recv()
            passed[j].start()
        copy(0, sibling, me).wait_recv()
        for j, chip in enumerate(chips):
            copy(4 + j, (*chip, 1 - c), me).wait_recv()
        for cp in first + passed:
            cp.wait_send()
        own.wait()
        acc = all_ref[pl.ds(0, rows), :]
        for d in range(1, 8):
            acc = acc + all_ref[pl.ds(d * rows, rows), :]
        out_ref[...] = acc

    vmem = pl.BlockSpec(memory_space=pltpu.VMEM)
    return pl.pallas_call(
        body, name=name, in_specs=[vmem] + [ANY] * len(after), out_specs=vmem,
        out_shape=jax.ShapeDtypeStruct(buf.shape, F32),
        scratch_shapes=[pltpu.VMEM((8 * rows, 128), F32), pltpu.SemaphoreType.DMA((7,)), pltpu.SemaphoreType.DMA((7,)),
                        pltpu.SemaphoreType.DMA],
    )(buf, *after)


ROW_TILE = 256
BIG_ROW_TILE = 1024


def _add_halves(arr, recv, c, name):
    _, _, hr, cols = arr.shape
    tr = _tile(hr, BIG_ROW_TILE)

    def body(c_ref, a_ref, r_ref, o_ref):
        o_ref[...] = (a_ref[...].astype(F32) + r_ref[...].astype(F32)).astype(o_ref.dtype)

    piece = pl.BlockSpec((None, tr, cols), lambda j, i, c_ref: (j, i, 0))
    grid_spec = pltpu.PrefetchScalarGridSpec(
        num_scalar_prefetch=1, grid=(N_CHIPS, hr // tr),
        in_specs=[pl.BlockSpec((None, None, tr, cols), lambda j, i, c_ref: (j, c_ref[0], i, 0)), piece], out_specs=piece)
    return pl.pallas_call(body, name=name, grid_spec=grid_spec, out_shape=jax.ShapeDtypeStruct(recv.shape, BF16),
                          compiler_params=_params(("parallel", "parallel")))(c.reshape(1).astype(jnp.int32), arr, recv)


def _flip_slot(d):
    return jnp.where(d == 1, 1, jnp.where(d == 3, 2, 0))


def _sum_chips(p, q, chip, name):
    _, hr, cols = p.shape
    tr = _tile(hr, BIG_ROW_TILE)

    def body(chip_ref, p_ref, q_ref, o_ref):
        j = pl.program_id(1)
        term = jnp.where(j == chip_ref[0], p_ref[...].astype(F32), q_ref[...].astype(F32))

        @pl.when(j == 0)
        def _():
            o_ref[...] = term

        @pl.when(j != 0)
        def _():
            o_ref[...] += term

    grid_spec = pltpu.PrefetchScalarGridSpec(
        num_scalar_prefetch=1, grid=(hr // tr, N_CHIPS),
        in_specs=[pl.BlockSpec((None, tr, cols), lambda i, j, chip_ref: (chip_ref[0], i, 0)),
                  pl.BlockSpec((None, tr, cols), lambda i, j, chip_ref: (_flip_slot(j ^ chip_ref[0]), i, 0))],
        out_specs=pl.BlockSpec((tr, cols), lambda i, j, chip_ref: (i, 0)))
    return pl.pallas_call(body, name=name, grid_spec=grid_spec, out_shape=jax.ShapeDtypeStruct((hr, cols), F32),
                          compiler_params=_params(("parallel", "arbitrary")))(chip.reshape(1).astype(jnp.int32), p, q)


def _adamw_halves(w, g_own, g_other, m, v, c, name):
    rows, cols = w.shape
    tr = _tile(rows // 2, ROW_TILE)
    per_half = rows // 2 // tr

    def body(c_ref, w_ref, own_ref, other_ref, m_ref, v_ref, g_ref, d_ref, nm_ref, nv_ref):
        mine = (pl.program_id(0) // per_half) == c_ref[0]
        g_ = jnp.where(mine, own_ref[...], other_ref[...])
        g_ref[...] = g_
        d_ref[...], nm_ref[...], nv_ref[...] = _adamw_math(w_ref[...], g_, m_ref[...], v_ref[...])

    blk = pl.BlockSpec((tr, cols), lambda i, c_ref: (i, 0))
    own = pl.BlockSpec((tr, cols), lambda i, c_ref: (jnp.where(i // per_half == c_ref[0], i % per_half, 0), 0))
    other = pl.BlockSpec((tr, cols), lambda i, c_ref: (jnp.where(i // per_half == c_ref[0], 0, i % per_half), 0))
    grid_spec = pltpu.PrefetchScalarGridSpec(num_scalar_prefetch=1, grid=(rows // tr,),
                                             in_specs=[blk, own, other, blk, blk], out_specs=[blk] * 4)
    return pl.pallas_call(body, name=name, grid_spec=grid_spec, out_shape=[jax.ShapeDtypeStruct(w.shape, F32)] * 4,
                          compiler_params=_params(("parallel",)))(c.reshape(1).astype(jnp.int32), w, g_own, g_other, m, v)


W_IN_COLS = (D_MAIN + N_DT) // N_CHIPS
W_IN_MAIN = W_IN_COLS // 128 * 128
W_IN_TAIL = W_IN_COLS - 128
W_IN_PARTS = ((0, W_IN_MAIN), (W_IN_TAIL, 128))


def _cast_w_in_transposed(w_t, chip, after=()):
    def body(chip_ref, w_ref, *rest):
        for start, size in W_IN_PARTS:
            rest[-1][:, pl.ds(start, size)] = w_ref[pl.ds(start, size), :].T.astype(BF16)

    grid_spec = pltpu.PrefetchScalarGridSpec(
        num_scalar_prefetch=1, grid=(D_MODEL // ROW_TILE,),
        in_specs=[pl.BlockSpec((W_IN_COLS, ROW_TILE), lambda i, chip_ref: (0, i))] + [pl.BlockSpec(memory_space=pl.ANY)] * len(after),
        out_specs=pl.BlockSpec((None, ROW_TILE, W_IN_COLS), lambda i, chip_ref: (chip_ref[0], i, 0)))
    return pl.pallas_call(body, name="cast_w_in", grid_spec=grid_spec,
                          out_shape=jax.ShapeDtypeStruct((N_CHIPS, D_MODEL, W_IN_COLS), BF16),
                          compiler_params=_params(("parallel",)))(chip.reshape(1).astype(jnp.int32), w_t, *after)


def _adamw_w_in_transposed(w_t, g_own, g_other, m_t, v_t, c):
    per_half = D_MODEL // 2 // ROW_TILE

    def body(c_ref, w_ref, own_ref, other_ref, m_ref, v_ref, g_ref, d_ref, nm_ref, nv_ref):
        mine = (pl.program_id(0) // per_half) == c_ref[0]
        for start, size in W_IN_PARTS:
            cols, rows = pl.ds(start, size), pl.ds(start, size)
            g_ = jnp.where(mine, own_ref[:, cols], other_ref[:, cols]).T
            g_ref[rows, :] = g_
            d_ref[rows, :], nm_ref[rows, :], nv_ref[rows, :] = _adamw_math(w_ref[rows, :], g_, m_ref[rows, :], v_ref[rows, :])

    blk = pl.BlockSpec((W_IN_COLS, ROW_TILE), lambda i, c_ref: (0, i))
    own = pl.BlockSpec((ROW_TILE, W_IN_COLS), lambda i, c_ref: (jnp.where(i // per_half == c_ref[0], i % per_half, 0), 0))
    other = pl.BlockSpec((ROW_TILE, W_IN_COLS), lambda i, c_ref: (jnp.where(i // per_half == c_ref[0], 0, i % per_half), 0))
    grid_spec = pltpu.PrefetchScalarGridSpec(num_scalar_prefetch=1, grid=(D_MODEL // ROW_TILE,),
                                             in_specs=[blk, own, other, blk, blk], out_specs=[blk] * 4)
    return pl.pallas_call(body, name="adamw_w_in", grid_spec=grid_spec, out_shape=[jax.ShapeDtypeStruct(w_t.shape, F32)] * 4,
                          compiler_params=_params(("parallel",)))(c.reshape(1).astype(jnp.int32), w_t, g_own, g_other, m_t, v_t)


def _adamw_math(w, g, m, v):
    m_new = ADAM_B1 * m + (1.0 - ADAM_B1) * g
    v_new = ADAM_B2 * v + (1.0 - ADAM_B2) * (g * g)
    m_hat = m_new / (1.0 - ADAM_B1 ** ADAM_STEP)
    v_hat = v_new / (1.0 - ADAM_B2 ** ADAM_STEP)
    return -ADAM_LR * (m_hat / (jnp.sqrt(v_hat) + ADAM_EPS) + ADAM_WD * w), m_new, v_new


VECTORS = ("g_mix", "g_q", "g_k", "g_attn_out", "conv_b", "dt_bias", "a_log", "d_skip", "g_ssm_out", "g_cross", "g_mem",
           "g_cq", "g_ck", "g_mlp")
WEIGHTS = ("g_mix", "w_in", "g_q", "g_k", "g_attn_out", "conv_w", "conv_b", "dt_bias", "a_log", "d_skip", "g_ssm_out", "w_out",
           "g_cross", "g_mem", "w_cq", "w_ckv", "g_cq", "g_ck", "w_co", "g_mlp", "w_up", "w_down")


def _pack(parts):
    flat = jnp.concatenate([t.reshape(-1) for t in parts])
    total = -(-flat.shape[0] // 1024) * 1024
    return jnp.pad(flat, (0, total - flat.shape[0])).reshape(total // 128, 128)


def _rows_of(n):
    return -(-n // 128)


def _slot_rows(n):
    return -(-n // 1024) * 8


def _pack_rows(parts):
    rows = []
    for t in parts:
        flat = t.reshape(-1)
        rows.append(jnp.pad(flat, (0, 128 * _slot_rows(flat.shape[0]) - flat.shape[0])).reshape(-1, 128))
    return jnp.concatenate(rows)


def _adamw_vectors(summed, chip, vectors, conv):
    groups = list(vectors) + [conv]
    offsets, row = [], 0
    for w, _, _ in groups:
        offsets.append(row)
        row += _slot_rows(w.shape[1]) if w.shape[0] == 1 else _slot_rows(4 * N_CHIPS * w.shape[1])
    conv_blocks = _rows_of(conv[0].shape[1])

    def body(chip_ref, sum_ref, *refs):
        ins, outs = refs[:3 * len(groups)], refs[3 * len(groups):]

        def update(i, g, idx):
            w_ref, m_ref, v_ref = ins[3 * i:3 * i + 3]
            delta, new_m, new_v = _adamw_math(w_ref[idx], g, m_ref[idx], v_ref[idx])
            for o_ref, val in zip(outs[4 * i:4 * i + 4], (g, delta, new_m, new_v)):
                o_ref[idx] = val

        for i, (w, _, _) in enumerate(vectors):
            for t in range(_rows_of(w.shape[1])):
                width = min(128, w.shape[1] - 128 * t)
                update(i, sum_ref[pl.ds(offsets[i] + t, 1), pl.ds(0, width)], (slice(None), pl.ds(128 * t, width)))
        for tap in range(4):
            for blk in range(conv_blocks):
                src = offsets[-1] + tap * N_CHIPS * conv_blocks + chip_ref[0] * conv_blocks + blk
                update(len(vectors), sum_ref[pl.ds(src, 1), :], (pl.ds(tap, 1), pl.ds(128 * blk, 128)))

    def whole(a):
        return pl.BlockSpec(a.shape, lambda i, chip_ref: (0,) * a.ndim)

    operands = [t for group in groups for t in group]
    grid_spec = pltpu.PrefetchScalarGridSpec(
        num_scalar_prefetch=1, grid=(1,), in_specs=[whole(summed)] + [whole(t) for t in operands],
        out_specs=[whole(w) for w, _, _ in groups for _ in range(4)])
    res = pl.pallas_call(body, name="adamw_vectors", grid_spec=grid_spec,
                         out_shape=[jax.ShapeDtypeStruct(w.shape, F32) for w, _, _ in groups for _ in range(4)],
                         compiler_params=_params(("arbitrary",)))(chip.reshape(1).astype(jnp.int32), summed, *operands)
    return [res[4 * i:4 * i + 4] for i in range(len(groups))]


def _unpack(buf, shapes):
    flat, out, pos = buf.reshape(-1), [], 0
    for shape in shapes:
        size = math.prod(shape)
        out.append(flat[pos:pos + size].reshape(shape))
        pos += size
    return out


def kernel(x, mem, positions, g_mix, w_in, g_q, g_k, g_attn_out, conv_w, conv_b, dt_bias, a_log, d_skip, g_ssm_out, w_out, g_cross, g_mem, w_cq, w_ckv, g_cq, g_ck, w_co, g_mlp, w_up, w_down, loss_target, m_g_mix, m_w_in, m_g_q, m_g_k, m_g_attn_out, m_conv_w, m_conv_b, m_dt_bias, m_a_log, m_d_skip, m_g_ssm_out, m_w_out, m_g_cross, m_g_mem, m_w_cq, m_w_ckv, m_g_cq, m_g_ck, m_w_co, m_g_mlp, m_w_up, m_w_down, v_g_mix, v_w_in, v_g_q, v_g_k, v_g_attn_out, v_conv_w, v_conv_b, v_dt_bias, v_a_log, v_d_skip, v_g_ssm_out, v_w_out, v_g_cross, v_g_mem, v_w_cq, v_w_ckv, v_g_cq, v_g_ck, v_w_co, v_g_mlp, v_w_up, v_w_down):
    args = dict(locals())
    weights = {n: args[n][0] for n in WEIGHTS}
    mom_m = {n: args["m_" + n][0] for n in WEIGHTS}
    mom_v = {n: args["v_" + n][0] for n in WEIGHTS}
    x_idx, y_idx, c_idx = _place()
    chip = 2 * x_idx + y_idx

    shapes = {n: weights[n].shape for n in MATRICES}
    first, mid, late = ("w_in",), ("w_out", "w_cq", "w_ckv", "w_co"), ("w_up", "w_down")
    w_in_t, m_in_t, v_in_t = (jnp.swapaxes(t, 1, 2)[0] for t in (w_in, m_w_in, v_w_in))
    w_in_buf = [_cast_w_in_transposed(w_in_t, chip)]
    sems_in, w_in_buf, token = _split_start("gather_ici_start_w_in", w_in_buf, _ici_plan(first, shapes), [3])
    bufs = [_cast_into_gathered(weights[n], n, chip, after=(token,)) for n in mid + late]
    params = {n: weights[n].reshape(1, -1) for n in VECTORS}
    h_in = _rowwise(_norm_fn, [_full(x[0])], [_full(params["g_mix"])], [(D_MODEL, BF16, D_MODEL, 0, False)], name="norm_in",
                    after=(token,))[0]
    conv_parts = _small_allreduce(_pack([jnp.zeros((N_CHIPS, 4, 512), F32).at[chip].set(0.5 * weights["conv_w"])]),
                                  "gather_conv_taps", after=(h_in, m_in_t, v_in_t, *bufs))
    w_in_buf = _split_wait("gather_ici_wait_w_in", w_in_buf, sems_in[0], _ici_plan(first, shapes), token, conv_parts)
    pass_sems, w_in_buf, token = _split_start("gather_pass_start_w_in", w_in_buf, _pass_on_plan(first, shapes), [3])
    plan = lambda refs: _ici_plan(mid, shapes)(refs[:4]) + _ici_plan(late, shapes)(refs[4:])
    sems_rest, bufs, token = _split_start("gather_ici_start_rest", bufs, plan, [12, 6], after=(token,))
    w_in_buf = _split_wait("gather_pass_wait_w_in", w_in_buf, pass_sems[0], _pass_on_plan(first, shapes), token)
    w_in_full = _w_in_columns(w_in_buf[0], to_shards=False)
    full = {"w_in": w_in_full,
            "w_dt": jnp.pad(w_in_full[:, D_MAIN:].reshape(D_MODEL, N_GROUPS, HEADS_PER_GROUP),
                            ((0, 0), (0, 0), (0, 128 - HEADS_PER_GROUP))).reshape(D_MODEL, DT_PAD)}
    in_flight = {}

    def more_weights(stage, after):
        if stage == "mixer_done":
            got = _split_wait("gather_ici_wait_mid", bufs[:4], sems_rest[0], _ici_plan(mid, shapes), after)
            sems, got, token = _split_start("gather_pass_start_mid", got, _pass_on_plan(mid, shapes), [12])
            return dict(zip(mid, _split_wait("gather_pass_wait_mid", got, sems[0], _pass_on_plan(mid, shapes), token)))
        if stage == "cross_started":
            got = _split_wait("gather_ici_wait_late", bufs[4:], sems_rest[1], _ici_plan(late, shapes), after)
            in_flight["late"] = _split_start("gather_pass_start_late", got, _pass_on_plan(late, shapes), [6])
            return {}
        sems, got, token = in_flight.pop("late")
        return dict(zip(late, _split_wait("gather_pass_wait_late", got, sems[0], _pass_on_plan(late, shapes), token, after)))

    params["conv_w"] = _unpack(conv_parts, [(N_CHIPS, 4, 512)])[0].transpose(1, 0, 2).reshape(4, 4 * 512)

    groups = (("w_down",), ("w_up",), ("w_co", "w_cq", "w_ckv", "w_out"), ("w_in",))
    scattered = []

    class GradStore(dict):
        pending = None

        def __setitem__(self, name, value):
            super().__setitem__(name, value)
            if "w_main" in self and "w_dt" in self and "w_in" not in self:
                gw_in = lax.dynamic_update_slice(self["w_main"], _unpad_heads(self["w_dt"]), (0, D_MAIN))
                self["w_in"] = _w_in_columns(gw_in, to_shards=True)
            for group in groups:
                if name in group and all(n in self for n in group):
                    self.settle()
                    pieces = [self[n].reshape(N_CHIPS, 2, shapes[n][0] // 2, shapes[n][1]) for n in group]
                    if group == groups[-1]:
                        self.scatter(group, pieces, _sibling_swap(pieces, "grad_swap_" + group[0]))
                    else:
                        landing = [lax.empty((N_CHIPS,) + a.shape[2:], BF16) for a in pieces]
                        sems, thru, self.token = _split_start("grad_swap_start_" + group[0], pieces + landing,
                                                              _swap_plan(len(pieces)), [len(pieces)])
                        self.pending = (group, sems[0], thru)

        def settle(self, *after):
            if self.pending is not None:
                group, sems, thru = self.pending
                self.pending = None
                thru = _split_wait("grad_swap_wait_" + group[0], thru, sems, _swap_plan(len(group)), *after)
                self.scatter(group, thru[:len(group)], thru[len(group):])

        def scatter(self, group, pieces, from_sibling):
            sums = [_add_halves(a, r, c_idx, "add_halves_" + n) for n, a, r in zip(group, pieces, from_sibling)]
            landing = [lax.empty((3,) + s.shape[1:], BF16) for s in sums]
            sems, thru, self.token = _split_start("grad_scatter_start_" + group[0], sums + landing,
                                                  _scatter_plan(len(sums)), [3 * len(sums)])
            scattered.append((group, sems[0], thru))

    loss, grad_x, grads = _local_step(x[0], mem[0], positions[0], loss_target[0], params, full, more_weights, GradStore(),
                                      h_in)

    out_g, out_d, out_m, out_v = {}, {}, {}, {}

    def finish(entries, order, token):
        halves = {}
        for group, sems, thru in entries:
            thru = _split_wait("grad_scatter_wait_" + group[0], thru, sems, _scatter_plan(len(group)), token)
            for i, n in enumerate(group):
                halves[n] = _sum_chips(thru[i], thru[len(group) + i], chip, "sum_chips_" + n)
        sources = [halves[n] for n in order]
        landing = [lax.empty(s.shape, F32) for s in sources]
        sems, thru, token = _split_start("grad_share_start_" + order[0], sources + landing, _share_plan(len(order)),
                                         [1] * len(order))
        for i, n in enumerate(order):
            own, other = _split_wait("grad_share_wait_" + n, [thru[i], thru[len(order) + i]], sems[i], _share_plan(1), token)
            if n == "w_in":
                res_t = _adamw_w_in_transposed(w_in_t, own, other, m_in_t, v_in_t, c_idx)
                out_g[n], out_d[n], out_m[n], out_v[n] = (t.T for t in res_t)
            else:
                out_g[n], out_d[n], out_m[n], out_v[n] = _adamw_halves(weights[n], own, other, mom_m[n], mom_v[n], c_idx,
                                                                       "adamw_" + n)
            token = out_v[n]
        return token

    token = finish(scattered[:-1], ("w_cq", "w_co", "w_ckv", "w_out", "w_up", "w_down"), grad_x)
    finish(scattered[-1:], ("w_in",), token)

    names = VECTORS + ("conv_w",)
    summed = _small_allreduce(_pack_rows([grads[n] for n in names] + [loss]), "allreduce_vectors")
    total_loss = summed[sum(_slot_rows(grads[n].size) for n in names), 0]
    small_out = _adamw_vectors(summed, chip, [(args[n], args["m_" + n], args["v_" + n]) for n in VECTORS],
                               (weights["conv_w"], mom_m["conv_w"], mom_v["conv_w"]))
    for n, res in zip(names, small_out):
        out_g[n], out_d[n], out_m[n], out_v[n] = (t.reshape(weights[n].shape) for t in res)

    outs =[total_loss, grad_x[None]]
    for group in (out_g, out_d, out_m, out_v):
        outs += [group[n][None] for n in WEIGHTS]
    return tuple(outs)
```

```python
import functools
import math

import jax
import jax.numpy as jnp
from jax import lax
from jax.experimental import pallas as pl
from jax.experimental.pallas import tpu as pltpu

F32 = jnp.float32
BF16 = jnp.bfloat16

SEQ = 2048
D_MODEL = 2048
HEAD = 64
D_ATTN = 1024
D_SSM = 1024
N_GROUPS = 4
N_STATE = 128
CHUNK = 128
ATT_BLK = 128
N_MEM = 256
D_CROSS = 512
D_MAIN = 6144
N_DT = 16
DT_PAD = 512
ROT = 16
ROPE_THETA = 500000.0
EPS = 1e-6
NEG = -1e30
BRANCH_BLOCKS = (16, 4, 1)
DILATIONS = (1, 4, 16)

ADAM_LR, ADAM_B1, ADAM_B2, ADAM_EPS, ADAM_WD, ADAM_STEP = 0.001, 0.9, 0.999, 1e-08, 0.01, 10

VMEM_LIMIT = 56 * 1024 * 1024
MESH = pl.DeviceIdType.MESH


def _params(sem, **kw):
    return pltpu.CompilerParams(dimension_semantics=sem, vmem_limit_bytes=VMEM_LIMIT, **kw)


def _bdot(a, b, dims):
    return lax.dot_general(a.astype(BF16), b.astype(BF16), (dims, ((), ())), preferred_element_type=F32)


def _fdot(a, b, dims):
    return lax.dot_general(a, b, (dims, ((), ())), preferred_element_type=F32, precision=lax.Precision.HIGHEST)


NN = ((1,), (0,))
NT = ((1,), (1,))
TN = ((0,), (0,))


def _tile(n, want):
    t = min(n, want)
    while n % t:
        t //= 2
    return t


def _matmul(a, b, *, mode, name, outs, extra=(), epilogue=None, col_shards=1, after=(), n_cols=None, out_cols=None,
            tile_sums=0, tm=1024, tn=1024, tk=2048):
    if mode == "nn":
        (m, k), n = a.shape, b.shape[1]
    elif mode == "nt":
        (m, k), n = a.shape, b.shape[0]
    else:
        (k, m), n = a.shape, b.shape[1]
    n = n if n_cols is None else n_cols
    tm, tn, tk = _tile(m, tm), _tile(n // col_shards, tn), _tile(k, tk)
    nk = k // tk
    per_shard = n // col_shards // tn
    dims = {"nn": NN, "nt": NT, "tn": TN}[mode]
    a_spec = pl.BlockSpec((tk, tm), lambda i, j, kk: (kk, i)) if mode == "tn" else pl.BlockSpec((tm, tk), lambda i, j, kk: (i, kk))
    b_spec = pl.BlockSpec((tn, tk), lambda i, j, kk: (j, kk)) if mode == "nt" else pl.BlockSpec((tk, tn), lambda i, j, kk: (kk, j))
    o_spec = pl.BlockSpec((tm, tn), lambda i, j, kk: (i, j))
    n_extra, n_out, n_after = len(extra), len(outs), len(after)

    def body(a_ref, b_ref, *rest):
        extra_refs, out_refs, acc_ref = rest[:n_extra], rest[n_extra + n_after:-1], rest[-1]

        def finish(acc):
            res = (acc,) if epilogue is None else epilogue(acc, *[e[...] for e in extra_refs])
            for o_ref, r in zip(out_refs[:n_out], res):
                o_ref[...] = r.astype(o_ref.dtype)
            for o_ref, r in zip(out_refs[n_out:], res[n_out:]):
                o_ref[...] = jnp.broadcast_to(r, o_ref.shape)

        if nk == 1:
            finish(_bdot(a_ref[...], b_ref[...], dims))
            return
        kk = pl.program_id(2)

        @pl.when(kk == 0)
        def _():
            acc_ref[...] = jnp.zeros_like(acc_ref)

        acc_ref[...] += _bdot(a_ref[...], b_ref[...], dims)

        @pl.when(kk == nk - 1)
        def _():
            finish(acc_ref[...])

    if col_shards == 1:
        out_specs, out_dims = [o_spec] * n_out, (m, n if out_cols is None else out_cols)
    else:
        sharded = pl.BlockSpec((None, tm, tn), lambda i, j, kk: (j // per_shard, i, j % per_shard))
        out_specs, out_dims = [sharded] * n_out, (col_shards, m, n // col_shards)
    res = pl.pallas_call(
        body, name=name, grid=(m // tm, n // tn, nk),
        in_specs=[a_spec, b_spec] + [o_spec] * n_extra + [pl.BlockSpec(memory_space=pl.ANY)] * n_after,
        out_specs=out_specs + [pl.BlockSpec((8, 128), lambda i, j, kk: (i, j))] * tile_sums,
        out_shape=[jax.ShapeDtypeStruct(out_dims, dt) for dt in outs]
        + [jax.ShapeDtypeStruct((m // tm * 8, n // tn * 128), F32)] * tile_sums,
        scratch_shapes=[pltpu.VMEM((tm, tn) if nk > 1 else (8, 128), F32)],
        compiler_params=_params(("parallel", "parallel", "arbitrary")),
    )(a, b, *extra, *after)
    res = list(res[:n_out]) + [t[::8, ::128] for t in res[n_out:]]
    return res[0] if len(res) == 1 else res


def _row_spec(tr, bw, cb, per_group):
    return pl.BlockSpec((tr, bw), (lambda g, i: (i, cb + g)) if per_group else (lambda g, i: (i, cb)))


def _vec_spec(bw, cb, per_group):
    return pl.BlockSpec((1, bw), (lambda g, i: (0, cb + g)) if per_group else (lambda g, i: (0, cb)))


def _rowwise(fn, rows, vecs, outs, *, name, n_rows=SEQ, tr=256, groups=1, after=()):
    n_r, n_v, n_after = len(rows), len(vecs), len(after)

    def body(*refs):
        vals = [r[...].astype(F32) for r in refs[:n_r + n_v]]
        res = fn(*vals)
        for o_ref, r in zip(refs[n_r + n_v + n_after:], res):
            o_ref[...] = r.astype(o_ref.dtype)

    res = pl.pallas_call(
        body, name=name, grid=(groups, n_rows // tr),
        in_specs=[_row_spec(tr, bw, cb, pg) for _, bw, cb, pg in rows] + [_vec_spec(bw, cb, pg) for _, bw, cb, pg in vecs]
        + [pl.BlockSpec(memory_space=pl.ANY)] * n_after,
        out_specs=[_row_spec(tr, bw, cb, pg) for _, _, bw, cb, pg in outs],
        out_shape=[jax.ShapeDtypeStruct((n_rows, w), dt) for w, dt, _, _, _ in outs],
        compiler_params=_params(("parallel", "parallel")),
    )(*[r[0] for r in rows], *[v[0] for v in vecs], *after)
    return res


def _rowwise_vjp(fn, rows, vecs, cts, row_grads, vec_grads, *, name, n_rows=SEQ, tr=256, groups=1, after=()):
    n_r, n_v, n_after = len(rows), len(vecs), len(after)
    ct_ops = [op for group in cts for op in group]
    ct_sizes = [len(group) for group in cts]
    res_ops = [g[6] for g in row_grads if g[6] is not None]
    n_ct, n_res, n_rg = len(ct_ops), len(res_ops), len(row_grads)

    def body(*refs):
        vals = [r[...].astype(F32) for r in refs[:n_r + n_v]]
        pos = n_r + n_v
        ct_vals = []
        for size in ct_sizes:
            acc = refs[pos][...].astype(F32)
            for t in range(1, size):
                acc = acc + refs[pos + t][...].astype(F32)
            ct_vals.append(acc)
            pos += size
        res_refs = refs[pos:pos + n_res]
        out_refs = refs[pos + n_res + n_after:]
        _, pullback = jax.vjp(fn, *vals)
        grads = pullback(tuple(ct_vals))
        r_i = 0
        for o_ref, g in zip(out_refs[:n_rg], row_grads):
            val = grads[g[0]]
            if g[6] is not None:
                val = val + res_refs[r_i][...].astype(F32)
                r_i += 1
            o_ref[...] = val.astype(o_ref.dtype)
        first = (pl.program_id(1) == 0)
        for o_ref, g in zip(out_refs[n_rg:], vec_grads):
            val = jnp.sum(grads[n_r + g[0]], axis=0, keepdims=True)
            init = first if g[4] else jnp.logical_and(first, pl.program_id(0) == 0)

            @pl.when(init)
            def _(o_ref=o_ref, val=val):
                o_ref[...] = val

            @pl.when(jnp.logical_not(init))
            def _(o_ref=o_ref, val=val):
                o_ref[...] += val

    in_specs = [_row_spec(tr, bw, cb, pg) for _, bw, cb, pg in rows] + [_vec_spec(bw, cb, pg) for _, bw, cb, pg in vecs]
    in_specs += [_row_spec(tr, bw, cb, pg) for _, bw, cb, pg in ct_ops + res_ops] + [pl.BlockSpec(memory_space=pl.ANY)] * n_after
    out_specs =[_row_spec(tr, g[3], g[4], g[5]) for g in row_grads] + [_vec_spec(g[2], g[3], g[4]) for g in vec_grads]
    out_shape = [jax.ShapeDtypeStruct((n_rows, g[1]), g[2]) for g in row_grads]
    out_shape += [jax.ShapeDtypeStruct((1, g[1]), F32) for g in vec_grads]
    return pl.pallas_call(
        body, name=name, grid=(groups, n_rows // tr),
        in_specs=in_specs, out_specs=out_specs, out_shape=out_shape,
        compiler_params=_params(("arbitrary", "arbitrary")),
    )(*[r[0] for r in rows], *[v[0] for v in vecs], *[c[0] for c in ct_ops], *[r[0] for r in res_ops], *after)


def _full(arr, width=None):
    return (arr, arr.shape[1] if width is None else width, 0, False)


def _make_xor(sh):
    def raw(x):
        n = x.shape[-1]
        lane = lax.broadcasted_iota(jnp.int32, x.shape, x.ndim - 1)
        up = pltpu.roll(x, n - sh, x.ndim - 1)
        down = pltpu.roll(x, sh, x.ndim - 1)
        return jnp.where((lane & sh) == 0, up, down)

    f = jax.custom_vjp(raw)
    f.defvjp(lambda x: (raw(x), None), lambda _, ct: (raw(ct),))
    return f


_SWAP_ROPE_HALVES = _make_xor(ROT // 2)


def _head_sum(x):
    n = x.shape[-1]
    same_head = (lax.broadcasted_iota(jnp.int32, (n, n), 0) // HEAD) == (lax.broadcasted_iota(jnp.int32, (n, n), 1) // HEAD)
    return _fdot(x, same_head.astype(F32), NN)


def _rms(x, g):
    return x * lax.rsqrt(jnp.mean(x * x, axis=-1, keepdims=True) + EPS) * g


def _head_rms_rope(x, g, cos, sin, scale):
    y = x * lax.rsqrt(_head_sum(x * x) * (1.0 / HEAD) + EPS) * g
    return (y * cos + _SWAP_ROPE_HALVES(y) * sin) * scale


def _qk_fn(q, k, v, cos, sin, gq, gk):
    return (_head_rms_rope(q, gq, cos, sin, HEAD ** -0.5), _head_rms_rope(k, gk, cos, sin, 1.0), v)


def _norm_fn(x, g):
    return (_rms(x, g),)


def _merge_fn(o0, o1, o2, l0, l1, l2, g):
    m = lax.stop_gradient(jnp.maximum(jnp.maximum(l0, l1), l2))
    e0, e1, e2 = jnp.exp(l0 - m), jnp.exp(l1 - m), jnp.exp(l2 - m)
    mix = (e0 * o0 + e1 * o1 + e2 * o2) / (e0 + e1 + e2)
    return (_rms(mix, g),)


def _gate_fn(y, z, g):
    return (_rms(y * (z * jax.nn.sigmoid(z)), g),)


def _attn_pair(q, kc, vc, kp=None, vp=None, has_prev=None):
    pick0, pick1 = _head_picks()
    k_band, v_band, mask = _attn_band(kc, vc, kp, vp, has_prev)
    s = jnp.where(mask, _bdot(jnp.concatenate([q * pick0, q * pick1], axis=0), k_band, NT), NEG)
    m = jnp.max(s, axis=-1, keepdims=True)
    p = jnp.exp(s - m)
    den = jnp.sum(p, axis=-1, keepdims=True)
    acc = _bdot(p, v_band, NN) * (1.0 / den)
    lse_rows = m + jnp.log(den)
    o = pick0 * acc[:ATT_BLK] + pick1 * acc[ATT_BLK:]
    lse = pick0 * lse_rows[:ATT_BLK] + pick1 * lse_rows[ATT_BLK:]
    return o, lse


def _head_picks():
    lane = lax.broadcasted_iota(jnp.int32, (1, 2 * HEAD), 1)
    return (lane < HEAD).astype(F32), (lane >= HEAD).astype(F32)


def _attn_band(kc, vc, kp, vp, has_prev):
    n_keys = ATT_BLK if kp is None else 2 * ATT_BLK
    qi = lax.broadcasted_iota(jnp.int32, (2 * ATT_BLK, n_keys), 0) & (ATT_BLK - 1)
    kj = lax.broadcasted_iota(jnp.int32, (2 * ATT_BLK, n_keys), 1)
    if kp is None:
        return kc, vc, qi >= kj
    in_prev = jnp.logical_and(jnp.logical_and(kj < ATT_BLK, kj >= qi), has_prev)
    mask = jnp.logical_or(in_prev, jnp.logical_and(kj >= ATT_BLK, qi >= kj - ATT_BLK))
    return jnp.concatenate([kp, kc], axis=0), jnp.concatenate([vp, vc], axis=0), mask


def _attn_config(b):
    r = DILATIONS[b]
    return r, ATT_BLK * r, (512 if r == 1 else 128), BRANCH_BLOCKS[b] > 1


def _for_residues(r, fn):
    if r <= 4:
        for rho in range(r):
            fn(rho)
    else:
        def step(t, carry):
            for u in range(4):
                fn(4 * t + u)
            return carry

        lax.fori_loop(0, r // 4, step, 0)


def _strided_rows(start, r):
    if r > 1:
        return pl.ds(start, ATT_BLK, stride=r)
    return pl.ds(start if isinstance(start, int) else pl.multiple_of(start, ATT_BLK), ATT_BLK)


def _attention_fwd(qn, kn, vn, b):
    r, rows, lanes, with_prev = _attn_config(b)
    cur = pl.BlockSpec((rows, lanes), lambda g, n: (n, g))
    prev = pl.BlockSpec((rows, lanes), lambda g, n: (jnp.maximum(n - 1, 0), g))

    def body(*refs):
        ins, (o_ref, l_ref) = refs[:-2], refs[-2:]
        has_prev = pl.program_id(1) > 0

        def one(rho):
            sub = _strided_rows(rho, r)
            for pair in range(lanes // 128):
                sl = pl.ds(pair * 128, 128)
                args = [ref[sub, sl] for ref in ins] + ([has_prev] if with_prev else [])
                o_ref[sub, sl], l_ref[sub, sl] = _attn_pair(*args)

        _for_residues(r, one)

    operands = (qn, kn, vn, kn, vn) if with_prev else (qn, kn, vn)
    return pl.pallas_call(
        body, name="attn_fwd_%d" % r, grid=(D_ATTN // lanes, SEQ // rows),
        in_specs=[cur, cur, cur] + ([prev, prev] if with_prev else []), out_specs=[cur, cur],
        out_shape=[jax.ShapeDtypeStruct((SEQ, D_ATTN), F32)] * 2,
        compiler_params=_params(("parallel", "parallel")),
    )(*operands)


def _attn_pair_bwd(q, kc, vc, kp, vp, o, lse, do, dl, has_prev):
    pick0, pick1 = _head_picks()
    lane = lax.broadcasted_iota(jnp.int32, (1, 2 * HEAD), 1)
    k_band, v_band, mask = _attn_band(kc, vc, kp, vp, has_prev)
    q2 = jnp.concatenate([q * pick0, q * pick1], axis=0)
    do2 = jnp.concatenate([do * pick0, do * pick1], axis=0)
    lse2 = jnp.concatenate([jnp.sum(lse * (lane == 0).astype(F32), axis=-1, keepdims=True),
                            jnp.sum(lse * (lane == HEAD).astype(F32), axis=-1, keepdims=True)], axis=0)
    base = jnp.sum(jnp.concatenate([dl * pick0, dl * pick1], axis=0) - do2 * jnp.concatenate([o, o], axis=0),
                   axis=-1, keepdims=True)
    p = jnp.exp(jnp.where(mask, _bdot(q2, k_band, NT), NEG) - lse2)
    ds = p * (_bdot(do2, v_band, NT) + base)
    dq2 = _bdot(ds, k_band, NN)
    dq = pick0 * dq2[:ATT_BLK] + pick1 * dq2[ATT_BLK:]
    dk, dv = _bdot(ds, q2, TN), _bdot(p, do2, TN)
    if kp is None:
        return dq, dk, dv
    return dq, dk[ATT_BLK:], dv[ATT_BLK:], dk[:ATT_BLK], dv[:ATT_BLK]


def _attention_bwd(qn, kn, vn, o, lse, do, dl, b):
    r, rows, lanes, with_prev = _attn_config(b)
    cur = pl.BlockSpec((rows, lanes), lambda g, n: (n, g))
    prev = pl.BlockSpec((rows, lanes), lambda g, n: (jnp.maximum(n - 1, 0), g))
    whole = pl.BlockSpec((SEQ, lanes), lambda g, n: (0, g))
    n_in = 5 if with_prev else 3

    def body(*refs):
        ins, (o_ref, l_ref, do_ref, dl_ref, dq_ref, dk_ref, dv_ref) = refs[:n_in], refs[n_in:]
        n = pl.program_id(1)

        @pl.when(n == 0)
        def _():
            dk_ref[...] = jnp.zeros_like(dk_ref)
            dv_ref[...] = jnp.zeros_like(dv_ref)

        def one(rho):
            sub = _strided_rows(rho, r)
            sub_c = _strided_rows(n * rows + rho, r)
            sub_p = _strided_rows(jnp.maximum(n - 1, 0) * rows + rho, r)
            for pair in range(lanes // 128):
                sl = pl.ds(pair * 128, 128)
                vals = [ref[sub, sl] for ref in ins] + ([] if with_prev else [None, None])
                grads = _attn_pair_bwd(*vals, o_ref[sub, sl], l_ref[sub, sl], do_ref[sub, sl], dl_ref[sub, sl], n > 0)
                dq_ref[sub, sl] = grads[0]
                dk_ref[sub_c, sl] += grads[1]
                dv_ref[sub_c, sl] += grads[2]
                if with_prev:
                    dk_ref[sub_p, sl] += grads[3]
                    dv_ref[sub_p, sl] += grads[4]

        _for_residues(r, one)

    operands = (qn, kn, vn, kn, vn) if with_prev else (qn, kn, vn)
    return pl.pallas_call(
        body, name="attn_bwd_%d" % r, grid=(D_ATTN // lanes, SEQ // rows),
        in_specs=[cur, cur, cur] + ([prev, prev] if with_prev else []) + [cur] * 4, out_specs=[cur, whole, whole],
        out_shape=[jax.ShapeDtypeStruct((SEQ, D_ATTN), F32)] * 3,
        compiler_params=_params(("parallel", "arbitrary")),
    )(*operands, o, lse, do, dl)


CONV_COLS = 256
XBC_BLOCK0 = 4096 // CONV_COLS


def _shift_rows(x, s):
    n = x.shape[0]
    t = lax.broadcasted_iota(jnp.int32, x.shape, 0)
    if s >= 0:
        return jnp.where(t >= s, pltpu.roll(x, s, 0), 0.0)
    return jnp.where(t < n + s, pltpu.roll(x, n + s, 0), 0.0)


def _conv_pre(x, w_ref, b_ref):
    delayed = [_shift_rows(x, 3 - k) for k in range(3)]
    pre = b_ref[...] + w_ref[3:4, :] * x
    for k in range(3):
        pre = pre + w_ref[k:k + 1, :] * delayed[k]
    return pre, delayed


def _conv_fwd(proj, conv_w, conv_b):
    cols = conv_w.shape[1]

    def body(x_ref, w_ref, b_ref, o_ref):
        pre, _ = _conv_pre(x_ref[...], w_ref, b_ref)
        o_ref[...] = pre * jax.nn.sigmoid(pre)

    blk = pl.BlockSpec((SEQ, CONV_COLS), lambda j: (0, j))
    return pl.pallas_call(
        body, name="conv_fwd", grid=(cols // CONV_COLS,),
        in_specs=[pl.BlockSpec((SEQ, CONV_COLS), lambda j: (0, XBC_BLOCK0 + j)),
                  pl.BlockSpec((4, CONV_COLS), lambda j: (0, j)), pl.BlockSpec((1, CONV_COLS), lambda j: (0, j))],
        out_specs=blk, out_shape=jax.ShapeDtypeStruct((SEQ, cols), F32),
        compiler_params=_params(("parallel",)),
    )(proj, conv_w, conv_b)


def _conv_bwd(proj, conv_w, conv_b, dxs, db, dc):
    cols = conv_w.shape[1]
    x_blocks, b_blocks = dxs.shape[1] // CONV_COLS, db.shape[1] // CONV_COLS

    def body(x_ref, w_ref, b_ref, dxs_ref, db_ref_in, dc_ref_in, dx_ref, dw_ref, db_ref):
        j = pl.program_id(0)
        dy = jnp.where(j < x_blocks, dxs_ref[...], jnp.where(j < x_blocks + b_blocks, db_ref_in[...], dc_ref_in[...]))
        x = x_ref[...]
        pre, delayed = _conv_pre(x, w_ref, b_ref)
        sg = jax.nn.sigmoid(pre)
        dpre = dy * (sg * (1.0 + pre * (1.0 - sg)))
        db_ref[...] = jnp.sum(dpre, axis=0, keepdims=True)
        dx = w_ref[3:4, :] * dpre
        dw_ref[3:4, :] = jnp.sum(dpre * x, axis=0, keepdims=True)
        for k in range(3):
            dx = dx + w_ref[k:k + 1, :] * _shift_rows(dpre, k - 3)
            dw_ref[k:k + 1, :] = jnp.sum(dpre * delayed[k], axis=0, keepdims=True)
        dw_ref[4:8, :] = jnp.zeros((4, CONV_COLS), F32)
        dx_ref[...] = dx.astype(dx_ref.dtype)

    blk = pl.BlockSpec((SEQ, CONV_COLS), lambda j: (0, j))
    parts = [pl.BlockSpec((SEQ, CONV_COLS), lambda j: (0, jnp.minimum(j, x_blocks - 1))),
             pl.BlockSpec((SEQ, CONV_COLS), lambda j: (0, jnp.clip(j - x_blocks, 0, b_blocks - 1))),
             pl.BlockSpec((SEQ, CONV_COLS), lambda j: (0, jnp.clip(j - x_blocks - b_blocks, 0, b_blocks - 1)))]
    return pl.pallas_call(
        body, name="conv_bwd", grid=(cols // CONV_COLS,),
        in_specs=[pl.BlockSpec((SEQ, CONV_COLS), lambda j: (0, XBC_BLOCK0 + j)),
                  pl.BlockSpec((4, CONV_COLS), lambda j: (0, j)), pl.BlockSpec((1, CONV_COLS), lambda j: (0, j))] + parts,
        out_specs=[blk, pl.BlockSpec((8, CONV_COLS), lambda j: (0, j)), pl.BlockSpec((1, CONV_COLS), lambda j: (0, j))],
        out_shape=[jax.ShapeDtypeStruct((SEQ, cols), BF16), jax.ShapeDtypeStruct((8, cols), F32),
                   jax.ShapeDtypeStruct((1, cols), F32)],
        compiler_params=_params(("parallel",)),
    )(proj, conv_w, conv_b, dxs, db, dc)


HEADS_PER_GROUP = 4


GROUP_WIDTH = HEADS_PER_GROUP * HEAD


def _ssd_chunk(x, bm, cm, dtr, bias, alog, dsk, h):
    row = lax.broadcasted_iota(jnp.int32, (CHUNK, CHUNK), 0)
    col = lax.broadcasted_iota(jnp.int32, (CHUNK, CHUNK), 1)
    causal = row >= col
    z = dtr + bias
    dt = jnp.maximum(z, 0.0) + jnp.log(1.0 + jnp.exp(-jnp.abs(z)))
    acs = _fdot(causal.astype(F32), dt * -jnp.exp(alog), NN)
    acs_t, dt_t = acs.T, dt.T
    cb = _bdot(cm, bm, NT)
    lane = lax.broadcasted_iota(jnp.int32, (1, CHUNK), 1)
    sub = lax.broadcasted_iota(jnp.int32, (CHUNK, 1), 0)
    wide = lax.broadcasted_iota(jnp.int32, (1, GROUP_WIDTH), 1) // HEAD
    tall = lax.broadcasted_iota(jnp.int32, (GROUP_WIDTH, 1), 0) // HEAD
    acs_last = jnp.sum(acs * (sub == CHUNK - 1).astype(F32), axis=0, keepdims=True)
    to_lanes = (lax.broadcasted_iota(jnp.int32, (CHUNK, GROUP_WIDTH), 0)
                == lax.broadcasted_iota(jnp.int32, (CHUNK, GROUP_WIDTH), 1) // HEAD).astype(F32)
    grow = _fdot(jnp.exp(acs), to_lanes, NN)
    keep = _fdot(jnp.exp(acs_last - acs) * dt, to_lanes, NN)
    w_parts, x_parts, skip, carry = [], [], 0.0, 0.0
    for j in range(HEADS_PER_GROUP):
        on_lane, on_sub = (lane == j).astype(F32), (sub == j).astype(F32)
        acs_c = jnp.sum(acs * on_lane, axis=1, keepdims=True)
        acs_r = jnp.sum(acs_t * on_sub, axis=0, keepdims=True)
        dt_r = jnp.sum(dt_t * on_sub, axis=0, keepdims=True)
        w_parts.append(cb * jnp.exp(jnp.where(causal, acs_c - acs_r, NEG)) * dt_r)
        x_parts.append(x * (wide == j).astype(F32))
        skip = skip + jnp.sum(dsk * on_lane, axis=1, keepdims=True) * (wide == j).astype(F32)
        carry = carry + jnp.sum(jnp.exp(acs_last) * on_lane, axis=1, keepdims=True) * (tall == j).astype(F32)
    y_diag = _bdot(jnp.concatenate(w_parts, axis=1), jnp.concatenate(x_parts, axis=0), NN)
    y = y_diag + _bdot(cm, h, NT) * grow + skip * x
    return y, h * carry + _bdot(x * keep, bm, TN)


GROUPS_PER_STEP = 2
SSD_STEPS = N_GROUPS // GROUPS_PER_STEP


def _ssd_specs(reverse):
    n_chunks = SEQ // CHUNK
    c_of = (lambda c: n_chunks - 1 - c) if reverse else (lambda c: c)
    x_w, n_w, dt_w = GROUPS_PER_STEP * GROUP_WIDTH, GROUPS_PER_STEP * N_STATE, GROUPS_PER_STEP * 128
    x_spec = pl.BlockSpec((CHUNK, x_w), lambda g, c: (c_of(c), g))
    b_spec = pl.BlockSpec((CHUNK, n_w), lambda g, c: (c_of(c), D_SSM // n_w + g))
    c_spec = pl.BlockSpec((CHUNK, n_w), lambda g, c: (c_of(c), (D_SSM + N_GROUPS * N_STATE) // n_w + g))
    dt_spec = pl.BlockSpec((CHUNK, dt_w), lambda g, c: (c_of(c), g))
    vec_spec = pl.BlockSpec((1, dt_w), lambda g, c: (0, g))
    h_spec = pl.BlockSpec((None, GROUPS_PER_STEP, GROUP_WIDTH, N_STATE), lambda g, c: (c_of(c), g, 0, 0))
    return x_spec, b_spec, c_spec, dt_spec, vec_spec, h_spec


def _group_slices(u):
    return pl.ds(u * GROUP_WIDTH, GROUP_WIDTH), pl.ds(u * N_STATE, N_STATE), pl.ds(u * 128, 128)


def _ssd_gated_chunk(x, bm, cm, dtr, bias, alog, dsk, h, z, g_out):
    y, h_new = _ssd_chunk(x, bm, cm, dtr, bias, alog, dsk, h)
    return _gate_fn(y, z, g_out)[0], h_new


def _ssd_gate_specs(reverse):
    x_spec = _ssd_specs(reverse)[0]
    z_block0 = 3 * D_ATTN // x_spec.block_shape[1]
    z_spec = pl.BlockSpec(x_spec.block_shape, lambda g, c: (x_spec.index_map(g, c)[0], z_block0 + g))
    return z_spec, pl.BlockSpec((1, x_spec.block_shape[1]), lambda g, c: (0, g))


def _ssd_fwd(xbc, dt_raw, bias, alog, dsk, proj, g_out):
    x_spec, b_spec, c_spec, dt_spec, vec_spec, h_spec = _ssd_specs(False)
    z_spec, g_spec = _ssd_gate_specs(False)

    def body(x_ref, b_ref, c_ref, dt_ref, bias_ref, alog_ref, dsk_ref, z_ref, g_ref, ssm_ref, hin_ref, h_scr):
        @pl.when(pl.program_id(1) == 0)
        def _():
            h_scr[...] = jnp.zeros_like(h_scr)

        for u in range(GROUPS_PER_STEP):
            xs, ns, ds = _group_slices(u)
            h = h_scr[u]
            hin_ref[u] = h
            ssm, h_scr[u] = _ssd_gated_chunk(x_ref[:, xs], b_ref[:, ns], c_ref[:, ns], dt_ref[:, ds], bias_ref[:, ds],
                                             alog_ref[:, ds], dsk_ref[:, ds], h, z_ref[:, xs], g_ref[:, xs])
            ssm_ref[:, xs] = ssm.astype(ssm_ref.dtype)

    return pl.pallas_call(
        body, name="ssd_fwd", grid=(SSD_STEPS, SEQ // CHUNK),
        in_specs=[x_spec, b_spec, c_spec, dt_spec, vec_spec, vec_spec, vec_spec, z_spec, g_spec],
        out_specs=[x_spec, h_spec],
        out_shape=[jax.ShapeDtypeStruct((SEQ, D_SSM), BF16),
                   jax.ShapeDtypeStruct((SEQ // CHUNK, N_GROUPS, GROUP_WIDTH, N_STATE), F32)],
        scratch_shapes=[pltpu.VMEM((GROUPS_PER_STEP, GROUP_WIDTH, N_STATE), F32)],
        compiler_params=_params(("parallel", "arbitrary")),
    )(xbc, xbc, xbc, dt_raw, bias, alog, dsk, proj, g_out)


def _ssd_bwd(xbc, dt_raw, bias, alog, dsk, h_in, proj, g_out, dmix):
    x_spec, b_spec, c_spec, dt_spec, vec_spec, h_spec = _ssd_specs(True)
    z_spec, g_spec = _ssd_gate_specs(True)
    ct_block0 = D_ATTN // x_spec.block_shape[1]
    ct_spec = pl.BlockSpec(x_spec.block_shape, lambda g, c: (x_spec.index_map(g, c)[0], ct_block0 + g))

    def body(x_ref, b_ref, c_ref, dt_ref, bias_ref, alog_ref, dsk_ref, hin_ref, z_ref, g_ref, ct_ref,
             dx_ref, db_ref, dc_ref, ddt_ref, dbias_ref, dalog_ref, ddsk_ref, dz_ref, dg_ref, dh_scr):
        first = pl.program_id(1) == 0

        @pl.when(first)
        def _():
            dh_scr[...] = jnp.zeros_like(dh_scr)

        for u in range(GROUPS_PER_STEP):
            xs, ns, ds = _group_slices(u)
            _, pullback = jax.vjp(_ssd_gated_chunk, x_ref[:, xs], b_ref[:, ns], c_ref[:, ns], dt_ref[:, ds], bias_ref[:, ds],
                                  alog_ref[:, ds], dsk_ref[:, ds], hin_ref[u], z_ref[:, xs], g_ref[:, xs])
            g = pullback((ct_ref[:, xs], dh_scr[u]))
            dx_ref[:, xs], db_ref[:, ns], dc_ref[:, ns] = g[0], g[1], g[2]
            ddt_ref[:, ds] = g[3].astype(ddt_ref.dtype)
            dh_scr[u] = g[7]
            dz_ref[:, xs] = g[8].astype(dz_ref.dtype)
            sums = ((dbias_ref, g[4], ds), (dalog_ref, g[5], ds), (ddsk_ref, g[6], ds),
                    (dg_ref, jnp.sum(g[9], axis=0, keepdims=True), xs))
            for o_ref, val, lanes in sums:
                @pl.when(first)
                def _(o_ref=o_ref, val=val, lanes=lanes):
                    o_ref[:, lanes] = val

                @pl.when(jnp.logical_not(first))
                def _(o_ref=o_ref, val=val, lanes=lanes):
                    o_ref[:, lanes] += val

    n_chunks = SEQ // CHUNK
    out_b = pl.BlockSpec((CHUNK, GROUPS_PER_STEP * N_STATE), lambda g, c: (n_chunks - 1 - c, g))
    return pl.pallas_call(
        body, name="ssd_bwd", grid=(SSD_STEPS, n_chunks),
        in_specs=[x_spec, b_spec, c_spec, dt_spec, vec_spec, vec_spec, vec_spec, h_spec, z_spec, g_spec, ct_spec],
        out_specs=[x_spec, out_b, out_b, dt_spec, vec_spec, vec_spec, vec_spec, x_spec, g_spec],
        out_shape=[jax.ShapeDtypeStruct((SEQ, D_SSM), F32), jax.ShapeDtypeStruct((SEQ, N_GROUPS * N_STATE), F32),
                   jax.ShapeDtypeStruct((SEQ, N_GROUPS * N_STATE), F32), jax.ShapeDtypeStruct((SEQ, DT_PAD), BF16),
                   jax.ShapeDtypeStruct((1, DT_PAD), F32), jax.ShapeDtypeStruct((1, DT_PAD), F32),
                   jax.ShapeDtypeStruct((1, DT_PAD), F32), jax.ShapeDtypeStruct((SEQ, D_SSM), BF16),
                   jax.ShapeDtypeStruct((1, D_SSM), F32)],
        scratch_shapes=[pltpu.VMEM((GROUPS_PER_STEP, GROUP_WIDTH, N_STATE), F32)],
        compiler_params=_params(("parallel", "arbitrary")),
    )(xbc, xbc, xbc, dt_raw, bias, alog, dsk, h_in, proj, g_out, dmix)


CROSS_HEAD = 128
CROSS_ROWS = 512


def _cross_head(q, k, v, gq, gk):
    qn = _rms(q, gq) * (CROSS_HEAD ** -0.5)
    kn = _rms(k, gk)
    s = _bdot(qn, kn, NT)
    p = jnp.exp(s - lax.stop_gradient(jnp.max(s, axis=-1, keepdims=True)))
    return _bdot(p, v, NN) * (1.0 / jnp.sum(p, axis=-1, keepdims=True))


def _cross_specs():
    q_spec = pl.BlockSpec((CROSS_ROWS, CROSS_HEAD), lambda h, i: (i, h))
    k_spec = pl.BlockSpec((N_MEM, CROSS_HEAD), lambda h, i: (0, h))
    v_spec = pl.BlockSpec((N_MEM, CROSS_HEAD), lambda h, i: (0, 4 + h))
    g_spec = pl.BlockSpec((1, CROSS_HEAD), lambda h, i: (0, 0))
    return q_spec, k_spec, v_spec, g_spec


def _cross_fwd(qc, kv, gq, gk):
    q_spec, k_spec, v_spec, g_spec = _cross_specs()

    def body(q_ref, k_ref, v_ref, gq_ref, gk_ref, o_ref):
        o_ref[...] = _cross_head(q_ref[...], k_ref[...], v_ref[...], gq_ref[...], gk_ref[...]).astype(o_ref.dtype)

    return pl.pallas_call(
        body, name="cross_fwd", grid=(4, SEQ // CROSS_ROWS),
        in_specs=[q_spec, k_spec, v_spec, g_spec, g_spec], out_specs=q_spec,
        out_shape=jax.ShapeDtypeStruct((SEQ, D_CROSS), BF16),
        compiler_params=_params(("parallel", "parallel")),
    )(qc, kv, kv, gq, gk)


def _cross_bwd(qc, kv, gq, gk, do):
    q_spec, k_spec, v_spec, g_spec = _cross_specs()

    def body(q_ref, k_ref, v_ref, gq_ref, gk_ref, do_ref, dq_ref, dk_ref, dv_ref, dgq_ref, dgk_ref):
        _, pullback = jax.vjp(_cross_head, q_ref[...], k_ref[...], v_ref[...], gq_ref[...], gk_ref[...])
        dq, dk, dv, dgq, dgk = pullback(do_ref[...].astype(F32))
        dq_ref[...] = dq.astype(dq_ref.dtype)
        row0 = pl.program_id(1) == 0
        all0 = jnp.logical_and(row0, pl.program_id(0) == 0)
        for o_ref, val, init in ((dk_ref, dk, row0), (dv_ref, dv, row0), (dgq_ref, dgq, all0), (dgk_ref, dgk, all0)):
            @pl.when(init)
            def _(o_ref=o_ref, val=val):
                o_ref[...] = val

            @pl.when(jnp.logical_not(init))
            def _(o_ref=o_ref, val=val):
                o_ref[...] += val

    return pl.pallas_call(
        body, name="cross_bwd", grid=(4, SEQ // CROSS_ROWS),
        in_specs=[q_spec, k_spec, v_spec, g_spec, g_spec, q_spec],
        out_specs=[q_spec, k_spec, k_spec, g_spec, g_spec],
        out_shape=[jax.ShapeDtypeStruct((SEQ, D_CROSS), BF16), jax.ShapeDtypeStruct((N_MEM, D_CROSS), F32),
                   jax.ShapeDtypeStruct((N_MEM, D_CROSS), F32), jax.ShapeDtypeStruct((1, CROSS_HEAD), F32),
                   jax.ShapeDtypeStruct((1, CROSS_HEAD), F32)],
        compiler_params=_params(("arbitrary", "arbitrary")),
    )(qc, kv, kv, gq, gk, do)


def _loss_epilogue(acc, residual, target):
    err = acc + residual - target
    dy = err * (1.0 / D_MODEL)
    part = jnp.sum(jnp.sum(err * err, axis=1, keepdims=True), axis=0, keepdims=True) * (0.5 / D_MODEL)
    return dy, dy, part


def _pad_heads(v):
    return jnp.pad(v.reshape(N_GROUPS, HEADS_PER_GROUP), ((0, 0), (0, 128 - HEADS_PER_GROUP))).reshape(1, DT_PAD)


def _unpad_heads(v):
    return v.reshape(v.shape[0], N_GROUPS, 128)[:, :, :HEADS_PER_GROUP].reshape(v.shape[0], N_DT)


def _rope_tables(positions):
    half = ROT // 2
    inv_freq = ROPE_THETA ** (-2.0 * jnp.arange(half, dtype=F32) / ROT)
    ang = positions.reshape(SEQ, 1).astype(F32) * inv_freq
    cos, sin = jnp.cos(ang), jnp.sin(ang)
    ones, zeros = jnp.ones((SEQ, HEAD - ROT), F32), jnp.zeros((SEQ, HEAD - ROT), F32)
    cos_h = jnp.concatenate([cos, cos, ones], axis=1)
    sin_h = jnp.concatenate([-sin, sin, zeros], axis=1)
    return jnp.tile(cos_h, (1, 2)), jnp.tile(sin_h, (1, 2))


def _add_res(acc, res):
    return (acc + res,)


def _settle(grads, *after):
    if hasattr(grads, "settle"):
        grads.settle(*after)


def _take_token(grads):
    token = getattr(grads, "token", None)
    if token is None:
        return ()
    grads.token = None
    return (token,)


def _local_step(x, mem, positions, target, p, w, more_weights=None, grads=None, h=None):
    grads = {} if grads is None else grads
    w = dict(w)
    cos, sin = _rope_tables(positions)
    gq2, gk2 = jnp.tile(p["g_q"], (1, 2)), jnp.tile(p["g_k"], (1, 2))
    bias, alog, dsk = _pad_heads(p["dt_bias"]), _pad_heads(p["a_log"]), _pad_heads(p["d_skip"])
    norm_out = [(D_MODEL, BF16, D_MODEL, 0, False)]

    if h is None:
        h = _rowwise(_norm_fn, [_full(x)], [_full(p["g_mix"])], norm_out, name="norm_in")[0]
    proj = _matmul(h, w["w_in"], mode="nn", name="in_proj", outs=[F32], n_cols=D_MAIN)
    dt_raw = _matmul(h, w["w_dt"], mode="nn", name="dt_proj", outs=[F32])
    qk_rows = [(proj, 128, 0, True), (proj, 128, 8, True), (proj, 128, 16, True), _full(cos), _full(sin)]
    qk_vecs = [_full(gq2), _full(gk2)]
    qn, kn, vn = _rowwise(_qk_fn, qk_rows, qk_vecs, [(D_ATTN, F32, 128, 0, True)] * 3, name="qk_prep", groups=8, tr=1024)
    branches = [_attention_fwd(qn, kn, vn, b) for b in range(3)]
    merge_rows = [_full(o) for o, _ in branches] + [_full(lse) for _, lse in branches]
    attn = _rowwise(_merge_fn, merge_rows, [_full(p["g_attn_out"])], [(D_ATTN, BF16, D_ATTN, 0, False)], name="attn_merge")[0]
    xbc = _conv_fwd(proj, p["conv_w"], p["conv_b"])
    ssm, h_in = _ssd_fwd(xbc, dt_raw, bias, alog, dsk, proj, p["g_ssm_out"])
    mix = jnp.concatenate([attn, ssm], axis=1)
    if more_weights is not None:
        w.update(more_weights("mixer_done", mix))
    x1 = _matmul(mix, w["w_out"], mode="nn", name="out_proj", outs=[F32], extra=(x,), epilogue=_add_res)
    hc = _rowwise(_norm_fn, [_full(x1)], [_full(p["g_cross"])], norm_out, name="norm_cross")[0]
    memh = _rowwise(_norm_fn, [_full(mem)], [_full(p["g_mem"])], norm_out, name="norm_mem", n_rows=N_MEM)[0]
    qc = _matmul(hc, w["w_cq"], mode="nn", name="cq_proj", outs=[F32])
    if more_weights is not None:
        w.update(more_weights("cross_started", qc))
    kv = _matmul(memh, w["w_ckv"], mode="nn", name="ckv_proj", outs=[F32])
    oc = _cross_fwd(qc, kv, p["g_cq"], p["g_ck"])
    x2 = _matmul(oc, w["w_co"], mode="nn", name="co_proj", outs=[F32], extra=(x1,), epilogue=_add_res)
    hm = _rowwise(_norm_fn, [_full(x2)], [_full(p["g_mlp"])], norm_out, name="norm_mlp")[0]
    if more_weights is not None:
        w.update(more_weights("cross_done", hm))
    u, act = _matmul(hm, w["w_up"], mode="nn", name="up_proj", outs=[F32, BF16],
                     epilogue=lambda acc: (acc, jnp.square(jnp.maximum(acc, 0.0))))
    dy, dyb, loss_tiles = _matmul(act, w["w_down"], mode="nn", name="down_proj", outs=[F32, BF16], extra=(x2, target),
                                  epilogue=_loss_epilogue, tile_sums=1)
    loss = jnp.sum(loss_tiles).reshape(1, 1)

    grads["w_down"] = _matmul(act, dyb, mode="tn", name="dw_down", outs=[BF16], after=_take_token(grads))
    du = _matmul(dyb, w["w_down"], mode="nt", name="d_act", outs=[BF16], extra=(u,), after=_take_token(grads),
                 epilogue=lambda acc, uu: (acc * (2.0 * jnp.maximum(uu, 0.0)),))
    _settle(grads, du)
    grads["w_up"] = _matmul(hm, du, mode="tn", name="dw_up", outs=[BF16], col_shards=4, after=_take_token(grads))
    dhm = _matmul(du, w["w_up"], mode="nt", name="d_hm", outs=[F32], after=_take_token(grads), tk=4096)
    _settle(grads, dhm)
    dx2, grads["g_mlp"] = _rowwise_vjp(
        _norm_fn, [_full(x2)], [_full(p["g_mlp"])], [[_full(dhm)]],
        [(0, D_MODEL, F32, D_MODEL, 0, False, _full(dy))], [(0, D_MODEL, D_MODEL, 0, False)], name="norm_mlp_bwd")
    grads["w_co"] = _matmul(oc, dx2, mode="tn", name="dw_co", outs=[BF16], col_shards=4, after=_take_token(grads))
    doc = _matmul(dx2, w["w_co"], mode="nt", name="d_oc", outs=[BF16])
    dqc, dkc, dvc, grads["g_cq"], grads["g_ck"] = _cross_bwd(qc, kv, p["g_cq"], p["g_ck"], doc)
    grads["w_cq"] = _matmul(hc, dqc, mode="tn", name="dw_cq", outs=[BF16])
    dhc = _matmul(dqc, w["w_cq"], mode="nt", name="d_hc", outs=[F32])
    dkv = jnp.concatenate([dkc, dvc], axis=1)
    grads["w_ckv"] = _matmul(memh, dkv, mode="tn", name="dw_ckv", outs=[BF16])
    dmemh = _matmul(dkv, w["w_ckv"], mode="nt", name="d_memh", outs=[F32])
    grads["g_mem"] = _rowwise_vjp(_norm_fn, [_full(mem)], [_full(p["g_mem"])], [[_full(dmemh)]], [],
                                  [(0, D_MODEL, D_MODEL, 0, False)], name="norm_mem_bwd", n_rows=N_MEM)[0]
    dx1, grads["g_cross"] = _rowwise_vjp(
        _norm_fn, [_full(x1)], [_full(p["g_cross"])], [[_full(dhc)]],
        [(0, D_MODEL, F32, D_MODEL, 0, False, _full(dx2))], [(0, D_MODEL, D_MODEL, 0, False)], name="norm_cross_bwd")
    grads["w_out"] = _matmul(mix, dx1, mode="tn", name="dw_out", outs=[BF16])
    dmix = _matmul(dx1, w["w_out"], mode="nt", name="d_mix", outs=[F32], after=_take_token(grads))
    _settle(grads, dmix)
    merge_grads = [(i, D_ATTN, F32, D_ATTN, 0, False, None) for i in range(6)]
    *dol, grads["g_attn_out"] = _rowwise_vjp(
        _merge_fn, merge_rows, [_full(p["g_attn_out"])], [[(dmix, D_ATTN, 0, False)]],
        merge_grads, [(0, D_ATTN, D_ATTN, 0, False)], name="attn_merge_bwd", after=_take_token(grads))
    dqkv = [_attention_bwd(qn, kn, vn, *branches[b], dol[b], dol[3 + b], b) for b in range(3)]
    qk_cts = [[(dqkv[b][i], 128, 0, True) for b in range(3)] for i in range(3)]
    dq, dk, dv, dgq2, dgk2 = _rowwise_vjp(
        _qk_fn, qk_rows, qk_vecs, qk_cts, [(i, D_ATTN, BF16, 128, 0, True, None) for i in range(3)],
        [(0, 128, 128, 0, False), (1, 128, 128, 0, False)], name="qk_prep_bwd", groups=8, tr=512)
    grads["g_q"] = dgq2[:, :HEAD] + dgq2[:, HEAD:]
    grads["g_k"] = dgk2[:, :HEAD] + dgk2[:, HEAD:]
    dxs, db, dc, ddt, dbias, dalog, ddsk, dz, grads["g_ssm_out"] = _ssd_bwd(xbc, dt_raw, bias, alog, dsk, h_in, proj,
                                                                             p["g_ssm_out"], dmix)
    grads["dt_bias"], grads["a_log"], grads["d_skip"] = _unpad_heads(dbias), _unpad_heads(dalog), _unpad_heads(ddsk)
    dxbc_raw, dconv_w, grads["conv_b"] = _conv_bwd(proj, p["conv_w"], p["conv_b"], dxs, db, dc)
    grads["conv_w"] = dconv_w[:4]
    dproj = jnp.concatenate([dq, dk, dv, dz, dxbc_raw], axis=1)
    grads["w_main"] = _matmul(h, dproj, mode="tn", name="dw_main", outs=[BF16], out_cols=D_MAIN + N_DT)
    grads["w_dt"] = _matmul(h, ddt, mode="tn", name="dw_dt", outs=[BF16])
    dh = _matmul(dproj, w["w_in"], mode="nt", name="d_h_main", outs=[F32], after=_take_token(grads))
    dh = _matmul(ddt, w["w_dt"], mode="nt", name="d_h_dt", outs=[F32], extra=(dh,), epilogue=_add_res)
    grad_x, grads["g_mix"] = _rowwise_vjp(
        _norm_fn, [_full(x)], [_full(p["g_mix"])], [[_full(dh)]],
        [(0, D_MODEL, F32, D_MODEL, 0, False, _full(dx1))], [(0, D_MODEL, D_MODEL, 0, False)], name="norm_in_bwd")
    return loss, grad_x, grads


MATRICES = ("w_in", "w_out", "w_cq", "w_ckv", "w_co", "w_up", "w_down")
ROW_SHARDED = ("w_out", "w_cq", "w_ckv", "w_down")
N_CHIPS = 4
ANY = pl.BlockSpec(memory_space=pl.ANY)


def _place():
    return lax.axis_index("x"), lax.axis_index("y"), lax.axis_index("c")


def _other_chips(x, y):
    return [(1 - x, y), (x, 1 - y), (1 - x, 1 - y)]


def _remote(src, dst, send_sem, recv_sem, device):
    return pltpu.make_async_remote_copy(src_ref=src, dst_ref=dst, send_sem=send_sem, recv_sem=recv_sem,
                                        device_id=device, device_id_type=MESH)


def _gathered_shape(name, shard):
    rows, cols = shard.shape
    if name == "w_in":
        return (N_CHIPS, rows, cols)
    return (N_CHIPS * rows, cols) if name in ROW_SHARDED else (rows, N_CHIPS * cols)


def _shard_window(name, ref, rows, cols, chip, half):
    r0, nr = (0, rows) if half is None else (half * (rows // 2), rows // 2)
    if name == "w_in":
        return ref.at[chip, pl.ds(r0, nr), :]
    if name in ROW_SHARDED:
        return ref.at[pl.ds(chip * rows + r0, nr), :]
    return ref.at[pl.ds(r0, nr), pl.ds(pl.multiple_of(chip * cols, 128), cols)]


def _cast_into_gathered(w, name, chip, after=()):
    rows, cols = w.shape
    tr = _tile(rows, ROW_TILE)

    def body(chip_ref, w_ref, *rest):
        rest[-1][...] = w_ref[...].astype(BF16)

    if name == "w_in":
        out_spec = pl.BlockSpec((None, tr, cols), lambda i, chip_ref: (chip_ref[0], i, 0))
    elif name in ROW_SHARDED:
        out_spec = pl.BlockSpec((tr, cols), lambda i, chip_ref: (chip_ref[0] * (rows // tr) + i, 0))
    else:
        out_spec = pl.BlockSpec((tr, cols), lambda i, chip_ref: (i, chip_ref[0]))
    grid_spec = pltpu.PrefetchScalarGridSpec(
        num_scalar_prefetch=1, grid=(rows // tr,),
        in_specs=[pl.BlockSpec((tr, cols), lambda i, chip_ref: (i, 0))] + [pl.BlockSpec(memory_space=pl.ANY)] * len(after),
        out_specs=out_spec)
    return pl.pallas_call(body, name="cast_" + name, grid_spec=grid_spec,
                          out_shape=jax.ShapeDtypeStruct(_gathered_shape(name, w), BF16),
                          compiler_params=_params(("parallel",)))(chip.reshape(1).astype(jnp.int32), w, *after)


def _w_in_columns(arr, to_shards):
    rows, piece = D_MODEL, (D_MAIN + N_DT) // N_CHIPS
    tr = ROW_TILE

    def body(a_ref, o_ref):
        for j in range(N_CHIPS):
            if to_shards:
                o_ref[j] = a_ref[:, pl.ds(piece * j, piece)]
            else:
                o_ref[:, pl.ds(piece * j, piece)] = a_ref[j]

    pieces = pl.BlockSpec((N_CHIPS, tr, piece), lambda i: (0, i, 0))
    matrix = pl.BlockSpec((tr, N_CHIPS * piece), lambda i: (i, 0))
    out_dims = (N_CHIPS, rows, piece) if to_shards else (rows, N_CHIPS * piece)
    return pl.pallas_call(
        body, name="w_in_to_shards" if to_shards else "w_in_from_shards", grid=(rows // tr,),
        in_specs=[matrix if to_shards else pieces], out_specs=pieces if to_shards else matrix,
        out_shape=jax.ShapeDtypeStruct(out_dims, arr.dtype), compiler_params=_params(("parallel",)))(arr)


HBM = pl.BlockSpec(memory_space=pltpu.HBM)
SEM = pl.BlockSpec(memory_space=pltpu.SEMAPHORE)
EFFECT = pltpu.SideEffectType.DATAFLOW_SIDE_EFFECTING


def _split_start(name, bufs, plan, counts, after=()):
    n, n_g, n_after = len(bufs), len(counts), len(after)

    def body(*refs):
        ins, sems, token = refs[:n], refs[n + n_after:n + n_after + 2 * n_g], refs[-1]
        for g, copies in enumerate(plan(ins)):
            for i, (src, dst, device, _) in enumerate(copies):
                _remote(src, dst, sems[2 * g].at[i], sems[2 * g + 1].at[i], device).start()
        token[...] = jnp.zeros_like(token)

    sem_shapes = [pltpu.SemaphoreType.DMA((cnt,)) for cnt in counts for _ in range(2)]
    res = pl.pallas_call(
        body, name=name,
        out_shape=(*sem_shapes, *[pltpu.HBM(b.shape, b.dtype) for b in bufs], jax.ShapeDtypeStruct((8, 128), F32)),
        in_specs=(*(HBM,) * n, *(ANY,) * n_after),
        out_specs=(*(SEM,) * (2 * n_g), *(HBM,) * n, pl.BlockSpec(memory_space=pltpu.VMEM)),
        input_output_aliases={i: 2 * n_g + i for i in range(n)},
        compiler_params=pltpu.CompilerParams(has_side_effects=EFFECT),
    )(*[pltpu.with_memory_space_constraint(b, pltpu.HBM) for b in bufs], *after)
    sems = [(res[2 * g], res[2 * g + 1]) for g in range(n_g)]
    return sems, list(res[2 * n_g:2 * n_g + n]), res[-1]


def _split_wait(name, bufs, sems, plan, *after):
    n = len(bufs)

    def body(*refs):
        ins, send, recv = refs[:n], refs[n], refs[n + 1]
        (copies,) = plan(ins)
        for i, (src, _, device, landing) in enumerate(copies):
            cp = _remote(src, landing, send.at[i], recv.at[i], device)
            cp.wait_send()
            cp.wait_recv()

    res = pl.pallas_call(
        body, name=name, out_shape=tuple(pltpu.HBM(b.shape, b.dtype) for b in bufs),
        in_specs=(*(HBM,) * n, SEM, SEM, *(ANY,) * len(after)), out_specs=(HBM,) * n,
        input_output_aliases={i: i for i in range(n)},
        compiler_params=pltpu.CompilerParams(has_side_effects=EFFECT),
    )(*bufs, sems[0], sems[1], *after)
    return list(res)


def _ici_plan(names, shard_shapes):
    def plan(refs):
        x, y, c = _place()
        copies = []
        for ref, name in zip(refs, names):
            win = _shard_window(name, ref, *shard_shapes[name], 2 * x + y, c)
            for px, py in _other_chips(x, y):
                copies.append((win, win, (px, py, c), _shard_window(name, ref, *shard_shapes[name], 2 * px + py, c)))
        return [copies]
    return plan


def _pass_on_plan(names, shard_shapes):
    def plan(refs):
        x, y, c = _place()
        copies = []
        for ref, name in zip(refs, names):
            for px, py in _other_chips(x, y):
                win = _shard_window(name, ref, *shard_shapes[name], 2 * px + py, c)
                copies.append((win, win, (x, y, 1 - c), _shard_window(name, ref, *shard_shapes[name], 2 * px + py, 1 - c)))
        return [copies]
    return plan


def _swap_plan(n_pairs):
    def plan(refs):
        x, y, c = _place()
        return [[(src.at[:, 1 - c], dst, (x, y, 1 - c), dst) for src, dst in zip(refs[:n_pairs], refs[n_pairs:])]]
    return plan


def _share_plan(n_pairs):
    def plan(refs):
        x, y, c = _place()
        return [[(src, dst, (x, y, 1 - c), dst)] for src, dst in zip(refs[:n_pairs], refs[n_pairs:])]
    return plan


def _scatter_plan(n_pairs):
    def plan(refs):
        x, y, c = _place()
        copies = []
        for src, dst in zip(refs[:n_pairs], refs[n_pairs:]):
            for k, (px, py) in enumerate(_other_chips(x, y)):
                copies.append((src.at[2 * px + py], dst.at[k], (px, py, c), dst.at[k]))
        return [copies]
    return plan


def _sibling_swap(arrs, name):
    n = len(arrs)

    def body(*refs):
        ins, outs, send, recv = refs[:n], refs[n:2 * n], refs[2 * n], refs[2 * n + 1]
        x, y, c = _place()
        cps = [_remote(ins[w].at[:, 1 - c], outs[w], send.at[w], recv.at[w], (x, y, 1 - c)) for w in range(n)]
        for cp in cps:
            cp.start()
        for cp in cps:
            cp.wait()

    return pl.pallas_call(
        body, name=name, in_specs=[ANY] * n, out_specs=[ANY] * n,
        out_shape=[jax.ShapeDtypeStruct((a.shape[0],) + a.shape[2:], a.dtype) for a in arrs],
        scratch_shapes=[pltpu.SemaphoreType.DMA((n,))] * 2,
    )(*arrs)


def _small_allreduce(buf, name, after=()):
    rows = buf.shape[0]

    def body(x_ref, *rest):
        out_ref, all_ref, send_sems, recv_sems, local_sem = rest[len(after):]
        x, y, c = _place()
        me, sibling, chips = (x, y, c), (x, y, 1 - c), _other_chips(x, y)

        def block(px, py, pc):
            return all_ref.at[pl.ds((4 * px + 2 * py + pc) * rows, rows), :]

        def copy(k, blk, to, src=None):
            return _remote(block(*blk) if src is None else src, block(*blk), send_sems.at[k], recv_sems.at[k], to)

        own = pltpu.make_async_copy(x_ref, block(*me), local_sem)
        own.start()
        first = [copy(0, me, sibling, src=x_ref)] + [copy(1 + j, me, (*chip, c), src=x_ref) for j, chip in enumerate(chips)]
        for cp in first:
            cp.start()
        passed = [copy(4 + j, (*chip, c), sibling) for j, chip in enumerate(chips)]
        for j, chip in enumerate(chips):
            copy(1 + j, (*chip, c), me).wait_recv()
            passed[j].start()
        copy(0, sibling, me).wait_recv()
        for j, chip in enumerate(chips):
            copy(4 + j, (*chip, 1 - c), me).wait_recv()
        for cp in first + passed:
            cp.wait_send()
        own.wait()
        acc = all_ref[pl.ds(0, rows), :]
        for d in range(1, 8):
            acc = acc + all_ref[pl.ds(d * rows, rows), :]
        out_ref[...] = acc

    vmem = pl.BlockSpec(memory_space=pltpu.VMEM)
    return pl.pallas_call(
        body, name=name, in_specs=[vmem] + [ANY] * len(after), out_specs=vmem,
        out_shape=jax.ShapeDtypeStruct(buf.shape, F32),
        scratch_shapes=[pltpu.VMEM((8 * rows, 128), F32), pltpu.SemaphoreType.DMA((7,)), pltpu.SemaphoreType.DMA((7,)),
                        pltpu.SemaphoreType.DMA],
    )(buf, *after)


ROW_TILE = 256
BIG_ROW_TILE = 1024


def _add_halves(arr, recv, c, name):
    _, _, hr, cols = arr.shape
    tr = _tile(hr, BIG_ROW_TILE)

    def body(c_ref, a_ref, r_ref, o_ref):
        o_ref[...] = (a_ref[...].astype(F32) + r_ref[...].astype(F32)).astype(o_ref.dtype)

    piece = pl.BlockSpec((None, tr, cols), lambda j, i, c_ref: (j, i, 0))
    grid_spec = pltpu.PrefetchScalarGridSpec(
        num_scalar_prefetch=1, grid=(N_CHIPS, hr // tr),
        in_specs=[pl.BlockSpec((None, None, tr, cols), lambda j, i, c_ref: (j, c_ref[0], i, 0)), piece], out_specs=piece)
    return pl.pallas_call(body, name=name, grid_spec=grid_spec, out_shape=jax.ShapeDtypeStruct(recv.shape, BF16),
                          compiler_params=_params(("parallel", "parallel")))(c.reshape(1).astype(jnp.int32), arr, recv)


def _flip_slot(d):
    return jnp.where(d == 1, 1, jnp.where(d == 3, 2, 0))


def _sum_chips(p, q, chip, name):
    _, hr, cols = p.shape
    tr = _tile(hr, BIG_ROW_TILE)

    def body(chip_ref, p_ref, q_ref, o_ref):
        j = pl.program_id(1)
        term = jnp.where(j == chip_ref[0], p_ref[...].astype(F32), q_ref[...].astype(F32))

        @pl.when(j == 0)
        def _():
            o_ref[...] = term

        @pl.when(j != 0)
        def _():
            o_ref[...] += term

    grid_spec = pltpu.PrefetchScalarGridSpec(
        num_scalar_prefetch=1, grid=(hr // tr, N_CHIPS),
        in_specs=[pl.BlockSpec((None, tr, cols), lambda i, j, chip_ref: (chip_ref[0], i, 0)),
                  pl.BlockSpec((None, tr, cols), lambda i, j, chip_ref: (_flip_slot(j ^ chip_ref[0]), i, 0))],
        out_specs=pl.BlockSpec((tr, cols), lambda i, j, chip_ref: (i, 0)))
    return pl.pallas_call(body, name=name, grid_spec=grid_spec, out_shape=jax.ShapeDtypeStruct((hr, cols), F32),
                          compiler_params=_params(("parallel", "arbitrary")))(chip.reshape(1).astype(jnp.int32), p, q)


def _adamw_halves(w, g_own, g_other, m, v, c, name):
    rows, cols = w.shape
    tr = _tile(rows // 2, ROW_TILE)
    per_half = rows // 2 // tr

    def body(c_ref, w_ref, own_ref, other_ref, m_ref, v_ref, g_ref, d_ref, nm_ref, nv_ref):
        mine = (pl.program_id(0) // per_half) == c_ref[0]
        g_ = jnp.where(mine, own_ref[...], other_ref[...])
        g_ref[...] = g_
        d_ref[...], nm_ref[...], nv_ref[...] = _adamw_math(w_ref[...], g_, m_ref[...], v_ref[...])

    blk = pl.BlockSpec((tr, cols), lambda i, c_ref: (i, 0))
    own = pl.BlockSpec((tr, cols), lambda i, c_ref: (jnp.where(i // per_half == c_ref[0], i % per_half, 0), 0))
    other = pl.BlockSpec((tr, cols), lambda i, c_ref: (jnp.where(i // per_half == c_ref[0], 0, i % per_half), 0))
    grid_spec = pltpu.PrefetchScalarGridSpec(num_scalar_prefetch=1, grid=(rows // tr,),
                                             in_specs=[blk, own, other, blk, blk], out_specs=[blk] * 4)
    return pl.pallas_call(body, name=name, grid_spec=grid_spec, out_shape=[jax.ShapeDtypeStruct(w.shape, F32)] * 4,
                          compiler_params=_params(("parallel",)))(c.reshape(1).astype(jnp.int32), w, g_own, g_other, m, v)


W_IN_COLS = (D_MAIN + N_DT) // N_CHIPS
W_IN_MAIN = W_IN_COLS // 128 * 128
W_IN_TAIL = W_IN_COLS - 128
W_IN_PARTS = ((0, W_IN_MAIN), (W_IN_TAIL, 128))


def _cast_w_in_transposed(w_t, chip, after=()):
    def body(chip_ref, w_ref, *rest):
        for start, size in W_IN_PARTS:
            rest[-1][:, pl.ds(start, size)] = w_ref[pl.ds(start, size), :].T.astype(BF16)

    grid_spec = pltpu.PrefetchScalarGridSpec(
        num_scalar_prefetch=1, grid=(D_MODEL // ROW_TILE,),
        in_specs=[pl.BlockSpec((W_IN_COLS, ROW_TILE), lambda i, chip_ref: (0, i))] + [pl.BlockSpec(memory_space=pl.ANY)] * len(after),
        out_specs=pl.BlockSpec((None, ROW_TILE, W_IN_COLS), lambda i, chip_ref: (chip_ref[0], i, 0)))
    return pl.pallas_call(body, name="cast_w_in", grid_spec=grid_spec,
                          out_shape=jax.ShapeDtypeStruct((N_CHIPS, D_MODEL, W_IN_COLS), BF16),
                          compiler_params=_params(("parallel",)))(chip.reshape(1).astype(jnp.int32), w_t, *after)


def _adamw_w_in_transposed(w_t, g_own, g_other, m_t, v_t, c):
    per_half = D_MODEL // 2 // ROW_TILE

    def body(c_ref, w_ref, own_ref, other_ref, m_ref, v_ref, g_ref, d_ref, nm_ref, nv_ref):
        mine = (pl.program_id(0) // per_half) == c_ref[0]
        for start, size in W_IN_PARTS:
            cols, rows = pl.ds(start, size), pl.ds(start, size)
            g_ = jnp.where(mine, own_ref[:, cols], other_ref[:, cols]).T
            g_ref[rows, :] = g_
            d_ref[rows, :], nm_ref[rows, :], nv_ref[rows, :] = _adamw_math(w_ref[rows, :], g_, m_ref[rows, :], v_ref[rows, :])

    blk = pl.BlockSpec((W_IN_COLS, ROW_TILE), lambda i, c_ref: (0, i))
    own = pl.BlockSpec((ROW_TILE, W_IN_COLS), lambda i, c_ref: (jnp.where(i // per_half == c_ref[0], i % per_half, 0), 0))
    other = pl.BlockSpec((ROW_TILE, W_IN_COLS), lambda i, c_ref: (jnp.where(i // per_half == c_ref[0], 0, i % per_half), 0))
    grid_spec = pltpu.PrefetchScalarGridSpec(num_scalar_prefetch=1, grid=(D_MODEL // ROW_TILE,),
                                             in_specs=[blk, own, other, blk, blk], out_specs=[blk] * 4)
    return pl.pallas_call(body, name="adamw_w_in", grid_spec=grid_spec, out_shape=[jax.ShapeDtypeStruct(w_t.shape, F32)] * 4,
                          compiler_params=_params(("parallel",)))(c.reshape(1).astype(jnp.int32), w_t, g_own, g_other, m_t, v_t)


def _adamw_math(w, g, m, v):
    m_new = ADAM_B1 * m + (1.0 - ADAM_B1) * g
    v_new = ADAM_B2 * v + (1.0 - ADAM_B2) * (g * g)
    m_hat = m_new / (1.0 - ADAM_B1 ** ADAM_STEP)
    v_hat = v_new / (1.0 - ADAM_B2 ** ADAM_STEP)
    return -ADAM_LR * (m_hat / (jnp.sqrt(v_hat) + ADAM_EPS) + ADAM_WD * w), m_new, v_new


VECTORS = ("g_mix", "g_q", "g_k", "g_attn_out", "conv_b", "dt_bias", "a_log", "d_skip", "g_ssm_out", "g_cross", "g_mem",
           "g_cq", "g_ck", "g_mlp")
WEIGHTS = ("g_mix", "w_in", "g_q", "g_k", "g_attn_out", "conv_w", "conv_b", "dt_bias", "a_log", "d_skip", "g_ssm_out", "w_out",
           "g_cross", "g_mem", "w_cq", "w_ckv", "g_cq", "g_ck", "w_co", "g_mlp", "w_up", "w_down")


def _pack(parts):
    flat = jnp.concatenate([t.reshape(-1) for t in parts])
    total = -(-flat.shape[0] // 1024) * 1024
    return jnp.pad(flat, (0, total - flat.shape[0])).reshape(total // 128, 128)


def _rows_of(n):
    return -(-n // 128)


def _slot_rows(n):
    return -(-n // 1024) * 8


def _pack_rows(parts):
    rows = []
    for t in parts:
        flat = t.reshape(-1)
        rows.append(jnp.pad(flat, (0, 128 * _slot_rows(flat.shape[0]) - flat.shape[0])).reshape(-1, 128))
    return jnp.concatenate(rows)


def _adamw_vectors(summed, chip, vectors, conv):
    groups = list(vectors) + [conv]
    offsets, row = [], 0
    for w, _, _ in groups:
        offsets.append(row)
        row += _slot_rows(w.shape[1]) if w.shape[0] == 1 else _slot_rows(4 * N_CHIPS * w.shape[1])
    conv_blocks = _rows_of(conv[0].shape[1])

    def body(chip_ref, sum_ref, *refs):
        ins, outs = refs[:3 * len(groups)], refs[3 * len(groups):]

        def update(i, g, idx):
            w_ref, m_ref, v_ref = ins[3 * i:3 * i + 3]
            delta, new_m, new_v = _adamw_math(w_ref[idx], g, m_ref[idx], v_ref[idx])
            for o_ref, val in zip(outs[4 * i:4 * i + 4], (g, delta, new_m, new_v)):
                o_ref[idx] = val

        for i, (w, _, _) in enumerate(vectors):
            for t in range(_rows_of(w.shape[1])):
                width = min(128, w.shape[1] - 128 * t)
                update(i, sum_ref[pl.ds(offsets[i] + t, 1), pl.ds(0, width)], (slice(None), pl.ds(128 * t, width)))
        for tap in range(4):
            for blk in range(conv_blocks):
                src = offsets[-1] + tap * N_CHIPS * conv_blocks + chip_ref[0] * conv_blocks + blk
                update(len(vectors), sum_ref[pl.ds(src, 1), :], (pl.ds(tap, 1), pl.ds(128 * blk, 128)))

    def whole(a):
        return pl.BlockSpec(a.shape, lambda i, chip_ref: (0,) * a.ndim)

    operands = [t for group in groups for t in group]
    grid_spec = pltpu.PrefetchScalarGridSpec(
        num_scalar_prefetch=1, grid=(1,), in_specs=[whole(summed)] + [whole(t) for t in operands],
        out_specs=[whole(w) for w, _, _ in groups for _ in range(4)])
    res = pl.pallas_call(body, name="adamw_vectors", grid_spec=grid_spec,
                         out_shape=[jax.ShapeDtypeStruct(w.shape, F32) for w, _, _ in groups for _ in range(4)],
                         compiler_params=_params(("arbitrary",)))(chip.reshape(1).astype(jnp.int32), summed, *operands)
    return [res[4 * i:4 * i + 4] for i in range(len(groups))]


def _unpack(buf, shapes):
    flat, out, pos = buf.reshape(-1), [], 0
    for shape in shapes:
        size = math.prod(shape)
        out.append(flat[pos:pos + size].reshape(shape))
        pos += size
    return out


def kernel(x, mem, positions, g_mix, w_in, g_q, g_k, g_attn_out, conv_w, conv_b, dt_bias, a_log, d_skip, g_ssm_out, w_out, g_cross, g_mem, w_cq, w_ckv, g_cq, g_ck, w_co, g_mlp, w_up, w_down, loss_target, m_g_mix, m_w_in, m_g_q, m_g_k, m_g_attn_out, m_conv_w, m_conv_b, m_dt_bias, m_a_log, m_d_skip, m_g_ssm_out, m_w_out, m_g_cross, m_g_mem, m_w_cq, m_w_ckv, m_g_cq, m_g_ck, m_w_co, m_g_mlp, m_w_up, m_w_down, v_g_mix, v_w_in, v_g_q, v_g_k, v_g_attn_out, v_conv_w, v_conv_b, v_dt_bias, v_a_log, v_d_skip, v_g_ssm_out, v_w_out, v_g_cross, v_g_mem, v_w_cq, v_w_ckv, v_g_cq, v_g_ck, v_w_co, v_g_mlp, v_w_up, v_w_down):
    args = dict(locals())
    weights = {n: args[n][0] for n in WEIGHTS}
    mom_m = {n: args["m_" + n][0] for n in WEIGHTS}
    mom_v = {n: args["v_" + n][0] for n in WEIGHTS}
    x_idx, y_idx, c_idx = _place()
    chip = 2 * x_idx + y_idx

    shapes = {n: weights[n].shape for n in MATRICES}
    first, mid, late = ("w_in",), ("w_out", "w_cq", "w_ckv", "w_co"), ("w_up", "w_down")
    w_in_t, m_in_t, v_in_t = (jnp.swapaxes(t, 1, 2)[0] for t in (w_in, m_w_in, v_w_in))
    w_in_buf = [_cast_w_in_transposed(w_in_t, chip)]
    sems_in, w_in_buf, token = _split_start("gather_ici_start_w_in", w_in_buf, _ici_plan(first, shapes), [3])
    bufs = [_cast_into_gathered(weights[n], n, chip, after=(token,)) for n in mid + late]
    params = {n: weights[n].reshape(1, -1) for n in VECTORS}
    h_in = _rowwise(_norm_fn, [_full(x[0])], [_full(params["g_mix"])], [(D_MODEL, BF16, D_MODEL, 0, False)], name="norm_in",
                    after=(token,))[0]
    conv_parts = _small_allreduce(_pack([jnp.zeros((N_CHIPS, 4, 512), F32).at[chip].set(0.5 * weights["conv_w"])]),
                                  "gather_conv_taps", after=(h_in, m_in_t, v_in_t, *bufs))
    w_in_buf = _split_wait("gather_ici_wait_w_in", w_in_buf, sems_in[0], _ici_plan(first, shapes), token, conv_parts)
    pass_sems, w_in_buf, token = _split_start("gather_pass_start_w_in", w_in_buf, _pass_on_plan(first, shapes), [3])
    plan = lambda refs: _ici_plan(mid, shapes)(refs[:4]) + _ici_plan(late, shapes)(refs[4:])
    sems_rest, bufs, token = _split_start("gather_ici_start_rest", bufs, plan, [12, 6], after=(token,))
    w_in_buf = _split_wait("gather_pass_wait_w_in", w_in_buf, pass_sems[0], _pass_on_plan(first, shapes), token)
    w_in_full = _w_in_columns(w_in_buf[0], to_shards=False)
    full = {"w_in": w_in_full,
            "w_dt": jnp.pad(w_in_full[:, D_MAIN:].reshape(D_MODEL, N_GROUPS, HEADS_PER_GROUP),
                            ((0, 0), (0, 0), (0, 128 - HEADS_PER_GROUP))).reshape(D_MODEL, DT_PAD)}
    in_flight = {}

    def more_weights(stage, after):
        if stage == "mixer_done":
            got = _split_wait("gather_ici_wait_mid", bufs[:4], sems_rest[0], _ici_plan(mid, shapes), after)
            sems, got, token = _split_start("gather_pass_start_mid", got, _pass_on_plan(mid, shapes), [12])
            return dict(zip(mid, _split_wait("gather_pass_wait_mid", got, sems[0], _pass_on_plan(mid, shapes), token)))
        if stage == "cross_started":
            got = _split_wait("gather_ici_wait_late", bufs[4:], sems_rest[1], _ici_plan(late, shapes), after)
            in_flight["late"] = _split_start("gather_pass_start_late", got, _pass_on_plan(late, shapes), [6])
            return {}
        sems, got, token = in_flight.pop("late")
        return dict(zip(late, _split_wait("gather_pass_wait_late", got, sems[0], _pass_on_plan(late, shapes), token, after)))

    params["conv_w"] = _unpack(conv_parts, [(N_CHIPS, 4, 512)])[0].transpose(1, 0, 2).reshape(4, 4 * 512)

    groups = (("w_down",), ("w_up",), ("w_co", "w_cq", "w_ckv", "w_out"), ("w_in",))
    scattered = []

    class GradStore(dict):
        pending = None

        def __setitem__(self, name, value):
            super().__setitem__(name, value)
            if "w_main" in self and "w_dt" in self and "w_in" not in self:
                gw_in = lax.dynamic_update_slice(self["w_main"], _unpad_heads(self["w_dt"]), (0, D_MAIN))
                self["w_in"] = _w_in_columns(gw_in, to_shards=True)
            for group in groups:
                if name in group and all(n in self for n in group):
                    self.settle()
                    pieces = [self[n].reshape(N_CHIPS, 2, shapes[n][0] // 2, shapes[n][1]) for n in group]
                    if group == groups[-1]:
                        self.scatter(group, pieces, _sibling_swap(pieces, "grad_swap_" + group[0]))
                    else:
                        landing = [lax.empty((N_CHIPS,) + a.shape[2:], BF16) for a in pieces]
                        sems, thru, self.token = _split_start("grad_swap_start_" + group[0], pieces + landing,
                                                              _swap_plan(len(pieces)), [len(pieces)])
                        self.pending = (group, sems[0], thru)

        def settle(self, *after):
            if self.pending is not None:
                group, sems, thru = self.pending
                self.pending = None
                thru = _split_wait("grad_swap_wait_" + group[0], thru, sems, _swap_plan(len(group)), *after)
                self.scatter(group, thru[:len(group)], thru[len(group):])

        def scatter(self, group, pieces, from_sibling):
            sums = [_add_halves(a, r, c_idx, "add_halves_" + n) for n, a, r in zip(group, pieces, from_sibling)]
            landing = [lax.empty((3,) + s.shape[1:], BF16) for s in sums]
            sems, thru, self.token = _split_start("grad_scatter_start_" + group[0], sums + landing,
                                                  _scatter_plan(len(sums)), [3 * len(sums)])
            scattered.append((group, sems[0], thru))

    loss, grad_x, grads = _local_step(x[0], mem[0], positions[0], loss_target[0], params, full, more_weights, GradStore(),
                                      h_in)

    out_g, out_d, out_m, out_v = {}, {}, {}, {}

    def finish(entries, order, token):
        halves = {}
        for group, sems, thru in entries:
            thru = _split_wait("grad_scatter_wait_" + group[0], thru, sems, _scatter_plan(len(group)), token)
            for i, n in enumerate(group):
                halves[n] = _sum_chips(thru[i], thru[len(group) + i], chip, "sum_chips_" + n)
        sources = [halves[n] for n in order]
        landing = [lax.empty(s.shape, F32) for s in sources]
        sems, thru, token = _split_start("grad_share_start_" + order[0], sources + landing, _share_plan(len(order)),
                                         [1] * len(order))
        for i, n in enumerate(order):
            own, other = _split_wait("grad_share_wait_" + n, [thru[i], thru[len(order) + i]], sems[i], _share_plan(1), token)
            if n == "w_in":
                res_t = _adamw_w_in_transposed(w_in_t, own, other, m_in_t, v_in_t, c_idx)
                out_g[n], out_d[n], out_m[n], out_v[n] = (t.T for t in res_t)
            else:
                out_g[n], out_d[n], out_m[n], out_v[n] = _adamw_halves(weights[n], own, other, mom_m[n], mom_v[n], c_idx,
                                                                       "adamw_" + n)
            token = out_v[n]
        return token

    token = finish(scattered[:-1], ("w_cq", "w_co", "w_ckv", "w_out", "w_up", "w_down"), grad_x)
    finish(scattered[-1:], ("w_in",), token)

    names = VECTORS + ("conv_w",)
    summed = _small_allreduce(_pack_rows([grads[n] for n in names] + [loss]), "allreduce_vectors")
    total_loss = summed[sum(_slot_rows(grads[n].size) for n in names), 0]
    small_out = _adamw_vectors(summed, chip, [(args[n], args["m_" + n], args["v_" + n]) for n in VECTORS],
                               (weights["conv_w"], mom_m["conv_w"], mom_v["conv_w"]))
    for n, res in zip(names, small_out):
        out_g[n], out_d[n], out_m[n], out_v[n] = (t.reshape(weights[n].shape) for t in res)

    outs =[total_loss, grad_x[None]]
    for group in (out_g, out_d, out_m, out_v):
        outs += [group[n][None] for n in WEIGHTS]
    return tuple(outs)
```

```python
import functools
import math

import jax
import jax.numpy as jnp
from jax import lax
from jax.experimental import pallas as pl
from jax.experimental.pallas import tpu as pltpu

F32 = jnp.float32
BF16 = jnp.bfloat16

SEQ = 2048
D_MODEL = 2048
HEAD = 64
D_ATTN = 1024
D_SSM = 1024
N_GROUPS = 4
N_STATE = 128
CHUNK = 128
ATT_BLK = 128
N_MEM = 256
D_CROSS = 512
D_MAIN = 6144
N_DT = 16
DT_PAD = 512
ROT = 16
ROPE_THETA = 500000.0
EPS = 1e-6
NEG = -1e30
BRANCH_BLOCKS = (16, 4, 1)
DILATIONS = (1, 4, 16)

ADAM_LR, ADAM_B1, ADAM_B2, ADAM_EPS, ADAM_WD, ADAM_STEP = 0.001, 0.9, 0.999, 1e-08, 0.01, 10

VMEM_LIMIT = 56 * 1024 * 1024
MESH = pl.DeviceIdType.MESH


def _params(sem, **kw):
    return pltpu.CompilerParams(dimension_semantics=sem, vmem_limit_bytes=VMEM_LIMIT, **kw)


def _bdot(a, b, dims):
    return lax.dot_general(a.astype(BF16), b.astype(BF16), (dims, ((), ())), preferred_element_type=F32)


def _fdot(a, b, dims):
    return lax.dot_general(a, b, (dims, ((), ())), preferred_element_type=F32, precision=lax.Precision.HIGHEST)


NN = ((1,), (0,))
NT = ((1,), (1,))
TN = ((0,), (0,))


def _tile(n, want):
    t = min(n, want)
    while n % t:
        t //= 2
    return t


def _matmul(a, b, *, mode, name, outs, extra=(), epilogue=None, col_shards=1, after=(), n_cols=None, out_cols=None,
            tile_sums=0, tm=1024, tn=1024, tk=2048):
    if mode == "nn":
        (m, k), n = a.shape, b.shape[1]
    elif mode == "nt":
        (m, k), n = a.shape, b.shape[0]
    else:
        (k, m), n = a.shape, b.shape[1]
    n = n if n_cols is None else n_cols
    tm, tn, tk = _tile(m, tm), _tile(n // col_shards, tn), _tile(k, tk)
    nk = k // tk
    per_shard = n // col_shards // tn
    dims = {"nn": NN, "nt": NT, "tn": TN}[mode]
    a_spec = pl.BlockSpec((tk, tm), lambda i, j, kk: (kk, i)) if mode == "tn" else pl.BlockSpec((tm, tk), lambda i, j, kk: (i, kk))
    b_spec = pl.BlockSpec((tn, tk), lambda i, j, kk: (j, kk)) if mode == "nt" else pl.BlockSpec((tk, tn), lambda i, j, kk: (kk, j))
    o_spec = pl.BlockSpec((tm, tn), lambda i, j, kk: (i, j))
    n_extra, n_out, n_after = len(extra), len(outs), len(after)

    def body(a_ref, b_ref, *rest):
        extra_refs, out_refs, acc_ref = rest[:n_extra], rest[n_extra + n_after:-1], rest[-1]

        def finish(acc):
            res = (acc,) if epilogue is None else epilogue(acc, *[e[...] for e in extra_refs])
            for o_ref, r in zip(out_refs[:n_out], res):
                o_ref[...] = r.astype(o_ref.dtype)
            for o_ref, r in zip(out_refs[n_out:], res[n_out:]):
                o_ref[...] = jnp.broadcast_to(r, o_ref.shape)

        if nk == 1:
            finish(_bdot(a_ref[...], b_ref[...], dims))
            return
        kk = pl.program_id(2)

        @pl.when(kk == 0)
        def _():
            acc_ref[...] = jnp.zeros_like(acc_ref)

        acc_ref[...] += _bdot(a_ref[...], b_ref[...], dims)

        @pl.when(kk == nk - 1)
        def _():
            finish(acc_ref[...])

    if col_shards == 1:
        out_specs, out_dims = [o_spec] * n_out, (m, n if out_cols is None else out_cols)
    else:
        sharded = pl.BlockSpec((None, tm, tn), lambda i, j, kk: (j // per_shard, i, j % per_shard))
        out_specs, out_dims = [sharded] * n_out, (col_shards, m, n // col_shards)
    res = pl.pallas_call(
        body, name=name, grid=(m // tm, n // tn, nk),
        in_specs=[a_spec, b_spec] + [o_spec] * n_extra + [pl.BlockSpec(memory_space=pl.ANY)] * n_after,
        out_specs=out_specs + [pl.BlockSpec((8, 128), lambda i, j, kk: (i, j))] * tile_sums,
        out_shape=[jax.ShapeDtypeStruct(out_dims, dt) for dt in outs]
        + [jax.ShapeDtypeStruct((m // tm * 8, n // tn * 128), F32)] * tile_sums,
        scratch_shapes=[pltpu.VMEM((tm, tn) if nk > 1 else (8, 128), F32)],
        compiler_params=_params(("parallel", "parallel", "arbitrary")),
    )(a, b, *extra, *after)
    res = list(res[:n_out]) + [t[::8, ::128] for t in res[n_out:]]
    return res[0] if len(res) == 1 else res


def _row_spec(tr, bw, cb, per_group):
    return pl.BlockSpec((tr, bw), (lambda g, i: (i, cb + g)) if per_group else (lambda g, i: (i, cb)))


def _vec_spec(bw, cb, per_group):
    return pl.BlockSpec((1, bw), (lambda g, i: (0, cb + g)) if per_group else (lambda g, i: (0, cb)))


def _rowwise(fn, rows, vecs, outs, *, name, n_rows=SEQ, tr=512, groups=1, after=()):
    n_r, n_v, n_after = len(rows), len(vecs), len(after)

    def body(*refs):
        vals = [r[...].astype(F32) for r in refs[:n_r + n_v]]
        res = fn(*vals)
        for o_ref, r in zip(refs[n_r + n_v + n_after:], res):
            o_ref[...] = r.astype(o_ref.dtype)

    res = pl.pallas_call(
        body, name=name, grid=(groups, n_rows // tr),
        in_specs=[_row_spec(tr, bw, cb, pg) for _, bw, cb, pg in rows] + [_vec_spec(bw, cb, pg) for _, bw, cb, pg in vecs]
        + [pl.BlockSpec(memory_space=pl.ANY)] * n_after,
        out_specs=[_row_spec(tr, bw, cb, pg) for _, _, bw, cb, pg in outs],
        out_shape=[jax.ShapeDtypeStruct((n_rows, w), dt) for w, dt, _, _, _ in outs],
        compiler_params=_params(("parallel", "parallel")),
    )(*[r[0] for r in rows], *[v[0] for v in vecs], *after)
    return res


def _rowwise_vjp(fn, rows, vecs, cts, row_grads, vec_grads, *, name, n_rows=SEQ, tr=512, groups=1, after=()):
    n_r, n_v, n_after = len(rows), len(vecs), len(after)
    ct_ops = [op for group in cts for op in group]
    ct_sizes = [len(group) for group in cts]
    res_ops = [g[6] for g in row_grads if g[6] is not None]
    n_ct, n_res, n_rg = len(ct_ops), len(res_ops), len(row_grads)

    def body(*refs):
        vals = [r[...].astype(F32) for r in refs[:n_r + n_v]]
        pos = n_r + n_v
        ct_vals = []
        for size in ct_sizes:
            acc = refs[pos][...].astype(F32)
            for t in range(1, size):
                acc = acc + refs[pos + t][...].astype(F32)
            ct_vals.append(acc)
            pos += size
        res_refs = refs[pos:pos + n_res]
        out_refs = refs[pos + n_res + n_after:]
        _, pullback = jax.vjp(fn, *vals)
        grads = pullback(tuple(ct_vals))
        r_i = 0
        for o_ref, g in zip(out_refs[:n_rg], row_grads):
            val = grads[g[0]]
            if g[6] is not None:
                val = val + res_refs[r_i][...].astype(F32)
                r_i += 1
            o_ref[...] = val.astype(o_ref.dtype)
        first = (pl.program_id(1) == 0)
        for o_ref, g in zip(out_refs[n_rg:], vec_grads):
            val = jnp.sum(grads[n_r + g[0]], axis=0, keepdims=True)
            init = first if g[4] else jnp.logical_and(first, pl.program_id(0) == 0)

            @pl.when(init)
            def _(o_ref=o_ref, val=val):
                o_ref[...] = val

            @pl.when(jnp.logical_not(init))
            def _(o_ref=o_ref, val=val):
                o_ref[...] += val

    in_specs = [_row_spec(tr, bw, cb, pg) for _, bw, cb, pg in rows] + [_vec_spec(bw, cb, pg) for _, bw, cb, pg in vecs]
    in_specs += [_row_spec(tr, bw, cb, pg) for _, bw, cb, pg in ct_ops + res_ops] + [pl.BlockSpec(memory_space=pl.ANY)] * n_after
    out_specs =[_row_spec(tr, g[3], g[4], g[5]) for g in row_grads] + [_vec_spec(g[2], g[3], g[4]) for g in vec_grads]
    out_shape = [jax.ShapeDtypeStruct((n_rows, g[1]), g[2]) for g in row_grads]
    out_shape += [jax.ShapeDtypeStruct((1, g[1]), F32) for g in vec_grads]
    return pl.pallas_call(
        body, name=name, grid=(groups, n_rows // tr),
        in_specs=in_specs, out_specs=out_specs, out_shape=out_shape,
        compiler_params=_params(("arbitrary", "arbitrary")),
    )(*[r[0] for r in rows], *[v[0] for v in vecs], *[c[0] for c in ct_ops], *[r[0] for r in res_ops], *after)


def _full(arr, width=None):
    return (arr, arr.shape[1] if width is None else width, 0, False)


def _make_xor(sh):
    def raw(x):
        n = x.shape[-1]
        lane = lax.broadcasted_iota(jnp.int32, x.shape, x.ndim - 1)
        up = pltpu.roll(x, n - sh, x.ndim - 1)
        down = pltpu.roll(x, sh, x.ndim - 1)
        return jnp.where((lane & sh) == 0, up, down)

    f = jax.custom_vjp(raw)
    f.defvjp(lambda x: (raw(x), None), lambda _, ct: (raw(ct),))
    return f


_SWAP_ROPE_HALVES = _make_xor(ROT // 2)


def _head_sum(x):
    n = x.shape[-1]
    same_head = (lax.broadcasted_iota(jnp.int32, (n, n), 0) // HEAD) == (lax.broadcasted_iota(jnp.int32, (n, n), 1) // HEAD)
    return _fdot(x, same_head.astype(F32), NN)


def _rms(x, g):
    return x * lax.rsqrt(jnp.mean(x * x, axis=-1, keepdims=True) + EPS) * g


def _head_rms_rope(x, g, cos, sin, scale):
    y = x * lax.rsqrt(_head_sum(x * x) * (1.0 / HEAD) + EPS) * g
    return (y * cos + _SWAP_ROPE_HALVES(y) * sin) * scale


def _qk_fn(q, k, v, cos, sin, gq, gk):
    return (_head_rms_rope(q, gq, cos, sin, HEAD ** -0.5), _head_rms_rope(k, gk, cos, sin, 1.0), v)


def _norm_fn(x, g):
    return (_rms(x, g),)


def _merge_fn(o0, o1, o2, l0, l1, l2, g):
    m = lax.stop_gradient(jnp.maximum(jnp.maximum(l0, l1), l2))
    e0, e1, e2 = jnp.exp(l0 - m), jnp.exp(l1 - m), jnp.exp(l2 - m)
    mix = (e0 * o0 + e1 * o1 + e2 * o2) / (e0 + e1 + e2)
    return (_rms(mix, g),)


def _gate_fn(y, z, g):
    return (_rms(y * (z * jax.nn.sigmoid(z)), g),)


def _attn_pair(q, kc, vc, kp=None, vp=None, has_prev=None):
    pick0, pick1 = _head_picks()
    k_band, v_band, mask = _attn_band(kc, vc, kp, vp, has_prev)
    s = jnp.where(mask, _bdot(jnp.concatenate([q * pick0, q * pick1], axis=0), k_band, NT), NEG)
    m = jnp.max(s, axis=-1, keepdims=True)
    p = jnp.exp(s - m)
    den = jnp.sum(p, axis=-1, keepdims=True)
    acc = _bdot(p, v_band, NN) * (1.0 / den)
    lse_rows = m + jnp.log(den)
    o = pick0 * acc[:ATT_BLK] + pick1 * acc[ATT_BLK:]
    lse = pick0 * lse_rows[:ATT_BLK] + pick1 * lse_rows[ATT_BLK:]
    return o, lse


def _head_picks():
    lane = lax.broadcasted_iota(jnp.int32, (1, 2 * HEAD), 1)
    return (lane < HEAD).astype(F32), (lane >= HEAD).astype(F32)


def _attn_band(kc, vc, kp, vp, has_prev):
    n_keys = ATT_BLK if kp is None else 2 * ATT_BLK
    qi = lax.broadcasted_iota(jnp.int32, (2 * ATT_BLK, n_keys), 0) & (ATT_BLK - 1)
    kj = lax.broadcasted_iota(jnp.int32, (2 * ATT_BLK, n_keys), 1)
    if kp is None:
        return kc, vc, qi >= kj
    in_prev = jnp.logical_and(jnp.logical_and(kj < ATT_BLK, kj >= qi), has_prev)
    mask = jnp.logical_or(in_prev, jnp.logical_and(kj >= ATT_BLK, qi >= kj - ATT_BLK))
    return jnp.concatenate([kp, kc], axis=0), jnp.concatenate([vp, vc], axis=0), mask


def _attn_config(b):
    r = DILATIONS[b]
    return r, ATT_BLK * r, (512 if r == 1 else 128), BRANCH_BLOCKS[b] > 1


def _for_residues(r, fn):
    if r <= 4:
        for rho in range(r):
            fn(rho)
    else:
        def step(t, carry):
            for u in range(4):
                fn(4 * t + u)
            return carry

        lax.fori_loop(0, r // 4, step, 0)


def _strided_rows(start, r):
    if r > 1:
        return pl.ds(start, ATT_BLK, stride=r)
    return pl.ds(start if isinstance(start, int) else pl.multiple_of(start, ATT_BLK), ATT_BLK)


def _attention_fwd(qn, kn, vn, b):
    r, rows, lanes, with_prev = _attn_config(b)
    cur = pl.BlockSpec((rows, lanes), lambda g, n: (n, g))
    prev = pl.BlockSpec((rows, lanes), lambda g, n: (jnp.maximum(n - 1, 0), g))

    def body(*refs):
        ins, (o_ref, l_ref) = refs[:-2], refs[-2:]
        has_prev = pl.program_id(1) > 0

        def one(rho):
            sub = _strided_rows(rho, r)
            for pair in range(lanes // 128):
                sl = pl.ds(pair * 128, 128)
                args = [ref[sub, sl] for ref in ins] + ([has_prev] if with_prev else [])
                o_ref[sub, sl], l_ref[sub, sl] = _attn_pair(*args)

        _for_residues(r, one)

    operands = (qn, kn, vn, kn, vn) if with_prev else (qn, kn, vn)
    return pl.pallas_call(
        body, name="attn_fwd_%d" % r, grid=(D_ATTN // lanes, SEQ // rows),
        in_specs=[cur, cur, cur] + ([prev, prev] if with_prev else []), out_specs=[cur, cur],
        out_shape=[jax.ShapeDtypeStruct((SEQ, D_ATTN), F32)] * 2,
        compiler_params=_params(("parallel", "parallel")),
    )(*operands)


def _attn_pair_bwd(q, kc, vc, kp, vp, o, lse, do, dl, has_prev):
    pick0, pick1 = _head_picks()
    lane = lax.broadcasted_iota(jnp.int32, (1, 2 * HEAD), 1)
    k_band, v_band, mask = _attn_band(kc, vc, kp, vp, has_prev)
    q2 = jnp.concatenate([q * pick0, q * pick1], axis=0)
    do2 = jnp.concatenate([do * pick0, do * pick1], axis=0)
    lse2 = jnp.concatenate([jnp.sum(lse * (lane == 0).astype(F32), axis=-1, keepdims=True),
                            jnp.sum(lse * (lane == HEAD).astype(F32), axis=-1, keepdims=True)], axis=0)
    base = jnp.sum(jnp.concatenate([dl * pick0, dl * pick1], axis=0) - do2 * jnp.concatenate([o, o], axis=0),
                   axis=-1, keepdims=True)
    p = jnp.exp(jnp.where(mask, _bdot(q2, k_band, NT), NEG) - lse2)
    ds = p * (_bdot(do2, v_band, NT) + base)
    dq2 = _bdot(ds, k_band, NN)
    dq = pick0 * dq2[:ATT_BLK] + pick1 * dq2[ATT_BLK:]
    dk, dv = _bdot(ds, q2, TN), _bdot(p, do2, TN)
    if kp is None:
        return dq, dk, dv
    return dq, dk[ATT_BLK:], dv[ATT_BLK:], dk[:ATT_BLK], dv[:ATT_BLK]


def _attention_bwd(qn, kn, vn, o, lse, do, dl, b):
    r, rows, lanes, with_prev = _attn_config(b)
    cur = pl.BlockSpec((rows, lanes), lambda g, n: (n, g))
    prev = pl.BlockSpec((rows, lanes), lambda g, n: (jnp.maximum(n - 1, 0), g))
    whole = pl.BlockSpec((SEQ, lanes), lambda g, n: (0, g))
    n_in = 5 if with_prev else 3

    def body(*refs):
        ins, (o_ref, l_ref, do_ref, dl_ref, dq_ref, dk_ref, dv_ref) = refs[:n_in], refs[n_in:]
        n = pl.program_id(1)

        @pl.when(n == 0)
        def _():
            dk_ref[...] = jnp.zeros_like(dk_ref)
            dv_ref[...] = jnp.zeros_like(dv_ref)

        def one(rho):
            sub = _strided_rows(rho, r)
            sub_c = _strided_rows(n * rows + rho, r)
            sub_p = _strided_rows(jnp.maximum(n - 1, 0) * rows + rho, r)
            for pair in range(lanes // 128):
                sl = pl.ds(pair * 128, 128)
                vals = [ref[sub, sl] for ref in ins] + ([] if with_prev else [None, None])
                grads = _attn_pair_bwd(*vals, o_ref[sub, sl], l_ref[sub, sl], do_ref[sub, sl], dl_ref[sub, sl], n > 0)
                dq_ref[sub, sl] = grads[0]
                dk_ref[sub_c, sl] += grads[1]
                dv_ref[sub_c, sl] += grads[2]
                if with_prev:
                    dk_ref[sub_p, sl] += grads[3]
                    dv_ref[sub_p, sl] += grads[4]

        _for_residues(r, one)

    operands = (qn, kn, vn, kn, vn) if with_prev else (qn, kn, vn)
    return pl.pallas_call(
        body, name="attn_bwd_%d" % r, grid=(D_ATTN // lanes, SEQ // rows),
        in_specs=[cur, cur, cur] + ([prev, prev] if with_prev else []) + [cur] * 4, out_specs=[cur, whole, whole],
        out_shape=[jax.ShapeDtypeStruct((SEQ, D_ATTN), F32)] * 3,
        compiler_params=_params(("parallel", "arbitrary")),
    )(*operands, o, lse, do, dl)


CONV_COLS = 256
XBC_BLOCK0 = (3 * D_ATTN + D_SSM) // CONV_COLS


def _shift_rows(x, s):
    n = x.shape[0]
    t = lax.broadcasted_iota(jnp.int32, x.shape, 0)
    if s >= 0:
        return jnp.where(t >= s, pltpu.roll(x, s, 0), 0.0)
    return jnp.where(t < n + s, pltpu.roll(x, n + s, 0), 0.0)


def _conv_pre(x, w_ref, b_ref):
    delayed = [_shift_rows(x, 3 - k) for k in range(3)]
    pre = b_ref[...] + w_ref[3:4, :] * x
    for k in range(3):
        pre = pre + w_ref[k:k + 1, :] * delayed[k]
    return pre, delayed


def _conv_fwd(proj, conv_w, conv_b):
    cols = conv_w.shape[1]

    def body(x_ref, w_ref, b_ref, o_ref):
        pre, _ = _conv_pre(x_ref[...], w_ref, b_ref)
        o_ref[...] = pre * jax.nn.sigmoid(pre)

    blk = pl.BlockSpec((SEQ, CONV_COLS), lambda j: (0, j))
    return pl.pallas_call(
        body, name="conv_fwd", grid=(cols // CONV_COLS,),
        in_specs=[pl.BlockSpec((SEQ, CONV_COLS), lambda j: (0, XBC_BLOCK0 + j)),
                  pl.BlockSpec((4, CONV_COLS), lambda j: (0, j)), pl.BlockSpec((1, CONV_COLS), lambda j: (0, j))],
        out_specs=blk, out_shape=jax.ShapeDtypeStruct((SEQ, cols), F32),
        compiler_params=_params(("parallel",)),
    )(proj, conv_w, conv_b)


def _conv_bwd(proj, conv_w, conv_b, dxs, db, dc):
    cols = conv_w.shape[1]
    x_blocks, b_blocks = dxs.shape[1] // CONV_COLS, db.shape[1] // CONV_COLS

    def body(x_ref, w_ref, b_ref, dxs_ref, db_ref_in, dc_ref_in, dx_ref, dw_ref, db_ref):
        j = pl.program_id(0)
        dy = jnp.where(j < x_blocks, dxs_ref[...], jnp.where(j < x_blocks + b_blocks, db_ref_in[...], dc_ref_in[...]))
        x = x_ref[...]
        pre, delayed = _conv_pre(x, w_ref, b_ref)
        sg = jax.nn.sigmoid(pre)
        dpre = dy * (sg * (1.0 + pre * (1.0 - sg)))
        db_ref[...] = jnp.sum(dpre, axis=0, keepdims=True)
        dx = w_ref[3:4, :] * dpre
        dw_ref[3:4, :] = jnp.sum(dpre * x, axis=0, keepdims=True)
        for k in range(3):
            dx = dx + w_ref[k:k + 1, :] * _shift_rows(dpre, k - 3)
            dw_ref[k:k + 1, :] = jnp.sum(dpre * delayed[k], axis=0, keepdims=True)
        dw_ref[4:8, :] = jnp.zeros((4, CONV_COLS), F32)
        dx_ref[...] = dx.astype(dx_ref.dtype)

    blk = pl.BlockSpec((SEQ, CONV_COLS), lambda j: (0, j))
    parts = [pl.BlockSpec((SEQ, CONV_COLS), lambda j: (0, jnp.minimum(j, x_blocks - 1))),
             pl.BlockSpec((SEQ, CONV_COLS), lambda j: (0, jnp.clip(j - x_blocks, 0, b_blocks - 1))),
             pl.BlockSpec((SEQ, CONV_COLS), lambda j: (0, jnp.clip(j - x_blocks - b_blocks, 0, b_blocks - 1)))]
    return pl.pallas_call(
        body, name="conv_bwd", grid=(cols // CONV_COLS,),
        in_specs=[pl.BlockSpec((SEQ, CONV_COLS), lambda j: (0, XBC_BLOCK0 + j)),
                  pl.BlockSpec((4, CONV_COLS), lambda j: (0, j)), pl.BlockSpec((1, CONV_COLS), lambda j: (0, j))] + parts,
        out_specs=[blk, pl.BlockSpec((8, CONV_COLS), lambda j: (0, j)), pl.BlockSpec((1, CONV_COLS), lambda j: (0, j))],
        out_shape=[jax.ShapeDtypeStruct((SEQ, cols), BF16), jax.ShapeDtypeStruct((8, cols), F32),
                   jax.ShapeDtypeStruct((1, cols), F32)],
        compiler_params=_params(("parallel",)),
    )(proj, conv_w, conv_b, dxs, db, dc)


HEADS_PER_GROUP = 4


GROUP_WIDTH = HEADS_PER_GROUP * HEAD


def _ssd_chunk(x, bm, cm, dtr, bias, alog, dsk, h):
    row = lax.broadcasted_iota(jnp.int32, (CHUNK, CHUNK), 0)
    col = lax.broadcasted_iota(jnp.int32, (CHUNK, CHUNK), 1)
    causal = row >= col
    z = dtr + bias
    dt = jnp.maximum(z, 0.0) + jnp.log(1.0 + jnp.exp(-jnp.abs(z)))
    acs = _fdot(causal.astype(F32), dt * -jnp.exp(alog), NN)
    acs_t, dt_t = acs.T, dt.T
    cb = _bdot(cm, bm, NT)
    lane = lax.broadcasted_iota(jnp.int32, (1, CHUNK), 1)
    sub = lax.broadcasted_iota(jnp.int32, (CHUNK, 1), 0)
    wide = lax.broadcasted_iota(jnp.int32, (1, GROUP_WIDTH), 1) // HEAD
    tall = lax.broadcasted_iota(jnp.int32, (GROUP_WIDTH, 1), 0) // HEAD
    acs_last = jnp.sum(acs * (sub == CHUNK - 1).astype(F32), axis=0, keepdims=True)
    to_lanes = (lax.broadcasted_iota(jnp.int32, (CHUNK, GROUP_WIDTH), 0)
                == lax.broadcasted_iota(jnp.int32, (CHUNK, GROUP_WIDTH), 1) // HEAD).astype(F32)
    grow = _fdot(jnp.exp(acs), to_lanes, NN)
    keep = _fdot(jnp.exp(acs_last - acs) * dt, to_lanes, NN)
    w_parts, x_parts, skip, carry = [], [], 0.0, 0.0
    for j in range(HEADS_PER_GROUP):
        on_lane, on_sub = (lane == j).astype(F32), (sub == j).astype(F32)
        acs_c = jnp.sum(acs * on_lane, axis=1, keepdims=True)
        acs_r = jnp.sum(acs_t * on_sub, axis=0, keepdims=True)
        dt_r = jnp.sum(dt_t * on_sub, axis=0, keepdims=True)
        w_parts.append(cb * jnp.exp(jnp.where(causal, acs_c - acs_r, NEG)) * dt_r)
        x_parts.append(x * (wide == j).astype(F32))
        skip = skip + jnp.sum(dsk * on_lane, axis=1, keepdims=True) * (wide == j).astype(F32)
        carry = carry + jnp.sum(jnp.exp(acs_last) * on_lane, axis=1, keepdims=True) * (tall == j).astype(F32)
    y_diag = _bdot(jnp.concatenate(w_parts, axis=1), jnp.concatenate(x_parts, axis=0), NN)
    y = y_diag + _bdot(cm, h, NT) * grow + skip * x
    return y, h * carry + _bdot(x * keep, bm, TN)


GROUPS_PER_STEP = 2
SSD_STEPS = N_GROUPS // GROUPS_PER_STEP


def _ssd_specs(reverse):
    n_chunks = SEQ // CHUNK
    c_of = (lambda c: n_chunks - 1 - c) if reverse else (lambda c: c)
    x_w, n_w, dt_w = GROUPS_PER_STEP * GROUP_WIDTH, GROUPS_PER_STEP * N_STATE, GROUPS_PER_STEP * 128
    x_spec = pl.BlockSpec((CHUNK, x_w), lambda g, c: (c_of(c), g))
    b_spec = pl.BlockSpec((CHUNK, n_w), lambda g, c: (c_of(c), D_SSM // n_w + g))
    c_spec = pl.BlockSpec((CHUNK, n_w), lambda g, c: (c_of(c), (D_SSM + N_GROUPS * N_STATE) // n_w + g))
    dt_spec = pl.BlockSpec((CHUNK, dt_w), lambda g, c: (c_of(c), g))
    vec_spec = pl.BlockSpec((1, dt_w), lambda g, c: (0, g))
    h_spec = pl.BlockSpec((None, GROUPS_PER_STEP, GROUP_WIDTH, N_STATE), lambda g, c: (c_of(c), g, 0, 0))
    return x_spec, b_spec, c_spec, dt_spec, vec_spec, h_spec


def _group_slices(u):
    return pl.ds(u * GROUP_WIDTH, GROUP_WIDTH), pl.ds(u * N_STATE, N_STATE), pl.ds(u * 128, 128)


def _ssd_gated_chunk(x, bm, cm, dtr, bias, alog, dsk, h, z, g_out):
    y, h_new = _ssd_chunk(x, bm, cm, dtr, bias, alog, dsk, h)
    return _gate_fn(y, z, g_out)[0], h_new


def _ssd_gate_specs(reverse):
    x_spec = _ssd_specs(reverse)[0]
    z_block0 = 3 * D_ATTN // x_spec.block_shape[1]
    z_spec = pl.BlockSpec(x_spec.block_shape, lambda g, c: (x_spec.index_map(g, c)[0], z_block0 + g))
    return z_spec, pl.BlockSpec((1, x_spec.block_shape[1]), lambda g, c: (0, g))


def _ssd_fwd(xbc, dt_raw, bias, alog, dsk, proj, g_out):
    x_spec, b_spec, c_spec, dt_spec, vec_spec, h_spec = _ssd_specs(False)
    z_spec, g_spec = _ssd_gate_specs(False)

    def body(x_ref, b_ref, c_ref, dt_ref, bias_ref, alog_ref, dsk_ref, z_ref, g_ref, ssm_ref, hin_ref, h_scr):
        @pl.when(pl.program_id(1) == 0)
        def _():
            h_scr[...] = jnp.zeros_like(h_scr)

        for u in range(GROUPS_PER_STEP):
            xs, ns, ds = _group_slices(u)
            h = h_scr[u]
            hin_ref[u] = h
            ssm, h_scr[u] = _ssd_gated_chunk(x_ref[:, xs], b_ref[:, ns], c_ref[:, ns], dt_ref[:, ds], bias_ref[:, ds],
                                             alog_ref[:, ds], dsk_ref[:, ds], h, z_ref[:, xs], g_ref[:, xs])
            ssm_ref[:, xs] = ssm.astype(ssm_ref.dtype)

    return pl.pallas_call(
        body, name="ssd_fwd", grid=(SSD_STEPS, SEQ // CHUNK),
        in_specs=[x_spec, b_spec, c_spec, dt_spec, vec_spec, vec_spec, vec_spec, z_spec, g_spec],
        out_specs=[x_spec, h_spec],
        out_shape=[jax.ShapeDtypeStruct((SEQ, D_SSM), BF16),
                   jax.ShapeDtypeStruct((SEQ // CHUNK, N_GROUPS, GROUP_WIDTH, N_STATE), F32)],
        scratch_shapes=[pltpu.VMEM((GROUPS_PER_STEP, GROUP_WIDTH, N_STATE), F32)],
        compiler_params=_params(("parallel", "arbitrary")),
    )(xbc, xbc, xbc, dt_raw, bias, alog, dsk, proj, g_out)


def _ssd_bwd(xbc, dt_raw, bias, alog, dsk, h_in, proj, g_out, dmix):
    x_spec, b_spec, c_spec, dt_spec, vec_spec, h_spec = _ssd_specs(True)
    z_spec, g_spec = _ssd_gate_specs(True)
    ct_block0 = D_ATTN // x_spec.block_shape[1]
    ct_spec = pl.BlockSpec(x_spec.block_shape, lambda g, c: (x_spec.index_map(g, c)[0], ct_block0 + g))

    def body(x_ref, b_ref, c_ref, dt_ref, bias_ref, alog_ref, dsk_ref, hin_ref, z_ref, g_ref, ct_ref,
             dx_ref, db_ref, dc_ref, ddt_ref, dbias_ref, dalog_ref, ddsk_ref, dz_ref, dg_ref, dh_scr):
        first = pl.program_id(1) == 0

        @pl.when(first)
        def _():
            dh_scr[...] = jnp.zeros_like(dh_scr)

        for u in range(GROUPS_PER_STEP):
            xs, ns, ds = _group_slices(u)
            _, pullback = jax.vjp(_ssd_gated_chunk, x_ref[:, xs], b_ref[:, ns], c_ref[:, ns], dt_ref[:, ds], bias_ref[:, ds],
                                  alog_ref[:, ds], dsk_ref[:, ds], hin_ref[u], z_ref[:, xs], g_ref[:, xs])
            g = pullback((ct_ref[:, xs], dh_scr[u]))
            dx_ref[:, xs], db_ref[:, ns], dc_ref[:, ns] = g[0], g[1], g[2]
            ddt_ref[:, ds] = g[3].astype(ddt_ref.dtype)
            dh_scr[u] = g[7]
            dz_ref[:, xs] = g[8].astype(dz_ref.dtype)
            sums = ((dbias_ref, g[4], ds), (dalog_ref, g[5], ds), (ddsk_ref, g[6], ds),
                    (dg_ref, jnp.sum(g[9], axis=0, keepdims=True), xs))
            for o_ref, val, lanes in sums:
                @pl.when(first)
                def _(o_ref=o_ref, val=val, lanes=lanes):
                    o_ref[:, lanes] = val

                @pl.when(jnp.logical_not(first))
                def _(o_ref=o_ref, val=val, lanes=lanes):
                    o_ref[:, lanes] += val

    n_chunks = SEQ // CHUNK
    out_b = pl.BlockSpec((CHUNK, GROUPS_PER_STEP * N_STATE), lambda g, c: (n_chunks - 1 - c, g))
    return pl.pallas_call(
        body, name="ssd_bwd", grid=(SSD_STEPS, n_chunks),
        in_specs=[x_spec, b_spec, c_spec, dt_spec, vec_spec, vec_spec, vec_spec, h_spec, z_spec, g_spec, ct_spec],
        out_specs=[x_spec, out_b, out_b, dt_spec, vec_spec, vec_spec, vec_spec, x_spec, g_spec],
        out_shape=[jax.ShapeDtypeStruct((SEQ, D_SSM), F32), jax.ShapeDtypeStruct((SEQ, N_GROUPS * N_STATE), F32),
                   jax.ShapeDtypeStruct((SEQ, N_GROUPS * N_STATE), F32), jax.ShapeDtypeStruct((SEQ, DT_PAD), BF16),
                   jax.ShapeDtypeStruct((1, DT_PAD), F32), jax.ShapeDtypeStruct((1, DT_PAD), F32),
                   jax.ShapeDtypeStruct((1, DT_PAD), F32), jax.ShapeDtypeStruct((SEQ, D_SSM), BF16),
                   jax.ShapeDtypeStruct((1, D_SSM), F32)],
        scratch_shapes=[pltpu.VMEM((GROUPS_PER_STEP, GROUP_WIDTH, N_STATE), F32)],
        compiler_params=_params(("parallel", "arbitrary")),
    )(xbc, xbc, xbc, dt_raw, bias, alog, dsk, h_in, proj, g_out, dmix)


CROSS_HEAD = 128
CROSS_ROWS = 512


def _cross_head(q, k, v, gq, gk):
    qn = _rms(q, gq) * (CROSS_HEAD ** -0.5)
    kn = _rms(k, gk)
    s = _bdot(qn, kn, NT)
    p = jnp.exp(s - lax.stop_gradient(jnp.max(s, axis=-1, keepdims=True)))
    return _bdot(p, v, NN) * (1.0 / jnp.sum(p, axis=-1, keepdims=True))


def _cross_specs():
    q_spec = pl.BlockSpec((CROSS_ROWS, CROSS_HEAD), lambda h, i: (i, h))
    k_spec = pl.BlockSpec((N_MEM, CROSS_HEAD), lambda h, i: (0, h))
    v_spec = pl.BlockSpec((N_MEM, CROSS_HEAD), lambda h, i: (0, 4 + h))
    g_spec = pl.BlockSpec((1, CROSS_HEAD), lambda h, i: (0, 0))
    return q_spec, k_spec, v_spec, g_spec


def _cross_fwd(qc, kv, gq, gk):
    q_spec, k_spec, v_spec, g_spec = _cross_specs()

    def body(q_ref, k_ref, v_ref, gq_ref, gk_ref, o_ref):
        o_ref[...] = _cross_head(q_ref[...], k_ref[...], v_ref[...], gq_ref[...], gk_ref[...]).astype(o_ref.dtype)

    return pl.pallas_call(
        body, name="cross_fwd", grid=(4, SEQ // CROSS_ROWS),
        in_specs=[q_spec, k_spec, v_spec, g_spec, g_spec], out_specs=q_spec,
        out_shape=jax.ShapeDtypeStruct((SEQ, D_CROSS), BF16),
        compiler_params=_params(("parallel", "parallel")),
    )(qc, kv, kv, gq, gk)


def _cross_bwd(qc, kv, gq, gk, do):
    q_spec, k_spec, v_spec, g_spec = _cross_specs()

    def body(q_ref, k_ref, v_ref, gq_ref, gk_ref, do_ref, dq_ref, dk_ref, dv_ref, dgq_ref, dgk_ref):
        _, pullback = jax.vjp(_cross_head, q_ref[...], k_ref[...], v_ref[...], gq_ref[...], gk_ref[...])
        dq, dk, dv, dgq, dgk = pullback(do_ref[...].astype(F32))
        dq_ref[...] = dq.astype(dq_ref.dtype)
        row0 = pl.program_id(1) == 0
        all0 = jnp.logical_and(row0, pl.program_id(0) == 0)
        for o_ref, val, init in ((dk_ref, dk, row0), (dv_ref, dv, row0), (dgq_ref, dgq, all0), (dgk_ref, dgk, all0)):
            @pl.when(init)
            def _(o_ref=o_ref, val=val):
                o_ref[...] = val

            @pl.when(jnp.logical_not(init))
            def _(o_ref=o_ref, val=val):
                o_ref[...] += val

    return pl.pallas_call(
        body, name="cross_bwd", grid=(4, SEQ // CROSS_ROWS),
        in_specs=[q_spec, k_spec, v_spec, g_spec, g_spec, q_spec],
        out_specs=[q_spec, k_spec, k_spec, g_spec, g_spec],
        out_shape=[jax.ShapeDtypeStruct((SEQ, D_CROSS), BF16), jax.ShapeDtypeStruct((N_MEM, D_CROSS), F32),
                   jax.ShapeDtypeStruct((N_MEM, D_CROSS), F32), jax.ShapeDtypeStruct((1, CROSS_HEAD), F32),
                   jax.ShapeDtypeStruct((1, CROSS_HEAD), F32)],
        compiler_params=_params(("arbitrary", "arbitrary")),
    )(qc, kv, kv, gq, gk, do)


def _loss_epilogue(acc, residual, target):
    err = acc + residual - target
    dy = err * (1.0 / D_MODEL)
    part = jnp.sum(jnp.sum(err * err, axis=1, keepdims=True), axis=0, keepdims=True) * (0.5 / D_MODEL)
    return dy, dy, part


def _pad_heads(v):
    return jnp.pad(v.reshape(N_GROUPS, HEADS_PER_GROUP), ((0, 0), (0, 128 - HEADS_PER_GROUP))).reshape(1, DT_PAD)


def _unpad_heads(v):
    return v.reshape(v.shape[0], N_GROUPS, 128)[:, :, :HEADS_PER_GROUP].reshape(v.shape[0], N_DT)


def _rope_tables(positions):
    half = ROT // 2
    inv_freq = ROPE_THETA ** (-2.0 * jnp.arange(half, dtype=F32) / ROT)
    ang = positions.reshape(SEQ, 1).astype(F32) * inv_freq
    cos, sin = jnp.cos(ang), jnp.sin(ang)
    ones, zeros = jnp.ones((SEQ, HEAD - ROT), F32), jnp.zeros((SEQ, HEAD - ROT), F32)
    cos_h = jnp.concatenate([cos, cos, ones], axis=1)
    sin_h = jnp.concatenate([-sin, sin, zeros], axis=1)
    return jnp.tile(cos_h, (1, 2)), jnp.tile(sin_h, (1, 2))


def _add_res(acc, res):
    return (acc + res,)


def _settle(grads, *after):
    if hasattr(grads, "settle"):
        grads.settle(*after)


def _take_token(grads):
    token = getattr(grads, "token", None)
    if token is None:
        return ()
    grads.token = None
    return (token,)


def _local_step(x, mem, positions, target, p, w, more_weights=None, grads=None, h=None):
    grads = {} if grads is None else grads
    w = dict(w)
    cos, sin = _rope_tables(positions)
    gq2, gk2 = jnp.tile(p["g_q"], (1, 2)), jnp.tile(p["g_k"], (1, 2))
    bias, alog, dsk = _pad_heads(p["dt_bias"]), _pad_heads(p["a_log"]), _pad_heads(p["d_skip"])
    norm_out = [(D_MODEL, BF16, D_MODEL, 0, False)]

    if h is None:
        h = _rowwise(_norm_fn, [_full(x)], [_full(p["g_mix"])], norm_out, name="norm_in")[0]
    proj = _matmul(h, w["w_in"], mode="nn", name="in_proj", outs=[F32], n_cols=D_MAIN)
    dt_raw = _matmul(h, w["w_dt"], mode="nn", name="dt_proj", outs=[F32])
    pairs = D_ATTN // 128
    qk_rows = [(proj, 128, 0, True), (proj, 128, pairs, True), (proj, 128, 2 * pairs, True), _full(cos), _full(sin)]
    qk_vecs = [_full(gq2), _full(gk2)]
    qn, kn, vn = _rowwise(_qk_fn, qk_rows, qk_vecs, [(D_ATTN, F32, 128, 0, True)] * 3, name="qk_prep", groups=8, tr=1024)
    branches = [_attention_fwd(qn, kn, vn, b) for b in range(3)]
    merge_rows = [_full(o) for o, _ in branches] + [_full(lse) for _, lse in branches]
    attn = _rowwise(_merge_fn, merge_rows, [_full(p["g_attn_out"])], [(D_ATTN, BF16, D_ATTN, 0, False)], name="attn_merge")[0]
    xbc = _conv_fwd(proj, p["conv_w"], p["conv_b"])
    ssm, h_in = _ssd_fwd(xbc, dt_raw, bias, alog, dsk, proj, p["g_ssm_out"])
    mix = jnp.concatenate([attn, ssm], axis=1)
    if more_weights is not None:
        w.update(more_weights("mixer_done", mix))
    x1 = _matmul(mix, w["w_out"], mode="nn", name="out_proj", outs=[F32], extra=(x,), epilogue=_add_res)
    hc = _rowwise(_norm_fn, [_full(x1)], [_full(p["g_cross"])], norm_out, name="norm_cross")[0]
    memh = _rowwise(_norm_fn, [_full(mem)], [_full(p["g_mem"])], norm_out, name="norm_mem", n_rows=N_MEM, tr=N_MEM)[0]
    qc = _matmul(hc, w["w_cq"], mode="nn", name="cq_proj", outs=[F32])
    if more_weights is not None:
        w.update(more_weights("cross_started", qc))
    kv = _matmul(memh, w["w_ckv"], mode="nn", name="ckv_proj", outs=[F32])
    oc = _cross_fwd(qc, kv, p["g_cq"], p["g_ck"])
    x2 = _matmul(oc, w["w_co"], mode="nn", name="co_proj", outs=[F32], extra=(x1,), epilogue=_add_res)
    hm = _rowwise(_norm_fn, [_full(x2)], [_full(p["g_mlp"])], norm_out, name="norm_mlp")[0]
    if more_weights is not None:
        w.update(more_weights("cross_done", hm))
    u, act = _matmul(hm, w["w_up"], mode="nn", name="up_proj", outs=[F32, BF16],
                     epilogue=lambda acc: (acc, jnp.square(jnp.maximum(acc, 0.0))))
    dy, dyb, loss_tiles = _matmul(act, w["w_down"], mode="nn", name="down_proj", outs=[F32, BF16], extra=(x2, target),
                                  epilogue=_loss_epilogue, tile_sums=1)
    loss = jnp.sum(loss_tiles).reshape(1, 1)

    grads["w_down"] = _matmul(act, dyb, mode="tn", name="dw_down", outs=[BF16], after=_take_token(grads))
    du = _matmul(dyb, w["w_down"], mode="nt", name="d_act", outs=[BF16], extra=(u,), after=_take_token(grads),
                 epilogue=lambda acc, uu: (acc * (2.0 * jnp.maximum(uu, 0.0)),))
    _settle(grads, du)
    grads["w_up"] = _matmul(hm, du, mode="tn", name="dw_up", outs=[BF16], col_shards=4, after=_take_token(grads))
    dhm = _matmul(du, w["w_up"], mode="nt", name="d_hm", outs=[F32], after=_take_token(grads), tk=4096)
    _settle(grads, dhm)
    dx2, grads["g_mlp"] = _rowwise_vjp(
        _norm_fn, [_full(x2)], [_full(p["g_mlp"])], [[_full(dhm)]],
        [(0, D_MODEL, F32, D_MODEL, 0, False, _full(dy))], [(0, D_MODEL, D_MODEL, 0, False)], name="norm_mlp_bwd")
    grads["w_co"] = _matmul(oc, dx2, mode="tn", name="dw_co", outs=[BF16], col_shards=4, after=_take_token(grads))
    doc = _matmul(dx2, w["w_co"], mode="nt", name="d_oc", outs=[BF16])
    dqc, dkc, dvc, grads["g_cq"], grads["g_ck"] = _cross_bwd(qc, kv, p["g_cq"], p["g_ck"], doc)
    grads["w_cq"] = _matmul(hc, dqc, mode="tn", name="dw_cq", outs=[BF16])
    dhc = _matmul(dqc, w["w_cq"], mode="nt", name="d_hc", outs=[F32])
    dkv = jnp.concatenate([dkc, dvc], axis=1)
    grads["w_ckv"] = _matmul(memh, dkv, mode="tn", name="dw_ckv", outs=[BF16])
    dmemh = _matmul(dkv, w["w_ckv"], mode="nt", name="d_memh", outs=[F32])
    grads["g_mem"] = _rowwise_vjp(_norm_fn, [_full(mem)], [_full(p["g_mem"])], [[_full(dmemh)]], [],
                                  [(0, D_MODEL, D_MODEL, 0, False)], name="norm_mem_bwd", n_rows=N_MEM, tr=N_MEM)[0]
    dx1, grads["g_cross"] = _rowwise_vjp(
        _norm_fn, [_full(x1)], [_full(p["g_cross"])], [[_full(dhc)]],
        [(0, D_MODEL, F32, D_MODEL, 0, False, _full(dx2))], [(0, D_MODEL, D_MODEL, 0, False)], name="norm_cross_bwd")
    grads["w_out"] = _matmul(mix, dx1, mode="tn", name="dw_out", outs=[BF16])
    dmix = _matmul(dx1, w["w_out"], mode="nt", name="d_mix", outs=[F32], after=_take_token(grads))
    _settle(grads, dmix)
    merge_grads = [(i, D_ATTN, F32, D_ATTN, 0, False, None) for i in range(6)]
    *dol, grads["g_attn_out"] = _rowwise_vjp(
        _merge_fn, merge_rows, [_full(p["g_attn_out"])], [[(dmix, D_ATTN, 0, False)]],
        merge_grads, [(0, D_ATTN, D_ATTN, 0, False)], name="attn_merge_bwd", tr=256, after=_take_token(grads))
    dqkv = [_attention_bwd(qn, kn, vn, *branches[b], dol[b], dol[3 + b], b) for b in range(3)]
    qk_cts = [[(dqkv[b][i], 128, 0, True) for b in range(3)] for i in range(3)]
    dq, dk, dv, dgq2, dgk2 = _rowwise_vjp(
        _qk_fn, qk_rows, qk_vecs, qk_cts, [(i, D_ATTN, BF16, 128, 0, True, None) for i in range(3)],
        [(0, 128, 128, 0, False), (1, 128, 128, 0, False)], name="qk_prep_bwd", groups=8, tr=512)
    grads["g_q"] = dgq2[:, :HEAD] + dgq2[:, HEAD:]
    grads["g_k"] = dgk2[:, :HEAD] + dgk2[:, HEAD:]
    dxs, db, dc, ddt, dbias, dalog, ddsk, dz, grads["g_ssm_out"] = _ssd_bwd(xbc, dt_raw, bias, alog, dsk, h_in, proj,
                                                                             p["g_ssm_out"], dmix)
    grads["dt_bias"], grads["a_log"], grads["d_skip"] = _unpad_heads(dbias), _unpad_heads(dalog), _unpad_heads(ddsk)
    dxbc_raw, dconv_w, grads["conv_b"] = _conv_bwd(proj, p["conv_w"], p["conv_b"], dxs, db, dc)
    grads["conv_w"] = dconv_w[:4]
    dproj = jnp.concatenate([dq, dk, dv, dz, dxbc_raw], axis=1)
    grads["w_main"] = _matmul(h, dproj, mode="tn", name="dw_main", outs=[BF16], out_cols=D_MAIN + N_DT)
    grads["w_dt"] = _matmul(h, ddt, mode="tn", name="dw_dt", outs=[BF16])
    dh = _matmul(dproj, w["w_in"], mode="nt", name="d_h_main", outs=[F32], after=_take_token(grads))
    dh = _matmul(ddt, w["w_dt"], mode="nt", name="d_h_dt", outs=[F32], extra=(dh,), epilogue=_add_res)
    grad_x, grads["g_mix"] = _rowwise_vjp(
        _norm_fn, [_full(x)], [_full(p["g_mix"])], [[_full(dh)]],
        [(0, D_MODEL, F32, D_MODEL, 0, False, _full(dx1))], [(0, D_MODEL, D_MODEL, 0, False)], name="norm_in_bwd")
    return loss, grad_x, grads


MATRICES = ("w_in", "w_out", "w_cq", "w_ckv", "w_co", "w_up", "w_down")
ROW_SHARDED = ("w_out", "w_cq", "w_ckv", "w_down")
N_CHIPS = 4
ANY = pl.BlockSpec(memory_space=pl.ANY)


def _place():
    return lax.axis_index("x"), lax.axis_index("y"), lax.axis_index("c")


def _other_chips(x, y):
    return [(1 - x, y), (x, 1 - y), (1 - x, 1 - y)]


def _remote(src, dst, send_sem, recv_sem, device):
    return pltpu.make_async_remote_copy(src_ref=src, dst_ref=dst, send_sem=send_sem, recv_sem=recv_sem,
                                        device_id=device, device_id_type=MESH)


def _gathered_shape(name, shard):
    rows, cols = shard.shape
    if name == "w_in":
        return (N_CHIPS, rows, cols)
    return (N_CHIPS * rows, cols) if name in ROW_SHARDED else (rows, N_CHIPS * cols)


def _shard_window(name, ref, rows, cols, chip, half):
    r0, nr = (0, rows) if half is None else (half * (rows // 2), rows // 2)
    if name == "w_in":
        return ref.at[chip, pl.ds(r0, nr), :]
    if name in ROW_SHARDED:
        return ref.at[pl.ds(chip * rows + r0, nr), :]
    return ref.at[pl.ds(r0, nr), pl.ds(pl.multiple_of(chip * cols, 128), cols)]


def _cast_into_gathered(w, name, chip, after=()):
    rows, cols = w.shape
    tr = _tile(rows, ROW_TILE)

    def body(chip_ref, w_ref, *rest):
        rest[-1][...] = w_ref[...].astype(BF16)

    if name == "w_in":
        out_spec = pl.BlockSpec((None, tr, cols), lambda i, chip_ref: (chip_ref[0], i, 0))
    elif name in ROW_SHARDED:
        out_spec = pl.BlockSpec((tr, cols), lambda i, chip_ref: (chip_ref[0] * (rows // tr) + i, 0))
    else:
        out_spec = pl.BlockSpec((tr, cols), lambda i, chip_ref: (i, chip_ref[0]))
    grid_spec = pltpu.PrefetchScalarGridSpec(
        num_scalar_prefetch=1, grid=(rows // tr,),
        in_specs=[pl.BlockSpec((tr, cols), lambda i, chip_ref: (i, 0))] + [pl.BlockSpec(memory_space=pl.ANY)] * len(after),
        out_specs=out_spec)
    return pl.pallas_call(body, name="cast_" + name, grid_spec=grid_spec,
                          out_shape=jax.ShapeDtypeStruct(_gathered_shape(name, w), BF16),
                          compiler_params=_params(("parallel",)))(chip.reshape(1).astype(jnp.int32), w, *after)


def _w_in_columns(arr, to_shards):
    rows, piece = D_MODEL, (D_MAIN + N_DT) // N_CHIPS
    tr = ROW_TILE

    def body(a_ref, o_ref):
        for j in range(N_CHIPS):
            if to_shards:
                o_ref[j] = a_ref[:, pl.ds(piece * j, piece)]
            else:
                o_ref[:, pl.ds(piece * j, piece)] = a_ref[j]

    pieces = pl.BlockSpec((N_CHIPS, tr, piece), lambda i: (0, i, 0))
    matrix = pl.BlockSpec((tr, N_CHIPS * piece), lambda i: (i, 0))
    out_dims = (N_CHIPS, rows, piece) if to_shards else (rows, N_CHIPS * piece)
    return pl.pallas_call(
        body, name="w_in_to_shards" if to_shards else "w_in_from_shards", grid=(rows // tr,),
        in_specs=[matrix if to_shards else pieces], out_specs=pieces if to_shards else matrix,
        out_shape=jax.ShapeDtypeStruct(out_dims, arr.dtype), compiler_params=_params(("parallel",)))(arr)


HBM = pl.BlockSpec(memory_space=pltpu.HBM)
SEM = pl.BlockSpec(memory_space=pltpu.SEMAPHORE)
EFFECT = pltpu.SideEffectType.DATAFLOW_SIDE_EFFECTING


def _split_start(name, bufs, plan, counts, after=()):
    n, n_g, n_after = len(bufs), len(counts), len(after)

    def body(*refs):
        ins, sems, token = refs[:n], refs[n + n_after:n + n_after + 2 * n_g], refs[-1]
        for g, copies in enumerate(plan(ins)):
            for i, (src, dst, device, _) in enumerate(copies):
                _remote(src, dst, sems[2 * g].at[i], sems[2 * g + 1].at[i], device).start()
        token[...] = jnp.zeros_like(token)

    sem_shapes = [pltpu.SemaphoreType.DMA((cnt,)) for cnt in counts for _ in range(2)]
    res = pl.pallas_call(
        body, name=name,
        out_shape=(*sem_shapes, *[pltpu.HBM(b.shape, b.dtype) for b in bufs], jax.ShapeDtypeStruct((8, 128), F32)),
        in_specs=(*(HBM,) * n, *(ANY,) * n_after),
        out_specs=(*(SEM,) * (2 * n_g), *(HBM,) * n, pl.BlockSpec(memory_space=pltpu.VMEM)),
        input_output_aliases={i: 2 * n_g + i for i in range(n)},
        compiler_params=pltpu.CompilerParams(has_side_effects=EFFECT),
    )(*[pltpu.with_memory_space_constraint(b, pltpu.HBM) for b in bufs], *after)
    sems = [(res[2 * g], res[2 * g + 1]) for g in range(n_g)]
    return sems, list(res[2 * n_g:2 * n_g + n]), res[-1]


def _split_wait(name, bufs, sems, plan, *after):
    n = len(bufs)

    def body(*refs):
        ins, send, recv = refs[:n], refs[n], refs[n + 1]
        (copies,) = plan(ins)
        for i, (src, _, device, landing) in enumerate(copies):
            cp = _remote(src, landing, send.at[i], recv.at[i], device)
            cp.wait_send()
            cp.wait_recv()

    res = pl.pallas_call(
        body, name=name, out_shape=tuple(pltpu.HBM(b.shape, b.dtype) for b in bufs),
        in_specs=(*(HBM,) * n, SEM, SEM, *(ANY,) * len(after)), out_specs=(HBM,) * n,
        input_output_aliases={i: i for i in range(n)},
        compiler_params=pltpu.CompilerParams(has_side_effects=EFFECT),
    )(*bufs, sems[0], sems[1], *after)
    return list(res)


def _ici_plan(names, shard_shapes):
    def plan(refs):
        x, y, c = _place()
        copies = []
        for ref, name in zip(refs, names):
            win = _shard_window(name, ref, *shard_shapes[name], 2 * x + y, c)
            for px, py in _other_chips(x, y):
                copies.append((win, win, (px, py, c), _shard_window(name, ref, *shard_shapes[name], 2 * px + py, c)))
        return [copies]
    return plan


def _pass_on_plan(names, shard_shapes):
    def plan(refs):
        x, y, c = _place()
        copies = []
        for ref, name in zip(refs, names):
            for px, py in _other_chips(x, y):
                win = _shard_window(name, ref, *shard_shapes[name], 2 * px + py, c)
                copies.append((win, win, (x, y, 1 - c), _shard_window(name, ref, *shard_shapes[name], 2 * px + py, 1 - c)))
        return [copies]
    return plan


def _swap_plan(n_pairs):
    def plan(refs):
        x, y, c = _place()
        return [[(src.at[:, 1 - c], dst, (x, y, 1 - c), dst) for src, dst in zip(refs[:n_pairs], refs[n_pairs:])]]
    return plan


def _share_plan(n_pairs):
    def plan(refs):
        x, y, c = _place()
        return [[(src, dst, (x, y, 1 - c), dst)] for src, dst in zip(refs[:n_pairs], refs[n_pairs:])]
    return plan


def _scatter_plan(n_pairs):
    def plan(refs):
        x, y, c = _place()
        copies = []
        for src, dst in zip(refs[:n_pairs], refs[n_pairs:]):
            for k, (px, py) in enumerate(_other_chips(x, y)):
                copies.append((src.at[2 * px + py], dst.at[k], (px, py, c), dst.at[k]))
        return [copies]
    return plan


def _sibling_swap(arrs, name):
    n = len(arrs)

    def body(*refs):
        ins, outs, send, recv = refs[:n], refs[n:2 * n], refs[2 * n], refs[2 * n + 1]
        x, y, c = _place()
        cps = [_remote(ins[w].at[:, 1 - c], outs[w], send.at[w], recv.at[w], (x, y, 1 - c)) for w in range(n)]
        for cp in cps:
            cp.start()
        for cp in cps:
            cp.wait()

    return pl.pallas_call(
        body, name=name, in_specs=[ANY] * n, out_specs=[ANY] * n,
        out_shape=[jax.ShapeDtypeStruct((a.shape[0],) + a.shape[2:], a.dtype) for a in arrs],
        scratch_shapes=[pltpu.SemaphoreType.DMA((n,))] * 2,
    )(*arrs)


def _small_allreduce(buf, name, after=()):
    rows = buf.shape[0]

    def body(x_ref, *rest):
        out_ref, all_ref, send_sems, recv_sems, local_sem = rest[len(after):]
        x, y, c = _place()
        me, sibling, chips = (x, y, c), (x, y, 1 - c), _other_chips(x, y)

        def block(px, py, pc):
            return all_ref.at[pl.ds((4 * px + 2 * py + pc) * rows, rows), :]

        def copy(k, blk, to, src=None):
            return _remote(block(*blk) if src is None else src, block(*blk), send_sems.at[k], recv_sems.at[k], to)

        own = pltpu.make_async_copy(x_ref, block(*me), local_sem)
        own.start()
        first = [copy(0, me, sibling, src=x_ref)] + [copy(1 + j, me, (*chip, c), src=x_ref) for j, chip in enumerate(chips)]
        for cp in first:
            cp.start()
        passed = [copy(4 + j, (*chip, c), sibling) for j, chip in enumerate(chips)]
        for j, chip in enumerate(chips):
            copy(1 + j, (*chip, c), me).wait_recv()
            passed[j].start()
        copy(0, sibling, me).wait_recv()
        for j, chip in enumerate(chips):
            copy(4 + j, (*chip, 1 - c), me).wait_recv()
        for cp in first + passed:
            cp.wait_send()
        own.wait()
        acc = all_ref[pl.ds(0, rows), :]
        for d in range(1, 8):
            acc = acc + all_ref[pl.ds(d * rows, rows), :]
        out_ref[...] = acc

    vmem = pl.BlockSpec(memory_space=pltpu.VMEM)
    return pl.pallas_call(
        body, name=name, in_specs=[vmem] + [ANY] * len(after), out_specs=vmem,
        out_shape=jax.ShapeDtypeStruct(buf.shape, F32),
        scratch_shapes=[pltpu.VMEM((8 * rows, 128), F32), pltpu.SemaphoreType.DMA((7,)), pltpu.SemaphoreType.DMA((7,)),
                        pltpu.SemaphoreType.DMA],
    )(buf, *after)


ROW_TILE = 256
BIG_ROW_TILE = 1024


def _add_halves(arr, recv, c, name):
    _, _, hr, cols = arr.shape
    tr = _tile(hr, BIG_ROW_TILE)

    def body(c_ref, a_ref, r_ref, o_ref):
        o_ref[...] = (a_ref[...].astype(F32) + r_ref[...].astype(F32)).astype(o_ref.dtype)

    piece = pl.BlockSpec((None, tr, cols), lambda j, i, c_ref: (j, i, 0))
    grid_spec = pltpu.PrefetchScalarGridSpec(
        num_scalar_prefetch=1, grid=(N_CHIPS, hr // tr),
        in_specs=[pl.BlockSpec((None, None, tr, cols), lambda j, i, c_ref: (j, c_ref[0], i, 0)), piece], out_specs=piece)
    return pl.pallas_call(body, name=name, grid_spec=grid_spec, out_shape=jax.ShapeDtypeStruct(recv.shape, BF16),
                          compiler_params=_params(("parallel", "parallel")))(c.reshape(1).astype(jnp.int32), arr, recv)


def _flip_slot(d):
    return jnp.where(d == 1, 1, jnp.where(d == 3, 2, 0))


def _sum_chips(p, q, chip, name):
    _, hr, cols = p.shape
    tr = _tile(hr, BIG_ROW_TILE)

    def body(chip_ref, p_ref, q_ref, o_ref):
        j = pl.program_id(1)
        term = jnp.where(j == chip_ref[0], p_ref[...].astype(F32), q_ref[...].astype(F32))

        @pl.when(j == 0)
        def _():
            o_ref[...] = term

        @pl.when(j != 0)
        def _():
            o_ref[...] += term

    grid_spec = pltpu.PrefetchScalarGridSpec(
        num_scalar_prefetch=1, grid=(hr // tr, N_CHIPS),
        in_specs=[pl.BlockSpec((None, tr, cols), lambda i, j, chip_ref: (chip_ref[0], i, 0)),
                  pl.BlockSpec((None, tr, cols), lambda i, j, chip_ref: (_flip_slot(j ^ chip_ref[0]), i, 0))],
        out_specs=pl.BlockSpec((tr, cols), lambda i, j, chip_ref: (i, 0)))
    return pl.pallas_call(body, name=name, grid_spec=grid_spec, out_shape=jax.ShapeDtypeStruct((hr, cols), F32),
                          compiler_params=_params(("parallel", "arbitrary")))(chip.reshape(1).astype(jnp.int32), p, q)


def _adamw_halves(w, g_own, g_other, m, v, c, name):
    rows, cols = w.shape
    tr = _tile(rows // 2, ROW_TILE)
    per_half = rows // 2 // tr

    def body(c_ref, w_ref, own_ref, other_ref, m_ref, v_ref, g_ref, d_ref, nm_ref, nv_ref):
        mine = (pl.program_id(0) // per_half) == c_ref[0]
        g_ = jnp.where(mine, own_ref[...], other_ref[...])
        g_ref[...] = g_
        d_ref[...], nm_ref[...], nv_ref[...] = _adamw_math(w_ref[...], g_, m_ref[...], v_ref[...])

    blk = pl.BlockSpec((tr, cols), lambda i, c_ref: (i, 0))
    own = pl.BlockSpec((tr, cols), lambda i, c_ref: (jnp.where(i // per_half == c_ref[0], i % per_half, 0), 0))
    other = pl.BlockSpec((tr, cols), lambda i, c_ref: (jnp.where(i // per_half == c_ref[0], 0, i % per_half), 0))
    grid_spec = pltpu.PrefetchScalarGridSpec(num_scalar_prefetch=1, grid=(rows // tr,),
                                             in_specs=[blk, own, other, blk, blk], out_specs=[blk] * 4)
    return pl.pallas_call(body, name=name, grid_spec=grid_spec, out_shape=[jax.ShapeDtypeStruct(w.shape, F32)] * 4,
                          compiler_params=_params(("parallel",)))(c.reshape(1).astype(jnp.int32), w, g_own, g_other, m, v)


W_IN_COLS = (D_MAIN + N_DT) // N_CHIPS
W_IN_MAIN = W_IN_COLS // 128 * 128
W_IN_TAIL = W_IN_COLS - 128
W_IN_PARTS = ((0, W_IN_MAIN), (W_IN_TAIL, 128))


def _cast_w_in_transposed(w_t, chip, after=()):
    def body(chip_ref, w_ref, *rest):
        for start, size in W_IN_PARTS:
            rest[-1][:, pl.ds(start, size)] = w_ref[pl.ds(start, size), :].T.astype(BF16)

    grid_spec = pltpu.PrefetchScalarGridSpec(
        num_scalar_prefetch=1, grid=(D_MODEL // ROW_TILE,),
        in_specs=[pl.BlockSpec((W_IN_COLS, ROW_TILE), lambda i, chip_ref: (0, i))] + [pl.BlockSpec(memory_space=pl.ANY)] * len(after),
        out_specs=pl.BlockSpec((None, ROW_TILE, W_IN_COLS), lambda i, chip_ref: (chip_ref[0], i, 0)))
    return pl.pallas_call(body, name="cast_w_in", grid_spec=grid_spec,
                          out_shape=jax.ShapeDtypeStruct((N_CHIPS, D_MODEL, W_IN_COLS), BF16),
                          compiler_params=_params(("parallel",)))(chip.reshape(1).astype(jnp.int32), w_t, *after)


def _adamw_w_in_transposed(w_t, g_own, g_other, m_t, v_t, c):
    per_half = D_MODEL // 2 // ROW_TILE

    def body(c_ref, w_ref, own_ref, other_ref, m_ref, v_ref, g_ref, d_ref, nm_ref, nv_ref):
        mine = (pl.program_id(0) // per_half) == c_ref[0]
        for start, size in W_IN_PARTS:
            cols, rows = pl.ds(start, size), pl.ds(start, size)
            g_ = jnp.where(mine, own_ref[:, cols], other_ref[:, cols]).T
            g_ref[rows, :] = g_
            d_ref[rows, :], nm_ref[rows, :], nv_ref[rows, :] = _adamw_math(w_ref[rows, :], g_, m_ref[rows, :], v_ref[rows, :])

    blk = pl.BlockSpec((W_IN_COLS, ROW_TILE), lambda i, c_ref: (0, i))
    own = pl.BlockSpec((ROW_TILE, W_IN_COLS), lambda i, c_ref: (jnp.where(i // per_half == c_ref[0], i % per_half, 0), 0))
    other = pl.BlockSpec((ROW_TILE, W_IN_COLS), lambda i, c_ref: (jnp.where(i // per_half == c_ref[0], 0, i % per_half), 0))
    grid_spec = pltpu.PrefetchScalarGridSpec(num_scalar_prefetch=1, grid=(D_MODEL // ROW_TILE,),
                                             in_specs=[blk, own, other, blk, blk], out_specs=[blk] * 4)
    return pl.pallas_call(body, name="adamw_w_in", grid_spec=grid_spec, out_shape=[jax.ShapeDtypeStruct(w_t.shape, F32)] * 4,
                          compiler_params=_params(("parallel",)))(c.reshape(1).astype(jnp.int32), w_t, g_own, g_other, m_t, v_t)


def _adamw_math(w, g, m, v):
    m_new = ADAM_B1 * m + (1.0 - ADAM_B1) * g
    v_new = ADAM_B2 * v + (1.0 - ADAM_B2) * (g * g)
    m_hat = m_new / (1.0 - ADAM_B1 ** ADAM_STEP)
    v_hat = v_new / (1.0 - ADAM_B2 ** ADAM_STEP)
    return -ADAM_LR * (m_hat / (jnp.sqrt(v_hat) + ADAM_EPS) + ADAM_WD * w), m_new, v_new


VECTORS = ("g_mix", "g_q", "g_k", "g_attn_out", "conv_b", "dt_bias", "a_log", "d_skip", "g_ssm_out", "g_cross", "g_mem",
           "g_cq", "g_ck", "g_mlp")
WEIGHTS = ("g_mix", "w_in", "g_q", "g_k", "g_attn_out", "conv_w", "conv_b", "dt_bias", "a_log", "d_skip", "g_ssm_out", "w_out",
           "g_cross", "g_mem", "w_cq", "w_ckv", "g_cq", "g_ck", "w_co", "g_mlp", "w_up", "w_down")


def _pack(parts):
    flat = jnp.concatenate([t.reshape(-1) for t in parts])
    total = -(-flat.shape[0] // 1024) * 1024
    return jnp.pad(flat, (0, total - flat.shape[0])).reshape(total // 128, 128)


def _rows_of(n):
    return -(-n // 128)


def _slot_rows(n):
    return -(-n // 1024) * 8


def _pack_rows(parts):
    rows = []
    for t in parts:
        flat = t.reshape(-1)
        rows.append(jnp.pad(flat, (0, 128 * _slot_rows(flat.shape[0]) - flat.shape[0])).reshape(-1, 128))
    return jnp.concatenate(rows)


def _adamw_vectors(summed, chip, vectors, conv):
    groups = list(vectors) + [conv]
    offsets, row = [], 0
    for w, _, _ in groups:
        offsets.append(row)
        row += _slot_rows(w.shape[1]) if w.shape[0] == 1 else _slot_rows(w.shape[0] * N_CHIPS * w.shape[1])
    conv_blocks = _rows_of(conv[0].shape[1])

    def body(chip_ref, sum_ref, *refs):
        ins, outs = refs[:3 * len(groups)], refs[3 * len(groups):]

        def update(i, g, idx):
            w_ref, m_ref, v_ref = ins[3 * i:3 * i + 3]
            delta, new_m, new_v = _adamw_math(w_ref[idx], g, m_ref[idx], v_ref[idx])
            for o_ref, val in zip(outs[4 * i:4 * i + 4], (g, delta, new_m, new_v)):
                o_ref[idx] = val

        for i, (w, _, _) in enumerate(vectors):
            for t in range(_rows_of(w.shape[1])):
                width = min(128, w.shape[1] - 128 * t)
                update(i, sum_ref[pl.ds(offsets[i] + t, 1), pl.ds(0, width)], (slice(None), pl.ds(128 * t, width)))
        for tap in range(conv[0].shape[0]):
            for blk in range(conv_blocks):
                src = offsets[-1] + tap * N_CHIPS * conv_blocks + chip_ref[0] * conv_blocks + blk
                update(len(vectors), sum_ref[pl.ds(src, 1), :], (pl.ds(tap, 1), pl.ds(128 * blk, 128)))

    def whole(a):
        return pl.BlockSpec(a.shape, lambda i, chip_ref: (0,) * a.ndim)

    operands = [t for group in groups for t in group]
    grid_spec = pltpu.PrefetchScalarGridSpec(
        num_scalar_prefetch=1, grid=(1,), in_specs=[whole(summed)] + [whole(t) for t in operands],
        out_specs=[whole(w) for w, _, _ in groups for _ in range(4)])
    res = pl.pallas_call(body, name="adamw_vectors", grid_spec=grid_spec,
                         out_shape=[jax.ShapeDtypeStruct(w.shape, F32) for w, _, _ in groups for _ in range(4)],
                         compiler_params=_params(("arbitrary",)))(chip.reshape(1).astype(jnp.int32), summed, *operands)
    return [res[4 * i:4 * i + 4] for i in range(len(groups))]


def _unpack(buf, shapes):
    flat, out, pos = buf.reshape(-1), [], 0
    for shape in shapes:
        size = math.prod(shape)
        out.append(flat[pos:pos + size].reshape(shape))
        pos += size
    return out


def kernel(x, mem, positions, g_mix, w_in, g_q, g_k, g_attn_out, conv_w, conv_b, dt_bias, a_log, d_skip, g_ssm_out, w_out, g_cross, g_mem, w_cq, w_ckv, g_cq, g_ck, w_co, g_mlp, w_up, w_down, loss_target, m_g_mix, m_w_in, m_g_q, m_g_k, m_g_attn_out, m_conv_w, m_conv_b, m_dt_bias, m_a_log, m_d_skip, m_g_ssm_out, m_w_out, m_g_cross, m_g_mem, m_w_cq, m_w_ckv, m_g_cq, m_g_ck, m_w_co, m_g_mlp, m_w_up, m_w_down, v_g_mix, v_w_in, v_g_q, v_g_k, v_g_attn_out, v_conv_w, v_conv_b, v_dt_bias, v_a_log, v_d_skip, v_g_ssm_out, v_w_out, v_g_cross, v_g_mem, v_w_cq, v_w_ckv, v_g_cq, v_g_ck, v_w_co, v_g_mlp, v_w_up, v_w_down):
    args = dict(locals())
    weights = {n: args[n][0] for n in WEIGHTS}
    mom_m = {n: args["m_" + n][0] for n in WEIGHTS}
    mom_v = {n: args["v_" + n][0] for n in WEIGHTS}
    x_idx, y_idx, c_idx = _place()
    chip = 2 * x_idx + y_idx

    shapes = {n: weights[n].shape for n in MATRICES}
    first, mid, late = ("w_in",), ("w_out", "w_cq", "w_ckv", "w_co"), ("w_up", "w_down")
    w_in_t, m_in_t, v_in_t = (jnp.swapaxes(t, 1, 2)[0] for t in (w_in, m_w_in, v_w_in))
    w_in_buf = [_cast_w_in_transposed(w_in_t, chip)]
    sems_in, w_in_buf, token = _split_start("gather_ici_start_w_in", w_in_buf, _ici_plan(first, shapes), [3])
    bufs = [_cast_into_gathered(weights[n], n, chip, after=(token,)) for n in mid + late]
    params = {n: weights[n].reshape(1, -1) for n in VECTORS}
    h_in = _rowwise(_norm_fn, [_full(x[0])], [_full(params["g_mix"])], [(D_MODEL, BF16, D_MODEL, 0, False)], name="norm_in",
                    after=(token,))[0]
    taps, tap_cols = weights["conv_w"].shape
    conv_parts = _small_allreduce(_pack([jnp.zeros((N_CHIPS, taps, tap_cols), F32).at[chip].set(0.5 * weights["conv_w"])]),
                                  "gather_conv_taps", after=(h_in, m_in_t, v_in_t, *bufs))
    w_in_buf = _split_wait("gather_ici_wait_w_in", w_in_buf, sems_in[0], _ici_plan(first, shapes), token, conv_parts)
    pass_sems, w_in_buf, token = _split_start("gather_pass_start_w_in", w_in_buf, _pass_on_plan(first, shapes), [3])
    plan = lambda refs: _ici_plan(mid, shapes)(refs[:4]) + _ici_plan(late, shapes)(refs[4:])
    sems_rest, bufs, token = _split_start("gather_ici_start_rest", bufs, plan, [12, 6], after=(token,))
    w_in_buf = _split_wait("gather_pass_wait_w_in", w_in_buf, pass_sems[0], _pass_on_plan(first, shapes), token)
    w_in_full = _w_in_columns(w_in_buf[0], to_shards=False)
    full = {"w_in": w_in_full,
            "w_dt": jnp.pad(w_in_full[:, D_MAIN:].reshape(D_MODEL, N_GROUPS, HEADS_PER_GROUP),
                            ((0, 0), (0, 0), (0, 128 - HEADS_PER_GROUP))).reshape(D_MODEL, DT_PAD)}
    in_flight = {}

    def more_weights(stage, after):
        if stage == "mixer_done":
            got = _split_wait("gather_ici_wait_mid", bufs[:4], sems_rest[0], _ici_plan(mid, shapes), after)
            sems, got, token = _split_start("gather_pass_start_mid", got, _pass_on_plan(mid, shapes), [12])
            return dict(zip(mid, _split_wait("gather_pass_wait_mid", got, sems[0], _pass_on_plan(mid, shapes), token)))
        if stage == "cross_started":
            got = _split_wait("gather_ici_wait_late", bufs[4:], sems_rest[1], _ici_plan(late, shapes), after)
            in_flight["late"] = _split_start("gather_pass_start_late", got, _pass_on_plan(late, shapes), [6])
            return {}
        sems, got, token = in_flight.pop("late")
        return dict(zip(late, _split_wait("gather_pass_wait_late", got, sems[0], _pass_on_plan(late, shapes), token, after)))

    params["conv_w"] = _unpack(conv_parts, [(N_CHIPS, taps, tap_cols)])[0].transpose(1, 0, 2).reshape(taps, N_CHIPS * tap_cols)

    groups = (("w_down",), ("w_up",), ("w_co", "w_cq", "w_ckv", "w_out"), ("w_in",))
    scattered = []

    class GradStore(dict):
        pending = None

        def __setitem__(self, name, value):
            super().__setitem__(name, value)
            if "w_main" in self and "w_dt" in self and "w_in" not in self:
                gw_in = lax.dynamic_update_slice(self["w_main"], _unpad_heads(self["w_dt"]), (0, D_MAIN))
                self["w_in"] = _w_in_columns(gw_in, to_shards=True)
            for group in groups:
                if name in group and all(n in self for n in group):
                    self.settle()
                    pieces = [self[n].reshape(N_CHIPS, 2, shapes[n][0] // 2, shapes[n][1]) for n in group]
                    if group == groups[-1]:
                        self.scatter(group, pieces, _sibling_swap(pieces, "grad_swap_" + group[0]))
                    else:
                        landing = [lax.empty((N_CHIPS,) + a.shape[2:], BF16) for a in pieces]
                        sems, thru, self.token = _split_start("grad_swap_start_" + group[0], pieces + landing,
                                                              _swap_plan(len(pieces)), [len(pieces)])
                        self.pending = (group, sems[0], thru)

        def settle(self, *after):
            if self.pending is not None:
                group, sems, thru = self.pending
                self.pending = None
                thru = _split_wait("grad_swap_wait_" + group[0], thru, sems, _swap_plan(len(group)), *after)
                self.scatter(group, thru[:len(group)], thru[len(group):])

        def scatter(self, group, pieces, from_sibling):
            sums = [_add_halves(a, r, c_idx, "add_halves_" + n) for n, a, r in zip(group, pieces, from_sibling)]
            landing = [lax.empty((3,) + s.shape[1:], BF16) for s in sums]
            sems, thru, self.token = _split_start("grad_scatter_start_" + group[0], sums + landing,
                                                  _scatter_plan(len(sums)), [3 * len(sums)])
            scattered.append((group, sems[0], thru))

    loss, grad_x, grads = _local_step(x[0], mem[0], positions[0], loss_target[0], params, full, more_weights, GradStore(),
                                      h_in)

    out_g, out_d, out_m, out_v = {}, {}, {}, {}

    def finish(entries, order, token):
        halves = {}
        for group, sems, thru in entries:
            thru = _split_wait("grad_scatter_wait_" + group[0], thru, sems, _scatter_plan(len(group)), token)
            for i, n in enumerate(group):
                halves[n] = _sum_chips(thru[i], thru[len(group) + i], chip, "sum_chips_" + n)
        sources = [halves[n] for n in order]
        landing = [lax.empty(s.shape, F32) for s in sources]
        sems, thru, token = _split_start("grad_share_start_" + order[0], sources + landing, _share_plan(len(order)),
                                         [1] * len(order))
        for i, n in enumerate(order):
            own, other = _split_wait("grad_share_wait_" + n, [thru[i], thru[len(order) + i]], sems[i], _share_plan(1), token)
            if n == "w_in":
                res_t = _adamw_w_in_transposed(w_in_t, own, other, m_in_t, v_in_t, c_idx)
                out_g[n], out_d[n], out_m[n], out_v[n] = (t.T for t in res_t)
            else:
                out_g[n], out_d[n], out_m[n], out_v[n] = _adamw_halves(weights[n], own, other, mom_m[n], mom_v[n], c_idx,
                                                                       "adamw_" + n)
            token = out_v[n]
        return token

    token = finish(scattered[:-1], ("w_cq", "w_co", "w_ckv", "w_out", "w_up", "w_down"), grad_x)
    finish(scattered[-1:], ("w_in",), token)

    names = VECTORS + ("conv_w",)
    summed = _small_allreduce(_pack_rows([grads[n] for n in names] + [loss]), "allreduce_vectors")
    total_loss = summed[sum(_slot_rows(grads[n].size) for n in names), 0]
    small_out = _adamw_vectors(summed, chip, [(args[n], args["m_" + n], args["v_" + n]) for n in VECTORS],
                               (weights["conv_w"], mom_m["conv_w"], mom_v["conv_w"]))
    for n, res in zip(names, small_out):
        out_g[n], out_d[n], out_m[n], out_v[n] = (t.reshape(weights[n].shape) for t in res)

    outs =[total_loss, grad_x[None]]
    for group in (out_g, out_d, out_m, out_v):
        outs += [group[n][None] for n in WEIGHTS]
    return tuple(outs)
```

```python
import functools
import math

import jax
import jax.numpy as jnp
from jax import lax
from jax.experimental import pallas as pl
from jax.experimental.pallas import tpu as pltpu

F32 = jnp.float32
BF16 = jnp.bfloat16

SEQ = 2048
D_MODEL = 2048
HEAD = 64
D_ATTN = 1024
D_SSM = 1024
N_GROUPS = 4
N_STATE = 128
CHUNK = 128
ATT_BLK = 128
N_MEM = 256
D_CROSS = 512
D_MAIN = 6144
N_DT = 16
DT_PAD = 512
ROT = 16
ROPE_THETA = 500000.0
EPS = 1e-6
NEG = -1e30
BRANCH_BLOCKS = (16, 4, 1)
DILATIONS = (1, 4, 16)

ADAM_LR, ADAM_B1, ADAM_B2, ADAM_EPS, ADAM_WD, ADAM_STEP = 0.001, 0.9, 0.999, 1e-08, 0.01, 10

VMEM_LIMIT = 56 * 1024 * 1024
MESH = pl.DeviceIdType.MESH


def _params(sem, **kw):
    return pltpu.CompilerParams(dimension_semantics=sem, vmem_limit_bytes=VMEM_LIMIT, **kw)


def _bdot(a, b, dims):
    return lax.dot_general(a.astype(BF16), b.astype(BF16), (dims, ((), ())), preferred_element_type=F32)


def _fdot(a, b, dims):
    return lax.dot_general(a, b, (dims, ((), ())), preferred_element_type=F32, precision=lax.Precision.HIGHEST)


NN = ((1,), (0,))
NT = ((1,), (1,))
TN = ((0,), (0,))


SIDE_BLOCKS = 16


def _tile(n, want):
    t = min(n, want)
    while n % t:
        t //= 2
    return t


def _matmul(a, b, *, mode, name, outs, extra=(), epilogue=None, col_shards=1, after=(), n_cols=None, out_cols=None,
            tile_sums=0, side_adamw=None, tm=1024, tn=1024, tk=2048):
    if mode == "nn":
        (m, k), n = a.shape, b.shape[1]
    elif mode == "nt":
        (m, k), n = a.shape, b.shape[0]
    else:
        (k, m), n = a.shape, b.shape[1]
    n = n if n_cols is None else n_cols
    tm, tn, tk = _tile(m, tm), _tile(n // col_shards, tn), _tile(k, tk)
    nk = k // tk
    per_shard = n // col_shards // tn
    dims = {"nn": NN, "nt": NT, "tn": TN}[mode]
    a_spec = pl.BlockSpec((tk, tm), lambda i, j, kk: (kk, i)) if mode == "tn" else pl.BlockSpec((tm, tk), lambda i, j, kk: (i, kk))
    b_spec = pl.BlockSpec((tn, tk), lambda i, j, kk: (j, kk)) if mode == "nt" else pl.BlockSpec((tk, tn), lambda i, j, kk: (kk, j))
    o_spec = pl.BlockSpec((tm, tn), lambda i, j, kk: (i, j))
    n_extra, n_out, n_after = len(extra), len(outs), len(after)
    grid = (m // tm, n // tn, nk)
    side_ins, side_specs, side_shapes = (), [], []
    if side_adamw is not None:
        side_ins = side_adamw
        s_rows, s_cols = side_adamw[1].shape
        s_tr = s_rows // SIDE_BLOCKS
        per_half = SIDE_BLOCKS // 2
        assert grid[0] * grid[1] * grid[2] == SIDE_BLOCKS and s_tr % 8 == 0
        step_of = lambda i, j, kk: (i * grid[1] + j) * grid[2] + kk
        whole = pl.BlockSpec((s_tr, s_cols), lambda i, j, kk: (step_of(i, j, kk), 0))
        half = pl.BlockSpec((s_tr, s_cols), lambda i, j, kk: (step_of(i, j, kk) % per_half, 0))
        side_specs = [pl.BlockSpec(memory_space=pltpu.SMEM), whole, half, half, whole, whole]
        side_shapes = [jax.ShapeDtypeStruct((s_rows, s_cols), F32)] * 4
    n_side = len(side_ins)

    def body(a_ref, b_ref, *rest):
        extra_refs, acc_ref = rest[:n_extra], rest[-1]
        out_refs = rest[n_extra + n_after + n_side:len(rest) - 1 - len(side_shapes)]
        if side_adamw is not None:
            c_ref, w_ref, own_ref, other_ref, m_ref, v_ref = rest[n_extra + n_after:n_extra + n_after + n_side]
            side_out = rest[len(rest) - 1 - len(side_shapes):-1]
            step = (pl.program_id(0) * grid[1] + pl.program_id(1)) * grid[2] + pl.program_id(2)

            g_ = jnp.where(step // per_half == c_ref[0, 0], own_ref[...], other_ref[...])
            side_out[0][...] = g_
            side_out[1][...], side_out[2][...], side_out[3][...] = _adamw_math(w_ref[...], g_, m_ref[...], v_ref[...])

        def finish(acc):
            res = (acc,) if epilogue is None else epilogue(acc, *[e[...] for e in extra_refs])
            for o_ref, r in zip(out_refs[:n_out], res):
                o_ref[...] = r.astype(o_ref.dtype)
            for o_ref, r in zip(out_refs[n_out:], res[n_out:]):
                o_ref[...] = jnp.broadcast_to(r, o_ref.shape)

        if nk == 1:
            finish(_bdot(a_ref[...], b_ref[...], dims))
            return
        kk = pl.program_id(2)

        @pl.when(kk == 0)
        def _():
            acc_ref[...] = jnp.zeros_like(acc_ref)

        acc_ref[...] += _bdot(a_ref[...], b_ref[...], dims)

        @pl.when(kk == nk - 1)
        def _():
            finish(acc_ref[...])

    if col_shards == 1:
        out_specs, out_dims = [o_spec] * n_out, (m, n if out_cols is None else out_cols)
    else:
        sharded = pl.BlockSpec((None, tm, tn), lambda i, j, kk: (j // per_shard, i, j % per_shard))
        out_specs, out_dims = [sharded] * n_out, (col_shards, m, n // col_shards)
    res = pl.pallas_call(
        body, name=name, grid=grid,
        in_specs=[a_spec, b_spec] + [o_spec] * n_extra + [pl.BlockSpec(memory_space=pl.ANY)] * n_after + side_specs,
        out_specs=out_specs + [pl.BlockSpec((8, 128), lambda i, j, kk: (i, j))] * tile_sums + side_specs[1:2] * len(side_shapes),
        out_shape=[jax.ShapeDtypeStruct(out_dims, dt) for dt in outs]
        + [jax.ShapeDtypeStruct((m // tm * 8, n // tn * 128), F32)] * tile_sums + side_shapes,
        scratch_shapes=[pltpu.VMEM((tm, tn) if nk > 1 else (8, 128), F32)],
        compiler_params=_params(("parallel", "parallel", "arbitrary") if side_adamw is None else ("arbitrary",) * 3),
    )(a, b, *extra, *after, *side_ins)
    res = list(res[:n_out]) + [t[::8, ::128] for t in res[n_out:n_out + tile_sums]] + list(res[n_out + tile_sums:])
    return res[0] if len(res) == 1 else res


def _row_spec(tr, bw, cb, per_group):
    return pl.BlockSpec((tr, bw), (lambda g, i: (i, cb + g)) if per_group else (lambda g, i: (i, cb)))


def _vec_spec(bw, cb, per_group):
    return pl.BlockSpec((1, bw), (lambda g, i: (0, cb + g)) if per_group else (lambda g, i: (0, cb)))


def _rowwise(fn, rows, vecs, outs, *, name, n_rows=SEQ, tr=512, groups=1, after=()):
    n_r, n_v, n_after = len(rows), len(vecs), len(after)

    def body(*refs):
        vals = [r[...].astype(F32) for r in refs[:n_r + n_v]]
        res = fn(*vals)
        for o_ref, r in zip(refs[n_r + n_v + n_after:], res):
            o_ref[...] = r.astype(o_ref.dtype)

    res = pl.pallas_call(
        body, name=name, grid=(groups, n_rows // tr),
        in_specs=[_row_spec(tr, bw, cb, pg) for _, bw, cb, pg in rows] + [_vec_spec(bw, cb, pg) for _, bw, cb, pg in vecs]
        + [pl.BlockSpec(memory_space=pl.ANY)] * n_after,
        out_specs=[_row_spec(tr, bw, cb, pg) for _, _, bw, cb, pg in outs],
        out_shape=[jax.ShapeDtypeStruct((n_rows, w), dt) for w, dt, _, _, _ in outs],
        compiler_params=_params(("parallel", "parallel")),
    )(*[r[0] for r in rows], *[v[0] for v in vecs], *after)
    return res


def _rowwise_vjp(fn, rows, vecs, cts, row_grads, vec_grads, *, name, n_rows=SEQ, tr=512, groups=1, after=()):
    n_r, n_v, n_after = len(rows), len(vecs), len(after)
    ct_ops = [op for group in cts for op in group]
    ct_sizes = [len(group) for group in cts]
    res_ops = [g[6] for g in row_grads if g[6] is not None]
    n_ct, n_res, n_rg = len(ct_ops), len(res_ops), len(row_grads)

    def body(*refs):
        vals = [r[...].astype(F32) for r in refs[:n_r + n_v]]
        pos = n_r + n_v
        ct_vals = []
        for size in ct_sizes:
            acc = refs[pos][...].astype(F32)
            for t in range(1, size):
                acc = acc + refs[pos + t][...].astype(F32)
            ct_vals.append(acc)
            pos += size
        res_refs = refs[pos:pos + n_res]
        out_refs = refs[pos + n_res + n_after:]
        _, pullback = jax.vjp(fn, *vals)
        grads = pullback(tuple(ct_vals))
        r_i = 0
        for o_ref, g in zip(out_refs[:n_rg], row_grads):
            val = grads[g[0]]
            if g[6] is not None:
                val = val + res_refs[r_i][...].astype(F32)
                r_i += 1
            o_ref[...] = val.astype(o_ref.dtype)
        first = (pl.program_id(1) == 0)
        for o_ref, g in zip(out_refs[n_rg:], vec_grads):
            val = jnp.sum(grads[n_r + g[0]], axis=0, keepdims=True)
            init = first if g[4] else jnp.logical_and(first, pl.program_id(0) == 0)

            @pl.when(init)
            def _(o_ref=o_ref, val=val):
                o_ref[...] = val

            @pl.when(jnp.logical_not(init))
            def _(o_ref=o_ref, val=val):
                o_ref[...] += val

    in_specs = [_row_spec(tr, bw, cb, pg) for _, bw, cb, pg in rows] + [_vec_spec(bw, cb, pg) for _, bw, cb, pg in vecs]
    in_specs += [_row_spec(tr, bw, cb, pg) for _, bw, cb, pg in ct_ops + res_ops] + [pl.BlockSpec(memory_space=pl.ANY)] * n_after
    out_specs =[_row_spec(tr, g[3], g[4], g[5]) for g in row_grads] + [_vec_spec(g[2], g[3], g[4]) for g in vec_grads]
    out_shape = [jax.ShapeDtypeStruct((n_rows, g[1]), g[2]) for g in row_grads]
    out_shape += [jax.ShapeDtypeStruct((1, g[1]), F32) for g in vec_grads]
    return pl.pallas_call(
        body, name=name, grid=(groups, n_rows // tr),
        in_specs=in_specs, out_specs=out_specs, out_shape=out_shape,
        compiler_params=_params(("arbitrary", "arbitrary")),
    )(*[r[0] for r in rows], *[v[0] for v in vecs], *[c[0] for c in ct_ops], *[r[0] for r in res_ops], *after)


def _full(arr, width=None):
    return (arr, arr.shape[1] if width is None else width, 0, False)


def _make_xor(sh):
    def raw(x):
        n = x.shape[-1]
        lane = lax.broadcasted_iota(jnp.int32, x.shape, x.ndim - 1)
        up = pltpu.roll(x, n - sh, x.ndim - 1)
        down = pltpu.roll(x, sh, x.ndim - 1)
        return jnp.where((lane & sh) == 0, up, down)

    f = jax.custom_vjp(raw)
    f.defvjp(lambda x: (raw(x), None), lambda _, ct: (raw(ct),))
    return f


_SWAP_ROPE_HALVES = _make_xor(ROT // 2)


def _head_sum(x):
    n = x.shape[-1]
    same_head = (lax.broadcasted_iota(jnp.int32, (n, n), 0) // HEAD) == (lax.broadcasted_iota(jnp.int32, (n, n), 1) // HEAD)
    return _fdot(x, same_head.astype(F32), NN)


def _rms(x, g):
    return x * lax.rsqrt(jnp.mean(x * x, axis=-1, keepdims=True) + EPS) * g


def _head_rms_rope(x, g, cos, sin, scale):
    y = x * lax.rsqrt(_head_sum(x * x) * (1.0 / HEAD) + EPS) * g
    return (y * cos + _SWAP_ROPE_HALVES(y) * sin) * scale


def _qk_fn(q, k, v, cos, sin, gq, gk):
    return (_head_rms_rope(q, gq, cos, sin, HEAD ** -0.5), _head_rms_rope(k, gk, cos, sin, 1.0), v)


def _norm_fn(x, g):
    return (_rms(x, g),)


def _merge_fn(o0, o1, o2, l0, l1, l2, g):
    m = lax.stop_gradient(jnp.maximum(jnp.maximum(l0, l1), l2))
    e0, e1, e2 = jnp.exp(l0 - m), jnp.exp(l1 - m), jnp.exp(l2 - m)
    mix = (e0 * o0 + e1 * o1 + e2 * o2) / (e0 + e1 + e2)
    return (_rms(mix, g),)


def _gate_fn(y, z, g):
    return (_rms(y * (z * jax.nn.sigmoid(z)), g),)


def _attn_pair(q, kc, vc, kp=None, vp=None, has_prev=None):
    pick0, pick1 = _head_picks()
    k_band, v_band, mask = _attn_band(kc, vc, kp, vp, has_prev)
    s = jnp.where(mask, _bdot(jnp.concatenate([q * pick0, q * pick1], axis=0), k_band, NT), NEG)
    m = jnp.max(s, axis=-1, keepdims=True)
    p = jnp.exp(s - m)
    den = jnp.sum(p, axis=-1, keepdims=True)
    acc = _bdot(p, v_band, NN) * (1.0 / den)
    lse_rows = m + jnp.log(den)
    o = pick0 * acc[:ATT_BLK] + pick1 * acc[ATT_BLK:]
    lse = pick0 * lse_rows[:ATT_BLK] + pick1 * lse_rows[ATT_BLK:]
    return o, lse


def _head_picks():
    lane = lax.broadcasted_iota(jnp.int32, (1, 2 * HEAD), 1)
    return (lane < HEAD).astype(F32), (lane >= HEAD).astype(F32)


def _attn_band(kc, vc, kp, vp, has_prev):
    n_keys = ATT_BLK if kp is None else 2 * ATT_BLK
    qi = lax.broadcasted_iota(jnp.int32, (2 * ATT_BLK, n_keys), 0) & (ATT_BLK - 1)
    kj = lax.broadcasted_iota(jnp.int32, (2 * ATT_BLK, n_keys), 1)
    if kp is None:
        return kc, vc, qi >= kj
    in_prev = jnp.logical_and(jnp.logical_and(kj < ATT_BLK, kj >= qi), has_prev)
    mask = jnp.logical_or(in_prev, jnp.logical_and(kj >= ATT_BLK, qi >= kj - ATT_BLK))
    return jnp.concatenate([kp, kc], axis=0), jnp.concatenate([vp, vc], axis=0), mask


def _attn_config(b):
    r = DILATIONS[b]
    return r, ATT_BLK * r, (512 if r == 1 else 128), BRANCH_BLOCKS[b] > 1


def _for_residues(r, fn):
    if r <= 4:
        for rho in range(r):
            fn(rho)
    else:
        def step(t, carry):
            for u in range(4):
                fn(4 * t + u)
            return carry

        lax.fori_loop(0, r // 4, step, 0)


def _strided_rows(start, r):
    if r > 1:
        return pl.ds(start, ATT_BLK, stride=r)
    return pl.ds(start if isinstance(start, int) else pl.multiple_of(start, ATT_BLK), ATT_BLK)


def _attention_fwd(qn, kn, vn, b):
    r, rows, lanes, with_prev = _attn_config(b)
    cur = pl.BlockSpec((rows, lanes), lambda g, n: (n, g))
    prev = pl.BlockSpec((rows, lanes), lambda g, n: (jnp.maximum(n - 1, 0), g))

    def body(*refs):
        ins, (o_ref, l_ref) = refs[:-2], refs[-2:]
        has_prev = pl.program_id(1) > 0

        def one(rho):
            sub = _strided_rows(rho, r)
            for pair in range(lanes // 128):
                sl = pl.ds(pair * 128, 128)
                args = [ref[sub, sl] for ref in ins] + ([has_prev] if with_prev else [])
                o_ref[sub, sl], l_ref[sub, sl] = _attn_pair(*args)

        _for_residues(r, one)

    operands = (qn, kn, vn, kn, vn) if with_prev else (qn, kn, vn)
    return pl.pallas_call(
        body, name="attn_fwd_%d" % r, grid=(D_ATTN // lanes, SEQ // rows),
        in_specs=[cur, cur, cur] + ([prev, prev] if with_prev else []), out_specs=[cur, cur],
        out_shape=[jax.ShapeDtypeStruct((SEQ, D_ATTN), F32)] * 2,
        compiler_params=_params(("parallel", "parallel")),
    )(*operands)


def _attn_pair_bwd(q, kc, vc, kp, vp, o, lse, do, dl, has_prev):
    pick0, pick1 = _head_picks()
    lane = lax.broadcasted_iota(jnp.int32, (1, 2 * HEAD), 1)
    k_band, v_band, mask = _attn_band(kc, vc, kp, vp, has_prev)
    q2 = jnp.concatenate([q * pick0, q * pick1], axis=0)
    do2 = jnp.concatenate([do * pick0, do * pick1], axis=0)
    lse2 = jnp.concatenate([jnp.sum(lse * (lane == 0).astype(F32), axis=-1, keepdims=True),
                            jnp.sum(lse * (lane == HEAD).astype(F32), axis=-1, keepdims=True)], axis=0)
    base = jnp.sum(jnp.concatenate([dl * pick0, dl * pick1], axis=0) - do2 * jnp.concatenate([o, o], axis=0),
                   axis=-1, keepdims=True)
    p = jnp.exp(jnp.where(mask, _bdot(q2, k_band, NT), NEG) - lse2)
    ds = p * (_bdot(do2, v_band, NT) + base)
    dq2 = _bdot(ds, k_band, NN)
    dq = pick0 * dq2[:ATT_BLK] + pick1 * dq2[ATT_BLK:]
    dk, dv = _bdot(ds, q2, TN), _bdot(p, do2, TN)
    if kp is None:
        return dq, dk, dv
    return dq, dk[ATT_BLK:], dv[ATT_BLK:], dk[:ATT_BLK], dv[:ATT_BLK]


def _attention_bwd(qn, kn, vn, o, lse, do, dl, b):
    r, rows, lanes, with_prev = _attn_config(b)
    cur = pl.BlockSpec((rows, lanes), lambda g, n: (n, g))
    prev = pl.BlockSpec((rows, lanes), lambda g, n: (jnp.maximum(n - 1, 0), g))
    whole = pl.BlockSpec((SEQ, lanes), lambda g, n: (0, g))
    n_in = 5 if with_prev else 3

    def body(*refs):
        ins, (o_ref, l_ref, do_ref, dl_ref, dq_ref, dk_ref, dv_ref) = refs[:n_in], refs[n_in:]
        n = pl.program_id(1)

        @pl.when(n == 0)
        def _():
            dk_ref[...] = jnp.zeros_like(dk_ref)
            dv_ref[...] = jnp.zeros_like(dv_ref)

        def one(rho):
            sub = _strided_rows(rho, r)
            sub_c = _strided_rows(n * rows + rho, r)
            sub_p = _strided_rows(jnp.maximum(n - 1, 0) * rows + rho, r)
            for pair in range(lanes // 128):
                sl = pl.ds(pair * 128, 128)
                vals = [ref[sub, sl] for ref in ins] + ([] if with_prev else [None, None])
                grads = _attn_pair_bwd(*vals, o_ref[sub, sl], l_ref[sub, sl], do_ref[sub, sl], dl_ref[sub, sl], n > 0)
                dq_ref[sub, sl] = grads[0]
                dk_ref[sub_c, sl] += grads[1]
                dv_ref[sub_c, sl] += grads[2]
                if with_prev:
                    dk_ref[sub_p, sl] += grads[3]
                    dv_ref[sub_p, sl] += grads[4]

        _for_residues(r, one)

    operands = (qn, kn, vn, kn, vn) if with_prev else (qn, kn, vn)
    return pl.pallas_call(
        body, name="attn_bwd_%d" % r, grid=(D_ATTN // lanes, SEQ // rows),
        in_specs=[cur, cur, cur] + ([prev, prev] if with_prev else []) + [cur] * 4, out_specs=[cur, whole, whole],
        out_shape=[jax.ShapeDtypeStruct((SEQ, D_ATTN), F32)] * 3,
        compiler_params=_params(("parallel", "arbitrary")),
    )(*operands, o, lse, do, dl)


CONV_COLS = 256
XBC_BLOCK0 = (3 * D_ATTN + D_SSM) // CONV_COLS


def _shift_rows(x, s):
    n = x.shape[0]
    t = lax.broadcasted_iota(jnp.int32, x.shape, 0)
    if s >= 0:
        return jnp.where(t >= s, pltpu.roll(x, s, 0), 0.0)
    return jnp.where(t < n + s, pltpu.roll(x, n + s, 0), 0.0)


def _conv_pre(x, w_ref, b_ref):
    delayed = [_shift_rows(x, 3 - k) for k in range(3)]
    pre = b_ref[...] + w_ref[3:4, :] * x
    for k in range(3):
        pre = pre + w_ref[k:k + 1, :] * delayed[k]
    return pre, delayed


def _conv_fwd(proj, conv_w, conv_b):
    cols = conv_w.shape[1]

    def body(x_ref, w_ref, b_ref, o_ref):
        pre, _ = _conv_pre(x_ref[...], w_ref, b_ref)
        o_ref[...] = pre * jax.nn.sigmoid(pre)

    blk = pl.BlockSpec((SEQ, CONV_COLS), lambda j: (0, j))
    return pl.pallas_call(
        body, name="conv_fwd", grid=(cols // CONV_COLS,),
        in_specs=[pl.BlockSpec((SEQ, CONV_COLS), lambda j: (0, XBC_BLOCK0 + j)),
                  pl.BlockSpec((4, CONV_COLS), lambda j: (0, j)), pl.BlockSpec((1, CONV_COLS), lambda j: (0, j))],
        out_specs=blk, out_shape=jax.ShapeDtypeStruct((SEQ, cols), F32),
        compiler_params=_params(("parallel",)),
    )(proj, conv_w, conv_b)


def _conv_bwd(proj, conv_w, conv_b, dxs, db, dc):
    cols = conv_w.shape[1]
    x_blocks, b_blocks = dxs.shape[1] // CONV_COLS, db.shape[1] // CONV_COLS

    def body(x_ref, w_ref, b_ref, dxs_ref, db_ref_in, dc_ref_in, dx_ref, dw_ref, db_ref):
        j = pl.program_id(0)
        dy = jnp.where(j < x_blocks, dxs_ref[...], jnp.where(j < x_blocks + b_blocks, db_ref_in[...], dc_ref_in[...]))
        x = x_ref[...]
        pre, delayed = _conv_pre(x, w_ref, b_ref)
        sg = jax.nn.sigmoid(pre)
        dpre = dy * (sg * (1.0 + pre * (1.0 - sg)))
        db_ref[...] = jnp.sum(dpre, axis=0, keepdims=True)
        dx = w_ref[3:4, :] * dpre
        dw_ref[3:4, :] = jnp.sum(dpre * x, axis=0, keepdims=True)
        for k in range(3):
            dx = dx + w_ref[k:k + 1, :] * _shift_rows(dpre, k - 3)
            dw_ref[k:k + 1, :] = jnp.sum(dpre * delayed[k], axis=0, keepdims=True)
        dw_ref[4:8, :] = jnp.zeros((4, CONV_COLS), F32)
        dx_ref[...] = dx.astype(dx_ref.dtype)

    blk = pl.BlockSpec((SEQ, CONV_COLS), lambda j: (0, j))
    parts = [pl.BlockSpec((SEQ, CONV_COLS), lambda j: (0, jnp.minimum(j, x_blocks - 1))),
             pl.BlockSpec((SEQ, CONV_COLS), lambda j: (0, jnp.clip(j - x_blocks, 0, b_blocks - 1))),
             pl.BlockSpec((SEQ, CONV_COLS), lambda j: (0, jnp.clip(j - x_blocks - b_blocks, 0, b_blocks - 1)))]
    return pl.pallas_call(
        body, name="conv_bwd", grid=(cols // CONV_COLS,),
        in_specs=[pl.BlockSpec((SEQ, CONV_COLS), lambda j: (0, XBC_BLOCK0 + j)),
                  pl.BlockSpec((4, CONV_COLS), lambda j: (0, j)), pl.BlockSpec((1, CONV_COLS), lambda j: (0, j))] + parts,
        out_specs=[blk, pl.BlockSpec((8, CONV_COLS), lambda j: (0, j)), pl.BlockSpec((1, CONV_COLS), lambda j: (0, j))],
        out_shape=[jax.ShapeDtypeStruct((SEQ, cols), BF16), jax.ShapeDtypeStruct((8, cols), F32),
                   jax.ShapeDtypeStruct((1, cols), F32)],
        compiler_params=_params(("parallel",)),
    )(proj, conv_w, conv_b, dxs, db, dc)


HEADS_PER_GROUP = 4


GROUP_WIDTH = HEADS_PER_GROUP * HEAD


def _ssd_chunk(x, bm, cm, dtr, bias, alog, dsk, h):
    row = lax.broadcasted_iota(jnp.int32, (CHUNK, CHUNK), 0)
    col = lax.broadcasted_iota(jnp.int32, (CHUNK, CHUNK), 1)
    causal = row >= col
    z = dtr + bias
    dt = jnp.maximum(z, 0.0) + jnp.log(1.0 + jnp.exp(-jnp.abs(z)))
    acs = _fdot(causal.astype(F32), dt * -jnp.exp(alog), NN)
    acs_t, dt_t = acs.T, dt.T
    cb = _bdot(cm, bm, NT)
    lane = lax.broadcasted_iota(jnp.int32, (1, CHUNK), 1)
    sub = lax.broadcasted_iota(jnp.int32, (CHUNK, 1), 0)
    wide = lax.broadcasted_iota(jnp.int32, (1, GROUP_WIDTH), 1) // HEAD
    tall = lax.broadcasted_iota(jnp.int32, (GROUP_WIDTH, 1), 0) // HEAD
    acs_last = jnp.sum(acs * (sub == CHUNK - 1).astype(F32), axis=0, keepdims=True)
    to_lanes = (lax.broadcasted_iota(jnp.int32, (CHUNK, GROUP_WIDTH), 0)
                == lax.broadcasted_iota(jnp.int32, (CHUNK, GROUP_WIDTH), 1) // HEAD).astype(F32)
    grow = _fdot(jnp.exp(acs), to_lanes, NN)
    keep = _fdot(jnp.exp(acs_last - acs) * dt, to_lanes, NN)
    w_parts, x_parts, skip, carry = [], [], 0.0, 0.0
    for j in range(HEADS_PER_GROUP):
        on_lane, on_sub = (lane == j).astype(F32), (sub == j).astype(F32)
        acs_c = jnp.sum(acs * on_lane, axis=1, keepdims=True)
        acs_r = jnp.sum(acs_t * on_sub, axis=0, keepdims=True)
        dt_r = jnp.sum(dt_t * on_sub, axis=0, keepdims=True)
        w_parts.append(cb * jnp.exp(jnp.where(causal, acs_c - acs_r, NEG)) * dt_r)
        x_parts.append(x * (wide == j).astype(F32))
        skip = skip + jnp.sum(dsk * on_lane, axis=1, keepdims=True) * (wide == j).astype(F32)
        carry = carry + jnp.sum(jnp.exp(acs_last) * on_lane, axis=1, keepdims=True) * (tall == j).astype(F32)
    y_diag = _bdot(jnp.concatenate(w_parts, axis=1), jnp.concatenate(x_parts, axis=0), NN)
    y = y_diag + _bdot(cm, h, NT) * grow + skip * x
    return y, h * carry + _bdot(x * keep, bm, TN)


GROUPS_PER_STEP = 2
SSD_STEPS = N_GROUPS // GROUPS_PER_STEP


def _ssd_specs(reverse):
    n_chunks = SEQ // CHUNK
    c_of = (lambda c: n_chunks - 1 - c) if reverse else (lambda c: c)
    x_w, n_w, dt_w = GROUPS_PER_STEP * GROUP_WIDTH, GROUPS_PER_STEP * N_STATE, GROUPS_PER_STEP * 128
    x_spec = pl.BlockSpec((CHUNK, x_w), lambda g, c: (c_of(c), g))
    b_spec = pl.BlockSpec((CHUNK, n_w), lambda g, c: (c_of(c), D_SSM // n_w + g))
    c_spec = pl.BlockSpec((CHUNK, n_w), lambda g, c: (c_of(c), (D_SSM + N_GROUPS * N_STATE) // n_w + g))
    dt_spec = pl.BlockSpec((CHUNK, dt_w), lambda g, c: (c_of(c), g))
    vec_spec = pl.BlockSpec((1, dt_w), lambda g, c: (0, g))
    h_spec = pl.BlockSpec((None, GROUPS_PER_STEP, GROUP_WIDTH, N_STATE), lambda g, c: (c_of(c), g, 0, 0))
    return x_spec, b_spec, c_spec, dt_spec, vec_spec, h_spec


def _group_slices(u):
    return pl.ds(u * GROUP_WIDTH, GROUP_WIDTH), pl.ds(u * N_STATE, N_STATE), pl.ds(u * 128, 128)


def _ssd_gated_chunk(x, bm, cm, dtr, bias, alog, dsk, h, z, g_out):
    y, h_new = _ssd_chunk(x, bm, cm, dtr, bias, alog, dsk, h)
    return _gate_fn(y, z, g_out)[0], h_new


def _ssd_gate_specs(reverse):
    x_spec = _ssd_specs(reverse)[0]
    z_block0 = 3 * D_ATTN // x_spec.block_shape[1]
    z_spec = pl.BlockSpec(x_spec.block_shape, lambda g, c: (x_spec.index_map(g, c)[0], z_block0 + g))
    return z_spec, pl.BlockSpec((1, x_spec.block_shape[1]), lambda g, c: (0, g))


def _ssd_fwd(xbc, dt_raw, bias, alog, dsk, proj, g_out):
    x_spec, b_spec, c_spec, dt_spec, vec_spec, h_spec = _ssd_specs(False)
    z_spec, g_spec = _ssd_gate_specs(False)

    def body(x_ref, b_ref, c_ref, dt_ref, bias_ref, alog_ref, dsk_ref, z_ref, g_ref, ssm_ref, hin_ref, h_scr):
        @pl.when(pl.program_id(1) == 0)
        def _():
            h_scr[...] = jnp.zeros_like(h_scr)

        for u in range(GROUPS_PER_STEP):
            xs, ns, ds = _group_slices(u)
            h = h_scr[u]
            hin_ref[u] = h
            ssm, h_scr[u] = _ssd_gated_chunk(x_ref[:, xs], b_ref[:, ns], c_ref[:, ns], dt_ref[:, ds], bias_ref[:, ds],
                                             alog_ref[:, ds], dsk_ref[:, ds], h, z_ref[:, xs], g_ref[:, xs])
            ssm_ref[:, xs] = ssm.astype(ssm_ref.dtype)

    return pl.pallas_call(
        body, name="ssd_fwd", grid=(SSD_STEPS, SEQ // CHUNK),
        in_specs=[x_spec, b_spec, c_spec, dt_spec, vec_spec, vec_spec, vec_spec, z_spec, g_spec],
        out_specs=[x_spec, h_spec],
        out_shape=[jax.ShapeDtypeStruct((SEQ, D_SSM), BF16),
                   jax.ShapeDtypeStruct((SEQ // CHUNK, N_GROUPS, GROUP_WIDTH, N_STATE), F32)],
        scratch_shapes=[pltpu.VMEM((GROUPS_PER_STEP, GROUP_WIDTH, N_STATE), F32)],
        compiler_params=_params(("parallel", "arbitrary")),
    )(xbc, xbc, xbc, dt_raw, bias, alog, dsk, proj, g_out)


def _ssd_bwd(xbc, dt_raw, bias, alog, dsk, h_in, proj, g_out, dmix):
    x_spec, b_spec, c_spec, dt_spec, vec_spec, h_spec = _ssd_specs(True)
    z_spec, g_spec = _ssd_gate_specs(True)
    ct_block0 = D_ATTN // x_spec.block_shape[1]
    ct_spec = pl.BlockSpec(x_spec.block_shape, lambda g, c: (x_spec.index_map(g, c)[0], ct_block0 + g))

    def body(x_ref, b_ref, c_ref, dt_ref, bias_ref, alog_ref, dsk_ref, hin_ref, z_ref, g_ref, ct_ref,
             dx_ref, db_ref, dc_ref, ddt_ref, dbias_ref, dalog_ref, ddsk_ref, dz_ref, dg_ref, dh_scr):
        first = pl.program_id(1) == 0

        @pl.when(first)
        def _():
            dh_scr[...] = jnp.zeros_like(dh_scr)

        for u in range(GROUPS_PER_STEP):
            xs, ns, ds = _group_slices(u)
            _, pullback = jax.vjp(_ssd_gated_chunk, x_ref[:, xs], b_ref[:, ns], c_ref[:, ns], dt_ref[:, ds], bias_ref[:, ds],
                                  alog_ref[:, ds], dsk_ref[:, ds], hin_ref[u], z_ref[:, xs], g_ref[:, xs])
            g = pullback((ct_ref[:, xs], dh_scr[u]))
            dx_ref[:, xs], db_ref[:, ns], dc_ref[:, ns] = g[0], g[1], g[2]
            ddt_ref[:, ds] = g[3].astype(ddt_ref.dtype)
            dh_scr[u] = g[7]
            dz_ref[:, xs] = g[8].astype(dz_ref.dtype)
            sums = ((dbias_ref, g[4], ds), (dalog_ref, g[5], ds), (ddsk_ref, g[6], ds),
                    (dg_ref, jnp.sum(g[9], axis=0, keepdims=True), xs))
            for o_ref, val, lanes in sums:
                @pl.when(first)
                def _(o_ref=o_ref, val=val, lanes=lanes):
                    o_ref[:, lanes] = val

                @pl.when(jnp.logical_not(first))
                def _(o_ref=o_ref, val=val, lanes=lanes):
                    o_ref[:, lanes] += val

    n_chunks = SEQ // CHUNK
    out_b = pl.BlockSpec((CHUNK, GROUPS_PER_STEP * N_STATE), lambda g, c: (n_chunks - 1 - c, g))
    return pl.pallas_call(
        body, name="ssd_bwd", grid=(SSD_STEPS, n_chunks),
        in_specs=[x_spec, b_spec, c_spec, dt_spec, vec_spec, vec_spec, vec_spec, h_spec, z_spec, g_spec, ct_spec],
        out_specs=[x_spec, out_b, out_b, dt_spec, vec_spec, vec_spec, vec_spec, x_spec, g_spec],
        out_shape=[jax.ShapeDtypeStruct((SEQ, D_SSM), F32), jax.ShapeDtypeStruct((SEQ, N_GROUPS * N_STATE), F32),
                   jax.ShapeDtypeStruct((SEQ, N_GROUPS * N_STATE), F32), jax.ShapeDtypeStruct((SEQ, DT_PAD), BF16),
                   jax.ShapeDtypeStruct((1, DT_PAD), F32), jax.ShapeDtypeStruct((1, DT_PAD), F32),
                   jax.ShapeDtypeStruct((1, DT_PAD), F32), jax.ShapeDtypeStruct((SEQ, D_SSM), BF16),
                   jax.ShapeDtypeStruct((1, D_SSM), F32)],
        scratch_shapes=[pltpu.VMEM((GROUPS_PER_STEP, GROUP_WIDTH, N_STATE), F32)],
        compiler_params=_params(("parallel", "arbitrary")),
    )(xbc, xbc, xbc, dt_raw, bias, alog, dsk, h_in, proj, g_out, dmix)


CROSS_HEAD = 128
CROSS_ROWS = 512


def _cross_head(q, k, v, gq, gk):
    qn = _rms(q, gq) * (CROSS_HEAD ** -0.5)
    kn = _rms(k, gk)
    s = _bdot(qn, kn, NT)
    p = jnp.exp(s - lax.stop_gradient(jnp.max(s, axis=-1, keepdims=True)))
    return _bdot(p, v, NN) * (1.0 / jnp.sum(p, axis=-1, keepdims=True))


def _cross_specs():
    q_spec = pl.BlockSpec((CROSS_ROWS, CROSS_HEAD), lambda h, i: (i, h))
    k_spec = pl.BlockSpec((N_MEM, CROSS_HEAD), lambda h, i: (0, h))
    v_spec = pl.BlockSpec((N_MEM, CROSS_HEAD), lambda h, i: (0, 4 + h))
    g_spec = pl.BlockSpec((1, CROSS_HEAD), lambda h, i: (0, 0))
    return q_spec, k_spec, v_spec, g_spec


def _cross_fwd(qc, kv, gq, gk):
    q_spec, k_spec, v_spec, g_spec = _cross_specs()

    def body(q_ref, k_ref, v_ref, gq_ref, gk_ref, o_ref):
        o_ref[...] = _cross_head(q_ref[...], k_ref[...], v_ref[...], gq_ref[...], gk_ref[...]).astype(o_ref.dtype)

    return pl.pallas_call(
        body, name="cross_fwd", grid=(4, SEQ // CROSS_ROWS),
        in_specs=[q_spec, k_spec, v_spec, g_spec, g_spec], out_specs=q_spec,
        out_shape=jax.ShapeDtypeStruct((SEQ, D_CROSS), BF16),
        compiler_params=_params(("parallel", "parallel")),
    )(qc, kv, kv, gq, gk)


def _cross_bwd(qc, kv, gq, gk, do):
    q_spec, k_spec, v_spec, g_spec = _cross_specs()

    def body(q_ref, k_ref, v_ref, gq_ref, gk_ref, do_ref, dq_ref, dk_ref, dv_ref, dgq_ref, dgk_ref):
        _, pullback = jax.vjp(_cross_head, q_ref[...], k_ref[...], v_ref[...], gq_ref[...], gk_ref[...])
        dq, dk, dv, dgq, dgk = pullback(do_ref[...].astype(F32))
        dq_ref[...] = dq.astype(dq_ref.dtype)
        row0 = pl.program_id(1) == 0
        all0 = jnp.logical_and(row0, pl.program_id(0) == 0)
        for o_ref, val, init in ((dk_ref, dk, row0), (dv_ref, dv, row0), (dgq_ref, dgq, all0), (dgk_ref, dgk, all0)):
            @pl.when(init)
            def _(o_ref=o_ref, val=val):
                o_ref[...] = val

            @pl.when(jnp.logical_not(init))
            def _(o_ref=o_ref, val=val):
                o_ref[...] += val

    return pl.pallas_call(
        body, name="cross_bwd", grid=(4, SEQ // CROSS_ROWS),
        in_specs=[q_spec, k_spec, v_spec, g_spec, g_spec, q_spec],
        out_specs=[q_spec, k_spec, k_spec, g_spec, g_spec],
        out_shape=[jax.ShapeDtypeStruct((SEQ, D_CROSS), BF16), jax.ShapeDtypeStruct((N_MEM, D_CROSS), F32),
                   jax.ShapeDtypeStruct((N_MEM, D_CROSS), F32), jax.ShapeDtypeStruct((1, CROSS_HEAD), F32),
                   jax.ShapeDtypeStruct((1, CROSS_HEAD), F32)],
        compiler_params=_params(("arbitrary", "arbitrary")),
    )(qc, kv, kv, gq, gk, do)


def _loss_epilogue(acc, residual, target):
    err = acc + residual - target
    dy = err * (1.0 / D_MODEL)
    part = jnp.sum(jnp.sum(err * err, axis=1, keepdims=True), axis=0, keepdims=True) * (0.5 / D_MODEL)
    return dy, dy, part


def _pad_heads(v):
    return jnp.pad(v.reshape(N_GROUPS, HEADS_PER_GROUP), ((0, 0), (0, 128 - HEADS_PER_GROUP))).reshape(1, DT_PAD)


def _unpad_heads(v):
    return v.reshape(v.shape[0], N_GROUPS, 128)[:, :, :HEADS_PER_GROUP].reshape(v.shape[0], N_DT)


def _rope_tables(positions):
    half = ROT // 2
    inv_freq = ROPE_THETA ** (-2.0 * jnp.arange(half, dtype=F32) / ROT)
    ang = positions.reshape(SEQ, 1).astype(F32) * inv_freq
    cos, sin = jnp.cos(ang), jnp.sin(ang)
    ones, zeros = jnp.ones((SEQ, HEAD - ROT), F32), jnp.zeros((SEQ, HEAD - ROT), F32)
    cos_h = jnp.concatenate([cos, cos, ones], axis=1)
    sin_h = jnp.concatenate([-sin, sin, zeros], axis=1)
    return jnp.tile(cos_h, (1, 2)), jnp.tile(sin_h, (1, 2))


def _add_res(acc, res):
    return (acc + res,)


def _settle(grads, *after):
    if hasattr(grads, "settle"):
        grads.settle(*after)


def _side_matmul(grads, matrix, *args, **kw):
    side = grads.side_inputs(matrix) if hasattr(grads, "side_inputs") else None
    if side is None:
        return _matmul(*args, **kw)
    *res, g, delta, new_m, new_v = _matmul(*args, side_adamw=side, **kw)
    grads.side_done(matrix, (g, delta, new_m, new_v))
    return res[0] if len(res) == 1 else res


def _take_token(grads):
    token = getattr(grads, "token", None)
    if token is None:
        return ()
    grads.token = None
    return (token,)


def _local_step(x, mem, positions, target, p, w, more_weights=None, grads=None, h=None):
    grads = {} if grads is None else grads
    w = dict(w)
    cos, sin = _rope_tables(positions)
    gq2, gk2 = jnp.tile(p["g_q"], (1, 2)), jnp.tile(p["g_k"], (1, 2))
    bias, alog, dsk = _pad_heads(p["dt_bias"]), _pad_heads(p["a_log"]), _pad_heads(p["d_skip"])
    norm_out = [(D_MODEL, BF16, D_MODEL, 0, False)]

    if h is None:
        h = _rowwise(_norm_fn, [_full(x)], [_full(p["g_mix"])], norm_out, name="norm_in")[0]
    proj = _matmul(h, w["w_in"], mode="nn", name="in_proj", outs=[F32], n_cols=D_MAIN)
    dt_raw = _matmul(h, w["w_dt"], mode="nn", name="dt_proj", outs=[F32])
    pairs = D_ATTN // 128
    qk_rows = [(proj, 128, 0, True), (proj, 128, pairs, True), (proj, 128, 2 * pairs, True), _full(cos), _full(sin)]
    qk_vecs = [_full(gq2), _full(gk2)]
    qn, kn, vn = _rowwise(_qk_fn, qk_rows, qk_vecs, [(D_ATTN, F32, 128, 0, True)] * 3, name="qk_prep", groups=8, tr=1024)
    branches = [_attention_fwd(qn, kn, vn, b) for b in range(3)]
    merge_rows = [_full(o) for o, _ in branches] + [_full(lse) for _, lse in branches]
    attn = _rowwise(_merge_fn, merge_rows, [_full(p["g_attn_out"])], [(D_ATTN, BF16, D_ATTN, 0, False)], name="attn_merge")[0]
    xbc = _conv_fwd(proj, p["conv_w"], p["conv_b"])
    ssm, h_in = _ssd_fwd(xbc, dt_raw, bias, alog, dsk, proj, p["g_ssm_out"])
    mix = jnp.concatenate([attn, ssm], axis=1)
    if more_weights is not None:
        w.update(more_weights("mixer_done", mix))
    x1 = _matmul(mix, w["w_out"], mode="nn", name="out_proj", outs=[F32], extra=(x,), epilogue=_add_res)
    hc = _rowwise(_norm_fn, [_full(x1)], [_full(p["g_cross"])], norm_out, name="norm_cross")[0]
    memh = _rowwise(_norm_fn, [_full(mem)], [_full(p["g_mem"])], norm_out, name="norm_mem", n_rows=N_MEM, tr=N_MEM)[0]
    qc = _matmul(hc, w["w_cq"], mode="nn", name="cq_proj", outs=[F32])
    if more_weights is not None:
        w.update(more_weights("cross_started", qc))
    kv = _matmul(memh, w["w_ckv"], mode="nn", name="ckv_proj", outs=[F32])
    oc = _cross_fwd(qc, kv, p["g_cq"], p["g_ck"])
    x2 = _matmul(oc, w["w_co"], mode="nn", name="co_proj", outs=[F32], extra=(x1,), epilogue=_add_res)
    hm = _rowwise(_norm_fn, [_full(x2)], [_full(p["g_mlp"])], norm_out, name="norm_mlp")[0]
    if more_weights is not None:
        w.update(more_weights("cross_done", hm))
    u, act = _matmul(hm, w["w_up"], mode="nn", name="up_proj", outs=[F32, BF16],
                     epilogue=lambda acc: (acc, jnp.square(jnp.maximum(acc, 0.0))))
    dy, dyb, loss_tiles = _matmul(act, w["w_down"], mode="nn", name="down_proj", outs=[F32, BF16], extra=(x2, target),
                                  epilogue=_loss_epilogue, tile_sums=1)
    loss = jnp.sum(loss_tiles).reshape(1, 1)

    grads["w_down"] = _matmul(act, dyb, mode="tn", name="dw_down", outs=[BF16], after=_take_token(grads))
    du = _matmul(dyb, w["w_down"], mode="nt", name="d_act", outs=[BF16], extra=(u,), after=_take_token(grads),
                 epilogue=lambda acc, uu: (acc * (2.0 * jnp.maximum(uu, 0.0)),))
    _settle(grads, du)
    grads["w_up"] = _matmul(hm, du, mode="tn", name="dw_up", outs=[BF16], col_shards=4, after=_take_token(grads))
    dhm = _matmul(du, w["w_up"], mode="nt", name="d_hm", outs=[F32], after=_take_token(grads), tk=4096)
    _settle(grads, dhm)
    dx2, grads["g_mlp"] = _rowwise_vjp(
        _norm_fn, [_full(x2)], [_full(p["g_mlp"])], [[_full(dhm)]],
        [(0, D_MODEL, F32, D_MODEL, 0, False, _full(dy))], [(0, D_MODEL, D_MODEL, 0, False)], name="norm_mlp_bwd")
    grads["w_co"] = _matmul(oc, dx2, mode="tn", name="dw_co", outs=[BF16], col_shards=4, after=_take_token(grads))
    doc = _matmul(dx2, w["w_co"], mode="nt", name="d_oc", outs=[BF16])
    dqc, dkc, dvc, grads["g_cq"], grads["g_ck"] = _cross_bwd(qc, kv, p["g_cq"], p["g_ck"], doc)
    grads["w_cq"] = _matmul(hc, dqc, mode="tn", name="dw_cq", outs=[BF16])
    dhc = _matmul(dqc, w["w_cq"], mode="nt", name="d_hc", outs=[F32])
    dkv = jnp.concatenate([dkc, dvc], axis=1)
    grads["w_ckv"] = _matmul(memh, dkv, mode="tn", name="dw_ckv", outs=[BF16])
    dmemh = _matmul(dkv, w["w_ckv"], mode="nt", name="d_memh", outs=[F32])
    grads["g_mem"] = _rowwise_vjp(_norm_fn, [_full(mem)], [_full(p["g_mem"])], [[_full(dmemh)]], [],
                                  [(0, D_MODEL, D_MODEL, 0, False)], name="norm_mem_bwd", n_rows=N_MEM, tr=N_MEM)[0]
    dx1, grads["g_cross"] = _rowwise_vjp(
        _norm_fn, [_full(x1)], [_full(p["g_cross"])], [[_full(dhc)]],
        [(0, D_MODEL, F32, D_MODEL, 0, False, _full(dx2))], [(0, D_MODEL, D_MODEL, 0, False)], name="norm_cross_bwd")
    grads["w_out"] = _matmul(mix, dx1, mode="tn", name="dw_out", outs=[BF16])
    dmix = _matmul(dx1, w["w_out"], mode="nt", name="d_mix", outs=[F32], after=_take_token(grads))
    _settle(grads, dmix)
    if hasattr(grads, "prepare_side"):
        grads.prepare_side("w_down", dmix)
    merge_grads = [(i, D_ATTN, F32, D_ATTN, 0, False, None) for i in range(6)]
    *dol, grads["g_attn_out"] = _rowwise_vjp(
        _merge_fn, merge_rows, [_full(p["g_attn_out"])], [[(dmix, D_ATTN, 0, False)]],
        merge_grads, [(0, D_ATTN, D_ATTN, 0, False)], name="attn_merge_bwd", tr=256, after=_take_token(grads))
    dqkv = [_attention_bwd(qn, kn, vn, *branches[b], dol[b], dol[3 + b], b) for b in range(3)]
    qk_cts = [[(dqkv[b][i], 128, 0, True) for b in range(3)] for i in range(3)]
    dq, dk, dv, dgq2, dgk2 = _rowwise_vjp(
        _qk_fn, qk_rows, qk_vecs, qk_cts, [(i, D_ATTN, BF16, 128, 0, True, None) for i in range(3)],
        [(0, 128, 128, 0, False), (1, 128, 128, 0, False)], name="qk_prep_bwd", groups=8, tr=512)
    grads["g_q"] = dgq2[:, :HEAD] + dgq2[:, HEAD:]
    grads["g_k"] = dgk2[:, :HEAD] + dgk2[:, HEAD:]
    dxs, db, dc, ddt, dbias, dalog, ddsk, dz, grads["g_ssm_out"] = _ssd_bwd(xbc, dt_raw, bias, alog, dsk, h_in, proj,
                                                                             p["g_ssm_out"], dmix)
    grads["dt_bias"], grads["a_log"], grads["d_skip"] = _unpad_heads(dbias), _unpad_heads(dalog), _unpad_heads(ddsk)
    dxbc_raw, dconv_w, grads["conv_b"] = _conv_bwd(proj, p["conv_w"], p["conv_b"], dxs, db, dc)
    grads["conv_w"] = dconv_w[:4]
    dproj = jnp.concatenate([dq, dk, dv, dz, dxbc_raw], axis=1)
    if hasattr(grads, "prepare_side"):
        grads.prepare_side("w_up", dproj)
    grads["w_main"] = _side_matmul(grads, "w_down", h, dproj, mode="tn", name="dw_main", outs=[BF16],
                                   out_cols=D_MAIN + N_DT, tn=768)
    grads["w_dt"] = _matmul(h, ddt, mode="tn", name="dw_dt", outs=[BF16])
    dh = _side_matmul(grads, "w_up", dproj, w["w_in"], mode="nt", name="d_h_main", outs=[F32], after=_take_token(grads),
                      tm=512, tk=3072)
    dh = _matmul(ddt, w["w_dt"], mode="nt", name="d_h_dt", outs=[F32], extra=(dh,), epilogue=_add_res)
    grad_x, grads["g_mix"] = _rowwise_vjp(
        _norm_fn, [_full(x)], [_full(p["g_mix"])], [[_full(dh)]],
        [(0, D_MODEL, F32, D_MODEL, 0, False, _full(dx1))], [(0, D_MODEL, D_MODEL, 0, False)], name="norm_in_bwd")
    return loss, grad_x, grads


MATRICES = ("w_in", "w_out", "w_cq", "w_ckv", "w_co", "w_up", "w_down")
ROW_SHARDED = ("w_out", "w_cq", "w_ckv", "w_down")
N_CHIPS = 4
ANY = pl.BlockSpec(memory_space=pl.ANY)


def _place():
    return lax.axis_index("x"), lax.axis_index("y"), lax.axis_index("c")


def _other_chips(x, y):
    return [(1 - x, y), (x, 1 - y), (1 - x, 1 - y)]


def _remote(src, dst, send_sem, recv_sem, device):
    return pltpu.make_async_remote_copy(src_ref=src, dst_ref=dst, send_sem=send_sem, recv_sem=recv_sem,
                                        device_id=device, device_id_type=MESH)


def _gathered_shape(name, shard):
    rows, cols = shard.shape
    if name == "w_in":
        return (N_CHIPS, rows, cols)
    return (N_CHIPS * rows, cols) if name in ROW_SHARDED else (rows, N_CHIPS * cols)


def _shard_window(name, ref, rows, cols, chip, half):
    r0, nr = (0, rows) if half is None else (half * (rows // 2), rows // 2)
    if name == "w_in":
        return ref.at[chip, pl.ds(r0, nr), :]
    if name in ROW_SHARDED:
        return ref.at[pl.ds(chip * rows + r0, nr), :]
    return ref.at[pl.ds(r0, nr), pl.ds(pl.multiple_of(chip * cols, 128), cols)]


def _cast_into_gathered(w, name, chip, after=()):
    rows, cols = w.shape
    tr = _tile(rows, ROW_TILE)

    def body(chip_ref, w_ref, *rest):
        rest[-1][...] = w_ref[...].astype(BF16)

    if name == "w_in":
        out_spec = pl.BlockSpec((None, tr, cols), lambda i, chip_ref: (chip_ref[0], i, 0))
    elif name in ROW_SHARDED:
        out_spec = pl.BlockSpec((tr, cols), lambda i, chip_ref: (chip_ref[0] * (rows // tr) + i, 0))
    else:
        out_spec = pl.BlockSpec((tr, cols), lambda i, chip_ref: (i, chip_ref[0]))
    grid_spec = pltpu.PrefetchScalarGridSpec(
        num_scalar_prefetch=1, grid=(rows // tr,),
        in_specs=[pl.BlockSpec((tr, cols), lambda i, chip_ref: (i, 0))] + [pl.BlockSpec(memory_space=pl.ANY)] * len(after),
        out_specs=out_spec)
    return pl.pallas_call(body, name="cast_" + name, grid_spec=grid_spec,
                          out_shape=jax.ShapeDtypeStruct(_gathered_shape(name, w), BF16),
                          compiler_params=_params(("parallel",)))(chip.reshape(1).astype(jnp.int32), w, *after)


def _w_in_columns(arr, to_shards):
    rows, piece = D_MODEL, (D_MAIN + N_DT) // N_CHIPS
    tr = ROW_TILE

    def body(a_ref, o_ref):
        for j in range(N_CHIPS):
            if to_shards:
                o_ref[j] = a_ref[:, pl.ds(piece * j, piece)]
            else:
                o_ref[:, pl.ds(piece * j, piece)] = a_ref[j]

    pieces = pl.BlockSpec((N_CHIPS, tr, piece), lambda i: (0, i, 0))
    matrix = pl.BlockSpec((tr, N_CHIPS * piece), lambda i: (i, 0))
    out_dims = (N_CHIPS, rows, piece) if to_shards else (rows, N_CHIPS * piece)
    return pl.pallas_call(
        body, name="w_in_to_shards" if to_shards else "w_in_from_shards", grid=(rows // tr,),
        in_specs=[matrix if to_shards else pieces], out_specs=pieces if to_shards else matrix,
        out_shape=jax.ShapeDtypeStruct(out_dims, arr.dtype), compiler_params=_params(("parallel",)))(arr)


HBM = pl.BlockSpec(memory_space=pltpu.HBM)
SEM = pl.BlockSpec(memory_space=pltpu.SEMAPHORE)
EFFECT = pltpu.SideEffectType.DATAFLOW_SIDE_EFFECTING


def _split_start(name, bufs, plan, counts, after=()):
    n, n_g, n_after = len(bufs), len(counts), len(after)

    def body(*refs):
        ins, sems, token = refs[:n], refs[n + n_after:n + n_after + 2 * n_g], refs[-1]
        for g, copies in enumerate(plan(ins)):
            for i, (src, dst, device, _) in enumerate(copies):
                _remote(src, dst, sems[2 * g].at[i], sems[2 * g + 1].at[i], device).start()
        token[...] = jnp.zeros_like(token)

    sem_shapes = [pltpu.SemaphoreType.DMA((cnt,)) for cnt in counts for _ in range(2)]
    res = pl.pallas_call(
        body, name=name,
        out_shape=(*sem_shapes, *[pltpu.HBM(b.shape, b.dtype) for b in bufs], jax.ShapeDtypeStruct((8, 128), F32)),
        in_specs=(*(HBM,) * n, *(ANY,) * n_after),
        out_specs=(*(SEM,) * (2 * n_g), *(HBM,) * n, pl.BlockSpec(memory_space=pltpu.VMEM)),
        input_output_aliases={i: 2 * n_g + i for i in range(n)},
        compiler_params=pltpu.CompilerParams(has_side_effects=EFFECT),
    )(*[pltpu.with_memory_space_constraint(b, pltpu.HBM) for b in bufs], *after)
    sems = [(res[2 * g], res[2 * g + 1]) for g in range(n_g)]
    return sems, list(res[2 * n_g:2 * n_g + n]), res[-1]


def _split_wait(name, bufs, sems, plan, *after):
    n = len(bufs)

    def body(*refs):
        ins, send, recv = refs[:n], refs[n], refs[n + 1]
        (copies,) = plan(ins)
        for i, (src, _, device, landing) in enumerate(copies):
            cp = _remote(src, landing, send.at[i], recv.at[i], device)
            cp.wait_send()
            cp.wait_recv()

    res = pl.pallas_call(
        body, name=name, out_shape=tuple(pltpu.HBM(b.shape, b.dtype) for b in bufs),
        in_specs=(*(HBM,) * n, SEM, SEM, *(ANY,) * len(after)), out_specs=(HBM,) * n,
        input_output_aliases={i: i for i in range(n)},
        compiler_params=pltpu.CompilerParams(has_side_effects=EFFECT),
    )(*bufs, sems[0], sems[1], *after)
    return list(res)


def _ici_plan(names, shard_shapes):
    def plan(refs):
        x, y, c = _place()
        copies = []
        for ref, name in zip(refs, names):
            win = _shard_window(name, ref, *shard_shapes[name], 2 * x + y, c)
            for px, py in _other_chips(x, y):
                copies.append((win, win, (px, py, c), _shard_window(name, ref, *shard_shapes[name], 2 * px + py, c)))
        return [copies]
    return plan


def _pass_on_plan(names, shard_shapes):
    def plan(refs):
        x, y, c = _place()
        copies = []
        for ref, name in zip(refs, names):
            for px, py in _other_chips(x, y):
                win = _shard_window(name, ref, *shard_shapes[name], 2 * px + py, c)
                copies.append((win, win, (x, y, 1 - c), _shard_window(name, ref, *shard_shapes[name], 2 * px + py, 1 - c)))
        return [copies]
    return plan


def _swap_plan(n_pairs):
    def plan(refs):
        x, y, c = _place()
        return [[(src.at[:, 1 - c], dst, (x, y, 1 - c), dst) for src, dst in zip(refs[:n_pairs], refs[n_pairs:])]]
    return plan


def _share_plan(n_pairs):
    def plan(refs):
        x, y, c = _place()
        return [[(src, dst, (x, y, 1 - c), dst)] for src, dst in zip(refs[:n_pairs], refs[n_pairs:])]
    return plan


def _scatter_plan(n_pairs):
    def plan(refs):
        x, y, c = _place()
        copies = []
        for src, dst in zip(refs[:n_pairs], refs[n_pairs:]):
            for k, (px, py) in enumerate(_other_chips(x, y)):
                copies.append((src.at[2 * px + py], dst.at[k], (px, py, c), dst.at[k]))
        return [copies]
    return plan


def _sibling_swap(arrs, name):
    n = len(arrs)

    def body(*refs):
        ins, outs, send, recv = refs[:n], refs[n:2 * n], refs[2 * n], refs[2 * n + 1]
        x, y, c = _place()
        cps = [_remote(ins[w].at[:, 1 - c], outs[w], send.at[w], recv.at[w], (x, y, 1 - c)) for w in range(n)]
        for cp in cps:
            cp.start()
        for cp in cps:
            cp.wait()

    return pl.pallas_call(
        body, name=name, in_specs=[ANY] * n, out_specs=[ANY] * n,
        out_shape=[jax.ShapeDtypeStruct((a.shape[0],) + a.shape[2:], a.dtype) for a in arrs],
        scratch_shapes=[pltpu.SemaphoreType.DMA((n,))] * 2,
    )(*arrs)


def _small_allreduce(buf, name, after=()):
    rows = buf.shape[0]

    def body(x_ref, *rest):
        out_ref, all_ref, send_sems, recv_sems, local_sem = rest[len(after):]
        x, y, c = _place()
        me, sibling, chips = (x, y, c), (x, y, 1 - c), _other_chips(x, y)

        def block(px, py, pc):
            return all_ref.at[pl.ds((4 * px + 2 * py + pc) * rows, rows), :]

        def copy(k, blk, to, src=None):
            return _remote(block(*blk) if src is None else src, block(*blk), send_sems.at[k], recv_sems.at[k], to)

        own = pltpu.make_async_copy(x_ref, block(*me), local_sem)
        own.start()
        first = [copy(0, me, sibling, src=x_ref)] + [copy(1 + j, me, (*chip, c), src=x_ref) for j, chip in enumerate(chips)]
        for cp in first:
            cp.start()
        passed = [copy(4 + j, (*chip, c), sibling) for j, chip in enumerate(chips)]
        for j, chip in enumerate(chips):
            copy(1 + j, (*chip, c), me).wait_recv()
            passed[j].start()
        copy(0, sibling, me).wait_recv()
        for j, chip in enumerate(chips):
            copy(4 + j, (*chip, 1 - c), me).wait_recv()
        for cp in first + passed:
            cp.wait_send()
        own.wait()
        acc = all_ref[pl.ds(0, rows), :]
        for d in range(1, 8):
            acc = acc + all_ref[pl.ds(d * rows, rows), :]
        out_ref[...] = acc

    vmem = pl.BlockSpec(memory_space=pltpu.VMEM)
    return pl.pallas_call(
        body, name=name, in_specs=[vmem] + [ANY] * len(after), out_specs=vmem,
        out_shape=jax.ShapeDtypeStruct(buf.shape, F32),
        scratch_shapes=[pltpu.VMEM((8 * rows, 128), F32), pltpu.SemaphoreType.DMA((7,)), pltpu.SemaphoreType.DMA((7,)),
                        pltpu.SemaphoreType.DMA],
    )(buf, *after)


ROW_TILE = 256
BIG_ROW_TILE = 1024


def _add_halves(arr, recv, c, name):
    _, _, hr, cols = arr.shape
    tr = _tile(hr, BIG_ROW_TILE)

    def body(c_ref, a_ref, r_ref, o_ref):
        o_ref[...] = (a_ref[...].astype(F32) + r_ref[...].astype(F32)).astype(o_ref.dtype)

    piece = pl.BlockSpec((None, tr, cols), lambda j, i, c_ref: (j, i, 0))
    grid_spec = pltpu.PrefetchScalarGridSpec(
        num_scalar_prefetch=1, grid=(N_CHIPS, hr // tr),
        in_specs=[pl.BlockSpec((None, None, tr, cols), lambda j, i, c_ref: (j, c_ref[0], i, 0)), piece], out_specs=piece)
    return pl.pallas_call(body, name=name, grid_spec=grid_spec, out_shape=jax.ShapeDtypeStruct(recv.shape, BF16),
                          compiler_params=_params(("parallel", "parallel")))(c.reshape(1).astype(jnp.int32), arr, recv)


def _flip_slot(d):
    return jnp.where(d == 1, 1, jnp.where(d == 3, 2, 0))


def _sum_chips(p, q, chip, name):
    _, hr, cols = p.shape
    tr = _tile(hr, BIG_ROW_TILE)

    def body(chip_ref, p_ref, q_ref, o_ref):
        j = pl.program_id(1)
        term = jnp.where(j == chip_ref[0], p_ref[...].astype(F32), q_ref[...].astype(F32))

        @pl.when(j == 0)
        def _():
            o_ref[...] = term

        @pl.when(j != 0)
        def _():
            o_ref[...] += term

    grid_spec = pltpu.PrefetchScalarGridSpec(
        num_scalar_prefetch=1, grid=(hr // tr, N_CHIPS),
        in_specs=[pl.BlockSpec((None, tr, cols), lambda i, j, chip_ref: (chip_ref[0], i, 0)),
                  pl.BlockSpec((None, tr, cols), lambda i, j, chip_ref: (_flip_slot(j ^ chip_ref[0]), i, 0))],
        out_specs=pl.BlockSpec((tr, cols), lambda i, j, chip_ref: (i, 0)))
    return pl.pallas_call(body, name=name, grid_spec=grid_spec, out_shape=jax.ShapeDtypeStruct((hr, cols), F32),
                          compiler_params=_params(("parallel", "arbitrary")))(chip.reshape(1).astype(jnp.int32), p, q)


def _adamw_halves(w, g_own, g_other, m, v, c, name):
    rows, cols = w.shape
    tr = _tile(rows // 2, ROW_TILE)
    per_half = rows // 2 // tr

    def body(c_ref, w_ref, own_ref, other_ref, m_ref, v_ref, g_ref, d_ref, nm_ref, nv_ref):
        mine = (pl.program_id(0) // per_half) == c_ref[0]
        g_ = jnp.where(mine, own_ref[...], other_ref[...])
        g_ref[...] = g_
        d_ref[...], nm_ref[...], nv_ref[...] = _adamw_math(w_ref[...], g_, m_ref[...], v_ref[...])

    blk = pl.BlockSpec((tr, cols), lambda i, c_ref: (i, 0))
    own = pl.BlockSpec((tr, cols), lambda i, c_ref: (jnp.where(i // per_half == c_ref[0], i % per_half, 0), 0))
    other = pl.BlockSpec((tr, cols), lambda i, c_ref: (jnp.where(i // per_half == c_ref[0], 0, i % per_half), 0))
    grid_spec = pltpu.PrefetchScalarGridSpec(num_scalar_prefetch=1, grid=(rows // tr,),
                                             in_specs=[blk, own, other, blk, blk], out_specs=[blk] * 4)
    return pl.pallas_call(body, name=name, grid_spec=grid_spec, out_shape=[jax.ShapeDtypeStruct(w.shape, F32)] * 4,
                          compiler_params=_params(("parallel",)))(c.reshape(1).astype(jnp.int32), w, g_own, g_other, m, v)


W_IN_COLS = (D_MAIN + N_DT) // N_CHIPS
W_IN_MAIN = W_IN_COLS // 128 * 128
W_IN_TAIL = W_IN_COLS - 128
W_IN_PARTS = ((0, W_IN_MAIN), (W_IN_TAIL, 128))


def _cast_w_in_transposed(w_t, chip, after=()):
    def body(chip_ref, w_ref, *rest):
        for start, size in W_IN_PARTS:
            rest[-1][:, pl.ds(start, size)] = w_ref[pl.ds(start, size), :].T.astype(BF16)

    grid_spec = pltpu.PrefetchScalarGridSpec(
        num_scalar_prefetch=1, grid=(D_MODEL // ROW_TILE,),
        in_specs=[pl.BlockSpec((W_IN_COLS, ROW_TILE), lambda i, chip_ref: (0, i))] + [pl.BlockSpec(memory_space=pl.ANY)] * len(after),
        out_specs=pl.BlockSpec((None, ROW_TILE, W_IN_COLS), lambda i, chip_ref: (chip_ref[0], i, 0)))
    return pl.pallas_call(body, name="cast_w_in", grid_spec=grid_spec,
                          out_shape=jax.ShapeDtypeStruct((N_CHIPS, D_MODEL, W_IN_COLS), BF16),
                          compiler_params=_params(("parallel",)))(chip.reshape(1).astype(jnp.int32), w_t, *after)


def _adamw_w_in_transposed(w_t, g_own, g_other, m_t, v_t, c):
    per_half = D_MODEL // 2 // ROW_TILE

    def body(c_ref, w_ref, own_ref, other_ref, m_ref, v_ref, g_ref, d_ref, nm_ref, nv_ref):
        mine = (pl.program_id(0) // per_half) == c_ref[0]
        for start, size in W_IN_PARTS:
            cols, rows = pl.ds(start, size), pl.ds(start, size)
            g_ = jnp.where(mine, own_ref[:, cols], other_ref[:, cols]).T
            g_ref[rows, :] = g_
            d_ref[rows, :], nm_ref[rows, :], nv_ref[rows, :] = _adamw_math(w_ref[rows, :], g_, m_ref[rows, :], v_ref[rows, :])

    blk = pl.BlockSpec((W_IN_COLS, ROW_TILE), lambda i, c_ref: (0, i))
    own = pl.BlockSpec((ROW_TILE, W_IN_COLS), lambda i, c_ref: (jnp.where(i // per_half == c_ref[0], i % per_half, 0), 0))
    other = pl.BlockSpec((ROW_TILE, W_IN_COLS), lambda i, c_ref: (jnp.where(i // per_half == c_ref[0], 0, i % per_half), 0))
    grid_spec = pltpu.PrefetchScalarGridSpec(num_scalar_prefetch=1, grid=(D_MODEL // ROW_TILE,),
                                             in_specs=[blk, own, other, blk, blk], out_specs=[blk] * 4)
    return pl.pallas_call(body, name="adamw_w_in", grid_spec=grid_spec, out_shape=[jax.ShapeDtypeStruct(w_t.shape, F32)] * 4,
                          compiler_params=_params(("parallel",)))(c.reshape(1).astype(jnp.int32), w_t, g_own, g_other, m_t, v_t)


def _adamw_math(w, g, m, v):
    m_new = ADAM_B1 * m + (1.0 - ADAM_B1) * g
    v_new = ADAM_B2 * v + (1.0 - ADAM_B2) * (g * g)
    m_hat = m_new / (1.0 - ADAM_B1 ** ADAM_STEP)
    v_hat = v_new / (1.0 - ADAM_B2 ** ADAM_STEP)
    return -ADAM_LR * (m_hat / (jnp.sqrt(v_hat) + ADAM_EPS) + ADAM_WD * w), m_new, v_new


VECTORS = ("g_mix", "g_q", "g_k", "g_attn_out", "conv_b", "dt_bias", "a_log", "d_skip", "g_ssm_out", "g_cross", "g_mem",
           "g_cq", "g_ck", "g_mlp")
WEIGHTS = ("g_mix", "w_in", "g_q", "g_k", "g_attn_out", "conv_w", "conv_b", "dt_bias", "a_log", "d_skip", "g_ssm_out", "w_out",
           "g_cross", "g_mem", "w_cq", "w_ckv", "g_cq", "g_ck", "w_co", "g_mlp", "w_up", "w_down")


def _pack(parts):
    flat = jnp.concatenate([t.reshape(-1) for t in parts])
    total = -(-flat.shape[0] // 1024) * 1024
    return jnp.pad(flat, (0, total - flat.shape[0])).reshape(total // 128, 128)


def _rows_of(n):
    return -(-n // 128)


def _slot_rows(n):
    return -(-n // 1024) * 8


def _pack_rows(parts):
    rows = []
    for t in parts:
        flat = t.reshape(-1)
        rows.append(jnp.pad(flat, (0, 128 * _slot_rows(flat.shape[0]) - flat.shape[0])).reshape(-1, 128))
    return jnp.concatenate(rows)


def _adamw_vectors(summed, chip, vectors, conv):
    groups = list(vectors) + [conv]
    offsets, row = [], 0
    for w, _, _ in groups:
        offsets.append(row)
        row += _slot_rows(w.shape[1]) if w.shape[0] == 1 else _slot_rows(w.shape[0] * N_CHIPS * w.shape[1])
    conv_blocks = _rows_of(conv[0].shape[1])

    def body(chip_ref, sum_ref, *refs):
        ins, outs = refs[:3 * len(groups)], refs[3 * len(groups):]

        def update(i, g, idx):
            w_ref, m_ref, v_ref = ins[3 * i:3 * i + 3]
            delta, new_m, new_v = _adamw_math(w_ref[idx], g, m_ref[idx], v_ref[idx])
            for o_ref, val in zip(outs[4 * i:4 * i + 4], (g, delta, new_m, new_v)):
                o_ref[idx] = val

        for i, (w, _, _) in enumerate(vectors):
            for t in range(_rows_of(w.shape[1])):
                width = min(128, w.shape[1] - 128 * t)
                update(i, sum_ref[pl.ds(offsets[i] + t, 1), pl.ds(0, width)], (slice(None), pl.ds(128 * t, width)))
        for tap in range(conv[0].shape[0]):
            for blk in range(conv_blocks):
                src = offsets[-1] + tap * N_CHIPS * conv_blocks + chip_ref[0] * conv_blocks + blk
                update(len(vectors), sum_ref[pl.ds(src, 1), :], (pl.ds(tap, 1), pl.ds(128 * blk, 128)))

    def whole(a):
        return pl.BlockSpec(a.shape, lambda i, chip_ref: (0,) * a.ndim)

    operands = [t for group in groups for t in group]
    grid_spec = pltpu.PrefetchScalarGridSpec(
        num_scalar_prefetch=1, grid=(1,), in_specs=[whole(summed)] + [whole(t) for t in operands],
        out_specs=[whole(w) for w, _, _ in groups for _ in range(4)])
    res = pl.pallas_call(body, name="adamw_vectors", grid_spec=grid_spec,
                         out_shape=[jax.ShapeDtypeStruct(w.shape, F32) for w, _, _ in groups for _ in range(4)],
                         compiler_params=_params(("arbitrary",)))(chip.reshape(1).astype(jnp.int32), summed, *operands)
    return [res[4 * i:4 * i + 4] for i in range(len(groups))]


def _unpack(buf, shapes):
    flat, out, pos = buf.reshape(-1), [], 0
    for shape in shapes:
        size = math.prod(shape)
        out.append(flat[pos:pos + size].reshape(shape))
        pos += size
    return out


def kernel(x, mem, positions, g_mix, w_in, g_q, g_k, g_attn_out, conv_w, conv_b, dt_bias, a_log, d_skip, g_ssm_out, w_out, g_cross, g_mem, w_cq, w_ckv, g_cq, g_ck, w_co, g_mlp, w_up, w_down, loss_target, m_g_mix, m_w_in, m_g_q, m_g_k, m_g_attn_out, m_conv_w, m_conv_b, m_dt_bias, m_a_log, m_d_skip, m_g_ssm_out, m_w_out, m_g_cross, m_g_mem, m_w_cq, m_w_ckv, m_g_cq, m_g_ck, m_w_co, m_g_mlp, m_w_up, m_w_down, v_g_mix, v_w_in, v_g_q, v_g_k, v_g_attn_out, v_conv_w, v_conv_b, v_dt_bias, v_a_log, v_d_skip, v_g_ssm_out, v_w_out, v_g_cross, v_g_mem, v_w_cq, v_w_ckv, v_g_cq, v_g_ck, v_w_co, v_g_mlp, v_w_up, v_w_down):
    args = dict(locals())
    weights = {n: args[n][0] for n in WEIGHTS}
    mom_m = {n: args["m_" + n][0] for n in WEIGHTS}
    mom_v = {n: args["v_" + n][0] for n in WEIGHTS}
    x_idx, y_idx, c_idx = _place()
    chip = 2 * x_idx + y_idx

    shapes = {n: weights[n].shape for n in MATRICES}
    first, mid, late = ("w_in",), ("w_out", "w_cq", "w_ckv", "w_co"), ("w_up", "w_down")
    w_in_t, m_in_t, v_in_t = (jnp.swapaxes(t, 1, 2)[0] for t in (w_in, m_w_in, v_w_in))
    w_in_buf = [_cast_w_in_transposed(w_in_t, chip)]
    sems_in, w_in_buf, token = _split_start("gather_ici_start_w_in", w_in_buf, _ici_plan(first, shapes), [3])
    bufs = [_cast_into_gathered(weights[n], n, chip, after=(token,)) for n in mid + late]
    params = {n: weights[n].reshape(1, -1) for n in VECTORS}
    h_in = _rowwise(_norm_fn, [_full(x[0])], [_full(params["g_mix"])], [(D_MODEL, BF16, D_MODEL, 0, False)], name="norm_in",
                    after=(token,))[0]
    taps, tap_cols = weights["conv_w"].shape
    conv_parts = _small_allreduce(_pack([jnp.zeros((N_CHIPS, taps, tap_cols), F32).at[chip].set(0.5 * weights["conv_w"])]),
                                  "gather_conv_taps", after=(h_in, m_in_t, v_in_t, *bufs))
    w_in_buf = _split_wait("gather_ici_wait_w_in", w_in_buf, sems_in[0], _ici_plan(first, shapes), token, conv_parts)
    pass_sems, w_in_buf, token = _split_start("gather_pass_start_w_in", w_in_buf, _pass_on_plan(first, shapes), [3])
    plan = lambda refs: _ici_plan(mid, shapes)(refs[:4]) + _ici_plan(late, shapes)(refs[4:])
    sems_rest, bufs, token = _split_start("gather_ici_start_rest", bufs, plan, [12, 6], after=(token,))
    w_in_buf = _split_wait("gather_pass_wait_w_in", w_in_buf, pass_sems[0], _pass_on_plan(first, shapes), token)
    w_in_full = _w_in_columns(w_in_buf[0], to_shards=False)
    full = {"w_in": w_in_full,
            "w_dt": jnp.pad(w_in_full[:, D_MAIN:].reshape(D_MODEL, N_GROUPS, HEADS_PER_GROUP),
                            ((0, 0), (0, 0), (0, 128 - HEADS_PER_GROUP))).reshape(D_MODEL, DT_PAD)}
    in_flight = {}

    def more_weights(stage, after):
        if stage == "mixer_done":
            got = _split_wait("gather_ici_wait_mid", bufs[:4], sems_rest[0], _ici_plan(mid, shapes), after)
            sems, got, token = _split_start("gather_pass_start_mid", got, _pass_on_plan(mid, shapes), [12])
            return dict(zip(mid, _split_wait("gather_pass_wait_mid", got, sems[0], _pass_on_plan(mid, shapes), token)))
        if stage == "cross_started":
            got = _split_wait("gather_ici_wait_late", bufs[4:], sems_rest[1], _ici_plan(late, shapes), after)
            in_flight["late"] = _split_start("gather_pass_start_late", got, _pass_on_plan(late, shapes), [6])
            return {}
        sems, got, token = in_flight.pop("late")
        return dict(zip(late, _split_wait("gather_pass_wait_late", got, sems[0], _pass_on_plan(late, shapes), token, after)))

    params["conv_w"] = _unpack(conv_parts, [(N_CHIPS, taps, tap_cols)])[0].transpose(1, 0, 2).reshape(taps, N_CHIPS * tap_cols)

    groups = (("w_down",), ("w_up",), ("w_co", "w_cq", "w_ckv", "w_out"), ("w_in",))
    scattered = []
    out_g, out_d, out_m, out_v = {}, {}, {}, {}

    class GradStore(dict):
        pending = None

        def __setitem__(self, name, value):
            super().__setitem__(name, value)
            if "w_main" in self and "w_dt" in self and "w_in" not in self:
                gw_in = lax.dynamic_update_slice(self["w_main"], _unpad_heads(self["w_dt"]), (0, D_MAIN))
                self["w_in"] = _w_in_columns(gw_in, to_shards=True)
            for group in groups:
                if name in group and all(n in self for n in group):
                    self.settle()
                    pieces = [self[n].reshape(N_CHIPS, 2, shapes[n][0] // 2, shapes[n][1]) for n in group]
                    if group == groups[-1]:
                        self.scatter(group, pieces, _sibling_swap(pieces, "grad_swap_" + group[0]))
                    else:
                        landing = [lax.empty((N_CHIPS,) + a.shape[2:], BF16) for a in pieces]
                        sems, thru, self.token = _split_start("grad_swap_start_" + group[0], pieces + landing,
                                                              _swap_plan(len(pieces)), [len(pieces)])
                        self.pending = (group, sems[0], thru)

        def settle(self, *after):
            if self.pending is not None:
                group, sems, thru = self.pending
                self.pending = None
                thru = _split_wait("grad_swap_wait_" + group[0], thru, sems, _swap_plan(len(group)), *after)
                self.scatter(group, thru[:len(group)], thru[len(group):])

        def scatter(self, group, pieces, from_sibling):
            sums = [_add_halves(a, r, c_idx, "add_halves_" + n) for n, a, r in zip(group, pieces, from_sibling)]
            landing = [lax.empty((3,) + s.shape[1:], BF16) for s in sums]
            sems, thru, self.token = _split_start("grad_scatter_start_" + group[0], sums + landing,
                                                  _scatter_plan(len(sums)), [3 * len(sums)])
            scattered.append((group, sems[0], thru))

        sides = {}

        def prepare_side(self, name, after):
            entry = next(e for e in scattered if e[0] == (name,))
            scattered.remove(entry)
            thru = _split_wait("grad_scatter_wait_" + name, entry[2], entry[1], _scatter_plan(1), after)
            own = _sum_chips(thru[0], thru[1], chip, "sum_chips_" + name)
            sems, thru, token = _split_start("grad_share_start_" + name, [own, lax.empty(own.shape, F32)], _share_plan(1), [1])
            self.sides[name] = (sems[0], thru, token)

        def side_inputs(self, name):
            sems, thru, token = self.sides.pop(name)
            own, other = _split_wait("grad_share_wait_" + name, thru, sems, _share_plan(1), token)
            return (c_idx.reshape(1, 1).astype(jnp.int32), weights[name], own, other, mom_m[name], mom_v[name])

        def side_done(self, name, results):
            out_g[name], out_d[name], out_m[name], out_v[name] = results

    loss, grad_x, grads = _local_step(x[0], mem[0], positions[0], loss_target[0], params, full, more_weights, GradStore(),
                                      h_in)


    def finish(entries, order, token):
        halves = {}
        for group, sems, thru in entries:
            thru = _split_wait("grad_scatter_wait_" + group[0], thru, sems, _scatter_plan(len(group)), token)
            for i, n in enumerate(group):
                halves[n] = _sum_chips(thru[i], thru[len(group) + i], chip, "sum_chips_" + n)
        sources = [halves[n] for n in order]
        landing = [lax.empty(s.shape, F32) for s in sources]
        sems, thru, token = _split_start("grad_share_start_" + order[0], sources + landing, _share_plan(len(order)),
                                         [1] * len(order))
        for i, n in enumerate(order):
            own, other = _split_wait("grad_share_wait_" + n, [thru[i], thru[len(order) + i]], sems[i], _share_plan(1), token)
            if n == "w_in":
                res_t = _adamw_w_in_transposed(w_in_t, own, other, m_in_t, v_in_t, c_idx)
                out_g[n], out_d[n], out_m[n], out_v[n] = (t.T for t in res_t)
            else:
                out_g[n], out_d[n], out_m[n], out_v[n] = _adamw_halves(weights[n], own, other, mom_m[n], mom_v[n], c_idx,
                                                                       "adamw_" + n)
            token = out_v[n]
        return token

    token = finish(scattered[:-1], ("w_cq", "w_co", "w_ckv", "w_out"), grad_x)
    finish(scattered[-1:], ("w_in",), token)

    names = VECTORS + ("conv_w",)
    summed = _small_allreduce(_pack_rows([grads[n] for n in names] + [loss]), "allreduce_vectors")
    total_loss = summed[sum(_slot_rows(grads[n].size) for n in names), 0]
    small_out = _adamw_vectors(summed, chip, [(args[n], args["m_" + n], args["v_" + n]) for n in VECTORS],
                               (weights["conv_w"], mom_m["conv_w"], mom_v["conv_w"]))
    for n, res in zip(names, small_out):
        out_g[n], out_d[n], out_m[n], out_v[n] = (t.reshape(weights[n].shape) for t in res)

    outs =[total_loss, grad_x[None]]
    for group in (out_g, out_d, out_m, out_v):
        outs += [group[n][None] for n in WEIGHTS]
    return tuple(outs)
```

```python
import functools
import math

import jax
import jax.numpy as jnp
from jax import lax
from jax.experimental import pallas as pl
from jax.experimental.pallas import tpu as pltpu

F32 = jnp.float32
BF16 = jnp.bfloat16

SEQ = 2048
D_MODEL = 2048
HEAD = 64
D_ATTN = 1024
D_SSM = 1024
N_GROUPS = 4
N_STATE = 128
CHUNK = 128
ATT_BLK = 128
N_MEM = 256
D_CROSS = 512
D_MAIN = 6144
N_DT = 16
DT_PAD = 512
ROT = 16
ROPE_THETA = 500000.0
EPS = 1e-6
NEG = -1e30
BRANCH_BLOCKS = (16, 4, 1)
DILATIONS = (1, 4, 16)

ADAM_LR, ADAM_B1, ADAM_B2, ADAM_EPS, ADAM_WD, ADAM_STEP = 0.001, 0.9, 0.999, 1e-08, 0.01, 10

VMEM_LIMIT = 56 * 1024 * 1024
MESH = pl.DeviceIdType.MESH


def _params(sem, **kw):
    return pltpu.CompilerParams(dimension_semantics=sem, vmem_limit_bytes=VMEM_LIMIT, **kw)


def _bdot(a, b, dims):
    return lax.dot_general(a.astype(BF16), b.astype(BF16), (dims, ((), ())), preferred_element_type=F32)


def _fdot(a, b, dims):
    return lax.dot_general(a, b, (dims, ((), ())), preferred_element_type=F32, precision=lax.Precision.HIGHEST)


NN = ((1,), (0,))
NT = ((1,), (1,))
TN = ((0,), (0,))


def _tile(n, want):
    t = min(n, want)
    while n % t:
        t //= 2
    return t


def _matmul(a, b, *, mode, name, outs, extra=(), vecs=(), epilogue=None, col_shards=1, after=(), n_cols=None, out_cols=None,
            tile_sums=0, tm=1024, tn=1024, tk=2048):
    if mode == "nn":
        (m, k), n = a.shape, b.shape[1]
    elif mode == "nt":
        (m, k), n = a.shape, b.shape[0]
    else:
        (k, m), n = a.shape, b.shape[1]
    n = n if n_cols is None else n_cols
    tm, tn, tk = _tile(m, tm), _tile(n // col_shards, tn), _tile(k, tk)
    nk = k // tk
    per_shard = n // col_shards // tn
    dims = {"nn": NN, "nt": NT, "tn": TN}[mode]
    a_spec = pl.BlockSpec((tk, tm), lambda i, j, kk: (kk, i)) if mode == "tn" else pl.BlockSpec((tm, tk), lambda i, j, kk: (i, kk))
    b_spec = pl.BlockSpec((tn, tk), lambda i, j, kk: (j, kk)) if mode == "nt" else pl.BlockSpec((tk, tn), lambda i, j, kk: (kk, j))
    o_spec = pl.BlockSpec((tm, tn), lambda i, j, kk: (i, j))
    n_extra, n_out, n_after = len(extra) + len(vecs), len(outs), len(after)

    def body(a_ref, b_ref, *rest):
        extra_refs, out_refs, acc_ref = rest[:n_extra], rest[n_extra + n_after:-1], rest[-1]

        def finish(acc):
            res = (acc,) if epilogue is None else epilogue(acc, *[e[...] for e in extra_refs])
            for o_ref, r in zip(out_refs[:n_out], res):
                o_ref[...] = r.astype(o_ref.dtype)
            for o_ref, r in zip(out_refs[n_out:], res[n_out:]):
                o_ref[...] = jnp.broadcast_to(r, o_ref.shape)

        if nk == 1:
            finish(_bdot(a_ref[...], b_ref[...], dims))
            return
        kk = pl.program_id(2)

        @pl.when(kk == 0)
        def _():
            acc_ref[...] = jnp.zeros_like(acc_ref)

        acc_ref[...] += _bdot(a_ref[...], b_ref[...], dims)

        @pl.when(kk == nk - 1)
        def _():
            finish(acc_ref[...])

    if col_shards == 1:
        out_specs, out_dims = [o_spec] * n_out, (m, n if out_cols is None else out_cols)
    else:
        sharded = pl.BlockSpec((None, tm, tn), lambda i, j, kk: (j // per_shard, i, j % per_shard))
        out_specs, out_dims = [sharded] * n_out, (col_shards, m, n // col_shards)
    res = pl.pallas_call(
        body, name=name, grid=(m // tm, n // tn, nk),
        in_specs=[a_spec, b_spec] + [o_spec] * len(extra) + [pl.BlockSpec((1, tn), lambda i, j, kk: (0, j))] * len(vecs)
        + [pl.BlockSpec(memory_space=pl.ANY)] * n_after,
        out_specs=out_specs + [pl.BlockSpec((8, 128), lambda i, j, kk: (i, j))] * tile_sums,
        out_shape=[jax.ShapeDtypeStruct(out_dims, dt) for dt in outs]
        + [jax.ShapeDtypeStruct((m // tm * 8, n // tn * 128), F32)] * tile_sums,
        scratch_shapes=[pltpu.VMEM((tm, tn) if nk > 1 else (8, 128), F32)],
        compiler_params=_params(("parallel", "parallel", "arbitrary")),
    )(a, b, *extra, *vecs, *after)
    res = list(res[:n_out]) + [t[::8, ::128] for t in res[n_out:]]
    return res[0] if len(res) == 1 else res


def _row_spec(tr, bw, cb, per_group):
    return pl.BlockSpec((tr, bw), (lambda g, i: (i, cb + g)) if per_group else (lambda g, i: (i, cb)))


def _vec_spec(bw, cb, per_group):
    return pl.BlockSpec((1, bw), (lambda g, i: (0, cb + g)) if per_group else (lambda g, i: (0, cb)))


def _rowwise(fn, rows, vecs, outs, *, name, n_rows=SEQ, tr=512, groups=1, after=()):
    n_r, n_v, n_after = len(rows), len(vecs), len(after)

    def body(*refs):
        vals = [r[...].astype(F32) for r in refs[:n_r + n_v]]
        res = fn(*vals)
        for o_ref, r in zip(refs[n_r + n_v + n_after:], res):
            o_ref[...] = r.astype(o_ref.dtype)

    res = pl.pallas_call(
        body, name=name, grid=(groups, n_rows // tr),
        in_specs=[_row_spec(tr, bw, cb, pg) for _, bw, cb, pg in rows] + [_vec_spec(bw, cb, pg) for _, bw, cb, pg in vecs]
        + [pl.BlockSpec(memory_space=pl.ANY)] * n_after,
        out_specs=[_row_spec(tr, bw, cb, pg) for _, _, bw, cb, pg in outs],
        out_shape=[jax.ShapeDtypeStruct((n_rows, w), dt) for w, dt, _, _, _ in outs],
        compiler_params=_params(("parallel", "parallel")),
    )(*[r[0] for r in rows], *[v[0] for v in vecs], *after)
    return res


def _rowwise_vjp(fn, rows, vecs, cts, row_grads, vec_grads, *, name, n_rows=SEQ, tr=512, groups=1, after=()):
    n_r, n_v, n_after = len(rows), len(vecs), len(after)
    ct_ops = [op for group in cts for op in group]
    ct_sizes = [len(group) for group in cts]
    res_ops = [g[6] for g in row_grads if g[6] is not None]
    n_ct, n_res, n_rg = len(ct_ops), len(res_ops), len(row_grads)

    def body(*refs):
        vals = [r[...].astype(F32) for r in refs[:n_r + n_v]]
        pos = n_r + n_v
        ct_vals = []
        for size in ct_sizes:
            acc = refs[pos][...].astype(F32)
            for t in range(1, size):
                acc = acc + refs[pos + t][...].astype(F32)
            ct_vals.append(acc)
            pos += size
        res_refs = refs[pos:pos + n_res]
        out_refs = refs[pos + n_res + n_after:]
        _, pullback = jax.vjp(fn, *vals)
        grads = pullback(tuple(ct_vals))
        r_i = 0
        for o_ref, g in zip(out_refs[:n_rg], row_grads):
            val = grads[g[0]]
            if g[6] is not None:
                val = val + res_refs[r_i][...].astype(F32)
                r_i += 1
            o_ref[...] = val.astype(o_ref.dtype)
        first = (pl.program_id(1) == 0)
        for o_ref, g in zip(out_refs[n_rg:], vec_grads):
            val = jnp.sum(grads[n_r + g[0]], axis=0, keepdims=True)
            init = first if g[4] else jnp.logical_and(first, pl.program_id(0) == 0)

            @pl.when(init)
            def _(o_ref=o_ref, val=val):
                o_ref[...] = val

            @pl.when(jnp.logical_not(init))
            def _(o_ref=o_ref, val=val):
                o_ref[...] += val

    in_specs = [_row_spec(tr, bw, cb, pg) for _, bw, cb, pg in rows] + [_vec_spec(bw, cb, pg) for _, bw, cb, pg in vecs]
    in_specs += [_row_spec(tr, bw, cb, pg) for _, bw, cb, pg in ct_ops + res_ops] + [pl.BlockSpec(memory_space=pl.ANY)] * n_after
    out_specs =[_row_spec(tr, g[3], g[4], g[5]) for g in row_grads] + [_vec_spec(g[2], g[3], g[4]) for g in vec_grads]
    out_shape = [jax.ShapeDtypeStruct((n_rows, g[1]), g[2]) for g in row_grads]
    out_shape += [jax.ShapeDtypeStruct((1, g[1]), F32) for g in vec_grads]
    return pl.pallas_call(
        body, name=name, grid=(groups, n_rows // tr),
        in_specs=in_specs, out_specs=out_specs, out_shape=out_shape,
        compiler_params=_params(("arbitrary", "arbitrary")),
    )(*[r[0] for r in rows], *[v[0] for v in vecs], *[c[0] for c in ct_ops], *[r[0] for r in res_ops], *after)


def _full(arr, width=None):
    return (arr, arr.shape[1] if width is None else width, 0, False)


def _make_xor(sh):
    def raw(x):
        n = x.shape[-1]
        lane = lax.broadcasted_iota(jnp.int32, x.shape, x.ndim - 1)
        up = pltpu.roll(x, n - sh, x.ndim - 1)
        down = pltpu.roll(x, sh, x.ndim - 1)
        return jnp.where((lane & sh) == 0, up, down)

    f = jax.custom_vjp(raw)
    f.defvjp(lambda x: (raw(x), None), lambda _, ct: (raw(ct),))
    return f


_SWAP_ROPE_HALVES = _make_xor(ROT // 2)


def _head_sum(x):
    n = x.shape[-1]
    same_head = (lax.broadcasted_iota(jnp.int32, (n, n), 0) // HEAD) == (lax.broadcasted_iota(jnp.int32, (n, n), 1) // HEAD)
    return _fdot(x, same_head.astype(F32), NN)


def _rms(x, g):
    return x * lax.rsqrt(jnp.mean(x * x, axis=-1, keepdims=True) + EPS) * g


def _head_rms_rope(x, g, cos, sin, scale):
    y = x * lax.rsqrt(_head_sum(x * x) * (1.0 / HEAD) + EPS) * g
    return (y * cos + _SWAP_ROPE_HALVES(y) * sin) * scale


def _qk_fn(q, k, v, cos, sin, gq, gk):
    return (_head_rms_rope(q, gq, cos, sin, HEAD ** -0.5), _head_rms_rope(k, gk, cos, sin, 1.0), v)


def _norm_fn(x, g):
    return (_rms(x, g),)


def _merge_fn(o0, o1, o2, l0, l1, l2, g):
    m = lax.stop_gradient(jnp.maximum(jnp.maximum(l0, l1), l2))
    e0, e1, e2 = jnp.exp(l0 - m), jnp.exp(l1 - m), jnp.exp(l2 - m)
    mix = (e0 * o0 + e1 * o1 + e2 * o2) / (e0 + e1 + e2)
    return (_rms(mix, g),)


def _gate_fn(y, z, g):
    return (_rms(y * (z * jax.nn.sigmoid(z)), g),)


def _attn_pair(q, kc, vc, kp=None, vp=None, has_prev=None):
    pick0, pick1 = _head_picks()
    k_band, v_band, mask = _attn_band(kc, vc, kp, vp, has_prev)
    s = jnp.where(mask, _bdot(jnp.concatenate([q * pick0, q * pick1], axis=0), k_band, NT), NEG)
    m = jnp.max(s, axis=-1, keepdims=True)
    p = jnp.exp(s - m)
    den = jnp.sum(p, axis=-1, keepdims=True)
    acc = _bdot(p, v_band, NN) * (1.0 / den)
    lse_rows = m + jnp.log(den)
    o = pick0 * acc[:ATT_BLK] + pick1 * acc[ATT_BLK:]
    lse = pick0 * lse_rows[:ATT_BLK] + pick1 * lse_rows[ATT_BLK:]
    return o, lse


def _head_picks():
    lane = lax.broadcasted_iota(jnp.int32, (1, 2 * HEAD), 1)
    return (lane < HEAD).astype(F32), (lane >= HEAD).astype(F32)


def _attn_band(kc, vc, kp, vp, has_prev):
    n_keys = ATT_BLK if kp is None else 2 * ATT_BLK
    qi = lax.broadcasted_iota(jnp.int32, (2 * ATT_BLK, n_keys), 0) & (ATT_BLK - 1)
    kj = lax.broadcasted_iota(jnp.int32, (2 * ATT_BLK, n_keys), 1)
    if kp is None:
        return kc, vc, qi >= kj
    in_prev = jnp.logical_and(jnp.logical_and(kj < ATT_BLK, kj >= qi), has_prev)
    mask = jnp.logical_or(in_prev, jnp.logical_and(kj >= ATT_BLK, qi >= kj - ATT_BLK))
    return jnp.concatenate([kp, kc], axis=0), jnp.concatenate([vp, vc], axis=0), mask


def _attn_config(b):
    r = DILATIONS[b]
    return r, ATT_BLK * r, (D_ATTN if r == 1 else 128), BRANCH_BLOCKS[b] > 1


def _for_residues(r, fn):
    if r <= 4:
        for rho in range(r):
            fn(rho)
    else:
        def step(t, carry):
            for u in range(4):
                fn(4 * t + u)
            return carry

        lax.fori_loop(0, r // 4, step, 0)


def _strided_rows(start, r):
    if r > 1:
        return pl.ds(start, ATT_BLK, stride=r)
    return pl.ds(start if isinstance(start, int) else pl.multiple_of(start, ATT_BLK), ATT_BLK)


def _attention_fwd(qn, kn, vn, b):
    r, rows, lanes, with_prev = _attn_config(b)
    cur = pl.BlockSpec((rows, lanes), lambda g, n: (n, g))
    prev = pl.BlockSpec((rows, lanes), lambda g, n: (jnp.maximum(n - 1, 0), g))

    def body(*refs):
        ins, (o_ref, l_ref) = refs[:-2], refs[-2:]
        has_prev = pl.program_id(1) > 0

        def one(rho):
            sub = _strided_rows(rho, r)
            for pair in range(lanes // 128):
                sl = pl.ds(pair * 128, 128)
                args = [ref[sub, sl] for ref in ins] + ([has_prev] if with_prev else [])
                o_ref[sub, sl], l_ref[sub, sl] = _attn_pair(*args)

        _for_residues(r, one)

    operands = (qn, kn, vn, kn, vn) if with_prev else (qn, kn, vn)
    return pl.pallas_call(
        body, name="attn_fwd_%d" % r, grid=(D_ATTN // lanes, SEQ // rows),
        in_specs=[cur, cur, cur] + ([prev, prev] if with_prev else []), out_specs=[cur, cur],
        out_shape=[jax.ShapeDtypeStruct((SEQ, D_ATTN), F32)] * 2,
        compiler_params=_params(("parallel", "parallel")),
    )(*operands)


def _attn_pair_bwd(q, kc, vc, kp, vp, o, lse, do, dl, has_prev):
    pick0, pick1 = _head_picks()
    lane = lax.broadcasted_iota(jnp.int32, (1, 2 * HEAD), 1)
    k_band, v_band, mask = _attn_band(kc, vc, kp, vp, has_prev)
    q2 = jnp.concatenate([q * pick0, q * pick1], axis=0)
    do2 = jnp.concatenate([do * pick0, do * pick1], axis=0)
    lse2 = jnp.concatenate([jnp.sum(lse * (lane == 0).astype(F32), axis=-1, keepdims=True),
                            jnp.sum(lse * (lane == HEAD).astype(F32), axis=-1, keepdims=True)], axis=0)
    base = jnp.sum(jnp.concatenate([dl * pick0, dl * pick1], axis=0) - do2 * jnp.concatenate([o, o], axis=0),
                   axis=-1, keepdims=True)
    p = jnp.exp(jnp.where(mask, _bdot(q2, k_band, NT), NEG) - lse2)
    ds = p * (_bdot(do2, v_band, NT) + base)
    dq2 = _bdot(ds, k_band, NN)
    dq = pick0 * dq2[:ATT_BLK] + pick1 * dq2[ATT_BLK:]
    dk, dv = _bdot(ds, q2, TN), _bdot(p, do2, TN)
    if kp is None:
        return dq, dk, dv
    return dq, dk[ATT_BLK:], dv[ATT_BLK:], dk[:ATT_BLK], dv[:ATT_BLK]


def _attention_bwd(qn, kn, vn, o, lse, do, dl, b):
    r, rows, lanes, with_prev = _attn_config(b)
    cur = pl.BlockSpec((rows, lanes), lambda g, n: (n, g))
    prev = pl.BlockSpec((rows, lanes), lambda g, n: (jnp.maximum(n - 1, 0), g))
    whole = pl.BlockSpec((SEQ, lanes), lambda g, n: (0, g))
    n_in = 5 if with_prev else 3

    def body(*refs):
        ins, (o_ref, l_ref, do_ref, dl_ref, dq_ref, dk_ref, dv_ref) = refs[:n_in], refs[n_in:]
        n = pl.program_id(1)

        @pl.when(n == 0)
        def _():
            dk_ref[...] = jnp.zeros_like(dk_ref)
            dv_ref[...] = jnp.zeros_like(dv_ref)

        def one(rho):
            sub = _strided_rows(rho, r)
            sub_c = _strided_rows(n * rows + rho, r)
            sub_p = _strided_rows(jnp.maximum(n - 1, 0) * rows + rho, r)
            for pair in range(lanes // 128):
                sl = pl.ds(pair * 128, 128)
                vals = [ref[sub, sl] for ref in ins] + ([] if with_prev else [None, None])
                grads = _attn_pair_bwd(*vals, o_ref[sub, sl], l_ref[sub, sl], do_ref[sub, sl], dl_ref[sub, sl], n > 0)
                dq_ref[sub, sl] = grads[0]
                dk_ref[sub_c, sl] += grads[1]
                dv_ref[sub_c, sl] += grads[2]
                if with_prev:
                    dk_ref[sub_p, sl] += grads[3]
                    dv_ref[sub_p, sl] += grads[4]

        _for_residues(r, one)

    operands = (qn, kn, vn, kn, vn) if with_prev else (qn, kn, vn)
    return pl.pallas_call(
        body, name="attn_bwd_%d" % r, grid=(D_ATTN // lanes, SEQ // rows),
        in_specs=[cur, cur, cur] + ([prev, prev] if with_prev else []) + [cur] * 4, out_specs=[cur, whole, whole],
        out_shape=[jax.ShapeDtypeStruct((SEQ, D_ATTN), F32)] * 3,
        compiler_params=_params(("parallel", "arbitrary")),
    )(*operands, o, lse, do, dl)


CONV_COLS = 256
XBC_BLOCK0 = (3 * D_ATTN + D_SSM) // CONV_COLS


def _shift_rows(x, s):
    n = x.shape[0]
    t = lax.broadcasted_iota(jnp.int32, x.shape, 0)
    if s >= 0:
        return jnp.where(t >= s, pltpu.roll(x, s, 0), 0.0)
    return jnp.where(t < n + s, pltpu.roll(x, n + s, 0), 0.0)


def _conv_pre(x, w_ref, b_ref):
    delayed = [_shift_rows(x, 3 - k) for k in range(3)]
    pre = b_ref[...] + w_ref[3:4, :] * x
    for k in range(3):
        pre = pre + w_ref[k:k + 1, :] * delayed[k]
    return pre, delayed


def _conv_fwd(proj, conv_w, conv_b):
    cols = conv_w.shape[1]

    def body(x_ref, w_ref, b_ref, o_ref):
        pre, _ = _conv_pre(x_ref[...], w_ref, b_ref)
        o_ref[...] = pre * jax.nn.sigmoid(pre)

    blk = pl.BlockSpec((SEQ, CONV_COLS), lambda j: (0, j))
    return pl.pallas_call(
        body, name="conv_fwd", grid=(cols // CONV_COLS,),
        in_specs=[pl.BlockSpec((SEQ, CONV_COLS), lambda j: (0, XBC_BLOCK0 + j)),
                  pl.BlockSpec((4, CONV_COLS), lambda j: (0, j)), pl.BlockSpec((1, CONV_COLS), lambda j: (0, j))],
        out_specs=blk, out_shape=jax.ShapeDtypeStruct((SEQ, cols), F32),
        compiler_params=_params(("parallel",)),
    )(proj, conv_w, conv_b)


def _conv_bwd(proj, conv_w, conv_b, dxs, db, dc):
    cols = conv_w.shape[1]
    x_blocks, b_blocks = dxs.shape[1] // CONV_COLS, db.shape[1] // CONV_COLS

    def body(x_ref, w_ref, b_ref, dxs_ref, db_ref_in, dc_ref_in, dx_ref, dw_ref, db_ref):
        j = pl.program_id(0)
        dy = jnp.where(j < x_blocks, dxs_ref[...], jnp.where(j < x_blocks + b_blocks, db_ref_in[...], dc_ref_in[...]))
        x = x_ref[...]
        pre, delayed = _conv_pre(x, w_ref, b_ref)
        sg = jax.nn.sigmoid(pre)
        dpre = dy * (sg * (1.0 + pre * (1.0 - sg)))
        db_ref[...] = jnp.sum(dpre, axis=0, keepdims=True)
        dx = w_ref[3:4, :] * dpre
        dw_ref[3:4, :] = jnp.sum(dpre * x, axis=0, keepdims=True)
        for k in range(3):
            dx = dx + w_ref[k:k + 1, :] * _shift_rows(dpre, k - 3)
            dw_ref[k:k + 1, :] = jnp.sum(dpre * delayed[k], axis=0, keepdims=True)
        dw_ref[4:8, :] = jnp.zeros((4, CONV_COLS), F32)
        dx_ref[...] = dx.astype(dx_ref.dtype)

    blk = pl.BlockSpec((SEQ, CONV_COLS), lambda j: (0, j))
    parts = [pl.BlockSpec((SEQ, CONV_COLS), lambda j: (0, jnp.minimum(j, x_blocks - 1))),
             pl.BlockSpec((SEQ, CONV_COLS), lambda j: (0, jnp.clip(j - x_blocks, 0, b_blocks - 1))),
             pl.BlockSpec((SEQ, CONV_COLS), lambda j: (0, jnp.clip(j - x_blocks - b_blocks, 0, b_blocks - 1)))]
    return pl.pallas_call(
        body, name="conv_bwd", grid=(cols // CONV_COLS,),
        in_specs=[pl.BlockSpec((SEQ, CONV_COLS), lambda j: (0, XBC_BLOCK0 + j)),
                  pl.BlockSpec((4, CONV_COLS), lambda j: (0, j)), pl.BlockSpec((1, CONV_COLS), lambda j: (0, j))] + parts,
        out_specs=[blk, pl.BlockSpec((8, CONV_COLS), lambda j: (0, j)), pl.BlockSpec((1, CONV_COLS), lambda j: (0, j))],
        out_shape=[jax.ShapeDtypeStruct((SEQ, cols), BF16), jax.ShapeDtypeStruct((8, cols), F32),
                   jax.ShapeDtypeStruct((1, cols), F32)],
        compiler_params=_params(("parallel",)),
    )(proj, conv_w, conv_b, dxs, db, dc)


HEADS_PER_GROUP = 4


GROUP_WIDTH = HEADS_PER_GROUP * HEAD


def _ssd_chunk(x, bm, cm, dtr, bias, alog, dsk, h):
    row = lax.broadcasted_iota(jnp.int32, (CHUNK, CHUNK), 0)
    col = lax.broadcasted_iota(jnp.int32, (CHUNK, CHUNK), 1)
    causal = row >= col
    z = dtr + bias
    dt = jnp.maximum(z, 0.0) + jnp.log(1.0 + jnp.exp(-jnp.abs(z)))
    acs = _fdot(causal.astype(F32), dt * -jnp.exp(alog), NN)
    acs_t, dt_t = acs.T, dt.T
    cb = _bdot(cm, bm, NT)
    lane = lax.broadcasted_iota(jnp.int32, (1, CHUNK), 1)
    sub = lax.broadcasted_iota(jnp.int32, (CHUNK, 1), 0)
    wide = lax.broadcasted_iota(jnp.int32, (1, GROUP_WIDTH), 1) // HEAD
    tall = lax.broadcasted_iota(jnp.int32, (GROUP_WIDTH, 1), 0) // HEAD
    acs_last = jnp.sum(acs * (sub == CHUNK - 1).astype(F32), axis=0, keepdims=True)
    to_lanes = (lax.broadcasted_iota(jnp.int32, (CHUNK, GROUP_WIDTH), 0)
                == lax.broadcasted_iota(jnp.int32, (CHUNK, GROUP_WIDTH), 1) // HEAD).astype(F32)
    grow = _fdot(jnp.exp(acs), to_lanes, NN)
    keep = _fdot(jnp.exp(acs_last - acs) * dt, to_lanes, NN)
    w_parts, x_parts, skip, carry = [], [], 0.0, 0.0
    for j in range(HEADS_PER_GROUP):
        on_lane, on_sub = (lane == j).astype(F32), (sub == j).astype(F32)
        acs_c = jnp.sum(acs * on_lane, axis=1, keepdims=True)
        acs_r = jnp.sum(acs_t * on_sub, axis=0, keepdims=True)
        dt_r = jnp.sum(dt_t * on_sub, axis=0, keepdims=True)
        w_parts.append(cb * jnp.exp(jnp.where(causal, acs_c - acs_r, NEG)) * dt_r)
        x_parts.append(x * (wide == j).astype(F32))
        skip = skip + jnp.sum(dsk * on_lane, axis=1, keepdims=True) * (wide == j).astype(F32)
        carry = carry + jnp.sum(jnp.exp(acs_last) * on_lane, axis=1, keepdims=True) * (tall == j).astype(F32)
    y_diag = _bdot(jnp.concatenate(w_parts, axis=1), jnp.concatenate(x_parts, axis=0), NN)
    y = y_diag + _bdot(cm, h, NT) * grow + skip * x
    return y, h * carry + _bdot(x * keep, bm, TN)


GROUPS_PER_STEP = 2
SSD_STEPS = N_GROUPS // GROUPS_PER_STEP


def _ssd_specs(reverse):
    n_chunks = SEQ // CHUNK
    c_of = (lambda c: n_chunks - 1 - c) if reverse else (lambda c: c)
    x_w, n_w, dt_w = GROUPS_PER_STEP * GROUP_WIDTH, GROUPS_PER_STEP * N_STATE, GROUPS_PER_STEP * 128
    x_spec = pl.BlockSpec((CHUNK, x_w), lambda g, c: (c_of(c), g))
    b_spec = pl.BlockSpec((CHUNK, n_w), lambda g, c: (c_of(c), D_SSM // n_w + g))
    c_spec = pl.BlockSpec((CHUNK, n_w), lambda g, c: (c_of(c), (D_SSM + N_GROUPS * N_STATE) // n_w + g))
    dt_spec = pl.BlockSpec((CHUNK, dt_w), lambda g, c: (c_of(c), g))
    vec_spec = pl.BlockSpec((1, dt_w), lambda g, c: (0, g))
    h_spec = pl.BlockSpec((None, GROUPS_PER_STEP, GROUP_WIDTH, N_STATE), lambda g, c: (c_of(c), g, 0, 0))
    return x_spec, b_spec, c_spec, dt_spec, vec_spec, h_spec


def _group_slices(u):
    return pl.ds(u * GROUP_WIDTH, GROUP_WIDTH), pl.ds(u * N_STATE, N_STATE), pl.ds(u * 128, 128)


def _ssd_gated_chunk(x, bm, cm, dtr, bias, alog, dsk, h, z, g_out):
    y, h_new = _ssd_chunk(x, bm, cm, dtr, bias, alog, dsk, h)
    return _gate_fn(y, z, g_out)[0], h_new


def _ssd_gate_specs(reverse):
    x_spec = _ssd_specs(reverse)[0]
    z_block0 = 3 * D_ATTN // x_spec.block_shape[1]
    z_spec = pl.BlockSpec(x_spec.block_shape, lambda g, c: (x_spec.index_map(g, c)[0], z_block0 + g))
    return z_spec, pl.BlockSpec((1, x_spec.block_shape[1]), lambda g, c: (0, g))


def _ssd_fwd(xbc, dt_raw, bias, alog, dsk, proj, g_out):
    x_spec, b_spec, c_spec, dt_spec, vec_spec, h_spec = _ssd_specs(False)
    z_spec, g_spec = _ssd_gate_specs(False)

    def body(x_ref, b_ref, c_ref, dt_ref, bias_ref, alog_ref, dsk_ref, z_ref, g_ref, ssm_ref, hin_ref, h_scr):
        @pl.when(pl.program_id(1) == 0)
        def _():
            h_scr[...] = jnp.zeros_like(h_scr)

        for u in range(GROUPS_PER_STEP):
            xs, ns, ds = _group_slices(u)
            h = h_scr[u]
            hin_ref[u] = h
            ssm, h_scr[u] = _ssd_gated_chunk(x_ref[:, xs], b_ref[:, ns], c_ref[:, ns], dt_ref[:, ds], bias_ref[:, ds],
                                             alog_ref[:, ds], dsk_ref[:, ds], h, z_ref[:, xs], g_ref[:, xs])
            ssm_ref[:, xs] = ssm.astype(ssm_ref.dtype)

    return pl.pallas_call(
        body, name="ssd_fwd", grid=(SSD_STEPS, SEQ // CHUNK),
        in_specs=[x_spec, b_spec, c_spec, dt_spec, vec_spec, vec_spec, vec_spec, z_spec, g_spec],
        out_specs=[x_spec, h_spec],
        out_shape=[jax.ShapeDtypeStruct((SEQ, D_SSM), BF16),
                   jax.ShapeDtypeStruct((SEQ // CHUNK, N_GROUPS, GROUP_WIDTH, N_STATE), F32)],
        scratch_shapes=[pltpu.VMEM((GROUPS_PER_STEP, GROUP_WIDTH, N_STATE), F32)],
        compiler_params=_params(("parallel", "arbitrary")),
    )(xbc, xbc, xbc, dt_raw, bias, alog, dsk, proj, g_out)


def _ssd_bwd(xbc, dt_raw, bias, alog, dsk, h_in, proj, g_out, dmix):
    x_spec, b_spec, c_spec, dt_spec, vec_spec, h_spec = _ssd_specs(True)
    z_spec, g_spec = _ssd_gate_specs(True)
    ct_block0 = D_ATTN // x_spec.block_shape[1]
    ct_spec = pl.BlockSpec(x_spec.block_shape, lambda g, c: (x_spec.index_map(g, c)[0], ct_block0 + g))

    def body(x_ref, b_ref, c_ref, dt_ref, bias_ref, alog_ref, dsk_ref, hin_ref, z_ref, g_ref, ct_ref,
             dx_ref, db_ref, dc_ref, ddt_ref, dbias_ref, dalog_ref, ddsk_ref, dz_ref, dg_ref, dh_scr):
        first = pl.program_id(1) == 0

        @pl.when(first)
        def _():
            dh_scr[...] = jnp.zeros_like(dh_scr)

        for u in range(GROUPS_PER_STEP):
            xs, ns, ds = _group_slices(u)
            _, pullback = jax.vjp(_ssd_gated_chunk, x_ref[:, xs], b_ref[:, ns], c_ref[:, ns], dt_ref[:, ds], bias_ref[:, ds],
                                  alog_ref[:, ds], dsk_ref[:, ds], hin_ref[u], z_ref[:, xs], g_ref[:, xs])
            g = pullback((ct_ref[:, xs], dh_scr[u]))
            dx_ref[:, xs], db_ref[:, ns], dc_ref[:, ns] = g[0], g[1], g[2]
            ddt_ref[:, ds] = g[3].astype(ddt_ref.dtype)
            dh_scr[u] = g[7]
            dz_ref[:, xs] = g[8].astype(dz_ref.dtype)
            sums = ((dbias_ref, g[4], ds), (dalog_ref, g[5], ds), (ddsk_ref, g[6], ds),
                    (dg_ref, jnp.sum(g[9], axis=0, keepdims=True), xs))
            for o_ref, val, lanes in sums:
                @pl.when(first)
                def _(o_ref=o_ref, val=val, lanes=lanes):
                    o_ref[:, lanes] = val

                @pl.when(jnp.logical_not(first))
                def _(o_ref=o_ref, val=val, lanes=lanes):
                    o_ref[:, lanes] += val

    n_chunks = SEQ // CHUNK
    out_b = pl.BlockSpec((CHUNK, GROUPS_PER_STEP * N_STATE), lambda g, c: (n_chunks - 1 - c, g))
    return pl.pallas_call(
        body, name="ssd_bwd", grid=(SSD_STEPS, n_chunks),
        in_specs=[x_spec, b_spec, c_spec, dt_spec, vec_spec, vec_spec, vec_spec, h_spec, z_spec, g_spec, ct_spec],
        out_specs=[x_spec, out_b, out_b, dt_spec, vec_spec, vec_spec, vec_spec, x_spec, g_spec],
        out_shape=[jax.ShapeDtypeStruct((SEQ, D_SSM), F32), jax.ShapeDtypeStruct((SEQ, N_GROUPS * N_STATE), F32),
                   jax.ShapeDtypeStruct((SEQ, N_GROUPS * N_STATE), F32), jax.ShapeDtypeStruct((SEQ, DT_PAD), BF16),
                   jax.ShapeDtypeStruct((1, DT_PAD), F32), jax.ShapeDtypeStruct((1, DT_PAD), F32),
                   jax.ShapeDtypeStruct((1, DT_PAD), F32), jax.ShapeDtypeStruct((SEQ, D_SSM), BF16),
                   jax.ShapeDtypeStruct((1, D_SSM), F32)],
        scratch_shapes=[pltpu.VMEM((GROUPS_PER_STEP, GROUP_WIDTH, N_STATE), F32)],
        compiler_params=_params(("parallel", "arbitrary")),
    )(xbc, xbc, xbc, dt_raw, bias, alog, dsk, h_in, proj, g_out, dmix)


CROSS_HEAD = 128
CROSS_ROWS = 1024


def _cross_head(q, k, v, gq, gk):
    qn = _rms(q, gq) * (CROSS_HEAD ** -0.5)
    kn = _rms(k, gk)
    s = _bdot(qn, kn, NT)
    p = jnp.exp(s - lax.stop_gradient(jnp.max(s, axis=-1, keepdims=True)))
    return _bdot(p, v, NN) * (1.0 / jnp.sum(p, axis=-1, keepdims=True))


def _cross_specs():
    q_spec = pl.BlockSpec((CROSS_ROWS, CROSS_HEAD), lambda h, i: (i, h))
    k_spec = pl.BlockSpec((N_MEM, CROSS_HEAD), lambda h, i: (0, h))
    v_spec = pl.BlockSpec((N_MEM, CROSS_HEAD), lambda h, i: (0, 4 + h))
    g_spec = pl.BlockSpec((1, CROSS_HEAD), lambda h, i: (0, 0))
    return q_spec, k_spec, v_spec, g_spec


def _cross_fwd(qc, kv, gq, gk):
    q_spec, k_spec, v_spec, g_spec = _cross_specs()

    def body(q_ref, k_ref, v_ref, gq_ref, gk_ref, o_ref):
        o_ref[...] = _cross_head(q_ref[...], k_ref[...], v_ref[...], gq_ref[...], gk_ref[...]).astype(o_ref.dtype)

    return pl.pallas_call(
        body, name="cross_fwd", grid=(4, SEQ // CROSS_ROWS),
        in_specs=[q_spec, k_spec, v_spec, g_spec, g_spec], out_specs=q_spec,
        out_shape=jax.ShapeDtypeStruct((SEQ, D_CROSS), BF16),
        compiler_params=_params(("parallel", "parallel")),
    )(qc, kv, kv, gq, gk)


def _cross_bwd(qc, kv, gq, gk, do):
    q_spec, k_spec, v_spec, g_spec = _cross_specs()

    def body(q_ref, k_ref, v_ref, gq_ref, gk_ref, do_ref, dq_ref, dk_ref, dv_ref, dgq_ref, dgk_ref):
        _, pullback = jax.vjp(_cross_head, q_ref[...], k_ref[...], v_ref[...], gq_ref[...], gk_ref[...])
        dq, dk, dv, dgq, dgk = pullback(do_ref[...].astype(F32))
        dq_ref[...] = dq.astype(dq_ref.dtype)
        row0 = pl.program_id(1) == 0
        all0 = jnp.logical_and(row0, pl.program_id(0) == 0)
        for o_ref, val, init in ((dk_ref, dk, row0), (dv_ref, dv, row0), (dgq_ref, dgq, all0), (dgk_ref, dgk, all0)):
            @pl.when(init)
            def _(o_ref=o_ref, val=val):
                o_ref[...] = val

            @pl.when(jnp.logical_not(init))
            def _(o_ref=o_ref, val=val):
                o_ref[...] += val

    return pl.pallas_call(
        body, name="cross_bwd", grid=(4, SEQ // CROSS_ROWS),
        in_specs=[q_spec, k_spec, v_spec, g_spec, g_spec, q_spec],
        out_specs=[q_spec, k_spec, k_spec, g_spec, g_spec],
        out_shape=[jax.ShapeDtypeStruct((SEQ, D_CROSS), BF16), jax.ShapeDtypeStruct((N_MEM, D_CROSS), F32),
                   jax.ShapeDtypeStruct((N_MEM, D_CROSS), F32), jax.ShapeDtypeStruct((1, CROSS_HEAD), F32),
                   jax.ShapeDtypeStruct((1, CROSS_HEAD), F32)],
        compiler_params=_params(("arbitrary", "arbitrary")),
    )(qc, kv, kv, gq, gk, do)


def _loss_epilogue(acc, residual, target):
    err = acc + residual - target
    dy = err * (1.0 / D_MODEL)
    part = jnp.sum(jnp.sum(err * err, axis=1, keepdims=True), axis=0, keepdims=True) * (0.5 / D_MODEL)
    return dy, dy, part


def _pad_heads(v):
    return jnp.pad(v.reshape(N_GROUPS, HEADS_PER_GROUP), ((0, 0), (0, 128 - HEADS_PER_GROUP))).reshape(1, DT_PAD)


def _unpad_heads(v):
    return v.reshape(v.shape[0], N_GROUPS, 128)[:, :, :HEADS_PER_GROUP].reshape(v.shape[0], N_DT)


def _rope_tables(positions):
    half = ROT // 2
    inv_freq = ROPE_THETA ** (-2.0 * jnp.arange(half, dtype=F32) / ROT)
    ang = positions.reshape(SEQ, 1).astype(F32) * inv_freq
    cos, sin = jnp.cos(ang), jnp.sin(ang)
    ones, zeros = jnp.ones((SEQ, HEAD - ROT), F32), jnp.zeros((SEQ, HEAD - ROT), F32)
    cos_h = jnp.concatenate([cos, cos, ones], axis=1)
    sin_h = jnp.concatenate([-sin, sin, zeros], axis=1)
    return jnp.tile(cos_h, (1, 2)), jnp.tile(sin_h, (1, 2))


def _add_res(acc, res):
    return (acc + res,)


def _add_res_and_norm(acc, res, g):
    y = acc + res
    return y, _rms(y, g)


def _settle(grads, *after):
    if hasattr(grads, "settle"):
        grads.settle(*after)


def _take_token(grads):
    token = getattr(grads, "token", None)
    if token is None:
        return ()
    grads.token = None
    return (token,)


def _local_step(x, mem, positions, target, p, w, more_weights=None, grads=None, h=None):
    grads = {} if grads is None else grads
    w = dict(w)
    cos, sin = _rope_tables(positions)
    gq2, gk2 = jnp.tile(p["g_q"], (1, 2)), jnp.tile(p["g_k"], (1, 2))
    bias, alog, dsk = _pad_heads(p["dt_bias"]), _pad_heads(p["a_log"]), _pad_heads(p["d_skip"])
    norm_out = [(D_MODEL, BF16, D_MODEL, 0, False)]

    if h is None:
        h = _rowwise(_norm_fn, [_full(x)], [_full(p["g_mix"])], norm_out, name="norm_in")[0]
    proj = _matmul(h, w["w_in"], mode="nn", name="in_proj", outs=[F32], n_cols=D_MAIN)
    dt_raw = _matmul(h, w["w_dt"], mode="nn", name="dt_proj", outs=[F32])
    pairs = D_ATTN // 128
    qk_rows = [(proj, 128, 0, True), (proj, 128, pairs, True), (proj, 128, 2 * pairs, True), _full(cos), _full(sin)]
    qk_vecs = [_full(gq2), _full(gk2)]
    qn, kn, vn = _rowwise(_qk_fn, qk_rows, qk_vecs, [(D_ATTN, F32, 128, 0, True)] * 3, name="qk_prep", groups=8, tr=1024)
    branches = [_attention_fwd(qn, kn, vn, b) for b in range(3)]
    merge_rows = [_full(o) for o, _ in branches] + [_full(lse) for _, lse in branches]
    attn = _rowwise(_merge_fn, merge_rows, [_full(p["g_attn_out"])], [(D_ATTN, BF16, D_ATTN, 0, False)], name="attn_merge")[0]
    xbc = _conv_fwd(proj, p["conv_w"], p["conv_b"])
    ssm, h_in = _ssd_fwd(xbc, dt_raw, bias, alog, dsk, proj, p["g_ssm_out"])
    mix = jnp.concatenate([attn, ssm], axis=1)
    if more_weights is not None:
        w.update(more_weights("mixer_done", mix))
    x1, hc = _matmul(mix, w["w_out"], mode="nn", name="out_proj", outs=[F32, BF16], extra=(x,), vecs=(p["g_cross"],),
                     epilogue=_add_res_and_norm, tm=512, tn=D_MODEL)
    memh = _rowwise(_norm_fn, [_full(mem)], [_full(p["g_mem"])], norm_out, name="norm_mem", n_rows=N_MEM, tr=N_MEM)[0]
    qc = _matmul(hc, w["w_cq"], mode="nn", name="cq_proj", outs=[F32])
    if more_weights is not None:
        w.update(more_weights("cross_started", qc))
    kv = _matmul(memh, w["w_ckv"], mode="nn", name="ckv_proj", outs=[F32])
    oc = _cross_fwd(qc, kv, p["g_cq"], p["g_ck"])
    x2, hm = _matmul(oc, w["w_co"], mode="nn", name="co_proj", outs=[F32, BF16], extra=(x1,), vecs=(p["g_mlp"],),
                     epilogue=_add_res_and_norm, tm=512, tn=D_MODEL)
    if more_weights is not None:
        w.update(more_weights("cross_done", hm))
    u, act = _matmul(hm, w["w_up"], mode="nn", name="up_proj", outs=[F32, BF16],
                     epilogue=lambda acc: (acc, jnp.square(jnp.maximum(acc, 0.0))))
    dy, dyb, loss_tiles = _matmul(act, w["w_down"], mode="nn", name="down_proj", outs=[F32, BF16], extra=(x2, target),
                                  epilogue=_loss_epilogue, tile_sums=1)
    loss = jnp.sum(loss_tiles).reshape(1, 1)

    grads["w_down"] = _matmul(act, dyb, mode="tn", name="dw_down", outs=[BF16], after=_take_token(grads))
    du = _matmul(dyb, w["w_down"], mode="nt", name="d_act", outs=[BF16], extra=(u,), after=_take_token(grads),
                 epilogue=lambda acc, uu: (acc * (2.0 * jnp.maximum(uu, 0.0)),))
    _settle(grads, du)
    grads["w_up"] = _matmul(hm, du, mode="tn", name="dw_up", outs=[BF16], col_shards=4, after=_take_token(grads))
    dhm = _matmul(du, w["w_up"], mode="nt", name="d_hm", outs=[F32], after=_take_token(grads), tk=4096)
    _settle(grads, dhm)
    dx2, grads["g_mlp"] = _rowwise_vjp(
        _norm_fn, [_full(x2)], [_full(p["g_mlp"])], [[_full(dhm)]],
        [(0, D_MODEL, F32, D_MODEL, 0, False, _full(dy))], [(0, D_MODEL, D_MODEL, 0, False)], name="norm_mlp_bwd")
    grads["w_co"] = _matmul(oc, dx2, mode="tn", name="dw_co", outs=[BF16], col_shards=4, after=_take_token(grads))
    doc = _matmul(dx2, w["w_co"], mode="nt", name="d_oc", outs=[BF16])
    dqc, dkc, dvc, grads["g_cq"], grads["g_ck"] = _cross_bwd(qc, kv, p["g_cq"], p["g_ck"], doc)
    grads["w_cq"] = _matmul(hc, dqc, mode="tn", name="dw_cq", outs=[BF16])
    dhc = _matmul(dqc, w["w_cq"], mode="nt", name="d_hc", outs=[F32])
    dkv = jnp.concatenate([dkc, dvc], axis=1)
    grads["w_ckv"] = _matmul(memh, dkv, mode="tn", name="dw_ckv", outs=[BF16])
    dmemh = _matmul(dkv, w["w_ckv"], mode="nt", name="d_memh", outs=[F32])
    grads["g_mem"] = _rowwise_vjp(_norm_fn, [_full(mem)], [_full(p["g_mem"])], [[_full(dmemh)]], [],
                                  [(0, D_MODEL, D_MODEL, 0, False)], name="norm_mem_bwd", n_rows=N_MEM, tr=N_MEM)[0]
    dx1, grads["g_cross"] = _rowwise_vjp(
        _norm_fn, [_full(x1)], [_full(p["g_cross"])], [[_full(dhc)]],
        [(0, D_MODEL, F32, D_MODEL, 0, False, _full(dx2))], [(0, D_MODEL, D_MODEL, 0, False)], name="norm_cross_bwd")
    grads["w_out"] = _matmul(mix, dx1, mode="tn", name="dw_out", outs=[BF16])
    dmix = _matmul(dx1, w["w_out"], mode="nt", name="d_mix", outs=[F32], after=_take_token(grads))
    _settle(grads, dmix)
    merge_grads = [(i, D_ATTN, F32, D_ATTN, 0, False, None) for i in range(6)]
    *dol, grads["g_attn_out"] = _rowwise_vjp(
        _merge_fn, merge_rows, [_full(p["g_attn_out"])], [[(dmix, D_ATTN, 0, False)]],
        merge_grads, [(0, D_ATTN, D_ATTN, 0, False)], name="attn_merge_bwd", tr=256, after=_take_token(grads))
    dqkv = [_attention_bwd(qn, kn, vn, *branches[b], dol[b], dol[3 + b], b) for b in range(3)]
    qk_cts = [[(dqkv[b][i], 128, 0, True) for b in range(3)] for i in range(3)]
    dq, dk, dv, dgq2, dgk2 = _rowwise_vjp(
        _qk_fn, qk_rows, qk_vecs, qk_cts, [(i, D_ATTN, BF16, 128, 0, True, None) for i in range(3)],
        [(0, 128, 128, 0, False), (1, 128, 128, 0, False)], name="qk_prep_bwd", groups=8, tr=1024)
    grads["g_q"] = dgq2[:, :HEAD] + dgq2[:, HEAD:]
    grads["g_k"] = dgk2[:, :HEAD] + dgk2[:, HEAD:]
    dxs, db, dc, ddt, dbias, dalog, ddsk, dz, grads["g_ssm_out"] = _ssd_bwd(xbc, dt_raw, bias, alog, dsk, h_in, proj,
                                                                             p["g_ssm_out"], dmix)
    grads["dt_bias"], grads["a_log"], grads["d_skip"] = _unpad_heads(dbias), _unpad_heads(dalog), _unpad_heads(ddsk)
    dxbc_raw, dconv_w, grads["conv_b"] = _conv_bwd(proj, p["conv_w"], p["conv_b"], dxs, db, dc)
    grads["conv_w"] = dconv_w[:4]
    dproj = jnp.concatenate([dq, dk, dv, dz, dxbc_raw], axis=1)
    grads["w_main"] = _matmul(h, dproj, mode="tn", name="dw_main", outs=[BF16], out_cols=D_MAIN + N_DT)
    grads["w_dt"] = _matmul(h, ddt, mode="tn", name="dw_dt", outs=[BF16])
    dh = _matmul(dproj, w["w_in"], mode="nt", name="d_h_main", outs=[F32], after=_take_token(grads))
    dh = _matmul(ddt, w["w_dt"], mode="nt", name="d_h_dt", outs=[F32], extra=(dh,), epilogue=_add_res)
    grad_x, grads["g_mix"] = _rowwise_vjp(
        _norm_fn, [_full(x)], [_full(p["g_mix"])], [[_full(dh)]],
        [(0, D_MODEL, F32, D_MODEL, 0, False, _full(dx1))], [(0, D_MODEL, D_MODEL, 0, False)], name="norm_in_bwd")
    return loss, grad_x, grads


MATRICES = ("w_in", "w_out", "w_cq", "w_ckv", "w_co", "w_up", "w_down")
ROW_SHARDED = ("w_out", "w_cq", "w_ckv", "w_down")
N_CHIPS = 4
ANY = pl.BlockSpec(memory_space=pl.ANY)


def _place():
    return lax.axis_index("x"), lax.axis_index("y"), lax.axis_index("c")


def _other_chips(x, y):
    return [(1 - x, y), (x, 1 - y), (1 - x, 1 - y)]


def _remote(src, dst, send_sem, recv_sem, device):
    return pltpu.make_async_remote_copy(src_ref=src, dst_ref=dst, send_sem=send_sem, recv_sem=recv_sem,
                                        device_id=device, device_id_type=MESH)


def _gathered_shape(name, shard):
    rows, cols = shard.shape
    if name == "w_in":
        return (N_CHIPS, rows, cols)
    return (N_CHIPS * rows, cols) if name in ROW_SHARDED else (rows, N_CHIPS * cols)


def _shard_window(name, ref, rows, cols, chip, half):
    r0, nr = (0, rows) if half is None else (half * (rows // 2), rows // 2)
    if name == "w_in":
        return ref.at[chip, pl.ds(r0, nr), :]
    if name in ROW_SHARDED:
        return ref.at[pl.ds(chip * rows + r0, nr), :]
    return ref.at[pl.ds(r0, nr), pl.ds(pl.multiple_of(chip * cols, 128), cols)]


def _cast_into_gathered(w, name, chip, after=()):
    rows, cols = w.shape
    tr = _tile(rows, ROW_TILE)

    def body(chip_ref, w_ref, *rest):
        rest[-1][...] = w_ref[...].astype(BF16)

    if name == "w_in":
        out_spec = pl.BlockSpec((None, tr, cols), lambda i, chip_ref: (chip_ref[0], i, 0))
    elif name in ROW_SHARDED:
        out_spec = pl.BlockSpec((tr, cols), lambda i, chip_ref: (chip_ref[0] * (rows // tr) + i, 0))
    else:
        out_spec = pl.BlockSpec((tr, cols), lambda i, chip_ref: (i, chip_ref[0]))
    grid_spec = pltpu.PrefetchScalarGridSpec(
        num_scalar_prefetch=1, grid=(rows // tr,),
        in_specs=[pl.BlockSpec((tr, cols), lambda i, chip_ref: (i, 0))] + [pl.BlockSpec(memory_space=pl.ANY)] * len(after),
        out_specs=out_spec)
    return pl.pallas_call(body, name="cast_" + name, grid_spec=grid_spec,
                          out_shape=jax.ShapeDtypeStruct(_gathered_shape(name, w), BF16),
                          compiler_params=_params(("parallel",)))(chip.reshape(1).astype(jnp.int32), w, *after)


def _w_in_columns(arr, to_shards):
    rows, piece = D_MODEL, (D_MAIN + N_DT) // N_CHIPS
    tr = ROW_TILE

    def body(a_ref, o_ref):
        for j in range(N_CHIPS):
            if to_shards:
                o_ref[j] = a_ref[:, pl.ds(piece * j, piece)]
            else:
                o_ref[:, pl.ds(piece * j, piece)] = a_ref[j]

    pieces = pl.BlockSpec((N_CHIPS, tr, piece), lambda i: (0, i, 0))
    matrix = pl.BlockSpec((tr, N_CHIPS * piece), lambda i: (i, 0))
    out_dims = (N_CHIPS, rows, piece) if to_shards else (rows, N_CHIPS * piece)
    return pl.pallas_call(
        body, name="w_in_to_shards" if to_shards else "w_in_from_shards", grid=(rows // tr,),
        in_specs=[matrix if to_shards else pieces], out_specs=pieces if to_shards else matrix,
        out_shape=jax.ShapeDtypeStruct(out_dims, arr.dtype), compiler_params=_params(("parallel",)))(arr)


HBM = pl.BlockSpec(memory_space=pltpu.HBM)
SEM = pl.BlockSpec(memory_space=pltpu.SEMAPHORE)
EFFECT = pltpu.SideEffectType.DATAFLOW_SIDE_EFFECTING


def _split_start(name, bufs, plan, counts, after=()):
    n, n_g, n_after = len(bufs), len(counts), len(after)

    def body(*refs):
        ins, sems, token = refs[:n], refs[n + n_after:n + n_after + 2 * n_g], refs[-1]
        for g, copies in enumerate(plan(ins)):
            for i, (src, dst, device, _) in enumerate(copies):
                _remote(src, dst, sems[2 * g].at[i], sems[2 * g + 1].at[i], device).start()
        token[...] = jnp.zeros_like(token)

    sem_shapes = [pltpu.SemaphoreType.DMA((cnt,)) for cnt in counts for _ in range(2)]
    res = pl.pallas_call(
        body, name=name,
        out_shape=(*sem_shapes, *[pltpu.HBM(b.shape, b.dtype) for b in bufs], jax.ShapeDtypeStruct((8, 128), F32)),
        in_specs=(*(HBM,) * n, *(ANY,) * n_after),
        out_specs=(*(SEM,) * (2 * n_g), *(HBM,) * n, pl.BlockSpec(memory_space=pltpu.VMEM)),
        input_output_aliases={i: 2 * n_g + i for i in range(n)},
        compiler_params=pltpu.CompilerParams(has_side_effects=EFFECT),
    )(*[pltpu.with_memory_space_constraint(b, pltpu.HBM) for b in bufs], *after)
    sems = [(res[2 * g], res[2 * g + 1]) for g in range(n_g)]
    return sems, list(res[2 * n_g:2 * n_g + n]), res[-1]


def _split_wait(name, bufs, sems, plan, *after):
    n = len(bufs)

    def body(*refs):
        ins, send, recv = refs[:n], refs[n], refs[n + 1]
        (copies,) = plan(ins)
        for i, (src, _, device, landing) in enumerate(copies):
            cp = _remote(src, landing, send.at[i], recv.at[i], device)
            cp.wait_send()
            cp.wait_recv()

    res = pl.pallas_call(
        body, name=name, out_shape=tuple(pltpu.HBM(b.shape, b.dtype) for b in bufs),
        in_specs=(*(HBM,) * n, SEM, SEM, *(ANY,) * len(after)), out_specs=(HBM,) * n,
        input_output_aliases={i: i for i in range(n)},
        compiler_params=pltpu.CompilerParams(has_side_effects=EFFECT),
    )(*bufs, sems[0], sems[1], *after)
    return list(res)


def _ici_plan(names, shard_shapes):
    def plan(refs):
        x, y, c = _place()
        copies = []
        for ref, name in zip(refs, names):
            win = _shard_window(name, ref, *shard_shapes[name], 2 * x + y, c)
            for px, py in _other_chips(x, y):
                copies.append((win, win, (px, py, c), _shard_window(name, ref, *shard_shapes[name], 2 * px + py, c)))
        return [copies]
    return plan


def _pass_on_plan(names, shard_shapes):
    def plan(refs):
        x, y, c = _place()
        copies = []
        for ref, name in zip(refs, names):
            for px, py in _other_chips(x, y):
                win = _shard_window(name, ref, *shard_shapes[name], 2 * px + py, c)
                copies.append((win, win, (x, y, 1 - c), _shard_window(name, ref, *shard_shapes[name], 2 * px + py, 1 - c)))
        return [copies]
    return plan


def _swap_plan(n_pairs):
    def plan(refs):
        x, y, c = _place()
        return [[(src.at[:, 1 - c], dst, (x, y, 1 - c), dst) for src, dst in zip(refs[:n_pairs], refs[n_pairs:])]]
    return plan


def _share_plan(n_pairs):
    def plan(refs):
        x, y, c = _place()
        return [[(src, dst, (x, y, 1 - c), dst)] for src, dst in zip(refs[:n_pairs], refs[n_pairs:])]
    return plan


def _scatter_plan(n_pairs):
    def plan(refs):
        x, y, c = _place()
        copies = []
        for src, dst in zip(refs[:n_pairs], refs[n_pairs:]):
            for k, (px, py) in enumerate(_other_chips(x, y)):
                copies.append((src.at[2 * px + py], dst.at[k], (px, py, c), dst.at[k]))
        return [copies]
    return plan


def _sibling_swap(arrs, name):
    n = len(arrs)

    def body(*refs):
        ins, outs, send, recv = refs[:n], refs[n:2 * n], refs[2 * n], refs[2 * n + 1]
        x, y, c = _place()
        cps = [_remote(ins[w].at[:, 1 - c], outs[w], send.at[w], recv.at[w], (x, y, 1 - c)) for w in range(n)]
        for cp in cps:
            cp.start()
        for cp in cps:
            cp.wait()

    return pl.pallas_call(
        body, name=name, in_specs=[ANY] * n, out_specs=[ANY] * n,
        out_shape=[jax.ShapeDtypeStruct((a.shape[0],) + a.shape[2:], a.dtype) for a in arrs],
        scratch_shapes=[pltpu.SemaphoreType.DMA((n,))] * 2,
    )(*arrs)


def _small_allreduce(buf, name, after=()):
    rows = buf.shape[0]

    def body(x_ref, *rest):
        out_ref, all_ref, send_sems, recv_sems, local_sem = rest[len(after):]
        x, y, c = _place()
        me, sibling, chips = (x, y, c), (x, y, 1 - c), _other_chips(x, y)

        def block(px, py, pc):
            return all_ref.at[pl.ds((4 * px + 2 * py + pc) * rows, rows), :]

        def copy(k, blk, to, src=None):
            return _remote(block(*blk) if src is None else src, block(*blk), send_sems.at[k], recv_sems.at[k], to)

        own = pltpu.make_async_copy(x_ref, block(*me), local_sem)
        own.start()
        first = [copy(0, me, sibling, src=x_ref)] + [copy(1 + j, me, (*chip, c), src=x_ref) for j, chip in enumerate(chips)]
        for cp in first:
            cp.start()
        passed = [copy(4 + j, (*chip, c), sibling) for j, chip in enumerate(chips)]
        for j, chip in enumerate(chips):
            copy(1 + j, (*chip, c), me).wait_recv()
            passed[j].start()
        copy(0, sibling, me).wait_recv()
        for j, chip in enumerate(chips):
            copy(4 + j, (*chip, 1 - c), me).wait_recv()
        for cp in first + passed:
            cp.wait_send()
        own.wait()
        acc = all_ref[pl.ds(0, rows), :]
        for d in range(1, 8):
            acc = acc + all_ref[pl.ds(d * rows, rows), :]
        out_ref[...] = acc

    vmem = pl.BlockSpec(memory_space=pltpu.VMEM)
    return pl.pallas_call(
        body, name=name, in_specs=[vmem] + [ANY] * len(after), out_specs=vmem,
        out_shape=jax.ShapeDtypeStruct(buf.shape, F32),
        scratch_shapes=[pltpu.VMEM((8 * rows, 128), F32), pltpu.SemaphoreType.DMA((7,)), pltpu.SemaphoreType.DMA((7,)),
                        pltpu.SemaphoreType.DMA],
    )(buf, *after)


ROW_TILE = 256
BIG_ROW_TILE = 1024


def _add_halves(arr, recv, c, name):
    _, _, hr, cols = arr.shape
    tr = _tile(hr, BIG_ROW_TILE)

    def body(c_ref, a_ref, r_ref, o_ref):
        o_ref[...] = (a_ref[...].astype(F32) + r_ref[...].astype(F32)).astype(o_ref.dtype)

    piece = pl.BlockSpec((None, tr, cols), lambda j, i, c_ref: (j, i, 0))
    grid_spec = pltpu.PrefetchScalarGridSpec(
        num_scalar_prefetch=1, grid=(N_CHIPS, hr // tr),
        in_specs=[pl.BlockSpec((None, None, tr, cols), lambda j, i, c_ref: (j, c_ref[0], i, 0)), piece], out_specs=piece)
    return pl.pallas_call(body, name=name, grid_spec=grid_spec, out_shape=jax.ShapeDtypeStruct(recv.shape, BF16),
                          compiler_params=_params(("parallel", "parallel")))(c.reshape(1).astype(jnp.int32), arr, recv)


def _flip_slot(d):
    return jnp.where(d == 1, 1, jnp.where(d == 3, 2, 0))


def _sum_chips(p, q, chip, name):
    _, hr, cols = p.shape
    tr = _tile(hr, BIG_ROW_TILE)

    def body(chip_ref, p_ref, q_ref, o_ref):
        j = pl.program_id(1)
        term = jnp.where(j == chip_ref[0], p_ref[...].astype(F32), q_ref[...].astype(F32))

        @pl.when(j == 0)
        def _():
            o_ref[...] = term

        @pl.when(j != 0)
        def _():
            o_ref[...] += term

    grid_spec = pltpu.PrefetchScalarGridSpec(
        num_scalar_prefetch=1, grid=(hr // tr, N_CHIPS),
        in_specs=[pl.BlockSpec((None, tr, cols), lambda i, j, chip_ref: (chip_ref[0], i, 0)),
                  pl.BlockSpec((None, tr, cols), lambda i, j, chip_ref: (_flip_slot(j ^ chip_ref[0]), i, 0))],
        out_specs=pl.BlockSpec((tr, cols), lambda i, j, chip_ref: (i, 0)))
    return pl.pallas_call(body, name=name, grid_spec=grid_spec, out_shape=jax.ShapeDtypeStruct((hr, cols), F32),
                          compiler_params=_params(("parallel", "arbitrary")))(chip.reshape(1).astype(jnp.int32), p, q)


def _adamw_halves(w, g_own, g_other, m, v, c, name):
    rows, cols = w.shape
    tr = _tile(rows // 2, ROW_TILE)
    per_half = rows // 2 // tr

    def body(c_ref, w_ref, own_ref, other_ref, m_ref, v_ref, g_ref, d_ref, nm_ref, nv_ref):
        mine = (pl.program_id(0) // per_half) == c_ref[0]
        g_ = jnp.where(mine, own_ref[...], other_ref[...])
        g_ref[...] = g_
        d_ref[...], nm_ref[...], nv_ref[...] = _adamw_math(w_ref[...], g_, m_ref[...], v_ref[...])

    blk = pl.BlockSpec((tr, cols), lambda i, c_ref: (i, 0))
    own = pl.BlockSpec((tr, cols), lambda i, c_ref: (jnp.where(i // per_half == c_ref[0], i % per_half, 0), 0))
    other = pl.BlockSpec((tr, cols), lambda i, c_ref: (jnp.where(i // per_half == c_ref[0], 0, i % per_half), 0))
    grid_spec = pltpu.PrefetchScalarGridSpec(num_scalar_prefetch=1, grid=(rows // tr,),
                                             in_specs=[blk, own, other, blk, blk], out_specs=[blk] * 4)
    return pl.pallas_call(body, name=name, grid_spec=grid_spec, out_shape=[jax.ShapeDtypeStruct(w.shape, F32)] * 4,
                          compiler_params=_params(("parallel",)))(c.reshape(1).astype(jnp.int32), w, g_own, g_other, m, v)


W_IN_COLS = (D_MAIN + N_DT) // N_CHIPS
W_IN_MAIN = W_IN_COLS // 128 * 128
W_IN_TAIL = W_IN_COLS - 128
W_IN_PARTS = ((0, W_IN_MAIN), (W_IN_TAIL, 128))


def _cast_w_in_transposed(w_t, chip, after=()):
    def body(chip_ref, w_ref, *rest):
        for start, size in W_IN_PARTS:
            rest[-1][:, pl.ds(start, size)] = w_ref[pl.ds(start, size), :].T.astype(BF16)

    grid_spec = pltpu.PrefetchScalarGridSpec(
        num_scalar_prefetch=1, grid=(D_MODEL // ROW_TILE,),
        in_specs=[pl.BlockSpec((W_IN_COLS, ROW_TILE), lambda i, chip_ref: (0, i))] + [pl.BlockSpec(memory_space=pl.ANY)] * len(after),
        out_specs=pl.BlockSpec((None, ROW_TILE, W_IN_COLS), lambda i, chip_ref: (chip_ref[0], i, 0)))
    return pl.pallas_call(body, name="cast_w_in", grid_spec=grid_spec,
                          out_shape=jax.ShapeDtypeStruct((N_CHIPS, D_MODEL, W_IN_COLS), BF16),
                          compiler_params=_params(("parallel",)))(chip.reshape(1).astype(jnp.int32), w_t, *after)


def _adamw_w_in_transposed(w_t, g_own, g_other, m_t, v_t, c):
    per_half = D_MODEL // 2 // ROW_TILE

    def body(c_ref, w_ref, own_ref, other_ref, m_ref, v_ref, g_ref, d_ref, nm_ref, nv_ref):
        mine = (pl.program_id(0) // per_half) == c_ref[0]
        for start, size in W_IN_PARTS:
            cols, rows = pl.ds(start, size), pl.ds(start, size)
            g_ = jnp.where(mine, own_ref[:, cols], other_ref[:, cols]).T
            g_ref[rows, :] = g_
            d_ref[rows, :], nm_ref[rows, :], nv_ref[rows, :] = _adamw_math(w_ref[rows, :], g_, m_ref[rows, :], v_ref[rows, :])

    blk = pl.BlockSpec((W_IN_COLS, ROW_TILE), lambda i, c_ref: (0, i))
    own = pl.BlockSpec((ROW_TILE, W_IN_COLS), lambda i, c_ref: (jnp.where(i // per_half == c_ref[0], i % per_half, 0), 0))
    other = pl.BlockSpec((ROW_TILE, W_IN_COLS), lambda i, c_ref: (jnp.where(i // per_half == c_ref[0], 0, i % per_half), 0))
    grid_spec = pltpu.PrefetchScalarGridSpec(num_scalar_prefetch=1, grid=(D_MODEL // ROW_TILE,),
                                             in_specs=[blk, own, other, blk, blk], out_specs=[blk] * 4)
    return pl.pallas_call(body, name="adamw_w_in", grid_spec=grid_spec, out_shape=[jax.ShapeDtypeStruct(w_t.shape, F32)] * 4,
                          compiler_params=_params(("parallel",)))(c.reshape(1).astype(jnp.int32), w_t, g_own, g_other, m_t, v_t)


def _adamw_math(w, g, m, v):
    m_new = ADAM_B1 * m + (1.0 - ADAM_B1) * g
    v_new = ADAM_B2 * v + (1.0 - ADAM_B2) * (g * g)
    m_hat = m_new / (1.0 - ADAM_B1 ** ADAM_STEP)
    v_hat = v_new / (1.0 - ADAM_B2 ** ADAM_STEP)
    return -ADAM_LR * (m_hat / (jnp.sqrt(v_hat) + ADAM_EPS) + ADAM_WD * w), m_new, v_new


VECTORS = ("g_mix", "g_q", "g_k", "g_attn_out", "conv_b", "dt_bias", "a_log", "d_skip", "g_ssm_out", "g_cross", "g_mem",
           "g_cq", "g_ck", "g_mlp")
WEIGHTS = ("g_mix", "w_in", "g_q", "g_k", "g_attn_out", "conv_w", "conv_b", "dt_bias", "a_log", "d_skip", "g_ssm_out", "w_out",
           "g_cross", "g_mem", "w_cq", "w_ckv", "g_cq", "g_ck", "w_co", "g_mlp", "w_up", "w_down")


def _pack(parts):
    flat = jnp.concatenate([t.reshape(-1) for t in parts])
    total = -(-flat.shape[0] // 1024) * 1024
    return jnp.pad(flat, (0, total - flat.shape[0])).reshape(total // 128, 128)


def _rows_of(n):
    return -(-n // 128)


def _slot_rows(n):
    return -(-n // 1024) * 8


def _pack_rows(parts):
    rows = []
    for t in parts:
        flat = t.reshape(-1)
        rows.append(jnp.pad(flat, (0, 128 * _slot_rows(flat.shape[0]) - flat.shape[0])).reshape(-1, 128))
    return jnp.concatenate(rows)


def _adamw_vectors(summed, chip, vectors, conv):
    groups = list(vectors) + [conv]
    offsets, row = [], 0
    for w, _, _ in groups:
        offsets.append(row)
        row += _slot_rows(w.shape[1]) if w.shape[0] == 1 else _slot_rows(w.shape[0] * N_CHIPS * w.shape[1])
    conv_blocks = _rows_of(conv[0].shape[1])

    def body(chip_ref, sum_ref, *refs):
        ins, outs = refs[:3 * len(groups)], refs[3 * len(groups):]

        def update(i, g, idx):
            w_ref, m_ref, v_ref = ins[3 * i:3 * i + 3]
            delta, new_m, new_v = _adamw_math(w_ref[idx], g, m_ref[idx], v_ref[idx])
            for o_ref, val in zip(outs[4 * i:4 * i + 4], (g, delta, new_m, new_v)):
                o_ref[idx] = val

        for i, (w, _, _) in enumerate(vectors):
            for t in range(_rows_of(w.shape[1])):
                width = min(128, w.shape[1] - 128 * t)
                update(i, sum_ref[pl.ds(offsets[i] + t, 1), pl.ds(0, width)], (slice(None), pl.ds(128 * t, width)))
        for tap in range(conv[0].shape[0]):
            for blk in range(conv_blocks):
                src = offsets[-1] + tap * N_CHIPS * conv_blocks + chip_ref[0] * conv_blocks + blk
                update(len(vectors), sum_ref[pl.ds(src, 1), :], (pl.ds(tap, 1), pl.ds(128 * blk, 128)))

    def whole(a):
        return pl.BlockSpec(a.shape, lambda i, chip_ref: (0,) * a.ndim)

    operands = [t for group in groups for t in group]
    grid_spec = pltpu.PrefetchScalarGridSpec(
        num_scalar_prefetch=1, grid=(1,), in_specs=[whole(summed)] + [whole(t) for t in operands],
        out_specs=[whole(w) for w, _, _ in groups for _ in range(4)])
    res = pl.pallas_call(body, name="adamw_vectors", grid_spec=grid_spec,
                         out_shape=[jax.ShapeDtypeStruct(w.shape, F32) for w, _, _ in groups for _ in range(4)],
                         compiler_params=_params(("arbitrary",)))(chip.reshape(1).astype(jnp.int32), summed, *operands)
    return [res[4 * i:4 * i + 4] for i in range(len(groups))]


def _unpack(buf, shapes):
    flat, out, pos = buf.reshape(-1), [], 0
    for shape in shapes:
        size = math.prod(shape)
        out.append(flat[pos:pos + size].reshape(shape))
        pos += size
    return out


def kernel(x, mem, positions, g_mix, w_in, g_q, g_k, g_attn_out, conv_w, conv_b, dt_bias, a_log, d_skip, g_ssm_out, w_out, g_cross, g_mem, w_cq, w_ckv, g_cq, g_ck, w_co, g_mlp, w_up, w_down, loss_target, m_g_mix, m_w_in, m_g_q, m_g_k, m_g_attn_out, m_conv_w, m_conv_b, m_dt_bias, m_a_log, m_d_skip, m_g_ssm_out, m_w_out, m_g_cross, m_g_mem, m_w_cq, m_w_ckv, m_g_cq, m_g_ck, m_w_co, m_g_mlp, m_w_up, m_w_down, v_g_mix, v_w_in, v_g_q, v_g_k, v_g_attn_out, v_conv_w, v_conv_b, v_dt_bias, v_a_log, v_d_skip, v_g_ssm_out, v_w_out, v_g_cross, v_g_mem, v_w_cq, v_w_ckv, v_g_cq, v_g_ck, v_w_co, v_g_mlp, v_w_up, v_w_down):
    args = dict(locals())
    weights = {n: args[n][0] for n in WEIGHTS}
    mom_m = {n: args["m_" + n][0] for n in WEIGHTS}
    mom_v = {n: args["v_" + n][0] for n in WEIGHTS}
    x_idx, y_idx, c_idx = _place()
    chip = 2 * x_idx + y_idx

    shapes = {n: weights[n].shape for n in MATRICES}
    first, mid, late = ("w_in",), ("w_out", "w_cq", "w_ckv", "w_co"), ("w_up", "w_down")
    w_in_t, m_in_t, v_in_t = (jnp.swapaxes(t, 1, 2)[0] for t in (w_in, m_w_in, v_w_in))
    w_in_buf = [_cast_w_in_transposed(w_in_t, chip)]
    sems_in, w_in_buf, token = _split_start("gather_ici_start_w_in", w_in_buf, _ici_plan(first, shapes), [3])
    bufs = [_cast_into_gathered(weights[n], n, chip, after=(token,)) for n in mid + late]
    params = {n: weights[n].reshape(1, -1) for n in VECTORS}
    h_in = _rowwise(_norm_fn, [_full(x[0])], [_full(params["g_mix"])], [(D_MODEL, BF16, D_MODEL, 0, False)], name="norm_in",
                    after=(token,))[0]
    taps, tap_cols = weights["conv_w"].shape
    conv_parts = _small_allreduce(_pack([jnp.zeros((N_CHIPS, taps, tap_cols), F32).at[chip].set(0.5 * weights["conv_w"])]),
                                  "gather_conv_taps", after=(h_in, m_in_t, v_in_t, *bufs))
    w_in_buf = _split_wait("gather_ici_wait_w_in", w_in_buf, sems_in[0], _ici_plan(first, shapes), token, conv_parts)
    pass_sems, w_in_buf, token = _split_start("gather_pass_start_w_in", w_in_buf, _pass_on_plan(first, shapes), [3])
    plan = lambda refs: _ici_plan(mid, shapes)(refs[:4]) + _ici_plan(late, shapes)(refs[4:])
    sems_rest, bufs, token = _split_start("gather_ici_start_rest", bufs, plan, [12, 6], after=(token,))
    w_in_buf = _split_wait("gather_pass_wait_w_in", w_in_buf, pass_sems[0], _pass_on_plan(first, shapes), token)
    w_in_full = _w_in_columns(w_in_buf[0], to_shards=False)
    full = {"w_in": w_in_full,
            "w_dt": jnp.pad(w_in_full[:, D_MAIN:].reshape(D_MODEL, N_GROUPS, HEADS_PER_GROUP),
                            ((0, 0), (0, 0), (0, 128 - HEADS_PER_GROUP))).reshape(D_MODEL, DT_PAD)}
    in_flight = {}

    def more_weights(stage, after):
        if stage == "mixer_done":
            got = _split_wait("gather_ici_wait_mid", bufs[:4], sems_rest[0], _ici_plan(mid, shapes), after)
            sems, got, token = _split_start("gather_pass_start_mid", got, _pass_on_plan(mid, shapes), [12])
            return dict(zip(mid, _split_wait("gather_pass_wait_mid", got, sems[0], _pass_on_plan(mid, shapes), token)))
        if stage == "cross_started":
            got = _split_wait("gather_ici_wait_late", bufs[4:], sems_rest[1], _ici_plan(late, shapes), after)
            in_flight["late"] = _split_start("gather_pass_start_late", got, _pass_on_plan(late, shapes), [6])
            return {}
        sems, got, token = in_flight.pop("late")
        return dict(zip(late, _split_wait("gather_pass_wait_late", got, sems[0], _pass_on_plan(late, shapes), token, after)))

    params["conv_w"] = _unpack(conv_parts, [(N_CHIPS, taps, tap_cols)])[0].transpose(1, 0, 2).reshape(taps, N_CHIPS * tap_cols)

    groups = (("w_down",), ("w_up",), ("w_co", "w_cq", "w_ckv", "w_out"), ("w_in",))
    scattered = []

    class GradStore(dict):
        pending = None

        def __setitem__(self, name, value):
            super().__setitem__(name, value)
            if "w_main" in self and "w_dt" in self and "w_in" not in self:
                gw_in = lax.dynamic_update_slice(self["w_main"], _unpad_heads(self["w_dt"]), (0, D_MAIN))
                self["w_in"] = _w_in_columns(gw_in, to_shards=True)
            for group in groups:
                if name in group and all(n in self for n in group):
                    self.settle()
                    pieces = [self[n].reshape(N_CHIPS, 2, shapes[n][0] // 2, shapes[n][1]) for n in group]
                    if group == groups[-1]:
                        self.scatter(group, pieces, _sibling_swap(pieces, "grad_swap_" + group[0]))
                    else:
                        landing = [lax.empty((N_CHIPS,) + a.shape[2:], BF16) for a in pieces]
                        sems, thru, self.token = _split_start("grad_swap_start_" + group[0], pieces + landing,
                                                              _swap_plan(len(pieces)), [len(pieces)])
                        self.pending = (group, sems[0], thru)

        def settle(self, *after):
            if self.pending is not None:
                group, sems, thru = self.pending
                self.pending = None
                thru = _split_wait("grad_swap_wait_" + group[0], thru, sems, _swap_plan(len(group)), *after)
                self.scatter(group, thru[:len(group)], thru[len(group):])

        def scatter(self, group, pieces, from_sibling):
            sums = [_add_halves(a, r, c_idx, "add_halves_" + n) for n, a, r in zip(group, pieces, from_sibling)]
            landing = [lax.empty((3,) + s.shape[1:], BF16) for s in sums]
            sems, thru, self.token = _split_start("grad_scatter_start_" + group[0], sums + landing,
                                                  _scatter_plan(len(sums)), [3 * len(sums)])
            scattered.append((group, sems[0], thru))

    loss, grad_x, grads = _local_step(x[0], mem[0], positions[0], loss_target[0], params, full, more_weights, GradStore(),
                                      h_in)

    out_g, out_d, out_m, out_v = {}, {}, {}, {}

    def finish(entries, order, token):
        halves = {}
        for group, sems, thru in entries:
            thru = _split_wait("grad_scatter_wait_" + group[0], thru, sems, _scatter_plan(len(group)), token)
            for i, n in enumerate(group):
                halves[n] = _sum_chips(thru[i], thru[len(group) + i], chip, "sum_chips_" + n)
        sources = [halves[n] for n in order]
        landing = [lax.empty(s.shape, F32) for s in sources]
        sems, thru, token = _split_start("grad_share_start_" + order[0], sources + landing, _share_plan(len(order)),
                                         [1] * len(order))
        for i, n in enumerate(order):
            own, other = _split_wait("grad_share_wait_" + n, [thru[i], thru[len(order) + i]], sems[i], _share_plan(1), token)
            if n == "w_in":
                res_t = _adamw_w_in_transposed(w_in_t, own, other, m_in_t, v_in_t, c_idx)
                out_g[n], out_d[n], out_m[n], out_v[n] = (t.T for t in res_t)
            else:
                out_g[n], out_d[n], out_m[n], out_v[n] = _adamw_halves(weights[n], own, other, mom_m[n], mom_v[n], c_idx,
                                                                       "adamw_" + n)
            token = out_v[n]
        return token

    token = finish(scattered[:-1], ("w_cq", "w_co", "w_ckv", "w_out", "w_up", "w_down"), grad_x)
    finish(scattered[-1:], ("w_in",), token)

    names = VECTORS + ("conv_w",)
    summed = _small_allreduce(_pack_rows([grads[n] for n in names] + [loss]), "allreduce_vectors")
    total_loss = summed[sum(_slot_rows(grads[n].size) for n in names), 0]
    small_out = _adamw_vectors(summed, chip, [(args[n], args["m_" + n], args["v_" + n]) for n in VECTORS],
                               (weights["conv_w"], mom_m["conv_w"], mom_v["conv_w"]))
    for n, res in zip(names, small_out):
        out_g[n], out_d[n], out_m[n], out_v[n] = (t.reshape(weights[n].shape) for t in res)

    outs =[total_loss, grad_x[None]]
    for group in (out_g, out_d, out_m, out_v):
        outs += [group[n][None] for n in WEIGHTS]
    return tuple(outs)
```

```python
import functools
import math

import jax
import jax.numpy as jnp
from jax import lax
from jax.experimental import pallas as pl
from jax.experimental.pallas import tpu as pltpu

F32 = jnp.float32
BF16 = jnp.bfloat16

SEQ = 2048
D_MODEL = 2048
HEAD = 64
D_ATTN = 1024
D_SSM = 1024
N_GROUPS = 4
N_STATE = 128
CHUNK = 128
ATT_BLK = 128
N_MEM = 256
D_CROSS = 512
D_MAIN = 6144
N_DT = 16
DT_PAD = 512
ROT = 16
ROPE_THETA = 500000.0
EPS = 1e-6
NEG = -1e30
BRANCH_BLOCKS = (16, 4, 1)
DILATIONS = (1, 4, 16)

ADAM_LR, ADAM_B1, ADAM_B2, ADAM_EPS, ADAM_WD, ADAM_STEP = 0.001, 0.9, 0.999, 1e-08, 0.01, 10

VMEM_LIMIT = 56 * 1024 * 1024
MESH = pl.DeviceIdType.MESH


def _params(sem, **kw):
    return pltpu.CompilerParams(dimension_semantics=sem, vmem_limit_bytes=VMEM_LIMIT, **kw)


def _bdot(a, b, dims):
    return lax.dot_general(a.astype(BF16), b.astype(BF16), (dims, ((), ())), preferred_element_type=F32)


def _fdot(a, b, dims):
    return lax.dot_general(a, b, (dims, ((), ())), preferred_element_type=F32, precision=lax.Precision.HIGHEST)


NN = ((1,), (0,))
NT = ((1,), (1,))
TN = ((0,), (0,))


def _tile(n, want):
    t = min(n, want)
    while n % t:
        t //= 2
    return t


def _matmul(a, b, *, mode, name, outs, extra=(), vecs=(), epilogue=None, col_shards=1, after=(), n_cols=None, out_cols=None,
            tile_rows=0, tile_sums=0, tm=1024, tn=1024, tk=2048):
    if mode == "nn":
        (m, k), n = a.shape, b.shape[1]
    elif mode == "nt":
        (m, k), n = a.shape, b.shape[0]
    else:
        (k, m), n = a.shape, b.shape[1]
    n = n if n_cols is None else n_cols
    tm, tn, tk = _tile(m, tm), _tile(n // col_shards, tn), _tile(k, tk)
    nk = k // tk
    per_shard = n // col_shards // tn
    dims = {"nn": NN, "nt": NT, "tn": TN}[mode]
    a_spec = pl.BlockSpec((tk, tm), lambda i, j, kk: (kk, i)) if mode == "tn" else pl.BlockSpec((tm, tk), lambda i, j, kk: (i, kk))
    b_spec = pl.BlockSpec((tn, tk), lambda i, j, kk: (j, kk)) if mode == "nt" else pl.BlockSpec((tk, tn), lambda i, j, kk: (kk, j))
    o_spec = pl.BlockSpec((tm, tn), lambda i, j, kk: (i, j))
    n_extra, n_out, n_after = len(extra) + len(vecs), len(outs), len(after)

    def body(a_ref, b_ref, *rest):
        extra_refs, out_refs, acc_ref = rest[:n_extra], rest[n_extra + n_after:-1], rest[-1]

        def finish(acc):
            res = (acc,) if epilogue is None else epilogue(acc, *[e[...] for e in extra_refs])
            for o_ref, r in zip(out_refs[:n_out], res):
                o_ref[...] = r.astype(o_ref.dtype)
            for o_ref, r in zip(out_refs[n_out:], res[n_out:]):
                o_ref[...] = jnp.broadcast_to(r, o_ref.shape)

        if nk == 1:
            finish(_bdot(a_ref[...], b_ref[...], dims))
            return
        kk = pl.program_id(2)

        @pl.when(kk == 0)
        def _():
            acc_ref[...] = jnp.zeros_like(acc_ref)

        acc_ref[...] += _bdot(a_ref[...], b_ref[...], dims)

        @pl.when(kk == nk - 1)
        def _():
            finish(acc_ref[...])

    if col_shards == 1:
        out_specs, out_dims = [o_spec] * n_out, (m, n if out_cols is None else out_cols)
    else:
        sharded = pl.BlockSpec((None, tm, tn), lambda i, j, kk: (j // per_shard, i, j % per_shard))
        out_specs, out_dims = [sharded] * n_out, (col_shards, m, n // col_shards)
    res = pl.pallas_call(
        body, name=name, grid=(m // tm, n // tn, nk),
        in_specs=[a_spec, b_spec] + [o_spec] * len(extra) + [pl.BlockSpec((1, tn), lambda i, j, kk: (0, j))] * len(vecs)
        + [pl.BlockSpec(memory_space=pl.ANY)] * n_after,
        out_specs=out_specs + [pl.BlockSpec((8, tn), lambda i, j, kk: (i, j))] * tile_rows
        + [pl.BlockSpec((8, 128), lambda i, j, kk: (i, j))] * tile_sums,
        out_shape=[jax.ShapeDtypeStruct(out_dims, dt) for dt in outs] + [jax.ShapeDtypeStruct((m // tm * 8, n), F32)] * tile_rows
        + [jax.ShapeDtypeStruct((m // tm * 8, n // tn * 128), F32)] * tile_sums,
        scratch_shapes=[pltpu.VMEM((tm, tn) if nk > 1 else (8, 128), F32)],
        compiler_params=_params(("parallel", "parallel", "arbitrary")),
    )(a, b, *extra, *vecs, *after)
    res = (list(res[:n_out]) + [jnp.sum(t[::8], axis=0, keepdims=True) for t in res[n_out:n_out + tile_rows]]
           + [t[::8, ::128] for t in res[n_out + tile_rows:]])
    return res[0] if len(res) == 1 else res


def _row_spec(tr, bw, cb, per_group):
    return pl.BlockSpec((tr, bw), (lambda g, i: (i, cb + g)) if per_group else (lambda g, i: (i, cb)))


def _vec_spec(bw, cb, per_group):
    return pl.BlockSpec((1, bw), (lambda g, i: (0, cb + g)) if per_group else (lambda g, i: (0, cb)))


def _rowwise(fn, rows, vecs, outs, *, name, n_rows=SEQ, tr=512, groups=1, after=()):
    n_r, n_v, n_after = len(rows), len(vecs), len(after)

    def body(*refs):
        vals = [r[...].astype(F32) for r in refs[:n_r + n_v]]
        res = fn(*vals)
        for o_ref, r in zip(refs[n_r + n_v + n_after:], res):
            o_ref[...] = r.astype(o_ref.dtype)

    res = pl.pallas_call(
        body, name=name, grid=(groups, n_rows // tr),
        in_specs=[_row_spec(tr, bw, cb, pg) for _, bw, cb, pg in rows] + [_vec_spec(bw, cb, pg) for _, bw, cb, pg in vecs]
        + [pl.BlockSpec(memory_space=pl.ANY)] * n_after,
        out_specs=[_row_spec(tr, bw, cb, pg) for _, _, bw, cb, pg in outs],
        out_shape=[jax.ShapeDtypeStruct((n_rows, w), dt) for w, dt, _, _, _ in outs],
        compiler_params=_params(("parallel", "parallel")),
    )(*[r[0] for r in rows], *[v[0] for v in vecs], *after)
    return res


def _rowwise_vjp(fn, rows, vecs, cts, row_grads, vec_grads, *, name, n_rows=SEQ, tr=512, groups=1, after=()):
    n_r, n_v, n_after = len(rows), len(vecs), len(after)
    ct_ops = [op for group in cts for op in group]
    ct_sizes = [len(group) for group in cts]
    res_ops = [g[6] for g in row_grads if g[6] is not None]
    n_ct, n_res, n_rg = len(ct_ops), len(res_ops), len(row_grads)

    def body(*refs):
        vals = [r[...].astype(F32) for r in refs[:n_r + n_v]]
        pos = n_r + n_v
        ct_vals = []
        for size in ct_sizes:
            acc = refs[pos][...].astype(F32)
            for t in range(1, size):
                acc = acc + refs[pos + t][...].astype(F32)
            ct_vals.append(acc)
            pos += size
        res_refs = refs[pos:pos + n_res]
        out_refs = refs[pos + n_res + n_after:]
        _, pullback = jax.vjp(fn, *vals)
        grads = pullback(tuple(ct_vals))
        r_i = 0
        for o_ref, g in zip(out_refs[:n_rg], row_grads):
            val = grads[g[0]]
            if g[6] is not None:
                val = val + res_refs[r_i][...].astype(F32)
                r_i += 1
            o_ref[...] = val.astype(o_ref.dtype)
        first = (pl.program_id(1) == 0)
        for o_ref, g in zip(out_refs[n_rg:], vec_grads):
            val = jnp.sum(grads[n_r + g[0]], axis=0, keepdims=True)
            init = first if g[4] else jnp.logical_and(first, pl.program_id(0) == 0)

            @pl.when(init)
            def _(o_ref=o_ref, val=val):
                o_ref[...] = val

            @pl.when(jnp.logical_not(init))
            def _(o_ref=o_ref, val=val):
                o_ref[...] += val

    in_specs = [_row_spec(tr, bw, cb, pg) for _, bw, cb, pg in rows] + [_vec_spec(bw, cb, pg) for _, bw, cb, pg in vecs]
    in_specs += [_row_spec(tr, bw, cb, pg) for _, bw, cb, pg in ct_ops + res_ops] + [pl.BlockSpec(memory_space=pl.ANY)] * n_after
    out_specs =[_row_spec(tr, g[3], g[4], g[5]) for g in row_grads] + [_vec_spec(g[2], g[3], g[4]) for g in vec_grads]
    out_shape = [jax.ShapeDtypeStruct((n_rows, g[1]), g[2]) for g in row_grads]
    out_shape += [jax.ShapeDtypeStruct((1, g[1]), F32) for g in vec_grads]
    return pl.pallas_call(
        body, name=name, grid=(groups, n_rows // tr),
        in_specs=in_specs, out_specs=out_specs, out_shape=out_shape,
        compiler_params=_params(("arbitrary", "arbitrary")),
    )(*[r[0] for r in rows], *[v[0] for v in vecs], *[c[0] for c in ct_ops], *[r[0] for r in res_ops], *after)


def _full(arr, width=None):
    return (arr, arr.shape[1] if width is None else width, 0, False)


def _make_xor(sh):
    def raw(x):
        n = x.shape[-1]
        lane = lax.broadcasted_iota(jnp.int32, x.shape, x.ndim - 1)
        up = pltpu.roll(x, n - sh, x.ndim - 1)
        down = pltpu.roll(x, sh, x.ndim - 1)
        return jnp.where((lane & sh) == 0, up, down)

    f = jax.custom_vjp(raw)
    f.defvjp(lambda x: (raw(x), None), lambda _, ct: (raw(ct),))
    return f


_SWAP_ROPE_HALVES = _make_xor(ROT // 2)


def _head_sum(x):
    n = x.shape[-1]
    same_head = (lax.broadcasted_iota(jnp.int32, (n, n), 0) // HEAD) == (lax.broadcasted_iota(jnp.int32, (n, n), 1) // HEAD)
    return _fdot(x, same_head.astype(F32), NN)


def _rms(x, g):
    return x * lax.rsqrt(jnp.mean(x * x, axis=-1, keepdims=True) + EPS) * g


def _head_rms_rope(x, g, cos, sin, scale):
    y = x * lax.rsqrt(_head_sum(x * x) * (1.0 / HEAD) + EPS) * g
    return (y * cos + _SWAP_ROPE_HALVES(y) * sin) * scale


def _qk_fn(q, k, v, cos, sin, gq, gk):
    return (_head_rms_rope(q, gq, cos, sin, HEAD ** -0.5), _head_rms_rope(k, gk, cos, sin, 1.0), v)


def _norm_fn(x, g):
    return (_rms(x, g),)


def _merge_fn(o0, o1, o2, l0, l1, l2, g):
    m = lax.stop_gradient(jnp.maximum(jnp.maximum(l0, l1), l2))
    e0, e1, e2 = jnp.exp(l0 - m), jnp.exp(l1 - m), jnp.exp(l2 - m)
    mix = (e0 * o0 + e1 * o1 + e2 * o2) / (e0 + e1 + e2)
    return (_rms(mix, g),)


def _gate_fn(y, z, g):
    return (_rms(y * (z * jax.nn.sigmoid(z)), g),)


def _attn_pair(q, kc, vc, kp=None, vp=None, has_prev=None):
    pick0, pick1 = _head_picks()
    k_band, v_band, mask = _attn_band(kc, vc, kp, vp, has_prev)
    s = jnp.where(mask, _bdot(jnp.concatenate([q * pick0, q * pick1], axis=0), k_band, NT), NEG)
    m = jnp.max(s, axis=-1, keepdims=True)
    p = jnp.exp(s - m)
    den = jnp.sum(p, axis=-1, keepdims=True)
    acc = _bdot(p, v_band, NN) * (1.0 / den)
    lse_rows = m + jnp.log(den)
    o = pick0 * acc[:ATT_BLK] + pick1 * acc[ATT_BLK:]
    lse = pick0 * lse_rows[:ATT_BLK] + pick1 * lse_rows[ATT_BLK:]
    return o, lse


def _head_picks():
    lane = lax.broadcasted_iota(jnp.int32, (1, 2 * HEAD), 1)
    return (lane < HEAD).astype(F32), (lane >= HEAD).astype(F32)


def _attn_band(kc, vc, kp, vp, has_prev):
    n_keys = ATT_BLK if kp is None else 2 * ATT_BLK
    qi = lax.broadcasted_iota(jnp.int32, (2 * ATT_BLK, n_keys), 0) & (ATT_BLK - 1)
    kj = lax.broadcasted_iota(jnp.int32, (2 * ATT_BLK, n_keys), 1)
    if kp is None:
        return kc, vc, qi >= kj
    in_prev = jnp.logical_and(jnp.logical_and(kj < ATT_BLK, kj >= qi), has_prev)
    mask = jnp.logical_or(in_prev, jnp.logical_and(kj >= ATT_BLK, qi >= kj - ATT_BLK))
    return jnp.concatenate([kp, kc], axis=0), jnp.concatenate([vp, vc], axis=0), mask


def _attn_config(b):
    r = DILATIONS[b]
    return r, ATT_BLK * r, (D_ATTN if r == 1 else 128), BRANCH_BLOCKS[b] > 1


def _for_residues(r, fn):
    if r <= 4:
        for rho in range(r):
            fn(rho)
    else:
        def step(t, carry):
            for u in range(4):
                fn(4 * t + u)
            return carry

        lax.fori_loop(0, r // 4, step, 0)


def _strided_rows(start, r):
    if r > 1:
        return pl.ds(start, ATT_BLK, stride=r)
    return pl.ds(start if isinstance(start, int) else pl.multiple_of(start, ATT_BLK), ATT_BLK)


def _attention_fwd(qn, kn, vn, b):
    r, rows, lanes, with_prev = _attn_config(b)
    cur = pl.BlockSpec((rows, lanes), lambda g, n: (n, g))
    prev = pl.BlockSpec((rows, lanes), lambda g, n: (jnp.maximum(n - 1, 0), g))

    def body(*refs):
        ins, (o_ref, l_ref) = refs[:-2], refs[-2:]
        has_prev = pl.program_id(1) > 0

        def one(rho):
            sub = _strided_rows(rho, r)
            for pair in range(lanes // 128):
                sl = pl.ds(pair * 128, 128)
                args = [ref[sub, sl] for ref in ins] + ([has_prev] if with_prev else [])
                o_ref[sub, sl], l_ref[sub, sl] = _attn_pair(*args)

        _for_residues(r, one)

    operands = (qn, kn, vn, kn, vn) if with_prev else (qn, kn, vn)
    return pl.pallas_call(
        body, name="attn_fwd_%d" % r, grid=(D_ATTN // lanes, SEQ // rows),
        in_specs=[cur, cur, cur] + ([prev, prev] if with_prev else []), out_specs=[cur, cur],
        out_shape=[jax.ShapeDtypeStruct((SEQ, D_ATTN), F32)] * 2,
        compiler_params=_params(("parallel", "parallel")),
    )(*operands)


def _attn_pair_bwd(q, kc, vc, kp, vp, o, lse, do, dl, has_prev):
    pick0, pick1 = _head_picks()
    lane = lax.broadcasted_iota(jnp.int32, (1, 2 * HEAD), 1)
    k_band, v_band, mask = _attn_band(kc, vc, kp, vp, has_prev)
    q2 = jnp.concatenate([q * pick0, q * pick1], axis=0)
    do2 = jnp.concatenate([do * pick0, do * pick1], axis=0)
    lse2 = jnp.concatenate([jnp.sum(lse * (lane == 0).astype(F32), axis=-1, keepdims=True),
                            jnp.sum(lse * (lane == HEAD).astype(F32), axis=-1, keepdims=True)], axis=0)
    base = jnp.sum(jnp.concatenate([dl * pick0, dl * pick1], axis=0) - do2 * jnp.concatenate([o, o], axis=0),
                   axis=-1, keepdims=True)
    p = jnp.exp(jnp.where(mask, _bdot(q2, k_band, NT), NEG) - lse2)
    ds = p * (_bdot(do2, v_band, NT) + base)
    dq2 = _bdot(ds, k_band, NN)
    dq = pick0 * dq2[:ATT_BLK] + pick1 * dq2[ATT_BLK:]
    dk, dv = _bdot(ds, q2, TN), _bdot(p, do2, TN)
    if kp is None:
        return dq, dk, dv
    return dq, dk[ATT_BLK:], dv[ATT_BLK:], dk[:ATT_BLK], dv[:ATT_BLK]


def _attention_bwd(qn, kn, vn, o, lse, do, dl, b):
    r, rows, lanes, with_prev = _attn_config(b)
    cur = pl.BlockSpec((rows, lanes), lambda g, n: (n, g))
    prev = pl.BlockSpec((rows, lanes), lambda g, n: (jnp.maximum(n - 1, 0), g))
    whole = pl.BlockSpec((SEQ, lanes), lambda g, n: (0, g))
    n_in = 5 if with_prev else 3

    def body(*refs):
        ins, (o_ref, l_ref, do_ref, dl_ref, dq_ref, dk_ref, dv_ref) = refs[:n_in], refs[n_in:]
        n = pl.program_id(1)

        @pl.when(n == 0)
        def _():
            dk_ref[...] = jnp.zeros_like(dk_ref)
            dv_ref[...] = jnp.zeros_like(dv_ref)

        def one(rho):
            sub = _strided_rows(rho, r)
            sub_c = _strided_rows(n * rows + rho, r)
            sub_p = _strided_rows(jnp.maximum(n - 1, 0) * rows + rho, r)
            for pair in range(lanes // 128):
                sl = pl.ds(pair * 128, 128)
                vals = [ref[sub, sl] for ref in ins] + ([] if with_prev else [None, None])
                grads = _attn_pair_bwd(*vals, o_ref[sub, sl], l_ref[sub, sl], do_ref[sub, sl], dl_ref[sub, sl], n > 0)
                dq_ref[sub, sl] = grads[0]
                dk_ref[sub_c, sl] += grads[1]
                dv_ref[sub_c, sl] += grads[2]
                if with_prev:
                    dk_ref[sub_p, sl] += grads[3]
                    dv_ref[sub_p, sl] += grads[4]

        _for_residues(r, one)

    operands = (qn, kn, vn, kn, vn) if with_prev else (qn, kn, vn)
    return pl.pallas_call(
        body, name="attn_bwd_%d" % r, grid=(D_ATTN // lanes, SEQ // rows),
        in_specs=[cur, cur, cur] + ([prev, prev] if with_prev else []) + [cur] * 4, out_specs=[cur, whole, whole],
        out_shape=[jax.ShapeDtypeStruct((SEQ, D_ATTN), F32)] * 3,
        compiler_params=_params(("parallel", "arbitrary")),
    )(*operands, o, lse, do, dl)


CONV_COLS = 256
XBC_BLOCK0 = (3 * D_ATTN + D_SSM) // CONV_COLS


def _shift_rows(x, s):
    n = x.shape[0]
    t = lax.broadcasted_iota(jnp.int32, x.shape, 0)
    if s >= 0:
        return jnp.where(t >= s, pltpu.roll(x, s, 0), 0.0)
    return jnp.where(t < n + s, pltpu.roll(x, n + s, 0), 0.0)


def _conv_pre(x, w_ref, b_ref):
    delayed = [_shift_rows(x, 3 - k) for k in range(3)]
    pre = b_ref[...] + w_ref[3:4, :] * x
    for k in range(3):
        pre = pre + w_ref[k:k + 1, :] * delayed[k]
    return pre, delayed


def _conv_fwd(proj, conv_w, conv_b):
    cols = conv_w.shape[1]

    def body(x_ref, w_ref, b_ref, o_ref):
        pre, _ = _conv_pre(x_ref[...], w_ref, b_ref)
        o_ref[...] = pre * jax.nn.sigmoid(pre)

    blk = pl.BlockSpec((SEQ, CONV_COLS), lambda j: (0, j))
    return pl.pallas_call(
        body, name="conv_fwd", grid=(cols // CONV_COLS,),
        in_specs=[pl.BlockSpec((SEQ, CONV_COLS), lambda j: (0, XBC_BLOCK0 + j)),
                  pl.BlockSpec((4, CONV_COLS), lambda j: (0, j)), pl.BlockSpec((1, CONV_COLS), lambda j: (0, j))],
        out_specs=blk, out_shape=jax.ShapeDtypeStruct((SEQ, cols), F32),
        compiler_params=_params(("parallel",)),
    )(proj, conv_w, conv_b)


def _conv_bwd(proj, conv_w, conv_b, dxs, db, dc):
    cols = conv_w.shape[1]
    x_blocks, b_blocks = dxs.shape[1] // CONV_COLS, db.shape[1] // CONV_COLS

    def body(x_ref, w_ref, b_ref, dxs_ref, db_ref_in, dc_ref_in, dx_ref, dw_ref, db_ref):
        j = pl.program_id(0)
        dy = jnp.where(j < x_blocks, dxs_ref[...], jnp.where(j < x_blocks + b_blocks, db_ref_in[...], dc_ref_in[...]))
        x = x_ref[...]
        pre, delayed = _conv_pre(x, w_ref, b_ref)
        sg = jax.nn.sigmoid(pre)
        dpre = dy * (sg * (1.0 + pre * (1.0 - sg)))
        db_ref[...] = jnp.sum(dpre, axis=0, keepdims=True)
        dx = w_ref[3:4, :] * dpre
        dw_ref[3:4, :] = jnp.sum(dpre * x, axis=0, keepdims=True)
        for k in range(3):
            dx = dx + w_ref[k:k + 1, :] * _shift_rows(dpre, k - 3)
            dw_ref[k:k + 1, :] = jnp.sum(dpre * delayed[k], axis=0, keepdims=True)
        dw_ref[4:8, :] = jnp.zeros((4, CONV_COLS), F32)
        dx_ref[...] = dx.astype(dx_ref.dtype)

    blk = pl.BlockSpec((SEQ, CONV_COLS), lambda j: (0, j))
    parts = [pl.BlockSpec((SEQ, CONV_COLS), lambda j: (0, jnp.minimum(j, x_blocks - 1))),
             pl.BlockSpec((SEQ, CONV_COLS), lambda j: (0, jnp.clip(j - x_blocks, 0, b_blocks - 1))),
             pl.BlockSpec((SEQ, CONV_COLS), lambda j: (0, jnp.clip(j - x_blocks - b_blocks, 0, b_blocks - 1)))]
    return pl.pallas_call(
        body, name="conv_bwd", grid=(cols // CONV_COLS,),
        in_specs=[pl.BlockSpec((SEQ, CONV_COLS), lambda j: (0, XBC_BLOCK0 + j)),
                  pl.BlockSpec((4, CONV_COLS), lambda j: (0, j)), pl.BlockSpec((1, CONV_COLS), lambda j: (0, j))] + parts,
        out_specs=[blk, pl.BlockSpec((8, CONV_COLS), lambda j: (0, j)), pl.BlockSpec((1, CONV_COLS), lambda j: (0, j))],
        out_shape=[jax.ShapeDtypeStruct((SEQ, cols), BF16), jax.ShapeDtypeStruct((8, cols), F32),
                   jax.ShapeDtypeStruct((1, cols), F32)],
        compiler_params=_params(("parallel",)),
    )(proj, conv_w, conv_b, dxs, db, dc)


HEADS_PER_GROUP = 4


GROUP_WIDTH = HEADS_PER_GROUP * HEAD


def _ssd_chunk(x, bm, cm, dtr, bias, alog, dsk, h):
    row = lax.broadcasted_iota(jnp.int32, (CHUNK, CHUNK), 0)
    col = lax.broadcasted_iota(jnp.int32, (CHUNK, CHUNK), 1)
    causal = row >= col
    z = dtr + bias
    dt = jnp.maximum(z, 0.0) + jnp.log(1.0 + jnp.exp(-jnp.abs(z)))
    acs = _fdot(causal.astype(F32), dt * -jnp.exp(alog), NN)
    acs_t, dt_t = acs.T, dt.T
    cb = _bdot(cm, bm, NT)
    lane = lax.broadcasted_iota(jnp.int32, (1, CHUNK), 1)
    sub = lax.broadcasted_iota(jnp.int32, (CHUNK, 1), 0)
    wide = lax.broadcasted_iota(jnp.int32, (1, GROUP_WIDTH), 1) // HEAD
    tall = lax.broadcasted_iota(jnp.int32, (GROUP_WIDTH, 1), 0) // HEAD
    acs_last = jnp.sum(acs * (sub == CHUNK - 1).astype(F32), axis=0, keepdims=True)
    to_lanes = (lax.broadcasted_iota(jnp.int32, (CHUNK, GROUP_WIDTH), 0)
                == lax.broadcasted_iota(jnp.int32, (CHUNK, GROUP_WIDTH), 1) // HEAD).astype(F32)
    grow = _fdot(jnp.exp(acs), to_lanes, NN)
    keep = _fdot(jnp.exp(acs_last - acs) * dt, to_lanes, NN)
    w_parts, x_parts, skip, carry = [], [], 0.0, 0.0
    for j in range(HEADS_PER_GROUP):
        on_lane, on_sub = (lane == j).astype(F32), (sub == j).astype(F32)
        acs_c = jnp.sum(acs * on_lane, axis=1, keepdims=True)
        acs_r = jnp.sum(acs_t * on_sub, axis=0, keepdims=True)
        dt_r = jnp.sum(dt_t * on_sub, axis=0, keepdims=True)
        w_parts.append(cb * jnp.exp(jnp.where(causal, acs_c - acs_r, NEG)) * dt_r)
        x_parts.append(x * (wide == j).astype(F32))
        skip = skip + jnp.sum(dsk * on_lane, axis=1, keepdims=True) * (wide == j).astype(F32)
        carry = carry + jnp.sum(jnp.exp(acs_last) * on_lane, axis=1, keepdims=True) * (tall == j).astype(F32)
    y_diag = _bdot(jnp.concatenate(w_parts, axis=1), jnp.concatenate(x_parts, axis=0), NN)
    y = y_diag + _bdot(cm, h, NT) * grow + skip * x
    return y, h * carry + _bdot(x * keep, bm, TN)


GROUPS_PER_STEP = 2
SSD_STEPS = N_GROUPS // GROUPS_PER_STEP


def _ssd_specs(reverse):
    n_chunks = SEQ // CHUNK
    c_of = (lambda c: n_chunks - 1 - c) if reverse else (lambda c: c)
    x_w, n_w, dt_w = GROUPS_PER_STEP * GROUP_WIDTH, GROUPS_PER_STEP * N_STATE, GROUPS_PER_STEP * 128
    x_spec = pl.BlockSpec((CHUNK, x_w), lambda g, c: (c_of(c), g))
    b_spec = pl.BlockSpec((CHUNK, n_w), lambda g, c: (c_of(c), D_SSM // n_w + g))
    c_spec = pl.BlockSpec((CHUNK, n_w), lambda g, c: (c_of(c), (D_SSM + N_GROUPS * N_STATE) // n_w + g))
    dt_spec = pl.BlockSpec((CHUNK, dt_w), lambda g, c: (c_of(c), g))
    vec_spec = pl.BlockSpec((1, dt_w), lambda g, c: (0, g))
    h_spec = pl.BlockSpec((None, GROUPS_PER_STEP, GROUP_WIDTH, N_STATE), lambda g, c: (c_of(c), g, 0, 0))
    return x_spec, b_spec, c_spec, dt_spec, vec_spec, h_spec


def _group_slices(u):
    return pl.ds(u * GROUP_WIDTH, GROUP_WIDTH), pl.ds(u * N_STATE, N_STATE), pl.ds(u * 128, 128)


def _ssd_gated_chunk(x, bm, cm, dtr, bias, alog, dsk, h, z, g_out):
    y, h_new = _ssd_chunk(x, bm, cm, dtr, bias, alog, dsk, h)
    return _gate_fn(y, z, g_out)[0], h_new


def _ssd_gate_specs(reverse):
    x_spec = _ssd_specs(reverse)[0]
    z_block0 = 3 * D_ATTN // x_spec.block_shape[1]
    z_spec = pl.BlockSpec(x_spec.block_shape, lambda g, c: (x_spec.index_map(g, c)[0], z_block0 + g))
    return z_spec, pl.BlockSpec((1, x_spec.block_shape[1]), lambda g, c: (0, g))


def _ssd_fwd(xbc, dt_raw, bias, alog, dsk, proj, g_out):
    x_spec, b_spec, c_spec, dt_spec, vec_spec, h_spec = _ssd_specs(False)
    z_spec, g_spec = _ssd_gate_specs(False)

    def body(x_ref, b_ref, c_ref, dt_ref, bias_ref, alog_ref, dsk_ref, z_ref, g_ref, ssm_ref, hin_ref, h_scr):
        @pl.when(pl.program_id(1) == 0)
        def _():
            h_scr[...] = jnp.zeros_like(h_scr)

        for u in range(GROUPS_PER_STEP):
            xs, ns, ds = _group_slices(u)
            h = h_scr[u]
            hin_ref[u] = h
            ssm, h_scr[u] = _ssd_gated_chunk(x_ref[:, xs], b_ref[:, ns], c_ref[:, ns], dt_ref[:, ds], bias_ref[:, ds],
                                             alog_ref[:, ds], dsk_ref[:, ds], h, z_ref[:, xs], g_ref[:, xs])
            ssm_ref[:, xs] = ssm.astype(ssm_ref.dtype)

    return pl.pallas_call(
        body, name="ssd_fwd", grid=(SSD_STEPS, SEQ // CHUNK),
        in_specs=[x_spec, b_spec, c_spec, dt_spec, vec_spec, vec_spec, vec_spec, z_spec, g_spec],
        out_specs=[x_spec, h_spec],
        out_shape=[jax.ShapeDtypeStruct((SEQ, D_SSM), BF16),
                   jax.ShapeDtypeStruct((SEQ // CHUNK, N_GROUPS, GROUP_WIDTH, N_STATE), F32)],
        scratch_shapes=[pltpu.VMEM((GROUPS_PER_STEP, GROUP_WIDTH, N_STATE), F32)],
        compiler_params=_params(("parallel", "arbitrary")),
    )(xbc, xbc, xbc, dt_raw, bias, alog, dsk, proj, g_out)


def _ssd_bwd(xbc, dt_raw, bias, alog, dsk, h_in, proj, g_out, dmix):
    x_spec, b_spec, c_spec, dt_spec, vec_spec, h_spec = _ssd_specs(True)
    z_spec, g_spec = _ssd_gate_specs(True)
    ct_block0 = D_ATTN // x_spec.block_shape[1]
    ct_spec = pl.BlockSpec(x_spec.block_shape, lambda g, c: (x_spec.index_map(g, c)[0], ct_block0 + g))

    def body(x_ref, b_ref, c_ref, dt_ref, bias_ref, alog_ref, dsk_ref, hin_ref, z_ref, g_ref, ct_ref,
             dx_ref, db_ref, dc_ref, ddt_ref, dbias_ref, dalog_ref, ddsk_ref, dz_ref, dg_ref, dh_scr):
        first = pl.program_id(1) == 0

        @pl.when(first)
        def _():
            dh_scr[...] = jnp.zeros_like(dh_scr)

        for u in range(GROUPS_PER_STEP):
            xs, ns, ds = _group_slices(u)
            _, pullback = jax.vjp(_ssd_gated_chunk, x_ref[:, xs], b_ref[:, ns], c_ref[:, ns], dt_ref[:, ds], bias_ref[:, ds],
                                  alog_ref[:, ds], dsk_ref[:, ds], hin_ref[u], z_ref[:, xs], g_ref[:, xs])
            g = pullback((ct_ref[:, xs], dh_scr[u]))
            dx_ref[:, xs], db_ref[:, ns], dc_ref[:, ns] = g[0], g[1], g[2]
            ddt_ref[:, ds] = g[3].astype(ddt_ref.dtype)
            dh_scr[u] = g[7]
            dz_ref[:, xs] = g[8].astype(dz_ref.dtype)
            sums = ((dbias_ref, g[4], ds), (dalog_ref, g[5], ds), (ddsk_ref, g[6], ds),
                    (dg_ref, jnp.sum(g[9], axis=0, keepdims=True), xs))
            for o_ref, val, lanes in sums:
                @pl.when(first)
                def _(o_ref=o_ref, val=val, lanes=lanes):
                    o_ref[:, lanes] = val

                @pl.when(jnp.logical_not(first))
                def _(o_ref=o_ref, val=val, lanes=lanes):
                    o_ref[:, lanes] += val

    n_chunks = SEQ // CHUNK
    out_b = pl.BlockSpec((CHUNK, GROUPS_PER_STEP * N_STATE), lambda g, c: (n_chunks - 1 - c, g))
    return pl.pallas_call(
        body, name="ssd_bwd", grid=(SSD_STEPS, n_chunks),
        in_specs=[x_spec, b_spec, c_spec, dt_spec, vec_spec, vec_spec, vec_spec, h_spec, z_spec, g_spec, ct_spec],
        out_specs=[x_spec, out_b, out_b, dt_spec, vec_spec, vec_spec, vec_spec, x_spec, g_spec],
        out_shape=[jax.ShapeDtypeStruct((SEQ, D_SSM), F32), jax.ShapeDtypeStruct((SEQ, N_GROUPS * N_STATE), F32),
                   jax.ShapeDtypeStruct((SEQ, N_GROUPS * N_STATE), F32), jax.ShapeDtypeStruct((SEQ, DT_PAD), BF16),
                   jax.ShapeDtypeStruct((1, DT_PAD), F32), jax.ShapeDtypeStruct((1, DT_PAD), F32),
                   jax.ShapeDtypeStruct((1, DT_PAD), F32), jax.ShapeDtypeStruct((SEQ, D_SSM), BF16),
                   jax.ShapeDtypeStruct((1, D_SSM), F32)],
        scratch_shapes=[pltpu.VMEM((GROUPS_PER_STEP, GROUP_WIDTH, N_STATE), F32)],
        compiler_params=_params(("parallel", "arbitrary")),
    )(xbc, xbc, xbc, dt_raw, bias, alog, dsk, h_in, proj, g_out, dmix)


CROSS_HEAD = 128
CROSS_ROWS = 1024


def _cross_head(q, k, v, gq, gk):
    qn = _rms(q, gq) * (CROSS_HEAD ** -0.5)
    kn = _rms(k, gk)
    s = _bdot(qn, kn, NT)
    p = jnp.exp(s - lax.stop_gradient(jnp.max(s, axis=-1, keepdims=True)))
    return _bdot(p, v, NN) * (1.0 / jnp.sum(p, axis=-1, keepdims=True))


def _cross_specs():
    q_spec = pl.BlockSpec((CROSS_ROWS, CROSS_HEAD), lambda h, i: (i, h))
    k_spec = pl.BlockSpec((N_MEM, CROSS_HEAD), lambda h, i: (0, h))
    v_spec = pl.BlockSpec((N_MEM, CROSS_HEAD), lambda h, i: (0, 4 + h))
    g_spec = pl.BlockSpec((1, CROSS_HEAD), lambda h, i: (0, 0))
    return q_spec, k_spec, v_spec, g_spec


def _cross_fwd(qc, kv, gq, gk):
    q_spec, k_spec, v_spec, g_spec = _cross_specs()

    def body(q_ref, k_ref, v_ref, gq_ref, gk_ref, o_ref):
        o_ref[...] = _cross_head(q_ref[...], k_ref[...], v_ref[...], gq_ref[...], gk_ref[...]).astype(o_ref.dtype)

    return pl.pallas_call(
        body, name="cross_fwd", grid=(4, SEQ // CROSS_ROWS),
        in_specs=[q_spec, k_spec, v_spec, g_spec, g_spec], out_specs=q_spec,
        out_shape=jax.ShapeDtypeStruct((SEQ, D_CROSS), BF16),
        compiler_params=_params(("parallel", "parallel")),
    )(qc, kv, kv, gq, gk)


def _cross_bwd(qc, kv, gq, gk, do):
    q_spec, k_spec, v_spec, g_spec = _cross_specs()

    def body(q_ref, k_ref, v_ref, gq_ref, gk_ref, do_ref, dq_ref, dk_ref, dv_ref, dgq_ref, dgk_ref):
        _, pullback = jax.vjp(_cross_head, q_ref[...], k_ref[...], v_ref[...], gq_ref[...], gk_ref[...])
        dq, dk, dv, dgq, dgk = pullback(do_ref[...].astype(F32))
        dq_ref[...] = dq.astype(dq_ref.dtype)
        row0 = pl.program_id(1) == 0
        all0 = jnp.logical_and(row0, pl.program_id(0) == 0)
        for o_ref, val, init in ((dk_ref, dk, row0), (dv_ref, dv, row0), (dgq_ref, dgq, all0), (dgk_ref, dgk, all0)):
            @pl.when(init)
            def _(o_ref=o_ref, val=val):
                o_ref[...] = val

            @pl.when(jnp.logical_not(init))
            def _(o_ref=o_ref, val=val):
                o_ref[...] += val

    return pl.pallas_call(
        body, name="cross_bwd", grid=(4, SEQ // CROSS_ROWS),
        in_specs=[q_spec, k_spec, v_spec, g_spec, g_spec, q_spec],
        out_specs=[q_spec, k_spec, k_spec, g_spec, g_spec],
        out_shape=[jax.ShapeDtypeStruct((SEQ, D_CROSS), BF16), jax.ShapeDtypeStruct((N_MEM, D_CROSS), F32),
                   jax.ShapeDtypeStruct((N_MEM, D_CROSS), F32), jax.ShapeDtypeStruct((1, CROSS_HEAD), F32),
                   jax.ShapeDtypeStruct((1, CROSS_HEAD), F32)],
        compiler_params=_params(("arbitrary", "arbitrary")),
    )(qc, kv, kv, gq, gk, do)


def _loss_epilogue(acc, residual, target):
    err = acc + residual - target
    dy = err * (1.0 / D_MODEL)
    part = jnp.sum(jnp.sum(err * err, axis=1, keepdims=True), axis=0, keepdims=True) * (0.5 / D_MODEL)
    return dy, dy, part


def _pad_heads(v):
    return jnp.pad(v.reshape(N_GROUPS, HEADS_PER_GROUP), ((0, 0), (0, 128 - HEADS_PER_GROUP))).reshape(1, DT_PAD)


def _unpad_heads(v):
    return v.reshape(v.shape[0], N_GROUPS, 128)[:, :, :HEADS_PER_GROUP].reshape(v.shape[0], N_DT)


def _rope_tables(positions):
    half = ROT // 2
    inv_freq = ROPE_THETA ** (-2.0 * jnp.arange(half, dtype=F32) / ROT)
    ang = positions.reshape(SEQ, 1).astype(F32) * inv_freq
    cos, sin = jnp.cos(ang), jnp.sin(ang)
    ones, zeros = jnp.ones((SEQ, HEAD - ROT), F32), jnp.zeros((SEQ, HEAD - ROT), F32)
    cos_h = jnp.concatenate([cos, cos, ones], axis=1)
    sin_h = jnp.concatenate([-sin, sin, zeros], axis=1)
    return jnp.tile(cos_h, (1, 2)), jnp.tile(sin_h, (1, 2))


def _add_res(acc, res):
    return (acc + res,)


def _norm_bwd_epilogue(acc, x, residual, *more):
    *part, g = more
    ct = acc + part[0] if part else acc
    _, pullback = jax.vjp(_rms, x, g)
    dx, dg = pullback(ct)
    return dx + residual, dg


def _add_res_and_norm(acc, res, g):
    y = acc + res
    return y, _rms(y, g)


def _settle(grads, *after):
    if hasattr(grads, "settle"):
        grads.settle(*after)


def _take_token(grads):
    token = getattr(grads, "token", None)
    if token is None:
        return ()
    grads.token = None
    return (token,)


def _local_step(x, mem, positions, target, p, w, more_weights=None, grads=None, h=None):
    grads = {} if grads is None else grads
    w = dict(w)
    cos, sin = _rope_tables(positions)
    gq2, gk2 = jnp.tile(p["g_q"], (1, 2)), jnp.tile(p["g_k"], (1, 2))
    bias, alog, dsk = _pad_heads(p["dt_bias"]), _pad_heads(p["a_log"]), _pad_heads(p["d_skip"])
    norm_out = [(D_MODEL, BF16, D_MODEL, 0, False)]

    if h is None:
        h = _rowwise(_norm_fn, [_full(x)], [_full(p["g_mix"])], norm_out, name="norm_in")[0]
    proj = _matmul(h, w["w_in"], mode="nn", name="in_proj", outs=[F32], n_cols=D_MAIN)
    dt_raw = _matmul(h, w["w_dt"], mode="nn", name="dt_proj", outs=[F32])
    pairs = D_ATTN // 128
    qk_rows = [(proj, 128, 0, True), (proj, 128, pairs, True), (proj, 128, 2 * pairs, True), _full(cos), _full(sin)]
    qk_vecs = [_full(gq2), _full(gk2)]
    qn, kn, vn = _rowwise(_qk_fn, qk_rows, qk_vecs, [(D_ATTN, F32, 128, 0, True)] * 3, name="qk_prep", groups=8, tr=1024)
    branches = [_attention_fwd(qn, kn, vn, b) for b in range(3)]
    merge_rows = [_full(o) for o, _ in branches] + [_full(lse) for _, lse in branches]
    attn = _rowwise(_merge_fn, merge_rows, [_full(p["g_attn_out"])], [(D_ATTN, BF16, D_ATTN, 0, False)], name="attn_merge")[0]
    xbc = _conv_fwd(proj, p["conv_w"], p["conv_b"])
    ssm, h_in = _ssd_fwd(xbc, dt_raw, bias, alog, dsk, proj, p["g_ssm_out"])
    mix = jnp.concatenate([attn, ssm], axis=1)
    if more_weights is not None:
        w.update(more_weights("mixer_done", mix))
    x1, hc = _matmul(mix, w["w_out"], mode="nn", name="out_proj", outs=[F32, BF16], extra=(x,), vecs=(p["g_cross"],),
                     epilogue=_add_res_and_norm, tm=512, tn=D_MODEL)
    memh = _rowwise(_norm_fn, [_full(mem)], [_full(p["g_mem"])], norm_out, name="norm_mem", n_rows=N_MEM, tr=N_MEM)[0]
    qc = _matmul(hc, w["w_cq"], mode="nn", name="cq_proj", outs=[F32])
    if more_weights is not None:
        w.update(more_weights("cross_started", qc))
    kv = _matmul(memh, w["w_ckv"], mode="nn", name="ckv_proj", outs=[F32])
    oc = _cross_fwd(qc, kv, p["g_cq"], p["g_ck"])
    x2, hm = _matmul(oc, w["w_co"], mode="nn", name="co_proj", outs=[F32, BF16], extra=(x1,), vecs=(p["g_mlp"],),
                     epilogue=_add_res_and_norm, tm=512, tn=D_MODEL)
    if more_weights is not None:
        w.update(more_weights("cross_done", hm))
    u, act = _matmul(hm, w["w_up"], mode="nn", name="up_proj", outs=[F32, BF16],
                     epilogue=lambda acc: (acc, jnp.square(jnp.maximum(acc, 0.0))))
    dy, dyb, loss_tiles = _matmul(act, w["w_down"], mode="nn", name="down_proj", outs=[F32, BF16], extra=(x2, target),
                                  epilogue=_loss_epilogue, tile_sums=1)
    loss = jnp.sum(loss_tiles).reshape(1, 1)

    grads["w_down"] = _matmul(act, dyb, mode="tn", name="dw_down", outs=[BF16], after=_take_token(grads))
    du = _matmul(dyb, w["w_down"], mode="nt", name="d_act", outs=[BF16], extra=(u,), after=_take_token(grads),
                 epilogue=lambda acc, uu: (acc * (2.0 * jnp.maximum(uu, 0.0)),))
    _settle(grads, du)
    grads["w_up"] = _matmul(hm, du, mode="tn", name="dw_up", outs=[BF16], col_shards=4, after=_take_token(grads))
    dx2, grads["g_mlp"] = _matmul(du, w["w_up"], mode="nt", name="d_hm", outs=[F32], extra=(x2, dy), vecs=(p["g_mlp"],),
                                  epilogue=_norm_bwd_epilogue, tile_rows=1, after=_take_token(grads), tm=512, tn=D_MODEL,
                                  tk=1024)
    _settle(grads, dx2)
    grads["w_co"] = _matmul(oc, dx2, mode="tn", name="dw_co", outs=[BF16], col_shards=4, after=_take_token(grads))
    doc = _matmul(dx2, w["w_co"], mode="nt", name="d_oc", outs=[BF16])
    dqc, dkc, dvc, grads["g_cq"], grads["g_ck"] = _cross_bwd(qc, kv, p["g_cq"], p["g_ck"], doc)
    grads["w_cq"] = _matmul(hc, dqc, mode="tn", name="dw_cq", outs=[BF16])
    dkv = jnp.concatenate([dkc, dvc], axis=1)
    grads["w_ckv"] = _matmul(memh, dkv, mode="tn", name="dw_ckv", outs=[BF16])
    dmemh = _matmul(dkv, w["w_ckv"], mode="nt", name="d_memh", outs=[F32])
    grads["g_mem"] = _rowwise_vjp(_norm_fn, [_full(mem)], [_full(p["g_mem"])], [[_full(dmemh)]], [],
                                  [(0, D_MODEL, D_MODEL, 0, False)], name="norm_mem_bwd", n_rows=N_MEM, tr=N_MEM)[0]
    dx1, grads["g_cross"] = _matmul(dqc, w["w_cq"], mode="nt", name="d_hc", outs=[F32], extra=(x1, dx2), vecs=(p["g_cross"],),
                                    epilogue=_norm_bwd_epilogue, tile_rows=1, tm=512, tn=D_MODEL)
    grads["w_out"] = _matmul(mix, dx1, mode="tn", name="dw_out", outs=[BF16])
    dmix = _matmul(dx1, w["w_out"], mode="nt", name="d_mix", outs=[F32], after=_take_token(grads))
    _settle(grads, dmix)
    merge_grads = [(i, D_ATTN, F32, D_ATTN, 0, False, None) for i in range(6)]
    *dol, grads["g_attn_out"] = _rowwise_vjp(
        _merge_fn, merge_rows, [_full(p["g_attn_out"])], [[(dmix, D_ATTN, 0, False)]],
        merge_grads, [(0, D_ATTN, D_ATTN, 0, False)], name="attn_merge_bwd", tr=256, after=_take_token(grads))
    dqkv = [_attention_bwd(qn, kn, vn, *branches[b], dol[b], dol[3 + b], b) for b in range(3)]
    qk_cts = [[(dqkv[b][i], 128, 0, True) for b in range(3)] for i in range(3)]
    dq, dk, dv, dgq2, dgk2 = _rowwise_vjp(
        _qk_fn, qk_rows, qk_vecs, qk_cts, [(i, D_ATTN, BF16, 128, 0, True, None) for i in range(3)],
        [(0, 128, 128, 0, False), (1, 128, 128, 0, False)], name="qk_prep_bwd", groups=8, tr=1024)
    grads["g_q"] = dgq2[:, :HEAD] + dgq2[:, HEAD:]
    grads["g_k"] = dgk2[:, :HEAD] + dgk2[:, HEAD:]
    dxs, db, dc, ddt, dbias, dalog, ddsk, dz, grads["g_ssm_out"] = _ssd_bwd(xbc, dt_raw, bias, alog, dsk, h_in, proj,
                                                                             p["g_ssm_out"], dmix)
    grads["dt_bias"], grads["a_log"], grads["d_skip"] = _unpad_heads(dbias), _unpad_heads(dalog), _unpad_heads(ddsk)
    dxbc_raw, dconv_w, grads["conv_b"] = _conv_bwd(proj, p["conv_w"], p["conv_b"], dxs, db, dc)
    grads["conv_w"] = dconv_w[:4]
    dproj = jnp.concatenate([dq, dk, dv, dz, dxbc_raw], axis=1)
    grads["w_main"] = _matmul(h, dproj, mode="tn", name="dw_main", outs=[BF16], out_cols=D_MAIN + N_DT)
    grads["w_dt"] = _matmul(h, ddt, mode="tn", name="dw_dt", outs=[BF16])
    dh = _matmul(dproj, w["w_in"], mode="nt", name="d_h_main", outs=[F32], after=_take_token(grads))
    grad_x, grads["g_mix"] = _matmul(ddt, w["w_dt"], mode="nt", name="d_h_dt", outs=[F32], extra=(x, dx1, dh),
                                     vecs=(p["g_mix"],), epilogue=_norm_bwd_epilogue, tile_rows=1, tm=512, tn=D_MODEL)
    return loss, grad_x, grads


MATRICES = ("w_in", "w_out", "w_cq", "w_ckv", "w_co", "w_up", "w_down")
ROW_SHARDED = ("w_out", "w_cq", "w_ckv", "w_down")
N_CHIPS = 4
ANY = pl.BlockSpec(memory_space=pl.ANY)


def _place():
    return lax.axis_index("x"), lax.axis_index("y"), lax.axis_index("c")


def _other_chips(x, y):
    return [(1 - x, y), (x, 1 - y), (1 - x, 1 - y)]


def _remote(src, dst, send_sem, recv_sem, device):
    return pltpu.make_async_remote_copy(src_ref=src, dst_ref=dst, send_sem=send_sem, recv_sem=recv_sem,
                                        device_id=device, device_id_type=MESH)


def _gathered_shape(name, shard):
    rows, cols = shard.shape
    if name == "w_in":
        return (N_CHIPS, rows, cols)
    return (N_CHIPS * rows, cols) if name in ROW_SHARDED else (rows, N_CHIPS * cols)


def _shard_window(name, ref, rows, cols, chip, half):
    r0, nr = (0, rows) if half is None else (half * (rows // 2), rows // 2)
    if name == "w_in":
        return ref.at[chip, pl.ds(r0, nr), :]
    if name in ROW_SHARDED:
        return ref.at[pl.ds(chip * rows + r0, nr), :]
    return ref.at[pl.ds(r0, nr), pl.ds(pl.multiple_of(chip * cols, 128), cols)]


def _cast_into_gathered(w, name, chip, after=()):
    rows, cols = w.shape
    tr = _tile(rows, ROW_TILE)

    def body(chip_ref, w_ref, *rest):
        rest[-1][...] = w_ref[...].astype(BF16)

    if name == "w_in":
        out_spec = pl.BlockSpec((None, tr, cols), lambda i, chip_ref: (chip_ref[0], i, 0))
    elif name in ROW_SHARDED:
        out_spec = pl.BlockSpec((tr, cols), lambda i, chip_ref: (chip_ref[0] * (rows // tr) + i, 0))
    else:
        out_spec = pl.BlockSpec((tr, cols), lambda i, chip_ref: (i, chip_ref[0]))
    grid_spec = pltpu.PrefetchScalarGridSpec(
        num_scalar_prefetch=1, grid=(rows // tr,),
        in_specs=[pl.BlockSpec((tr, cols), lambda i, chip_ref: (i, 0))] + [pl.BlockSpec(memory_space=pl.ANY)] * len(after),
        out_specs=out_spec)
    return pl.pallas_call(body, name="cast_" + name, grid_spec=grid_spec,
                          out_shape=jax.ShapeDtypeStruct(_gathered_shape(name, w), BF16),
                          compiler_params=_params(("parallel",)))(chip.reshape(1).astype(jnp.int32), w, *after)


def _w_in_columns(arr, to_shards):
    rows, piece = D_MODEL, (D_MAIN + N_DT) // N_CHIPS
    tr = ROW_TILE

    def body(a_ref, o_ref):
        for j in range(N_CHIPS):
            if to_shards:
                o_ref[j] = a_ref[:, pl.ds(piece * j, piece)]
            else:
                o_ref[:, pl.ds(piece * j, piece)] = a_ref[j]

    pieces = pl.BlockSpec((N_CHIPS, tr, piece), lambda i: (0, i, 0))
    matrix = pl.BlockSpec((tr, N_CHIPS * piece), lambda i: (i, 0))
    out_dims = (N_CHIPS, rows, piece) if to_shards else (rows, N_CHIPS * piece)
    return pl.pallas_call(
        body, name="w_in_to_shards" if to_shards else "w_in_from_shards", grid=(rows // tr,),
        in_specs=[matrix if to_shards else pieces], out_specs=pieces if to_shards else matrix,
        out_shape=jax.ShapeDtypeStruct(out_dims, arr.dtype), compiler_params=_params(("parallel",)))(arr)


HBM = pl.BlockSpec(memory_space=pltpu.HBM)
SEM = pl.BlockSpec(memory_space=pltpu.SEMAPHORE)
EFFECT = pltpu.SideEffectType.DATAFLOW_SIDE_EFFECTING


def _split_start(name, bufs, plan, counts, after=()):
    n, n_g, n_after = len(bufs), len(counts), len(after)

    def body(*refs):
        ins, sems, token = refs[:n], refs[n + n_after:n + n_after + 2 * n_g], refs[-1]
        for g, copies in enumerate(plan(ins)):
            for i, (src, dst, device, _) in enumerate(copies):
                _remote(src, dst, sems[2 * g].at[i], sems[2 * g + 1].at[i], device).start()
        token[...] = jnp.zeros_like(token)

    sem_shapes = [pltpu.SemaphoreType.DMA((cnt,)) for cnt in counts for _ in range(2)]
    res = pl.pallas_call(
        body, name=name,
        out_shape=(*sem_shapes, *[pltpu.HBM(b.shape, b.dtype) for b in bufs], jax.ShapeDtypeStruct((8, 128), F32)),
        in_specs=(*(HBM,) * n, *(ANY,) * n_after),
        out_specs=(*(SEM,) * (2 * n_g), *(HBM,) * n, pl.BlockSpec(memory_space=pltpu.VMEM)),
        input_output_aliases={i: 2 * n_g + i for i in range(n)},
        compiler_params=pltpu.CompilerParams(has_side_effects=EFFECT),
    )(*[pltpu.with_memory_space_constraint(b, pltpu.HBM) for b in bufs], *after)
    sems = [(res[2 * g], res[2 * g + 1]) for g in range(n_g)]
    return sems, list(res[2 * n_g:2 * n_g + n]), res[-1]


def _split_wait(name, bufs, sems, plan, *after):
    n = len(bufs)

    def body(*refs):
        ins, send, recv = refs[:n], refs[n], refs[n + 1]
        (copies,) = plan(ins)
        for i, (src, _, device, landing) in enumerate(copies):
            cp = _remote(src, landing, send.at[i], recv.at[i], device)
            cp.wait_send()
            cp.wait_recv()

    res = pl.pallas_call(
        body, name=name, out_shape=tuple(pltpu.HBM(b.shape, b.dtype) for b in bufs),
        in_specs=(*(HBM,) * n, SEM, SEM, *(ANY,) * len(after)), out_specs=(HBM,) * n,
        input_output_aliases={i: i for i in range(n)},
        compiler_params=pltpu.CompilerParams(has_side_effects=EFFECT),
    )(*bufs, sems[0], sems[1], *after)
    return list(res)


def _ici_plan(names, shard_shapes):
    def plan(refs):
        x, y, c = _place()
        copies = []
        for ref, name in zip(refs, names):
            win = _shard_window(name, ref, *shard_shapes[name], 2 * x + y, c)
            for px, py in _other_chips(x, y):
                copies.append((win, win, (px, py, c), _shard_window(name, ref, *shard_shapes[name], 2 * px + py, c)))
        return [copies]
    return plan


def _pass_on_plan(names, shard_shapes):
    def plan(refs):
        x, y, c = _place()
        copies = []
        for ref, name in zip(refs, names):
            for px, py in _other_chips(x, y):
                win = _shard_window(name, ref, *shard_shapes[name], 2 * px + py, c)
                copies.append((win, win, (x, y, 1 - c), _shard_window(name, ref, *shard_shapes[name], 2 * px + py, 1 - c)))
        return [copies]
    return plan


def _swap_plan(n_pairs):
    def plan(refs):
        x, y, c = _place()
        return [[(src.at[:, 1 - c], dst, (x, y, 1 - c), dst) for src, dst in zip(refs[:n_pairs], refs[n_pairs:])]]
    return plan


def _share_plan(n_pairs):
    def plan(refs):
        x, y, c = _place()
        return [[(src, dst, (x, y, 1 - c), dst)] for src, dst in zip(refs[:n_pairs], refs[n_pairs:])]
    return plan


def _scatter_plan(n_pairs):
    def plan(refs):
        x, y, c = _place()
        copies = []
        for src, dst in zip(refs[:n_pairs], refs[n_pairs:]):
            for k, (px, py) in enumerate(_other_chips(x, y)):
                copies.append((src.at[2 * px + py], dst.at[k], (px, py, c), dst.at[k]))
        return [copies]
    return plan


def _sibling_swap(arrs, name):
    n = len(arrs)

    def body(*refs):
        ins, outs, send, recv = refs[:n], refs[n:2 * n], refs[2 * n], refs[2 * n + 1]
        x, y, c = _place()
        cps = [_remote(ins[w].at[:, 1 - c], outs[w], send.at[w], recv.at[w], (x, y, 1 - c)) for w in range(n)]
        for cp in cps:
            cp.start()
        for cp in cps:
            cp.wait()

    return pl.pallas_call(
        body, name=name, in_specs=[ANY] * n, out_specs=[ANY] * n,
        out_shape=[jax.ShapeDtypeStruct((a.shape[0],) + a.shape[2:], a.dtype) for a in arrs],
        scratch_shapes=[pltpu.SemaphoreType.DMA((n,))] * 2,
    )(*arrs)


def _small_allreduce(buf, name, after=()):
    rows = buf.shape[0]

    def body(x_ref, *rest):
        out_ref, all_ref, send_sems, recv_sems, local_sem = rest[len(after):]
        x, y, c = _place()
        me, sibling, chips = (x, y, c), (x, y, 1 - c), _other_chips(x, y)

        def block(px, py, pc):
            return all_ref.at[pl.ds((4 * px + 2 * py + pc) * rows, rows), :]

        def copy(k, blk, to, src=None):
            return _remote(block(*blk) if src is None else src, block(*blk), send_sems.at[k], recv_sems.at[k], to)

        own = pltpu.make_async_copy(x_ref, block(*me), local_sem)
        own.start()
        first = [copy(0, me, sibling, src=x_ref)] + [copy(1 + j, me, (*chip, c), src=x_ref) for j, chip in enumerate(chips)]
        for cp in first:
            cp.start()
        passed = [copy(4 + j, (*chip, c), sibling) for j, chip in enumerate(chips)]
        for j, chip in enumerate(chips):
            copy(1 + j, (*chip, c), me).wait_recv()
            passed[j].start()
        copy(0, sibling, me).wait_recv()
        for j, chip in enumerate(chips):
            copy(4 + j, (*chip, 1 - c), me).wait_recv()
        for cp in first + passed:
            cp.wait_send()
        own.wait()
        acc = all_ref[pl.ds(0, rows), :]
        for d in range(1, 8):
            acc = acc + all_ref[pl.ds(d * rows, rows), :]
        out_ref[...] = acc

    vmem = pl.BlockSpec(memory_space=pltpu.VMEM)
    return pl.pallas_call(
        body, name=name, in_specs=[vmem] + [ANY] * len(after), out_specs=vmem,
        out_shape=jax.ShapeDtypeStruct(buf.shape, F32),
        scratch_shapes=[pltpu.VMEM((8 * rows, 128), F32), pltpu.SemaphoreType.DMA((7,)), pltpu.SemaphoreType.DMA((7,)),
                        pltpu.SemaphoreType.DMA],
    )(buf, *after)


ROW_TILE = 256
BIG_ROW_TILE = 1024


def _add_halves(arr, recv, c, name):
    _, _, hr, cols = arr.shape
    tr = _tile(hr, BIG_ROW_TILE)

    def body(c_ref, a_ref, r_ref, o_ref):
        o_ref[...] = (a_ref[...].astype(F32) + r_ref[...].astype(F32)).astype(o_ref.dtype)

    piece = pl.BlockSpec((None, tr, cols), lambda j, i, c_ref: (j, i, 0))
    grid_spec = pltpu.PrefetchScalarGridSpec(
        num_scalar_prefetch=1, grid=(N_CHIPS, hr // tr),
        in_specs=[pl.BlockSpec((None, None, tr, cols), lambda j, i, c_ref: (j, c_ref[0], i, 0)), piece], out_specs=piece)
    return pl.pallas_call(body, name=name, grid_spec=grid_spec, out_shape=jax.ShapeDtypeStruct(recv.shape, BF16),
                          compiler_params=_params(("parallel", "parallel")))(c.reshape(1).astype(jnp.int32), arr, recv)


def _flip_slot(d):
    return jnp.where(d == 1, 1, jnp.where(d == 3, 2, 0))


def _sum_chips(p, q, chip, name):
    _, hr, cols = p.shape
    tr = _tile(hr, BIG_ROW_TILE)

    def body(chip_ref, p_ref, q_ref, o_ref):
        j = pl.program_id(1)
        term = jnp.where(j == chip_ref[0], p_ref[...].astype(F32), q_ref[...].astype(F32))

        @pl.when(j == 0)
        def _():
            o_ref[...] = term

        @pl.when(j != 0)
        def _():
            o_ref[...] += term

    grid_spec = pltpu.PrefetchScalarGridSpec(
        num_scalar_prefetch=1, grid=(hr // tr, N_CHIPS),
        in_specs=[pl.BlockSpec((None, tr, cols), lambda i, j, chip_ref: (chip_ref[0], i, 0)),
                  pl.BlockSpec((None, tr, cols), lambda i, j, chip_ref: (_flip_slot(j ^ chip_ref[0]), i, 0))],
        out_specs=pl.BlockSpec((tr, cols), lambda i, j, chip_ref: (i, 0)))
    return pl.pallas_call(body, name=name, grid_spec=grid_spec, out_shape=jax.ShapeDtypeStruct((hr, cols), F32),
                          compiler_params=_params(("parallel", "arbitrary")))(chip.reshape(1).astype(jnp.int32), p, q)


def _adamw_halves(w, g_own, g_other, m, v, c, name):
    rows, cols = w.shape
    tr = _tile(rows // 2, ROW_TILE)
    per_half = rows // 2 // tr

    def body(c_ref, w_ref, own_ref, other_ref, m_ref, v_ref, g_ref, d_ref, nm_ref, nv_ref):
        mine = (pl.program_id(0) // per_half) == c_ref[0]
        g_ = jnp.where(mine, own_ref[...], other_ref[...])
        g_ref[...] = g_
        d_ref[...], nm_ref[...], nv_ref[...] = _adamw_math(w_ref[...], g_, m_ref[...], v_ref[...])

    blk = pl.BlockSpec((tr, cols), lambda i, c_ref: (i, 0))
    own = pl.BlockSpec((tr, cols), lambda i, c_ref: (jnp.where(i // per_half == c_ref[0], i % per_half, 0), 0))
    other = pl.BlockSpec((tr, cols), lambda i, c_ref: (jnp.where(i // per_half == c_ref[0], 0, i % per_half), 0))
    grid_spec = pltpu.PrefetchScalarGridSpec(num_scalar_prefetch=1, grid=(rows // tr,),
                                             in_specs=[blk, own, other, blk, blk], out_specs=[blk] * 4)
    return pl.pallas_call(body, name=name, grid_spec=grid_spec, out_shape=[jax.ShapeDtypeStruct(w.shape, F32)] * 4,
                          compiler_params=_params(("parallel",)))(c.reshape(1).astype(jnp.int32), w, g_own, g_other, m, v)


W_IN_COLS = (D_MAIN + N_DT) // N_CHIPS
W_IN_MAIN = W_IN_COLS // 128 * 128
W_IN_TAIL = W_IN_COLS - 128
W_IN_PARTS = ((0, W_IN_MAIN), (W_IN_TAIL, 128))


def _cast_w_in_transposed(w_t, chip, after=()):
    def body(chip_ref, w_ref, *rest):
        for start, size in W_IN_PARTS:
            rest[-1][:, pl.ds(start, size)] = w_ref[pl.ds(start, size), :].T.astype(BF16)

    grid_spec = pltpu.PrefetchScalarGridSpec(
        num_scalar_prefetch=1, grid=(D_MODEL // ROW_TILE,),
        in_specs=[pl.BlockSpec((W_IN_COLS, ROW_TILE), lambda i, chip_ref: (0, i))] + [pl.BlockSpec(memory_space=pl.ANY)] * len(after),
        out_specs=pl.BlockSpec((None, ROW_TILE, W_IN_COLS), lambda i, chip_ref: (chip_ref[0], i, 0)))
    return pl.pallas_call(body, name="cast_w_in", grid_spec=grid_spec,
                          out_shape=jax.ShapeDtypeStruct((N_CHIPS, D_MODEL, W_IN_COLS), BF16),
                          compiler_params=_params(("parallel",)))(chip.reshape(1).astype(jnp.int32), w_t, *after)


def _adamw_w_in_transposed(w_t, g_own, g_other, m_t, v_t, c):
    per_half = D_MODEL // 2 // ROW_TILE

    def body(c_ref, w_ref, own_ref, other_ref, m_ref, v_ref, g_ref, d_ref, nm_ref, nv_ref):
        mine = (pl.program_id(0) // per_half) == c_ref[0]
        for start, size in W_IN_PARTS:
            cols, rows = pl.ds(start, size), pl.ds(start, size)
            g_ = jnp.where(mine, own_ref[:, cols], other_ref[:, cols]).T
            g_ref[rows, :] = g_
            d_ref[rows, :], nm_ref[rows, :], nv_ref[rows, :] = _adamw_math(w_ref[rows, :], g_, m_ref[rows, :], v_ref[rows, :])

    blk = pl.BlockSpec((W_IN_COLS, ROW_TILE), lambda i, c_ref: (0, i))
    own = pl.BlockSpec((ROW_TILE, W_IN_COLS), lambda i, c_ref: (jnp.where(i // per_half == c_ref[0], i % per_half, 0), 0))
    other = pl.BlockSpec((ROW_TILE, W_IN_COLS), lambda i, c_ref: (jnp.where(i // per_half == c_ref[0], 0, i % per_half), 0))
    grid_spec = pltpu.PrefetchScalarGridSpec(num_scalar_prefetch=1, grid=(D_MODEL // ROW_TILE,),
                                             in_specs=[blk, own, other, blk, blk], out_specs=[blk] * 4)
    return pl.pallas_call(body, name="adamw_w_in", grid_spec=grid_spec, out_shape=[jax.ShapeDtypeStruct(w_t.shape, F32)] * 4,
                          compiler_params=_params(("parallel",)))(c.reshape(1).astype(jnp.int32), w_t, g_own, g_other, m_t, v_t)


def _adamw_math(w, g, m, v):
    m_new = ADAM_B1 * m + (1.0 - ADAM_B1) * g
    v_new = ADAM_B2 * v + (1.0 - ADAM_B2) * (g * g)
    m_hat = m_new / (1.0 - ADAM_B1 ** ADAM_STEP)
    v_hat = v_new / (1.0 - ADAM_B2 ** ADAM_STEP)
    return -ADAM_LR * (m_hat / (jnp.sqrt(v_hat) + ADAM_EPS) + ADAM_WD * w), m_new, v_new


VECTORS = ("g_mix", "g_q", "g_k", "g_attn_out", "conv_b", "dt_bias", "a_log", "d_skip", "g_ssm_out", "g_cross", "g_mem",
           "g_cq", "g_ck", "g_mlp")
WEIGHTS = ("g_mix", "w_in", "g_q", "g_k", "g_attn_out", "conv_w", "conv_b", "dt_bias", "a_log", "d_skip", "g_ssm_out", "w_out",
           "g_cross", "g_mem", "w_cq", "w_ckv", "g_cq", "g_ck", "w_co", "g_mlp", "w_up", "w_down")


def _pack(parts):
    flat = jnp.concatenate([t.reshape(-1) for t in parts])
    total = -(-flat.shape[0] // 1024) * 1024
    return jnp.pad(flat, (0, total - flat.shape[0])).reshape(total // 128, 128)


def _rows_of(n):
    return -(-n // 128)


def _slot_rows(n):
    return -(-n // 1024) * 8


def _pack_rows(parts):
    rows = []
    for t in parts:
        flat = t.reshape(-1)
        rows.append(jnp.pad(flat, (0, 128 * _slot_rows(flat.shape[0]) - flat.shape[0])).reshape(-1, 128))
    return jnp.concatenate(rows)


def _adamw_vectors(summed, chip, vectors, conv):
    groups = list(vectors) + [conv]
    offsets, row = [], 0
    for w, _, _ in groups:
        offsets.append(row)
        row += _slot_rows(w.shape[1]) if w.shape[0] == 1 else _slot_rows(w.shape[0] * N_CHIPS * w.shape[1])
    conv_blocks = _rows_of(conv[0].shape[1])

    def body(chip_ref, sum_ref, *refs):
        ins, outs = refs[:3 * len(groups)], refs[3 * len(groups):]

        def update(i, g, idx):
            w_ref, m_ref, v_ref = ins[3 * i:3 * i + 3]
            delta, new_m, new_v = _adamw_math(w_ref[idx], g, m_ref[idx], v_ref[idx])
            for o_ref, val in zip(outs[4 * i:4 * i + 4], (g, delta, new_m, new_v)):
                o_ref[idx] = val

        for i, (w, _, _) in enumerate(vectors):
            for t in range(_rows_of(w.shape[1])):
                width = min(128, w.shape[1] - 128 * t)
                update(i, sum_ref[pl.ds(offsets[i] + t, 1), pl.ds(0, width)], (slice(None), pl.ds(128 * t, width)))
        for tap in range(conv[0].shape[0]):
            for blk in range(conv_blocks):
                src = offsets[-1] + tap * N_CHIPS * conv_blocks + chip_ref[0] * conv_blocks + blk
                update(len(vectors), sum_ref[pl.ds(src, 1), :], (pl.ds(tap, 1), pl.ds(128 * blk, 128)))

    def whole(a):
        return pl.BlockSpec(a.shape, lambda i, chip_ref: (0,) * a.ndim)

    operands = [t for group in groups for t in group]
    grid_spec = pltpu.PrefetchScalarGridSpec(
        num_scalar_prefetch=1, grid=(1,), in_specs=[whole(summed)] + [whole(t) for t in operands],
        out_specs=[whole(w) for w, _, _ in groups for _ in range(4)])
    res = pl.pallas_call(body, name="adamw_vectors", grid_spec=grid_spec,
                         out_shape=[jax.ShapeDtypeStruct(w.shape, F32) for w, _, _ in groups for _ in range(4)],
                         compiler_params=_params(("arbitrary",)))(chip.reshape(1).astype(jnp.int32), summed, *operands)
    return [res[4 * i:4 * i + 4] for i in range(len(groups))]


def _unpack(buf, shapes):
    flat, out, pos = buf.reshape(-1), [], 0
    for shape in shapes:
        size = math.prod(shape)
        out.append(flat[pos:pos + size].reshape(shape))
        pos += size
    return out


def kernel(x, mem, positions, g_mix, w_in, g_q, g_k, g_attn_out, conv_w, conv_b, dt_bias, a_log, d_skip, g_ssm_out, w_out, g_cross, g_mem, w_cq, w_ckv, g_cq, g_ck, w_co, g_mlp, w_up, w_down, loss_target, m_g_mix, m_w_in, m_g_q, m_g_k, m_g_attn_out, m_conv_w, m_conv_b, m_dt_bias, m_a_log, m_d_skip, m_g_ssm_out, m_w_out, m_g_cross, m_g_mem, m_w_cq, m_w_ckv, m_g_cq, m_g_ck, m_w_co, m_g_mlp, m_w_up, m_w_down, v_g_mix, v_w_in, v_g_q, v_g_k, v_g_attn_out, v_conv_w, v_conv_b, v_dt_bias, v_a_log, v_d_skip, v_g_ssm_out, v_w_out, v_g_cross, v_g_mem, v_w_cq, v_w_ckv, v_g_cq, v_g_ck, v_w_co, v_g_mlp, v_w_up, v_w_down):
    args = dict(locals())
    weights = {n: args[n][0] for n in WEIGHTS}
    mom_m = {n: args["m_" + n][0] for n in WEIGHTS}
    mom_v = {n: args["v_" + n][0] for n in WEIGHTS}
    x_idx, y_idx, c_idx = _place()
    chip = 2 * x_idx + y_idx

    shapes = {n: weights[n].shape for n in MATRICES}
    first, mid, late = ("w_in",), ("w_out", "w_cq", "w_ckv", "w_co"), ("w_up", "w_down")
    w_in_t, m_in_t, v_in_t = (jnp.swapaxes(t, 1, 2)[0] for t in (w_in, m_w_in, v_w_in))
    w_in_buf = [_cast_w_in_transposed(w_in_t, chip)]
    sems_in, w_in_buf, token = _split_start("gather_ici_start_w_in", w_in_buf, _ici_plan(first, shapes), [3])
    bufs = [_cast_into_gathered(weights[n], n, chip, after=(token,)) for n in mid + late]
    params = {n: weights[n].reshape(1, -1) for n in VECTORS}
    h_in = _rowwise(_norm_fn, [_full(x[0])], [_full(params["g_mix"])], [(D_MODEL, BF16, D_MODEL, 0, False)], name="norm_in",
                    after=(token,))[0]
    taps, tap_cols = weights["conv_w"].shape
    conv_parts = _small_allreduce(_pack([jnp.zeros((N_CHIPS, taps, tap_cols), F32).at[chip].set(0.5 * weights["conv_w"])]),
                                  "gather_conv_taps", after=(h_in, m_in_t, v_in_t, *bufs))
    w_in_buf = _split_wait("gather_ici_wait_w_in", w_in_buf, sems_in[0], _ici_plan(first, shapes), token, conv_parts)
    pass_sems, w_in_buf, token = _split_start("gather_pass_start_w_in", w_in_buf, _pass_on_plan(first, shapes), [3])
    plan = lambda refs: _ici_plan(mid, shapes)(refs[:4]) + _ici_plan(late, shapes)(refs[4:])
    sems_rest, bufs, token = _split_start("gather_ici_start_rest", bufs, plan, [12, 6], after=(token,))
    w_in_buf = _split_wait("gather_pass_wait_w_in", w_in_buf, pass_sems[0], _pass_on_plan(first, shapes), token)
    w_in_full = _w_in_columns(w_in_buf[0], to_shards=False)
    full = {"w_in": w_in_full,
            "w_dt": jnp.pad(w_in_full[:, D_MAIN:].reshape(D_MODEL, N_GROUPS, HEADS_PER_GROUP),
                            ((0, 0), (0, 0), (0, 128 - HEADS_PER_GROUP))).reshape(D_MODEL, DT_PAD)}
    in_flight = {}

    def more_weights(stage, after):
        if stage == "mixer_done":
            got = _split_wait("gather_ici_wait_mid", bufs[:4], sems_rest[0], _ici_plan(mid, shapes), after)
            sems, got, token = _split_start("gather_pass_start_mid", got, _pass_on_plan(mid, shapes), [12])
            return dict(zip(mid, _split_wait("gather_pass_wait_mid", got, sems[0], _pass_on_plan(mid, shapes), token)))
        if stage == "cross_started":
            got = _split_wait("gather_ici_wait_late", bufs[4:], sems_rest[1], _ici_plan(late, shapes), after)
            in_flight["late"] = _split_start("gather_pass_start_late", got, _pass_on_plan(late, shapes), [6])
            return {}
        sems, got, token = in_flight.pop("late")
        return dict(zip(late, _split_wait("gather_pass_wait_late", got, sems[0], _pass_on_plan(late, shapes), token, after)))

    params["conv_w"] = _unpack(conv_parts, [(N_CHIPS, taps, tap_cols)])[0].transpose(1, 0, 2).reshape(taps, N_CHIPS * tap_cols)

    groups = (("w_down",), ("w_up",), ("w_co", "w_cq", "w_ckv", "w_out"), ("w_in",))
    scattered = []

    class GradStore(dict):
        pending = None

        def __setitem__(self, name, value):
            super().__setitem__(name, value)
            if "w_main" in self and "w_dt" in self and "w_in" not in self:
                gw_in = lax.dynamic_update_slice(self["w_main"], _unpad_heads(self["w_dt"]), (0, D_MAIN))
                self["w_in"] = _w_in_columns(gw_in, to_shards=True)
            for group in groups:
                if name in group and all(n in self for n in group):
                    self.settle()
                    pieces = [self[n].reshape(N_CHIPS, 2, shapes[n][0] // 2, shapes[n][1]) for n in group]
                    if group == groups[-1]:
                        self.scatter(group, pieces, _sibling_swap(pieces, "grad_swap_" + group[0]))
                    else:
                        landing = [lax.empty((N_CHIPS,) + a.shape[2:], BF16) for a in pieces]
                        sems, thru, self.token = _split_start("grad_swap_start_" + group[0], pieces + landing,
                                                              _swap_plan(len(pieces)), [len(pieces)])
                        self.pending = (group, sems[0], thru)

        def settle(self, *after):
            if self.pending is not None:
                group, sems, thru = self.pending
                self.pending = None
                thru = _split_wait("grad_swap_wait_" + group[0], thru, sems, _swap_plan(len(group)), *after)
                self.scatter(group, thru[:len(group)], thru[len(group):])

        def scatter(self, group, pieces, from_sibling):
            sums = [_add_halves(a, r, c_idx, "add_halves_" + n) for n, a, r in zip(group, pieces, from_sibling)]
            landing = [lax.empty((3,) + s.shape[1:], BF16) for s in sums]
            sems, thru, self.token = _split_start("grad_scatter_start_" + group[0], sums + landing,
                                                  _scatter_plan(len(sums)), [3 * len(sums)])
            scattered.append((group, sems[0], thru))

    loss, grad_x, grads = _local_step(x[0], mem[0], positions[0], loss_target[0], params, full, more_weights, GradStore(),
                                      h_in)

    out_g, out_d, out_m, out_v = {}, {}, {}, {}

    def finish(entries, order, token):
        halves = {}
        for group, sems, thru in entries:
            thru = _split_wait("grad_scatter_wait_" + group[0], thru, sems, _scatter_plan(len(group)), token)
            for i, n in enumerate(group):
                halves[n] = _sum_chips(thru[i], thru[len(group) + i], chip, "sum_chips_" + n)
        sources = [halves[n] for n in order]
        landing = [lax.empty(s.shape, F32) for s in sources]
        sems, thru, token = _split_start("grad_share_start_" + order[0], sources + landing, _share_plan(len(order)),
                                         [1] * len(order))
        for i, n in enumerate(order):
            own, other = _split_wait("grad_share_wait_" + n, [thru[i], thru[len(order) + i]], sems[i], _share_plan(1), token)
            if n == "w_in":
                res_t = _adamw_w_in_transposed(w_in_t, own, other, m_in_t, v_in_t, c_idx)
                out_g[n], out_d[n], out_m[n], out_v[n] = (t.T for t in res_t)
            else:
                out_g[n], out_d[n], out_m[n], out_v[n] = _adamw_halves(weights[n], own, other, mom_m[n], mom_v[n], c_idx,
                                                                       "adamw_" + n)
            token = out_v[n]
        return token

    token = finish(scattered[:-1], ("w_cq", "w_co", "w_ckv", "w_out", "w_up", "w_down"), grad_x)
    finish(scattered[-1:], ("w_in",), token)

    names = VECTORS + ("conv_w",)
    summed = _small_allreduce(_pack_rows([grads[n] for n in names] + [loss]), "allreduce_vectors")
    total_loss = summed[sum(_slot_rows(grads[n].size) for n in names), 0]
    small_out = _adamw_vectors(summed, chip, [(args[n], args["m_" + n], args["v_" + n]) for n in VECTORS],
                               (weights["conv_w"], mom_m["conv_w"], mom_v["conv_w"]))
    for n, res in zip(names, small_out):
        out_g[n], out_d[n], out_m[n], out_v[n] = (t.reshape(weights[n].shape) for t in res)

    outs =[total_loss, grad_x[None]]
    for group in (out_g, out_d, out_m, out_v):
        outs += [group[n][None] for n in WEIGHTS]
    return tuple(outs)
```

```python
import functools
import math

import jax
import jax.numpy as jnp
from jax import lax
from jax.experimental import pallas as pl
from jax.experimental.pallas import tpu as pltpu

F32 = jnp.float32
BF16 = jnp.bfloat16

SEQ = 2048
D_MODEL = 2048
HEAD = 64
D_ATTN = 1024
D_SSM = 1024
N_GROUPS = 4
N_STATE = 128
CHUNK = 128
ATT_BLK = 128
N_MEM = 256
D_CROSS = 512
D_MAIN = 6144
N_DT = 16
DT_PAD = 512
ROT = 16
ROPE_THETA = 500000.0
EPS = 1e-6
NEG = -1e30
BRANCH_BLOCKS = (16, 4, 1)
DILATIONS = (1, 4, 16)

ADAM_LR, ADAM_B1, ADAM_B2, ADAM_EPS, ADAM_WD, ADAM_STEP = 0.001, 0.9, 0.999, 1e-08, 0.01, 10

VMEM_LIMIT = 56 * 1024 * 1024
MESH = pl.DeviceIdType.MESH


def _params(sem, **kw):
    return pltpu.CompilerParams(dimension_semantics=sem, vmem_limit_bytes=VMEM_LIMIT, **kw)


def _bdot(a, b, dims):
    return lax.dot_general(a.astype(BF16), b.astype(BF16), (dims, ((), ())), preferred_element_type=F32)


def _fdot(a, b, dims):
    return lax.dot_general(a, b, (dims, ((), ())), preferred_element_type=F32, precision=lax.Precision.HIGHEST)


NN = ((1,), (0,))
NT = ((1,), (1,))
TN = ((0,), (0,))


def _tile(n, want):
    t = min(n, want)
    while n % t:
        t //= 2
    return t


def _matmul(a, b, *, mode, name, outs, extra=(), vecs=(), epilogue=None, col_shards=1, after=(), n_cols=None, out_cols=None,
            tile_rows=0, tile_sums=0, tm=1024, tn=1024, tk=2048):
    if mode == "nn":
        (m, k), n = a.shape, b.shape[1]
    elif mode == "nt":
        (m, k), n = a.shape, b.shape[0]
    else:
        (k, m), n = a.shape, b.shape[1]
    n = n if n_cols is None else n_cols
    tm, tn, tk = _tile(m, tm), _tile(n // col_shards, tn), _tile(k, tk)
    nk = k // tk
    per_shard = n // col_shards // tn
    dims = {"nn": NN, "nt": NT, "tn": TN}[mode]
    a_spec = pl.BlockSpec((tk, tm), lambda i, j, kk: (kk, i)) if mode == "tn" else pl.BlockSpec((tm, tk), lambda i, j, kk: (i, kk))
    b_spec = pl.BlockSpec((tn, tk), lambda i, j, kk: (j, kk)) if mode == "nt" else pl.BlockSpec((tk, tn), lambda i, j, kk: (kk, j))
    o_spec = pl.BlockSpec((tm, tn), lambda i, j, kk: (i, j))
    n_extra, n_out, n_after = len(extra) + len(vecs), len(outs), len(after)

    def body(a_ref, b_ref, *rest):
        extra_refs, out_refs, acc_ref = rest[:n_extra], rest[n_extra + n_after:-1], rest[-1]

        def finish(acc):
            res = (acc,) if epilogue is None else epilogue(acc, *[e[...] for e in extra_refs])
            for o_ref, r in zip(out_refs[:n_out], res):
                o_ref[...] = r.astype(o_ref.dtype)
            for o_ref, r in zip(out_refs[n_out:], res[n_out:]):
                o_ref[...] = jnp.broadcast_to(r, o_ref.shape)

        if nk == 1:
            finish(_bdot(a_ref[...], b_ref[...], dims))
            return
        kk = pl.program_id(2)

        @pl.when(kk == 0)
        def _():
            acc_ref[...] = jnp.zeros_like(acc_ref)

        acc_ref[...] += _bdot(a_ref[...], b_ref[...], dims)

        @pl.when(kk == nk - 1)
        def _():
            finish(acc_ref[...])

    if col_shards == 1:
        out_specs, out_dims = [o_spec] * n_out, (m, n if out_cols is None else out_cols)
    else:
        sharded = pl.BlockSpec((None, tm, tn), lambda i, j, kk: (j // per_shard, i, j % per_shard))
        out_specs, out_dims = [sharded] * n_out, (col_shards, m, n // col_shards)
    res = pl.pallas_call(
        body, name=name, grid=(m // tm, n // tn, nk),
        in_specs=[a_spec, b_spec] + [o_spec] * len(extra) + [pl.BlockSpec((1, tn), lambda i, j, kk: (0, j))] * len(vecs)
        + [pl.BlockSpec(memory_space=pl.ANY)] * n_after,
        out_specs=out_specs + [pl.BlockSpec((8, tn), lambda i, j, kk: (i, j))] * tile_rows
        + [pl.BlockSpec((8, 128), lambda i, j, kk: (i, j))] * tile_sums,
        out_shape=[jax.ShapeDtypeStruct(out_dims, dt) for dt in outs] + [jax.ShapeDtypeStruct((m // tm * 8, n), F32)] * tile_rows
        + [jax.ShapeDtypeStruct((m // tm * 8, n // tn * 128), F32)] * tile_sums,
        scratch_shapes=[pltpu.VMEM((tm, tn) if nk > 1 else (8, 128), F32)],
        compiler_params=_params(("parallel", "parallel", "arbitrary")),
    )(a, b, *extra, *vecs, *after)
    res = (list(res[:n_out]) + [jnp.sum(t[::8], axis=0, keepdims=True) for t in res[n_out:n_out + tile_rows]]
           + [t[::8, ::128] for t in res[n_out + tile_rows:]])
    return res[0] if len(res) == 1 else res


def _row_spec(tr, bw, cb, per_group):
    return pl.BlockSpec((tr, bw), (lambda g, i: (i, cb + g)) if per_group else (lambda g, i: (i, cb)))


def _vec_spec(bw, cb, per_group):
    return pl.BlockSpec((1, bw), (lambda g, i: (0, cb + g)) if per_group else (lambda g, i: (0, cb)))


def _rowwise(fn, rows, vecs, outs, *, name, n_rows=SEQ, tr=512, groups=1, after=()):
    n_r, n_v, n_after = len(rows), len(vecs), len(after)

    def body(*refs):
        vals = [r[...].astype(F32) for r in refs[:n_r + n_v]]
        res = fn(*vals)
        for o_ref, r in zip(refs[n_r + n_v + n_after:], res):
            o_ref[...] = r.astype(o_ref.dtype)

    res = pl.pallas_call(
        body, name=name, grid=(groups, n_rows // tr),
        in_specs=[_row_spec(tr, bw, cb, pg) for _, bw, cb, pg in rows] + [_vec_spec(bw, cb, pg) for _, bw, cb, pg in vecs]
        + [pl.BlockSpec(memory_space=pl.ANY)] * n_after,
        out_specs=[_row_spec(tr, bw, cb, pg) for _, _, bw, cb, pg in outs],
        out_shape=[jax.ShapeDtypeStruct((n_rows, w), dt) for w, dt, _, _, _ in outs],
        compiler_params=_params(("parallel", "parallel")),
    )(*[r[0] for r in rows], *[v[0] for v in vecs], *after)
    return res


def _rowwise_vjp(fn, rows, vecs, cts, row_grads, vec_grads, *, name, n_rows=SEQ, tr=512, groups=1, after=()):
    n_r, n_v, n_after = len(rows), len(vecs), len(after)
    ct_ops = [op for group in cts for op in group]
    ct_sizes = [len(group) for group in cts]
    res_ops = [g[6] for g in row_grads if g[6] is not None]
    n_ct, n_res, n_rg = len(ct_ops), len(res_ops), len(row_grads)

    def body(*refs):
        vals = [r[...].astype(F32) for r in refs[:n_r + n_v]]
        pos = n_r + n_v
        ct_vals = []
        for size in ct_sizes:
            acc = refs[pos][...].astype(F32)
            for t in range(1, size):
                acc = acc + refs[pos + t][...].astype(F32)
            ct_vals.append(acc)
            pos += size
        res_refs = refs[pos:pos + n_res]
        out_refs = refs[pos + n_res + n_after:]
        _, pullback = jax.vjp(fn, *vals)
        grads = pullback(tuple(ct_vals))
        r_i = 0
        for o_ref, g in zip(out_refs[:n_rg], row_grads):
            val = grads[g[0]]
            if g[6] is not None:
                val = val + res_refs[r_i][...].astype(F32)
                r_i += 1
            o_ref[...] = val.astype(o_ref.dtype)
        first = (pl.program_id(1) == 0)
        for o_ref, g in zip(out_refs[n_rg:], vec_grads):
            val = jnp.sum(grads[n_r + g[0]], axis=0, keepdims=True)
            init = first if g[4] else jnp.logical_and(first, pl.program_id(0) == 0)

            @pl.when(init)
            def _(o_ref=o_ref, val=val):
                o_ref[...] = val

            @pl.when(jnp.logical_not(init))
            def _(o_ref=o_ref, val=val):
                o_ref[...] += val

    in_specs = [_row_spec(tr, bw, cb, pg) for _, bw, cb, pg in rows] + [_vec_spec(bw, cb, pg) for _, bw, cb, pg in vecs]
    in_specs += [_row_spec(tr, bw, cb, pg) for _, bw, cb, pg in ct_ops + res_ops] + [pl.BlockSpec(memory_space=pl.ANY)] * n_after
    out_specs =[_row_spec(tr, g[3], g[4], g[5]) for g in row_grads] + [_vec_spec(g[2], g[3], g[4]) for g in vec_grads]
    out_shape = [jax.ShapeDtypeStruct((n_rows, g[1]), g[2]) for g in row_grads]
    out_shape += [jax.ShapeDtypeStruct((1, g[1]), F32) for g in vec_grads]
    return pl.pallas_call(
        body, name=name, grid=(groups, n_rows // tr),
        in_specs=in_specs, out_specs=out_specs, out_shape=out_shape,
        compiler_params=_params(("arbitrary", "arbitrary")),
    )(*[r[0] for r in rows], *[v[0] for v in vecs], *[c[0] for c in ct_ops], *[r[0] for r in res_ops], *after)


def _full(arr, width=None):
    return (arr, arr.shape[1] if width is None else width, 0, False)


def _make_xor(sh):
    def raw(x):
        n = x.shape[-1]
        lane = lax.broadcasted_iota(jnp.int32, x.shape, x.ndim - 1)
        up = pltpu.roll(x, n - sh, x.ndim - 1)
        down = pltpu.roll(x, sh, x.ndim - 1)
        return jnp.where((lane & sh) == 0, up, down)

    f = jax.custom_vjp(raw)
    f.defvjp(lambda x: (raw(x), None), lambda _, ct: (raw(ct),))
    return f


_SWAP_ROPE_HALVES = _make_xor(ROT // 2)


def _head_sum(x):
    n = x.shape[-1]
    same_head = (lax.broadcasted_iota(jnp.int32, (n, n), 0) // HEAD) == (lax.broadcasted_iota(jnp.int32, (n, n), 1) // HEAD)
    return _fdot(x, same_head.astype(F32), NN)


def _rms(x, g):
    return x * lax.rsqrt(jnp.mean(x * x, axis=-1, keepdims=True) + EPS) * g


def _head_rms_rope(x, g, cos, sin, scale):
    y = x * lax.rsqrt(_head_sum(x * x) * (1.0 / HEAD) + EPS) * g
    return (y * cos + _SWAP_ROPE_HALVES(y) * sin) * scale


def _qk_fn(q, k, v, cos, sin, gq, gk):
    return (_head_rms_rope(q, gq, cos, sin, HEAD ** -0.5), _head_rms_rope(k, gk, cos, sin, 1.0), v)


def _norm_fn(x, g):
    return (_rms(x, g),)


def _merge_fn(o0, o1, o2, l0, l1, l2, g):
    m = lax.stop_gradient(jnp.maximum(jnp.maximum(l0, l1), l2))
    e0, e1, e2 = jnp.exp(l0 - m), jnp.exp(l1 - m), jnp.exp(l2 - m)
    mix = (e0 * o0 + e1 * o1 + e2 * o2) / (e0 + e1 + e2)
    return (_rms(mix, g),)


def _gate_fn(y, z, g):
    return (_rms(y * (z * jax.nn.sigmoid(z)), g),)


def _attn_pair(q, kc, vc, kp=None, vp=None, has_prev=None):
    pick0, pick1 = _head_picks()
    k_band, v_band, mask = _attn_band(kc, vc, kp, vp, has_prev)
    s = jnp.where(mask, _bdot(jnp.concatenate([q * pick0, q * pick1], axis=0), k_band, NT), NEG)
    m = jnp.max(s, axis=-1, keepdims=True)
    p = jnp.exp(s - m)
    den = jnp.sum(p, axis=-1, keepdims=True)
    acc = _bdot(p, v_band, NN) * (1.0 / den)
    lse_rows = m + jnp.log(den)
    o = pick0 * acc[:ATT_BLK] + pick1 * acc[ATT_BLK:]
    lse = pick0 * lse_rows[:ATT_BLK] + pick1 * lse_rows[ATT_BLK:]
    return o, lse


def _head_picks():
    lane = lax.broadcasted_iota(jnp.int32, (1, 2 * HEAD), 1)
    return (lane < HEAD).astype(F32), (lane >= HEAD).astype(F32)


def _attn_band(kc, vc, kp, vp, has_prev):
    n_keys = ATT_BLK if kp is None else 2 * ATT_BLK
    qi = lax.broadcasted_iota(jnp.int32, (2 * ATT_BLK, n_keys), 0) & (ATT_BLK - 1)
    kj = lax.broadcasted_iota(jnp.int32, (2 * ATT_BLK, n_keys), 1)
    if kp is None:
        return kc, vc, qi >= kj
    in_prev = jnp.logical_and(jnp.logical_and(kj < ATT_BLK, kj >= qi), has_prev)
    mask = jnp.logical_or(in_prev, jnp.logical_and(kj >= ATT_BLK, qi >= kj - ATT_BLK))
    return jnp.concatenate([kp, kc], axis=0), jnp.concatenate([vp, vc], axis=0), mask


def _attn_config(b):
    r = DILATIONS[b]
    return r, ATT_BLK * r, (D_ATTN if r == 1 else 128), BRANCH_BLOCKS[b] > 1


def _for_residues(r, fn):
    if r <= 4:
        for rho in range(r):
            fn(rho)
    else:
        def step(t, carry):
            for u in range(4):
                fn(4 * t + u)
            return carry

        lax.fori_loop(0, r // 4, step, 0)


def _strided_rows(start, r):
    if r > 1:
        return pl.ds(start, ATT_BLK, stride=r)
    return pl.ds(start if isinstance(start, int) else pl.multiple_of(start, ATT_BLK), ATT_BLK)


def _attention_fwd(qn, kn, vn, b):
    r, rows, lanes, with_prev = _attn_config(b)
    cur = pl.BlockSpec((rows, lanes), lambda g, n: (n, g))
    prev = pl.BlockSpec((rows, lanes), lambda g, n: (jnp.maximum(n - 1, 0), g))

    def body(*refs):
        ins, (o_ref, l_ref) = refs[:-2], refs[-2:]
        has_prev = pl.program_id(1) > 0

        def one(rho):
            sub = _strided_rows(rho, r)
            for pair in range(lanes // 128):
                sl = pl.ds(pair * 128, 128)
                args = [ref[sub, sl] for ref in ins] + ([has_prev] if with_prev else [])
                o_ref[sub, sl], l_ref[sub, sl] = _attn_pair(*args)

        _for_residues(r, one)

    operands = (qn, kn, vn, kn, vn) if with_prev else (qn, kn, vn)
    return pl.pallas_call(
        body, name="attn_fwd_%d" % r, grid=(D_ATTN // lanes, SEQ // rows),
        in_specs=[cur, cur, cur] + ([prev, prev] if with_prev else []), out_specs=[cur, cur],
        out_shape=[jax.ShapeDtypeStruct((SEQ, D_ATTN), F32)] * 2,
        compiler_params=_params(("parallel", "parallel")),
    )(*operands)


def _attn_pair_bwd(q, kc, vc, kp, vp, o, lse, do, dl, has_prev):
    pick0, pick1 = _head_picks()
    lane = lax.broadcasted_iota(jnp.int32, (1, 2 * HEAD), 1)
    k_band, v_band, mask = _attn_band(kc, vc, kp, vp, has_prev)
    q2 = jnp.concatenate([q * pick0, q * pick1], axis=0)
    do2 = jnp.concatenate([do * pick0, do * pick1], axis=0)
    lse2 = jnp.concatenate([jnp.sum(lse * (lane == 0).astype(F32), axis=-1, keepdims=True),
                            jnp.sum(lse * (lane == HEAD).astype(F32), axis=-1, keepdims=True)], axis=0)
    base = jnp.sum(jnp.concatenate([dl * pick0, dl * pick1], axis=0) - do2 * jnp.concatenate([o, o], axis=0),
                   axis=-1, keepdims=True)
    p = jnp.exp(jnp.where(mask, _bdot(q2, k_band, NT), NEG) - lse2)
    ds = p * (_bdot(do2, v_band, NT) + base)
    dq2 = _bdot(ds, k_band, NN)
    dq = pick0 * dq2[:ATT_BLK] + pick1 * dq2[ATT_BLK:]
    dk, dv = _bdot(ds, q2, TN), _bdot(p, do2, TN)
    if kp is None:
        return dq, dk, dv
    return dq, dk[ATT_BLK:], dv[ATT_BLK:], dk[:ATT_BLK], dv[:ATT_BLK]


def _attention_bwd(qn, kn, vn, o, lse, do, dl, b):
    r, rows, lanes, with_prev = _attn_config(b)
    cur = pl.BlockSpec((rows, lanes), lambda g, n: (n, g))
    prev = pl.BlockSpec((rows, lanes), lambda g, n: (jnp.maximum(n - 1, 0), g))
    whole = pl.BlockSpec((SEQ, lanes), lambda g, n: (0, g))
    n_in = 5 if with_prev else 3

    def body(*refs):
        ins, (o_ref, l_ref, do_ref, dl_ref, dq_ref, dk_ref, dv_ref) = refs[:n_in], refs[n_in:]
        n = pl.program_id(1)

        @pl.when(n == 0)
        def _():
            dk_ref[...] = jnp.zeros_like(dk_ref)
            dv_ref[...] = jnp.zeros_like(dv_ref)

        def one(rho):
            sub = _strided_rows(rho, r)
            sub_c = _strided_rows(n * rows + rho, r)
            sub_p = _strided_rows(jnp.maximum(n - 1, 0) * rows + rho, r)
            for pair in range(lanes // 128):
                sl = pl.ds(pair * 128, 128)
                vals = [ref[sub, sl] for ref in ins] + ([] if with_prev else [None, None])
                grads = _attn_pair_bwd(*vals, o_ref[sub, sl], l_ref[sub, sl], do_ref[sub, sl], dl_ref[sub, sl], n > 0)
                dq_ref[sub, sl] = grads[0]
                dk_ref[sub_c, sl] += grads[1]
                dv_ref[sub_c, sl] += grads[2]
                if with_prev:
                    dk_ref[sub_p, sl] += grads[3]
                    dv_ref[sub_p, sl] += grads[4]

        _for_residues(r, one)

    operands = (qn, kn, vn, kn, vn) if with_prev else (qn, kn, vn)
    return pl.pallas_call(
        body, name="attn_bwd_%d" % r, grid=(D_ATTN // lanes, SEQ // rows),
        in_specs=[cur, cur, cur] + ([prev, prev] if with_prev else []) + [cur] * 4, out_specs=[cur, whole, whole],
        out_shape=[jax.ShapeDtypeStruct((SEQ, D_ATTN), F32)] * 3,
        compiler_params=_params(("parallel", "arbitrary")),
    )(*operands, o, lse, do, dl)


CONV_COLS = 256
XBC_BLOCK0 = (3 * D_ATTN + D_SSM) // CONV_COLS


def _shift_rows(x, s):
    n = x.shape[0]
    t = lax.broadcasted_iota(jnp.int32, x.shape, 0)
    if s >= 0:
        return jnp.where(t >= s, pltpu.roll(x, s, 0), 0.0)
    return jnp.where(t < n + s, pltpu.roll(x, n + s, 0), 0.0)


def _conv_pre(x, w_ref, b_ref):
    delayed = [_shift_rows(x, 3 - k) for k in range(3)]
    pre = b_ref[...] + w_ref[3:4, :] * x
    for k in range(3):
        pre = pre + w_ref[k:k + 1, :] * delayed[k]
    return pre, delayed


def _conv_fwd(proj, conv_w, conv_b):
    cols = conv_w.shape[1]

    def body(x_ref, w_ref, b_ref, o_ref):
        pre, _ = _conv_pre(x_ref[...], w_ref, b_ref)
        o_ref[...] = pre * jax.nn.sigmoid(pre)

    blk = pl.BlockSpec((SEQ, CONV_COLS), lambda j: (0, j))
    return pl.pallas_call(
        body, name="conv_fwd", grid=(cols // CONV_COLS,),
        in_specs=[pl.BlockSpec((SEQ, CONV_COLS), lambda j: (0, XBC_BLOCK0 + j)),
                  pl.BlockSpec((4, CONV_COLS), lambda j: (0, j)), pl.BlockSpec((1, CONV_COLS), lambda j: (0, j))],
        out_specs=blk, out_shape=jax.ShapeDtypeStruct((SEQ, cols), F32),
        compiler_params=_params(("parallel",)),
    )(proj, conv_w, conv_b)


def _conv_bwd(proj, conv_w, conv_b, dxs, db, dc):
    cols = conv_w.shape[1]
    x_blocks, b_blocks = dxs.shape[1] // CONV_COLS, db.shape[1] // CONV_COLS

    def body(x_ref, w_ref, b_ref, dxs_ref, db_ref_in, dc_ref_in, dx_ref, dw_ref, db_ref):
        j = pl.program_id(0)
        dy = jnp.where(j < x_blocks, dxs_ref[...], jnp.where(j < x_blocks + b_blocks, db_ref_in[...], dc_ref_in[...]))
        x = x_ref[...]
        pre, delayed = _conv_pre(x, w_ref, b_ref)
        sg = jax.nn.sigmoid(pre)
        dpre = dy * (sg * (1.0 + pre * (1.0 - sg)))
        db_ref[...] = jnp.sum(dpre, axis=0, keepdims=True)
        dx = w_ref[3:4, :] * dpre
        dw_ref[3:4, :] = jnp.sum(dpre * x, axis=0, keepdims=True)
        for k in range(3):
            dx = dx + w_ref[k:k + 1, :] * _shift_rows(dpre, k - 3)
            dw_ref[k:k + 1, :] = jnp.sum(dpre * delayed[k], axis=0, keepdims=True)
        dw_ref[4:8, :] = jnp.zeros((4, CONV_COLS), F32)
        dx_ref[...] = dx.astype(dx_ref.dtype)

    blk = pl.BlockSpec((SEQ, CONV_COLS), lambda j: (0, j))
    parts = [pl.BlockSpec((SEQ, CONV_COLS), lambda j: (0, jnp.minimum(j, x_blocks - 1))),
             pl.BlockSpec((SEQ, CONV_COLS), lambda j: (0, jnp.clip(j - x_blocks, 0, b_blocks - 1))),
             pl.BlockSpec((SEQ, CONV_COLS), lambda j: (0, jnp.clip(j - x_blocks - b_blocks, 0, b_blocks - 1)))]
    return pl.pallas_call(
        body, name="conv_bwd", grid=(cols // CONV_COLS,),
        in_specs=[pl.BlockSpec((SEQ, CONV_COLS), lambda j: (0, XBC_BLOCK0 + j)),
                  pl.BlockSpec((4, CONV_COLS), lambda j: (0, j)), pl.BlockSpec((1, CONV_COLS), lambda j: (0, j))] + parts,
        out_specs=[blk, pl.BlockSpec((8, CONV_COLS), lambda j: (0, j)), pl.BlockSpec((1, CONV_COLS), lambda j: (0, j))],
        out_shape=[jax.ShapeDtypeStruct((SEQ, cols), BF16), jax.ShapeDtypeStruct((8, cols), F32),
                   jax.ShapeDtypeStruct((1, cols), F32)],
        compiler_params=_params(("parallel",)),
    )(proj, conv_w, conv_b, dxs, db, dc)


HEADS_PER_GROUP = 4


GROUP_WIDTH = HEADS_PER_GROUP * HEAD


def _ssd_chunk(x, bm, cm, dtr, bias, alog, dsk, h):
    row = lax.broadcasted_iota(jnp.int32, (CHUNK, CHUNK), 0)
    col = lax.broadcasted_iota(jnp.int32, (CHUNK, CHUNK), 1)
    causal = row >= col
    z = dtr + bias
    dt = jnp.maximum(z, 0.0) + jnp.log(1.0 + jnp.exp(-jnp.abs(z)))
    acs = _fdot(causal.astype(F32), dt * -jnp.exp(alog), NN)
    acs_t, dt_t = acs.T, dt.T
    cb = _bdot(cm, bm, NT)
    lane = lax.broadcasted_iota(jnp.int32, (1, CHUNK), 1)
    sub = lax.broadcasted_iota(jnp.int32, (CHUNK, 1), 0)
    wide = lax.broadcasted_iota(jnp.int32, (1, GROUP_WIDTH), 1) // HEAD
    tall = lax.broadcasted_iota(jnp.int32, (GROUP_WIDTH, 1), 0) // HEAD
    acs_last = jnp.sum(acs * (sub == CHUNK - 1).astype(F32), axis=0, keepdims=True)
    to_lanes = (lax.broadcasted_iota(jnp.int32, (CHUNK, GROUP_WIDTH), 0)
                == lax.broadcasted_iota(jnp.int32, (CHUNK, GROUP_WIDTH), 1) // HEAD).astype(F32)
    grow = _fdot(jnp.exp(acs), to_lanes, NN)
    keep = _fdot(jnp.exp(acs_last - acs) * dt, to_lanes, NN)
    w_parts, x_parts, skip, carry = [], [], 0.0, 0.0
    for j in range(HEADS_PER_GROUP):
        on_lane, on_sub = (lane == j).astype(F32), (sub == j).astype(F32)
        acs_c = jnp.sum(acs * on_lane, axis=1, keepdims=True)
        acs_r = jnp.sum(acs_t * on_sub, axis=0, keepdims=True)
        dt_r = jnp.sum(dt_t * on_sub, axis=0, keepdims=True)
        w_parts.append(cb * jnp.exp(jnp.where(causal, acs_c - acs_r, NEG)) * dt_r)
        x_parts.append(x * (wide == j).astype(F32))
        skip = skip + jnp.sum(dsk * on_lane, axis=1, keepdims=True) * (wide == j).astype(F32)
        carry = carry + jnp.sum(jnp.exp(acs_last) * on_lane, axis=1, keepdims=True) * (tall == j).astype(F32)
    y_diag = _bdot(jnp.concatenate(w_parts, axis=1), jnp.concatenate(x_parts, axis=0), NN)
    y = y_diag + _bdot(cm, h, NT) * grow + skip * x
    return y, h * carry + _bdot(x * keep, bm, TN)


GROUPS_PER_STEP = 2
SSD_STEPS = N_GROUPS // GROUPS_PER_STEP


def _ssd_specs(reverse):
    n_chunks = SEQ // CHUNK
    c_of = (lambda c: n_chunks - 1 - c) if reverse else (lambda c: c)
    x_w, n_w, dt_w = GROUPS_PER_STEP * GROUP_WIDTH, GROUPS_PER_STEP * N_STATE, GROUPS_PER_STEP * 128
    x_spec = pl.BlockSpec((CHUNK, x_w), lambda g, c: (c_of(c), g))
    b_spec = pl.BlockSpec((CHUNK, n_w), lambda g, c: (c_of(c), D_SSM // n_w + g))
    c_spec = pl.BlockSpec((CHUNK, n_w), lambda g, c: (c_of(c), (D_SSM + N_GROUPS * N_STATE) // n_w + g))
    dt_spec = pl.BlockSpec((CHUNK, dt_w), lambda g, c: (c_of(c), g))
    vec_spec = pl.BlockSpec((1, dt_w), lambda g, c: (0, g))
    h_spec = pl.BlockSpec((None, GROUPS_PER_STEP, GROUP_WIDTH, N_STATE), lambda g, c: (c_of(c), g, 0, 0))
    return x_spec, b_spec, c_spec, dt_spec, vec_spec, h_spec


def _group_slices(u):
    return pl.ds(u * GROUP_WIDTH, GROUP_WIDTH), pl.ds(u * N_STATE, N_STATE), pl.ds(u * 128, 128)


def _ssd_gated_chunk(x, bm, cm, dtr, bias, alog, dsk, h, z, g_out):
    y, h_new = _ssd_chunk(x, bm, cm, dtr, bias, alog, dsk, h)
    return _gate_fn(y, z, g_out)[0], h_new


def _ssd_gate_specs(reverse):
    x_spec = _ssd_specs(reverse)[0]
    z_block0 = 3 * D_ATTN // x_spec.block_shape[1]
    z_spec = pl.BlockSpec(x_spec.block_shape, lambda g, c: (x_spec.index_map(g, c)[0], z_block0 + g))
    return z_spec, pl.BlockSpec((1, x_spec.block_shape[1]), lambda g, c: (0, g))


def _ssd_fwd(xbc, dt_raw, bias, alog, dsk, proj, g_out):
    x_spec, b_spec, c_spec, dt_spec, vec_spec, h_spec = _ssd_specs(False)
    z_spec, g_spec = _ssd_gate_specs(False)

    def body(x_ref, b_ref, c_ref, dt_ref, bias_ref, alog_ref, dsk_ref, z_ref, g_ref, ssm_ref, hin_ref, h_scr):
        @pl.when(pl.program_id(1) == 0)
        def _():
            h_scr[...] = jnp.zeros_like(h_scr)

        for u in range(GROUPS_PER_STEP):
            xs, ns, ds = _group_slices(u)
            h = h_scr[u]
            hin_ref[u] = h
            ssm, h_scr[u] = _ssd_gated_chunk(x_ref[:, xs], b_ref[:, ns], c_ref[:, ns], dt_ref[:, ds], bias_ref[:, ds],
                                             alog_ref[:, ds], dsk_ref[:, ds], h, z_ref[:, xs], g_ref[:, xs])
            ssm_ref[:, xs] = ssm.astype(ssm_ref.dtype)

    return pl.pallas_call(
        body, name="ssd_fwd", grid=(SSD_STEPS, SEQ // CHUNK),
        in_specs=[x_spec, b_spec, c_spec, dt_spec, vec_spec, vec_spec, vec_spec, z_spec, g_spec],
        out_specs=[x_spec, h_spec],
        out_shape=[jax.ShapeDtypeStruct((SEQ, D_SSM), BF16),
                   jax.ShapeDtypeStruct((SEQ // CHUNK, N_GROUPS, GROUP_WIDTH, N_STATE), F32)],
        scratch_shapes=[pltpu.VMEM((GROUPS_PER_STEP, GROUP_WIDTH, N_STATE), F32)],
        compiler_params=_params(("parallel", "arbitrary")),
    )(xbc, xbc, xbc, dt_raw, bias, alog, dsk, proj, g_out)


def _ssd_bwd(xbc, dt_raw, bias, alog, dsk, h_in, proj, g_out, dmix):
    x_spec, b_spec, c_spec, dt_spec, vec_spec, h_spec = _ssd_specs(True)
    z_spec, g_spec = _ssd_gate_specs(True)
    ct_block0 = D_ATTN // x_spec.block_shape[1]
    ct_spec = pl.BlockSpec(x_spec.block_shape, lambda g, c: (x_spec.index_map(g, c)[0], ct_block0 + g))

    def body(x_ref, b_ref, c_ref, dt_ref, bias_ref, alog_ref, dsk_ref, hin_ref, z_ref, g_ref, ct_ref,
             dx_ref, db_ref, dc_ref, ddt_ref, dbias_ref, dalog_ref, ddsk_ref, dz_ref, dg_ref, dh_scr):
        first = pl.program_id(1) == 0

        @pl.when(first)
        def _():
            dh_scr[...] = jnp.zeros_like(dh_scr)

        for u in range(GROUPS_PER_STEP):
            xs, ns, ds = _group_slices(u)
            _, pullback = jax.vjp(_ssd_gated_chunk, x_ref[:, xs], b_ref[:, ns], c_ref[:, ns], dt_ref[:, ds], bias_ref[:, ds],
                                  alog_ref[:, ds], dsk_ref[:, ds], hin_ref[u], z_ref[:, xs], g_ref[:, xs])
            g = pullback((ct_ref[:, xs], dh_scr[u]))
            dx_ref[:, xs], db_ref[:, ns], dc_ref[:, ns] = g[0], g[1], g[2]
            ddt_ref[:, ds] = g[3].astype(ddt_ref.dtype)
            dh_scr[u] = g[7]
            dz_ref[:, xs] = g[8].astype(dz_ref.dtype)
            sums = ((dbias_ref, g[4], ds), (dalog_ref, g[5], ds), (ddsk_ref, g[6], ds),
                    (dg_ref, jnp.sum(g[9], axis=0, keepdims=True), xs))
            for o_ref, val, lanes in sums:
                @pl.when(first)
                def _(o_ref=o_ref, val=val, lanes=lanes):
                    o_ref[:, lanes] = val

                @pl.when(jnp.logical_not(first))
                def _(o_ref=o_ref, val=val, lanes=lanes):
                    o_ref[:, lanes] += val

    n_chunks = SEQ // CHUNK
    out_b = pl.BlockSpec((CHUNK, GROUPS_PER_STEP * N_STATE), lambda g, c: (n_chunks - 1 - c, g))
    return pl.pallas_call(
        body, name="ssd_bwd", grid=(SSD_STEPS, n_chunks),
        in_specs=[x_spec, b_spec, c_spec, dt_spec, vec_spec, vec_spec, vec_spec, h_spec, z_spec, g_spec, ct_spec],
        out_specs=[x_spec, out_b, out_b, dt_spec, vec_spec, vec_spec, vec_spec, x_spec, g_spec],
        out_shape=[jax.ShapeDtypeStruct((SEQ, D_SSM), F32), jax.ShapeDtypeStruct((SEQ, N_GROUPS * N_STATE), F32),
                   jax.ShapeDtypeStruct((SEQ, N_GROUPS * N_STATE), F32), jax.ShapeDtypeStruct((SEQ, DT_PAD), BF16),
                   jax.ShapeDtypeStruct((1, DT_PAD), F32), jax.ShapeDtypeStruct((1, DT_PAD), F32),
                   jax.ShapeDtypeStruct((1, DT_PAD), F32), jax.ShapeDtypeStruct((SEQ, D_SSM), BF16),
                   jax.ShapeDtypeStruct((1, D_SSM), F32)],
        scratch_shapes=[pltpu.VMEM((GROUPS_PER_STEP, GROUP_WIDTH, N_STATE), F32)],
        compiler_params=_params(("parallel", "arbitrary")),
    )(xbc, xbc, xbc, dt_raw, bias, alog, dsk, h_in, proj, g_out, dmix)


CROSS_HEAD = 128
CROSS_ROWS = 1024


def _cross_head(q, k, v, gq, gk):
    qn = _rms(q, gq) * (CROSS_HEAD ** -0.5)
    kn = _rms(k, gk)
    s = _bdot(qn, kn, NT)
    p = jnp.exp(s - lax.stop_gradient(jnp.max(s, axis=-1, keepdims=True)))
    return _bdot(p, v, NN) * (1.0 / jnp.sum(p, axis=-1, keepdims=True))


def _cross_specs():
    q_spec = pl.BlockSpec((CROSS_ROWS, CROSS_HEAD), lambda h, i: (i, h))
    k_spec = pl.BlockSpec((N_MEM, CROSS_HEAD), lambda h, i: (0, h))
    v_spec = pl.BlockSpec((N_MEM, CROSS_HEAD), lambda h, i: (0, 4 + h))
    g_spec = pl.BlockSpec((1, CROSS_HEAD), lambda h, i: (0, 0))
    return q_spec, k_spec, v_spec, g_spec


def _cross_fwd(qc, kv, gq, gk):
    q_spec, k_spec, v_spec, g_spec = _cross_specs()

    def body(q_ref, k_ref, v_ref, gq_ref, gk_ref, o_ref):
        o_ref[...] = _cross_head(q_ref[...], k_ref[...], v_ref[...], gq_ref[...], gk_ref[...]).astype(o_ref.dtype)

    return pl.pallas_call(
        body, name="cross_fwd", grid=(4, SEQ // CROSS_ROWS),
        in_specs=[q_spec, k_spec, v_spec, g_spec, g_spec], out_specs=q_spec,
        out_shape=jax.ShapeDtypeStruct((SEQ, D_CROSS), BF16),
        compiler_params=_params(("parallel", "parallel")),
    )(qc, kv, kv, gq, gk)


def _cross_bwd(qc, kv, gq, gk, do):
    q_spec, k_spec, v_spec, g_spec = _cross_specs()

    def body(q_ref, k_ref, v_ref, gq_ref, gk_ref, do_ref, dq_ref, dk_ref, dv_ref, dgq_ref, dgk_ref):
        _, pullback = jax.vjp(_cross_head, q_ref[...], k_ref[...], v_ref[...], gq_ref[...], gk_ref[...])
        dq, dk, dv, dgq, dgk = pullback(do_ref[...].astype(F32))
        dq_ref[...] = dq.astype(dq_ref.dtype)
        row0 = pl.program_id(1) == 0
        all0 = jnp.logical_and(row0, pl.program_id(0) == 0)
        for o_ref, val, init in ((dk_ref, dk, row0), (dv_ref, dv, row0), (dgq_ref, dgq, all0), (dgk_ref, dgk, all0)):
            @pl.when(init)
            def _(o_ref=o_ref, val=val):
                o_ref[...] = val

            @pl.when(jnp.logical_not(init))
            def _(o_ref=o_ref, val=val):
                o_ref[...] += val

    return pl.pallas_call(
        body, name="cross_bwd", grid=(4, SEQ // CROSS_ROWS),
        in_specs=[q_spec, k_spec, v_spec, g_spec, g_spec, q_spec],
        out_specs=[q_spec, k_spec, k_spec, g_spec, g_spec],
        out_shape=[jax.ShapeDtypeStruct((SEQ, D_CROSS), BF16), jax.ShapeDtypeStruct((N_MEM, D_CROSS), F32),
                   jax.ShapeDtypeStruct((N_MEM, D_CROSS), F32), jax.ShapeDtypeStruct((1, CROSS_HEAD), F32),
                   jax.ShapeDtypeStruct((1, CROSS_HEAD), F32)],
        compiler_params=_params(("arbitrary", "arbitrary")),
    )(qc, kv, kv, gq, gk, do)


def _loss_epilogue(acc, residual, target):
    err = acc + residual - target
    dy = err * (1.0 / D_MODEL)
    part = jnp.sum(jnp.sum(err * err, axis=1, keepdims=True), axis=0, keepdims=True) * (0.5 / D_MODEL)
    return dy, dy, part


def _pad_heads(v):
    return jnp.pad(v.reshape(N_GROUPS, HEADS_PER_GROUP), ((0, 0), (0, 128 - HEADS_PER_GROUP))).reshape(1, DT_PAD)


def _unpad_heads(v):
    return v.reshape(v.shape[0], N_GROUPS, 128)[:, :, :HEADS_PER_GROUP].reshape(v.shape[0], N_DT)


def _rope_tables(positions):
    half = ROT // 2
    inv_freq = ROPE_THETA ** (-2.0 * jnp.arange(half, dtype=F32) / ROT)
    ang = positions.reshape(SEQ, 1).astype(F32) * inv_freq
    cos, sin = jnp.cos(ang), jnp.sin(ang)
    ones, zeros = jnp.ones((SEQ, HEAD - ROT), F32), jnp.zeros((SEQ, HEAD - ROT), F32)
    cos_h = jnp.concatenate([cos, cos, ones], axis=1)
    sin_h = jnp.concatenate([-sin, sin, zeros], axis=1)
    return jnp.tile(cos_h, (1, 2)), jnp.tile(sin_h, (1, 2))


def _add_res(acc, res):
    return (acc + res,)


def _norm_bwd_epilogue(acc, x, residual, *more):
    *part, g = more
    ct = acc + part[0] if part else acc
    _, pullback = jax.vjp(_rms, x, g)
    dx, dg = pullback(ct)
    return dx + residual, dg


def _add_res_and_norm(acc, res, g):
    y = acc + res
    return y, _rms(y, g)


def _settle(grads, *after):
    if hasattr(grads, "settle"):
        grads.settle(*after)


def _take_token(grads):
    token = getattr(grads, "token", None)
    if token is None:
        return ()
    grads.token = None
    return (token,)


def _local_step(x, mem, positions, target, p, w, more_weights=None, grads=None, h=None):
    grads = {} if grads is None else grads
    w = dict(w)
    cos, sin = _rope_tables(positions)
    gq2, gk2 = jnp.tile(p["g_q"], (1, 2)), jnp.tile(p["g_k"], (1, 2))
    bias, alog, dsk = _pad_heads(p["dt_bias"]), _pad_heads(p["a_log"]), _pad_heads(p["d_skip"])
    norm_out = [(D_MODEL, BF16, D_MODEL, 0, False)]

    if h is None:
        h = _rowwise(_norm_fn, [_full(x)], [_full(p["g_mix"])], norm_out, name="norm_in")[0]
    proj = _matmul(h, w["w_in"], mode="nn", name="in_proj", outs=[F32], n_cols=D_MAIN)
    dt_raw = _matmul(h, w["w_dt"], mode="nn", name="dt_proj", outs=[F32])
    pairs = D_ATTN // 128
    qk_rows = [(proj, 128, 0, True), (proj, 128, pairs, True), (proj, 128, 2 * pairs, True), _full(cos), _full(sin)]
    qk_vecs = [_full(gq2), _full(gk2)]
    qn, kn, vn = _rowwise(_qk_fn, qk_rows, qk_vecs, [(D_ATTN, F32, 128, 0, True)] * 3, name="qk_prep", groups=8, tr=1024)
    branches = [_attention_fwd(qn, kn, vn, b) for b in range(3)]
    merge_rows = [_full(o) for o, _ in branches] + [_full(lse) for _, lse in branches]
    attn = _rowwise(_merge_fn, merge_rows, [_full(p["g_attn_out"])], [(D_ATTN, BF16, D_ATTN, 0, False)], name="attn_merge")[0]
    xbc = _conv_fwd(proj, p["conv_w"], p["conv_b"])
    ssm, h_in = _ssd_fwd(xbc, dt_raw, bias, alog, dsk, proj, p["g_ssm_out"])
    mix = jnp.concatenate([attn, ssm], axis=1)
    if more_weights is not None:
        w.update(more_weights("mixer_done", mix))
    x1, hc = _matmul(mix, w["w_out"], mode="nn", name="out_proj", outs=[F32, BF16], extra=(x,), vecs=(p["g_cross"],),
                     epilogue=_add_res_and_norm, tm=512, tn=D_MODEL)
    memh = _rowwise(_norm_fn, [_full(mem)], [_full(p["g_mem"])], norm_out, name="norm_mem", n_rows=N_MEM, tr=N_MEM)[0]
    qc = _matmul(hc, w["w_cq"], mode="nn", name="cq_proj", outs=[F32])
    if more_weights is not None:
        w.update(more_weights("cross_started", qc))
    kv = _matmul(memh, w["w_ckv"], mode="nn", name="ckv_proj", outs=[F32])
    oc = _cross_fwd(qc, kv, p["g_cq"], p["g_ck"])
    x2, hm = _matmul(oc, w["w_co"], mode="nn", name="co_proj", outs=[F32, BF16], extra=(x1,), vecs=(p["g_mlp"],),
                     epilogue=_add_res_and_norm, tm=512, tn=D_MODEL)
    if more_weights is not None:
        w.update(more_weights("cross_done", hm))
    u, act = _matmul(hm, w["w_up"], mode="nn", name="up_proj", outs=[F32, BF16],
                     epilogue=lambda acc: (acc, jnp.square(jnp.maximum(acc, 0.0))))
    dy, dyb, loss_tiles = _matmul(act, w["w_down"], mode="nn", name="down_proj", outs=[F32, BF16], extra=(x2, target),
                                  epilogue=_loss_epilogue, tile_sums=1)
    loss = jnp.sum(loss_tiles).reshape(1, 1)

    grads["w_down"] = _matmul(act, dyb, mode="tn", name="dw_down", outs=[BF16], after=_take_token(grads))
    du = _matmul(dyb, w["w_down"], mode="nt", name="d_act", outs=[BF16], extra=(u,), after=_take_token(grads),
                 epilogue=lambda acc, uu: (acc * (2.0 * jnp.maximum(uu, 0.0)),))
    _settle(grads, du)
    grads["w_up"] = _matmul(hm, du, mode="tn", name="dw_up", outs=[BF16], col_shards=4, after=_take_token(grads))
    dx2, grads["g_mlp"] = _matmul(du, w["w_up"], mode="nt", name="d_hm", outs=[F32], extra=(x2, dy), vecs=(p["g_mlp"],),
                                  epilogue=_norm_bwd_epilogue, tile_rows=1, after=_take_token(grads), tm=512, tn=D_MODEL,
                                  tk=1024)
    _settle(grads, dx2)
    grads["w_co"] = _matmul(oc, dx2, mode="tn", name="dw_co", outs=[BF16], col_shards=4, after=_take_token(grads))
    doc = _matmul(dx2, w["w_co"], mode="nt", name="d_oc", outs=[BF16])
    dqc, dkc, dvc, grads["g_cq"], grads["g_ck"] = _cross_bwd(qc, kv, p["g_cq"], p["g_ck"], doc)
    grads["w_cq"] = _matmul(hc, dqc, mode="tn", name="dw_cq", outs=[BF16])
    dkv = jnp.concatenate([dkc, dvc], axis=1)
    grads["w_ckv"] = _matmul(memh, dkv, mode="tn", name="dw_ckv", outs=[BF16])
    dmemh = _matmul(dkv, w["w_ckv"], mode="nt", name="d_memh", outs=[F32])
    grads["g_mem"] = _rowwise_vjp(_norm_fn, [_full(mem)], [_full(p["g_mem"])], [[_full(dmemh)]], [],
                                  [(0, D_MODEL, D_MODEL, 0, False)], name="norm_mem_bwd", n_rows=N_MEM, tr=N_MEM)[0]
    dx1, grads["g_cross"] = _matmul(dqc, w["w_cq"], mode="nt", name="d_hc", outs=[F32], extra=(x1, dx2), vecs=(p["g_cross"],),
                                    epilogue=_norm_bwd_epilogue, tile_rows=1, tm=512, tn=D_MODEL)
    grads["w_out"] = _matmul(mix, dx1, mode="tn", name="dw_out", outs=[BF16])
    dmix = _matmul(dx1, w["w_out"], mode="nt", name="d_mix", outs=[F32], after=_take_token(grads))
    _settle(grads, dmix)
    merge_grads = [(i, D_ATTN, F32, D_ATTN, 0, False, None) for i in range(6)]
    *dol, grads["g_attn_out"] = _rowwise_vjp(
        _merge_fn, merge_rows, [_full(p["g_attn_out"])], [[(dmix, D_ATTN, 0, False)]],
        merge_grads, [(0, D_ATTN, D_ATTN, 0, False)], name="attn_merge_bwd", tr=256, after=_take_token(grads))
    dqkv = [_attention_bwd(qn, kn, vn, *branches[b], dol[b], dol[3 + b], b) for b in range(3)]
    qk_cts = [[(dqkv[b][i], 128, 0, True) for b in range(3)] for i in range(3)]
    dq, dk, dv, dgq2, dgk2 = _rowwise_vjp(
        _qk_fn, qk_rows, qk_vecs, qk_cts, [(i, D_ATTN, BF16, 128, 0, True, None) for i in range(3)],
        [(0, 128, 128, 0, False), (1, 128, 128, 0, False)], name="qk_prep_bwd", groups=8, tr=1024)
    grads["g_q"] = dgq2[:, :HEAD] + dgq2[:, HEAD:]
    grads["g_k"] = dgk2[:, :HEAD] + dgk2[:, HEAD:]
    dxs, db, dc, ddt, dbias, dalog, ddsk, dz, grads["g_ssm_out"] = _ssd_bwd(xbc, dt_raw, bias, alog, dsk, h_in, proj,
                                                                             p["g_ssm_out"], dmix)
    grads["dt_bias"], grads["a_log"], grads["d_skip"] = _unpad_heads(dbias), _unpad_heads(dalog), _unpad_heads(ddsk)
    dxbc_raw, dconv_w, grads["conv_b"] = _conv_bwd(proj, p["conv_w"], p["conv_b"], dxs, db, dc)
    grads["conv_w"] = dconv_w[:4]
    dproj = jnp.concatenate([dq, dk, dv, dz, dxbc_raw], axis=1)
    grads["w_main"] = _matmul(h, dproj, mode="tn", name="dw_main", outs=[BF16], out_cols=D_MAIN + N_DT)
    grads["w_dt"] = _matmul(h, ddt, mode="tn", name="dw_dt", outs=[BF16])
    dh = _matmul(dproj, w["w_in"], mode="nt", name="d_h_main", outs=[F32], after=_take_token(grads))
    grad_x, grads["g_mix"] = _matmul(ddt, w["w_dt"], mode="nt", name="d_h_dt", outs=[F32], extra=(x, dx1, dh),
                                     vecs=(p["g_mix"],), epilogue=_norm_bwd_epilogue, tile_rows=1, tm=512, tn=D_MODEL)
    return loss, grad_x, grads


MATRICES = ("w_in", "w_out", "w_cq", "w_ckv", "w_co", "w_up", "w_down")
ROW_SHARDED = ("w_out", "w_cq", "w_ckv", "w_down")
N_CHIPS = 4
ANY = pl.BlockSpec(memory_space=pl.ANY)


def _place():
    return lax.axis_index("x"), lax.axis_index("y"), lax.axis_index("c")


def _other_chips(x, y):
    return [(1 - x, y), (x, 1 - y), (1 - x, 1 - y)]


def _remote(src, dst, send_sem, recv_sem, device):
    return pltpu.make_async_remote_copy(src_ref=src, dst_ref=dst, send_sem=send_sem, recv_sem=recv_sem,
                                        device_id=device, device_id_type=MESH)


def _gathered_shape(name, shard):
    rows, cols = shard.shape
    if name == "w_in":
        return (N_CHIPS, rows, cols)
    return (N_CHIPS * rows, cols) if name in ROW_SHARDED else (rows, N_CHIPS * cols)


def _shard_window(name, ref, rows, cols, chip, half):
    r0, nr = (0, rows) if half is None else (half * (rows // 2), rows // 2)
    if name == "w_in":
        return ref.at[chip, pl.ds(r0, nr), :]
    if name in ROW_SHARDED:
        return ref.at[pl.ds(chip * rows + r0, nr), :]
    return ref.at[pl.ds(r0, nr), pl.ds(pl.multiple_of(chip * cols, 128), cols)]


def _cast_into_gathered(w, name, chip, after=()):
    rows, cols = w.shape
    tr = _tile(rows, ROW_TILE)

    def body(chip_ref, w_ref, *rest):
        rest[-1][...] = w_ref[...].astype(BF16)

    if name == "w_in":
        out_spec = pl.BlockSpec((None, tr, cols), lambda i, chip_ref: (chip_ref[0], i, 0))
    elif name in ROW_SHARDED:
        out_spec = pl.BlockSpec((tr, cols), lambda i, chip_ref: (chip_ref[0] * (rows // tr) + i, 0))
    else:
        out_spec = pl.BlockSpec((tr, cols), lambda i, chip_ref: (i, chip_ref[0]))
    grid_spec = pltpu.PrefetchScalarGridSpec(
        num_scalar_prefetch=1, grid=(rows // tr,),
        in_specs=[pl.BlockSpec((tr, cols), lambda i, chip_ref: (i, 0))] + [pl.BlockSpec(memory_space=pl.ANY)] * len(after),
        out_specs=out_spec)
    return pl.pallas_call(body, name="cast_" + name, grid_spec=grid_spec,
                          out_shape=jax.ShapeDtypeStruct(_gathered_shape(name, w), BF16),
                          compiler_params=_params(("parallel",)))(chip.reshape(1).astype(jnp.int32), w, *after)


def _w_in_columns(arr, to_shards):
    rows, piece = D_MODEL, (D_MAIN + N_DT) // N_CHIPS
    tr = ROW_TILE

    def body(a_ref, o_ref):
        for j in range(N_CHIPS):
            if to_shards:
                o_ref[j] = a_ref[:, pl.ds(piece * j, piece)]
            else:
                o_ref[:, pl.ds(piece * j, piece)] = a_ref[j]

    pieces = pl.BlockSpec((N_CHIPS, tr, piece), lambda i: (0, i, 0))
    matrix = pl.BlockSpec((tr, N_CHIPS * piece), lambda i: (i, 0))
    out_dims = (N_CHIPS, rows, piece) if to_shards else (rows, N_CHIPS * piece)
    return pl.pallas_call(
        body, name="w_in_to_shards" if to_shards else "w_in_from_shards", grid=(rows // tr,),
        in_specs=[matrix if to_shards else pieces], out_specs=pieces if to_shards else matrix,
        out_shape=jax.ShapeDtypeStruct(out_dims, arr.dtype), compiler_params=_params(("parallel",)))(arr)


HBM = pl.BlockSpec(memory_space=pltpu.HBM)
SEM = pl.BlockSpec(memory_space=pltpu.SEMAPHORE)
EFFECT = pltpu.SideEffectType.DATAFLOW_SIDE_EFFECTING


def _split_start(name, bufs, plan, counts, after=()):
    n, n_g, n_after = len(bufs), len(counts), len(after)

    def body(*refs):
        ins, sems, token = refs[:n], refs[n + n_after:n + n_after + 2 * n_g], refs[-1]
        for g, copies in enumerate(plan(ins)):
            for i, (src, dst, device, _) in enumerate(copies):
                _remote(src, dst, sems[2 * g].at[i], sems[2 * g + 1].at[i], device).start()
        token[...] = jnp.zeros_like(token)

    sem_shapes = [pltpu.SemaphoreType.DMA((cnt,)) for cnt in counts for _ in range(2)]
    res = pl.pallas_call(
        body, name=name,
        out_shape=(*sem_shapes, *[pltpu.HBM(b.shape, b.dtype) for b in bufs], jax.ShapeDtypeStruct((8, 128), F32)),
        in_specs=(*(HBM,) * n, *(ANY,) * n_after),
        out_specs=(*(SEM,) * (2 * n_g), *(HBM,) * n, pl.BlockSpec(memory_space=pltpu.VMEM)),
        input_output_aliases={i: 2 * n_g + i for i in range(n)},
        compiler_params=pltpu.CompilerParams(has_side_effects=EFFECT),
    )(*[pltpu.with_memory_space_constraint(b, pltpu.HBM) for b in bufs], *after)
    sems = [(res[2 * g], res[2 * g + 1]) for g in range(n_g)]
    return sems, list(res[2 * n_g:2 * n_g + n]), res[-1]


def _split_wait(name, bufs, sems, plan, *after):
    n = len(bufs)

    def body(*refs):
        ins, send, recv = refs[:n], refs[n], refs[n + 1]
        (copies,) = plan(ins)
        for i, (src, _, device, landing) in enumerate(copies):
            cp = _remote(src, landing, send.at[i], recv.at[i], device)
            cp.wait_send()
            cp.wait_recv()

    res = pl.pallas_call(
        body, name=name, out_shape=tuple(pltpu.HBM(b.shape, b.dtype) for b in bufs),
        in_specs=(*(HBM,) * n, SEM, SEM, *(ANY,) * len(after)), out_specs=(HBM,) * n,
        input_output_aliases={i: i for i in range(n)},
        compiler_params=pltpu.CompilerParams(has_side_effects=EFFECT),
    )(*bufs, sems[0], sems[1], *after)
    return list(res)


def _ici_plan(names, shard_shapes):
    def plan(refs):
        x, y, c = _place()
        copies = []
        for ref, name in zip(refs, names):
            win = _shard_window(name, ref, *shard_shapes[name], 2 * x + y, c)
            for px, py in _other_chips(x, y):
                copies.append((win, win, (px, py, c), _shard_window(name, ref, *shard_shapes[name], 2 * px + py, c)))
        return [copies]
    return plan


def _pass_on_plan(names, shard_shapes):
    def plan(refs):
        x, y, c = _place()
        copies = []
        for ref, name in zip(refs, names):
            for px, py in _other_chips(x, y):
                win = _shard_window(name, ref, *shard_shapes[name], 2 * px + py, c)
                copies.append((win, win, (x, y, 1 - c), _shard_window(name, ref, *shard_shapes[name], 2 * px + py, 1 - c)))
        return [copies]
    return plan


def _swap_plan(n_pairs):
    def plan(refs):
        x, y, c = _place()
        return [[(src.at[:, 1 - c], dst, (x, y, 1 - c), dst) for src, dst in zip(refs[:n_pairs], refs[n_pairs:])]]
    return plan


def _share_plan(n_pairs):
    def plan(refs):
        x, y, c = _place()
        return [[(src, dst, (x, y, 1 - c), dst)] for src, dst in zip(refs[:n_pairs], refs[n_pairs:])]
    return plan


def _scatter_plan(n_pairs):
    def plan(refs):
        x, y, c = _place()
        copies = []
        for src, dst in zip(refs[:n_pairs], refs[n_pairs:]):
            for k, (px, py) in enumerate(_other_chips(x, y)):
                copies.append((src.at[2 * px + py], dst.at[k], (px, py, c), dst.at[k]))
        return [copies]
    return plan


def _sibling_swap(arrs, name):
    n = len(arrs)

    def body(*refs):
        ins, outs, send, recv = refs[:n], refs[n:2 * n], refs[2 * n], refs[2 * n + 1]
        x, y, c = _place()
        cps = [_remote(ins[w].at[:, 1 - c], outs[w], send.at[w], recv.at[w], (x, y, 1 - c)) for w in range(n)]
        for cp in cps:
            cp.start()
        for cp in cps:
            cp.wait()

    return pl.pallas_call(
        body, name=name, in_specs=[ANY] * n, out_specs=[ANY] * n,
        out_shape=[jax.ShapeDtypeStruct((a.shape[0],) + a.shape[2:], a.dtype) for a in arrs],
        scratch_shapes=[pltpu.SemaphoreType.DMA((n,))] * 2,
    )(*arrs)


def _small_allreduce(buf, name, after=()):
    rows = buf.shape[0]

    def body(x_ref, *rest):
        out_ref, all_ref, send_sems, recv_sems, local_sem = rest[len(after):]
        x, y, c = _place()
        me, sibling, chips = (x, y, c), (x, y, 1 - c), _other_chips(x, y)

        def block(px, py, pc):
            return all_ref.at[pl.ds((4 * px + 2 * py + pc) * rows, rows), :]

        def copy(k, blk, to, src=None):
            return _remote(block(*blk) if src is None else src, block(*blk), send_sems.at[k], recv_sems.at[k], to)

        own = pltpu.make_async_copy(x_ref, block(*me), local_sem)
        own.start()
        first = [copy(0, me, sibling, src=x_ref)] + [copy(1 + j, me, (*chip, c), src=x_ref) for j, chip in enumerate(chips)]
        for cp in first:
            cp.start()
        passed = [copy(4 + j, (*chip, c), sibling) for j, chip in enumerate(chips)]
        for j, chip in enumerate(chips):
            copy(1 + j, (*chip, c), me).wait_recv()
            passed[j].start()
        copy(0, sibling, me).wait_recv()
        for j, chip in enumerate(chips):
            copy(4 + j, (*chip, 1 - c), me).wait_recv()
        for cp in first + passed:
            cp.wait_send()
        own.wait()
        acc = all_ref[pl.ds(0, rows), :]
        for d in range(1, 8):
            acc = acc + all_ref[pl.ds(d * rows, rows), :]
        out_ref[...] = acc

    vmem = pl.BlockSpec(memory_space=pltpu.VMEM)
    return pl.pallas_call(
        body, name=name, in_specs=[vmem] + [ANY] * len(after), out_specs=vmem,
        out_shape=jax.ShapeDtypeStruct(buf.shape, F32),
        scratch_shapes=[pltpu.VMEM((8 * rows, 128), F32), pltpu.SemaphoreType.DMA((7,)), pltpu.SemaphoreType.DMA((7,)),
                        pltpu.SemaphoreType.DMA],
    )(buf, *after)


ROW_TILE = 256
BIG_ROW_TILE = 1024


def _add_halves(arr, recv, c, name):
    _, _, hr, cols = arr.shape
    tr = _tile(hr, BIG_ROW_TILE)

    def body(c_ref, a_ref, r_ref, o_ref):
        o_ref[...] = (a_ref[...].astype(F32) + r_ref[...].astype(F32)).astype(o_ref.dtype)

    piece = pl.BlockSpec((None, tr, cols), lambda j, i, c_ref: (j, i, 0))
    grid_spec = pltpu.PrefetchScalarGridSpec(
        num_scalar_prefetch=1, grid=(N_CHIPS, hr // tr),
        in_specs=[pl.BlockSpec((None, None, tr, cols), lambda j, i, c_ref: (j, c_ref[0], i, 0)), piece], out_specs=piece)
    return pl.pallas_call(body, name=name, grid_spec=grid_spec, out_shape=jax.ShapeDtypeStruct(recv.shape, BF16),
                          compiler_params=_params(("parallel", "parallel")))(c.reshape(1).astype(jnp.int32), arr, recv)


def _flip_slot(d):
    return jnp.where(d == 1, 1, jnp.where(d == 3, 2, 0))


def _sum_chips(p, q, chip, name):
    _, hr, cols = p.shape
    tr = _tile(hr, BIG_ROW_TILE)

    def body(chip_ref, p_ref, q_ref, o_ref):
        j = pl.program_id(1)
        term = jnp.where(j == chip_ref[0], p_ref[...].astype(F32), q_ref[...].astype(F32))

        @pl.when(j == 0)
        def _():
            o_ref[...] = term

        @pl.when(j != 0)
        def _():
            o_ref[...] += term

    grid_spec = pltpu.PrefetchScalarGridSpec(
        num_scalar_prefetch=1, grid=(hr // tr, N_CHIPS),
        in_specs=[pl.BlockSpec((None, tr, cols), lambda i, j, chip_ref: (chip_ref[0], i, 0)),
                  pl.BlockSpec((None, tr, cols), lambda i, j, chip_ref: (_flip_slot(j ^ chip_ref[0]), i, 0))],
        out_specs=pl.BlockSpec((tr, cols), lambda i, j, chip_ref: (i, 0)))
    return pl.pallas_call(body, name=name, grid_spec=grid_spec, out_shape=jax.ShapeDtypeStruct((hr, cols), F32),
                          compiler_params=_params(("parallel", "arbitrary")))(chip.reshape(1).astype(jnp.int32), p, q)


def _adamw_halves(w, g_own, g_other, m, v, c, name):
    rows, cols = w.shape
    tr = _tile(rows // 2, ROW_TILE)
    per_half = rows // 2 // tr

    def body(c_ref, w_ref, own_ref, other_ref, m_ref, v_ref, g_ref, d_ref, nm_ref, nv_ref):
        mine = (pl.program_id(0) // per_half) == c_ref[0]
        g_ = jnp.where(mine, own_ref[...], other_ref[...])
        g_ref[...] = g_
        d_ref[...], nm_ref[...], nv_ref[...] = _adamw_math(w_ref[...], g_, m_ref[...], v_ref[...])

    blk = pl.BlockSpec((tr, cols), lambda i, c_ref: (i, 0))
    own = pl.BlockSpec((tr, cols), lambda i, c_ref: (jnp.where(i // per_half == c_ref[0], i % per_half, 0), 0))
    other = pl.BlockSpec((tr, cols), lambda i, c_ref: (jnp.where(i // per_half == c_ref[0], 0, i % per_half), 0))
    grid_spec = pltpu.PrefetchScalarGridSpec(num_scalar_prefetch=1, grid=(rows // tr,),
                                             in_specs=[blk, own, other, blk, blk], out_specs=[blk] * 4)
    return pl.pallas_call(body, name=name, grid_spec=grid_spec, out_shape=[jax.ShapeDtypeStruct(w.shape, F32)] * 4,
                          compiler_params=_params(("parallel",)))(c.reshape(1).astype(jnp.int32), w, g_own, g_other, m, v)


W_IN_COLS = (D_MAIN + N_DT) // N_CHIPS
W_IN_MAIN = W_IN_COLS // 128 * 128
W_IN_TAIL = W_IN_COLS - 128
W_IN_PARTS = ((0, W_IN_MAIN), (W_IN_TAIL, 128))


def _cast_w_in_transposed(w_t, chip, after=()):
    def body(chip_ref, w_ref, *rest):
        for start, size in W_IN_PARTS:
            rest[-1][:, pl.ds(start, size)] = w_ref[pl.ds(start, size), :].T.astype(BF16)

    grid_spec = pltpu.PrefetchScalarGridSpec(
        num_scalar_prefetch=1, grid=(D_MODEL // ROW_TILE,),
        in_specs=[pl.BlockSpec((W_IN_COLS, ROW_TILE), lambda i, chip_ref: (0, i))] + [pl.BlockSpec(memory_space=pl.ANY)] * len(after),
        out_specs=pl.BlockSpec((None, ROW_TILE, W_IN_COLS), lambda i, chip_ref: (chip_ref[0], i, 0)))
    return pl.pallas_call(body, name="cast_w_in", grid_spec=grid_spec,
                          out_shape=jax.ShapeDtypeStruct((N_CHIPS, D_MODEL, W_IN_COLS), BF16),
                          compiler_params=_params(("parallel",)))(chip.reshape(1).astype(jnp.int32), w_t, *after)


def _adamw_w_in_transposed(w_t, g_own, g_other, m_t, v_t, c):
    per_half = D_MODEL // 2 // ROW_TILE

    def body(c_ref, w_ref, own_ref, other_ref, m_ref, v_ref, g_ref, d_ref, nm_ref, nv_ref):
        mine = (pl.program_id(0) // per_half) == c_ref[0]
        for start, size in W_IN_PARTS:
            cols, rows = pl.ds(start, size), pl.ds(start, size)
            g_ = jnp.where(mine, own_ref[:, cols], other_ref[:, cols]).T
            g_ref[rows, :] = g_
            d_ref[rows, :], nm_ref[rows, :], nv_ref[rows, :] = _adamw_math(w_ref[rows, :], g_, m_ref[rows, :], v_ref[rows, :])

    blk = pl.BlockSpec((W_IN_COLS, ROW_TILE), lambda i, c_ref: (0, i))
    own = pl.BlockSpec((ROW_TILE, W_IN_COLS), lambda i, c_ref: (jnp.where(i // per_half == c_ref[0], i % per_half, 0), 0))
    other = pl.BlockSpec((ROW_TILE, W_IN_COLS), lambda i, c_ref: (jnp.where(i // per_half == c_ref[0], 0, i % per_half), 0))
    grid_spec = pltpu.PrefetchScalarGridSpec(num_scalar_prefetch=1, grid=(D_MODEL // ROW_TILE,),
                                             in_specs=[blk, own, other, blk, blk], out_specs=[blk] * 4)
    return pl.pallas_call(body, name="adamw_w_in", grid_spec=grid_spec, out_shape=[jax.ShapeDtypeStruct(w_t.shape, F32)] * 4,
                          compiler_params=_params(("parallel",)))(c.reshape(1).astype(jnp.int32), w_t, g_own, g_other, m_t, v_t)


def _adamw_math(w, g, m, v):
    m_new = ADAM_B1 * m + (1.0 - ADAM_B1) * g
    v_new = ADAM_B2 * v + (1.0 - ADAM_B2) * (g * g)
    m_hat = m_new / (1.0 - ADAM_B1 ** ADAM_STEP)
    v_hat = v_new / (1.0 - ADAM_B2 ** ADAM_STEP)
    return -ADAM_LR * (m_hat / (jnp.sqrt(v_hat) + ADAM_EPS) + ADAM_WD * w), m_new, v_new


VECTORS = ("g_mix", "g_q", "g_k", "g_attn_out", "conv_b", "dt_bias", "a_log", "d_skip", "g_ssm_out", "g_cross", "g_mem",
           "g_cq", "g_ck", "g_mlp")
WEIGHTS = ("g_mix", "w_in", "g_q", "g_k", "g_attn_out", "conv_w", "conv_b", "dt_bias", "a_log", "d_skip", "g_ssm_out", "w_out",
           "g_cross", "g_mem", "w_cq", "w_ckv", "g_cq", "g_ck", "w_co", "g_mlp", "w_up", "w_down")


def _pack(parts):
    flat = jnp.concatenate([t.reshape(-1) for t in parts])
    total = -(-flat.shape[0] // 1024) * 1024
    return jnp.pad(flat, (0, total - flat.shape[0])).reshape(total // 128, 128)


def _rows_of(n):
    return -(-n // 128)


def _slot_rows(n):
    return -(-n // 1024) * 8


def _pack_rows(parts):
    rows = []
    for t in parts:
        flat = t.reshape(-1)
        rows.append(jnp.pad(flat, (0, 128 * _slot_rows(flat.shape[0]) - flat.shape[0])).reshape(-1, 128))
    return jnp.concatenate(rows)


def _adamw_vectors(summed, chip, vectors, conv):
    groups = list(vectors) + [conv]
    offsets, row = [], 0
    for w, _, _ in groups:
        offsets.append(row)
        row += _slot_rows(w.shape[1]) if w.shape[0] == 1 else _slot_rows(w.shape[0] * N_CHIPS * w.shape[1])
    conv_blocks = _rows_of(conv[0].shape[1])

    def body(chip_ref, sum_ref, *refs):
        ins, outs = refs[:3 * len(groups)], refs[3 * len(groups):]

        def update(i, g, idx):
            w_ref, m_ref, v_ref = ins[3 * i:3 * i + 3]
            delta, new_m, new_v = _adamw_math(w_ref[idx], g, m_ref[idx], v_ref[idx])
            for o_ref, val in zip(outs[4 * i:4 * i + 4], (g, delta, new_m, new_v)):
                o_ref[idx] = val

        for i, (w, _, _) in enumerate(vectors):
            for t in range(_rows_of(w.shape[1])):
                width = min(128, w.shape[1] - 128 * t)
                update(i, sum_ref[pl.ds(offsets[i] + t, 1), pl.ds(0, width)], (slice(None), pl.ds(128 * t, width)))
        for tap in range(conv[0].shape[0]):
            for blk in range(conv_blocks):
                src = offsets[-1] + tap * N_CHIPS * conv_blocks + chip_ref[0] * conv_blocks + blk
                update(len(vectors), sum_ref[pl.ds(src, 1), :], (pl.ds(tap, 1), pl.ds(128 * blk, 128)))

    def whole(a):
        return pl.BlockSpec(a.shape, lambda i, chip_ref: (0,) * a.ndim)

    operands = [t for group in groups for t in group]
    grid_spec = pltpu.PrefetchScalarGridSpec(
        num_scalar_prefetch=1, grid=(1,), in_specs=[whole(summed)] + [whole(t) for t in operands],
        out_specs=[whole(w) for w, _, _ in groups for _ in range(4)])
    res = pl.pallas_call(body, name="adamw_vectors", grid_spec=grid_spec,
                         out_shape=[jax.ShapeDtypeStruct(w.shape, F32) for w, _, _ in groups for _ in range(4)],
                         compiler_params=_params(("arbitrary",)))(chip.reshape(1).astype(jnp.int32), summed, *operands)
    return [res[4 * i:4 * i + 4] for i in range(len(groups))]


def _unpack(buf, shapes):
    flat, out, pos = buf.reshape(-1), [], 0
    for shape in shapes:
        size = math.prod(shape)
        out.append(flat[pos:pos + size].reshape(shape))
        pos += size
    return out


def kernel(x, mem, positions, g_mix, w_in, g_q, g_k, g_attn_out, conv_w, conv_b, dt_bias, a_log, d_skip, g_ssm_out, w_out, g_cross, g_mem, w_cq, w_ckv, g_cq, g_ck, w_co, g_mlp, w_up, w_down, loss_target, m_g_mix, m_w_in, m_g_q, m_g_k, m_g_attn_out, m_conv_w, m_conv_b, m_dt_bias, m_a_log, m_d_skip, m_g_ssm_out, m_w_out, m_g_cross, m_g_mem, m_w_cq, m_w_ckv, m_g_cq, m_g_ck, m_w_co, m_g_mlp, m_w_up, m_w_down, v_g_mix, v_w_in, v_g_q, v_g_k, v_g_attn_out, v_conv_w, v_conv_b, v_dt_bias, v_a_log, v_d_skip, v_g_ssm_out, v_w_out, v_g_cross, v_g_mem, v_w_cq, v_w_ckv, v_g_cq, v_g_ck, v_w_co, v_g_mlp, v_w_up, v_w_down):
    args = dict(locals())
    weights = {n: args[n][0] for n in WEIGHTS}
    mom_m = {n: args["m_" + n][0] for n in WEIGHTS}
    mom_v = {n: args["v_" + n][0] for n in WEIGHTS}
    x_idx, y_idx, c_idx = _place()
    chip = 2 * x_idx + y_idx

    shapes = {n: weights[n].shape for n in MATRICES}
    first, mid, late = ("w_in",), ("w_out", "w_cq", "w_ckv", "w_co"), ("w_up", "w_down")
    w_in_t, m_in_t, v_in_t = (jnp.swapaxes(t, 1, 2)[0] for t in (w_in, m_w_in, v_w_in))
    w_in_buf = [_cast_w_in_transposed(w_in_t, chip)]
    taps, tap_cols = weights["conv_w"].shape
    conv_parts = _small_allreduce(_pack([jnp.zeros((N_CHIPS, taps, tap_cols), F32).at[chip].set(0.5 * weights["conv_w"])]),
                                  "gather_conv_taps")
    sems_in, w_in_buf, token = _split_start("gather_ici_start_w_in", w_in_buf, _ici_plan(first, shapes), [3], after=(conv_parts,))
    bufs = [_cast_into_gathered(weights[n], n, chip, after=(token,)) for n in mid + late]
    plan = lambda refs: _ici_plan(mid, shapes)(refs[:4]) + _ici_plan(late, shapes)(refs[4:])
    sems_rest, bufs, token = _split_start("gather_ici_start_rest", bufs, plan, [12, 6], after=(token,))
    params = {n: weights[n].reshape(1, -1) for n in VECTORS}
    h_in = _rowwise(_norm_fn, [_full(x[0])], [_full(params["g_mix"])], [(D_MODEL, BF16, D_MODEL, 0, False)], name="norm_in",
                    after=(token,))[0]
    w_in_buf = _split_wait("gather_ici_wait_w_in", w_in_buf, sems_in[0], _ici_plan(first, shapes), token, h_in, m_in_t, v_in_t)
    pass_sems, w_in_buf, token = _split_start("gather_pass_start_w_in", w_in_buf, _pass_on_plan(first, shapes), [3])
    w_in_buf = _split_wait("gather_pass_wait_w_in", w_in_buf, pass_sems[0], _pass_on_plan(first, shapes), token)
    w_in_full = _w_in_columns(w_in_buf[0], to_shards=False)
    full = {"w_in": w_in_full,
            "w_dt": jnp.pad(w_in_full[:, D_MAIN:].reshape(D_MODEL, N_GROUPS, HEADS_PER_GROUP),
                            ((0, 0), (0, 0), (0, 128 - HEADS_PER_GROUP))).reshape(D_MODEL, DT_PAD)}
    in_flight = {}

    def more_weights(stage, after):
        if stage == "mixer_done":
            got = _split_wait("gather_ici_wait_mid", bufs[:4], sems_rest[0], _ici_plan(mid, shapes), after)
            sems, got, token = _split_start("gather_pass_start_mid", got, _pass_on_plan(mid, shapes), [12])
            return dict(zip(mid, _split_wait("gather_pass_wait_mid", got, sems[0], _pass_on_plan(mid, shapes), token)))
        if stage == "cross_started":
            got = _split_wait("gather_ici_wait_late", bufs[4:], sems_rest[1], _ici_plan(late, shapes), after)
            in_flight["late"] = _split_start("gather_pass_start_late", got, _pass_on_plan(late, shapes), [6])
            return {}
        sems, got, token = in_flight.pop("late")
        return dict(zip(late, _split_wait("gather_pass_wait_late", got, sems[0], _pass_on_plan(late, shapes), token, after)))

    params["conv_w"] = _unpack(conv_parts, [(N_CHIPS, taps, tap_cols)])[0].transpose(1, 0, 2).reshape(taps, N_CHIPS * tap_cols)

    groups = (("w_down",), ("w_up",), ("w_co", "w_cq", "w_ckv", "w_out"), ("w_in",))
    scattered = []

    class GradStore(dict):
        pending = None

        def __setitem__(self, name, value):
            super().__setitem__(name, value)
            if "w_main" in self and "w_dt" in self and "w_in" not in self:
                gw_in = lax.dynamic_update_slice(self["w_main"], _unpad_heads(self["w_dt"]), (0, D_MAIN))
                self["w_in"] = _w_in_columns(gw_in, to_shards=True)
            for group in groups:
                if name in group and all(n in self for n in group):
                    self.settle()
                    pieces = [self[n].reshape(N_CHIPS, 2, shapes[n][0] // 2, shapes[n][1]) for n in group]
                    if group == groups[-1]:
                        self.scatter(group, pieces, _sibling_swap(pieces, "grad_swap_" + group[0]))
                    else:
                        landing = [lax.empty((N_CHIPS,) + a.shape[2:], BF16) for a in pieces]
                        sems, thru, self.token = _split_start("grad_swap_start_" + group[0], pieces + landing,
                                                              _swap_plan(len(pieces)), [len(pieces)])
                        self.pending = (group, sems[0], thru)

        def settle(self, *after):
            if self.pending is not None:
                group, sems, thru = self.pending
                self.pending = None
                thru = _split_wait("grad_swap_wait_" + group[0], thru, sems, _swap_plan(len(group)), *after)
                self.scatter(group, thru[:len(group)], thru[len(group):])

        def scatter(self, group, pieces, from_sibling):
            sums = [_add_halves(a, r, c_idx, "add_halves_" + n) for n, a, r in zip(group, pieces, from_sibling)]
            landing = [lax.empty((3,) + s.shape[1:], BF16) for s in sums]
            sems, thru, self.token = _split_start("grad_scatter_start_" + group[0], sums + landing,
                                                  _scatter_plan(len(sums)), [3 * len(sums)])
            scattered.append((group, sems[0], thru))

    loss, grad_x, grads = _local_step(x[0], mem[0], positions[0], loss_target[0], params, full, more_weights, GradStore(),
                                      h_in)

    out_g, out_d, out_m, out_v = {}, {}, {}, {}

    def finish(entries, order, token):
        halves = {}
        for group, sems, thru in entries:
            thru = _split_wait("grad_scatter_wait_" + group[0], thru, sems, _scatter_plan(len(group)), token)
            for i, n in enumerate(group):
                halves[n] = _sum_chips(thru[i], thru[len(group) + i], chip, "sum_chips_" + n)
        sources = [halves[n] for n in order]
        landing = [lax.empty(s.shape, F32) for s in sources]
        sems, thru, token = _split_start("grad_share_start_" + order[0], sources + landing, _share_plan(len(order)),
                                         [1] * len(order))
        for i, n in enumerate(order):
            own, other = _split_wait("grad_share_wait_" + n, [thru[i], thru[len(order) + i]], sems[i], _share_plan(1), token)
            if n == "w_in":
                res_t = _adamw_w_in_transposed(w_in_t, own, other, m_in_t, v_in_t, c_idx)
                out_g[n], out_d[n], out_m[n], out_v[n] = (t.T for t in res_t)
            else:
                out_g[n], out_d[n], out_m[n], out_v[n] = _adamw_halves(weights[n], own, other, mom_m[n], mom_v[n], c_idx,
                                                                       "adamw_" + n)
            token = out_v[n]
        return token

    token = finish(scattered[:-1], ("w_cq", "w_co", "w_ckv", "w_out", "w_up", "w_down"), grad_x)
    finish(scattered[-1:], ("w_in",), token)

    names = VECTORS + ("conv_w",)
    summed = _small_allreduce(_pack_rows([grads[n] for n in names] + [loss]), "allreduce_vectors")
    total_loss = summed[sum(_slot_rows(grads[n].size) for n in names), 0]
    small_out = _adamw_vectors(summed, chip, [(args[n], args["m_" + n], args["v_" + n]) for n in VECTORS],
                               (weights["conv_w"], mom_m["conv_w"], mom_v["conv_w"]))
    for n, res in zip(names, small_out):
        out_g[n], out_d[n], out_m[n], out_v[n] = (t.reshape(weights[n].shape) for t in res)

    outs =[total_loss, grad_x[None]]
    for group in (out_g, out_d, out_m, out_v):
        outs += [group[n][None] for n in WEIGHTS]
    return tuple(outs)
```

```python
import functools
import math

import jax
import jax.numpy as jnp
from jax import lax
from jax.experimental import pallas as pl
from jax.experimental.pallas import tpu as pltpu

F32 = jnp.float32
BF16 = jnp.bfloat16

SEQ = 2048
D_MODEL = 2048
HEAD = 64
D_ATTN = 1024
D_SSM = 1024
N_GROUPS = 4
N_STATE = 128
CHUNK = 128
ATT_BLK = 128
N_MEM = 256
D_CROSS = 512
D_MAIN = 6144
N_DT = 16
DT_PAD = 512
ROT = 16
ROPE_THETA = 500000.0
EPS = 1e-6
NEG = -1e30
BRANCH_BLOCKS = (16, 4, 1)
DILATIONS = (1, 4, 16)

ADAM_LR, ADAM_B1, ADAM_B2, ADAM_EPS, ADAM_WD, ADAM_STEP = 0.001, 0.9, 0.999, 1e-08, 0.01, 10

VMEM_LIMIT = 56 * 1024 * 1024
MESH = pl.DeviceIdType.MESH


def _params(sem, **kw):
    return pltpu.CompilerParams(dimension_semantics=sem, vmem_limit_bytes=VMEM_LIMIT, **kw)


def _bdot(a, b, dims):
    return lax.dot_general(a.astype(BF16), b.astype(BF16), (dims, ((), ())), preferred_element_type=F32)


def _fdot(a, b, dims):
    return lax.dot_general(a, b, (dims, ((), ())), preferred_element_type=F32, precision=lax.Precision.HIGHEST)


NN = ((1,), (0,))
NT = ((1,), (1,))
TN = ((0,), (0,))


def _tile(n, want):
    t = min(n, want)
    while n % t:
        t //= 2
    return t


def _matmul(a, b, *, mode, name, outs, extra=(), vecs=(), epilogue=None, col_shards=1, after=(), n_cols=None, out_cols=None,
            tile_rows=0, tile_sums=0, tm=1024, tn=1024, tk=2048):
    if mode == "nn":
        (m, k), n = a.shape, b.shape[1]
    elif mode == "nt":
        (m, k), n = a.shape, b.shape[0]
    else:
        (k, m), n = a.shape, b.shape[1]
    n = n if n_cols is None else n_cols
    tm, tn, tk = _tile(m, tm), _tile(n // col_shards, tn), _tile(k, tk)
    nk = k // tk
    per_shard = n // col_shards // tn
    dims = {"nn": NN, "nt": NT, "tn": TN}[mode]
    a_spec = pl.BlockSpec((tk, tm), lambda i, j, kk: (kk, i)) if mode == "tn" else pl.BlockSpec((tm, tk), lambda i, j, kk: (i, kk))
    b_spec = pl.BlockSpec((tn, tk), lambda i, j, kk: (j, kk)) if mode == "nt" else pl.BlockSpec((tk, tn), lambda i, j, kk: (kk, j))
    o_spec = pl.BlockSpec((tm, tn), lambda i, j, kk: (i, j))
    n_extra, n_out, n_after = len(extra) + len(vecs), len(outs), len(after)

    def body(a_ref, b_ref, *rest):
        extra_refs, out_refs, acc_ref = rest[:n_extra], rest[n_extra + n_after:-1], rest[-1]

        def finish(acc):
            res = (acc,) if epilogue is None else epilogue(acc, *[e[...] for e in extra_refs])
            for o_ref, r in zip(out_refs[:n_out], res):
                o_ref[...] = r.astype(o_ref.dtype)
            for o_ref, r in zip(out_refs[n_out:], res[n_out:]):
                o_ref[...] = jnp.broadcast_to(r, o_ref.shape)

        if nk == 1:
            finish(_bdot(a_ref[...], b_ref[...], dims))
            return
        kk = pl.program_id(2)

        @pl.when(kk == 0)
        def _():
            acc_ref[...] = jnp.zeros_like(acc_ref)

        acc_ref[...] += _bdot(a_ref[...], b_ref[...], dims)

        @pl.when(kk == nk - 1)
        def _():
            finish(acc_ref[...])

    if col_shards == 1:
        out_specs, out_dims = [o_spec] * n_out, (m, n if out_cols is None else out_cols)
    else:
        sharded = pl.BlockSpec((None, tm, tn), lambda i, j, kk: (j // per_shard, i, j % per_shard))
        out_specs, out_dims = [sharded] * n_out, (col_shards, m, n // col_shards)
    res = pl.pallas_call(
        body, name=name, grid=(m // tm, n // tn, nk),
        in_specs=[a_spec, b_spec] + [o_spec] * len(extra) + [pl.BlockSpec((1, tn), lambda i, j, kk: (0, j))] * len(vecs)
        + [pl.BlockSpec(memory_space=pl.ANY)] * n_after,
        out_specs=out_specs + [pl.BlockSpec((8, tn), lambda i, j, kk: (i, j))] * tile_rows
        + [pl.BlockSpec((8, 128), lambda i, j, kk: (i, j))] * tile_sums,
        out_shape=[jax.ShapeDtypeStruct(out_dims, dt) for dt in outs] + [jax.ShapeDtypeStruct((m // tm * 8, n), F32)] * tile_rows
        + [jax.ShapeDtypeStruct((m // tm * 8, n // tn * 128), F32)] * tile_sums,
        scratch_shapes=[pltpu.VMEM((tm, tn) if nk > 1 else (8, 128), F32)],
        compiler_params=_params(("parallel", "parallel", "arbitrary")),
    )(a, b, *extra, *vecs, *after)
    res = (list(res[:n_out]) + [jnp.sum(t[::8], axis=0, keepdims=True) for t in res[n_out:n_out + tile_rows]]
           + [t[::8, ::128] for t in res[n_out + tile_rows:]])
    return res[0] if len(res) == 1 else res


def _row_spec(tr, bw, cb, per_group):
    return pl.BlockSpec((tr, bw), (lambda g, i: (i, cb + g)) if per_group else (lambda g, i: (i, cb)))


def _vec_spec(bw, cb, per_group):
    return pl.BlockSpec((1, bw), (lambda g, i: (0, cb + g)) if per_group else (lambda g, i: (0, cb)))


def _rowwise(fn, rows, vecs, outs, *, name, n_rows=SEQ, tr=512, groups=1, after=()):
    n_r, n_v, n_after = len(rows), len(vecs), len(after)

    def body(*refs):
        vals = [r[...].astype(F32) for r in refs[:n_r + n_v]]
        res = fn(*vals)
        for o_ref, r in zip(refs[n_r + n_v + n_after:], res):
            o_ref[...] = r.astype(o_ref.dtype)

    res = pl.pallas_call(
        body, name=name, grid=(groups, n_rows // tr),
        in_specs=[_row_spec(tr, bw, cb, pg) for _, bw, cb, pg in rows] + [_vec_spec(bw, cb, pg) for _, bw, cb, pg in vecs]
        + [pl.BlockSpec(memory_space=pl.ANY)] * n_after,
        out_specs=[_row_spec(tr, bw, cb, pg) for _, _, bw, cb, pg in outs],
        out_shape=[jax.ShapeDtypeStruct((n_rows, w), dt) for w, dt, _, _, _ in outs],
        compiler_params=_params(("parallel", "parallel")),
    )(*[r[0] for r in rows], *[v[0] for v in vecs], *after)
    return res


def _rowwise_vjp(fn, rows, vecs, cts, row_grads, vec_grads, *, name, n_rows=SEQ, tr=512, groups=1, after=()):
    n_r, n_v, n_after = len(rows), len(vecs), len(after)
    ct_ops = [op for group in cts for op in group]
    ct_sizes = [len(group) for group in cts]
    res_ops = [g[6] for g in row_grads if g[6] is not None]
    n_ct, n_res, n_rg = len(ct_ops), len(res_ops), len(row_grads)

    def body(*refs):
        vals = [r[...].astype(F32) for r in refs[:n_r + n_v]]
        pos = n_r + n_v
        ct_vals = []
        for size in ct_sizes:
            acc = refs[pos][...].astype(F32)
            for t in range(1, size):
                acc = acc + refs[pos + t][...].astype(F32)
            ct_vals.append(acc)
            pos += size
        res_refs = refs[pos:pos + n_res]
        out_refs = refs[pos + n_res + n_after:]
        _, pullback = jax.vjp(fn, *vals)
        grads = pullback(tuple(ct_vals))
        r_i = 0
        for o_ref, g in zip(out_refs[:n_rg], row_grads):
            val = grads[g[0]]
            if g[6] is not None:
                val = val + res_refs[r_i][...].astype(F32)
                r_i += 1
            o_ref[...] = val.astype(o_ref.dtype)
        first = (pl.program_id(1) == 0)
        for o_ref, g in zip(out_refs[n_rg:], vec_grads):
            val = jnp.sum(grads[n_r + g[0]], axis=0, keepdims=True)
            init = first if g[4] else jnp.logical_and(first, pl.program_id(0) == 0)

            @pl.when(init)
            def _(o_ref=o_ref, val=val):
                o_ref[...] = val

            @pl.when(jnp.logical_not(init))
            def _(o_ref=o_ref, val=val):
                o_ref[...] += val

    in_specs = [_row_spec(tr, bw, cb, pg) for _, bw, cb, pg in rows] + [_vec_spec(bw, cb, pg) for _, bw, cb, pg in vecs]
    in_specs += [_row_spec(tr, bw, cb, pg) for _, bw, cb, pg in ct_ops + res_ops] + [pl.BlockSpec(memory_space=pl.ANY)] * n_after
    out_specs =[_row_spec(tr, g[3], g[4], g[5]) for g in row_grads] + [_vec_spec(g[2], g[3], g[4]) for g in vec_grads]
    out_shape = [jax.ShapeDtypeStruct((n_rows, g[1]), g[2]) for g in row_grads]
    out_shape += [jax.ShapeDtypeStruct((1, g[1]), F32) for g in vec_grads]
    return pl.pallas_call(
        body, name=name, grid=(groups, n_rows // tr),
        in_specs=in_specs, out_specs=out_specs, out_shape=out_shape,
        compiler_params=_params(("arbitrary", "arbitrary")),
    )(*[r[0] for r in rows], *[v[0] for v in vecs], *[c[0] for c in ct_ops], *[r[0] for r in res_ops], *after)


def _full(arr, width=None):
    return (arr, arr.shape[1] if width is None else width, 0, False)


def _make_xor(sh):
    def raw(x):
        n = x.shape[-1]
        lane = lax.broadcasted_iota(jnp.int32, x.shape, x.ndim - 1)
        up = pltpu.roll(x, n - sh, x.ndim - 1)
        down = pltpu.roll(x, sh, x.ndim - 1)
        return jnp.where((lane & sh) == 0, up, down)

    f = jax.custom_vjp(raw)
    f.defvjp(lambda x: (raw(x), None), lambda _, ct: (raw(ct),))
    return f


_SWAP_ROPE_HALVES = _make_xor(ROT // 2)


def _head_sum(x):
    n = x.shape[-1]
    same_head = (lax.broadcasted_iota(jnp.int32, (n, n), 0) // HEAD) == (lax.broadcasted_iota(jnp.int32, (n, n), 1) // HEAD)
    return _fdot(x, same_head.astype(F32), NN)


def _rms(x, g):
    return x * lax.rsqrt(jnp.mean(x * x, axis=-1, keepdims=True) + EPS) * g


def _head_rms_rope(x, g, cos, sin, scale):
    y = x * lax.rsqrt(_head_sum(x * x) * (1.0 / HEAD) + EPS) * g
    return (y * cos + _SWAP_ROPE_HALVES(y) * sin) * scale


def _qk_fn(q, k, v, cos, sin, gq, gk):
    return (_head_rms_rope(q, gq, cos, sin, HEAD ** -0.5), _head_rms_rope(k, gk, cos, sin, 1.0), v)


def _norm_fn(x, g):
    return (_rms(x, g),)


def _merge_fn(o0, o1, o2, l0, l1, l2, g):
    m = lax.stop_gradient(jnp.maximum(jnp.maximum(l0, l1), l2))
    e0, e1, e2 = jnp.exp(l0 - m), jnp.exp(l1 - m), jnp.exp(l2 - m)
    mix = (e0 * o0 + e1 * o1 + e2 * o2) / (e0 + e1 + e2)
    return (_rms(mix, g),)


def _gate_fn(y, z, g):
    return (_rms(y * (z * jax.nn.sigmoid(z)), g),)


def _attn_pair(q, kc, vc, kp=None, vp=None, has_prev=None):
    pick0, pick1 = _head_picks()
    k_band, v_band, mask = _attn_band(kc, vc, kp, vp, has_prev)
    s = jnp.where(mask, _bdot(jnp.concatenate([q * pick0, q * pick1], axis=0), k_band, NT), NEG)
    m = jnp.max(s, axis=-1, keepdims=True)
    p = jnp.exp(s - m)
    den = jnp.sum(p, axis=-1, keepdims=True)
    acc = _bdot(p, v_band, NN) * (1.0 / den)
    lse_rows = m + jnp.log(den)
    o = pick0 * acc[:ATT_BLK] + pick1 * acc[ATT_BLK:]
    lse = pick0 * lse_rows[:ATT_BLK] + pick1 * lse_rows[ATT_BLK:]
    return o, lse


def _head_picks():
    lane = lax.broadcasted_iota(jnp.int32, (1, 2 * HEAD), 1)
    return (lane < HEAD).astype(F32), (lane >= HEAD).astype(F32)


def _attn_band(kc, vc, kp, vp, has_prev):
    n_keys = ATT_BLK if kp is None else 2 * ATT_BLK
    qi = lax.broadcasted_iota(jnp.int32, (2 * ATT_BLK, n_keys), 0) & (ATT_BLK - 1)
    kj = lax.broadcasted_iota(jnp.int32, (2 * ATT_BLK, n_keys), 1)
    if kp is None:
        return kc, vc, qi >= kj
    in_prev = jnp.logical_and(jnp.logical_and(kj < ATT_BLK, kj >= qi), has_prev)
    mask = jnp.logical_or(in_prev, jnp.logical_and(kj >= ATT_BLK, qi >= kj - ATT_BLK))
    return jnp.concatenate([kp, kc], axis=0), jnp.concatenate([vp, vc], axis=0), mask


def _attn_config(b):
    r = DILATIONS[b]
    return r, ATT_BLK * r, (D_ATTN if r == 1 else 128), BRANCH_BLOCKS[b] > 1


def _for_residues(r, fn):
    if r <= 4:
        for rho in range(r):
            fn(rho)
    else:
        def step(t, carry):
            for u in range(4):
                fn(4 * t + u)
            return carry

        lax.fori_loop(0, r // 4, step, 0)


def _strided_rows(start, r):
    if r > 1:
        return pl.ds(start, ATT_BLK, stride=r)
    return pl.ds(start if isinstance(start, int) else pl.multiple_of(start, ATT_BLK), ATT_BLK)


def _attention_fwd(qn, kn, vn, b):
    r, rows, lanes, with_prev = _attn_config(b)
    cur = pl.BlockSpec((rows, lanes), lambda g, n: (n, g))
    prev = pl.BlockSpec((rows, lanes), lambda g, n: (jnp.maximum(n - 1, 0), g))

    def body(*refs):
        ins, (o_ref, l_ref) = refs[:-2], refs[-2:]
        has_prev = pl.program_id(1) > 0

        def one(rho):
            sub = _strided_rows(rho, r)
            for pair in range(lanes // 128):
                sl = pl.ds(pair * 128, 128)
                args = [ref[sub, sl] for ref in ins] + ([has_prev] if with_prev else [])
                o_ref[sub, sl], l_ref[sub, sl] = _attn_pair(*args)

        _for_residues(r, one)

    operands = (qn, kn, vn, kn, vn) if with_prev else (qn, kn, vn)
    return pl.pallas_call(
        body, name="attn_fwd_%d" % r, grid=(D_ATTN // lanes, SEQ // rows),
        in_specs=[cur, cur, cur] + ([prev, prev] if with_prev else []), out_specs=[cur, cur],
        out_shape=[jax.ShapeDtypeStruct((SEQ, D_ATTN), F32)] * 2,
        compiler_params=_params(("parallel", "parallel")),
    )(*operands)


def _attn_pair_bwd(q, kc, vc, kp, vp, o, lse, do, dl, has_prev):
    pick0, pick1 = _head_picks()
    lane = lax.broadcasted_iota(jnp.int32, (1, 2 * HEAD), 1)
    k_band, v_band, mask = _attn_band(kc, vc, kp, vp, has_prev)
    q2 = jnp.concatenate([q * pick0, q * pick1], axis=0)
    do2 = jnp.concatenate([do * pick0, do * pick1], axis=0)
    lse2 = jnp.concatenate([jnp.sum(lse * (lane == 0).astype(F32), axis=-1, keepdims=True),
                            jnp.sum(lse * (lane == HEAD).astype(F32), axis=-1, keepdims=True)], axis=0)
    base = jnp.sum(jnp.concatenate([dl * pick0, dl * pick1], axis=0) - do2 * jnp.concatenate([o, o], axis=0),
                   axis=-1, keepdims=True)
    p = jnp.exp(jnp.where(mask, _bdot(q2, k_band, NT), NEG) - lse2)
    ds = p * (_bdot(do2, v_band, NT) + base)
    dq2 = _bdot(ds, k_band, NN)
    dq = pick0 * dq2[:ATT_BLK] + pick1 * dq2[ATT_BLK:]
    dk, dv = _bdot(ds, q2, TN), _bdot(p, do2, TN)
    if kp is None:
        return dq, dk, dv
    return dq, dk[ATT_BLK:], dv[ATT_BLK:], dk[:ATT_BLK], dv[:ATT_BLK]


def _attention_bwd(qn, kn, vn, o, lse, do, dl, b):
    r, rows, lanes, with_prev = _attn_config(b)
    cur = pl.BlockSpec((rows, lanes), lambda g, n: (n, g))
    prev = pl.BlockSpec((rows, lanes), lambda g, n: (jnp.maximum(n - 1, 0), g))
    whole = pl.BlockSpec((SEQ, lanes), lambda g, n: (0, g))
    n_in = 5 if with_prev else 3

    def body(*refs):
        ins, (o_ref, l_ref, do_ref, dl_ref, dq_ref, dk_ref, dv_ref) = refs[:n_in], refs[n_in:]
        n = pl.program_id(1)

        @pl.when(n == 0)
        def _():
            dk_ref[...] = jnp.zeros_like(dk_ref)
            dv_ref[...] = jnp.zeros_like(dv_ref)

        def one(rho):
            sub = _strided_rows(rho, r)
            sub_c = _strided_rows(n * rows + rho, r)
            sub_p = _strided_rows(jnp.maximum(n - 1, 0) * rows + rho, r)
            for pair in range(lanes // 128):
                sl = pl.ds(pair * 128, 128)
                vals = [ref[sub, sl] for ref in ins] + ([] if with_prev else [None, None])
                grads = _attn_pair_bwd(*vals, o_ref[sub, sl], l_ref[sub, sl], do_ref[sub, sl], dl_ref[sub, sl], n > 0)
                dq_ref[sub, sl] = grads[0]
                dk_ref[sub_c, sl] += grads[1]
                dv_ref[sub_c, sl] += grads[2]
                if with_prev:
                    dk_ref[sub_p, sl] += grads[3]
                    dv_ref[sub_p, sl] += grads[4]

        _for_residues(r, one)

    operands = (qn, kn, vn, kn, vn) if with_prev else (qn, kn, vn)
    return pl.pallas_call(
        body, name="attn_bwd_%d" % r, grid=(D_ATTN // lanes, SEQ // rows),
        in_specs=[cur, cur, cur] + ([prev, prev] if with_prev else []) + [cur] * 4, out_specs=[cur, whole, whole],
        out_shape=[jax.ShapeDtypeStruct((SEQ, D_ATTN), F32)] * 3,
        compiler_params=_params(("parallel", "arbitrary")),
    )(*operands, o, lse, do, dl)


CONV_COLS = 256
XBC_BLOCK0 = (3 * D_ATTN + D_SSM) // CONV_COLS


def _shift_rows(x, s):
    n = x.shape[0]
    t = lax.broadcasted_iota(jnp.int32, x.shape, 0)
    if s >= 0:
        return jnp.where(t >= s, pltpu.roll(x, s, 0), 0.0)
    return jnp.where(t < n + s, pltpu.roll(x, n + s, 0), 0.0)


def _conv_pre(x, w_ref, b_ref):
    delayed = [_shift_rows(x, 3 - k) for k in range(3)]
    pre = b_ref[...] + w_ref[3:4, :] * x
    for k in range(3):
        pre = pre + w_ref[k:k + 1, :] * delayed[k]
    return pre, delayed


def _conv_fwd(proj, conv_w, conv_b):
    cols = conv_w.shape[1]

    def body(x_ref, w_ref, b_ref, o_ref):
        pre, _ = _conv_pre(x_ref[...], w_ref, b_ref)
        o_ref[...] = pre * jax.nn.sigmoid(pre)

    blk = pl.BlockSpec((SEQ, CONV_COLS), lambda j: (0, j))
    return pl.pallas_call(
        body, name="conv_fwd", grid=(cols // CONV_COLS,),
        in_specs=[pl.BlockSpec((SEQ, CONV_COLS), lambda j: (0, XBC_BLOCK0 + j)),
                  pl.BlockSpec((4, CONV_COLS), lambda j: (0, j)), pl.BlockSpec((1, CONV_COLS), lambda j: (0, j))],
        out_specs=blk, out_shape=jax.ShapeDtypeStruct((SEQ, cols), F32),
        compiler_params=_params(("parallel",)),
    )(proj, conv_w, conv_b)


def _conv_bwd(proj, conv_w, conv_b, dxs, db, dc):
    cols = conv_w.shape[1]
    x_blocks, b_blocks = dxs.shape[1] // CONV_COLS, db.shape[1] // CONV_COLS

    def body(x_ref, w_ref, b_ref, dxs_ref, db_ref_in, dc_ref_in, dx_ref, dw_ref, db_ref):
        j = pl.program_id(0)
        dy = jnp.where(j < x_blocks, dxs_ref[...], jnp.where(j < x_blocks + b_blocks, db_ref_in[...], dc_ref_in[...]))
        x = x_ref[...]
        pre, delayed = _conv_pre(x, w_ref, b_ref)
        sg = jax.nn.sigmoid(pre)
        dpre = dy * (sg * (1.0 + pre * (1.0 - sg)))
        db_ref[...] = jnp.sum(dpre, axis=0, keepdims=True)
        dx = w_ref[3:4, :] * dpre
        dw_ref[3:4, :] = jnp.sum(dpre * x, axis=0, keepdims=True)
        for k in range(3):
            dx = dx + w_ref[k:k + 1, :] * _shift_rows(dpre, k - 3)
            dw_ref[k:k + 1, :] = jnp.sum(dpre * delayed[k], axis=0, keepdims=True)
        dw_ref[4:8, :] = jnp.zeros((4, CONV_COLS), F32)
        dx_ref[...] = dx.astype(dx_ref.dtype)

    blk = pl.BlockSpec((SEQ, CONV_COLS), lambda j: (0, j))
    parts = [pl.BlockSpec((SEQ, CONV_COLS), lambda j: (0, jnp.minimum(j, x_blocks - 1))),
             pl.BlockSpec((SEQ, CONV_COLS), lambda j: (0, jnp.clip(j - x_blocks, 0, b_blocks - 1))),
             pl.BlockSpec((SEQ, CONV_COLS), lambda j: (0, jnp.clip(j - x_blocks - b_blocks, 0, b_blocks - 1)))]
    return pl.pallas_call(
        body, name="conv_bwd", grid=(cols // CONV_COLS,),
        in_specs=[pl.BlockSpec((SEQ, CONV_COLS), lambda j: (0, XBC_BLOCK0 + j)),
                  pl.BlockSpec((4, CONV_COLS), lambda j: (0, j)), pl.BlockSpec((1, CONV_COLS), lambda j: (0, j))] + parts,
        out_specs=[blk, pl.BlockSpec((8, CONV_COLS), lambda j: (0, j)), pl.BlockSpec((1, CONV_COLS), lambda j: (0, j))],
        out_shape=[jax.ShapeDtypeStruct((SEQ, cols), BF16), jax.ShapeDtypeStruct((8, cols), F32),
                   jax.ShapeDtypeStruct((1, cols), F32)],
        compiler_params=_params(("parallel",)),
    )(proj, conv_w, conv_b, dxs, db, dc)


HEADS_PER_GROUP = 4


GROUP_WIDTH = HEADS_PER_GROUP * HEAD


def _ssd_chunk(x, bm, cm, dtr, bias, alog, dsk, h):
    row = lax.broadcasted_iota(jnp.int32, (CHUNK, CHUNK), 0)
    col = lax.broadcasted_iota(jnp.int32, (CHUNK, CHUNK), 1)
    causal = row >= col
    z = dtr + bias
    dt = jnp.maximum(z, 0.0) + jnp.log(1.0 + jnp.exp(-jnp.abs(z)))
    acs = _fdot(causal.astype(F32), dt * -jnp.exp(alog), NN)
    acs_t, dt_t = acs.T, dt.T
    cb = _bdot(cm, bm, NT)
    lane = lax.broadcasted_iota(jnp.int32, (1, CHUNK), 1)
    sub = lax.broadcasted_iota(jnp.int32, (CHUNK, 1), 0)
    wide = lax.broadcasted_iota(jnp.int32, (1, GROUP_WIDTH), 1) // HEAD
    tall = lax.broadcasted_iota(jnp.int32, (GROUP_WIDTH, 1), 0) // HEAD
    acs_last = jnp.sum(acs * (sub == CHUNK - 1).astype(F32), axis=0, keepdims=True)
    to_lanes = (lax.broadcasted_iota(jnp.int32, (CHUNK, GROUP_WIDTH), 0)
                == lax.broadcasted_iota(jnp.int32, (CHUNK, GROUP_WIDTH), 1) // HEAD).astype(F32)
    grow = _fdot(jnp.exp(acs), to_lanes, NN)
    keep = _fdot(jnp.exp(acs_last - acs) * dt, to_lanes, NN)
    w_parts, x_parts, skip, carry = [], [], 0.0, 0.0
    for j in range(HEADS_PER_GROUP):
        on_lane, on_sub = (lane == j).astype(F32), (sub == j).astype(F32)
        acs_c = jnp.sum(acs * on_lane, axis=1, keepdims=True)
        acs_r = jnp.sum(acs_t * on_sub, axis=0, keepdims=True)
        dt_r = jnp.sum(dt_t * on_sub, axis=0, keepdims=True)
        w_parts.append(cb * jnp.exp(jnp.where(causal, acs_c - acs_r, NEG)) * dt_r)
        x_parts.append(x * (wide == j).astype(F32))
        skip = skip + jnp.sum(dsk * on_lane, axis=1, keepdims=True) * (wide == j).astype(F32)
        carry = carry + jnp.sum(jnp.exp(acs_last) * on_lane, axis=1, keepdims=True) * (tall == j).astype(F32)
    y_diag = _bdot(jnp.concatenate(w_parts, axis=1), jnp.concatenate(x_parts, axis=0), NN)
    y = y_diag + _bdot(cm, h, NT) * grow + skip * x
    return y, h * carry + _bdot(x * keep, bm, TN)


GROUPS_PER_STEP = 4
SSD_STEPS = N_GROUPS // GROUPS_PER_STEP


def _ssd_specs(reverse):
    n_chunks = SEQ // CHUNK
    c_of = (lambda c: n_chunks - 1 - c) if reverse else (lambda c: c)
    x_w, n_w, dt_w = GROUPS_PER_STEP * GROUP_WIDTH, GROUPS_PER_STEP * N_STATE, GROUPS_PER_STEP * 128
    x_spec = pl.BlockSpec((CHUNK, x_w), lambda g, c: (c_of(c), g))
    b_spec = pl.BlockSpec((CHUNK, n_w), lambda g, c: (c_of(c), D_SSM // n_w + g))
    c_spec = pl.BlockSpec((CHUNK, n_w), lambda g, c: (c_of(c), (D_SSM + N_GROUPS * N_STATE) // n_w + g))
    dt_spec = pl.BlockSpec((CHUNK, dt_w), lambda g, c: (c_of(c), g))
    vec_spec = pl.BlockSpec((1, dt_w), lambda g, c: (0, g))
    h_spec = pl.BlockSpec((None, GROUPS_PER_STEP, GROUP_WIDTH, N_STATE), lambda g, c: (c_of(c), g, 0, 0))
    return x_spec, b_spec, c_spec, dt_spec, vec_spec, h_spec


def _group_slices(u):
    return pl.ds(u * GROUP_WIDTH, GROUP_WIDTH), pl.ds(u * N_STATE, N_STATE), pl.ds(u * 128, 128)


def _ssd_gated_chunk(x, bm, cm, dtr, bias, alog, dsk, h, z, g_out):
    y, h_new = _ssd_chunk(x, bm, cm, dtr, bias, alog, dsk, h)
    return _gate_fn(y, z, g_out)[0], h_new


def _ssd_gate_specs(reverse):
    x_spec = _ssd_specs(reverse)[0]
    z_block0 = 3 * D_ATTN // x_spec.block_shape[1]
    z_spec = pl.BlockSpec(x_spec.block_shape, lambda g, c: (x_spec.index_map(g, c)[0], z_block0 + g))
    return z_spec, pl.BlockSpec((1, x_spec.block_shape[1]), lambda g, c: (0, g))


def _ssd_fwd(xbc, dt_raw, bias, alog, dsk, proj, g_out):
    x_spec, b_spec, c_spec, dt_spec, vec_spec, h_spec = _ssd_specs(False)
    z_spec, g_spec = _ssd_gate_specs(False)

    def body(x_ref, b_ref, c_ref, dt_ref, bias_ref, alog_ref, dsk_ref, z_ref, g_ref, ssm_ref, hin_ref, h_scr):
        @pl.when(pl.program_id(1) == 0)
        def _():
            h_scr[...] = jnp.zeros_like(h_scr)

        for u in range(GROUPS_PER_STEP):
            xs, ns, ds = _group_slices(u)
            h = h_scr[u]
            hin_ref[u] = h
            ssm, h_scr[u] = _ssd_gated_chunk(x_ref[:, xs], b_ref[:, ns], c_ref[:, ns], dt_ref[:, ds], bias_ref[:, ds],
                                             alog_ref[:, ds], dsk_ref[:, ds], h, z_ref[:, xs], g_ref[:, xs])
            ssm_ref[:, xs] = ssm.astype(ssm_ref.dtype)

    return pl.pallas_call(
        body, name="ssd_fwd", grid=(SSD_STEPS, SEQ // CHUNK),
        in_specs=[x_spec, b_spec, c_spec, dt_spec, vec_spec, vec_spec, vec_spec, z_spec, g_spec],
        out_specs=[x_spec, h_spec],
        out_shape=[jax.ShapeDtypeStruct((SEQ, D_SSM), BF16),
                   jax.ShapeDtypeStruct((SEQ // CHUNK, N_GROUPS, GROUP_WIDTH, N_STATE), F32)],
        scratch_shapes=[pltpu.VMEM((GROUPS_PER_STEP, GROUP_WIDTH, N_STATE), F32)],
        compiler_params=_params(("parallel", "arbitrary")),
    )(xbc, xbc, xbc, dt_raw, bias, alog, dsk, proj, g_out)


def _ssd_bwd(xbc, dt_raw, bias, alog, dsk, h_in, proj, g_out, dmix):
    x_spec, b_spec, c_spec, dt_spec, vec_spec, h_spec = _ssd_specs(True)
    z_spec, g_spec = _ssd_gate_specs(True)
    ct_block0 = D_ATTN // x_spec.block_shape[1]
    ct_spec = pl.BlockSpec(x_spec.block_shape, lambda g, c: (x_spec.index_map(g, c)[0], ct_block0 + g))

    def body(x_ref, b_ref, c_ref, dt_ref, bias_ref, alog_ref, dsk_ref, hin_ref, z_ref, g_ref, ct_ref,
             dx_ref, db_ref, dc_ref, ddt_ref, dbias_ref, dalog_ref, ddsk_ref, dz_ref, dg_ref, dh_scr):
        first = pl.program_id(1) == 0

        @pl.when(first)
        def _():
            dh_scr[...] = jnp.zeros_like(dh_scr)

        for u in range(GROUPS_PER_STEP):
            xs, ns, ds = _group_slices(u)
            _, pullback = jax.vjp(_ssd_gated_chunk, x_ref[:, xs], b_ref[:, ns], c_ref[:, ns], dt_ref[:, ds], bias_ref[:, ds],
                                  alog_ref[:, ds], dsk_ref[:, ds], hin_ref[u], z_ref[:, xs], g_ref[:, xs])
            g = pullback((ct_ref[:, xs], dh_scr[u]))
            dx_ref[:, xs], db_ref[:, ns], dc_ref[:, ns] = g[0], g[1], g[2]
            ddt_ref[:, ds] = g[3].astype(ddt_ref.dtype)
            dh_scr[u] = g[7]
            dz_ref[:, xs] = g[8].astype(dz_ref.dtype)
            sums = ((dbias_ref, g[4], ds), (dalog_ref, g[5], ds), (ddsk_ref, g[6], ds),
                    (dg_ref, jnp.sum(g[9], axis=0, keepdims=True), xs))
            for o_ref, val, lanes in sums:
                @pl.when(first)
                def _(o_ref=o_ref, val=val, lanes=lanes):
                    o_ref[:, lanes] = val

                @pl.when(jnp.logical_not(first))
                def _(o_ref=o_ref, val=val, lanes=lanes):
                    o_ref[:, lanes] += val

    n_chunks = SEQ // CHUNK
    out_b = pl.BlockSpec((CHUNK, GROUPS_PER_STEP * N_STATE), lambda g, c: (n_chunks - 1 - c, g))
    return pl.pallas_call(
        body, name="ssd_bwd", grid=(SSD_STEPS, n_chunks),
        in_specs=[x_spec, b_spec, c_spec, dt_spec, vec_spec, vec_spec, vec_spec, h_spec, z_spec, g_spec, ct_spec],
        out_specs=[x_spec, out_b, out_b, dt_spec, vec_spec, vec_spec, vec_spec, x_spec, g_spec],
        out_shape=[jax.ShapeDtypeStruct((SEQ, D_SSM), F32), jax.ShapeDtypeStruct((SEQ, N_GROUPS * N_STATE), F32),
                   jax.ShapeDtypeStruct((SEQ, N_GROUPS * N_STATE), F32), jax.ShapeDtypeStruct((SEQ, DT_PAD), BF16),
                   jax.ShapeDtypeStruct((1, DT_PAD), F32), jax.ShapeDtypeStruct((1, DT_PAD), F32),
                   jax.ShapeDtypeStruct((1, DT_PAD), F32), jax.ShapeDtypeStruct((SEQ, D_SSM), BF16),
                   jax.ShapeDtypeStruct((1, D_SSM), F32)],
        scratch_shapes=[pltpu.VMEM((GROUPS_PER_STEP, GROUP_WIDTH, N_STATE), F32)],
        compiler_params=_params(("parallel", "arbitrary")),
    )(xbc, xbc, xbc, dt_raw, bias, alog, dsk, h_in, proj, g_out, dmix)


CROSS_HEAD = 128
CROSS_ROWS = 1024


def _cross_head(q, k, v, gq, gk):
    qn = _rms(q, gq) * (CROSS_HEAD ** -0.5)
    kn = _rms(k, gk)
    s = _bdot(qn, kn, NT)
    p = jnp.exp(s - lax.stop_gradient(jnp.max(s, axis=-1, keepdims=True)))
    return _bdot(p, v, NN) * (1.0 / jnp.sum(p, axis=-1, keepdims=True))


def _cross_specs():
    q_spec = pl.BlockSpec((CROSS_ROWS, CROSS_HEAD), lambda h, i: (i, h))
    k_spec = pl.BlockSpec((N_MEM, CROSS_HEAD), lambda h, i: (0, h))
    v_spec = pl.BlockSpec((N_MEM, CROSS_HEAD), lambda h, i: (0, 4 + h))
    g_spec = pl.BlockSpec((1, CROSS_HEAD), lambda h, i: (0, 0))
    return q_spec, k_spec, v_spec, g_spec


def _cross_fwd(qc, kv, gq, gk):
    q_spec, k_spec, v_spec, g_spec = _cross_specs()

    def body(q_ref, k_ref, v_ref, gq_ref, gk_ref, o_ref):
        o_ref[...] = _cross_head(q_ref[...], k_ref[...], v_ref[...], gq_ref[...], gk_ref[...]).astype(o_ref.dtype)

    return pl.pallas_call(
        body, name="cross_fwd", grid=(4, SEQ // CROSS_ROWS),
        in_specs=[q_spec, k_spec, v_spec, g_spec, g_spec], out_specs=q_spec,
        out_shape=jax.ShapeDtypeStruct((SEQ, D_CROSS), BF16),
        compiler_params=_params(("parallel", "parallel")),
    )(qc, kv, kv, gq, gk)


def _cross_bwd(qc, kv, gq, gk, do):
    q_spec, k_spec, v_spec, g_spec = _cross_specs()

    def body(q_ref, k_ref, v_ref, gq_ref, gk_ref, do_ref, dq_ref, dk_ref, dv_ref, dgq_ref, dgk_ref):
        _, pullback = jax.vjp(_cross_head, q_ref[...], k_ref[...], v_ref[...], gq_ref[...], gk_ref[...])
        dq, dk, dv, dgq, dgk = pullback(do_ref[...].astype(F32))
        dq_ref[...] = dq.astype(dq_ref.dtype)
        row0 = pl.program_id(1) == 0
        all0 = jnp.logical_and(row0, pl.program_id(0) == 0)
        for o_ref, val, init in ((dk_ref, dk, row0), (dv_ref, dv, row0), (dgq_ref, dgq, all0), (dgk_ref, dgk, all0)):
            @pl.when(init)
            def _(o_ref=o_ref, val=val):
                o_ref[...] = val

            @pl.when(jnp.logical_not(init))
            def _(o_ref=o_ref, val=val):
                o_ref[...] += val

    return pl.pallas_call(
        body, name="cross_bwd", grid=(4, SEQ // CROSS_ROWS),
        in_specs=[q_spec, k_spec, v_spec, g_spec, g_spec, q_spec],
        out_specs=[q_spec, k_spec, k_spec, g_spec, g_spec],
        out_shape=[jax.ShapeDtypeStruct((SEQ, D_CROSS), BF16), jax.ShapeDtypeStruct((N_MEM, D_CROSS), F32),
                   jax.ShapeDtypeStruct((N_MEM, D_CROSS), F32), jax.ShapeDtypeStruct((1, CROSS_HEAD), F32),
                   jax.ShapeDtypeStruct((1, CROSS_HEAD), F32)],
        compiler_params=_params(("arbitrary", "arbitrary")),
    )(qc, kv, kv, gq, gk, do)


def _loss_epilogue(acc, residual, target):
    err = acc + residual - target
    dy = err * (1.0 / D_MODEL)
    part = jnp.sum(jnp.sum(err * err, axis=1, keepdims=True), axis=0, keepdims=True) * (0.5 / D_MODEL)
    return dy, dy, part


def _pad_heads(v):
    return jnp.pad(v.reshape(N_GROUPS, HEADS_PER_GROUP), ((0, 0), (0, 128 - HEADS_PER_GROUP))).reshape(1, DT_PAD)


def _unpad_heads(v):
    return v.reshape(v.shape[0], N_GROUPS, 128)[:, :, :HEADS_PER_GROUP].reshape(v.shape[0], N_DT)


def _rope_tables(positions):
    half = ROT // 2
    inv_freq = ROPE_THETA ** (-2.0 * jnp.arange(half, dtype=F32) / ROT)
    ang = positions.reshape(SEQ, 1).astype(F32) * inv_freq
    cos, sin = jnp.cos(ang), jnp.sin(ang)
    ones, zeros = jnp.ones((SEQ, HEAD - ROT), F32), jnp.zeros((SEQ, HEAD - ROT), F32)
    cos_h = jnp.concatenate([cos, cos, ones], axis=1)
    sin_h = jnp.concatenate([-sin, sin, zeros], axis=1)
    return jnp.tile(cos_h, (1, 2)), jnp.tile(sin_h, (1, 2))


def _add_res(acc, res):
    return (acc + res,)


def _norm_bwd_epilogue(acc, x, residual, *more):
    *part, g = more
    ct = acc + part[0] if part else acc
    _, pullback = jax.vjp(_rms, x, g)
    dx, dg = pullback(ct)
    return dx + residual, dg


def _add_res_and_norm(acc, res, g):
    y = acc + res
    return y, _rms(y, g)


def _settle(grads, *after):
    if hasattr(grads, "settle"):
        grads.settle(*after)


def _take_token(grads):
    token = getattr(grads, "token", None)
    if token is None:
        return ()
    grads.token = None
    return (token,)


def _local_step(x, mem, positions, target, p, w, more_weights=None, grads=None, h=None):
    grads = {} if grads is None else grads
    w = dict(w)
    cos, sin = _rope_tables(positions)
    gq2, gk2 = jnp.tile(p["g_q"], (1, 2)), jnp.tile(p["g_k"], (1, 2))
    bias, alog, dsk = _pad_heads(p["dt_bias"]), _pad_heads(p["a_log"]), _pad_heads(p["d_skip"])
    norm_out = [(D_MODEL, BF16, D_MODEL, 0, False)]

    if h is None:
        h = _rowwise(_norm_fn, [_full(x)], [_full(p["g_mix"])], norm_out, name="norm_in")[0]
    proj = _matmul(h, w["w_in"], mode="nn", name="in_proj", outs=[F32], n_cols=D_MAIN)
    dt_raw = _matmul(h, w["w_dt"], mode="nn", name="dt_proj", outs=[F32])
    pairs = D_ATTN // 128
    qk_rows = [(proj, 128, 0, True), (proj, 128, pairs, True), (proj, 128, 2 * pairs, True), _full(cos), _full(sin)]
    qk_vecs = [_full(gq2), _full(gk2)]
    qn, kn, vn = _rowwise(_qk_fn, qk_rows, qk_vecs, [(D_ATTN, F32, 128, 0, True)] * 3, name="qk_prep", groups=8, tr=1024)
    branches = [_attention_fwd(qn, kn, vn, b) for b in range(3)]
    merge_rows = [_full(o) for o, _ in branches] + [_full(lse) for _, lse in branches]
    attn = _rowwise(_merge_fn, merge_rows, [_full(p["g_attn_out"])], [(D_ATTN, BF16, D_ATTN, 0, False)], name="attn_merge")[0]
    xbc = _conv_fwd(proj, p["conv_w"], p["conv_b"])
    ssm, h_in = _ssd_fwd(xbc, dt_raw, bias, alog, dsk, proj, p["g_ssm_out"])
    mix = jnp.concatenate([attn, ssm], axis=1)
    if more_weights is not None:
        w.update(more_weights("mixer_done", mix))
    x1, hc = _matmul(mix, w["w_out"], mode="nn", name="out_proj", outs=[F32, BF16], extra=(x,), vecs=(p["g_cross"],),
                     epilogue=_add_res_and_norm, tm=512, tn=D_MODEL)
    memh = _rowwise(_norm_fn, [_full(mem)], [_full(p["g_mem"])], norm_out, name="norm_mem", n_rows=N_MEM, tr=N_MEM)[0]
    qc = _matmul(hc, w["w_cq"], mode="nn", name="cq_proj", outs=[F32])
    if more_weights is not None:
        w.update(more_weights("cross_started", qc))
    kv = _matmul(memh, w["w_ckv"], mode="nn", name="ckv_proj", outs=[F32])
    oc = _cross_fwd(qc, kv, p["g_cq"], p["g_ck"])
    x2, hm = _matmul(oc, w["w_co"], mode="nn", name="co_proj", outs=[F32, BF16], extra=(x1,), vecs=(p["g_mlp"],),
                     epilogue=_add_res_and_norm, tm=512, tn=D_MODEL)
    if more_weights is not None:
        w.update(more_weights("cross_done", hm))
    u, act = _matmul(hm, w["w_up"], mode="nn", name="up_proj", outs=[F32, BF16],
                     epilogue=lambda acc: (acc, jnp.square(jnp.maximum(acc, 0.0))))
    dy, dyb, loss_tiles = _matmul(act, w["w_down"], mode="nn", name="down_proj", outs=[F32, BF16], extra=(x2, target),
                                  epilogue=_loss_epilogue, tile_sums=1)
    loss = jnp.sum(loss_tiles).reshape(1, 1)

    grads["w_down"] = _matmul(act, dyb, mode="tn", name="dw_down", outs=[BF16], after=_take_token(grads))
    du = _matmul(dyb, w["w_down"], mode="nt", name="d_act", outs=[BF16], extra=(u,), after=_take_token(grads),
                 epilogue=lambda acc, uu: (acc * (2.0 * jnp.maximum(uu, 0.0)),))
    _settle(grads, du)
    grads["w_up"] = _matmul(hm, du, mode="tn", name="dw_up", outs=[BF16], col_shards=4, after=_take_token(grads))
    dx2, grads["g_mlp"] = _matmul(du, w["w_up"], mode="nt", name="d_hm", outs=[F32], extra=(x2, dy), vecs=(p["g_mlp"],),
                                  epilogue=_norm_bwd_epilogue, tile_rows=1, after=_take_token(grads), tm=512, tn=D_MODEL,
                                  tk=1024)
    _settle(grads, dx2)
    grads["w_co"] = _matmul(oc, dx2, mode="tn", name="dw_co", outs=[BF16], col_shards=4, after=_take_token(grads))
    doc = _matmul(dx2, w["w_co"], mode="nt", name="d_oc", outs=[BF16])
    dqc, dkc, dvc, grads["g_cq"], grads["g_ck"] = _cross_bwd(qc, kv, p["g_cq"], p["g_ck"], doc)
    grads["w_cq"] = _matmul(hc, dqc, mode="tn", name="dw_cq", outs=[BF16])
    dkv = jnp.concatenate([dkc, dvc], axis=1)
    grads["w_ckv"] = _matmul(memh, dkv, mode="tn", name="dw_ckv", outs=[BF16])
    dmemh = _matmul(dkv, w["w_ckv"], mode="nt", name="d_memh", outs=[F32])
    grads["g_mem"] = _rowwise_vjp(_norm_fn, [_full(mem)], [_full(p["g_mem"])], [[_full(dmemh)]], [],
                                  [(0, D_MODEL, D_MODEL, 0, False)], name="norm_mem_bwd", n_rows=N_MEM, tr=N_MEM)[0]
    dx1, grads["g_cross"] = _matmul(dqc, w["w_cq"], mode="nt", name="d_hc", outs=[F32], extra=(x1, dx2), vecs=(p["g_cross"],),
                                    epilogue=_norm_bwd_epilogue, tile_rows=1, tm=512, tn=D_MODEL)
    grads["w_out"] = _matmul(mix, dx1, mode="tn", name="dw_out", outs=[BF16])
    dmix = _matmul(dx1, w["w_out"], mode="nt", name="d_mix", outs=[F32], after=_take_token(grads))
    _settle(grads, dmix)
    merge_grads = [(i, D_ATTN, F32, D_ATTN, 0, False, None) for i in range(6)]
    *dol, grads["g_attn_out"] = _rowwise_vjp(
        _merge_fn, merge_rows, [_full(p["g_attn_out"])], [[(dmix, D_ATTN, 0, False)]],
        merge_grads, [(0, D_ATTN, D_ATTN, 0, False)], name="attn_merge_bwd", tr=256, after=_take_token(grads))
    dqkv = [_attention_bwd(qn, kn, vn, *branches[b], dol[b], dol[3 + b], b) for b in range(3)]
    qk_cts = [[(dqkv[b][i], 128, 0, True) for b in range(3)] for i in range(3)]
    dq, dk, dv, dgq2, dgk2 = _rowwise_vjp(
        _qk_fn, qk_rows, qk_vecs, qk_cts, [(i, D_ATTN, BF16, 128, 0, True, None) for i in range(3)],
        [(0, 128, 128, 0, False), (1, 128, 128, 0, False)], name="qk_prep_bwd", groups=8, tr=1024)
    grads["g_q"] = dgq2[:, :HEAD] + dgq2[:, HEAD:]
    grads["g_k"] = dgk2[:, :HEAD] + dgk2[:, HEAD:]
    dxs, db, dc, ddt, dbias, dalog, ddsk, dz, grads["g_ssm_out"] = _ssd_bwd(xbc, dt_raw, bias, alog, dsk, h_in, proj,
                                                                             p["g_ssm_out"], dmix)
    grads["dt_bias"], grads["a_log"], grads["d_skip"] = _unpad_heads(dbias), _unpad_heads(dalog), _unpad_heads(ddsk)
    dxbc_raw, dconv_w, grads["conv_b"] = _conv_bwd(proj, p["conv_w"], p["conv_b"], dxs, db, dc)
    grads["conv_w"] = dconv_w[:4]
    dproj = jnp.concatenate([dq, dk, dv, dz, dxbc_raw], axis=1)
    grads["w_main"] = _matmul(h, dproj, mode="tn", name="dw_main", outs=[BF16], out_cols=D_MAIN + N_DT)
    grads["w_dt"] = _matmul(h, ddt, mode="tn", name="dw_dt", outs=[BF16])
    dh = _matmul(dproj, w["w_in"], mode="nt", name="d_h_main", outs=[F32], after=_take_token(grads))
    grad_x, grads["g_mix"] = _matmul(ddt, w["w_dt"], mode="nt", name="d_h_dt", outs=[F32], extra=(x, dx1, dh),
                                     vecs=(p["g_mix"],), epilogue=_norm_bwd_epilogue, tile_rows=1, tm=512, tn=D_MODEL)
    return loss, grad_x, grads


MATRICES = ("w_in", "w_out", "w_cq", "w_ckv", "w_co", "w_up", "w_down")
ROW_SHARDED = ("w_out", "w_cq", "w_ckv", "w_down")
N_CHIPS = 4
ANY = pl.BlockSpec(memory_space=pl.ANY)


def _place():
    return lax.axis_index("x"), lax.axis_index("y"), lax.axis_index("c")


def _other_chips(x, y):
    return [(1 - x, y), (x, 1 - y), (1 - x, 1 - y)]


def _remote(src, dst, send_sem, recv_sem, device):
    return pltpu.make_async_remote_copy(src_ref=src, dst_ref=dst, send_sem=send_sem, recv_sem=recv_sem,
                                        device_id=device, device_id_type=MESH)


def _gathered_shape(name, shard):
    rows, cols = shard.shape
    if name == "w_in":
        return (N_CHIPS, rows, cols)
    return (N_CHIPS * rows, cols) if name in ROW_SHARDED else (rows, N_CHIPS * cols)


def _shard_window(name, ref, rows, cols, chip, half):
    r0, nr = (0, rows) if half is None else (half * (rows // 2), rows // 2)
    if name == "w_in":
        return ref.at[chip, pl.ds(r0, nr), :]
    if name in ROW_SHARDED:
        return ref.at[pl.ds(chip * rows + r0, nr), :]
    return ref.at[pl.ds(r0, nr), pl.ds(pl.multiple_of(chip * cols, 128), cols)]


def _cast_into_gathered(w, name, chip, after=()):
    rows, cols = w.shape
    tr = _tile(rows, ROW_TILE)

    def body(chip_ref, w_ref, *rest):
        rest[-1][...] = w_ref[...].astype(BF16)

    if name == "w_in":
        out_spec = pl.BlockSpec((None, tr, cols), lambda i, chip_ref: (chip_ref[0], i, 0))
    elif name in ROW_SHARDED:
        out_spec = pl.BlockSpec((tr, cols), lambda i, chip_ref: (chip_ref[0] * (rows // tr) + i, 0))
    else:
        out_spec = pl.BlockSpec((tr, cols), lambda i, chip_ref: (i, chip_ref[0]))
    grid_spec = pltpu.PrefetchScalarGridSpec(
        num_scalar_prefetch=1, grid=(rows // tr,),
        in_specs=[pl.BlockSpec((tr, cols), lambda i, chip_ref: (i, 0))] + [pl.BlockSpec(memory_space=pl.ANY)] * len(after),
        out_specs=out_spec)
    return pl.pallas_call(body, name="cast_" + name, grid_spec=grid_spec,
                          out_shape=jax.ShapeDtypeStruct(_gathered_shape(name, w), BF16),
                          compiler_params=_params(("parallel",)))(chip.reshape(1).astype(jnp.int32), w, *after)


def _w_in_columns(arr, to_shards):
    rows, piece = D_MODEL, (D_MAIN + N_DT) // N_CHIPS
    tr = ROW_TILE

    def body(a_ref, o_ref):
        for j in range(N_CHIPS):
            if to_shards:
                o_ref[j] = a_ref[:, pl.ds(piece * j, piece)]
            else:
                o_ref[:, pl.ds(piece * j, piece)] = a_ref[j]

    pieces = pl.BlockSpec((N_CHIPS, tr, piece), lambda i: (0, i, 0))
    matrix = pl.BlockSpec((tr, N_CHIPS * piece), lambda i: (i, 0))
    out_dims = (N_CHIPS, rows, piece) if to_shards else (rows, N_CHIPS * piece)
    return pl.pallas_call(
        body, name="w_in_to_shards" if to_shards else "w_in_from_shards", grid=(rows // tr,),
        in_specs=[matrix if to_shards else pieces], out_specs=pieces if to_shards else matrix,
        out_shape=jax.ShapeDtypeStruct(out_dims, arr.dtype), compiler_params=_params(("parallel",)))(arr)


HBM = pl.BlockSpec(memory_space=pltpu.HBM)
SEM = pl.BlockSpec(memory_space=pltpu.SEMAPHORE)
EFFECT = pltpu.SideEffectType.DATAFLOW_SIDE_EFFECTING


def _split_start(name, bufs, plan, counts, after=()):
    n, n_g, n_after = len(bufs), len(counts), len(after)

    def body(*refs):
        ins, sems, token = refs[:n], refs[n + n_after:n + n_after + 2 * n_g], refs[-1]
        for g, copies in enumerate(plan(ins)):
            for i, (src, dst, device, _) in enumerate(copies):
                _remote(src, dst, sems[2 * g].at[i], sems[2 * g + 1].at[i], device).start()
        token[...] = jnp.zeros_like(token)

    sem_shapes = [pltpu.SemaphoreType.DMA((cnt,)) for cnt in counts for _ in range(2)]
    res = pl.pallas_call(
        body, name=name,
        out_shape=(*sem_shapes, *[pltpu.HBM(b.shape, b.dtype) for b in bufs], jax.ShapeDtypeStruct((8, 128), F32)),
        in_specs=(*(HBM,) * n, *(ANY,) * n_after),
        out_specs=(*(SEM,) * (2 * n_g), *(HBM,) * n, pl.BlockSpec(memory_space=pltpu.VMEM)),
        input_output_aliases={i: 2 * n_g + i for i in range(n)},
        compiler_params=pltpu.CompilerParams(has_side_effects=EFFECT),
    )(*[pltpu.with_memory_space_constraint(b, pltpu.HBM) for b in bufs], *after)
    sems = [(res[2 * g], res[2 * g + 1]) for g in range(n_g)]
    return sems, list(res[2 * n_g:2 * n_g + n]), res[-1]


def _split_wait(name, bufs, sems, plan, *after):
    n = len(bufs)

    def body(*refs):
        ins, send, recv = refs[:n], refs[n], refs[n + 1]
        (copies,) = plan(ins)
        for i, (src, _, device, landing) in enumerate(copies):
            cp = _remote(src, landing, send.at[i], recv.at[i], device)
            cp.wait_send()
            cp.wait_recv()

    res = pl.pallas_call(
        body, name=name, out_shape=tuple(pltpu.HBM(b.shape, b.dtype) for b in bufs),
        in_specs=(*(HBM,) * n, SEM, SEM, *(ANY,) * len(after)), out_specs=(HBM,) * n,
        input_output_aliases={i: i for i in range(n)},
        compiler_params=pltpu.CompilerParams(has_side_effects=EFFECT),
    )(*bufs, sems[0], sems[1], *after)
    return list(res)


def _ici_plan(names, shard_shapes):
    def plan(refs):
        x, y, c = _place()
        copies = []
        for ref, name in zip(refs, names):
            win = _shard_window(name, ref, *shard_shapes[name], 2 * x + y, c)
            for px, py in _other_chips(x, y):
                copies.append((win, win, (px, py, c), _shard_window(name, ref, *shard_shapes[name], 2 * px + py, c)))
        return [copies]
    return plan


def _pass_on_plan(names, shard_shapes):
    def plan(refs):
        x, y, c = _place()
        copies = []
        for ref, name in zip(refs, names):
            for px, py in _other_chips(x, y):
                win = _shard_window(name, ref, *shard_shapes[name], 2 * px + py, c)
                copies.append((win, win, (x, y, 1 - c), _shard_window(name, ref, *shard_shapes[name], 2 * px + py, 1 - c)))
        return [copies]
    return plan


def _swap_plan(n_pairs):
    def plan(refs):
        x, y, c = _place()
        return [[(src.at[:, 1 - c], dst, (x, y, 1 - c), dst) for src, dst in zip(refs[:n_pairs], refs[n_pairs:])]]
    return plan


def _share_plan(n_pairs):
    def plan(refs):
        x, y, c = _place()
        return [[(src, dst, (x, y, 1 - c), dst)] for src, dst in zip(refs[:n_pairs], refs[n_pairs:])]
    return plan


def _scatter_plan(n_pairs):
    def plan(refs):
        x, y, c = _place()
        copies = []
        for src, dst in zip(refs[:n_pairs], refs[n_pairs:]):
            for k, (px, py) in enumerate(_other_chips(x, y)):
                copies.append((src.at[2 * px + py], dst.at[k], (px, py, c), dst.at[k]))
        return [copies]
    return plan


def _sibling_swap(arrs, name):
    n = len(arrs)

    def body(*refs):
        ins, outs, send, recv = refs[:n], refs[n:2 * n], refs[2 * n], refs[2 * n + 1]
        x, y, c = _place()
        cps = [_remote(ins[w].at[:, 1 - c], outs[w], send.at[w], recv.at[w], (x, y, 1 - c)) for w in range(n)]
        for cp in cps:
            cp.start()
        for cp in cps:
            cp.wait()

    return pl.pallas_call(
        body, name=name, in_specs=[ANY] * n, out_specs=[ANY] * n,
        out_shape=[jax.ShapeDtypeStruct((a.shape[0],) + a.shape[2:], a.dtype) for a in arrs],
        scratch_shapes=[pltpu.SemaphoreType.DMA((n,))] * 2,
    )(*arrs)


def _small_allreduce(buf, name, after=()):
    rows = buf.shape[0]

    def body(x_ref, *rest):
        out_ref, all_ref, send_sems, recv_sems, local_sem = rest[len(after):]
        x, y, c = _place()
        me, sibling, chips = (x, y, c), (x, y, 1 - c), _other_chips(x, y)

        def block(px, py, pc):
            return all_ref.at[pl.ds((4 * px + 2 * py + pc) * rows, rows), :]

        def copy(k, blk, to, src=None):
            return _remote(block(*blk) if src is None else src, block(*blk), send_sems.at[k], recv_sems.at[k], to)

        own = pltpu.make_async_copy(x_ref, block(*me), local_sem)
        own.start()
        first = [copy(0, me, sibling, src=x_ref)] + [copy(1 + j, me, (*chip, c), src=x_ref) for j, chip in enumerate(chips)]
        for cp in first:
            cp.start()
        passed = [copy(4 + j, (*chip, c), sibling) for j, chip in enumerate(chips)]
        for j, chip in enumerate(chips):
            copy(1 + j, (*chip, c), me).wait_recv()
            passed[j].start()
        copy(0, sibling, me).wait_recv()
        for j, chip in enumerate(chips):
            copy(4 + j, (*chip, 1 - c), me).wait_recv()
        for cp in first + passed:
            cp.wait_send()
        own.wait()
        acc = all_ref[pl.ds(0, rows), :]
        for d in range(1, 8):
            acc = acc + all_ref[pl.ds(d * rows, rows), :]
        out_ref[...] = acc

    vmem = pl.BlockSpec(memory_space=pltpu.VMEM)
    return pl.pallas_call(
        body, name=name, in_specs=[vmem] + [ANY] * len(after), out_specs=vmem,
        out_shape=jax.ShapeDtypeStruct(buf.shape, F32),
        scratch_shapes=[pltpu.VMEM((8 * rows, 128), F32), pltpu.SemaphoreType.DMA((7,)), pltpu.SemaphoreType.DMA((7,)),
                        pltpu.SemaphoreType.DMA],
    )(buf, *after)


ROW_TILE = 256
BIG_ROW_TILE = 1024


def _add_halves(arr, recv, c, name):
    _, _, hr, cols = arr.shape
    tr = _tile(hr, BIG_ROW_TILE)

    def body(c_ref, a_ref, r_ref, o_ref):
        o_ref[...] = (a_ref[...].astype(F32) + r_ref[...].astype(F32)).astype(o_ref.dtype)

    piece = pl.BlockSpec((None, tr, cols), lambda j, i, c_ref: (j, i, 0))
    grid_spec = pltpu.PrefetchScalarGridSpec(
        num_scalar_prefetch=1, grid=(N_CHIPS, hr // tr),
        in_specs=[pl.BlockSpec((None, None, tr, cols), lambda j, i, c_ref: (j, c_ref[0], i, 0)), piece], out_specs=piece)
    return pl.pallas_call(body, name=name, grid_spec=grid_spec, out_shape=jax.ShapeDtypeStruct(recv.shape, BF16),
                          compiler_params=_params(("parallel", "parallel")))(c.reshape(1).astype(jnp.int32), arr, recv)


def _flip_slot(d):
    return jnp.where(d == 1, 1, jnp.where(d == 3, 2, 0))


def _sum_chips(p, q, chip, name):
    _, hr, cols = p.shape
    tr = _tile(hr, BIG_ROW_TILE)

    def body(chip_ref, p_ref, q_ref, o_ref):
        j = pl.program_id(1)
        term = jnp.where(j == chip_ref[0], p_ref[...].astype(F32), q_ref[...].astype(F32))

        @pl.when(j == 0)
        def _():
            o_ref[...] = term

        @pl.when(j != 0)
        def _():
            o_ref[...] += term

    grid_spec = pltpu.PrefetchScalarGridSpec(
        num_scalar_prefetch=1, grid=(hr // tr, N_CHIPS),
        in_specs=[pl.BlockSpec((None, tr, cols), lambda i, j, chip_ref: (chip_ref[0], i, 0)),
                  pl.BlockSpec((None, tr, cols), lambda i, j, chip_ref: (_flip_slot(j ^ chip_ref[0]), i, 0))],
        out_specs=pl.BlockSpec((tr, cols), lambda i, j, chip_ref: (i, 0)))
    return pl.pallas_call(body, name=name, grid_spec=grid_spec, out_shape=jax.ShapeDtypeStruct((hr, cols), F32),
                          compiler_params=_params(("parallel", "arbitrary")))(chip.reshape(1).astype(jnp.int32), p, q)


def _adamw_halves(w, g_own, g_other, m, v, c, name):
    rows, cols = w.shape
    tr = _tile(rows // 2, ROW_TILE)
    per_half = rows // 2 // tr

    def body(c_ref, w_ref, own_ref, other_ref, m_ref, v_ref, g_ref, d_ref, nm_ref, nv_ref):
        mine = (pl.program_id(0) // per_half) == c_ref[0]
        g_ = jnp.where(mine, own_ref[...], other_ref[...])
        g_ref[...] = g_
        d_ref[...], nm_ref[...], nv_ref[...] = _adamw_math(w_ref[...], g_, m_ref[...], v_ref[...])

    blk = pl.BlockSpec((tr, cols), lambda i, c_ref: (i, 0))
    own = pl.BlockSpec((tr, cols), lambda i, c_ref: (jnp.where(i // per_half == c_ref[0], i % per_half, 0), 0))
    other = pl.BlockSpec((tr, cols), lambda i, c_ref: (jnp.where(i // per_half == c_ref[0], 0, i % per_half), 0))
    grid_spec = pltpu.PrefetchScalarGridSpec(num_scalar_prefetch=1, grid=(rows // tr,),
                                             in_specs=[blk, own, other, blk, blk], out_specs=[blk] * 4)
    return pl.pallas_call(body, name=name, grid_spec=grid_spec, out_shape=[jax.ShapeDtypeStruct(w.shape, F32)] * 4,
                          compiler_params=_params(("parallel",)))(c.reshape(1).astype(jnp.int32), w, g_own, g_other, m, v)


W_IN_COLS = (D_MAIN + N_DT) // N_CHIPS
W_IN_MAIN = W_IN_COLS // 128 * 128
W_IN_TAIL = W_IN_COLS - 128
W_IN_PARTS = ((0, W_IN_MAIN), (W_IN_TAIL, 128))


def _cast_w_in_transposed(w_t, chip, after=()):
    def body(chip_ref, w_ref, *rest):
        for start, size in W_IN_PARTS:
            rest[-1][:, pl.ds(start, size)] = w_ref[pl.ds(start, size), :].T.astype(BF16)

    grid_spec = pltpu.PrefetchScalarGridSpec(
        num_scalar_prefetch=1, grid=(D_MODEL // ROW_TILE,),
        in_specs=[pl.BlockSpec((W_IN_COLS, ROW_TILE), lambda i, chip_ref: (0, i))] + [pl.BlockSpec(memory_space=pl.ANY)] * len(after),
        out_specs=pl.BlockSpec((None, ROW_TILE, W_IN_COLS), lambda i, chip_ref: (chip_ref[0], i, 0)))
    return pl.pallas_call(body, name="cast_w_in", grid_spec=grid_spec,
                          out_shape=jax.ShapeDtypeStruct((N_CHIPS, D_MODEL, W_IN_COLS), BF16),
                          compiler_params=_params(("parallel",)))(chip.reshape(1).astype(jnp.int32), w_t, *after)


def _adamw_w_in_transposed(w_t, g_own, g_other, m_t, v_t, c):
    per_half = D_MODEL // 2 // ROW_TILE

    def body(c_ref, w_ref, own_ref, other_ref, m_ref, v_ref, g_ref, d_ref, nm_ref, nv_ref):
        mine = (pl.program_id(0) // per_half) == c_ref[0]
        for start, size in W_IN_PARTS:
            cols, rows = pl.ds(start, size), pl.ds(start, size)
            g_ = jnp.where(mine, own_ref[:, cols], other_ref[:, cols]).T
            g_ref[rows, :] = g_
            d_ref[rows, :], nm_ref[rows, :], nv_ref[rows, :] = _adamw_math(w_ref[rows, :], g_, m_ref[rows, :], v_ref[rows, :])

    blk = pl.BlockSpec((W_IN_COLS, ROW_TILE), lambda i, c_ref: (0, i))
    own = pl.BlockSpec((ROW_TILE, W_IN_COLS), lambda i, c_ref: (jnp.where(i // per_half == c_ref[0], i % per_half, 0), 0))
    other = pl.BlockSpec((ROW_TILE, W_IN_COLS), lambda i, c_ref: (jnp.where(i // per_half == c_ref[0], 0, i % per_half), 0))
    grid_spec = pltpu.PrefetchScalarGridSpec(num_scalar_prefetch=1, grid=(D_MODEL // ROW_TILE,),
                                             in_specs=[blk, own, other, blk, blk], out_specs=[blk] * 4)
    return pl.pallas_call(body, name="adamw_w_in", grid_spec=grid_spec, out_shape=[jax.ShapeDtypeStruct(w_t.shape, F32)] * 4,
                          compiler_params=_params(("parallel",)))(c.reshape(1).astype(jnp.int32), w_t, g_own, g_other, m_t, v_t)


def _adamw_math(w, g, m, v):
    m_new = ADAM_B1 * m + (1.0 - ADAM_B1) * g
    v_new = ADAM_B2 * v + (1.0 - ADAM_B2) * (g * g)
    m_hat = m_new / (1.0 - ADAM_B1 ** ADAM_STEP)
    v_hat = v_new / (1.0 - ADAM_B2 ** ADAM_STEP)
    return -ADAM_LR * (m_hat / (jnp.sqrt(v_hat) + ADAM_EPS) + ADAM_WD * w), m_new, v_new


VECTORS = ("g_mix", "g_q", "g_k", "g_attn_out", "conv_b", "dt_bias", "a_log", "d_skip", "g_ssm_out", "g_cross", "g_mem",
           "g_cq", "g_ck", "g_mlp")
WEIGHTS = ("g_mix", "w_in", "g_q", "g_k", "g_attn_out", "conv_w", "conv_b", "dt_bias", "a_log", "d_skip", "g_ssm_out", "w_out",
           "g_cross", "g_mem", "w_cq", "w_ckv", "g_cq", "g_ck", "w_co", "g_mlp", "w_up", "w_down")


def _pack(parts):
    flat = jnp.concatenate([t.reshape(-1) for t in parts])
    total = -(-flat.shape[0] // 1024) * 1024
    return jnp.pad(flat, (0, total - flat.shape[0])).reshape(total // 128, 128)


def _rows_of(n):
    return -(-n // 128)


def _slot_rows(n):
    return -(-n // 1024) * 8


def _pack_rows(parts):
    rows = []
    for t in parts:
        flat = t.reshape(-1)
        rows.append(jnp.pad(flat, (0, 128 * _slot_rows(flat.shape[0]) - flat.shape[0])).reshape(-1, 128))
    return jnp.concatenate(rows)


def _adamw_vectors(summed, chip, vectors, conv):
    groups = list(vectors) + [conv]
    offsets, row = [], 0
    for w, _, _ in groups:
        offsets.append(row)
        row += _slot_rows(w.shape[1]) if w.shape[0] == 1 else _slot_rows(w.shape[0] * N_CHIPS * w.shape[1])
    conv_blocks = _rows_of(conv[0].shape[1])

    def body(chip_ref, sum_ref, *refs):
        ins, outs = refs[:3 * len(groups)], refs[3 * len(groups):]

        def update(i, g, idx):
            w_ref, m_ref, v_ref = ins[3 * i:3 * i + 3]
            delta, new_m, new_v = _adamw_math(w_ref[idx], g, m_ref[idx], v_ref[idx])
            for o_ref, val in zip(outs[4 * i:4 * i + 4], (g, delta, new_m, new_v)):
                o_ref[idx] = val

        for i, (w, _, _) in enumerate(vectors):
            for t in range(_rows_of(w.shape[1])):
                width = min(128, w.shape[1] - 128 * t)
                update(i, sum_ref[pl.ds(offsets[i] + t, 1), pl.ds(0, width)], (slice(None), pl.ds(128 * t, width)))
        for tap in range(conv[0].shape[0]):
            for blk in range(conv_blocks):
                src = offsets[-1] + tap * N_CHIPS * conv_blocks + chip_ref[0] * conv_blocks + blk
                update(len(vectors), sum_ref[pl.ds(src, 1), :], (pl.ds(tap, 1), pl.ds(128 * blk, 128)))

    def whole(a):
        return pl.BlockSpec(a.shape, lambda i, chip_ref: (0,) * a.ndim)

    operands = [t for group in groups for t in group]
    grid_spec = pltpu.PrefetchScalarGridSpec(
        num_scalar_prefetch=1, grid=(1,), in_specs=[whole(summed)] + [whole(t) for t in operands],
        out_specs=[whole(w) for w, _, _ in groups for _ in range(4)])
    res = pl.pallas_call(body, name="adamw_vectors", grid_spec=grid_spec,
                         out_shape=[jax.ShapeDtypeStruct(w.shape, F32) for w, _, _ in groups for _ in range(4)],
                         compiler_params=_params(("arbitrary",)))(chip.reshape(1).astype(jnp.int32), summed, *operands)
    return [res[4 * i:4 * i + 4] for i in range(len(groups))]


def _unpack(buf, shapes):
    flat, out, pos = buf.reshape(-1), [], 0
    for shape in shapes:
        size = math.prod(shape)
        out.append(flat[pos:pos + size].reshape(shape))
        pos += size
    return out


def kernel(x, mem, positions, g_mix, w_in, g_q, g_k, g_attn_out, conv_w, conv_b, dt_bias, a_log, d_skip, g_ssm_out, w_out, g_cross, g_mem, w_cq, w_ckv, g_cq, g_ck, w_co, g_mlp, w_up, w_down, loss_target, m_g_mix, m_w_in, m_g_q, m_g_k, m_g_attn_out, m_conv_w, m_conv_b, m_dt_bias, m_a_log, m_d_skip, m_g_ssm_out, m_w_out, m_g_cross, m_g_mem, m_w_cq, m_w_ckv, m_g_cq, m_g_ck, m_w_co, m_g_mlp, m_w_up, m_w_down, v_g_mix, v_w_in, v_g_q, v_g_k, v_g_attn_out, v_conv_w, v_conv_b, v_dt_bias, v_a_log, v_d_skip, v_g_ssm_out, v_w_out, v_g_cross, v_g_mem, v_w_cq, v_w_ckv, v_g_cq, v_g_ck, v_w_co, v_g_mlp, v_w_up, v_w_down):
    args = dict(locals())
    weights = {n: args[n][0] for n in WEIGHTS}
    mom_m = {n: args["m_" + n][0] for n in WEIGHTS}
    mom_v = {n: args["v_" + n][0] for n in WEIGHTS}
    x_idx, y_idx, c_idx = _place()
    chip = 2 * x_idx + y_idx

    shapes = {n: weights[n].shape for n in MATRICES}
    first, mid, late = ("w_in",), ("w_out", "w_cq", "w_ckv", "w_co"), ("w_up", "w_down")
    w_in_t, m_in_t, v_in_t = (jnp.swapaxes(t, 1, 2)[0] for t in (w_in, m_w_in, v_w_in))
    w_in_buf = [_cast_w_in_transposed(w_in_t, chip)]
    taps, tap_cols = weights["conv_w"].shape
    conv_parts = _small_allreduce(_pack([jnp.zeros((N_CHIPS, taps, tap_cols), F32).at[chip].set(0.5 * weights["conv_w"])]),
                                  "gather_conv_taps")
    sems_in, w_in_buf, token = _split_start("gather_ici_start_w_in", w_in_buf, _ici_plan(first, shapes), [3], after=(conv_parts,))
    bufs = [_cast_into_gathered(weights[n], n, chip, after=(token,)) for n in mid + late]
    plan = lambda refs: _ici_plan(mid, shapes)(refs[:4]) + _ici_plan(late, shapes)(refs[4:])
    sems_rest, bufs, token = _split_start("gather_ici_start_rest", bufs, plan, [12, 6], after=(token,))
    params = {n: weights[n].reshape(1, -1) for n in VECTORS}
    h_in = _rowwise(_norm_fn, [_full(x[0])], [_full(params["g_mix"])], [(D_MODEL, BF16, D_MODEL, 0, False)], name="norm_in",
                    after=(token,))[0]
    w_in_buf = _split_wait("gather_ici_wait_w_in", w_in_buf, sems_in[0], _ici_plan(first, shapes), token, h_in, m_in_t, v_in_t)
    pass_sems, w_in_buf, token = _split_start("gather_pass_start_w_in", w_in_buf, _pass_on_plan(first, shapes), [3])
    w_in_buf = _split_wait("gather_pass_wait_w_in", w_in_buf, pass_sems[0], _pass_on_plan(first, shapes), token)
    w_in_full = _w_in_columns(w_in_buf[0], to_shards=False)
    full = {"w_in": w_in_full,
            "w_dt": jnp.pad(w_in_full[:, D_MAIN:].reshape(D_MODEL, N_GROUPS, HEADS_PER_GROUP),
                            ((0, 0), (0, 0), (0, 128 - HEADS_PER_GROUP))).reshape(D_MODEL, DT_PAD)}
    in_flight = {}

    def more_weights(stage, after):
        if stage == "mixer_done":
            got = _split_wait("gather_ici_wait_mid", bufs[:4], sems_rest[0], _ici_plan(mid, shapes), after)
            sems, got, token = _split_start("gather_pass_start_mid", got, _pass_on_plan(mid, shapes), [12])
            return dict(zip(mid, _split_wait("gather_pass_wait_mid", got, sems[0], _pass_on_plan(mid, shapes), token)))
        if stage == "cross_started":
            got = _split_wait("gather_ici_wait_late", bufs[4:], sems_rest[1], _ici_plan(late, shapes), after)
            in_flight["late"] = _split_start("gather_pass_start_late", got, _pass_on_plan(late, shapes), [6])
            return {}
        sems, got, token = in_flight.pop("late")
        return dict(zip(late, _split_wait("gather_pass_wait_late", got, sems[0], _pass_on_plan(late, shapes), token, after)))

    params["conv_w"] = _unpack(conv_parts, [(N_CHIPS, taps, tap_cols)])[0].transpose(1, 0, 2).reshape(taps, N_CHIPS * tap_cols)

    groups = (("w_down",), ("w_up",), ("w_co", "w_cq", "w_ckv", "w_out"), ("w_in",))
    scattered = []

    class GradStore(dict):
        pending = None

        def __setitem__(self, name, value):
            super().__setitem__(name, value)
            if "w_main" in self and "w_dt" in self and "w_in" not in self:
                gw_in = lax.dynamic_update_slice(self["w_main"], _unpad_heads(self["w_dt"]), (0, D_MAIN))
                self["w_in"] = _w_in_columns(gw_in, to_shards=True)
            for group in groups:
                if name in group and all(n in self for n in group):
                    self.settle()
                    pieces = [self[n].reshape(N_CHIPS, 2, shapes[n][0] // 2, shapes[n][1]) for n in group]
                    if group == groups[-1]:
                        self.scatter(group, pieces, _sibling_swap(pieces, "grad_swap_" + group[0]))
                    else:
                        landing = [lax.empty((N_CHIPS,) + a.shape[2:], BF16) for a in pieces]
                        sems, thru, self.token = _split_start("grad_swap_start_" + group[0], pieces + landing,
                                                              _swap_plan(len(pieces)), [len(pieces)])
                        self.pending = (group, sems[0], thru)

        def settle(self, *after):
            if self.pending is not None:
                group, sems, thru = self.pending
                self.pending = None
                thru = _split_wait("grad_swap_wait_" + group[0], thru, sems, _swap_plan(len(group)), *after)
                self.scatter(group, thru[:len(group)], thru[len(group):])

        def scatter(self, group, pieces, from_sibling):
            sums = [_add_halves(a, r, c_idx, "add_halves_" + n) for n, a, r in zip(group, pieces, from_sibling)]
            landing = [lax.empty((3,) + s.shape[1:], BF16) for s in sums]
            sems, thru, self.token = _split_start("grad_scatter_start_" + group[0], sums + landing,
                                                  _scatter_plan(len(sums)), [3 * len(sums)])
            scattered.append((group, sems[0], thru))

    loss, grad_x, grads = _local_step(x[0], mem[0], positions[0], loss_target[0], params, full, more_weights, GradStore(),
                                      h_in)

    out_g, out_d, out_m, out_v = {}, {}, {}, {}

    def finish(entries, order, token):
        halves = {}
        for group, sems, thru in entries:
            thru = _split_wait("grad_scatter_wait_" + group[0], thru, sems, _scatter_plan(len(group)), token)
            for i, n in enumerate(group):
                halves[n] = _sum_chips(thru[i], thru[len(group) + i], chip, "sum_chips_" + n)
        sources = [halves[n] for n in order]
        landing = [lax.empty(s.shape, F32) for s in sources]
        sems, thru, token = _split_start("grad_share_start_" + order[0], sources + landing, _share_plan(len(order)),
                                         [1] * len(order))
        for i, n in enumerate(order):
            own, other = _split_wait("grad_share_wait_" + n, [thru[i], thru[len(order) + i]], sems[i], _share_plan(1), token)
            if n == "w_in":
                res_t = _adamw_w_in_transposed(w_in_t, own, other, m_in_t, v_in_t, c_idx)
                out_g[n], out_d[n], out_m[n], out_v[n] = (t.T for t in res_t)
            else:
                out_g[n], out_d[n], out_m[n], out_v[n] = _adamw_halves(weights[n], own, other, mom_m[n], mom_v[n], c_idx,
                                                                       "adamw_" + n)
            token = out_v[n]
        return token

    token = finish(scattered[:-1], ("w_cq", "w_co", "w_ckv", "w_out", "w_up", "w_down"), grad_x)
    finish(scattered[-1:], ("w_in",), token)

    names = VECTORS + ("conv_w",)
    summed = _small_allreduce(_pack_rows([grads[n] for n in names] + [loss]), "allreduce_vectors")
    total_loss = summed[sum(_slot_rows(grads[n].size) for n in names), 0]
    small_out = _adamw_vectors(summed, chip, [(args[n], args["m_" + n], args["v_" + n]) for n in VECTORS],
                               (weights["conv_w"], mom_m["conv_w"], mom_v["conv_w"]))
    for n, res in zip(names, small_out):
        out_g[n], out_d[n], out_m[n], out_v[n] = (t.reshape(weights[n].shape) for t in res)

    outs =[total_loss, grad_x[None]]
    for group in (out_g, out_d, out_m, out_v):
        outs += [group[n][None] for n in WEIGHTS]
    return tuple(outs)
```

```python
import functools
import math

import jax
import jax.numpy as jnp
from jax import lax
from jax.experimental import pallas as pl
from jax.experimental.pallas import tpu as pltpu

F32 = jnp.float32
BF16 = jnp.bfloat16

SEQ = 2048
D_MODEL = 2048
HEAD = 64
D_ATTN = 1024
D_SSM = 1024
N_GROUPS = 4
N_STATE = 128
CHUNK = 128
ATT_BLK = 128
N_MEM = 256
D_CROSS = 512
D_MAIN = 6144
N_DT = 16
DT_PAD = 512
ROT = 16
ROPE_THETA = 500000.0
EPS = 1e-6
NEG = -1e30
BRANCH_BLOCKS = (16, 4, 1)
DILATIONS = (1, 4, 16)

ADAM_LR, ADAM_B1, ADAM_B2, ADAM_EPS, ADAM_WD, ADAM_STEP = 0.001, 0.9, 0.999, 1e-08, 0.01, 10

VMEM_LIMIT = 56 * 1024 * 1024
MESH = pl.DeviceIdType.MESH


def _params(sem, **kw):
    return pltpu.CompilerParams(dimension_semantics=sem, vmem_limit_bytes=VMEM_LIMIT, **kw)


def _bdot(a, b, dims):
    return lax.dot_general(a.astype(BF16), b.astype(BF16), (dims, ((), ())), preferred_element_type=F32)


def _fdot(a, b, dims):
    return lax.dot_general(a, b, (dims, ((), ())), preferred_element_type=F32, precision=lax.Precision.HIGHEST)


NN = ((1,), (0,))
NT = ((1,), (1,))
TN = ((0,), (0,))


def _tile(n, want):
    t = min(n, want)
    while n % t:
        t //= 2
    return t


def _matmul(a, b, *, mode, name, outs, extra=(), vecs=(), epilogue=None, col_shards=1, after=(), n_cols=None, out_cols=None,
            tile_rows=0, tile_sums=0, tm=1024, tn=1024, tk=2048):
    if mode == "nn":
        (m, k), n = a.shape, b.shape[1]
    elif mode == "nt":
        (m, k), n = a.shape, b.shape[0]
    else:
        (k, m), n = a.shape, b.shape[1]
    n = n if n_cols is None else n_cols
    tm, tn, tk = _tile(m, tm), _tile(n // col_shards, tn), _tile(k, tk)
    nk = k // tk
    per_shard = n // col_shards // tn
    dims = {"nn": NN, "nt": NT, "tn": TN}[mode]
    a_spec = pl.BlockSpec((tk, tm), lambda i, j, kk: (kk, i)) if mode == "tn" else pl.BlockSpec((tm, tk), lambda i, j, kk: (i, kk))
    b_spec = pl.BlockSpec((tn, tk), lambda i, j, kk: (j, kk)) if mode == "nt" else pl.BlockSpec((tk, tn), lambda i, j, kk: (kk, j))
    o_spec = pl.BlockSpec((tm, tn), lambda i, j, kk: (i, j))
    n_extra, n_out, n_after = len(extra) + len(vecs), len(outs), len(after)

    def body(a_ref, b_ref, *rest):
        extra_refs, out_refs, acc_ref = rest[:n_extra], rest[n_extra + n_after:-1], rest[-1]

        def finish(acc):
            res = (acc,) if epilogue is None else epilogue(acc, *[e[...] for e in extra_refs])
            for o_ref, r in zip(out_refs[:n_out], res):
                o_ref[...] = r.astype(o_ref.dtype)
            for o_ref, r in zip(out_refs[n_out:], res[n_out:]):
                o_ref[...] = jnp.broadcast_to(r, o_ref.shape)

        if nk == 1:
            finish(_bdot(a_ref[...], b_ref[...], dims))
            return
        kk = pl.program_id(2)

        @pl.when(kk == 0)
        def _():
            acc_ref[...] = jnp.zeros_like(acc_ref)

        acc_ref[...] += _bdot(a_ref[...], b_ref[...], dims)

        @pl.when(kk == nk - 1)
        def _():
            finish(acc_ref[...])

    if col_shards == 1:
        out_specs, out_dims = [o_spec] * n_out, (m, n if out_cols is None else out_cols)
    else:
        sharded = pl.BlockSpec((None, tm, tn), lambda i, j, kk: (j // per_shard, i, j % per_shard))
        out_specs, out_dims = [sharded] * n_out, (col_shards, m, n // col_shards)
    res = pl.pallas_call(
        body, name=name, grid=(m // tm, n // tn, nk),
        in_specs=[a_spec, b_spec] + [o_spec] * len(extra) + [pl.BlockSpec((1, tn), lambda i, j, kk: (0, j))] * len(vecs)
        + [pl.BlockSpec(memory_space=pl.ANY)] * n_after,
        out_specs=out_specs + [pl.BlockSpec((8, tn), lambda i, j, kk: (i, j))] * tile_rows
        + [pl.BlockSpec((8, 128), lambda i, j, kk: (i, j))] * tile_sums,
        out_shape=[jax.ShapeDtypeStruct(out_dims, dt) for dt in outs] + [jax.ShapeDtypeStruct((m // tm * 8, n), F32)] * tile_rows
        + [jax.ShapeDtypeStruct((m // tm * 8, n // tn * 128), F32)] * tile_sums,
        scratch_shapes=[pltpu.VMEM((tm, tn) if nk > 1 else (8, 128), F32)],
        compiler_params=_params(("parallel", "parallel", "arbitrary")),
    )(a, b, *extra, *vecs, *after)
    res = (list(res[:n_out]) + [jnp.sum(t[::8], axis=0, keepdims=True) for t in res[n_out:n_out + tile_rows]]
           + [t[::8, ::128] for t in res[n_out + tile_rows:]])
    return res[0] if len(res) == 1 else res


def _row_spec(tr, bw, cb, per_group):
    return pl.BlockSpec((tr, bw), (lambda g, i: (i, cb + g)) if per_group else (lambda g, i: (i, cb)))


def _vec_spec(bw, cb, per_group):
    return pl.BlockSpec((1, bw), (lambda g, i: (0, cb + g)) if per_group else (lambda g, i: (0, cb)))


def _rowwise(fn, rows, vecs, outs, *, name, n_rows=SEQ, tr=512, groups=1, after=()):
    n_r, n_v, n_after = len(rows), len(vecs), len(after)

    def body(*refs):
        vals = [r[...].astype(F32) for r in refs[:n_r + n_v]]
        res = fn(*vals)
        for o_ref, r in zip(refs[n_r + n_v + n_after:], res):
            o_ref[...] = r.astype(o_ref.dtype)

    res = pl.pallas_call(
        body, name=name, grid=(groups, n_rows // tr),
        in_specs=[_row_spec(tr, bw, cb, pg) for _, bw, cb, pg in rows] + [_vec_spec(bw, cb, pg) for _, bw, cb, pg in vecs]
        + [pl.BlockSpec(memory_space=pl.ANY)] * n_after,
        out_specs=[_row_spec(tr, bw, cb, pg) for _, _, bw, cb, pg in outs],
        out_shape=[jax.ShapeDtypeStruct((n_rows, w), dt) for w, dt, _, _, _ in outs],
        compiler_params=_params(("parallel", "parallel")),
    )(*[r[0] for r in rows], *[v[0] for v in vecs], *after)
    return res


def _rowwise_vjp(fn, rows, vecs, cts, row_grads, vec_grads, *, name, n_rows=SEQ, tr=512, groups=1, after=()):
    n_r, n_v, n_after = len(rows), len(vecs), len(after)
    ct_ops = [op for group in cts for op in group]
    ct_sizes = [len(group) for group in cts]
    res_ops = [g[6] for g in row_grads if g[6] is not None]
    n_ct, n_res, n_rg = len(ct_ops), len(res_ops), len(row_grads)

    def body(*refs):
        vals = [r[...].astype(F32) for r in refs[:n_r + n_v]]
        pos = n_r + n_v
        ct_vals = []
        for size in ct_sizes:
            acc = refs[pos][...].astype(F32)
            for t in range(1, size):
                acc = acc + refs[pos + t][...].astype(F32)
            ct_vals.append(acc)
            pos += size
        res_refs = refs[pos:pos + n_res]
        out_refs = refs[pos + n_res + n_after:]
        _, pullback = jax.vjp(fn, *vals)
        grads = pullback(tuple(ct_vals))
        r_i = 0
        for o_ref, g in zip(out_refs[:n_rg], row_grads):
            val = grads[g[0]]
            if g[6] is not None:
                val = val + res_refs[r_i][...].astype(F32)
                r_i += 1
            o_ref[...] = val.astype(o_ref.dtype)
        first = (pl.program_id(1) == 0)
        for o_ref, g in zip(out_refs[n_rg:], vec_grads):
            val = jnp.sum(grads[n_r + g[0]], axis=0, keepdims=True)
            init = first if g[4] else jnp.logical_and(first, pl.program_id(0) == 0)

            @pl.when(init)
            def _(o_ref=o_ref, val=val):
                o_ref[...] = val

            @pl.when(jnp.logical_not(init))
            def _(o_ref=o_ref, val=val):
                o_ref[...] += val

    in_specs = [_row_spec(tr, bw, cb, pg) for _, bw, cb, pg in rows] + [_vec_spec(bw, cb, pg) for _, bw, cb, pg in vecs]
    in_specs += [_row_spec(tr, bw, cb, pg) for _, bw, cb, pg in ct_ops + res_ops] + [pl.BlockSpec(memory_space=pl.ANY)] * n_after
    out_specs =[_row_spec(tr, g[3], g[4], g[5]) for g in row_grads] + [_vec_spec(g[2], g[3], g[4]) for g in vec_grads]
    out_shape = [jax.ShapeDtypeStruct((n_rows, g[1]), g[2]) for g in row_grads]
    out_shape += [jax.ShapeDtypeStruct((1, g[1]), F32) for g in vec_grads]
    return pl.pallas_call(
        body, name=name, grid=(groups, n_rows // tr),
        in_specs=in_specs, out_specs=out_specs, out_shape=out_shape,
        compiler_params=_params(("arbitrary", "arbitrary")),
    )(*[r[0] for r in rows], *[v[0] for v in vecs], *[c[0] for c in ct_ops], *[r[0] for r in res_ops], *after)


def _full(arr, width=None):
    return (arr, arr.shape[1] if width is None else width, 0, False)


def _make_xor(sh):
    def raw(x):
        n = x.shape[-1]
        lane = lax.broadcasted_iota(jnp.int32, x.shape, x.ndim - 1)
        up = pltpu.roll(x, n - sh, x.ndim - 1)
        down = pltpu.roll(x, sh, x.ndim - 1)
        return jnp.where((lane & sh) == 0, up, down)

    f = jax.custom_vjp(raw)
    f.defvjp(lambda x: (raw(x), None), lambda _, ct: (raw(ct),))
    return f


_SWAP_ROPE_HALVES = _make_xor(ROT // 2)


def _head_sum(x):
    n = x.shape[-1]
    same_head = (lax.broadcasted_iota(jnp.int32, (n, n), 0) // HEAD) == (lax.broadcasted_iota(jnp.int32, (n, n), 1) // HEAD)
    return _fdot(x, same_head.astype(F32), NN)


def _rms(x, g):
    return x * lax.rsqrt(jnp.mean(x * x, axis=-1, keepdims=True) + EPS) * g


def _head_rms_rope(x, g, cos, sin, scale):
    y = x * lax.rsqrt(_head_sum(x * x) * (1.0 / HEAD) + EPS) * g
    return (y * cos + _SWAP_ROPE_HALVES(y) * sin) * scale


def _qk_fn(q, k, v, cos, sin, gq, gk):
    return (_head_rms_rope(q, gq, cos, sin, HEAD ** -0.5), _head_rms_rope(k, gk, cos, sin, 1.0), v)


def _norm_fn(x, g):
    return (_rms(x, g),)


def _merge_fn(o0, o1, o2, l0, l1, l2, g):
    m = lax.stop_gradient(jnp.maximum(jnp.maximum(l0, l1), l2))
    e0, e1, e2 = jnp.exp(l0 - m), jnp.exp(l1 - m), jnp.exp(l2 - m)
    mix = (e0 * o0 + e1 * o1 + e2 * o2) / (e0 + e1 + e2)
    return (_rms(mix, g),)


def _gate_fn(y, z, g):
    return (_rms(y * (z * jax.nn.sigmoid(z)), g),)


def _attn_pair(q, kc, vc, kp=None, vp=None, has_prev=None):
    pick0, pick1 = _head_picks()
    k_band, v_band, mask = _attn_band(kc, vc, kp, vp, has_prev)
    s = jnp.where(mask, _bdot(jnp.concatenate([q * pick0, q * pick1], axis=0), k_band, NT), NEG)
    m = jnp.max(s, axis=-1, keepdims=True)
    p = jnp.exp(s - m)
    den = jnp.sum(p, axis=-1, keepdims=True)
    acc = _bdot(p, v_band, NN) * (1.0 / den)
    lse_rows = m + jnp.log(den)
    o = pick0 * acc[:ATT_BLK] + pick1 * acc[ATT_BLK:]
    lse = pick0 * lse_rows[:ATT_BLK] + pick1 * lse_rows[ATT_BLK:]
    return o, lse


def _head_picks():
    lane = lax.broadcasted_iota(jnp.int32, (1, 2 * HEAD), 1)
    return (lane < HEAD).astype(F32), (lane >= HEAD).astype(F32)


def _attn_band(kc, vc, kp, vp, has_prev):
    n_keys = ATT_BLK if kp is None else 2 * ATT_BLK
    qi = lax.broadcasted_iota(jnp.int32, (2 * ATT_BLK, n_keys), 0) & (ATT_BLK - 1)
    kj = lax.broadcasted_iota(jnp.int32, (2 * ATT_BLK, n_keys), 1)
    if kp is None:
        return kc, vc, qi >= kj
    in_prev = jnp.logical_and(jnp.logical_and(kj < ATT_BLK, kj >= qi), has_prev)
    mask = jnp.logical_or(in_prev, jnp.logical_and(kj >= ATT_BLK, qi >= kj - ATT_BLK))
    return jnp.concatenate([kp, kc], axis=0), jnp.concatenate([vp, vc], axis=0), mask


def _attn_config(b):
    r = DILATIONS[b]
    return r, ATT_BLK * r, (D_ATTN if r == 1 else 128), BRANCH_BLOCKS[b] > 1


RESIDUES_UNROLLED = 8


def _for_residues(r, fn):
    if r <= RESIDUES_UNROLLED:
        for rho in range(r):
            fn(rho)
    else:
        def step(t, carry):
            for u in range(RESIDUES_UNROLLED):
                fn(RESIDUES_UNROLLED * t + u)
            return carry

        lax.fori_loop(0, r // RESIDUES_UNROLLED, step, 0)


def _strided_rows(start, r):
    if r > 1:
        return pl.ds(start, ATT_BLK, stride=r)
    return pl.ds(start if isinstance(start, int) else pl.multiple_of(start, ATT_BLK), ATT_BLK)


def _attention_fwd(qn, kn, vn, b):
    r, rows, lanes, with_prev = _attn_config(b)
    cur = pl.BlockSpec((rows, lanes), lambda g, n: (n, g))
    prev = pl.BlockSpec((rows, lanes), lambda g, n: (jnp.maximum(n - 1, 0), g))

    def body(*refs):
        ins, (o_ref, l_ref) = refs[:-2], refs[-2:]
        has_prev = pl.program_id(1) > 0

        def one(rho):
            sub = _strided_rows(rho, r)
            for pair in range(lanes // 128):
                sl = pl.ds(pair * 128, 128)
                args = [ref[sub, sl] for ref in ins] + ([has_prev] if with_prev else [])
                o_ref[sub, sl], l_ref[sub, sl] = _attn_pair(*args)

        _for_residues(r, one)

    operands = (qn, kn, vn, kn, vn) if with_prev else (qn, kn, vn)
    return pl.pallas_call(
        body, name="attn_fwd_%d" % r, grid=(D_ATTN // lanes, SEQ // rows),
        in_specs=[cur, cur, cur] + ([prev, prev] if with_prev else []), out_specs=[cur, cur],
        out_shape=[jax.ShapeDtypeStruct((SEQ, D_ATTN), F32)] * 2,
        compiler_params=_params(("parallel", "parallel")),
    )(*operands)


def _attn_pair_bwd(q, kc, vc, kp, vp, o, lse, do, dl, has_prev):
    pick0, pick1 = _head_picks()
    lane = lax.broadcasted_iota(jnp.int32, (1, 2 * HEAD), 1)
    k_band, v_band, mask = _attn_band(kc, vc, kp, vp, has_prev)
    q2 = jnp.concatenate([q * pick0, q * pick1], axis=0)
    do2 = jnp.concatenate([do * pick0, do * pick1], axis=0)
    lse2 = jnp.concatenate([jnp.sum(lse * (lane == 0).astype(F32), axis=-1, keepdims=True),
                            jnp.sum(lse * (lane == HEAD).astype(F32), axis=-1, keepdims=True)], axis=0)
    base = jnp.sum(jnp.concatenate([dl * pick0, dl * pick1], axis=0) - do2 * jnp.concatenate([o, o], axis=0),
                   axis=-1, keepdims=True)
    p = jnp.exp(jnp.where(mask, _bdot(q2, k_band, NT), NEG) - lse2)
    ds = p * (_bdot(do2, v_band, NT) + base)
    dq2 = _bdot(ds, k_band, NN)
    dq = pick0 * dq2[:ATT_BLK] + pick1 * dq2[ATT_BLK:]
    dk, dv = _bdot(ds, q2, TN), _bdot(p, do2, TN)
    if kp is None:
        return dq, dk, dv
    return dq, dk[ATT_BLK:], dv[ATT_BLK:], dk[:ATT_BLK], dv[:ATT_BLK]


def _attention_bwd(qn, kn, vn, o, lse, do, dl, b):
    r, rows, lanes, with_prev = _attn_config(b)
    cur = pl.BlockSpec((rows, lanes), lambda g, n: (n, g))
    prev = pl.BlockSpec((rows, lanes), lambda g, n: (jnp.maximum(n - 1, 0), g))
    whole = pl.BlockSpec((SEQ, lanes), lambda g, n: (0, g))
    n_in = 5 if with_prev else 3

    def body(*refs):
        ins, (o_ref, l_ref, do_ref, dl_ref, dq_ref, dk_ref, dv_ref) = refs[:n_in], refs[n_in:]
        n = pl.program_id(1)

        @pl.when(n == 0)
        def _():
            dk_ref[...] = jnp.zeros_like(dk_ref)
            dv_ref[...] = jnp.zeros_like(dv_ref)

        def one(rho):
            sub = _strided_rows(rho, r)
            sub_c = _strided_rows(n * rows + rho, r)
            sub_p = _strided_rows(jnp.maximum(n - 1, 0) * rows + rho, r)
            for pair in range(lanes // 128):
                sl = pl.ds(pair * 128, 128)
                vals = [ref[sub, sl] for ref in ins] + ([] if with_prev else [None, None])
                grads = _attn_pair_bwd(*vals, o_ref[sub, sl], l_ref[sub, sl], do_ref[sub, sl], dl_ref[sub, sl], n > 0)
                dq_ref[sub, sl] = grads[0]
                dk_ref[sub_c, sl] += grads[1]
                dv_ref[sub_c, sl] += grads[2]
                if with_prev:
                    dk_ref[sub_p, sl] += grads[3]
                    dv_ref[sub_p, sl] += grads[4]

        _for_residues(r, one)

    operands = (qn, kn, vn, kn, vn) if with_prev else (qn, kn, vn)
    return pl.pallas_call(
        body, name="attn_bwd_%d" % r, grid=(D_ATTN // lanes, SEQ // rows),
        in_specs=[cur, cur, cur] + ([prev, prev] if with_prev else []) + [cur] * 4, out_specs=[cur, whole, whole],
        out_shape=[jax.ShapeDtypeStruct((SEQ, D_ATTN), F32)] * 3,
        compiler_params=_params(("parallel", "arbitrary")),
    )(*operands, o, lse, do, dl)


CONV_COLS = 256
XBC_BLOCK0 = (3 * D_ATTN + D_SSM) // CONV_COLS


def _shift_rows(x, s):
    n = x.shape[0]
    t = lax.broadcasted_iota(jnp.int32, x.shape, 0)
    if s >= 0:
        return jnp.where(t >= s, pltpu.roll(x, s, 0), 0.0)
    return jnp.where(t < n + s, pltpu.roll(x, n + s, 0), 0.0)


def _conv_pre(x, w_ref, b_ref):
    delayed = [_shift_rows(x, 3 - k) for k in range(3)]
    pre = b_ref[...] + w_ref[3:4, :] * x
    for k in range(3):
        pre = pre + w_ref[k:k + 1, :] * delayed[k]
    return pre, delayed


def _conv_fwd(proj, conv_w, conv_b):
    cols = conv_w.shape[1]

    def body(x_ref, w_ref, b_ref, o_ref):
        pre, _ = _conv_pre(x_ref[...], w_ref, b_ref)
        o_ref[...] = pre * jax.nn.sigmoid(pre)

    blk = pl.BlockSpec((SEQ, CONV_COLS), lambda j: (0, j))
    return pl.pallas_call(
        body, name="conv_fwd", grid=(cols // CONV_COLS,),
        in_specs=[pl.BlockSpec((SEQ, CONV_COLS), lambda j: (0, XBC_BLOCK0 + j)),
                  pl.BlockSpec((4, CONV_COLS), lambda j: (0, j)), pl.BlockSpec((1, CONV_COLS), lambda j: (0, j))],
        out_specs=blk, out_shape=jax.ShapeDtypeStruct((SEQ, cols), F32),
        compiler_params=_params(("parallel",)),
    )(proj, conv_w, conv_b)


def _conv_bwd(proj, conv_w, conv_b, dxs, db, dc):
    cols = conv_w.shape[1]
    x_blocks, b_blocks = dxs.shape[1] // CONV_COLS, db.shape[1] // CONV_COLS

    def body(x_ref, w_ref, b_ref, dxs_ref, db_ref_in, dc_ref_in, dx_ref, dw_ref, db_ref):
        j = pl.program_id(0)
        dy = jnp.where(j < x_blocks, dxs_ref[...], jnp.where(j < x_blocks + b_blocks, db_ref_in[...], dc_ref_in[...]))
        x = x_ref[...]
        pre, delayed = _conv_pre(x, w_ref, b_ref)
        sg = jax.nn.sigmoid(pre)
        dpre = dy * (sg * (1.0 + pre * (1.0 - sg)))
        db_ref[...] = jnp.sum(dpre, axis=0, keepdims=True)
        dx = w_ref[3:4, :] * dpre
        dw_ref[3:4, :] = jnp.sum(dpre * x, axis=0, keepdims=True)
        for k in range(3):
            dx = dx + w_ref[k:k + 1, :] * _shift_rows(dpre, k - 3)
            dw_ref[k:k + 1, :] = jnp.sum(dpre * delayed[k], axis=0, keepdims=True)
        dw_ref[4:8, :] = jnp.zeros((4, CONV_COLS), F32)
        dx_ref[...] = dx.astype(dx_ref.dtype)

    blk = pl.BlockSpec((SEQ, CONV_COLS), lambda j: (0, j))
    parts = [pl.BlockSpec((SEQ, CONV_COLS), lambda j: (0, jnp.minimum(j, x_blocks - 1))),
             pl.BlockSpec((SEQ, CONV_COLS), lambda j: (0, jnp.clip(j - x_blocks, 0, b_blocks - 1))),
             pl.BlockSpec((SEQ, CONV_COLS), lambda j: (0, jnp.clip(j - x_blocks - b_blocks, 0, b_blocks - 1)))]
    return pl.pallas_call(
        body, name="conv_bwd", grid=(cols // CONV_COLS,),
        in_specs=[pl.BlockSpec((SEQ, CONV_COLS), lambda j: (0, XBC_BLOCK0 + j)),
                  pl.BlockSpec((4, CONV_COLS), lambda j: (0, j)), pl.BlockSpec((1, CONV_COLS), lambda j: (0, j))] + parts,
        out_specs=[blk, pl.BlockSpec((8, CONV_COLS), lambda j: (0, j)), pl.BlockSpec((1, CONV_COLS), lambda j: (0, j))],
        out_shape=[jax.ShapeDtypeStruct((SEQ, cols), BF16), jax.ShapeDtypeStruct((8, cols), F32),
                   jax.ShapeDtypeStruct((1, cols), F32)],
        compiler_params=_params(("parallel",)),
    )(proj, conv_w, conv_b, dxs, db, dc)


HEADS_PER_GROUP = 4


GROUP_WIDTH = HEADS_PER_GROUP * HEAD


def _ssd_chunk(x, bm, cm, dtr, bias, alog, dsk, h):
    row = lax.broadcasted_iota(jnp.int32, (CHUNK, CHUNK), 0)
    col = lax.broadcasted_iota(jnp.int32, (CHUNK, CHUNK), 1)
    causal = row >= col
    z = dtr + bias
    dt = jnp.maximum(z, 0.0) + jnp.log(1.0 + jnp.exp(-jnp.abs(z)))
    acs = _fdot(causal.astype(F32), dt * -jnp.exp(alog), NN)
    acs_t, dt_t = acs.T, dt.T
    cb = _bdot(cm, bm, NT)
    lane = lax.broadcasted_iota(jnp.int32, (1, CHUNK), 1)
    sub = lax.broadcasted_iota(jnp.int32, (CHUNK, 1), 0)
    wide = lax.broadcasted_iota(jnp.int32, (1, GROUP_WIDTH), 1) // HEAD
    tall = lax.broadcasted_iota(jnp.int32, (GROUP_WIDTH, 1), 0) // HEAD
    acs_last = jnp.sum(acs * (sub == CHUNK - 1).astype(F32), axis=0, keepdims=True)
    to_lanes = (lax.broadcasted_iota(jnp.int32, (CHUNK, GROUP_WIDTH), 0)
                == lax.broadcasted_iota(jnp.int32, (CHUNK, GROUP_WIDTH), 1) // HEAD).astype(F32)
    grow = _fdot(jnp.exp(acs), to_lanes, NN)
    keep = _fdot(jnp.exp(acs_last - acs) * dt, to_lanes, NN)
    w_parts, x_parts, skip, carry = [], [], 0.0, 0.0
    for j in range(HEADS_PER_GROUP):
        on_lane, on_sub = (lane == j).astype(F32), (sub == j).astype(F32)
        acs_c = jnp.sum(acs * on_lane, axis=1, keepdims=True)
        acs_r = jnp.sum(acs_t * on_sub, axis=0, keepdims=True)
        dt_r = jnp.sum(dt_t * on_sub, axis=0, keepdims=True)
        w_parts.append(cb * jnp.exp(jnp.where(causal, acs_c - acs_r, NEG)) * dt_r)
        x_parts.append(x * (wide == j).astype(F32))
        skip = skip + jnp.sum(dsk * on_lane, axis=1, keepdims=True) * (wide == j).astype(F32)
        carry = carry + jnp.sum(jnp.exp(acs_last) * on_lane, axis=1, keepdims=True) * (tall == j).astype(F32)
    y_diag = _bdot(jnp.concatenate(w_parts, axis=1), jnp.concatenate(x_parts, axis=0), NN)
    y = y_diag + _bdot(cm, h, NT) * grow + skip * x
    return y, h * carry + _bdot(x * keep, bm, TN)


GROUPS_PER_STEP = 4
SSD_STEPS = N_GROUPS // GROUPS_PER_STEP


def _ssd_specs(reverse):
    n_chunks = SEQ // CHUNK
    c_of = (lambda c: n_chunks - 1 - c) if reverse else (lambda c: c)
    x_w, n_w, dt_w = GROUPS_PER_STEP * GROUP_WIDTH, GROUPS_PER_STEP * N_STATE, GROUPS_PER_STEP * 128
    x_spec = pl.BlockSpec((CHUNK, x_w), lambda g, c: (c_of(c), g))
    b_spec = pl.BlockSpec((CHUNK, n_w), lambda g, c: (c_of(c), D_SSM // n_w + g))
    c_spec = pl.BlockSpec((CHUNK, n_w), lambda g, c: (c_of(c), (D_SSM + N_GROUPS * N_STATE) // n_w + g))
    dt_spec = pl.BlockSpec((CHUNK, dt_w), lambda g, c: (c_of(c), g))
    vec_spec = pl.BlockSpec((1, dt_w), lambda g, c: (0, g))
    h_spec = pl.BlockSpec((None, GROUPS_PER_STEP, GROUP_WIDTH, N_STATE), lambda g, c: (c_of(c), g, 0, 0))
    return x_spec, b_spec, c_spec, dt_spec, vec_spec, h_spec


def _group_slices(u):
    return pl.ds(u * GROUP_WIDTH, GROUP_WIDTH), pl.ds(u * N_STATE, N_STATE), pl.ds(u * 128, 128)


def _ssd_gated_chunk(x, bm, cm, dtr, bias, alog, dsk, h, z, g_out):
    y, h_new = _ssd_chunk(x, bm, cm, dtr, bias, alog, dsk, h)
    return _gate_fn(y, z, g_out)[0], h_new


def _ssd_gate_specs(reverse):
    x_spec = _ssd_specs(reverse)[0]
    z_block0 = 3 * D_ATTN // x_spec.block_shape[1]
    z_spec = pl.BlockSpec(x_spec.block_shape, lambda g, c: (x_spec.index_map(g, c)[0], z_block0 + g))
    return z_spec, pl.BlockSpec((1, x_spec.block_shape[1]), lambda g, c: (0, g))


def _ssd_fwd(xbc, dt_raw, bias, alog, dsk, proj, g_out):
    x_spec, b_spec, c_spec, dt_spec, vec_spec, h_spec = _ssd_specs(False)
    z_spec, g_spec = _ssd_gate_specs(False)

    def body(x_ref, b_ref, c_ref, dt_ref, bias_ref, alog_ref, dsk_ref, z_ref, g_ref, ssm_ref, hin_ref, h_scr):
        @pl.when(pl.program_id(1) == 0)
        def _():
            h_scr[...] = jnp.zeros_like(h_scr)

        for u in range(GROUPS_PER_STEP):
            xs, ns, ds = _group_slices(u)
            h = h_scr[u]
            hin_ref[u] = h
            ssm, h_scr[u] = _ssd_gated_chunk(x_ref[:, xs], b_ref[:, ns], c_ref[:, ns], dt_ref[:, ds], bias_ref[:, ds],
                                             alog_ref[:, ds], dsk_ref[:, ds], h, z_ref[:, xs], g_ref[:, xs])
            ssm_ref[:, xs] = ssm.astype(ssm_ref.dtype)

    return pl.pallas_call(
        body, name="ssd_fwd", grid=(SSD_STEPS, SEQ // CHUNK),
        in_specs=[x_spec, b_spec, c_spec, dt_spec, vec_spec, vec_spec, vec_spec, z_spec, g_spec],
        out_specs=[x_spec, h_spec],
        out_shape=[jax.ShapeDtypeStruct((SEQ, D_SSM), BF16),
                   jax.ShapeDtypeStruct((SEQ // CHUNK, N_GROUPS, GROUP_WIDTH, N_STATE), F32)],
        scratch_shapes=[pltpu.VMEM((GROUPS_PER_STEP, GROUP_WIDTH, N_STATE), F32)],
        compiler_params=_params(("parallel", "arbitrary")),
    )(xbc, xbc, xbc, dt_raw, bias, alog, dsk, proj, g_out)


def _ssd_bwd(xbc, dt_raw, bias, alog, dsk, h_in, proj, g_out, dmix):
    x_spec, b_spec, c_spec, dt_spec, vec_spec, h_spec = _ssd_specs(True)
    z_spec, g_spec = _ssd_gate_specs(True)
    ct_block0 = D_ATTN // x_spec.block_shape[1]
    ct_spec = pl.BlockSpec(x_spec.block_shape, lambda g, c: (x_spec.index_map(g, c)[0], ct_block0 + g))

    def body(x_ref, b_ref, c_ref, dt_ref, bias_ref, alog_ref, dsk_ref, hin_ref, z_ref, g_ref, ct_ref,
             dx_ref, db_ref, dc_ref, ddt_ref, dbias_ref, dalog_ref, ddsk_ref, dz_ref, dg_ref, dh_scr):
        first = pl.program_id(1) == 0

        @pl.when(first)
        def _():
            dh_scr[...] = jnp.zeros_like(dh_scr)

        for u in range(GROUPS_PER_STEP):
            xs, ns, ds = _group_slices(u)
            _, pullback = jax.vjp(_ssd_gated_chunk, x_ref[:, xs], b_ref[:, ns], c_ref[:, ns], dt_ref[:, ds], bias_ref[:, ds],
                                  alog_ref[:, ds], dsk_ref[:, ds], hin_ref[u], z_ref[:, xs], g_ref[:, xs])
            g = pullback((ct_ref[:, xs], dh_scr[u]))
            dx_ref[:, xs], db_ref[:, ns], dc_ref[:, ns] = g[0], g[1], g[2]
            ddt_ref[:, ds] = g[3].astype(ddt_ref.dtype)
            dh_scr[u] = g[7]
            dz_ref[:, xs] = g[8].astype(dz_ref.dtype)
            sums = ((dbias_ref, g[4], ds), (dalog_ref, g[5], ds), (ddsk_ref, g[6], ds),
                    (dg_ref, jnp.sum(g[9], axis=0, keepdims=True), xs))
            for o_ref, val, lanes in sums:
                @pl.when(first)
                def _(o_ref=o_ref, val=val, lanes=lanes):
                    o_ref[:, lanes] = val

                @pl.when(jnp.logical_not(first))
                def _(o_ref=o_ref, val=val, lanes=lanes):
                    o_ref[:, lanes] += val

    n_chunks = SEQ // CHUNK
    out_b = pl.BlockSpec((CHUNK, GROUPS_PER_STEP * N_STATE), lambda g, c: (n_chunks - 1 - c, g))
    return pl.pallas_call(
        body, name="ssd_bwd", grid=(SSD_STEPS, n_chunks),
        in_specs=[x_spec, b_spec, c_spec, dt_spec, vec_spec, vec_spec, vec_spec, h_spec, z_spec, g_spec, ct_spec],
        out_specs=[x_spec, out_b, out_b, dt_spec, vec_spec, vec_spec, vec_spec, x_spec, g_spec],
        out_shape=[jax.ShapeDtypeStruct((SEQ, D_SSM), F32), jax.ShapeDtypeStruct((SEQ, N_GROUPS * N_STATE), F32),
                   jax.ShapeDtypeStruct((SEQ, N_GROUPS * N_STATE), F32), jax.ShapeDtypeStruct((SEQ, DT_PAD), BF16),
                   jax.ShapeDtypeStruct((1, DT_PAD), F32), jax.ShapeDtypeStruct((1, DT_PAD), F32),
                   jax.ShapeDtypeStruct((1, DT_PAD), F32), jax.ShapeDtypeStruct((SEQ, D_SSM), BF16),
                   jax.ShapeDtypeStruct((1, D_SSM), F32)],
        scratch_shapes=[pltpu.VMEM((GROUPS_PER_STEP, GROUP_WIDTH, N_STATE), F32)],
        compiler_params=_params(("parallel", "arbitrary")),
    )(xbc, xbc, xbc, dt_raw, bias, alog, dsk, h_in, proj, g_out, dmix)


CROSS_HEAD = 128
CROSS_ROWS = 1024


def _cross_head(q, k, v, gq, gk):
    qn = _rms(q, gq) * (CROSS_HEAD ** -0.5)
    kn = _rms(k, gk)
    s = _bdot(qn, kn, NT)
    p = jnp.exp(s - lax.stop_gradient(jnp.max(s, axis=-1, keepdims=True)))
    return _bdot(p, v, NN) * (1.0 / jnp.sum(p, axis=-1, keepdims=True))


def _cross_specs():
    q_spec = pl.BlockSpec((CROSS_ROWS, CROSS_HEAD), lambda h, i: (i, h))
    k_spec = pl.BlockSpec((N_MEM, CROSS_HEAD), lambda h, i: (0, h))
    v_spec = pl.BlockSpec((N_MEM, CROSS_HEAD), lambda h, i: (0, 4 + h))
    g_spec = pl.BlockSpec((1, CROSS_HEAD), lambda h, i: (0, 0))
    return q_spec, k_spec, v_spec, g_spec


def _cross_fwd(qc, kv, gq, gk):
    q_spec, k_spec, v_spec, g_spec = _cross_specs()

    def body(q_ref, k_ref, v_ref, gq_ref, gk_ref, o_ref):
        o_ref[...] = _cross_head(q_ref[...], k_ref[...], v_ref[...], gq_ref[...], gk_ref[...]).astype(o_ref.dtype)

    return pl.pallas_call(
        body, name="cross_fwd", grid=(4, SEQ // CROSS_ROWS),
        in_specs=[q_spec, k_spec, v_spec, g_spec, g_spec], out_specs=q_spec,
        out_shape=jax.ShapeDtypeStruct((SEQ, D_CROSS), BF16),
        compiler_params=_params(("parallel", "parallel")),
    )(qc, kv, kv, gq, gk)


def _cross_bwd(qc, kv, gq, gk, do):
    q_spec, k_spec, v_spec, g_spec = _cross_specs()

    def body(q_ref, k_ref, v_ref, gq_ref, gk_ref, do_ref, dq_ref, dk_ref, dv_ref, dgq_ref, dgk_ref):
        _, pullback = jax.vjp(_cross_head, q_ref[...], k_ref[...], v_ref[...], gq_ref[...], gk_ref[...])
        dq, dk, dv, dgq, dgk = pullback(do_ref[...].astype(F32))
        dq_ref[...] = dq.astype(dq_ref.dtype)
        row0 = pl.program_id(1) == 0
        all0 = jnp.logical_and(row0, pl.program_id(0) == 0)
        for o_ref, val, init in ((dk_ref, dk, row0), (dv_ref, dv, row0), (dgq_ref, dgq, all0), (dgk_ref, dgk, all0)):
            @pl.when(init)
            def _(o_ref=o_ref, val=val):
                o_ref[...] = val

            @pl.when(jnp.logical_not(init))
            def _(o_ref=o_ref, val=val):
                o_ref[...] += val

    return pl.pallas_call(
        body, name="cross_bwd", grid=(4, SEQ // CROSS_ROWS),
        in_specs=[q_spec, k_spec, v_spec, g_spec, g_spec, q_spec],
        out_specs=[q_spec, k_spec, k_spec, g_spec, g_spec],
        out_shape=[jax.ShapeDtypeStruct((SEQ, D_CROSS), BF16), jax.ShapeDtypeStruct((N_MEM, D_CROSS), F32),
                   jax.ShapeDtypeStruct((N_MEM, D_CROSS), F32), jax.ShapeDtypeStruct((1, CROSS_HEAD), F32),
                   jax.ShapeDtypeStruct((1, CROSS_HEAD), F32)],
        compiler_params=_params(("arbitrary", "arbitrary")),
    )(qc, kv, kv, gq, gk, do)


def _loss_epilogue(acc, residual, target):
    err = acc + residual - target
    dy = err * (1.0 / D_MODEL)
    part = jnp.sum(jnp.sum(err * err, axis=1, keepdims=True), axis=0, keepdims=True) * (0.5 / D_MODEL)
    return dy, dy, part


def _pad_heads(v):
    return jnp.pad(v.reshape(N_GROUPS, HEADS_PER_GROUP), ((0, 0), (0, 128 - HEADS_PER_GROUP))).reshape(1, DT_PAD)


def _unpad_heads(v):
    return v.reshape(v.shape[0], N_GROUPS, 128)[:, :, :HEADS_PER_GROUP].reshape(v.shape[0], N_DT)


def _rope_tables(positions):
    half = ROT // 2
    inv_freq = ROPE_THETA ** (-2.0 * jnp.arange(half, dtype=F32) / ROT)
    ang = positions.reshape(SEQ, 1).astype(F32) * inv_freq
    cos, sin = jnp.cos(ang), jnp.sin(ang)
    ones, zeros = jnp.ones((SEQ, HEAD - ROT), F32), jnp.zeros((SEQ, HEAD - ROT), F32)
    cos_h = jnp.concatenate([cos, cos, ones], axis=1)
    sin_h = jnp.concatenate([-sin, sin, zeros], axis=1)
    return jnp.tile(cos_h, (1, 2)), jnp.tile(sin_h, (1, 2))


def _add_res(acc, res):
    return (acc + res,)


def _norm_bwd_epilogue(acc, x, residual, *more):
    *part, g = more
    ct = acc + part[0] if part else acc
    _, pullback = jax.vjp(_rms, x, g)
    dx, dg = pullback(ct)
    return dx + residual, dg


def _add_res_and_norm(acc, res, g):
    y = acc + res
    return y, _rms(y, g)


def _settle(grads, *after):
    if hasattr(grads, "settle"):
        grads.settle(*after)


def _take_token(grads):
    token = getattr(grads, "token", None)
    if token is None:
        return ()
    grads.token = None
    return (token,)


def _local_step(x, mem, positions, target, p, w, more_weights=None, grads=None, h=None):
    grads = {} if grads is None else grads
    w = dict(w)
    cos, sin = _rope_tables(positions)
    gq2, gk2 = jnp.tile(p["g_q"], (1, 2)), jnp.tile(p["g_k"], (1, 2))
    bias, alog, dsk = _pad_heads(p["dt_bias"]), _pad_heads(p["a_log"]), _pad_heads(p["d_skip"])
    norm_out = [(D_MODEL, BF16, D_MODEL, 0, False)]

    if h is None:
        h = _rowwise(_norm_fn, [_full(x)], [_full(p["g_mix"])], norm_out, name="norm_in")[0]
    proj = _matmul(h, w["w_in"], mode="nn", name="in_proj", outs=[F32], n_cols=D_MAIN)
    dt_raw = _matmul(h, w["w_dt"], mode="nn", name="dt_proj", outs=[F32])
    pairs = D_ATTN // 128
    qk_rows = [(proj, 128, 0, True), (proj, 128, pairs, True), (proj, 128, 2 * pairs, True), _full(cos), _full(sin)]
    qk_vecs = [_full(gq2), _full(gk2)]
    qn, kn, vn = _rowwise(_qk_fn, qk_rows, qk_vecs, [(D_ATTN, F32, 128, 0, True)] * 3, name="qk_prep", groups=8, tr=1024)
    branches = [_attention_fwd(qn, kn, vn, b) for b in range(3)]
    merge_rows = [_full(o) for o, _ in branches] + [_full(lse) for _, lse in branches]
    attn = _rowwise(_merge_fn, merge_rows, [_full(p["g_attn_out"])], [(D_ATTN, BF16, D_ATTN, 0, False)], name="attn_merge")[0]
    xbc = _conv_fwd(proj, p["conv_w"], p["conv_b"])
    ssm, h_in = _ssd_fwd(xbc, dt_raw, bias, alog, dsk, proj, p["g_ssm_out"])
    mix = jnp.concatenate([attn, ssm], axis=1)
    if more_weights is not None:
        w.update(more_weights("mixer_done", mix))
    x1, hc = _matmul(mix, w["w_out"], mode="nn", name="out_proj", outs=[F32, BF16], extra=(x,), vecs=(p["g_cross"],),
                     epilogue=_add_res_and_norm, tm=512, tn=D_MODEL)
    memh = _rowwise(_norm_fn, [_full(mem)], [_full(p["g_mem"])], norm_out, name="norm_mem", n_rows=N_MEM, tr=N_MEM)[0]
    qc = _matmul(hc, w["w_cq"], mode="nn", name="cq_proj", outs=[F32])
    if more_weights is not None:
        w.update(more_weights("cross_started", qc))
    kv = _matmul(memh, w["w_ckv"], mode="nn", name="ckv_proj", outs=[F32])
    oc = _cross_fwd(qc, kv, p["g_cq"], p["g_ck"])
    x2, hm = _matmul(oc, w["w_co"], mode="nn", name="co_proj", outs=[F32, BF16], extra=(x1,), vecs=(p["g_mlp"],),
                     epilogue=_add_res_and_norm, tm=512, tn=D_MODEL)
    if more_weights is not None:
        w.update(more_weights("cross_done", hm))
    u, act = _matmul(hm, w["w_up"], mode="nn", name="up_proj", outs=[F32, BF16],
                     epilogue=lambda acc: (acc, jnp.square(jnp.maximum(acc, 0.0))))
    dy, dyb, loss_tiles = _matmul(act, w["w_down"], mode="nn", name="down_proj", outs=[F32, BF16], extra=(x2, target),
                                  epilogue=_loss_epilogue, tile_sums=1)
    loss = jnp.sum(loss_tiles).reshape(1, 1)

    grads["w_down"] = _matmul(act, dyb, mode="tn", name="dw_down", outs=[BF16], after=_take_token(grads))
    du = _matmul(dyb, w["w_down"], mode="nt", name="d_act", outs=[BF16], extra=(u,), after=_take_token(grads),
                 epilogue=lambda acc, uu: (acc * (2.0 * jnp.maximum(uu, 0.0)),))
    _settle(grads, du)
    grads["w_up"] = _matmul(hm, du, mode="tn", name="dw_up", outs=[BF16], col_shards=4, after=_take_token(grads))
    dx2, grads["g_mlp"] = _matmul(du, w["w_up"], mode="nt", name="d_hm", outs=[F32], extra=(x2, dy), vecs=(p["g_mlp"],),
                                  epilogue=_norm_bwd_epilogue, tile_rows=1, after=_take_token(grads), tm=512, tn=D_MODEL,
                                  tk=1024)
    _settle(grads, dx2)
    grads["w_co"] = _matmul(oc, dx2, mode="tn", name="dw_co", outs=[BF16], col_shards=4, after=_take_token(grads))
    doc = _matmul(dx2, w["w_co"], mode="nt", name="d_oc", outs=[BF16])
    dqc, dkc, dvc, grads["g_cq"], grads["g_ck"] = _cross_bwd(qc, kv, p["g_cq"], p["g_ck"], doc)
    grads["w_cq"] = _matmul(hc, dqc, mode="tn", name="dw_cq", outs=[BF16])
    dkv = jnp.concatenate([dkc, dvc], axis=1)
    grads["w_ckv"] = _matmul(memh, dkv, mode="tn", name="dw_ckv", outs=[BF16])
    dmemh = _matmul(dkv, w["w_ckv"], mode="nt", name="d_memh", outs=[F32])
    grads["g_mem"] = _rowwise_vjp(_norm_fn, [_full(mem)], [_full(p["g_mem"])], [[_full(dmemh)]], [],
                                  [(0, D_MODEL, D_MODEL, 0, False)], name="norm_mem_bwd", n_rows=N_MEM, tr=N_MEM)[0]
    dx1, grads["g_cross"] = _matmul(dqc, w["w_cq"], mode="nt", name="d_hc", outs=[F32], extra=(x1, dx2), vecs=(p["g_cross"],),
                                    epilogue=_norm_bwd_epilogue, tile_rows=1, tm=512, tn=D_MODEL)
    grads["w_out"] = _matmul(mix, dx1, mode="tn", name="dw_out", outs=[BF16])
    dmix = _matmul(dx1, w["w_out"], mode="nt", name="d_mix", outs=[F32], after=_take_token(grads))
    _settle(grads, dmix)
    merge_grads = [(i, D_ATTN, F32, D_ATTN, 0, False, None) for i in range(6)]
    *dol, grads["g_attn_out"] = _rowwise_vjp(
        _merge_fn, merge_rows, [_full(p["g_attn_out"])], [[(dmix, D_ATTN, 0, False)]],
        merge_grads, [(0, D_ATTN, D_ATTN, 0, False)], name="attn_merge_bwd", tr=256, after=_take_token(grads))
    dqkv = [_attention_bwd(qn, kn, vn, *branches[b], dol[b], dol[3 + b], b) for b in range(3)]
    qk_cts = [[(dqkv[b][i], 128, 0, True) for b in range(3)] for i in range(3)]
    dq, dk, dv, dgq2, dgk2 = _rowwise_vjp(
        _qk_fn, qk_rows, qk_vecs, qk_cts, [(i, D_ATTN, BF16, 128, 0, True, None) for i in range(3)],
        [(0, 128, 128, 0, False), (1, 128, 128, 0, False)], name="qk_prep_bwd", groups=8, tr=1024)
    grads["g_q"] = dgq2[:, :HEAD] + dgq2[:, HEAD:]
    grads["g_k"] = dgk2[:, :HEAD] + dgk2[:, HEAD:]
    dxs, db, dc, ddt, dbias, dalog, ddsk, dz, grads["g_ssm_out"] = _ssd_bwd(xbc, dt_raw, bias, alog, dsk, h_in, proj,
                                                                             p["g_ssm_out"], dmix)
    grads["dt_bias"], grads["a_log"], grads["d_skip"] = _unpad_heads(dbias), _unpad_heads(dalog), _unpad_heads(ddsk)
    dxbc_raw, dconv_w, grads["conv_b"] = _conv_bwd(proj, p["conv_w"], p["conv_b"], dxs, db, dc)
    grads["conv_w"] = dconv_w[:4]
    dproj = jnp.concatenate([dq, dk, dv, dz, dxbc_raw], axis=1)
    grads["w_main"] = _matmul(h, dproj, mode="tn", name="dw_main", outs=[BF16], out_cols=D_MAIN + N_DT)
    grads["w_dt"] = _matmul(h, ddt, mode="tn", name="dw_dt", outs=[BF16])
    dh = _matmul(dproj, w["w_in"], mode="nt", name="d_h_main", outs=[F32], after=_take_token(grads))
    grad_x, grads["g_mix"] = _matmul(ddt, w["w_dt"], mode="nt", name="d_h_dt", outs=[F32], extra=(x, dx1, dh),
                                     vecs=(p["g_mix"],), epilogue=_norm_bwd_epilogue, tile_rows=1, tm=512, tn=D_MODEL)
    return loss, grad_x, grads


MATRICES = ("w_in", "w_out", "w_cq", "w_ckv", "w_co", "w_up", "w_down")
ROW_SHARDED = ("w_out", "w_cq", "w_ckv", "w_down")
N_CHIPS = 4
ANY = pl.BlockSpec(memory_space=pl.ANY)


def _place():
    return lax.axis_index("x"), lax.axis_index("y"), lax.axis_index("c")


def _other_chips(x, y):
    return [(1 - x, y), (x, 1 - y), (1 - x, 1 - y)]


def _remote(src, dst, send_sem, recv_sem, device):
    return pltpu.make_async_remote_copy(src_ref=src, dst_ref=dst, send_sem=send_sem, recv_sem=recv_sem,
                                        device_id=device, device_id_type=MESH)


def _gathered_shape(name, shard):
    rows, cols = shard.shape
    if name == "w_in":
        return (N_CHIPS, rows, cols)
    return (N_CHIPS * rows, cols) if name in ROW_SHARDED else (rows, N_CHIPS * cols)


def _shard_window(name, ref, rows, cols, chip, half):
    r0, nr = (0, rows) if half is None else (half * (rows // 2), rows // 2)
    if name == "w_in":
        return ref.at[chip, pl.ds(r0, nr), :]
    if name in ROW_SHARDED:
        return ref.at[pl.ds(chip * rows + r0, nr), :]
    return ref.at[pl.ds(r0, nr), pl.ds(pl.multiple_of(chip * cols, 128), cols)]


def _cast_into_gathered(w, name, chip, after=()):
    rows, cols = w.shape
    tr = _tile(rows, ROW_TILE)

    def body(chip_ref, w_ref, *rest):
        rest[-1][...] = w_ref[...].astype(BF16)

    if name == "w_in":
        out_spec = pl.BlockSpec((None, tr, cols), lambda i, chip_ref: (chip_ref[0], i, 0))
    elif name in ROW_SHARDED:
        out_spec = pl.BlockSpec((tr, cols), lambda i, chip_ref: (chip_ref[0] * (rows // tr) + i, 0))
    else:
        out_spec = pl.BlockSpec((tr, cols), lambda i, chip_ref: (i, chip_ref[0]))
    grid_spec = pltpu.PrefetchScalarGridSpec(
        num_scalar_prefetch=1, grid=(rows // tr,),
        in_specs=[pl.BlockSpec((tr, cols), lambda i, chip_ref: (i, 0))] + [pl.BlockSpec(memory_space=pl.ANY)] * len(after),
        out_specs=out_spec)
    return pl.pallas_call(body, name="cast_" + name, grid_spec=grid_spec,
                          out_shape=jax.ShapeDtypeStruct(_gathered_shape(name, w), BF16),
                          compiler_params=_params(("parallel",)))(chip.reshape(1).astype(jnp.int32), w, *after)


def _w_in_columns(arr, to_shards):
    rows, piece = D_MODEL, (D_MAIN + N_DT) // N_CHIPS
    tr = ROW_TILE

    def body(a_ref, o_ref):
        for j in range(N_CHIPS):
            if to_shards:
                o_ref[j] = a_ref[:, pl.ds(piece * j, piece)]
            else:
                o_ref[:, pl.ds(piece * j, piece)] = a_ref[j]

    pieces = pl.BlockSpec((N_CHIPS, tr, piece), lambda i: (0, i, 0))
    matrix = pl.BlockSpec((tr, N_CHIPS * piece), lambda i: (i, 0))
    out_dims = (N_CHIPS, rows, piece) if to_shards else (rows, N_CHIPS * piece)
    return pl.pallas_call(
        body, name="w_in_to_shards" if to_shards else "w_in_from_shards", grid=(rows // tr,),
        in_specs=[matrix if to_shards else pieces], out_specs=pieces if to_shards else matrix,
        out_shape=jax.ShapeDtypeStruct(out_dims, arr.dtype), compiler_params=_params(("parallel",)))(arr)


HBM = pl.BlockSpec(memory_space=pltpu.HBM)
SEM = pl.BlockSpec(memory_space=pltpu.SEMAPHORE)
EFFECT = pltpu.SideEffectType.DATAFLOW_SIDE_EFFECTING


def _split_start(name, bufs, plan, counts, after=()):
    n, n_g, n_after = len(bufs), len(counts), len(after)

    def body(*refs):
        ins, sems, token = refs[:n], refs[n + n_after:n + n_after + 2 * n_g], refs[-1]
        for g, copies in enumerate(plan(ins)):
            for i, (src, dst, device, _) in enumerate(copies):
                _remote(src, dst, sems[2 * g].at[i], sems[2 * g + 1].at[i], device).start()
        token[...] = jnp.zeros_like(token)

    sem_shapes = [pltpu.SemaphoreType.DMA((cnt,)) for cnt in counts for _ in range(2)]
    res = pl.pallas_call(
        body, name=name,
        out_shape=(*sem_shapes, *[pltpu.HBM(b.shape, b.dtype) for b in bufs], jax.ShapeDtypeStruct((8, 128), F32)),
        in_specs=(*(HBM,) * n, *(ANY,) * n_after),
        out_specs=(*(SEM,) * (2 * n_g), *(HBM,) * n, pl.BlockSpec(memory_space=pltpu.VMEM)),
        input_output_aliases={i: 2 * n_g + i for i in range(n)},
        compiler_params=pltpu.CompilerParams(has_side_effects=EFFECT),
    )(*[pltpu.with_memory_space_constraint(b, pltpu.HBM) for b in bufs], *after)
    sems = [(res[2 * g], res[2 * g + 1]) for g in range(n_g)]
    return sems, list(res[2 * n_g:2 * n_g + n]), res[-1]


def _split_wait(name, bufs, sems, plan, *after):
    n = len(bufs)

    def body(*refs):
        ins, send, recv = refs[:n], refs[n], refs[n + 1]
        (copies,) = plan(ins)
        for i, (src, _, device, landing) in enumerate(copies):
            cp = _remote(src, landing, send.at[i], recv.at[i], device)
            cp.wait_send()
            cp.wait_recv()

    res = pl.pallas_call(
        body, name=name, out_shape=tuple(pltpu.HBM(b.shape, b.dtype) for b in bufs),
        in_specs=(*(HBM,) * n, SEM, SEM, *(ANY,) * len(after)), out_specs=(HBM,) * n,
        input_output_aliases={i: i for i in range(n)},
        compiler_params=pltpu.CompilerParams(has_side_effects=EFFECT),
    )(*bufs, sems[0], sems[1], *after)
    return list(res)


def _ici_plan(names, shard_shapes):
    def plan(refs):
        x, y, c = _place()
        copies = []
        for ref, name in zip(refs, names):
            win = _shard_window(name, ref, *shard_shapes[name], 2 * x + y, c)
            for px, py in _other_chips(x, y):
                copies.append((win, win, (px, py, c), _shard_window(name, ref, *shard_shapes[name], 2 * px + py, c)))
        return [copies]
    return plan


def _pass_on_plan(names, shard_shapes):
    def plan(refs):
        x, y, c = _place()
        copies = []
        for ref, name in zip(refs, names):
            for px, py in _other_chips(x, y):
                win = _shard_window(name, ref, *shard_shapes[name], 2 * px + py, c)
                copies.append((win, win, (x, y, 1 - c), _shard_window(name, ref, *shard_shapes[name], 2 * px + py, 1 - c)))
        return [copies]
    return plan


def _swap_plan(n_pairs):
    def plan(refs):
        x, y, c = _place()
        return [[(src.at[:, 1 - c], dst, (x, y, 1 - c), dst) for src, dst in zip(refs[:n_pairs], refs[n_pairs:])]]
    return plan


def _share_plan(n_pairs):
    def plan(refs):
        x, y, c = _place()
        return [[(src, dst, (x, y, 1 - c), dst)] for src, dst in zip(refs[:n_pairs], refs[n_pairs:])]
    return plan


def _scatter_plan(n_pairs):
    def plan(refs):
        x, y, c = _place()
        copies = []
        for src, dst in zip(refs[:n_pairs], refs[n_pairs:]):
            for k, (px, py) in enumerate(_other_chips(x, y)):
                copies.append((src.at[2 * px + py], dst.at[k], (px, py, c), dst.at[k]))
        return [copies]
    return plan


def _sibling_swap(arrs, name):
    n = len(arrs)

    def body(*refs):
        ins, outs, send, recv = refs[:n], refs[n:2 * n], refs[2 * n], refs[2 * n + 1]
        x, y, c = _place()
        cps = [_remote(ins[w].at[:, 1 - c], outs[w], send.at[w], recv.at[w], (x, y, 1 - c)) for w in range(n)]
        for cp in cps:
            cp.start()
        for cp in cps:
            cp.wait()

    return pl.pallas_call(
        body, name=name, in_specs=[ANY] * n, out_specs=[ANY] * n,
        out_shape=[jax.ShapeDtypeStruct((a.shape[0],) + a.shape[2:], a.dtype) for a in arrs],
        scratch_shapes=[pltpu.SemaphoreType.DMA((n,))] * 2,
    )(*arrs)


def _small_allreduce(buf, name, after=()):
    rows = buf.shape[0]

    def body(x_ref, *rest):
        out_ref, all_ref, send_sems, recv_sems, local_sem = rest[len(after):]
        x, y, c = _place()
        me, sibling, chips = (x, y, c), (x, y, 1 - c), _other_chips(x, y)

        def block(px, py, pc):
            return all_ref.at[pl.ds((4 * px + 2 * py + pc) * rows, rows), :]

        def copy(k, blk, to, src=None):
            return _remote(block(*blk) if src is None else src, block(*blk), send_sems.at[k], recv_sems.at[k], to)

        own = pltpu.make_async_copy(x_ref, block(*me), local_sem)
        own.start()
        first = [copy(0, me, sibling, src=x_ref)] + [copy(1 + j, me, (*chip, c), src=x_ref) for j, chip in enumerate(chips)]
        for cp in first:
            cp.start()
        passed = [copy(4 + j, (*chip, c), sibling) for j, chip in enumerate(chips)]
        for j, chip in enumerate(chips):
            copy(1 + j, (*chip, c), me).wait_recv()
            passed[j].start()
        copy(0, sibling, me).wait_recv()
        for j, chip in enumerate(chips):
            copy(4 + j, (*chip, 1 - c), me).wait_recv()
        for cp in first + passed:
            cp.wait_send()
        own.wait()
        acc = all_ref[pl.ds(0, rows), :]
        for d in range(1, 8):
            acc = acc + all_ref[pl.ds(d * rows, rows), :]
        out_ref[...] = acc

    vmem = pl.BlockSpec(memory_space=pltpu.VMEM)
    return pl.pallas_call(
        body, name=name, in_specs=[vmem] + [ANY] * len(after), out_specs=vmem,
        out_shape=jax.ShapeDtypeStruct(buf.shape, F32),
        scratch_shapes=[pltpu.VMEM((8 * rows, 128), F32), pltpu.SemaphoreType.DMA((7,)), pltpu.SemaphoreType.DMA((7,)),
                        pltpu.SemaphoreType.DMA],
    )(buf, *after)


ROW_TILE = 256
BIG_ROW_TILE = 1024


def _add_halves(arr, recv, c, name):
    _, _, hr, cols = arr.shape
    tr = _tile(hr, BIG_ROW_TILE)

    def body(c_ref, a_ref, r_ref, o_ref):
        o_ref[...] = (a_ref[...].astype(F32) + r_ref[...].astype(F32)).astype(o_ref.dtype)

    piece = pl.BlockSpec((None, tr, cols), lambda j, i, c_ref: (j, i, 0))
    grid_spec = pltpu.PrefetchScalarGridSpec(
        num_scalar_prefetch=1, grid=(N_CHIPS, hr // tr),
        in_specs=[pl.BlockSpec((None, None, tr, cols), lambda j, i, c_ref: (j, c_ref[0], i, 0)), piece], out_specs=piece)
    return pl.pallas_call(body, name=name, grid_spec=grid_spec, out_shape=jax.ShapeDtypeStruct(recv.shape, BF16),
                          compiler_params=_params(("parallel", "parallel")))(c.reshape(1).astype(jnp.int32), arr, recv)


def _flip_slot(d):
    return jnp.where(d == 1, 1, jnp.where(d == 3, 2, 0))


def _sum_chips(p, q, chip, name):
    _, hr, cols = p.shape
    tr = _tile(hr, BIG_ROW_TILE)

    def body(chip_ref, p_ref, q_ref, o_ref):
        j = pl.program_id(1)
        term = jnp.where(j == chip_ref[0], p_ref[...].astype(F32), q_ref[...].astype(F32))

        @pl.when(j == 0)
        def _():
            o_ref[...] = term

        @pl.when(j != 0)
        def _():
            o_ref[...] += term

    grid_spec = pltpu.PrefetchScalarGridSpec(
        num_scalar_prefetch=1, grid=(hr // tr, N_CHIPS),
        in_specs=[pl.BlockSpec((None, tr, cols), lambda i, j, chip_ref: (chip_ref[0], i, 0)),
                  pl.BlockSpec((None, tr, cols), lambda i, j, chip_ref: (_flip_slot(j ^ chip_ref[0]), i, 0))],
        out_specs=pl.BlockSpec((tr, cols), lambda i, j, chip_ref: (i, 0)))
    return pl.pallas_call(body, name=name, grid_spec=grid_spec, out_shape=jax.ShapeDtypeStruct((hr, cols), F32),
                          compiler_params=_params(("parallel", "arbitrary")))(chip.reshape(1).astype(jnp.int32), p, q)


def _adamw_halves(w, g_own, g_other, m, v, c, name):
    rows, cols = w.shape
    tr = _tile(rows // 2, ROW_TILE)
    per_half = rows // 2 // tr

    def body(c_ref, w_ref, own_ref, other_ref, m_ref, v_ref, g_ref, d_ref, nm_ref, nv_ref):
        mine = (pl.program_id(0) // per_half) == c_ref[0]
        g_ = jnp.where(mine, own_ref[...], other_ref[...])
        g_ref[...] = g_
        d_ref[...], nm_ref[...], nv_ref[...] = _adamw_math(w_ref[...], g_, m_ref[...], v_ref[...])

    blk = pl.BlockSpec((tr, cols), lambda i, c_ref: (i, 0))
    own = pl.BlockSpec((tr, cols), lambda i, c_ref: (jnp.where(i // per_half == c_ref[0], i % per_half, 0), 0))
    other = pl.BlockSpec((tr, cols), lambda i, c_ref: (jnp.where(i // per_half == c_ref[0], 0, i % per_half), 0))
    grid_spec = pltpu.PrefetchScalarGridSpec(num_scalar_prefetch=1, grid=(rows // tr,),
                                             in_specs=[blk, own, other, blk, blk], out_specs=[blk] * 4)
    return pl.pallas_call(body, name=name, grid_spec=grid_spec, out_shape=[jax.ShapeDtypeStruct(w.shape, F32)] * 4,
                          compiler_params=_params(("parallel",)))(c.reshape(1).astype(jnp.int32), w, g_own, g_other, m, v)


W_IN_COLS = (D_MAIN + N_DT) // N_CHIPS
W_IN_MAIN = W_IN_COLS // 128 * 128
W_IN_TAIL = W_IN_COLS - 128
W_IN_PARTS = ((0, W_IN_MAIN), (W_IN_TAIL, 128))


def _cast_w_in_transposed(w_t, chip, after=()):
    def body(chip_ref, w_ref, *rest):
        for start, size in W_IN_PARTS:
            rest[-1][:, pl.ds(start, size)] = w_ref[pl.ds(start, size), :].T.astype(BF16)

    grid_spec = pltpu.PrefetchScalarGridSpec(
        num_scalar_prefetch=1, grid=(D_MODEL // ROW_TILE,),
        in_specs=[pl.BlockSpec((W_IN_COLS, ROW_TILE), lambda i, chip_ref: (0, i))] + [pl.BlockSpec(memory_space=pl.ANY)] * len(after),
        out_specs=pl.BlockSpec((None, ROW_TILE, W_IN_COLS), lambda i, chip_ref: (chip_ref[0], i, 0)))
    return pl.pallas_call(body, name="cast_w_in", grid_spec=grid_spec,
                          out_shape=jax.ShapeDtypeStruct((N_CHIPS, D_MODEL, W_IN_COLS), BF16),
                          compiler_params=_params(("parallel",)))(chip.reshape(1).astype(jnp.int32), w_t, *after)


def _adamw_w_in_transposed(w_t, g_own, g_other, m_t, v_t, c):
    per_half = D_MODEL // 2 // ROW_TILE

    def body(c_ref, w_ref, own_ref, other_ref, m_ref, v_ref, g_ref, d_ref, nm_ref, nv_ref):
        mine = (pl.program_id(0) // per_half) == c_ref[0]
        for start, size in W_IN_PARTS:
            cols, rows = pl.ds(start, size), pl.ds(start, size)
            g_ = jnp.where(mine, own_ref[:, cols], other_ref[:, cols]).T
            g_ref[rows, :] = g_
            d_ref[rows, :], nm_ref[rows, :], nv_ref[rows, :] = _adamw_math(w_ref[rows, :], g_, m_ref[rows, :], v_ref[rows, :])

    blk = pl.BlockSpec((W_IN_COLS, ROW_TILE), lambda i, c_ref: (0, i))
    own = pl.BlockSpec((ROW_TILE, W_IN_COLS), lambda i, c_ref: (jnp.where(i // per_half == c_ref[0], i % per_half, 0), 0))
    other = pl.BlockSpec((ROW_TILE, W_IN_COLS), lambda i, c_ref: (jnp.where(i // per_half == c_ref[0], 0, i % per_half), 0))
    grid_spec = pltpu.PrefetchScalarGridSpec(num_scalar_prefetch=1, grid=(D_MODEL // ROW_TILE,),
                                             in_specs=[blk, own, other, blk, blk], out_specs=[blk] * 4)
    return pl.pallas_call(body, name="adamw_w_in", grid_spec=grid_spec, out_shape=[jax.ShapeDtypeStruct(w_t.shape, F32)] * 4,
                          compiler_params=_params(("parallel",)))(c.reshape(1).astype(jnp.int32), w_t, g_own, g_other, m_t, v_t)


def _adamw_math(w, g, m, v):
    m_new = ADAM_B1 * m + (1.0 - ADAM_B1) * g
    v_new = ADAM_B2 * v + (1.0 - ADAM_B2) * (g * g)
    m_hat = m_new / (1.0 - ADAM_B1 ** ADAM_STEP)
    v_hat = v_new / (1.0 - ADAM_B2 ** ADAM_STEP)
    return -ADAM_LR * (m_hat / (jnp.sqrt(v_hat) + ADAM_EPS) + ADAM_WD * w), m_new, v_new


VECTORS = ("g_mix", "g_q", "g_k", "g_attn_out", "conv_b", "dt_bias", "a_log", "d_skip", "g_ssm_out", "g_cross", "g_mem",
           "g_cq", "g_ck", "g_mlp")
WEIGHTS = ("g_mix", "w_in", "g_q", "g_k", "g_attn_out", "conv_w", "conv_b", "dt_bias", "a_log", "d_skip", "g_ssm_out", "w_out",
           "g_cross", "g_mem", "w_cq", "w_ckv", "g_cq", "g_ck", "w_co", "g_mlp", "w_up", "w_down")


def _pack(parts):
    flat = jnp.concatenate([t.reshape(-1) for t in parts])
    total = -(-flat.shape[0] // 1024) * 1024
    return jnp.pad(flat, (0, total - flat.shape[0])).reshape(total // 128, 128)


def _rows_of(n):
    return -(-n // 128)


def _slot_rows(n):
    return -(-n // 1024) * 8


def _pack_rows(parts):
    rows = []
    for t in parts:
        flat = t.reshape(-1)
        rows.append(jnp.pad(flat, (0, 128 * _slot_rows(flat.shape[0]) - flat.shape[0])).reshape(-1, 128))
    return jnp.concatenate(rows)


def _adamw_vectors(summed, chip, vectors, conv):
    groups = list(vectors) + [conv]
    offsets, row = [], 0
    for w, _, _ in groups:
        offsets.append(row)
        row += _slot_rows(w.shape[1]) if w.shape[0] == 1 else _slot_rows(w.shape[0] * N_CHIPS * w.shape[1])
    conv_blocks = _rows_of(conv[0].shape[1])

    def body(chip_ref, sum_ref, *refs):
        ins, outs = refs[:3 * len(groups)], refs[3 * len(groups):]

        def update(i, g, idx):
            w_ref, m_ref, v_ref = ins[3 * i:3 * i + 3]
            delta, new_m, new_v = _adamw_math(w_ref[idx], g, m_ref[idx], v_ref[idx])
            for o_ref, val in zip(outs[4 * i:4 * i + 4], (g, delta, new_m, new_v)):
                o_ref[idx] = val

        for i, (w, _, _) in enumerate(vectors):
            for t in range(_rows_of(w.shape[1])):
                width = min(128, w.shape[1] - 128 * t)
                update(i, sum_ref[pl.ds(offsets[i] + t, 1), pl.ds(0, width)], (slice(None), pl.ds(128 * t, width)))
        for tap in range(conv[0].shape[0]):
            for blk in range(conv_blocks):
                src = offsets[-1] + tap * N_CHIPS * conv_blocks + chip_ref[0] * conv_blocks + blk
                update(len(vectors), sum_ref[pl.ds(src, 1), :], (pl.ds(tap, 1), pl.ds(128 * blk, 128)))

    def whole(a):
        return pl.BlockSpec(a.shape, lambda i, chip_ref: (0,) * a.ndim)

    operands = [t for group in groups for t in group]
    grid_spec = pltpu.PrefetchScalarGridSpec(
        num_scalar_prefetch=1, grid=(1,), in_specs=[whole(summed)] + [whole(t) for t in operands],
        out_specs=[whole(w) for w, _, _ in groups for _ in range(4)])
    res = pl.pallas_call(body, name="adamw_vectors", grid_spec=grid_spec,
                         out_shape=[jax.ShapeDtypeStruct(w.shape, F32) for w, _, _ in groups for _ in range(4)],
                         compiler_params=_params(("arbitrary",)))(chip.reshape(1).astype(jnp.int32), summed, *operands)
    return [res[4 * i:4 * i + 4] for i in range(len(groups))]


def _unpack(buf, shapes):
    flat, out, pos = buf.reshape(-1), [], 0
    for shape in shapes:
        size = math.prod(shape)
        out.append(flat[pos:pos + size].reshape(shape))
        pos += size
    return out


def kernel(x, mem, positions, g_mix, w_in, g_q, g_k, g_attn_out, conv_w, conv_b, dt_bias, a_log, d_skip, g_ssm_out, w_out, g_cross, g_mem, w_cq, w_ckv, g_cq, g_ck, w_co, g_mlp, w_up, w_down, loss_target, m_g_mix, m_w_in, m_g_q, m_g_k, m_g_attn_out, m_conv_w, m_conv_b, m_dt_bias, m_a_log, m_d_skip, m_g_ssm_out, m_w_out, m_g_cross, m_g_mem, m_w_cq, m_w_ckv, m_g_cq, m_g_ck, m_w_co, m_g_mlp, m_w_up, m_w_down, v_g_mix, v_w_in, v_g_q, v_g_k, v_g_attn_out, v_conv_w, v_conv_b, v_dt_bias, v_a_log, v_d_skip, v_g_ssm_out, v_w_out, v_g_cross, v_g_mem, v_w_cq, v_w_ckv, v_g_cq, v_g_ck, v_w_co, v_g_mlp, v_w_up, v_w_down):
    args = dict(locals())
    weights = {n: args[n][0] for n in WEIGHTS}
    mom_m = {n: args["m_" + n][0] for n in WEIGHTS}
    mom_v = {n: args["v_" + n][0] for n in WEIGHTS}
    x_idx, y_idx, c_idx = _place()
    chip = 2 * x_idx + y_idx

    shapes = {n: weights[n].shape for n in MATRICES}
    first, mid, late = ("w_in",), ("w_out", "w_cq", "w_ckv", "w_co"), ("w_up", "w_down")
    w_in_t, m_in_t, v_in_t = (jnp.swapaxes(t, 1, 2)[0] for t in (w_in, m_w_in, v_w_in))
    w_in_buf = [_cast_w_in_transposed(w_in_t, chip)]
    taps, tap_cols = weights["conv_w"].shape
    conv_parts = _small_allreduce(_pack([jnp.zeros((N_CHIPS, taps, tap_cols), F32).at[chip].set(0.5 * weights["conv_w"])]),
                                  "gather_conv_taps")
    sems_in, w_in_buf, token = _split_start("gather_ici_start_w_in", w_in_buf, _ici_plan(first, shapes), [3], after=(conv_parts,))
    bufs = [_cast_into_gathered(weights[n], n, chip, after=(token,)) for n in mid + late]
    plan = lambda refs: _ici_plan(mid, shapes)(refs[:4]) + _ici_plan(late, shapes)(refs[4:])
    sems_rest, bufs, token = _split_start("gather_ici_start_rest", bufs, plan, [12, 6], after=(token,))
    params = {n: weights[n].reshape(1, -1) for n in VECTORS}
    h_in = _rowwise(_norm_fn, [_full(x[0])], [_full(params["g_mix"])], [(D_MODEL, BF16, D_MODEL, 0, False)], name="norm_in",
                    after=(token,))[0]
    w_in_buf = _split_wait("gather_ici_wait_w_in", w_in_buf, sems_in[0], _ici_plan(first, shapes), token, h_in, m_in_t, v_in_t)
    pass_sems, w_in_buf, token = _split_start("gather_pass_start_w_in", w_in_buf, _pass_on_plan(first, shapes), [3])
    w_in_buf = _split_wait("gather_pass_wait_w_in", w_in_buf, pass_sems[0], _pass_on_plan(first, shapes), token)
    w_in_full = _w_in_columns(w_in_buf[0], to_shards=False)
    full = {"w_in": w_in_full,
            "w_dt": jnp.pad(w_in_full[:, D_MAIN:].reshape(D_MODEL, N_GROUPS, HEADS_PER_GROUP),
                            ((0, 0), (0, 0), (0, 128 - HEADS_PER_GROUP))).reshape(D_MODEL, DT_PAD)}
    in_flight = {}

    def more_weights(stage, after):
        if stage == "mixer_done":
            got = _split_wait("gather_ici_wait_mid", bufs[:4], sems_rest[0], _ici_plan(mid, shapes), after)
            sems, got, token = _split_start("gather_pass_start_mid", got, _pass_on_plan(mid, shapes), [12])
            return dict(zip(mid, _split_wait("gather_pass_wait_mid", got, sems[0], _pass_on_plan(mid, shapes), token)))
        if stage == "cross_started":
            got = _split_wait("gather_ici_wait_late", bufs[4:], sems_rest[1], _ici_plan(late, shapes), after)
            in_flight["late"] = _split_start("gather_pass_start_late", got, _pass_on_plan(late, shapes), [6])
            return {}
        sems, got, token = in_flight.pop("late")
        return dict(zip(late, _split_wait("gather_pass_wait_late", got, sems[0], _pass_on_plan(late, shapes), token, after)))

    params["conv_w"] = _unpack(conv_parts, [(N_CHIPS, taps, tap_cols)])[0].transpose(1, 0, 2).reshape(taps, N_CHIPS * tap_cols)

    groups = (("w_down",), ("w_up",), ("w_co", "w_cq", "w_ckv", "w_out"), ("w_in",))
    scattered = []

    class GradStore(dict):
        pending = None

        def __setitem__(self, name, value):
            super().__setitem__(name, value)
            if "w_main" in self and "w_dt" in self and "w_in" not in self:
                gw_in = lax.dynamic_update_slice(self["w_main"], _unpad_heads(self["w_dt"]), (0, D_MAIN))
                self["w_in"] = _w_in_columns(gw_in, to_shards=True)
            for group in groups:
                if name in group and all(n in self for n in group):
                    self.settle()
                    pieces = [self[n].reshape(N_CHIPS, 2, shapes[n][0] // 2, shapes[n][1]) for n in group]
                    if group == groups[-1]:
                        self.scatter(group, pieces, _sibling_swap(pieces, "grad_swap_" + group[0]))
                    else:
                        landing = [lax.empty((N_CHIPS,) + a.shape[2:], BF16) for a in pieces]
                        sems, thru, self.token = _split_start("grad_swap_start_" + group[0], pieces + landing,
                                                              _swap_plan(len(pieces)), [len(pieces)])
                        self.pending = (group, sems[0], thru)

        def settle(self, *after):
            if self.pending is not None:
                group, sems, thru = self.pending
                self.pending = None
                thru = _split_wait("grad_swap_wait_" + group[0], thru, sems, _swap_plan(len(group)), *after)
                self.scatter(group, thru[:len(group)], thru[len(group):])

        def scatter(self, group, pieces, from_sibling):
            sums = [_add_halves(a, r, c_idx, "add_halves_" + n) for n, a, r in zip(group, pieces, from_sibling)]
            landing = [lax.empty((3,) + s.shape[1:], BF16) for s in sums]
            sems, thru, self.token = _split_start("grad_scatter_start_" + group[0], sums + landing,
                                                  _scatter_plan(len(sums)), [3 * len(sums)])
            scattered.append((group, sems[0], thru))

    loss, grad_x, grads = _local_step(x[0], mem[0], positions[0], loss_target[0], params, full, more_weights, GradStore(),
                                      h_in)

    out_g, out_d, out_m, out_v = {}, {}, {}, {}

    def finish(entries, order, token):
        halves = {}
        for group, sems, thru in entries:
            thru = _split_wait("grad_scatter_wait_" + group[0], thru, sems, _scatter_plan(len(group)), token)
            for i, n in enumerate(group):
                halves[n] = _sum_chips(thru[i], thru[len(group) + i], chip, "sum_chips_" + n)
        sources = [halves[n] for n in order]
        landing = [lax.empty(s.shape, F32) for s in sources]
        sems, thru, token = _split_start("grad_share_start_" + order[0], sources + landing, _share_plan(len(order)),
                                         [1] * len(order))
        for i, n in enumerate(order):
            own, other = _split_wait("grad_share_wait_" + n, [thru[i], thru[len(order) + i]], sems[i], _share_plan(1), token)
            if n == "w_in":
                res_t = _adamw_w_in_transposed(w_in_t, own, other, m_in_t, v_in_t, c_idx)
                out_g[n], out_d[n], out_m[n], out_v[n] = (t.T for t in res_t)
            else:
                out_g[n], out_d[n], out_m[n], out_v[n] = _adamw_halves(weights[n], own, other, mom_m[n], mom_v[n], c_idx,
                                                                       "adamw_" + n)
            token = out_v[n]
        return token

    token = finish(scattered[:-1], ("w_cq", "w_co", "w_ckv", "w_out", "w_up", "w_down"), grad_x)
    finish(scattered[-1:], ("w_in",), token)

    names = VECTORS + ("conv_w",)
    summed = _small_allreduce(_pack_rows([grads[n] for n in names] + [loss]), "allreduce_vectors")
    total_loss = summed[sum(_slot_rows(grads[n].size) for n in names), 0]
    small_out = _adamw_vectors(summed, chip, [(args[n], args["m_" + n], args["v_" + n]) for n in VECTORS],
                               (weights["conv_w"], mom_m["conv_w"], mom_v["conv_w"]))
    for n, res in zip(names, small_out):
        out_g[n], out_d[n], out_m[n], out_v[n] = (t.reshape(weights[n].shape) for t in res)

    outs =[total_loss, grad_x[None]]
    for group in (out_g, out_d, out_m, out_v):
        outs += [group[n][None] for n in WEIGHTS]
    return tuple(outs)
```

```python
import functools
import math

import jax
import jax.numpy as jnp
from jax import lax
from jax.experimental import pallas as pl
from jax.experimental.pallas import tpu as pltpu

F32 = jnp.float32
BF16 = jnp.bfloat16

SEQ = 2048
D_MODEL = 2048
HEAD = 64
D_ATTN = 1024
D_SSM = 1024
N_GROUPS = 4
N_STATE = 128
CHUNK = 128
ATT_BLK = 128
N_MEM = 256
D_CROSS = 512
D_MAIN = 6144
N_DT = 16
DT_PAD = 512
ROT = 16
ROPE_THETA = 500000.0
EPS = 1e-6
NEG = -1e30
BRANCH_BLOCKS = (16, 4, 1)
DILATIONS = (1, 4, 16)

ADAM_LR, ADAM_B1, ADAM_B2, ADAM_EPS, ADAM_WD, ADAM_STEP = 0.001, 0.9, 0.999, 1e-08, 0.01, 10

VMEM_LIMIT = 56 * 1024 * 1024
MESH = pl.DeviceIdType.MESH


def _params(sem, **kw):
    return pltpu.CompilerParams(dimension_semantics=sem, vmem_limit_bytes=VMEM_LIMIT, **kw)


def _bdot(a, b, dims):
    return lax.dot_general(a.astype(BF16), b.astype(BF16), (dims, ((), ())), preferred_element_type=F32)


def _fdot(a, b, dims):
    return lax.dot_general(a, b, (dims, ((), ())), preferred_element_type=F32, precision=lax.Precision.HIGHEST)


NN = ((1,), (0,))
NT = ((1,), (1,))
TN = ((0,), (0,))


def _tile(n, want):
    t = min(n, want)
    while n % t:
        t //= 2
    return t


def _matmul(a, b, *, mode, name, outs, extra=(), vecs=(), epilogue=None, col_shards=1, after=(), n_cols=None, out_cols=None,
            tile_rows=0, tile_sums=0, tm=1024, tn=1024, tk=2048):
    if mode == "nn":
        (m, k), n = a.shape, b.shape[1]
    elif mode == "nt":
        (m, k), n = a.shape, b.shape[0]
    else:
        (k, m), n = a.shape, b.shape[1]
    n = n if n_cols is None else n_cols
    tm, tn, tk = _tile(m, tm), _tile(n // col_shards, tn), _tile(k, tk)
    nk = k // tk
    per_shard = n // col_shards // tn
    dims = {"nn": NN, "nt": NT, "tn": TN}[mode]
    a_spec = pl.BlockSpec((tk, tm), lambda i, j, kk: (kk, i)) if mode == "tn" else pl.BlockSpec((tm, tk), lambda i, j, kk: (i, kk))
    b_spec = pl.BlockSpec((tn, tk), lambda i, j, kk: (j, kk)) if mode == "nt" else pl.BlockSpec((tk, tn), lambda i, j, kk: (kk, j))
    o_spec = pl.BlockSpec((tm, tn), lambda i, j, kk: (i, j))
    n_extra, n_out, n_after = len(extra) + len(vecs), len(outs), len(after)

    def body(a_ref, b_ref, *rest):
        extra_refs, out_refs, acc_ref = rest[:n_extra], rest[n_extra + n_after:-1], rest[-1]

        def finish(acc):
            res = (acc,) if epilogue is None else epilogue(acc, *[e[...] for e in extra_refs])
            for o_ref, r in zip(out_refs[:n_out], res):
                o_ref[...] = r.astype(o_ref.dtype)
            for o_ref, r in zip(out_refs[n_out:], res[n_out:]):
                o_ref[...] = jnp.broadcast_to(r, o_ref.shape)

        if nk == 1:
            finish(_bdot(a_ref[...], b_ref[...], dims))
            return
        kk = pl.program_id(2)

        @pl.when(kk == 0)
        def _():
            acc_ref[...] = jnp.zeros_like(acc_ref)

        acc_ref[...] += _bdot(a_ref[...], b_ref[...], dims)

        @pl.when(kk == nk - 1)
        def _():
            finish(acc_ref[...])

    if col_shards == 1:
        out_specs, out_dims = [o_spec] * n_out, (m, n if out_cols is None else out_cols)
    else:
        sharded = pl.BlockSpec((None, tm, tn), lambda i, j, kk: (j // per_shard, i, j % per_shard))
        out_specs, out_dims = [sharded] * n_out, (col_shards, m, n // col_shards)
    res = pl.pallas_call(
        body, name=name, grid=(m // tm, n // tn, nk),
        in_specs=[a_spec, b_spec] + [o_spec] * len(extra) + [pl.BlockSpec((1, tn), lambda i, j, kk: (0, j))] * len(vecs)
        + [pl.BlockSpec(memory_space=pl.ANY)] * n_after,
        out_specs=out_specs + [pl.BlockSpec((8, tn), lambda i, j, kk: (i, j))] * tile_rows
        + [pl.BlockSpec((8, 128), lambda i, j, kk: (i, j))] * tile_sums,
        out_shape=[jax.ShapeDtypeStruct(out_dims, dt) for dt in outs] + [jax.ShapeDtypeStruct((m // tm * 8, n), F32)] * tile_rows
        + [jax.ShapeDtypeStruct((m // tm * 8, n // tn * 128), F32)] * tile_sums,
        scratch_shapes=[pltpu.VMEM((tm, tn) if nk > 1 else (8, 128), F32)],
        compiler_params=_params(("parallel", "parallel", "arbitrary")),
    )(a, b, *extra, *vecs, *after)
    res = (list(res[:n_out]) + [jnp.sum(t[::8], axis=0, keepdims=True) for t in res[n_out:n_out + tile_rows]]
           + [t[::8, ::128] for t in res[n_out + tile_rows:]])
    return res[0] if len(res) == 1 else res


def _row_spec(tr, bw, cb, per_group):
    return pl.BlockSpec((tr, bw), (lambda g, i: (i, cb + g)) if per_group else (lambda g, i: (i, cb)))


def _vec_spec(bw, cb, per_group):
    return pl.BlockSpec((1, bw), (lambda g, i: (0, cb + g)) if per_group else (lambda g, i: (0, cb)))


def _rowwise(fn, rows, vecs, outs, *, name, n_rows=SEQ, tr=512, groups=1, after=()):
    n_r, n_v, n_after = len(rows), len(vecs), len(after)

    def body(*refs):
        vals = [r[...].astype(F32) for r in refs[:n_r + n_v]]
        res = fn(*vals)
        for o_ref, r in zip(refs[n_r + n_v + n_after:], res):
            o_ref[...] = r.astype(o_ref.dtype)

    res = pl.pallas_call(
        body, name=name, grid=(groups, n_rows // tr),
        in_specs=[_row_spec(tr, bw, cb, pg) for _, bw, cb, pg in rows] + [_vec_spec(bw, cb, pg) for _, bw, cb, pg in vecs]
        + [pl.BlockSpec(memory_space=pl.ANY)] * n_after,
        out_specs=[_row_spec(tr, bw, cb, pg) for _, _, bw, cb, pg in outs],
        out_shape=[jax.ShapeDtypeStruct((n_rows, w), dt) for w, dt, _, _, _ in outs],
        compiler_params=_params(("parallel", "parallel")),
    )(*[r[0] for r in rows], *[v[0] for v in vecs], *after)
    return res


def _rowwise_vjp(fn, rows, vecs, cts, row_grads, vec_grads, *, name, n_rows=SEQ, tr=512, groups=1, after=()):
    n_r, n_v, n_after = len(rows), len(vecs), len(after)
    ct_ops = [op for group in cts for op in group]
    ct_sizes = [len(group) for group in cts]
    res_ops = [g[6] for g in row_grads if g[6] is not None]
    n_ct, n_res, n_rg = len(ct_ops), len(res_ops), len(row_grads)

    def body(*refs):
        vals = [r[...].astype(F32) for r in refs[:n_r + n_v]]
        pos = n_r + n_v
        ct_vals = []
        for size in ct_sizes:
            acc = refs[pos][...].astype(F32)
            for t in range(1, size):
                acc = acc + refs[pos + t][...].astype(F32)
            ct_vals.append(acc)
            pos += size
        res_refs = refs[pos:pos + n_res]
        out_refs = refs[pos + n_res + n_after:]
        _, pullback = jax.vjp(fn, *vals)
        grads = pullback(tuple(ct_vals))
        r_i = 0
        for o_ref, g in zip(out_refs[:n_rg], row_grads):
            val = grads[g[0]]
            if g[6] is not None:
                val = val + res_refs[r_i][...].astype(F32)
                r_i += 1
            o_ref[...] = val.astype(o_ref.dtype)
        first = (pl.program_id(1) == 0)
        for o_ref, g in zip(out_refs[n_rg:], vec_grads):
            val = jnp.sum(grads[n_r + g[0]], axis=0, keepdims=True)
            init = first if g[4] else jnp.logical_and(first, pl.program_id(0) == 0)

            @pl.when(init)
            def _(o_ref=o_ref, val=val):
                o_ref[...] = val

            @pl.when(jnp.logical_not(init))
            def _(o_ref=o_ref, val=val):
                o_ref[...] += val

    in_specs = [_row_spec(tr, bw, cb, pg) for _, bw, cb, pg in rows] + [_vec_spec(bw, cb, pg) for _, bw, cb, pg in vecs]
    in_specs += [_row_spec(tr, bw, cb, pg) for _, bw, cb, pg in ct_ops + res_ops] + [pl.BlockSpec(memory_space=pl.ANY)] * n_after
    out_specs =[_row_spec(tr, g[3], g[4], g[5]) for g in row_grads] + [_vec_spec(g[2], g[3], g[4]) for g in vec_grads]
    out_shape = [jax.ShapeDtypeStruct((n_rows, g[1]), g[2]) for g in row_grads]
    out_shape += [jax.ShapeDtypeStruct((1, g[1]), F32) for g in vec_grads]
    return pl.pallas_call(
        body, name=name, grid=(groups, n_rows // tr),
        in_specs=in_specs, out_specs=out_specs, out_shape=out_shape,
        compiler_params=_params(("arbitrary", "arbitrary")),
    )(*[r[0] for r in rows], *[v[0] for v in vecs], *[c[0] for c in ct_ops], *[r[0] for r in res_ops], *after)


def _full(arr, width=None):
    return (arr, arr.shape[1] if width is None else width, 0, False)


def _make_xor(sh):
    def raw(x):
        n = x.shape[-1]
        lane = lax.broadcasted_iota(jnp.int32, x.shape, x.ndim - 1)
        up = pltpu.roll(x, n - sh, x.ndim - 1)
        down = pltpu.roll(x, sh, x.ndim - 1)
        return jnp.where((lane & sh) == 0, up, down)

    f = jax.custom_vjp(raw)
    f.defvjp(lambda x: (raw(x), None), lambda _, ct: (raw(ct),))
    return f


_SWAP_ROPE_HALVES = _make_xor(ROT // 2)


def _head_sum(x):
    n = x.shape[-1]
    same_head = (lax.broadcasted_iota(jnp.int32, (n, n), 0) // HEAD) == (lax.broadcasted_iota(jnp.int32, (n, n), 1) // HEAD)
    return _fdot(x, same_head.astype(F32), NN)


def _rms(x, g):
    return x * lax.rsqrt(jnp.mean(x * x, axis=-1, keepdims=True) + EPS) * g


def _head_rms_rope(x, g, cos, sin, scale):
    y = x * lax.rsqrt(_head_sum(x * x) * (1.0 / HEAD) + EPS) * g
    return (y * cos + _SWAP_ROPE_HALVES(y) * sin) * scale


def _qk_fn(q, k, v, cos, sin, gq, gk):
    return (_head_rms_rope(q, gq, cos, sin, HEAD ** -0.5), _head_rms_rope(k, gk, cos, sin, 1.0), v)


def _norm_fn(x, g):
    return (_rms(x, g),)


def _merge_fn(o0, o1, o2, l0, l1, l2, g):
    m = lax.stop_gradient(jnp.maximum(jnp.maximum(l0, l1), l2))
    e0, e1, e2 = jnp.exp(l0 - m), jnp.exp(l1 - m), jnp.exp(l2 - m)
    mix = (e0 * o0 + e1 * o1 + e2 * o2) / (e0 + e1 + e2)
    return (_rms(mix, g),)


def _gate_fn(y, z, g):
    return (_rms(y * (z * jax.nn.sigmoid(z)), g),)


def _attn_pair(q, kc, vc, kp=None, vp=None, has_prev=None):
    pick0, pick1 = _head_picks()
    k_band, v_band, mask = _attn_band(kc, vc, kp, vp, has_prev)
    s = jnp.where(mask, _bdot(jnp.concatenate([q * pick0, q * pick1], axis=0), k_band, NT), NEG)
    m = jnp.max(s, axis=-1, keepdims=True)
    p = jnp.exp(s - m)
    den = jnp.sum(p, axis=-1, keepdims=True)
    acc = _bdot(p, v_band, NN) * (1.0 / den)
    lse_rows = m + jnp.log(den)
    o = pick0 * acc[:ATT_BLK] + pick1 * acc[ATT_BLK:]
    lse = pick0 * lse_rows[:ATT_BLK] + pick1 * lse_rows[ATT_BLK:]
    return o, lse


def _head_picks():
    lane = lax.broadcasted_iota(jnp.int32, (1, 2 * HEAD), 1)
    return (lane < HEAD).astype(F32), (lane >= HEAD).astype(F32)


def _attn_band(kc, vc, kp, vp, has_prev):
    n_keys = ATT_BLK if kp is None else 2 * ATT_BLK
    qi = lax.broadcasted_iota(jnp.int32, (2 * ATT_BLK, n_keys), 0) & (ATT_BLK - 1)
    kj = lax.broadcasted_iota(jnp.int32, (2 * ATT_BLK, n_keys), 1)
    if kp is None:
        return kc, vc, qi >= kj
    in_prev = jnp.logical_and(jnp.logical_and(kj < ATT_BLK, kj >= qi), has_prev)
    mask = jnp.logical_or(in_prev, jnp.logical_and(kj >= ATT_BLK, qi >= kj - ATT_BLK))
    return jnp.concatenate([kp, kc], axis=0), jnp.concatenate([vp, vc], axis=0), mask


def _attn_config(b):
    r = DILATIONS[b]
    return r, ATT_BLK * r, (D_ATTN if r == 1 else 128), BRANCH_BLOCKS[b] > 1


RESIDUES_UNROLLED = 8


def _for_residues(r, fn):
    if r <= RESIDUES_UNROLLED:
        for rho in range(r):
            fn(rho)
    else:
        def step(t, carry):
            for u in range(RESIDUES_UNROLLED):
                fn(RESIDUES_UNROLLED * t + u)
            return carry

        lax.fori_loop(0, r // RESIDUES_UNROLLED, step, 0)


def _strided_rows(start, r):
    if r > 1:
        return pl.ds(start, ATT_BLK, stride=r)
    return pl.ds(start if isinstance(start, int) else pl.multiple_of(start, ATT_BLK), ATT_BLK)


def _attention_fwd(qn, kn, vn, b):
    r, rows, lanes, with_prev = _attn_config(b)
    cur = pl.BlockSpec((rows, lanes), lambda g, n: (n, g))
    prev = pl.BlockSpec((rows, lanes), lambda g, n: (jnp.maximum(n - 1, 0), g))

    def body(*refs):
        ins, (o_ref, l_ref) = refs[:-2], refs[-2:]
        has_prev = pl.program_id(1) > 0

        def one(rho):
            sub = _strided_rows(rho, r)
            for pair in range(lanes // 128):
                sl = pl.ds(pair * 128, 128)
                args = [ref[sub, sl] for ref in ins] + ([has_prev] if with_prev else [])
                o_ref[sub, sl], l_ref[sub, sl] = _attn_pair(*args)

        _for_residues(r, one)

    operands = (qn, kn, vn, kn, vn) if with_prev else (qn, kn, vn)
    return pl.pallas_call(
        body, name="attn_fwd_%d" % r, grid=(D_ATTN // lanes, SEQ // rows),
        in_specs=[cur, cur, cur] + ([prev, prev] if with_prev else []), out_specs=[cur, cur],
        out_shape=[jax.ShapeDtypeStruct((SEQ, D_ATTN), F32)] * 2,
        compiler_params=_params(("parallel", "parallel")),
    )(*operands)


def _attn_pair_bwd(q, kc, vc, kp, vp, o, lse, do, dl, has_prev):
    pick0, pick1 = _head_picks()
    lane = lax.broadcasted_iota(jnp.int32, (1, 2 * HEAD), 1)
    k_band, v_band, mask = _attn_band(kc, vc, kp, vp, has_prev)
    q2 = jnp.concatenate([q * pick0, q * pick1], axis=0)
    do2 = jnp.concatenate([do * pick0, do * pick1], axis=0)
    lse2 = jnp.concatenate([jnp.sum(lse * (lane == 0).astype(F32), axis=-1, keepdims=True),
                            jnp.sum(lse * (lane == HEAD).astype(F32), axis=-1, keepdims=True)], axis=0)
    base = jnp.sum(jnp.concatenate([dl * pick0, dl * pick1], axis=0) - do2 * jnp.concatenate([o, o], axis=0),
                   axis=-1, keepdims=True)
    p = jnp.exp(jnp.where(mask, _bdot(q2, k_band, NT), NEG) - lse2)
    ds = p * (_bdot(do2, v_band, NT) + base)
    dq2 = _bdot(ds, k_band, NN)
    dq = pick0 * dq2[:ATT_BLK] + pick1 * dq2[ATT_BLK:]
    dk, dv = _bdot(ds, q2, TN), _bdot(p, do2, TN)
    if kp is None:
        return dq, dk, dv
    return dq, dk[ATT_BLK:], dv[ATT_BLK:], dk[:ATT_BLK], dv[:ATT_BLK]


def _attention_bwd(qn, kn, vn, o, lse, do, dl, b):
    r, rows, lanes, with_prev = _attn_config(b)
    cur = pl.BlockSpec((rows, lanes), lambda g, n: (n, g))
    prev = pl.BlockSpec((rows, lanes), lambda g, n: (jnp.maximum(n - 1, 0), g))
    whole = pl.BlockSpec((SEQ, lanes), lambda g, n: (0, g))
    n_in = 5 if with_prev else 3

    def body(*refs):
        ins, (o_ref, l_ref, do_ref, dl_ref, dq_ref, dk_ref, dv_ref) = refs[:n_in], refs[n_in:]
        n = pl.program_id(1)

        @pl.when(n == 0)
        def _():
            dk_ref[...] = jnp.zeros_like(dk_ref)
            dv_ref[...] = jnp.zeros_like(dv_ref)

        def one(rho):
            sub = _strided_rows(rho, r)
            sub_c = _strided_rows(n * rows + rho, r)
            sub_p = _strided_rows(jnp.maximum(n - 1, 0) * rows + rho, r)
            for pair in range(lanes // 128):
                sl = pl.ds(pair * 128, 128)
                vals = [ref[sub, sl] for ref in ins] + ([] if with_prev else [None, None])
                grads = _attn_pair_bwd(*vals, o_ref[sub, sl], l_ref[sub, sl], do_ref[sub, sl], dl_ref[sub, sl], n > 0)
                dq_ref[sub, sl] = grads[0]
                dk_ref[sub_c, sl] += grads[1]
                dv_ref[sub_c, sl] += grads[2]
                if with_prev:
                    dk_ref[sub_p, sl] += grads[3]
                    dv_ref[sub_p, sl] += grads[4]

        _for_residues(r, one)

    operands = (qn, kn, vn, kn, vn) if with_prev else (qn, kn, vn)
    return pl.pallas_call(
        body, name="attn_bwd_%d" % r, grid=(D_ATTN // lanes, SEQ // rows),
        in_specs=[cur, cur, cur] + ([prev, prev] if with_prev else []) + [cur] * 4, out_specs=[cur, whole, whole],
        out_shape=[jax.ShapeDtypeStruct((SEQ, D_ATTN), F32)] * 3,
        compiler_params=_params(("parallel", "arbitrary")),
    )(*operands, o, lse, do, dl)


CONV_COLS = 256
XBC_BLOCK0 = (3 * D_ATTN + D_SSM) // CONV_COLS


def _shift_rows(x, s):
    n = x.shape[0]
    t = lax.broadcasted_iota(jnp.int32, x.shape, 0)
    if s >= 0:
        return jnp.where(t >= s, pltpu.roll(x, s, 0), 0.0)
    return jnp.where(t < n + s, pltpu.roll(x, n + s, 0), 0.0)


def _conv_pre(x, w_ref, b_ref):
    delayed = [_shift_rows(x, 3 - k) for k in range(3)]
    pre = b_ref[...] + w_ref[3:4, :] * x
    for k in range(3):
        pre = pre + w_ref[k:k + 1, :] * delayed[k]
    return pre, delayed


def _conv_fwd(proj, conv_w, conv_b):
    cols = conv_w.shape[1]

    def body(x_ref, w_ref, b_ref, o_ref):
        pre, _ = _conv_pre(x_ref[...], w_ref, b_ref)
        o_ref[...] = pre * jax.nn.sigmoid(pre)

    blk = pl.BlockSpec((SEQ, CONV_COLS), lambda j: (0, j))
    return pl.pallas_call(
        body, name="conv_fwd", grid=(cols // CONV_COLS,),
        in_specs=[pl.BlockSpec((SEQ, CONV_COLS), lambda j: (0, XBC_BLOCK0 + j)),
                  pl.BlockSpec((4, CONV_COLS), lambda j: (0, j)), pl.BlockSpec((1, CONV_COLS), lambda j: (0, j))],
        out_specs=blk, out_shape=jax.ShapeDtypeStruct((SEQ, cols), F32),
        compiler_params=_params(("parallel",)),
    )(proj, conv_w, conv_b)


def _conv_bwd(proj, conv_w, conv_b, dxs, db, dc):
    cols = conv_w.shape[1]
    x_blocks, b_blocks = dxs.shape[1] // CONV_COLS, db.shape[1] // CONV_COLS

    def body(x_ref, w_ref, b_ref, dxs_ref, db_ref_in, dc_ref_in, dx_ref, dw_ref, db_ref):
        j = pl.program_id(0)
        dy = jnp.where(j < x_blocks, dxs_ref[...], jnp.where(j < x_blocks + b_blocks, db_ref_in[...], dc_ref_in[...]))
        x = x_ref[...]
        pre, delayed = _conv_pre(x, w_ref, b_ref)
        sg = jax.nn.sigmoid(pre)
        dpre = dy * (sg * (1.0 + pre * (1.0 - sg)))
        db_ref[...] = jnp.sum(dpre, axis=0, keepdims=True)
        dx = w_ref[3:4, :] * dpre
        dw_ref[3:4, :] = jnp.sum(dpre * x, axis=0, keepdims=True)
        for k in range(3):
            dx = dx + w_ref[k:k + 1, :] * _shift_rows(dpre, k - 3)
            dw_ref[k:k + 1, :] = jnp.sum(dpre * delayed[k], axis=0, keepdims=True)
        dw_ref[4:8, :] = jnp.zeros((4, CONV_COLS), F32)
        dx_ref[...] = dx.astype(dx_ref.dtype)

    blk = pl.BlockSpec((SEQ, CONV_COLS), lambda j: (0, j))
    parts = [pl.BlockSpec((SEQ, CONV_COLS), lambda j: (0, jnp.minimum(j, x_blocks - 1))),
             pl.BlockSpec((SEQ, CONV_COLS), lambda j: (0, jnp.clip(j - x_blocks, 0, b_blocks - 1))),
             pl.BlockSpec((SEQ, CONV_COLS), lambda j: (0, jnp.clip(j - x_blocks - b_blocks, 0, b_blocks - 1)))]
    return pl.pallas_call(
        body, name="conv_bwd", grid=(cols // CONV_COLS,),
        in_specs=[pl.BlockSpec((SEQ, CONV_COLS), lambda j: (0, XBC_BLOCK0 + j)),
                  pl.BlockSpec((4, CONV_COLS), lambda j: (0, j)), pl.BlockSpec((1, CONV_COLS), lambda j: (0, j))] + parts,
        out_specs=[blk, pl.BlockSpec((8, CONV_COLS), lambda j: (0, j)), pl.BlockSpec((1, CONV_COLS), lambda j: (0, j))],
        out_shape=[jax.ShapeDtypeStruct((SEQ, cols), BF16), jax.ShapeDtypeStruct((8, cols), F32),
                   jax.ShapeDtypeStruct((1, cols), F32)],
        compiler_params=_params(("parallel",)),
    )(proj, conv_w, conv_b, dxs, db, dc)


HEADS_PER_GROUP = 4


GROUP_WIDTH = HEADS_PER_GROUP * HEAD


def _ssd_chunk(x, bm, cm, dtr, bias, alog, dsk, h):
    row = lax.broadcasted_iota(jnp.int32, (CHUNK, CHUNK), 0)
    col = lax.broadcasted_iota(jnp.int32, (CHUNK, CHUNK), 1)
    causal = row >= col
    z = dtr + bias
    dt = jnp.maximum(z, 0.0) + jnp.log(1.0 + jnp.exp(-jnp.abs(z)))
    acs = _fdot(causal.astype(F32), dt * -jnp.exp(alog), NN)
    acs_t, dt_t = acs.T, dt.T
    cb = _bdot(cm, bm, NT)
    lane = lax.broadcasted_iota(jnp.int32, (1, CHUNK), 1)
    sub = lax.broadcasted_iota(jnp.int32, (CHUNK, 1), 0)
    wide = lax.broadcasted_iota(jnp.int32, (1, GROUP_WIDTH), 1) // HEAD
    tall = lax.broadcasted_iota(jnp.int32, (GROUP_WIDTH, 1), 0) // HEAD
    acs_last = jnp.sum(acs * (sub == CHUNK - 1).astype(F32), axis=0, keepdims=True)
    to_lanes = (lax.broadcasted_iota(jnp.int32, (CHUNK, GROUP_WIDTH), 0)
                == lax.broadcasted_iota(jnp.int32, (CHUNK, GROUP_WIDTH), 1) // HEAD).astype(F32)
    grow = _fdot(jnp.exp(acs), to_lanes, NN)
    keep = _fdot(jnp.exp(acs_last - acs) * dt, to_lanes, NN)
    w_parts, x_parts, skip, carry = [], [], 0.0, 0.0
    for j in range(HEADS_PER_GROUP):
        on_lane, on_sub = (lane == j).astype(F32), (sub == j).astype(F32)
        acs_c = jnp.sum(acs * on_lane, axis=1, keepdims=True)
        acs_r = jnp.sum(acs_t * on_sub, axis=0, keepdims=True)
        dt_r = jnp.sum(dt_t * on_sub, axis=0, keepdims=True)
        w_parts.append(cb * jnp.exp(jnp.where(causal, acs_c - acs_r, NEG)) * dt_r)
        x_parts.append(x * (wide == j).astype(F32))
        skip = skip + jnp.sum(dsk * on_lane, axis=1, keepdims=True) * (wide == j).astype(F32)
        carry = carry + jnp.sum(jnp.exp(acs_last) * on_lane, axis=1, keepdims=True) * (tall == j).astype(F32)
    y_diag = _bdot(jnp.concatenate(w_parts, axis=1), jnp.concatenate(x_parts, axis=0), NN)
    y = y_diag + _bdot(cm, h, NT) * grow + skip * x
    return y, h * carry + _bdot(x * keep, bm, TN)


GROUPS_PER_STEP = 4
SSD_STEPS = N_GROUPS // GROUPS_PER_STEP


def _ssd_specs(reverse):
    n_chunks = SEQ // CHUNK
    c_of = (lambda c: n_chunks - 1 - c) if reverse else (lambda c: c)
    x_w, n_w, dt_w = GROUPS_PER_STEP * GROUP_WIDTH, GROUPS_PER_STEP * N_STATE, GROUPS_PER_STEP * 128
    x_spec = pl.BlockSpec((CHUNK, x_w), lambda g, c: (c_of(c), g))
    b_spec = pl.BlockSpec((CHUNK, n_w), lambda g, c: (c_of(c), D_SSM // n_w + g))
    c_spec = pl.BlockSpec((CHUNK, n_w), lambda g, c: (c_of(c), (D_SSM + N_GROUPS * N_STATE) // n_w + g))
    dt_spec = pl.BlockSpec((CHUNK, dt_w), lambda g, c: (c_of(c), g))
    vec_spec = pl.BlockSpec((1, dt_w), lambda g, c: (0, g))
    h_spec = pl.BlockSpec((None, GROUPS_PER_STEP, GROUP_WIDTH, N_STATE), lambda g, c: (c_of(c), g, 0, 0))
    return x_spec, b_spec, c_spec, dt_spec, vec_spec, h_spec


def _group_slices(u):
    return pl.ds(u * GROUP_WIDTH, GROUP_WIDTH), pl.ds(u * N_STATE, N_STATE), pl.ds(u * 128, 128)


def _ssd_gated_chunk(x, bm, cm, dtr, bias, alog, dsk, h, z, g_out):
    y, h_new = _ssd_chunk(x, bm, cm, dtr, bias, alog, dsk, h)
    return _gate_fn(y, z, g_out)[0], h_new


def _ssd_gate_specs(reverse):
    x_spec = _ssd_specs(reverse)[0]
    z_block0 = 3 * D_ATTN // x_spec.block_shape[1]
    z_spec = pl.BlockSpec(x_spec.block_shape, lambda g, c: (x_spec.index_map(g, c)[0], z_block0 + g))
    return z_spec, pl.BlockSpec((1, x_spec.block_shape[1]), lambda g, c: (0, g))


def _ssd_fwd(xbc, dt_raw, bias, alog, dsk, proj, g_out):
    x_spec, b_spec, c_spec, dt_spec, vec_spec, h_spec = _ssd_specs(False)
    z_spec, g_spec = _ssd_gate_specs(False)

    def body(x_ref, b_ref, c_ref, dt_ref, bias_ref, alog_ref, dsk_ref, z_ref, g_ref, ssm_ref, hin_ref, h_scr):
        @pl.when(pl.program_id(1) == 0)
        def _():
            h_scr[...] = jnp.zeros_like(h_scr)

        for u in range(GROUPS_PER_STEP):
            xs, ns, ds = _group_slices(u)
            h = h_scr[u]
            hin_ref[u] = h
            ssm, h_scr[u] = _ssd_gated_chunk(x_ref[:, xs], b_ref[:, ns], c_ref[:, ns], dt_ref[:, ds], bias_ref[:, ds],
                                             alog_ref[:, ds], dsk_ref[:, ds], h, z_ref[:, xs], g_ref[:, xs])
            ssm_ref[:, xs] = ssm.astype(ssm_ref.dtype)

    return pl.pallas_call(
        body, name="ssd_fwd", grid=(SSD_STEPS, SEQ // CHUNK),
        in_specs=[x_spec, b_spec, c_spec, dt_spec, vec_spec, vec_spec, vec_spec, z_spec, g_spec],
        out_specs=[x_spec, h_spec],
        out_shape=[jax.ShapeDtypeStruct((SEQ, D_SSM), BF16),
                   jax.ShapeDtypeStruct((SEQ // CHUNK, N_GROUPS, GROUP_WIDTH, N_STATE), F32)],
        scratch_shapes=[pltpu.VMEM((GROUPS_PER_STEP, GROUP_WIDTH, N_STATE), F32)],
        compiler_params=_params(("parallel", "arbitrary")),
    )(xbc, xbc, xbc, dt_raw, bias, alog, dsk, proj, g_out)


def _ssd_bwd(xbc, dt_raw, bias, alog, dsk, h_in, proj, g_out, dmix):
    x_spec, b_spec, c_spec, dt_spec, vec_spec, h_spec = _ssd_specs(True)
    z_spec, g_spec = _ssd_gate_specs(True)
    ct_block0 = D_ATTN // x_spec.block_shape[1]
    ct_spec = pl.BlockSpec(x_spec.block_shape, lambda g, c: (x_spec.index_map(g, c)[0], ct_block0 + g))

    def body(x_ref, b_ref, c_ref, dt_ref, bias_ref, alog_ref, dsk_ref, hin_ref, z_ref, g_ref, ct_ref,
             dx_ref, db_ref, dc_ref, ddt_ref, dbias_ref, dalog_ref, ddsk_ref, dz_ref, dg_ref, dh_scr):
        first = pl.program_id(1) == 0

        @pl.when(first)
        def _():
            dh_scr[...] = jnp.zeros_like(dh_scr)

        for u in range(GROUPS_PER_STEP):
            xs, ns, ds = _group_slices(u)
            _, pullback = jax.vjp(_ssd_gated_chunk, x_ref[:, xs], b_ref[:, ns], c_ref[:, ns], dt_ref[:, ds], bias_ref[:, ds],
                                  alog_ref[:, ds], dsk_ref[:, ds], hin_ref[u], z_ref[:, xs], g_ref[:, xs])
            g = pullback((ct_ref[:, xs], dh_scr[u]))
            dx_ref[:, xs], db_ref[:, ns], dc_ref[:, ns] = g[0], g[1], g[2]
            ddt_ref[:, ds] = g[3].astype(ddt_ref.dtype)
            dh_scr[u] = g[7]
            dz_ref[:, xs] = g[8].astype(dz_ref.dtype)
            sums = ((dbias_ref, g[4], ds), (dalog_ref, g[5], ds), (ddsk_ref, g[6], ds),
                    (dg_ref, jnp.sum(g[9], axis=0, keepdims=True), xs))
            for o_ref, val, lanes in sums:
                @pl.when(first)
                def _(o_ref=o_ref, val=val, lanes=lanes):
                    o_ref[:, lanes] = val

                @pl.when(jnp.logical_not(first))
                def _(o_ref=o_ref, val=val, lanes=lanes):
                    o_ref[:, lanes] += val

    n_chunks = SEQ // CHUNK
    out_b = pl.BlockSpec((CHUNK, GROUPS_PER_STEP * N_STATE), lambda g, c: (n_chunks - 1 - c, g))
    return pl.pallas_call(
        body, name="ssd_bwd", grid=(SSD_STEPS, n_chunks),
        in_specs=[x_spec, b_spec, c_spec, dt_spec, vec_spec, vec_spec, vec_spec, h_spec, z_spec, g_spec, ct_spec],
        out_specs=[x_spec, out_b, out_b, dt_spec, vec_spec, vec_spec, vec_spec, x_spec, g_spec],
        out_shape=[jax.ShapeDtypeStruct((SEQ, D_SSM), F32), jax.ShapeDtypeStruct((SEQ, N_GROUPS * N_STATE), F32),
                   jax.ShapeDtypeStruct((SEQ, N_GROUPS * N_STATE), F32), jax.ShapeDtypeStruct((SEQ, DT_PAD), BF16),
                   jax.ShapeDtypeStruct((1, DT_PAD), F32), jax.ShapeDtypeStruct((1, DT_PAD), F32),
                   jax.ShapeDtypeStruct((1, DT_PAD), F32), jax.ShapeDtypeStruct((SEQ, D_SSM), BF16),
                   jax.ShapeDtypeStruct((1, D_SSM), F32)],
        scratch_shapes=[pltpu.VMEM((GROUPS_PER_STEP, GROUP_WIDTH, N_STATE), F32)],
        compiler_params=_params(("parallel", "arbitrary")),
    )(xbc, xbc, xbc, dt_raw, bias, alog, dsk, h_in, proj, g_out, dmix)


CROSS_HEAD = 128
CROSS_ROWS = 1024


def _cross_head(q, k, v, gq, gk):
    qn = _rms(q, gq) * (CROSS_HEAD ** -0.5)
    kn = _rms(k, gk)
    s = _bdot(qn, kn, NT)
    p = jnp.exp(s - lax.stop_gradient(jnp.max(s, axis=-1, keepdims=True)))
    return _bdot(p, v, NN) * (1.0 / jnp.sum(p, axis=-1, keepdims=True))


def _cross_specs():
    q_spec = pl.BlockSpec((CROSS_ROWS, CROSS_HEAD), lambda h, i: (i, h))
    k_spec = pl.BlockSpec((N_MEM, CROSS_HEAD), lambda h, i: (0, h))
    v_spec = pl.BlockSpec((N_MEM, CROSS_HEAD), lambda h, i: (0, 4 + h))
    g_spec = pl.BlockSpec((1, CROSS_HEAD), lambda h, i: (0, 0))
    return q_spec, k_spec, v_spec, g_spec


def _cross_fwd(qc, kv, gq, gk):
    q_spec, k_spec, v_spec, g_spec = _cross_specs()

    def body(q_ref, k_ref, v_ref, gq_ref, gk_ref, o_ref):
        o_ref[...] = _cross_head(q_ref[...], k_ref[...], v_ref[...], gq_ref[...], gk_ref[...]).astype(o_ref.dtype)

    return pl.pallas_call(
        body, name="cross_fwd", grid=(4, SEQ // CROSS_ROWS),
        in_specs=[q_spec, k_spec, v_spec, g_spec, g_spec], out_specs=q_spec,
        out_shape=jax.ShapeDtypeStruct((SEQ, D_CROSS), BF16),
        compiler_params=_params(("parallel", "parallel")),
    )(qc, kv, kv, gq, gk)


def _cross_bwd(qc, kv, gq, gk, do):
    q_spec, k_spec, v_spec, g_spec = _cross_specs()

    def body(q_ref, k_ref, v_ref, gq_ref, gk_ref, do_ref, dq_ref, dk_ref, dv_ref, dgq_ref, dgk_ref):
        _, pullback = jax.vjp(_cross_head, q_ref[...], k_ref[...], v_ref[...], gq_ref[...], gk_ref[...])
        dq, dk, dv, dgq, dgk = pullback(do_ref[...].astype(F32))
        dq_ref[...] = dq.astype(dq_ref.dtype)
        row0 = pl.program_id(1) == 0
        all0 = jnp.logical_and(row0, pl.program_id(0) == 0)
        for o_ref, val, init in ((dk_ref, dk, row0), (dv_ref, dv, row0), (dgq_ref, dgq, all0), (dgk_ref, dgk, all0)):
            @pl.when(init)
            def _(o_ref=o_ref, val=val):
                o_ref[...] = val

            @pl.when(jnp.logical_not(init))
            def _(o_ref=o_ref, val=val):
                o_ref[...] += val

    return pl.pallas_call(
        body, name="cross_bwd", grid=(4, SEQ // CROSS_ROWS),
        in_specs=[q_spec, k_spec, v_spec, g_spec, g_spec, q_spec],
        out_specs=[q_spec, k_spec, k_spec, g_spec, g_spec],
        out_shape=[jax.ShapeDtypeStruct((SEQ, D_CROSS), BF16), jax.ShapeDtypeStruct((N_MEM, D_CROSS), F32),
                   jax.ShapeDtypeStruct((N_MEM, D_CROSS), F32), jax.ShapeDtypeStruct((1, CROSS_HEAD), F32),
                   jax.ShapeDtypeStruct((1, CROSS_HEAD), F32)],
        compiler_params=_params(("arbitrary", "arbitrary")),
    )(qc, kv, kv, gq, gk, do)


def _loss_epilogue(acc, residual, target):
    err = acc + residual - target
    dy = err * (1.0 / D_MODEL)
    part = jnp.sum(jnp.sum(err * err, axis=1, keepdims=True), axis=0, keepdims=True) * (0.5 / D_MODEL)
    return dy, dy, part


def _pad_heads(v):
    return jnp.pad(v.reshape(N_GROUPS, HEADS_PER_GROUP), ((0, 0), (0, 128 - HEADS_PER_GROUP))).reshape(1, DT_PAD)


def _unpad_heads(v):
    return v.reshape(v.shape[0], N_GROUPS, 128)[:, :, :HEADS_PER_GROUP].reshape(v.shape[0], N_DT)


def _rope_tables(positions):
    half = ROT // 2
    inv_freq = ROPE_THETA ** (-2.0 * jnp.arange(half, dtype=F32) / ROT)
    ang = positions.reshape(SEQ, 1).astype(F32) * inv_freq
    cos, sin = jnp.cos(ang), jnp.sin(ang)
    ones, zeros = jnp.ones((SEQ, HEAD - ROT), F32), jnp.zeros((SEQ, HEAD - ROT), F32)
    cos_h = jnp.concatenate([cos, cos, ones], axis=1)
    sin_h = jnp.concatenate([-sin, sin, zeros], axis=1)
    return jnp.tile(cos_h, (1, 2)), jnp.tile(sin_h, (1, 2))


def _add_res(acc, res):
    return (acc + res,)


def _norm_bwd_epilogue(acc, x, residual, *more):
    *part, g = more
    ct = acc + part[0] if part else acc
    _, pullback = jax.vjp(_rms, x, g)
    dx, dg = pullback(ct)
    return dx + residual, dg


def _add_res_and_norm(acc, res, g):
    y = acc + res
    return y, _rms(y, g)


def _settle(grads, *after):
    if hasattr(grads, "settle"):
        grads.settle(*after)


def _take_token(grads):
    token = getattr(grads, "token", None)
    if token is None:
        return ()
    grads.token = None
    return (token,)


def _local_step(x, mem, positions, target, p, w, more_weights=None, grads=None, h=None):
    grads = {} if grads is None else grads
    w = dict(w)
    cos, sin = _rope_tables(positions)
    gq2, gk2 = jnp.tile(p["g_q"], (1, 2)), jnp.tile(p["g_k"], (1, 2))
    bias, alog, dsk = _pad_heads(p["dt_bias"]), _pad_heads(p["a_log"]), _pad_heads(p["d_skip"])
    norm_out = [(D_MODEL, BF16, D_MODEL, 0, False)]

    if h is None:
        h = _rowwise(_norm_fn, [_full(x)], [_full(p["g_mix"])], norm_out, name="norm_in")[0]
    proj = _matmul(h, w["w_in"], mode="nn", name="in_proj", outs=[F32], n_cols=D_MAIN)
    dt_raw = _matmul(h, w["w_dt"], mode="nn", name="dt_proj", outs=[F32])
    pairs = D_ATTN // 128
    qk_rows = [(proj, 128, 0, True), (proj, 128, pairs, True), (proj, 128, 2 * pairs, True), _full(cos), _full(sin)]
    qk_vecs = [_full(gq2), _full(gk2)]
    qn, kn, vn = _rowwise(_qk_fn, qk_rows, qk_vecs, [(D_ATTN, F32, 128, 0, True)] * 3, name="qk_prep", groups=8, tr=1024)
    branches = [_attention_fwd(qn, kn, vn, b) for b in range(3)]
    merge_rows = [_full(o) for o, _ in branches] + [_full(lse) for _, lse in branches]
    attn = _rowwise(_merge_fn, merge_rows, [_full(p["g_attn_out"])], [(D_ATTN, BF16, D_ATTN, 0, False)], name="attn_merge")[0]
    xbc = _conv_fwd(proj, p["conv_w"], p["conv_b"])
    ssm, h_in = _ssd_fwd(xbc, dt_raw, bias, alog, dsk, proj, p["g_ssm_out"])
    mix = jnp.concatenate([attn, ssm], axis=1)
    if more_weights is not None:
        w.update(more_weights("mixer_done", mix))
    x1, hc = _matmul(mix, w["w_out"], mode="nn", name="out_proj", outs=[F32, BF16], extra=(x,), vecs=(p["g_cross"],),
                     epilogue=_add_res_and_norm, tm=512, tn=D_MODEL)
    memh = _rowwise(_norm_fn, [_full(mem)], [_full(p["g_mem"])], norm_out, name="norm_mem", n_rows=N_MEM, tr=N_MEM)[0]
    qc = _matmul(hc, w["w_cq"], mode="nn", name="cq_proj", outs=[F32])
    if more_weights is not None:
        w.update(more_weights("cross_started", qc))
    kv = _matmul(memh, w["w_ckv"], mode="nn", name="ckv_proj", outs=[F32])
    oc = _cross_fwd(qc, kv, p["g_cq"], p["g_ck"])
    x2, hm = _matmul(oc, w["w_co"], mode="nn", name="co_proj", outs=[F32, BF16], extra=(x1,), vecs=(p["g_mlp"],),
                     epilogue=_add_res_and_norm, tm=512, tn=D_MODEL)
    if more_weights is not None:
        w.update(more_weights("cross_done", hm))
    u, act = _matmul(hm, w["w_up"], mode="nn", name="up_proj", outs=[F32, BF16],
                     epilogue=lambda acc: (acc, jnp.square(jnp.maximum(acc, 0.0))))
    dy, dyb, loss_tiles = _matmul(act, w["w_down"], mode="nn", name="down_proj", outs=[F32, BF16], extra=(x2, target),
                                  epilogue=_loss_epilogue, tile_sums=1)
    loss = jnp.sum(loss_tiles).reshape(1, 1)

    grads["w_down"] = _matmul(act, dyb, mode="tn", name="dw_down", outs=[BF16], after=_take_token(grads))
    du = _matmul(dyb, w["w_down"], mode="nt", name="d_act", outs=[BF16], extra=(u,), after=_take_token(grads),
                 epilogue=lambda acc, uu: (acc * (2.0 * jnp.maximum(uu, 0.0)),))
    _settle(grads, du)
    grads["w_up"] = _matmul(hm, du, mode="tn", name="dw_up", outs=[BF16], col_shards=4, after=_take_token(grads))
    dx2, grads["g_mlp"] = _matmul(du, w["w_up"], mode="nt", name="d_hm", outs=[F32], extra=(x2, dy), vecs=(p["g_mlp"],),
                                  epilogue=_norm_bwd_epilogue, tile_rows=1, after=_take_token(grads), tm=512, tn=D_MODEL,
                                  tk=1024)
    _settle(grads, dx2)
    grads["w_co"] = _matmul(oc, dx2, mode="tn", name="dw_co", outs=[BF16], col_shards=4, after=_take_token(grads))
    doc = _matmul(dx2, w["w_co"], mode="nt", name="d_oc", outs=[BF16])
    dqc, dkc, dvc, grads["g_cq"], grads["g_ck"] = _cross_bwd(qc, kv, p["g_cq"], p["g_ck"], doc)
    grads["w_cq"] = _matmul(hc, dqc, mode="tn", name="dw_cq", outs=[BF16])
    dkv = jnp.concatenate([dkc, dvc], axis=1)
    grads["w_ckv"] = _matmul(memh, dkv, mode="tn", name="dw_ckv", outs=[BF16])
    dmemh = _matmul(dkv, w["w_ckv"], mode="nt", name="d_memh", outs=[F32])
    grads["g_mem"] = _rowwise_vjp(_norm_fn, [_full(mem)], [_full(p["g_mem"])], [[_full(dmemh)]], [],
                                  [(0, D_MODEL, D_MODEL, 0, False)], name="norm_mem_bwd", n_rows=N_MEM, tr=N_MEM)[0]
    dx1, grads["g_cross"] = _matmul(dqc, w["w_cq"], mode="nt", name="d_hc", outs=[F32], extra=(x1, dx2), vecs=(p["g_cross"],),
                                    epilogue=_norm_bwd_epilogue, tile_rows=1, tm=512, tn=D_MODEL)
    grads["w_out"] = _matmul(mix, dx1, mode="tn", name="dw_out", outs=[BF16])
    dmix = _matmul(dx1, w["w_out"], mode="nt", name="d_mix", outs=[F32], after=_take_token(grads))
    _settle(grads, dmix)
    merge_grads = [(i, D_ATTN, F32, D_ATTN, 0, False, None) for i in range(6)]
    *dol, grads["g_attn_out"] = _rowwise_vjp(
        _merge_fn, merge_rows, [_full(p["g_attn_out"])], [[(dmix, D_ATTN, 0, False)]],
        merge_grads, [(0, D_ATTN, D_ATTN, 0, False)], name="attn_merge_bwd", tr=256, after=_take_token(grads))
    dqkv = [_attention_bwd(qn, kn, vn, *branches[b], dol[b], dol[3 + b], b) for b in range(3)]
    qk_cts = [[(dqkv[b][i], 128, 0, True) for b in range(3)] for i in range(3)]
    dq, dk, dv, dgq2, dgk2 = _rowwise_vjp(
        _qk_fn, qk_rows, qk_vecs, qk_cts, [(i, D_ATTN, BF16, 128, 0, True, None) for i in range(3)],
        [(0, 128, 128, 0, False), (1, 128, 128, 0, False)], name="qk_prep_bwd", groups=8, tr=1024)
    grads["g_q"] = dgq2[:, :HEAD] + dgq2[:, HEAD:]
    grads["g_k"] = dgk2[:, :HEAD] + dgk2[:, HEAD:]
    dxs, db, dc, ddt, dbias, dalog, ddsk, dz, grads["g_ssm_out"] = _ssd_bwd(xbc, dt_raw, bias, alog, dsk, h_in, proj,
                                                                             p["g_ssm_out"], dmix)
    grads["dt_bias"], grads["a_log"], grads["d_skip"] = _unpad_heads(dbias), _unpad_heads(dalog), _unpad_heads(ddsk)
    dxbc_raw, dconv_w, grads["conv_b"] = _conv_bwd(proj, p["conv_w"], p["conv_b"], dxs, db, dc)
    grads["conv_w"] = dconv_w[:4]
    dproj = jnp.concatenate([dq, dk, dv, dz, dxbc_raw], axis=1)
    grads["w_main"] = _matmul(h, dproj, mode="tn", name="dw_main", outs=[BF16], out_cols=D_MAIN + N_DT)
    grads["w_dt"] = _matmul(h, ddt, mode="tn", name="dw_dt", outs=[BF16])
    dh = _matmul(dproj, w["w_in"], mode="nt", name="d_h_main", outs=[F32], after=_take_token(grads))
    grad_x, grads["g_mix"] = _matmul(ddt, w["w_dt"], mode="nt", name="d_h_dt", outs=[F32], extra=(x, dx1, dh),
                                     vecs=(p["g_mix"],), epilogue=_norm_bwd_epilogue, tile_rows=1, tm=512, tn=D_MODEL)
    return loss, grad_x, grads


MATRICES = ("w_in", "w_out", "w_cq", "w_ckv", "w_co", "w_up", "w_down")
ROW_SHARDED = ("w_out", "w_cq", "w_ckv", "w_down")
N_CHIPS = 4
ANY = pl.BlockSpec(memory_space=pl.ANY)


def _place():
    return lax.axis_index("x"), lax.axis_index("y"), lax.axis_index("c")


def _other_chips(x, y):
    return [(1 - x, y), (x, 1 - y), (1 - x, 1 - y)]


def _remote(src, dst, send_sem, recv_sem, device):
    return pltpu.make_async_remote_copy(src_ref=src, dst_ref=dst, send_sem=send_sem, recv_sem=recv_sem,
                                        device_id=device, device_id_type=MESH)


def _gathered_shape(name, shard):
    rows, cols = shard.shape
    if name == "w_in":
        return (N_CHIPS, rows, cols)
    return (N_CHIPS * rows, cols) if name in ROW_SHARDED else (rows, N_CHIPS * cols)


def _shard_window(name, ref, rows, cols, chip, half):
    r0, nr = (0, rows) if half is None else (half * (rows // 2), rows // 2)
    if name == "w_in":
        return ref.at[chip, pl.ds(r0, nr), :]
    if name in ROW_SHARDED:
        return ref.at[pl.ds(chip * rows + r0, nr), :]
    return ref.at[pl.ds(r0, nr), pl.ds(pl.multiple_of(chip * cols, 128), cols)]


def _cast_into_gathered(w, name, chip, after=()):
    rows, cols = w.shape
    tr = _tile(rows, ROW_TILE)

    def body(chip_ref, w_ref, *rest):
        rest[-1][...] = w_ref[...].astype(BF16)

    if name == "w_in":
        out_spec = pl.BlockSpec((None, tr, cols), lambda i, chip_ref: (chip_ref[0], i, 0))
    elif name in ROW_SHARDED:
        out_spec = pl.BlockSpec((tr, cols), lambda i, chip_ref: (chip_ref[0] * (rows // tr) + i, 0))
    else:
        out_spec = pl.BlockSpec((tr, cols), lambda i, chip_ref: (i, chip_ref[0]))
    grid_spec = pltpu.PrefetchScalarGridSpec(
        num_scalar_prefetch=1, grid=(rows // tr,),
        in_specs=[pl.BlockSpec((tr, cols), lambda i, chip_ref: (i, 0))] + [pl.BlockSpec(memory_space=pl.ANY)] * len(after),
        out_specs=out_spec)
    return pl.pallas_call(body, name="cast_" + name, grid_spec=grid_spec,
                          out_shape=jax.ShapeDtypeStruct(_gathered_shape(name, w), BF16),
                          compiler_params=_params(("parallel",)))(chip.reshape(1).astype(jnp.int32), w, *after)


def _w_in_columns(arr, to_shards):
    rows, piece = D_MODEL, (D_MAIN + N_DT) // N_CHIPS
    tr = ROW_TILE

    def body(a_ref, o_ref):
        for j in range(N_CHIPS):
            if to_shards:
                o_ref[j] = a_ref[:, pl.ds(piece * j, piece)]
            else:
                o_ref[:, pl.ds(piece * j, piece)] = a_ref[j]

    pieces = pl.BlockSpec((N_CHIPS, tr, piece), lambda i: (0, i, 0))
    matrix = pl.BlockSpec((tr, N_CHIPS * piece), lambda i: (i, 0))
    out_dims = (N_CHIPS, rows, piece) if to_shards else (rows, N_CHIPS * piece)
    return pl.pallas_call(
        body, name="w_in_to_shards" if to_shards else "w_in_from_shards", grid=(rows // tr,),
        in_specs=[matrix if to_shards else pieces], out_specs=pieces if to_shards else matrix,
        out_shape=jax.ShapeDtypeStruct(out_dims, arr.dtype), compiler_params=_params(("parallel",)))(arr)


HBM = pl.BlockSpec(memory_space=pltpu.HBM)
SEM = pl.BlockSpec(memory_space=pltpu.SEMAPHORE)
EFFECT = pltpu.SideEffectType.DATAFLOW_SIDE_EFFECTING


def _split_start(name, bufs, plan, counts, after=()):
    n, n_g, n_after = len(bufs), len(counts), len(after)

    def body(*refs):
        ins, sems, token = refs[:n], refs[n + n_after:n + n_after + 2 * n_g], refs[-1]
        for g, copies in enumerate(plan(ins)):
            for i, (src, dst, device, _) in enumerate(copies):
                _remote(src, dst, sems[2 * g].at[i], sems[2 * g + 1].at[i], device).start()
        token[...] = jnp.zeros_like(token)

    sem_shapes = [pltpu.SemaphoreType.DMA((cnt,)) for cnt in counts for _ in range(2)]
    res = pl.pallas_call(
        body, name=name,
        out_shape=(*sem_shapes, *[pltpu.HBM(b.shape, b.dtype) for b in bufs], jax.ShapeDtypeStruct((8, 128), F32)),
        in_specs=(*(HBM,) * n, *(ANY,) * n_after),
        out_specs=(*(SEM,) * (2 * n_g), *(HBM,) * n, pl.BlockSpec(memory_space=pltpu.VMEM)),
        input_output_aliases={i: 2 * n_g + i for i in range(n)},
        compiler_params=pltpu.CompilerParams(has_side_effects=EFFECT),
    )(*[pltpu.with_memory_space_constraint(b, pltpu.HBM) for b in bufs], *after)
    sems = [(res[2 * g], res[2 * g + 1]) for g in range(n_g)]
    return sems, list(res[2 * n_g:2 * n_g + n]), res[-1]


def _split_wait(name, bufs, sems, plan, *after):
    n = len(bufs)

    def body(*refs):
        ins, send, recv = refs[:n], refs[n], refs[n + 1]
        (copies,) = plan(ins)
        for i, (src, _, device, landing) in enumerate(copies):
            cp = _remote(src, landing, send.at[i], recv.at[i], device)
            cp.wait_send()
            cp.wait_recv()

    res = pl.pallas_call(
        body, name=name, out_shape=tuple(pltpu.HBM(b.shape, b.dtype) for b in bufs),
        in_specs=(*(HBM,) * n, SEM, SEM, *(ANY,) * len(after)), out_specs=(HBM,) * n,
        input_output_aliases={i: i for i in range(n)},
        compiler_params=pltpu.CompilerParams(has_side_effects=EFFECT),
    )(*bufs, sems[0], sems[1], *after)
    return list(res)


def _ici_plan(names, shard_shapes):
    def plan(refs):
        x, y, c = _place()
        copies = []
        for ref, name in zip(refs, names):
            win = _shard_window(name, ref, *shard_shapes[name], 2 * x + y, c)
            for px, py in _other_chips(x, y):
                copies.append((win, win, (px, py, c), _shard_window(name, ref, *shard_shapes[name], 2 * px + py, c)))
        return [copies]
    return plan


def _pass_on_plan(names, shard_shapes):
    def plan(refs):
        x, y, c = _place()
        copies = []
        for ref, name in zip(refs, names):
            for px, py in _other_chips(x, y):
                win = _shard_window(name, ref, *shard_shapes[name], 2 * px + py, c)
                copies.append((win, win, (x, y, 1 - c), _shard_window(name, ref, *shard_shapes[name], 2 * px + py, 1 - c)))
        return [copies]
    return plan


def _swap_plan(n_pairs):
    def plan(refs):
        x, y, c = _place()
        return [[(src.at[:, 1 - c], dst, (x, y, 1 - c), dst) for src, dst in zip(refs[:n_pairs], refs[n_pairs:])]]
    return plan


def _share_plan(n_pairs):
    def plan(refs):
        x, y, c = _place()
        return [[(src, dst, (x, y, 1 - c), dst)] for src, dst in zip(refs[:n_pairs], refs[n_pairs:])]
    return plan


def _scatter_plan(n_pairs):
    def plan(refs):
        x, y, c = _place()
        copies = []
        for src, dst in zip(refs[:n_pairs], refs[n_pairs:]):
            for k, (px, py) in enumerate(_other_chips(x, y)):
                copies.append((src.at[2 * px + py], dst.at[k], (px, py, c), dst.at[k]))
        return [copies]
    return plan


def _sibling_swap(arrs, name):
    n = len(arrs)

    def body(*refs):
        ins, outs, send, recv = refs[:n], refs[n:2 * n], refs[2 * n], refs[2 * n + 1]
        x, y, c = _place()
        cps = [_remote(ins[w].at[:, 1 - c], outs[w], send.at[w], recv.at[w], (x, y, 1 - c)) for w in range(n)]
        for cp in cps:
            cp.start()
        for cp in cps:
            cp.wait()

    return pl.pallas_call(
        body, name=name, in_specs=[ANY] * n, out_specs=[ANY] * n,
        out_shape=[jax.ShapeDtypeStruct((a.shape[0],) + a.shape[2:], a.dtype) for a in arrs],
        scratch_shapes=[pltpu.SemaphoreType.DMA((n,))] * 2,
    )(*arrs)


def _small_allreduce(buf, name, after=()):
    rows = buf.shape[0]

    def body(x_ref, *rest):
        out_ref, all_ref, send_sems, recv_sems, local_sem = rest[len(after):]
        x, y, c = _place()
        me, sibling, chips = (x, y, c), (x, y, 1 - c), _other_chips(x, y)

        def block(px, py, pc):
            return all_ref.at[pl.ds((4 * px + 2 * py + pc) * rows, rows), :]

        def copy(k, blk, to, src=None):
            return _remote(block(*blk) if src is None else src, block(*blk), send_sems.at[k], recv_sems.at[k], to)

        own = pltpu.make_async_copy(x_ref, block(*me), local_sem)
        own.start()
        first = [copy(0, me, sibling, src=x_ref)] + [copy(1 + j, me, (*chip, c), src=x_ref) for j, chip in enumerate(chips)]
        for cp in first:
            cp.start()
        passed = [copy(4 + j, (*chip, c), sibling) for j, chip in enumerate(chips)]
        for j, chip in enumerate(chips):
            copy(1 + j, (*chip, c), me).wait_recv()
            passed[j].start()
        copy(0, sibling, me).wait_recv()
        for j, chip in enumerate(chips):
            copy(4 + j, (*chip, 1 - c), me).wait_recv()
        for cp in first + passed:
            cp.wait_send()
        own.wait()
        acc = all_ref[pl.ds(0, rows), :]
        for d in range(1, 8):
            acc = acc + all_ref[pl.ds(d * rows, rows), :]
        out_ref[...] = acc

    vmem = pl.BlockSpec(memory_space=pltpu.VMEM)
    return pl.pallas_call(
        body, name=name, in_specs=[vmem] + [ANY] * len(after), out_specs=vmem,
        out_shape=jax.ShapeDtypeStruct(buf.shape, F32),
        scratch_shapes=[pltpu.VMEM((8 * rows, 128), F32), pltpu.SemaphoreType.DMA((7,)), pltpu.SemaphoreType.DMA((7,)),
                        pltpu.SemaphoreType.DMA],
    )(buf, *after)


ROW_TILE = 256
BIG_ROW_TILE = 1024


def _add_halves(arr, recv, c, name):
    _, _, hr, cols = arr.shape
    tr = _tile(hr, BIG_ROW_TILE)

    def body(c_ref, a_ref, r_ref, o_ref):
        o_ref[...] = (a_ref[...].astype(F32) + r_ref[...].astype(F32)).astype(o_ref.dtype)

    piece = pl.BlockSpec((None, tr, cols), lambda j, i, c_ref: (j, i, 0))
    grid_spec = pltpu.PrefetchScalarGridSpec(
        num_scalar_prefetch=1, grid=(N_CHIPS, hr // tr),
        in_specs=[pl.BlockSpec((None, None, tr, cols), lambda j, i, c_ref: (j, c_ref[0], i, 0)), piece], out_specs=piece)
    return pl.pallas_call(body, name=name, grid_spec=grid_spec, out_shape=jax.ShapeDtypeStruct(recv.shape, BF16),
                          compiler_params=_params(("parallel", "parallel")))(c.reshape(1).astype(jnp.int32), arr, recv)


def _flip_slot(d):
    return jnp.where(d == 1, 1, jnp.where(d == 3, 2, 0))


def _sum_chips(p, q, chip, name):
    _, hr, cols = p.shape
    tr = _tile(hr, BIG_ROW_TILE)

    def body(chip_ref, p_ref, q_ref, o_ref):
        j = pl.program_id(1)
        term = jnp.where(j == chip_ref[0], p_ref[...].astype(F32), q_ref[...].astype(F32))

        @pl.when(j == 0)
        def _():
            o_ref[...] = term

        @pl.when(j != 0)
        def _():
            o_ref[...] += term

    grid_spec = pltpu.PrefetchScalarGridSpec(
        num_scalar_prefetch=1, grid=(hr // tr, N_CHIPS),
        in_specs=[pl.BlockSpec((None, tr, cols), lambda i, j, chip_ref: (chip_ref[0], i, 0)),
                  pl.BlockSpec((None, tr, cols), lambda i, j, chip_ref: (_flip_slot(j ^ chip_ref[0]), i, 0))],
        out_specs=pl.BlockSpec((tr, cols), lambda i, j, chip_ref: (i, 0)))
    return pl.pallas_call(body, name=name, grid_spec=grid_spec, out_shape=jax.ShapeDtypeStruct((hr, cols), F32),
                          compiler_params=_params(("parallel", "arbitrary")))(chip.reshape(1).astype(jnp.int32), p, q)


def _adamw_halves(w, g_own, g_other, m, v, c, name):
    rows, cols = w.shape
    tr = _tile(rows // 2, ROW_TILE)
    per_half = rows // 2 // tr

    def body(c_ref, w_ref, own_ref, other_ref, m_ref, v_ref, g_ref, d_ref, nm_ref, nv_ref):
        mine = (pl.program_id(0) // per_half) == c_ref[0]
        g_ = jnp.where(mine, own_ref[...], other_ref[...])
        g_ref[...] = g_
        d_ref[...], nm_ref[...], nv_ref[...] = _adamw_math(w_ref[...], g_, m_ref[...], v_ref[...])

    blk = pl.BlockSpec((tr, cols), lambda i, c_ref: (i, 0))
    own = pl.BlockSpec((tr, cols), lambda i, c_ref: (jnp.where(i // per_half == c_ref[0], i % per_half, 0), 0))
    other = pl.BlockSpec((tr, cols), lambda i, c_ref: (jnp.where(i // per_half == c_ref[0], 0, i % per_half), 0))
    grid_spec = pltpu.PrefetchScalarGridSpec(num_scalar_prefetch=1, grid=(rows // tr,),
                                             in_specs=[blk, own, other, blk, blk], out_specs=[blk] * 4)
    return pl.pallas_call(body, name=name, grid_spec=grid_spec, out_shape=[jax.ShapeDtypeStruct(w.shape, F32)] * 4,
                          compiler_params=_params(("parallel",)))(c.reshape(1).astype(jnp.int32), w, g_own, g_other, m, v)


W_IN_COLS = (D_MAIN + N_DT) // N_CHIPS
W_IN_MAIN = W_IN_COLS // 128 * 128
W_IN_TAIL = W_IN_COLS - 128
W_IN_PARTS = ((0, W_IN_MAIN), (W_IN_TAIL, 128))


def _cast_w_in_transposed(w_t, chip, after=()):
    def body(chip_ref, w_ref, *rest):
        for start, size in W_IN_PARTS:
            rest[-1][:, pl.ds(start, size)] = w_ref[pl.ds(start, size), :].T.astype(BF16)

    grid_spec = pltpu.PrefetchScalarGridSpec(
        num_scalar_prefetch=1, grid=(D_MODEL // ROW_TILE,),
        in_specs=[pl.BlockSpec((W_IN_COLS, ROW_TILE), lambda i, chip_ref: (0, i))] + [pl.BlockSpec(memory_space=pl.ANY)] * len(after),
        out_specs=pl.BlockSpec((None, ROW_TILE, W_IN_COLS), lambda i, chip_ref: (chip_ref[0], i, 0)))
    return pl.pallas_call(body, name="cast_w_in", grid_spec=grid_spec,
                          out_shape=jax.ShapeDtypeStruct((N_CHIPS, D_MODEL, W_IN_COLS), BF16),
                          compiler_params=_params(("parallel",)))(chip.reshape(1).astype(jnp.int32), w_t, *after)


def _adamw_w_in_transposed(w_t, g_own, g_other, m_t, v_t, c):
    per_half = D_MODEL // 2 // ROW_TILE

    def body(c_ref, w_ref, own_ref, other_ref, m_ref, v_ref, g_ref, d_ref, nm_ref, nv_ref):
        mine = (pl.program_id(0) // per_half) == c_ref[0]
        for start, size in W_IN_PARTS:
            cols, rows = pl.ds(start, size), pl.ds(start, size)
            g_ = jnp.where(mine, own_ref[:, cols], other_ref[:, cols]).T
            g_ref[rows, :] = g_
            d_ref[rows, :], nm_ref[rows, :], nv_ref[rows, :] = _adamw_math(w_ref[rows, :], g_, m_ref[rows, :], v_ref[rows, :])

    blk = pl.BlockSpec((W_IN_COLS, ROW_TILE), lambda i, c_ref: (0, i))
    own = pl.BlockSpec((ROW_TILE, W_IN_COLS), lambda i, c_ref: (jnp.where(i // per_half == c_ref[0], i % per_half, 0), 0))
    other = pl.BlockSpec((ROW_TILE, W_IN_COLS), lambda i, c_ref: (jnp.where(i // per_half == c_ref[0], 0, i % per_half), 0))
    grid_spec = pltpu.PrefetchScalarGridSpec(num_scalar_prefetch=1, grid=(D_MODEL // ROW_TILE,),
                                             in_specs=[blk, own, other, blk, blk], out_specs=[blk] * 4)
    return pl.pallas_call(body, name="adamw_w_in", grid_spec=grid_spec, out_shape=[jax.ShapeDtypeStruct(w_t.shape, F32)] * 4,
                          compiler_params=_params(("parallel",)))(c.reshape(1).astype(jnp.int32), w_t, g_own, g_other, m_t, v_t)


def _adamw_math(w, g, m, v):
    m_new = ADAM_B1 * m + (1.0 - ADAM_B1) * g
    v_new = ADAM_B2 * v + (1.0 - ADAM_B2) * (g * g)
    m_hat = m_new / (1.0 - ADAM_B1 ** ADAM_STEP)
    v_hat = v_new / (1.0 - ADAM_B2 ** ADAM_STEP)
    return -ADAM_LR * (m_hat / (jnp.sqrt(v_hat) + ADAM_EPS) + ADAM_WD * w), m_new, v_new


VECTORS = ("g_mix", "g_q", "g_k", "g_attn_out", "conv_b", "dt_bias", "a_log", "d_skip", "g_ssm_out", "g_cross", "g_mem",
           "g_cq", "g_ck", "g_mlp")
WEIGHTS = ("g_mix", "w_in", "g_q", "g_k", "g_attn_out", "conv_w", "conv_b", "dt_bias", "a_log", "d_skip", "g_ssm_out", "w_out",
           "g_cross", "g_mem", "w_cq", "w_ckv", "g_cq", "g_ck", "w_co", "g_mlp", "w_up", "w_down")


def _pack(parts):
    flat = jnp.concatenate([t.reshape(-1) for t in parts])
    total = -(-flat.shape[0] // 1024) * 1024
    return jnp.pad(flat, (0, total - flat.shape[0])).reshape(total // 128, 128)


def _rows_of(n):
    return -(-n // 128)


def _slot_rows(n):
    return -(-n // 1024) * 8


def _pack_rows(parts):
    rows = []
    for t in parts:
        flat = t.reshape(-1)
        rows.append(jnp.pad(flat, (0, 128 * _slot_rows(flat.shape[0]) - flat.shape[0])).reshape(-1, 128))
    return jnp.concatenate(rows)


def _adamw_vectors(summed, chip, vectors, conv):
    groups = list(vectors) + [conv]
    offsets, row = [], 0
    for w, _, _ in groups:
        offsets.append(row)
        row += _slot_rows(w.shape[1]) if w.shape[0] == 1 else _slot_rows(w.shape[0] * N_CHIPS * w.shape[1])
    conv_blocks = _rows_of(conv[0].shape[1])

    def body(chip_ref, sum_ref, *refs):
        ins, outs = refs[:3 * len(groups)], refs[3 * len(groups):]

        def update(i, g, idx):
            w_ref, m_ref, v_ref = ins[3 * i:3 * i + 3]
            delta, new_m, new_v = _adamw_math(w_ref[idx], g, m_ref[idx], v_ref[idx])
            for o_ref, val in zip(outs[4 * i:4 * i + 4], (g, delta, new_m, new_v)):
                o_ref[idx] = val

        for i, (w, _, _) in enumerate(vectors):
            for t in range(_rows_of(w.shape[1])):
                width = min(128, w.shape[1] - 128 * t)
                update(i, sum_ref[pl.ds(offsets[i] + t, 1), pl.ds(0, width)], (slice(None), pl.ds(128 * t, width)))
        for tap in range(conv[0].shape[0]):
            for blk in range(conv_blocks):
                src = offsets[-1] + tap * N_CHIPS * conv_blocks + chip_ref[0] * conv_blocks + blk
                update(len(vectors), sum_ref[pl.ds(src, 1), :], (pl.ds(tap, 1), pl.ds(128 * blk, 128)))

    def whole(a):
        return pl.BlockSpec(a.shape, lambda i, chip_ref: (0,) * a.ndim)

    operands = [t for group in groups for t in group]
    grid_spec = pltpu.PrefetchScalarGridSpec(
        num_scalar_prefetch=1, grid=(1,), in_specs=[whole(summed)] + [whole(t) for t in operands],
        out_specs=[whole(w) for w, _, _ in groups for _ in range(4)])
    res = pl.pallas_call(body, name="adamw_vectors", grid_spec=grid_spec,
                         out_shape=[jax.ShapeDtypeStruct(w.shape, F32) for w, _, _ in groups for _ in range(4)],
                         compiler_params=_params(("arbitrary",)))(chip.reshape(1).astype(jnp.int32), summed, *operands)
    return [res[4 * i:4 * i + 4] for i in range(len(groups))]


def _unpack(buf, shapes):
    flat, out, pos = buf.reshape(-1), [], 0
    for shape in shapes:
        size = math.prod(shape)
        out.append(flat[pos:pos + size].reshape(shape))
        pos += size
    return out


def kernel(x, mem, positions, g_mix, w_in, g_q, g_k, g_attn_out, conv_w, conv_b, dt_bias, a_log, d_skip, g_ssm_out, w_out, g_cross, g_mem, w_cq, w_ckv, g_cq, g_ck, w_co, g_mlp, w_up, w_down, loss_target, m_g_mix, m_w_in, m_g_q, m_g_k, m_g_attn_out, m_conv_w, m_conv_b, m_dt_bias, m_a_log, m_d_skip, m_g_ssm_out, m_w_out, m_g_cross, m_g_mem, m_w_cq, m_w_ckv, m_g_cq, m_g_ck, m_w_co, m_g_mlp, m_w_up, m_w_down, v_g_mix, v_w_in, v_g_q, v_g_k, v_g_attn_out, v_conv_w, v_conv_b, v_dt_bias, v_a_log, v_d_skip, v_g_ssm_out, v_w_out, v_g_cross, v_g_mem, v_w_cq, v_w_ckv, v_g_cq, v_g_ck, v_w_co, v_g_mlp, v_w_up, v_w_down):
    args = dict(locals())
    weights = {n: args[n][0] for n in WEIGHTS}
    mom_m = {n: args["m_" + n][0] for n in WEIGHTS}
    mom_v = {n: args["v_" + n][0] for n in WEIGHTS}
    x_idx, y_idx, c_idx = _place()
    chip = 2 * x_idx + y_idx

    shapes = {n: weights[n].shape for n in MATRICES}
    first, mid, late = ("w_in",), ("w_out", "w_cq", "w_ckv", "w_co"), ("w_up", "w_down")
    w_in_t, m_in_t, v_in_t = (jnp.swapaxes(t, 1, 2)[0] for t in (w_in, m_w_in, v_w_in))
    w_in_buf = [_cast_w_in_transposed(w_in_t, chip)]
    taps, tap_cols = weights["conv_w"].shape
    conv_parts = _small_allreduce(_pack([jnp.zeros((N_CHIPS, taps, tap_cols), F32).at[chip].set(0.5 * weights["conv_w"])]),
                                  "gather_conv_taps")
    sems_in, w_in_buf, token = _split_start("gather_ici_start_w_in", w_in_buf, _ici_plan(first, shapes), [3], after=(conv_parts,))
    bufs = [_cast_into_gathered(weights[n], n, chip, after=(token,)) for n in mid + late]
    plan = lambda refs: _ici_plan(mid, shapes)(refs[:4]) + _ici_plan(late, shapes)(refs[4:])
    sems_rest, bufs, token = _split_start("gather_ici_start_rest", bufs, plan, [12, 6], after=(token,))
    params = {n: weights[n].reshape(1, -1) for n in VECTORS}
    h_in = _rowwise(_norm_fn, [_full(x[0])], [_full(params["g_mix"])], [(D_MODEL, BF16, D_MODEL, 0, False)], name="norm_in",
                    after=(token,))[0]
    w_in_buf = _split_wait("gather_ici_wait_w_in", w_in_buf, sems_in[0], _ici_plan(first, shapes), token, h_in, m_in_t, v_in_t)
    pass_sems, w_in_buf, token = _split_start("gather_pass_start_w_in", w_in_buf, _pass_on_plan(first, shapes), [3])
    w_in_buf = _split_wait("gather_pass_wait_w_in", w_in_buf, pass_sems[0], _pass_on_plan(first, shapes), token)
    w_in_full = _w_in_columns(w_in_buf[0], to_shards=False)
    full = {"w_in": w_in_full,
            "w_dt": jnp.pad(w_in_full[:, D_MAIN:].reshape(D_MODEL, N_GROUPS, HEADS_PER_GROUP),
                            ((0, 0), (0, 0), (0, 128 - HEADS_PER_GROUP))).reshape(D_MODEL, DT_PAD)}
    in_flight = {}

    def more_weights(stage, after):
        if stage == "mixer_done":
            got = _split_wait("gather_ici_wait_mid", bufs[:4], sems_rest[0], _ici_plan(mid, shapes), after)
            sems, got, token = _split_start("gather_pass_start_mid", got, _pass_on_plan(mid, shapes), [12])
            return dict(zip(mid, _split_wait("gather_pass_wait_mid", got, sems[0], _pass_on_plan(mid, shapes), token)))
        if stage == "cross_started":
            got = _split_wait("gather_ici_wait_late", bufs[4:], sems_rest[1], _ici_plan(late, shapes), after)
            in_flight["late"] = _split_start("gather_pass_start_late", got, _pass_on_plan(late, shapes), [6])
            return {}
        sems, got, token = in_flight.pop("late")
        return dict(zip(late, _split_wait("gather_pass_wait_late", got, sems[0], _pass_on_plan(late, shapes), token, after)))

    params["conv_w"] = _unpack(conv_parts, [(N_CHIPS, taps, tap_cols)])[0].transpose(1, 0, 2).reshape(taps, N_CHIPS * tap_cols)

    groups = (("w_down",), ("w_up",), ("w_co", "w_cq", "w_ckv", "w_out"), ("w_in",))
    scattered = []

    class GradStore(dict):
        pending = None

        def __setitem__(self, name, value):
            super().__setitem__(name, value)
            if "w_main" in self and "w_dt" in self and "w_in" not in self:
                gw_in = lax.dynamic_update_slice(self["w_main"], _unpad_heads(self["w_dt"]), (0, D_MAIN))
                self["w_in"] = _w_in_columns(gw_in, to_shards=True)
            for group in groups:
                if name in group and all(n in self for n in group):
                    self.settle()
                    pieces = [self[n].reshape(N_CHIPS, 2, shapes[n][0] // 2, shapes[n][1]) for n in group]
                    if group == groups[-1]:
                        self.scatter(group, pieces, _sibling_swap(pieces, "grad_swap_" + group[0]))
                    else:
                        landing = [lax.empty((N_CHIPS,) + a.shape[2:], BF16) for a in pieces]
                        sems, thru, self.token = _split_start("grad_swap_start_" + group[0], pieces + landing,
                                                              _swap_plan(len(pieces)), [len(pieces)])
                        self.pending = (group, sems[0], thru)

        def settle(self, *after):
            if self.pending is not None:
                group, sems, thru = self.pending
                self.pending = None
                thru = _split_wait("grad_swap_wait_" + group[0], thru, sems, _swap_plan(len(group)), *after)
                self.scatter(group, thru[:len(group)], thru[len(group):])

        def scatter(self, group, pieces, from_sibling):
            sums = [_add_halves(a, r, c_idx, "add_halves_" + n) for n, a, r in zip(group, pieces, from_sibling)]
            landing = [lax.empty((3,) + s.shape[1:], BF16) for s in sums]
            sems, thru, self.token = _split_start("grad_scatter_start_" + group[0], sums + landing,
                                                  _scatter_plan(len(sums)), [3 * len(sums)])
            scattered.append((group, sems[0], thru))

    loss, grad_x, grads = _local_step(x[0], mem[0], positions[0], loss_target[0], params, full, more_weights, GradStore(),
                                      h_in)

    out_g, out_d, out_m, out_v = {}, {}, {}, {}

    def finish(entries, order, token):
        halves = {}
        for group, sems, thru in entries:
            thru = _split_wait("grad_scatter_wait_" + group[0], thru, sems, _scatter_plan(len(group)), token)
            for i, n in enumerate(group):
                halves[n] = _sum_chips(thru[i], thru[len(group) + i], chip, "sum_chips_" + n)
        sources = [halves[n] for n in order]
        landing = [lax.empty(s.shape, F32) for s in sources]
        sems, thru, token = _split_start("grad_share_start_" + order[0], sources + landing, _share_plan(len(order)),
                                         [1] * len(order))
        for i, n in enumerate(order):
            own, other = _split_wait("grad_share_wait_" + n, [thru[i], thru[len(order) + i]], sems[i], _share_plan(1), token)
            if n == "w_in":
                res_t = _adamw_w_in_transposed(w_in_t, own, other, m_in_t, v_in_t, c_idx)
                out_g[n], out_d[n], out_m[n], out_v[n] = (t.T for t in res_t)
            else:
                out_g[n], out_d[n], out_m[n], out_v[n] = _adamw_halves(weights[n], own, other, mom_m[n], mom_v[n], c_idx,
                                                                       "adamw_" + n)
            token = out_v[n]
        return token

    token = finish(scattered[2:3], ("w_cq", "w_co", "w_ckv", "w_out"), grad_x)
    token = finish(scattered[3:], ("w_in",), token)
    finish(scattered[:2], ("w_up", "w_down"), token)

    names = VECTORS + ("conv_w",)
    summed = _small_allreduce(_pack_rows([grads[n] for n in names] + [loss]), "allreduce_vectors")
    total_loss = summed[sum(_slot_rows(grads[n].size) for n in names), 0]
    small_out = _adamw_vectors(summed, chip, [(args[n], args["m_" + n], args["v_" + n]) for n in VECTORS],
                               (weights["conv_w"], mom_m["conv_w"], mom_v["conv_w"]))
    for n, res in zip(names, small_out):
        out_g[n], out_d[n], out_m[n], out_v[n] = (t.reshape(weights[n].shape) for t in res)

    outs =[total_loss, grad_x[None]]
    for group in (out_g, out_d, out_m, out_v):
        outs += [group[n][None] for n in WEIGHTS]
    return tuple(outs)
```

```python
import functools
import math

import jax
import jax.numpy as jnp
from jax import lax
from jax.experimental import pallas as pl
from jax.experimental.pallas import tpu as pltpu

F32 = jnp.float32
BF16 = jnp.bfloat16

SEQ = 2048
D_MODEL = 2048
HEAD = 64
D_ATTN = 1024
D_SSM = 1024
N_GROUPS = 4
N_STATE = 128
CHUNK = 128
ATT_BLK = 128
N_MEM = 256
D_CROSS = 512
D_MAIN = 6144
N_DT = 16
DT_PAD = 512
ROT = 16
ROPE_THETA = 500000.0
EPS = 1e-6
NEG = -1e30
BRANCH_BLOCKS = (16, 4, 1)
DILATIONS = (1, 4, 16)

ADAM_LR, ADAM_B1, ADAM_B2, ADAM_EPS, ADAM_WD, ADAM_STEP = 0.001, 0.9, 0.999, 1e-08, 0.01, 10

VMEM_LIMIT = 56 * 1024 * 1024
MESH = pl.DeviceIdType.MESH


def _params(sem, **kw):
    return pltpu.CompilerParams(dimension_semantics=sem, vmem_limit_bytes=VMEM_LIMIT, **kw)


def _bdot(a, b, dims):
    return lax.dot_general(a.astype(BF16), b.astype(BF16), (dims, ((), ())), preferred_element_type=F32)


def _fdot(a, b, dims):
    return lax.dot_general(a, b, (dims, ((), ())), preferred_element_type=F32, precision=lax.Precision.HIGHEST)


NN = ((1,), (0,))
NT = ((1,), (1,))
TN = ((0,), (0,))


def _tile(n, want):
    t = min(n, want)
    while n % t:
        t //= 2
    return t


def _matmul(a, b, *, mode, name, outs, extra=(), vecs=(), epilogue=None, col_shards=1, after=(), n_cols=None, out_cols=None,
            tile_rows=0, tile_sums=0, tm=1024, tn=1024, tk=2048):
    if mode == "nn":
        (m, k), n = a.shape, b.shape[1]
    elif mode == "nt":
        (m, k), n = a.shape, b.shape[0]
    else:
        (k, m), n = a.shape, b.shape[1]
    n = n if n_cols is None else n_cols
    tm, tn, tk = _tile(m, tm), _tile(n // col_shards, tn), _tile(k, tk)
    nk = k // tk
    per_shard = n // col_shards // tn
    dims = {"nn": NN, "nt": NT, "tn": TN}[mode]
    a_spec = pl.BlockSpec((tk, tm), lambda i, j, kk: (kk, i)) if mode == "tn" else pl.BlockSpec((tm, tk), lambda i, j, kk: (i, kk))
    b_spec = pl.BlockSpec((tn, tk), lambda i, j, kk: (j, kk)) if mode == "nt" else pl.BlockSpec((tk, tn), lambda i, j, kk: (kk, j))
    o_spec = pl.BlockSpec((tm, tn), lambda i, j, kk: (i, j))
    n_extra, n_out, n_after = len(extra) + len(vecs), len(outs), len(after)

    def body(a_ref, b_ref, *rest):
        extra_refs, out_refs, acc_ref = rest[:n_extra], rest[n_extra + n_after:-1], rest[-1]

        def finish(acc):
            res = (acc,) if epilogue is None else epilogue(acc, *[e[...] for e in extra_refs])
            for o_ref, r in zip(out_refs[:n_out], res):
                o_ref[...] = r.astype(o_ref.dtype)
            for o_ref, r in zip(out_refs[n_out:], res[n_out:]):
                o_ref[...] = jnp.broadcast_to(r, o_ref.shape)

        if nk == 1:
            finish(_bdot(a_ref[...], b_ref[...], dims))
            return
        kk = pl.program_id(2)

        @pl.when(kk == 0)
        def _():
            acc_ref[...] = jnp.zeros_like(acc_ref)

        acc_ref[...] += _bdot(a_ref[...], b_ref[...], dims)

        @pl.when(kk == nk - 1)
        def _():
            finish(acc_ref[...])

    if col_shards == 1:
        out_specs, out_dims = [o_spec] * n_out, (m, n if out_cols is None else out_cols)
    else:
        sharded = pl.BlockSpec((None, tm, tn), lambda i, j, kk: (j // per_shard, i, j % per_shard))
        out_specs, out_dims = [sharded] * n_out, (col_shards, m, n // col_shards)
    res = pl.pallas_call(
        body, name=name, grid=(m // tm, n // tn, nk),
        in_specs=[a_spec, b_spec] + [o_spec] * len(extra) + [pl.BlockSpec((1, tn), lambda i, j, kk: (0, j))] * len(vecs)
        + [pl.BlockSpec(memory_space=pl.ANY)] * n_after,
        out_specs=out_specs + [pl.BlockSpec((8, tn), lambda i, j, kk: (i, j))] * tile_rows
        + [pl.BlockSpec((8, 128), lambda i, j, kk: (i, j))] * tile_sums,
        out_shape=[jax.ShapeDtypeStruct(out_dims, dt) for dt in outs] + [jax.ShapeDtypeStruct((m // tm * 8, n), F32)] * tile_rows
        + [jax.ShapeDtypeStruct((m // tm * 8, n // tn * 128), F32)] * tile_sums,
        scratch_shapes=[pltpu.VMEM((tm, tn) if nk > 1 else (8, 128), F32)],
        compiler_params=_params(("parallel", "parallel", "arbitrary")),
    )(a, b, *extra, *vecs, *after)
    res = (list(res[:n_out]) + [jnp.sum(t[::8], axis=0, keepdims=True) for t in res[n_out:n_out + tile_rows]]
           + [t[::8, ::128] for t in res[n_out + tile_rows:]])
    return res[0] if len(res) == 1 else res


def _row_spec(tr, bw, cb, per_group):
    return pl.BlockSpec((tr, bw), (lambda g, i: (i, cb + g)) if per_group else (lambda g, i: (i, cb)))


def _vec_spec(bw, cb, per_group):
    return pl.BlockSpec((1, bw), (lambda g, i: (0, cb + g)) if per_group else (lambda g, i: (0, cb)))


def _rowwise(fn, rows, vecs, outs, *, name, n_rows=SEQ, tr=512, groups=1, after=()):
    n_r, n_v, n_after = len(rows), len(vecs), len(after)

    def body(*refs):
        vals = [r[...].astype(F32) for r in refs[:n_r + n_v]]
        res = fn(*vals)
        for o_ref, r in zip(refs[n_r + n_v + n_after:], res):
            o_ref[...] = r.astype(o_ref.dtype)

    res = pl.pallas_call(
        body, name=name, grid=(groups, n_rows // tr),
        in_specs=[_row_spec(tr, bw, cb, pg) for _, bw, cb, pg in rows] + [_vec_spec(bw, cb, pg) for _, bw, cb, pg in vecs]
        + [pl.BlockSpec(memory_space=pl.ANY)] * n_after,
        out_specs=[_row_spec(tr, bw, cb, pg) for _, _, bw, cb, pg in outs],
        out_shape=[jax.ShapeDtypeStruct((n_rows, w), dt) for w, dt, _, _, _ in outs],
        compiler_params=_params(("parallel", "parallel")),
    )(*[r[0] for r in rows], *[v[0] for v in vecs], *after)
    return res


def _rowwise_vjp(fn, rows, vecs, cts, row_grads, vec_grads, *, name, n_rows=SEQ, tr=512, groups=1, after=()):
    n_r, n_v, n_after = len(rows), len(vecs), len(after)
    ct_ops = [op for group in cts for op in group]
    ct_sizes = [len(group) for group in cts]
    res_ops = [g[6] for g in row_grads if g[6] is not None]
    n_ct, n_res, n_rg = len(ct_ops), len(res_ops), len(row_grads)

    def body(*refs):
        vals = [r[...].astype(F32) for r in refs[:n_r + n_v]]
        pos = n_r + n_v
        ct_vals = []
        for size in ct_sizes:
            acc = refs[pos][...].astype(F32)
            for t in range(1, size):
                acc = acc + refs[pos + t][...].astype(F32)
            ct_vals.append(acc)
            pos += size
        res_refs = refs[pos:pos + n_res]
        out_refs = refs[pos + n_res + n_after:]
        _, pullback = jax.vjp(fn, *vals)
        grads = pullback(tuple(ct_vals))
        r_i = 0
        for o_ref, g in zip(out_refs[:n_rg], row_grads):
            val = grads[g[0]]
            if g[6] is not None:
                val = val + res_refs[r_i][...].astype(F32)
                r_i += 1
            o_ref[...] = val.astype(o_ref.dtype)
        first = (pl.program_id(1) == 0)
        for o_ref, g in zip(out_refs[n_rg:], vec_grads):
            val = jnp.sum(grads[n_r + g[0]], axis=0, keepdims=True)
            init = first if g[4] else jnp.logical_and(first, pl.program_id(0) == 0)

            @pl.when(init)
            def _(o_ref=o_ref, val=val):
                o_ref[...] = val

            @pl.when(jnp.logical_not(init))
            def _(o_ref=o_ref, val=val):
                o_ref[...] += val

    in_specs = [_row_spec(tr, bw, cb, pg) for _, bw, cb, pg in rows] + [_vec_spec(bw, cb, pg) for _, bw, cb, pg in vecs]
    in_specs += [_row_spec(tr, bw, cb, pg) for _, bw, cb, pg in ct_ops + res_ops] + [pl.BlockSpec(memory_space=pl.ANY)] * n_after
    out_specs =[_row_spec(tr, g[3], g[4], g[5]) for g in row_grads] + [_vec_spec(g[2], g[3], g[4]) for g in vec_grads]
    out_shape = [jax.ShapeDtypeStruct((n_rows, g[1]), g[2]) for g in row_grads]
    out_shape += [jax.ShapeDtypeStruct((1, g[1]), F32) for g in vec_grads]
    return pl.pallas_call(
        body, name=name, grid=(groups, n_rows // tr),
        in_specs=in_specs, out_specs=out_specs, out_shape=out_shape,
        compiler_params=_params(("arbitrary", "arbitrary")),
    )(*[r[0] for r in rows], *[v[0] for v in vecs], *[c[0] for c in ct_ops], *[r[0] for r in res_ops], *after)


def _full(arr, width=None):
    return (arr, arr.shape[1] if width is None else width, 0, False)


def _make_xor(sh):
    def raw(x):
        n = x.shape[-1]
        lane = lax.broadcasted_iota(jnp.int32, x.shape, x.ndim - 1)
        up = pltpu.roll(x, n - sh, x.ndim - 1)
        down = pltpu.roll(x, sh, x.ndim - 1)
        return jnp.where((lane & sh) == 0, up, down)

    f = jax.custom_vjp(raw)
    f.defvjp(lambda x: (raw(x), None), lambda _, ct: (raw(ct),))
    return f


_SWAP_ROPE_HALVES = _make_xor(ROT // 2)


def _head_sum(x):
    n = x.shape[-1]
    same_head = (lax.broadcasted_iota(jnp.int32, (n, n), 0) // HEAD) == (lax.broadcasted_iota(jnp.int32, (n, n), 1) // HEAD)
    return _fdot(x, same_head.astype(F32), NN)


def _rms(x, g):
    return x * lax.rsqrt(jnp.mean(x * x, axis=-1, keepdims=True) + EPS) * g


def _head_rms_rope(x, g, cos, sin, scale):
    y = x * lax.rsqrt(_head_sum(x * x) * (1.0 / HEAD) + EPS) * g
    return (y * cos + _SWAP_ROPE_HALVES(y) * sin) * scale


def _qk_fn(q, k, v, cos, sin, gq, gk):
    return (_head_rms_rope(q, gq, cos, sin, HEAD ** -0.5), _head_rms_rope(k, gk, cos, sin, 1.0), v)


def _norm_fn(x, g):
    return (_rms(x, g),)


def _merge_fn(o0, o1, o2, l0, l1, l2, g):
    m = lax.stop_gradient(jnp.maximum(jnp.maximum(l0, l1), l2))
    e0, e1, e2 = jnp.exp(l0 - m), jnp.exp(l1 - m), jnp.exp(l2 - m)
    mix = (e0 * o0 + e1 * o1 + e2 * o2) / (e0 + e1 + e2)
    return (_rms(mix, g),)


def _gate_fn(y, z, g):
    return (_rms(y * (z * jax.nn.sigmoid(z)), g),)


def _attn_pair(q, kc, vc, kp=None, vp=None, has_prev=None):
    pick0, pick1 = _head_picks()
    k_band, v_band, mask = _attn_band(kc, vc, kp, vp, has_prev)
    s = jnp.where(mask, _bdot(jnp.concatenate([q * pick0, q * pick1], axis=0), k_band, NT), NEG)
    m = jnp.max(s, axis=-1, keepdims=True)
    p = jnp.exp(s - m)
    den = jnp.sum(p, axis=-1, keepdims=True)
    acc = _bdot(p, v_band, NN) * (1.0 / den)
    lse_rows = m + jnp.log(den)
    o = pick0 * acc[:ATT_BLK] + pick1 * acc[ATT_BLK:]
    lse = pick0 * lse_rows[:ATT_BLK] + pick1 * lse_rows[ATT_BLK:]
    return o, lse


def _head_picks():
    lane = lax.broadcasted_iota(jnp.int32, (1, 2 * HEAD), 1)
    return (lane < HEAD).astype(F32), (lane >= HEAD).astype(F32)


def _attn_band(kc, vc, kp, vp, has_prev):
    n_keys = ATT_BLK if kp is None else 2 * ATT_BLK
    qi = lax.broadcasted_iota(jnp.int32, (2 * ATT_BLK, n_keys), 0) & (ATT_BLK - 1)
    kj = lax.broadcasted_iota(jnp.int32, (2 * ATT_BLK, n_keys), 1)
    if kp is None:
        return kc, vc, qi >= kj
    in_prev = jnp.logical_and(jnp.logical_and(kj < ATT_BLK, kj >= qi), has_prev)
    mask = jnp.logical_or(in_prev, jnp.logical_and(kj >= ATT_BLK, qi >= kj - ATT_BLK))
    return jnp.concatenate([kp, kc], axis=0), jnp.concatenate([vp, vc], axis=0), mask


def _attn_config(b):
    r = DILATIONS[b]
    return r, ATT_BLK * r, (D_ATTN if r == 1 else 128), BRANCH_BLOCKS[b] > 1


RESIDUES_UNROLLED = 16


def _for_residues(r, fn):
    if r <= RESIDUES_UNROLLED:
        for rho in range(r):
            fn(rho)
    else:
        def step(t, carry):
            for u in range(RESIDUES_UNROLLED):
                fn(RESIDUES_UNROLLED * t + u)
            return carry

        lax.fori_loop(0, r // RESIDUES_UNROLLED, step, 0)


def _strided_rows(start, r):
    if r > 1:
        return pl.ds(start, ATT_BLK, stride=r)
    return pl.ds(start if isinstance(start, int) else pl.multiple_of(start, ATT_BLK), ATT_BLK)


def _attention_fwd(qn, kn, vn, b):
    r, rows, lanes, with_prev = _attn_config(b)
    cur = pl.BlockSpec((rows, lanes), lambda g, n: (n, g))
    prev = pl.BlockSpec((rows, lanes), lambda g, n: (jnp.maximum(n - 1, 0), g))

    def body(*refs):
        ins, (o_ref, l_ref) = refs[:-2], refs[-2:]
        has_prev = pl.program_id(1) > 0

        def one(rho):
            sub = _strided_rows(rho, r)
            for pair in range(lanes // 128):
                sl = pl.ds(pair * 128, 128)
                args = [ref[sub, sl] for ref in ins] + ([has_prev] if with_prev else [])
                o_ref[sub, sl], l_ref[sub, sl] = _attn_pair(*args)

        _for_residues(r, one)

    operands = (qn, kn, vn, kn, vn) if with_prev else (qn, kn, vn)
    return pl.pallas_call(
        body, name="attn_fwd_%d" % r, grid=(D_ATTN // lanes, SEQ // rows),
        in_specs=[cur, cur, cur] + ([prev, prev] if with_prev else []), out_specs=[cur, cur],
        out_shape=[jax.ShapeDtypeStruct((SEQ, D_ATTN), F32)] * 2,
        compiler_params=_params(("parallel", "parallel")),
    )(*operands)


def _attn_pair_bwd(q, kc, vc, kp, vp, o, lse, do, dl, has_prev):
    pick0, pick1 = _head_picks()
    lane = lax.broadcasted_iota(jnp.int32, (1, 2 * HEAD), 1)
    k_band, v_band, mask = _attn_band(kc, vc, kp, vp, has_prev)
    q2 = jnp.concatenate([q * pick0, q * pick1], axis=0)
    do2 = jnp.concatenate([do * pick0, do * pick1], axis=0)
    lse2 = jnp.concatenate([jnp.sum(lse * (lane == 0).astype(F32), axis=-1, keepdims=True),
                            jnp.sum(lse * (lane == HEAD).astype(F32), axis=-1, keepdims=True)], axis=0)
    base = jnp.sum(jnp.concatenate([dl * pick0, dl * pick1], axis=0) - do2 * jnp.concatenate([o, o], axis=0),
                   axis=-1, keepdims=True)
    p = jnp.exp(jnp.where(mask, _bdot(q2, k_band, NT), NEG) - lse2)
    ds = p * (_bdot(do2, v_band, NT) + base)
    dq2 = _bdot(ds, k_band, NN)
    dq = pick0 * dq2[:ATT_BLK] + pick1 * dq2[ATT_BLK:]
    dk, dv = _bdot(ds, q2, TN), _bdot(p, do2, TN)
    if kp is None:
        return dq, dk, dv
    return dq, dk[ATT_BLK:], dv[ATT_BLK:], dk[:ATT_BLK], dv[:ATT_BLK]


def _attention_bwd(qn, kn, vn, o, lse, do, dl, b):
    r, rows, lanes, with_prev = _attn_config(b)
    cur = pl.BlockSpec((rows, lanes), lambda g, n: (n, g))
    prev = pl.BlockSpec((rows, lanes), lambda g, n: (jnp.maximum(n - 1, 0), g))
    whole = pl.BlockSpec((SEQ, lanes), lambda g, n: (0, g))
    n_in = 5 if with_prev else 3

    def body(*refs):
        ins, (o_ref, l_ref, do_ref, dl_ref, dq_ref, dk_ref, dv_ref) = refs[:n_in], refs[n_in:]
        n = pl.program_id(1)

        @pl.when(n == 0)
        def _():
            dk_ref[...] = jnp.zeros_like(dk_ref)
            dv_ref[...] = jnp.zeros_like(dv_ref)

        def one(rho):
            sub = _strided_rows(rho, r)
            sub_c = _strided_rows(n * rows + rho, r)
            sub_p = _strided_rows(jnp.maximum(n - 1, 0) * rows + rho, r)
            for pair in range(lanes // 128):
                sl = pl.ds(pair * 128, 128)
                vals = [ref[sub, sl] for ref in ins] + ([] if with_prev else [None, None])
                grads = _attn_pair_bwd(*vals, o_ref[sub, sl], l_ref[sub, sl], do_ref[sub, sl], dl_ref[sub, sl], n > 0)
                dq_ref[sub, sl] = grads[0]
                dk_ref[sub_c, sl] += grads[1]
                dv_ref[sub_c, sl] += grads[2]
                if with_prev:
                    dk_ref[sub_p, sl] += grads[3]
                    dv_ref[sub_p, sl] += grads[4]

        _for_residues(r, one)

    operands = (qn, kn, vn, kn, vn) if with_prev else (qn, kn, vn)
    return pl.pallas_call(
        body, name="attn_bwd_%d" % r, grid=(D_ATTN // lanes, SEQ // rows),
        in_specs=[cur, cur, cur] + ([prev, prev] if with_prev else []) + [cur] * 4, out_specs=[cur, whole, whole],
        out_shape=[jax.ShapeDtypeStruct((SEQ, D_ATTN), F32)] * 3,
        compiler_params=_params(("parallel", "arbitrary")),
    )(*operands, o, lse, do, dl)


CONV_COLS = 256
XBC_BLOCK0 = (3 * D_ATTN + D_SSM) // CONV_COLS


def _shift_rows(x, s):
    n = x.shape[0]
    t = lax.broadcasted_iota(jnp.int32, x.shape, 0)
    if s >= 0:
        return jnp.where(t >= s, pltpu.roll(x, s, 0), 0.0)
    return jnp.where(t < n + s, pltpu.roll(x, n + s, 0), 0.0)


def _conv_pre(x, w_ref, b_ref):
    delayed = [_shift_rows(x, 3 - k) for k in range(3)]
    pre = b_ref[...] + w_ref[3:4, :] * x
    for k in range(3):
        pre = pre + w_ref[k:k + 1, :] * delayed[k]
    return pre, delayed


def _conv_fwd(proj, conv_w, conv_b):
    cols = conv_w.shape[1]

    def body(x_ref, w_ref, b_ref, o_ref):
        pre, _ = _conv_pre(x_ref[...], w_ref, b_ref)
        o_ref[...] = pre * jax.nn.sigmoid(pre)

    blk = pl.BlockSpec((SEQ, CONV_COLS), lambda j: (0, j))
    return pl.pallas_call(
        body, name="conv_fwd", grid=(cols // CONV_COLS,),
        in_specs=[pl.BlockSpec((SEQ, CONV_COLS), lambda j: (0, XBC_BLOCK0 + j)),
                  pl.BlockSpec((4, CONV_COLS), lambda j: (0, j)), pl.BlockSpec((1, CONV_COLS), lambda j: (0, j))],
        out_specs=blk, out_shape=jax.ShapeDtypeStruct((SEQ, cols), F32),
        compiler_params=_params(("parallel",)),
    )(proj, conv_w, conv_b)


def _conv_bwd(proj, conv_w, conv_b, dxs, db, dc):
    cols = conv_w.shape[1]
    x_blocks, b_blocks = dxs.shape[1] // CONV_COLS, db.shape[1] // CONV_COLS

    def body(x_ref, w_ref, b_ref, dxs_ref, db_ref_in, dc_ref_in, dx_ref, dw_ref, db_ref):
        j = pl.program_id(0)
        dy = jnp.where(j < x_blocks, dxs_ref[...], jnp.where(j < x_blocks + b_blocks, db_ref_in[...], dc_ref_in[...]))
        x = x_ref[...]
        pre, delayed = _conv_pre(x, w_ref, b_ref)
        sg = jax.nn.sigmoid(pre)
        dpre = dy * (sg * (1.0 + pre * (1.0 - sg)))
        db_ref[...] = jnp.sum(dpre, axis=0, keepdims=True)
        dx = w_ref[3:4, :] * dpre
        dw_ref[3:4, :] = jnp.sum(dpre * x, axis=0, keepdims=True)
        for k in range(3):
            dx = dx + w_ref[k:k + 1, :] * _shift_rows(dpre, k - 3)
            dw_ref[k:k + 1, :] = jnp.sum(dpre * delayed[k], axis=0, keepdims=True)
        dw_ref[4:8, :] = jnp.zeros((4, CONV_COLS), F32)
        dx_ref[...] = dx.astype(dx_ref.dtype)

    blk = pl.BlockSpec((SEQ, CONV_COLS), lambda j: (0, j))
    parts = [pl.BlockSpec((SEQ, CONV_COLS), lambda j: (0, jnp.minimum(j, x_blocks - 1))),
             pl.BlockSpec((SEQ, CONV_COLS), lambda j: (0, jnp.clip(j - x_blocks, 0, b_blocks - 1))),
             pl.BlockSpec((SEQ, CONV_COLS), lambda j: (0, jnp.clip(j - x_blocks - b_blocks, 0, b_blocks - 1)))]
    return pl.pallas_call(
        body, name="conv_bwd", grid=(cols // CONV_COLS,),
        in_specs=[pl.BlockSpec((SEQ, CONV_COLS), lambda j: (0, XBC_BLOCK0 + j)),
                  pl.BlockSpec((4, CONV_COLS), lambda j: (0, j)), pl.BlockSpec((1, CONV_COLS), lambda j: (0, j))] + parts,
        out_specs=[blk, pl.BlockSpec((8, CONV_COLS), lambda j: (0, j)), pl.BlockSpec((1, CONV_COLS), lambda j: (0, j))],
        out_shape=[jax.ShapeDtypeStruct((SEQ, cols), BF16), jax.ShapeDtypeStruct((8, cols), F32),
                   jax.ShapeDtypeStruct((1, cols), F32)],
        compiler_params=_params(("parallel",)),
    )(proj, conv_w, conv_b, dxs, db, dc)


HEADS_PER_GROUP = 4


GROUP_WIDTH = HEADS_PER_GROUP * HEAD


def _ssd_chunk(x, bm, cm, dtr, bias, alog, dsk, h):
    row = lax.broadcasted_iota(jnp.int32, (CHUNK, CHUNK), 0)
    col = lax.broadcasted_iota(jnp.int32, (CHUNK, CHUNK), 1)
    causal = row >= col
    z = dtr + bias
    dt = jnp.maximum(z, 0.0) + jnp.log(1.0 + jnp.exp(-jnp.abs(z)))
    acs = _fdot(causal.astype(F32), dt * -jnp.exp(alog), NN)
    acs_t, dt_t = acs.T, dt.T
    cb = _bdot(cm, bm, NT)
    lane = lax.broadcasted_iota(jnp.int32, (1, CHUNK), 1)
    sub = lax.broadcasted_iota(jnp.int32, (CHUNK, 1), 0)
    wide = lax.broadcasted_iota(jnp.int32, (1, GROUP_WIDTH), 1) // HEAD
    tall = lax.broadcasted_iota(jnp.int32, (GROUP_WIDTH, 1), 0) // HEAD
    acs_last = jnp.sum(acs * (sub == CHUNK - 1).astype(F32), axis=0, keepdims=True)
    to_lanes = (lax.broadcasted_iota(jnp.int32, (CHUNK, GROUP_WIDTH), 0)
                == lax.broadcasted_iota(jnp.int32, (CHUNK, GROUP_WIDTH), 1) // HEAD).astype(F32)
    grow = _fdot(jnp.exp(acs), to_lanes, NN)
    keep = _fdot(jnp.exp(acs_last - acs) * dt, to_lanes, NN)
    w_parts, x_parts, skip, carry = [], [], 0.0, 0.0
    for j in range(HEADS_PER_GROUP):
        on_lane, on_sub = (lane == j).astype(F32), (sub == j).astype(F32)
        acs_c = jnp.sum(acs * on_lane, axis=1, keepdims=True)
        acs_r = jnp.sum(acs_t * on_sub, axis=0, keepdims=True)
        dt_r = jnp.sum(dt_t * on_sub, axis=0, keepdims=True)
        w_parts.append(cb * jnp.exp(jnp.where(causal, acs_c - acs_r, NEG)) * dt_r)
        x_parts.append(x * (wide == j).astype(F32))
        skip = skip + jnp.sum(dsk * on_lane, axis=1, keepdims=True) * (wide == j).astype(F32)
        carry = carry + jnp.sum(jnp.exp(acs_last) * on_lane, axis=1, keepdims=True) * (tall == j).astype(F32)
    y_diag = _bdot(jnp.concatenate(w_parts, axis=1), jnp.concatenate(x_parts, axis=0), NN)
    y = y_diag + _bdot(cm, h, NT) * grow + skip * x
    return y, h * carry + _bdot(x * keep, bm, TN)


GROUPS_PER_STEP = 4
SSD_STEPS = N_GROUPS // GROUPS_PER_STEP


def _ssd_specs(reverse):
    n_chunks = SEQ // CHUNK
    c_of = (lambda c: n_chunks - 1 - c) if reverse else (lambda c: c)
    x_w, n_w, dt_w = GROUPS_PER_STEP * GROUP_WIDTH, GROUPS_PER_STEP * N_STATE, GROUPS_PER_STEP * 128
    x_spec = pl.BlockSpec((CHUNK, x_w), lambda g, c: (c_of(c), g))
    b_spec = pl.BlockSpec((CHUNK, n_w), lambda g, c: (c_of(c), D_SSM // n_w + g))
    c_spec = pl.BlockSpec((CHUNK, n_w), lambda g, c: (c_of(c), (D_SSM + N_GROUPS * N_STATE) // n_w + g))
    dt_spec = pl.BlockSpec((CHUNK, dt_w), lambda g, c: (c_of(c), g))
    vec_spec = pl.BlockSpec((1, dt_w), lambda g, c: (0, g))
    h_spec = pl.BlockSpec((None, GROUPS_PER_STEP, GROUP_WIDTH, N_STATE), lambda g, c: (c_of(c), g, 0, 0))
    return x_spec, b_spec, c_spec, dt_spec, vec_spec, h_spec


def _group_slices(u):
    return pl.ds(u * GROUP_WIDTH, GROUP_WIDTH), pl.ds(u * N_STATE, N_STATE), pl.ds(u * 128, 128)


def _ssd_gated_chunk(x, bm, cm, dtr, bias, alog, dsk, h, z, g_out):
    y, h_new = _ssd_chunk(x, bm, cm, dtr, bias, alog, dsk, h)
    return _gate_fn(y, z, g_out)[0], h_new


def _ssd_gate_specs(reverse):
    x_spec = _ssd_specs(reverse)[0]
    z_block0 = 3 * D_ATTN // x_spec.block_shape[1]
    z_spec = pl.BlockSpec(x_spec.block_shape, lambda g, c: (x_spec.index_map(g, c)[0], z_block0 + g))
    return z_spec, pl.BlockSpec((1, x_spec.block_shape[1]), lambda g, c: (0, g))


def _ssd_fwd(xbc, dt_raw, bias, alog, dsk, proj, g_out):
    x_spec, b_spec, c_spec, dt_spec, vec_spec, h_spec = _ssd_specs(False)
    z_spec, g_spec = _ssd_gate_specs(False)

    def body(x_ref, b_ref, c_ref, dt_ref, bias_ref, alog_ref, dsk_ref, z_ref, g_ref, ssm_ref, hin_ref, h_scr):
        @pl.when(pl.program_id(1) == 0)
        def _():
            h_scr[...] = jnp.zeros_like(h_scr)

        for u in range(GROUPS_PER_STEP):
            xs, ns, ds = _group_slices(u)
            h = h_scr[u]
            hin_ref[u] = h
            ssm, h_scr[u] = _ssd_gated_chunk(x_ref[:, xs], b_ref[:, ns], c_ref[:, ns], dt_ref[:, ds], bias_ref[:, ds],
                                             alog_ref[:, ds], dsk_ref[:, ds], h, z_ref[:, xs], g_ref[:, xs])
            ssm_ref[:, xs] = ssm.astype(ssm_ref.dtype)

    return pl.pallas_call(
        body, name="ssd_fwd", grid=(SSD_STEPS, SEQ // CHUNK),
        in_specs=[x_spec, b_spec, c_spec, dt_spec, vec_spec, vec_spec, vec_spec, z_spec, g_spec],
        out_specs=[x_spec, h_spec],
        out_shape=[jax.ShapeDtypeStruct((SEQ, D_SSM), BF16),
                   jax.ShapeDtypeStruct((SEQ // CHUNK, N_GROUPS, GROUP_WIDTH, N_STATE), F32)],
        scratch_shapes=[pltpu.VMEM((GROUPS_PER_STEP, GROUP_WIDTH, N_STATE), F32)],
        compiler_params=_params(("parallel", "arbitrary")),
    )(xbc, xbc, xbc, dt_raw, bias, alog, dsk, proj, g_out)


def _ssd_bwd(xbc, dt_raw, bias, alog, dsk, h_in, proj, g_out, dmix):
    x_spec, b_spec, c_spec, dt_spec, vec_spec, h_spec = _ssd_specs(True)
    z_spec, g_spec = _ssd_gate_specs(True)
    ct_block0 = D_ATTN // x_spec.block_shape[1]
    ct_spec = pl.BlockSpec(x_spec.block_shape, lambda g, c: (x_spec.index_map(g, c)[0], ct_block0 + g))

    def body(x_ref, b_ref, c_ref, dt_ref, bias_ref, alog_ref, dsk_ref, hin_ref, z_ref, g_ref, ct_ref,
             dx_ref, db_ref, dc_ref, ddt_ref, dbias_ref, dalog_ref, ddsk_ref, dz_ref, dg_ref, dh_scr):
        first = pl.program_id(1) == 0

        @pl.when(first)
        def _():
            dh_scr[...] = jnp.zeros_like(dh_scr)

        for u in range(GROUPS_PER_STEP):
            xs, ns, ds = _group_slices(u)
            _, pullback = jax.vjp(_ssd_gated_chunk, x_ref[:, xs], b_ref[:, ns], c_ref[:, ns], dt_ref[:, ds], bias_ref[:, ds],
                                  alog_ref[:, ds], dsk_ref[:, ds], hin_ref[u], z_ref[:, xs], g_ref[:, xs])
            g = pullback((ct_ref[:, xs], dh_scr[u]))
            dx_ref[:, xs], db_ref[:, ns], dc_ref[:, ns] = g[0], g[1], g[2]
            ddt_ref[:, ds] = g[3].astype(ddt_ref.dtype)
            dh_scr[u] = g[7]
            dz_ref[:, xs] = g[8].astype(dz_ref.dtype)
            sums = ((dbias_ref, g[4], ds), (dalog_ref, g[5], ds), (ddsk_ref, g[6], ds),
                    (dg_ref, jnp.sum(g[9], axis=0, keepdims=True), xs))
            for o_ref, val, lanes in sums:
                @pl.when(first)
                def _(o_ref=o_ref, val=val, lanes=lanes):
                    o_ref[:, lanes] = val

                @pl.when(jnp.logical_not(first))
                def _(o_ref=o_ref, val=val, lanes=lanes):
                    o_ref[:, lanes] += val

    n_chunks = SEQ // CHUNK
    out_b = pl.BlockSpec((CHUNK, GROUPS_PER_STEP * N_STATE), lambda g, c: (n_chunks - 1 - c, g))
    return pl.pallas_call(
        body, name="ssd_bwd", grid=(SSD_STEPS, n_chunks),
        in_specs=[x_spec, b_spec, c_spec, dt_spec, vec_spec, vec_spec, vec_spec, h_spec, z_spec, g_spec, ct_spec],
        out_specs=[x_spec, out_b, out_b, dt_spec, vec_spec, vec_spec, vec_spec, x_spec, g_spec],
        out_shape=[jax.ShapeDtypeStruct((SEQ, D_SSM), F32), jax.ShapeDtypeStruct((SEQ, N_GROUPS * N_STATE), F32),
                   jax.ShapeDtypeStruct((SEQ, N_GROUPS * N_STATE), F32), jax.ShapeDtypeStruct((SEQ, DT_PAD), BF16),
                   jax.ShapeDtypeStruct((1, DT_PAD), F32), jax.ShapeDtypeStruct((1, DT_PAD), F32),
                   jax.ShapeDtypeStruct((1, DT_PAD), F32), jax.ShapeDtypeStruct((SEQ, D_SSM), BF16),
                   jax.ShapeDtypeStruct((1, D_SSM), F32)],
        scratch_shapes=[pltpu.VMEM((GROUPS_PER_STEP, GROUP_WIDTH, N_STATE), F32)],
        compiler_params=_params(("parallel", "arbitrary")),
    )(xbc, xbc, xbc, dt_raw, bias, alog, dsk, h_in, proj, g_out, dmix)


CROSS_HEAD = 128
CROSS_ROWS = 1024


def _cross_head(q, k, v, gq, gk):
    qn = _rms(q, gq) * (CROSS_HEAD ** -0.5)
    kn = _rms(k, gk)
    s = _bdot(qn, kn, NT)
    p = jnp.exp(s - lax.stop_gradient(jnp.max(s, axis=-1, keepdims=True)))
    return _bdot(p, v, NN) * (1.0 / jnp.sum(p, axis=-1, keepdims=True))


def _cross_specs():
    q_spec = pl.BlockSpec((CROSS_ROWS, CROSS_HEAD), lambda h, i: (i, h))
    k_spec = pl.BlockSpec((N_MEM, CROSS_HEAD), lambda h, i: (0, h))
    v_spec = pl.BlockSpec((N_MEM, CROSS_HEAD), lambda h, i: (0, 4 + h))
    g_spec = pl.BlockSpec((1, CROSS_HEAD), lambda h, i: (0, 0))
    return q_spec, k_spec, v_spec, g_spec


def _cross_fwd(qc, kv, gq, gk):
    q_spec, k_spec, v_spec, g_spec = _cross_specs()

    def body(q_ref, k_ref, v_ref, gq_ref, gk_ref, o_ref):
        o_ref[...] = _cross_head(q_ref[...], k_ref[...], v_ref[...], gq_ref[...], gk_ref[...]).astype(o_ref.dtype)

    return pl.pallas_call(
        body, name="cross_fwd", grid=(4, SEQ // CROSS_ROWS),
        in_specs=[q_spec, k_spec, v_spec, g_spec, g_spec], out_specs=q_spec,
        out_shape=jax.ShapeDtypeStruct((SEQ, D_CROSS), BF16),
        compiler_params=_params(("parallel", "parallel")),
    )(qc, kv, kv, gq, gk)


def _cross_bwd(qc, kv, gq, gk, do):
    q_spec, k_spec, v_spec, g_spec = _cross_specs()

    def body(q_ref, k_ref, v_ref, gq_ref, gk_ref, do_ref, dq_ref, dk_ref, dv_ref, dgq_ref, dgk_ref):
        _, pullback = jax.vjp(_cross_head, q_ref[...], k_ref[...], v_ref[...], gq_ref[...], gk_ref[...])
        dq, dk, dv, dgq, dgk = pullback(do_ref[...].astype(F32))
        dq_ref[...] = dq.astype(dq_ref.dtype)
        row0 = pl.program_id(1) == 0
        all0 = jnp.logical_and(row0, pl.program_id(0) == 0)
        for o_ref, val, init in ((dk_ref, dk, row0), (dv_ref, dv, row0), (dgq_ref, dgq, all0), (dgk_ref, dgk, all0)):
            @pl.when(init)
            def _(o_ref=o_ref, val=val):
                o_ref[...] = val

            @pl.when(jnp.logical_not(init))
            def _(o_ref=o_ref, val=val):
                o_ref[...] += val

    return pl.pallas_call(
        body, name="cross_bwd", grid=(4, SEQ // CROSS_ROWS),
        in_specs=[q_spec, k_spec, v_spec, g_spec, g_spec, q_spec],
        out_specs=[q_spec, k_spec, k_spec, g_spec, g_spec],
        out_shape=[jax.ShapeDtypeStruct((SEQ, D_CROSS), BF16), jax.ShapeDtypeStruct((N_MEM, D_CROSS), F32),
                   jax.ShapeDtypeStruct((N_MEM, D_CROSS), F32), jax.ShapeDtypeStruct((1, CROSS_HEAD), F32),
                   jax.ShapeDtypeStruct((1, CROSS_HEAD), F32)],
        compiler_params=_params(("arbitrary", "arbitrary")),
    )(qc, kv, kv, gq, gk, do)


def _loss_epilogue(acc, residual, target):
    err = acc + residual - target
    dy = err * (1.0 / D_MODEL)
    part = jnp.sum(jnp.sum(err * err, axis=1, keepdims=True), axis=0, keepdims=True) * (0.5 / D_MODEL)
    return dy, dy, part


def _pad_heads(v):
    return jnp.pad(v.reshape(N_GROUPS, HEADS_PER_GROUP), ((0, 0), (0, 128 - HEADS_PER_GROUP))).reshape(1, DT_PAD)


def _unpad_heads(v):
    return v.reshape(v.shape[0], N_GROUPS, 128)[:, :, :HEADS_PER_GROUP].reshape(v.shape[0], N_DT)


def _rope_tables(positions):
    half = ROT // 2
    inv_freq = ROPE_THETA ** (-2.0 * jnp.arange(half, dtype=F32) / ROT)
    ang = positions.reshape(SEQ, 1).astype(F32) * inv_freq
    cos, sin = jnp.cos(ang), jnp.sin(ang)
    ones, zeros = jnp.ones((SEQ, HEAD - ROT), F32), jnp.zeros((SEQ, HEAD - ROT), F32)
    cos_h = jnp.concatenate([cos, cos, ones], axis=1)
    sin_h = jnp.concatenate([-sin, sin, zeros], axis=1)
    return jnp.tile(cos_h, (1, 2)), jnp.tile(sin_h, (1, 2))


def _add_res(acc, res):
    return (acc + res,)


def _norm_bwd_epilogue(acc, x, residual, *more):
    *part, g = more
    ct = acc + part[0] if part else acc
    _, pullback = jax.vjp(_rms, x, g)
    dx, dg = pullback(ct)
    return dx + residual, dg


def _add_res_and_norm(acc, res, g):
    y = acc + res
    return y, _rms(y, g)


def _settle(grads, *after):
    if hasattr(grads, "settle"):
        grads.settle(*after)


def _take_token(grads):
    token = getattr(grads, "token", None)
    if token is None:
        return ()
    grads.token = None
    return (token,)


def _local_step(x, mem, positions, target, p, w, more_weights=None, grads=None, h=None):
    grads = {} if grads is None else grads
    w = dict(w)
    cos, sin = _rope_tables(positions)
    gq2, gk2 = jnp.tile(p["g_q"], (1, 2)), jnp.tile(p["g_k"], (1, 2))
    bias, alog, dsk = _pad_heads(p["dt_bias"]), _pad_heads(p["a_log"]), _pad_heads(p["d_skip"])
    norm_out = [(D_MODEL, BF16, D_MODEL, 0, False)]

    if h is None:
        h = _rowwise(_norm_fn, [_full(x)], [_full(p["g_mix"])], norm_out, name="norm_in")[0]
    proj = _matmul(h, w["w_in"], mode="nn", name="in_proj", outs=[F32], n_cols=D_MAIN)
    dt_raw = _matmul(h, w["w_dt"], mode="nn", name="dt_proj", outs=[F32])
    pairs = D_ATTN // 128
    qk_rows = [(proj, 128, 0, True), (proj, 128, pairs, True), (proj, 128, 2 * pairs, True), _full(cos), _full(sin)]
    qk_vecs = [_full(gq2), _full(gk2)]
    qn, kn, vn = _rowwise(_qk_fn, qk_rows, qk_vecs, [(D_ATTN, F32, 128, 0, True)] * 3, name="qk_prep", groups=8, tr=1024)
    branches = [_attention_fwd(qn, kn, vn, b) for b in range(3)]
    merge_rows = [_full(o) for o, _ in branches] + [_full(lse) for _, lse in branches]
    attn = _rowwise(_merge_fn, merge_rows, [_full(p["g_attn_out"])], [(D_ATTN, BF16, D_ATTN, 0, False)], name="attn_merge")[0]
    xbc = _conv_fwd(proj, p["conv_w"], p["conv_b"])
    ssm, h_in = _ssd_fwd(xbc, dt_raw, bias, alog, dsk, proj, p["g_ssm_out"])
    mix = jnp.concatenate([attn, ssm], axis=1)
    if more_weights is not None:
        w.update(more_weights("mixer_done", mix))
    x1, hc = _matmul(mix, w["w_out"], mode="nn", name="out_proj", outs=[F32, BF16], extra=(x,), vecs=(p["g_cross"],),
                     epilogue=_add_res_and_norm, tm=512, tn=D_MODEL)
    memh = _rowwise(_norm_fn, [_full(mem)], [_full(p["g_mem"])], norm_out, name="norm_mem", n_rows=N_MEM, tr=N_MEM)[0]
    qc = _matmul(hc, w["w_cq"], mode="nn", name="cq_proj", outs=[F32])
    if more_weights is not None:
        w.update(more_weights("cross_started", qc))
    kv = _matmul(memh, w["w_ckv"], mode="nn", name="ckv_proj", outs=[F32])
    oc = _cross_fwd(qc, kv, p["g_cq"], p["g_ck"])
    x2, hm = _matmul(oc, w["w_co"], mode="nn", name="co_proj", outs=[F32, BF16], extra=(x1,), vecs=(p["g_mlp"],),
                     epilogue=_add_res_and_norm, tm=512, tn=D_MODEL)
    if more_weights is not None:
        w.update(more_weights("cross_done", hm))
    u, act = _matmul(hm, w["w_up"], mode="nn", name="up_proj", outs=[F32, BF16],
                     epilogue=lambda acc: (acc, jnp.square(jnp.maximum(acc, 0.0))))
    dy, dyb, loss_tiles = _matmul(act, w["w_down"], mode="nn", name="down_proj", outs=[F32, BF16], extra=(x2, target),
                                  epilogue=_loss_epilogue, tile_sums=1)
    loss = jnp.sum(loss_tiles).reshape(1, 1)

    grads["w_down"] = _matmul(act, dyb, mode="tn", name="dw_down", outs=[BF16], after=_take_token(grads))
    du = _matmul(dyb, w["w_down"], mode="nt", name="d_act", outs=[BF16], extra=(u,), after=_take_token(grads),
                 epilogue=lambda acc, uu: (acc * (2.0 * jnp.maximum(uu, 0.0)),))
    _settle(grads, du)
    grads["w_up"] = _matmul(hm, du, mode="tn", name="dw_up", outs=[BF16], col_shards=4, after=_take_token(grads))
    dx2, grads["g_mlp"] = _matmul(du, w["w_up"], mode="nt", name="d_hm", outs=[F32], extra=(x2, dy), vecs=(p["g_mlp"],),
                                  epilogue=_norm_bwd_epilogue, tile_rows=1, after=_take_token(grads), tm=512, tn=D_MODEL,
                                  tk=1024)
    _settle(grads, dx2)
    grads["w_co"] = _matmul(oc, dx2, mode="tn", name="dw_co", outs=[BF16], col_shards=4, after=_take_token(grads))
    doc = _matmul(dx2, w["w_co"], mode="nt", name="d_oc", outs=[BF16])
    dqc, dkc, dvc, grads["g_cq"], grads["g_ck"] = _cross_bwd(qc, kv, p["g_cq"], p["g_ck"], doc)
    grads["w_cq"] = _matmul(hc, dqc, mode="tn", name="dw_cq", outs=[BF16])
    dkv = jnp.concatenate([dkc, dvc], axis=1)
    grads["w_ckv"] = _matmul(memh, dkv, mode="tn", name="dw_ckv", outs=[BF16])
    dmemh = _matmul(dkv, w["w_ckv"], mode="nt", name="d_memh", outs=[F32])
    grads["g_mem"] = _rowwise_vjp(_norm_fn, [_full(mem)], [_full(p["g_mem"])], [[_full(dmemh)]], [],
                                  [(0, D_MODEL, D_MODEL, 0, False)], name="norm_mem_bwd", n_rows=N_MEM, tr=N_MEM)[0]
    dx1, grads["g_cross"] = _matmul(dqc, w["w_cq"], mode="nt", name="d_hc", outs=[F32], extra=(x1, dx2), vecs=(p["g_cross"],),
                                    epilogue=_norm_bwd_epilogue, tile_rows=1, tm=512, tn=D_MODEL)
    grads["w_out"] = _matmul(mix, dx1, mode="tn", name="dw_out", outs=[BF16])
    dmix = _matmul(dx1, w["w_out"], mode="nt", name="d_mix", outs=[F32], after=_take_token(grads))
    _settle(grads, dmix)
    merge_grads = [(i, D_ATTN, F32, D_ATTN, 0, False, None) for i in range(6)]
    *dol, grads["g_attn_out"] = _rowwise_vjp(
        _merge_fn, merge_rows, [_full(p["g_attn_out"])], [[(dmix, D_ATTN, 0, False)]],
        merge_grads, [(0, D_ATTN, D_ATTN, 0, False)], name="attn_merge_bwd", tr=256, after=_take_token(grads))
    dqkv = [_attention_bwd(qn, kn, vn, *branches[b], dol[b], dol[3 + b], b) for b in range(3)]
    qk_cts = [[(dqkv[b][i], 128, 0, True) for b in range(3)] for i in range(3)]
    dq, dk, dv, dgq2, dgk2 = _rowwise_vjp(
        _qk_fn, qk_rows, qk_vecs, qk_cts, [(i, D_ATTN, BF16, 128, 0, True, None) for i in range(3)],
        [(0, 128, 128, 0, False), (1, 128, 128, 0, False)], name="qk_prep_bwd", groups=8, tr=1024)
    grads["g_q"] = dgq2[:, :HEAD] + dgq2[:, HEAD:]
    grads["g_k"] = dgk2[:, :HEAD] + dgk2[:, HEAD:]
    dxs, db, dc, ddt, dbias, dalog, ddsk, dz, grads["g_ssm_out"] = _ssd_bwd(xbc, dt_raw, bias, alog, dsk, h_in, proj,
                                                                             p["g_ssm_out"], dmix)
    grads["dt_bias"], grads["a_log"], grads["d_skip"] = _unpad_heads(dbias), _unpad_heads(dalog), _unpad_heads(ddsk)
    dxbc_raw, dconv_w, grads["conv_b"] = _conv_bwd(proj, p["conv_w"], p["conv_b"], dxs, db, dc)
    grads["conv_w"] = dconv_w[:4]
    dproj = jnp.concatenate([dq, dk, dv, dz, dxbc_raw], axis=1)
    grads["w_main"] = _matmul(h, dproj, mode="tn", name="dw_main", outs=[BF16], out_cols=D_MAIN + N_DT)
    grads["w_dt"] = _matmul(h, ddt, mode="tn", name="dw_dt", outs=[BF16])
    dh = _matmul(dproj, w["w_in"], mode="nt", name="d_h_main", outs=[F32], after=_take_token(grads))
    grad_x, grads["g_mix"] = _matmul(ddt, w["w_dt"], mode="nt", name="d_h_dt", outs=[F32], extra=(x, dx1, dh),
                                     vecs=(p["g_mix"],), epilogue=_norm_bwd_epilogue, tile_rows=1, tm=512, tn=D_MODEL)
    return loss, grad_x, grads


MATRICES = ("w_in", "w_out", "w_cq", "w_ckv", "w_co", "w_up", "w_down")
ROW_SHARDED = ("w_out", "w_cq", "w_ckv", "w_down")
N_CHIPS = 4
ANY = pl.BlockSpec(memory_space=pl.ANY)


def _place():
    return lax.axis_index("x"), lax.axis_index("y"), lax.axis_index("c")


def _other_chips(x, y):
    return [(1 - x, y), (x, 1 - y), (1 - x, 1 - y)]


def _remote(src, dst, send_sem, recv_sem, device):
    return pltpu.make_async_remote_copy(src_ref=src, dst_ref=dst, send_sem=send_sem, recv_sem=recv_sem,
                                        device_id=device, device_id_type=MESH)


def _gathered_shape(name, shard):
    rows, cols = shard.shape
    if name == "w_in":
        return (N_CHIPS, rows, cols)
    return (N_CHIPS * rows, cols) if name in ROW_SHARDED else (rows, N_CHIPS * cols)


def _shard_window(name, ref, rows, cols, chip, half):
    r0, nr = (0, rows) if half is None else (half * (rows // 2), rows // 2)
    if name == "w_in":
        return ref.at[chip, pl.ds(r0, nr), :]
    if name in ROW_SHARDED:
        return ref.at[pl.ds(chip * rows + r0, nr), :]
    return ref.at[pl.ds(r0, nr), pl.ds(pl.multiple_of(chip * cols, 128), cols)]


def _cast_into_gathered(w, name, chip, after=()):
    rows, cols = w.shape
    tr = _tile(rows, ROW_TILE)

    def body(chip_ref, w_ref, *rest):
        rest[-1][...] = w_ref[...].astype(BF16)

    if name == "w_in":
        out_spec = pl.BlockSpec((None, tr, cols), lambda i, chip_ref: (chip_ref[0], i, 0))
    elif name in ROW_SHARDED:
        out_spec = pl.BlockSpec((tr, cols), lambda i, chip_ref: (chip_ref[0] * (rows // tr) + i, 0))
    else:
        out_spec = pl.BlockSpec((tr, cols), lambda i, chip_ref: (i, chip_ref[0]))
    grid_spec = pltpu.PrefetchScalarGridSpec(
        num_scalar_prefetch=1, grid=(rows // tr,),
        in_specs=[pl.BlockSpec((tr, cols), lambda i, chip_ref: (i, 0))] + [pl.BlockSpec(memory_space=pl.ANY)] * len(after),
        out_specs=out_spec)
    return pl.pallas_call(body, name="cast_" + name, grid_spec=grid_spec,
                          out_shape=jax.ShapeDtypeStruct(_gathered_shape(name, w), BF16),
                          compiler_params=_params(("parallel",)))(chip.reshape(1).astype(jnp.int32), w, *after)


def _w_in_columns(arr, to_shards):
    rows, piece = D_MODEL, (D_MAIN + N_DT) // N_CHIPS
    tr = ROW_TILE

    def body(a_ref, o_ref):
        for j in range(N_CHIPS):
            if to_shards:
                o_ref[j] = a_ref[:, pl.ds(piece * j, piece)]
            else:
                o_ref[:, pl.ds(piece * j, piece)] = a_ref[j]

    pieces = pl.BlockSpec((N_CHIPS, tr, piece), lambda i: (0, i, 0))
    matrix = pl.BlockSpec((tr, N_CHIPS * piece), lambda i: (i, 0))
    out_dims = (N_CHIPS, rows, piece) if to_shards else (rows, N_CHIPS * piece)
    return pl.pallas_call(
        body, name="w_in_to_shards" if to_shards else "w_in_from_shards", grid=(rows // tr,),
        in_specs=[matrix if to_shards else pieces], out_specs=pieces if to_shards else matrix,
        out_shape=jax.ShapeDtypeStruct(out_dims, arr.dtype), compiler_params=_params(("parallel",)))(arr)


HBM = pl.BlockSpec(memory_space=pltpu.HBM)
SEM = pl.BlockSpec(memory_space=pltpu.SEMAPHORE)
EFFECT = pltpu.SideEffectType.DATAFLOW_SIDE_EFFECTING


def _split_start(name, bufs, plan, counts, after=()):
    n, n_g, n_after = len(bufs), len(counts), len(after)

    def body(*refs):
        ins, sems, token = refs[:n], refs[n + n_after:n + n_after + 2 * n_g], refs[-1]
        for g, copies in enumerate(plan(ins)):
            for i, (src, dst, device, _) in enumerate(copies):
                _remote(src, dst, sems[2 * g].at[i], sems[2 * g + 1].at[i], device).start()
        token[...] = jnp.zeros_like(token)

    sem_shapes = [pltpu.SemaphoreType.DMA((cnt,)) for cnt in counts for _ in range(2)]
    res = pl.pallas_call(
        body, name=name,
        out_shape=(*sem_shapes, *[pltpu.HBM(b.shape, b.dtype) for b in bufs], jax.ShapeDtypeStruct((8, 128), F32)),
        in_specs=(*(HBM,) * n, *(ANY,) * n_after),
        out_specs=(*(SEM,) * (2 * n_g), *(HBM,) * n, pl.BlockSpec(memory_space=pltpu.VMEM)),
        input_output_aliases={i: 2 * n_g + i for i in range(n)},
        compiler_params=pltpu.CompilerParams(has_side_effects=EFFECT),
    )(*[pltpu.with_memory_space_constraint(b, pltpu.HBM) for b in bufs], *after)
    sems = [(res[2 * g], res[2 * g + 1]) for g in range(n_g)]
    return sems, list(res[2 * n_g:2 * n_g + n]), res[-1]


def _split_wait(name, bufs, sems, plan, *after):
    n = len(bufs)

    def body(*refs):
        ins, send, recv = refs[:n], refs[n], refs[n + 1]
        (copies,) = plan(ins)
        for i, (src, _, device, landing) in enumerate(copies):
            cp = _remote(src, landing, send.at[i], recv.at[i], device)
            cp.wait_send()
            cp.wait_recv()

    res = pl.pallas_call(
        body, name=name, out_shape=tuple(pltpu.HBM(b.shape, b.dtype) for b in bufs),
        in_specs=(*(HBM,) * n, SEM, SEM, *(ANY,) * len(after)), out_specs=(HBM,) * n,
        input_output_aliases={i: i for i in range(n)},
        compiler_params=pltpu.CompilerParams(has_side_effects=EFFECT),
    )(*bufs, sems[0], sems[1], *after)
    return list(res)


def _ici_plan(names, shard_shapes):
    def plan(refs):
        x, y, c = _place()
        copies = []
        for ref, name in zip(refs, names):
            win = _shard_window(name, ref, *shard_shapes[name], 2 * x + y, c)
            for px, py in _other_chips(x, y):
                copies.append((win, win, (px, py, c), _shard_window(name, ref, *shard_shapes[name], 2 * px + py, c)))
        return [copies]
    return plan


def _pass_on_plan(names, shard_shapes):
    def plan(refs):
        x, y, c = _place()
        copies = []
        for ref, name in zip(refs, names):
            for px, py in _other_chips(x, y):
                win = _shard_window(name, ref, *shard_shapes[name], 2 * px + py, c)
                copies.append((win, win, (x, y, 1 - c), _shard_window(name, ref, *shard_shapes[name], 2 * px + py, 1 - c)))
        return [copies]
    return plan


def _swap_plan(n_pairs):
    def plan(refs):
        x, y, c = _place()
        return [[(src.at[:, 1 - c], dst, (x, y, 1 - c), dst) for src, dst in zip(refs[:n_pairs], refs[n_pairs:])]]
    return plan


def _share_plan(n_pairs):
    def plan(refs):
        x, y, c = _place()
        return [[(src, dst, (x, y, 1 - c), dst)] for src, dst in zip(refs[:n_pairs], refs[n_pairs:])]
    return plan


def _scatter_plan(n_pairs):
    def plan(refs):
        x, y, c = _place()
        copies = []
        for src, dst in zip(refs[:n_pairs], refs[n_pairs:]):
            for k, (px, py) in enumerate(_other_chips(x, y)):
                copies.append((src.at[2 * px + py], dst.at[k], (px, py, c), dst.at[k]))
        return [copies]
    return plan


def _sibling_swap(arrs, name):
    n = len(arrs)

    def body(*refs):
        ins, outs, send, recv = refs[:n], refs[n:2 * n], refs[2 * n], refs[2 * n + 1]
        x, y, c = _place()
        cps = [_remote(ins[w].at[:, 1 - c], outs[w], send.at[w], recv.at[w], (x, y, 1 - c)) for w in range(n)]
        for cp in cps:
            cp.start()
        for cp in cps:
            cp.wait()

    return pl.pallas_call(
        body, name=name, in_specs=[ANY] * n, out_specs=[ANY] * n,
        out_shape=[jax.ShapeDtypeStruct((a.shape[0],) + a.shape[2:], a.dtype) for a in arrs],
        scratch_shapes=[pltpu.SemaphoreType.DMA((n,))] * 2,
    )(*arrs)


def _small_allreduce(buf, name, after=()):
    rows = buf.shape[0]

    def body(x_ref, *rest):
        out_ref, all_ref, send_sems, recv_sems, local_sem = rest[len(after):]
        x, y, c = _place()
        me, sibling, chips = (x, y, c), (x, y, 1 - c), _other_chips(x, y)

        def block(px, py, pc):
            return all_ref.at[pl.ds((4 * px + 2 * py + pc) * rows, rows), :]

        def copy(k, blk, to, src=None):
            return _remote(block(*blk) if src is None else src, block(*blk), send_sems.at[k], recv_sems.at[k], to)

        own = pltpu.make_async_copy(x_ref, block(*me), local_sem)
        own.start()
        first = [copy(0, me, sibling, src=x_ref)] + [copy(1 + j, me, (*chip, c), src=x_ref) for j, chip in enumerate(chips)]
        for cp in first:
            cp.start()
        passed = [copy(4 + j, (*chip, c), sibling) for j, chip in enumerate(chips)]
        for j, chip in enumerate(chips):
            copy(1 + j, (*chip, c), me).wait_recv()
            passed[j].start()
        copy(0, sibling, me).wait_recv()
        for j, chip in enumerate(chips):
            copy(4 + j, (*chip, 1 - c), me).wait_recv()
        for cp in first + passed:
            cp.wait_send()
        own.wait()
        acc = all_ref[pl.ds(0, rows), :]
        for d in range(1, 8):
            acc = acc + all_ref[pl.ds(d * rows, rows), :]
        out_ref[...] = acc

    vmem = pl.BlockSpec(memory_space=pltpu.VMEM)
    return pl.pallas_call(
        body, name=name, in_specs=[vmem] + [ANY] * len(after), out_specs=vmem,
        out_shape=jax.ShapeDtypeStruct(buf.shape, F32),
        scratch_shapes=[pltpu.VMEM((8 * rows, 128), F32), pltpu.SemaphoreType.DMA((7,)), pltpu.SemaphoreType.DMA((7,)),
                        pltpu.SemaphoreType.DMA],
    )(buf, *after)


ROW_TILE = 256
BIG_ROW_TILE = 1024


def _add_halves(arr, recv, c, name):
    _, _, hr, cols = arr.shape
    tr = _tile(hr, BIG_ROW_TILE)

    def body(c_ref, a_ref, r_ref, o_ref):
        o_ref[...] = (a_ref[...].astype(F32) + r_ref[...].astype(F32)).astype(o_ref.dtype)

    piece = pl.BlockSpec((None, tr, cols), lambda j, i, c_ref: (j, i, 0))
    grid_spec = pltpu.PrefetchScalarGridSpec(
        num_scalar_prefetch=1, grid=(N_CHIPS, hr // tr),
        in_specs=[pl.BlockSpec((None, None, tr, cols), lambda j, i, c_ref: (j, c_ref[0], i, 0)), piece], out_specs=piece)
    return pl.pallas_call(body, name=name, grid_spec=grid_spec, out_shape=jax.ShapeDtypeStruct(recv.shape, BF16),
                          compiler_params=_params(("parallel", "parallel")))(c.reshape(1).astype(jnp.int32), arr, recv)


def _flip_slot(d):
    return jnp.where(d == 1, 1, jnp.where(d == 3, 2, 0))


def _sum_chips(p, q, chip, name):
    _, hr, cols = p.shape
    tr = _tile(hr, BIG_ROW_TILE)

    def body(chip_ref, p_ref, q_ref, o_ref):
        j = pl.program_id(1)
        term = jnp.where(j == chip_ref[0], p_ref[...].astype(F32), q_ref[...].astype(F32))

        @pl.when(j == 0)
        def _():
            o_ref[...] = term

        @pl.when(j != 0)
        def _():
            o_ref[...] += term

    grid_spec = pltpu.PrefetchScalarGridSpec(
        num_scalar_prefetch=1, grid=(hr // tr, N_CHIPS),
        in_specs=[pl.BlockSpec((None, tr, cols), lambda i, j, chip_ref: (chip_ref[0], i, 0)),
                  pl.BlockSpec((None, tr, cols), lambda i, j, chip_ref: (_flip_slot(j ^ chip_ref[0]), i, 0))],
        out_specs=pl.BlockSpec((tr, cols), lambda i, j, chip_ref: (i, 0)))
    return pl.pallas_call(body, name=name, grid_spec=grid_spec, out_shape=jax.ShapeDtypeStruct((hr, cols), F32),
                          compiler_params=_params(("parallel", "arbitrary")))(chip.reshape(1).astype(jnp.int32), p, q)


def _adamw_halves(w, g_own, g_other, m, v, c, name):
    rows, cols = w.shape
    tr = _tile(rows // 2, ROW_TILE)
    per_half = rows // 2 // tr

    def body(c_ref, w_ref, own_ref, other_ref, m_ref, v_ref, g_ref, d_ref, nm_ref, nv_ref):
        mine = (pl.program_id(0) // per_half) == c_ref[0]
        g_ = jnp.where(mine, own_ref[...], other_ref[...])
        g_ref[...] = g_
        d_ref[...], nm_ref[...], nv_ref[...] = _adamw_math(w_ref[...], g_, m_ref[...], v_ref[...])

    blk = pl.BlockSpec((tr, cols), lambda i, c_ref: (i, 0))
    own = pl.BlockSpec((tr, cols), lambda i, c_ref: (jnp.where(i // per_half == c_ref[0], i % per_half, 0), 0))
    other = pl.BlockSpec((tr, cols), lambda i, c_ref: (jnp.where(i // per_half == c_ref[0], 0, i % per_half), 0))
    grid_spec = pltpu.PrefetchScalarGridSpec(num_scalar_prefetch=1, grid=(rows // tr,),
                                             in_specs=[blk, own, other, blk, blk], out_specs=[blk] * 4)
    return pl.pallas_call(body, name=name, grid_spec=grid_spec, out_shape=[jax.ShapeDtypeStruct(w.shape, F32)] * 4,
                          compiler_params=_params(("parallel",)))(c.reshape(1).astype(jnp.int32), w, g_own, g_other, m, v)


W_IN_COLS = (D_MAIN + N_DT) // N_CHIPS
W_IN_MAIN = W_IN_COLS // 128 * 128
W_IN_TAIL = W_IN_COLS - 128
W_IN_PARTS = ((0, W_IN_MAIN), (W_IN_TAIL, 128))


def _cast_w_in_transposed(w_t, chip, after=()):
    def body(chip_ref, w_ref, *rest):
        for start, size in W_IN_PARTS:
            rest[-1][:, pl.ds(start, size)] = w_ref[pl.ds(start, size), :].T.astype(BF16)

    grid_spec = pltpu.PrefetchScalarGridSpec(
        num_scalar_prefetch=1, grid=(D_MODEL // ROW_TILE,),
        in_specs=[pl.BlockSpec((W_IN_COLS, ROW_TILE), lambda i, chip_ref: (0, i))] + [pl.BlockSpec(memory_space=pl.ANY)] * len(after),
        out_specs=pl.BlockSpec((None, ROW_TILE, W_IN_COLS), lambda i, chip_ref: (chip_ref[0], i, 0)))
    return pl.pallas_call(body, name="cast_w_in", grid_spec=grid_spec,
                          out_shape=jax.ShapeDtypeStruct((N_CHIPS, D_MODEL, W_IN_COLS), BF16),
                          compiler_params=_params(("parallel",)))(chip.reshape(1).astype(jnp.int32), w_t, *after)


def _adamw_w_in_transposed(w_t, g_own, g_other, m_t, v_t, c):
    per_half = D_MODEL // 2 // ROW_TILE

    def body(c_ref, w_ref, own_ref, other_ref, m_ref, v_ref, g_ref, d_ref, nm_ref, nv_ref):
        mine = (pl.program_id(0) // per_half) == c_ref[0]
        for start, size in W_IN_PARTS:
            cols, rows = pl.ds(start, size), pl.ds(start, size)
            g_ = jnp.where(mine, own_ref[:, cols], other_ref[:, cols]).T
            g_ref[rows, :] = g_
            d_ref[rows, :], nm_ref[rows, :], nv_ref[rows, :] = _adamw_math(w_ref[rows, :], g_, m_ref[rows, :], v_ref[rows, :])

    blk = pl.BlockSpec((W_IN_COLS, ROW_TILE), lambda i, c_ref: (0, i))
    own = pl.BlockSpec((ROW_TILE, W_IN_COLS), lambda i, c_ref: (jnp.where(i // per_half == c_ref[0], i % per_half, 0), 0))
    other = pl.BlockSpec((ROW_TILE, W_IN_COLS), lambda i, c_ref: (jnp.where(i // per_half == c_ref[0], 0, i % per_half), 0))
    grid_spec = pltpu.PrefetchScalarGridSpec(num_scalar_prefetch=1, grid=(D_MODEL // ROW_TILE,),
                                             in_specs=[blk, own, other, blk, blk], out_specs=[blk] * 4)
    return pl.pallas_call(body, name="adamw_w_in", grid_spec=grid_spec, out_shape=[jax.ShapeDtypeStruct(w_t.shape, F32)] * 4,
                          compiler_params=_params(("parallel",)))(c.reshape(1).astype(jnp.int32), w_t, g_own, g_other, m_t, v_t)


def _adamw_math(w, g, m, v):
    m_new = ADAM_B1 * m + (1.0 - ADAM_B1) * g
    v_new = ADAM_B2 * v + (1.0 - ADAM_B2) * (g * g)
    m_hat = m_new / (1.0 - ADAM_B1 ** ADAM_STEP)
    v_hat = v_new / (1.0 - ADAM_B2 ** ADAM_STEP)
    return -ADAM_LR * (m_hat / (jnp.sqrt(v_hat) + ADAM_EPS) + ADAM_WD * w), m_new, v_new


VECTORS = ("g_mix", "g_q", "g_k", "g_attn_out", "conv_b", "dt_bias", "a_log", "d_skip", "g_ssm_out", "g_cross", "g_mem",
           "g_cq", "g_ck", "g_mlp")
WEIGHTS = ("g_mix", "w_in", "g_q", "g_k", "g_attn_out", "conv_w", "conv_b", "dt_bias", "a_log", "d_skip", "g_ssm_out", "w_out",
           "g_cross", "g_mem", "w_cq", "w_ckv", "g_cq", "g_ck", "w_co", "g_mlp", "w_up", "w_down")


def _pack(parts):
    flat = jnp.concatenate([t.reshape(-1) for t in parts])
    total = -(-flat.shape[0] // 1024) * 1024
    return jnp.pad(flat, (0, total - flat.shape[0])).reshape(total // 128, 128)


def _rows_of(n):
    return -(-n // 128)


def _slot_rows(n):
    return -(-n // 1024) * 8


def _pack_rows(parts):
    rows = []
    for t in parts:
        flat = t.reshape(-1)
        rows.append(jnp.pad(flat, (0, 128 * _slot_rows(flat.shape[0]) - flat.shape[0])).reshape(-1, 128))
    return jnp.concatenate(rows)


def _adamw_vectors(summed, chip, vectors, conv):
    groups = list(vectors) + [conv]
    offsets, row = [], 0
    for w, _, _ in groups:
        offsets.append(row)
        row += _slot_rows(w.shape[1]) if w.shape[0] == 1 else _slot_rows(w.shape[0] * N_CHIPS * w.shape[1])
    conv_blocks = _rows_of(conv[0].shape[1])

    def body(chip_ref, sum_ref, *refs):
        ins, outs = refs[:3 * len(groups)], refs[3 * len(groups):]

        def update(i, g, idx):
            w_ref, m_ref, v_ref = ins[3 * i:3 * i + 3]
            delta, new_m, new_v = _adamw_math(w_ref[idx], g, m_ref[idx], v_ref[idx])
            for o_ref, val in zip(outs[4 * i:4 * i + 4], (g, delta, new_m, new_v)):
                o_ref[idx] = val

        for i, (w, _, _) in enumerate(vectors):
            for t in range(_rows_of(w.shape[1])):
                width = min(128, w.shape[1] - 128 * t)
                update(i, sum_ref[pl.ds(offsets[i] + t, 1), pl.ds(0, width)], (slice(None), pl.ds(128 * t, width)))
        for tap in range(conv[0].shape[0]):
            for blk in range(conv_blocks):
                src = offsets[-1] + tap * N_CHIPS * conv_blocks + chip_ref[0] * conv_blocks + blk
                update(len(vectors), sum_ref[pl.ds(src, 1), :], (pl.ds(tap, 1), pl.ds(128 * blk, 128)))

    def whole(a):
        return pl.BlockSpec(a.shape, lambda i, chip_ref: (0,) * a.ndim)

    operands = [t for group in groups for t in group]
    grid_spec = pltpu.PrefetchScalarGridSpec(
        num_scalar_prefetch=1, grid=(1,), in_specs=[whole(summed)] + [whole(t) for t in operands],
        out_specs=[whole(w) for w, _, _ in groups for _ in range(4)])
    res = pl.pallas_call(body, name="adamw_vectors", grid_spec=grid_spec,
                         out_shape=[jax.ShapeDtypeStruct(w.shape, F32) for w, _, _ in groups for _ in range(4)],
                         compiler_params=_params(("arbitrary",)))(chip.reshape(1).astype(jnp.int32), summed, *operands)
    return [res[4 * i:4 * i + 4] for i in range(len(groups))]


def _unpack(buf, shapes):
    flat, out, pos = buf.reshape(-1), [], 0
    for shape in shapes:
        size = math.prod(shape)
        out.append(flat[pos:pos + size].reshape(shape))
        pos += size
    return out


def kernel(x, mem, positions, g_mix, w_in, g_q, g_k, g_attn_out, conv_w, conv_b, dt_bias, a_log, d_skip, g_ssm_out, w_out, g_cross, g_mem, w_cq, w_ckv, g_cq, g_ck, w_co, g_mlp, w_up, w_down, loss_target, m_g_mix, m_w_in, m_g_q, m_g_k, m_g_attn_out, m_conv_w, m_conv_b, m_dt_bias, m_a_log, m_d_skip, m_g_ssm_out, m_w_out, m_g_cross, m_g_mem, m_w_cq, m_w_ckv, m_g_cq, m_g_ck, m_w_co, m_g_mlp, m_w_up, m_w_down, v_g_mix, v_w_in, v_g_q, v_g_k, v_g_attn_out, v_conv_w, v_conv_b, v_dt_bias, v_a_log, v_d_skip, v_g_ssm_out, v_w_out, v_g_cross, v_g_mem, v_w_cq, v_w_ckv, v_g_cq, v_g_ck, v_w_co, v_g_mlp, v_w_up, v_w_down):
    args = dict(locals())
    weights = {n: args[n][0] for n in WEIGHTS}
    mom_m = {n: args["m_" + n][0] for n in WEIGHTS}
    mom_v = {n: args["v_" + n][0] for n in WEIGHTS}
    x_idx, y_idx, c_idx = _place()
    chip = 2 * x_idx + y_idx

    shapes = {n: weights[n].shape for n in MATRICES}
    first, mid, late = ("w_in",), ("w_out", "w_cq", "w_ckv", "w_co"), ("w_up", "w_down")
    w_in_t, m_in_t, v_in_t = (jnp.swapaxes(t, 1, 2)[0] for t in (w_in, m_w_in, v_w_in))
    w_in_buf = [_cast_w_in_transposed(w_in_t, chip)]
    taps, tap_cols = weights["conv_w"].shape
    conv_parts = _small_allreduce(_pack([jnp.zeros((N_CHIPS, taps, tap_cols), F32).at[chip].set(0.5 * weights["conv_w"])]),
                                  "gather_conv_taps")
    sems_in, w_in_buf, token = _split_start("gather_ici_start_w_in", w_in_buf, _ici_plan(first, shapes), [3], after=(conv_parts,))
    bufs = [_cast_into_gathered(weights[n], n, chip, after=(token,)) for n in mid + late]
    plan = lambda refs: _ici_plan(mid, shapes)(refs[:4]) + _ici_plan(late, shapes)(refs[4:])
    sems_rest, bufs, token = _split_start("gather_ici_start_rest", bufs, plan, [12, 6], after=(token,))
    params = {n: weights[n].reshape(1, -1) for n in VECTORS}
    h_in = _rowwise(_norm_fn, [_full(x[0])], [_full(params["g_mix"])], [(D_MODEL, BF16, D_MODEL, 0, False)], name="norm_in",
                    after=(token,))[0]
    w_in_buf = _split_wait("gather_ici_wait_w_in", w_in_buf, sems_in[0], _ici_plan(first, shapes), token, h_in, m_in_t, v_in_t)
    pass_sems, w_in_buf, token = _split_start("gather_pass_start_w_in", w_in_buf, _pass_on_plan(first, shapes), [3])
    w_in_buf = _split_wait("gather_pass_wait_w_in", w_in_buf, pass_sems[0], _pass_on_plan(first, shapes), token)
    w_in_full = _w_in_columns(w_in_buf[0], to_shards=False)
    full = {"w_in": w_in_full,
            "w_dt": jnp.pad(w_in_full[:, D_MAIN:].reshape(D_MODEL, N_GROUPS, HEADS_PER_GROUP),
                            ((0, 0), (0, 0), (0, 128 - HEADS_PER_GROUP))).reshape(D_MODEL, DT_PAD)}
    in_flight = {}

    def more_weights(stage, after):
        if stage == "mixer_done":
            got = _split_wait("gather_ici_wait_mid", bufs[:4], sems_rest[0], _ici_plan(mid, shapes), after)
            sems, got, token = _split_start("gather_pass_start_mid", got, _pass_on_plan(mid, shapes), [12])
            return dict(zip(mid, _split_wait("gather_pass_wait_mid", got, sems[0], _pass_on_plan(mid, shapes), token)))
        if stage == "cross_started":
            got = _split_wait("gather_ici_wait_late", bufs[4:], sems_rest[1], _ici_plan(late, shapes), after)
            in_flight["late"] = _split_start("gather_pass_start_late", got, _pass_on_plan(late, shapes), [6])
            return {}
        sems, got, token = in_flight.pop("late")
        return dict(zip(late, _split_wait("gather_pass_wait_late", got, sems[0], _pass_on_plan(late, shapes), token, after)))

    params["conv_w"] = _unpack(conv_parts, [(N_CHIPS, taps, tap_cols)])[0].transpose(1, 0, 2).reshape(taps, N_CHIPS * tap_cols)

    groups = (("w_down",), ("w_up",), ("w_co", "w_cq", "w_ckv", "w_out"), ("w_in",))
    scattered = []

    class GradStore(dict):
        pending = None

        def __setitem__(self, name, value):
            super().__setitem__(name, value)
            if "w_main" in self and "w_dt" in self and "w_in" not in self:
                gw_in = lax.dynamic_update_slice(self["w_main"], _unpad_heads(self["w_dt"]), (0, D_MAIN))
                self["w_in"] = _w_in_columns(gw_in, to_shards=True)
            for group in groups:
                if name in group and all(n in self for n in group):
                    self.settle()
                    pieces = [self[n].reshape(N_CHIPS, 2, shapes[n][0] // 2, shapes[n][1]) for n in group]
                    if group == groups[-1]:
                        self.scatter(group, pieces, _sibling_swap(pieces, "grad_swap_" + group[0]))
                    else:
                        landing = [lax.empty((N_CHIPS,) + a.shape[2:], BF16) for a in pieces]
                        sems, thru, self.token = _split_start("grad_swap_start_" + group[0], pieces + landing,
                                                              _swap_plan(len(pieces)), [len(pieces)])
                        self.pending = (group, sems[0], thru)

        def settle(self, *after):
            if self.pending is not None:
                group, sems, thru = self.pending
                self.pending = None
                thru = _split_wait("grad_swap_wait_" + group[0], thru, sems, _swap_plan(len(group)), *after)
                self.scatter(group, thru[:len(group)], thru[len(group):])

        def scatter(self, group, pieces, from_sibling):
            sums = [_add_halves(a, r, c_idx, "add_halves_" + n) for n, a, r in zip(group, pieces, from_sibling)]
            landing = [lax.empty((3,) + s.shape[1:], BF16) for s in sums]
            sems, thru, self.token = _split_start("grad_scatter_start_" + group[0], sums + landing,
                                                  _scatter_plan(len(sums)), [3 * len(sums)])
            scattered.append((group, sems[0], thru))

    loss, grad_x, grads = _local_step(x[0], mem[0], positions[0], loss_target[0], params, full, more_weights, GradStore(),
                                      h_in)

    out_g, out_d, out_m, out_v = {}, {}, {}, {}

    def finish(entries, order, token):
        halves = {}
        for group, sems, thru in entries:
            thru = _split_wait("grad_scatter_wait_" + group[0], thru, sems, _scatter_plan(len(group)), token)
            for i, n in enumerate(group):
                halves[n] = _sum_chips(thru[i], thru[len(group) + i], chip, "sum_chips_" + n)
        sources = [halves[n] for n in order]
        landing = [lax.empty(s.shape, F32) for s in sources]
        sems, thru, token = _split_start("grad_share_start_" + order[0], sources + landing, _share_plan(len(order)),
                                         [1] * len(order))
        for i, n in enumerate(order):
            own, other = _split_wait("grad_share_wait_" + n, [thru[i], thru[len(order) + i]], sems[i], _share_plan(1), token)
            if n == "w_in":
                res_t = _adamw_w_in_transposed(w_in_t, own, other, m_in_t, v_in_t, c_idx)
                out_g[n], out_d[n], out_m[n], out_v[n] = (t.T for t in res_t)
            else:
                out_g[n], out_d[n], out_m[n], out_v[n] = _adamw_halves(weights[n], own, other, mom_m[n], mom_v[n], c_idx,
                                                                       "adamw_" + n)
            token = out_v[n]
        return token

    token = finish(scattered[:-1], ("w_cq", "w_co", "w_ckv", "w_out", "w_up", "w_down"), grad_x)
    finish(scattered[-1:], ("w_in",), token)

    names = VECTORS + ("conv_w",)
    summed = _small_allreduce(_pack_rows([grads[n] for n in names] + [loss]), "allreduce_vectors")
    total_loss = summed[sum(_slot_rows(grads[n].size) for n in names), 0]
    small_out = _adamw_vectors(summed, chip, [(args[n], args["m_" + n], args["v_" + n]) for n in VECTORS],
                               (weights["conv_w"], mom_m["conv_w"], mom_v["conv_w"]))
    for n, res in zip(names, small_out):
        out_g[n], out_d[n], out_m[n], out_v[n] = (t.reshape(weights[n].shape) for t in res)

    outs =[total_loss, grad_x[None]]
    for group in (out_g, out_d, out_m, out_v):
        outs += [group[n][None] for n in WEIGHTS]
    return tuple(outs)
```

```python
import functools
import math

import jax
import jax.numpy as jnp
from jax import lax
from jax.experimental import pallas as pl
from jax.experimental.pallas import tpu as pltpu

F32 = jnp.float32
BF16 = jnp.bfloat16

SEQ = 2048
D_MODEL = 2048
HEAD = 64
D_ATTN = 1024
D_SSM = 1024
N_GROUPS = 4
N_STATE = 128
CHUNK = 128
ATT_BLK = 128
N_MEM = 256
D_CROSS = 512
D_MAIN = 6144
N_DT = 16
DT_PAD = 512
ROT = 16
ROPE_THETA = 500000.0
EPS = 1e-6
NEG = -1e30
BRANCH_BLOCKS = (16, 4, 1)
DILATIONS = (1, 4, 16)

ADAM_LR, ADAM_B1, ADAM_B2, ADAM_EPS, ADAM_WD, ADAM_STEP = 0.001, 0.9, 0.999, 1e-08, 0.01, 10

VMEM_LIMIT = 56 * 1024 * 1024
MESH = pl.DeviceIdType.MESH


def _params(sem, **kw):
    return pltpu.CompilerParams(dimension_semantics=sem, vmem_limit_bytes=VMEM_LIMIT, **kw)


def _bdot(a, b, dims):
    return lax.dot_general(a.astype(BF16), b.astype(BF16), (dims, ((), ())), preferred_element_type=F32)


def _fdot(a, b, dims):
    return lax.dot_general(a, b, (dims, ((), ())), preferred_element_type=F32, precision=lax.Precision.HIGHEST)


NN = ((1,), (0,))
NT = ((1,), (1,))
TN = ((0,), (0,))


def _tile(n, want):
    t = min(n, want)
    while n % t:
        t //= 2
    return t


def _matmul(a, b, *, mode, name, outs, extra=(), vecs=(), epilogue=None, col_shards=1, after=(), n_cols=None, out_cols=None,
            tile_rows=0, tile_sums=0, tm=1024, tn=1024, tk=2048):
    if mode == "nn":
        (m, k), n = a.shape, b.shape[1]
    elif mode == "nt":
        (m, k), n = a.shape, b.shape[0]
    else:
        (k, m), n = a.shape, b.shape[1]
    n = n if n_cols is None else n_cols
    tm, tn, tk = _tile(m, tm), _tile(n // col_shards, tn), _tile(k, tk)
    nk = k // tk
    per_shard = n // col_shards // tn
    dims = {"nn": NN, "nt": NT, "tn": TN}[mode]
    a_spec = pl.BlockSpec((tk, tm), lambda i, j, kk: (kk, i)) if mode == "tn" else pl.BlockSpec((tm, tk), lambda i, j, kk: (i, kk))
    b_spec = pl.BlockSpec((tn, tk), lambda i, j, kk: (j, kk)) if mode == "nt" else pl.BlockSpec((tk, tn), lambda i, j, kk: (kk, j))
    o_spec = pl.BlockSpec((tm, tn), lambda i, j, kk: (i, j))
    n_extra, n_out, n_after = len(extra) + len(vecs), len(outs), len(after)

    def body(a_ref, b_ref, *rest):
        extra_refs, out_refs, acc_ref = rest[:n_extra], rest[n_extra + n_after:-1], rest[-1]

        def finish(acc):
            res = (acc,) if epilogue is None else epilogue(acc, *[e[...] for e in extra_refs])
            for o_ref, r in zip(out_refs[:n_out], res):
                o_ref[...] = r.astype(o_ref.dtype)
            for o_ref, r in zip(out_refs[n_out:], res[n_out:]):
                o_ref[...] = jnp.broadcast_to(r, o_ref.shape)

        if nk == 1:
            finish(_bdot(a_ref[...], b_ref[...], dims))
            return
        kk = pl.program_id(2)

        @pl.when(kk == 0)
        def _():
            acc_ref[...] = jnp.zeros_like(acc_ref)

        acc_ref[...] += _bdot(a_ref[...], b_ref[...], dims)

        @pl.when(kk == nk - 1)
        def _():
            finish(acc_ref[...])

    if col_shards == 1:
        out_specs, out_dims = [o_spec] * n_out, (m, n if out_cols is None else out_cols)
    else:
        sharded = pl.BlockSpec((None, tm, tn), lambda i, j, kk: (j // per_shard, i, j % per_shard))
        out_specs, out_dims = [sharded] * n_out, (col_shards, m, n // col_shards)
    res = pl.pallas_call(
        body, name=name, grid=(m // tm, n // tn, nk),
        in_specs=[a_spec, b_spec] + [o_spec] * len(extra) + [pl.BlockSpec((1, tn), lambda i, j, kk: (0, j))] * len(vecs)
        + [pl.BlockSpec(memory_space=pl.ANY)] * n_after,
        out_specs=out_specs + [pl.BlockSpec((8, tn), lambda i, j, kk: (i, j))] * tile_rows
        + [pl.BlockSpec((8, 128), lambda i, j, kk: (i, j))] * tile_sums,
        out_shape=[jax.ShapeDtypeStruct(out_dims, dt) for dt in outs] + [jax.ShapeDtypeStruct((m // tm * 8, n), F32)] * tile_rows
        + [jax.ShapeDtypeStruct((m // tm * 8, n // tn * 128), F32)] * tile_sums,
        scratch_shapes=[pltpu.VMEM((tm, tn) if nk > 1 else (8, 128), F32)],
        compiler_params=_params(("parallel", "parallel", "arbitrary")),
    )(a, b, *extra, *vecs, *after)
    res = (list(res[:n_out]) + [jnp.sum(t[::8], axis=0, keepdims=True) for t in res[n_out:n_out + tile_rows]]
           + [t[::8, ::128] for t in res[n_out + tile_rows:]])
    return res[0] if len(res) == 1 else res


def _row_spec(tr, bw, cb, per_group):
    return pl.BlockSpec((tr, bw), (lambda g, i: (i, cb + g)) if per_group else (lambda g, i: (i, cb)))


def _vec_spec(bw, cb, per_group):
    return pl.BlockSpec((1, bw), (lambda g, i: (0, cb + g)) if per_group else (lambda g, i: (0, cb)))


def _rowwise(fn, rows, vecs, outs, *, name, n_rows=SEQ, tr=512, groups=1, after=()):
    n_r, n_v, n_after = len(rows), len(vecs), len(after)

    def body(*refs):
        vals = [r[...].astype(F32) for r in refs[:n_r + n_v]]
        res = fn(*vals)
        for o_ref, r in zip(refs[n_r + n_v + n_after:], res):
            o_ref[...] = r.astype(o_ref.dtype)

    res = pl.pallas_call(
        body, name=name, grid=(groups, n_rows // tr),
        in_specs=[_row_spec(tr, bw, cb, pg) for _, bw, cb, pg in rows] + [_vec_spec(bw, cb, pg) for _, bw, cb, pg in vecs]
        + [pl.BlockSpec(memory_space=pl.ANY)] * n_after,
        out_specs=[_row_spec(tr, bw, cb, pg) for _, _, bw, cb, pg in outs],
        out_shape=[jax.ShapeDtypeStruct((n_rows, w), dt) for w, dt, _, _, _ in outs],
        compiler_params=_params(("parallel", "parallel")),
    )(*[r[0] for r in rows], *[v[0] for v in vecs], *after)
    return res


def _rowwise_vjp(fn, rows, vecs, cts, row_grads, vec_grads, *, name, n_rows=SEQ, tr=512, groups=1, after=()):
    n_r, n_v, n_after = len(rows), len(vecs), len(after)
    ct_ops = [op for group in cts for op in group]
    ct_sizes = [len(group) for group in cts]
    res_ops = [g[6] for g in row_grads if g[6] is not None]
    n_ct, n_res, n_rg = len(ct_ops), len(res_ops), len(row_grads)

    def body(*refs):
        vals = [r[...].astype(F32) for r in refs[:n_r + n_v]]
        pos = n_r + n_v
        ct_vals = []
        for size in ct_sizes:
            acc = refs[pos][...].astype(F32)
            for t in range(1, size):
                acc = acc + refs[pos + t][...].astype(F32)
            ct_vals.append(acc)
            pos += size
        res_refs = refs[pos:pos + n_res]
        out_refs = refs[pos + n_res + n_after:]
        _, pullback = jax.vjp(fn, *vals)
        grads = pullback(tuple(ct_vals))
        r_i = 0
        for o_ref, g in zip(out_refs[:n_rg], row_grads):
            val = grads[g[0]]
            if g[6] is not None:
                val = val + res_refs[r_i][...].astype(F32)
                r_i += 1
            o_ref[...] = val.astype(o_ref.dtype)
        first = (pl.program_id(1) == 0)
        for o_ref, g in zip(out_refs[n_rg:], vec_grads):
            val = jnp.sum(grads[n_r + g[0]], axis=0, keepdims=True)
            init = first if g[4] else jnp.logical_and(first, pl.program_id(0) == 0)

            @pl.when(init)
            def _(o_ref=o_ref, val=val):
                o_ref[...] = val

            @pl.when(jnp.logical_not(init))
            def _(o_ref=o_ref, val=val):
                o_ref[...] += val

    in_specs = [_row_spec(tr, bw, cb, pg) for _, bw, cb, pg in rows] + [_vec_spec(bw, cb, pg) for _, bw, cb, pg in vecs]
    in_specs += [_row_spec(tr, bw, cb, pg) for _, bw, cb, pg in ct_ops + res_ops] + [pl.BlockSpec(memory_space=pl.ANY)] * n_after
    out_specs =[_row_spec(tr, g[3], g[4], g[5]) for g in row_grads] + [_vec_spec(g[2], g[3], g[4]) for g in vec_grads]
    out_shape = [jax.ShapeDtypeStruct((n_rows, g[1]), g[2]) for g in row_grads]
    out_shape += [jax.ShapeDtypeStruct((1, g[1]), F32) for g in vec_grads]
    return pl.pallas_call(
        body, name=name, grid=(groups, n_rows // tr),
        in_specs=in_specs, out_specs=out_specs, out_shape=out_shape,
        compiler_params=_params(("arbitrary", "arbitrary")),
    )(*[r[0] for r in rows], *[v[0] for v in vecs], *[c[0] for c in ct_ops], *[r[0] for r in res_ops], *after)


def _full(arr, width=None):
    return (arr, arr.shape[1] if width is None else width, 0, False)


def _make_xor(sh):
    def raw(x):
        n = x.shape[-1]
        lane = lax.broadcasted_iota(jnp.int32, x.shape, x.ndim - 1)
        up = pltpu.roll(x, n - sh, x.ndim - 1)
        down = pltpu.roll(x, sh, x.ndim - 1)
        return jnp.where((lane & sh) == 0, up, down)

    f = jax.custom_vjp(raw)
    f.defvjp(lambda x: (raw(x), None), lambda _, ct: (raw(ct),))
    return f


_SWAP_ROPE_HALVES = _make_xor(ROT // 2)


def _head_sum(x):
    n = x.shape[-1]
    same_head = (lax.broadcasted_iota(jnp.int32, (n, n), 0) // HEAD) == (lax.broadcasted_iota(jnp.int32, (n, n), 1) // HEAD)
    return _fdot(x, same_head.astype(F32), NN)


def _rms(x, g):
    return x * lax.rsqrt(jnp.mean(x * x, axis=-1, keepdims=True) + EPS) * g


def _head_rms_rope(x, g, cos, sin, scale):
    y = x * lax.rsqrt(_head_sum(x * x) * (1.0 / HEAD) + EPS) * g
    return (y * cos + _SWAP_ROPE_HALVES(y) * sin) * scale


def _qk_fn(q, k, v, cos, sin, gq, gk):
    return (_head_rms_rope(q, gq, cos, sin, HEAD ** -0.5), _head_rms_rope(k, gk, cos, sin, 1.0), v)


def _norm_fn(x, g):
    return (_rms(x, g),)


def _merge_fn(o0, o1, o2, l0, l1, l2, g):
    m = lax.stop_gradient(jnp.maximum(jnp.maximum(l0, l1), l2))
    e0, e1, e2 = jnp.exp(l0 - m), jnp.exp(l1 - m), jnp.exp(l2 - m)
    mix = (e0 * o0 + e1 * o1 + e2 * o2) / (e0 + e1 + e2)
    return (_rms(mix, g),)


def _gate_fn(y, z, g):
    return (_rms(y * (z * jax.nn.sigmoid(z)), g),)


def _attn_pair(q, kc, vc, kp=None, vp=None, has_prev=None):
    pick0, pick1 = _head_picks()
    k_band, v_band, mask = _attn_band(kc, vc, kp, vp, has_prev)
    s = jnp.where(mask, _bdot(jnp.concatenate([q * pick0, q * pick1], axis=0), k_band, NT), NEG)
    m = jnp.max(s, axis=-1, keepdims=True)
    p = jnp.exp(s - m)
    den = jnp.sum(p, axis=-1, keepdims=True)
    acc = _bdot(p, v_band, NN) * (1.0 / den)
    lse_rows = m + jnp.log(den)
    o = pick0 * acc[:ATT_BLK] + pick1 * acc[ATT_BLK:]
    lse = pick0 * lse_rows[:ATT_BLK] + pick1 * lse_rows[ATT_BLK:]
    return o, lse


def _head_picks():
    lane = lax.broadcasted_iota(jnp.int32, (1, 2 * HEAD), 1)
    return (lane < HEAD).astype(F32), (lane >= HEAD).astype(F32)


def _attn_band(kc, vc, kp, vp, has_prev):
    n_keys = ATT_BLK if kp is None else 2 * ATT_BLK
    qi = lax.broadcasted_iota(jnp.int32, (2 * ATT_BLK, n_keys), 0) & (ATT_BLK - 1)
    kj = lax.broadcasted_iota(jnp.int32, (2 * ATT_BLK, n_keys), 1)
    if kp is None:
        return kc, vc, qi >= kj
    in_prev = jnp.logical_and(jnp.logical_and(kj < ATT_BLK, kj >= qi), has_prev)
    mask = jnp.logical_or(in_prev, jnp.logical_and(kj >= ATT_BLK, qi >= kj - ATT_BLK))
    return jnp.concatenate([kp, kc], axis=0), jnp.concatenate([vp, vc], axis=0), mask


def _attn_config(b):
    r = DILATIONS[b]
    return r, ATT_BLK * r, (D_ATTN if r == 1 else 128), BRANCH_BLOCKS[b] > 1


RESIDUES_UNROLLED = 16


def _for_residues(r, fn):
    if r <= RESIDUES_UNROLLED:
        for rho in range(r):
            fn(rho)
    else:
        def step(t, carry):
            for u in range(RESIDUES_UNROLLED):
                fn(RESIDUES_UNROLLED * t + u)
            return carry

        lax.fori_loop(0, r // RESIDUES_UNROLLED, step, 0)


def _strided_rows(start, r):
    if r > 1:
        return pl.ds(start, ATT_BLK, stride=r)
    return pl.ds(start if isinstance(start, int) else pl.multiple_of(start, ATT_BLK), ATT_BLK)


def _attention_fwd(qn, kn, vn, b):
    r, rows, lanes, with_prev = _attn_config(b)
    cur = pl.BlockSpec((rows, lanes), lambda g, n: (n, g))
    prev = pl.BlockSpec((rows, lanes), lambda g, n: (jnp.maximum(n - 1, 0), g))

    def body(*refs):
        ins, (o_ref, l_ref) = refs[:-2], refs[-2:]
        has_prev = pl.program_id(1) > 0

        def one(rho):
            sub = _strided_rows(rho, r)
            for pair in range(lanes // 128):
                sl = pl.ds(pair * 128, 128)
                args = [ref[sub, sl] for ref in ins] + ([has_prev] if with_prev else [])
                o_ref[sub, sl], l_ref[sub, sl] = _attn_pair(*args)

        _for_residues(r, one)

    operands = (qn, kn, vn, kn, vn) if with_prev else (qn, kn, vn)
    return pl.pallas_call(
        body, name="attn_fwd_%d" % r, grid=(D_ATTN // lanes, SEQ // rows),
        in_specs=[cur, cur, cur] + ([prev, prev] if with_prev else []), out_specs=[cur, cur],
        out_shape=[jax.ShapeDtypeStruct((SEQ, D_ATTN), F32)] * 2,
        compiler_params=_params(("parallel", "parallel")),
    )(*operands)


def _attn_pair_bwd(q, kc, vc, kp, vp, o, lse, do, dl, has_prev):
    pick0, pick1 = _head_picks()
    lane = lax.broadcasted_iota(jnp.int32, (1, 2 * HEAD), 1)
    k_band, v_band, mask = _attn_band(kc, vc, kp, vp, has_prev)
    q2 = jnp.concatenate([q * pick0, q * pick1], axis=0)
    do2 = jnp.concatenate([do * pick0, do * pick1], axis=0)
    lse2 = jnp.concatenate([jnp.sum(lse * (lane == 0).astype(F32), axis=-1, keepdims=True),
                            jnp.sum(lse * (lane == HEAD).astype(F32), axis=-1, keepdims=True)], axis=0)
    base = jnp.sum(jnp.concatenate([dl * pick0, dl * pick1], axis=0) - do2 * jnp.concatenate([o, o], axis=0),
                   axis=-1, keepdims=True)
    p = jnp.exp(jnp.where(mask, _bdot(q2, k_band, NT), NEG) - lse2)
    ds = p * (_bdot(do2, v_band, NT) + base)
    dq2 = _bdot(ds, k_band, NN)
    dq = pick0 * dq2[:ATT_BLK] + pick1 * dq2[ATT_BLK:]
    dk, dv = _bdot(ds, q2, TN), _bdot(p, do2, TN)
    if kp is None:
        return dq, dk, dv
    return dq, dk[ATT_BLK:], dv[ATT_BLK:], dk[:ATT_BLK], dv[:ATT_BLK]


def _attention_bwd(qn, kn, vn, o, lse, do, dl, b):
    r, rows, lanes, with_prev = _attn_config(b)
    cur = pl.BlockSpec((rows, lanes), lambda g, n: (n, g))
    prev = pl.BlockSpec((rows, lanes), lambda g, n: (jnp.maximum(n - 1, 0), g))
    whole = pl.BlockSpec((SEQ, lanes), lambda g, n: (0, g))
    n_in = 5 if with_prev else 3

    def body(*refs):
        ins, (o_ref, l_ref, do_ref, dl_ref, dq_ref, dk_ref, dv_ref) = refs[:n_in], refs[n_in:]
        n = pl.program_id(1)

        @pl.when(n == 0)
        def _():
            dk_ref[...] = jnp.zeros_like(dk_ref)
            dv_ref[...] = jnp.zeros_like(dv_ref)

        def one(rho):
            sub = _strided_rows(rho, r)
            sub_c = _strided_rows(n * rows + rho, r)
            sub_p = _strided_rows(jnp.maximum(n - 1, 0) * rows + rho, r)
            for pair in range(lanes // 128):
                sl = pl.ds(pair * 128, 128)
                vals = [ref[sub, sl] for ref in ins] + ([] if with_prev else [None, None])
                grads = _attn_pair_bwd(*vals, o_ref[sub, sl], l_ref[sub, sl], do_ref[sub, sl], dl_ref[sub, sl], n > 0)
                dq_ref[sub, sl] = grads[0]
                dk_ref[sub_c, sl] += grads[1]
                dv_ref[sub_c, sl] += grads[2]
                if with_prev:
                    dk_ref[sub_p, sl] += grads[3]
                    dv_ref[sub_p, sl] += grads[4]

        _for_residues(r, one)

    operands = (qn, kn, vn, kn, vn) if with_prev else (qn, kn, vn)
    return pl.pallas_call(
        body, name="attn_bwd_%d" % r, grid=(D_ATTN // lanes, SEQ // rows),
        in_specs=[cur, cur, cur] + ([prev, prev] if with_prev else []) + [cur] * 4, out_specs=[cur, whole, whole],
        out_shape=[jax.ShapeDtypeStruct((SEQ, D_ATTN), F32)] * 3,
        compiler_params=_params(("parallel", "arbitrary")),
    )(*operands, o, lse, do, dl)


CONV_COLS = 256
XBC_BLOCK0 = (3 * D_ATTN + D_SSM) // CONV_COLS


def _shift_rows(x, s):
    n = x.shape[0]
    t = lax.broadcasted_iota(jnp.int32, x.shape, 0)
    if s >= 0:
        return jnp.where(t >= s, pltpu.roll(x, s, 0), 0.0)
    return jnp.where(t < n + s, pltpu.roll(x, n + s, 0), 0.0)


def _conv_pre(x, w_ref, b_ref):
    delayed = [_shift_rows(x, 3 - k) for k in range(3)]
    pre = b_ref[...] + w_ref[3:4, :] * x
    for k in range(3):
        pre = pre + w_ref[k:k + 1, :] * delayed[k]
    return pre, delayed


def _conv_fwd(proj, conv_w, conv_b):
    cols = conv_w.shape[1]

    def body(x_ref, w_ref, b_ref, o_ref):
        pre, _ = _conv_pre(x_ref[...], w_ref, b_ref)
        o_ref[...] = pre * jax.nn.sigmoid(pre)

    blk = pl.BlockSpec((SEQ, CONV_COLS), lambda j: (0, j))
    return pl.pallas_call(
        body, name="conv_fwd", grid=(cols // CONV_COLS,),
        in_specs=[pl.BlockSpec((SEQ, CONV_COLS), lambda j: (0, XBC_BLOCK0 + j)),
                  pl.BlockSpec((4, CONV_COLS), lambda j: (0, j)), pl.BlockSpec((1, CONV_COLS), lambda j: (0, j))],
        out_specs=blk, out_shape=jax.ShapeDtypeStruct((SEQ, cols), F32),
        compiler_params=_params(("parallel",)),
    )(proj, conv_w, conv_b)


def _conv_bwd(proj, conv_w, conv_b, dxs, db, dc):
    cols = conv_w.shape[1]
    x_blocks, b_blocks = dxs.shape[1] // CONV_COLS, db.shape[1] // CONV_COLS

    def body(x_ref, w_ref, b_ref, dxs_ref, db_ref_in, dc_ref_in, dx_ref, dw_ref, db_ref):
        j = pl.program_id(0)
        dy = jnp.where(j < x_blocks, dxs_ref[...], jnp.where(j < x_blocks + b_blocks, db_ref_in[...], dc_ref_in[...]))
        x = x_ref[...]
        pre, delayed = _conv_pre(x, w_ref, b_ref)
        sg = jax.nn.sigmoid(pre)
        dpre = dy * (sg * (1.0 + pre * (1.0 - sg)))
        db_ref[...] = jnp.sum(dpre, axis=0, keepdims=True)
        dx = w_ref[3:4, :] * dpre
        dw_ref[3:4, :] = jnp.sum(dpre * x, axis=0, keepdims=True)
        for k in range(3):
            dx = dx + w_ref[k:k + 1, :] * _shift_rows(dpre, k - 3)
            dw_ref[k:k + 1, :] = jnp.sum(dpre * delayed[k], axis=0, keepdims=True)
        dw_ref[4:8, :] = jnp.zeros((4, CONV_COLS), F32)
        dx_ref[...] = dx.astype(dx_ref.dtype)

    blk = pl.BlockSpec((SEQ, CONV_COLS), lambda j: (0, j))
    parts = [pl.BlockSpec((SEQ, CONV_COLS), lambda j: (0, jnp.minimum(j, x_blocks - 1))),
             pl.BlockSpec((SEQ, CONV_COLS), lambda j: (0, jnp.clip(j - x_blocks, 0, b_blocks - 1))),
             pl.BlockSpec((SEQ, CONV_COLS), lambda j: (0, jnp.clip(j - x_blocks - b_blocks, 0, b_blocks - 1)))]
    return pl.pallas_call(
        body, name="conv_bwd", grid=(cols // CONV_COLS,),
        in_specs=[pl.BlockSpec((SEQ, CONV_COLS), lambda j: (0, XBC_BLOCK0 + j)),
                  pl.BlockSpec((4, CONV_COLS), lambda j: (0, j)), pl.BlockSpec((1, CONV_COLS), lambda j: (0, j))] + parts,
        out_specs=[blk, pl.BlockSpec((8, CONV_COLS), lambda j: (0, j)), pl.BlockSpec((1, CONV_COLS), lambda j: (0, j))],
        out_shape=[jax.ShapeDtypeStruct((SEQ, cols), BF16), jax.ShapeDtypeStruct((8, cols), F32),
                   jax.ShapeDtypeStruct((1, cols), F32)],
        compiler_params=_params(("parallel",)),
    )(proj, conv_w, conv_b, dxs, db, dc)


HEADS_PER_GROUP = 4


GROUP_WIDTH = HEADS_PER_GROUP * HEAD


def _ssd_chunk(x, bm, cm, dtr, bias, alog, dsk, h):
    row = lax.broadcasted_iota(jnp.int32, (CHUNK, CHUNK), 0)
    col = lax.broadcasted_iota(jnp.int32, (CHUNK, CHUNK), 1)
    causal = row >= col
    z = dtr + bias
    dt = jnp.maximum(z, 0.0) + jnp.log(1.0 + jnp.exp(-jnp.abs(z)))
    acs = _fdot(causal.astype(F32), dt * -jnp.exp(alog), NN)
    acs_t, dt_t = acs.T, dt.T
    cb = _bdot(cm, bm, NT)
    lane = lax.broadcasted_iota(jnp.int32, (1, CHUNK), 1)
    sub = lax.broadcasted_iota(jnp.int32, (CHUNK, 1), 0)
    wide = lax.broadcasted_iota(jnp.int32, (1, GROUP_WIDTH), 1) // HEAD
    tall = lax.broadcasted_iota(jnp.int32, (GROUP_WIDTH, 1), 0) // HEAD
    acs_last = jnp.sum(acs * (sub == CHUNK - 1).astype(F32), axis=0, keepdims=True)
    to_lanes = (lax.broadcasted_iota(jnp.int32, (CHUNK, GROUP_WIDTH), 0)
                == lax.broadcasted_iota(jnp.int32, (CHUNK, GROUP_WIDTH), 1) // HEAD).astype(F32)
    grow = _fdot(jnp.exp(acs), to_lanes, NN)
    keep = _fdot(jnp.exp(acs_last - acs) * dt, to_lanes, NN)
    w_parts, x_parts, skip, carry = [], [], 0.0, 0.0
    for j in range(HEADS_PER_GROUP):
        on_lane, on_sub = (lane == j).astype(F32), (sub == j).astype(F32)
        acs_c = jnp.sum(acs * on_lane, axis=1, keepdims=True)
        acs_r = jnp.sum(acs_t * on_sub, axis=0, keepdims=True)
        dt_r = jnp.sum(dt_t * on_sub, axis=0, keepdims=True)
        w_parts.append(cb * jnp.exp(jnp.where(causal, acs_c - acs_r, NEG)) * dt_r)
        x_parts.append(x * (wide == j).astype(F32))
        skip = skip + jnp.sum(dsk * on_lane, axis=1, keepdims=True) * (wide == j).astype(F32)
        carry = carry + jnp.sum(jnp.exp(acs_last) * on_lane, axis=1, keepdims=True) * (tall == j).astype(F32)
    y_diag = _bdot(jnp.concatenate(w_parts, axis=1), jnp.concatenate(x_parts, axis=0), NN)
    y = y_diag + _bdot(cm, h, NT) * grow + skip * x
    return y, h * carry + _bdot(x * keep, bm, TN)


GROUPS_PER_STEP = 4
SSD_STEPS = N_GROUPS // GROUPS_PER_STEP


def _ssd_specs(reverse):
    n_chunks = SEQ // CHUNK
    c_of = (lambda c: n_chunks - 1 - c) if reverse else (lambda c: c)
    x_w, n_w, dt_w = GROUPS_PER_STEP * GROUP_WIDTH, GROUPS_PER_STEP * N_STATE, GROUPS_PER_STEP * 128
    x_spec = pl.BlockSpec((CHUNK, x_w), lambda g, c: (c_of(c), g))
    b_spec = pl.BlockSpec((CHUNK, n_w), lambda g, c: (c_of(c), D_SSM // n_w + g))
    c_spec = pl.BlockSpec((CHUNK, n_w), lambda g, c: (c_of(c), (D_SSM + N_GROUPS * N_STATE) // n_w + g))
    dt_spec = pl.BlockSpec((CHUNK, dt_w), lambda g, c: (c_of(c), g))
    vec_spec = pl.BlockSpec((1, dt_w), lambda g, c: (0, g))
    h_spec = pl.BlockSpec((None, GROUPS_PER_STEP, GROUP_WIDTH, N_STATE), lambda g, c: (c_of(c), g, 0, 0))
    return x_spec, b_spec, c_spec, dt_spec, vec_spec, h_spec


def _group_slices(u):
    return pl.ds(u * GROUP_WIDTH, GROUP_WIDTH), pl.ds(u * N_STATE, N_STATE), pl.ds(u * 128, 128)


def _ssd_gated_chunk(x, bm, cm, dtr, bias, alog, dsk, h, z, g_out):
    y, h_new = _ssd_chunk(x, bm, cm, dtr, bias, alog, dsk, h)
    return _gate_fn(y, z, g_out)[0], h_new


def _ssd_gate_specs(reverse):
    x_spec = _ssd_specs(reverse)[0]
    z_block0 = 3 * D_ATTN // x_spec.block_shape[1]
    z_spec = pl.BlockSpec(x_spec.block_shape, lambda g, c: (x_spec.index_map(g, c)[0], z_block0 + g))
    return z_spec, pl.BlockSpec((1, x_spec.block_shape[1]), lambda g, c: (0, g))


def _ssd_fwd(xbc, dt_raw, bias, alog, dsk, proj, g_out):
    x_spec, b_spec, c_spec, dt_spec, vec_spec, h_spec = _ssd_specs(False)
    z_spec, g_spec = _ssd_gate_specs(False)

    def body(x_ref, b_ref, c_ref, dt_ref, bias_ref, alog_ref, dsk_ref, z_ref, g_ref, ssm_ref, hin_ref, h_scr):
        @pl.when(pl.program_id(1) == 0)
        def _():
            h_scr[...] = jnp.zeros_like(h_scr)

        for u in range(GROUPS_PER_STEP):
            xs, ns, ds = _group_slices(u)
            h = h_scr[u]
            hin_ref[u] = h
            ssm, h_scr[u] = _ssd_gated_chunk(x_ref[:, xs], b_ref[:, ns], c_ref[:, ns], dt_ref[:, ds], bias_ref[:, ds],
                                             alog_ref[:, ds], dsk_ref[:, ds], h, z_ref[:, xs], g_ref[:, xs])
            ssm_ref[:, xs] = ssm.astype(ssm_ref.dtype)

    return pl.pallas_call(
        body, name="ssd_fwd", grid=(SSD_STEPS, SEQ // CHUNK),
        in_specs=[x_spec, b_spec, c_spec, dt_spec, vec_spec, vec_spec, vec_spec, z_spec, g_spec],
        out_specs=[x_spec, h_spec],
        out_shape=[jax.ShapeDtypeStruct((SEQ, D_SSM), BF16),
                   jax.ShapeDtypeStruct((SEQ // CHUNK, N_GROUPS, GROUP_WIDTH, N_STATE), F32)],
        scratch_shapes=[pltpu.VMEM((GROUPS_PER_STEP, GROUP_WIDTH, N_STATE), F32)],
        compiler_params=_params(("parallel", "arbitrary")),
    )(xbc, xbc, xbc, dt_raw, bias, alog, dsk, proj, g_out)


def _ssd_bwd(xbc, dt_raw, bias, alog, dsk, h_in, proj, g_out, dmix):
    x_spec, b_spec, c_spec, dt_spec, vec_spec, h_spec = _ssd_specs(True)
    z_spec, g_spec = _ssd_gate_specs(True)
    ct_block0 = D_ATTN // x_spec.block_shape[1]
    ct_spec = pl.BlockSpec(x_spec.block_shape, lambda g, c: (x_spec.index_map(g, c)[0], ct_block0 + g))

    def body(x_ref, b_ref, c_ref, dt_ref, bias_ref, alog_ref, dsk_ref, hin_ref, z_ref, g_ref, ct_ref,
             dx_ref, db_ref, dc_ref, ddt_ref, dbias_ref, dalog_ref, ddsk_ref, dz_ref, dg_ref, dh_scr):
        first = pl.program_id(1) == 0

        @pl.when(first)
        def _():
            dh_scr[...] = jnp.zeros_like(dh_scr)

        for u in range(GROUPS_PER_STEP):
            xs, ns, ds = _group_slices(u)
            _, pullback = jax.vjp(_ssd_gated_chunk, x_ref[:, xs], b_ref[:, ns], c_ref[:, ns], dt_ref[:, ds], bias_ref[:, ds],
                                  alog_ref[:, ds], dsk_ref[:, ds], hin_ref[u], z_ref[:, xs], g_ref[:, xs])
            g = pullback((ct_ref[:, xs], dh_scr[u]))
            dx_ref[:, xs], db_ref[:, ns], dc_ref[:, ns] = g[0], g[1], g[2]
            ddt_ref[:, ds] = g[3].astype(ddt_ref.dtype)
            dh_scr[u] = g[7]
            dz_ref[:, xs] = g[8].astype(dz_ref.dtype)
            sums = ((dbias_ref, g[4], ds), (dalog_ref, g[5], ds), (ddsk_ref, g[6], ds),
                    (dg_ref, jnp.sum(g[9], axis=0, keepdims=True), xs))
            for o_ref, val, lanes in sums:
                @pl.when(first)
                def _(o_ref=o_ref, val=val, lanes=lanes):
                    o_ref[:, lanes] = val

                @pl.when(jnp.logical_not(first))
                def _(o_ref=o_ref, val=val, lanes=lanes):
                    o_ref[:, lanes] += val

    n_chunks = SEQ // CHUNK
    out_b = pl.BlockSpec((CHUNK, GROUPS_PER_STEP * N_STATE), lambda g, c: (n_chunks - 1 - c, g))
    return pl.pallas_call(
        body, name="ssd_bwd", grid=(SSD_STEPS, n_chunks),
        in_specs=[x_spec, b_spec, c_spec, dt_spec, vec_spec, vec_spec, vec_spec, h_spec, z_spec, g_spec, ct_spec],
        out_specs=[x_spec, out_b, out_b, dt_spec, vec_spec, vec_spec, vec_spec, x_spec, g_spec],
        out_shape=[jax.ShapeDtypeStruct((SEQ, D_SSM), F32), jax.ShapeDtypeStruct((SEQ, N_GROUPS * N_STATE), F32),
                   jax.ShapeDtypeStruct((SEQ, N_GROUPS * N_STATE), F32), jax.ShapeDtypeStruct((SEQ, DT_PAD), BF16),
                   jax.ShapeDtypeStruct((1, DT_PAD), F32), jax.ShapeDtypeStruct((1, DT_PAD), F32),
                   jax.ShapeDtypeStruct((1, DT_PAD), F32), jax.ShapeDtypeStruct((SEQ, D_SSM), BF16),
                   jax.ShapeDtypeStruct((1, D_SSM), F32)],
        scratch_shapes=[pltpu.VMEM((GROUPS_PER_STEP, GROUP_WIDTH, N_STATE), F32)],
        compiler_params=_params(("parallel", "arbitrary")),
    )(xbc, xbc, xbc, dt_raw, bias, alog, dsk, h_in, proj, g_out, dmix)


CROSS_HEAD = 128
CROSS_ROWS = 1024


def _cross_head(q, k, v, gq, gk):
    qn = _rms(q, gq) * (CROSS_HEAD ** -0.5)
    kn = _rms(k, gk)
    s = _bdot(qn, kn, NT)
    p = jnp.exp(s - lax.stop_gradient(jnp.max(s, axis=-1, keepdims=True)))
    return _bdot(p, v, NN) * (1.0 / jnp.sum(p, axis=-1, keepdims=True))


def _cross_specs():
    q_spec = pl.BlockSpec((CROSS_ROWS, CROSS_HEAD), lambda h, i: (i, h))
    k_spec = pl.BlockSpec((N_MEM, CROSS_HEAD), lambda h, i: (0, h))
    v_spec = pl.BlockSpec((N_MEM, CROSS_HEAD), lambda h, i: (0, 4 + h))
    g_spec = pl.BlockSpec((1, CROSS_HEAD), lambda h, i: (0, 0))
    return q_spec, k_spec, v_spec, g_spec


def _cross_fwd(qc, kv, gq, gk):
    q_spec, k_spec, v_spec, g_spec = _cross_specs()

    def body(q_ref, k_ref, v_ref, gq_ref, gk_ref, o_ref):
        o_ref[...] = _cross_head(q_ref[...], k_ref[...], v_ref[...], gq_ref[...], gk_ref[...]).astype(o_ref.dtype)

    return pl.pallas_call(
        body, name="cross_fwd", grid=(4, SEQ // CROSS_ROWS),
        in_specs=[q_spec, k_spec, v_spec, g_spec, g_spec], out_specs=q_spec,
        out_shape=jax.ShapeDtypeStruct((SEQ, D_CROSS), BF16),
        compiler_params=_params(("parallel", "parallel")),
    )(qc, kv, kv, gq, gk)


def _cross_bwd(qc, kv, gq, gk, do):
    q_spec, k_spec, v_spec, g_spec = _cross_specs()

    def body(q_ref, k_ref, v_ref, gq_ref, gk_ref, do_ref, dq_ref, dk_ref, dv_ref, dgq_ref, dgk_ref):
        _, pullback = jax.vjp(_cross_head, q_ref[...], k_ref[...], v_ref[...], gq_ref[...], gk_ref[...])
        dq, dk, dv, dgq, dgk = pullback(do_ref[...].astype(F32))
        dq_ref[...] = dq.astype(dq_ref.dtype)
        row0 = pl.program_id(1) == 0
        all0 = jnp.logical_and(row0, pl.program_id(0) == 0)
        for o_ref, val, init in ((dk_ref, dk, row0), (dv_ref, dv, row0), (dgq_ref, dgq, all0), (dgk_ref, dgk, all0)):
            @pl.when(init)
            def _(o_ref=o_ref, val=val):
                o_ref[...] = val

            @pl.when(jnp.logical_not(init))
            def _(o_ref=o_ref, val=val):
                o_ref[...] += val

    return pl.pallas_call(
        body, name="cross_bwd", grid=(4, SEQ // CROSS_ROWS),
        in_specs=[q_spec, k_spec, v_spec, g_spec, g_spec, q_spec],
        out_specs=[q_spec, k_spec, k_spec, g_spec, g_spec],
        out_shape=[jax.ShapeDtypeStruct((SEQ, D_CROSS), BF16), jax.ShapeDtypeStruct((N_MEM, D_CROSS), F32),
                   jax.ShapeDtypeStruct((N_MEM, D_CROSS), F32), jax.ShapeDtypeStruct((1, CROSS_HEAD), F32),
                   jax.ShapeDtypeStruct((1, CROSS_HEAD), F32)],
        compiler_params=_params(("arbitrary", "arbitrary")),
    )(qc, kv, kv, gq, gk, do)


def _loss_epilogue(acc, residual, target):
    err = acc + residual - target
    dy = err * (1.0 / D_MODEL)
    part = jnp.sum(jnp.sum(err * err, axis=1, keepdims=True), axis=0, keepdims=True) * (0.5 / D_MODEL)
    return dy, dy, part


def _pad_heads(v):
    return jnp.pad(v.reshape(N_GROUPS, HEADS_PER_GROUP), ((0, 0), (0, 128 - HEADS_PER_GROUP))).reshape(1, DT_PAD)


def _unpad_heads(v):
    return v.reshape(v.shape[0], N_GROUPS, 128)[:, :, :HEADS_PER_GROUP].reshape(v.shape[0], N_DT)


def _rope_tables(positions):
    half = ROT // 2
    inv_freq = ROPE_THETA ** (-2.0 * jnp.arange(half, dtype=F32) / ROT)
    ang = positions.reshape(SEQ, 1).astype(F32) * inv_freq
    cos, sin = jnp.cos(ang), jnp.sin(ang)
    ones, zeros = jnp.ones((SEQ, HEAD - ROT), F32), jnp.zeros((SEQ, HEAD - ROT), F32)
    cos_h = jnp.concatenate([cos, cos, ones], axis=1)
    sin_h = jnp.concatenate([-sin, sin, zeros], axis=1)
    return jnp.tile(cos_h, (1, 2)), jnp.tile(sin_h, (1, 2))


def _add_res(acc, res):
    return (acc + res,)


def _norm_bwd_epilogue(acc, x, residual, *more):
    *part, g = more
    ct = acc + part[0] if part else acc
    _, pullback = jax.vjp(_rms, x, g)
    dx, dg = pullback(ct)
    return dx + residual, dg


def _add_res_and_norm(acc, res, g):
    y = acc + res
    return y, _rms(y, g)


def _settle(grads, *after):
    if hasattr(grads, "settle"):
        grads.settle(*after)


def _take_token(grads):
    token = getattr(grads, "token", None)
    if token is None:
        return ()
    grads.token = None
    return (token,)


def _local_step(x, mem, positions, target, p, w, more_weights=None, grads=None, h=None):
    grads = {} if grads is None else grads
    w = dict(w)
    cos, sin = _rope_tables(positions)
    gq2, gk2 = jnp.tile(p["g_q"], (1, 2)), jnp.tile(p["g_k"], (1, 2))
    bias, alog, dsk = _pad_heads(p["dt_bias"]), _pad_heads(p["a_log"]), _pad_heads(p["d_skip"])
    norm_out = [(D_MODEL, BF16, D_MODEL, 0, False)]

    if h is None:
        h = _rowwise(_norm_fn, [_full(x)], [_full(p["g_mix"])], norm_out, name="norm_in")[0]
    proj = _matmul(h, w["w_in"], mode="nn", name="in_proj", outs=[F32], n_cols=D_MAIN)
    dt_raw = _matmul(h, w["w_dt"], mode="nn", name="dt_proj", outs=[F32])
    pairs = D_ATTN // 128
    qk_rows = [(proj, 128, 0, True), (proj, 128, pairs, True), (proj, 128, 2 * pairs, True), _full(cos), _full(sin)]
    qk_vecs = [_full(gq2), _full(gk2)]
    qn, kn, vn = _rowwise(_qk_fn, qk_rows, qk_vecs, [(D_ATTN, F32, 128, 0, True)] * 3, name="qk_prep", groups=8, tr=1024)
    branches = [_attention_fwd(qn, kn, vn, b) for b in range(3)]
    merge_rows = [_full(o) for o, _ in branches] + [_full(lse) for _, lse in branches]
    attn = _rowwise(_merge_fn, merge_rows, [_full(p["g_attn_out"])], [(D_ATTN, BF16, D_ATTN, 0, False)], name="attn_merge")[0]
    xbc = _conv_fwd(proj, p["conv_w"], p["conv_b"])
    ssm, h_in = _ssd_fwd(xbc, dt_raw, bias, alog, dsk, proj, p["g_ssm_out"])
    mix = jnp.concatenate([attn, ssm], axis=1)
    if more_weights is not None:
        w.update(more_weights("mixer_done", mix))
    x1, hc = _matmul(mix, w["w_out"], mode="nn", name="out_proj", outs=[F32, BF16], extra=(x,), vecs=(p["g_cross"],),
                     epilogue=_add_res_and_norm, tm=512, tn=D_MODEL)
    memh = _rowwise(_norm_fn, [_full(mem)], [_full(p["g_mem"])], norm_out, name="norm_mem", n_rows=N_MEM, tr=N_MEM)[0]
    qc = _matmul(hc, w["w_cq"], mode="nn", name="cq_proj", outs=[F32])
    if more_weights is not None:
        w.update(more_weights("cross_started", qc))
    kv = _matmul(memh, w["w_ckv"], mode="nn", name="ckv_proj", outs=[F32])
    oc = _cross_fwd(qc, kv, p["g_cq"], p["g_ck"])
    x2, hm = _matmul(oc, w["w_co"], mode="nn", name="co_proj", outs=[F32, BF16], extra=(x1,), vecs=(p["g_mlp"],),
                     epilogue=_add_res_and_norm, tm=512, tn=D_MODEL)
    if more_weights is not None:
        w.update(more_weights("cross_done", hm))
    u, act = _matmul(hm, w["w_up"], mode="nn", name="up_proj", outs=[F32, BF16],
                     epilogue=lambda acc: (acc, jnp.square(jnp.maximum(acc, 0.0))))
    dy, dyb, loss_tiles = _matmul(act, w["w_down"], mode="nn", name="down_proj", outs=[F32, BF16], extra=(x2, target),
                                  epilogue=_loss_epilogue, tile_sums=1)
    loss = jnp.sum(loss_tiles).reshape(1, 1)

    grads["w_down"] = _matmul(act, dyb, mode="tn", name="dw_down", outs=[BF16], after=_take_token(grads))
    du = _matmul(dyb, w["w_down"], mode="nt", name="d_act", outs=[BF16], extra=(u,), after=_take_token(grads),
                 epilogue=lambda acc, uu: (acc * (2.0 * jnp.maximum(uu, 0.0)),))
    _settle(grads, du)
    grads["w_up"] = _matmul(hm, du, mode="tn", name="dw_up", outs=[BF16], col_shards=4, after=_take_token(grads))
    dx2, grads["g_mlp"] = _matmul(du, w["w_up"], mode="nt", name="d_hm", outs=[F32], extra=(x2, dy), vecs=(p["g_mlp"],),
                                  epilogue=_norm_bwd_epilogue, tile_rows=1, after=_take_token(grads), tm=512, tn=D_MODEL,
                                  tk=1024)
    _settle(grads, dx2)
    grads["w_co"] = _matmul(oc, dx2, mode="tn", name="dw_co", outs=[BF16], col_shards=4, after=_take_token(grads))
    doc = _matmul(dx2, w["w_co"], mode="nt", name="d_oc", outs=[BF16])
    dqc, dkc, dvc, grads["g_cq"], grads["g_ck"] = _cross_bwd(qc, kv, p["g_cq"], p["g_ck"], doc)
    grads["w_cq"] = _matmul(hc, dqc, mode="tn", name="dw_cq", outs=[BF16])
    dkv = jnp.concatenate([dkc, dvc], axis=1)
    grads["w_ckv"] = _matmul(memh, dkv, mode="tn", name="dw_ckv", outs=[BF16])
    dmemh = _matmul(dkv, w["w_ckv"], mode="nt", name="d_memh", outs=[F32])
    grads["g_mem"] = _rowwise_vjp(_norm_fn, [_full(mem)], [_full(p["g_mem"])], [[_full(dmemh)]], [],
                                  [(0, D_MODEL, D_MODEL, 0, False)], name="norm_mem_bwd", n_rows=N_MEM, tr=N_MEM)[0]
    dx1, grads["g_cross"] = _matmul(dqc, w["w_cq"], mode="nt", name="d_hc", outs=[F32], extra=(x1, dx2), vecs=(p["g_cross"],),
                                    epilogue=_norm_bwd_epilogue, tile_rows=1, tm=512, tn=D_MODEL)
    grads["w_out"] = _matmul(mix, dx1, mode="tn", name="dw_out", outs=[BF16])
    dmix = _matmul(dx1, w["w_out"], mode="nt", name="d_mix", outs=[F32], after=_take_token(grads))
    _settle(grads, dmix)
    merge_grads = [(i, D_ATTN, F32, D_ATTN, 0, False, None) for i in range(6)]
    *dol, grads["g_attn_out"] = _rowwise_vjp(
        _merge_fn, merge_rows, [_full(p["g_attn_out"])], [[(dmix, D_ATTN, 0, False)]],
        merge_grads, [(0, D_ATTN, D_ATTN, 0, False)], name="attn_merge_bwd", tr=256, after=_take_token(grads))
    dqkv = [_attention_bwd(qn, kn, vn, *branches[b], dol[b], dol[3 + b], b) for b in range(3)]
    qk_cts = [[(dqkv[b][i], 128, 0, True) for b in range(3)] for i in range(3)]
    dq, dk, dv, dgq2, dgk2 = _rowwise_vjp(
        _qk_fn, qk_rows, qk_vecs, qk_cts, [(i, D_ATTN, BF16, 128, 0, True, None) for i in range(3)],
        [(0, 128, 128, 0, False), (1, 128, 128, 0, False)], name="qk_prep_bwd", groups=8, tr=1024)
    grads["g_q"] = dgq2[:, :HEAD] + dgq2[:, HEAD:]
    grads["g_k"] = dgk2[:, :HEAD] + dgk2[:, HEAD:]
    dxs, db, dc, ddt, dbias, dalog, ddsk, dz, grads["g_ssm_out"] = _ssd_bwd(xbc, dt_raw, bias, alog, dsk, h_in, proj,
                                                                             p["g_ssm_out"], dmix)
    grads["dt_bias"], grads["a_log"], grads["d_skip"] = _unpad_heads(dbias), _unpad_heads(dalog), _unpad_heads(ddsk)
    dxbc_raw, dconv_w, grads["conv_b"] = _conv_bwd(proj, p["conv_w"], p["conv_b"], dxs, db, dc)
    grads["conv_w"] = dconv_w[:4]
    dproj = jnp.concatenate([dq, dk, dv, dz, dxbc_raw], axis=1)
    grads["w_main"] = _matmul(h, dproj, mode="tn", name="dw_main", outs=[BF16], out_cols=D_MAIN + N_DT)
    grads["w_dt"] = _matmul(h, ddt, mode="tn", name="dw_dt", outs=[BF16])
    dh = _matmul(dproj, w["w_in"], mode="nt", name="d_h_main", outs=[F32], after=_take_token(grads))
    grad_x, grads["g_mix"] = _matmul(ddt, w["w_dt"], mode="nt", name="d_h_dt", outs=[F32], extra=(x, dx1, dh),
                                     vecs=(p["g_mix"],), epilogue=_norm_bwd_epilogue, tile_rows=1, tm=512, tn=D_MODEL)
    return loss, grad_x, grads


MATRICES = ("w_in", "w_out", "w_cq", "w_ckv", "w_co", "w_up", "w_down")
ROW_SHARDED = ("w_out", "w_cq", "w_ckv", "w_down")
N_CHIPS = 4
ANY = pl.BlockSpec(memory_space=pl.ANY)


def _place():
    return lax.axis_index("x"), lax.axis_index("y"), lax.axis_index("c")


def _other_chips(x, y):
    return [(1 - x, y), (x, 1 - y), (1 - x, 1 - y)]


def _remote(src, dst, send_sem, recv_sem, device):
    return pltpu.make_async_remote_copy(src_ref=src, dst_ref=dst, send_sem=send_sem, recv_sem=recv_sem,
                                        device_id=device, device_id_type=MESH)


def _gathered_shape(name, shard):
    rows, cols = shard.shape
    if name == "w_in":
        return (N_CHIPS, rows, cols)
    return (N_CHIPS * rows, cols) if name in ROW_SHARDED else (rows, N_CHIPS * cols)


def _shard_window(name, ref, rows, cols, chip, half):
    r0, nr = (0, rows) if half is None else (half * (rows // 2), rows // 2)
    if name == "w_in":
        return ref.at[chip, pl.ds(r0, nr), :]
    if name in ROW_SHARDED:
        return ref.at[pl.ds(chip * rows + r0, nr), :]
    return ref.at[pl.ds(r0, nr), pl.ds(pl.multiple_of(chip * cols, 128), cols)]


def _cast_into_gathered(w, name, chip, after=()):
    rows, cols = w.shape
    tr = _tile(rows, ROW_TILE)

    def body(chip_ref, w_ref, *rest):
        rest[-1][...] = w_ref[...].astype(BF16)

    if name == "w_in":
        out_spec = pl.BlockSpec((None, tr, cols), lambda i, chip_ref: (chip_ref[0], i, 0))
    elif name in ROW_SHARDED:
        out_spec = pl.BlockSpec((tr, cols), lambda i, chip_ref: (chip_ref[0] * (rows // tr) + i, 0))
    else:
        out_spec = pl.BlockSpec((tr, cols), lambda i, chip_ref: (i, chip_ref[0]))
    grid_spec = pltpu.PrefetchScalarGridSpec(
        num_scalar_prefetch=1, grid=(rows // tr,),
        in_specs=[pl.BlockSpec((tr, cols), lambda i, chip_ref: (i, 0))] + [pl.BlockSpec(memory_space=pl.ANY)] * len(after),
        out_specs=out_spec)
    return pl.pallas_call(body, name="cast_" + name, grid_spec=grid_spec,
                          out_shape=jax.ShapeDtypeStruct(_gathered_shape(name, w), BF16),
                          compiler_params=_params(("parallel",)))(chip.reshape(1).astype(jnp.int32), w, *after)


def _w_in_columns(arr, to_shards):
    rows, piece = D_MODEL, (D_MAIN + N_DT) // N_CHIPS
    tr = ROW_TILE

    def body(a_ref, o_ref):
        for j in range(N_CHIPS):
            if to_shards:
                o_ref[j] = a_ref[:, pl.ds(piece * j, piece)]
            else:
                o_ref[:, pl.ds(piece * j, piece)] = a_ref[j]

    pieces = pl.BlockSpec((N_CHIPS, tr, piece), lambda i: (0, i, 0))
    matrix = pl.BlockSpec((tr, N_CHIPS * piece), lambda i: (i, 0))
    out_dims = (N_CHIPS, rows, piece) if to_shards else (rows, N_CHIPS * piece)
    return pl.pallas_call(
        body, name="w_in_to_shards" if to_shards else "w_in_from_shards", grid=(rows // tr,),
        in_specs=[matrix if to_shards else pieces], out_specs=pieces if to_shards else matrix,
        out_shape=jax.ShapeDtypeStruct(out_dims, arr.dtype), compiler_params=_params(("parallel",)))(arr)


HBM = pl.BlockSpec(memory_space=pltpu.HBM)
SEM = pl.BlockSpec(memory_space=pltpu.SEMAPHORE)
EFFECT = pltpu.SideEffectType.DATAFLOW_SIDE_EFFECTING


def _split_start(name, bufs, plan, counts, after=()):
    n, n_g, n_after = len(bufs), len(counts), len(after)

    def body(*refs):
        ins, sems, token = refs[:n], refs[n + n_after:n + n_after + 2 * n_g], refs[-1]
        for g, copies in enumerate(plan(ins)):
            for i, (src, dst, device, _) in enumerate(copies):
                _remote(src, dst, sems[2 * g].at[i], sems[2 * g + 1].at[i], device).start()
        token[...] = jnp.zeros_like(token)

    sem_shapes = [pltpu.SemaphoreType.DMA((cnt,)) for cnt in counts for _ in range(2)]
    res = pl.pallas_call(
        body, name=name,
        out_shape=(*sem_shapes, *[pltpu.HBM(b.shape, b.dtype) for b in bufs], jax.ShapeDtypeStruct((8, 128), F32)),
        in_specs=(*(HBM,) * n, *(ANY,) * n_after),
        out_specs=(*(SEM,) * (2 * n_g), *(HBM,) * n, pl.BlockSpec(memory_space=pltpu.VMEM)),
        input_output_aliases={i: 2 * n_g + i for i in range(n)},
        compiler_params=pltpu.CompilerParams(has_side_effects=EFFECT),
    )(*[pltpu.with_memory_space_constraint(b, pltpu.HBM) for b in bufs], *after)
    sems = [(res[2 * g], res[2 * g + 1]) for g in range(n_g)]
    return sems, list(res[2 * n_g:2 * n_g + n]), res[-1]


def _split_wait(name, bufs, sems, plan, *after):
    n = len(bufs)

    def body(*refs):
        ins, send, recv = refs[:n], refs[n], refs[n + 1]
        (copies,) = plan(ins)
        for i, (src, _, device, landing) in enumerate(copies):
            cp = _remote(src, landing, send.at[i], recv.at[i], device)
            cp.wait_send()
            cp.wait_recv()

    res = pl.pallas_call(
        body, name=name, out_shape=tuple(pltpu.HBM(b.shape, b.dtype) for b in bufs),
        in_specs=(*(HBM,) * n, SEM, SEM, *(ANY,) * len(after)), out_specs=(HBM,) * n,
        input_output_aliases={i: i for i in range(n)},
        compiler_params=pltpu.CompilerParams(has_side_effects=EFFECT),
    )(*bufs, sems[0], sems[1], *after)
    return list(res)


def _ici_plan(names, shard_shapes):
    def plan(refs):
        x, y, c = _place()
        copies = []
        for ref, name in zip(refs, names):
            win = _shard_window(name, ref, *shard_shapes[name], 2 * x + y, c)
            for px, py in _other_chips(x, y):
                copies.append((win, win, (px, py, c), _shard_window(name, ref, *shard_shapes[name], 2 * px + py, c)))
        return [copies]
    return plan


def _pass_on_plan(names, shard_shapes):
    def plan(refs):
        x, y, c = _place()
        copies = []
        for ref, name in zip(refs, names):
            for px, py in _other_chips(x, y):
                win = _shard_window(name, ref, *shard_shapes[name], 2 * px + py, c)
                copies.append((win, win, (x, y, 1 - c), _shard_window(name, ref, *shard_shapes[name], 2 * px + py, 1 - c)))
        return [copies]
    return plan


def _swap_plan(n_pairs):
    def plan(refs):
        x, y, c = _place()
        return [[(src.at[:, 1 - c], dst, (x, y, 1 - c), dst) for src, dst in zip(refs[:n_pairs], refs[n_pairs:])]]
    return plan


def _share_plan(n_pairs):
    def plan(refs):
        x, y, c = _place()
        return [[(src, dst, (x, y, 1 - c), dst)] for src, dst in zip(refs[:n_pairs], refs[n_pairs:])]
    return plan


def _scatter_plan(n_pairs):
    def plan(refs):
        x, y, c = _place()
        copies = []
        for src, dst in zip(refs[:n_pairs], refs[n_pairs:]):
            for k, (px, py) in enumerate(_other_chips(x, y)):
                copies.append((src.at[2 * px + py], dst.at[k], (px, py, c), dst.at[k]))
        return [copies]
    return plan


def _sibling_swap(arrs, name):
    n = len(arrs)

    def body(*refs):
        ins, outs, send, recv = refs[:n], refs[n:2 * n], refs[2 * n], refs[2 * n + 1]
        x, y, c = _place()
        cps = [_remote(ins[w].at[:, 1 - c], outs[w], send.at[w], recv.at[w], (x, y, 1 - c)) for w in range(n)]
        for cp in cps:
            cp.start()
        for cp in cps:
            cp.wait()

    return pl.pallas_call(
        body, name=name, in_specs=[ANY] * n, out_specs=[ANY] * n,
        out_shape=[jax.ShapeDtypeStruct((a.shape[0],) + a.shape[2:], a.dtype) for a in arrs],
        scratch_shapes=[pltpu.SemaphoreType.DMA((n,))] * 2,
    )(*arrs)


def _small_allreduce(buf, name, after=()):
    rows = buf.shape[0]

    def body(x_ref, *rest):
        out_ref, all_ref, send_sems, recv_sems, local_sem = rest[len(after):]
        x, y, c = _place()
        me, sibling, chips = (x, y, c), (x, y, 1 - c), _other_chips(x, y)

        def block(px, py, pc):
            return all_ref.at[pl.ds((4 * px + 2 * py + pc) * rows, rows), :]

        def copy(k, blk, to, src=None):
            return _remote(block(*blk) if src is None else src, block(*blk), send_sems.at[k], recv_sems.at[k], to)

        own = pltpu.make_async_copy(x_ref, block(*me), local_sem)
        own.start()
        first = [copy(0, me, sibling, src=x_ref)] + [copy(1 + j, me, (*chip, c), src=x_ref) for j, chip in enumerate(chips)]
        for cp in first:
            cp.start()
        passed = [copy(4 + j, (*chip, c), sibling) for j, chip in enumerate(chips)]
        for j, chip in enumerate(chips):
            copy(1 + j, (*chip, c), me).wait_recv()
            passed[j].start()
        copy(0, sibling, me).wait_recv()
        for j, chip in enumerate(chips):
            copy(4 + j, (*chip, 1 - c), me).wait_recv()
        for cp in first + passed:
            cp.wait_send()
        own.wait()
        acc = all_ref[pl.ds(0, rows), :]
        for d in range(1, 8):
            acc = acc + all_ref[pl.ds(d * rows, rows), :]
        out_ref[...] = acc

    vmem = pl.BlockSpec(memory_space=pltpu.VMEM)
    return pl.pallas_call(
        body, name=name, in_specs=[vmem] + [ANY] * len(after), out_specs=vmem,
        out_shape=jax.ShapeDtypeStruct(buf.shape, F32),
        scratch_shapes=[pltpu.VMEM((8 * rows, 128), F32), pltpu.SemaphoreType.DMA((7,)), pltpu.SemaphoreType.DMA((7,)),
                        pltpu.SemaphoreType.DMA],
    )(buf, *after)


ROW_TILE = 256
BIG_ROW_TILE = 1024


def _add_halves(arr, recv, c, name):
    _, _, hr, cols = arr.shape
    tr = _tile(hr, BIG_ROW_TILE)

    def body(c_ref, a_ref, r_ref, o_ref):
        o_ref[...] = (a_ref[...].astype(F32) + r_ref[...].astype(F32)).astype(o_ref.dtype)

    piece = pl.BlockSpec((None, tr, cols), lambda j, i, c_ref: (j, i, 0))
    grid_spec = pltpu.PrefetchScalarGridSpec(
        num_scalar_prefetch=1, grid=(N_CHIPS, hr // tr),
        in_specs=[pl.BlockSpec((None, None, tr, cols), lambda j, i, c_ref: (j, c_ref[0], i, 0)), piece], out_specs=piece)
    return pl.pallas_call(body, name=name, grid_spec=grid_spec, out_shape=jax.ShapeDtypeStruct(recv.shape, BF16),
                          compiler_params=_params(("parallel", "parallel")))(c.reshape(1).astype(jnp.int32), arr, recv)


def _flip_slot(d):
    return jnp.where(d == 1, 1, jnp.where(d == 3, 2, 0))


def _sum_chips(p, q, chip, name):
    _, hr, cols = p.shape
    tr = _tile(hr, BIG_ROW_TILE)

    def body(chip_ref, p_ref, q_ref, o_ref):
        j = pl.program_id(1)
        term = jnp.where(j == chip_ref[0], p_ref[...].astype(F32), q_ref[...].astype(F32))

        @pl.when(j == 0)
        def _():
            o_ref[...] = term

        @pl.when(j != 0)
        def _():
            o_ref[...] += term

    grid_spec = pltpu.PrefetchScalarGridSpec(
        num_scalar_prefetch=1, grid=(hr // tr, N_CHIPS),
        in_specs=[pl.BlockSpec((None, tr, cols), lambda i, j, chip_ref: (chip_ref[0], i, 0)),
                  pl.BlockSpec((None, tr, cols), lambda i, j, chip_ref: (_flip_slot(j ^ chip_ref[0]), i, 0))],
        out_specs=pl.BlockSpec((tr, cols), lambda i, j, chip_ref: (i, 0)))
    return pl.pallas_call(body, name=name, grid_spec=grid_spec, out_shape=jax.ShapeDtypeStruct((hr, cols), F32),
                          compiler_params=_params(("parallel", "arbitrary")))(chip.reshape(1).astype(jnp.int32), p, q)


def _adamw_halves(w, g_own, g_other, m, v, c, name):
    rows, cols = w.shape
    tr = _tile(rows // 2, ROW_TILE)
    per_half = rows // 2 // tr

    def body(c_ref, w_ref, own_ref, other_ref, m_ref, v_ref, g_ref, d_ref, nm_ref, nv_ref):
        mine = (pl.program_id(0) // per_half) == c_ref[0]
        g_ = jnp.where(mine, own_ref[...], other_ref[...])
        g_ref[...] = g_
        d_ref[...], nm_ref[...], nv_ref[...] = _adamw_math(w_ref[...], g_, m_ref[...], v_ref[...])

    blk = pl.BlockSpec((tr, cols), lambda i, c_ref: (i, 0))
    own = pl.BlockSpec((tr, cols), lambda i, c_ref: (jnp.where(i // per_half == c_ref[0], i % per_half, 0), 0))
    other = pl.BlockSpec((tr, cols), lambda i, c_ref: (jnp.where(i // per_half == c_ref[0], 0, i % per_half), 0))
    grid_spec = pltpu.PrefetchScalarGridSpec(num_scalar_prefetch=1, grid=(rows // tr,),
                                             in_specs=[blk, own, other, blk, blk], out_specs=[blk] * 4)
    return pl.pallas_call(body, name=name, grid_spec=grid_spec, out_shape=[jax.ShapeDtypeStruct(w.shape, F32)] * 4,
                          compiler_params=_params(("parallel",)))(c.reshape(1).astype(jnp.int32), w, g_own, g_other, m, v)


W_IN_COLS = (D_MAIN + N_DT) // N_CHIPS
W_IN_MAIN = W_IN_COLS // 128 * 128
W_IN_TAIL = W_IN_COLS - 128
W_IN_PARTS = ((0, W_IN_MAIN), (W_IN_TAIL, 128))


def _cast_w_in_transposed(w_t, chip, after=()):
    def body(chip_ref, w_ref, *rest):
        for start, size in W_IN_PARTS:
            rest[-1][:, pl.ds(start, size)] = w_ref[pl.ds(start, size), :].T.astype(BF16)

    grid_spec = pltpu.PrefetchScalarGridSpec(
        num_scalar_prefetch=1, grid=(D_MODEL // ROW_TILE,),
        in_specs=[pl.BlockSpec((W_IN_COLS, ROW_TILE), lambda i, chip_ref: (0, i))] + [pl.BlockSpec(memory_space=pl.ANY)] * len(after),
        out_specs=pl.BlockSpec((None, ROW_TILE, W_IN_COLS), lambda i, chip_ref: (chip_ref[0], i, 0)))
    return pl.pallas_call(body, name="cast_w_in", grid_spec=grid_spec,
                          out_shape=jax.ShapeDtypeStruct((N_CHIPS, D_MODEL, W_IN_COLS), BF16),
                          compiler_params=_params(("parallel",)))(chip.reshape(1).astype(jnp.int32), w_t, *after)


def _adamw_w_in_transposed(w_t, g_own, g_other, m_t, v_t, c):
    per_half = D_MODEL // 2 // ROW_TILE

    def body(c_ref, w_ref, own_ref, other_ref, m_ref, v_ref, g_ref, d_ref, nm_ref, nv_ref):
        mine = (pl.program_id(0) // per_half) == c_ref[0]
        for start, size in W_IN_PARTS:
            cols, rows = pl.ds(start, size), pl.ds(start, size)
            g_ = jnp.where(mine, own_ref[:, cols], other_ref[:, cols]).T
            g_ref[rows, :] = g_
            d_ref[rows, :], nm_ref[rows, :], nv_ref[rows, :] = _adamw_math(w_ref[rows, :], g_, m_ref[rows, :], v_ref[rows, :])

    blk = pl.BlockSpec((W_IN_COLS, ROW_TILE), lambda i, c_ref: (0, i))
    own = pl.BlockSpec((ROW_TILE, W_IN_COLS), lambda i, c_ref: (jnp.where(i // per_half == c_ref[0], i % per_half, 0), 0))
    other = pl.BlockSpec((ROW_TILE, W_IN_COLS), lambda i, c_ref: (jnp.where(i // per_half == c_ref[0], 0, i % per_half), 0))
    grid_spec = pltpu.PrefetchScalarGridSpec(num_scalar_prefetch=1, grid=(D_MODEL // ROW_TILE,),
                                             in_specs=[blk, own, other, blk, blk], out_specs=[blk] * 4)
    return pl.pallas_call(body, name="adamw_w_in", grid_spec=grid_spec, out_shape=[jax.ShapeDtypeStruct(w_t.shape, F32)] * 4,
                          compiler_params=_params(("parallel",)))(c.reshape(1).astype(jnp.int32), w_t, g_own, g_other, m_t, v_t)


def _adamw_math(w, g, m, v):
    m_new = ADAM_B1 * m + (1.0 - ADAM_B1) * g
    v_new = ADAM_B2 * v + (1.0 - ADAM_B2) * (g * g)
    m_hat = m_new / (1.0 - ADAM_B1 ** ADAM_STEP)
    v_hat = v_new / (1.0 - ADAM_B2 ** ADAM_STEP)
    return -ADAM_LR * (m_hat / (jnp.sqrt(v_hat) + ADAM_EPS) + ADAM_WD * w), m_new, v_new


VECTORS = ("g_mix", "g_q", "g_k", "g_attn_out", "conv_b", "dt_bias", "a_log", "d_skip", "g_ssm_out", "g_cross", "g_mem",
           "g_cq", "g_ck", "g_mlp")
WEIGHTS = ("g_mix", "w_in", "g_q", "g_k", "g_attn_out", "conv_w", "conv_b", "dt_bias", "a_log", "d_skip", "g_ssm_out", "w_out",
           "g_cross", "g_mem", "w_cq", "w_ckv", "g_cq", "g_ck", "w_co", "g_mlp", "w_up", "w_down")


def _pack(parts):
    flat = jnp.concatenate([t.reshape(-1) for t in parts])
    total = -(-flat.shape[0] // 1024) * 1024
    return jnp.pad(flat, (0, total - flat.shape[0])).reshape(total // 128, 128)


def _rows_of(n):
    return -(-n // 128)


def _slot_rows(n):
    return -(-n // 1024) * 8


def _pack_rows(parts):
    rows = []
    for t in parts:
        flat = t.reshape(-1)
        rows.append(jnp.pad(flat, (0, 128 * _slot_rows(flat.shape[0]) - flat.shape[0])).reshape(-1, 128))
    return jnp.concatenate(rows)


def _adamw_vectors(summed, chip, vectors, conv):
    groups = list(vectors) + [conv]
    offsets, row = [], 0
    for w, _, _ in groups:
        offsets.append(row)
        row += _slot_rows(w.shape[1]) if w.shape[0] == 1 else _slot_rows(w.shape[0] * N_CHIPS * w.shape[1])
    conv_blocks = _rows_of(conv[0].shape[1])

    def body(chip_ref, sum_ref, *refs):
        ins, outs = refs[:3 * len(groups)], refs[3 * len(groups):]

        def update(i, g, idx):
            w_ref, m_ref, v_ref = ins[3 * i:3 * i + 3]
            delta, new_m, new_v = _adamw_math(w_ref[idx], g, m_ref[idx], v_ref[idx])
            for o_ref, val in zip(outs[4 * i:4 * i + 4], (g, delta, new_m, new_v)):
                o_ref[idx] = val

        for i, (w, _, _) in enumerate(vectors):
            for t in range(_rows_of(w.shape[1])):
                width = min(128, w.shape[1] - 128 * t)
                update(i, sum_ref[pl.ds(offsets[i] + t, 1), pl.ds(0, width)], (slice(None), pl.ds(128 * t, width)))
        for tap in range(conv[0].shape[0]):
            for blk in range(conv_blocks):
                src = offsets[-1] + tap * N_CHIPS * conv_blocks + chip_ref[0] * conv_blocks + blk
                update(len(vectors), sum_ref[pl.ds(src, 1), :], (pl.ds(tap, 1), pl.ds(128 * blk, 128)))

    def whole(a):
        return pl.BlockSpec(a.shape, lambda i, chip_ref: (0,) * a.ndim)

    operands = [t for group in groups for t in group]
    grid_spec = pltpu.PrefetchScalarGridSpec(
        num_scalar_prefetch=1, grid=(1,), in_specs=[whole(summed)] + [whole(t) for t in operands],
        out_specs=[whole(w) for w, _, _ in groups for _ in range(4)])
    res = pl.pallas_call(body, name="adamw_vectors", grid_spec=grid_spec,
                         out_shape=[jax.ShapeDtypeStruct(w.shape, F32) for w, _, _ in groups for _ in range(4)],
                         compiler_params=_params(("arbitrary",)))(chip.reshape(1).astype(jnp.int32), summed, *operands)
    return [res[4 * i:4 * i + 4] for i in range(len(groups))]


def _unpack(buf, shapes):
    flat, out, pos = buf.reshape(-1), [], 0
    for shape in shapes:
        size = math.prod(shape)
        out.append(flat[pos:pos + size].reshape(shape))
        pos += size
    return out


def kernel(x, mem, positions, g_mix, w_in, g_q, g_k, g_attn_out, conv_w, conv_b, dt_bias, a_log, d_skip, g_ssm_out, w_out, g_cross, g_mem, w_cq, w_ckv, g_cq, g_ck, w_co, g_mlp, w_up, w_down, loss_target, m_g_mix, m_w_in, m_g_q, m_g_k, m_g_attn_out, m_conv_w, m_conv_b, m_dt_bias, m_a_log, m_d_skip, m_g_ssm_out, m_w_out, m_g_cross, m_g_mem, m_w_cq, m_w_ckv, m_g_cq, m_g_ck, m_w_co, m_g_mlp, m_w_up, m_w_down, v_g_mix, v_w_in, v_g_q, v_g_k, v_g_attn_out, v_conv_w, v_conv_b, v_dt_bias, v_a_log, v_d_skip, v_g_ssm_out, v_w_out, v_g_cross, v_g_mem, v_w_cq, v_w_ckv, v_g_cq, v_g_ck, v_w_co, v_g_mlp, v_w_up, v_w_down):
    args = dict(locals())
    weights = {n: args[n][0] for n in WEIGHTS}
    mom_m = {n: args["m_" + n][0] for n in WEIGHTS}
    mom_v = {n: args["v_" + n][0] for n in WEIGHTS}
    x_idx, y_idx, c_idx = _place()
    chip = 2 * x_idx + y_idx

    shapes = {n: weights[n].shape for n in MATRICES}
    first, mid, late = ("w_in",), ("w_out", "w_cq", "w_ckv", "w_co"), ("w_up", "w_down")
    w_in_t, m_in_t, v_in_t = (jnp.swapaxes(t, 1, 2)[0] for t in (w_in, m_w_in, v_w_in))
    w_in_buf = [_cast_w_in_transposed(w_in_t, chip)]
    taps, tap_cols = weights["conv_w"].shape
    conv_parts = _small_allreduce(_pack([jnp.zeros((N_CHIPS, taps, tap_cols), F32).at[chip].set(0.5 * weights["conv_w"])]),
                                  "gather_conv_taps")
    sems_in, w_in_buf, token = _split_start("gather_ici_start_w_in", w_in_buf, _ici_plan(first, shapes), [3], after=(conv_parts,))
    bufs = [_cast_into_gathered(weights[n], n, chip, after=(token,)) for n in mid + late]
    plan = lambda refs: _ici_plan(mid, shapes)(refs[:4]) + _ici_plan(late, shapes)(refs[4:])
    sems_rest, bufs, token = _split_start("gather_ici_start_rest", bufs, plan, [12, 6], after=(token,))
    params = {n: weights[n].reshape(1, -1) for n in VECTORS}
    h_in = _rowwise(_norm_fn, [_full(x[0])], [_full(params["g_mix"])], [(D_MODEL, BF16, D_MODEL, 0, False)], name="norm_in",
                    after=(token,))[0]
    w_in_buf = _split_wait("gather_ici_wait_w_in", w_in_buf, sems_in[0], _ici_plan(first, shapes), token, h_in, m_in_t, v_in_t)
    pass_sems, w_in_buf, token = _split_start("gather_pass_start_w_in", w_in_buf, _pass_on_plan(first, shapes), [3])
    w_in_buf = _split_wait("gather_pass_wait_w_in", w_in_buf, pass_sems[0], _pass_on_plan(first, shapes), token)
    w_in_full = _w_in_columns(w_in_buf[0], to_shards=False)
    full = {"w_in": w_in_full,
            "w_dt": jnp.pad(w_in_full[:, D_MAIN:].reshape(D_MODEL, N_GROUPS, HEADS_PER_GROUP),
                            ((0, 0), (0, 0), (0, 128 - HEADS_PER_GROUP))).reshape(D_MODEL, DT_PAD)}
    in_flight = {}

    def more_weights(stage, after):
        if stage == "mixer_done":
            got = _split_wait("gather_ici_wait_mid", bufs[:4], sems_rest[0], _ici_plan(mid, shapes), after)
            sems, got, token = _split_start("gather_pass_start_mid", got, _pass_on_plan(mid, shapes), [12])
            return dict(zip(mid, _split_wait("gather_pass_wait_mid", got, sems[0], _pass_on_plan(mid, shapes), token)))
        if stage == "cross_started":
            got = _split_wait("gather_ici_wait_late", bufs[4:], sems_rest[1], _ici_plan(late, shapes), after)
            in_flight["late"] = _split_start("gather_pass_start_late", got, _pass_on_plan(late, shapes), [6])
            return {}
        sems, got, token = in_flight.pop("late")
        return dict(zip(late, _split_wait("gather_pass_wait_late", got, sems[0], _pass_on_plan(late, shapes), token, after)))

    params["conv_w"] = _unpack(conv_parts, [(N_CHIPS, taps, tap_cols)])[0].transpose(1, 0, 2).reshape(taps, N_CHIPS * tap_cols)

    groups = (("w_down",), ("w_up",), ("w_co", "w_cq", "w_ckv", "w_out"), ("w_in",))
    scattered = []

    class GradStore(dict):
        pending = None

        def __setitem__(self, name, value):
            super().__setitem__(name, value)
            if "w_main" in self and "w_dt" in self and "w_in" not in self:
                gw_in = lax.dynamic_update_slice(self["w_main"], _unpad_heads(self["w_dt"]), (0, D_MAIN))
                self["w_in"] = _w_in_columns(gw_in, to_shards=True)
            for group in groups:
                if name in group and all(n in self for n in group):
                    self.settle()
                    pieces = [self[n].reshape(N_CHIPS, 2, shapes[n][0] // 2, shapes[n][1]) for n in group]
                    if group == groups[-1]:
                        self.scatter(group, pieces, _sibling_swap(pieces, "grad_swap_" + group[0]))
                    else:
                        landing = [lax.empty((N_CHIPS,) + a.shape[2:], BF16) for a in pieces]
                        sems, thru, self.token = _split_start("grad_swap_start_" + group[0], pieces + landing,
                                                              _swap_plan(len(pieces)), [len(pieces)])
                        self.pending = (group, sems[0], thru)

        def settle(self, *after):
            if self.pending is not None:
                group, sems, thru = self.pending
                self.pending = None
                thru = _split_wait("grad_swap_wait_" + group[0], thru, sems, _swap_plan(len(group)), *after)
                self.scatter(group, thru[:len(group)], thru[len(group):])

        def scatter(self, group, pieces, from_sibling):
            sums = [_add_halves(a, r, c_idx, "add_halves_" + n) for n, a, r in zip(group, pieces, from_sibling)]
            landing = [lax.empty((3,) + s.shape[1:], BF16) for s in sums]
            sems, thru, self.token = _split_start("grad_scatter_start_" + group[0], sums + landing,
                                                  _scatter_plan(len(sums)), [3 * len(sums)])
            scattered.append((group, sems[0], thru))

    loss, grad_x, grads = _local_step(x[0], mem[0], positions[0], loss_target[0], params, full, more_weights, GradStore(),
                                      h_in)

    out_g, out_d, out_m, out_v = {}, {}, {}, {}

    halves = {}

    def finish(entries, order, token):
        for group, sems, thru in entries:
            thru = _split_wait("grad_scatter_wait_" + group[0], thru, sems, _scatter_plan(len(group)), token)
            for i, n in enumerate(group):
                halves[n] = _sum_chips(thru[i], thru[len(group) + i], chip, "sum_chips_" + n)
        sources = [halves[n] for n in order]
        landing = [lax.empty(s.shape, F32) for s in sources]
        sems, thru, token = _split_start("grad_share_start_" + order[0], sources + landing, _share_plan(len(order)),
                                         [1] * len(order))
        for i, n in enumerate(order):
            own, other = _split_wait("grad_share_wait_" + n, [thru[i], thru[len(order) + i]], sems[i], _share_plan(1), token)
            if n == "w_in":
                res_t = _adamw_w_in_transposed(w_in_t, own, other, m_in_t, v_in_t, c_idx)
                out_g[n], out_d[n], out_m[n], out_v[n] = (t.T for t in res_t)
            else:
                out_g[n], out_d[n], out_m[n], out_v[n] = _adamw_halves(weights[n], own, other, mom_m[n], mom_v[n], c_idx,
                                                                       "adamw_" + n)
            token = out_v[n]
        return token

    token = finish(scattered[:-1], ("w_cq", "w_up", "w_down"), grad_x)
    finish(scattered[-1:], ("w_in", "w_co", "w_ckv", "w_out"), token)

    names = VECTORS + ("conv_w",)
    summed = _small_allreduce(_pack_rows([grads[n] for n in names] + [loss]), "allreduce_vectors")
    total_loss = summed[sum(_slot_rows(grads[n].size) for n in names), 0]
    small_out = _adamw_vectors(summed, chip, [(args[n], args["m_" + n], args["v_" + n]) for n in VECTORS],
                               (weights["conv_w"], mom_m["conv_w"], mom_v["conv_w"]))
    for n, res in zip(names, small_out):
        out_g[n], out_d[n], out_m[n], out_v[n] = (t.reshape(weights[n].shape) for t in res)

    outs =[total_loss, grad_x[None]]
    for group in (out_g, out_d, out_m, out_v):
        outs += [group[n][None] for n in WEIGHTS]
    return tuple(outs)
```

```python
import functools
import math

import jax
import jax.numpy as jnp
from jax import lax
from jax.experimental import pallas as pl
from jax.experimental.pallas import tpu as pltpu

F32 = jnp.float32
BF16 = jnp.bfloat16

SEQ = 2048
D_MODEL = 2048
HEAD = 64
D_ATTN = 1024
D_SSM = 1024
N_GROUPS = 4
N_STATE = 128
CHUNK = 128
ATT_BLK = 128
N_MEM = 256
D_CROSS = 512
D_MAIN = 6144
N_DT = 16
DT_PAD = 512
ROT = 16
ROPE_THETA = 500000.0
EPS = 1e-6
NEG = -1e30
BRANCH_BLOCKS = (16, 4, 1)
DILATIONS = (1, 4, 16)

ADAM_LR, ADAM_B1, ADAM_B2, ADAM_EPS, ADAM_WD, ADAM_STEP = 0.001, 0.9, 0.999, 1e-08, 0.01, 10

VMEM_LIMIT = 56 * 1024 * 1024
MESH = pl.DeviceIdType.MESH


def _params(sem, **kw):
    return pltpu.CompilerParams(dimension_semantics=sem, vmem_limit_bytes=VMEM_LIMIT, **kw)


def _bdot(a, b, dims):
    return lax.dot_general(a.astype(BF16), b.astype(BF16), (dims, ((), ())), preferred_element_type=F32)


def _fdot(a, b, dims):
    return lax.dot_general(a, b, (dims, ((), ())), preferred_element_type=F32, precision=lax.Precision.HIGHEST)


NN = ((1,), (0,))
NT = ((1,), (1,))
TN = ((0,), (0,))


def _tile(n, want):
    t = min(n, want)
    while n % t:
        t //= 2
    return t


def _matmul(a, b, *, mode, name, outs, extra=(), vecs=(), epilogue=None, col_shards=1, after=(), n_cols=None, out_cols=None,
            tile_rows=0, tile_sums=0, tm=1024, tn=1024, tk=2048):
    if mode == "nn":
        (m, k), n = a.shape, b.shape[1]
    elif mode == "nt":
        (m, k), n = a.shape, b.shape[0]
    else:
        (k, m), n = a.shape, b.shape[1]
    n = n if n_cols is None else n_cols
    tm, tn, tk = _tile(m, tm), _tile(n // col_shards, tn), _tile(k, tk)
    nk = k // tk
    per_shard = n // col_shards // tn
    dims = {"nn": NN, "nt": NT, "tn": TN}[mode]
    a_spec = pl.BlockSpec((tk, tm), lambda i, j, kk: (kk, i)) if mode == "tn" else pl.BlockSpec((tm, tk), lambda i, j, kk: (i, kk))
    b_spec = pl.BlockSpec((tn, tk), lambda i, j, kk: (j, kk)) if mode == "nt" else pl.BlockSpec((tk, tn), lambda i, j, kk: (kk, j))
    o_spec = pl.BlockSpec((tm, tn), lambda i, j, kk: (i, j))
    n_extra, n_out, n_after = len(extra) + len(vecs), len(outs), len(after)

    def body(a_ref, b_ref, *rest):
        extra_refs, out_refs, acc_ref = rest[:n_extra], rest[n_extra + n_after:-1], rest[-1]

        def finish(acc):
            res = (acc,) if epilogue is None else epilogue(acc, *[e[...] for e in extra_refs])
            for o_ref, r in zip(out_refs[:n_out], res):
                o_ref[...] = r.astype(o_ref.dtype)
            for o_ref, r in zip(out_refs[n_out:], res[n_out:]):
                o_ref[...] = jnp.broadcast_to(r, o_ref.shape)

        if nk == 1:
            finish(_bdot(a_ref[...], b_ref[...], dims))
            return
        kk = pl.program_id(2)

        @pl.when(kk == 0)
        def _():
            acc_ref[...] = jnp.zeros_like(acc_ref)

        acc_ref[...] += _bdot(a_ref[...], b_ref[...], dims)

        @pl.when(kk == nk - 1)
        def _():
            finish(acc_ref[...])

    if col_shards == 1:
        out_specs, out_dims = [o_spec] * n_out, (m, n if out_cols is None else out_cols)
    else:
        sharded = pl.BlockSpec((None, tm, tn), lambda i, j, kk: (j // per_shard, i, j % per_shard))
        out_specs, out_dims = [sharded] * n_out, (col_shards, m, n // col_shards)
    res = pl.pallas_call(
        body, name=name, grid=(m // tm, n // tn, nk),
        in_specs=[a_spec, b_spec] + [o_spec] * len(extra) + [pl.BlockSpec((1, tn), lambda i, j, kk: (0, j))] * len(vecs)
        + [pl.BlockSpec(memory_space=pl.ANY)] * n_after,
        out_specs=out_specs + [pl.BlockSpec((8, tn), lambda i, j, kk: (i, j))] * tile_rows
        + [pl.BlockSpec((8, 128), lambda i, j, kk: (i, j))] * tile_sums,
        out_shape=[jax.ShapeDtypeStruct(out_dims, dt) for dt in outs] + [jax.ShapeDtypeStruct((m // tm * 8, n), F32)] * tile_rows
        + [jax.ShapeDtypeStruct((m // tm * 8, n // tn * 128), F32)] * tile_sums,
        scratch_shapes=[pltpu.VMEM((tm, tn) if nk > 1 else (8, 128), F32)],
        compiler_params=_params(("parallel", "parallel", "arbitrary")),
    )(a, b, *extra, *vecs, *after)
    res = (list(res[:n_out]) + [jnp.sum(t[::8], axis=0, keepdims=True) for t in res[n_out:n_out + tile_rows]]
           + [t[::8, ::128] for t in res[n_out + tile_rows:]])
    return res[0] if len(res) == 1 else res


def _row_spec(tr, bw, cb, per_group):
    return pl.BlockSpec((tr, bw), (lambda g, i: (i, cb + g)) if per_group else (lambda g, i: (i, cb)))


def _vec_spec(bw, cb, per_group):
    return pl.BlockSpec((1, bw), (lambda g, i: (0, cb + g)) if per_group else (lambda g, i: (0, cb)))


def _rowwise(fn, rows, vecs, outs, *, name, n_rows=SEQ, tr=512, groups=1, after=()):
    n_r, n_v, n_after = len(rows), len(vecs), len(after)

    def body(*refs):
        vals = [r[...].astype(F32) for r in refs[:n_r + n_v]]
        res = fn(*vals)
        for o_ref, r in zip(refs[n_r + n_v + n_after:], res):
            o_ref[...] = r.astype(o_ref.dtype)

    res = pl.pallas_call(
        body, name=name, grid=(groups, n_rows // tr),
        in_specs=[_row_spec(tr, bw, cb, pg) for _, bw, cb, pg in rows] + [_vec_spec(bw, cb, pg) for _, bw, cb, pg in vecs]
        + [pl.BlockSpec(memory_space=pl.ANY)] * n_after,
        out_specs=[_row_spec(tr, bw, cb, pg) for _, _, bw, cb, pg in outs],
        out_shape=[jax.ShapeDtypeStruct((n_rows, w), dt) for w, dt, _, _, _ in outs],
        compiler_params=_params(("parallel", "parallel")),
    )(*[r[0] for r in rows], *[v[0] for v in vecs], *after)
    return res


def _rowwise_vjp(fn, rows, vecs, cts, row_grads, vec_grads, *, name, n_rows=SEQ, tr=512, groups=1, after=()):
    n_r, n_v, n_after = len(rows), len(vecs), len(after)
    ct_ops = [op for group in cts for op in group]
    ct_sizes = [len(group) for group in cts]
    res_ops = [g[6] for g in row_grads if g[6] is not None]
    n_ct, n_res, n_rg = len(ct_ops), len(res_ops), len(row_grads)

    def body(*refs):
        vals = [r[...].astype(F32) for r in refs[:n_r + n_v]]
        pos = n_r + n_v
        ct_vals = []
        for size in ct_sizes:
            acc = refs[pos][...].astype(F32)
            for t in range(1, size):
                acc = acc + refs[pos + t][...].astype(F32)
            ct_vals.append(acc)
            pos += size
        res_refs = refs[pos:pos + n_res]
        out_refs = refs[pos + n_res + n_after:]
        _, pullback = jax.vjp(fn, *vals)
        grads = pullback(tuple(ct_vals))
        r_i = 0
        for o_ref, g in zip(out_refs[:n_rg], row_grads):
            val = grads[g[0]]
            if g[6] is not None:
                val = val + res_refs[r_i][...].astype(F32)
                r_i += 1
            o_ref[...] = val.astype(o_ref.dtype)
        first = (pl.program_id(1) == 0)
        for o_ref, g in zip(out_refs[n_rg:], vec_grads):
            val = jnp.sum(grads[n_r + g[0]], axis=0, keepdims=True)
            init = first if g[4] else jnp.logical_and(first, pl.program_id(0) == 0)

            @pl.when(init)
            def _(o_ref=o_ref, val=val):
                o_ref[...] = val

            @pl.when(jnp.logical_not(init))
            def _(o_ref=o_ref, val=val):
                o_ref[...] += val

    in_specs = [_row_spec(tr, bw, cb, pg) for _, bw, cb, pg in rows] + [_vec_spec(bw, cb, pg) for _, bw, cb, pg in vecs]
    in_specs += [_row_spec(tr, bw, cb, pg) for _, bw, cb, pg in ct_ops + res_ops] + [pl.BlockSpec(memory_space=pl.ANY)] * n_after
    out_specs =[_row_spec(tr, g[3], g[4], g[5]) for g in row_grads] + [_vec_spec(g[2], g[3], g[4]) for g in vec_grads]
    out_shape = [jax.ShapeDtypeStruct((n_rows, g[1]), g[2]) for g in row_grads]
    out_shape += [jax.ShapeDtypeStruct((1, g[1]), F32) for g in vec_grads]
    return pl.pallas_call(
        body, name=name, grid=(groups, n_rows // tr),
        in_specs=in_specs, out_specs=out_specs, out_shape=out_shape,
        compiler_params=_params(("arbitrary", "arbitrary")),
    )(*[r[0] for r in rows], *[v[0] for v in vecs], *[c[0] for c in ct_ops], *[r[0] for r in res_ops], *after)


def _full(arr, width=None):
    return (arr, arr.shape[1] if width is None else width, 0, False)


def _make_xor(sh):
    def raw(x):
        n = x.shape[-1]
        lane = lax.broadcasted_iota(jnp.int32, x.shape, x.ndim - 1)
        up = pltpu.roll(x, n - sh, x.ndim - 1)
        down = pltpu.roll(x, sh, x.ndim - 1)
        return jnp.where((lane & sh) == 0, up, down)

    f = jax.custom_vjp(raw)
    f.defvjp(lambda x: (raw(x), None), lambda _, ct: (raw(ct),))
    return f


_SWAP_ROPE_HALVES = _make_xor(ROT // 2)


def _head_sum(x):
    n = x.shape[-1]
    same_head = (lax.broadcasted_iota(jnp.int32, (n, n), 0) // HEAD) == (lax.broadcasted_iota(jnp.int32, (n, n), 1) // HEAD)
    return _fdot(x, same_head.astype(F32), NN)


def _rms(x, g):
    return x * lax.rsqrt(jnp.mean(x * x, axis=-1, keepdims=True) + EPS) * g


def _head_rms_rope(x, g, cos, sin, scale):
    y = x * lax.rsqrt(_head_sum(x * x) * (1.0 / HEAD) + EPS) * g
    return (y * cos + _SWAP_ROPE_HALVES(y) * sin) * scale


def _qk_fn(q, k, v, cos, sin, gq, gk):
    return (_head_rms_rope(q, gq, cos, sin, HEAD ** -0.5), _head_rms_rope(k, gk, cos, sin, 1.0), v)


def _norm_fn(x, g):
    return (_rms(x, g),)


def _merge_fn(o0, o1, o2, l0, l1, l2, g):
    m = lax.stop_gradient(jnp.maximum(jnp.maximum(l0, l1), l2))
    e0, e1, e2 = jnp.exp(l0 - m), jnp.exp(l1 - m), jnp.exp(l2 - m)
    mix = (e0 * o0 + e1 * o1 + e2 * o2) / (e0 + e1 + e2)
    return (_rms(mix, g),)


def _gate_fn(y, z, g):
    return (_rms(y * (z * jax.nn.sigmoid(z)), g),)


def _attn_pair(q, kc, vc, kp=None, vp=None, has_prev=None):
    pick0, pick1 = _head_picks()
    k_band, v_band, mask = _attn_band(kc, vc, kp, vp, has_prev)
    s = jnp.where(mask, _bdot(jnp.concatenate([q * pick0, q * pick1], axis=0), k_band, NT), NEG)
    m = jnp.max(s, axis=-1, keepdims=True)
    p = jnp.exp(s - m)
    den = jnp.sum(p, axis=-1, keepdims=True)
    acc = _bdot(p, v_band, NN) * (1.0 / den)
    lse_rows = m + jnp.log(den)
    o = pick0 * acc[:ATT_BLK] + pick1 * acc[ATT_BLK:]
    lse = pick0 * lse_rows[:ATT_BLK] + pick1 * lse_rows[ATT_BLK:]
    return o, lse


def _head_picks():
    lane = lax.broadcasted_iota(jnp.int32, (1, 2 * HEAD), 1)
    return (lane < HEAD).astype(F32), (lane >= HEAD).astype(F32)


def _attn_band(kc, vc, kp, vp, has_prev):
    n_keys = ATT_BLK if kp is None else 2 * ATT_BLK
    qi = lax.broadcasted_iota(jnp.int32, (2 * ATT_BLK, n_keys), 0) & (ATT_BLK - 1)
    kj = lax.broadcasted_iota(jnp.int32, (2 * ATT_BLK, n_keys), 1)
    if kp is None:
        return kc, vc, qi >= kj
    in_prev = jnp.logical_and(jnp.logical_and(kj < ATT_BLK, kj >= qi), has_prev)
    mask = jnp.logical_or(in_prev, jnp.logical_and(kj >= ATT_BLK, qi >= kj - ATT_BLK))
    return jnp.concatenate([kp, kc], axis=0), jnp.concatenate([vp, vc], axis=0), mask


def _attn_config(b):
    r = DILATIONS[b]
    return r, ATT_BLK * r, (D_ATTN if r == 1 else 128), BRANCH_BLOCKS[b] > 1


RESIDUES_UNROLLED = 16


def _for_residues(r, fn):
    if r <= RESIDUES_UNROLLED:
        for rho in range(r):
            fn(rho)
    else:
        def step(t, carry):
            for u in range(RESIDUES_UNROLLED):
                fn(RESIDUES_UNROLLED * t + u)
            return carry

        lax.fori_loop(0, r // RESIDUES_UNROLLED, step, 0)


def _strided_rows(start, r):
    if r > 1:
        return pl.ds(start, ATT_BLK, stride=r)
    return pl.ds(start if isinstance(start, int) else pl.multiple_of(start, ATT_BLK), ATT_BLK)


def _attention_fwd(qn, kn, vn, b):
    r, rows, lanes, with_prev = _attn_config(b)
    cur = pl.BlockSpec((rows, lanes), lambda g, n: (n, g))
    prev = pl.BlockSpec((rows, lanes), lambda g, n: (jnp.maximum(n - 1, 0), g))

    def body(*refs):
        ins, (o_ref, l_ref) = refs[:-2], refs[-2:]
        has_prev = pl.program_id(1) > 0

        def one(rho):
            sub = _strided_rows(rho, r)
            for pair in range(lanes // 128):
                sl = pl.ds(pair * 128, 128)
                args = [ref[sub, sl] for ref in ins] + ([has_prev] if with_prev else [])
                o_ref[sub, sl], l_ref[sub, sl] = _attn_pair(*args)

        _for_residues(r, one)

    operands = (qn, kn, vn, kn, vn) if with_prev else (qn, kn, vn)
    return pl.pallas_call(
        body, name="attn_fwd_%d" % r, grid=(D_ATTN // lanes, SEQ // rows),
        in_specs=[cur, cur, cur] + ([prev, prev] if with_prev else []), out_specs=[cur, cur],
        out_shape=[jax.ShapeDtypeStruct((SEQ, D_ATTN), F32)] * 2,
        compiler_params=_params(("parallel", "parallel")),
    )(*operands)


def _attn_pair_bwd(q, kc, vc, kp, vp, o, lse, do, dl, has_prev):
    pick0, pick1 = _head_picks()
    lane = lax.broadcasted_iota(jnp.int32, (1, 2 * HEAD), 1)
    k_band, v_band, mask = _attn_band(kc, vc, kp, vp, has_prev)
    q2 = jnp.concatenate([q * pick0, q * pick1], axis=0)
    do2 = jnp.concatenate([do * pick0, do * pick1], axis=0)
    lse2 = jnp.concatenate([jnp.sum(lse * (lane == 0).astype(F32), axis=-1, keepdims=True),
                            jnp.sum(lse * (lane == HEAD).astype(F32), axis=-1, keepdims=True)], axis=0)
    base = jnp.sum(jnp.concatenate([dl * pick0, dl * pick1], axis=0) - do2 * jnp.concatenate([o, o], axis=0),
                   axis=-1, keepdims=True)
    p = jnp.exp(jnp.where(mask, _bdot(q2, k_band, NT), NEG) - lse2)
    ds = p * (_bdot(do2, v_band, NT) + base)
    dq2 = _bdot(ds, k_band, NN)
    dq = pick0 * dq2[:ATT_BLK] + pick1 * dq2[ATT_BLK:]
    dk, dv = _bdot(ds, q2, TN), _bdot(p, do2, TN)
    if kp is None:
        return dq, dk, dv
    return dq, dk[ATT_BLK:], dv[ATT_BLK:], dk[:ATT_BLK], dv[:ATT_BLK]


def _attention_bwd(qn, kn, vn, o, lse, do, dl, b):
    r, rows, lanes, with_prev = _attn_config(b)
    cur = pl.BlockSpec((rows, lanes), lambda g, n: (n, g))
    prev = pl.BlockSpec((rows, lanes), lambda g, n: (jnp.maximum(n - 1, 0), g))
    whole = pl.BlockSpec((SEQ, lanes), lambda g, n: (0, g))
    n_in = 5 if with_prev else 3

    def body(*refs):
        ins, (o_ref, l_ref, do_ref, dl_ref, dq_ref, dk_ref, dv_ref) = refs[:n_in], refs[n_in:]
        n = pl.program_id(1)

        @pl.when(n == 0)
        def _():
            dk_ref[...] = jnp.zeros_like(dk_ref)
            dv_ref[...] = jnp.zeros_like(dv_ref)

        def one(rho):
            sub = _strided_rows(rho, r)
            sub_c = _strided_rows(n * rows + rho, r)
            sub_p = _strided_rows(jnp.maximum(n - 1, 0) * rows + rho, r)
            for pair in range(lanes // 128):
                sl = pl.ds(pair * 128, 128)
                vals = [ref[sub, sl] for ref in ins] + ([] if with_prev else [None, None])
                grads = _attn_pair_bwd(*vals, o_ref[sub, sl], l_ref[sub, sl], do_ref[sub, sl], dl_ref[sub, sl], n > 0)
                dq_ref[sub, sl] = grads[0]
                dk_ref[sub_c, sl] += grads[1]
                dv_ref[sub_c, sl] += grads[2]
                if with_prev:
                    dk_ref[sub_p, sl] += grads[3]
                    dv_ref[sub_p, sl] += grads[4]

        _for_residues(r, one)

    operands = (qn, kn, vn, kn, vn) if with_prev else (qn, kn, vn)
    return pl.pallas_call(
        body, name="attn_bwd_%d" % r, grid=(D_ATTN // lanes, SEQ // rows),
        in_specs=[cur, cur, cur] + ([prev, prev] if with_prev else []) + [cur] * 4, out_specs=[cur, whole, whole],
        out_shape=[jax.ShapeDtypeStruct((SEQ, D_ATTN), F32)] * 3,
        compiler_params=_params(("parallel", "arbitrary")),
    )(*operands, o, lse, do, dl)


CONV_COLS = 256
XBC_BLOCK0 = (3 * D_ATTN + D_SSM) // CONV_COLS


def _shift_rows(x, s):
    n = x.shape[0]
    t = lax.broadcasted_iota(jnp.int32, x.shape, 0)
    if s >= 0:
        return jnp.where(t >= s, pltpu.roll(x, s, 0), 0.0)
    return jnp.where(t < n + s, pltpu.roll(x, n + s, 0), 0.0)


def _conv_pre(x, w_ref, b_ref):
    delayed = [_shift_rows(x, 3 - k) for k in range(3)]
    pre = b_ref[...] + w_ref[3:4, :] * x
    for k in range(3):
        pre = pre + w_ref[k:k + 1, :] * delayed[k]
    return pre, delayed


def _conv_fwd(proj, conv_w, conv_b):
    cols = conv_w.shape[1]

    def body(x_ref, w_ref, b_ref, o_ref):
        pre, _ = _conv_pre(x_ref[...], w_ref, b_ref)
        o_ref[...] = pre * jax.nn.sigmoid(pre)

    blk = pl.BlockSpec((SEQ, CONV_COLS), lambda j: (0, j))
    return pl.pallas_call(
        body, name="conv_fwd", grid=(cols // CONV_COLS,),
        in_specs=[pl.BlockSpec((SEQ, CONV_COLS), lambda j: (0, XBC_BLOCK0 + j)),
                  pl.BlockSpec((4, CONV_COLS), lambda j: (0, j)), pl.BlockSpec((1, CONV_COLS), lambda j: (0, j))],
        out_specs=blk, out_shape=jax.ShapeDtypeStruct((SEQ, cols), F32),
        compiler_params=_params(("parallel",)),
    )(proj, conv_w, conv_b)


def _conv_bwd(proj, conv_w, conv_b, dxs, db, dc):
    cols = conv_w.shape[1]
    x_blocks, b_blocks = dxs.shape[1] // CONV_COLS, db.shape[1] // CONV_COLS

    def body(x_ref, w_ref, b_ref, dxs_ref, db_ref_in, dc_ref_in, dx_ref, dw_ref, db_ref):
        j = pl.program_id(0)
        dy = jnp.where(j < x_blocks, dxs_ref[...], jnp.where(j < x_blocks + b_blocks, db_ref_in[...], dc_ref_in[...]))
        x = x_ref[...]
        pre, delayed = _conv_pre(x, w_ref, b_ref)
        sg = jax.nn.sigmoid(pre)
        dpre = dy * (sg * (1.0 + pre * (1.0 - sg)))
        db_ref[...] = jnp.sum(dpre, axis=0, keepdims=True)
        dx = w_ref[3:4, :] * dpre
        dw_ref[3:4, :] = jnp.sum(dpre * x, axis=0, keepdims=True)
        for k in range(3):
            dx = dx + w_ref[k:k + 1, :] * _shift_rows(dpre, k - 3)
            dw_ref[k:k + 1, :] = jnp.sum(dpre * delayed[k], axis=0, keepdims=True)
        dw_ref[4:8, :] = jnp.zeros((4, CONV_COLS), F32)
        dx_ref[...] = dx.astype(dx_ref.dtype)

    blk = pl.BlockSpec((SEQ, CONV_COLS), lambda j: (0, j))
    parts = [pl.BlockSpec((SEQ, CONV_COLS), lambda j: (0, jnp.minimum(j, x_blocks - 1))),
             pl.BlockSpec((SEQ, CONV_COLS), lambda j: (0, jnp.clip(j - x_blocks, 0, b_blocks - 1))),
             pl.BlockSpec((SEQ, CONV_COLS), lambda j: (0, jnp.clip(j - x_blocks - b_blocks, 0, b_blocks - 1)))]
    return pl.pallas_call(
        body, name="conv_bwd", grid=(cols // CONV_COLS,),
        in_specs=[pl.BlockSpec((SEQ, CONV_COLS), lambda j: (0, XBC_BLOCK0 + j)),
                  pl.BlockSpec((4, CONV_COLS), lambda j: (0, j)), pl.BlockSpec((1, CONV_COLS), lambda j: (0, j))] + parts,
        out_specs=[blk, pl.BlockSpec((8, CONV_COLS), lambda j: (0, j)), pl.BlockSpec((1, CONV_COLS), lambda j: (0, j))],
        out_shape=[jax.ShapeDtypeStruct((SEQ, cols), BF16), jax.ShapeDtypeStruct((8, cols), F32),
                   jax.ShapeDtypeStruct((1, cols), F32)],
        compiler_params=_params(("parallel",)),
    )(proj, conv_w, conv_b, dxs, db, dc)


HEADS_PER_GROUP = 4


GROUP_WIDTH = HEADS_PER_GROUP * HEAD


def _ssd_chunk(x, bm, cm, dtr, bias, alog, dsk, h):
    row = lax.broadcasted_iota(jnp.int32, (CHUNK, CHUNK), 0)
    col = lax.broadcasted_iota(jnp.int32, (CHUNK, CHUNK), 1)
    causal = row >= col
    z = dtr + bias
    dt = jnp.maximum(z, 0.0) + jnp.log(1.0 + jnp.exp(-jnp.abs(z)))
    acs = _fdot(causal.astype(F32), dt * -jnp.exp(alog), NN)
    acs_t, dt_t = acs.T, dt.T
    cb = _bdot(cm, bm, NT)
    lane = lax.broadcasted_iota(jnp.int32, (1, CHUNK), 1)
    sub = lax.broadcasted_iota(jnp.int32, (CHUNK, 1), 0)
    wide = lax.broadcasted_iota(jnp.int32, (1, GROUP_WIDTH), 1) // HEAD
    tall = lax.broadcasted_iota(jnp.int32, (GROUP_WIDTH, 1), 0) // HEAD
    acs_last = jnp.sum(acs * (sub == CHUNK - 1).astype(F32), axis=0, keepdims=True)
    to_lanes = (lax.broadcasted_iota(jnp.int32, (CHUNK, GROUP_WIDTH), 0)
                == lax.broadcasted_iota(jnp.int32, (CHUNK, GROUP_WIDTH), 1) // HEAD).astype(F32)
    grow = _fdot(jnp.exp(acs), to_lanes, NN)
    keep = _fdot(jnp.exp(acs_last - acs) * dt, to_lanes, NN)
    w_parts, x_parts, skip, carry = [], [], 0.0, 0.0
    for j in range(HEADS_PER_GROUP):
        on_lane, on_sub = (lane == j).astype(F32), (sub == j).astype(F32)
        acs_c = jnp.sum(acs * on_lane, axis=1, keepdims=True)
        acs_r = jnp.sum(acs_t * on_sub, axis=0, keepdims=True)
        dt_r = jnp.sum(dt_t * on_sub, axis=0, keepdims=True)
        w_parts.append(cb * jnp.exp(jnp.where(causal, acs_c - acs_r, NEG)) * dt_r)
        x_parts.append(x * (wide == j).astype(F32))
        skip = skip + jnp.sum(dsk * on_lane, axis=1, keepdims=True) * (wide == j).astype(F32)
        carry = carry + jnp.sum(jnp.exp(acs_last) * on_lane, axis=1, keepdims=True) * (tall == j).astype(F32)
    y_diag = _bdot(jnp.concatenate(w_parts, axis=1), jnp.concatenate(x_parts, axis=0), NN)
    y = y_diag + _bdot(cm, h, NT) * grow + skip * x
    return y, h * carry + _bdot(x * keep, bm, TN)


GROUPS_PER_STEP = 4
SSD_STEPS = N_GROUPS // GROUPS_PER_STEP


def _ssd_specs(reverse):
    n_chunks = SEQ // CHUNK
    c_of = (lambda c: n_chunks - 1 - c) if reverse else (lambda c: c)
    x_w, n_w, dt_w = GROUPS_PER_STEP * GROUP_WIDTH, GROUPS_PER_STEP * N_STATE, GROUPS_PER_STEP * 128
    x_spec = pl.BlockSpec((CHUNK, x_w), lambda g, c: (c_of(c), g))
    b_spec = pl.BlockSpec((CHUNK, n_w), lambda g, c: (c_of(c), D_SSM // n_w + g))
    c_spec = pl.BlockSpec((CHUNK, n_w), lambda g, c: (c_of(c), (D_SSM + N_GROUPS * N_STATE) // n_w + g))
    dt_spec = pl.BlockSpec((CHUNK, dt_w), lambda g, c: (c_of(c), g))
    vec_spec = pl.BlockSpec((1, dt_w), lambda g, c: (0, g))
    h_spec = pl.BlockSpec((None, GROUPS_PER_STEP, GROUP_WIDTH, N_STATE), lambda g, c: (c_of(c), g, 0, 0))
    return x_spec, b_spec, c_spec, dt_spec, vec_spec, h_spec


def _group_slices(u):
    return pl.ds(u * GROUP_WIDTH, GROUP_WIDTH), pl.ds(u * N_STATE, N_STATE), pl.ds(u * 128, 128)


def _ssd_gated_chunk(x, bm, cm, dtr, bias, alog, dsk, h, z, g_out):
    y, h_new = _ssd_chunk(x, bm, cm, dtr, bias, alog, dsk, h)
    return _gate_fn(y, z, g_out)[0], h_new


def _ssd_gate_specs(reverse):
    x_spec = _ssd_specs(reverse)[0]
    z_block0 = 3 * D_ATTN // x_spec.block_shape[1]
    z_spec = pl.BlockSpec(x_spec.block_shape, lambda g, c: (x_spec.index_map(g, c)[0], z_block0 + g))
    return z_spec, pl.BlockSpec((1, x_spec.block_shape[1]), lambda g, c: (0, g))


def _ssd_fwd(xbc, dt_raw, bias, alog, dsk, proj, g_out):
    x_spec, b_spec, c_spec, dt_spec, vec_spec, h_spec = _ssd_specs(False)
    z_spec, g_spec = _ssd_gate_specs(False)

    def body(x_ref, b_ref, c_ref, dt_ref, bias_ref, alog_ref, dsk_ref, z_ref, g_ref, ssm_ref, hin_ref, h_scr):
        @pl.when(pl.program_id(1) == 0)
        def _():
            h_scr[...] = jnp.zeros_like(h_scr)

        for u in range(GROUPS_PER_STEP):
            xs, ns, ds = _group_slices(u)
            h = h_scr[u]
            hin_ref[u] = h
            ssm, h_scr[u] = _ssd_gated_chunk(x_ref[:, xs], b_ref[:, ns], c_ref[:, ns], dt_ref[:, ds], bias_ref[:, ds],
                                             alog_ref[:, ds], dsk_ref[:, ds], h, z_ref[:, xs], g_ref[:, xs])
            ssm_ref[:, xs] = ssm.astype(ssm_ref.dtype)

    return pl.pallas_call(
        body, name="ssd_fwd", grid=(SSD_STEPS, SEQ // CHUNK),
        in_specs=[x_spec, b_spec, c_spec, dt_spec, vec_spec, vec_spec, vec_spec, z_spec, g_spec],
        out_specs=[x_spec, h_spec],
        out_shape=[jax.ShapeDtypeStruct((SEQ, D_SSM), BF16),
                   jax.ShapeDtypeStruct((SEQ // CHUNK, N_GROUPS, GROUP_WIDTH, N_STATE), F32)],
        scratch_shapes=[pltpu.VMEM((GROUPS_PER_STEP, GROUP_WIDTH, N_STATE), F32)],
        compiler_params=_params(("parallel", "arbitrary")),
    )(xbc, xbc, xbc, dt_raw, bias, alog, dsk, proj, g_out)


def _ssd_bwd(xbc, dt_raw, bias, alog, dsk, h_in, proj, g_out, dmix):
    x_spec, b_spec, c_spec, dt_spec, vec_spec, h_spec = _ssd_specs(True)
    z_spec, g_spec = _ssd_gate_specs(True)
    ct_block0 = D_ATTN // x_spec.block_shape[1]
    ct_spec = pl.BlockSpec(x_spec.block_shape, lambda g, c: (x_spec.index_map(g, c)[0], ct_block0 + g))

    def body(x_ref, b_ref, c_ref, dt_ref, bias_ref, alog_ref, dsk_ref, hin_ref, z_ref, g_ref, ct_ref,
             dx_ref, db_ref, dc_ref, ddt_ref, dbias_ref, dalog_ref, ddsk_ref, dz_ref, dg_ref, dh_scr):
        first = pl.program_id(1) == 0

        @pl.when(first)
        def _():
            dh_scr[...] = jnp.zeros_like(dh_scr)

        for u in range(GROUPS_PER_STEP):
            xs, ns, ds = _group_slices(u)
            _, pullback = jax.vjp(_ssd_gated_chunk, x_ref[:, xs], b_ref[:, ns], c_ref[:, ns], dt_ref[:, ds], bias_ref[:, ds],
                                  alog_ref[:, ds], dsk_ref[:, ds], hin_ref[u], z_ref[:, xs], g_ref[:, xs])
            g = pullback((ct_ref[:, xs], dh_scr[u]))
            dx_ref[:, xs], db_ref[:, ns], dc_ref[:, ns] = g[0], g[1], g[2]
            ddt_ref[:, ds] = g[3].astype(ddt_ref.dtype)
            dh_scr[u] = g[7]
            dz_ref[:, xs] = g[8].astype(dz_ref.dtype)
            sums = ((dbias_ref, g[4], ds), (dalog_ref, g[5], ds), (ddsk_ref, g[6], ds),
                    (dg_ref, jnp.sum(g[9], axis=0, keepdims=True), xs))
            for o_ref, val, lanes in sums:
                @pl.when(first)
                def _(o_ref=o_ref, val=val, lanes=lanes):
                    o_ref[:, lanes] = val

                @pl.when(jnp.logical_not(first))
                def _(o_ref=o_ref, val=val, lanes=lanes):
                    o_ref[:, lanes] += val

    n_chunks = SEQ // CHUNK
    out_b = pl.BlockSpec((CHUNK, GROUPS_PER_STEP * N_STATE), lambda g, c: (n_chunks - 1 - c, g))
    return pl.pallas_call(
        body, name="ssd_bwd", grid=(SSD_STEPS, n_chunks),
        in_specs=[x_spec, b_spec, c_spec, dt_spec, vec_spec, vec_spec, vec_spec, h_spec, z_spec, g_spec, ct_spec],
        out_specs=[x_spec, out_b, out_b, dt_spec, vec_spec, vec_spec, vec_spec, x_spec, g_spec],
        out_shape=[jax.ShapeDtypeStruct((SEQ, D_SSM), F32), jax.ShapeDtypeStruct((SEQ, N_GROUPS * N_STATE), F32),
                   jax.ShapeDtypeStruct((SEQ, N_GROUPS * N_STATE), F32), jax.ShapeDtypeStruct((SEQ, DT_PAD), BF16),
                   jax.ShapeDtypeStruct((1, DT_PAD), F32), jax.ShapeDtypeStruct((1, DT_PAD), F32),
                   jax.ShapeDtypeStruct((1, DT_PAD), F32), jax.ShapeDtypeStruct((SEQ, D_SSM), BF16),
                   jax.ShapeDtypeStruct((1, D_SSM), F32)],
        scratch_shapes=[pltpu.VMEM((GROUPS_PER_STEP, GROUP_WIDTH, N_STATE), F32)],
        compiler_params=_params(("parallel", "arbitrary")),
    )(xbc, xbc, xbc, dt_raw, bias, alog, dsk, h_in, proj, g_out, dmix)


CROSS_HEAD = 128
CROSS_ROWS = 1024


def _cross_head(q, k, v, gq, gk):
    qn = _rms(q, gq) * (CROSS_HEAD ** -0.5)
    kn = _rms(k, gk)
    s = _bdot(qn, kn, NT)
    p = jnp.exp(s - lax.stop_gradient(jnp.max(s, axis=-1, keepdims=True)))
    return _bdot(p, v, NN) * (1.0 / jnp.sum(p, axis=-1, keepdims=True))


def _cross_specs():
    q_spec = pl.BlockSpec((CROSS_ROWS, CROSS_HEAD), lambda h, i: (i, h))
    k_spec = pl.BlockSpec((N_MEM, CROSS_HEAD), lambda h, i: (0, h))
    v_spec = pl.BlockSpec((N_MEM, CROSS_HEAD), lambda h, i: (0, 4 + h))
    g_spec = pl.BlockSpec((1, CROSS_HEAD), lambda h, i: (0, 0))
    return q_spec, k_spec, v_spec, g_spec


def _cross_fwd(qc, kv, gq, gk):
    q_spec, k_spec, v_spec, g_spec = _cross_specs()

    def body(q_ref, k_ref, v_ref, gq_ref, gk_ref, o_ref):
        o_ref[...] = _cross_head(q_ref[...], k_ref[...], v_ref[...], gq_ref[...], gk_ref[...]).astype(o_ref.dtype)

    return pl.pallas_call(
        body, name="cross_fwd", grid=(4, SEQ // CROSS_ROWS),
        in_specs=[q_spec, k_spec, v_spec, g_spec, g_spec], out_specs=q_spec,
        out_shape=jax.ShapeDtypeStruct((SEQ, D_CROSS), BF16),
        compiler_params=_params(("parallel", "parallel")),
    )(qc, kv, kv, gq, gk)


def _cross_bwd(qc, kv, gq, gk, do):
    q_spec, k_spec, v_spec, g_spec = _cross_specs()

    def body(q_ref, k_ref, v_ref, gq_ref, gk_ref, do_ref, dq_ref, dk_ref, dv_ref, dgq_ref, dgk_ref):
        _, pullback = jax.vjp(_cross_head, q_ref[...], k_ref[...], v_ref[...], gq_ref[...], gk_ref[...])
        dq, dk, dv, dgq, dgk = pullback(do_ref[...].astype(F32))
        dq_ref[...] = dq.astype(dq_ref.dtype)
        row0 = pl.program_id(1) == 0
        all0 = jnp.logical_and(row0, pl.program_id(0) == 0)
        for o_ref, val, init in ((dk_ref, dk, row0), (dv_ref, dv, row0), (dgq_ref, dgq, all0), (dgk_ref, dgk, all0)):
            @pl.when(init)
            def _(o_ref=o_ref, val=val):
                o_ref[...] = val

            @pl.when(jnp.logical_not(init))
            def _(o_ref=o_ref, val=val):
                o_ref[...] += val

    return pl.pallas_call(
        body, name="cross_bwd", grid=(4, SEQ // CROSS_ROWS),
        in_specs=[q_spec, k_spec, v_spec, g_spec, g_spec, q_spec],
        out_specs=[q_spec, k_spec, k_spec, g_spec, g_spec],
        out_shape=[jax.ShapeDtypeStruct((SEQ, D_CROSS), BF16), jax.ShapeDtypeStruct((N_MEM, D_CROSS), F32),
                   jax.ShapeDtypeStruct((N_MEM, D_CROSS), F32), jax.ShapeDtypeStruct((1, CROSS_HEAD), F32),
                   jax.ShapeDtypeStruct((1, CROSS_HEAD), F32)],
        compiler_params=_params(("arbitrary", "arbitrary")),
    )(qc, kv, kv, gq, gk, do)


def _loss_epilogue(acc, residual, target):
    err = acc + residual - target
    dy = err * (1.0 / D_MODEL)
    part = jnp.sum(jnp.sum(err * err, axis=1, keepdims=True), axis=0, keepdims=True) * (0.5 / D_MODEL)
    return dy, dy, part


def _pad_heads(v):
    return jnp.pad(v.reshape(N_GROUPS, HEADS_PER_GROUP), ((0, 0), (0, 128 - HEADS_PER_GROUP))).reshape(1, DT_PAD)


def _unpad_heads(v):
    return v.reshape(v.shape[0], N_GROUPS, 128)[:, :, :HEADS_PER_GROUP].reshape(v.shape[0], N_DT)


def _rope_tables(positions):
    half = ROT // 2
    inv_freq = ROPE_THETA ** (-2.0 * jnp.arange(half, dtype=F32) / ROT)
    ang = positions.reshape(SEQ, 1).astype(F32) * inv_freq
    cos, sin = jnp.cos(ang), jnp.sin(ang)
    ones, zeros = jnp.ones((SEQ, HEAD - ROT), F32), jnp.zeros((SEQ, HEAD - ROT), F32)
    cos_h = jnp.concatenate([cos, cos, ones], axis=1)
    sin_h = jnp.concatenate([-sin, sin, zeros], axis=1)
    return jnp.tile(cos_h, (1, 2)), jnp.tile(sin_h, (1, 2))


def _add_res(acc, res):
    return (acc + res,)


def _norm_bwd_epilogue(acc, x, residual, *more):
    *part, g = more
    ct = acc + part[0] if part else acc
    _, pullback = jax.vjp(_rms, x, g)
    dx, dg = pullback(ct)
    return dx + residual, dg


def _add_res_and_norm(acc, res, g):
    y = acc + res
    return y, _rms(y, g)


def _settle(grads, *after):
    if hasattr(grads, "settle"):
        grads.settle(*after)


def _take_token(grads):
    token = getattr(grads, "token", None)
    if token is None:
        return ()
    grads.token = None
    return (token,)


def _local_step(x, mem, positions, target, p, w, more_weights=None, grads=None, h=None):
    grads = {} if grads is None else grads
    w = dict(w)
    cos, sin = _rope_tables(positions)
    gq2, gk2 = jnp.tile(p["g_q"], (1, 2)), jnp.tile(p["g_k"], (1, 2))
    bias, alog, dsk = _pad_heads(p["dt_bias"]), _pad_heads(p["a_log"]), _pad_heads(p["d_skip"])
    norm_out = [(D_MODEL, BF16, D_MODEL, 0, False)]

    if h is None:
        h = _rowwise(_norm_fn, [_full(x)], [_full(p["g_mix"])], norm_out, name="norm_in")[0]
    proj = _matmul(h, w["w_in"], mode="nn", name="in_proj", outs=[F32], n_cols=D_MAIN)
    dt_raw = _matmul(h, w["w_dt"], mode="nn", name="dt_proj", outs=[F32])
    pairs = D_ATTN // 128
    qk_rows = [(proj, 128, 0, True), (proj, 128, pairs, True), (proj, 128, 2 * pairs, True), _full(cos), _full(sin)]
    qk_vecs = [_full(gq2), _full(gk2)]
    qn, kn, vn = _rowwise(_qk_fn, qk_rows, qk_vecs, [(D_ATTN, F32, 128, 0, True)] * 3, name="qk_prep", groups=8, tr=1024)
    branches = [_attention_fwd(qn, kn, vn, b) for b in range(3)]
    merge_rows = [_full(o) for o, _ in branches] + [_full(lse) for _, lse in branches]
    attn = _rowwise(_merge_fn, merge_rows, [_full(p["g_attn_out"])], [(D_ATTN, BF16, D_ATTN, 0, False)], name="attn_merge")[0]
    xbc = _conv_fwd(proj, p["conv_w"], p["conv_b"])
    ssm, h_in = _ssd_fwd(xbc, dt_raw, bias, alog, dsk, proj, p["g_ssm_out"])
    mix = jnp.concatenate([attn, ssm], axis=1)
    if more_weights is not None:
        w.update(more_weights("mixer_done", mix))
    x1, hc = _matmul(mix, w["w_out"], mode="nn", name="out_proj", outs=[F32, BF16], extra=(x,), vecs=(p["g_cross"],),
                     epilogue=_add_res_and_norm, tm=512, tn=D_MODEL)
    memh = _rowwise(_norm_fn, [_full(mem)], [_full(p["g_mem"])], norm_out, name="norm_mem", n_rows=N_MEM, tr=N_MEM)[0]
    qc = _matmul(hc, w["w_cq"], mode="nn", name="cq_proj", outs=[F32])
    if more_weights is not None:
        w.update(more_weights("cross_started", qc))
    kv = _matmul(memh, w["w_ckv"], mode="nn", name="ckv_proj", outs=[F32])
    oc = _cross_fwd(qc, kv, p["g_cq"], p["g_ck"])
    x2, hm = _matmul(oc, w["w_co"], mode="nn", name="co_proj", outs=[F32, BF16], extra=(x1,), vecs=(p["g_mlp"],),
                     epilogue=_add_res_and_norm, tm=512, tn=D_MODEL)
    if more_weights is not None:
        w.update(more_weights("cross_done", hm))
    u, act = _matmul(hm, w["w_up"], mode="nn", name="up_proj", outs=[F32, BF16],
                     epilogue=lambda acc: (acc, jnp.square(jnp.maximum(acc, 0.0))))
    dy, dyb, loss_tiles = _matmul(act, w["w_down"], mode="nn", name="down_proj", outs=[F32, BF16], extra=(x2, target),
                                  epilogue=_loss_epilogue, tile_sums=1)
    loss = jnp.sum(loss_tiles).reshape(1, 1)

    grads["w_down"] = _matmul(act, dyb, mode="tn", name="dw_down", outs=[BF16], after=_take_token(grads))
    du = _matmul(dyb, w["w_down"], mode="nt", name="d_act", outs=[BF16], extra=(u,), after=_take_token(grads),
                 epilogue=lambda acc, uu: (acc * (2.0 * jnp.maximum(uu, 0.0)),))
    _settle(grads, du)
    grads["w_up"] = _matmul(hm, du, mode="tn", name="dw_up", outs=[BF16], col_shards=4, after=_take_token(grads))
    dx2, grads["g_mlp"] = _matmul(du, w["w_up"], mode="nt", name="d_hm", outs=[F32], extra=(x2, dy), vecs=(p["g_mlp"],),
                                  epilogue=_norm_bwd_epilogue, tile_rows=1, after=_take_token(grads), tm=512, tn=D_MODEL,
                                  tk=1024)
    _settle(grads, dx2)
    grads["w_co"] = _matmul(oc, dx2, mode="tn", name="dw_co", outs=[BF16], col_shards=4, after=_take_token(grads))
    doc = _matmul(dx2, w["w_co"], mode="nt", name="d_oc", outs=[BF16])
    dqc, dkc, dvc, grads["g_cq"], grads["g_ck"] = _cross_bwd(qc, kv, p["g_cq"], p["g_ck"], doc)
    grads["w_cq"] = _matmul(hc, dqc, mode="tn", name="dw_cq", outs=[BF16])
    dkv = jnp.concatenate([dkc, dvc], axis=1)
    grads["w_ckv"] = _matmul(memh, dkv, mode="tn", name="dw_ckv", outs=[BF16])
    dmemh = _matmul(dkv, w["w_ckv"], mode="nt", name="d_memh", outs=[F32])
    grads["g_mem"] = _rowwise_vjp(_norm_fn, [_full(mem)], [_full(p["g_mem"])], [[_full(dmemh)]], [],
                                  [(0, D_MODEL, D_MODEL, 0, False)], name="norm_mem_bwd", n_rows=N_MEM, tr=N_MEM)[0]
    dx1, grads["g_cross"] = _matmul(dqc, w["w_cq"], mode="nt", name="d_hc", outs=[F32], extra=(x1, dx2), vecs=(p["g_cross"],),
                                    epilogue=_norm_bwd_epilogue, tile_rows=1, tm=512, tn=D_MODEL)
    grads["w_out"] = _matmul(mix, dx1, mode="tn", name="dw_out", outs=[BF16])
    dmix = _matmul(dx1, w["w_out"], mode="nt", name="d_mix", outs=[F32], after=_take_token(grads))
    _settle(grads, dmix)
    merge_grads = [(i, D_ATTN, F32, D_ATTN, 0, False, None) for i in range(6)]
    *dol, grads["g_attn_out"] = _rowwise_vjp(
        _merge_fn, merge_rows, [_full(p["g_attn_out"])], [[(dmix, D_ATTN, 0, False)]],
        merge_grads, [(0, D_ATTN, D_ATTN, 0, False)], name="attn_merge_bwd", tr=256, after=_take_token(grads))
    dqkv = [_attention_bwd(qn, kn, vn, *branches[b], dol[b], dol[3 + b], b) for b in range(3)]
    qk_cts = [[(dqkv[b][i], 128, 0, True) for b in range(3)] for i in range(3)]
    dq, dk, dv, dgq2, dgk2 = _rowwise_vjp(
        _qk_fn, qk_rows, qk_vecs, qk_cts, [(i, D_ATTN, BF16, 128, 0, True, None) for i in range(3)],
        [(0, 128, 128, 0, False), (1, 128, 128, 0, False)], name="qk_prep_bwd", groups=8, tr=1024)
    grads["g_q"] = dgq2[:, :HEAD] + dgq2[:, HEAD:]
    grads["g_k"] = dgk2[:, :HEAD] + dgk2[:, HEAD:]
    dxs, db, dc, ddt, dbias, dalog, ddsk, dz, grads["g_ssm_out"] = _ssd_bwd(xbc, dt_raw, bias, alog, dsk, h_in, proj,
                                                                             p["g_ssm_out"], dmix)
    grads["dt_bias"], grads["a_log"], grads["d_skip"] = _unpad_heads(dbias), _unpad_heads(dalog), _unpad_heads(ddsk)
    dxbc_raw, dconv_w, grads["conv_b"] = _conv_bwd(proj, p["conv_w"], p["conv_b"], dxs, db, dc)
    grads["conv_w"] = dconv_w[:4]
    dproj = jnp.concatenate([dq, dk, dv, dz, dxbc_raw], axis=1)
    grads["w_main"] = _matmul(h, dproj, mode="tn", name="dw_main", outs=[BF16], out_cols=D_MAIN + N_DT)
    grads["w_dt"] = _matmul(h, ddt, mode="tn", name="dw_dt", outs=[BF16])
    dh = _matmul(dproj, w["w_in"], mode="nt", name="d_h_main", outs=[F32], after=_take_token(grads))
    grad_x, grads["g_mix"] = _matmul(ddt, w["w_dt"], mode="nt", name="d_h_dt", outs=[F32], extra=(x, dx1, dh),
                                     vecs=(p["g_mix"],), epilogue=_norm_bwd_epilogue, tile_rows=1, tm=512, tn=D_MODEL)
    return loss, grad_x, grads


MATRICES = ("w_in", "w_out", "w_cq", "w_ckv", "w_co", "w_up", "w_down")
ROW_SHARDED = ("w_out", "w_cq", "w_ckv", "w_down")
N_CHIPS = 4
ANY = pl.BlockSpec(memory_space=pl.ANY)


def _place():
    return lax.axis_index("x"), lax.axis_index("y"), lax.axis_index("c")


def _other_chips(x, y):
    return [(1 - x, y), (x, 1 - y), (1 - x, 1 - y)]


def _remote(src, dst, send_sem, recv_sem, device):
    return pltpu.make_async_remote_copy(src_ref=src, dst_ref=dst, send_sem=send_sem, recv_sem=recv_sem,
                                        device_id=device, device_id_type=MESH)


def _gathered_shape(name, shard):
    rows, cols = shard.shape
    if name == "w_in":
        return (N_CHIPS, rows, cols)
    return (N_CHIPS * rows, cols) if name in ROW_SHARDED else (rows, N_CHIPS * cols)


def _shard_window(name, ref, rows, cols, chip, half, piece=(0, 1)):
    r0, nr = (0, rows) if half is None else (half * (rows // 2), rows // 2)
    nr = nr // piece[1]
    r0 = r0 + piece[0] * nr
    if name == "w_in":
        return ref.at[chip, pl.ds(r0, nr), :]
    if name in ROW_SHARDED:
        return ref.at[pl.ds(chip * rows + r0, nr), :]
    return ref.at[pl.ds(r0, nr), pl.ds(pl.multiple_of(chip * cols, 128), cols)]


def _cast_into_gathered(w, name, chip, after=()):
    rows, cols = w.shape
    tr = _tile(rows, ROW_TILE)

    def body(chip_ref, w_ref, *rest):
        rest[-1][...] = w_ref[...].astype(BF16)

    if name == "w_in":
        out_spec = pl.BlockSpec((None, tr, cols), lambda i, chip_ref: (chip_ref[0], i, 0))
    elif name in ROW_SHARDED:
        out_spec = pl.BlockSpec((tr, cols), lambda i, chip_ref: (chip_ref[0] * (rows // tr) + i, 0))
    else:
        out_spec = pl.BlockSpec((tr, cols), lambda i, chip_ref: (i, chip_ref[0]))
    grid_spec = pltpu.PrefetchScalarGridSpec(
        num_scalar_prefetch=1, grid=(rows // tr,),
        in_specs=[pl.BlockSpec((tr, cols), lambda i, chip_ref: (i, 0))] + [pl.BlockSpec(memory_space=pl.ANY)] * len(after),
        out_specs=out_spec)
    return pl.pallas_call(body, name="cast_" + name, grid_spec=grid_spec,
                          out_shape=jax.ShapeDtypeStruct(_gathered_shape(name, w), BF16),
                          compiler_params=_params(("parallel",)))(chip.reshape(1).astype(jnp.int32), w, *after)


def _w_in_columns(arr, to_shards):
    rows, piece = D_MODEL, (D_MAIN + N_DT) // N_CHIPS
    tr = ROW_TILE

    def body(a_ref, o_ref):
        for j in range(N_CHIPS):
            if to_shards:
                o_ref[j] = a_ref[:, pl.ds(piece * j, piece)]
            else:
                o_ref[:, pl.ds(piece * j, piece)] = a_ref[j]

    pieces = pl.BlockSpec((N_CHIPS, tr, piece), lambda i: (0, i, 0))
    matrix = pl.BlockSpec((tr, N_CHIPS * piece), lambda i: (i, 0))
    out_dims = (N_CHIPS, rows, piece) if to_shards else (rows, N_CHIPS * piece)
    return pl.pallas_call(
        body, name="w_in_to_shards" if to_shards else "w_in_from_shards", grid=(rows // tr,),
        in_specs=[matrix if to_shards else pieces], out_specs=pieces if to_shards else matrix,
        out_shape=jax.ShapeDtypeStruct(out_dims, arr.dtype), compiler_params=_params(("parallel",)))(arr)


HBM = pl.BlockSpec(memory_space=pltpu.HBM)
SEM = pl.BlockSpec(memory_space=pltpu.SEMAPHORE)
EFFECT = pltpu.SideEffectType.DATAFLOW_SIDE_EFFECTING


def _split_start(name, bufs, plan, counts, after=()):
    n, n_g, n_after = len(bufs), len(counts), len(after)

    def body(*refs):
        ins, sems, token = refs[:n], refs[n + n_after:n + n_after + 2 * n_g], refs[-1]
        for g, copies in enumerate(plan(ins)):
            for i, (src, dst, device, _) in enumerate(copies):
                _remote(src, dst, sems[2 * g].at[i], sems[2 * g + 1].at[i], device).start()
        token[...] = jnp.zeros_like(token)

    sem_shapes = [pltpu.SemaphoreType.DMA((cnt,)) for cnt in counts for _ in range(2)]
    res = pl.pallas_call(
        body, name=name,
        out_shape=(*sem_shapes, *[pltpu.HBM(b.shape, b.dtype) for b in bufs], jax.ShapeDtypeStruct((8, 128), F32)),
        in_specs=(*(HBM,) * n, *(ANY,) * n_after),
        out_specs=(*(SEM,) * (2 * n_g), *(HBM,) * n, pl.BlockSpec(memory_space=pltpu.VMEM)),
        input_output_aliases={i: 2 * n_g + i for i in range(n)},
        compiler_params=pltpu.CompilerParams(has_side_effects=EFFECT),
    )(*[pltpu.with_memory_space_constraint(b, pltpu.HBM) for b in bufs], *after)
    sems = [(res[2 * g], res[2 * g + 1]) for g in range(n_g)]
    return sems, list(res[2 * n_g:2 * n_g + n]), res[-1]


def _split_wait(name, bufs, sems, plan, *after):
    n = len(bufs)

    def body(*refs):
        ins, send, recv = refs[:n], refs[n], refs[n + 1]
        (copies,) = plan(ins)
        for i, (src, _, device, landing) in enumerate(copies):
            cp = _remote(src, landing, send.at[i], recv.at[i], device)
            cp.wait_send()
            cp.wait_recv()

    res = pl.pallas_call(
        body, name=name, out_shape=tuple(pltpu.HBM(b.shape, b.dtype) for b in bufs),
        in_specs=(*(HBM,) * n, SEM, SEM, *(ANY,) * len(after)), out_specs=(HBM,) * n,
        input_output_aliases={i: i for i in range(n)},
        compiler_params=pltpu.CompilerParams(has_side_effects=EFFECT),
    )(*bufs, sems[0], sems[1], *after)
    return list(res)


def _ici_plan(names, shard_shapes, piece=(0, 1)):
    def plan(refs):
        x, y, c = _place()
        copies = []
        for ref, name in zip(refs, names):
            win = _shard_window(name, ref, *shard_shapes[name], 2 * x + y, c, piece)
            for px, py in _other_chips(x, y):
                copies.append((win, win, (px, py, c), _shard_window(name, ref, *shard_shapes[name], 2 * px + py, c, piece)))
        return [copies]
    return plan


def _pass_on_plan(names, shard_shapes, piece=(0, 1)):
    def plan(refs):
        x, y, c = _place()
        copies = []
        for ref, name in zip(refs, names):
            for px, py in _other_chips(x, y):
                win = _shard_window(name, ref, *shard_shapes[name], 2 * px + py, c, piece)
                copies.append((win, win, (x, y, 1 - c),
                               _shard_window(name, ref, *shard_shapes[name], 2 * px + py, 1 - c, piece)))
        return [copies]
    return plan


def _swap_plan(n_pairs):
    def plan(refs):
        x, y, c = _place()
        return [[(src.at[:, 1 - c], dst, (x, y, 1 - c), dst) for src, dst in zip(refs[:n_pairs], refs[n_pairs:])]]
    return plan


def _share_plan(n_pairs):
    def plan(refs):
        x, y, c = _place()
        return [[(src, dst, (x, y, 1 - c), dst)] for src, dst in zip(refs[:n_pairs], refs[n_pairs:])]
    return plan


def _scatter_plan(n_pairs):
    def plan(refs):
        x, y, c = _place()
        copies = []
        for src, dst in zip(refs[:n_pairs], refs[n_pairs:]):
            for k, (px, py) in enumerate(_other_chips(x, y)):
                copies.append((src.at[2 * px + py], dst.at[k], (px, py, c), dst.at[k]))
        return [copies]
    return plan


def _sibling_swap(arrs, name):
    n = len(arrs)

    def body(*refs):
        ins, outs, send, recv = refs[:n], refs[n:2 * n], refs[2 * n], refs[2 * n + 1]
        x, y, c = _place()
        cps = [_remote(ins[w].at[:, 1 - c], outs[w], send.at[w], recv.at[w], (x, y, 1 - c)) for w in range(n)]
        for cp in cps:
            cp.start()
        for cp in cps:
            cp.wait()

    return pl.pallas_call(
        body, name=name, in_specs=[ANY] * n, out_specs=[ANY] * n,
        out_shape=[jax.ShapeDtypeStruct((a.shape[0],) + a.shape[2:], a.dtype) for a in arrs],
        scratch_shapes=[pltpu.SemaphoreType.DMA((n,))] * 2,
    )(*arrs)


def _small_allreduce(buf, name, after=()):
    rows = buf.shape[0]

    def body(x_ref, *rest):
        out_ref, all_ref, send_sems, recv_sems, local_sem = rest[len(after):]
        x, y, c = _place()
        me, sibling, chips = (x, y, c), (x, y, 1 - c), _other_chips(x, y)

        def block(px, py, pc):
            return all_ref.at[pl.ds((4 * px + 2 * py + pc) * rows, rows), :]

        def copy(k, blk, to, src=None):
            return _remote(block(*blk) if src is None else src, block(*blk), send_sems.at[k], recv_sems.at[k], to)

        own = pltpu.make_async_copy(x_ref, block(*me), local_sem)
        own.start()
        first = [copy(0, me, sibling, src=x_ref)] + [copy(1 + j, me, (*chip, c), src=x_ref) for j, chip in enumerate(chips)]
        for cp in first:
            cp.start()
        passed = [copy(4 + j, (*chip, c), sibling) for j, chip in enumerate(chips)]
        for j, chip in enumerate(chips):
            copy(1 + j, (*chip, c), me).wait_recv()
            passed[j].start()
        copy(0, sibling, me).wait_recv()
        for j, chip in enumerate(chips):
            copy(4 + j, (*chip, 1 - c), me).wait_recv()
        for cp in first + passed:
            cp.wait_send()
        own.wait()
        acc = all_ref[pl.ds(0, rows), :]
        for d in range(1, 8):
            acc = acc + all_ref[pl.ds(d * rows, rows), :]
        out_ref[...] = acc

    vmem = pl.BlockSpec(memory_space=pltpu.VMEM)
    return pl.pallas_call(
        body, name=name, in_specs=[vmem] + [ANY] * len(after), out_specs=vmem,
        out_shape=jax.ShapeDtypeStruct(buf.shape, F32),
        scratch_shapes=[pltpu.VMEM((8 * rows, 128), F32), pltpu.SemaphoreType.DMA((7,)), pltpu.SemaphoreType.DMA((7,)),
                        pltpu.SemaphoreType.DMA],
    )(buf, *after)


ROW_TILE = 256
BIG_ROW_TILE = 1024


def _add_halves(arr, recv, c, name):
    _, _, hr, cols = arr.shape
    tr = _tile(hr, BIG_ROW_TILE)

    def body(c_ref, a_ref, r_ref, o_ref):
        o_ref[...] = (a_ref[...].astype(F32) + r_ref[...].astype(F32)).astype(o_ref.dtype)

    piece = pl.BlockSpec((None, tr, cols), lambda j, i, c_ref: (j, i, 0))
    grid_spec = pltpu.PrefetchScalarGridSpec(
        num_scalar_prefetch=1, grid=(N_CHIPS, hr // tr),
        in_specs=[pl.BlockSpec((None, None, tr, cols), lambda j, i, c_ref: (j, c_ref[0], i, 0)), piece], out_specs=piece)
    return pl.pallas_call(body, name=name, grid_spec=grid_spec, out_shape=jax.ShapeDtypeStruct(recv.shape, BF16),
                          compiler_params=_params(("parallel", "parallel")))(c.reshape(1).astype(jnp.int32), arr, recv)


def _flip_slot(d):
    return jnp.where(d == 1, 1, jnp.where(d == 3, 2, 0))


def _sum_chips(p, q, chip, name):
    _, hr, cols = p.shape
    tr = _tile(hr, BIG_ROW_TILE)

    def body(chip_ref, p_ref, q_ref, o_ref):
        j = pl.program_id(1)
        term = jnp.where(j == chip_ref[0], p_ref[...].astype(F32), q_ref[...].astype(F32))

        @pl.when(j == 0)
        def _():
            o_ref[...] = term

        @pl.when(j != 0)
        def _():
            o_ref[...] += term

    grid_spec = pltpu.PrefetchScalarGridSpec(
        num_scalar_prefetch=1, grid=(hr // tr, N_CHIPS),
        in_specs=[pl.BlockSpec((None, tr, cols), lambda i, j, chip_ref: (chip_ref[0], i, 0)),
                  pl.BlockSpec((None, tr, cols), lambda i, j, chip_ref: (_flip_slot(j ^ chip_ref[0]), i, 0))],
        out_specs=pl.BlockSpec((tr, cols), lambda i, j, chip_ref: (i, 0)))
    return pl.pallas_call(body, name=name, grid_spec=grid_spec, out_shape=jax.ShapeDtypeStruct((hr, cols), F32),
                          compiler_params=_params(("parallel", "arbitrary")))(chip.reshape(1).astype(jnp.int32), p, q)


def _adamw_halves(w, g_own, g_other, m, v, c, name):
    rows, cols = w.shape
    tr = _tile(rows // 2, ROW_TILE)
    per_half = rows // 2 // tr

    def body(c_ref, w_ref, own_ref, other_ref, m_ref, v_ref, g_ref, d_ref, nm_ref, nv_ref):
        mine = (pl.program_id(0) // per_half) == c_ref[0]
        g_ = jnp.where(mine, own_ref[...], other_ref[...])
        g_ref[...] = g_
        d_ref[...], nm_ref[...], nv_ref[...] = _adamw_math(w_ref[...], g_, m_ref[...], v_ref[...])

    blk = pl.BlockSpec((tr, cols), lambda i, c_ref: (i, 0))
    own = pl.BlockSpec((tr, cols), lambda i, c_ref: (jnp.where(i // per_half == c_ref[0], i % per_half, 0), 0))
    other = pl.BlockSpec((tr, cols), lambda i, c_ref: (jnp.where(i // per_half == c_ref[0], 0, i % per_half), 0))
    grid_spec = pltpu.PrefetchScalarGridSpec(num_scalar_prefetch=1, grid=(rows // tr,),
                                             in_specs=[blk, own, other, blk, blk], out_specs=[blk] * 4)
    return pl.pallas_call(body, name=name, grid_spec=grid_spec, out_shape=[jax.ShapeDtypeStruct(w.shape, F32)] * 4,
                          compiler_params=_params(("parallel",)))(c.reshape(1).astype(jnp.int32), w, g_own, g_other, m, v)


W_IN_COLS = (D_MAIN + N_DT) // N_CHIPS
W_IN_MAIN = W_IN_COLS // 128 * 128
W_IN_TAIL = W_IN_COLS - 128
W_IN_PARTS = ((0, W_IN_MAIN), (W_IN_TAIL, 128))


def _cast_w_in_transposed(w_t, chip, after=()):
    def body(chip_ref, w_ref, *rest):
        for start, size in W_IN_PARTS:
            rest[-1][:, pl.ds(start, size)] = w_ref[pl.ds(start, size), :].T.astype(BF16)

    grid_spec = pltpu.PrefetchScalarGridSpec(
        num_scalar_prefetch=1, grid=(D_MODEL // ROW_TILE,),
        in_specs=[pl.BlockSpec((W_IN_COLS, ROW_TILE), lambda i, chip_ref: (0, i))] + [pl.BlockSpec(memory_space=pl.ANY)] * len(after),
        out_specs=pl.BlockSpec((None, ROW_TILE, W_IN_COLS), lambda i, chip_ref: (chip_ref[0], i, 0)))
    return pl.pallas_call(body, name="cast_w_in", grid_spec=grid_spec,
                          out_shape=jax.ShapeDtypeStruct((N_CHIPS, D_MODEL, W_IN_COLS), BF16),
                          compiler_params=_params(("parallel",)))(chip.reshape(1).astype(jnp.int32), w_t, *after)


def _adamw_w_in_transposed(w_t, g_own, g_other, m_t, v_t, c):
    per_half = D_MODEL // 2 // ROW_TILE

    def body(c_ref, w_ref, own_ref, other_ref, m_ref, v_ref, g_ref, d_ref, nm_ref, nv_ref):
        mine = (pl.program_id(0) // per_half) == c_ref[0]
        for start, size in W_IN_PARTS:
            cols, rows = pl.ds(start, size), pl.ds(start, size)
            g_ = jnp.where(mine, own_ref[:, cols], other_ref[:, cols]).T
            g_ref[rows, :] = g_
            d_ref[rows, :], nm_ref[rows, :], nv_ref[rows, :] = _adamw_math(w_ref[rows, :], g_, m_ref[rows, :], v_ref[rows, :])

    blk = pl.BlockSpec((W_IN_COLS, ROW_TILE), lambda i, c_ref: (0, i))
    own = pl.BlockSpec((ROW_TILE, W_IN_COLS), lambda i, c_ref: (jnp.where(i // per_half == c_ref[0], i % per_half, 0), 0))
    other = pl.BlockSpec((ROW_TILE, W_IN_COLS), lambda i, c_ref: (jnp.where(i // per_half == c_ref[0], 0, i % per_half), 0))
    grid_spec = pltpu.PrefetchScalarGridSpec(num_scalar_prefetch=1, grid=(D_MODEL // ROW_TILE,),
                                             in_specs=[blk, own, other, blk, blk], out_specs=[blk] * 4)
    return pl.pallas_call(body, name="adamw_w_in", grid_spec=grid_spec, out_shape=[jax.ShapeDtypeStruct(w_t.shape, F32)] * 4,
                          compiler_params=_params(("parallel",)))(c.reshape(1).astype(jnp.int32), w_t, g_own, g_other, m_t, v_t)


def _adamw_math(w, g, m, v):
    m_new = ADAM_B1 * m + (1.0 - ADAM_B1) * g
    v_new = ADAM_B2 * v + (1.0 - ADAM_B2) * (g * g)
    m_hat = m_new / (1.0 - ADAM_B1 ** ADAM_STEP)
    v_hat = v_new / (1.0 - ADAM_B2 ** ADAM_STEP)
    return -ADAM_LR * (m_hat / (jnp.sqrt(v_hat) + ADAM_EPS) + ADAM_WD * w), m_new, v_new


VECTORS = ("g_mix", "g_q", "g_k", "g_attn_out", "conv_b", "dt_bias", "a_log", "d_skip", "g_ssm_out", "g_cross", "g_mem",
           "g_cq", "g_ck", "g_mlp")
WEIGHTS = ("g_mix", "w_in", "g_q", "g_k", "g_attn_out", "conv_w", "conv_b", "dt_bias", "a_log", "d_skip", "g_ssm_out", "w_out",
           "g_cross", "g_mem", "w_cq", "w_ckv", "g_cq", "g_ck", "w_co", "g_mlp", "w_up", "w_down")


def _pack(parts):
    flat = jnp.concatenate([t.reshape(-1) for t in parts])
    total = -(-flat.shape[0] // 1024) * 1024
    return jnp.pad(flat, (0, total - flat.shape[0])).reshape(total // 128, 128)


def _rows_of(n):
    return -(-n // 128)


def _slot_rows(n):
    return -(-n // 1024) * 8


def _pack_rows(parts):
    rows = []
    for t in parts:
        flat = t.reshape(-1)
        rows.append(jnp.pad(flat, (0, 128 * _slot_rows(flat.shape[0]) - flat.shape[0])).reshape(-1, 128))
    return jnp.concatenate(rows)


def _adamw_vectors(summed, chip, vectors, conv):
    groups = list(vectors) + [conv]
    offsets, row = [], 0
    for w, _, _ in groups:
        offsets.append(row)
        row += _slot_rows(w.shape[1]) if w.shape[0] == 1 else _slot_rows(w.shape[0] * N_CHIPS * w.shape[1])
    conv_blocks = _rows_of(conv[0].shape[1])

    def body(chip_ref, sum_ref, *refs):
        ins, outs = refs[:3 * len(groups)], refs[3 * len(groups):]

        def update(i, g, idx):
            w_ref, m_ref, v_ref = ins[3 * i:3 * i + 3]
            delta, new_m, new_v = _adamw_math(w_ref[idx], g, m_ref[idx], v_ref[idx])
            for o_ref, val in zip(outs[4 * i:4 * i + 4], (g, delta, new_m, new_v)):
                o_ref[idx] = val

        for i, (w, _, _) in enumerate(vectors):
            for t in range(_rows_of(w.shape[1])):
                width = min(128, w.shape[1] - 128 * t)
                update(i, sum_ref[pl.ds(offsets[i] + t, 1), pl.ds(0, width)], (slice(None), pl.ds(128 * t, width)))
        for tap in range(conv[0].shape[0]):
            for blk in range(conv_blocks):
                src = offsets[-1] + tap * N_CHIPS * conv_blocks + chip_ref[0] * conv_blocks + blk
                update(len(vectors), sum_ref[pl.ds(src, 1), :], (pl.ds(tap, 1), pl.ds(128 * blk, 128)))

    def whole(a):
        return pl.BlockSpec(a.shape, lambda i, chip_ref: (0,) * a.ndim)

    operands = [t for group in groups for t in group]
    grid_spec = pltpu.PrefetchScalarGridSpec(
        num_scalar_prefetch=1, grid=(1,), in_specs=[whole(summed)] + [whole(t) for t in operands],
        out_specs=[whole(w) for w, _, _ in groups for _ in range(4)])
    res = pl.pallas_call(body, name="adamw_vectors", grid_spec=grid_spec,
                         out_shape=[jax.ShapeDtypeStruct(w.shape, F32) for w, _, _ in groups for _ in range(4)],
                         compiler_params=_params(("arbitrary",)))(chip.reshape(1).astype(jnp.int32), summed, *operands)
    return [res[4 * i:4 * i + 4] for i in range(len(groups))]


def _unpack(buf, shapes):
    flat, out, pos = buf.reshape(-1), [], 0
    for shape in shapes:
        size = math.prod(shape)
        out.append(flat[pos:pos + size].reshape(shape))
        pos += size
    return out


def kernel(x, mem, positions, g_mix, w_in, g_q, g_k, g_attn_out, conv_w, conv_b, dt_bias, a_log, d_skip, g_ssm_out, w_out, g_cross, g_mem, w_cq, w_ckv, g_cq, g_ck, w_co, g_mlp, w_up, w_down, loss_target, m_g_mix, m_w_in, m_g_q, m_g_k, m_g_attn_out, m_conv_w, m_conv_b, m_dt_bias, m_a_log, m_d_skip, m_g_ssm_out, m_w_out, m_g_cross, m_g_mem, m_w_cq, m_w_ckv, m_g_cq, m_g_ck, m_w_co, m_g_mlp, m_w_up, m_w_down, v_g_mix, v_w_in, v_g_q, v_g_k, v_g_attn_out, v_conv_w, v_conv_b, v_dt_bias, v_a_log, v_d_skip, v_g_ssm_out, v_w_out, v_g_cross, v_g_mem, v_w_cq, v_w_ckv, v_g_cq, v_g_ck, v_w_co, v_g_mlp, v_w_up, v_w_down):
    args = dict(locals())
    weights = {n: args[n][0] for n in WEIGHTS}
    mom_m = {n: args["m_" + n][0] for n in WEIGHTS}
    mom_v = {n: args["v_" + n][0] for n in WEIGHTS}
    x_idx, y_idx, c_idx = _place()
    chip = 2 * x_idx + y_idx

    shapes = {n: weights[n].shape for n in MATRICES}
    first, mid, late = ("w_in",), ("w_out", "w_cq", "w_ckv", "w_co"), ("w_up", "w_down")
    w_in_t, m_in_t, v_in_t = (jnp.swapaxes(t, 1, 2)[0] for t in (w_in, m_w_in, v_w_in))
    w_in_buf = [_cast_w_in_transposed(w_in_t, chip)]
    taps, tap_cols = weights["conv_w"].shape
    conv_parts = _small_allreduce(_pack([jnp.zeros((N_CHIPS, taps, tap_cols), F32).at[chip].set(0.5 * weights["conv_w"])]),
                                  "gather_conv_taps")
    ici_in = [_ici_plan(first, shapes, (k, 2)) for k in range(2)]
    pass_in = [_pass_on_plan(first, shapes, (k, 2)) for k in range(2)]
    sems_in, w_in_buf, token = _split_start("gather_ici_start_w_in", w_in_buf, lambda refs: ici_in[0](refs) + ici_in[1](refs),
                                            [3, 3], after=(conv_parts,))
    bufs = [_cast_into_gathered(weights[n], n, chip, after=(token,)) for n in mid + late]
    plan = lambda refs: _ici_plan(mid, shapes)(refs[:4]) + _ici_plan(late, shapes)(refs[4:])
    sems_rest, bufs, token = _split_start("gather_ici_start_rest", bufs, plan, [12, 6], after=(token,))
    params = {n: weights[n].reshape(1, -1) for n in VECTORS}
    h_in = _rowwise(_norm_fn, [_full(x[0])], [_full(params["g_mix"])], [(D_MODEL, BF16, D_MODEL, 0, False)], name="norm_in",
                    after=(token,))[0]
    w_in_buf = _split_wait("gather_ici_wait_w_in", w_in_buf, sems_in[0], ici_in[0], token, h_in, m_in_t, v_in_t)
    pass_sems_0, w_in_buf, token = _split_start("gather_pass_start_w_in", w_in_buf, pass_in[0], [3])
    w_in_buf = _split_wait("gather_ici_wait_w_in_rest", w_in_buf, sems_in[1], ici_in[1], token)
    pass_sems_1, w_in_buf, token = _split_start("gather_pass_start_w_in_rest", w_in_buf, pass_in[1], [3])
    w_in_buf = _split_wait("gather_pass_wait_w_in", w_in_buf, pass_sems_0[0], pass_in[0], token)
    w_in_buf = _split_wait("gather_pass_wait_w_in_rest", w_in_buf, pass_sems_1[0], pass_in[1])
    w_in_full = _w_in_columns(w_in_buf[0], to_shards=False)
    full = {"w_in": w_in_full,
            "w_dt": jnp.pad(w_in_full[:, D_MAIN:].reshape(D_MODEL, N_GROUPS, HEADS_PER_GROUP),
                            ((0, 0), (0, 0), (0, 128 - HEADS_PER_GROUP))).reshape(D_MODEL, DT_PAD)}
    in_flight = {}

    def more_weights(stage, after):
        if stage == "mixer_done":
            got = _split_wait("gather_ici_wait_mid", bufs[:4], sems_rest[0], _ici_plan(mid, shapes), after)
            sems, got, token = _split_start("gather_pass_start_mid", got, _pass_on_plan(mid, shapes), [12])
            return dict(zip(mid, _split_wait("gather_pass_wait_mid", got, sems[0], _pass_on_plan(mid, shapes), token)))
        if stage == "cross_started":
            got = _split_wait("gather_ici_wait_late", bufs[4:], sems_rest[1], _ici_plan(late, shapes), after)
            in_flight["late"] = _split_start("gather_pass_start_late", got, _pass_on_plan(late, shapes), [6])
            return {}
        sems, got, token = in_flight.pop("late")
        return dict(zip(late, _split_wait("gather_pass_wait_late", got, sems[0], _pass_on_plan(late, shapes), token, after)))

    params["conv_w"] = _unpack(conv_parts, [(N_CHIPS, taps, tap_cols)])[0].transpose(1, 0, 2).reshape(taps, N_CHIPS * tap_cols)

    groups = (("w_down",), ("w_up",), ("w_co", "w_cq", "w_ckv", "w_out"), ("w_in",))
    scattered = []

    class GradStore(dict):
        pending = None

        def __setitem__(self, name, value):
            super().__setitem__(name, value)
            if "w_main" in self and "w_dt" in self and "w_in" not in self:
                gw_in = lax.dynamic_update_slice(self["w_main"], _unpad_heads(self["w_dt"]), (0, D_MAIN))
                self["w_in"] = _w_in_columns(gw_in, to_shards=True)
            for group in groups:
                if name in group and all(n in self for n in group):
                    self.settle()
                    pieces = [self[n].reshape(N_CHIPS, 2, shapes[n][0] // 2, shapes[n][1]) for n in group]
                    if group == groups[-1]:
                        self.scatter(group, pieces, _sibling_swap(pieces, "grad_swap_" + group[0]))
                    else:
                        landing = [lax.empty((N_CHIPS,) + a.shape[2:], BF16) for a in pieces]
                        sems, thru, self.token = _split_start("grad_swap_start_" + group[0], pieces + landing,
                                                              _swap_plan(len(pieces)), [len(pieces)])
                        self.pending = (group, sems[0], thru)

        def settle(self, *after):
            if self.pending is not None:
                group, sems, thru = self.pending
                self.pending = None
                thru = _split_wait("grad_swap_wait_" + group[0], thru, sems, _swap_plan(len(group)), *after)
                self.scatter(group, thru[:len(group)], thru[len(group):])

        def scatter(self, group, pieces, from_sibling):
            sums = [_add_halves(a, r, c_idx, "add_halves_" + n) for n, a, r in zip(group, pieces, from_sibling)]
            landing = [lax.empty((3,) + s.shape[1:], BF16) for s in sums]
            sems, thru, self.token = _split_start("grad_scatter_start_" + group[0], sums + landing,
                                                  _scatter_plan(len(sums)), [3 * len(sums)])
            scattered.append((group, sems[0], thru))

    loss, grad_x, grads = _local_step(x[0], mem[0], positions[0], loss_target[0], params, full, more_weights, GradStore(),
                                      h_in)

    out_g, out_d, out_m, out_v = {}, {}, {}, {}

    def finish(entries, order, token):
        halves = {}
        for group, sems, thru in entries:
            thru = _split_wait("grad_scatter_wait_" + group[0], thru, sems, _scatter_plan(len(group)), token)
            for i, n in enumerate(group):
                halves[n] = _sum_chips(thru[i], thru[len(group) + i], chip, "sum_chips_" + n)
        sources = [halves[n] for n in order]
        landing = [lax.empty(s.shape, F32) for s in sources]
        sems, thru, token = _split_start("grad_share_start_" + order[0], sources + landing, _share_plan(len(order)),
                                         [1] * len(order))
        for i, n in enumerate(order):
            own, other = _split_wait("grad_share_wait_" + n, [thru[i], thru[len(order) + i]], sems[i], _share_plan(1), token)
            if n == "w_in":
                res_t = _adamw_w_in_transposed(w_in_t, own, other, m_in_t, v_in_t, c_idx)
                out_g[n], out_d[n], out_m[n], out_v[n] = (t.T for t in res_t)
            else:
                out_g[n], out_d[n], out_m[n], out_v[n] = _adamw_halves(weights[n], own, other, mom_m[n], mom_v[n], c_idx,
                                                                       "adamw_" + n)
            token = out_v[n]
        return token

    token = finish(scattered[:-1], ("w_cq", "w_co", "w_ckv", "w_out", "w_up", "w_down"), grad_x)
    finish(scattered[-1:], ("w_in",), token)

    names = VECTORS + ("conv_w",)
    summed = _small_allreduce(_pack_rows([grads[n] for n in names] + [loss]), "allreduce_vectors")
    total_loss = summed[sum(_slot_rows(grads[n].size) for n in names), 0]
    small_out = _adamw_vectors(summed, chip, [(args[n], args["m_" + n], args["v_" + n]) for n in VECTORS],
                               (weights["conv_w"], mom_m["conv_w"], mom_v["conv_w"]))
    for n, res in zip(names, small_out):
        out_g[n], out_d[n], out_m[n], out_v[n] = (t.reshape(weights[n].shape) for t in res)

    outs =[total_loss, grad_x[None]]
    for group in (out_g, out_d, out_m, out_v):
        outs += [group[n][None] for n in WEIGHTS]
    return tuple(outs)
```

```python
import functools
import math

import jax
import jax.numpy as jnp
from jax import lax
from jax.experimental import pallas as pl
from jax.experimental.pallas import tpu as pltpu

F32 = jnp.float32
BF16 = jnp.bfloat16

SEQ = 2048
D_MODEL = 2048
HEAD = 64
D_ATTN = 1024
D_SSM = 1024
N_GROUPS = 4
N_STATE = 128
CHUNK = 128
ATT_BLK = 128
N_MEM = 256
D_CROSS = 512
D_MAIN = 6144
N_DT = 16
DT_PAD = 512
ROT = 16
ROPE_THETA = 500000.0
EPS = 1e-6
NEG = -1e30
BRANCH_BLOCKS = (16, 4, 1)
DILATIONS = (1, 4, 16)

ADAM_LR, ADAM_B1, ADAM_B2, ADAM_EPS, ADAM_WD, ADAM_STEP = 0.001, 0.9, 0.999, 1e-08, 0.01, 10

VMEM_LIMIT = 56 * 1024 * 1024
MESH = pl.DeviceIdType.MESH


def _params(sem, **kw):
    return pltpu.CompilerParams(dimension_semantics=sem, vmem_limit_bytes=VMEM_LIMIT, **kw)


def _bdot(a, b, dims):
    return lax.dot_general(a.astype(BF16), b.astype(BF16), (dims, ((), ())), preferred_element_type=F32)


def _fdot(a, b, dims):
    return lax.dot_general(a, b, (dims, ((), ())), preferred_element_type=F32, precision=lax.Precision.HIGHEST)


NN = ((1,), (0,))
NT = ((1,), (1,))
TN = ((0,), (0,))


def _tile(n, want):
    t = min(n, want)
    while n % t:
        t //= 2
    return t


def _matmul(a, b, *, mode, name, outs, extra=(), vecs=(), epilogue=None, col_shards=1, after=(), n_cols=None, out_cols=None,
            tile_rows=0, tile_sums=0, tm=1024, tn=1024, tk=2048):
    if mode == "nn":
        (m, k), n = a.shape, b.shape[1]
    elif mode == "nt":
        (m, k), n = a.shape, b.shape[0]
    else:
        (k, m), n = a.shape, b.shape[1]
    n = n if n_cols is None else n_cols
    tm, tn, tk = _tile(m, tm), _tile(n // col_shards, tn), _tile(k, tk)
    nk = k // tk
    per_shard = n // col_shards // tn
    dims = {"nn": NN, "nt": NT, "tn": TN}[mode]
    a_spec = pl.BlockSpec((tk, tm), lambda i, j, kk: (kk, i)) if mode == "tn" else pl.BlockSpec((tm, tk), lambda i, j, kk: (i, kk))
    b_spec = pl.BlockSpec((tn, tk), lambda i, j, kk: (j, kk)) if mode == "nt" else pl.BlockSpec((tk, tn), lambda i, j, kk: (kk, j))
    o_spec = pl.BlockSpec((tm, tn), lambda i, j, kk: (i, j))
    n_extra, n_out, n_after = len(extra) + len(vecs), len(outs), len(after)

    def body(a_ref, b_ref, *rest):
        extra_refs, out_refs, acc_ref = rest[:n_extra], rest[n_extra + n_after:-1], rest[-1]

        def finish(acc):
            res = (acc,) if epilogue is None else epilogue(acc, *[e[...] for e in extra_refs])
            for o_ref, r in zip(out_refs[:n_out], res):
                o_ref[...] = r.astype(o_ref.dtype)
            for o_ref, r in zip(out_refs[n_out:], res[n_out:]):
                o_ref[...] = jnp.broadcast_to(r, o_ref.shape)

        if nk == 1:
            finish(_bdot(a_ref[...], b_ref[...], dims))
            return
        kk = pl.program_id(2)

        @pl.when(kk == 0)
        def _():
            acc_ref[...] = jnp.zeros_like(acc_ref)

        acc_ref[...] += _bdot(a_ref[...], b_ref[...], dims)

        @pl.when(kk == nk - 1)
        def _():
            finish(acc_ref[...])

    if col_shards == 1:
        out_specs, out_dims = [o_spec] * n_out, (m, n if out_cols is None else out_cols)
    else:
        sharded = pl.BlockSpec((None, tm, tn), lambda i, j, kk: (j // per_shard, i, j % per_shard))
        out_specs, out_dims = [sharded] * n_out, (col_shards, m, n // col_shards)
    res = pl.pallas_call(
        body, name=name, grid=(m // tm, n // tn, nk),
        in_specs=[a_spec, b_spec] + [o_spec] * len(extra) + [pl.BlockSpec((1, tn), lambda i, j, kk: (0, j))] * len(vecs)
        + [pl.BlockSpec(memory_space=pl.ANY)] * n_after,
        out_specs=out_specs + [pl.BlockSpec((8, tn), lambda i, j, kk: (i, j))] * tile_rows
        + [pl.BlockSpec((8, 128), lambda i, j, kk: (i, j))] * tile_sums,
        out_shape=[jax.ShapeDtypeStruct(out_dims, dt) for dt in outs] + [jax.ShapeDtypeStruct((m // tm * 8, n), F32)] * tile_rows
        + [jax.ShapeDtypeStruct((m // tm * 8, n // tn * 128), F32)] * tile_sums,
        scratch_shapes=[pltpu.VMEM((tm, tn) if nk > 1 else (8, 128), F32)],
        compiler_params=_params(("parallel", "parallel", "arbitrary")),
    )(a, b, *extra, *vecs, *after)
    res = (list(res[:n_out]) + [jnp.sum(t[::8], axis=0, keepdims=True) for t in res[n_out:n_out + tile_rows]]
           + [t[::8, ::128] for t in res[n_out + tile_rows:]])
    return res[0] if len(res) == 1 else res


def _row_spec(tr, bw, cb, per_group):
    return pl.BlockSpec((tr, bw), (lambda g, i: (i, cb + g)) if per_group else (lambda g, i: (i, cb)))


def _vec_spec(bw, cb, per_group):
    return pl.BlockSpec((1, bw), (lambda g, i: (0, cb + g)) if per_group else (lambda g, i: (0, cb)))


def _rowwise(fn, rows, vecs, outs, *, name, n_rows=SEQ, tr=512, groups=1, after=()):
    n_r, n_v, n_after = len(rows), len(vecs), len(after)

    def body(*refs):
        vals = [r[...].astype(F32) for r in refs[:n_r + n_v]]
        res = fn(*vals)
        for o_ref, r in zip(refs[n_r + n_v + n_after:], res):
            o_ref[...] = r.astype(o_ref.dtype)

    res = pl.pallas_call(
        body, name=name, grid=(groups, n_rows // tr),
        in_specs=[_row_spec(tr, bw, cb, pg) for _, bw, cb, pg in rows] + [_vec_spec(bw, cb, pg) for _, bw, cb, pg in vecs]
        + [pl.BlockSpec(memory_space=pl.ANY)] * n_after,
        out_specs=[_row_spec(tr, bw, cb, pg) for _, _, bw, cb, pg in outs],
        out_shape=[jax.ShapeDtypeStruct((n_rows, w), dt) for w, dt, _, _, _ in outs],
        compiler_params=_params(("parallel", "parallel")),
    )(*[r[0] for r in rows], *[v[0] for v in vecs], *after)
    return res


def _rowwise_vjp(fn, rows, vecs, cts, row_grads, vec_grads, *, name, n_rows=SEQ, tr=512, groups=1, after=()):
    n_r, n_v, n_after = len(rows), len(vecs), len(after)
    ct_ops = [op for group in cts for op in group]
    ct_sizes = [len(group) for group in cts]
    res_ops = [g[6] for g in row_grads if g[6] is not None]
    n_ct, n_res, n_rg = len(ct_ops), len(res_ops), len(row_grads)

    def body(*refs):
        vals = [r[...].astype(F32) for r in refs[:n_r + n_v]]
        pos = n_r + n_v
        ct_vals = []
        for size in ct_sizes:
            acc = refs[pos][...].astype(F32)
            for t in range(1, size):
                acc = acc + refs[pos + t][...].astype(F32)
            ct_vals.append(acc)
            pos += size
        res_refs = refs[pos:pos + n_res]
        out_refs = refs[pos + n_res + n_after:]
        _, pullback = jax.vjp(fn, *vals)
        grads = pullback(tuple(ct_vals))
        r_i = 0
        for o_ref, g in zip(out_refs[:n_rg], row_grads):
            val = grads[g[0]]
            if g[6] is not None:
                val = val + res_refs[r_i][...].astype(F32)
                r_i += 1
            o_ref[...] = val.astype(o_ref.dtype)
        first = (pl.program_id(1) == 0)
        for o_ref, g in zip(out_refs[n_rg:], vec_grads):
            val = jnp.sum(grads[n_r + g[0]], axis=0, keepdims=True)
            init = first if g[4] else jnp.logical_and(first, pl.program_id(0) == 0)

            @pl.when(init)
            def _(o_ref=o_ref, val=val):
                o_ref[...] = val

            @pl.when(jnp.logical_not(init))
            def _(o_ref=o_ref, val=val):
                o_ref[...] += val

    in_specs = [_row_spec(tr, bw, cb, pg) for _, bw, cb, pg in rows] + [_vec_spec(bw, cb, pg) for _, bw, cb, pg in vecs]
    in_specs += [_row_spec(tr, bw, cb, pg) for _, bw, cb, pg in ct_ops + res_ops] + [pl.BlockSpec(memory_space=pl.ANY)] * n_after
    out_specs =[_row_spec(tr, g[3], g[4], g[5]) for g in row_grads] + [_vec_spec(g[2], g[3], g[4]) for g in vec_grads]
    out_shape = [jax.ShapeDtypeStruct((n_rows, g[1]), g[2]) for g in row_grads]
    out_shape += [jax.ShapeDtypeStruct((1, g[1]), F32) for g in vec_grads]
    return pl.pallas_call(
        body, name=name, grid=(groups, n_rows // tr),
        in_specs=in_specs, out_specs=out_specs, out_shape=out_shape,
        compiler_params=_params(("arbitrary", "arbitrary")),
    )(*[r[0] for r in rows], *[v[0] for v in vecs], *[c[0] for c in ct_ops], *[r[0] for r in res_ops], *after)


def _full(arr, width=None):
    return (arr, arr.shape[1] if width is None else width, 0, False)


def _make_xor(sh):
    def raw(x):
        n = x.shape[-1]
        lane = lax.broadcasted_iota(jnp.int32, x.shape, x.ndim - 1)
        up = pltpu.roll(x, n - sh, x.ndim - 1)
        down = pltpu.roll(x, sh, x.ndim - 1)
        return jnp.where((lane & sh) == 0, up, down)

    f = jax.custom_vjp(raw)
    f.defvjp(lambda x: (raw(x), None), lambda _, ct: (raw(ct),))
    return f


_SWAP_ROPE_HALVES = _make_xor(ROT // 2)


def _head_sum(x):
    n = x.shape[-1]
    same_head = (lax.broadcasted_iota(jnp.int32, (n, n), 0) // HEAD) == (lax.broadcasted_iota(jnp.int32, (n, n), 1) // HEAD)
    return _fdot(x, same_head.astype(F32), NN)


def _rms(x, g):
    return x * lax.rsqrt(jnp.mean(x * x, axis=-1, keepdims=True) + EPS) * g


def _head_rms_rope(x, g, cos, sin, scale):
    y = x * lax.rsqrt(_head_sum(x * x) * (1.0 / HEAD) + EPS) * g
    return (y * cos + _SWAP_ROPE_HALVES(y) * sin) * scale


def _qk_fn(q, k, v, cos, sin, gq, gk):
    return (_head_rms_rope(q, gq, cos, sin, HEAD ** -0.5), _head_rms_rope(k, gk, cos, sin, 1.0), v)


def _norm_fn(x, g):
    return (_rms(x, g),)


def _merge_fn(o0, o1, o2, l0, l1, l2, g):
    m = lax.stop_gradient(jnp.maximum(jnp.maximum(l0, l1), l2))
    e0, e1, e2 = jnp.exp(l0 - m), jnp.exp(l1 - m), jnp.exp(l2 - m)
    mix = (e0 * o0 + e1 * o1 + e2 * o2) / (e0 + e1 + e2)
    return (_rms(mix, g),)


def _gate_fn(y, z, g):
    return (_rms(y * (z * jax.nn.sigmoid(z)), g),)


def _attn_pair(q, kc, vc, kp=None, vp=None, has_prev=None):
    pick0, pick1 = _head_picks()
    k_band, v_band, mask = _attn_band(kc, vc, kp, vp, has_prev)
    s = jnp.where(mask, _bdot(jnp.concatenate([q * pick0, q * pick1], axis=0), k_band, NT), NEG)
    m = jnp.max(s, axis=-1, keepdims=True)
    p = jnp.exp(s - m)
    den = jnp.sum(p, axis=-1, keepdims=True)
    acc = _bdot(p, v_band, NN) * (1.0 / den)
    lse_rows = m + jnp.log(den)
    o = pick0 * acc[:ATT_BLK] + pick1 * acc[ATT_BLK:]
    lse = pick0 * lse_rows[:ATT_BLK] + pick1 * lse_rows[ATT_BLK:]
    return o, lse


def _head_picks():
    lane = lax.broadcasted_iota(jnp.int32, (1, 2 * HEAD), 1)
    return (lane < HEAD).astype(F32), (lane >= HEAD).astype(F32)


def _attn_band(kc, vc, kp, vp, has_prev):
    n_keys = ATT_BLK if kp is None else 2 * ATT_BLK
    qi = lax.broadcasted_iota(jnp.int32, (2 * ATT_BLK, n_keys), 0) & (ATT_BLK - 1)
    kj = lax.broadcasted_iota(jnp.int32, (2 * ATT_BLK, n_keys), 1)
    if kp is None:
        return kc, vc, qi >= kj
    in_prev = jnp.logical_and(jnp.logical_and(kj < ATT_BLK, kj >= qi), has_prev)
    mask = jnp.logical_or(in_prev, jnp.logical_and(kj >= ATT_BLK, qi >= kj - ATT_BLK))
    return jnp.concatenate([kp, kc], axis=0), jnp.concatenate([vp, vc], axis=0), mask


def _attn_config(b):
    r = DILATIONS[b]
    return r, ATT_BLK * r, (D_ATTN if r == 1 else 128), BRANCH_BLOCKS[b] > 1


RESIDUES_UNROLLED = 16


def _for_residues(r, fn):
    if r <= RESIDUES_UNROLLED:
        for rho in range(r):
            fn(rho)
    else:
        def step(t, carry):
            for u in range(RESIDUES_UNROLLED):
                fn(RESIDUES_UNROLLED * t + u)
            return carry

        lax.fori_loop(0, r // RESIDUES_UNROLLED, step, 0)


def _strided_rows(start, r):
    if r > 1:
        return pl.ds(start, ATT_BLK, stride=r)
    return pl.ds(start if isinstance(start, int) else pl.multiple_of(start, ATT_BLK), ATT_BLK)


def _attention_fwd(qn, kn, vn, b):
    r, rows, lanes, with_prev = _attn_config(b)
    cur = pl.BlockSpec((rows, lanes), lambda g, n: (n, g))
    prev = pl.BlockSpec((rows, lanes), lambda g, n: (jnp.maximum(n - 1, 0), g))

    def body(*refs):
        ins, (o_ref, l_ref) = refs[:-2], refs[-2:]
        has_prev = pl.program_id(1) > 0

        def one(rho):
            sub = _strided_rows(rho, r)
            for pair in range(lanes // 128):
                sl = pl.ds(pair * 128, 128)
                args = [ref[sub, sl] for ref in ins] + ([has_prev] if with_prev else [])
                o_ref[sub, sl], l_ref[sub, sl] = _attn_pair(*args)

        _for_residues(r, one)

    operands = (qn, kn, vn, kn, vn) if with_prev else (qn, kn, vn)
    return pl.pallas_call(
        body, name="attn_fwd_%d" % r, grid=(D_ATTN // lanes, SEQ // rows),
        in_specs=[cur, cur, cur] + ([prev, prev] if with_prev else []), out_specs=[cur, cur],
        out_shape=[jax.ShapeDtypeStruct((SEQ, D_ATTN), F32)] * 2,
        compiler_params=_params(("parallel", "parallel")),
    )(*operands)


def _attn_pair_bwd(q, kc, vc, kp, vp, o, lse, do, dl, has_prev):
    pick0, pick1 = _head_picks()
    lane = lax.broadcasted_iota(jnp.int32, (1, 2 * HEAD), 1)
    k_band, v_band, mask = _attn_band(kc, vc, kp, vp, has_prev)
    q2 = jnp.concatenate([q * pick0, q * pick1], axis=0)
    do2 = jnp.concatenate([do * pick0, do * pick1], axis=0)
    lse2 = jnp.concatenate([jnp.sum(lse * (lane == 0).astype(F32), axis=-1, keepdims=True),
                            jnp.sum(lse * (lane == HEAD).astype(F32), axis=-1, keepdims=True)], axis=0)
    base = jnp.sum(jnp.concatenate([dl * pick0, dl * pick1], axis=0) - do2 * jnp.concatenate([o, o], axis=0),
                   axis=-1, keepdims=True)
    p = jnp.exp(jnp.where(mask, _bdot(q2, k_band, NT), NEG) - lse2)
    ds = p * (_bdot(do2, v_band, NT) + base)
    dq2 = _bdot(ds, k_band, NN)
    dq = pick0 * dq2[:ATT_BLK] + pick1 * dq2[ATT_BLK:]
    dk, dv = _bdot(ds, q2, TN), _bdot(p, do2, TN)
    if kp is None:
        return dq, dk, dv
    return dq, dk[ATT_BLK:], dv[ATT_BLK:], dk[:ATT_BLK], dv[:ATT_BLK]


def _attention_bwd(qn, kn, vn, o, lse, do, dl, b):
    r, rows, lanes, with_prev = _attn_config(b)
    cur = pl.BlockSpec((rows, lanes), lambda g, n: (n, g))
    prev = pl.BlockSpec((rows, lanes), lambda g, n: (jnp.maximum(n - 1, 0), g))
    whole = pl.BlockSpec((SEQ, lanes), lambda g, n: (0, g))
    n_in = 5 if with_prev else 3

    def body(*refs):
        ins, (o_ref, l_ref, do_ref, dl_ref, dq_ref, dk_ref, dv_ref) = refs[:n_in], refs[n_in:]
        n = pl.program_id(1)

        @pl.when(n == 0)
        def _():
            dk_ref[...] = jnp.zeros_like(dk_ref)
            dv_ref[...] = jnp.zeros_like(dv_ref)

        def one(rho):
            sub = _strided_rows(rho, r)
            sub_c = _strided_rows(n * rows + rho, r)
            sub_p = _strided_rows(jnp.maximum(n - 1, 0) * rows + rho, r)
            for pair in range(lanes // 128):
                sl = pl.ds(pair * 128, 128)
                vals = [ref[sub, sl] for ref in ins] + ([] if with_prev else [None, None])
                grads = _attn_pair_bwd(*vals, o_ref[sub, sl], l_ref[sub, sl], do_ref[sub, sl], dl_ref[sub, sl], n > 0)
                dq_ref[sub, sl] = grads[0]
                dk_ref[sub_c, sl] += grads[1]
                dv_ref[sub_c, sl] += grads[2]
                if with_prev:
                    dk_ref[sub_p, sl] += grads[3]
                    dv_ref[sub_p, sl] += grads[4]

        _for_residues(r, one)

    operands = (qn, kn, vn, kn, vn) if with_prev else (qn, kn, vn)
    return pl.pallas_call(
        body, name="attn_bwd_%d" % r, grid=(D_ATTN // lanes, SEQ // rows),
        in_specs=[cur, cur, cur] + ([prev, prev] if with_prev else []) + [cur] * 4, out_specs=[cur, whole, whole],
        out_shape=[jax.ShapeDtypeStruct((SEQ, D_ATTN), F32)] * 3,
        compiler_params=_params(("parallel", "arbitrary")),
    )(*operands, o, lse, do, dl)


CONV_COLS = 256
XBC_BLOCK0 = (3 * D_ATTN + D_SSM) // CONV_COLS


def _shift_rows(x, s):
    n = x.shape[0]
    t = lax.broadcasted_iota(jnp.int32, x.shape, 0)
    if s >= 0:
        return jnp.where(t >= s, pltpu.roll(x, s, 0), 0.0)
    return jnp.where(t < n + s, pltpu.roll(x, n + s, 0), 0.0)


def _conv_pre(x, w_ref, b_ref):
    delayed = [_shift_rows(x, 3 - k) for k in range(3)]
    pre = b_ref[...] + w_ref[3:4, :] * x
    for k in range(3):
        pre = pre + w_ref[k:k + 1, :] * delayed[k]
    return pre, delayed


def _conv_fwd(proj, conv_w, conv_b):
    cols = conv_w.shape[1]

    def body(x_ref, w_ref, b_ref, o_ref):
        pre, _ = _conv_pre(x_ref[...], w_ref, b_ref)
        o_ref[...] = pre * jax.nn.sigmoid(pre)

    blk = pl.BlockSpec((SEQ, CONV_COLS), lambda j: (0, j))
    return pl.pallas_call(
        body, name="conv_fwd", grid=(cols // CONV_COLS,),
        in_specs=[pl.BlockSpec((SEQ, CONV_COLS), lambda j: (0, XBC_BLOCK0 + j)),
                  pl.BlockSpec((4, CONV_COLS), lambda j: (0, j)), pl.BlockSpec((1, CONV_COLS), lambda j: (0, j))],
        out_specs=blk, out_shape=jax.ShapeDtypeStruct((SEQ, cols), F32),
        compiler_params=_params(("parallel",)),
    )(proj, conv_w, conv_b)


def _conv_bwd(proj, conv_w, conv_b, dxs, db, dc):
    cols = conv_w.shape[1]
    x_blocks, b_blocks = dxs.shape[1] // CONV_COLS, db.shape[1] // CONV_COLS

    def body(x_ref, w_ref, b_ref, dxs_ref, db_ref_in, dc_ref_in, dx_ref, dw_ref, db_ref):
        j = pl.program_id(0)
        dy = jnp.where(j < x_blocks, dxs_ref[...], jnp.where(j < x_blocks + b_blocks, db_ref_in[...], dc_ref_in[...]))
        x = x_ref[...]
        pre, delayed = _conv_pre(x, w_ref, b_ref)
        sg = jax.nn.sigmoid(pre)
        dpre = dy * (sg * (1.0 + pre * (1.0 - sg)))
        db_ref[...] = jnp.sum(dpre, axis=0, keepdims=True)
        dx = w_ref[3:4, :] * dpre
        dw_ref[3:4, :] = jnp.sum(dpre * x, axis=0, keepdims=True)
        for k in range(3):
            dx = dx + w_ref[k:k + 1, :] * _shift_rows(dpre, k - 3)
            dw_ref[k:k + 1, :] = jnp.sum(dpre * delayed[k], axis=0, keepdims=True)
        dw_ref[4:8, :] = jnp.zeros((4, CONV_COLS), F32)
        dx_ref[...] = dx.astype(dx_ref.dtype)

    blk = pl.BlockSpec((SEQ, CONV_COLS), lambda j: (0, j))
    parts = [pl.BlockSpec((SEQ, CONV_COLS), lambda j: (0, jnp.minimum(j, x_blocks - 1))),
             pl.BlockSpec((SEQ, CONV_COLS), lambda j: (0, jnp.clip(j - x_blocks, 0, b_blocks - 1))),
             pl.BlockSpec((SEQ, CONV_COLS), lambda j: (0, jnp.clip(j - x_blocks - b_blocks, 0, b_blocks - 1)))]
    return pl.pallas_call(
        body, name="conv_bwd", grid=(cols // CONV_COLS,),
        in_specs=[pl.BlockSpec((SEQ, CONV_COLS), lambda j: (0, XBC_BLOCK0 + j)),
                  pl.BlockSpec((4, CONV_COLS), lambda j: (0, j)), pl.BlockSpec((1, CONV_COLS), lambda j: (0, j))] + parts,
        out_specs=[blk, pl.BlockSpec((8, CONV_COLS), lambda j: (0, j)), pl.BlockSpec((1, CONV_COLS), lambda j: (0, j))],
        out_shape=[jax.ShapeDtypeStruct((SEQ, cols), BF16), jax.ShapeDtypeStruct((8, cols), F32),
                   jax.ShapeDtypeStruct((1, cols), F32)],
        compiler_params=_params(("parallel",)),
    )(proj, conv_w, conv_b, dxs, db, dc)


HEADS_PER_GROUP = 4


GROUP_WIDTH = HEADS_PER_GROUP * HEAD


def _ssd_chunk(x, bm, cm, dtr, bias, alog, dsk, h):
    row = lax.broadcasted_iota(jnp.int32, (CHUNK, CHUNK), 0)
    col = lax.broadcasted_iota(jnp.int32, (CHUNK, CHUNK), 1)
    causal = row >= col
    z = dtr + bias
    dt = jnp.maximum(z, 0.0) + jnp.log(1.0 + jnp.exp(-jnp.abs(z)))
    acs = _fdot(causal.astype(F32), dt * -jnp.exp(alog), NN)
    acs_t, dt_t = acs.T, dt.T
    cb = _bdot(cm, bm, NT)
    lane = lax.broadcasted_iota(jnp.int32, (1, CHUNK), 1)
    sub = lax.broadcasted_iota(jnp.int32, (CHUNK, 1), 0)
    wide = lax.broadcasted_iota(jnp.int32, (1, GROUP_WIDTH), 1) // HEAD
    tall = lax.broadcasted_iota(jnp.int32, (GROUP_WIDTH, 1), 0) // HEAD
    acs_last = jnp.sum(acs * (sub == CHUNK - 1).astype(F32), axis=0, keepdims=True)
    to_lanes = (lax.broadcasted_iota(jnp.int32, (CHUNK, GROUP_WIDTH), 0)
                == lax.broadcasted_iota(jnp.int32, (CHUNK, GROUP_WIDTH), 1) // HEAD).astype(F32)
    grow = _fdot(jnp.exp(acs), to_lanes, NN)
    keep = _fdot(jnp.exp(acs_last - acs) * dt, to_lanes, NN)
    w_parts, x_parts, skip, carry = [], [], 0.0, 0.0
    for j in range(HEADS_PER_GROUP):
        on_lane, on_sub = (lane == j).astype(F32), (sub == j).astype(F32)
        acs_c = jnp.sum(acs * on_lane, axis=1, keepdims=True)
        acs_r = jnp.sum(acs_t * on_sub, axis=0, keepdims=True)
        dt_r = jnp.sum(dt_t * on_sub, axis=0, keepdims=True)
        w_parts.append(cb * jnp.exp(jnp.where(causal, acs_c - acs_r, NEG)) * dt_r)
        x_parts.append(x * (wide == j).astype(F32))
        skip = skip + jnp.sum(dsk * on_lane, axis=1, keepdims=True) * (wide == j).astype(F32)
        carry = carry + jnp.sum(jnp.exp(acs_last) * on_lane, axis=1, keepdims=True) * (tall == j).astype(F32)
    y_diag = _bdot(jnp.concatenate(w_parts, axis=1), jnp.concatenate(x_parts, axis=0), NN)
    y = y_diag + _bdot(cm, h, NT) * grow + skip * x
    return y, h * carry + _bdot(x * keep, bm, TN)


GROUPS_PER_STEP = 4
SSD_STEPS = N_GROUPS // GROUPS_PER_STEP


def _ssd_specs(reverse):
    n_chunks = SEQ // CHUNK
    c_of = (lambda c: n_chunks - 1 - c) if reverse else (lambda c: c)
    x_w, n_w, dt_w = GROUPS_PER_STEP * GROUP_WIDTH, GROUPS_PER_STEP * N_STATE, GROUPS_PER_STEP * 128
    x_spec = pl.BlockSpec((CHUNK, x_w), lambda g, c: (c_of(c), g))
    b_spec = pl.BlockSpec((CHUNK, n_w), lambda g, c: (c_of(c), D_SSM // n_w + g))
    c_spec = pl.BlockSpec((CHUNK, n_w), lambda g, c: (c_of(c), (D_SSM + N_GROUPS * N_STATE) // n_w + g))
    dt_spec = pl.BlockSpec((CHUNK, dt_w), lambda g, c: (c_of(c), g))
    vec_spec = pl.BlockSpec((1, dt_w), lambda g, c: (0, g))
    h_spec = pl.BlockSpec((None, GROUPS_PER_STEP, GROUP_WIDTH, N_STATE), lambda g, c: (c_of(c), g, 0, 0))
    return x_spec, b_spec, c_spec, dt_spec, vec_spec, h_spec


def _group_slices(u):
    return pl.ds(u * GROUP_WIDTH, GROUP_WIDTH), pl.ds(u * N_STATE, N_STATE), pl.ds(u * 128, 128)


def _ssd_gated_chunk(x, bm, cm, dtr, bias, alog, dsk, h, z, g_out):
    y, h_new = _ssd_chunk(x, bm, cm, dtr, bias, alog, dsk, h)
    return _gate_fn(y, z, g_out)[0], h_new


def _ssd_gate_specs(reverse):
    x_spec = _ssd_specs(reverse)[0]
    z_block0 = 3 * D_ATTN // x_spec.block_shape[1]
    z_spec = pl.BlockSpec(x_spec.block_shape, lambda g, c: (x_spec.index_map(g, c)[0], z_block0 + g))
    return z_spec, pl.BlockSpec((1, x_spec.block_shape[1]), lambda g, c: (0, g))


def _ssd_fwd(xbc, dt_raw, bias, alog, dsk, proj, g_out):
    x_spec, b_spec, c_spec, dt_spec, vec_spec, h_spec = _ssd_specs(False)
    z_spec, g_spec = _ssd_gate_specs(False)

    def body(x_ref, b_ref, c_ref, dt_ref, bias_ref, alog_ref, dsk_ref, z_ref, g_ref, ssm_ref, hin_ref, h_scr):
        @pl.when(pl.program_id(1) == 0)
        def _():
            h_scr[...] = jnp.zeros_like(h_scr)

        for u in range(GROUPS_PER_STEP):
            xs, ns, ds = _group_slices(u)
            h = h_scr[u]
            hin_ref[u] = h
            ssm, h_scr[u] = _ssd_gated_chunk(x_ref[:, xs], b_ref[:, ns], c_ref[:, ns], dt_ref[:, ds], bias_ref[:, ds],
                                             alog_ref[:, ds], dsk_ref[:, ds], h, z_ref[:, xs], g_ref[:, xs])
            ssm_ref[:, xs] = ssm.astype(ssm_ref.dtype)

    return pl.pallas_call(
        body, name="ssd_fwd", grid=(SSD_STEPS, SEQ // CHUNK),
        in_specs=[x_spec, b_spec, c_spec, dt_spec, vec_spec, vec_spec, vec_spec, z_spec, g_spec],
        out_specs=[x_spec, h_spec],
        out_shape=[jax.ShapeDtypeStruct((SEQ, D_SSM), BF16),
                   jax.ShapeDtypeStruct((SEQ // CHUNK, N_GROUPS, GROUP_WIDTH, N_STATE), F32)],
        scratch_shapes=[pltpu.VMEM((GROUPS_PER_STEP, GROUP_WIDTH, N_STATE), F32)],
        compiler_params=_params(("parallel", "arbitrary")),
    )(xbc, xbc, xbc, dt_raw, bias, alog, dsk, proj, g_out)


def _ssd_bwd(xbc, dt_raw, bias, alog, dsk, h_in, proj, g_out, dmix):
    x_spec, b_spec, c_spec, dt_spec, vec_spec, h_spec = _ssd_specs(True)
    z_spec, g_spec = _ssd_gate_specs(True)
    ct_block0 = D_ATTN // x_spec.block_shape[1]
    ct_spec = pl.BlockSpec(x_spec.block_shape, lambda g, c: (x_spec.index_map(g, c)[0], ct_block0 + g))

    def body(x_ref, b_ref, c_ref, dt_ref, bias_ref, alog_ref, dsk_ref, hin_ref, z_ref, g_ref, ct_ref,
             dx_ref, db_ref, dc_ref, ddt_ref, dbias_ref, dalog_ref, ddsk_ref, dz_ref, dg_ref, dh_scr):
        first = pl.program_id(1) == 0

        @pl.when(first)
        def _():
            dh_scr[...] = jnp.zeros_like(dh_scr)

        for u in range(GROUPS_PER_STEP):
            xs, ns, ds = _group_slices(u)
            _, pullback = jax.vjp(_ssd_gated_chunk, x_ref[:, xs], b_ref[:, ns], c_ref[:, ns], dt_ref[:, ds], bias_ref[:, ds],
                                  alog_ref[:, ds], dsk_ref[:, ds], hin_ref[u], z_ref[:, xs], g_ref[:, xs])
            g = pullback((ct_ref[:, xs], dh_scr[u]))
            dx_ref[:, xs], db_ref[:, ns], dc_ref[:, ns] = g[0], g[1], g[2]
            ddt_ref[:, ds] = g[3].astype(ddt_ref.dtype)
            dh_scr[u] = g[7]
            dz_ref[:, xs] = g[8].astype(dz_ref.dtype)
            sums = ((dbias_ref, g[4], ds), (dalog_ref, g[5], ds), (ddsk_ref, g[6], ds),
                    (dg_ref, jnp.sum(g[9], axis=0, keepdims=True), xs))
            for o_ref, val, lanes in sums:
                @pl.when(first)
                def _(o_ref=o_ref, val=val, lanes=lanes):
                    o_ref[:, lanes] = val

                @pl.when(jnp.logical_not(first))
                def _(o_ref=o_ref, val=val, lanes=lanes):
                    o_ref[:, lanes] += val

    n_chunks = SEQ // CHUNK
    out_b = pl.BlockSpec((CHUNK, GROUPS_PER_STEP * N_STATE), lambda g, c: (n_chunks - 1 - c, g))
    return pl.pallas_call(
        body, name="ssd_bwd", grid=(SSD_STEPS, n_chunks),
        in_specs=[x_spec, b_spec, c_spec, dt_spec, vec_spec, vec_spec, vec_spec, h_spec, z_spec, g_spec, ct_spec],
        out_specs=[x_spec, out_b, out_b, dt_spec, vec_spec, vec_spec, vec_spec, x_spec, g_spec],
        out_shape=[jax.ShapeDtypeStruct((SEQ, D_SSM), F32), jax.ShapeDtypeStruct((SEQ, N_GROUPS * N_STATE), F32),
                   jax.ShapeDtypeStruct((SEQ, N_GROUPS * N_STATE), F32), jax.ShapeDtypeStruct((SEQ, DT_PAD), BF16),
                   jax.ShapeDtypeStruct((1, DT_PAD), F32), jax.ShapeDtypeStruct((1, DT_PAD), F32),
                   jax.ShapeDtypeStruct((1, DT_PAD), F32), jax.ShapeDtypeStruct((SEQ, D_SSM), BF16),
                   jax.ShapeDtypeStruct((1, D_SSM), F32)],
        scratch_shapes=[pltpu.VMEM((GROUPS_PER_STEP, GROUP_WIDTH, N_STATE), F32)],
        compiler_params=_params(("parallel", "arbitrary")),
    )(xbc, xbc, xbc, dt_raw, bias, alog, dsk, h_in, proj, g_out, dmix)


CROSS_HEAD = 128
CROSS_ROWS = 1024


def _cross_head(q, k, v, gq, gk):
    qn = _rms(q, gq) * (CROSS_HEAD ** -0.5)
    kn = _rms(k, gk)
    s = _bdot(qn, kn, NT)
    p = jnp.exp(s - lax.stop_gradient(jnp.max(s, axis=-1, keepdims=True)))
    return _bdot(p, v, NN) * (1.0 / jnp.sum(p, axis=-1, keepdims=True))


def _cross_specs():
    q_spec = pl.BlockSpec((CROSS_ROWS, CROSS_HEAD), lambda h, i: (i, h))
    k_spec = pl.BlockSpec((N_MEM, CROSS_HEAD), lambda h, i: (0, h))
    v_spec = pl.BlockSpec((N_MEM, CROSS_HEAD), lambda h, i: (0, 4 + h))
    g_spec = pl.BlockSpec((1, CROSS_HEAD), lambda h, i: (0, 0))
    return q_spec, k_spec, v_spec, g_spec


def _cross_fwd(qc, kv, gq, gk):
    q_spec, k_spec, v_spec, g_spec = _cross_specs()

    def body(q_ref, k_ref, v_ref, gq_ref, gk_ref, o_ref):
        o_ref[...] = _cross_head(q_ref[...], k_ref[...], v_ref[...], gq_ref[...], gk_ref[...]).astype(o_ref.dtype)

    return pl.pallas_call(
        body, name="cross_fwd", grid=(4, SEQ // CROSS_ROWS),
        in_specs=[q_spec, k_spec, v_spec, g_spec, g_spec], out_specs=q_spec,
        out_shape=jax.ShapeDtypeStruct((SEQ, D_CROSS), BF16),
        compiler_params=_params(("parallel", "parallel")),
    )(qc, kv, kv, gq, gk)


def _cross_bwd(qc, kv, gq, gk, do):
    q_spec, k_spec, v_spec, g_spec = _cross_specs()

    def body(q_ref, k_ref, v_ref, gq_ref, gk_ref, do_ref, dq_ref, dk_ref, dv_ref, dgq_ref, dgk_ref):
        _, pullback = jax.vjp(_cross_head, q_ref[...], k_ref[...], v_ref[...], gq_ref[...], gk_ref[...])
        dq, dk, dv, dgq, dgk = pullback(do_ref[...].astype(F32))
        dq_ref[...] = dq.astype(dq_ref.dtype)
        row0 = pl.program_id(1) == 0
        all0 = jnp.logical_and(row0, pl.program_id(0) == 0)
        for o_ref, val, init in ((dk_ref, dk, row0), (dv_ref, dv, row0), (dgq_ref, dgq, all0), (dgk_ref, dgk, all0)):
            @pl.when(init)
            def _(o_ref=o_ref, val=val):
                o_ref[...] = val

            @pl.when(jnp.logical_not(init))
            def _(o_ref=o_ref, val=val):
                o_ref[...] += val

    return pl.pallas_call(
        body, name="cross_bwd", grid=(4, SEQ // CROSS_ROWS),
        in_specs=[q_spec, k_spec, v_spec, g_spec, g_spec, q_spec],
        out_specs=[q_spec, k_spec, k_spec, g_spec, g_spec],
        out_shape=[jax.ShapeDtypeStruct((SEQ, D_CROSS), BF16), jax.ShapeDtypeStruct((N_MEM, D_CROSS), F32),
                   jax.ShapeDtypeStruct((N_MEM, D_CROSS), F32), jax.ShapeDtypeStruct((1, CROSS_HEAD), F32),
                   jax.ShapeDtypeStruct((1, CROSS_HEAD), F32)],
        compiler_params=_params(("arbitrary", "arbitrary")),
    )(qc, kv, kv, gq, gk, do)


def _loss_epilogue(acc, residual, target):
    err = acc + residual - target
    dy = err * (1.0 / D_MODEL)
    part = jnp.sum(jnp.sum(err * err, axis=1, keepdims=True), axis=0, keepdims=True) * (0.5 / D_MODEL)
    return dy, dy, part


def _pad_heads(v):
    return jnp.pad(v.reshape(N_GROUPS, HEADS_PER_GROUP), ((0, 0), (0, 128 - HEADS_PER_GROUP))).reshape(1, DT_PAD)


def _unpad_heads(v):
    return v.reshape(v.shape[0], N_GROUPS, 128)[:, :, :HEADS_PER_GROUP].reshape(v.shape[0], N_DT)


def _rope_tables(positions):
    half = ROT // 2
    inv_freq = ROPE_THETA ** (-2.0 * jnp.arange(half, dtype=F32) / ROT)
    ang = positions.reshape(SEQ, 1).astype(F32) * inv_freq
    cos, sin = jnp.cos(ang), jnp.sin(ang)
    ones, zeros = jnp.ones((SEQ, HEAD - ROT), F32), jnp.zeros((SEQ, HEAD - ROT), F32)
    cos_h = jnp.concatenate([cos, cos, ones], axis=1)
    sin_h = jnp.concatenate([-sin, sin, zeros], axis=1)
    return jnp.tile(cos_h, (1, 2)), jnp.tile(sin_h, (1, 2))


def _add_res(acc, res):
    return (acc + res,)


def _norm_bwd_epilogue(acc, x, residual, *more):
    *part, g = more
    ct = acc + part[0] if part else acc
    _, pullback = jax.vjp(_rms, x, g)
    dx, dg = pullback(ct)
    return dx + residual, dg


def _add_res_and_norm(acc, res, g):
    y = acc + res
    return y, _rms(y, g)


def _settle(grads, *after):
    if hasattr(grads, "settle"):
        grads.settle(*after)


def _take_token(grads):
    token = getattr(grads, "token", None)
    if token is None:
        return ()
    grads.token = None
    return (token,)


def _local_step(x, mem, positions, target, p, w, more_weights=None, grads=None, h=None):
    grads = {} if grads is None else grads
    w = dict(w)
    cos, sin = _rope_tables(positions)
    gq2, gk2 = jnp.tile(p["g_q"], (1, 2)), jnp.tile(p["g_k"], (1, 2))
    bias, alog, dsk = _pad_heads(p["dt_bias"]), _pad_heads(p["a_log"]), _pad_heads(p["d_skip"])
    norm_out = [(D_MODEL, BF16, D_MODEL, 0, False)]

    if h is None:
        h = _rowwise(_norm_fn, [_full(x)], [_full(p["g_mix"])], norm_out, name="norm_in")[0]
    proj = _matmul(h, w["w_in"], mode="nn", name="in_proj", outs=[F32], n_cols=D_MAIN)
    dt_raw = _matmul(h, w["w_dt"], mode="nn", name="dt_proj", outs=[F32])
    pairs = D_ATTN // 128
    qk_rows = [(proj, 128, 0, True), (proj, 128, pairs, True), (proj, 128, 2 * pairs, True), _full(cos), _full(sin)]
    qk_vecs = [_full(gq2), _full(gk2)]
    qn, kn, vn = _rowwise(_qk_fn, qk_rows, qk_vecs, [(D_ATTN, F32, 128, 0, True)] * 3, name="qk_prep", groups=8, tr=1024)
    branches = [_attention_fwd(qn, kn, vn, b) for b in range(3)]
    merge_rows = [_full(o) for o, _ in branches] + [_full(lse) for _, lse in branches]
    attn = _rowwise(_merge_fn, merge_rows, [_full(p["g_attn_out"])], [(D_ATTN, BF16, D_ATTN, 0, False)], name="attn_merge")[0]
    xbc = _conv_fwd(proj, p["conv_w"], p["conv_b"])
    ssm, h_in = _ssd_fwd(xbc, dt_raw, bias, alog, dsk, proj, p["g_ssm_out"])
    mix = jnp.concatenate([attn, ssm], axis=1)
    if more_weights is not None:
        w.update(more_weights("mixer_done", mix))
    x1, hc = _matmul(mix, w["w_out"], mode="nn", name="out_proj", outs=[F32, BF16], extra=(x,), vecs=(p["g_cross"],),
                     epilogue=_add_res_and_norm, tm=512, tn=D_MODEL)
    memh = _rowwise(_norm_fn, [_full(mem)], [_full(p["g_mem"])], norm_out, name="norm_mem", n_rows=N_MEM, tr=N_MEM)[0]
    qc = _matmul(hc, w["w_cq"], mode="nn", name="cq_proj", outs=[F32])
    if more_weights is not None:
        w.update(more_weights("cross_started", qc))
    kv = _matmul(memh, w["w_ckv"], mode="nn", name="ckv_proj", outs=[F32])
    oc = _cross_fwd(qc, kv, p["g_cq"], p["g_ck"])
    x2, hm = _matmul(oc, w["w_co"], mode="nn", name="co_proj", outs=[F32, BF16], extra=(x1,), vecs=(p["g_mlp"],),
                     epilogue=_add_res_and_norm, tm=512, tn=D_MODEL)
    if more_weights is not None:
        w.update(more_weights("cross_done", hm))
    u, act = _matmul(hm, w["w_up"], mode="nn", name="up_proj", outs=[F32, BF16],
                     epilogue=lambda acc: (acc, jnp.square(jnp.maximum(acc, 0.0))))
    if more_weights is not None:
        w.update(more_weights("up_done", act))
    dy, dyb, loss_tiles = _matmul(act, w["w_down"], mode="nn", name="down_proj", outs=[F32, BF16], extra=(x2, target),
                                  epilogue=_loss_epilogue, tile_sums=1)
    loss = jnp.sum(loss_tiles).reshape(1, 1)

    grads["w_down"] = _matmul(act, dyb, mode="tn", name="dw_down", outs=[BF16], after=_take_token(grads))
    du = _matmul(dyb, w["w_down"], mode="nt", name="d_act", outs=[BF16], extra=(u,), after=_take_token(grads),
                 epilogue=lambda acc, uu: (acc * (2.0 * jnp.maximum(uu, 0.0)),))
    _settle(grads, du)
    grads["w_up"] = _matmul(hm, du, mode="tn", name="dw_up", outs=[BF16], col_shards=4, after=_take_token(grads))
    dx2, grads["g_mlp"] = _matmul(du, w["w_up"], mode="nt", name="d_hm", outs=[F32], extra=(x2, dy), vecs=(p["g_mlp"],),
                                  epilogue=_norm_bwd_epilogue, tile_rows=1, after=_take_token(grads), tm=512, tn=D_MODEL,
                                  tk=1024)
    _settle(grads, dx2)
    grads["w_co"] = _matmul(oc, dx2, mode="tn", name="dw_co", outs=[BF16], col_shards=4, after=_take_token(grads))
    doc = _matmul(dx2, w["w_co"], mode="nt", name="d_oc", outs=[BF16])
    dqc, dkc, dvc, grads["g_cq"], grads["g_ck"] = _cross_bwd(qc, kv, p["g_cq"], p["g_ck"], doc)
    grads["w_cq"] = _matmul(hc, dqc, mode="tn", name="dw_cq", outs=[BF16])
    dkv = jnp.concatenate([dkc, dvc], axis=1)
    grads["w_ckv"] = _matmul(memh, dkv, mode="tn", name="dw_ckv", outs=[BF16])
    dmemh = _matmul(dkv, w["w_ckv"], mode="nt", name="d_memh", outs=[F32])
    grads["g_mem"] = _rowwise_vjp(_norm_fn, [_full(mem)], [_full(p["g_mem"])], [[_full(dmemh)]], [],
                                  [(0, D_MODEL, D_MODEL, 0, False)], name="norm_mem_bwd", n_rows=N_MEM, tr=N_MEM)[0]
    dx1, grads["g_cross"] = _matmul(dqc, w["w_cq"], mode="nt", name="d_hc", outs=[F32], extra=(x1, dx2), vecs=(p["g_cross"],),
                                    epilogue=_norm_bwd_epilogue, tile_rows=1, tm=512, tn=D_MODEL)
    grads["w_out"] = _matmul(mix, dx1, mode="tn", name="dw_out", outs=[BF16])
    dmix = _matmul(dx1, w["w_out"], mode="nt", name="d_mix", outs=[F32], after=_take_token(grads))
    _settle(grads, dmix)
    merge_grads = [(i, D_ATTN, F32, D_ATTN, 0, False, None) for i in range(6)]
    *dol, grads["g_attn_out"] = _rowwise_vjp(
        _merge_fn, merge_rows, [_full(p["g_attn_out"])], [[(dmix, D_ATTN, 0, False)]],
        merge_grads, [(0, D_ATTN, D_ATTN, 0, False)], name="attn_merge_bwd", tr=256, after=_take_token(grads))
    dqkv = [_attention_bwd(qn, kn, vn, *branches[b], dol[b], dol[3 + b], b) for b in range(3)]
    qk_cts = [[(dqkv[b][i], 128, 0, True) for b in range(3)] for i in range(3)]
    dq, dk, dv, dgq2, dgk2 = _rowwise_vjp(
        _qk_fn, qk_rows, qk_vecs, qk_cts, [(i, D_ATTN, BF16, 128, 0, True, None) for i in range(3)],
        [(0, 128, 128, 0, False), (1, 128, 128, 0, False)], name="qk_prep_bwd", groups=8, tr=1024)
    grads["g_q"] = dgq2[:, :HEAD] + dgq2[:, HEAD:]
    grads["g_k"] = dgk2[:, :HEAD] + dgk2[:, HEAD:]
    dxs, db, dc, ddt, dbias, dalog, ddsk, dz, grads["g_ssm_out"] = _ssd_bwd(xbc, dt_raw, bias, alog, dsk, h_in, proj,
                                                                             p["g_ssm_out"], dmix)
    grads["dt_bias"], grads["a_log"], grads["d_skip"] = _unpad_heads(dbias), _unpad_heads(dalog), _unpad_heads(ddsk)
    dxbc_raw, dconv_w, grads["conv_b"] = _conv_bwd(proj, p["conv_w"], p["conv_b"], dxs, db, dc)
    grads["conv_w"] = dconv_w[:4]
    dproj = jnp.concatenate([dq, dk, dv, dz, dxbc_raw], axis=1)
    grads["w_main"] = _matmul(h, dproj, mode="tn", name="dw_main", outs=[BF16], out_cols=D_MAIN + N_DT)
    grads["w_dt"] = _matmul(h, ddt, mode="tn", name="dw_dt", outs=[BF16])
    dh = _matmul(dproj, w["w_in"], mode="nt", name="d_h_main", outs=[F32], after=_take_token(grads))
    grad_x, grads["g_mix"] = _matmul(ddt, w["w_dt"], mode="nt", name="d_h_dt", outs=[F32], extra=(x, dx1, dh),
                                     vecs=(p["g_mix"],), epilogue=_norm_bwd_epilogue, tile_rows=1, tm=512, tn=D_MODEL)
    return loss, grad_x, grads


MATRICES = ("w_in", "w_out", "w_cq", "w_ckv", "w_co", "w_up", "w_down")
ROW_SHARDED = ("w_out", "w_cq", "w_ckv", "w_down")
N_CHIPS = 4
ANY = pl.BlockSpec(memory_space=pl.ANY)


def _place():
    return lax.axis_index("x"), lax.axis_index("y"), lax.axis_index("c")


def _other_chips(x, y):
    return [(1 - x, y), (x, 1 - y), (1 - x, 1 - y)]


def _remote(src, dst, send_sem, recv_sem, device):
    return pltpu.make_async_remote_copy(src_ref=src, dst_ref=dst, send_sem=send_sem, recv_sem=recv_sem,
                                        device_id=device, device_id_type=MESH)


def _gathered_shape(name, shard):
    rows, cols = shard.shape
    if name == "w_in":
        return (N_CHIPS, rows, cols)
    return (N_CHIPS * rows, cols) if name in ROW_SHARDED else (rows, N_CHIPS * cols)


def _shard_window(name, ref, rows, cols, chip, half):
    r0, nr = (0, rows) if half is None else (half * (rows // 2), rows // 2)
    if name == "w_in":
        return ref.at[chip, pl.ds(r0, nr), :]
    if name in ROW_SHARDED:
        return ref.at[pl.ds(chip * rows + r0, nr), :]
    return ref.at[pl.ds(r0, nr), pl.ds(pl.multiple_of(chip * cols, 128), cols)]


def _cast_into_gathered(w, name, chip, after=()):
    rows, cols = w.shape
    tr = _tile(rows, ROW_TILE)

    def body(chip_ref, w_ref, *rest):
        rest[-1][...] = w_ref[...].astype(BF16)

    if name == "w_in":
        out_spec = pl.BlockSpec((None, tr, cols), lambda i, chip_ref: (chip_ref[0], i, 0))
    elif name in ROW_SHARDED:
        out_spec = pl.BlockSpec((tr, cols), lambda i, chip_ref: (chip_ref[0] * (rows // tr) + i, 0))
    else:
        out_spec = pl.BlockSpec((tr, cols), lambda i, chip_ref: (i, chip_ref[0]))
    grid_spec = pltpu.PrefetchScalarGridSpec(
        num_scalar_prefetch=1, grid=(rows // tr,),
        in_specs=[pl.BlockSpec((tr, cols), lambda i, chip_ref: (i, 0))] + [pl.BlockSpec(memory_space=pl.ANY)] * len(after),
        out_specs=out_spec)
    return pl.pallas_call(body, name="cast_" + name, grid_spec=grid_spec,
                          out_shape=jax.ShapeDtypeStruct(_gathered_shape(name, w), BF16),
                          compiler_params=_params(("parallel",)))(chip.reshape(1).astype(jnp.int32), w, *after)


def _w_in_columns(arr, to_shards):
    rows, piece = D_MODEL, (D_MAIN + N_DT) // N_CHIPS
    tr = ROW_TILE

    def body(a_ref, o_ref):
        for j in range(N_CHIPS):
            if to_shards:
                o_ref[j] = a_ref[:, pl.ds(piece * j, piece)]
            else:
                o_ref[:, pl.ds(piece * j, piece)] = a_ref[j]

    pieces = pl.BlockSpec((N_CHIPS, tr, piece), lambda i: (0, i, 0))
    matrix = pl.BlockSpec((tr, N_CHIPS * piece), lambda i: (i, 0))
    out_dims = (N_CHIPS, rows, piece) if to_shards else (rows, N_CHIPS * piece)
    return pl.pallas_call(
        body, name="w_in_to_shards" if to_shards else "w_in_from_shards", grid=(rows // tr,),
        in_specs=[matrix if to_shards else pieces], out_specs=pieces if to_shards else matrix,
        out_shape=jax.ShapeDtypeStruct(out_dims, arr.dtype), compiler_params=_params(("parallel",)))(arr)


HBM = pl.BlockSpec(memory_space=pltpu.HBM)
SEM = pl.BlockSpec(memory_space=pltpu.SEMAPHORE)
EFFECT = pltpu.SideEffectType.DATAFLOW_SIDE_EFFECTING


def _split_start(name, bufs, plan, counts, after=()):
    n, n_g, n_after = len(bufs), len(counts), len(after)

    def body(*refs):
        ins, sems, token = refs[:n], refs[n + n_after:n + n_after + 2 * n_g], refs[-1]
        for g, copies in enumerate(plan(ins)):
            for i, (src, dst, device, _) in enumerate(copies):
                _remote(src, dst, sems[2 * g].at[i], sems[2 * g + 1].at[i], device).start()
        token[...] = jnp.zeros_like(token)

    sem_shapes = [pltpu.SemaphoreType.DMA((cnt,)) for cnt in counts for _ in range(2)]
    res = pl.pallas_call(
        body, name=name,
        out_shape=(*sem_shapes, *[pltpu.HBM(b.shape, b.dtype) for b in bufs], jax.ShapeDtypeStruct((8, 128), F32)),
        in_specs=(*(HBM,) * n, *(ANY,) * n_after),
        out_specs=(*(SEM,) * (2 * n_g), *(HBM,) * n, pl.BlockSpec(memory_space=pltpu.VMEM)),
        input_output_aliases={i: 2 * n_g + i for i in range(n)},
        compiler_params=pltpu.CompilerParams(has_side_effects=EFFECT),
    )(*[pltpu.with_memory_space_constraint(b, pltpu.HBM) for b in bufs], *after)
    sems = [(res[2 * g], res[2 * g + 1]) for g in range(n_g)]
    return sems, list(res[2 * n_g:2 * n_g + n]), res[-1]


def _split_wait(name, bufs, sems, plan, *after):
    n = len(bufs)

    def body(*refs):
        ins, send, recv = refs[:n], refs[n], refs[n + 1]
        (copies,) = plan(ins)
        for i, (src, _, device, landing) in enumerate(copies):
            cp = _remote(src, landing, send.at[i], recv.at[i], device)
            cp.wait_send()
            cp.wait_recv()

    res = pl.pallas_call(
        body, name=name, out_shape=tuple(pltpu.HBM(b.shape, b.dtype) for b in bufs),
        in_specs=(*(HBM,) * n, SEM, SEM, *(ANY,) * len(after)), out_specs=(HBM,) * n,
        input_output_aliases={i: i for i in range(n)},
        compiler_params=pltpu.CompilerParams(has_side_effects=EFFECT),
    )(*bufs, sems[0], sems[1], *after)
    return list(res)


def _ici_plan(names, shard_shapes):
    def plan(refs):
        x, y, c = _place()
        copies = []
        for ref, name in zip(refs, names):
            win = _shard_window(name, ref, *shard_shapes[name], 2 * x + y, c)
            for px, py in _other_chips(x, y):
                copies.append((win, win, (px, py, c), _shard_window(name, ref, *shard_shapes[name], 2 * px + py, c)))
        return [copies]
    return plan


def _pass_on_plan(names, shard_shapes):
    def plan(refs):
        x, y, c = _place()
        copies = []
        for ref, name in zip(refs, names):
            for px, py in _other_chips(x, y):
                win = _shard_window(name, ref, *shard_shapes[name], 2 * px + py, c)
                copies.append((win, win, (x, y, 1 - c), _shard_window(name, ref, *shard_shapes[name], 2 * px + py, 1 - c)))
        return [copies]
    return plan


def _swap_plan(n_pairs):
    def plan(refs):
        x, y, c = _place()
        return [[(src.at[:, 1 - c], dst, (x, y, 1 - c), dst) for src, dst in zip(refs[:n_pairs], refs[n_pairs:])]]
    return plan


def _share_plan(n_pairs):
    def plan(refs):
        x, y, c = _place()
        return [[(src, dst, (x, y, 1 - c), dst)] for src, dst in zip(refs[:n_pairs], refs[n_pairs:])]
    return plan


def _scatter_plan(n_pairs):
    def plan(refs):
        x, y, c = _place()
        copies = []
        for src, dst in zip(refs[:n_pairs], refs[n_pairs:]):
            for k, (px, py) in enumerate(_other_chips(x, y)):
                copies.append((src.at[2 * px + py], dst.at[k], (px, py, c), dst.at[k]))
        return [copies]
    return plan


def _sibling_swap(arrs, name):
    n = len(arrs)

    def body(*refs):
        ins, outs, send, recv = refs[:n], refs[n:2 * n], refs[2 * n], refs[2 * n + 1]
        x, y, c = _place()
        cps = [_remote(ins[w].at[:, 1 - c], outs[w], send.at[w], recv.at[w], (x, y, 1 - c)) for w in range(n)]
        for cp in cps:
            cp.start()
        for cp in cps:
            cp.wait()

    return pl.pallas_call(
        body, name=name, in_specs=[ANY] * n, out_specs=[ANY] * n,
        out_shape=[jax.ShapeDtypeStruct((a.shape[0],) + a.shape[2:], a.dtype) for a in arrs],
        scratch_shapes=[pltpu.SemaphoreType.DMA((n,))] * 2,
    )(*arrs)


def _small_allreduce(buf, name, after=()):
    rows = buf.shape[0]

    def body(x_ref, *rest):
        out_ref, all_ref, send_sems, recv_sems, local_sem = rest[len(after):]
        x, y, c = _place()
        me, sibling, chips = (x, y, c), (x, y, 1 - c), _other_chips(x, y)

        def block(px, py, pc):
            return all_ref.at[pl.ds((4 * px + 2 * py + pc) * rows, rows), :]

        def copy(k, blk, to, src=None):
            return _remote(block(*blk) if src is None else src, block(*blk), send_sems.at[k], recv_sems.at[k], to)

        own = pltpu.make_async_copy(x_ref, block(*me), local_sem)
        own.start()
        first = [copy(0, me, sibling, src=x_ref)] + [copy(1 + j, me, (*chip, c), src=x_ref) for j, chip in enumerate(chips)]
        for cp in first:
            cp.start()
        passed = [copy(4 + j, (*chip, c), sibling) for j, chip in enumerate(chips)]
        for j, chip in enumerate(chips):
            copy(1 + j, (*chip, c), me).wait_recv()
            passed[j].start()
        copy(0, sibling, me).wait_recv()
        for j, chip in enumerate(chips):
            copy(4 + j, (*chip, 1 - c), me).wait_recv()
        for cp in first + passed:
            cp.wait_send()
        own.wait()
        acc = all_ref[pl.ds(0, rows), :]
        for d in range(1, 8):
            acc = acc + all_ref[pl.ds(d * rows, rows), :]
        out_ref[...] = acc

    vmem = pl.BlockSpec(memory_space=pltpu.VMEM)
    return pl.pallas_call(
        body, name=name, in_specs=[vmem] + [ANY] * len(after), out_specs=vmem,
        out_shape=jax.ShapeDtypeStruct(buf.shape, F32),
        scratch_shapes=[pltpu.VMEM((8 * rows, 128), F32), pltpu.SemaphoreType.DMA((7,)), pltpu.SemaphoreType.DMA((7,)),
                        pltpu.SemaphoreType.DMA],
    )(buf, *after)


ROW_TILE = 256
BIG_ROW_TILE = 1024


def _add_halves(arr, recv, c, name):
    _, _, hr, cols = arr.shape
    tr = _tile(hr, BIG_ROW_TILE)

    def body(c_ref, a_ref, r_ref, o_ref):
        o_ref[...] = (a_ref[...].astype(F32) + r_ref[...].astype(F32)).astype(o_ref.dtype)

    piece = pl.BlockSpec((None, tr, cols), lambda j, i, c_ref: (j, i, 0))
    grid_spec = pltpu.PrefetchScalarGridSpec(
        num_scalar_prefetch=1, grid=(N_CHIPS, hr // tr),
        in_specs=[pl.BlockSpec((None, None, tr, cols), lambda j, i, c_ref: (j, c_ref[0], i, 0)), piece], out_specs=piece)
    return pl.pallas_call(body, name=name, grid_spec=grid_spec, out_shape=jax.ShapeDtypeStruct(recv.shape, BF16),
                          compiler_params=_params(("parallel", "parallel")))(c.reshape(1).astype(jnp.int32), arr, recv)


def _flip_slot(d):
    return jnp.where(d == 1, 1, jnp.where(d == 3, 2, 0))


def _sum_chips(p, q, chip, name):
    _, hr, cols = p.shape
    tr = _tile(hr, BIG_ROW_TILE)

    def body(chip_ref, p_ref, q_ref, o_ref):
        j = pl.program_id(1)
        term = jnp.where(j == chip_ref[0], p_ref[...].astype(F32), q_ref[...].astype(F32))

        @pl.when(j == 0)
        def _():
            o_ref[...] = term

        @pl.when(j != 0)
        def _():
            o_ref[...] += term

    grid_spec = pltpu.PrefetchScalarGridSpec(
        num_scalar_prefetch=1, grid=(hr // tr, N_CHIPS),
        in_specs=[pl.BlockSpec((None, tr, cols), lambda i, j, chip_ref: (chip_ref[0], i, 0)),
                  pl.BlockSpec((None, tr, cols), lambda i, j, chip_ref: (_flip_slot(j ^ chip_ref[0]), i, 0))],
        out_specs=pl.BlockSpec((tr, cols), lambda i, j, chip_ref: (i, 0)))
    return pl.pallas_call(body, name=name, grid_spec=grid_spec, out_shape=jax.ShapeDtypeStruct((hr, cols), F32),
                          compiler_params=_params(("parallel", "arbitrary")))(chip.reshape(1).astype(jnp.int32), p, q)


def _adamw_halves(w, g_own, g_other, m, v, c, name):
    rows, cols = w.shape
    tr = _tile(rows // 2, ROW_TILE)
    per_half = rows // 2 // tr

    def body(c_ref, w_ref, own_ref, other_ref, m_ref, v_ref, g_ref, d_ref, nm_ref, nv_ref):
        mine = (pl.program_id(0) // per_half) == c_ref[0]
        g_ = jnp.where(mine, own_ref[...], other_ref[...])
        g_ref[...] = g_
        d_ref[...], nm_ref[...], nv_ref[...] = _adamw_math(w_ref[...], g_, m_ref[...], v_ref[...])

    blk = pl.BlockSpec((tr, cols), lambda i, c_ref: (i, 0))
    own = pl.BlockSpec((tr, cols), lambda i, c_ref: (jnp.where(i // per_half == c_ref[0], i % per_half, 0), 0))
    other = pl.BlockSpec((tr, cols), lambda i, c_ref: (jnp.where(i // per_half == c_ref[0], 0, i % per_half), 0))
    grid_spec = pltpu.PrefetchScalarGridSpec(num_scalar_prefetch=1, grid=(rows // tr,),
                                             in_specs=[blk, own, other, blk, blk], out_specs=[blk] * 4)
    return pl.pallas_call(body, name=name, grid_spec=grid_spec, out_shape=[jax.ShapeDtypeStruct(w.shape, F32)] * 4,
                          compiler_params=_params(("parallel",)))(c.reshape(1).astype(jnp.int32), w, g_own, g_other, m, v)


W_IN_COLS = (D_MAIN + N_DT) // N_CHIPS
W_IN_MAIN = W_IN_COLS // 128 * 128
W_IN_TAIL = W_IN_COLS - 128
W_IN_PARTS = ((0, W_IN_MAIN), (W_IN_TAIL, 128))


def _cast_w_in_transposed(w_t, chip, after=()):
    def body(chip_ref, w_ref, *rest):
        for start, size in W_IN_PARTS:
            rest[-1][:, pl.ds(start, size)] = w_ref[pl.ds(start, size), :].T.astype(BF16)

    grid_spec = pltpu.PrefetchScalarGridSpec(
        num_scalar_prefetch=1, grid=(D_MODEL // ROW_TILE,),
        in_specs=[pl.BlockSpec((W_IN_COLS, ROW_TILE), lambda i, chip_ref: (0, i))] + [pl.BlockSpec(memory_space=pl.ANY)] * len(after),
        out_specs=pl.BlockSpec((None, ROW_TILE, W_IN_COLS), lambda i, chip_ref: (chip_ref[0], i, 0)))
    return pl.pallas_call(body, name="cast_w_in", grid_spec=grid_spec,
                          out_shape=jax.ShapeDtypeStruct((N_CHIPS, D_MODEL, W_IN_COLS), BF16),
                          compiler_params=_params(("parallel",)))(chip.reshape(1).astype(jnp.int32), w_t, *after)


def _adamw_w_in_transposed(w_t, g_own, g_other, m_t, v_t, c):
    per_half = D_MODEL // 2 // ROW_TILE

    def body(c_ref, w_ref, own_ref, other_ref, m_ref, v_ref, g_ref, d_ref, nm_ref, nv_ref):
        mine = (pl.program_id(0) // per_half) == c_ref[0]
        for start, size in W_IN_PARTS:
            cols, rows = pl.ds(start, size), pl.ds(start, size)
            g_ = jnp.where(mine, own_ref[:, cols], other_ref[:, cols]).T
            g_ref[rows, :] = g_
            d_ref[rows, :], nm_ref[rows, :], nv_ref[rows, :] = _adamw_math(w_ref[rows, :], g_, m_ref[rows, :], v_ref[rows, :])

    blk = pl.BlockSpec((W_IN_COLS, ROW_TILE), lambda i, c_ref: (0, i))
    own = pl.BlockSpec((ROW_TILE, W_IN_COLS), lambda i, c_ref: (jnp.where(i // per_half == c_ref[0], i % per_half, 0), 0))
    other = pl.BlockSpec((ROW_TILE, W_IN_COLS), lambda i, c_ref: (jnp.where(i // per_half == c_ref[0], 0, i % per_half), 0))
    grid_spec = pltpu.PrefetchScalarGridSpec(num_scalar_prefetch=1, grid=(D_MODEL // ROW_TILE,),
                                             in_specs=[blk, own, other, blk, blk], out_specs=[blk] * 4)
    return pl.pallas_call(body, name="adamw_w_in", grid_spec=grid_spec, out_shape=[jax.ShapeDtypeStruct(w_t.shape, F32)] * 4,
                          compiler_params=_params(("parallel",)))(c.reshape(1).astype(jnp.int32), w_t, g_own, g_other, m_t, v_t)


def _adamw_math(w, g, m, v):
    m_new = ADAM_B1 * m + (1.0 - ADAM_B1) * g
    v_new = ADAM_B2 * v + (1.0 - ADAM_B2) * (g * g)
    m_hat = m_new / (1.0 - ADAM_B1 ** ADAM_STEP)
    v_hat = v_new / (1.0 - ADAM_B2 ** ADAM_STEP)
    return -ADAM_LR * (m_hat / (jnp.sqrt(v_hat) + ADAM_EPS) + ADAM_WD * w), m_new, v_new


VECTORS = ("g_mix", "g_q", "g_k", "g_attn_out", "conv_b", "dt_bias", "a_log", "d_skip", "g_ssm_out", "g_cross", "g_mem",
           "g_cq", "g_ck", "g_mlp")
WEIGHTS = ("g_mix", "w_in", "g_q", "g_k", "g_attn_out", "conv_w", "conv_b", "dt_bias", "a_log", "d_skip", "g_ssm_out", "w_out",
           "g_cross", "g_mem", "w_cq", "w_ckv", "g_cq", "g_ck", "w_co", "g_mlp", "w_up", "w_down")


def _pack(parts):
    flat = jnp.concatenate([t.reshape(-1) for t in parts])
    total = -(-flat.shape[0] // 1024) * 1024
    return jnp.pad(flat, (0, total - flat.shape[0])).reshape(total // 128, 128)


def _rows_of(n):
    return -(-n // 128)


def _slot_rows(n):
    return -(-n // 1024) * 8


def _pack_rows(parts):
    rows = []
    for t in parts:
        flat = t.reshape(-1)
        rows.append(jnp.pad(flat, (0, 128 * _slot_rows(flat.shape[0]) - flat.shape[0])).reshape(-1, 128))
    return jnp.concatenate(rows)


def _adamw_vectors(summed, chip, vectors, conv):
    groups = list(vectors) + [conv]
    offsets, row = [], 0
    for w, _, _ in groups:
        offsets.append(row)
        row += _slot_rows(w.shape[1]) if w.shape[0] == 1 else _slot_rows(w.shape[0] * N_CHIPS * w.shape[1])
    conv_blocks = _rows_of(conv[0].shape[1])

    def body(chip_ref, sum_ref, *refs):
        ins, outs = refs[:3 * len(groups)], refs[3 * len(groups):]

        def update(i, g, idx):
            w_ref, m_ref, v_ref = ins[3 * i:3 * i + 3]
            delta, new_m, new_v = _adamw_math(w_ref[idx], g, m_ref[idx], v_ref[idx])
            for o_ref, val in zip(outs[4 * i:4 * i + 4], (g, delta, new_m, new_v)):
                o_ref[idx] = val

        for i, (w, _, _) in enumerate(vectors):
            for t in range(_rows_of(w.shape[1])):
                width = min(128, w.shape[1] - 128 * t)
                update(i, sum_ref[pl.ds(offsets[i] + t, 1), pl.ds(0, width)], (slice(None), pl.ds(128 * t, width)))
        for tap in range(conv[0].shape[0]):
            for blk in range(conv_blocks):
                src = offsets[-1] + tap * N_CHIPS * conv_blocks + chip_ref[0] * conv_blocks + blk
                update(len(vectors), sum_ref[pl.ds(src, 1), :], (pl.ds(tap, 1), pl.ds(128 * blk, 128)))

    def whole(a):
        return pl.BlockSpec(a.shape, lambda i, chip_ref: (0,) * a.ndim)

    operands = [t for group in groups for t in group]
    grid_spec = pltpu.PrefetchScalarGridSpec(
        num_scalar_prefetch=1, grid=(1,), in_specs=[whole(summed)] + [whole(t) for t in operands],
        out_specs=[whole(w) for w, _, _ in groups for _ in range(4)])
    res = pl.pallas_call(body, name="adamw_vectors", grid_spec=grid_spec,
                         out_shape=[jax.ShapeDtypeStruct(w.shape, F32) for w, _, _ in groups for _ in range(4)],
                         compiler_params=_params(("arbitrary",)))(chip.reshape(1).astype(jnp.int32), summed, *operands)
    return [res[4 * i:4 * i + 4] for i in range(len(groups))]


def _unpack(buf, shapes):
    flat, out, pos = buf.reshape(-1), [], 0
    for shape in shapes:
        size = math.prod(shape)
        out.append(flat[pos:pos + size].reshape(shape))
        pos += size
    return out


def kernel(x, mem, positions, g_mix, w_in, g_q, g_k, g_attn_out, conv_w, conv_b, dt_bias, a_log, d_skip, g_ssm_out, w_out, g_cross, g_mem, w_cq, w_ckv, g_cq, g_ck, w_co, g_mlp, w_up, w_down, loss_target, m_g_mix, m_w_in, m_g_q, m_g_k, m_g_attn_out, m_conv_w, m_conv_b, m_dt_bias, m_a_log, m_d_skip, m_g_ssm_out, m_w_out, m_g_cross, m_g_mem, m_w_cq, m_w_ckv, m_g_cq, m_g_ck, m_w_co, m_g_mlp, m_w_up, m_w_down, v_g_mix, v_w_in, v_g_q, v_g_k, v_g_attn_out, v_conv_w, v_conv_b, v_dt_bias, v_a_log, v_d_skip, v_g_ssm_out, v_w_out, v_g_cross, v_g_mem, v_w_cq, v_w_ckv, v_g_cq, v_g_ck, v_w_co, v_g_mlp, v_w_up, v_w_down):
    args = dict(locals())
    weights = {n: args[n][0] for n in WEIGHTS}
    mom_m = {n: args["m_" + n][0] for n in WEIGHTS}
    mom_v = {n: args["v_" + n][0] for n in WEIGHTS}
    x_idx, y_idx, c_idx = _place()
    chip = 2 * x_idx + y_idx

    shapes = {n: weights[n].shape for n in MATRICES}
    first, mid, late = ("w_in",), ("w_out", "w_cq", "w_ckv", "w_co"), ("w_up", "w_down")
    w_in_t, m_in_t, v_in_t = (jnp.swapaxes(t, 1, 2)[0] for t in (w_in, m_w_in, v_w_in))
    w_in_buf = [_cast_w_in_transposed(w_in_t, chip)]
    taps, tap_cols = weights["conv_w"].shape
    conv_parts = _small_allreduce(_pack([jnp.zeros((N_CHIPS, taps, tap_cols), F32).at[chip].set(0.5 * weights["conv_w"])]),
                                  "gather_conv_taps")
    sems_in, w_in_buf, token = _split_start("gather_ici_start_w_in", w_in_buf, _ici_plan(first, shapes), [3], after=(conv_parts,))
    bufs = [_cast_into_gathered(weights[n], n, chip, after=(token,)) for n in mid + late]
    plan = lambda refs: (_ici_plan(mid, shapes)(refs[:4]) + _ici_plan(late[:1], shapes)(refs[4:5])
                         + _ici_plan(late[1:], shapes)(refs[5:]))
    sems_rest, bufs, token = _split_start("gather_ici_start_rest", bufs, plan, [12, 3, 3], after=(token,))
    params = {n: weights[n].reshape(1, -1) for n in VECTORS}
    h_in = _rowwise(_norm_fn, [_full(x[0])], [_full(params["g_mix"])], [(D_MODEL, BF16, D_MODEL, 0, False)], name="norm_in",
                    after=(token,))[0]
    w_in_buf = _split_wait("gather_ici_wait_w_in", w_in_buf, sems_in[0], _ici_plan(first, shapes), token, h_in, m_in_t, v_in_t)
    pass_sems, w_in_buf, token = _split_start("gather_pass_start_w_in", w_in_buf, _pass_on_plan(first, shapes), [3])
    w_in_buf = _split_wait("gather_pass_wait_w_in", w_in_buf, pass_sems[0], _pass_on_plan(first, shapes), token)
    w_in_full = _w_in_columns(w_in_buf[0], to_shards=False)
    full = {"w_in": w_in_full,
            "w_dt": jnp.pad(w_in_full[:, D_MAIN:].reshape(D_MODEL, N_GROUPS, HEADS_PER_GROUP),
                            ((0, 0), (0, 0), (0, 128 - HEADS_PER_GROUP))).reshape(D_MODEL, DT_PAD)}
    in_flight = {}

    def more_weights(stage, after):
        if stage == "mixer_done":
            got = _split_wait("gather_ici_wait_mid", bufs[:4], sems_rest[0], _ici_plan(mid, shapes), after)
            sems, got, token = _split_start("gather_pass_start_mid", got, _pass_on_plan(mid, shapes), [12])
            return dict(zip(mid, _split_wait("gather_pass_wait_mid", got, sems[0], _pass_on_plan(mid, shapes), token)))
        up, down = late[:1], late[1:]
        if stage == "cross_started":
            got = _split_wait("gather_ici_wait_w_up", bufs[4:5], sems_rest[1], _ici_plan(up, shapes), after)
            in_flight["w_up"] = _split_start("gather_pass_start_w_up", got, _pass_on_plan(up, shapes), [3])
            return {}
        if stage == "cross_done":
            sems, got, token = in_flight.pop("w_up")
            w_up_full = _split_wait("gather_pass_wait_w_up", got, sems[0], _pass_on_plan(up, shapes), token, after)
            got = _split_wait("gather_ici_wait_w_down", bufs[5:], sems_rest[2], _ici_plan(down, shapes), w_up_full[0])
            in_flight["w_down"] = _split_start("gather_pass_start_w_down", got, _pass_on_plan(down, shapes), [3])
            return dict(zip(up, w_up_full))
        sems, got, token = in_flight.pop("w_down")
        return dict(zip(down, _split_wait("gather_pass_wait_w_down", got, sems[0], _pass_on_plan(down, shapes), token, after)))

    params["conv_w"] = _unpack(conv_parts, [(N_CHIPS, taps, tap_cols)])[0].transpose(1, 0, 2).reshape(taps, N_CHIPS * tap_cols)

    groups = (("w_down",), ("w_up",), ("w_co", "w_cq", "w_ckv", "w_out"), ("w_in",))
    scattered = []

    class GradStore(dict):
        pending = None

        def __setitem__(self, name, value):
            super().__setitem__(name, value)
            if "w_main" in self and "w_dt" in self and "w_in" not in self:
                gw_in = lax.dynamic_update_slice(self["w_main"], _unpad_heads(self["w_dt"]), (0, D_MAIN))
                self["w_in"] = _w_in_columns(gw_in, to_shards=True)
            for group in groups:
                if name in group and all(n in self for n in group):
                    self.settle()
                    pieces = [self[n].reshape(N_CHIPS, 2, shapes[n][0] // 2, shapes[n][1]) for n in group]
                    if group == groups[-1]:
                        self.scatter(group, pieces, _sibling_swap(pieces, "grad_swap_" + group[0]))
                    else:
                        landing = [lax.empty((N_CHIPS,) + a.shape[2:], BF16) for a in pieces]
                        sems, thru, self.token = _split_start("grad_swap_start_" + group[0], pieces + landing,
                                                              _swap_plan(len(pieces)), [len(pieces)])
                        self.pending = (group, sems[0], thru)

        def settle(self, *after):
            if self.pending is not None:
                group, sems, thru = self.pending
                self.pending = None
                thru = _split_wait("grad_swap_wait_" + group[0], thru, sems, _swap_plan(len(group)), *after)
                self.scatter(group, thru[:len(group)], thru[len(group):])

        def scatter(self, group, pieces, from_sibling):
            sums = [_add_halves(a, r, c_idx, "add_halves_" + n) for n, a, r in zip(group, pieces, from_sibling)]
            landing = [lax.empty((3,) + s.shape[1:], BF16) for s in sums]
            sems, thru, self.token = _split_start("grad_scatter_start_" + group[0], sums + landing,
                                                  _scatter_plan(len(sums)), [3 * len(sums)])
            scattered.append((group, sems[0], thru))

    loss, grad_x, grads = _local_step(x[0], mem[0], positions[0], loss_target[0], params, full, more_weights, GradStore(),
                                      h_in)

    out_g, out_d, out_m, out_v = {}, {}, {}, {}

    def finish(entries, order, token):
        halves = {}
        for group, sems, thru in entries:
            thru = _split_wait("grad_scatter_wait_" + group[0], thru, sems, _scatter_plan(len(group)), token)
            for i, n in enumerate(group):
                halves[n] = _sum_chips(thru[i], thru[len(group) + i], chip, "sum_chips_" + n)
        sources = [halves[n] for n in order]
        landing = [lax.empty(s.shape, F32) for s in sources]
        sems, thru, token = _split_start("grad_share_start_" + order[0], sources + landing, _share_plan(len(order)),
                                         [1] * len(order))
        for i, n in enumerate(order):
            own, other = _split_wait("grad_share_wait_" + n, [thru[i], thru[len(order) + i]], sems[i], _share_plan(1), token)
            if n == "w_in":
                res_t = _adamw_w_in_transposed(w_in_t, own, other, m_in_t, v_in_t, c_idx)
                out_g[n], out_d[n], out_m[n], out_v[n] = (t.T for t in res_t)
            else:
                out_g[n], out_d[n], out_m[n], out_v[n] = _adamw_halves(weights[n], own, other, mom_m[n], mom_v[n], c_idx,
                                                                       "adamw_" + n)
            token = out_v[n]
        return token

    token = finish(scattered[:-1], ("w_cq", "w_co", "w_ckv", "w_out", "w_up", "w_down"), grad_x)
    finish(scattered[-1:], ("w_in",), token)

    names = VECTORS + ("conv_w",)
    summed = _small_allreduce(_pack_rows([grads[n] for n in names] + [loss]), "allreduce_vectors")
    total_loss = summed[sum(_slot_rows(grads[n].size) for n in names), 0]
    small_out = _adamw_vectors(summed, chip, [(args[n], args["m_" + n], args["v_" + n]) for n in VECTORS],
                               (weights["conv_w"], mom_m["conv_w"], mom_v["conv_w"]))
    for n, res in zip(names, small_out):
        out_g[n], out_d[n], out_m[n], out_v[n] = (t.reshape(weights[n].shape) for t in res)

    outs =[total_loss, grad_x[None]]
    for group in (out_g, out_d, out_m, out_v):
        outs += [group[n][None] for n in WEIGHTS]
    return tuple(outs)
```

```python
import functools
import math

import jax
import jax.numpy as jnp
from jax import lax
from jax.experimental import pallas as pl
from jax.experimental.pallas import tpu as pltpu

F32 = jnp.float32
BF16 = jnp.bfloat16

SEQ = 2048
D_MODEL = 2048
HEAD = 64
D_ATTN = 1024
D_SSM = 1024
N_GROUPS = 4
N_STATE = 128
CHUNK = 128
ATT_BLK = 128
N_MEM = 256
D_CROSS = 512
D_MAIN = 6144
N_DT = 16
DT_PAD = 512
ROT = 16
ROPE_THETA = 500000.0
EPS = 1e-6
NEG = -1e30
BRANCH_BLOCKS = (16, 4, 1)
DILATIONS = (1, 4, 16)

ADAM_LR, ADAM_B1, ADAM_B2, ADAM_EPS, ADAM_WD, ADAM_STEP = 0.001, 0.9, 0.999, 1e-08, 0.01, 10

VMEM_LIMIT = 56 * 1024 * 1024
MESH = pl.DeviceIdType.MESH


def _params(sem, **kw):
    return pltpu.CompilerParams(dimension_semantics=sem, vmem_limit_bytes=VMEM_LIMIT, **kw)


def _bdot(a, b, dims):
    return lax.dot_general(a.astype(BF16), b.astype(BF16), (dims, ((), ())), preferred_element_type=F32)


def _fdot(a, b, dims):
    return lax.dot_general(a, b, (dims, ((), ())), preferred_element_type=F32, precision=lax.Precision.HIGHEST)


NN = ((1,), (0,))
NT = ((1,), (1,))
TN = ((0,), (0,))


def _tile(n, want):
    t = min(n, want)
    while n % t:
        t //= 2
    return t


def _matmul(a, b, *, mode, name, outs, extra=(), vecs=(), epilogue=None, col_shards=1, after=(), n_cols=None, out_cols=None,
            tile_rows=0, tile_sums=0, tm=1024, tn=1024, tk=2048):
    if mode == "nn":
        (m, k), n = a.shape, b.shape[1]
    elif mode == "nt":
        (m, k), n = a.shape, b.shape[0]
    else:
        (k, m), n = a.shape, b.shape[1]
    n = n if n_cols is None else n_cols
    tm, tn, tk = _tile(m, tm), _tile(n // col_shards, tn), _tile(k, tk)
    nk = k // tk
    per_shard = n // col_shards // tn
    dims = {"nn": NN, "nt": NT, "tn": TN}[mode]
    a_spec = pl.BlockSpec((tk, tm), lambda i, j, kk: (kk, i)) if mode == "tn" else pl.BlockSpec((tm, tk), lambda i, j, kk: (i, kk))
    b_spec = pl.BlockSpec((tn, tk), lambda i, j, kk: (j, kk)) if mode == "nt" else pl.BlockSpec((tk, tn), lambda i, j, kk: (kk, j))
    o_spec = pl.BlockSpec((tm, tn), lambda i, j, kk: (i, j))
    n_extra, n_out, n_after = len(extra) + len(vecs), len(outs), len(after)

    def body(a_ref, b_ref, *rest):
        extra_refs, out_refs, acc_ref = rest[:n_extra], rest[n_extra + n_after:-1], rest[-1]

        def finish(acc):
            res = (acc,) if epilogue is None else epilogue(acc, *[e[...] for e in extra_refs])
            for o_ref, r in zip(out_refs[:n_out], res):
                o_ref[...] = r.astype(o_ref.dtype)
            for o_ref, r in zip(out_refs[n_out:], res[n_out:]):
                o_ref[...] = jnp.broadcast_to(r, o_ref.shape)

        if nk == 1:
            finish(_bdot(a_ref[...], b_ref[...], dims))
            return
        kk = pl.program_id(2)

        @pl.when(kk == 0)
        def _():
            acc_ref[...] = jnp.zeros_like(acc_ref)

        acc_ref[...] += _bdot(a_ref[...], b_ref[...], dims)

        @pl.when(kk == nk - 1)
        def _():
            finish(acc_ref[...])

    if col_shards == 1:
        out_specs, out_dims = [o_spec] * n_out, (m, n if out_cols is None else out_cols)
    else:
        sharded = pl.BlockSpec((None, tm, tn), lambda i, j, kk: (j // per_shard, i, j % per_shard))
        out_specs, out_dims = [sharded] * n_out, (col_shards, m, n // col_shards)
    res = pl.pallas_call(
        body, name=name, grid=(m // tm, n // tn, nk),
        in_specs=[a_spec, b_spec] + [o_spec] * len(extra) + [pl.BlockSpec((1, tn), lambda i, j, kk: (0, j))] * len(vecs)
        + [pl.BlockSpec(memory_space=pl.ANY)] * n_after,
        out_specs=out_specs + [pl.BlockSpec((8, tn), lambda i, j, kk: (i, j))] * tile_rows
        + [pl.BlockSpec((8, 128), lambda i, j, kk: (i, j))] * tile_sums,
        out_shape=[jax.ShapeDtypeStruct(out_dims, dt) for dt in outs] + [jax.ShapeDtypeStruct((m // tm * 8, n), F32)] * tile_rows
        + [jax.ShapeDtypeStruct((m // tm * 8, n // tn * 128), F32)] * tile_sums,
        scratch_shapes=[pltpu.VMEM((tm, tn) if nk > 1 else (8, 128), F32)],
        compiler_params=_params(("parallel", "parallel", "arbitrary")),
    )(a, b, *extra, *vecs, *after)
    res = (list(res[:n_out]) + [jnp.sum(t[::8], axis=0, keepdims=True) for t in res[n_out:n_out + tile_rows]]
           + [t[::8, ::128] for t in res[n_out + tile_rows:]])
    return res[0] if len(res) == 1 else res


def _row_spec(tr, bw, cb, per_group):
    return pl.BlockSpec((tr, bw), (lambda g, i: (i, cb + g)) if per_group else (lambda g, i: (i, cb)))


def _vec_spec(bw, cb, per_group):
    return pl.BlockSpec((1, bw), (lambda g, i: (0, cb + g)) if per_group else (lambda g, i: (0, cb)))


def _rowwise(fn, rows, vecs, outs, *, name, n_rows=SEQ, tr=512, groups=1, after=()):
    n_r, n_v, n_after = len(rows), len(vecs), len(after)

    def body(*refs):
        vals = [r[...].astype(F32) for r in refs[:n_r + n_v]]
        res = fn(*vals)
        for o_ref, r in zip(refs[n_r + n_v + n_after:], res):
            o_ref[...] = r.astype(o_ref.dtype)

    res = pl.pallas_call(
        body, name=name, grid=(groups, n_rows // tr),
        in_specs=[_row_spec(tr, bw, cb, pg) for _, bw, cb, pg in rows] + [_vec_spec(bw, cb, pg) for _, bw, cb, pg in vecs]
        + [pl.BlockSpec(memory_space=pl.ANY)] * n_after,
        out_specs=[_row_spec(tr, bw, cb, pg) for _, _, bw, cb, pg in outs],
        out_shape=[jax.ShapeDtypeStruct((n_rows, w), dt) for w, dt, _, _, _ in outs],
        compiler_params=_params(("parallel", "parallel")),
    )(*[r[0] for r in rows], *[v[0] for v in vecs], *after)
    return res


def _rowwise_vjp(fn, rows, vecs, cts, row_grads, vec_grads, *, name, n_rows=SEQ, tr=512, groups=1, after=()):
    n_r, n_v, n_after = len(rows), len(vecs), len(after)
    ct_ops = [op for group in cts for op in group]
    ct_sizes = [len(group) for group in cts]
    res_ops = [g[6] for g in row_grads if g[6] is not None]
    n_ct, n_res, n_rg = len(ct_ops), len(res_ops), len(row_grads)

    def body(*refs):
        vals = [r[...].astype(F32) for r in refs[:n_r + n_v]]
        pos = n_r + n_v
        ct_vals = []
        for size in ct_sizes:
            acc = refs[pos][...].astype(F32)
            for t in range(1, size):
                acc = acc + refs[pos + t][...].astype(F32)
            ct_vals.append(acc)
            pos += size
        res_refs = refs[pos:pos + n_res]
        out_refs = refs[pos + n_res + n_after:]
        _, pullback = jax.vjp(fn, *vals)
        grads = pullback(tuple(ct_vals))
        r_i = 0
        for o_ref, g in zip(out_refs[:n_rg], row_grads):
            val = grads[g[0]]
            if g[6] is not None:
                val = val + res_refs[r_i][...].astype(F32)
                r_i += 1
            o_ref[...] = val.astype(o_ref.dtype)
        first = (pl.program_id(1) == 0)
        for o_ref, g in zip(out_refs[n_rg:], vec_grads):
            val = jnp.sum(grads[n_r + g[0]], axis=0, keepdims=True)
            init = first if g[4] else jnp.logical_and(first, pl.program_id(0) == 0)

            @pl.when(init)
            def _(o_ref=o_ref, val=val):
                o_ref[...] = val

            @pl.when(jnp.logical_not(init))
            def _(o_ref=o_ref, val=val):
                o_ref[...] += val

    in_specs = [_row_spec(tr, bw, cb, pg) for _, bw, cb, pg in rows] + [_vec_spec(bw, cb, pg) for _, bw, cb, pg in vecs]
    in_specs += [_row_spec(tr, bw, cb, pg) for _, bw, cb, pg in ct_ops + res_ops] + [pl.BlockSpec(memory_space=pl.ANY)] * n_after
    out_specs =[_row_spec(tr, g[3], g[4], g[5]) for g in row_grads] + [_vec_spec(g[2], g[3], g[4]) for g in vec_grads]
    out_shape = [jax.ShapeDtypeStruct((n_rows, g[1]), g[2]) for g in row_grads]
    out_shape += [jax.ShapeDtypeStruct((1, g[1]), F32) for g in vec_grads]
    return pl.pallas_call(
        body, name=name, grid=(groups, n_rows // tr),
        in_specs=in_specs, out_specs=out_specs, out_shape=out_shape,
        compiler_params=_params(("arbitrary", "arbitrary")),
    )(*[r[0] for r in rows], *[v[0] for v in vecs], *[c[0] for c in ct_ops], *[r[0] for r in res_ops], *after)


def _full(arr, width=None):
    return (arr, arr.shape[1] if width is None else width, 0, False)


def _make_xor(sh):
    def raw(x):
        n = x.shape[-1]
        lane = lax.broadcasted_iota(jnp.int32, x.shape, x.ndim - 1)
        up = pltpu.roll(x, n - sh, x.ndim - 1)
        down = pltpu.roll(x, sh, x.ndim - 1)
        return jnp.where((lane & sh) == 0, up, down)

    f = jax.custom_vjp(raw)
    f.defvjp(lambda x: (raw(x), None), lambda _, ct: (raw(ct),))
    return f


_SWAP_ROPE_HALVES = _make_xor(ROT // 2)


def _head_sum(x):
    n = x.shape[-1]
    same_head = (lax.broadcasted_iota(jnp.int32, (n, n), 0) // HEAD) == (lax.broadcasted_iota(jnp.int32, (n, n), 1) // HEAD)
    return _fdot(x, same_head.astype(F32), NN)


def _rms(x, g):
    return x * lax.rsqrt(jnp.mean(x * x, axis=-1, keepdims=True) + EPS) * g


def _head_rms_rope(x, g, cos, sin, scale):
    y = x * lax.rsqrt(_head_sum(x * x) * (1.0 / HEAD) + EPS) * g
    return (y * cos + _SWAP_ROPE_HALVES(y) * sin) * scale


def _qk_fn(q, k, v, cos, sin, gq, gk):
    return (_head_rms_rope(q, gq, cos, sin, HEAD ** -0.5), _head_rms_rope(k, gk, cos, sin, 1.0), v)


def _norm_fn(x, g):
    return (_rms(x, g),)


def _merge_fn(o0, o1, o2, l0, l1, l2, g):
    m = lax.stop_gradient(jnp.maximum(jnp.maximum(l0, l1), l2))
    e0, e1, e2 = jnp.exp(l0 - m), jnp.exp(l1 - m), jnp.exp(l2 - m)
    mix = (e0 * o0 + e1 * o1 + e2 * o2) / (e0 + e1 + e2)
    return (_rms(mix, g),)


def _gate_fn(y, z, g):
    return (_rms(y * (z * jax.nn.sigmoid(z)), g),)


def _attn_pair(q, kc, vc, kp=None, vp=None, has_prev=None):
    pick0, pick1 = _head_picks()
    k_band, v_band, mask = _attn_band(kc, vc, kp, vp, has_prev)
    s = jnp.where(mask, _bdot(jnp.concatenate([q * pick0, q * pick1], axis=0), k_band, NT), NEG)
    m = jnp.max(s, axis=-1, keepdims=True)
    p = jnp.exp(s - m)
    den = jnp.sum(p, axis=-1, keepdims=True)
    acc = _bdot(p, v_band, NN) * (1.0 / den)
    lse_rows = m + jnp.log(den)
    o = pick0 * acc[:ATT_BLK] + pick1 * acc[ATT_BLK:]
    lse = pick0 * lse_rows[:ATT_BLK] + pick1 * lse_rows[ATT_BLK:]
    return o, lse


def _head_picks():
    lane = lax.broadcasted_iota(jnp.int32, (1, 2 * HEAD), 1)
    return (lane < HEAD).astype(F32), (lane >= HEAD).astype(F32)


def _attn_band(kc, vc, kp, vp, has_prev):
    n_keys = ATT_BLK if kp is None else 2 * ATT_BLK
    qi = lax.broadcasted_iota(jnp.int32, (2 * ATT_BLK, n_keys), 0) & (ATT_BLK - 1)
    kj = lax.broadcasted_iota(jnp.int32, (2 * ATT_BLK, n_keys), 1)
    if kp is None:
        return kc, vc, qi >= kj
    in_prev = jnp.logical_and(jnp.logical_and(kj < ATT_BLK, kj >= qi), has_prev)
    mask = jnp.logical_or(in_prev, jnp.logical_and(kj >= ATT_BLK, qi >= kj - ATT_BLK))
    return jnp.concatenate([kp, kc], axis=0), jnp.concatenate([vp, vc], axis=0), mask


def _attn_config(b):
    r = DILATIONS[b]
    return r, ATT_BLK * r, (D_ATTN if r == 1 else 128), BRANCH_BLOCKS[b] > 1


RESIDUES_UNROLLED = 16


def _for_residues(r, fn):
    if r <= RESIDUES_UNROLLED:
        for rho in range(r):
            fn(rho)
    else:
        def step(t, carry):
            for u in range(RESIDUES_UNROLLED):
                fn(RESIDUES_UNROLLED * t + u)
            return carry

        lax.fori_loop(0, r // RESIDUES_UNROLLED, step, 0)


def _strided_rows(start, r):
    if r > 1:
        return pl.ds(start, ATT_BLK, stride=r)
    return pl.ds(start if isinstance(start, int) else pl.multiple_of(start, ATT_BLK), ATT_BLK)


def _attention_fwd(qn, kn, vn, b):
    r, rows, lanes, with_prev = _attn_config(b)
    cur = pl.BlockSpec((rows, lanes), lambda g, n: (n, g))
    prev = pl.BlockSpec((rows, lanes), lambda g, n: (jnp.maximum(n - 1, 0), g))

    def body(*refs):
        ins, (o_ref, l_ref) = refs[:-2], refs[-2:]
        has_prev = pl.program_id(1) > 0

        def one(rho):
            sub = _strided_rows(rho, r)
            for pair in range(lanes // 128):
                sl = pl.ds(pair * 128, 128)
                args = [ref[sub, sl] for ref in ins] + ([has_prev] if with_prev else [])
                o_ref[sub, sl], l_ref[sub, sl] = _attn_pair(*args)

        _for_residues(r, one)

    operands = (qn, kn, vn, kn, vn) if with_prev else (qn, kn, vn)
    return pl.pallas_call(
        body, name="attn_fwd_%d" % r, grid=(D_ATTN // lanes, SEQ // rows),
        in_specs=[cur, cur, cur] + ([prev, prev] if with_prev else []), out_specs=[cur, cur],
        out_shape=[jax.ShapeDtypeStruct((SEQ, D_ATTN), F32)] * 2,
        compiler_params=_params(("parallel", "parallel")),
    )(*operands)


def _attn_pair_bwd(q, kc, vc, kp, vp, o, lse, do, dl, has_prev):
    pick0, pick1 = _head_picks()
    lane = lax.broadcasted_iota(jnp.int32, (1, 2 * HEAD), 1)
    k_band, v_band, mask = _attn_band(kc, vc, kp, vp, has_prev)
    q2 = jnp.concatenate([q * pick0, q * pick1], axis=0)
    do2 = jnp.concatenate([do * pick0, do * pick1], axis=0)
    lse2 = jnp.concatenate([jnp.sum(lse * (lane == 0).astype(F32), axis=-1, keepdims=True),
                            jnp.sum(lse * (lane == HEAD).astype(F32), axis=-1, keepdims=True)], axis=0)
    base = jnp.sum(jnp.concatenate([dl * pick0, dl * pick1], axis=0) - do2 * jnp.concatenate([o, o], axis=0),
                   axis=-1, keepdims=True)
    p = jnp.exp(jnp.where(mask, _bdot(q2, k_band, NT), NEG) - lse2)
    ds = p * (_bdot(do2, v_band, NT) + base)
    dq2 = _bdot(ds, k_band, NN)
    dq = pick0 * dq2[:ATT_BLK] + pick1 * dq2[ATT_BLK:]
    dk, dv = _bdot(ds, q2, TN), _bdot(p, do2, TN)
    if kp is None:
        return dq, dk, dv
    return dq, dk[ATT_BLK:], dv[ATT_BLK:], dk[:ATT_BLK], dv[:ATT_BLK]


def _attention_bwd(qn, kn, vn, o, lse, do, dl, b):
    r, rows, lanes, with_prev = _attn_config(b)
    cur = pl.BlockSpec((rows, lanes), lambda g, n: (n, g))
    prev = pl.BlockSpec((rows, lanes), lambda g, n: (jnp.maximum(n - 1, 0), g))
    whole = pl.BlockSpec((SEQ, lanes), lambda g, n: (0, g))
    n_in = 5 if with_prev else 3

    def body(*refs):
        ins, (o_ref, l_ref, do_ref, dl_ref, dq_ref, dk_ref, dv_ref) = refs[:n_in], refs[n_in:]
        n = pl.program_id(1)

        @pl.when(n == 0)
        def _():
            dk_ref[...] = jnp.zeros_like(dk_ref)
            dv_ref[...] = jnp.zeros_like(dv_ref)

        def one(rho):
            sub = _strided_rows(rho, r)
            sub_c = _strided_rows(n * rows + rho, r)
            sub_p = _strided_rows(jnp.maximum(n - 1, 0) * rows + rho, r)
            for pair in range(lanes // 128):
                sl = pl.ds(pair * 128, 128)
                vals = [ref[sub, sl] for ref in ins] + ([] if with_prev else [None, None])
                grads = _attn_pair_bwd(*vals, o_ref[sub, sl], l_ref[sub, sl], do_ref[sub, sl], dl_ref[sub, sl], n > 0)
                dq_ref[sub, sl] = grads[0]
                dk_ref[sub_c, sl] += grads[1]
                dv_ref[sub_c, sl] += grads[2]
                if with_prev:
                    dk_ref[sub_p, sl] += grads[3]
                    dv_ref[sub_p, sl] += grads[4]

        _for_residues(r, one)

    operands = (qn, kn, vn, kn, vn) if with_prev else (qn, kn, vn)
    return pl.pallas_call(
        body, name="attn_bwd_%d" % r, grid=(D_ATTN // lanes, SEQ // rows),
        in_specs=[cur, cur, cur] + ([prev, prev] if with_prev else []) + [cur] * 4, out_specs=[cur, whole, whole],
        out_shape=[jax.ShapeDtypeStruct((SEQ, D_ATTN), F32)] * 3,
        compiler_params=_params(("parallel", "arbitrary")),
    )(*operands, o, lse, do, dl)


CONV_COLS = 256
XBC_BLOCK0 = (3 * D_ATTN + D_SSM) // CONV_COLS


def _shift_rows(x, s):
    n = x.shape[0]
    t = lax.broadcasted_iota(jnp.int32, x.shape, 0)
    if s >= 0:
        return jnp.where(t >= s, pltpu.roll(x, s, 0), 0.0)
    return jnp.where(t < n + s, pltpu.roll(x, n + s, 0), 0.0)


def _conv_pre(x, w_ref, b_ref):
    delayed = [_shift_rows(x, 3 - k) for k in range(3)]
    pre = b_ref[...] + w_ref[3:4, :] * x
    for k in range(3):
        pre = pre + w_ref[k:k + 1, :] * delayed[k]
    return pre, delayed


def _conv_fwd(proj, conv_w, conv_b):
    cols = conv_w.shape[1]

    def body(x_ref, w_ref, b_ref, o_ref):
        pre, _ = _conv_pre(x_ref[...], w_ref, b_ref)
        o_ref[...] = pre * jax.nn.sigmoid(pre)

    blk = pl.BlockSpec((SEQ, CONV_COLS), lambda j: (0, j))
    return pl.pallas_call(
        body, name="conv_fwd", grid=(cols // CONV_COLS,),
        in_specs=[pl.BlockSpec((SEQ, CONV_COLS), lambda j: (0, XBC_BLOCK0 + j)),
                  pl.BlockSpec((4, CONV_COLS), lambda j: (0, j)), pl.BlockSpec((1, CONV_COLS), lambda j: (0, j))],
        out_specs=blk, out_shape=jax.ShapeDtypeStruct((SEQ, cols), F32),
        compiler_params=_params(("parallel",)),
    )(proj, conv_w, conv_b)


def _conv_bwd(proj, conv_w, conv_b, dxs, db, dc):
    cols = conv_w.shape[1]
    x_blocks, b_blocks = dxs.shape[1] // CONV_COLS, db.shape[1] // CONV_COLS

    def body(x_ref, w_ref, b_ref, dxs_ref, db_ref_in, dc_ref_in, dx_ref, dw_ref, db_ref):
        j = pl.program_id(0)
        dy = jnp.where(j < x_blocks, dxs_ref[...], jnp.where(j < x_blocks + b_blocks, db_ref_in[...], dc_ref_in[...]))
        x = x_ref[...]
        pre, delayed = _conv_pre(x, w_ref, b_ref)
        sg = jax.nn.sigmoid(pre)
        dpre = dy * (sg * (1.0 + pre * (1.0 - sg)))
        db_ref[...] = jnp.sum(dpre, axis=0, keepdims=True)
        dx = w_ref[3:4, :] * dpre
        dw_ref[3:4, :] = jnp.sum(dpre * x, axis=0, keepdims=True)
        for k in range(3):
            dx = dx + w_ref[k:k + 1, :] * _shift_rows(dpre, k - 3)
            dw_ref[k:k + 1, :] = jnp.sum(dpre * delayed[k], axis=0, keepdims=True)
        dw_ref[4:8, :] = jnp.zeros((4, CONV_COLS), F32)
        dx_ref[...] = dx.astype(dx_ref.dtype)

    blk = pl.BlockSpec((SEQ, CONV_COLS), lambda j: (0, j))
    parts = [pl.BlockSpec((SEQ, CONV_COLS), lambda j: (0, jnp.minimum(j, x_blocks - 1))),
             pl.BlockSpec((SEQ, CONV_COLS), lambda j: (0, jnp.clip(j - x_blocks, 0, b_blocks - 1))),
             pl.BlockSpec((SEQ, CONV_COLS), lambda j: (0, jnp.clip(j - x_blocks - b_blocks, 0, b_blocks - 1)))]
    return pl.pallas_call(
        body, name="conv_bwd", grid=(cols // CONV_COLS,),
        in_specs=[pl.BlockSpec((SEQ, CONV_COLS), lambda j: (0, XBC_BLOCK0 + j)),
                  pl.BlockSpec((4, CONV_COLS), lambda j: (0, j)), pl.BlockSpec((1, CONV_COLS), lambda j: (0, j))] + parts,
        out_specs=[blk, pl.BlockSpec((8, CONV_COLS), lambda j: (0, j)), pl.BlockSpec((1, CONV_COLS), lambda j: (0, j))],
        out_shape=[jax.ShapeDtypeStruct((SEQ, cols), BF16), jax.ShapeDtypeStruct((8, cols), F32),
                   jax.ShapeDtypeStruct((1, cols), F32)],
        compiler_params=_params(("parallel",)),
    )(proj, conv_w, conv_b, dxs, db, dc)


HEADS_PER_GROUP = 4


GROUP_WIDTH = HEADS_PER_GROUP * HEAD


def _ssd_chunk(x, bm, cm, dtr, bias, alog, dsk, h):
    row = lax.broadcasted_iota(jnp.int32, (CHUNK, CHUNK), 0)
    col = lax.broadcasted_iota(jnp.int32, (CHUNK, CHUNK), 1)
    causal = row >= col
    z = dtr + bias
    dt = jnp.maximum(z, 0.0) + jnp.log(1.0 + jnp.exp(-jnp.abs(z)))
    acs = _fdot(causal.astype(F32), dt * -jnp.exp(alog), NN)
    acs_t, dt_t = acs.T, dt.T
    cb = _bdot(cm, bm, NT)
    lane = lax.broadcasted_iota(jnp.int32, (1, CHUNK), 1)
    sub = lax.broadcasted_iota(jnp.int32, (CHUNK, 1), 0)
    wide = lax.broadcasted_iota(jnp.int32, (1, GROUP_WIDTH), 1) // HEAD
    tall = lax.broadcasted_iota(jnp.int32, (GROUP_WIDTH, 1), 0) // HEAD
    acs_last = jnp.sum(acs * (sub == CHUNK - 1).astype(F32), axis=0, keepdims=True)
    to_lanes = (lax.broadcasted_iota(jnp.int32, (CHUNK, GROUP_WIDTH), 0)
                == lax.broadcasted_iota(jnp.int32, (CHUNK, GROUP_WIDTH), 1) // HEAD).astype(F32)
    grow = _fdot(jnp.exp(acs), to_lanes, NN)
    keep = _fdot(jnp.exp(acs_last - acs) * dt, to_lanes, NN)
    w_parts, x_parts, skip, carry = [], [], 0.0, 0.0
    for j in range(HEADS_PER_GROUP):
        on_lane, on_sub = (lane == j).astype(F32), (sub == j).astype(F32)
        acs_c = jnp.sum(acs * on_lane, axis=1, keepdims=True)
        acs_r = jnp.sum(acs_t * on_sub, axis=0, keepdims=True)
        dt_r = jnp.sum(dt_t * on_sub, axis=0, keepdims=True)
        w_parts.append(cb * jnp.exp(jnp.where(causal, acs_c - acs_r, NEG)) * dt_r)
        x_parts.append(x * (wide == j).astype(F32))
        skip = skip + jnp.sum(dsk * on_lane, axis=1, keepdims=True) * (wide == j).astype(F32)
        carry = carry + jnp.sum(jnp.exp(acs_last) * on_lane, axis=1, keepdims=True) * (tall == j).astype(F32)
    y_diag = _bdot(jnp.concatenate(w_parts, axis=1), jnp.concatenate(x_parts, axis=0), NN)
    y = y_diag + _bdot(cm, h, NT) * grow + skip * x
    return y, h * carry + _bdot(x * keep, bm, TN)


GROUPS_PER_STEP = 4
SSD_STEPS = N_GROUPS // GROUPS_PER_STEP


def _ssd_specs(reverse):
    n_chunks = SEQ // CHUNK
    c_of = (lambda c: n_chunks - 1 - c) if reverse else (lambda c: c)
    x_w, n_w, dt_w = GROUPS_PER_STEP * GROUP_WIDTH, GROUPS_PER_STEP * N_STATE, GROUPS_PER_STEP * 128
    x_spec = pl.BlockSpec((CHUNK, x_w), lambda g, c: (c_of(c), g))
    b_spec = pl.BlockSpec((CHUNK, n_w), lambda g, c: (c_of(c), D_SSM // n_w + g))
    c_spec = pl.BlockSpec((CHUNK, n_w), lambda g, c: (c_of(c), (D_SSM + N_GROUPS * N_STATE) // n_w + g))
    dt_spec = pl.BlockSpec((CHUNK, dt_w), lambda g, c: (c_of(c), g))
    vec_spec = pl.BlockSpec((1, dt_w), lambda g, c: (0, g))
    h_spec = pl.BlockSpec((None, GROUPS_PER_STEP, GROUP_WIDTH, N_STATE), lambda g, c: (c_of(c), g, 0, 0))
    return x_spec, b_spec, c_spec, dt_spec, vec_spec, h_spec


def _group_slices(u):
    return pl.ds(u * GROUP_WIDTH, GROUP_WIDTH), pl.ds(u * N_STATE, N_STATE), pl.ds(u * 128, 128)


def _ssd_gated_chunk(x, bm, cm, dtr, bias, alog, dsk, h, z, g_out):
    y, h_new = _ssd_chunk(x, bm, cm, dtr, bias, alog, dsk, h)
    return _gate_fn(y, z, g_out)[0], h_new


def _ssd_gate_specs(reverse):
    x_spec = _ssd_specs(reverse)[0]
    z_block0 = 3 * D_ATTN // x_spec.block_shape[1]
    z_spec = pl.BlockSpec(x_spec.block_shape, lambda g, c: (x_spec.index_map(g, c)[0], z_block0 + g))
    return z_spec, pl.BlockSpec((1, x_spec.block_shape[1]), lambda g, c: (0, g))


def _ssd_fwd(xbc, dt_raw, bias, alog, dsk, proj, g_out):
    x_spec, b_spec, c_spec, dt_spec, vec_spec, h_spec = _ssd_specs(False)
    z_spec, g_spec = _ssd_gate_specs(False)

    def body(x_ref, b_ref, c_ref, dt_ref, bias_ref, alog_ref, dsk_ref, z_ref, g_ref, ssm_ref, hin_ref, h_scr):
        @pl.when(pl.program_id(1) == 0)
        def _():
            h_scr[...] = jnp.zeros_like(h_scr)

        for u in range(GROUPS_PER_STEP):
            xs, ns, ds = _group_slices(u)
            h = h_scr[u]
            hin_ref[u] = h
            ssm, h_scr[u] = _ssd_gated_chunk(x_ref[:, xs], b_ref[:, ns], c_ref[:, ns], dt_ref[:, ds], bias_ref[:, ds],
                                             alog_ref[:, ds], dsk_ref[:, ds], h, z_ref[:, xs], g_ref[:, xs])
            ssm_ref[:, xs] = ssm.astype(ssm_ref.dtype)

    return pl.pallas_call(
        body, name="ssd_fwd", grid=(SSD_STEPS, SEQ // CHUNK),
        in_specs=[x_spec, b_spec, c_spec, dt_spec, vec_spec, vec_spec, vec_spec, z_spec, g_spec],
        out_specs=[x_spec, h_spec],
        out_shape=[jax.ShapeDtypeStruct((SEQ, D_SSM), BF16),
                   jax.ShapeDtypeStruct((SEQ // CHUNK, N_GROUPS, GROUP_WIDTH, N_STATE), F32)],
        scratch_shapes=[pltpu.VMEM((GROUPS_PER_STEP, GROUP_WIDTH, N_STATE), F32)],
        compiler_params=_params(("parallel", "arbitrary")),
    )(xbc, xbc, xbc, dt_raw, bias, alog, dsk, proj, g_out)


def _ssd_bwd(xbc, dt_raw, bias, alog, dsk, h_in, proj, g_out, dmix):
    x_spec, b_spec, c_spec, dt_spec, vec_spec, h_spec = _ssd_specs(True)
    z_spec, g_spec = _ssd_gate_specs(True)
    ct_block0 = D_ATTN // x_spec.block_shape[1]
    ct_spec = pl.BlockSpec(x_spec.block_shape, lambda g, c: (x_spec.index_map(g, c)[0], ct_block0 + g))

    def body(x_ref, b_ref, c_ref, dt_ref, bias_ref, alog_ref, dsk_ref, hin_ref, z_ref, g_ref, ct_ref,
             dx_ref, db_ref, dc_ref, ddt_ref, dbias_ref, dalog_ref, ddsk_ref, dz_ref, dg_ref, dh_scr):
        first = pl.program_id(1) == 0

        @pl.when(first)
        def _():
            dh_scr[...] = jnp.zeros_like(dh_scr)

        for u in range(GROUPS_PER_STEP):
            xs, ns, ds = _group_slices(u)
            _, pullback = jax.vjp(_ssd_gated_chunk, x_ref[:, xs], b_ref[:, ns], c_ref[:, ns], dt_ref[:, ds], bias_ref[:, ds],
                                  alog_ref[:, ds], dsk_ref[:, ds], hin_ref[u], z_ref[:, xs], g_ref[:, xs])
            g = pullback((ct_ref[:, xs], dh_scr[u]))
            dx_ref[:, xs], db_ref[:, ns], dc_ref[:, ns] = g[0], g[1], g[2]
            ddt_ref[:, ds] = g[3].astype(ddt_ref.dtype)
            dh_scr[u] = g[7]
            dz_ref[:, xs] = g[8].astype(dz_ref.dtype)
            sums = ((dbias_ref, g[4], ds), (dalog_ref, g[5], ds), (ddsk_ref, g[6], ds),
                    (dg_ref, jnp.sum(g[9], axis=0, keepdims=True), xs))
            for o_ref, val, lanes in sums:
                @pl.when(first)
                def _(o_ref=o_ref, val=val, lanes=lanes):
                    o_ref[:, lanes] = val

                @pl.when(jnp.logical_not(first))
                def _(o_ref=o_ref, val=val, lanes=lanes):
                    o_ref[:, lanes] += val

    n_chunks = SEQ // CHUNK
    out_b = pl.BlockSpec((CHUNK, GROUPS_PER_STEP * N_STATE), lambda g, c: (n_chunks - 1 - c, g))
    return pl.pallas_call(
        body, name="ssd_bwd", grid=(SSD_STEPS, n_chunks),
        in_specs=[x_spec, b_spec, c_spec, dt_spec, vec_spec, vec_spec, vec_spec, h_spec, z_spec, g_spec, ct_spec],
        out_specs=[x_spec, out_b, out_b, dt_spec, vec_spec, vec_spec, vec_spec, x_spec, g_spec],
        out_shape=[jax.ShapeDtypeStruct((SEQ, D_SSM), F32), jax.ShapeDtypeStruct((SEQ, N_GROUPS * N_STATE), F32),
                   jax.ShapeDtypeStruct((SEQ, N_GROUPS * N_STATE), F32), jax.ShapeDtypeStruct((SEQ, DT_PAD), BF16),
                   jax.ShapeDtypeStruct((1, DT_PAD), F32), jax.ShapeDtypeStruct((1, DT_PAD), F32),
                   jax.ShapeDtypeStruct((1, DT_PAD), F32), jax.ShapeDtypeStruct((SEQ, D_SSM), BF16),
                   jax.ShapeDtypeStruct((1, D_SSM), F32)],
        scratch_shapes=[pltpu.VMEM((GROUPS_PER_STEP, GROUP_WIDTH, N_STATE), F32)],
        compiler_params=_params(("parallel", "arbitrary")),
    )(xbc, xbc, xbc, dt_raw, bias, alog, dsk, h_in, proj, g_out, dmix)


CROSS_HEAD = 128
CROSS_ROWS = 1024


def _cross_head(q, k, v, gq, gk):
    qn = _rms(q, gq) * (CROSS_HEAD ** -0.5)
    kn = _rms(k, gk)
    s = _bdot(qn, kn, NT)
    p = jnp.exp(s - lax.stop_gradient(jnp.max(s, axis=-1, keepdims=True)))
    return _bdot(p, v, NN) * (1.0 / jnp.sum(p, axis=-1, keepdims=True))


def _cross_specs():
    q_spec = pl.BlockSpec((CROSS_ROWS, CROSS_HEAD), lambda h, i: (i, h))
    k_spec = pl.BlockSpec((N_MEM, CROSS_HEAD), lambda h, i: (0, h))
    v_spec = pl.BlockSpec((N_MEM, CROSS_HEAD), lambda h, i: (0, 4 + h))
    g_spec = pl.BlockSpec((1, CROSS_HEAD), lambda h, i: (0, 0))
    return q_spec, k_spec, v_spec, g_spec


def _cross_fwd(qc, kv, gq, gk):
    q_spec, k_spec, v_spec, g_spec = _cross_specs()

    def body(q_ref, k_ref, v_ref, gq_ref, gk_ref, o_ref):
        o_ref[...] = _cross_head(q_ref[...], k_ref[...], v_ref[...], gq_ref[...], gk_ref[...]).astype(o_ref.dtype)

    return pl.pallas_call(
        body, name="cross_fwd", grid=(4, SEQ // CROSS_ROWS),
        in_specs=[q_spec, k_spec, v_spec, g_spec, g_spec], out_specs=q_spec,
        out_shape=jax.ShapeDtypeStruct((SEQ, D_CROSS), BF16),
        compiler_params=_params(("parallel", "parallel")),
    )(qc, kv, kv, gq, gk)


def _cross_bwd(qc, kv, gq, gk, do):
    q_spec, k_spec, v_spec, g_spec = _cross_specs()

    def body(q_ref, k_ref, v_ref, gq_ref, gk_ref, do_ref, dq_ref, dk_ref, dv_ref, dgq_ref, dgk_ref):
        _, pullback = jax.vjp(_cross_head, q_ref[...], k_ref[...], v_ref[...], gq_ref[...], gk_ref[...])
        dq, dk, dv, dgq, dgk = pullback(do_ref[...].astype(F32))
        dq_ref[...] = dq.astype(dq_ref.dtype)
        row0 = pl.program_id(1) == 0
        all0 = jnp.logical_and(row0, pl.program_id(0) == 0)
        for o_ref, val, init in ((dk_ref, dk, row0), (dv_ref, dv, row0), (dgq_ref, dgq, all0), (dgk_ref, dgk, all0)):
            @pl.when(init)
            def _(o_ref=o_ref, val=val):
                o_ref[...] = val

            @pl.when(jnp.logical_not(init))
            def _(o_ref=o_ref, val=val):
                o_ref[...] += val

    return pl.pallas_call(
        body, name="cross_bwd", grid=(4, SEQ // CROSS_ROWS),
        in_specs=[q_spec, k_spec, v_spec, g_spec, g_spec, q_spec],
        out_specs=[q_spec, k_spec, k_spec, g_spec, g_spec],
        out_shape=[jax.ShapeDtypeStruct((SEQ, D_CROSS), BF16), jax.ShapeDtypeStruct((N_MEM, D_CROSS), F32),
                   jax.ShapeDtypeStruct((N_MEM, D_CROSS), F32), jax.ShapeDtypeStruct((1, CROSS_HEAD), F32),
                   jax.ShapeDtypeStruct((1, CROSS_HEAD), F32)],
        compiler_params=_params(("arbitrary", "arbitrary")),
    )(qc, kv, kv, gq, gk, do)


def _loss_epilogue(acc, residual, target):
    err = acc + residual - target
    dy = err * (1.0 / D_MODEL)
    part = jnp.sum(jnp.sum(err * err, axis=1, keepdims=True), axis=0, keepdims=True) * (0.5 / D_MODEL)
    return dy, dy, part


def _pad_heads(v):
    return jnp.pad(v.reshape(N_GROUPS, HEADS_PER_GROUP), ((0, 0), (0, 128 - HEADS_PER_GROUP))).reshape(1, DT_PAD)


def _unpad_heads(v):
    return v.reshape(v.shape[0], N_GROUPS, 128)[:, :, :HEADS_PER_GROUP].reshape(v.shape[0], N_DT)


def _rope_tables(positions):
    half = ROT // 2
    inv_freq = ROPE_THETA ** (-2.0 * jnp.arange(half, dtype=F32) / ROT)
    ang = positions.reshape(SEQ, 1).astype(F32) * inv_freq
    cos, sin = jnp.cos(ang), jnp.sin(ang)
    ones, zeros = jnp.ones((SEQ, HEAD - ROT), F32), jnp.zeros((SEQ, HEAD - ROT), F32)
    cos_h = jnp.concatenate([cos, cos, ones], axis=1)
    sin_h = jnp.concatenate([-sin, sin, zeros], axis=1)
    return jnp.tile(cos_h, (1, 2)), jnp.tile(sin_h, (1, 2))


def _add_res(acc, res):
    return (acc + res,)


def _norm_bwd_epilogue(acc, x, residual, *more):
    *part, g = more
    ct = acc + part[0] if part else acc
    _, pullback = jax.vjp(_rms, x, g)
    dx, dg = pullback(ct)
    return dx + residual, dg


def _add_res_and_norm(acc, res, g):
    y = acc + res
    return y, _rms(y, g)


def _settle(grads, *after):
    if hasattr(grads, "settle"):
        grads.settle(*after)


def _take_token(grads):
    token = getattr(grads, "token", None)
    if token is None:
        return ()
    grads.token = None
    return (token,)


def _local_step(x, mem, positions, target, p, w, more_weights=None, grads=None, h=None):
    grads = {} if grads is None else grads
    w = dict(w)
    cos, sin = _rope_tables(positions)
    gq2, gk2 = jnp.tile(p["g_q"], (1, 2)), jnp.tile(p["g_k"], (1, 2))
    bias, alog, dsk = _pad_heads(p["dt_bias"]), _pad_heads(p["a_log"]), _pad_heads(p["d_skip"])
    norm_out = [(D_MODEL, BF16, D_MODEL, 0, False)]

    if h is None:
        h = _rowwise(_norm_fn, [_full(x)], [_full(p["g_mix"])], norm_out, name="norm_in")[0]
    proj = _matmul(h, w["w_in"], mode="nn", name="in_proj", outs=[F32], n_cols=D_MAIN)
    dt_raw = _matmul(h, w["w_dt"], mode="nn", name="dt_proj", outs=[F32])
    pairs = D_ATTN // 128
    qk_rows = [(proj, 128, 0, True), (proj, 128, pairs, True), (proj, 128, 2 * pairs, True), _full(cos), _full(sin)]
    qk_vecs = [_full(gq2), _full(gk2)]
    qn, kn, vn = _rowwise(_qk_fn, qk_rows, qk_vecs, [(D_ATTN, F32, 128, 0, True)] * 3, name="qk_prep", groups=8, tr=1024)
    branches = [_attention_fwd(qn, kn, vn, b) for b in range(3)]
    merge_rows = [_full(o) for o, _ in branches] + [_full(lse) for _, lse in branches]
    attn = _rowwise(_merge_fn, merge_rows, [_full(p["g_attn_out"])], [(D_ATTN, BF16, D_ATTN, 0, False)], name="attn_merge")[0]
    if more_weights is not None:
        w.update(more_weights("attention_done", attn))
    xbc = _conv_fwd(proj, p["conv_w"], p["conv_b"])
    ssm, h_in = _ssd_fwd(xbc, dt_raw, bias, alog, dsk, proj, p["g_ssm_out"])
    mix = jnp.concatenate([attn, ssm], axis=1)
    if more_weights is not None:
        w.update(more_weights("mixer_done", mix))
    x1, hc = _matmul(mix, w["w_out"], mode="nn", name="out_proj", outs=[F32, BF16], extra=(x,), vecs=(p["g_cross"],),
                     epilogue=_add_res_and_norm, tm=512, tn=D_MODEL)
    memh = _rowwise(_norm_fn, [_full(mem)], [_full(p["g_mem"])], norm_out, name="norm_mem", n_rows=N_MEM, tr=N_MEM)[0]
    qc = _matmul(hc, w["w_cq"], mode="nn", name="cq_proj", outs=[F32])
    if more_weights is not None:
        w.update(more_weights("cross_started", qc))
    kv = _matmul(memh, w["w_ckv"], mode="nn", name="ckv_proj", outs=[F32])
    oc = _cross_fwd(qc, kv, p["g_cq"], p["g_ck"])
    x2, hm = _matmul(oc, w["w_co"], mode="nn", name="co_proj", outs=[F32, BF16], extra=(x1,), vecs=(p["g_mlp"],),
                     epilogue=_add_res_and_norm, tm=512, tn=D_MODEL)
    if more_weights is not None:
        w.update(more_weights("cross_done", hm))
    u, act = _matmul(hm, w["w_up"], mode="nn", name="up_proj", outs=[F32, BF16],
                     epilogue=lambda acc: (acc, jnp.square(jnp.maximum(acc, 0.0))))
    if more_weights is not None:
        w.update(more_weights("up_done", act))
    dy, dyb, loss_tiles = _matmul(act, w["w_down"], mode="nn", name="down_proj", outs=[F32, BF16], extra=(x2, target),
                                  epilogue=_loss_epilogue, tile_sums=1)
    loss = jnp.sum(loss_tiles).reshape(1, 1)

    grads["w_down"] = _matmul(act, dyb, mode="tn", name="dw_down", outs=[BF16], after=_take_token(grads))
    du = _matmul(dyb, w["w_down"], mode="nt", name="d_act", outs=[BF16], extra=(u,), after=_take_token(grads),
                 epilogue=lambda acc, uu: (acc * (2.0 * jnp.maximum(uu, 0.0)),))
    _settle(grads, du)
    grads["w_up"] = _matmul(hm, du, mode="tn", name="dw_up", outs=[BF16], col_shards=4, after=_take_token(grads))
    dx2, grads["g_mlp"] = _matmul(du, w["w_up"], mode="nt", name="d_hm", outs=[F32], extra=(x2, dy), vecs=(p["g_mlp"],),
                                  epilogue=_norm_bwd_epilogue, tile_rows=1, after=_take_token(grads), tm=512, tn=D_MODEL,
                                  tk=1024)
    _settle(grads, dx2)
    grads["w_co"] = _matmul(oc, dx2, mode="tn", name="dw_co", outs=[BF16], col_shards=4, after=_take_token(grads))
    doc = _matmul(dx2, w["w_co"], mode="nt", name="d_oc", outs=[BF16])
    dqc, dkc, dvc, grads["g_cq"], grads["g_ck"] = _cross_bwd(qc, kv, p["g_cq"], p["g_ck"], doc)
    grads["w_cq"] = _matmul(hc, dqc, mode="tn", name="dw_cq", outs=[BF16])
    dkv = jnp.concatenate([dkc, dvc], axis=1)
    grads["w_ckv"] = _matmul(memh, dkv, mode="tn", name="dw_ckv", outs=[BF16])
    dmemh = _matmul(dkv, w["w_ckv"], mode="nt", name="d_memh", outs=[F32])
    grads["g_mem"] = _rowwise_vjp(_norm_fn, [_full(mem)], [_full(p["g_mem"])], [[_full(dmemh)]], [],
                                  [(0, D_MODEL, D_MODEL, 0, False)], name="norm_mem_bwd", n_rows=N_MEM, tr=N_MEM)[0]
    dx1, grads["g_cross"] = _matmul(dqc, w["w_cq"], mode="nt", name="d_hc", outs=[F32], extra=(x1, dx2), vecs=(p["g_cross"],),
                                    epilogue=_norm_bwd_epilogue, tile_rows=1, tm=512, tn=D_MODEL)
    grads["w_out"] = _matmul(mix, dx1, mode="tn", name="dw_out", outs=[BF16])
    dmix = _matmul(dx1, w["w_out"], mode="nt", name="d_mix", outs=[F32], after=_take_token(grads))
    _settle(grads, dmix)
    merge_grads = [(i, D_ATTN, F32, D_ATTN, 0, False, None) for i in range(6)]
    *dol, grads["g_attn_out"] = _rowwise_vjp(
        _merge_fn, merge_rows, [_full(p["g_attn_out"])], [[(dmix, D_ATTN, 0, False)]],
        merge_grads, [(0, D_ATTN, D_ATTN, 0, False)], name="attn_merge_bwd", tr=256, after=_take_token(grads))
    dqkv = [_attention_bwd(qn, kn, vn, *branches[b], dol[b], dol[3 + b], b) for b in range(3)]
    qk_cts = [[(dqkv[b][i], 128, 0, True) for b in range(3)] for i in range(3)]
    dq, dk, dv, dgq2, dgk2 = _rowwise_vjp(
        _qk_fn, qk_rows, qk_vecs, qk_cts, [(i, D_ATTN, BF16, 128, 0, True, None) for i in range(3)],
        [(0, 128, 128, 0, False), (1, 128, 128, 0, False)], name="qk_prep_bwd", groups=8, tr=1024)
    grads["g_q"] = dgq2[:, :HEAD] + dgq2[:, HEAD:]
    grads["g_k"] = dgk2[:, :HEAD] + dgk2[:, HEAD:]
    dxs, db, dc, ddt, dbias, dalog, ddsk, dz, grads["g_ssm_out"] = _ssd_bwd(xbc, dt_raw, bias, alog, dsk, h_in, proj,
                                                                             p["g_ssm_out"], dmix)
    grads["dt_bias"], grads["a_log"], grads["d_skip"] = _unpad_heads(dbias), _unpad_heads(dalog), _unpad_heads(ddsk)
    dxbc_raw, dconv_w, grads["conv_b"] = _conv_bwd(proj, p["conv_w"], p["conv_b"], dxs, db, dc)
    grads["conv_w"] = dconv_w[:4]
    dproj = jnp.concatenate([dq, dk, dv, dz, dxbc_raw], axis=1)
    grads["w_main"] = _matmul(h, dproj, mode="tn", name="dw_main", outs=[BF16], out_cols=D_MAIN + N_DT)
    grads["w_dt"] = _matmul(h, ddt, mode="tn", name="dw_dt", outs=[BF16])
    dh = _matmul(dproj, w["w_in"], mode="nt", name="d_h_main", outs=[F32], after=_take_token(grads))
    grad_x, grads["g_mix"] = _matmul(ddt, w["w_dt"], mode="nt", name="d_h_dt", outs=[F32], extra=(x, dx1, dh),
                                     vecs=(p["g_mix"],), epilogue=_norm_bwd_epilogue, tile_rows=1, tm=512, tn=D_MODEL)
    return loss, grad_x, grads


MATRICES = ("w_in", "w_out", "w_cq", "w_ckv", "w_co", "w_up", "w_down")
ROW_SHARDED = ("w_out", "w_cq", "w_ckv", "w_down")
N_CHIPS = 4
ANY = pl.BlockSpec(memory_space=pl.ANY)


def _place():
    return lax.axis_index("x"), lax.axis_index("y"), lax.axis_index("c")


def _other_chips(x, y):
    return [(1 - x, y), (x, 1 - y), (1 - x, 1 - y)]


def _remote(src, dst, send_sem, recv_sem, device):
    return pltpu.make_async_remote_copy(src_ref=src, dst_ref=dst, send_sem=send_sem, recv_sem=recv_sem,
                                        device_id=device, device_id_type=MESH)


def _gathered_shape(name, shard):
    rows, cols = shard.shape
    if name == "w_in":
        return (N_CHIPS, rows, cols)
    return (N_CHIPS * rows, cols) if name in ROW_SHARDED else (rows, N_CHIPS * cols)


def _shard_window(name, ref, rows, cols, chip, half):
    r0, nr = (0, rows) if half is None else (half * (rows // 2), rows // 2)
    if name == "w_in":
        return ref.at[chip, pl.ds(r0, nr), :]
    if name in ROW_SHARDED:
        return ref.at[pl.ds(chip * rows + r0, nr), :]
    return ref.at[pl.ds(r0, nr), pl.ds(pl.multiple_of(chip * cols, 128), cols)]


def _cast_into_gathered(w, name, chip, after=()):
    rows, cols = w.shape
    tr = _tile(rows, ROW_TILE)

    def body(chip_ref, w_ref, *rest):
        rest[-1][...] = w_ref[...].astype(BF16)

    if name == "w_in":
        out_spec = pl.BlockSpec((None, tr, cols), lambda i, chip_ref: (chip_ref[0], i, 0))
    elif name in ROW_SHARDED:
        out_spec = pl.BlockSpec((tr, cols), lambda i, chip_ref: (chip_ref[0] * (rows // tr) + i, 0))
    else:
        out_spec = pl.BlockSpec((tr, cols), lambda i, chip_ref: (i, chip_ref[0]))
    grid_spec = pltpu.PrefetchScalarGridSpec(
        num_scalar_prefetch=1, grid=(rows // tr,),
        in_specs=[pl.BlockSpec((tr, cols), lambda i, chip_ref: (i, 0))] + [pl.BlockSpec(memory_space=pl.ANY)] * len(after),
        out_specs=out_spec)
    return pl.pallas_call(body, name="cast_" + name, grid_spec=grid_spec,
                          out_shape=jax.ShapeDtypeStruct(_gathered_shape(name, w), BF16),
                          compiler_params=_params(("parallel",)))(chip.reshape(1).astype(jnp.int32), w, *after)


def _w_in_columns(arr, to_shards):
    rows, piece = D_MODEL, (D_MAIN + N_DT) // N_CHIPS
    tr = ROW_TILE

    def body(a_ref, o_ref):
        for j in range(N_CHIPS):
            if to_shards:
                o_ref[j] = a_ref[:, pl.ds(piece * j, piece)]
            else:
                o_ref[:, pl.ds(piece * j, piece)] = a_ref[j]

    pieces = pl.BlockSpec((N_CHIPS, tr, piece), lambda i: (0, i, 0))
    matrix = pl.BlockSpec((tr, N_CHIPS * piece), lambda i: (i, 0))
    out_dims = (N_CHIPS, rows, piece) if to_shards else (rows, N_CHIPS * piece)
    return pl.pallas_call(
        body, name="w_in_to_shards" if to_shards else "w_in_from_shards", grid=(rows // tr,),
        in_specs=[matrix if to_shards else pieces], out_specs=pieces if to_shards else matrix,
        out_shape=jax.ShapeDtypeStruct(out_dims, arr.dtype), compiler_params=_params(("parallel",)))(arr)


HBM = pl.BlockSpec(memory_space=pltpu.HBM)
SEM = pl.BlockSpec(memory_space=pltpu.SEMAPHORE)
EFFECT = pltpu.SideEffectType.DATAFLOW_SIDE_EFFECTING


def _split_start(name, bufs, plan, counts, after=()):
    n, n_g, n_after = len(bufs), len(counts), len(after)

    def body(*refs):
        ins, sems, token = refs[:n], refs[n + n_after:n + n_after + 2 * n_g], refs[-1]
        for g, copies in enumerate(plan(ins)):
            for i, (src, dst, device, _) in enumerate(copies):
                _remote(src, dst, sems[2 * g].at[i], sems[2 * g + 1].at[i], device).start()
        token[...] = jnp.zeros_like(token)

    sem_shapes = [pltpu.SemaphoreType.DMA((cnt,)) for cnt in counts for _ in range(2)]
    res = pl.pallas_call(
        body, name=name,
        out_shape=(*sem_shapes, *[pltpu.HBM(b.shape, b.dtype) for b in bufs], jax.ShapeDtypeStruct((8, 128), F32)),
        in_specs=(*(HBM,) * n, *(ANY,) * n_after),
        out_specs=(*(SEM,) * (2 * n_g), *(HBM,) * n, pl.BlockSpec(memory_space=pltpu.VMEM)),
        input_output_aliases={i: 2 * n_g + i for i in range(n)},
        compiler_params=pltpu.CompilerParams(has_side_effects=EFFECT),
    )(*[pltpu.with_memory_space_constraint(b, pltpu.HBM) for b in bufs], *after)
    sems = [(res[2 * g], res[2 * g + 1]) for g in range(n_g)]
    return sems, list(res[2 * n_g:2 * n_g + n]), res[-1]


def _split_wait(name, bufs, sems, plan, *after):
    n = len(bufs)

    def body(*refs):
        ins, send, recv = refs[:n], refs[n], refs[n + 1]
        (copies,) = plan(ins)
        for i, (src, _, device, landing) in enumerate(copies):
            cp = _remote(src, landing, send.at[i], recv.at[i], device)
            cp.wait_send()
            cp.wait_recv()

    res = pl.pallas_call(
        body, name=name, out_shape=tuple(pltpu.HBM(b.shape, b.dtype) for b in bufs),
        in_specs=(*(HBM,) * n, SEM, SEM, *(ANY,) * len(after)), out_specs=(HBM,) * n,
        input_output_aliases={i: i for i in range(n)},
        compiler_params=pltpu.CompilerParams(has_side_effects=EFFECT),
    )(*bufs, sems[0], sems[1], *after)
    return list(res)


def _ici_plan(names, shard_shapes):
    def plan(refs):
        x, y, c = _place()
        copies = []
        for ref, name in zip(refs, names):
            win = _shard_window(name, ref, *shard_shapes[name], 2 * x + y, c)
            for px, py in _other_chips(x, y):
                copies.append((win, win, (px, py, c), _shard_window(name, ref, *shard_shapes[name], 2 * px + py, c)))
        return [copies]
    return plan


def _pass_on_plan(names, shard_shapes):
    def plan(refs):
        x, y, c = _place()
        copies = []
        for ref, name in zip(refs, names):
            for px, py in _other_chips(x, y):
                win = _shard_window(name, ref, *shard_shapes[name], 2 * px + py, c)
                copies.append((win, win, (x, y, 1 - c), _shard_window(name, ref, *shard_shapes[name], 2 * px + py, 1 - c)))
        return [copies]
    return plan


def _swap_plan(n_pairs):
    def plan(refs):
        x, y, c = _place()
        return [[(src.at[:, 1 - c], dst, (x, y, 1 - c), dst) for src, dst in zip(refs[:n_pairs], refs[n_pairs:])]]
    return plan


def _share_plan(n_pairs):
    def plan(refs):
        x, y, c = _place()
        return [[(src, dst, (x, y, 1 - c), dst)] for src, dst in zip(refs[:n_pairs], refs[n_pairs:])]
    return plan


def _scatter_plan(n_pairs):
    def plan(refs):
        x, y, c = _place()
        copies = []
        for src, dst in zip(refs[:n_pairs], refs[n_pairs:]):
            for k, (px, py) in enumerate(_other_chips(x, y)):
                copies.append((src.at[2 * px + py], dst.at[k], (px, py, c), dst.at[k]))
        return [copies]
    return plan


def _sibling_swap(arrs, name):
    n = len(arrs)

    def body(*refs):
        ins, outs, send, recv = refs[:n], refs[n:2 * n], refs[2 * n], refs[2 * n + 1]
        x, y, c = _place()
        cps = [_remote(ins[w].at[:, 1 - c], outs[w], send.at[w], recv.at[w], (x, y, 1 - c)) for w in range(n)]
        for cp in cps:
            cp.start()
        for cp in cps:
            cp.wait()

    return pl.pallas_call(
        body, name=name, in_specs=[ANY] * n, out_specs=[ANY] * n,
        out_shape=[jax.ShapeDtypeStruct((a.shape[0],) + a.shape[2:], a.dtype) for a in arrs],
        scratch_shapes=[pltpu.SemaphoreType.DMA((n,))] * 2,
    )(*arrs)


def _small_allreduce(buf, name, after=()):
    rows = buf.shape[0]

    def body(x_ref, *rest):
        out_ref, all_ref, send_sems, recv_sems, local_sem = rest[len(after):]
        x, y, c = _place()
        me, sibling, chips = (x, y, c), (x, y, 1 - c), _other_chips(x, y)

        def block(px, py, pc):
            return all_ref.at[pl.ds((4 * px + 2 * py + pc) * rows, rows), :]

        def copy(k, blk, to, src=None):
            return _remote(block(*blk) if src is None else src, block(*blk), send_sems.at[k], recv_sems.at[k], to)

        own = pltpu.make_async_copy(x_ref, block(*me), local_sem)
        own.start()
        first = [copy(0, me, sibling, src=x_ref)] + [copy(1 + j, me, (*chip, c), src=x_ref) for j, chip in enumerate(chips)]
        for cp in first:
            cp.start()
        passed = [copy(4 + j, (*chip, c), sibling) for j, chip in enumerate(chips)]
        for j, chip in enumerate(chips):
            copy(1 + j, (*chip, c), me).wait_recv()
            passed[j].start()
        copy(0, sibling, me).wait_recv()
        for j, chip in enumerate(chips):
            copy(4 + j, (*chip, 1 - c), me).wait_recv()
        for cp in first + passed:
            cp.wait_send()
        own.wait()
        acc = all_ref[pl.ds(0, rows), :]
        for d in range(1, 8):
            acc = acc + all_ref[pl.ds(d * rows, rows), :]
        out_ref[...] = acc

    vmem = pl.BlockSpec(memory_space=pltpu.VMEM)
    return pl.pallas_call(
        body, name=name, in_specs=[vmem] + [ANY] * len(after), out_specs=vmem,
        out_shape=jax.ShapeDtypeStruct(buf.shape, F32),
        scratch_shapes=[pltpu.VMEM((8 * rows, 128), F32), pltpu.SemaphoreType.DMA((7,)), pltpu.SemaphoreType.DMA((7,)),
                        pltpu.SemaphoreType.DMA],
    )(buf, *after)


ROW_TILE = 256
BIG_ROW_TILE = 1024


def _add_halves(arr, recv, c, name):
    _, _, hr, cols = arr.shape
    tr = _tile(hr, BIG_ROW_TILE)

    def body(c_ref, a_ref, r_ref, o_ref):
        o_ref[...] = (a_ref[...].astype(F32) + r_ref[...].astype(F32)).astype(o_ref.dtype)

    piece = pl.BlockSpec((None, tr, cols), lambda j, i, c_ref: (j, i, 0))
    grid_spec = pltpu.PrefetchScalarGridSpec(
        num_scalar_prefetch=1, grid=(N_CHIPS, hr // tr),
        in_specs=[pl.BlockSpec((None, None, tr, cols), lambda j, i, c_ref: (j, c_ref[0], i, 0)), piece], out_specs=piece)
    return pl.pallas_call(body, name=name, grid_spec=grid_spec, out_shape=jax.ShapeDtypeStruct(recv.shape, BF16),
                          compiler_params=_params(("parallel", "parallel")))(c.reshape(1).astype(jnp.int32), arr, recv)


def _flip_slot(d):
    return jnp.where(d == 1, 1, jnp.where(d == 3, 2, 0))


def _sum_chips(p, q, chip, name):
    _, hr, cols = p.shape
    tr = _tile(hr, BIG_ROW_TILE)

    def body(chip_ref, p_ref, q_ref, o_ref):
        j = pl.program_id(1)
        term = jnp.where(j == chip_ref[0], p_ref[...].astype(F32), q_ref[...].astype(F32))

        @pl.when(j == 0)
        def _():
            o_ref[...] = term

        @pl.when(j != 0)
        def _():
            o_ref[...] += term

    grid_spec = pltpu.PrefetchScalarGridSpec(
        num_scalar_prefetch=1, grid=(hr // tr, N_CHIPS),
        in_specs=[pl.BlockSpec((None, tr, cols), lambda i, j, chip_ref: (chip_ref[0], i, 0)),
                  pl.BlockSpec((None, tr, cols), lambda i, j, chip_ref: (_flip_slot(j ^ chip_ref[0]), i, 0))],
        out_specs=pl.BlockSpec((tr, cols), lambda i, j, chip_ref: (i, 0)))
    return pl.pallas_call(body, name=name, grid_spec=grid_spec, out_shape=jax.ShapeDtypeStruct((hr, cols), F32),
                          compiler_params=_params(("parallel", "arbitrary")))(chip.reshape(1).astype(jnp.int32), p, q)


def _adamw_halves(w, g_own, g_other, m, v, c, name):
    rows, cols = w.shape
    tr = _tile(rows // 2, ROW_TILE)
    per_half = rows // 2 // tr

    def body(c_ref, w_ref, own_ref, other_ref, m_ref, v_ref, g_ref, d_ref, nm_ref, nv_ref):
        mine = (pl.program_id(0) // per_half) == c_ref[0]
        g_ = jnp.where(mine, own_ref[...], other_ref[...])
        g_ref[...] = g_
        d_ref[...], nm_ref[...], nv_ref[...] = _adamw_math(w_ref[...], g_, m_ref[...], v_ref[...])

    blk = pl.BlockSpec((tr, cols), lambda i, c_ref: (i, 0))
    own = pl.BlockSpec((tr, cols), lambda i, c_ref: (jnp.where(i // per_half == c_ref[0], i % per_half, 0), 0))
    other = pl.BlockSpec((tr, cols), lambda i, c_ref: (jnp.where(i // per_half == c_ref[0], 0, i % per_half), 0))
    grid_spec = pltpu.PrefetchScalarGridSpec(num_scalar_prefetch=1, grid=(rows // tr,),
                                             in_specs=[blk, own, other, blk, blk], out_specs=[blk] * 4)
    return pl.pallas_call(body, name=name, grid_spec=grid_spec, out_shape=[jax.ShapeDtypeStruct(w.shape, F32)] * 4,
                          compiler_params=_params(("parallel",)))(c.reshape(1).astype(jnp.int32), w, g_own, g_other, m, v)


W_IN_COLS = (D_MAIN + N_DT) // N_CHIPS
W_IN_MAIN = W_IN_COLS // 128 * 128
W_IN_TAIL = W_IN_COLS - 128
W_IN_PARTS = ((0, W_IN_MAIN), (W_IN_TAIL, 128))


def _cast_w_in_transposed(w_t, chip, after=()):
    def body(chip_ref, w_ref, *rest):
        for start, size in W_IN_PARTS:
            rest[-1][:, pl.ds(start, size)] = w_ref[pl.ds(start, size), :].T.astype(BF16)

    grid_spec = pltpu.PrefetchScalarGridSpec(
        num_scalar_prefetch=1, grid=(D_MODEL // ROW_TILE,),
        in_specs=[pl.BlockSpec((W_IN_COLS, ROW_TILE), lambda i, chip_ref: (0, i))] + [pl.BlockSpec(memory_space=pl.ANY)] * len(after),
        out_specs=pl.BlockSpec((None, ROW_TILE, W_IN_COLS), lambda i, chip_ref: (chip_ref[0], i, 0)))
    return pl.pallas_call(body, name="cast_w_in", grid_spec=grid_spec,
                          out_shape=jax.ShapeDtypeStruct((N_CHIPS, D_MODEL, W_IN_COLS), BF16),
                          compiler_params=_params(("parallel",)))(chip.reshape(1).astype(jnp.int32), w_t, *after)


def _adamw_w_in_transposed(w_t, g_own, g_other, m_t, v_t, c):
    per_half = D_MODEL // 2 // ROW_TILE

    def body(c_ref, w_ref, own_ref, other_ref, m_ref, v_ref, g_ref, d_ref, nm_ref, nv_ref):
        mine = (pl.program_id(0) // per_half) == c_ref[0]
        for start, size in W_IN_PARTS:
            cols, rows = pl.ds(start, size), pl.ds(start, size)
            g_ = jnp.where(mine, own_ref[:, cols], other_ref[:, cols]).T
            g_ref[rows, :] = g_
            d_ref[rows, :], nm_ref[rows, :], nv_ref[rows, :] = _adamw_math(w_ref[rows, :], g_, m_ref[rows, :], v_ref[rows, :])

    blk = pl.BlockSpec((W_IN_COLS, ROW_TILE), lambda i, c_ref: (0, i))
    own = pl.BlockSpec((ROW_TILE, W_IN_COLS), lambda i, c_ref: (jnp.where(i // per_half == c_ref[0], i % per_half, 0), 0))
    other = pl.BlockSpec((ROW_TILE, W_IN_COLS), lambda i, c_ref: (jnp.where(i // per_half == c_ref[0], 0, i % per_half), 0))
    grid_spec = pltpu.PrefetchScalarGridSpec(num_scalar_prefetch=1, grid=(D_MODEL // ROW_TILE,),
                                             in_specs=[blk, own, other, blk, blk], out_specs=[blk] * 4)
    return pl.pallas_call(body, name="adamw_w_in", grid_spec=grid_spec, out_shape=[jax.ShapeDtypeStruct(w_t.shape, F32)] * 4,
                          compiler_params=_params(("parallel",)))(c.reshape(1).astype(jnp.int32), w_t, g_own, g_other, m_t, v_t)


def _adamw_math(w, g, m, v):
    m_new = ADAM_B1 * m + (1.0 - ADAM_B1) * g
    v_new = ADAM_B2 * v + (1.0 - ADAM_B2) * (g * g)
    m_hat = m_new / (1.0 - ADAM_B1 ** ADAM_STEP)
    v_hat = v_new / (1.0 - ADAM_B2 ** ADAM_STEP)
    return -ADAM_LR * (m_hat / (jnp.sqrt(v_hat) + ADAM_EPS) + ADAM_WD * w), m_new, v_new


VECTORS = ("g_mix", "g_q", "g_k", "g_attn_out", "conv_b", "dt_bias", "a_log", "d_skip", "g_ssm_out", "g_cross", "g_mem",
           "g_cq", "g_ck", "g_mlp")
WEIGHTS = ("g_mix", "w_in", "g_q", "g_k", "g_attn_out", "conv_w", "conv_b", "dt_bias", "a_log", "d_skip", "g_ssm_out", "w_out",
           "g_cross", "g_mem", "w_cq", "w_ckv", "g_cq", "g_ck", "w_co", "g_mlp", "w_up", "w_down")


def _pack(parts):
    flat = jnp.concatenate([t.reshape(-1) for t in parts])
    total = -(-flat.shape[0] // 1024) * 1024
    return jnp.pad(flat, (0, total - flat.shape[0])).reshape(total // 128, 128)


def _rows_of(n):
    return -(-n // 128)


def _slot_rows(n):
    return -(-n // 1024) * 8


def _pack_rows(parts):
    rows = []
    for t in parts:
        flat = t.reshape(-1)
        rows.append(jnp.pad(flat, (0, 128 * _slot_rows(flat.shape[0]) - flat.shape[0])).reshape(-1, 128))
    return jnp.concatenate(rows)


def _adamw_vectors(summed, chip, vectors, conv):
    groups = list(vectors) + [conv]
    offsets, row = [], 0
    for w, _, _ in groups:
        offsets.append(row)
        row += _slot_rows(w.shape[1]) if w.shape[0] == 1 else _slot_rows(w.shape[0] * N_CHIPS * w.shape[1])
    conv_blocks = _rows_of(conv[0].shape[1])

    def body(chip_ref, sum_ref, *refs):
        ins, outs = refs[:3 * len(groups)], refs[3 * len(groups):]

        def update(i, g, idx):
            w_ref, m_ref, v_ref = ins[3 * i:3 * i + 3]
            delta, new_m, new_v = _adamw_math(w_ref[idx], g, m_ref[idx], v_ref[idx])
            for o_ref, val in zip(outs[4 * i:4 * i + 4], (g, delta, new_m, new_v)):
                o_ref[idx] = val

        for i, (w, _, _) in enumerate(vectors):
            for t in range(_rows_of(w.shape[1])):
                width = min(128, w.shape[1] - 128 * t)
                update(i, sum_ref[pl.ds(offsets[i] + t, 1), pl.ds(0, width)], (slice(None), pl.ds(128 * t, width)))
        for tap in range(conv[0].shape[0]):
            for blk in range(conv_blocks):
                src = offsets[-1] + tap * N_CHIPS * conv_blocks + chip_ref[0] * conv_blocks + blk
                update(len(vectors), sum_ref[pl.ds(src, 1), :], (pl.ds(tap, 1), pl.ds(128 * blk, 128)))

    def whole(a):
        return pl.BlockSpec(a.shape, lambda i, chip_ref: (0,) * a.ndim)

    operands = [t for group in groups for t in group]
    grid_spec = pltpu.PrefetchScalarGridSpec(
        num_scalar_prefetch=1, grid=(1,), in_specs=[whole(summed)] + [whole(t) for t in operands],
        out_specs=[whole(w) for w, _, _ in groups for _ in range(4)])
    res = pl.pallas_call(body, name="adamw_vectors", grid_spec=grid_spec,
                         out_shape=[jax.ShapeDtypeStruct(w.shape, F32) for w, _, _ in groups for _ in range(4)],
                         compiler_params=_params(("arbitrary",)))(chip.reshape(1).astype(jnp.int32), summed, *operands)
    return [res[4 * i:4 * i + 4] for i in range(len(groups))]


def _unpack(buf, shapes):
    flat, out, pos = buf.reshape(-1), [], 0
    for shape in shapes:
        size = math.prod(shape)
        out.append(flat[pos:pos + size].reshape(shape))
        pos += size
    return out


def kernel(x, mem, positions, g_mix, w_in, g_q, g_k, g_attn_out, conv_w, conv_b, dt_bias, a_log, d_skip, g_ssm_out, w_out, g_cross, g_mem, w_cq, w_ckv, g_cq, g_ck, w_co, g_mlp, w_up, w_down, loss_target, m_g_mix, m_w_in, m_g_q, m_g_k, m_g_attn_out, m_conv_w, m_conv_b, m_dt_bias, m_a_log, m_d_skip, m_g_ssm_out, m_w_out, m_g_cross, m_g_mem, m_w_cq, m_w_ckv, m_g_cq, m_g_ck, m_w_co, m_g_mlp, m_w_up, m_w_down, v_g_mix, v_w_in, v_g_q, v_g_k, v_g_attn_out, v_conv_w, v_conv_b, v_dt_bias, v_a_log, v_d_skip, v_g_ssm_out, v_w_out, v_g_cross, v_g_mem, v_w_cq, v_w_ckv, v_g_cq, v_g_ck, v_w_co, v_g_mlp, v_w_up, v_w_down):
    args = dict(locals())
    weights = {n: args[n][0] for n in WEIGHTS}
    mom_m = {n: args["m_" + n][0] for n in WEIGHTS}
    mom_v = {n: args["v_" + n][0] for n in WEIGHTS}
    x_idx, y_idx, c_idx = _place()
    chip = 2 * x_idx + y_idx

    shapes = {n: weights[n].shape for n in MATRICES}
    first, mid, late = ("w_in",), ("w_out", "w_cq", "w_ckv", "w_co"), ("w_up", "w_down")
    w_in_t, m_in_t, v_in_t = (jnp.swapaxes(t, 1, 2)[0] for t in (w_in, m_w_in, v_w_in))
    w_in_buf = [_cast_w_in_transposed(w_in_t, chip)]
    taps, tap_cols = weights["conv_w"].shape
    conv_parts = _small_allreduce(_pack([jnp.zeros((N_CHIPS, taps, tap_cols), F32).at[chip].set(0.5 * weights["conv_w"])]),
                                  "gather_conv_taps")
    sems_in, w_in_buf, token = _split_start("gather_ici_start_w_in", w_in_buf, _ici_plan(first, shapes), [3], after=(conv_parts,))
    bufs = [_cast_into_gathered(weights[n], n, chip, after=(token,)) for n in mid + late]
    plan = lambda refs: (_ici_plan(mid, shapes)(refs[:4]) + _ici_plan(late[:1], shapes)(refs[4:5])
                         + _ici_plan(late[1:], shapes)(refs[5:]))
    sems_rest, bufs, token = _split_start("gather_ici_start_rest", bufs, plan, [12, 3, 3], after=(token,))
    params = {n: weights[n].reshape(1, -1) for n in VECTORS}
    h_in = _rowwise(_norm_fn, [_full(x[0])], [_full(params["g_mix"])], [(D_MODEL, BF16, D_MODEL, 0, False)], name="norm_in",
                    after=(token,))[0]
    w_in_buf = _split_wait("gather_ici_wait_w_in", w_in_buf, sems_in[0], _ici_plan(first, shapes), token, h_in, m_in_t, v_in_t)
    pass_sems, w_in_buf, token = _split_start("gather_pass_start_w_in", w_in_buf, _pass_on_plan(first, shapes), [3])
    w_in_buf = _split_wait("gather_pass_wait_w_in", w_in_buf, pass_sems[0], _pass_on_plan(first, shapes), token)
    w_in_full = _w_in_columns(w_in_buf[0], to_shards=False)
    full = {"w_in": w_in_full,
            "w_dt": jnp.pad(w_in_full[:, D_MAIN:].reshape(D_MODEL, N_GROUPS, HEADS_PER_GROUP),
                            ((0, 0), (0, 0), (0, 128 - HEADS_PER_GROUP))).reshape(D_MODEL, DT_PAD)}
    in_flight = {}

    def more_weights(stage, after):
        if stage == "attention_done":
            got = _split_wait("gather_ici_wait_mid", bufs[:4], sems_rest[0], _ici_plan(mid, shapes), after)
            in_flight["mid"] = _split_start("gather_pass_start_mid", got, _pass_on_plan(mid, shapes), [12])
            return {}
        if stage == "mixer_done":
            sems, got, token = in_flight.pop("mid")
            return dict(zip(mid, _split_wait("gather_pass_wait_mid", got, sems[0], _pass_on_plan(mid, shapes), token, after)))
        up, down = late[:1], late[1:]
        if stage == "cross_started":
            got = _split_wait("gather_ici_wait_w_up", bufs[4:5], sems_rest[1], _ici_plan(up, shapes), after)
            in_flight["w_up"] = _split_start("gather_pass_start_w_up", got, _pass_on_plan(up, shapes), [3])
            return {}
        if stage == "cross_done":
            sems, got, token = in_flight.pop("w_up")
            w_up_full = _split_wait("gather_pass_wait_w_up", got, sems[0], _pass_on_plan(up, shapes), token, after)
            got = _split_wait("gather_ici_wait_w_down", bufs[5:], sems_rest[2], _ici_plan(down, shapes), w_up_full[0])
            in_flight["w_down"] = _split_start("gather_pass_start_w_down", got, _pass_on_plan(down, shapes), [3])
            return dict(zip(up, w_up_full))
        sems, got, token = in_flight.pop("w_down")
        return dict(zip(down, _split_wait("gather_pass_wait_w_down", got, sems[0], _pass_on_plan(down, shapes), token, after)))

    params["conv_w"] = _unpack(conv_parts, [(N_CHIPS, taps, tap_cols)])[0].transpose(1, 0, 2).reshape(taps, N_CHIPS * tap_cols)

    groups = (("w_down",), ("w_up",), ("w_co", "w_cq", "w_ckv", "w_out"), ("w_in",))
    scattered = []

    class GradStore(dict):
        pending = None

        def __setitem__(self, name, value):
            super().__setitem__(name, value)
            if "w_main" in self and "w_dt" in self and "w_in" not in self:
                gw_in = lax.dynamic_update_slice(self["w_main"], _unpad_heads(self["w_dt"]), (0, D_MAIN))
                self["w_in"] = _w_in_columns(gw_in, to_shards=True)
            for group in groups:
                if name in group and all(n in self for n in group):
                    self.settle()
                    pieces = [self[n].reshape(N_CHIPS, 2, shapes[n][0] // 2, shapes[n][1]) for n in group]
                    if group == groups[-1]:
                        self.scatter(group, pieces, _sibling_swap(pieces, "grad_swap_" + group[0]))
                    else:
                        landing = [lax.empty((N_CHIPS,) + a.shape[2:], BF16) for a in pieces]
                        sems, thru, self.token = _split_start("grad_swap_start_" + group[0], pieces + landing,
                                                              _swap_plan(len(pieces)), [len(pieces)])
                        self.pending = (group, sems[0], thru)

        def settle(self, *after):
            if self.pending is not None:
                group, sems, thru = self.pending
                self.pending = None
                thru = _split_wait("grad_swap_wait_" + group[0], thru, sems, _swap_plan(len(group)), *after)
                self.scatter(group, thru[:len(group)], thru[len(group):])

        def scatter(self, group, pieces, from_sibling):
            sums = [_add_halves(a, r, c_idx, "add_halves_" + n) for n, a, r in zip(group, pieces, from_sibling)]
            landing = [lax.empty((3,) + s.shape[1:], BF16) for s in sums]
            sems, thru, self.token = _split_start("grad_scatter_start_" + group[0], sums + landing,
                                                  _scatter_plan(len(sums)), [3 * len(sums)])
            scattered.append((group, sems[0], thru))

    loss, grad_x, grads = _local_step(x[0], mem[0], positions[0], loss_target[0], params, full, more_weights, GradStore(),
                                      h_in)

    out_g, out_d, out_m, out_v = {}, {}, {}, {}

    def finish(entries, order, token):
        halves = {}
        for group, sems, thru in entries:
            thru = _split_wait("grad_scatter_wait_" + group[0], thru, sems, _scatter_plan(len(group)), token)
            for i, n in enumerate(group):
                halves[n] = _sum_chips(thru[i], thru[len(group) + i], chip, "sum_chips_" + n)
        sources = [halves[n] for n in order]
        landing = [lax.empty(s.shape, F32) for s in sources]
        sems, thru, token = _split_start("grad_share_start_" + order[0], sources + landing, _share_plan(len(order)),
                                         [1] * len(order))
        for i, n in enumerate(order):
            own, other = _split_wait("grad_share_wait_" + n, [thru[i], thru[len(order) + i]], sems[i], _share_plan(1), token)
            if n == "w_in":
                res_t = _adamw_w_in_transposed(w_in_t, own, other, m_in_t, v_in_t, c_idx)
                out_g[n], out_d[n], out_m[n], out_v[n] = (t.T for t in res_t)
            else:
                out_g[n], out_d[n], out_m[n], out_v[n] = _adamw_halves(weights[n], own, other, mom_m[n], mom_v[n], c_idx,
                                                                       "adamw_" + n)
            token = out_v[n]
        return token

    token = finish(scattered[:-1], ("w_cq", "w_co", "w_ckv", "w_out", "w_up", "w_down"), grad_x)
    finish(scattered[-1:], ("w_in",), token)

    names = VECTORS + ("conv_w",)
    summed = _small_allreduce(_pack_rows([grads[n] for n in names] + [loss]), "allreduce_vectors")
    total_loss = summed[sum(_slot_rows(grads[n].size) for n in names), 0]
    small_out = _adamw_vectors(summed, chip, [(args[n], args["m_" + n], args["v_" + n]) for n in VECTORS],
                               (weights["conv_w"], mom_m["conv_w"], mom_v["conv_w"]))
    for n, res in zip(names, small_out):
        out_g[n], out_d[n], out_m[n], out_v[n] = (t.reshape(weights[n].shape) for t in res)

    outs =[total_loss, grad_x[None]]
    for group in (out_g, out_d, out_m, out_v):
        outs += [group[n][None] for n in WEIGHTS]
    return tuple(outs)
```
